```python
import jax, jax.numpy as jnp
from jax import lax
import numpy as np

D_MODEL = 1024
BATCH = 8
SEQ = 8192
DEPTH = 1

HEAD_DIM = 64
ROT_DIM = HEAD_DIM // 4
ROPE_THETA = 500000.0
BLOCK = 128
N_MEM = 256
D_MIX = D_MODEL
C_HEADS = 4
C_W = C_HEADS * HEAD_DIM
A_Q_HEADS = (D_MIX - C_W) // (2 * HEAD_DIM)
A_KV_HEADS = 2
A_GROUP = A_Q_HEADS // A_KV_HEADS
A_W = A_Q_HEADS * HEAD_DIM
A_KV_W = A_KV_HEADS * HEAD_DIM
A_WINDOW = 128
B_HEADS = A_Q_HEADS
B_W = B_HEADS * HEAD_DIM
B_CONFIGS = ((128, 1), (512, 4), (2048, 16))
RMS_EPS = 1e-6
IN_WIDTHS = (A_W, A_KV_W, A_KV_W, A_W,
             B_W, B_W, B_W, B_W,
             C_W, C_W)
D_IN = sum(IN_WIDTHS)
IN_SPLITS = tuple(int(s) for s in np.cumsum(IN_WIDTHS)[:-1])

kernel_name = "hybrid_swa_sink_dilated_memxattn_layer"


def rms_norm(x, g):
    xf = x.astype(jnp.float32)
    y = xf * lax.rsqrt(jnp.mean(xf * xf, axis=-1, keepdims=True) + RMS_EPS)
    return (y * g.astype(jnp.float32)).astype(x.dtype)


def rope_tables(seq):
    inv_freq = ROPE_THETA ** (-jnp.arange(0, ROT_DIM, 2, dtype=jnp.float32) / ROT_DIM)
    ang = jnp.arange(seq, dtype=jnp.float32)[:, None] * inv_freq[None, :]
    return jnp.cos(ang)[:, None, :], jnp.sin(ang)[:, None, :]


def apply_partial_rope(t, cos, sin):
    tf = t.astype(jnp.float32)
    half = ROT_DIM // 2
    r1, r2, rest = tf[..., :half], tf[..., half:ROT_DIM], tf[..., ROT_DIM:]
    out = jnp.concatenate([r1 * cos - r2 * sin, r2 * cos + r1 * sin, rest], axis=-1)
    return out.astype(t.dtype)


def banded_attention(q, k, v, max_dist, sink):
    N, L, KVH, G, Dh = q.shape
    nb = -(-L // BLOCK)
    Lp = nb * BLOCK
    pad = Lp - L
    n_prev = -(-max_dist // BLOCK)
    W = (n_prev + 1) * BLOCK
    qp = jnp.pad(q, ((0, 0), (0, pad), (0, 0), (0, 0), (0, 0)))
    kp = jnp.pad(k, ((0, 0), (n_prev * BLOCK, pad), (0, 0), (0, 0)))
    vp = jnp.pad(v, ((0, 0), (n_prev * BLOCK, pad), (0, 0), (0, 0)))

    def windows(t):
        return jnp.concatenate(
            [t[:, j * BLOCK:(j + nb) * BLOCK].reshape(N, nb, BLOCK, KVH, Dh)
             for j in range(n_prev + 1)], axis=2)

    kw, vw = windows(kp), windows(vp)
    qb = qp.reshape(N, nb, BLOCK, KVH, G, Dh)
    s = jnp.einsum('nbqhgd,nbkhd->nbhgqk', qb, kw,
                   preferred_element_type=jnp.float32)
    qi = jnp.arange(BLOCK)[:, None]
    kj = jnp.arange(W)[None, :]
    dist = qi + n_prev * BLOCK - kj
    kpos = jnp.arange(nb)[:, None, None] * BLOCK + kj[None] - n_prev * BLOCK
    mask = (dist >= 0)[None] & (dist <= max_dist)[None] & (kpos >= 0)
    s = jnp.where(mask[:, None, None], s, -jnp.inf)
    m = jnp.max(s, axis=-1)
    if sink is not None:
        sink_b = sink.astype(jnp.float32)[None, None, :, :, None]
        m = jnp.maximum(m, sink_b)
    p = jnp.exp(s - m[..., None])
    denom = jnp.sum(p, axis=-1)
    if sink is not None:
        denom = denom + jnp.exp(sink_b - m)
    o = jnp.einsum('nbhgqk,nbkhd->nbqhgd', p, vw.astype(jnp.float32))
    o = o / jnp.moveaxis(denom, -1, 2)[..., None]
    lse = jnp.moveaxis(m + jnp.log(denom), -1, 2)
    o = o.reshape(N, Lp, KVH, G, Dh)[:, :L]
    lse = lse.reshape(N, Lp, KVH, G)[:, :L]
    return o.astype(q.dtype), lse


def to_strided(t, d):
    B, S = t.shape[:2]
    rest = t.shape[2:]
    t = t.reshape((B, S // d, d) + rest)
    t = jnp.moveaxis(t, 2, 1)
    return t.reshape((B * d, S // d) + rest)


def from_strided(t, B, d):
    L = t.shape[1]
    rest = t.shape[2:]
    t = t.reshape((B, d, L) + rest)
    t = jnp.moveaxis(t, 1, 2)
    return t.reshape((B, d * L) + rest)


def _fwd_setup_inputs(seed: int = 0) -> dict:
    key = jax.random.key(seed)
    ks = jax.random.split(key, 10)
    f32 = jnp.float32
    x = jax.random.normal(ks[0], (BATCH, SEQ, D_MODEL), f32)
    mem = jax.random.normal(ks[1], (BATCH, N_MEM, D_MODEL), f32)
    pre_norm = 1.0 + 0.02 * jax.random.normal(ks[2], (DEPTH, D_MODEL), f32)
    w_in = jax.random.normal(ks[3], (DEPTH, D_MODEL, D_IN), f32) * D_MODEL ** -0.5
    sink_a = 0.5 * jax.random.normal(ks[4], (DEPTH, A_Q_HEADS), f32)
    mem_norm = 1.0 + 0.02 * jax.random.normal(ks[5], (DEPTH, D_MODEL), f32)
    w_mem_kv = jax.random.normal(ks[6], (DEPTH, D_MODEL, 2 * C_W), f32) * D_MODEL ** -0.5
    w_out = jax.random.normal(ks[7], (DEPTH, D_MIX, D_MODEL), f32) * D_MIX ** -0.5
    post_norm = 1.0 + 0.02 * jax.random.normal(ks[8], (DEPTH, D_MODEL), f32)
    return {"x": x, "mem": mem, "pre_norm": pre_norm, "w_in": w_in,
            "sink_a": sink_a, "mem_norm": mem_norm, "w_mem_kv": w_mem_kv,
            "w_out": w_out, "post_norm": post_norm}


def _fwd_reference(x, mem, pre_norm, w_in, sink_a, mem_norm, w_mem_kv, w_out, post_norm):
    B, S, _ = x.shape
    scale = HEAD_DIM ** -0.5
    cos, sin = rope_tables(S)
    h = x
    for l in range(DEPTH):
        u = rms_norm(h, pre_norm[l])
        proj = jnp.einsum('bsd,de->bse', u, w_in[l])
        qa, ka, va, ga, qb, kb, vb, gb, qc, gc = jnp.split(proj, IN_SPLITS, axis=-1)

        qa = apply_partial_rope(qa.reshape(B, S, A_Q_HEADS, HEAD_DIM), cos, sin)
        qa = qa.reshape(B, S, A_KV_HEADS, A_GROUP, HEAD_DIM)
        ka = apply_partial_rope(ka.reshape(B, S, A_KV_HEADS, HEAD_DIM), cos, sin)
        va = va.reshape(B, S, A_KV_HEADS, HEAD_DIM)
        sink = sink_a[l].reshape(A_KV_HEADS, A_GROUP)
        oa, _ = banded_attention(qa * scale, ka, va, A_WINDOW - 1, sink)
        oa = oa.reshape(B, S, A_W)

        qb = apply_partial_rope(qb.reshape(B, S, B_HEADS, HEAD_DIM), cos, sin)
        kb = apply_partial_rope(kb.reshape(B, S, B_HEADS, HEAD_DIM), cos, sin)
        vb = vb.reshape(B, S, B_HEADS, HEAD_DIM)
        outs, lses = [], []
        for (win, dil) in B_CONFIGS:
            qs = to_strided(qb * scale, dil)[:, :, :, None, :]
            o, lse = banded_attention(qs, to_strided(kb, dil), to_strided(vb, dil),
                                      win // dil, None)
            outs.append(from_strided(o[:, :, :, 0], B, dil).astype(jnp.float32))
            lses.append(from_strided(lse[..., 0], B, dil))
        wts = jax.nn.softmax(jnp.stack(lses, axis=0), axis=0)
        ob = jnp.sum(wts[..., None] * jnp.stack(outs, axis=0), axis=0)
        ob = ob.astype(x.dtype).reshape(B, S, B_W)

        mkv = jnp.einsum('bmd,de->bme', rms_norm(mem, mem_norm[l]), w_mem_kv[l])
        mk, mv = jnp.split(mkv, 2, axis=-1)
        mk = mk.reshape(B, N_MEM, C_HEADS, HEAD_DIM)
        mv = mv.reshape(B, N_MEM, C_HEADS, HEAD_DIM)
        qc = qc.reshape(B, S, C_HEADS, HEAD_DIM)
        sc = jnp.einsum('bshd,bmhd->bhsm', qc * scale, mk,
                        preferred_element_type=jnp.float32)
        pc = jax.nn.softmax(sc, axis=-1)
        oc = jnp.einsum('bhsm,bmhd->bshd', pc, mv.astype(jnp.float32))
        oc = oc.astype(x.dtype).reshape(B, S, C_W)

        y = jnp.concatenate([oa * jax.nn.silu(ga), ob * jax.nn.silu(gb),
                             oc * jax.nn.silu(gc)], axis=-1)
        y = jnp.einsum('bse,ed->bsd', y, w_out[l])
        h = h + rms_norm(y, post_norm[l])
    return h


import jax as _jax
import jax.numpy as _jnp

TWIN_FORMAT = 'train_step'
FWD_PARAMS = ['x', 'mem', 'pre_norm', 'w_in', 'sink_a', 'mem_norm', 'w_mem_kv', 'w_out', 'post_norm']
TWIN_WEIGHTS = ['pre_norm', 'w_in', 'sink_a', 'mem_norm', 'w_mem_kv', 'w_out', 'post_norm']
TWIN_DIFF_INPUT = 'x'
TWIN_INPUTS = ['x', 'mem', 'pre_norm', 'w_in', 'sink_a', 'mem_norm', 'w_mem_kv', 'w_out', 'post_norm', 'loss_target', 'm_pre_norm', 'm_w_in', 'm_sink_a', 'm_mem_norm', 'm_w_mem_kv', 'm_w_out', 'm_post_norm', 'v_pre_norm', 'v_w_in', 'v_sink_a', 'v_mem_norm', 'v_w_mem_kv', 'v_w_out', 'v_post_norm']
TWIN_OUTPUTS = ['loss', 'grad_x', 'grad_pre_norm', 'grad_w_in', 'grad_sink_a', 'grad_mem_norm', 'grad_w_mem_kv', 'grad_w_out', 'grad_post_norm', 'delta_pre_norm', 'delta_w_in', 'delta_sink_a', 'delta_mem_norm', 'delta_w_mem_kv', 'delta_w_out', 'delta_post_norm', 'new_m_pre_norm', 'new_m_w_in', 'new_m_sink_a', 'new_m_mem_norm', 'new_m_w_mem_kv', 'new_m_w_out', 'new_m_post_norm', 'new_v_pre_norm', 'new_v_w_in', 'new_v_sink_a', 'new_v_mem_norm', 'new_v_w_mem_kv', 'new_v_w_out', 'new_v_post_norm']
TWIN_LEAF_KINDS = {'loss': 'loss', 'grad_x': 'grad_x', 'grad_pre_norm': 'grad_w', 'grad_w_in': 'grad_w', 'grad_sink_a': 'grad_w', 'grad_mem_norm': 'grad_w', 'grad_w_mem_kv': 'grad_w', 'grad_w_out': 'grad_w', 'grad_post_norm': 'grad_w', 'delta_pre_norm': 'delta_w', 'delta_w_in': 'delta_w', 'delta_sink_a': 'delta_w', 'delta_mem_norm': 'delta_w', 'delta_w_mem_kv': 'delta_w', 'delta_w_out': 'delta_w', 'delta_post_norm': 'delta_w', 'new_m_pre_norm': 'new_m', 'new_m_w_in': 'new_m', 'new_m_sink_a': 'new_m', 'new_m_mem_norm': 'new_m', 'new_m_w_mem_kv': 'new_m', 'new_m_w_out': 'new_m', 'new_m_post_norm': 'new_m', 'new_v_pre_norm': 'new_v', 'new_v_w_in': 'new_v', 'new_v_sink_a': 'new_v', 'new_v_mem_norm': 'new_v', 'new_v_w_mem_kv': 'new_v', 'new_v_w_out': 'new_v', 'new_v_post_norm': 'new_v'}


def _forward(args):
    return _fwd_reference(*[args[k] for k in FWD_PARAMS])


def _output_shape():
    def fwd():
        inp = _fwd_setup_inputs(0)
        return _fwd_reference(*[inp[k] for k in FWD_PARAMS])
    out = _jax.eval_shape(fwd)
    return out.shape, out.dtype

N_MICROBATCH = 1
ADAM_LR = 0.001
ADAM_B1 = 0.9
ADAM_B2 = 0.999
ADAM_EPS = 1e-08
ADAM_WD = 0.01
ADAM_STEP = 10
PER_EXAMPLE_BATCH_AXIS = {'x': 0, 'mem': 0, 'loss_target': 0}
SHARED_INPUTS = []
_WEIGHT_DTYPES = {'pre_norm': _jnp.float32, 'w_in': _jnp.float32, 'sink_a': _jnp.float32, 'mem_norm': _jnp.float32, 'w_mem_kv': _jnp.float32, 'w_out': _jnp.float32, 'post_norm': _jnp.float32}
MOMENT_SCALE = {'pre_norm': 6.721057e-01, 'w_in': 4.040869e-01, 'sink_a': 5.639471e-02, 'mem_norm': 2.796478e-01, 'w_mem_kv': 4.039070e-01, 'w_out': 3.788500e-01, 'post_norm': 6.406205e+01}


def _to_microbatches(a, axis):
    t = _jnp.moveaxis(a, axis, 0)
    t = t.reshape((N_MICROBATCH, t.shape[0] // N_MICROBATCH) + t.shape[1:])
    return _jnp.moveaxis(t, 1, axis + 1)


def setup_inputs(seed: int = 0) -> dict:
    inp = _fwd_setup_inputs(seed)
    key = _jax.random.fold_in(_jax.random.key(seed), 7919)
    shape, _ = _output_shape()
    out = dict(inp)
    out["loss_target"] = _jax.random.normal(_jax.random.fold_in(key, 0), shape, _jnp.float32)
    for i, name in enumerate(TWIN_WEIGHTS):
        w = inp[name].astype(_jnp.float32)
        if MOMENT_SCALE is None:
            s = _jnp.sqrt(_jnp.mean(_jnp.square(w)) + 1e-30)
        else:
            s = MOMENT_SCALE[name]
        km, kv = _jax.random.split(_jax.random.fold_in(key, i + 1))
        out[name] = w
        out["m_" + name] = s * _jax.random.normal(km, w.shape, _jnp.float32)
        out["v_" + name] = (s * s) * _jax.random.uniform(kv, w.shape, _jnp.float32, 0.5, 1.5)
    if N_MICROBATCH > 1:
        for name, axis in PER_EXAMPLE_BATCH_AXIS.items():
            out[name] = _to_microbatches(out[name], axis)
    return {'x': out['x'], 'mem': out['mem'], 'pre_norm': out['pre_norm'], 'w_in': out['w_in'], 'sink_a': out['sink_a'], 'mem_norm': out['mem_norm'], 'w_mem_kv': out['w_mem_kv'], 'w_out': out['w_out'], 'post_norm': out['post_norm'], 'loss_target': out['loss_target'], 'm_pre_norm': out['m_pre_norm'], 'm_w_in': out['m_w_in'], 'm_sink_a': out['m_sink_a'], 'm_mem_norm': out['m_mem_norm'], 'm_w_mem_kv': out['m_w_mem_kv'], 'm_w_out': out['m_w_out'], 'm_post_norm': out['m_post_norm'], 'v_pre_norm': out['v_pre_norm'], 'v_w_in': out['v_w_in'], 'v_sink_a': out['v_sink_a'], 'v_mem_norm': out['v_mem_norm'], 'v_w_mem_kv': out['v_w_mem_kv'], 'v_w_out': out['v_w_out'], 'v_post_norm': out['v_post_norm']}


def _loss(weights, diff, rest, loss_target):
    with _jax.named_scope("forward"):
        args = {**rest, TWIN_DIFF_INPUT: diff, **{k: w.astype(_WEIGHT_DTYPES[k]) for k, w in weights.items()}}
        y = _forward(args)
    with _jax.named_scope("loss_head"):
        err = _jnp.square(y.astype(_jnp.float32) - loss_target)
        return 0.5 * _jnp.sum(_jnp.mean(err, axis=-1)) if err.ndim else 0.5 * err


def _adamw(w, g, m, v):
    m = ADAM_B1 * m + (1.0 - ADAM_B1) * g
    v = ADAM_B2 * v + (1.0 - ADAM_B2) * _jnp.square(g)
    m_hat = m / (1.0 - ADAM_B1 ** ADAM_STEP)
    v_hat = v / (1.0 - ADAM_B2 ** ADAM_STEP)
    delta = -ADAM_LR * (m_hat / (_jnp.sqrt(v_hat) + ADAM_EPS) + ADAM_WD * w)
    return delta, m, v


def reference(x, mem, pre_norm, w_in, sink_a, mem_norm, w_mem_kv, w_out, post_norm, loss_target, m_pre_norm, m_w_in, m_sink_a, m_mem_norm, m_w_mem_kv, m_w_out, m_post_norm, v_pre_norm, v_w_in, v_sink_a, v_mem_norm, v_w_mem_kv, v_w_out, v_post_norm):
    given = dict(x=x, mem=mem, pre_norm=pre_norm, w_in=w_in, sink_a=sink_a, mem_norm=mem_norm, w_mem_kv=w_mem_kv, w_out=w_out, post_norm=post_norm, loss_target=loss_target, m_pre_norm=m_pre_norm, m_w_in=m_w_in, m_sink_a=m_sink_a, m_mem_norm=m_mem_norm, m_w_mem_kv=m_w_mem_kv, m_w_out=m_w_out, m_post_norm=m_post_norm, v_pre_norm=v_pre_norm, v_w_in=v_w_in, v_sink_a=v_sink_a, v_mem_norm=v_mem_norm, v_w_mem_kv=v_w_mem_kv, v_w_out=v_w_out, v_post_norm=v_post_norm)
    weights = {n: given[n] for n in TWIN_WEIGHTS}
    shared = {n: given[n] for n in SHARED_INPUTS}
    per_example = {n: given[n] for n in ['x', 'mem']}
    grad_fn = _jax.value_and_grad(_loss, argnums=(0, 1))

    def one_microbatch(ex, loss_target):
        ex = dict(ex)
        diff = ex.pop(TWIN_DIFF_INPUT)
        return grad_fn(weights, diff, {**shared, **ex}, loss_target)

    if N_MICROBATCH == 1:
        loss, (grad_w, grad_x) = one_microbatch(per_example, given["loss_target"])
    else:
        def body(carry, xs):
            loss_sum, grad_sum = carry
            l_k, (gw_k, gx_k) = one_microbatch(xs[0], xs[1])
            with _jax.named_scope("update"):
                return (loss_sum + l_k, _jax.tree.map(_jnp.add, grad_sum, gw_k)), gx_k

        init = (_jnp.zeros((), _jnp.float32), _jax.tree.map(_jnp.zeros_like, weights))
        (loss, grad_w), grad_x = _jax.lax.scan(body, init, (per_example, given["loss_target"]))
    with _jax.named_scope("update"):
        delta_w, new_m, new_v = {}, {}, {}
        for n in TWIN_WEIGHTS:
            delta_w[n], new_m[n], new_v[n] = _adamw(weights[n], grad_w[n], given["m_" + n], given["v_" + n])
    return (loss, grad_x, *[grad_w[n] for n in TWIN_WEIGHTS], *[delta_w[n] for n in TWIN_WEIGHTS],
            *[new_m[n] for n in TWIN_WEIGHTS], *[new_v[n] for n in TWIN_WEIGHTS])
```

```python
import functools

import jax
import jax.numpy as jnp
from jax import lax
from jax.experimental import pallas as pl
from jax.experimental.pallas import tpu as pltpu

F32 = jnp.float32
BF16 = jnp.bfloat16

D_MODEL = 1024
HEAD_DIM = 64
ROT_DIM = 16
ROPE_THETA = 500000.0
BLOCK = 128
LANES = 128
N_MEM = 256
RMS_EPS = 1e-6
SCALE = HEAD_DIM ** -0.5
A_HEADS, A_GROUP = 6, 3
B_HEADS = 6
C_HEADS = 4
A_W, A_KV_W, B_W, C_W = 384, 128, 384, 256
D_IN = 3072
N_DEV = 8
SHARD_IN = D_IN // N_DEV
SHARD_ROWS = D_MODEL // N_DEV
B_CONFIGS = ((128, 1), (512, 4), (2048, 16))
NEG = -1e30
VMEM_LIMIT = 56 * 1024 * 1024

ADAM_LR, ADAM_B1, ADAM_B2, ADAM_EPS, ADAM_WD, ADAM_STEP = 0.001, 0.9, 0.999, 1e-08, 0.01, 10
MESH_ID = pl.DeviceIdType.MESH


def _params(**kw):
    return pltpu.CompilerParams(vmem_limit_bytes=VMEM_LIMIT, **kw)


def _full(shape):
    n = len(shape)
    return pl.BlockSpec(shape, lambda *_: (0,) * n)


def _rope_tables(seq):
    inv_freq = ROPE_THETA ** (-jnp.arange(0, ROT_DIM, 2, dtype=F32) / ROT_DIM)
    ang = jnp.arange(seq, dtype=F32)[:, None] * inv_freq[None, :]
    cos, sin = jnp.cos(ang), jnp.sin(ang)
    z8 = jnp.zeros((seq, 8), F32)
    rest = HEAD_DIM - ROT_DIM
    c64 = jnp.concatenate([cos, cos, jnp.ones((seq, rest), F32)], axis=1)
    up64 = jnp.concatenate([z8, sin, jnp.zeros((seq, rest), F32)], axis=1)
    dn64 = jnp.concatenate([-sin, z8, jnp.zeros((seq, rest), F32)], axis=1)
    two = lambda t: jnp.concatenate([t, t], axis=1)
    return two(c64), two(up64), two(dn64)


def _rotate(t, c, up, dn):
    outs = []
    for j in range(t.shape[1] // LANES):
        tj = t[:, LANES * j:LANES * (j + 1)]
        outs.append(tj * c + pltpu.roll(tj, 8, 1) * up + pltpu.roll(tj, LANES - 8, 1) * dn)
    return outs[0] if len(outs) == 1 else jnp.concatenate(outs, axis=1)


def _inproj(x, pre_g, w_in_full, tabs, tm=512):
    seq = x.shape[0]

    def body(x_ref, g_ref, w_ref, c_ref, up_ref, dn_ref,
             u_ref, qa_ref, ka_ref, va_ref, qb_ref, kb_ref, vb_ref, qc_ref, gate_ref, proj):
        xv = x_ref[...]
        r = lax.rsqrt(jnp.mean(xv * xv, axis=-1, keepdims=True) + RMS_EPS)
        u = ((xv * r) * g_ref[...]).astype(BF16)
        u_ref[...] = u
        for k in range(N_DEV):
            proj[:, SHARD_IN * k:SHARD_IN * (k + 1)] = jnp.dot(u, w_ref[k], preferred_element_type=F32)
        c, up, dn = c_ref[...], up_ref[...], dn_ref[...]
        rot = lambda t: _rotate(t, c, up, dn)
        qa_ref[...] = (rot(proj[:, 0:384]) * SCALE).astype(BF16)
        ka_ref[...] = rot(proj[:, 384:512]).astype(BF16)
        va_ref[...] = proj[:, 512:640].astype(BF16)
        gate_ref[:, 0:384] = proj[:, 640:1024].astype(BF16)
        qb_ref[...] = (rot(proj[:, 1024:1408]) * SCALE).astype(BF16)
        kb_ref[...] = rot(proj[:, 1408:1792]).astype(BF16)
        vb_ref[...] = proj[:, 1792:2176].astype(BF16)
        gate_ref[:, 384:768] = proj[:, 2176:2560].astype(BF16)
        qc_ref[...] = (proj[:, 2560:2816] * SCALE).astype(BF16)
        gate_ref[:, 768:1024] = proj[:, 2816:3072].astype(BF16)

    row = lambda w: pl.BlockSpec((tm, w), lambda i: (i, 0))
    widths = (D_MODEL, A_W, A_KV_W, A_KV_W, B_W, B_W, B_W, C_W, D_MODEL)
    return pl.pallas_call(
        body, name="inproj", grid=(seq // tm,),
        in_specs=[row(D_MODEL), _full((1, D_MODEL)), _full((N_DEV, D_MODEL, SHARD_IN)),
                  row(LANES), row(LANES), row(LANES)],
        out_specs=[row(w) for w in widths],
        out_shape=[jax.ShapeDtypeStruct((seq, w), BF16) for w in widths],
        scratch_shapes=[pltpu.VMEM((tm, D_IN), F32)],
        compiler_params=_params(dimension_semantics=("arbitrary",)),
    )(x, pre_g, w_in_full, *tabs)


def _memkv_fwd(mem, mem_g, w_mem_full):
    def body(mem_ref, g_ref, w_ref, mn_ref, mk_ref, mv_ref):
        mv_ = mem_ref[...]
        r = lax.rsqrt(jnp.mean(mv_ * mv_, axis=-1, keepdims=True) + RMS_EPS)
        mn = ((mv_ * r) * g_ref[...]).astype(BF16)
        mn_ref[...] = mn
        mkv = jnp.dot(mn, w_ref[...], preferred_element_type=F32)
        mk_ref[...] = mkv[:, 0:C_W].astype(BF16)
        mv_ref[...] = mkv[:, C_W:2 * C_W].astype(BF16)

    return pl.pallas_call(
        body, name="memkv_fwd",
        out_shape=[jax.ShapeDtypeStruct((N_MEM, D_MODEL), BF16),
                   jax.ShapeDtypeStruct((N_MEM, C_W), BF16), jax.ShapeDtypeStruct((N_MEM, C_W), BF16)],
        compiler_params=_params(),
    )(mem, mem_g, w_mem_full)


def _memkv_bwd(mem, mem_g, mn, w_mem_full, dmk, dmv):
    def body(mem_ref, g_ref, mn_ref, w_ref, dmk_ref, dmv_ref, dw_ref, st_ref):
        dmkv = jnp.concatenate([dmk_ref[...], dmv_ref[...]], axis=1).astype(BF16)
        dw_ref[...] = lax.dot_general(mn_ref[...], dmkv, (((0,), (0,)), ((), ())), preferred_element_type=F32)
        dmn = lax.dot_general(dmkv, w_ref[...], (((1,), (1,)), ((), ())), preferred_element_type=F32)
        mv_ = mem_ref[...]
        r = lax.rsqrt(jnp.mean(mv_ * mv_, axis=-1, keepdims=True) + RMS_EPS)
        st_ref[...] = jnp.zeros_like(st_ref)
        st_ref[0:1, :] = jnp.sum(dmn * (mv_ * r), axis=0, keepdims=True)

    return pl.pallas_call(
        body, name="memkv_bwd",
        out_shape=[jax.ShapeDtypeStruct((D_MODEL, 2 * C_W), F32), jax.ShapeDtypeStruct((8, D_MODEL), F32)],
        compiler_params=_params(),
    )(mem, mem_g, mn, w_mem_full, dmk, dmv)


def _band_mask(i, max_dist):
    qi = lax.broadcasted_iota(jnp.int32, (BLOCK, 2 * BLOCK), 0)
    kj = lax.broadcasted_iota(jnp.int32, (BLOCK, 2 * BLOCK), 1)
    dist = qi + BLOCK - kj
    return (dist >= 0) & (dist <= max_dist) & ((kj >= BLOCK) | (i > 0))


def _banded_fwd(q, k, v, sink, *, dil, heads, group, max_dist, name):
    seq = q.shape[0]
    kvh = heads // group
    sd = seq // dil
    nb = sd // BLOCK
    qw, kw = heads * HEAD_DIM, kvh * HEAD_DIM

    def body(*refs):
        if sink is not None:
            sink_ref, refs = refs[0], refs[1:]
        q_ref, kp_ref, kc_ref, vp_ref, vc_ref, o_ref, lse_ref = refs
        i = pl.program_id(1)
        qv = q_ref[...]
        kcat = jnp.concatenate([kp_ref[...], kc_ref[...]], axis=0)
        vcat = jnp.concatenate([vp_ref[...], vc_ref[...]], axis=0)
        valid = _band_mask(i, max_dist)
        lane = lax.broadcasted_iota(jnp.int32, (BLOCK, LANES), 1)
        lse_tile = jnp.zeros((BLOCK, LANES), F32)
        outs = []
        for h in range(heads):
            g = h // group
            qh = qv[:, HEAD_DIM * h:HEAD_DIM * (h + 1)]
            kh = kcat[:, HEAD_DIM * g:HEAD_DIM * (g + 1)]
            vh = vcat[:, HEAD_DIM * g:HEAD_DIM * (g + 1)]
            s = lax.dot_general(qh, kh, (((1,), (1,)), ((), ())), preferred_element_type=F32)
            s = jnp.where(valid, s, NEG)
            m = jnp.max(s, axis=-1, keepdims=True)
            if sink is not None:
                sk = sink_ref[0:1, h:h + 1]
                m = jnp.maximum(m, sk)
            p = jnp.exp(s - m)
            l = jnp.sum(p, axis=-1, keepdims=True)
            if sink is not None:
                l = l + jnp.exp(sk - m)
            o = jnp.dot(p.astype(BF16), vh, preferred_element_type=F32)
            outs.append(o / l)
            lse_tile = jnp.where(lane == h, m + jnp.log(l), lse_tile)
        o_ref[...] = jnp.concatenate(outs, axis=1)
        lse_ref[...] = lse_tile

    own = lambda w: pl.BlockSpec((BLOCK, w), lambda r, i: (i, r))
    prev = lambda w: pl.BlockSpec((BLOCK, w), lambda r, i: (jnp.maximum(i - 1, 0), r))
    in_specs = [own(qw), prev(kw), own(kw), prev(kw), own(kw)]
    kview, vview = k.reshape(sd, dil * kw), v.reshape(sd, dil * kw)
    args = [q.reshape(sd, dil * qw), kview, kview, vview, vview]
    if sink is not None:
        in_specs = [_full((8, LANES))] + in_specs
        args = [sink] + args
    o, lse = pl.pallas_call(
        body, name=name, grid=(dil, nb), in_specs=in_specs,
        out_specs=[own(qw), own(LANES)],
        out_shape=[jax.ShapeDtypeStruct((sd, dil * qw), F32), jax.ShapeDtypeStruct((sd, dil * LANES), F32)],
        compiler_params=_params(dimension_semantics=("arbitrary", "arbitrary")),
    )(*args)
    return o.reshape(seq, qw), lse.reshape(seq, LANES)


def _banded_bwd(q, k, v, d_out, out, lse, sink, *, dil, heads, group, max_dist, name):
    seq = q.shape[0]
    kvh = heads // group
    sd = seq // dil
    nb = sd // BLOCK
    qw, kw = heads * HEAD_DIM, kvh * HEAD_DIM
    t_dims = (((0,), (0,)), ((), ()))
    nt_dims = (((1,), (1,)), ((), ()))

    def body(*refs):
        if sink is not None:
            sink_ref, refs = refs[0], refs[1:]
            dsink_ref, refs = refs[8], refs[:8] + refs[9:]
        q_ref, kp_ref, kc_ref, vp_ref, vc_ref, do_ref, o_ref, lse_ref, dq_ref, dk_ref, dv_ref, kcar, vcar = refs
        r, i = pl.program_id(0), pl.program_id(1)

        @pl.when(i == 0)
        def _():
            kcar[...] = jnp.zeros_like(kcar)
            vcar[...] = jnp.zeros_like(vcar)

        if sink is not None:
            @pl.when((i == 0) & (r == 0))
            def _():
                dsink_ref[...] = jnp.zeros_like(dsink_ref)

        @pl.when(i < nb)
        def _():
            qv = q_ref[...]
            kcat = jnp.concatenate([kp_ref[...], kc_ref[...]], axis=0)
            vcat = jnp.concatenate([vp_ref[...], vc_ref[...]], axis=0)
            dov = do_ref[...]
            ov = o_ref[...]
            lse_tile = lse_ref[...]
            valid = _band_mask(i, max_dist)
            lane = lax.broadcasted_iota(jnp.int32, (1, LANES), 1)
            dqs = []
            dks = [None] * kvh
            dvs = [None] * kvh
            dsink_row = jnp.zeros((1, LANES), F32)
            for h in range(heads):
                g = h // group
                qh = qv[:, HEAD_DIM * h:HEAD_DIM * (h + 1)]
                doh = dov[:, HEAD_DIM * h:HEAD_DIM * (h + 1)]
                oh = ov[:, HEAD_DIM * h:HEAD_DIM * (h + 1)]
                kh = kcat[:, HEAD_DIM * g:HEAD_DIM * (g + 1)]
                vh = vcat[:, HEAD_DIM * g:HEAD_DIM * (g + 1)]
                lse_h = lse_tile[:, h:h + 1]
                s = lax.dot_general(qh, kh, nt_dims, preferred_element_type=F32)
                p = jnp.where(valid, jnp.exp(s - lse_h), 0.0)
                dp = lax.dot_general(doh, vh, nt_dims, preferred_element_type=F32)
                delta = jnp.sum(doh.astype(F32) * oh, axis=-1, keepdims=True)
                ds = (p * (dp - delta)).astype(BF16)
                dqs.append(jnp.dot(ds, kh, preferred_element_type=F32))
                dk_h = lax.dot_general(ds, qh, t_dims, preferred_element_type=F32)
                dv_h = lax.dot_general(p.astype(BF16), doh, t_dims, preferred_element_type=F32)
                dks[g] = dk_h if dks[g] is None else dks[g] + dk_h
                dvs[g] = dv_h if dvs[g] is None else dvs[g] + dv_h
                if sink is not None:
                    sk = sink_ref[0:1, h:h + 1]
                    ds_sink = jnp.sum(-jnp.exp(sk - lse_h) * delta, axis=0, keepdims=True)
                    dsink_row = jnp.where(lane == h, ds_sink, dsink_row)
            dq_ref[...] = jnp.concatenate(dqs, axis=1)
            dkcat = dks[0] if kvh == 1 else jnp.concatenate(dks, axis=1)
            dvcat = dvs[0] if kvh == 1 else jnp.concatenate(dvs, axis=1)
            dk_ref[...] = kcar[...] + dkcat[0:BLOCK]
            dv_ref[...] = vcar[...] + dvcat[0:BLOCK]
            kcar[...] = dkcat[BLOCK:2 * BLOCK]
            vcar[...] = dvcat[BLOCK:2 * BLOCK]
            if sink is not None:
                dsink_ref[0:1, :] += dsink_row

        @pl.when(i == nb)
        def _():
            dk_ref[...] = kcar[...]
            dv_ref[...] = vcar[...]

    cur = lambda i: jnp.minimum(i, nb - 1)
    own = lambda w: pl.BlockSpec((BLOCK, w), lambda r, i: (cur(i), r))
    prev = lambda w: pl.BlockSpec((BLOCK, w), lambda r, i: (jnp.maximum(cur(i) - 1, 0), r))
    late = lambda w: pl.BlockSpec((BLOCK, w), lambda r, i: (jnp.maximum(i - 1, 0), r))
    kview, vview = k.reshape(sd, dil * kw), v.reshape(sd, dil * kw)
    in_specs = [own(qw), prev(kw), own(kw), prev(kw), own(kw), own(qw), own(qw), own(LANES)]
    args = [q.reshape(sd, dil * qw), kview, kview, vview, vview,
            d_out.reshape(sd, dil * qw), out.reshape(sd, dil * qw), lse.reshape(sd, dil * LANES)]
    out_specs = [own(qw), late(kw), late(kw)]
    out_shape = [jax.ShapeDtypeStruct((sd, dil * qw), F32), jax.ShapeDtypeStruct((sd, dil * kw), F32),
                 jax.ShapeDtypeStruct((sd, dil * kw), F32)]
    if sink is not None:
        in_specs = [_full((8, LANES))] + in_specs
        args = [sink] + args
        out_specs = [_full((8, LANES))] + out_specs
        out_shape = [jax.ShapeDtypeStruct((8, LANES), F32)] + out_shape
    res = pl.pallas_call(
        body, name=name, grid=(dil, nb + 1), in_specs=in_specs, out_specs=out_specs, out_shape=out_shape,
        scratch_shapes=[pltpu.VMEM((BLOCK, kw), F32), pltpu.VMEM((BLOCK, kw), F32)],
        compiler_params=_params(dimension_semantics=("arbitrary", "arbitrary")),
    )(*args)
    if sink is not None:
        dsink, res = res[0], res[1:]
    dq, dk, dv = res
    dq, dk, dv = dq.reshape(seq, qw), dk.reshape(seq, kw), dv.reshape(seq, kw)
    return (dq, dk, dv, dsink) if sink is not None else (dq, dk, dv)


def _cross_fwd(q, mk, mv, tq=512):
    seq = q.shape[0]

    def body(q_ref, mk_ref, mv_ref, o_ref, lse_ref):
        qv, mkv, mvv = q_ref[...], mk_ref[...], mv_ref[...]
        lane = lax.broadcasted_iota(jnp.int32, (tq, LANES), 1)
        lse_tile = jnp.zeros((tq, LANES), F32)
        outs = []
        for h in range(C_HEADS):
            sl = slice(HEAD_DIM * h, HEAD_DIM * (h + 1))
            s = lax.dot_general(qv[:, sl], mkv[:, sl], (((1,), (1,)), ((), ())), preferred_element_type=F32)
            m = jnp.max(s, axis=-1, keepdims=True)
            p = jnp.exp(s - m)
            l = jnp.sum(p, axis=-1, keepdims=True)
            outs.append(jnp.dot(p.astype(BF16), mvv[:, sl], preferred_element_type=F32) / l)
            lse_tile = jnp.where(lane == h, m + jnp.log(l), lse_tile)
        o_ref[...] = jnp.concatenate(outs, axis=1)
        lse_ref[...] = lse_tile

    row = lambda w: pl.BlockSpec((tq, w), lambda i: (i, 0))
    return pl.pallas_call(
        body, name="cross_fwd", grid=(seq // tq,),
        in_specs=[row(C_W), _full((N_MEM, C_W)), _full((N_MEM, C_W))],
        out_specs=[row(C_W), row(LANES)],
        out_shape=[jax.ShapeDtypeStruct((seq, C_W), F32), jax.ShapeDtypeStruct((seq, LANES), F32)],
        compiler_params=_params(dimension_semantics=("arbitrary",)),
    )(q, mk, mv)


def _cross_bwd(q, mk, mv, d_out, out, lse, tq=512):
    seq = q.shape[0]
    t_dims = (((0,), (0,)), ((), ()))
    nt_dims = (((1,), (1,)), ((), ()))

    def body(q_ref, mk_ref, mv_ref, do_ref, o_ref, lse_ref, dq_ref, dmk_ref, dmv_ref):
        @pl.when(pl.program_id(0) == 0)
        def _():
            dmk_ref[...] = jnp.zeros_like(dmk_ref)
            dmv_ref[...] = jnp.zeros_like(dmv_ref)

        qv, mkv, mvv = q_ref[...], mk_ref[...], mv_ref[...]
        dov, ov, lse_tile = do_ref[...], o_ref[...], lse_ref[...]
        dqs, dks, dvs = [], [], []
        for h in range(C_HEADS):
            sl = slice(HEAD_DIM * h, HEAD_DIM * (h + 1))
            qh, kh, vh, doh = qv[:, sl], mkv[:, sl], mvv[:, sl], dov[:, sl]
            s = lax.dot_general(qh, kh, nt_dims, preferred_element_type=F32)
            p = jnp.exp(s - lse_tile[:, h:h + 1])
            dp = lax.dot_general(doh, vh, nt_dims, preferred_element_type=F32)
            delta = jnp.sum(doh.astype(F32) * ov[:, sl], axis=-1, keepdims=True)
            ds = (p * (dp - delta)).astype(BF16)
            dqs.append(jnp.dot(ds, kh, preferred_element_type=F32))
            dks.append(lax.dot_general(ds, qh, t_dims, preferred_element_type=F32))
            dvs.append(lax.dot_general(p.astype(BF16), doh, t_dims, preferred_element_type=F32))
        dq_ref[...] = jnp.concatenate(dqs, axis=1)
        dmk_ref[...] += jnp.concatenate(dks, axis=1)
        dmv_ref[...] += jnp.concatenate(dvs, axis=1)

    row = lambda w: pl.BlockSpec((tq, w), lambda i: (i, 0))
    return pl.pallas_call(
        body, name="cross_bwd", grid=(seq // tq,),
        in_specs=[row(C_W), _full((N_MEM, C_W)), _full((N_MEM, C_W)), row(C_W), row(C_W), row(LANES)],
        out_specs=[row(C_W), _full((N_MEM, C_W)), _full((N_MEM, C_W))],
        out_shape=[jax.ShapeDtypeStruct((seq, C_W), F32), jax.ShapeDtypeStruct((N_MEM, C_W), F32),
                   jax.ShapeDtypeStruct((N_MEM, C_W), F32)],
        compiler_params=_params(dimension_semantics=("arbitrary",)),
    )(q, mk, mv, d_out, out, lse)


def _per_head(tile, width):
    rows = tile.shape[0]
    return jnp.concatenate(
        [jnp.broadcast_to(tile[:, h:h + 1], (rows, HEAD_DIM)) for h in range(width // HEAD_DIM)], axis=1)


def _mid(oa, ob_parts, lse_parts, oc, gate, x, target, w_out_full, post_g, tm=256):
    seq = x.shape[0]

    def body(oa_ref, b1_ref, b4_ref, b16_ref, l1_ref, l4_ref, l16_ref, oc_ref, gate_ref, x_ref, t_ref, w_ref, pg_ref,
             dh_ref, doa_ref, dob_ref, doc_ref, dg_ref, ob_ref, lb_ref, dw_ref, st_ref):
        @pl.when(pl.program_id(0) == 0)
        def _():
            dw_ref[...] = jnp.zeros_like(dw_ref)
            st_ref[...] = jnp.zeros_like(st_ref)

        l1, l4, l16 = l1_ref[...], l4_ref[...], l16_ref[...]
        lm = jnp.maximum(jnp.maximum(l1, l4), l16)
        e1, e4, e16 = jnp.exp(l1 - lm), jnp.exp(l4 - lm), jnp.exp(l16 - lm)
        den = e1 + e4 + e16
        lb_ref[...] = lm + jnp.log(den)
        ob = (_per_head(e1 / den, B_W) * b1_ref[...] + _per_head(e4 / den, B_W) * b4_ref[...]
              + _per_head(e16 / den, B_W) * b16_ref[...])
        ob_ref[...] = ob
        o_all = jnp.concatenate([oa_ref[...], ob, oc_ref[...]], axis=1)
        g = gate_ref[...].astype(F32)
        sig = 1.0 / (1.0 + jnp.exp(-g))
        silu = g * sig
        y = (o_all * silu).astype(BF16)
        w = w_ref[...]
        z = jnp.dot(y, w, preferred_element_type=F32)
        rz = lax.rsqrt(jnp.mean(z * z, axis=-1, keepdims=True) + RMS_EPS)
        hn = z * rz
        pg = pg_ref[...]
        err = (x_ref[...] + hn * pg) - t_ref[...]
        loss = 0.5 * jnp.sum(jnp.mean(err * err, axis=-1, keepdims=True), axis=0, keepdims=True)
        dh = err * (1.0 / D_MODEL)
        dh_ref[...] = dh
        st_ref[0:1, :] += jnp.sum(dh * hn, axis=0, keepdims=True)
        st_ref[1:2, :] += jnp.broadcast_to(loss, (1, D_MODEL))
        dhn = dh * pg
        dz = (rz * (dhn - hn * jnp.mean(dhn * hn, axis=-1, keepdims=True))).astype(BF16)
        dy = lax.dot_general(dz, w, (((1,), (1,)), ((), ())), preferred_element_type=F32)
        dw_ref[...] += lax.dot_general(y, dz, (((0,), (0,)), ((), ())), preferred_element_type=F32)
        d_o = (dy * silu).astype(BF16)
        doa_ref[...] = d_o[:, 0:A_W]
        dob_ref[...] = d_o[:, A_W:A_W + B_W]
        doc_ref[...] = d_o[:, A_W + B_W:D_MODEL]
        dg_ref[...] = (dy * o_all * (sig * (1.0 + g * (1.0 - sig)))).astype(BF16)

    row = lambda w: pl.BlockSpec((tm, w), lambda i: (i, 0))
    in_specs = [row(A_W), row(B_W), row(B_W), row(B_W), row(LANES), row(LANES), row(LANES), row(C_W),
                row(D_MODEL), row(D_MODEL), row(D_MODEL), _full((D_MODEL, D_MODEL)), _full((1, D_MODEL))]
    out_specs = [row(D_MODEL), row(A_W), row(B_W), row(C_W), row(D_MODEL), row(B_W), row(LANES),
                 _full((D_MODEL, D_MODEL)), _full((8, D_MODEL))]
    out_shape = [jax.ShapeDtypeStruct((seq, D_MODEL), F32), jax.ShapeDtypeStruct((seq, A_W), BF16),
                 jax.ShapeDtypeStruct((seq, B_W), BF16), jax.ShapeDtypeStruct((seq, C_W), BF16),
                 jax.ShapeDtypeStruct((seq, D_MODEL), BF16), jax.ShapeDtypeStruct((seq, B_W), F32),
                 jax.ShapeDtypeStruct((seq, LANES), F32), jax.ShapeDtypeStruct((D_MODEL, D_MODEL), F32),
                 jax.ShapeDtypeStruct((8, D_MODEL), F32)]
    return pl.pallas_call(
        body, name="mid", grid=(seq // tm,), in_specs=in_specs, out_specs=out_specs, out_shape=out_shape,
        compiler_params=_params(dimension_semantics=("arbitrary",)),
    )(oa, *ob_parts, *lse_parts, oc, gate, x, target, w_out_full, post_g)


def _inproj_bwd_x(x, dh, pre_g, w_in_full, tabs, dqa, dka, dva, dqb, dkb, dvb, dqc, dgate, tm=256):
    seq = x.shape[0]

    def body(x_ref, dh_ref, g_ref, w_ref, c_ref, up_ref, dn_ref, dqa_ref, dka_ref, dva_ref,
             dqb1, dqb4, dqb16, dkb1, dkb4, dkb16, dvb1, dvb4, dvb16, dqc_ref, dg_ref,
             gx_ref, dp_ref, st_ref):
        @pl.when(pl.program_id(0) == 0)
        def _():
            st_ref[...] = jnp.zeros_like(st_ref)

        c, up, dn = c_ref[...], -up_ref[...], -dn_ref[...]
        unrot = lambda t: _rotate(t, c, up, dn)
        dp_ref[:, 0:384] = (unrot(dqa_ref[...]) * SCALE).astype(BF16)
        dp_ref[:, 384:512] = unrot(dka_ref[...]).astype(BF16)
        dp_ref[:, 512:640] = dva_ref[...].astype(BF16)
        dp_ref[:, 640:1024] = dg_ref[:, 0:384]
        dp_ref[:, 1024:1408] = (unrot(dqb1[...] + dqb4[...] + dqb16[...]) * SCALE).astype(BF16)
        dp_ref[:, 1408:1792] = unrot(dkb1[...] + dkb4[...] + dkb16[...]).astype(BF16)
        dp_ref[:, 1792:2176] = (dvb1[...] + dvb4[...] + dvb16[...]).astype(BF16)
        dp_ref[:, 2176:2560] = dg_ref[:, 384:768]
        dp_ref[:, 2560:2816] = (dqc_ref[...] * SCALE).astype(BF16)
        dp_ref[:, 2816:3072] = dg_ref[:, 768:1024]
        du = jnp.zeros((tm, D_MODEL), F32)
        for k in range(N_DEV):
            du = du + lax.dot_general(dp_ref[:, SHARD_IN * k:SHARD_IN * (k + 1)], w_ref[k],
                                      (((1,), (1,)), ((), ())), preferred_element_type=F32)
        xv = x_ref[...]
        r = lax.rsqrt(jnp.mean(xv * xv, axis=-1, keepdims=True) + RMS_EPS)
        xh = xv * r
        st_ref[0:1, :] += jnp.sum(du * xh, axis=0, keepdims=True)
        dxh = du * g_ref[...]
        gx_ref[...] = dh_ref[...] + r * (dxh - xh * jnp.mean(dxh * xh, axis=-1, keepdims=True))

    row = lambda w: pl.BlockSpec((tm, w), lambda i: (i, 0))
    in_specs = ([row(D_MODEL), row(D_MODEL), _full((1, D_MODEL)), _full((N_DEV, D_MODEL, SHARD_IN)),
                 row(LANES), row(LANES), row(LANES), row(A_W), row(A_KV_W), row(A_KV_W)]
                + [row(B_W)] * 9 + [row(C_W), row(D_MODEL)])
    return pl.pallas_call(
        body, name="inproj_bwd_x", grid=(seq // tm,), in_specs=in_specs,
        out_specs=[row(D_MODEL), row(D_IN), _full((8, D_MODEL))],
        out_shape=[jax.ShapeDtypeStruct((seq, D_MODEL), F32), jax.ShapeDtypeStruct((seq, D_IN), BF16),
                   jax.ShapeDtypeStruct((8, D_MODEL), F32)],
        compiler_params=_params(dimension_semantics=("arbitrary",)),
    )(x, dh, pre_g, w_in_full, *tabs, dqa, dka, dva, *dqb, *dkb, *dvb, dqc, dgate)


def _inproj_bwd_w(u, dproj, tm=1024):
    seq = u.shape[0]

    def body(u_ref, dp_ref, dw_ref):
        @pl.when(pl.program_id(1) == 0)
        def _():
            dw_ref[...] = jnp.zeros_like(dw_ref)

        dw_ref[0] += lax.dot_general(u_ref[...], dp_ref[...], (((0,), (0,)), ((), ())), preferred_element_type=F32)

    return pl.pallas_call(
        body, name="inproj_bwd_w", grid=(N_DEV, seq // tm),
        in_specs=[pl.BlockSpec((tm, D_MODEL), lambda k, t: (t, 0)), pl.BlockSpec((tm, SHARD_IN), lambda k, t: (t, k))],
        out_specs=pl.BlockSpec((1, D_MODEL, SHARD_IN), lambda k, t: (k, 0, 0)),
        out_shape=jax.ShapeDtypeStruct((N_DEV, D_MODEL, SHARD_IN), F32),
        compiler_params=_params(dimension_semantics=("arbitrary", "arbitrary")),
    )(u, dproj)


def _local_step(x, mem, pre_g, w_in_full, sink, mem_g, w_mem_full, w_out_full, post_g, target):
    seq = x.shape[0]
    tabs = _rope_tables(seq)
    u, qa, ka, va, qb, kb, vb, qc, gate = _inproj(x, pre_g, w_in_full, tabs)
    mn, mk, mv = _memkv_fwd(mem, mem_g, w_mem_full)

    a_cfg = dict(dil=1, heads=A_HEADS, group=A_GROUP, max_dist=BLOCK - 1)
    b_cfgs = [dict(dil=dil, heads=B_HEADS, group=1, max_dist=win // dil) for win, dil in B_CONFIGS]
    oa, lse_a = _banded_fwd(qa, ka, va, sink, name="attn_a_fwd", **a_cfg)
    ob_parts, lse_parts = [], []
    for cfg in b_cfgs:
        o, l = _banded_fwd(qb, kb, vb, None, name=f"attn_b{cfg['dil']}_fwd", **cfg)
        ob_parts.append(o)
        lse_parts.append(l)
    oc, lse_c = _cross_fwd(qc, mk, mv)

    dh, do_a, do_b, do_c, d_gate, ob, lse_b, d_wout, st_mid = _mid(
        oa, ob_parts, lse_parts, oc, gate, x, target, w_out_full, post_g)
    dqa, dka, dva, dsink = _banded_bwd(qa, ka, va, do_a, oa, lse_a, sink, name="attn_a_bwd", **a_cfg)
    dqb, dkb, dvb = [], [], []
    for cfg in b_cfgs:
        dq, dk, dv = _banded_bwd(qb, kb, vb, do_b, ob, lse_b, None, name=f"attn_b{cfg['dil']}_bwd", **cfg)
        dqb.append(dq)
        dkb.append(dk)
        dvb.append(dv)
    dqc, dmk, dmv = _cross_bwd(qc, mk, mv, do_c, oc, lse_c)
    d_wmem, st_mem = _memkv_bwd(mem, mem_g, mn, w_mem_full, dmk, dmv)

    grad_x, dproj, st_pre = _inproj_bwd_x(x, dh, pre_g, w_in_full, tabs, dqa, dka, dva, dqb, dkb, dvb, dqc, d_gate)
    d_win = _inproj_bwd_w(u, dproj)

    dsink_row = jnp.pad(dsink[0:1, :], ((0, 0), (0, D_MODEL - LANES)))
    stats = jnp.concatenate([st_pre[0:1], st_mem[0:1], st_mid[0:1], dsink_row, st_mid[1:2],
                             jnp.zeros((3, D_MODEL), F32)], axis=0)
    return grad_x, d_win, d_wmem, d_wout, stats


def _mesh_pos():
    return lax.axis_index("x"), lax.axis_index("y"), lax.axis_index("c")


def _all_gather_weights(w_in, w_mem, w_out):
    def body(win_ref, wm_ref, wo_ref, win_out, wm_out, wo_out, win_b, wm_b, wo_b, send_sems, recv_sems, local_sems):
        x, y, c = _mesh_pos()
        win_b[...] = win_ref[...].astype(BF16)
        wm_b[...] = wm_ref[...].astype(BF16)
        wo_b[...] = wo_ref[...].astype(BF16)
        srcs = (win_b, wm_b, wo_b)
        outs = (win_out, wm_out, wo_out)
        me, sibling = (x, y, c), (x, y, 1 - c)
        chips = [(1 - x, y), (x, 1 - y), (1 - x, 1 - y)]

        def slot(a, p):
            return outs[a].at[4 * p[0] + 2 * p[1] + p[2]]

        def copy(a, k, block, to, src=None):
            return pltpu.make_async_remote_copy(
                src_ref=slot(a, block) if src is None else src, dst_ref=slot(a, block),
                send_sem=send_sems.at[a, k], recv_sem=recv_sems.at[a, k], device_id=to, device_id_type=MESH_ID)

        mine = [pltpu.make_async_copy(srcs[a], slot(a, me), local_sems.at[a]) for a in range(3)]
        for cp in mine:
            cp.start()
        first = []
        for a in range(3):
            first.append(copy(a, 0, me, sibling, src=srcs[a]))
            first += [copy(a, 1 + j, me, (*chip, c), src=srcs[a]) for j, chip in enumerate(chips)]
        for cp in first:
            cp.start()
        passed = []
        for j, chip in enumerate(chips):
            for a in range(3):
                copy(a, 1 + j, (*chip, c), me).wait_recv()
                fwd = copy(a, 4 + j, (*chip, c), sibling)
                fwd.start()
                passed.append(fwd)
        for a in range(3):
            copy(a, 0, sibling, me).wait_recv()
            for j, chip in enumerate(chips):
                copy(a, 4 + j, (*chip, 1 - c), me).wait_recv()
        for cp in first + passed:
            cp.wait_send()
        for cp in mine:
            cp.wait()

    shapes = [(D_MODEL, SHARD_IN), (SHARD_ROWS, 2 * C_W), (SHARD_ROWS, D_MODEL)]
    vmem = pl.BlockSpec(memory_space=pltpu.VMEM)
    hbm = pl.BlockSpec(memory_space=pl.ANY)
    return pl.pallas_call(
        body, name="all_gather_weights",
        in_specs=[vmem] * 3, out_specs=[hbm] * 3,
        out_shape=[jax.ShapeDtypeStruct((N_DEV,) + s, BF16) for s in shapes],
        scratch_shapes=[pltpu.VMEM(s, BF16) for s in shapes]
        + [pltpu.SemaphoreType.DMA((3, 7)), pltpu.SemaphoreType.DMA((3, 7)), pltpu.SemaphoreType.DMA((3,))],
        compiler_params=_params(),
    )(w_in, w_mem, w_out)


def _exchange_grads(d_win, d_wmem, d_wout, stats):
    def body(win, wm, wo, st, r_win, r_wm, r_wo, r_st, send_sems, recv_sems, local_sems):
        x, y, c = _mesh_pos()
        me = 4 * x + 2 * y + c
        ins, outs = (win, wm, wo), (r_win, r_wm, r_wo)

        def src(a, pid):
            return st if a == 3 else ins[a].at[pid]

        def dst(a, pid):
            return r_st.at[pid] if a == 3 else outs[a].at[pid]

        def remote(a, s, peer, s_ref, d_ref):
            return pltpu.make_async_remote_copy(
                src_ref=s_ref, dst_ref=d_ref, send_sem=send_sems.at[a, s], recv_sem=recv_sems.at[a, s],
                device_id=peer, device_id_type=MESH_ID)

        local = [pltpu.make_async_copy(src(a, me), dst(a, me), local_sems.at[a]) for a in range(4)]
        for cp in local:
            cp.start()
        sends, recvs = [], []
        for s in range(1, N_DEV):
            peer = (1 - x if s & 4 else x, 1 - y if s & 2 else y, 1 - c if s & 1 else c)
            pid = 4 * peer[0] + 2 * peer[1] + peer[2]
            for a in range(4):
                sends.append(remote(a, s, peer, src(a, pid), dst(a, me)))
                recvs.append(remote(a, s, peer, src(a, pid), dst(a, pid)))
        for cp in sends:
            cp.start()
        for cp in recvs:
            cp.wait_recv()
        for cp in sends:
            cp.wait_send()
        for cp in local:
            cp.wait()

    hbm = pl.BlockSpec(memory_space=pl.ANY)
    out_shape = [jax.ShapeDtypeStruct(d_win.shape, F32), jax.ShapeDtypeStruct(d_wmem.shape, F32),
                 jax.ShapeDtypeStruct(d_wout.shape, F32), jax.ShapeDtypeStruct((N_DEV,) + stats.shape, F32)]
    return pl.pallas_call(
        body, name="exchange_grads", in_specs=[hbm] * 4, out_specs=[hbm] * 4, out_shape=out_shape,
        scratch_shapes=[pltpu.SemaphoreType.DMA((4, N_DEV)), pltpu.SemaphoreType.DMA((4, N_DEV)),
                        pltpu.SemaphoreType.DMA((4,))],
        compiler_params=_params(),
    )(d_win, d_wmem, d_wout, stats)


def _reduce_adamw(recv, w, m, v, *, tr, name):
    _, rows, cols = recv.shape

    def body(r_ref, w_ref, m_ref, v_ref, g_ref, d_ref, nm_ref, nv_ref):
        g = r_ref[0]
        for s in range(1, N_DEV):
            g = g + r_ref[s]
        g_ref[...] = g
        m2 = ADAM_B1 * m_ref[...] + (1.0 - ADAM_B1) * g
        v2 = ADAM_B2 * v_ref[...] + (1.0 - ADAM_B2) * (g * g)
        nm_ref[...] = m2
        nv_ref[...] = v2
        m_hat = m2 / (1.0 - ADAM_B1 ** ADAM_STEP)
        v_hat = v2 / (1.0 - ADAM_B2 ** ADAM_STEP)
        d_ref[...] = -ADAM_LR * (m_hat / (jnp.sqrt(v_hat) + ADAM_EPS) + ADAM_WD * w_ref[...])

    blk = pl.BlockSpec((tr, cols), lambda i: (i, 0))
    return pl.pallas_call(
        body, name=name, grid=(rows // tr,),
        in_specs=[pl.BlockSpec((N_DEV, tr, cols), lambda i: (0, i, 0)), blk, blk, blk],
        out_specs=[blk] * 4, out_shape=[jax.ShapeDtypeStruct((rows, cols), F32)] * 4,
        compiler_params=_params(dimension_semantics=("arbitrary",)),
    )(recv, w, m, v)


def _pack_rows(pre, memn, post, sink):
    sink_row = jnp.pad(sink, ((0, 0), (0, D_MODEL - A_HEADS)))
    return jnp.concatenate([pre, memn, post, sink_row, jnp.zeros((4, D_MODEL), F32)], axis=0)


def kernel(x, mem, pre_norm, w_in, sink_a, mem_norm, w_mem_kv, w_out, post_norm, loss_target, m_pre_norm, m_w_in, m_sink_a, m_mem_norm, m_w_mem_kv, m_w_out, m_post_norm, v_pre_norm, v_w_in, v_sink_a, v_mem_norm, v_w_mem_kv, v_w_out, v_post_norm):
    w_in_full, w_mem_full, w_out_full = _all_gather_weights(w_in[0], w_mem_kv[0], w_out[0])
    sink_tile = jnp.pad(sink_a, ((0, 7), (0, LANES - A_HEADS)))
    grad_x, d_win, d_wmem, d_wout, stats = _local_step(
        x[0], mem[0], pre_norm, w_in_full, sink_tile, mem_norm, w_mem_full.reshape(D_MODEL, 2 * C_W),
        w_out_full.reshape(D_MODEL, D_MODEL), post_norm, loss_target[0])
    r_win, r_wmem, r_wout, r_stats = _exchange_grads(
        d_win, d_wmem.reshape(N_DEV, SHARD_ROWS, 2 * C_W), d_wout.reshape(N_DEV, SHARD_ROWS, D_MODEL), stats)

    big = {}
    for nm, recv, w, m, v in (("w_in", r_win, w_in, m_w_in, v_w_in),
                              ("w_mem_kv", r_wmem, w_mem_kv, m_w_mem_kv, v_w_mem_kv),
                              ("w_out", r_wout, w_out, m_w_out, v_w_out)):
        res = _reduce_adamw(recv, w[0], m[0], v[0], tr=SHARD_ROWS, name="adamw_" + nm)
        big[nm] = [t[None] for t in res]
    small = _reduce_adamw(
        r_stats, _pack_rows(pre_norm, mem_norm, post_norm, sink_a),
        _pack_rows(m_pre_norm, m_mem_norm, m_post_norm, m_sink_a),
        _pack_rows(v_pre_norm, v_mem_norm, v_post_norm, v_sink_a), tr=8, name="adamw_small")

    def unpack(t):
        return {"pre_norm": t[0:1], "mem_norm": t[1:2], "post_norm": t[2:3], "sink_a": t[3:4, 0:A_HEADS]}

    order = ("pre_norm", "w_in", "sink_a", "mem_norm", "w_mem_kv", "w_out", "post_norm")
    outs = [small[0][4, 0], grad_x[None]]
    for j in range(4):
        sm = unpack(small[j])
        outs += [big[n][j] if n in big else sm[n] for n in order]
    return tuple(outs)
```

```python
import jax
import jax.numpy as jnp
from jax import lax
from jax.experimental import pallas as pl
from jax.experimental.pallas import tpu as pltpu

F32 = jnp.float32
BF16 = jnp.bfloat16

D_MODEL = 1024
HEAD_DIM = 64
ROT_DIM = 16
ROPE_THETA = 500000.0
BLOCK = 128
LANES = 128
N_MEM = 256
RMS_EPS = 1e-6
SCALE = HEAD_DIM ** -0.5
A_HEADS, A_GROUP = 6, 3
B_HEADS = 6
C_HEADS = 4
A_W, A_KV_W, B_W, C_W = 384, 128, 384, 256
D_IN = 3072
N_DEV = 8
SHARD_IN = D_IN // N_DEV
SHARD_ROWS = D_MODEL // N_DEV
B_CONFIGS = ((128, 1), (512, 4), (2048, 16))
DILS = (4, 16)
NEG = -1e30
DELTA_LANE = 64
VMEM_LIMIT = 56 * 1024 * 1024

ADAM_LR, ADAM_B1, ADAM_B2, ADAM_EPS, ADAM_WD, ADAM_STEP = 0.001, 0.9, 0.999, 1e-08, 0.01, 10
MESH_ID = pl.DeviceIdType.MESH


def _params(**kw):
    return pltpu.CompilerParams(vmem_limit_bytes=VMEM_LIMIT, **kw)


def _full(shape):
    n = len(shape)
    return pl.BlockSpec(shape, lambda *_: (0,) * n)


def _row(tm, w):
    return pl.BlockSpec((tm, w), lambda i: (i, 0))


def _perm_view(a, dil):
    return a.reshape(a.shape[0] // (BLOCK * dil), dil, BLOCK, a.shape[1])


def _perm_spec(tm, dil, w):
    per = BLOCK * dil // tm
    return pl.BlockSpec((1, dil, tm // dil, w), lambda i: (i // per, 0, i % per, 0))


def _put(scr, val):
    for c in range(val.shape[1] // LANES):
        scr[c] = val[:, LANES * c:LANES * (c + 1)]


def _get(scr):
    n = scr.shape[0]
    return scr[0] if n == 1 else jnp.concatenate([scr[c] for c in range(n)], axis=1)


def _get_class(scr, r, dil):
    n, rows = scr.shape[0], scr.shape[1]
    parts = [scr.at[c][pl.ds(r, rows // dil, stride=dil), :] for c in range(n)]
    return parts[0] if n == 1 else jnp.concatenate(parts, axis=1)


def _store_permuted(scr, out_ref, dil, dtype):
    for r in range(dil):
        out_ref[0, r] = _get_class(scr, r, dil).astype(dtype)


def _load_permuted(in_ref, scr, dil):
    n, rows = scr.shape[0], scr.shape[1]
    for r in range(dil):
        val = in_ref[0, r].astype(F32)
        for c in range(n):
            scr.at[c][pl.ds(r, rows // dil, stride=dil), :] = val[:, LANES * c:LANES * (c + 1)]
    return _get(scr)


def _rope_tables(seq):
    j = jnp.arange(LANES) % HEAD_DIM
    freq = ROPE_THETA ** (-(2 * (j % (ROT_DIM // 2))).astype(F32) / ROT_DIM)
    ang = jnp.arange(seq, dtype=F32)[:, None] * freq[None, :]
    cos, sin = jnp.cos(ang), jnp.sin(ang)
    half = ROT_DIM // 2
    c = jnp.where(j < ROT_DIM, cos, 1.0)
    up = jnp.where((j >= half) & (j < ROT_DIM), sin, 0.0)
    dn = jnp.where(j < half, -sin, 0.0)
    return c, up, dn


def _rotate128(t, c, up, dn):
    return t * c + pltpu.roll(t, 8, 1) * up + pltpu.roll(t, LANES - 8, 1) * dn


def _rotate(t, c, up, dn):
    outs = [_rotate128(t[:, LANES * j:LANES * (j + 1)], c, up, dn) for j in range(t.shape[1] // LANES)]
    return outs[0] if len(outs) == 1 else jnp.concatenate(outs, axis=1)


def _inproj(x, pre_g, w_in_full, tabs, tm=512):
    seq = x.shape[0]
    n_chunk = D_IN // LANES

    def body(x_ref, g_ref, w_ref, c_ref, up_ref, dn_ref, u_ref, qa_ref, ka_ref, va_ref,
             qb1_ref, kb1_ref, vb1_ref, qb4_ref, kb4_ref, vb4_ref, qb16_ref, kb16_ref, vb16_ref,
             qc_ref, gate_ref, proj):
        xv = x_ref[...]
        r = lax.rsqrt(jnp.mean(xv * xv, axis=-1, keepdims=True) + RMS_EPS)
        u = ((xv * r) * g_ref[...]).astype(BF16)
        u_ref[...] = u
        per = SHARD_IN // LANES
        for k in range(N_DEV):
            acc = jnp.dot(u, w_ref[k], preferred_element_type=F32)
            for c3 in range(per):
                proj[per * k + c3] = acc[:, LANES * c3:LANES * (c3 + 1)]
        c, up, dn = c_ref[...], up_ref[...], dn_ref[...]

        def cols(lo, hi, rot=False, scale=None):
            parts = []
            for ch in range(lo // LANES, hi // LANES):
                t = proj[ch]
                if rot:
                    t = _rotate128(t, c, up, dn)
                if scale is not None:
                    t = t * scale
                parts.append(t)
            return parts[0] if len(parts) == 1 else jnp.concatenate(parts, axis=1)

        qa_ref[...] = cols(0, 384, True, SCALE).astype(BF16)
        ka_ref[...] = cols(384, 512, True).astype(BF16)
        va_ref[...] = cols(512, 640).astype(BF16)
        gate_ref[:, 0:384] = cols(640, 1024).astype(BF16)
        gate_ref[:, 384:768] = cols(2176, 2560).astype(BF16)
        gate_ref[:, 768:1024] = cols(2816, 3072).astype(BF16)
        qc_ref[...] = cols(2560, 2816, False, SCALE).astype(BF16)
        for ch in range(1024 // LANES, 1408 // LANES):
            proj[ch] = _rotate128(proj[ch], c, up, dn) * SCALE
        for ch in range(1408 // LANES, 1792 // LANES):
            proj[ch] = _rotate128(proj[ch], c, up, dn)
        for lo, nat, p4, p16 in ((1024, qb1_ref, qb4_ref, qb16_ref), (1408, kb1_ref, kb4_ref, kb16_ref),
                                 (1792, vb1_ref, vb4_ref, vb16_ref)):
            chunks = range(lo // LANES, lo // LANES + B_W // LANES)
            nat[...] = jnp.concatenate([proj[ch] for ch in chunks], axis=1).astype(BF16)
            for dil, ref in ((4, p4), (16, p16)):
                for rr in range(dil):
                    ref[0, rr] = jnp.concatenate(
                        [proj.at[ch][pl.ds(rr, tm // dil, stride=dil), :] for ch in chunks], axis=1).astype(BF16)

    nat_w = (D_MODEL, A_W, A_KV_W, A_KV_W, B_W, B_W, B_W)
    out_specs = [_row(tm, w) for w in nat_w]
    out_shape = [jax.ShapeDtypeStruct((seq, w), BF16) for w in nat_w]
    for dil in DILS:
        out_specs += [_perm_spec(tm, dil, B_W)] * 3
        out_shape += [jax.ShapeDtypeStruct((seq // (BLOCK * dil), dil, BLOCK, B_W), BF16)] * 3
    out_specs += [_row(tm, C_W), _row(tm, D_MODEL)]
    out_shape += [jax.ShapeDtypeStruct((seq, C_W), BF16), jax.ShapeDtypeStruct((seq, D_MODEL), BF16)]
    res = pl.pallas_call(
        body, name="inproj", grid=(seq // tm,),
        in_specs=[_row(tm, D_MODEL), _full((1, D_MODEL)), _full((N_DEV, D_MODEL, SHARD_IN)),
                  _row(tm, LANES), _row(tm, LANES), _row(tm, LANES)],
        out_specs=out_specs, out_shape=out_shape,
        scratch_shapes=[pltpu.VMEM((n_chunk, tm, LANES), F32)],
        compiler_params=_params(dimension_semantics=("arbitrary",)),
    )(x, pre_g, w_in_full, *tabs)
    u, qa, ka, va = res[0:4]
    qkv_b = {1: res[4:7], 4: [t.reshape(seq, B_W) for t in res[7:10]], 16: [t.reshape(seq, B_W) for t in res[10:13]]}
    return u, qa, ka, va, qkv_b, res[13], res[14]


def _memkv_fwd(mem, mem_g, w_mem_full):
    def body(mem_ref, g_ref, w_ref, mn_ref, mk_ref, mv_ref):
        mv_ = mem_ref[...]
        r = lax.rsqrt(jnp.mean(mv_ * mv_, axis=-1, keepdims=True) + RMS_EPS)
        mn = ((mv_ * r) * g_ref[...]).astype(BF16)
        mn_ref[...] = mn
        mkv = jnp.dot(mn, w_ref[...], preferred_element_type=F32)
        mk_ref[...] = mkv[:, 0:C_W].astype(BF16)
        mv_ref[...] = mkv[:, C_W:2 * C_W].astype(BF16)

    return pl.pallas_call(
        body, name="memkv_fwd",
        out_shape=[jax.ShapeDtypeStruct((N_MEM, D_MODEL), BF16),
                   jax.ShapeDtypeStruct((N_MEM, C_W), BF16), jax.ShapeDtypeStruct((N_MEM, C_W), BF16)],
        compiler_params=_params(),
    )(mem, mem_g, w_mem_full)


def _memkv_bwd(mem, mem_g, mn, w_mem_full, dmk, dmv):
    def body(mem_ref, g_ref, mn_ref, w_ref, dmk_ref, dmv_ref, dw_ref, st_ref):
        dmkv = jnp.concatenate([dmk_ref[...], dmv_ref[...]], axis=1).astype(BF16)
        dw_ref[...] = lax.dot_general(mn_ref[...], dmkv, (((0,), (0,)), ((), ())), preferred_element_type=F32)
        dmn = lax.dot_general(dmkv, w_ref[...], (((1,), (1,)), ((), ())), preferred_element_type=F32)
        mv_ = mem_ref[...]
        r = lax.rsqrt(jnp.mean(mv_ * mv_, axis=-1, keepdims=True) + RMS_EPS)
        st_ref[...] = jnp.zeros_like(st_ref)
        st_ref[0:1, :] = jnp.sum(dmn * (mv_ * r), axis=0, keepdims=True)

    return pl.pallas_call(
        body, name="memkv_bwd",
        out_shape=[jax.ShapeDtypeStruct((D_MODEL, 2 * C_W), F32), jax.ShapeDtypeStruct((8, D_MODEL), F32)],
        compiler_params=_params(),
    )(mem, mem_g, mn, w_mem_full, dmk, dmv)


def _band_mask(i, max_dist):
    qi = lax.broadcasted_iota(jnp.int32, (BLOCK, 2 * BLOCK), 0)
    kj = lax.broadcasted_iota(jnp.int32, (BLOCK, 2 * BLOCK), 1)
    dist = qi + BLOCK - kj
    return (dist >= 0) & (dist <= max_dist) & ((kj >= BLOCK) | (i > 0))


def _banded_fwd(q, k, v, sink, *, dil, heads, group, max_dist, name):
    seq = q.shape[0]
    kvh = heads // group
    nb = seq // dil // BLOCK
    qw, kw = heads * HEAD_DIM, kvh * HEAD_DIM

    def body(*refs):
        if sink is not None:
            sink_ref, refs = refs[0], refs[1:]
        q_ref, kp_ref, kc_ref, vp_ref, vc_ref, o_ref, lse_ref = refs
        i = pl.program_id(1)
        qv = q_ref[...]
        kcat = jnp.concatenate([kp_ref[...], kc_ref[...]], axis=0)
        vcat = jnp.concatenate([vp_ref[...], vc_ref[...]], axis=0)
        valid = _band_mask(i, max_dist)
        lane = lax.broadcasted_iota(jnp.int32, (BLOCK, LANES), 1)
        lse_tile = jnp.zeros((BLOCK, LANES), F32)
        outs = []
        for h in range(heads):
            g = h // group
            qh = qv[:, HEAD_DIM * h:HEAD_DIM * (h + 1)]
            kh = kcat[:, HEAD_DIM * g:HEAD_DIM * (g + 1)]
            vh = vcat[:, HEAD_DIM * g:HEAD_DIM * (g + 1)]
            s = lax.dot_general(qh, kh, (((1,), (1,)), ((), ())), preferred_element_type=F32)
            s = jnp.where(valid, s, NEG)
            m = jnp.max(s, axis=-1, keepdims=True)
            if sink is not None:
                sk = sink_ref[h]
                m = jnp.maximum(m, sk)
            p = jnp.exp(s - m)
            l = jnp.sum(p, axis=-1, keepdims=True)
            if sink is not None:
                l = l + jnp.exp(sk - m)
            o = jnp.dot(p.astype(BF16), vh, preferred_element_type=F32)
            outs.append(o / l)
            lse_tile = jnp.where(lane == h, m + jnp.log(l), lse_tile)
        o_ref[...] = jnp.concatenate(outs, axis=1)
        lse_ref[...] = lse_tile

    own = lambda w: pl.BlockSpec((BLOCK, w), lambda r, i: (i * dil + r, 0))
    prev = lambda w: pl.BlockSpec((BLOCK, w), lambda r, i: (jnp.maximum(i - 1, 0) * dil + r, 0))
    in_specs = [own(qw), prev(kw), own(kw), prev(kw), own(kw)]
    args = [q, k, k, v, v]
    if sink is not None:
        in_specs = [pl.BlockSpec(memory_space=pltpu.SMEM)] + in_specs
        args = [sink] + args
    return pl.pallas_call(
        body, name=name, grid=(dil, nb), in_specs=in_specs,
        out_specs=[own(qw), own(LANES)],
        out_shape=[jax.ShapeDtypeStruct((seq, qw), F32), jax.ShapeDtypeStruct((seq, LANES), F32)],
        compiler_params=_params(dimension_semantics=("arbitrary", "arbitrary")),
    )(*args)


def _banded_bwd(q, k, v, d_out, stat, sink, *, dil, heads, group, max_dist, name):
    seq = q.shape[0]
    kvh = heads // group
    nb = seq // dil // BLOCK
    qw, kw = heads * HEAD_DIM, kvh * HEAD_DIM
    t_dims = (((0,), (0,)), ((), ()))
    nt_dims = (((1,), (1,)), ((), ()))
    n_in = 7

    def body(*refs):
        if sink is not None:
            sink_ref, refs = refs[0], refs[1:]
            dsink_ref, refs = refs[n_in], refs[:n_in] + refs[n_in + 1:]
        q_ref, kp_ref, kc_ref, vp_ref, vc_ref, do_ref, st_ref, dq_ref, dk_ref, dv_ref, kcar, vcar = refs
        r, i = pl.program_id(0), pl.program_id(1)

        @pl.when(i == 0)
        def _():
            kcar[...] = jnp.zeros_like(kcar)
            vcar[...] = jnp.zeros_like(vcar)

        if sink is not None:
            @pl.when((i == 0) & (r == 0))
            def _():
                dsink_ref[...] = jnp.zeros_like(dsink_ref)

        @pl.when(i < nb)
        def _():
            qv = q_ref[...]
            kcat = jnp.concatenate([kp_ref[...], kc_ref[...]], axis=0)
            vcat = jnp.concatenate([vp_ref[...], vc_ref[...]], axis=0)
            dov = do_ref[...]
            st = st_ref[...]
            valid = _band_mask(i, max_dist)
            lane = lax.broadcasted_iota(jnp.int32, (1, LANES), 1)
            dqs = []
            dks = [None] * kvh
            dvs = [None] * kvh
            dsink_row = jnp.zeros((1, LANES), F32)
            for h in range(heads):
                g = h // group
                qh = qv[:, HEAD_DIM * h:HEAD_DIM * (h + 1)]
                doh = dov[:, HEAD_DIM * h:HEAD_DIM * (h + 1)]
                kh = kcat[:, HEAD_DIM * g:HEAD_DIM * (g + 1)]
                vh = vcat[:, HEAD_DIM * g:HEAD_DIM * (g + 1)]
                lse_h = st[:, h:h + 1]
                delta = st[:, DELTA_LANE + h:DELTA_LANE + h + 1]
                s = lax.dot_general(qh, kh, nt_dims, preferred_element_type=F32)
                p = jnp.where(valid, jnp.exp(s - lse_h), 0.0)
                dp = lax.dot_general(doh, vh, nt_dims, preferred_element_type=F32)
                ds = (p * (dp - delta)).astype(BF16)
                dqs.append(jnp.dot(ds, kh, preferred_element_type=F32))
                dk_h = lax.dot_general(ds, qh, t_dims, preferred_element_type=F32)
                dv_h = lax.dot_general(p.astype(BF16), doh, t_dims, preferred_element_type=F32)
                dks[g] = dk_h if dks[g] is None else dks[g] + dk_h
                dvs[g] = dv_h if dvs[g] is None else dvs[g] + dv_h
                if sink is not None:
                    ds_sink = jnp.sum(-jnp.exp(sink_ref[h] - lse_h) * delta, axis=0, keepdims=True)
                    dsink_row = jnp.where(lane == h, ds_sink, dsink_row)
            dq_ref[...] = jnp.concatenate(dqs, axis=1)
            dkcat = dks[0] if kvh == 1 else jnp.concatenate(dks, axis=1)
            dvcat = dvs[0] if kvh == 1 else jnp.concatenate(dvs, axis=1)
            dk_ref[...] = kcar[...] + dkcat[0:BLOCK]
            dv_ref[...] = vcar[...] + dvcat[0:BLOCK]
            kcar[...] = dkcat[BLOCK:2 * BLOCK]
            vcar[...] = dvcat[BLOCK:2 * BLOCK]
            if sink is not None:
                dsink_ref[0:1, :] += dsink_row

        @pl.when(i == nb)
        def _():
            dk_ref[...] = kcar[...]
            dv_ref[...] = vcar[...]

    cur = lambda i: jnp.minimum(i, nb - 1)
    own = lambda w: pl.BlockSpec((BLOCK, w), lambda r, i: (cur(i) * dil + r, 0))
    prev = lambda w: pl.BlockSpec((BLOCK, w), lambda r, i: (jnp.maximum(cur(i) - 1, 0) * dil + r, 0))
    late = lambda w: pl.BlockSpec((BLOCK, w), lambda r, i: (jnp.maximum(i - 1, 0) * dil + r, 0))
    in_specs = [own(qw), prev(kw), own(kw), prev(kw), own(kw), own(qw), own(LANES)]
    args = [q, k, k, v, v, d_out, stat]
    out_specs = [own(qw), late(kw), late(kw)]
    out_shape = [jax.ShapeDtypeStruct((seq, qw), F32), jax.ShapeDtypeStruct((seq, kw), F32),
                 jax.ShapeDtypeStruct((seq, kw), F32)]
    if sink is not None:
        in_specs = [pl.BlockSpec(memory_space=pltpu.SMEM)] + in_specs
        args = [sink] + args
        out_specs = [_full((8, LANES))] + out_specs
        out_shape = [jax.ShapeDtypeStruct((8, LANES), F32)] + out_shape
    res = pl.pallas_call(
        body, name=name, grid=(dil, nb + 1), in_specs=in_specs, out_specs=out_specs, out_shape=out_shape,
        scratch_shapes=[pltpu.VMEM((BLOCK, kw), F32), pltpu.VMEM((BLOCK, kw), F32)],
        compiler_params=_params(dimension_semantics=("arbitrary", "arbitrary")),
    )(*args)
    if sink is not None:
        return res[1], res[2], res[3], res[0]
    return res


def _cross_fwd(q, mk, mv, tq=512):
    seq = q.shape[0]

    def body(q_ref, mk_ref, mv_ref, o_ref, lse_ref):
        qv, mkv, mvv = q_ref[...], mk_ref[...], mv_ref[...]
        lane = lax.broadcasted_iota(jnp.int32, (tq, LANES), 1)
        lse_tile = jnp.zeros((tq, LANES), F32)
        outs = []
        for h in range(C_HEADS):
            sl = slice(HEAD_DIM * h, HEAD_DIM * (h + 1))
            s = lax.dot_general(qv[:, sl], mkv[:, sl], (((1,), (1,)), ((), ())), preferred_element_type=F32)
            m = jnp.max(s, axis=-1, keepdims=True)
            p = jnp.exp(s - m)
            l = jnp.sum(p, axis=-1, keepdims=True)
            outs.append(jnp.dot(p.astype(BF16), mvv[:, sl], preferred_element_type=F32) / l)
            lse_tile = jnp.where(lane == h, m + jnp.log(l), lse_tile)
        o_ref[...] = jnp.concatenate(outs, axis=1)
        lse_ref[...] = lse_tile

    return pl.pallas_call(
        body, name="cross_fwd", grid=(seq // tq,),
        in_specs=[_row(tq, C_W), _full((N_MEM, C_W)), _full((N_MEM, C_W))],
        out_specs=[_row(tq, C_W), _row(tq, LANES)],
        out_shape=[jax.ShapeDtypeStruct((seq, C_W), F32), jax.ShapeDtypeStruct((seq, LANES), F32)],
        compiler_params=_params(dimension_semantics=("arbitrary",)),
    )(q, mk, mv)


def _cross_bwd(q, mk, mv, d_out, stat, tq=512):
    seq = q.shape[0]
    t_dims = (((0,), (0,)), ((), ()))
    nt_dims = (((1,), (1,)), ((), ()))

    def body(q_ref, mk_ref, mv_ref, do_ref, st_ref, dq_ref, dmk_ref, dmv_ref):
        @pl.when(pl.program_id(0) == 0)
        def _():
            dmk_ref[...] = jnp.zeros_like(dmk_ref)
            dmv_ref[...] = jnp.zeros_like(dmv_ref)

        qv, mkv, mvv = q_ref[...], mk_ref[...], mv_ref[...]
        dov, st = do_ref[...], st_ref[...]
        dqs, dks, dvs = [], [], []
        for h in range(C_HEADS):
            sl = slice(HEAD_DIM * h, HEAD_DIM * (h + 1))
            qh, kh, vh, doh = qv[:, sl], mkv[:, sl], mvv[:, sl], dov[:, sl]
            s = lax.dot_general(qh, kh, nt_dims, preferred_element_type=F32)
            p = jnp.exp(s - st[:, h:h + 1])
            dp = lax.dot_general(doh, vh, nt_dims, preferred_element_type=F32)
            ds = (p * (dp - st[:, DELTA_LANE + h:DELTA_LANE + h + 1])).astype(BF16)
            dqs.append(jnp.dot(ds, kh, preferred_element_type=F32))
            dks.append(lax.dot_general(ds, qh, t_dims, preferred_element_type=F32))
            dvs.append(lax.dot_general(p.astype(BF16), doh, t_dims, preferred_element_type=F32))
        dq_ref[...] = jnp.concatenate(dqs, axis=1)
        dmk_ref[...] += jnp.concatenate(dks, axis=1)
        dmv_ref[...] += jnp.concatenate(dvs, axis=1)

    return pl.pallas_call(
        body, name="cross_bwd", grid=(seq // tq,),
        in_specs=[_row(tq, C_W), _full((N_MEM, C_W)), _full((N_MEM, C_W)), _row(tq, C_W), _row(tq, LANES)],
        out_specs=[_row(tq, C_W), _full((N_MEM, C_W)), _full((N_MEM, C_W))],
        out_shape=[jax.ShapeDtypeStruct((seq, C_W), F32), jax.ShapeDtypeStruct((N_MEM, C_W), F32),
                   jax.ShapeDtypeStruct((N_MEM, C_W), F32)],
        compiler_params=_params(dimension_semantics=("arbitrary",)),
    )(q, mk, mv, d_out, stat)


def _per_head(tile, width):
    rows = tile.shape[0]
    return jnp.concatenate(
        [jnp.broadcast_to(tile[:, h:h + 1], (rows, HEAD_DIM)) for h in range(width // HEAD_DIM)], axis=1)


def _with_delta(lse_tile, prod):
    rows = lse_tile.shape[0]
    lane = lax.broadcasted_iota(jnp.int32, (rows, LANES), 1)
    tile = lse_tile
    for h in range(prod.shape[1] // HEAD_DIM):
        d = jnp.sum(prod[:, HEAD_DIM * h:HEAD_DIM * (h + 1)], axis=-1, keepdims=True)
        tile = jnp.where(lane == DELTA_LANE + h, d, tile)
    return tile


def _mid(oa, lse_a, ob, lse_b, oc, lse_c, gate, x, target, w_out_full, post_g, tm=256):
    seq = x.shape[0]
    n_b = B_W // LANES

    def body(oa_ref, la_ref, b1_ref, l1_ref, b4_ref, l4_ref, b16_ref, l16_ref, oc_ref, lc_ref,
             gate_ref, x_ref, t_ref, w_ref, pg_ref,
             dh_ref, dg_ref, doa_ref, sa_ref, dob1_ref, sb1_ref, dob4_ref, sb4_ref, dob16_ref, sb16_ref,
             doc_ref, sc_ref, dw_ref, st_ref, scr_b4, scr_b16, scr_l4, scr_l16, scr_do, scr_sb):
        @pl.when(pl.program_id(0) == 0)
        def _():
            dw_ref[...] = jnp.zeros_like(dw_ref)
            st_ref[...] = jnp.zeros_like(st_ref)

        b1, l1 = b1_ref[...], l1_ref[...]
        b4, l4 = _load_permuted(b4_ref, scr_b4, 4), _load_permuted(l4_ref, scr_l4, 4)
        b16, l16 = _load_permuted(b16_ref, scr_b16, 16), _load_permuted(l16_ref, scr_l16, 16)
        lm = jnp.maximum(jnp.maximum(l1, l4), l16)
        e1, e4, e16 = jnp.exp(l1 - lm), jnp.exp(l4 - lm), jnp.exp(l16 - lm)
        den = e1 + e4 + e16
        lse_b_tile = lm + jnp.log(den)
        ob_v = _per_head(e1 / den, B_W) * b1 + _per_head(e4 / den, B_W) * b4 + _per_head(e16 / den, B_W) * b16
        o_all = jnp.concatenate([oa_ref[...], ob_v, oc_ref[...]], axis=1)
        g = gate_ref[...].astype(F32)
        sig = 1.0 / (1.0 + jnp.exp(-g))
        silu = g * sig
        y = (o_all * silu).astype(BF16)
        w = w_ref[...]
        z = jnp.dot(y, w, preferred_element_type=F32)
        rz = lax.rsqrt(jnp.mean(z * z, axis=-1, keepdims=True) + RMS_EPS)
        hn = z * rz
        pg = pg_ref[...]
        err = (x_ref[...] + hn * pg) - t_ref[...]
        loss = 0.5 * jnp.sum(jnp.mean(err * err, axis=-1, keepdims=True), axis=0, keepdims=True)
        dh = err * (1.0 / D_MODEL)
        dh_ref[...] = dh
        st_ref[0:1, :] += jnp.sum(dh * hn, axis=0, keepdims=True)
        st_ref[1:2, :] += jnp.broadcast_to(loss, (1, D_MODEL))
        dhn = dh * pg
        dz = (rz * (dhn - hn * jnp.mean(dhn * hn, axis=-1, keepdims=True))).astype(BF16)
        dy = lax.dot_general(dz, w, (((1,), (1,)), ((), ())), preferred_element_type=F32)
        dw_ref[...] += lax.dot_general(y, dz, (((0,), (0,)), ((), ())), preferred_element_type=F32)
        dg_ref[...] = (dy * o_all * (sig * (1.0 + g * (1.0 - sig)))).astype(BF16)
        d_o = (dy * silu).astype(BF16)
        prod = d_o.astype(F32) * o_all
        doa_ref[...] = d_o[:, 0:A_W]
        sa_ref[...] = _with_delta(la_ref[...], prod[:, 0:A_W])
        doc_ref[...] = d_o[:, A_W + B_W:D_MODEL]
        sc_ref[...] = _with_delta(lc_ref[...], prod[:, A_W + B_W:D_MODEL])
        d_ob = d_o[:, A_W:A_W + B_W]
        stat_b = _with_delta(lse_b_tile, prod[:, A_W:A_W + B_W])
        dob1_ref[...] = d_ob
        sb1_ref[...] = stat_b
        _put(scr_do, d_ob.astype(F32))
        _put(scr_sb, stat_b)
        _store_permuted(scr_do, dob4_ref, 4, BF16)
        _store_permuted(scr_sb, sb4_ref, 4, F32)
        _store_permuted(scr_do, dob16_ref, 16, BF16)
        _store_permuted(scr_sb, sb16_ref, 16, F32)

    p4 = lambda w: _perm_spec(tm, 4, w)
    p16 = lambda w: _perm_spec(tm, 16, w)
    in_specs = [_row(tm, A_W), _row(tm, LANES), _row(tm, B_W), _row(tm, LANES), p4(B_W), p4(LANES), p16(B_W), p16(LANES),
                _row(tm, C_W), _row(tm, LANES), _row(tm, D_MODEL), _row(tm, D_MODEL), _row(tm, D_MODEL),
                _full((D_MODEL, D_MODEL)), _full((1, D_MODEL))]
    sds = jax.ShapeDtypeStruct
    v4 = lambda w, dt: sds((seq // (BLOCK * 4), 4, BLOCK, w), dt)
    v16 = lambda w, dt: sds((seq // (BLOCK * 16), 16, BLOCK, w), dt)
    out_specs = [_row(tm, D_MODEL), _row(tm, D_MODEL), _row(tm, A_W), _row(tm, LANES), _row(tm, B_W), _row(tm, LANES),
                 p4(B_W), p4(LANES), p16(B_W), p16(LANES), _row(tm, C_W), _row(tm, LANES),
                 _full((D_MODEL, D_MODEL)), _full((8, D_MODEL))]
    out_shape = [sds((seq, D_MODEL), F32), sds((seq, D_MODEL), BF16), sds((seq, A_W), BF16), sds((seq, LANES), F32),
                 sds((seq, B_W), BF16), sds((seq, LANES), F32), v4(B_W, BF16), v4(LANES, F32), v16(B_W, BF16),
                 v16(LANES, F32), sds((seq, C_W), BF16), sds((seq, LANES), F32),
                 sds((D_MODEL, D_MODEL), F32), sds((8, D_MODEL), F32)]
    res = pl.pallas_call(
        body, name="mid", grid=(seq // tm,), in_specs=in_specs, out_specs=out_specs, out_shape=out_shape,
        scratch_shapes=[pltpu.VMEM((n_b, tm, LANES), F32), pltpu.VMEM((n_b, tm, LANES), F32),
                        pltpu.VMEM((1, tm, LANES), F32), pltpu.VMEM((1, tm, LANES), F32),
                        pltpu.VMEM((n_b, tm, LANES), F32), pltpu.VMEM((1, tm, LANES), F32)],
        compiler_params=_params(dimension_semantics=("arbitrary",)),
    )(oa, lse_a, ob[1], lse_b[1], _perm_view(ob[4], 4), _perm_view(lse_b[4], 4), _perm_view(ob[16], 16),
      _perm_view(lse_b[16], 16), oc, lse_c, gate, x, target, w_out_full, post_g)
    dh, d_gate, do_a, st_a, do_b1, st_b1, do_b4, st_b4, do_b16, st_b16, do_c, st_c, d_wout, stats = res
    flat = lambda t: t.reshape(seq, t.shape[-1])
    d_b = {1: (do_b1, st_b1), 4: (flat(do_b4), flat(st_b4)), 16: (flat(do_b16), flat(st_b16))}
    return dh, d_gate, (do_a, st_a), d_b, (do_c, st_c), d_wout, stats


def _inproj_bwd_x(x, dh, pre_g, w_in_full, tabs, dqa, dka, dva, dqkv_b, dqc, dgate, tm=256):
    seq = x.shape[0]
    n_b = B_W // LANES

    def body(x_ref, dh_ref, g_ref, w_ref, c_ref, up_ref, dn_ref, dqa_ref, dka_ref, dva_ref,
             dq1, dk1, dv1, dq4, dk4, dv4, dq16, dk16, dv16, dqc_ref, dg_ref,
             gx_ref, dp_ref, st_ref, scr4, scr16):
        @pl.when(pl.program_id(0) == 0)
        def _():
            st_ref[...] = jnp.zeros_like(st_ref)

        c, up, dn = c_ref[...], -up_ref[...], -dn_ref[...]
        unrot = lambda t: _rotate(t, c, up, dn)
        total = lambda r1, r4, r16: r1[...] + _load_permuted(r4, scr4, 4) + _load_permuted(r16, scr16, 16)
        dp_ref[:, 0:384] = (unrot(dqa_ref[...]) * SCALE).astype(BF16)
        dp_ref[:, 384:512] = unrot(dka_ref[...]).astype(BF16)
        dp_ref[:, 512:640] = dva_ref[...].astype(BF16)
        dp_ref[:, 640:1024] = dg_ref[:, 0:384]
        dp_ref[:, 1024:1408] = (unrot(total(dq1, dq4, dq16)) * SCALE).astype(BF16)
        dp_ref[:, 1408:1792] = unrot(total(dk1, dk4, dk16)).astype(BF16)
        dp_ref[:, 1792:2176] = total(dv1, dv4, dv16).astype(BF16)
        dp_ref[:, 2176:2560] = dg_ref[:, 384:768]
        dp_ref[:, 2560:2816] = (dqc_ref[...] * SCALE).astype(BF16)
        dp_ref[:, 2816:3072] = dg_ref[:, 768:1024]
        du = jnp.zeros((tm, D_MODEL), F32)
        for k in range(N_DEV):
            du = du + lax.dot_general(dp_ref[:, SHARD_IN * k:SHARD_IN * (k + 1)], w_ref[k],
                                      (((1,), (1,)), ((), ())), preferred_element_type=F32)
        xv = x_ref[...]
        r = lax.rsqrt(jnp.mean(xv * xv, axis=-1, keepdims=True) + RMS_EPS)
        xh = xv * r
        st_ref[0:1, :] += jnp.sum(du * xh, axis=0, keepdims=True)
        dxh = du * g_ref[...]
        gx_ref[...] = dh_ref[...] + r * (dxh - xh * jnp.mean(dxh * xh, axis=-1, keepdims=True))

    in_specs = ([_row(tm, D_MODEL), _row(tm, D_MODEL), _full((1, D_MODEL)), _full((N_DEV, D_MODEL, SHARD_IN)),
                 _row(tm, LANES), _row(tm, LANES), _row(tm, LANES), _row(tm, A_W), _row(tm, A_KV_W), _row(tm, A_KV_W)]
                + [_row(tm, B_W)] * 3 + [_perm_spec(tm, 4, B_W)] * 3 + [_perm_spec(tm, 16, B_W)] * 3
                + [_row(tm, C_W), _row(tm, D_MODEL)])
    return pl.pallas_call(
        body, name="inproj_bwd_x", grid=(seq // tm,), in_specs=in_specs,
        out_specs=[_row(tm, D_MODEL), _row(tm, D_IN), _full((8, D_MODEL))],
        out_shape=[jax.ShapeDtypeStruct((seq, D_MODEL), F32), jax.ShapeDtypeStruct((seq, D_IN), BF16),
                   jax.ShapeDtypeStruct((8, D_MODEL), F32)],
        scratch_shapes=[pltpu.VMEM((n_b, tm, LANES), F32), pltpu.VMEM((n_b, tm, LANES), F32)],
        compiler_params=_params(dimension_semantics=("arbitrary",)),
    )(x, dh, pre_g, w_in_full, *tabs, dqa, dka, dva, *dqkv_b[1], *[_perm_view(t, 4) for t in dqkv_b[4]],
      *[_perm_view(t, 16) for t in dqkv_b[16]], dqc, dgate)


def _inproj_bwd_w(u, dproj, tm=1024):
    seq = u.shape[0]

    def body(u_ref, dp_ref, dw_ref):
        @pl.when(pl.program_id(1) == 0)
        def _():
            dw_ref[...] = jnp.zeros_like(dw_ref)

        dw_ref[0] += lax.dot_general(u_ref[...], dp_ref[...], (((0,), (0,)), ((), ())), preferred_element_type=F32)

    return pl.pallas_call(
        body, name="inproj_bwd_w", grid=(N_DEV, seq // tm),
        in_specs=[pl.BlockSpec((tm, D_MODEL), lambda k, t: (t, 0)), pl.BlockSpec((tm, SHARD_IN), lambda k, t: (t, k))],
        out_specs=pl.BlockSpec((1, D_MODEL, SHARD_IN), lambda k, t: (k, 0, 0)),
        out_shape=jax.ShapeDtypeStruct((N_DEV, D_MODEL, SHARD_IN), F32),
        compiler_params=_params(dimension_semantics=("arbitrary", "arbitrary")),
    )(u, dproj)


def _local_step(x, mem, pre_g, w_in_full, sink, mem_g, w_mem_full, w_out_full, post_g, target):
    seq = x.shape[0]
    tabs = _rope_tables(seq)
    u, qa, ka, va, qkv_b, qc, gate = _inproj(x, pre_g, w_in_full, tabs)
    mn, mk, mv = _memkv_fwd(mem, mem_g, w_mem_full)

    a_cfg = dict(dil=1, heads=A_HEADS, group=A_GROUP, max_dist=BLOCK - 1)
    b_cfgs = {dil: dict(dil=dil, heads=B_HEADS, group=1, max_dist=win // dil) for win, dil in B_CONFIGS}
    oa, lse_a = _banded_fwd(qa, ka, va, sink, name="attn_a_fwd", **a_cfg)
    ob, lse_b = {}, {}
    for dil, cfg in b_cfgs.items():
        ob[dil], lse_b[dil] = _banded_fwd(*qkv_b[dil], None, name=f"attn_b{dil}_fwd", **cfg)
    oc, lse_c = _cross_fwd(qc, mk, mv)

    dh, d_gate, d_a, d_b, d_c, d_wout, st_mid = _mid(oa, lse_a, ob, lse_b, oc, lse_c, gate, x, target, w_out_full, post_g)

    dqa, dka, dva, dsink = _banded_bwd(qa, ka, va, *d_a, sink, name="attn_a_bwd", **a_cfg)
    dqkv_b = {dil: _banded_bwd(*qkv_b[dil], *d_b[dil], None, name=f"attn_b{dil}_bwd", **cfg)
              for dil, cfg in b_cfgs.items()}
    dqc, dmk, dmv = _cross_bwd(qc, mk, mv, *d_c)
    d_wmem, st_mem = _memkv_bwd(mem, mem_g, mn, w_mem_full, dmk, dmv)

    grad_x, dproj, st_pre = _inproj_bwd_x(x, dh, pre_g, w_in_full, tabs, dqa, dka, dva, dqkv_b, dqc, d_gate)
    d_win = _inproj_bwd_w(u, dproj)

    dsink_row = jnp.pad(dsink[0:1, :], ((0, 0), (0, D_MODEL - LANES)))
    stats = jnp.concatenate([st_pre[0:1], st_mem[0:1], st_mid[0:1], dsink_row, st_mid[1:2],
                             jnp.zeros((3, D_MODEL), F32)], axis=0)
    return grad_x, d_win, d_wmem, d_wout, stats


def _mesh_pos():
    return lax.axis_index("x"), lax.axis_index("y"), lax.axis_index("c")


def _all_gather_weights(w_in, w_mem, w_out):
    def body(win_ref, wm_ref, wo_ref, win_out, wm_out, wo_out, win_b, wm_b, wo_b, send_sems, recv_sems, local_sems):
        x, y, c = _mesh_pos()
        win_b[...] = win_ref[...].astype(BF16)
        wm_b[...] = wm_ref[...].astype(BF16)
        wo_b[...] = wo_ref[...].astype(BF16)
        srcs = (win_b, wm_b, wo_b)
        outs = (win_out, wm_out, wo_out)
        me, sibling = (x, y, c), (x, y, 1 - c)
        chips = [(1 - x, y), (x, 1 - y), (1 - x, 1 - y)]

        def slot(a, p):
            return outs[a].at[4 * p[0] + 2 * p[1] + p[2]]

        def copy(a, k, block, to, src=None):
            return pltpu.make_async_remote_copy(
                src_ref=slot(a, block) if src is None else src, dst_ref=slot(a, block),
                send_sem=send_sems.at[a, k], recv_sem=recv_sems.at[a, k], device_id=to, device_id_type=MESH_ID)

        mine = [pltpu.make_async_copy(srcs[a], slot(a, me), local_sems.at[a]) for a in range(3)]
        for cp in mine:
            cp.start()
        first = []
        for a in range(3):
            first.append(copy(a, 0, me, sibling, src=srcs[a]))
            first += [copy(a, 1 + j, me, (*chip, c), src=srcs[a]) for j, chip in enumerate(chips)]
        for cp in first:
            cp.start()
        passed = []
        for j, chip in enumerate(chips):
            for a in range(3):
                copy(a, 1 + j, (*chip, c), me).wait_recv()
                fwd = copy(a, 4 + j, (*chip, c), sibling)
                fwd.start()
                passed.append(fwd)
        for a in range(3):
            copy(a, 0, sibling, me).wait_recv()
            for j, chip in enumerate(chips):
                copy(a, 4 + j, (*chip, 1 - c), me).wait_recv()
        for cp in first + passed:
            cp.wait_send()
        for cp in mine:
            cp.wait()

    shapes = [(D_MODEL, SHARD_IN), (SHARD_ROWS, 2 * C_W), (SHARD_ROWS, D_MODEL)]
    vmem = pl.BlockSpec(memory_space=pltpu.VMEM)
    hbm = pl.BlockSpec(memory_space=pl.ANY)
    return pl.pallas_call(
        body, name="all_gather_weights",
        in_specs=[vmem] * 3, out_specs=[hbm] * 3,
        out_shape=[jax.ShapeDtypeStruct((N_DEV,) + s, BF16) for s in shapes],
        scratch_shapes=[pltpu.VMEM(s, BF16) for s in shapes]
        + [pltpu.SemaphoreType.DMA((3, 7)), pltpu.SemaphoreType.DMA((3, 7)), pltpu.SemaphoreType.DMA((3,))],
        compiler_params=_params(),
    )(w_in, w_mem, w_out)


def _exchange_grads(d_win, d_wmem, d_wout, stats):
    def body(win, wm, wo, st, r_win, r_wm, r_wo, r_st, send_sems, recv_sems, local_sems):
        x, y, c = _mesh_pos()
        me = 4 * x + 2 * y + c
        ins, outs = (win, wm, wo), (r_win, r_wm, r_wo)

        def src(a, pid):
            return st if a == 3 else ins[a].at[pid]

        def dst(a, pid):
            return r_st.at[pid] if a == 3 else outs[a].at[pid]

        def remote(a, s, peer, s_ref, d_ref):
            return pltpu.make_async_remote_copy(
                src_ref=s_ref, dst_ref=d_ref, send_sem=send_sems.at[a, s], recv_sem=recv_sems.at[a, s],
                device_id=peer, device_id_type=MESH_ID)

        local = [pltpu.make_async_copy(src(a, me), dst(a, me), local_sems.at[a]) for a in range(4)]
        for cp in local:
            cp.start()
        sends, recvs = [], []
        for s in range(1, N_DEV):
            peer = (1 - x if s & 4 else x, 1 - y if s & 2 else y, 1 - c if s & 1 else c)
            pid = 4 * peer[0] + 2 * peer[1] + peer[2]
            for a in range(4):
                sends.append(remote(a, s, peer, src(a, pid), dst(a, me)))
                recvs.append(remote(a, s, peer, src(a, pid), dst(a, pid)))
        for cp in sends:
            cp.start()
        for cp in recvs:
            cp.wait_recv()
        for cp in sends:
            cp.wait_send()
        for cp in local:
            cp.wait()

    hbm = pl.BlockSpec(memory_space=pl.ANY)
    out_shape = [jax.ShapeDtypeStruct(d_win.shape, F32), jax.ShapeDtypeStruct(d_wmem.shape, F32),
                 jax.ShapeDtypeStruct(d_wout.shape, F32), jax.ShapeDtypeStruct((N_DEV,) + stats.shape, F32)]
    return pl.pallas_call(
        body, name="exchange_grads", in_specs=[hbm] * 4, out_specs=[hbm] * 4, out_shape=out_shape,
        scratch_shapes=[pltpu.SemaphoreType.DMA((4, N_DEV)), pltpu.SemaphoreType.DMA((4, N_DEV)),
                        pltpu.SemaphoreType.DMA((4,))],
        compiler_params=_params(),
    )(d_win, d_wmem, d_wout, stats)


def _reduce_adamw(recv, w, m, v, *, tr, name):
    _, rows, cols = recv.shape

    def body(r_ref, w_ref, m_ref, v_ref, g_ref, d_ref, nm_ref, nv_ref):
        g = r_ref[0]
        for s in range(1, N_DEV):
            g = g + r_ref[s]
        g_ref[...] = g
        m2 = ADAM_B1 * m_ref[...] + (1.0 - ADAM_B1) * g
        v2 = ADAM_B2 * v_ref[...] + (1.0 - ADAM_B2) * (g * g)
        nm_ref[...] = m2
        nv_ref[...] = v2
        m_hat = m2 / (1.0 - ADAM_B1 ** ADAM_STEP)
        v_hat = v2 / (1.0 - ADAM_B2 ** ADAM_STEP)
        d_ref[...] = -ADAM_LR * (m_hat / (jnp.sqrt(v_hat) + ADAM_EPS) + ADAM_WD * w_ref[...])

    blk = pl.BlockSpec((tr, cols), lambda i: (i, 0))
    return pl.pallas_call(
        body, name=name, grid=(rows // tr,),
        in_specs=[pl.BlockSpec((N_DEV, tr, cols), lambda i: (0, i, 0)), blk, blk, blk],
        out_specs=[blk] * 4, out_shape=[jax.ShapeDtypeStruct((rows, cols), F32)] * 4,
        compiler_params=_params(dimension_semantics=("arbitrary",)),
    )(recv, w, m, v)


def _pack_rows(pre, memn, post, sink):
    sink_row = jnp.pad(sink, ((0, 0), (0, D_MODEL - A_HEADS)))
    return jnp.concatenate([pre, memn, post, sink_row, jnp.zeros((4, D_MODEL), F32)], axis=0)


def kernel(x, mem, pre_norm, w_in, sink_a, mem_norm, w_mem_kv, w_out, post_norm, loss_target, m_pre_norm, m_w_in, m_sink_a, m_mem_norm, m_w_mem_kv, m_w_out, m_post_norm, v_pre_norm, v_w_in, v_sink_a, v_mem_norm, v_w_mem_kv, v_w_out, v_post_norm):
    w_in_full, w_mem_full, w_out_full = _all_gather_weights(w_in[0], w_mem_kv[0], w_out[0])
    sink = jnp.pad(sink_a[0], (0, 8 - A_HEADS))
    grad_x, d_win, d_wmem, d_wout, stats = _local_step(
        x[0], mem[0], pre_norm, w_in_full, sink, mem_norm, w_mem_full.reshape(D_MODEL, 2 * C_W),
        w_out_full.reshape(D_MODEL, D_MODEL), post_norm, loss_target[0])
    r_win, r_wmem, r_wout, r_stats = _exchange_grads(
        d_win, d_wmem.reshape(N_DEV, SHARD_ROWS, 2 * C_W), d_wout.reshape(N_DEV, SHARD_ROWS, D_MODEL), stats)

    big = {}
    for nm, recv, w, m, v in (("w_in", r_win, w_in, m_w_in, v_w_in),
                              ("w_mem_kv", r_wmem, w_mem_kv, m_w_mem_kv, v_w_mem_kv),
                              ("w_out", r_wout, w_out, m_w_out, v_w_out)):
        res = _reduce_adamw(recv, w[0], m[0], v[0], tr=SHARD_ROWS, name="adamw_" + nm)
        big[nm] = [t[None] for t in res]
    small = _reduce_adamw(
        r_stats, _pack_rows(pre_norm, mem_norm, post_norm, sink_a),
        _pack_rows(m_pre_norm, m_mem_norm, m_post_norm, m_sink_a),
        _pack_rows(v_pre_norm, v_mem_norm, v_post_norm, v_sink_a), tr=8, name="adamw_small")

    def unpack(t):
        return {"pre_norm": t[0:1], "mem_norm": t[1:2], "post_norm": t[2:3], "sink_a": t[3:4, 0:A_HEADS]}

    order = ("pre_norm", "w_in", "sink_a", "mem_norm", "w_mem_kv", "w_out", "post_norm")
    outs = [small[0][4, 0], grad_x[None]]
    for j in range(4):
        sm = unpack(small[j])
        outs += [big[n][j] if n in big else sm[n] for n in order]
    return tuple(outs)
```

```python
import jax
import jax.numpy as jnp
from jax import lax
from jax.experimental import pallas as pl
from jax.experimental.pallas import tpu as pltpu

F32 = jnp.float32
BF16 = jnp.bfloat16

D_MODEL = 1024
HEAD_DIM = 64
ROT_DIM = 16
ROPE_THETA = 500000.0
BLOCK = 128
LANES = 128
N_MEM = 256
RMS_EPS = 1e-6
SCALE = HEAD_DIM ** -0.5
A_HEADS, A_GROUP = 6, 3
B_HEADS = 6
C_HEADS = 4
A_W, A_KV_W, B_W, C_W = 384, 128, 384, 256
D_IN = 3072
N_DEV = 8
SHARD_IN = D_IN // N_DEV
SHARD_ROWS = D_MODEL // N_DEV
B_CONFIGS = ((128, 1), (512, 4), (2048, 16))
DILS = (4, 16)
NEG = -1e30
DELTA_LANE = 64
VMEM_LIMIT = 56 * 1024 * 1024

ADAM_LR, ADAM_B1, ADAM_B2, ADAM_EPS, ADAM_WD, ADAM_STEP = 0.001, 0.9, 0.999, 1e-08, 0.01, 10
MESH_ID = pl.DeviceIdType.MESH


def _params(**kw):
    return pltpu.CompilerParams(vmem_limit_bytes=VMEM_LIMIT, **kw)


def _full(shape):
    n = len(shape)
    return pl.BlockSpec(shape, lambda *_: (0,) * n)


def _row(tm, w):
    return pl.BlockSpec((tm, w), lambda i: (i, 0))


def _perm_view(a, dil):
    return a.reshape(a.shape[0] // (BLOCK * dil), dil, BLOCK, a.shape[1])


def _perm_spec(tm, dil, w):
    per = BLOCK * dil // tm
    return pl.BlockSpec((1, dil, tm // dil, w), lambda i: (i // per, 0, i % per, 0))


def _put(scr, val):
    for c in range(val.shape[1] // LANES):
        scr[c] = val[:, LANES * c:LANES * (c + 1)]


def _get(scr):
    n = scr.shape[0]
    return scr[0] if n == 1 else jnp.concatenate([scr[c] for c in range(n)], axis=1)


def _get_class(scr, r, dil):
    n, rows = scr.shape[0], scr.shape[1]
    parts = [scr.at[c][pl.ds(r, rows // dil, stride=dil), :] for c in range(n)]
    return parts[0] if n == 1 else jnp.concatenate(parts, axis=1)


def _store_permuted(scr, out_ref, dil, dtype):
    for r in range(dil):
        out_ref[0, r] = _get_class(scr, r, dil).astype(dtype)


def _load_permuted(in_ref, scr, dil):
    n, rows = scr.shape[0], scr.shape[1]
    for r in range(dil):
        val = in_ref[0, r].astype(F32)
        for c in range(n):
            scr.at[c][pl.ds(r, rows // dil, stride=dil), :] = val[:, LANES * c:LANES * (c + 1)]
    return _get(scr)


def _rope_tables(seq):
    j = jnp.arange(LANES) % HEAD_DIM
    freq = ROPE_THETA ** (-(2 * (j % (ROT_DIM // 2))).astype(F32) / ROT_DIM)
    ang = jnp.arange(seq, dtype=F32)[:, None] * freq[None, :]
    cos, sin = jnp.cos(ang), jnp.sin(ang)
    half = ROT_DIM // 2
    c = jnp.where(j < ROT_DIM, cos, 1.0)
    up = jnp.where((j >= half) & (j < ROT_DIM), sin, 0.0)
    dn = jnp.where(j < half, -sin, 0.0)
    return c, up, dn


def _rotate128(t, c, up, dn):
    return t * c + pltpu.roll(t, 8, 1) * up + pltpu.roll(t, LANES - 8, 1) * dn


def _rotate(t, c, up, dn):
    outs = [_rotate128(t[:, LANES * j:LANES * (j + 1)], c, up, dn) for j in range(t.shape[1] // LANES)]
    return outs[0] if len(outs) == 1 else jnp.concatenate(outs, axis=1)


def _inproj(x, pre_g, w_in_full, tabs, tm=512):
    seq = x.shape[0]
    n_chunk = D_IN // LANES

    def body(x_ref, g_ref, w_ref, c_ref, up_ref, dn_ref, u_ref, qa_ref, ka_ref, va_ref,
             qb1_ref, kb1_ref, vb1_ref, qb4_ref, kb4_ref, vb4_ref, qb16_ref, kb16_ref, vb16_ref,
             qc_ref, gate_ref, proj):
        xv = x_ref[...]
        r = lax.rsqrt(jnp.mean(xv * xv, axis=-1, keepdims=True) + RMS_EPS)
        u = ((xv * r) * g_ref[...]).astype(BF16)
        u_ref[...] = u
        per = SHARD_IN // LANES
        for k in range(N_DEV):
            acc = jnp.dot(u, w_ref[k], preferred_element_type=F32)
            for c3 in range(per):
                proj[per * k + c3] = acc[:, LANES * c3:LANES * (c3 + 1)]
        c, up, dn = c_ref[...], up_ref[...], dn_ref[...]

        def cols(lo, hi, rot=False, scale=None):
            parts = []
            for ch in range(lo // LANES, hi // LANES):
                t = proj[ch]
                if rot:
                    t = _rotate128(t, c, up, dn)
                if scale is not None:
                    t = t * scale
                parts.append(t)
            return parts[0] if len(parts) == 1 else jnp.concatenate(parts, axis=1)

        qa_ref[...] = cols(0, 384, True, SCALE).astype(BF16)
        ka_ref[...] = cols(384, 512, True).astype(BF16)
        va_ref[...] = cols(512, 640).astype(BF16)
        gate_ref[:, 0:384] = cols(640, 1024).astype(BF16)
        gate_ref[:, 384:768] = cols(2176, 2560).astype(BF16)
        gate_ref[:, 768:1024] = cols(2816, 3072).astype(BF16)
        qc_ref[...] = cols(2560, 2816, False, SCALE).astype(BF16)
        for ch in range(1024 // LANES, 1408 // LANES):
            proj[ch] = _rotate128(proj[ch], c, up, dn) * SCALE
        for ch in range(1408 // LANES, 1792 // LANES):
            proj[ch] = _rotate128(proj[ch], c, up, dn)
        for lo, nat, p4, p16 in ((1024, qb1_ref, qb4_ref, qb16_ref), (1408, kb1_ref, kb4_ref, kb16_ref),
                                 (1792, vb1_ref, vb4_ref, vb16_ref)):
            chunks = range(lo // LANES, lo // LANES + B_W // LANES)
            nat[...] = jnp.concatenate([proj[ch] for ch in chunks], axis=1).astype(BF16)
            for dil, ref in ((4, p4), (16, p16)):
                for rr in range(dil):
                    ref[0, rr] = jnp.concatenate(
                        [proj.at[ch][pl.ds(rr, tm // dil, stride=dil), :] for ch in chunks], axis=1).astype(BF16)

    nat_w = (D_MODEL, A_W, A_KV_W, A_KV_W, B_W, B_W, B_W)
    out_specs = [_row(tm, w) for w in nat_w]
    out_shape = [jax.ShapeDtypeStruct((seq, w), BF16) for w in nat_w]
    for dil in DILS:
        out_specs += [_perm_spec(tm, dil, B_W)] * 3
        out_shape += [jax.ShapeDtypeStruct((seq // (BLOCK * dil), dil, BLOCK, B_W), BF16)] * 3
    out_specs += [_row(tm, C_W), _row(tm, D_MODEL)]
    out_shape += [jax.ShapeDtypeStruct((seq, C_W), BF16), jax.ShapeDtypeStruct((seq, D_MODEL), BF16)]
    res = pl.pallas_call(
        body, name="inproj", grid=(seq // tm,),
        in_specs=[_row(tm, D_MODEL), _full((1, D_MODEL)), _full((N_DEV, D_MODEL, SHARD_IN)),
                  _row(tm, LANES), _row(tm, LANES), _row(tm, LANES)],
        out_specs=out_specs, out_shape=out_shape,
        scratch_shapes=[pltpu.VMEM((n_chunk, tm, LANES), F32)],
        compiler_params=_params(dimension_semantics=("arbitrary",)),
    )(x, pre_g, w_in_full, *tabs)
    u, qa, ka, va = res[0:4]
    qkv_b = {1: res[4:7], 4: [t.reshape(seq, B_W) for t in res[7:10]], 16: [t.reshape(seq, B_W) for t in res[10:13]]}
    return u, qa, ka, va, qkv_b, res[13], res[14]


def _memkv_fwd(mem, mem_g, w_mem_full):
    def body(mem_ref, g_ref, w_ref, mn_ref, mk_ref, mv_ref):
        mv_ = mem_ref[...]
        r = lax.rsqrt(jnp.mean(mv_ * mv_, axis=-1, keepdims=True) + RMS_EPS)
        mn = ((mv_ * r) * g_ref[...]).astype(BF16)
        mn_ref[...] = mn
        mkv = jnp.dot(mn, w_ref[...], preferred_element_type=F32)
        mk_ref[...] = mkv[:, 0:C_W].astype(BF16)
        mv_ref[...] = mkv[:, C_W:2 * C_W].astype(BF16)

    return pl.pallas_call(
        body, name="memkv_fwd",
        out_shape=[jax.ShapeDtypeStruct((N_MEM, D_MODEL), BF16),
                   jax.ShapeDtypeStruct((N_MEM, C_W), BF16), jax.ShapeDtypeStruct((N_MEM, C_W), BF16)],
        compiler_params=_params(),
    )(mem, mem_g, w_mem_full)


def _memkv_bwd(mem, mem_g, mn, w_mem_full, dmk, dmv):
    def body(mem_ref, g_ref, mn_ref, w_ref, dmk_ref, dmv_ref, dw_ref, st_ref):
        dmkv = jnp.concatenate([dmk_ref[...], dmv_ref[...]], axis=1).astype(BF16)
        dw_ref[...] = lax.dot_general(mn_ref[...], dmkv, (((0,), (0,)), ((), ())), preferred_element_type=F32)
        dmn = lax.dot_general(dmkv, w_ref[...], (((1,), (1,)), ((), ())), preferred_element_type=F32)
        mv_ = mem_ref[...]
        r = lax.rsqrt(jnp.mean(mv_ * mv_, axis=-1, keepdims=True) + RMS_EPS)
        st_ref[...] = jnp.zeros_like(st_ref)
        st_ref[0:1, :] = jnp.sum(dmn * (mv_ * r), axis=0, keepdims=True)

    return pl.pallas_call(
        body, name="memkv_bwd",
        out_shape=[jax.ShapeDtypeStruct((D_MODEL, 2 * C_W), F32), jax.ShapeDtypeStruct((8, D_MODEL), F32)],
        compiler_params=_params(),
    )(mem, mem_g, mn, w_mem_full, dmk, dmv)


def _band_mask(i, max_dist):
    qi = lax.broadcasted_iota(jnp.int32, (BLOCK, 2 * BLOCK), 0)
    kj = lax.broadcasted_iota(jnp.int32, (BLOCK, 2 * BLOCK), 1)
    dist = qi + BLOCK - kj
    return (dist >= 0) & (dist <= max_dist) & ((kj >= BLOCK) | (i > 0))


def _banded_fwd(q, k, v, sink, *, dil, heads, group, max_dist, name):
    seq = q.shape[0]
    kvh = heads // group
    nb = seq // dil // BLOCK
    qw, kw = heads * HEAD_DIM, kvh * HEAD_DIM

    def body(*refs):
        if sink is not None:
            sink_ref, refs = refs[0], refs[1:]
        q_ref, kp_ref, kc_ref, vp_ref, vc_ref, o_ref, lse_ref = refs
        i = pl.program_id(1)
        qv = q_ref[...]
        kcat = jnp.concatenate([kp_ref[...], kc_ref[...]], axis=0)
        vcat = jnp.concatenate([vp_ref[...], vc_ref[...]], axis=0)
        valid = _band_mask(i, max_dist)
        lane = lax.broadcasted_iota(jnp.int32, (BLOCK, LANES), 1)
        lse_tile = jnp.zeros((BLOCK, LANES), F32)
        outs = []
        for h in range(heads):
            g = h // group
            qh = qv[:, HEAD_DIM * h:HEAD_DIM * (h + 1)]
            kh = kcat[:, HEAD_DIM * g:HEAD_DIM * (g + 1)]
            vh = vcat[:, HEAD_DIM * g:HEAD_DIM * (g + 1)]
            s = lax.dot_general(qh, kh, (((1,), (1,)), ((), ())), preferred_element_type=F32)
            s = jnp.where(valid, s, NEG)
            m = jnp.max(s, axis=-1, keepdims=True)
            if sink is not None:
                sk = sink_ref[h]
                m = jnp.maximum(m, sk)
            p = jnp.exp(s - m)
            l = jnp.sum(p, axis=-1, keepdims=True)
            if sink is not None:
                l = l + jnp.exp(sk - m)
            o = jnp.dot(p.astype(BF16), vh, preferred_element_type=F32)
            outs.append(o / l)
            lse_tile = jnp.where(lane == h, m + jnp.log(l), lse_tile)
        o_ref[...] = jnp.concatenate(outs, axis=1)
        lse_ref[...] = lse_tile

    own = lambda w: pl.BlockSpec((BLOCK, w), lambda r, i: (i * dil + r, 0))
    prev = lambda w: pl.BlockSpec((BLOCK, w), lambda r, i: (jnp.maximum(i - 1, 0) * dil + r, 0))
    in_specs = [own(qw), prev(kw), own(kw), prev(kw), own(kw)]
    args = [q, k, k, v, v]
    if sink is not None:
        in_specs = [pl.BlockSpec(memory_space=pltpu.SMEM)] + in_specs
        args = [sink] + args
    return pl.pallas_call(
        body, name=name, grid=(dil, nb), in_specs=in_specs,
        out_specs=[own(qw), own(LANES)],
        out_shape=[jax.ShapeDtypeStruct((seq, qw), F32), jax.ShapeDtypeStruct((seq, LANES), F32)],
        compiler_params=_params(dimension_semantics=("arbitrary", "arbitrary")),
    )(*args)


def _banded_bwd(q, k, v, d_out, stat, sink, *, dil, heads, group, max_dist, name):
    seq = q.shape[0]
    kvh = heads // group
    nb = seq // dil // BLOCK
    qw, kw = heads * HEAD_DIM, kvh * HEAD_DIM
    t_dims = (((0,), (0,)), ((), ()))
    nt_dims = (((1,), (1,)), ((), ()))
    n_in = 7

    def body(*refs):
        if sink is not None:
            sink_ref, refs = refs[0], refs[1:]
            dsink_ref, refs = refs[n_in], refs[:n_in] + refs[n_in + 1:]
        q_ref, kp_ref, kc_ref, vp_ref, vc_ref, do_ref, st_ref, dq_ref, dk_ref, dv_ref, kcar, vcar = refs
        r, i = pl.program_id(0), pl.program_id(1)

        @pl.when(i == 0)
        def _():
            kcar[...] = jnp.zeros_like(kcar)
            vcar[...] = jnp.zeros_like(vcar)

        if sink is not None:
            @pl.when((i == 0) & (r == 0))
            def _():
                dsink_ref[...] = jnp.zeros_like(dsink_ref)

        @pl.when(i < nb)
        def _():
            qv = q_ref[...]
            kcat = jnp.concatenate([kp_ref[...], kc_ref[...]], axis=0)
            vcat = jnp.concatenate([vp_ref[...], vc_ref[...]], axis=0)
            dov = do_ref[...]
            st = st_ref[...]
            valid = _band_mask(i, max_dist)
            lane = lax.broadcasted_iota(jnp.int32, (1, LANES), 1)
            dqs = []
            dks = [None] * kvh
            dvs = [None] * kvh
            dsink_row = jnp.zeros((1, LANES), F32)
            for h in range(heads):
                g = h // group
                qh = qv[:, HEAD_DIM * h:HEAD_DIM * (h + 1)]
                doh = dov[:, HEAD_DIM * h:HEAD_DIM * (h + 1)]
                kh = kcat[:, HEAD_DIM * g:HEAD_DIM * (g + 1)]
                vh = vcat[:, HEAD_DIM * g:HEAD_DIM * (g + 1)]
                lse_h = st[:, h:h + 1]
                delta = st[:, DELTA_LANE + h:DELTA_LANE + h + 1]
                s = lax.dot_general(qh, kh, nt_dims, preferred_element_type=F32)
                p = jnp.where(valid, jnp.exp(s - lse_h), 0.0)
                dp = lax.dot_general(doh, vh, nt_dims, preferred_element_type=F32)
                ds = (p * (dp - delta)).astype(BF16)
                dqs.append(jnp.dot(ds, kh, preferred_element_type=F32))
                dk_h = lax.dot_general(ds, qh, t_dims, preferred_element_type=F32)
                dv_h = lax.dot_general(p.astype(BF16), doh, t_dims, preferred_element_type=F32)
                dks[g] = dk_h if dks[g] is None else dks[g] + dk_h
                dvs[g] = dv_h if dvs[g] is None else dvs[g] + dv_h
                if sink is not None:
                    ds_sink = jnp.sum(-jnp.exp(sink_ref[h] - lse_h) * delta, axis=0, keepdims=True)
                    dsink_row = jnp.where(lane == h, ds_sink, dsink_row)
            dq_ref[...] = jnp.concatenate(dqs, axis=1)
            dkcat = dks[0] if kvh == 1 else jnp.concatenate(dks, axis=1)
            dvcat = dvs[0] if kvh == 1 else jnp.concatenate(dvs, axis=1)
            dk_ref[...] = kcar[...] + dkcat[0:BLOCK]
            dv_ref[...] = vcar[...] + dvcat[0:BLOCK]
            kcar[...] = dkcat[BLOCK:2 * BLOCK]
            vcar[...] = dvcat[BLOCK:2 * BLOCK]
            if sink is not None:
                dsink_ref[0:1, :] += dsink_row

        @pl.when(i == nb)
        def _():
            dk_ref[...] = kcar[...]
            dv_ref[...] = vcar[...]

    cur = lambda i: jnp.minimum(i, nb - 1)
    own = lambda w: pl.BlockSpec((BLOCK, w), lambda r, i: (cur(i) * dil + r, 0))
    prev = lambda w: pl.BlockSpec((BLOCK, w), lambda r, i: (jnp.maximum(cur(i) - 1, 0) * dil + r, 0))
    late = lambda w: pl.BlockSpec((BLOCK, w), lambda r, i: (jnp.maximum(i - 1, 0) * dil + r, 0))
    in_specs = [own(qw), prev(kw), own(kw), prev(kw), own(kw), own(qw), own(LANES)]
    args = [q, k, k, v, v, d_out, stat]
    out_specs = [own(qw), late(kw), late(kw)]
    out_shape = [jax.ShapeDtypeStruct((seq, qw), F32), jax.ShapeDtypeStruct((seq, kw), F32),
                 jax.ShapeDtypeStruct((seq, kw), F32)]
    if sink is not None:
        in_specs = [pl.BlockSpec(memory_space=pltpu.SMEM)] + in_specs
        args = [sink] + args
        out_specs = [_full((8, LANES))] + out_specs
        out_shape = [jax.ShapeDtypeStruct((8, LANES), F32)] + out_shape
    res = pl.pallas_call(
        body, name=name, grid=(dil, nb + 1), in_specs=in_specs, out_specs=out_specs, out_shape=out_shape,
        scratch_shapes=[pltpu.VMEM((BLOCK, kw), F32), pltpu.VMEM((BLOCK, kw), F32)],
        compiler_params=_params(dimension_semantics=("arbitrary", "arbitrary")),
    )(*args)
    if sink is not None:
        return res[1], res[2], res[3], res[0]
    return res


def _cross_fwd(q, mk, mv, tq=512):
    seq = q.shape[0]

    def body(q_ref, mk_ref, mv_ref, o_ref, lse_ref):
        qv, mkv, mvv = q_ref[...], mk_ref[...], mv_ref[...]
        lane = lax.broadcasted_iota(jnp.int32, (tq, LANES), 1)
        lse_tile = jnp.zeros((tq, LANES), F32)
        outs = []
        for h in range(C_HEADS):
            sl = slice(HEAD_DIM * h, HEAD_DIM * (h + 1))
            s = lax.dot_general(qv[:, sl], mkv[:, sl], (((1,), (1,)), ((), ())), preferred_element_type=F32)
            m = jnp.max(s, axis=-1, keepdims=True)
            p = jnp.exp(s - m)
            l = jnp.sum(p, axis=-1, keepdims=True)
            outs.append(jnp.dot(p.astype(BF16), mvv[:, sl], preferred_element_type=F32) / l)
            lse_tile = jnp.where(lane == h, m + jnp.log(l), lse_tile)
        o_ref[...] = jnp.concatenate(outs, axis=1)
        lse_ref[...] = lse_tile

    return pl.pallas_call(
        body, name="cross_fwd", grid=(seq // tq,),
        in_specs=[_row(tq, C_W), _full((N_MEM, C_W)), _full((N_MEM, C_W))],
        out_specs=[_row(tq, C_W), _row(tq, LANES)],
        out_shape=[jax.ShapeDtypeStruct((seq, C_W), F32), jax.ShapeDtypeStruct((seq, LANES), F32)],
        compiler_params=_params(dimension_semantics=("arbitrary",)),
    )(q, mk, mv)


def _cross_bwd(q, mk, mv, d_out, stat, tq=512):
    seq = q.shape[0]
    t_dims = (((0,), (0,)), ((), ()))
    nt_dims = (((1,), (1,)), ((), ()))

    def body(q_ref, mk_ref, mv_ref, do_ref, st_ref, dq_ref, dmk_ref, dmv_ref):
        @pl.when(pl.program_id(0) == 0)
        def _():
            dmk_ref[...] = jnp.zeros_like(dmk_ref)
            dmv_ref[...] = jnp.zeros_like(dmv_ref)

        qv, mkv, mvv = q_ref[...], mk_ref[...], mv_ref[...]
        dov, st = do_ref[...], st_ref[...]
        dqs, dks, dvs = [], [], []
        for h in range(C_HEADS):
            sl = slice(HEAD_DIM * h, HEAD_DIM * (h + 1))
            qh, kh, vh, doh = qv[:, sl], mkv[:, sl], mvv[:, sl], dov[:, sl]
            s = lax.dot_general(qh, kh, nt_dims, preferred_element_type=F32)
            p = jnp.exp(s - st[:, h:h + 1])
            dp = lax.dot_general(doh, vh, nt_dims, preferred_element_type=F32)
            ds = (p * (dp - st[:, DELTA_LANE + h:DELTA_LANE + h + 1])).astype(BF16)
            dqs.append(jnp.dot(ds, kh, preferred_element_type=F32))
            dks.append(lax.dot_general(ds, qh, t_dims, preferred_element_type=F32))
            dvs.append(lax.dot_general(p.astype(BF16), doh, t_dims, preferred_element_type=F32))
        dq_ref[...] = jnp.concatenate(dqs, axis=1)
        dmk_ref[...] += jnp.concatenate(dks, axis=1)
        dmv_ref[...] += jnp.concatenate(dvs, axis=1)

    return pl.pallas_call(
        body, name="cross_bwd", grid=(seq // tq,),
        in_specs=[_row(tq, C_W), _full((N_MEM, C_W)), _full((N_MEM, C_W)), _row(tq, C_W), _row(tq, LANES)],
        out_specs=[_row(tq, C_W), _full((N_MEM, C_W)), _full((N_MEM, C_W))],
        out_shape=[jax.ShapeDtypeStruct((seq, C_W), F32), jax.ShapeDtypeStruct((N_MEM, C_W), F32),
                   jax.ShapeDtypeStruct((N_MEM, C_W), F32)],
        compiler_params=_params(dimension_semantics=("arbitrary",)),
    )(q, mk, mv, d_out, stat)


def _per_head(tile, width):
    rows = tile.shape[0]
    return jnp.concatenate(
        [jnp.broadcast_to(tile[:, h:h + 1], (rows, HEAD_DIM)) for h in range(width // HEAD_DIM)], axis=1)


def _with_delta(lse_tile, prod):
    rows = lse_tile.shape[0]
    lane = lax.broadcasted_iota(jnp.int32, (rows, LANES), 1)
    tile = lse_tile
    for h in range(prod.shape[1] // HEAD_DIM):
        d = jnp.sum(prod[:, HEAD_DIM * h:HEAD_DIM * (h + 1)], axis=-1, keepdims=True)
        tile = jnp.where(lane == DELTA_LANE + h, d, tile)
    return tile


def _mid(oa, lse_a, ob, lse_b, oc, lse_c, gate, x, target, w_out_full, post_g, tm=256):
    seq = x.shape[0]
    n_b = B_W // LANES

    def body(oa_ref, la_ref, b1_ref, l1_ref, b4_ref, l4_ref, b16_ref, l16_ref, oc_ref, lc_ref,
             gate_ref, x_ref, t_ref, w_ref, pg_ref,
             dh_ref, dg_ref, doa_ref, sa_ref, dob1_ref, sb1_ref, dob4_ref, sb4_ref, dob16_ref, sb16_ref,
             doc_ref, sc_ref, dw_ref, st_ref, scr_b4, scr_b16, scr_l4, scr_l16, scr_do, scr_sb):
        @pl.when(pl.program_id(0) == 0)
        def _():
            dw_ref[...] = jnp.zeros_like(dw_ref)
            st_ref[...] = jnp.zeros_like(st_ref)

        b1, l1 = b1_ref[...], l1_ref[...]
        b4, l4 = _load_permuted(b4_ref, scr_b4, 4), _load_permuted(l4_ref, scr_l4, 4)
        b16, l16 = _load_permuted(b16_ref, scr_b16, 16), _load_permuted(l16_ref, scr_l16, 16)
        lm = jnp.maximum(jnp.maximum(l1, l4), l16)
        e1, e4, e16 = jnp.exp(l1 - lm), jnp.exp(l4 - lm), jnp.exp(l16 - lm)
        den = e1 + e4 + e16
        lse_b_tile = lm + jnp.log(den)
        ob_v = _per_head(e1 / den, B_W) * b1 + _per_head(e4 / den, B_W) * b4 + _per_head(e16 / den, B_W) * b16
        o_all = jnp.concatenate([oa_ref[...], ob_v, oc_ref[...]], axis=1)
        g = gate_ref[...].astype(F32)
        sig = 1.0 / (1.0 + jnp.exp(-g))
        silu = g * sig
        y = (o_all * silu).astype(BF16)
        w = w_ref[...]
        z = jnp.dot(y, w, preferred_element_type=F32)
        rz = lax.rsqrt(jnp.mean(z * z, axis=-1, keepdims=True) + RMS_EPS)
        hn = z * rz
        pg = pg_ref[...]
        err = (x_ref[...] + hn * pg) - t_ref[...]
        loss = 0.5 * jnp.sum(jnp.mean(err * err, axis=-1, keepdims=True), axis=0, keepdims=True)
        dh = err * (1.0 / D_MODEL)
        dh_ref[...] = dh
        st_ref[0:1, :] += jnp.sum(dh * hn, axis=0, keepdims=True)
        st_ref[1:2, :] += jnp.broadcast_to(loss, (1, D_MODEL))
        dhn = dh * pg
        dz = (rz * (dhn - hn * jnp.mean(dhn * hn, axis=-1, keepdims=True))).astype(BF16)
        dy = lax.dot_general(dz, w, (((1,), (1,)), ((), ())), preferred_element_type=F32)
        dw_ref[...] += lax.dot_general(y, dz, (((0,), (0,)), ((), ())), preferred_element_type=F32)
        dg_ref[...] = (dy * o_all * (sig * (1.0 + g * (1.0 - sig)))).astype(BF16)
        d_o = (dy * silu).astype(BF16)
        prod = d_o.astype(F32) * o_all
        doa_ref[...] = d_o[:, 0:A_W]
        sa_ref[...] = _with_delta(la_ref[...], prod[:, 0:A_W])
        doc_ref[...] = d_o[:, A_W + B_W:D_MODEL]
        sc_ref[...] = _with_delta(lc_ref[...], prod[:, A_W + B_W:D_MODEL])
        d_ob = d_o[:, A_W:A_W + B_W]
        stat_b = _with_delta(lse_b_tile, prod[:, A_W:A_W + B_W])
        dob1_ref[...] = d_ob
        sb1_ref[...] = stat_b
        _put(scr_do, d_ob.astype(F32))
        _put(scr_sb, stat_b)
        _store_permuted(scr_do, dob4_ref, 4, BF16)
        _store_permuted(scr_sb, sb4_ref, 4, F32)
        _store_permuted(scr_do, dob16_ref, 16, BF16)
        _store_permuted(scr_sb, sb16_ref, 16, F32)

    p4 = lambda w: _perm_spec(tm, 4, w)
    p16 = lambda w: _perm_spec(tm, 16, w)
    in_specs = [_row(tm, A_W), _row(tm, LANES), _row(tm, B_W), _row(tm, LANES), p4(B_W), p4(LANES), p16(B_W), p16(LANES),
                _row(tm, C_W), _row(tm, LANES), _row(tm, D_MODEL), _row(tm, D_MODEL), _row(tm, D_MODEL),
                _full((D_MODEL, D_MODEL)), _full((1, D_MODEL))]
    sds = jax.ShapeDtypeStruct
    v4 = lambda w, dt: sds((seq // (BLOCK * 4), 4, BLOCK, w), dt)
    v16 = lambda w, dt: sds((seq // (BLOCK * 16), 16, BLOCK, w), dt)
    out_specs = [_row(tm, D_MODEL), _row(tm, D_MODEL), _row(tm, A_W), _row(tm, LANES), _row(tm, B_W), _row(tm, LANES),
                 p4(B_W), p4(LANES), p16(B_W), p16(LANES), _row(tm, C_W), _row(tm, LANES),
                 _full((D_MODEL, D_MODEL)), _full((8, D_MODEL))]
    out_shape = [sds((seq, D_MODEL), F32), sds((seq, D_MODEL), BF16), sds((seq, A_W), BF16), sds((seq, LANES), F32),
                 sds((seq, B_W), BF16), sds((seq, LANES), F32), v4(B_W, BF16), v4(LANES, F32), v16(B_W, BF16),
                 v16(LANES, F32), sds((seq, C_W), BF16), sds((seq, LANES), F32),
                 sds((D_MODEL, D_MODEL), F32), sds((8, D_MODEL), F32)]
    res = pl.pallas_call(
        body, name="mid", grid=(seq // tm,), in_specs=in_specs, out_specs=out_specs, out_shape=out_shape,
        scratch_shapes=[pltpu.VMEM((n_b, tm, LANES), F32), pltpu.VMEM((n_b, tm, LANES), F32),
                        pltpu.VMEM((1, tm, LANES), F32), pltpu.VMEM((1, tm, LANES), F32),
                        pltpu.VMEM((n_b, tm, LANES), F32), pltpu.VMEM((1, tm, LANES), F32)],
        compiler_params=_params(dimension_semantics=("arbitrary",)),
    )(oa, lse_a, ob[1], lse_b[1], _perm_view(ob[4], 4), _perm_view(lse_b[4], 4), _perm_view(ob[16], 16),
      _perm_view(lse_b[16], 16), oc, lse_c, gate, x, target, w_out_full, post_g)
    dh, d_gate, do_a, st_a, do_b1, st_b1, do_b4, st_b4, do_b16, st_b16, do_c, st_c, d_wout, stats = res
    flat = lambda t: t.reshape(seq, t.shape[-1])
    d_b = {1: (do_b1, st_b1), 4: (flat(do_b4), flat(st_b4)), 16: (flat(do_b16), flat(st_b16))}
    return dh, d_gate, (do_a, st_a), d_b, (do_c, st_c), d_wout, stats


def _inproj_bwd_x(x, dh, pre_g, w_in_full, tabs, dqa, dka, dva, dqkv_b, dqc, dgate, tm=256):
    seq = x.shape[0]
    n_b = B_W // LANES

    def body(x_ref, dh_ref, g_ref, w_ref, c_ref, up_ref, dn_ref, dqa_ref, dka_ref, dva_ref,
             dq1, dk1, dv1, dq4, dk4, dv4, dq16, dk16, dv16, dqc_ref, dg_ref,
             gx_ref, dp_ref, st_ref, scr4, scr16):
        @pl.when(pl.program_id(0) == 0)
        def _():
            st_ref[...] = jnp.zeros_like(st_ref)

        c, up, dn = c_ref[...], -up_ref[...], -dn_ref[...]
        unrot = lambda t: _rotate(t, c, up, dn)
        total = lambda r1, r4, r16: r1[...] + _load_permuted(r4, scr4, 4) + _load_permuted(r16, scr16, 16)
        dp_ref[:, 0:384] = (unrot(dqa_ref[...]) * SCALE).astype(BF16)
        dp_ref[:, 384:512] = unrot(dka_ref[...]).astype(BF16)
        dp_ref[:, 512:640] = dva_ref[...].astype(BF16)
        dp_ref[:, 640:1024] = dg_ref[:, 0:384]
        dp_ref[:, 1024:1408] = (unrot(total(dq1, dq4, dq16)) * SCALE).astype(BF16)
        dp_ref[:, 1408:1792] = unrot(total(dk1, dk4, dk16)).astype(BF16)
        dp_ref[:, 1792:2176] = total(dv1, dv4, dv16).astype(BF16)
        dp_ref[:, 2176:2560] = dg_ref[:, 384:768]
        dp_ref[:, 2560:2816] = (dqc_ref[...] * SCALE).astype(BF16)
        dp_ref[:, 2816:3072] = dg_ref[:, 768:1024]
        du = jnp.zeros((tm, D_MODEL), F32)
        for k in range(N_DEV):
            du = du + lax.dot_general(dp_ref[:, SHARD_IN * k:SHARD_IN * (k + 1)], w_ref[k],
                                      (((1,), (1,)), ((), ())), preferred_element_type=F32)
        xv = x_ref[...]
        r = lax.rsqrt(jnp.mean(xv * xv, axis=-1, keepdims=True) + RMS_EPS)
        xh = xv * r
        st_ref[0:1, :] += jnp.sum(du * xh, axis=0, keepdims=True)
        dxh = du * g_ref[...]
        gx_ref[...] = dh_ref[...] + r * (dxh - xh * jnp.mean(dxh * xh, axis=-1, keepdims=True))

    in_specs = ([_row(tm, D_MODEL), _row(tm, D_MODEL), _full((1, D_MODEL)), _full((N_DEV, D_MODEL, SHARD_IN)),
                 _row(tm, LANES), _row(tm, LANES), _row(tm, LANES), _row(tm, A_W), _row(tm, A_KV_W), _row(tm, A_KV_W)]
                + [_row(tm, B_W)] * 3 + [_perm_spec(tm, 4, B_W)] * 3 + [_perm_spec(tm, 16, B_W)] * 3
                + [_row(tm, C_W), _row(tm, D_MODEL)])
    return pl.pallas_call(
        body, name="inproj_bwd_x", grid=(seq // tm,), in_specs=in_specs,
        out_specs=[_row(tm, D_MODEL), _row(tm, D_IN), _full((8, D_MODEL))],
        out_shape=[jax.ShapeDtypeStruct((seq, D_MODEL), F32), jax.ShapeDtypeStruct((seq, D_IN), BF16),
                   jax.ShapeDtypeStruct((8, D_MODEL), F32)],
        scratch_shapes=[pltpu.VMEM((n_b, tm, LANES), F32), pltpu.VMEM((n_b, tm, LANES), F32)],
        compiler_params=_params(dimension_semantics=("arbitrary",)),
    )(x, dh, pre_g, w_in_full, *tabs, dqa, dka, dva, *dqkv_b[1], *[_perm_view(t, 4) for t in dqkv_b[4]],
      *[_perm_view(t, 16) for t in dqkv_b[16]], dqc, dgate)


def _inproj_bwd_w(u, dproj, tm=1024):
    seq = u.shape[0]

    def body(u_ref, dp_ref, dw_ref):
        @pl.when(pl.program_id(1) == 0)
        def _():
            dw_ref[...] = jnp.zeros_like(dw_ref)

        dw_ref[0] += lax.dot_general(u_ref[...], dp_ref[...], (((0,), (0,)), ((), ())), preferred_element_type=F32)

    return pl.pallas_call(
        body, name="inproj_bwd_w", grid=(N_DEV, seq // tm),
        in_specs=[pl.BlockSpec((tm, D_MODEL), lambda k, t: (t, 0)), pl.BlockSpec((tm, SHARD_IN), lambda k, t: (t, k))],
        out_specs=pl.BlockSpec((1, D_MODEL, SHARD_IN), lambda k, t: (k, 0, 0)),
        out_shape=jax.ShapeDtypeStruct((N_DEV, D_MODEL, SHARD_IN), F32),
        compiler_params=_params(dimension_semantics=("arbitrary", "arbitrary")),
    )(u, dproj)


def _local_step(x, mem, pre_g, w_in_full, sink, mem_g, w_mem_full, w_out_full, post_g, target):
    seq = x.shape[0]
    tabs = _rope_tables(seq)
    u, qa, ka, va, qkv_b, qc, gate = _inproj(x, pre_g, w_in_full, tabs)
    mn, mk, mv = _memkv_fwd(mem, mem_g, w_mem_full)

    a_cfg = dict(dil=1, heads=A_HEADS, group=A_GROUP, max_dist=BLOCK - 1)
    b_cfgs = {dil: dict(dil=dil, heads=B_HEADS, group=1, max_dist=win // dil) for win, dil in B_CONFIGS}
    oa, lse_a = _banded_fwd(qa, ka, va, sink, name="attn_a_fwd", **a_cfg)
    ob, lse_b = {}, {}
    for dil, cfg in b_cfgs.items():
        ob[dil], lse_b[dil] = _banded_fwd(*qkv_b[dil], None, name=f"attn_b{dil}_fwd", **cfg)
    oc, lse_c = _cross_fwd(qc, mk, mv)

    dh, d_gate, d_a, d_b, d_c, d_wout, st_mid = _mid(oa, lse_a, ob, lse_b, oc, lse_c, gate, x, target, w_out_full, post_g)

    dqa, dka, dva, dsink = _banded_bwd(qa, ka, va, *d_a, sink, name="attn_a_bwd", **a_cfg)
    dqkv_b = {dil: _banded_bwd(*qkv_b[dil], *d_b[dil], None, name=f"attn_b{dil}_bwd", **cfg)
              for dil, cfg in b_cfgs.items()}
    dqc, dmk, dmv = _cross_bwd(qc, mk, mv, *d_c)
    d_wmem, st_mem = _memkv_bwd(mem, mem_g, mn, w_mem_full, dmk, dmv)

    grad_x, dproj, st_pre = _inproj_bwd_x(x, dh, pre_g, w_in_full, tabs, dqa, dka, dva, dqkv_b, dqc, d_gate)
    d_win = _inproj_bwd_w(u, dproj)

    dsink_row = jnp.pad(dsink[0:1, :], ((0, 0), (0, D_MODEL - LANES)))
    stats = jnp.concatenate([st_pre[0:1], st_mem[0:1], st_mid[0:1], dsink_row, st_mid[1:2],
                             jnp.zeros((3, D_MODEL), F32)], axis=0)
    return grad_x, d_win, d_wmem, d_wout, stats


def _mesh_pos():
    return lax.axis_index("x"), lax.axis_index("y"), lax.axis_index("c")


def _all_gather_weights(w_in, w_mem, w_out):
    def body(win_ref, wm_ref, wo_ref, win_out, wm_out, wo_out, win_b, wm_b, wo_b, send_sems, recv_sems, local_sems):
        x, y, c = _mesh_pos()
        win_b[...] = win_ref[...].astype(BF16)
        wm_b[...] = wm_ref[...].astype(BF16)
        wo_b[...] = wo_ref[...].astype(BF16)
        srcs = (win_b, wm_b, wo_b)
        outs = (win_out, wm_out, wo_out)
        me, sibling = (x, y, c), (x, y, 1 - c)
        chips = [(1 - x, y), (x, 1 - y), (1 - x, 1 - y)]

        def slot(a, p):
            return outs[a].at[4 * p[0] + 2 * p[1] + p[2]]

        def copy(a, k, block, to, src=None):
            return pltpu.make_async_remote_copy(
                src_ref=slot(a, block) if src is None else src, dst_ref=slot(a, block),
                send_sem=send_sems.at[a, k], recv_sem=recv_sems.at[a, k], device_id=to, device_id_type=MESH_ID)

        mine = [pltpu.make_async_copy(srcs[a], slot(a, me), local_sems.at[a]) for a in range(3)]
        for cp in mine:
            cp.start()
        first = []
        for a in range(3):
            first.append(copy(a, 0, me, sibling, src=srcs[a]))
            first += [copy(a, 1 + j, me, (*chip, c), src=srcs[a]) for j, chip in enumerate(chips)]
        for cp in first:
            cp.start()
        passed = []
        for j, chip in enumerate(chips):
            for a in range(3):
                copy(a, 1 + j, (*chip, c), me).wait_recv()
                fwd = copy(a, 4 + j, (*chip, c), sibling)
                fwd.start()
                passed.append(fwd)
        for a in range(3):
            copy(a, 0, sibling, me).wait_recv()
            for j, chip in enumerate(chips):
                copy(a, 4 + j, (*chip, 1 - c), me).wait_recv()
        for cp in first + passed:
            cp.wait_send()
        for cp in mine:
            cp.wait()

    shapes = [(D_MODEL, SHARD_IN), (SHARD_ROWS, 2 * C_W), (SHARD_ROWS, D_MODEL)]
    vmem = pl.BlockSpec(memory_space=pltpu.VMEM)
    hbm = pl.BlockSpec(memory_space=pl.ANY)
    return pl.pallas_call(
        body, name="all_gather_weights",
        in_specs=[vmem] * 3, out_specs=[hbm] * 3,
        out_shape=[jax.ShapeDtypeStruct((N_DEV,) + s, BF16) for s in shapes],
        scratch_shapes=[pltpu.VMEM(s, BF16) for s in shapes]
        + [pltpu.SemaphoreType.DMA((3, 7)), pltpu.SemaphoreType.DMA((3, 7)), pltpu.SemaphoreType.DMA((3,))],
        compiler_params=_params(),
    )(w_in, w_mem, w_out)


def _exchange_grads(d_win, d_wmem, d_wout, stats):
    shapes = [d_win.shape[1:], d_wmem.shape[1:], d_wout.shape[1:]]
    n_arr = len(shapes)

    def body(win, wm, wo, st, g_win, g_wm, g_wo, r_st, *scratch):
        mine, got, snd, rcv = (scratch[n_arr * t:n_arr * (t + 1)] for t in range(4))
        load_sems, d2d_send, d2d_recv, ici_send, ici_recv, st_send, st_recv, st_local = scratch[4 * n_arr:]
        x, y, c = _mesh_pos()
        me = 4 * x + 2 * y + c
        ins, outs = (win, wm, wo), (g_win, g_wm, g_wo)

        def chip(kk):
            return (1 - x if kk & 2 else x, 1 - y if kk & 1 else y)

        st_own = pltpu.make_async_copy(st, r_st.at[me], st_local)
        st_own.start()
        st_copies = []
        for s in range(1, N_DEV):
            peer = (1 - x if s & 4 else x, 1 - y if s & 2 else y, 1 - c if s & 1 else c)
            pid = 4 * peer[0] + 2 * peer[1] + peer[2]
            cp = pltpu.make_async_remote_copy(src_ref=st, dst_ref=r_st.at[me], send_sem=st_send.at[s],
                                              recv_sem=st_recv.at[s], device_id=peer, device_id_type=MESH_ID)
            cp.start()
            arrive = pltpu.make_async_remote_copy(src_ref=st, dst_ref=r_st.at[pid], send_sem=st_send.at[s],
                                                  recv_sem=st_recv.at[s], device_id=peer, device_id_type=MESH_ID)
            st_copies.append((cp, arrive))

        loads, swaps = {}, {}
        for kk in range(4):
            ox, oy = chip(kk)
            for a in range(n_arr):
                ld = pltpu.make_async_copy(ins[a].at[4 * ox + 2 * oy + c], mine[a].at[kk], load_sems.at[a, kk])
                ld.start()
                sw = pltpu.make_async_remote_copy(
                    src_ref=ins[a].at[4 * ox + 2 * oy + (1 - c)], dst_ref=got[a].at[kk],
                    send_sem=d2d_send.at[a, kk], recv_sem=d2d_recv.at[a, kk],
                    device_id=(x, y, 1 - c), device_id_type=MESH_ID)
                sw.start()
                loads[a, kk], swaps[a, kk] = ld, sw

        hops = {}
        for kk in (3, 1, 2):
            ox, oy = chip(kk)
            for a in range(n_arr):
                loads[a, kk].wait()
                swaps[a, kk].wait_recv()
                snd[a][kk - 1] = (mine[a][kk] + got[a][kk]).astype(BF16)
                hop = pltpu.make_async_remote_copy(
                    src_ref=snd[a].at[kk - 1], dst_ref=rcv[a].at[kk - 1],
                    send_sem=ici_send.at[a, kk], recv_sem=ici_recv.at[a, kk],
                    device_id=(ox, oy, c), device_id_type=MESH_ID)
                hop.start()
                hops[a, kk] = hop

        for a in range(n_arr):
            loads[a, 0].wait()
            swaps[a, 0].wait_recv()
            acc = mine[a][0] + got[a][0]
            for kk in (1, 2, 3):
                hops[a, kk].wait_recv()
                acc = acc + rcv[a][kk - 1].astype(F32)
            outs[a][...] = acc

        for sw in swaps.values():
            sw.wait_send()
        for hop in hops.values():
            hop.wait_send()
        for cp, arrive in st_copies:
            arrive.wait_recv()
            cp.wait_send()
        st_own.wait()

    hbm = pl.BlockSpec(memory_space=pl.ANY)
    vmem = pl.BlockSpec(memory_space=pltpu.VMEM)
    out_shape = [jax.ShapeDtypeStruct(s, F32) for s in shapes] + [jax.ShapeDtypeStruct((N_DEV,) + stats.shape, F32)]
    scratch = ([pltpu.VMEM((4,) + s, F32) for s in shapes] + [pltpu.VMEM((4,) + s, F32) for s in shapes]
               + [pltpu.VMEM((3,) + s, BF16) for s in shapes] + [pltpu.VMEM((3,) + s, BF16) for s in shapes]
               + [pltpu.SemaphoreType.DMA((n_arr, 4))] * 5
               + [pltpu.SemaphoreType.DMA((N_DEV,)), pltpu.SemaphoreType.DMA((N_DEV,)), pltpu.SemaphoreType.DMA(())])
    return pl.pallas_call(
        body, name="exchange_grads", in_specs=[hbm] * 4, out_specs=[vmem] * n_arr + [hbm], out_shape=out_shape,
        scratch_shapes=scratch, compiler_params=_params(),
    )(d_win, d_wmem, d_wout, stats)


def _reduce_adamw(recv, w, m, v, *, tr, name):
    n_part, rows, cols = recv.shape

    def body(r_ref, w_ref, m_ref, v_ref, g_ref, d_ref, nm_ref, nv_ref):
        g = r_ref[0]
        for s in range(1, n_part):
            g = g + r_ref[s]
        g_ref[...] = g
        m2 = ADAM_B1 * m_ref[...] + (1.0 - ADAM_B1) * g
        v2 = ADAM_B2 * v_ref[...] + (1.0 - ADAM_B2) * (g * g)
        nm_ref[...] = m2
        nv_ref[...] = v2
        m_hat = m2 / (1.0 - ADAM_B1 ** ADAM_STEP)
        v_hat = v2 / (1.0 - ADAM_B2 ** ADAM_STEP)
        d_ref[...] = -ADAM_LR * (m_hat / (jnp.sqrt(v_hat) + ADAM_EPS) + ADAM_WD * w_ref[...])

    blk = pl.BlockSpec((tr, cols), lambda i: (i, 0))
    return pl.pallas_call(
        body, name=name, grid=(rows // tr,),
        in_specs=[pl.BlockSpec((n_part, tr, cols), lambda i: (0, i, 0)), blk, blk, blk],
        out_specs=[blk] * 4, out_shape=[jax.ShapeDtypeStruct((rows, cols), F32)] * 4,
        compiler_params=_params(dimension_semantics=("arbitrary",)),
    )(recv, w, m, v)


def _pack_rows(pre, memn, post, sink):
    sink_row = jnp.pad(sink, ((0, 0), (0, D_MODEL - A_HEADS)))
    return jnp.concatenate([pre, memn, post, sink_row, jnp.zeros((4, D_MODEL), F32)], axis=0)


def kernel(x, mem, pre_norm, w_in, sink_a, mem_norm, w_mem_kv, w_out, post_norm, loss_target, m_pre_norm, m_w_in, m_sink_a, m_mem_norm, m_w_mem_kv, m_w_out, m_post_norm, v_pre_norm, v_w_in, v_sink_a, v_mem_norm, v_w_mem_kv, v_w_out, v_post_norm):
    w_in_full, w_mem_full, w_out_full = _all_gather_weights(w_in[0], w_mem_kv[0], w_out[0])
    sink = jnp.pad(sink_a[0], (0, 8 - A_HEADS))
    grad_x, d_win, d_wmem, d_wout, stats = _local_step(
        x[0], mem[0], pre_norm, w_in_full, sink, mem_norm, w_mem_full.reshape(D_MODEL, 2 * C_W),
        w_out_full.reshape(D_MODEL, D_MODEL), post_norm, loss_target[0])
    g_win, g_wmem, g_wout, r_stats = _exchange_grads(
        d_win, d_wmem.reshape(N_DEV, SHARD_ROWS, 2 * C_W), d_wout.reshape(N_DEV, SHARD_ROWS, D_MODEL), stats)

    big = {}
    for nm, g, w, m, v in (("w_in", g_win, w_in, m_w_in, v_w_in),
                           ("w_mem_kv", g_wmem, w_mem_kv, m_w_mem_kv, v_w_mem_kv),
                           ("w_out", g_wout, w_out, m_w_out, v_w_out)):
        res = _reduce_adamw(g[None], w[0], m[0], v[0], tr=SHARD_ROWS, name="adamw_" + nm)
        big[nm] = [t[None] for t in res]
    small = _reduce_adamw(
        r_stats, _pack_rows(pre_norm, mem_norm, post_norm, sink_a),
        _pack_rows(m_pre_norm, m_mem_norm, m_post_norm, m_sink_a),
        _pack_rows(v_pre_norm, v_mem_norm, v_post_norm, v_sink_a), tr=8, name="adamw_small")

    def unpack(t):
        return {"pre_norm": t[0:1], "mem_norm": t[1:2], "post_norm": t[2:3], "sink_a": t[3:4, 0:A_HEADS]}

    order = ("pre_norm", "w_in", "sink_a", "mem_norm", "w_mem_kv", "w_out", "post_norm")
    outs = [small[0][4, 0], grad_x[None]]
    for j in range(4):
        sm = unpack(small[j])
        outs += [big[n][j] if n in big else sm[n] for n in order]
    return tuple(outs)
```

```python
import jax
import jax.numpy as jnp
from jax import lax
from jax.experimental import pallas as pl
from jax.experimental.pallas import tpu as pltpu

F32 = jnp.float32
BF16 = jnp.bfloat16

D_MODEL = 1024
HEAD_DIM = 64
ROT_DIM = 16
ROPE_THETA = 500000.0
BLOCK = 128
LANES = 128
N_MEM = 256
RMS_EPS = 1e-6
SCALE = HEAD_DIM ** -0.5
A_HEADS, A_GROUP = 6, 3
B_HEADS = 6
C_HEADS = 4
A_W, A_KV_W, B_W, C_W = 384, 128, 384, 256
D_IN = 3072
N_DEV = 8
SHARD_IN = D_IN // N_DEV
SHARD_ROWS = D_MODEL // N_DEV
B_CONFIGS = ((128, 1), (512, 4), (2048, 16))
DILS = (4, 16)
NEG = -1e30
DELTA_LANE = 64
VMEM_LIMIT = 56 * 1024 * 1024

ADAM_LR, ADAM_B1, ADAM_B2, ADAM_EPS, ADAM_WD, ADAM_STEP = 0.001, 0.9, 0.999, 1e-08, 0.01, 10
MESH_ID = pl.DeviceIdType.MESH


def _params(**kw):
    return pltpu.CompilerParams(vmem_limit_bytes=VMEM_LIMIT, **kw)


def _full(shape):
    n = len(shape)
    return pl.BlockSpec(shape, lambda *_: (0,) * n)


def _row(tm, w):
    return pl.BlockSpec((tm, w), lambda i: (i, 0))


def _perm_view(a, dil):
    return a.reshape(a.shape[0] // (BLOCK * dil), dil, BLOCK, a.shape[1])


def _perm_spec(tm, dil, w):
    per = BLOCK * dil // tm
    return pl.BlockSpec((1, dil, tm // dil, w), lambda i: (i // per, 0, i % per, 0))


def _put(scr, val):
    for c in range(val.shape[1] // LANES):
        scr[c] = val[:, LANES * c:LANES * (c + 1)]


def _get(scr):
    n = scr.shape[0]
    return scr[0] if n == 1 else jnp.concatenate([scr[c] for c in range(n)], axis=1)


def _get_class(scr, r, dil):
    n, rows = scr.shape[0], scr.shape[1]
    parts = [scr.at[c][pl.ds(r, rows // dil, stride=dil), :] for c in range(n)]
    return parts[0] if n == 1 else jnp.concatenate(parts, axis=1)


def _store_permuted(scr, out_ref, dil, dtype):
    for r in range(dil):
        out_ref[0, r] = _get_class(scr, r, dil).astype(dtype)


def _load_permuted(in_ref, scr, dil):
    n, rows = scr.shape[0], scr.shape[1]
    for r in range(dil):
        val = in_ref[0, r].astype(F32)
        for c in range(n):
            scr.at[c][pl.ds(r, rows // dil, stride=dil), :] = val[:, LANES * c:LANES * (c + 1)]
    return _get(scr)


def _rope_tables(seq):
    j = jnp.arange(LANES) % HEAD_DIM
    freq = ROPE_THETA ** (-(2 * (j % (ROT_DIM // 2))).astype(F32) / ROT_DIM)
    ang = jnp.arange(seq, dtype=F32)[:, None] * freq[None, :]
    cos, sin = jnp.cos(ang), jnp.sin(ang)
    half = ROT_DIM // 2
    c = jnp.where(j < ROT_DIM, cos, 1.0)
    up = jnp.where((j >= half) & (j < ROT_DIM), sin, 0.0)
    dn = jnp.where(j < half, -sin, 0.0)
    return c, up, dn


def _rotate128(t, c, up, dn):
    return t * c + pltpu.roll(t, 8, 1) * up + pltpu.roll(t, LANES - 8, 1) * dn


def _rotate(t, c, up, dn):
    outs = [_rotate128(t[:, LANES * j:LANES * (j + 1)], c, up, dn) for j in range(t.shape[1] // LANES)]
    return outs[0] if len(outs) == 1 else jnp.concatenate(outs, axis=1)


def _inproj(x, pre_g, w_in_full, tabs, tm=512):
    seq = x.shape[0]
    n_chunk = D_IN // LANES

    def body(x_ref, g_ref, w_ref, c_ref, up_ref, dn_ref, u_ref, qa_ref, ka_ref, va_ref,
             qb1_ref, kb1_ref, vb1_ref, qb4_ref, kb4_ref, vb4_ref, qb16_ref, kb16_ref, vb16_ref,
             qc_ref, gate_ref, proj):
        xv = x_ref[...]
        r = lax.rsqrt(jnp.mean(xv * xv, axis=-1, keepdims=True) + RMS_EPS)
        u = ((xv * r) * g_ref[...]).astype(BF16)
        u_ref[...] = u
        per = SHARD_IN // LANES
        for k in range(N_DEV):
            acc = jnp.dot(u, w_ref[k], preferred_element_type=F32)
            for c3 in range(per):
                proj[per * k + c3] = acc[:, LANES * c3:LANES * (c3 + 1)]
        c, up, dn = c_ref[...], up_ref[...], dn_ref[...]

        def cols(lo, hi, rot=False, scale=None):
            parts = []
            for ch in range(lo // LANES, hi // LANES):
                t = proj[ch]
                if rot:
                    t = _rotate128(t, c, up, dn)
                if scale is not None:
                    t = t * scale
                parts.append(t)
            return parts[0] if len(parts) == 1 else jnp.concatenate(parts, axis=1)

        qa_ref[...] = cols(0, 384, True, SCALE).astype(BF16)
        ka_ref[...] = cols(384, 512, True).astype(BF16)
        va_ref[...] = cols(512, 640).astype(BF16)
        gate_ref[:, 0:384] = cols(640, 1024).astype(BF16)
        gate_ref[:, 384:768] = cols(2176, 2560).astype(BF16)
        gate_ref[:, 768:1024] = cols(2816, 3072).astype(BF16)
        qc_ref[...] = cols(2560, 2816, False, SCALE).astype(BF16)
        for ch in range(1024 // LANES, 1408 // LANES):
            proj[ch] = _rotate128(proj[ch], c, up, dn) * SCALE
        for ch in range(1408 // LANES, 1792 // LANES):
            proj[ch] = _rotate128(proj[ch], c, up, dn)
        for lo, nat, p4, p16 in ((1024, qb1_ref, qb4_ref, qb16_ref), (1408, kb1_ref, kb4_ref, kb16_ref),
                                 (1792, vb1_ref, vb4_ref, vb16_ref)):
            chunks = range(lo // LANES, lo // LANES + B_W // LANES)
            nat[...] = jnp.concatenate([proj[ch] for ch in chunks], axis=1).astype(BF16)
            for dil, ref in ((4, p4), (16, p16)):
                for rr in range(dil):
                    ref[0, rr] = jnp.concatenate(
                        [proj.at[ch][pl.ds(rr, tm // dil, stride=dil), :] for ch in chunks], axis=1).astype(BF16)

    nat_w = (D_MODEL, A_W, A_KV_W, A_KV_W, B_W, B_W, B_W)
    out_specs = [_row(tm, w) for w in nat_w]
    out_shape = [jax.ShapeDtypeStruct((seq, w), BF16) for w in nat_w]
    for dil in DILS:
        out_specs += [_perm_spec(tm, dil, B_W)] * 3
        out_shape += [jax.ShapeDtypeStruct((seq // (BLOCK * dil), dil, BLOCK, B_W), BF16)] * 3
    out_specs += [_row(tm, C_W), _row(tm, D_MODEL)]
    out_shape += [jax.ShapeDtypeStruct((seq, C_W), BF16), jax.ShapeDtypeStruct((seq, D_MODEL), BF16)]
    res = pl.pallas_call(
        body, name="inproj", grid=(seq // tm,),
        in_specs=[_row(tm, D_MODEL), _full((1, D_MODEL)), _full((N_DEV, D_MODEL, SHARD_IN)),
                  _row(tm, LANES), _row(tm, LANES), _row(tm, LANES)],
        out_specs=out_specs, out_shape=out_shape,
        scratch_shapes=[pltpu.VMEM((n_chunk, tm, LANES), F32)],
        compiler_params=_params(dimension_semantics=("arbitrary",)),
    )(x, pre_g, w_in_full, *tabs)
    u, qa, ka, va = res[0:4]
    qkv_b = {1: res[4:7], 4: [t.reshape(seq, B_W) for t in res[7:10]], 16: [t.reshape(seq, B_W) for t in res[10:13]]}
    return u, qa, ka, va, qkv_b, res[13], res[14]


def _memkv_fwd(mem, mem_g, w_mem_full):
    def body(mem_ref, g_ref, w_ref, mn_ref, mk_ref, mv_ref):
        mv_ = mem_ref[...]
        r = lax.rsqrt(jnp.mean(mv_ * mv_, axis=-1, keepdims=True) + RMS_EPS)
        mn = ((mv_ * r) * g_ref[...]).astype(BF16)
        mn_ref[...] = mn
        mkv = jnp.dot(mn, w_ref[...], preferred_element_type=F32)
        mk_ref[...] = mkv[:, 0:C_W].astype(BF16)
        mv_ref[...] = mkv[:, C_W:2 * C_W].astype(BF16)

    return pl.pallas_call(
        body, name="memkv_fwd",
        out_shape=[jax.ShapeDtypeStruct((N_MEM, D_MODEL), BF16),
                   jax.ShapeDtypeStruct((N_MEM, C_W), BF16), jax.ShapeDtypeStruct((N_MEM, C_W), BF16)],
        compiler_params=_params(),
    )(mem, mem_g, w_mem_full)


def _memkv_bwd(mem, mem_g, mn, w_mem_full, dmk, dmv):
    def body(mem_ref, g_ref, mn_ref, w_ref, dmk_ref, dmv_ref, dw_ref, st_ref):
        dmkv = jnp.concatenate([dmk_ref[...], dmv_ref[...]], axis=1).astype(BF16)
        dw_ref[...] = lax.dot_general(mn_ref[...], dmkv, (((0,), (0,)), ((), ())), preferred_element_type=F32)
        dmn = lax.dot_general(dmkv, w_ref[...], (((1,), (1,)), ((), ())), preferred_element_type=F32)
        mv_ = mem_ref[...]
        r = lax.rsqrt(jnp.mean(mv_ * mv_, axis=-1, keepdims=True) + RMS_EPS)
        st_ref[...] = jnp.zeros_like(st_ref)
        st_ref[0:1, :] = jnp.sum(dmn * (mv_ * r), axis=0, keepdims=True)

    return pl.pallas_call(
        body, name="memkv_bwd",
        out_shape=[jax.ShapeDtypeStruct((D_MODEL, 2 * C_W), F32), jax.ShapeDtypeStruct((8, D_MODEL), F32)],
        compiler_params=_params(),
    )(mem, mem_g, mn, w_mem_full, dmk, dmv)


def _band_mask(i, max_dist):
    qi = lax.broadcasted_iota(jnp.int32, (BLOCK, 2 * BLOCK), 0)
    kj = lax.broadcasted_iota(jnp.int32, (BLOCK, 2 * BLOCK), 1)
    dist = qi + BLOCK - kj
    return (dist >= 0) & (dist <= max_dist) & ((kj >= BLOCK) | (i > 0))


_NT = (((1,), (1,)), ((), ()))
_TN = (((0,), (0,)), ((), ()))


def _head_only(val, h):
    slab = val[:, LANES * (h // 2):LANES * (h // 2 + 1)]
    lane = lax.broadcasted_iota(jnp.int32, slab.shape, 1)
    keep = (lane < HEAD_DIM) if h % 2 == 0 else (lane >= HEAD_DIM)
    return jnp.where(keep, slab, jnp.zeros((), slab.dtype))


class _KvSlabs:
    def __init__(self, cat, group):
        self.cat, self.group, self.swapped = cat, group, {}

    def is_swapped(self, h):
        return (h // self.group) % 2 != h % 2

    def __call__(self, h):
        j = (h // self.group) // 2
        slab = self.cat[:, LANES * j:LANES * (j + 1)]
        if not self.is_swapped(h):
            return slab
        if j not in self.swapped:
            self.swapped[j] = jnp.concatenate([slab[:, HEAD_DIM:], slab[:, :HEAD_DIM]], axis=1)
        return self.swapped[j]


def _banded_fwd(q, k, v, sink, *, dil, heads, group, max_dist, name):
    seq = q.shape[0]
    kvh = heads // group
    nb = seq // dil // BLOCK
    qw, kw = heads * HEAD_DIM, kvh * HEAD_DIM

    def body(*refs):
        if sink is not None:
            sink_ref, refs = refs[0], refs[1:]
        q_ref, kp_ref, kc_ref, vp_ref, vc_ref, o_ref, lse_ref, s_scr, p_scr = refs
        i = pl.program_id(1)
        qv = q_ref[...]
        k_of = _KvSlabs(jnp.concatenate([kp_ref[...], kc_ref[...]], axis=0), group)
        v_of = _KvSlabs(jnp.concatenate([vp_ref[...], vc_ref[...]], axis=0), group)
        valid = _band_mask(i, max_dist)
        lane = lax.broadcasted_iota(jnp.int32, (BLOCK, LANES), 1)
        lse_tile = jnp.zeros((BLOCK, LANES), F32)
        for h in range(heads):
            s_scr[h] = lax.dot_general(_head_only(qv, h), k_of(h), _NT, preferred_element_type=F32)
        ls = []
        for h in range(heads):
            s = jnp.where(valid, s_scr[h], NEG)
            m = jnp.max(s, axis=-1, keepdims=True)
            if sink is not None:
                sk = sink_ref[h]
                m = jnp.maximum(m, sk)
            p = jnp.exp(s - m)
            l = jnp.sum(p, axis=-1, keepdims=True)
            if sink is not None:
                l = l + jnp.exp(sk - m)
            p_scr[h] = p.astype(BF16)
            ls.append(l)
            lse_tile = jnp.where(lane == h, m + jnp.log(l), lse_tile)
        for pr in range(heads // 2):
            even = jnp.dot(p_scr[2 * pr], v_of(2 * pr), preferred_element_type=F32) / ls[2 * pr]
            odd = jnp.dot(p_scr[2 * pr + 1], v_of(2 * pr + 1), preferred_element_type=F32) / ls[2 * pr + 1]
            o_ref[:, LANES * pr:LANES * (pr + 1)] = jnp.where(lane < HEAD_DIM, even, odd)
        lse_ref[...] = lse_tile

    own = lambda w: pl.BlockSpec((BLOCK, w), lambda r, i: (i * dil + r, 0))
    prev = lambda w: pl.BlockSpec((BLOCK, w), lambda r, i: (jnp.maximum(i - 1, 0) * dil + r, 0))
    in_specs = [own(qw), prev(kw), own(kw), prev(kw), own(kw)]
    args = [q, k, k, v, v]
    if sink is not None:
        in_specs = [pl.BlockSpec(memory_space=pltpu.SMEM)] + in_specs
        args = [sink] + args
    return pl.pallas_call(
        body, name=name, grid=(dil, nb), in_specs=in_specs,
        out_specs=[own(qw), own(LANES)],
        out_shape=[jax.ShapeDtypeStruct((seq, qw), F32), jax.ShapeDtypeStruct((seq, LANES), F32)],
        scratch_shapes=[pltpu.VMEM((heads, BLOCK, 2 * BLOCK), F32), pltpu.VMEM((heads, BLOCK, 2 * BLOCK), BF16)],
        compiler_params=_params(dimension_semantics=("arbitrary", "arbitrary")),
    )(*args)


def _banded_bwd(q, k, v, d_out, stat, sink, *, dil, heads, group, max_dist, name):
    seq = q.shape[0]
    kvh = heads // group
    nb = seq // dil // BLOCK
    qw, kw = heads * HEAD_DIM, kvh * HEAD_DIM
    n_in = 7

    def body(*refs):
        if sink is not None:
            sink_ref, refs = refs[0], refs[1:]
            dsink_ref, refs = refs[n_in], refs[:n_in] + refs[n_in + 1:]
        (q_ref, kp_ref, kc_ref, vp_ref, vc_ref, do_ref, st_ref, dq_ref, dk_ref, dv_ref,
         kcar, vcar, s_scr, dp_scr, p_scr, ds_scr) = refs
        r, i = pl.program_id(0), pl.program_id(1)

        @pl.when(i == 0)
        def _():
            kcar[...] = jnp.zeros_like(kcar)
            vcar[...] = jnp.zeros_like(vcar)

        if sink is not None:
            @pl.when((i == 0) & (r == 0))
            def _():
                dsink_ref[...] = jnp.zeros_like(dsink_ref)

        @pl.when(i < nb)
        def _():
            qv, dov, st = q_ref[...], do_ref[...], st_ref[...]
            k_of = _KvSlabs(jnp.concatenate([kp_ref[...], kc_ref[...]], axis=0), group)
            v_of = _KvSlabs(jnp.concatenate([vp_ref[...], vc_ref[...]], axis=0), group)
            valid = _band_mask(i, max_dist)
            lane = lax.broadcasted_iota(jnp.int32, (1, LANES), 1)
            qms = [_head_only(qv, h) for h in range(heads)]
            doms = [_head_only(dov, h) for h in range(heads)]
            for h in range(heads):
                s_scr[h] = lax.dot_general(qms[h], k_of(h), _NT, preferred_element_type=F32)
                dp_scr[h] = lax.dot_general(doms[h], v_of(h), _NT, preferred_element_type=F32)
            dsink_row = jnp.zeros((1, LANES), F32)
            for h in range(heads):
                lse_h = st[:, h:h + 1]
                delta = st[:, DELTA_LANE + h:DELTA_LANE + h + 1]
                p = jnp.where(valid, jnp.exp(s_scr[h] - lse_h), 0.0)
                p_scr[h] = p.astype(BF16)
                ds_scr[h] = (p * (dp_scr[h] - delta)).astype(BF16)
                if sink is not None:
                    ds_sink = jnp.sum(-jnp.exp(sink_ref[h] - lse_h) * delta, axis=0, keepdims=True)
                    dsink_row = jnp.where(lane == h, ds_sink, dsink_row)
            lane_q = lax.broadcasted_iota(jnp.int32, (BLOCK, LANES), 1)
            for pr in range(heads // 2):
                even = jnp.dot(ds_scr[2 * pr], k_of(2 * pr), preferred_element_type=F32)
                odd = jnp.dot(ds_scr[2 * pr + 1], k_of(2 * pr + 1), preferred_element_type=F32)
                dq_ref[:, LANES * pr:LANES * (pr + 1)] = jnp.where(lane_q < HEAD_DIM, even, odd)
            for j in range(kw // LANES):
                acc = {}
                for h in range(heads):
                    if (h // group) // 2 != j:
                        continue
                    key = k_of.is_swapped(h)
                    dk_h = lax.dot_general(ds_scr[h], qms[h], _TN, preferred_element_type=F32)
                    dv_h = lax.dot_general(p_scr[h], doms[h], _TN, preferred_element_type=F32)
                    acc[key] = (dk_h, dv_h) if key not in acc else (acc[key][0] + dk_h, acc[key][1] + dv_h)
                dk_j, dv_j = acc.get(False, (None, None))
                if True in acc:
                    unswap = lambda t: jnp.concatenate([t[:, HEAD_DIM:], t[:, :HEAD_DIM]], axis=1)
                    dk_s, dv_s = unswap(acc[True][0]), unswap(acc[True][1])
                    dk_j = dk_s if dk_j is None else dk_j + dk_s
                    dv_j = dv_s if dv_j is None else dv_j + dv_s
                sl = slice(LANES * j, LANES * (j + 1))
                dk_ref[:, sl] = kcar[:, sl] + dk_j[0:BLOCK]
                dv_ref[:, sl] = vcar[:, sl] + dv_j[0:BLOCK]
                kcar[:, sl] = dk_j[BLOCK:2 * BLOCK]
                vcar[:, sl] = dv_j[BLOCK:2 * BLOCK]
            if sink is not None:
                dsink_ref[0:1, :] += dsink_row

        @pl.when(i == nb)
        def _():
            dk_ref[...] = kcar[...]
            dv_ref[...] = vcar[...]

    cur = lambda i: jnp.minimum(i, nb - 1)
    own = lambda w: pl.BlockSpec((BLOCK, w), lambda r, i: (cur(i) * dil + r, 0))
    prev = lambda w: pl.BlockSpec((BLOCK, w), lambda r, i: (jnp.maximum(cur(i) - 1, 0) * dil + r, 0))
    late = lambda w: pl.BlockSpec((BLOCK, w), lambda r, i: (jnp.maximum(i - 1, 0) * dil + r, 0))
    in_specs = [own(qw), prev(kw), own(kw), prev(kw), own(kw), own(qw), own(LANES)]
    args = [q, k, k, v, v, d_out, stat]
    out_specs = [own(qw), late(kw), late(kw)]
    out_shape = [jax.ShapeDtypeStruct((seq, qw), F32), jax.ShapeDtypeStruct((seq, kw), F32),
                 jax.ShapeDtypeStruct((seq, kw), F32)]
    if sink is not None:
        in_specs = [pl.BlockSpec(memory_space=pltpu.SMEM)] + in_specs
        args = [sink] + args
        out_specs = [_full((8, LANES))] + out_specs
        out_shape = [jax.ShapeDtypeStruct((8, LANES), F32)] + out_shape
    res = pl.pallas_call(
        body, name=name, grid=(dil, nb + 1), in_specs=in_specs, out_specs=out_specs, out_shape=out_shape,
        scratch_shapes=[pltpu.VMEM((BLOCK, kw), F32), pltpu.VMEM((BLOCK, kw), F32)]
        + [pltpu.VMEM((heads, BLOCK, 2 * BLOCK), F32)] * 2 + [pltpu.VMEM((heads, BLOCK, 2 * BLOCK), BF16)] * 2,
        compiler_params=_params(dimension_semantics=("arbitrary", "arbitrary")),
    )(*args)
    if sink is not None:
        return res[1], res[2], res[3], res[0]
    return res


def _cross_fwd(q, mk, mv, tq=512):
    seq = q.shape[0]

    def body(q_ref, mk_ref, mv_ref, o_ref, lse_ref, s_scr, p_scr):
        qv = q_ref[...]
        k_of, v_of = _KvSlabs(mk_ref[...], 1), _KvSlabs(mv_ref[...], 1)
        lane = lax.broadcasted_iota(jnp.int32, (tq, LANES), 1)
        lse_tile = jnp.zeros((tq, LANES), F32)
        for h in range(C_HEADS):
            s_scr[h] = lax.dot_general(_head_only(qv, h), k_of(h), _NT, preferred_element_type=F32)
        ls = []
        for h in range(C_HEADS):
            s = s_scr[h]
            m = jnp.max(s, axis=-1, keepdims=True)
            p = jnp.exp(s - m)
            l = jnp.sum(p, axis=-1, keepdims=True)
            p_scr[h] = p.astype(BF16)
            ls.append(l)
            lse_tile = jnp.where(lane == h, m + jnp.log(l), lse_tile)
        for pr in range(C_HEADS // 2):
            even = jnp.dot(p_scr[2 * pr], v_of(2 * pr), preferred_element_type=F32) / ls[2 * pr]
            odd = jnp.dot(p_scr[2 * pr + 1], v_of(2 * pr + 1), preferred_element_type=F32) / ls[2 * pr + 1]
            o_ref[:, LANES * pr:LANES * (pr + 1)] = jnp.where(lane < HEAD_DIM, even, odd)
        lse_ref[...] = lse_tile

    return pl.pallas_call(
        body, name="cross_fwd", grid=(seq // tq,),
        in_specs=[_row(tq, C_W), _full((N_MEM, C_W)), _full((N_MEM, C_W))],
        out_specs=[_row(tq, C_W), _row(tq, LANES)],
        out_shape=[jax.ShapeDtypeStruct((seq, C_W), F32), jax.ShapeDtypeStruct((seq, LANES), F32)],
        scratch_shapes=[pltpu.VMEM((C_HEADS, tq, N_MEM), F32), pltpu.VMEM((C_HEADS, tq, N_MEM), BF16)],
        compiler_params=_params(dimension_semantics=("arbitrary",)),
    )(q, mk, mv)


def _cross_bwd(q, mk, mv, d_out, stat, tq=512):
    seq = q.shape[0]

    def body(q_ref, mk_ref, mv_ref, do_ref, st_ref, dq_ref, dmk_ref, dmv_ref, s_scr, dp_scr, p_scr, ds_scr):
        @pl.when(pl.program_id(0) == 0)
        def _():
            dmk_ref[...] = jnp.zeros_like(dmk_ref)
            dmv_ref[...] = jnp.zeros_like(dmv_ref)

        qv, dov, st = q_ref[...], do_ref[...], st_ref[...]
        k_of, v_of = _KvSlabs(mk_ref[...], 1), _KvSlabs(mv_ref[...], 1)
        qms = [_head_only(qv, h) for h in range(C_HEADS)]
        doms = [_head_only(dov, h) for h in range(C_HEADS)]
        for h in range(C_HEADS):
            s_scr[h] = lax.dot_general(qms[h], k_of(h), _NT, preferred_element_type=F32)
            dp_scr[h] = lax.dot_general(doms[h], v_of(h), _NT, preferred_element_type=F32)
        for h in range(C_HEADS):
            p = jnp.exp(s_scr[h] - st[:, h:h + 1])
            p_scr[h] = p.astype(BF16)
            ds_scr[h] = (p * (dp_scr[h] - st[:, DELTA_LANE + h:DELTA_LANE + h + 1])).astype(BF16)
        lane = lax.broadcasted_iota(jnp.int32, (tq, LANES), 1)
        for pr in range(C_HEADS // 2):
            sl = slice(LANES * pr, LANES * (pr + 1))
            even = jnp.dot(ds_scr[2 * pr], k_of(2 * pr), preferred_element_type=F32)
            odd = jnp.dot(ds_scr[2 * pr + 1], k_of(2 * pr + 1), preferred_element_type=F32)
            dq_ref[:, sl] = jnp.where(lane < HEAD_DIM, even, odd)
            dmk_ref[:, sl] += (lax.dot_general(ds_scr[2 * pr], qms[2 * pr], _TN, preferred_element_type=F32)
                               + lax.dot_general(ds_scr[2 * pr + 1], qms[2 * pr + 1], _TN, preferred_element_type=F32))
            dmv_ref[:, sl] += (lax.dot_general(p_scr[2 * pr], doms[2 * pr], _TN, preferred_element_type=F32)
                               + lax.dot_general(p_scr[2 * pr + 1], doms[2 * pr + 1], _TN, preferred_element_type=F32))

    return pl.pallas_call(
        body, name="cross_bwd", grid=(seq // tq,),
        in_specs=[_row(tq, C_W), _full((N_MEM, C_W)), _full((N_MEM, C_W)), _row(tq, C_W), _row(tq, LANES)],
        out_specs=[_row(tq, C_W), _full((N_MEM, C_W)), _full((N_MEM, C_W))],
        out_shape=[jax.ShapeDtypeStruct((seq, C_W), F32), jax.ShapeDtypeStruct((N_MEM, C_W), F32),
                   jax.ShapeDtypeStruct((N_MEM, C_W), F32)],
        scratch_shapes=[pltpu.VMEM((C_HEADS, tq, N_MEM), F32)] * 2 + [pltpu.VMEM((C_HEADS, tq, N_MEM), BF16)] * 2,
        compiler_params=_params(dimension_semantics=("arbitrary",)),
    )(q, mk, mv, d_out, stat)


def _per_head(tile, width):
    rows = tile.shape[0]
    return jnp.concatenate(
        [jnp.broadcast_to(tile[:, h:h + 1], (rows, HEAD_DIM)) for h in range(width // HEAD_DIM)], axis=1)


def _with_delta(lse_tile, prod):
    rows = lse_tile.shape[0]
    lane = lax.broadcasted_iota(jnp.int32, (rows, LANES), 1)
    tile = lse_tile
    for h in range(prod.shape[1] // HEAD_DIM):
        d = jnp.sum(prod[:, HEAD_DIM * h:HEAD_DIM * (h + 1)], axis=-1, keepdims=True)
        tile = jnp.where(lane == DELTA_LANE + h, d, tile)
    return tile


def _mid(oa, lse_a, ob, lse_b, oc, lse_c, gate, x, target, w_out_full, post_g, tm=256):
    seq = x.shape[0]
    n_b = B_W // LANES

    def body(oa_ref, la_ref, b1_ref, l1_ref, b4_ref, l4_ref, b16_ref, l16_ref, oc_ref, lc_ref,
             gate_ref, x_ref, t_ref, w_ref, pg_ref,
             dh_ref, dg_ref, doa_ref, sa_ref, dob1_ref, sb1_ref, dob4_ref, sb4_ref, dob16_ref, sb16_ref,
             doc_ref, sc_ref, dw_ref, st_ref, scr_b4, scr_b16, scr_l4, scr_l16, scr_do, scr_sb):
        @pl.when(pl.program_id(0) == 0)
        def _():
            dw_ref[...] = jnp.zeros_like(dw_ref)
            st_ref[...] = jnp.zeros_like(st_ref)

        b1, l1 = b1_ref[...], l1_ref[...]
        b4, l4 = _load_permuted(b4_ref, scr_b4, 4), _load_permuted(l4_ref, scr_l4, 4)
        b16, l16 = _load_permuted(b16_ref, scr_b16, 16), _load_permuted(l16_ref, scr_l16, 16)
        lm = jnp.maximum(jnp.maximum(l1, l4), l16)
        e1, e4, e16 = jnp.exp(l1 - lm), jnp.exp(l4 - lm), jnp.exp(l16 - lm)
        den = e1 + e4 + e16
        lse_b_tile = lm + jnp.log(den)
        ob_v = _per_head(e1 / den, B_W) * b1 + _per_head(e4 / den, B_W) * b4 + _per_head(e16 / den, B_W) * b16
        o_all = jnp.concatenate([oa_ref[...], ob_v, oc_ref[...]], axis=1)
        g = gate_ref[...].astype(F32)
        sig = 1.0 / (1.0 + jnp.exp(-g))
        silu = g * sig
        y = (o_all * silu).astype(BF16)
        w = w_ref[...]
        z = jnp.dot(y, w, preferred_element_type=F32)
        rz = lax.rsqrt(jnp.mean(z * z, axis=-1, keepdims=True) + RMS_EPS)
        hn = z * rz
        pg = pg_ref[...]
        err = (x_ref[...] + hn * pg) - t_ref[...]
        loss = 0.5 * jnp.sum(jnp.mean(err * err, axis=-1, keepdims=True), axis=0, keepdims=True)
        dh = err * (1.0 / D_MODEL)
        dh_ref[...] = dh
        st_ref[0:1, :] += jnp.sum(dh * hn, axis=0, keepdims=True)
        st_ref[1:2, :] += jnp.broadcast_to(loss, (1, D_MODEL))
        dhn = dh * pg
        dz = (rz * (dhn - hn * jnp.mean(dhn * hn, axis=-1, keepdims=True))).astype(BF16)
        dy = lax.dot_general(dz, w, (((1,), (1,)), ((), ())), preferred_element_type=F32)
        dw_ref[...] += lax.dot_general(y, dz, (((0,), (0,)), ((), ())), preferred_element_type=F32)
        dg_ref[...] = (dy * o_all * (sig * (1.0 + g * (1.0 - sig)))).astype(BF16)
        d_o = (dy * silu).astype(BF16)
        prod = d_o.astype(F32) * o_all
        doa_ref[...] = d_o[:, 0:A_W]
        sa_ref[...] = _with_delta(la_ref[...], prod[:, 0:A_W])
        doc_ref[...] = d_o[:, A_W + B_W:D_MODEL]
        sc_ref[...] = _with_delta(lc_ref[...], prod[:, A_W + B_W:D_MODEL])
        d_ob = d_o[:, A_W:A_W + B_W]
        stat_b = _with_delta(lse_b_tile, prod[:, A_W:A_W + B_W])
        dob1_ref[...] = d_ob
        sb1_ref[...] = stat_b
        _put(scr_do, d_ob.astype(F32))
        _put(scr_sb, stat_b)
        _store_permuted(scr_do, dob4_ref, 4, BF16)
        _store_permuted(scr_sb, sb4_ref, 4, F32)
        _store_permuted(scr_do, dob16_ref, 16, BF16)
        _store_permuted(scr_sb, sb16_ref, 16, F32)

    p4 = lambda w: _perm_spec(tm, 4, w)
    p16 = lambda w: _perm_spec(tm, 16, w)
    in_specs = [_row(tm, A_W), _row(tm, LANES), _row(tm, B_W), _row(tm, LANES), p4(B_W), p4(LANES), p16(B_W), p16(LANES),
                _row(tm, C_W), _row(tm, LANES), _row(tm, D_MODEL), _row(tm, D_MODEL), _row(tm, D_MODEL),
                _full((D_MODEL, D_MODEL)), _full((1, D_MODEL))]
    sds = jax.ShapeDtypeStruct
    v4 = lambda w, dt: sds((seq // (BLOCK * 4), 4, BLOCK, w), dt)
    v16 = lambda w, dt: sds((seq // (BLOCK * 16), 16, BLOCK, w), dt)
    out_specs = [_row(tm, D_MODEL), _row(tm, D_MODEL), _row(tm, A_W), _row(tm, LANES), _row(tm, B_W), _row(tm, LANES),
                 p4(B_W), p4(LANES), p16(B_W), p16(LANES), _row(tm, C_W), _row(tm, LANES),
                 _full((D_MODEL, D_MODEL)), _full((8, D_MODEL))]
    out_shape = [sds((seq, D_MODEL), F32), sds((seq, D_MODEL), BF16), sds((seq, A_W), BF16), sds((seq, LANES), F32),
                 sds((seq, B_W), BF16), sds((seq, LANES), F32), v4(B_W, BF16), v4(LANES, F32), v16(B_W, BF16),
                 v16(LANES, F32), sds((seq, C_W), BF16), sds((seq, LANES), F32),
                 sds((D_MODEL, D_MODEL), F32), sds((8, D_MODEL), F32)]
    res = pl.pallas_call(
        body, name="mid", grid=(seq // tm,), in_specs=in_specs, out_specs=out_specs, out_shape=out_shape,
        scratch_shapes=[pltpu.VMEM((n_b, tm, LANES), F32), pltpu.VMEM((n_b, tm, LANES), F32),
                        pltpu.VMEM((1, tm, LANES), F32), pltpu.VMEM((1, tm, LANES), F32),
                        pltpu.VMEM((n_b, tm, LANES), F32), pltpu.VMEM((1, tm, LANES), F32)],
        compiler_params=_params(dimension_semantics=("arbitrary",)),
    )(oa, lse_a, ob[1], lse_b[1], _perm_view(ob[4], 4), _perm_view(lse_b[4], 4), _perm_view(ob[16], 16),
      _perm_view(lse_b[16], 16), oc, lse_c, gate, x, target, w_out_full, post_g)
    dh, d_gate, do_a, st_a, do_b1, st_b1, do_b4, st_b4, do_b16, st_b16, do_c, st_c, d_wout, stats = res
    flat = lambda t: t.reshape(seq, t.shape[-1])
    d_b = {1: (do_b1, st_b1), 4: (flat(do_b4), flat(st_b4)), 16: (flat(do_b16), flat(st_b16))}
    return dh, d_gate, (do_a, st_a), d_b, (do_c, st_c), d_wout, stats


def _inproj_bwd_x(x, dh, pre_g, w_in_full, tabs, dqa, dka, dva, dqkv_b, dqc, dgate, tm=256):
    seq = x.shape[0]
    n_b = B_W // LANES

    def body(x_ref, dh_ref, g_ref, w_ref, c_ref, up_ref, dn_ref, dqa_ref, dka_ref, dva_ref,
             dq1, dk1, dv1, dq4, dk4, dv4, dq16, dk16, dv16, dqc_ref, dg_ref,
             gx_ref, dp_ref, st_ref, scr4, scr16):
        @pl.when(pl.program_id(0) == 0)
        def _():
            st_ref[...] = jnp.zeros_like(st_ref)

        c, up, dn = c_ref[...], -up_ref[...], -dn_ref[...]
        unrot = lambda t: _rotate(t, c, up, dn)
        total = lambda r1, r4, r16: r1[...] + _load_permuted(r4, scr4, 4) + _load_permuted(r16, scr16, 16)
        dp_ref[:, 0:384] = (unrot(dqa_ref[...]) * SCALE).astype(BF16)
        dp_ref[:, 384:512] = unrot(dka_ref[...]).astype(BF16)
        dp_ref[:, 512:640] = dva_ref[...].astype(BF16)
        dp_ref[:, 640:1024] = dg_ref[:, 0:384]
        dp_ref[:, 1024:1408] = (unrot(total(dq1, dq4, dq16)) * SCALE).astype(BF16)
        dp_ref[:, 1408:1792] = unrot(total(dk1, dk4, dk16)).astype(BF16)
        dp_ref[:, 1792:2176] = total(dv1, dv4, dv16).astype(BF16)
        dp_ref[:, 2176:2560] = dg_ref[:, 384:768]
        dp_ref[:, 2560:2816] = (dqc_ref[...] * SCALE).astype(BF16)
        dp_ref[:, 2816:3072] = dg_ref[:, 768:1024]
        du = jnp.zeros((tm, D_MODEL), F32)
        for k in range(N_DEV):
            du = du + lax.dot_general(dp_ref[:, SHARD_IN * k:SHARD_IN * (k + 1)], w_ref[k],
                                      (((1,), (1,)), ((), ())), preferred_element_type=F32)
        xv = x_ref[...]
        r = lax.rsqrt(jnp.mean(xv * xv, axis=-1, keepdims=True) + RMS_EPS)
        xh = xv * r
        st_ref[0:1, :] += jnp.sum(du * xh, axis=0, keepdims=True)
        dxh = du * g_ref[...]
        gx_ref[...] = dh_ref[...] + r * (dxh - xh * jnp.mean(dxh * xh, axis=-1, keepdims=True))

    in_specs = ([_row(tm, D_MODEL), _row(tm, D_MODEL), _full((1, D_MODEL)), _full((N_DEV, D_MODEL, SHARD_IN)),
                 _row(tm, LANES), _row(tm, LANES), _row(tm, LANES), _row(tm, A_W), _row(tm, A_KV_W), _row(tm, A_KV_W)]
                + [_row(tm, B_W)] * 3 + [_perm_spec(tm, 4, B_W)] * 3 + [_perm_spec(tm, 16, B_W)] * 3
                + [_row(tm, C_W), _row(tm, D_MODEL)])
    return pl.pallas_call(
        body, name="inproj_bwd_x", grid=(seq // tm,), in_specs=in_specs,
        out_specs=[_row(tm, D_MODEL), _row(tm, D_IN), _full((8, D_MODEL))],
        out_shape=[jax.ShapeDtypeStruct((seq, D_MODEL), F32), jax.ShapeDtypeStruct((seq, D_IN), BF16),
                   jax.ShapeDtypeStruct((8, D_MODEL), F32)],
        scratch_shapes=[pltpu.VMEM((n_b, tm, LANES), F32), pltpu.VMEM((n_b, tm, LANES), F32)],
        compiler_params=_params(dimension_semantics=("arbitrary",)),
    )(x, dh, pre_g, w_in_full, *tabs, dqa, dka, dva, *dqkv_b[1], *[_perm_view(t, 4) for t in dqkv_b[4]],
      *[_perm_view(t, 16) for t in dqkv_b[16]], dqc, dgate)


def _inproj_bwd_w(u, dproj, tm=1024):
    seq = u.shape[0]

    def body(u_ref, dp_ref, dw_ref):
        @pl.when(pl.program_id(1) == 0)
        def _():
            dw_ref[...] = jnp.zeros_like(dw_ref)

        dw_ref[0] += lax.dot_general(u_ref[...], dp_ref[...], (((0,), (0,)), ((), ())), preferred_element_type=F32)

    return pl.pallas_call(
        body, name="inproj_bwd_w", grid=(N_DEV, seq // tm),
        in_specs=[pl.BlockSpec((tm, D_MODEL), lambda k, t: (t, 0)), pl.BlockSpec((tm, SHARD_IN), lambda k, t: (t, k))],
        out_specs=pl.BlockSpec((1, D_MODEL, SHARD_IN), lambda k, t: (k, 0, 0)),
        out_shape=jax.ShapeDtypeStruct((N_DEV, D_MODEL, SHARD_IN), F32),
        compiler_params=_params(dimension_semantics=("arbitrary", "arbitrary")),
    )(u, dproj)


def _local_step(x, mem, pre_g, w_in_full, sink, mem_g, w_mem_full, w_out_full, post_g, target):
    seq = x.shape[0]
    tabs = _rope_tables(seq)
    u, qa, ka, va, qkv_b, qc, gate = _inproj(x, pre_g, w_in_full, tabs)
    mn, mk, mv = _memkv_fwd(mem, mem_g, w_mem_full)

    a_cfg = dict(dil=1, heads=A_HEADS, group=A_GROUP, max_dist=BLOCK - 1)
    b_cfgs = {dil: dict(dil=dil, heads=B_HEADS, group=1, max_dist=win // dil) for win, dil in B_CONFIGS}
    oa, lse_a = _banded_fwd(qa, ka, va, sink, name="attn_a_fwd", **a_cfg)
    ob, lse_b = {}, {}
    for dil, cfg in b_cfgs.items():
        ob[dil], lse_b[dil] = _banded_fwd(*qkv_b[dil], None, name=f"attn_b{dil}_fwd", **cfg)
    oc, lse_c = _cross_fwd(qc, mk, mv)

    dh, d_gate, d_a, d_b, d_c, d_wout, st_mid = _mid(oa, lse_a, ob, lse_b, oc, lse_c, gate, x, target, w_out_full, post_g)

    dqa, dka, dva, dsink = _banded_bwd(qa, ka, va, *d_a, sink, name="attn_a_bwd", **a_cfg)
    dqkv_b = {dil: _banded_bwd(*qkv_b[dil], *d_b[dil], None, name=f"attn_b{dil}_bwd", **cfg)
              for dil, cfg in b_cfgs.items()}
    dqc, dmk, dmv = _cross_bwd(qc, mk, mv, *d_c)
    d_wmem, st_mem = _memkv_bwd(mem, mem_g, mn, w_mem_full, dmk, dmv)

    grad_x, dproj, st_pre = _inproj_bwd_x(x, dh, pre_g, w_in_full, tabs, dqa, dka, dva, dqkv_b, dqc, d_gate)
    d_win = _inproj_bwd_w(u, dproj)

    dsink_row = jnp.pad(dsink[0:1, :], ((0, 0), (0, D_MODEL - LANES)))
    stats = jnp.concatenate([st_pre[0:1], st_mem[0:1], st_mid[0:1], dsink_row, st_mid[1:2],
                             jnp.zeros((3, D_MODEL), F32)], axis=0)
    return grad_x, d_win, d_wmem, d_wout, stats


def _mesh_pos():
    return lax.axis_index("x"), lax.axis_index("y"), lax.axis_index("c")


def _all_gather_weights(w_in, w_mem, w_out):
    def body(win_ref, wm_ref, wo_ref, win_out, wm_out, wo_out, win_b, wm_b, wo_b, send_sems, recv_sems, local_sems):
        x, y, c = _mesh_pos()
        win_b[...] = win_ref[...].astype(BF16)
        wm_b[...] = wm_ref[...].astype(BF16)
        wo_b[...] = wo_ref[...].astype(BF16)
        srcs = (win_b, wm_b, wo_b)
        outs = (win_out, wm_out, wo_out)
        me, sibling = (x, y, c), (x, y, 1 - c)
        chips = [(1 - x, y), (x, 1 - y), (1 - x, 1 - y)]

        def slot(a, p):
            return outs[a].at[4 * p[0] + 2 * p[1] + p[2]]

        def copy(a, k, block, to, src=None):
            return pltpu.make_async_remote_copy(
                src_ref=slot(a, block) if src is None else src, dst_ref=slot(a, block),
                send_sem=send_sems.at[a, k], recv_sem=recv_sems.at[a, k], device_id=to, device_id_type=MESH_ID)

        mine = [pltpu.make_async_copy(srcs[a], slot(a, me), local_sems.at[a]) for a in range(3)]
        for cp in mine:
            cp.start()
        first = []
        for a in range(3):
            first.append(copy(a, 0, me, sibling, src=srcs[a]))
            first += [copy(a, 1 + j, me, (*chip, c), src=srcs[a]) for j, chip in enumerate(chips)]
        for cp in first:
            cp.start()
        passed = []
        for j, chip in enumerate(chips):
            for a in range(3):
                copy(a, 1 + j, (*chip, c), me).wait_recv()
                fwd = copy(a, 4 + j, (*chip, c), sibling)
                fwd.start()
                passed.append(fwd)
        for a in range(3):
            copy(a, 0, sibling, me).wait_recv()
            for j, chip in enumerate(chips):
                copy(a, 4 + j, (*chip, 1 - c), me).wait_recv()
        for cp in first + passed:
            cp.wait_send()
        for cp in mine:
            cp.wait()

    shapes = [(D_MODEL, SHARD_IN), (SHARD_ROWS, 2 * C_W), (SHARD_ROWS, D_MODEL)]
    vmem = pl.BlockSpec(memory_space=pltpu.VMEM)
    hbm = pl.BlockSpec(memory_space=pl.ANY)
    return pl.pallas_call(
        body, name="all_gather_weights",
        in_specs=[vmem] * 3, out_specs=[hbm] * 3,
        out_shape=[jax.ShapeDtypeStruct((N_DEV,) + s, BF16) for s in shapes],
        scratch_shapes=[pltpu.VMEM(s, BF16) for s in shapes]
        + [pltpu.SemaphoreType.DMA((3, 7)), pltpu.SemaphoreType.DMA((3, 7)), pltpu.SemaphoreType.DMA((3,))],
        compiler_params=_params(),
    )(w_in, w_mem, w_out)


def _exchange_grads(d_win, d_wmem, d_wout, stats):
    shapes = [d_win.shape[1:], d_wmem.shape[1:], d_wout.shape[1:]]
    n_arr = len(shapes)

    def body(win, wm, wo, st, g_win, g_wm, g_wo, r_st, *scratch):
        mine, got, snd, rcv = (scratch[n_arr * t:n_arr * (t + 1)] for t in range(4))
        load_sems, d2d_send, d2d_recv, ici_send, ici_recv, st_send, st_recv, st_local = scratch[4 * n_arr:]
        x, y, c = _mesh_pos()
        me = 4 * x + 2 * y + c
        ins, outs = (win, wm, wo), (g_win, g_wm, g_wo)

        def chip(kk):
            return (1 - x if kk & 2 else x, 1 - y if kk & 1 else y)

        st_own = pltpu.make_async_copy(st, r_st.at[me], st_local)
        st_own.start()
        st_copies = []
        for s in range(1, N_DEV):
            peer = (1 - x if s & 4 else x, 1 - y if s & 2 else y, 1 - c if s & 1 else c)
            pid = 4 * peer[0] + 2 * peer[1] + peer[2]
            cp = pltpu.make_async_remote_copy(src_ref=st, dst_ref=r_st.at[me], send_sem=st_send.at[s],
                                              recv_sem=st_recv.at[s], device_id=peer, device_id_type=MESH_ID)
            cp.start()
            arrive = pltpu.make_async_remote_copy(src_ref=st, dst_ref=r_st.at[pid], send_sem=st_send.at[s],
                                                  recv_sem=st_recv.at[s], device_id=peer, device_id_type=MESH_ID)
            st_copies.append((cp, arrive))

        loads, swaps = {}, {}
        for kk in range(4):
            ox, oy = chip(kk)
            for a in range(n_arr):
                ld = pltpu.make_async_copy(ins[a].at[4 * ox + 2 * oy + c], mine[a].at[kk], load_sems.at[a, kk])
                ld.start()
                sw = pltpu.make_async_remote_copy(
                    src_ref=ins[a].at[4 * ox + 2 * oy + (1 - c)], dst_ref=got[a].at[kk],
                    send_sem=d2d_send.at[a, kk], recv_sem=d2d_recv.at[a, kk],
                    device_id=(x, y, 1 - c), device_id_type=MESH_ID)
                sw.start()
                loads[a, kk], swaps[a, kk] = ld, sw

        hops = {}
        for kk in (3, 1, 2):
            ox, oy = chip(kk)
            for a in range(n_arr):
                loads[a, kk].wait()
                swaps[a, kk].wait_recv()
                snd[a][kk - 1] = (mine[a][kk] + got[a][kk]).astype(BF16)
                hop = pltpu.make_async_remote_copy(
                    src_ref=snd[a].at[kk - 1], dst_ref=rcv[a].at[kk - 1],
                    send_sem=ici_send.at[a, kk], recv_sem=ici_recv.at[a, kk],
                    device_id=(ox, oy, c), device_id_type=MESH_ID)
                hop.start()
                hops[a, kk] = hop

        for a in range(n_arr):
            loads[a, 0].wait()
            swaps[a, 0].wait_recv()
            acc = mine[a][0] + got[a][0]
            for kk in (1, 2, 3):
                hops[a, kk].wait_recv()
                acc = acc + rcv[a][kk - 1].astype(F32)
            outs[a][...] = acc

        for sw in swaps.values():
            sw.wait_send()
        for hop in hops.values():
            hop.wait_send()
        for cp, arrive in st_copies:
            arrive.wait_recv()
            cp.wait_send()
        st_own.wait()

    hbm = pl.BlockSpec(memory_space=pl.ANY)
    vmem = pl.BlockSpec(memory_space=pltpu.VMEM)
    out_shape = [jax.ShapeDtypeStruct(s, F32) for s in shapes] + [jax.ShapeDtypeStruct((N_DEV,) + stats.shape, F32)]
    scratch = ([pltpu.VMEM((4,) + s, F32) for s in shapes] + [pltpu.VMEM((4,) + s, F32) for s in shapes]
               + [pltpu.VMEM((3,) + s, BF16) for s in shapes] + [pltpu.VMEM((3,) + s, BF16) for s in shapes]
               + [pltpu.SemaphoreType.DMA((n_arr, 4))] * 5
               + [pltpu.SemaphoreType.DMA((N_DEV,)), pltpu.SemaphoreType.DMA((N_DEV,)), pltpu.SemaphoreType.DMA(())])
    return pl.pallas_call(
        body, name="exchange_grads", in_specs=[hbm] * 4, out_specs=[vmem] * n_arr + [hbm], out_shape=out_shape,
        scratch_shapes=scratch, compiler_params=_params(),
    )(d_win, d_wmem, d_wout, stats)


def _reduce_adamw(recv, w, m, v, *, tr, name):
    n_part, rows, cols = recv.shape

    def body(r_ref, w_ref, m_ref, v_ref, g_ref, d_ref, nm_ref, nv_ref):
        g = r_ref[0]
        for s in range(1, n_part):
            g = g + r_ref[s]
        g_ref[...] = g
        m2 = ADAM_B1 * m_ref[...] + (1.0 - ADAM_B1) * g
        v2 = ADAM_B2 * v_ref[...] + (1.0 - ADAM_B2) * (g * g)
        nm_ref[...] = m2
        nv_ref[...] = v2
        m_hat = m2 / (1.0 - ADAM_B1 ** ADAM_STEP)
        v_hat = v2 / (1.0 - ADAM_B2 ** ADAM_STEP)
        d_ref[...] = -ADAM_LR * (m_hat / (jnp.sqrt(v_hat) + ADAM_EPS) + ADAM_WD * w_ref[...])

    blk = pl.BlockSpec((tr, cols), lambda i: (i, 0))
    return pl.pallas_call(
        body, name=name, grid=(rows // tr,),
        in_specs=[pl.BlockSpec((n_part, tr, cols), lambda i: (0, i, 0)), blk, blk, blk],
        out_specs=[blk] * 4, out_shape=[jax.ShapeDtypeStruct((rows, cols), F32)] * 4,
        compiler_params=_params(dimension_semantics=("arbitrary",)),
    )(recv, w, m, v)


def _pack_rows(pre, memn, post, sink):
    sink_row = jnp.pad(sink, ((0, 0), (0, D_MODEL - A_HEADS)))
    return jnp.concatenate([pre, memn, post, sink_row, jnp.zeros((4, D_MODEL), F32)], axis=0)


def kernel(x, mem, pre_norm, w_in, sink_a, mem_norm, w_mem_kv, w_out, post_norm, loss_target, m_pre_norm, m_w_in, m_sink_a, m_mem_norm, m_w_mem_kv, m_w_out, m_post_norm, v_pre_norm, v_w_in, v_sink_a, v_mem_norm, v_w_mem_kv, v_w_out, v_post_norm):
    w_in_full, w_mem_full, w_out_full = _all_gather_weights(w_in[0], w_mem_kv[0], w_out[0])
    sink = jnp.pad(sink_a[0], (0, 8 - A_HEADS))
    grad_x, d_win, d_wmem, d_wout, stats = _local_step(
        x[0], mem[0], pre_norm, w_in_full, sink, mem_norm, w_mem_full.reshape(D_MODEL, 2 * C_W),
        w_out_full.reshape(D_MODEL, D_MODEL), post_norm, loss_target[0])
    g_win, g_wmem, g_wout, r_stats = _exchange_grads(
        d_win, d_wmem.reshape(N_DEV, SHARD_ROWS, 2 * C_W), d_wout.reshape(N_DEV, SHARD_ROWS, D_MODEL), stats)

    big = {}
    for nm, g, w, m, v in (("w_in", g_win, w_in, m_w_in, v_w_in),
                           ("w_mem_kv", g_wmem, w_mem_kv, m_w_mem_kv, v_w_mem_kv),
                           ("w_out", g_wout, w_out, m_w_out, v_w_out)):
        res = _reduce_adamw(g[None], w[0], m[0], v[0], tr=SHARD_ROWS, name="adamw_" + nm)
        big[nm] = [t[None] for t in res]
    small = _reduce_adamw(
        r_stats, _pack_rows(pre_norm, mem_norm, post_norm, sink_a),
        _pack_rows(m_pre_norm, m_mem_norm, m_post_norm, m_sink_a),
        _pack_rows(v_pre_norm, v_mem_norm, v_post_norm, v_sink_a), tr=8, name="adamw_small")

    def unpack(t):
        return {"pre_norm": t[0:1], "mem_norm": t[1:2], "post_norm": t[2:3], "sink_a": t[3:4, 0:A_HEADS]}

    order = ("pre_norm", "w_in", "sink_a", "mem_norm", "w_mem_kv", "w_out", "post_norm")
    outs = [small[0][4, 0], grad_x[None]]
    for j in range(4):
        sm = unpack(small[j])
        outs += [big[n][j] if n in big else sm[n] for n in order]
    return tuple(outs)
```

```python
import jax
import jax.numpy as jnp
from jax import lax
from jax.experimental import pallas as pl
from jax.experimental.pallas import tpu as pltpu

F32 = jnp.float32
BF16 = jnp.bfloat16

D_MODEL = 1024
HEAD_DIM = 64
ROT_DIM = 16
ROPE_THETA = 500000.0
BLOCK = 128
LANES = 128
N_MEM = 256
RMS_EPS = 1e-6
SCALE = HEAD_DIM ** -0.5
A_HEADS, A_GROUP = 6, 3
B_HEADS = 6
C_HEADS = 4
A_W, A_KV_W, B_W, C_W = 384, 128, 384, 256
D_IN = 3072
N_DEV = 8
SHARD_IN = D_IN // N_DEV
SHARD_ROWS = D_MODEL // N_DEV
B_CONFIGS = ((128, 1), (512, 4), (2048, 16))
DILS = (4, 16)
NEG = -1e30
ATTN_BLOCKS_PER_STEP = 4
DELTA_LANE = 64
VMEM_LIMIT = 56 * 1024 * 1024

ADAM_LR, ADAM_B1, ADAM_B2, ADAM_EPS, ADAM_WD, ADAM_STEP = 0.001, 0.9, 0.999, 1e-08, 0.01, 10
MESH_ID = pl.DeviceIdType.MESH


def _params(**kw):
    return pltpu.CompilerParams(vmem_limit_bytes=VMEM_LIMIT, **kw)


def _full(shape):
    n = len(shape)
    return pl.BlockSpec(shape, lambda *_: (0,) * n)


def _row(tm, w):
    return pl.BlockSpec((tm, w), lambda i: (i, 0))


def _perm_view(a, dil):
    return a.reshape(a.shape[0] // (BLOCK * dil), dil, BLOCK, a.shape[1])


def _perm_spec(tm, dil, w):
    per = BLOCK * dil // tm
    return pl.BlockSpec((1, dil, tm // dil, w), lambda i: (i // per, 0, i % per, 0))


def _put(scr, val):
    for c in range(val.shape[1] // LANES):
        scr[c] = val[:, LANES * c:LANES * (c + 1)]


def _get(scr):
    n = scr.shape[0]
    return scr[0] if n == 1 else jnp.concatenate([scr[c] for c in range(n)], axis=1)


def _get_class(scr, r, dil):
    n, rows = scr.shape[0], scr.shape[1]
    parts = [scr.at[c][pl.ds(r, rows // dil, stride=dil), :] for c in range(n)]
    return parts[0] if n == 1 else jnp.concatenate(parts, axis=1)


def _store_permuted(scr, out_ref, dil, dtype):
    for r in range(dil):
        out_ref[0, r] = _get_class(scr, r, dil).astype(dtype)


def _load_permuted(in_ref, scr, dil):
    n, rows = scr.shape[0], scr.shape[1]
    for r in range(dil):
        val = in_ref[0, r].astype(F32)
        for c in range(n):
            scr.at[c][pl.ds(r, rows // dil, stride=dil), :] = val[:, LANES * c:LANES * (c + 1)]
    return _get(scr)


def _rope_tables(seq):
    j = jnp.arange(LANES) % HEAD_DIM
    freq = ROPE_THETA ** (-(2 * (j % (ROT_DIM // 2))).astype(F32) / ROT_DIM)
    ang = jnp.arange(seq, dtype=F32)[:, None] * freq[None, :]
    cos, sin = jnp.cos(ang), jnp.sin(ang)
    half = ROT_DIM // 2
    c = jnp.where(j < ROT_DIM, cos, 1.0)
    up = jnp.where((j >= half) & (j < ROT_DIM), sin, 0.0)
    dn = jnp.where(j < half, -sin, 0.0)
    return c, up, dn


def _rotate128(t, c, up, dn):
    return t * c + pltpu.roll(t, 8, 1) * up + pltpu.roll(t, LANES - 8, 1) * dn


def _rotate(t, c, up, dn):
    outs = [_rotate128(t[:, LANES * j:LANES * (j + 1)], c, up, dn) for j in range(t.shape[1] // LANES)]
    return outs[0] if len(outs) == 1 else jnp.concatenate(outs, axis=1)


def _inproj(x, pre_g, w_in_full, tabs, tm=512):
    seq = x.shape[0]
    n_chunk = D_IN // LANES

    def body(x_ref, g_ref, w_ref, c_ref, up_ref, dn_ref, u_ref, qa_ref, ka_ref, va_ref,
             qb1_ref, kb1_ref, vb1_ref, qb4_ref, kb4_ref, vb4_ref, qb16_ref, kb16_ref, vb16_ref,
             qc_ref, gate_ref, proj):
        xv = x_ref[...]
        r = lax.rsqrt(jnp.mean(xv * xv, axis=-1, keepdims=True) + RMS_EPS)
        u = ((xv * r) * g_ref[...]).astype(BF16)
        u_ref[...] = u
        per = SHARD_IN // LANES
        for k in range(N_DEV):
            acc = jnp.dot(u, w_ref[k], preferred_element_type=F32)
            for c3 in range(per):
                proj[per * k + c3] = acc[:, LANES * c3:LANES * (c3 + 1)]
        c, up, dn = c_ref[...], up_ref[...], dn_ref[...]

        def cols(lo, hi, rot=False, scale=None):
            parts = []
            for ch in range(lo // LANES, hi // LANES):
                t = proj[ch]
                if rot:
                    t = _rotate128(t, c, up, dn)
                if scale is not None:
                    t = t * scale
                parts.append(t)
            return parts[0] if len(parts) == 1 else jnp.concatenate(parts, axis=1)

        qa_ref[...] = cols(0, 384, True, SCALE).astype(BF16)
        ka_ref[...] = cols(384, 512, True).astype(BF16)
        va_ref[...] = cols(512, 640).astype(BF16)
        gate_ref[:, 0:384] = cols(640, 1024).astype(BF16)
        gate_ref[:, 384:768] = cols(2176, 2560).astype(BF16)
        gate_ref[:, 768:1024] = cols(2816, 3072).astype(BF16)
        qc_ref[...] = cols(2560, 2816, False, SCALE).astype(BF16)
        for ch in range(1024 // LANES, 1408 // LANES):
            proj[ch] = _rotate128(proj[ch], c, up, dn) * SCALE
        for ch in range(1408 // LANES, 1792 // LANES):
            proj[ch] = _rotate128(proj[ch], c, up, dn)
        for lo, nat, p4, p16 in ((1024, qb1_ref, qb4_ref, qb16_ref), (1408, kb1_ref, kb4_ref, kb16_ref),
                                 (1792, vb1_ref, vb4_ref, vb16_ref)):
            chunks = range(lo // LANES, lo // LANES + B_W // LANES)
            nat[...] = jnp.concatenate([proj[ch] for ch in chunks], axis=1).astype(BF16)
            for dil, ref in ((4, p4), (16, p16)):
                for rr in range(dil):
                    ref[0, rr] = jnp.concatenate(
                        [proj.at[ch][pl.ds(rr, tm // dil, stride=dil), :] for ch in chunks], axis=1).astype(BF16)

    nat_w = (D_MODEL, A_W, A_KV_W, A_KV_W, B_W, B_W, B_W)
    out_specs = [_row(tm, w) for w in nat_w]
    out_shape = [jax.ShapeDtypeStruct((seq, w), BF16) for w in nat_w]
    for dil in DILS:
        out_specs += [_perm_spec(tm, dil, B_W)] * 3
        out_shape += [jax.ShapeDtypeStruct((seq // (BLOCK * dil), dil, BLOCK, B_W), BF16)] * 3
    out_specs += [_row(tm, C_W), _row(tm, D_MODEL)]
    out_shape += [jax.ShapeDtypeStruct((seq, C_W), BF16), jax.ShapeDtypeStruct((seq, D_MODEL), BF16)]
    res = pl.pallas_call(
        body, name="inproj", grid=(seq // tm,),
        in_specs=[_row(tm, D_MODEL), _full((1, D_MODEL)), _full((N_DEV, D_MODEL, SHARD_IN)),
                  _row(tm, LANES), _row(tm, LANES), _row(tm, LANES)],
        out_specs=out_specs, out_shape=out_shape,
        scratch_shapes=[pltpu.VMEM((n_chunk, tm, LANES), F32)],
        compiler_params=_params(dimension_semantics=("arbitrary",)),
    )(x, pre_g, w_in_full, *tabs)
    u, qa, ka, va = res[0:4]
    qkv_b = {1: res[4:7], 4: [t.reshape(seq, B_W) for t in res[7:10]], 16: [t.reshape(seq, B_W) for t in res[10:13]]}
    return u, qa, ka, va, qkv_b, res[13], res[14]


def _memkv_fwd(mem, mem_g, w_mem_full):
    def body(mem_ref, g_ref, w_ref, mn_ref, mk_ref, mv_ref):
        mv_ = mem_ref[...]
        r = lax.rsqrt(jnp.mean(mv_ * mv_, axis=-1, keepdims=True) + RMS_EPS)
        mn = ((mv_ * r) * g_ref[...]).astype(BF16)
        mn_ref[...] = mn
        mkv = jnp.dot(mn, w_ref[...], preferred_element_type=F32)
        mk_ref[...] = mkv[:, 0:C_W].astype(BF16)
        mv_ref[...] = mkv[:, C_W:2 * C_W].astype(BF16)

    return pl.pallas_call(
        body, name="memkv_fwd",
        out_shape=[jax.ShapeDtypeStruct((N_MEM, D_MODEL), BF16),
                   jax.ShapeDtypeStruct((N_MEM, C_W), BF16), jax.ShapeDtypeStruct((N_MEM, C_W), BF16)],
        compiler_params=_params(),
    )(mem, mem_g, w_mem_full)


def _memkv_bwd(mem, mem_g, mn, w_mem_full, dmk, dmv):
    def body(mem_ref, g_ref, mn_ref, w_ref, dmk_ref, dmv_ref, dw_ref, st_ref):
        dmkv = jnp.concatenate([dmk_ref[...], dmv_ref[...]], axis=1).astype(BF16)
        dw_ref[...] = lax.dot_general(mn_ref[...], dmkv, (((0,), (0,)), ((), ())), preferred_element_type=F32)
        dmn = lax.dot_general(dmkv, w_ref[...], (((1,), (1,)), ((), ())), preferred_element_type=F32)
        mv_ = mem_ref[...]
        r = lax.rsqrt(jnp.mean(mv_ * mv_, axis=-1, keepdims=True) + RMS_EPS)
        st_ref[...] = jnp.zeros_like(st_ref)
        st_ref[0:1, :] = jnp.sum(dmn * (mv_ * r), axis=0, keepdims=True)

    return pl.pallas_call(
        body, name="memkv_bwd",
        out_shape=[jax.ShapeDtypeStruct((D_MODEL, 2 * C_W), F32), jax.ShapeDtypeStruct((8, D_MODEL), F32)],
        compiler_params=_params(),
    )(mem, mem_g, mn, w_mem_full, dmk, dmv)


def _band_mask(has_prev, max_dist):
    qi = lax.broadcasted_iota(jnp.int32, (BLOCK, 2 * BLOCK), 0)
    kj = lax.broadcasted_iota(jnp.int32, (BLOCK, 2 * BLOCK), 1)
    dist = qi + BLOCK - kj
    return (dist >= 0) & (dist <= max_dist) & ((kj >= BLOCK) | has_prev)


_NT = (((1,), (1,)), ((), ()))
_TN = (((0,), (0,)), ((), ()))


def _head_only(val, h):
    slab = val[:, LANES * (h // 2):LANES * (h // 2 + 1)]
    lane = lax.broadcasted_iota(jnp.int32, slab.shape, 1)
    keep = (lane < HEAD_DIM) if h % 2 == 0 else (lane >= HEAD_DIM)
    return jnp.where(keep, slab, jnp.zeros((), slab.dtype))


class _KvSlabs:
    def __init__(self, cat, group):
        self.cat, self.group, self.swapped = cat, group, {}

    def is_swapped(self, h):
        return (h // self.group) % 2 != h % 2

    def __call__(self, h):
        j = (h // self.group) // 2
        slab = self.cat[:, LANES * j:LANES * (j + 1)]
        if not self.is_swapped(h):
            return slab
        if j not in self.swapped:
            self.swapped[j] = jnp.concatenate([slab[:, HEAD_DIM:], slab[:, :HEAD_DIM]], axis=1)
        return self.swapped[j]


class _BandSteps:
    def __init__(self, seq, dil, nq):
        self.nq, self.rows, self.consecutive = nq, nq * BLOCK, dil == 1
        nb = seq // dil // BLOCK
        if self.consecutive:
            assert nb % nq == 0
            self.outer, self.inner, self.stride = 1, nb // nq, 1
        else:
            assert dil % nq == 0
            self.outer, self.inner, self.stride = dil // nq, nb, dil // nq

    def own(self, w, clamp=False):
        cur = (lambda i: jnp.minimum(i, self.inner - 1)) if clamp else (lambda i: i)
        return pl.BlockSpec((self.rows, w), lambda r, i: (cur(i) * self.stride + r, 0))

    def prev(self, w, clamp=False):
        cur = (lambda i: jnp.minimum(i, self.inner - 1)) if clamp else (lambda i: i)
        if self.consecutive:
            return pl.BlockSpec((BLOCK, w), lambda r, i: (jnp.maximum(cur(i) * self.nq - 1, 0), 0))
        return pl.BlockSpec((self.rows, w), lambda r, i: (jnp.maximum(cur(i) - 1, 0) * self.stride + r, 0))

    def late(self, w):
        return pl.BlockSpec((self.rows, w), lambda r, i: (jnp.maximum(i - 1, 0) * self.stride + r, 0))

    def rows_of(self, j):
        return slice(BLOCK * j, BLOCK * (j + 1))

    def keys(self, p_ref, c_ref, j):
        if not self.consecutive:
            before = p_ref[self.rows_of(j), :]
        elif j == 0:
            before = p_ref[...]
        else:
            before = c_ref[self.rows_of(j - 1), :]
        return jnp.concatenate([before, c_ref[self.rows_of(j), :]], axis=0)

    def has_prev(self, i, j):
        return True if (self.consecutive and j > 0) else (i > 0)


def _banded_fwd(q, k, v, sink, *, dil, heads, group, max_dist, nq, name):
    seq = q.shape[0]
    kvh = heads // group
    qw, kw = heads * HEAD_DIM, kvh * HEAD_DIM
    steps = _BandSteps(seq, dil, nq)

    def body(*refs):
        if sink is not None:
            sink_ref, refs = refs[0], refs[1:]
        q_ref, kp_ref, kc_ref, vp_ref, vc_ref, o_ref, lse_ref, s_scr, p_scr = refs
        i = pl.program_id(1)
        lane = lax.broadcasted_iota(jnp.int32, (BLOCK, LANES), 1)
        k_of = [_KvSlabs(steps.keys(kp_ref, kc_ref, j), group) for j in range(nq)]
        v_of = [_KvSlabs(steps.keys(vp_ref, vc_ref, j), group) for j in range(nq)]
        for j in range(nq):
            qv = q_ref[steps.rows_of(j), :]
            for h in range(heads):
                s_scr[j * heads + h] = lax.dot_general(_head_only(qv, h), k_of[j](h), _NT, preferred_element_type=F32)
        ls = {}
        for j in range(nq):
            valid = _band_mask(steps.has_prev(i, j), max_dist)
            lse_tile = jnp.zeros((BLOCK, LANES), F32)
            for h in range(heads):
                s = jnp.where(valid, s_scr[j * heads + h], NEG)
                m = jnp.max(s, axis=-1, keepdims=True)
                if sink is not None:
                    sk = sink_ref[h]
                    m = jnp.maximum(m, sk)
                p = jnp.exp(s - m)
                l = jnp.sum(p, axis=-1, keepdims=True)
                if sink is not None:
                    l = l + jnp.exp(sk - m)
                p_scr[j * heads + h] = p.astype(BF16)
                ls[j, h] = l
                lse_tile = jnp.where(lane == h, m + jnp.log(l), lse_tile)
            lse_ref[steps.rows_of(j), :] = lse_tile
        for j in range(nq):
            for pr in range(heads // 2):
                he, ho = 2 * pr, 2 * pr + 1
                even = jnp.dot(p_scr[j * heads + he], v_of[j](he), preferred_element_type=F32) / ls[j, he]
                odd = jnp.dot(p_scr[j * heads + ho], v_of[j](ho), preferred_element_type=F32) / ls[j, ho]
                o_ref[steps.rows_of(j), LANES * pr:LANES * (pr + 1)] = jnp.where(lane < HEAD_DIM, even, odd)

    in_specs = [steps.own(qw), steps.prev(kw), steps.own(kw), steps.prev(kw), steps.own(kw)]
    args = [q, k, k, v, v]
    if sink is not None:
        in_specs = [pl.BlockSpec(memory_space=pltpu.SMEM)] + in_specs
        args = [sink] + args
    return pl.pallas_call(
        body, name=name, grid=(steps.outer, steps.inner), in_specs=in_specs,
        out_specs=[steps.own(qw), steps.own(LANES)],
        out_shape=[jax.ShapeDtypeStruct((seq, qw), F32), jax.ShapeDtypeStruct((seq, LANES), F32)],
        scratch_shapes=[pltpu.VMEM((nq * heads, BLOCK, 2 * BLOCK), F32), pltpu.VMEM((nq * heads, BLOCK, 2 * BLOCK), BF16)],
        compiler_params=_params(dimension_semantics=("arbitrary", "arbitrary")),
    )(*args)


def _banded_bwd(q, k, v, d_out, stat, sink, *, dil, heads, group, max_dist, nq, name):
    seq = q.shape[0]
    kvh = heads // group
    qw, kw = heads * HEAD_DIM, kvh * HEAD_DIM
    steps = _BandSteps(seq, dil, nq)
    n_in = 7

    def body(*refs):
        if sink is not None:
            sink_ref, refs = refs[0], refs[1:]
            dsink_ref, refs = refs[n_in], refs[:n_in] + refs[n_in + 1:]
        (q_ref, kp_ref, kc_ref, vp_ref, vc_ref, do_ref, st_ref, dq_ref, dk_ref, dv_ref,
         kcar, vcar, s_scr, dp_scr, p_scr, ds_scr) = refs
        r, i = pl.program_id(0), pl.program_id(1)

        @pl.when(i == 0)
        def _():
            kcar[...] = jnp.zeros_like(kcar)
            vcar[...] = jnp.zeros_like(vcar)

        if sink is not None:
            @pl.when((i == 0) & (r == 0))
            def _():
                dsink_ref[...] = jnp.zeros_like(dsink_ref)

        @pl.when(i < steps.inner)
        def _():
            lane = lax.broadcasted_iota(jnp.int32, (1, LANES), 1)
            lane_q = lax.broadcasted_iota(jnp.int32, (BLOCK, LANES), 1)
            k_of = [_KvSlabs(steps.keys(kp_ref, kc_ref, j), group) for j in range(nq)]
            v_of = [_KvSlabs(steps.keys(vp_ref, vc_ref, j), group) for j in range(nq)]
            qms, doms = {}, {}
            for j in range(nq):
                qv, dov = q_ref[steps.rows_of(j), :], do_ref[steps.rows_of(j), :]
                for h in range(heads):
                    qms[j, h], doms[j, h] = _head_only(qv, h), _head_only(dov, h)
                    s_scr[j * heads + h] = lax.dot_general(qms[j, h], k_of[j](h), _NT, preferred_element_type=F32)
                    dp_scr[j * heads + h] = lax.dot_general(doms[j, h], v_of[j](h), _NT, preferred_element_type=F32)
            dsink_row = jnp.zeros((1, LANES), F32)
            for j in range(nq):
                st = st_ref[steps.rows_of(j), :]
                valid = _band_mask(steps.has_prev(i, j), max_dist)
                for h in range(heads):
                    lse_h = st[:, h:h + 1]
                    delta = st[:, DELTA_LANE + h:DELTA_LANE + h + 1]
                    p = jnp.where(valid, jnp.exp(s_scr[j * heads + h] - lse_h), 0.0)
                    p_scr[j * heads + h] = p.astype(BF16)
                    ds_scr[j * heads + h] = (p * (dp_scr[j * heads + h] - delta)).astype(BF16)
                    if sink is not None:
                        ds_sink = jnp.sum(-jnp.exp(sink_ref[h] - lse_h) * delta, axis=0, keepdims=True)
                        dsink_row = dsink_row + jnp.where(lane == h, ds_sink, 0.0)
            for j in range(nq):
                for pr in range(heads // 2):
                    he, ho = 2 * pr, 2 * pr + 1
                    even = jnp.dot(ds_scr[j * heads + he], k_of[j](he), preferred_element_type=F32)
                    odd = jnp.dot(ds_scr[j * heads + ho], k_of[j](ho), preferred_element_type=F32)
                    dq_ref[steps.rows_of(j), LANES * pr:LANES * (pr + 1)] = jnp.where(lane_q < HEAD_DIM, even, odd)
            if steps.consecutive:
                dk_ref[...] = kcar[...]
                dv_ref[...] = vcar[...]
            for j in range(nq):
                for slab in range(kw // LANES):
                    acc = {}
                    for h in range(heads):
                        if (h // group) // 2 != slab:
                            continue
                        key = k_of[j].is_swapped(h)
                        dk_h = lax.dot_general(ds_scr[j * heads + h], qms[j, h], _TN, preferred_element_type=F32)
                        dv_h = lax.dot_general(p_scr[j * heads + h], doms[j, h], _TN, preferred_element_type=F32)
                        acc[key] = (dk_h, dv_h) if key not in acc else (acc[key][0] + dk_h, acc[key][1] + dv_h)
                    dk_j, dv_j = acc.get(False, (None, None))
                    if True in acc:
                        unswap = lambda t: jnp.concatenate([t[:, HEAD_DIM:], t[:, :HEAD_DIM]], axis=1)
                        dk_s, dv_s = unswap(acc[True][0]), unswap(acc[True][1])
                        dk_j = dk_s if dk_j is None else dk_j + dk_s
                        dv_j = dv_s if dv_j is None else dv_j + dv_s
                    sl = slice(LANES * slab, LANES * (slab + 1))
                    own_rows = steps.rows_of(j)
                    if not steps.consecutive:
                        dk_ref[own_rows, sl] = kcar[own_rows, sl] + dk_j[0:BLOCK]
                        dv_ref[own_rows, sl] = vcar[own_rows, sl] + dv_j[0:BLOCK]
                    elif j == 0:
                        last = steps.rows_of(nq - 1)
                        dk_ref[last, sl] += dk_j[0:BLOCK]
                        dv_ref[last, sl] += dv_j[0:BLOCK]
                    else:
                        before = steps.rows_of(j - 1)
                        kcar[before, sl] += dk_j[0:BLOCK]
                        vcar[before, sl] += dv_j[0:BLOCK]
                    kcar[own_rows, sl] = dk_j[BLOCK:2 * BLOCK]
                    vcar[own_rows, sl] = dv_j[BLOCK:2 * BLOCK]
            if sink is not None:
                dsink_ref[0:1, :] += dsink_row

        @pl.when(i == steps.inner)
        def _():
            dk_ref[...] = kcar[...]
            dv_ref[...] = vcar[...]

    own, prev = (lambda w: steps.own(w, clamp=True)), (lambda w: steps.prev(w, clamp=True))
    in_specs = [own(qw), prev(kw), own(kw), prev(kw), own(kw), own(qw), own(LANES)]
    args = [q, k, k, v, v, d_out, stat]
    out_specs = [own(qw), steps.late(kw), steps.late(kw)]
    out_shape = [jax.ShapeDtypeStruct((seq, qw), F32), jax.ShapeDtypeStruct((seq, kw), F32),
                 jax.ShapeDtypeStruct((seq, kw), F32)]
    if sink is not None:
        in_specs = [pl.BlockSpec(memory_space=pltpu.SMEM)] + in_specs
        args = [sink] + args
        out_specs = [_full((8, LANES))] + out_specs
        out_shape = [jax.ShapeDtypeStruct((8, LANES), F32)] + out_shape
    n_hb = nq * heads
    res = pl.pallas_call(
        body, name=name, grid=(steps.outer, steps.inner + 1), in_specs=in_specs, out_specs=out_specs,
        out_shape=out_shape,
        scratch_shapes=[pltpu.VMEM((steps.rows, kw), F32), pltpu.VMEM((steps.rows, kw), F32)]
        + [pltpu.VMEM((n_hb, BLOCK, 2 * BLOCK), F32)] * 2 + [pltpu.VMEM((n_hb, BLOCK, 2 * BLOCK), BF16)] * 2,
        compiler_params=_params(dimension_semantics=("arbitrary", "arbitrary")),
    )(*args)
    if sink is not None:
        return res[1], res[2], res[3], res[0]
    return res


def _cross_fwd(q, mk, mv, tq=512):
    seq = q.shape[0]

    def body(q_ref, mk_ref, mv_ref, o_ref, lse_ref, s_scr, p_scr):
        qv = q_ref[...]
        k_of, v_of = _KvSlabs(mk_ref[...], 1), _KvSlabs(mv_ref[...], 1)
        lane = lax.broadcasted_iota(jnp.int32, (tq, LANES), 1)
        lse_tile = jnp.zeros((tq, LANES), F32)
        for h in range(C_HEADS):
            s_scr[h] = lax.dot_general(_head_only(qv, h), k_of(h), _NT, preferred_element_type=F32)
        ls = []
        for h in range(C_HEADS):
            s = s_scr[h]
            m = jnp.max(s, axis=-1, keepdims=True)
            p = jnp.exp(s - m)
            l = jnp.sum(p, axis=-1, keepdims=True)
            p_scr[h] = p.astype(BF16)
            ls.append(l)
            lse_tile = jnp.where(lane == h, m + jnp.log(l), lse_tile)
        for pr in range(C_HEADS // 2):
            even = jnp.dot(p_scr[2 * pr], v_of(2 * pr), preferred_element_type=F32) / ls[2 * pr]
            odd = jnp.dot(p_scr[2 * pr + 1], v_of(2 * pr + 1), preferred_element_type=F32) / ls[2 * pr + 1]
            o_ref[:, LANES * pr:LANES * (pr + 1)] = jnp.where(lane < HEAD_DIM, even, odd)
        lse_ref[...] = lse_tile

    return pl.pallas_call(
        body, name="cross_fwd", grid=(seq // tq,),
        in_specs=[_row(tq, C_W), _full((N_MEM, C_W)), _full((N_MEM, C_W))],
        out_specs=[_row(tq, C_W), _row(tq, LANES)],
        out_shape=[jax.ShapeDtypeStruct((seq, C_W), F32), jax.ShapeDtypeStruct((seq, LANES), F32)],
        scratch_shapes=[pltpu.VMEM((C_HEADS, tq, N_MEM), F32), pltpu.VMEM((C_HEADS, tq, N_MEM), BF16)],
        compiler_params=_params(dimension_semantics=("arbitrary",)),
    )(q, mk, mv)


def _cross_bwd(q, mk, mv, d_out, stat, tq=512):
    seq = q.shape[0]

    def body(q_ref, mk_ref, mv_ref, do_ref, st_ref, dq_ref, dmk_ref, dmv_ref, s_scr, dp_scr, p_scr, ds_scr):
        @pl.when(pl.program_id(0) == 0)
        def _():
            dmk_ref[...] = jnp.zeros_like(dmk_ref)
            dmv_ref[...] = jnp.zeros_like(dmv_ref)

        qv, dov, st = q_ref[...], do_ref[...], st_ref[...]
        k_of, v_of = _KvSlabs(mk_ref[...], 1), _KvSlabs(mv_ref[...], 1)
        qms = [_head_only(qv, h) for h in range(C_HEADS)]
        doms = [_head_only(dov, h) for h in range(C_HEADS)]
        for h in range(C_HEADS):
            s_scr[h] = lax.dot_general(qms[h], k_of(h), _NT, preferred_element_type=F32)
            dp_scr[h] = lax.dot_general(doms[h], v_of(h), _NT, preferred_element_type=F32)
        for h in range(C_HEADS):
            p = jnp.exp(s_scr[h] - st[:, h:h + 1])
            p_scr[h] = p.astype(BF16)
            ds_scr[h] = (p * (dp_scr[h] - st[:, DELTA_LANE + h:DELTA_LANE + h + 1])).astype(BF16)
        lane = lax.broadcasted_iota(jnp.int32, (tq, LANES), 1)
        for pr in range(C_HEADS // 2):
            sl = slice(LANES * pr, LANES * (pr + 1))
            even = jnp.dot(ds_scr[2 * pr], k_of(2 * pr), preferred_element_type=F32)
            odd = jnp.dot(ds_scr[2 * pr + 1], k_of(2 * pr + 1), preferred_element_type=F32)
            dq_ref[:, sl] = jnp.where(lane < HEAD_DIM, even, odd)
            dmk_ref[:, sl] += (lax.dot_general(ds_scr[2 * pr], qms[2 * pr], _TN, preferred_element_type=F32)
                               + lax.dot_general(ds_scr[2 * pr + 1], qms[2 * pr + 1], _TN, preferred_element_type=F32))
            dmv_ref[:, sl] += (lax.dot_general(p_scr[2 * pr], doms[2 * pr], _TN, preferred_element_type=F32)
                               + lax.dot_general(p_scr[2 * pr + 1], doms[2 * pr + 1], _TN, preferred_element_type=F32))

    return pl.pallas_call(
        body, name="cross_bwd", grid=(seq // tq,),
        in_specs=[_row(tq, C_W), _full((N_MEM, C_W)), _full((N_MEM, C_W)), _row(tq, C_W), _row(tq, LANES)],
        out_specs=[_row(tq, C_W), _full((N_MEM, C_W)), _full((N_MEM, C_W))],
        out_shape=[jax.ShapeDtypeStruct((seq, C_W), F32), jax.ShapeDtypeStruct((N_MEM, C_W), F32),
                   jax.ShapeDtypeStruct((N_MEM, C_W), F32)],
        scratch_shapes=[pltpu.VMEM((C_HEADS, tq, N_MEM), F32)] * 2 + [pltpu.VMEM((C_HEADS, tq, N_MEM), BF16)] * 2,
        compiler_params=_params(dimension_semantics=("arbitrary",)),
    )(q, mk, mv, d_out, stat)


def _per_head(tile, width):
    rows = tile.shape[0]
    return jnp.concatenate(
        [jnp.broadcast_to(tile[:, h:h + 1], (rows, HEAD_DIM)) for h in range(width // HEAD_DIM)], axis=1)


def _with_delta(lse_tile, prod):
    rows = lse_tile.shape[0]
    lane = lax.broadcasted_iota(jnp.int32, (rows, LANES), 1)
    tile = lse_tile
    for h in range(prod.shape[1] // HEAD_DIM):
        d = jnp.sum(prod[:, HEAD_DIM * h:HEAD_DIM * (h + 1)], axis=-1, keepdims=True)
        tile = jnp.where(lane == DELTA_LANE + h, d, tile)
    return tile


def _mid(oa, lse_a, ob, lse_b, oc, lse_c, gate, x, target, w_out_full, post_g, tm=256):
    seq = x.shape[0]
    n_b = B_W // LANES

    def body(oa_ref, la_ref, b1_ref, l1_ref, b4_ref, l4_ref, b16_ref, l16_ref, oc_ref, lc_ref,
             gate_ref, x_ref, t_ref, w_ref, pg_ref,
             dh_ref, dg_ref, doa_ref, sa_ref, dob1_ref, sb1_ref, dob4_ref, sb4_ref, dob16_ref, sb16_ref,
             doc_ref, sc_ref, dw_ref, st_ref, scr_b4, scr_b16, scr_l4, scr_l16, scr_do, scr_sb):
        @pl.when(pl.program_id(0) == 0)
        def _():
            dw_ref[...] = jnp.zeros_like(dw_ref)
            st_ref[...] = jnp.zeros_like(st_ref)

        b1, l1 = b1_ref[...], l1_ref[...]
        b4, l4 = _load_permuted(b4_ref, scr_b4, 4), _load_permuted(l4_ref, scr_l4, 4)
        b16, l16 = _load_permuted(b16_ref, scr_b16, 16), _load_permuted(l16_ref, scr_l16, 16)
        lm = jnp.maximum(jnp.maximum(l1, l4), l16)
        e1, e4, e16 = jnp.exp(l1 - lm), jnp.exp(l4 - lm), jnp.exp(l16 - lm)
        den = e1 + e4 + e16
        lse_b_tile = lm + jnp.log(den)
        ob_v = _per_head(e1 / den, B_W) * b1 + _per_head(e4 / den, B_W) * b4 + _per_head(e16 / den, B_W) * b16
        o_all = jnp.concatenate([oa_ref[...], ob_v, oc_ref[...]], axis=1)
        g = gate_ref[...].astype(F32)
        sig = 1.0 / (1.0 + jnp.exp(-g))
        silu = g * sig
        y = (o_all * silu).astype(BF16)
        w = w_ref[...]
        z = jnp.dot(y, w, preferred_element_type=F32)
        rz = lax.rsqrt(jnp.mean(z * z, axis=-1, keepdims=True) + RMS_EPS)
        hn = z * rz
        pg = pg_ref[...]
        err = (x_ref[...] + hn * pg) - t_ref[...]
        loss = 0.5 * jnp.sum(jnp.mean(err * err, axis=-1, keepdims=True), axis=0, keepdims=True)
        dh = err * (1.0 / D_MODEL)
        dh_ref[...] = dh
        st_ref[0:1, :] += jnp.sum(dh * hn, axis=0, keepdims=True)
        st_ref[1:2, :] += jnp.broadcast_to(loss, (1, D_MODEL))
        dhn = dh * pg
        dz = (rz * (dhn - hn * jnp.mean(dhn * hn, axis=-1, keepdims=True))).astype(BF16)
        dy = lax.dot_general(dz, w, (((1,), (1,)), ((), ())), preferred_element_type=F32)
        dw_ref[...] += lax.dot_general(y, dz, (((0,), (0,)), ((), ())), preferred_element_type=F32)
        dg_ref[...] = (dy * o_all * (sig * (1.0 + g * (1.0 - sig)))).astype(BF16)
        d_o = (dy * silu).astype(BF16)
        prod = d_o.astype(F32) * o_all
        doa_ref[...] = d_o[:, 0:A_W]
        sa_ref[...] = _with_delta(la_ref[...], prod[:, 0:A_W])
        doc_ref[...] = d_o[:, A_W + B_W:D_MODEL]
        sc_ref[...] = _with_delta(lc_ref[...], prod[:, A_W + B_W:D_MODEL])
        d_ob = d_o[:, A_W:A_W + B_W]
        stat_b = _with_delta(lse_b_tile, prod[:, A_W:A_W + B_W])
        dob1_ref[...] = d_ob
        sb1_ref[...] = stat_b
        _put(scr_do, d_ob.astype(F32))
        _put(scr_sb, stat_b)
        _store_permuted(scr_do, dob4_ref, 4, BF16)
        _store_permuted(scr_sb, sb4_ref, 4, F32)
        _store_permuted(scr_do, dob16_ref, 16, BF16)
        _store_permuted(scr_sb, sb16_ref, 16, F32)

    p4 = lambda w: _perm_spec(tm, 4, w)
    p16 = lambda w: _perm_spec(tm, 16, w)
    in_specs = [_row(tm, A_W), _row(tm, LANES), _row(tm, B_W), _row(tm, LANES), p4(B_W), p4(LANES), p16(B_W), p16(LANES),
                _row(tm, C_W), _row(tm, LANES), _row(tm, D_MODEL), _row(tm, D_MODEL), _row(tm, D_MODEL),
                _full((D_MODEL, D_MODEL)), _full((1, D_MODEL))]
    sds = jax.ShapeDtypeStruct
    v4 = lambda w, dt: sds((seq // (BLOCK * 4), 4, BLOCK, w), dt)
    v16 = lambda w, dt: sds((seq // (BLOCK * 16), 16, BLOCK, w), dt)
    out_specs = [_row(tm, D_MODEL), _row(tm, D_MODEL), _row(tm, A_W), _row(tm, LANES), _row(tm, B_W), _row(tm, LANES),
                 p4(B_W), p4(LANES), p16(B_W), p16(LANES), _row(tm, C_W), _row(tm, LANES),
                 _full((D_MODEL, D_MODEL)), _full((8, D_MODEL))]
    out_shape = [sds((seq, D_MODEL), F32), sds((seq, D_MODEL), BF16), sds((seq, A_W), BF16), sds((seq, LANES), F32),
                 sds((seq, B_W), BF16), sds((seq, LANES), F32), v4(B_W, BF16), v4(LANES, F32), v16(B_W, BF16),
                 v16(LANES, F32), sds((seq, C_W), BF16), sds((seq, LANES), F32),
                 sds((D_MODEL, D_MODEL), F32), sds((8, D_MODEL), F32)]
    res = pl.pallas_call(
        body, name="mid", grid=(seq // tm,), in_specs=in_specs, out_specs=out_specs, out_shape=out_shape,
        scratch_shapes=[pltpu.VMEM((n_b, tm, LANES), F32), pltpu.VMEM((n_b, tm, LANES), F32),
                        pltpu.VMEM((1, tm, LANES), F32), pltpu.VMEM((1, tm, LANES), F32),
                        pltpu.VMEM((n_b, tm, LANES), F32), pltpu.VMEM((1, tm, LANES), F32)],
        compiler_params=_params(dimension_semantics=("arbitrary",)),
    )(oa, lse_a, ob[1], lse_b[1], _perm_view(ob[4], 4), _perm_view(lse_b[4], 4), _perm_view(ob[16], 16),
      _perm_view(lse_b[16], 16), oc, lse_c, gate, x, target, w_out_full, post_g)
    dh, d_gate, do_a, st_a, do_b1, st_b1, do_b4, st_b4, do_b16, st_b16, do_c, st_c, d_wout, stats = res
    flat = lambda t: t.reshape(seq, t.shape[-1])
    d_b = {1: (do_b1, st_b1), 4: (flat(do_b4), flat(st_b4)), 16: (flat(do_b16), flat(st_b16))}
    return dh, d_gate, (do_a, st_a), d_b, (do_c, st_c), d_wout, stats


def _inproj_bwd_x(x, dh, pre_g, w_in_full, tabs, dqa, dka, dva, dqkv_b, dqc, dgate, tm=256):
    seq = x.shape[0]
    n_b = B_W // LANES

    def body(x_ref, dh_ref, g_ref, w_ref, c_ref, up_ref, dn_ref, dqa_ref, dka_ref, dva_ref,
             dq1, dk1, dv1, dq4, dk4, dv4, dq16, dk16, dv16, dqc_ref, dg_ref,
             gx_ref, dp_ref, st_ref, scr4, scr16):
        @pl.when(pl.program_id(0) == 0)
        def _():
            st_ref[...] = jnp.zeros_like(st_ref)

        c, up, dn = c_ref[...], -up_ref[...], -dn_ref[...]
        unrot = lambda t: _rotate(t, c, up, dn)
        total = lambda r1, r4, r16: r1[...] + _load_permuted(r4, scr4, 4) + _load_permuted(r16, scr16, 16)
        dp_ref[:, 0:384] = (unrot(dqa_ref[...]) * SCALE).astype(BF16)
        dp_ref[:, 384:512] = unrot(dka_ref[...]).astype(BF16)
        dp_ref[:, 512:640] = dva_ref[...].astype(BF16)
        dp_ref[:, 640:1024] = dg_ref[:, 0:384]
        dp_ref[:, 1024:1408] = (unrot(total(dq1, dq4, dq16)) * SCALE).astype(BF16)
        dp_ref[:, 1408:1792] = unrot(total(dk1, dk4, dk16)).astype(BF16)
        dp_ref[:, 1792:2176] = total(dv1, dv4, dv16).astype(BF16)
        dp_ref[:, 2176:2560] = dg_ref[:, 384:768]
        dp_ref[:, 2560:2816] = (dqc_ref[...] * SCALE).astype(BF16)
        dp_ref[:, 2816:3072] = dg_ref[:, 768:1024]
        du = jnp.zeros((tm, D_MODEL), F32)
        for k in range(N_DEV):
            du = du + lax.dot_general(dp_ref[:, SHARD_IN * k:SHARD_IN * (k + 1)], w_ref[k],
                                      (((1,), (1,)), ((), ())), preferred_element_type=F32)
        xv = x_ref[...]
        r = lax.rsqrt(jnp.mean(xv * xv, axis=-1, keepdims=True) + RMS_EPS)
        xh = xv * r
        st_ref[0:1, :] += jnp.sum(du * xh, axis=0, keepdims=True)
        dxh = du * g_ref[...]
        gx_ref[...] = dh_ref[...] + r * (dxh - xh * jnp.mean(dxh * xh, axis=-1, keepdims=True))

    in_specs = ([_row(tm, D_MODEL), _row(tm, D_MODEL), _full((1, D_MODEL)), _full((N_DEV, D_MODEL, SHARD_IN)),
                 _row(tm, LANES), _row(tm, LANES), _row(tm, LANES), _row(tm, A_W), _row(tm, A_KV_W), _row(tm, A_KV_W)]
                + [_row(tm, B_W)] * 3 + [_perm_spec(tm, 4, B_W)] * 3 + [_perm_spec(tm, 16, B_W)] * 3
                + [_row(tm, C_W), _row(tm, D_MODEL)])
    return pl.pallas_call(
        body, name="inproj_bwd_x", grid=(seq // tm,), in_specs=in_specs,
        out_specs=[_row(tm, D_MODEL), _row(tm, D_IN), _full((8, D_MODEL))],
        out_shape=[jax.ShapeDtypeStruct((seq, D_MODEL), F32), jax.ShapeDtypeStruct((seq, D_IN), BF16),
                   jax.ShapeDtypeStruct((8, D_MODEL), F32)],
        scratch_shapes=[pltpu.VMEM((n_b, tm, LANES), F32), pltpu.VMEM((n_b, tm, LANES), F32)],
        compiler_params=_params(dimension_semantics=("arbitrary",)),
    )(x, dh, pre_g, w_in_full, *tabs, dqa, dka, dva, *dqkv_b[1], *[_perm_view(t, 4) for t in dqkv_b[4]],
      *[_perm_view(t, 16) for t in dqkv_b[16]], dqc, dgate)


def _inproj_bwd_w(u, dproj, tm=1024):
    seq = u.shape[0]

    def body(u_ref, dp_ref, dw_ref):
        @pl.when(pl.program_id(1) == 0)
        def _():
            dw_ref[...] = jnp.zeros_like(dw_ref)

        dw_ref[0] += lax.dot_general(u_ref[...], dp_ref[...], (((0,), (0,)), ((), ())), preferred_element_type=F32)

    return pl.pallas_call(
        body, name="inproj_bwd_w", grid=(N_DEV, seq // tm),
        in_specs=[pl.BlockSpec((tm, D_MODEL), lambda k, t: (t, 0)), pl.BlockSpec((tm, SHARD_IN), lambda k, t: (t, k))],
        out_specs=pl.BlockSpec((1, D_MODEL, SHARD_IN), lambda k, t: (k, 0, 0)),
        out_shape=jax.ShapeDtypeStruct((N_DEV, D_MODEL, SHARD_IN), F32),
        compiler_params=_params(dimension_semantics=("arbitrary", "arbitrary")),
    )(u, dproj)


def _local_step(x, mem, pre_g, w_in_full, sink, mem_g, w_mem_full, w_out_full, post_g, target):
    seq = x.shape[0]
    tabs = _rope_tables(seq)
    u, qa, ka, va, qkv_b, qc, gate = _inproj(x, pre_g, w_in_full, tabs)
    mn, mk, mv = _memkv_fwd(mem, mem_g, w_mem_full)

    a_cfg = dict(dil=1, heads=A_HEADS, group=A_GROUP, max_dist=BLOCK - 1, nq=ATTN_BLOCKS_PER_STEP)
    b_cfgs = {dil: dict(dil=dil, heads=B_HEADS, group=1, max_dist=win // dil, nq=ATTN_BLOCKS_PER_STEP)
              for win, dil in B_CONFIGS}
    oa, lse_a = _banded_fwd(qa, ka, va, sink, name="attn_a_fwd", **a_cfg)
    ob, lse_b = {}, {}
    for dil, cfg in b_cfgs.items():
        ob[dil], lse_b[dil] = _banded_fwd(*qkv_b[dil], None, name=f"attn_b{dil}_fwd", **cfg)
    oc, lse_c = _cross_fwd(qc, mk, mv)

    dh, d_gate, d_a, d_b, d_c, d_wout, st_mid = _mid(oa, lse_a, ob, lse_b, oc, lse_c, gate, x, target, w_out_full, post_g)

    dqa, dka, dva, dsink = _banded_bwd(qa, ka, va, *d_a, sink, name="attn_a_bwd", **a_cfg)
    dqkv_b = {dil: _banded_bwd(*qkv_b[dil], *d_b[dil], None, name=f"attn_b{dil}_bwd", **cfg)
              for dil, cfg in b_cfgs.items()}
    dqc, dmk, dmv = _cross_bwd(qc, mk, mv, *d_c)
    d_wmem, st_mem = _memkv_bwd(mem, mem_g, mn, w_mem_full, dmk, dmv)

    grad_x, dproj, st_pre = _inproj_bwd_x(x, dh, pre_g, w_in_full, tabs, dqa, dka, dva, dqkv_b, dqc, d_gate)
    d_win = _inproj_bwd_w(u, dproj)

    dsink_row = jnp.pad(dsink[0:1, :], ((0, 0), (0, D_MODEL - LANES)))
    stats = jnp.concatenate([st_pre[0:1], st_mem[0:1], st_mid[0:1], dsink_row, st_mid[1:2],
                             jnp.zeros((3, D_MODEL), F32)], axis=0)
    return grad_x, d_win, d_wmem, d_wout, stats


def _mesh_pos():
    return lax.axis_index("x"), lax.axis_index("y"), lax.axis_index("c")


def _all_gather_weights(w_in, w_mem, w_out):
    def body(win_ref, wm_ref, wo_ref, win_out, wm_out, wo_out, win_b, wm_b, wo_b, send_sems, recv_sems, local_sems):
        x, y, c = _mesh_pos()
        win_b[...] = win_ref[...].astype(BF16)
        wm_b[...] = wm_ref[...].astype(BF16)
        wo_b[...] = wo_ref[...].astype(BF16)
        srcs = (win_b, wm_b, wo_b)
        outs = (win_out, wm_out, wo_out)
        me, sibling = (x, y, c), (x, y, 1 - c)
        chips = [(1 - x, y), (x, 1 - y), (1 - x, 1 - y)]

        def slot(a, p):
            return outs[a].at[4 * p[0] + 2 * p[1] + p[2]]

        def copy(a, k, block, to, src=None):
            return pltpu.make_async_remote_copy(
                src_ref=slot(a, block) if src is None else src, dst_ref=slot(a, block),
                send_sem=send_sems.at[a, k], recv_sem=recv_sems.at[a, k], device_id=to, device_id_type=MESH_ID)

        mine = [pltpu.make_async_copy(srcs[a], slot(a, me), local_sems.at[a]) for a in range(3)]
        for cp in mine:
            cp.start()
        first = []
        for a in range(3):
            first.append(copy(a, 0, me, sibling, src=srcs[a]))
            first += [copy(a, 1 + j, me, (*chip, c), src=srcs[a]) for j, chip in enumerate(chips)]
        for cp in first:
            cp.start()
        passed = []
        for j, chip in enumerate(chips):
            for a in range(3):
                copy(a, 1 + j, (*chip, c), me).wait_recv()
                fwd = copy(a, 4 + j, (*chip, c), sibling)
                fwd.start()
                passed.append(fwd)
        for a in range(3):
            copy(a, 0, sibling, me).wait_recv()
            for j, chip in enumerate(chips):
                copy(a, 4 + j, (*chip, 1 - c), me).wait_recv()
        for cp in first + passed:
            cp.wait_send()
        for cp in mine:
            cp.wait()

    shapes = [(D_MODEL, SHARD_IN), (SHARD_ROWS, 2 * C_W), (SHARD_ROWS, D_MODEL)]
    vmem = pl.BlockSpec(memory_space=pltpu.VMEM)
    hbm = pl.BlockSpec(memory_space=pl.ANY)
    return pl.pallas_call(
        body, name="all_gather_weights",
        in_specs=[vmem] * 3, out_specs=[hbm] * 3,
        out_shape=[jax.ShapeDtypeStruct((N_DEV,) + s, BF16) for s in shapes],
        scratch_shapes=[pltpu.VMEM(s, BF16) for s in shapes]
        + [pltpu.SemaphoreType.DMA((3, 7)), pltpu.SemaphoreType.DMA((3, 7)), pltpu.SemaphoreType.DMA((3,))],
        compiler_params=_params(),
    )(w_in, w_mem, w_out)


def _exchange_grads(d_win, d_wmem, d_wout, stats):
    shapes = [d_win.shape[1:], d_wmem.shape[1:], d_wout.shape[1:]]
    n_arr = len(shapes)

    def body(win, wm, wo, st, g_win, g_wm, g_wo, r_st, *scratch):
        mine, got, snd, rcv = (scratch[n_arr * t:n_arr * (t + 1)] for t in range(4))
        load_sems, d2d_send, d2d_recv, ici_send, ici_recv, st_send, st_recv, st_local = scratch[4 * n_arr:]
        x, y, c = _mesh_pos()
        me = 4 * x + 2 * y + c
        ins, outs = (win, wm, wo), (g_win, g_wm, g_wo)

        def chip(kk):
            return (1 - x if kk & 2 else x, 1 - y if kk & 1 else y)

        st_own = pltpu.make_async_copy(st, r_st.at[me], st_local)
        st_own.start()
        st_copies = []
        for s in range(1, N_DEV):
            peer = (1 - x if s & 4 else x, 1 - y if s & 2 else y, 1 - c if s & 1 else c)
            pid = 4 * peer[0] + 2 * peer[1] + peer[2]
            cp = pltpu.make_async_remote_copy(src_ref=st, dst_ref=r_st.at[me], send_sem=st_send.at[s],
                                              recv_sem=st_recv.at[s], device_id=peer, device_id_type=MESH_ID)
            cp.start()
            arrive = pltpu.make_async_remote_copy(src_ref=st, dst_ref=r_st.at[pid], send_sem=st_send.at[s],
                                                  recv_sem=st_recv.at[s], device_id=peer, device_id_type=MESH_ID)
            st_copies.append((cp, arrive))

        loads, swaps = {}, {}
        for kk in range(4):
            ox, oy = chip(kk)
            for a in range(n_arr):
                ld = pltpu.make_async_copy(ins[a].at[4 * ox + 2 * oy + c], mine[a].at[kk], load_sems.at[a, kk])
                ld.start()
                sw = pltpu.make_async_remote_copy(
                    src_ref=ins[a].at[4 * ox + 2 * oy + (1 - c)], dst_ref=got[a].at[kk],
                    send_sem=d2d_send.at[a, kk], recv_sem=d2d_recv.at[a, kk],
                    device_id=(x, y, 1 - c), device_id_type=MESH_ID)
                sw.start()
                loads[a, kk], swaps[a, kk] = ld, sw

        hops = {}
        for kk in (3, 1, 2):
            ox, oy = chip(kk)
            for a in range(n_arr):
                loads[a, kk].wait()
                swaps[a, kk].wait_recv()
                snd[a][kk - 1] = (mine[a][kk] + got[a][kk]).astype(BF16)
                hop = pltpu.make_async_remote_copy(
                    src_ref=snd[a].at[kk - 1], dst_ref=rcv[a].at[kk - 1],
                    send_sem=ici_send.at[a, kk], recv_sem=ici_recv.at[a, kk],
                    device_id=(ox, oy, c), device_id_type=MESH_ID)
                hop.start()
                hops[a, kk] = hop

        for a in range(n_arr):
            loads[a, 0].wait()
            swaps[a, 0].wait_recv()
            acc = mine[a][0] + got[a][0]
            for kk in (1, 2, 3):
                hops[a, kk].wait_recv()
                acc = acc + rcv[a][kk - 1].astype(F32)
            outs[a][...] = acc

        for sw in swaps.values():
            sw.wait_send()
        for hop in hops.values():
            hop.wait_send()
        for cp, arrive in st_copies:
            arrive.wait_recv()
            cp.wait_send()
        st_own.wait()

    hbm = pl.BlockSpec(memory_space=pl.ANY)
    vmem = pl.BlockSpec(memory_space=pltpu.VMEM)
    out_shape = [jax.ShapeDtypeStruct(s, F32) for s in shapes] + [jax.ShapeDtypeStruct((N_DEV,) + stats.shape, F32)]
    scratch = ([pltpu.VMEM((4,) + s, F32) for s in shapes] + [pltpu.VMEM((4,) + s, F32) for s in shapes]
               + [pltpu.VMEM((3,) + s, BF16) for s in shapes] + [pltpu.VMEM((3,) + s, BF16) for s in shapes]
               + [pltpu.SemaphoreType.DMA((n_arr, 4))] * 5
               + [pltpu.SemaphoreType.DMA((N_DEV,)), pltpu.SemaphoreType.DMA((N_DEV,)), pltpu.SemaphoreType.DMA(())])
    return pl.pallas_call(
        body, name="exchange_grads", in_specs=[hbm] * 4, out_specs=[vmem] * n_arr + [hbm], out_shape=out_shape,
        scratch_shapes=scratch, compiler_params=_params(),
    )(d_win, d_wmem, d_wout, stats)


def _reduce_adamw(recv, w, m, v, *, tr, name):
    n_part, rows, cols = recv.shape

    def body(r_ref, w_ref, m_ref, v_ref, g_ref, d_ref, nm_ref, nv_ref):
        g = r_ref[0]
        for s in range(1, n_part):
            g = g + r_ref[s]
        g_ref[...] = g
        m2 = ADAM_B1 * m_ref[...] + (1.0 - ADAM_B1) * g
        v2 = ADAM_B2 * v_ref[...] + (1.0 - ADAM_B2) * (g * g)
        nm_ref[...] = m2
        nv_ref[...] = v2
        m_hat = m2 / (1.0 - ADAM_B1 ** ADAM_STEP)
        v_hat = v2 / (1.0 - ADAM_B2 ** ADAM_STEP)
        d_ref[...] = -ADAM_LR * (m_hat / (jnp.sqrt(v_hat) + ADAM_EPS) + ADAM_WD * w_ref[...])

    blk = pl.BlockSpec((tr, cols), lambda i: (i, 0))
    return pl.pallas_call(
        body, name=name, grid=(rows // tr,),
        in_specs=[pl.BlockSpec((n_part, tr, cols), lambda i: (0, i, 0)), blk, blk, blk],
        out_specs=[blk] * 4, out_shape=[jax.ShapeDtypeStruct((rows, cols), F32)] * 4,
        compiler_params=_params(dimension_semantics=("arbitrary",)),
    )(recv, w, m, v)


def _pack_rows(pre, memn, post, sink):
    sink_row = jnp.pad(sink, ((0, 0), (0, D_MODEL - A_HEADS)))
    return jnp.concatenate([pre, memn, post, sink_row, jnp.zeros((4, D_MODEL), F32)], axis=0)


def kernel(x, mem, pre_norm, w_in, sink_a, mem_norm, w_mem_kv, w_out, post_norm, loss_target, m_pre_norm, m_w_in, m_sink_a, m_mem_norm, m_w_mem_kv, m_w_out, m_post_norm, v_pre_norm, v_w_in, v_sink_a, v_mem_norm, v_w_mem_kv, v_w_out, v_post_norm):
    w_in_full, w_mem_full, w_out_full = _all_gather_weights(w_in[0], w_mem_kv[0], w_out[0])
    sink = jnp.pad(sink_a[0], (0, 8 - A_HEADS))
    grad_x, d_win, d_wmem, d_wout, stats = _local_step(
        x[0], mem[0], pre_norm, w_in_full, sink, mem_norm, w_mem_full.reshape(D_MODEL, 2 * C_W),
        w_out_full.reshape(D_MODEL, D_MODEL), post_norm, loss_target[0])
    g_win, g_wmem, g_wout, r_stats = _exchange_grads(
        d_win, d_wmem.reshape(N_DEV, SHARD_ROWS, 2 * C_W), d_wout.reshape(N_DEV, SHARD_ROWS, D_MODEL), stats)

    big = {}
    for nm, g, w, m, v in (("w_in", g_win, w_in, m_w_in, v_w_in),
                           ("w_mem_kv", g_wmem, w_mem_kv, m_w_mem_kv, v_w_mem_kv),
                           ("w_out", g_wout, w_out, m_w_out, v_w_out)):
        res = _reduce_adamw(g[None], w[0], m[0], v[0], tr=SHARD_ROWS, name="adamw_" + nm)
        big[nm] = [t[None] for t in res]
    small = _reduce_adamw(
        r_stats, _pack_rows(pre_norm, mem_norm, post_norm, sink_a),
        _pack_rows(m_pre_norm, m_mem_norm, m_post_norm, m_sink_a),
        _pack_rows(v_pre_norm, v_mem_norm, v_post_norm, v_sink_a), tr=8, name="adamw_small")

    def unpack(t):
        return {"pre_norm": t[0:1], "mem_norm": t[1:2], "post_norm": t[2:3], "sink_a": t[3:4, 0:A_HEADS]}

    order = ("pre_norm", "w_in", "sink_a", "mem_norm", "w_mem_kv", "w_out", "post_norm")
    outs = [small[0][4, 0], grad_x[None]]
    for j in range(4):
        sm = unpack(small[j])
        outs += [big[n][j] if n in big else sm[n] for n in order]
    return tuple(outs)
```

```python
import jax
import jax.numpy as jnp
from jax import lax
from jax.experimental import pallas as pl
from jax.experimental.pallas import tpu as pltpu

F32 = jnp.float32
BF16 = jnp.bfloat16

D_MODEL = 1024
HEAD_DIM = 64
ROT_DIM = 16
ROPE_THETA = 500000.0
BLOCK = 128
LANES = 128
N_MEM = 256
RMS_EPS = 1e-6
SCALE = HEAD_DIM ** -0.5
A_HEADS, A_GROUP = 6, 3
B_HEADS = 6
C_HEADS = 4
A_W, A_KV_W, B_W, C_W = 384, 128, 384, 256
D_IN = 3072
N_DEV = 8
SHARD_IN = D_IN // N_DEV
SHARD_ROWS = D_MODEL // N_DEV
B_CONFIGS = ((128, 1), (512, 4), (2048, 16))
DILS = (4, 16)
NEG = -1e30
ATTN_BLOCKS_PER_STEP = 4
DELTA_LANE = 64
VMEM_LIMIT = 56 * 1024 * 1024

ADAM_LR, ADAM_B1, ADAM_B2, ADAM_EPS, ADAM_WD, ADAM_STEP = 0.001, 0.9, 0.999, 1e-08, 0.01, 10
MESH_ID = pl.DeviceIdType.MESH


def _params(**kw):
    return pltpu.CompilerParams(vmem_limit_bytes=VMEM_LIMIT, **kw)


def _full(shape):
    n = len(shape)
    return pl.BlockSpec(shape, lambda *_: (0,) * n)


def _row(tm, w):
    return pl.BlockSpec((tm, w), lambda i: (i, 0))


def _perm_view(a, dil):
    return a.reshape(a.shape[0] // (BLOCK * dil), dil, BLOCK, a.shape[1])


def _perm_spec(tm, dil, w):
    per = BLOCK * dil // tm
    return pl.BlockSpec((1, dil, tm // dil, w), lambda i: (i // per, 0, i % per, 0))


def _put(scr, val):
    for c in range(val.shape[1] // LANES):
        scr[c] = val[:, LANES * c:LANES * (c + 1)]


def _get(scr):
    n = scr.shape[0]
    return scr[0] if n == 1 else jnp.concatenate([scr[c] for c in range(n)], axis=1)


def _get_class(scr, r, dil):
    n, rows = scr.shape[0], scr.shape[1]
    parts = [scr.at[c][pl.ds(r, rows // dil, stride=dil), :] for c in range(n)]
    return parts[0] if n == 1 else jnp.concatenate(parts, axis=1)


def _store_permuted(scr, out_ref, dil, dtype):
    for r in range(dil):
        out_ref[0, r] = _get_class(scr, r, dil).astype(dtype)


def _load_permuted(in_ref, scr, dil):
    n, rows = scr.shape[0], scr.shape[1]
    for r in range(dil):
        val = in_ref[0, r].astype(F32)
        for c in range(n):
            scr.at[c][pl.ds(r, rows // dil, stride=dil), :] = val[:, LANES * c:LANES * (c + 1)]
    return _get(scr)


def _rope_tables(seq):
    j = jnp.arange(LANES) % HEAD_DIM
    freq = ROPE_THETA ** (-(2 * (j % (ROT_DIM // 2))).astype(F32) / ROT_DIM)
    ang = jnp.arange(seq, dtype=F32)[:, None] * freq[None, :]
    cos, sin = jnp.cos(ang), jnp.sin(ang)
    half = ROT_DIM // 2
    c = jnp.where(j < ROT_DIM, cos, 1.0)
    up = jnp.where((j >= half) & (j < ROT_DIM), sin, 0.0)
    dn = jnp.where(j < half, -sin, 0.0)
    return c, up, dn


def _rotate128(t, c, up, dn):
    return t * c + pltpu.roll(t, 8, 1) * up + pltpu.roll(t, LANES - 8, 1) * dn


def _rotate(t, c, up, dn):
    outs = [_rotate128(t[:, LANES * j:LANES * (j + 1)], c, up, dn) for j in range(t.shape[1] // LANES)]
    return outs[0] if len(outs) == 1 else jnp.concatenate(outs, axis=1)


def _inproj(x, pre_g, w_in_full, tabs, tm=512):
    seq = x.shape[0]
    n_chunk = D_IN // LANES

    def body(x_ref, g_ref, w_ref, c_ref, up_ref, dn_ref, u_ref, qa_ref, ka_ref, va_ref,
             qb1_ref, kb1_ref, vb1_ref, qb4_ref, kb4_ref, vb4_ref, qb16_ref, kb16_ref, vb16_ref,
             qc_ref, gate_ref, proj):
        xv = x_ref[...]
        r = lax.rsqrt(jnp.mean(xv * xv, axis=-1, keepdims=True) + RMS_EPS)
        u = ((xv * r) * g_ref[...]).astype(BF16)
        u_ref[...] = u
        per = SHARD_IN // LANES
        for k in range(N_DEV):
            acc = jnp.dot(u, w_ref[k], preferred_element_type=F32)
            for c3 in range(per):
                proj[per * k + c3] = acc[:, LANES * c3:LANES * (c3 + 1)]
        c, up, dn = c_ref[...], up_ref[...], dn_ref[...]

        def cols(lo, hi, rot=False, scale=None):
            parts = []
            for ch in range(lo // LANES, hi // LANES):
                t = proj[ch]
                if rot:
                    t = _rotate128(t, c, up, dn)
                if scale is not None:
                    t = t * scale
                parts.append(t)
            return parts[0] if len(parts) == 1 else jnp.concatenate(parts, axis=1)

        qa_ref[...] = cols(0, 384, True, SCALE).astype(BF16)
        ka_ref[...] = cols(384, 512, True).astype(BF16)
        va_ref[...] = cols(512, 640).astype(BF16)
        gate_ref[:, 0:384] = cols(640, 1024).astype(BF16)
        gate_ref[:, 384:768] = cols(2176, 2560).astype(BF16)
        gate_ref[:, 768:1024] = cols(2816, 3072).astype(BF16)
        qc_ref[...] = cols(2560, 2816, False, SCALE).astype(BF16)
        for ch in range(1024 // LANES, 1408 // LANES):
            proj[ch] = _rotate128(proj[ch], c, up, dn) * SCALE
        for ch in range(1408 // LANES, 1792 // LANES):
            proj[ch] = _rotate128(proj[ch], c, up, dn)
        for lo, nat, p4, p16 in ((1024, qb1_ref, qb4_ref, qb16_ref), (1408, kb1_ref, kb4_ref, kb16_ref),
                                 (1792, vb1_ref, vb4_ref, vb16_ref)):
            chunks = range(lo // LANES, lo // LANES + B_W // LANES)
            nat[...] = jnp.concatenate([proj[ch] for ch in chunks], axis=1).astype(BF16)
            for dil, ref in ((4, p4), (16, p16)):
                for rr in range(dil):
                    ref[0, rr] = jnp.concatenate(
                        [proj.at[ch][pl.ds(rr, tm // dil, stride=dil), :] for ch in chunks], axis=1).astype(BF16)

    nat_w = (D_MODEL, A_W, A_KV_W, A_KV_W, B_W, B_W, B_W)
    out_specs = [_row(tm, w) for w in nat_w]
    out_shape = [jax.ShapeDtypeStruct((seq, w), BF16) for w in nat_w]
    for dil in DILS:
        out_specs += [_perm_spec(tm, dil, B_W)] * 3
        out_shape += [jax.ShapeDtypeStruct((seq // (BLOCK * dil), dil, BLOCK, B_W), BF16)] * 3
    out_specs += [_row(tm, C_W), _row(tm, D_MODEL)]
    out_shape += [jax.ShapeDtypeStruct((seq, C_W), BF16), jax.ShapeDtypeStruct((seq, D_MODEL), BF16)]
    res = pl.pallas_call(
        body, name="inproj", grid=(seq // tm,),
        in_specs=[_row(tm, D_MODEL), _full((1, D_MODEL)), _full((N_DEV, D_MODEL, SHARD_IN)),
                  _row(tm, LANES), _row(tm, LANES), _row(tm, LANES)],
        out_specs=out_specs, out_shape=out_shape,
        scratch_shapes=[pltpu.VMEM((n_chunk, tm, LANES), F32)],
        compiler_params=_params(dimension_semantics=("arbitrary",)),
    )(x, pre_g, w_in_full, *tabs)
    u, qa, ka, va = res[0:4]
    qkv_b = {1: res[4:7], 4: [t.reshape(seq, B_W) for t in res[7:10]], 16: [t.reshape(seq, B_W) for t in res[10:13]]}
    return u, qa, ka, va, qkv_b, res[13], res[14]


def _memkv_fwd(mem, mem_g, w_mem_full):
    def body(mem_ref, g_ref, w_ref, mn_ref, mk_ref, mv_ref):
        mv_ = mem_ref[...]
        r = lax.rsqrt(jnp.mean(mv_ * mv_, axis=-1, keepdims=True) + RMS_EPS)
        mn = ((mv_ * r) * g_ref[...]).astype(BF16)
        mn_ref[...] = mn
        mkv = jnp.dot(mn, w_ref[...], preferred_element_type=F32)
        mk_ref[...] = mkv[:, 0:C_W].astype(BF16)
        mv_ref[...] = mkv[:, C_W:2 * C_W].astype(BF16)

    return pl.pallas_call(
        body, name="memkv_fwd",
        out_shape=[jax.ShapeDtypeStruct((N_MEM, D_MODEL), BF16),
                   jax.ShapeDtypeStruct((N_MEM, C_W), BF16), jax.ShapeDtypeStruct((N_MEM, C_W), BF16)],
        compiler_params=_params(),
    )(mem, mem_g, w_mem_full)


def _memkv_bwd(mem, mem_g, mn, w_mem_full, dmk, dmv):
    def body(mem_ref, g_ref, mn_ref, w_ref, dmk_ref, dmv_ref, dw_ref, st_ref):
        dmkv = jnp.concatenate([dmk_ref[...], dmv_ref[...]], axis=1).astype(BF16)
        dw_ref[...] = lax.dot_general(mn_ref[...], dmkv, (((0,), (0,)), ((), ())), preferred_element_type=F32)
        dmn = lax.dot_general(dmkv, w_ref[...], (((1,), (1,)), ((), ())), preferred_element_type=F32)
        mv_ = mem_ref[...]
        r = lax.rsqrt(jnp.mean(mv_ * mv_, axis=-1, keepdims=True) + RMS_EPS)
        st_ref[...] = jnp.zeros_like(st_ref)
        st_ref[0:1, :] = jnp.sum(dmn * (mv_ * r), axis=0, keepdims=True)

    return pl.pallas_call(
        body, name="memkv_bwd",
        out_shape=[jax.ShapeDtypeStruct((D_MODEL, 2 * C_W), F32), jax.ShapeDtypeStruct((8, D_MODEL), F32)],
        compiler_params=_params(),
    )(mem, mem_g, mn, w_mem_full, dmk, dmv)


def _band_mask(has_prev, max_dist):
    qi = lax.broadcasted_iota(jnp.int32, (BLOCK, 2 * BLOCK), 0)
    kj = lax.broadcasted_iota(jnp.int32, (BLOCK, 2 * BLOCK), 1)
    dist = qi + BLOCK - kj
    return (dist >= 0) & (dist <= max_dist) & ((kj >= BLOCK) | has_prev)


_NT = (((1,), (1,)), ((), ()))
_TN = (((0,), (0,)), ((), ()))


def _head_only(val, h):
    slab = val[:, LANES * (h // 2):LANES * (h // 2 + 1)]
    lane = lax.broadcasted_iota(jnp.int32, slab.shape, 1)
    keep = (lane < HEAD_DIM) if h % 2 == 0 else (lane >= HEAD_DIM)
    return jnp.where(keep, slab, jnp.zeros((), slab.dtype))


class _KvSlabs:
    def __init__(self, cat, group):
        self.cat, self.group, self.swapped = cat, group, {}

    def is_swapped(self, h):
        return (h // self.group) % 2 != h % 2

    def __call__(self, h):
        j = (h // self.group) // 2
        slab = self.cat[:, LANES * j:LANES * (j + 1)]
        if not self.is_swapped(h):
            return slab
        if j not in self.swapped:
            self.swapped[j] = jnp.concatenate([slab[:, HEAD_DIM:], slab[:, :HEAD_DIM]], axis=1)
        return self.swapped[j]


class _BandSteps:
    def __init__(self, seq, dil, nq):
        self.nq, self.rows, self.consecutive = nq, nq * BLOCK, dil == 1
        nb = seq // dil // BLOCK
        if self.consecutive:
            assert nb % nq == 0
            self.outer, self.inner, self.stride = 1, nb // nq, 1
        else:
            assert dil % nq == 0
            self.outer, self.inner, self.stride = dil // nq, nb, dil // nq

    def own(self, w, clamp=False):
        cur = (lambda i: jnp.minimum(i, self.inner - 1)) if clamp else (lambda i: i)
        return pl.BlockSpec((self.rows, w), lambda r, i: (cur(i) * self.stride + r, 0))

    def prev(self, w, clamp=False):
        cur = (lambda i: jnp.minimum(i, self.inner - 1)) if clamp else (lambda i: i)
        if self.consecutive:
            return pl.BlockSpec((BLOCK, w), lambda r, i: (jnp.maximum(cur(i) * self.nq - 1, 0), 0))
        return pl.BlockSpec((self.rows, w), lambda r, i: (jnp.maximum(cur(i) - 1, 0) * self.stride + r, 0))

    def late(self, w):
        return pl.BlockSpec((self.rows, w), lambda r, i: (jnp.maximum(i - 1, 0) * self.stride + r, 0))

    def rows_of(self, j):
        return slice(BLOCK * j, BLOCK * (j + 1))

    def keys(self, p_ref, c_ref, j):
        if not self.consecutive:
            before = p_ref[self.rows_of(j), :]
        elif j == 0:
            before = p_ref[...]
        else:
            before = c_ref[self.rows_of(j - 1), :]
        return jnp.concatenate([before, c_ref[self.rows_of(j), :]], axis=0)

    def has_prev(self, i, j):
        return True if (self.consecutive and j > 0) else (i > 0)


def _banded_fwd(q, k, v, sink, *, dil, heads, group, max_dist, nq, name):
    seq = q.shape[0]
    kvh = heads // group
    qw, kw = heads * HEAD_DIM, kvh * HEAD_DIM
    steps = _BandSteps(seq, dil, nq)

    def body(*refs):
        if sink is not None:
            sink_ref, refs = refs[0], refs[1:]
        q_ref, kp_ref, kc_ref, vp_ref, vc_ref, o_ref, lse_ref, s_scr, p_scr = refs
        i = pl.program_id(1)
        lane = lax.broadcasted_iota(jnp.int32, (BLOCK, LANES), 1)
        k_of = [_KvSlabs(steps.keys(kp_ref, kc_ref, j), group) for j in range(nq)]
        v_of = [_KvSlabs(steps.keys(vp_ref, vc_ref, j), group) for j in range(nq)]
        for j in range(nq):
            qv = q_ref[steps.rows_of(j), :]
            for h in range(heads):
                s_scr[j * heads + h] = lax.dot_general(_head_only(qv, h), k_of[j](h), _NT, preferred_element_type=F32)
        ls = {}
        for j in range(nq):
            valid = _band_mask(steps.has_prev(i, j), max_dist)
            lse_tile = jnp.zeros((BLOCK, LANES), F32)
            for h in range(heads):
                s = jnp.where(valid, s_scr[j * heads + h], NEG)
                m = jnp.max(s, axis=-1, keepdims=True)
                if sink is not None:
                    sk = sink_ref[h]
                    m = jnp.maximum(m, sk)
                p = jnp.exp(s - m)
                l = jnp.sum(p, axis=-1, keepdims=True)
                if sink is not None:
                    l = l + jnp.exp(sk - m)
                p_scr[j * heads + h] = p.astype(BF16)
                ls[j, h] = l
                lse_tile = jnp.where(lane == h, m + jnp.log(l), lse_tile)
            lse_ref[steps.rows_of(j), :] = lse_tile
        for j in range(nq):
            for pr in range(heads // 2):
                he, ho = 2 * pr, 2 * pr + 1
                even = jnp.dot(p_scr[j * heads + he], v_of[j](he), preferred_element_type=F32) / ls[j, he]
                odd = jnp.dot(p_scr[j * heads + ho], v_of[j](ho), preferred_element_type=F32) / ls[j, ho]
                o_ref[steps.rows_of(j), LANES * pr:LANES * (pr + 1)] = jnp.where(lane < HEAD_DIM, even, odd).astype(BF16)

    in_specs = [steps.own(qw), steps.prev(kw), steps.own(kw), steps.prev(kw), steps.own(kw)]
    args = [q, k, k, v, v]
    if sink is not None:
        in_specs = [pl.BlockSpec(memory_space=pltpu.SMEM)] + in_specs
        args = [sink] + args
    return pl.pallas_call(
        body, name=name, grid=(steps.outer, steps.inner), in_specs=in_specs,
        out_specs=[steps.own(qw), steps.own(LANES)],
        out_shape=[jax.ShapeDtypeStruct((seq, qw), BF16), jax.ShapeDtypeStruct((seq, LANES), F32)],
        scratch_shapes=[pltpu.VMEM((nq * heads, BLOCK, 2 * BLOCK), F32), pltpu.VMEM((nq * heads, BLOCK, 2 * BLOCK), BF16)],
        compiler_params=_params(dimension_semantics=("arbitrary", "arbitrary")),
    )(*args)


def _banded_bwd(q, k, v, d_out, stat, sink, *, dil, heads, group, max_dist, nq, name):
    seq = q.shape[0]
    kvh = heads // group
    qw, kw = heads * HEAD_DIM, kvh * HEAD_DIM
    steps = _BandSteps(seq, dil, nq)
    n_in = 7

    def body(*refs):
        if sink is not None:
            sink_ref, refs = refs[0], refs[1:]
            dsink_ref, refs = refs[n_in], refs[:n_in] + refs[n_in + 1:]
        (q_ref, kp_ref, kc_ref, vp_ref, vc_ref, do_ref, st_ref, dq_ref, dk_ref, dv_ref,
         kcar, vcar, s_scr, dp_scr, p_scr, ds_scr) = refs
        r, i = pl.program_id(0), pl.program_id(1)

        @pl.when(i == 0)
        def _():
            kcar[...] = jnp.zeros_like(kcar)
            vcar[...] = jnp.zeros_like(vcar)

        if sink is not None:
            @pl.when((i == 0) & (r == 0))
            def _():
                dsink_ref[...] = jnp.zeros_like(dsink_ref)

        @pl.when(i < steps.inner)
        def _():
            lane = lax.broadcasted_iota(jnp.int32, (1, LANES), 1)
            lane_q = lax.broadcasted_iota(jnp.int32, (BLOCK, LANES), 1)
            k_of = [_KvSlabs(steps.keys(kp_ref, kc_ref, j), group) for j in range(nq)]
            v_of = [_KvSlabs(steps.keys(vp_ref, vc_ref, j), group) for j in range(nq)]
            qms, doms = {}, {}
            for j in range(nq):
                qv, dov = q_ref[steps.rows_of(j), :], do_ref[steps.rows_of(j), :]
                for h in range(heads):
                    qms[j, h], doms[j, h] = _head_only(qv, h), _head_only(dov, h)
                    s_scr[j * heads + h] = lax.dot_general(qms[j, h], k_of[j](h), _NT, preferred_element_type=F32)
                    dp_scr[j * heads + h] = lax.dot_general(doms[j, h], v_of[j](h), _NT, preferred_element_type=F32)
            dsink_row = jnp.zeros((1, LANES), F32)
            for j in range(nq):
                st = st_ref[steps.rows_of(j), :]
                valid = _band_mask(steps.has_prev(i, j), max_dist)
                for h in range(heads):
                    lse_h = st[:, h:h + 1]
                    delta = st[:, DELTA_LANE + h:DELTA_LANE + h + 1]
                    p = jnp.where(valid, jnp.exp(s_scr[j * heads + h] - lse_h), 0.0)
                    p_scr[j * heads + h] = p.astype(BF16)
                    ds_scr[j * heads + h] = (p * (dp_scr[j * heads + h] - delta)).astype(BF16)
                    if sink is not None:
                        ds_sink = jnp.sum(-jnp.exp(sink_ref[h] - lse_h) * delta, axis=0, keepdims=True)
                        dsink_row = dsink_row + jnp.where(lane == h, ds_sink, 0.0)
            for j in range(nq):
                for pr in range(heads // 2):
                    he, ho = 2 * pr, 2 * pr + 1
                    even = jnp.dot(ds_scr[j * heads + he], k_of[j](he), preferred_element_type=F32)
                    odd = jnp.dot(ds_scr[j * heads + ho], k_of[j](ho), preferred_element_type=F32)
                    dq_ref[steps.rows_of(j), LANES * pr:LANES * (pr + 1)] = (
                        jnp.where(lane_q < HEAD_DIM, even, odd).astype(BF16))
            if steps.consecutive:
                dk_ref[...] = kcar[...].astype(BF16)
                dv_ref[...] = vcar[...].astype(BF16)
            for j in range(nq):
                for slab in range(kw // LANES):
                    acc = {}
                    for h in range(heads):
                        if (h // group) // 2 != slab:
                            continue
                        key = k_of[j].is_swapped(h)
                        dk_h = lax.dot_general(ds_scr[j * heads + h], qms[j, h], _TN, preferred_element_type=F32)
                        dv_h = lax.dot_general(p_scr[j * heads + h], doms[j, h], _TN, preferred_element_type=F32)
                        acc[key] = (dk_h, dv_h) if key not in acc else (acc[key][0] + dk_h, acc[key][1] + dv_h)
                    dk_j, dv_j = acc.get(False, (None, None))
                    if True in acc:
                        unswap = lambda t: jnp.concatenate([t[:, HEAD_DIM:], t[:, :HEAD_DIM]], axis=1)
                        dk_s, dv_s = unswap(acc[True][0]), unswap(acc[True][1])
                        dk_j = dk_s if dk_j is None else dk_j + dk_s
                        dv_j = dv_s if dv_j is None else dv_j + dv_s
                    sl = slice(LANES * slab, LANES * (slab + 1))
                    own_rows = steps.rows_of(j)
                    if not steps.consecutive:
                        dk_ref[own_rows, sl] = (kcar[own_rows, sl] + dk_j[0:BLOCK]).astype(BF16)
                        dv_ref[own_rows, sl] = (vcar[own_rows, sl] + dv_j[0:BLOCK]).astype(BF16)
                    elif j == 0:
                        last = steps.rows_of(nq - 1)
                        dk_ref[last, sl] = (kcar[last, sl] + dk_j[0:BLOCK]).astype(BF16)
                        dv_ref[last, sl] = (vcar[last, sl] + dv_j[0:BLOCK]).astype(BF16)
                    else:
                        before = steps.rows_of(j - 1)
                        kcar[before, sl] += dk_j[0:BLOCK]
                        vcar[before, sl] += dv_j[0:BLOCK]
                    kcar[own_rows, sl] = dk_j[BLOCK:2 * BLOCK]
                    vcar[own_rows, sl] = dv_j[BLOCK:2 * BLOCK]
            if sink is not None:
                dsink_ref[0:1, :] += dsink_row

        @pl.when(i == steps.inner)
        def _():
            dk_ref[...] = kcar[...].astype(BF16)
            dv_ref[...] = vcar[...].astype(BF16)

    own, prev = (lambda w: steps.own(w, clamp=True)), (lambda w: steps.prev(w, clamp=True))
    in_specs = [own(qw), prev(kw), own(kw), prev(kw), own(kw), own(qw), own(LANES)]
    args = [q, k, k, v, v, d_out, stat]
    out_specs = [own(qw), steps.late(kw), steps.late(kw)]
    out_shape = [jax.ShapeDtypeStruct((seq, qw), BF16), jax.ShapeDtypeStruct((seq, kw), BF16),
                 jax.ShapeDtypeStruct((seq, kw), BF16)]
    if sink is not None:
        in_specs = [pl.BlockSpec(memory_space=pltpu.SMEM)] + in_specs
        args = [sink] + args
        out_specs = [_full((8, LANES))] + out_specs
        out_shape = [jax.ShapeDtypeStruct((8, LANES), F32)] + out_shape
    n_hb = nq * heads
    res = pl.pallas_call(
        body, name=name, grid=(steps.outer, steps.inner + 1), in_specs=in_specs, out_specs=out_specs,
        out_shape=out_shape,
        scratch_shapes=[pltpu.VMEM((steps.rows, kw), F32), pltpu.VMEM((steps.rows, kw), F32)]
        + [pltpu.VMEM((n_hb, BLOCK, 2 * BLOCK), F32)] * 2 + [pltpu.VMEM((n_hb, BLOCK, 2 * BLOCK), BF16)] * 2,
        compiler_params=_params(dimension_semantics=("arbitrary", "arbitrary")),
    )(*args)
    if sink is not None:
        return res[1], res[2], res[3], res[0]
    return res


def _cross_fwd(q, mk, mv, tq=512):
    seq = q.shape[0]

    def body(q_ref, mk_ref, mv_ref, o_ref, lse_ref, s_scr, p_scr):
        qv = q_ref[...]
        k_of, v_of = _KvSlabs(mk_ref[...], 1), _KvSlabs(mv_ref[...], 1)
        lane = lax.broadcasted_iota(jnp.int32, (tq, LANES), 1)
        lse_tile = jnp.zeros((tq, LANES), F32)
        for h in range(C_HEADS):
            s_scr[h] = lax.dot_general(_head_only(qv, h), k_of(h), _NT, preferred_element_type=F32)
        ls = []
        for h in range(C_HEADS):
            s = s_scr[h]
            m = jnp.max(s, axis=-1, keepdims=True)
            p = jnp.exp(s - m)
            l = jnp.sum(p, axis=-1, keepdims=True)
            p_scr[h] = p.astype(BF16)
            ls.append(l)
            lse_tile = jnp.where(lane == h, m + jnp.log(l), lse_tile)
        for pr in range(C_HEADS // 2):
            even = jnp.dot(p_scr[2 * pr], v_of(2 * pr), preferred_element_type=F32) / ls[2 * pr]
            odd = jnp.dot(p_scr[2 * pr + 1], v_of(2 * pr + 1), preferred_element_type=F32) / ls[2 * pr + 1]
            o_ref[:, LANES * pr:LANES * (pr + 1)] = jnp.where(lane < HEAD_DIM, even, odd).astype(BF16)
        lse_ref[...] = lse_tile

    return pl.pallas_call(
        body, name="cross_fwd", grid=(seq // tq,),
        in_specs=[_row(tq, C_W), _full((N_MEM, C_W)), _full((N_MEM, C_W))],
        out_specs=[_row(tq, C_W), _row(tq, LANES)],
        out_shape=[jax.ShapeDtypeStruct((seq, C_W), BF16), jax.ShapeDtypeStruct((seq, LANES), F32)],
        scratch_shapes=[pltpu.VMEM((C_HEADS, tq, N_MEM), F32), pltpu.VMEM((C_HEADS, tq, N_MEM), BF16)],
        compiler_params=_params(dimension_semantics=("arbitrary",)),
    )(q, mk, mv)


def _cross_bwd(q, mk, mv, d_out, stat, tq=512):
    seq = q.shape[0]

    def body(q_ref, mk_ref, mv_ref, do_ref, st_ref, dq_ref, dmk_ref, dmv_ref, s_scr, dp_scr, p_scr, ds_scr):
        @pl.when(pl.program_id(0) == 0)
        def _():
            dmk_ref[...] = jnp.zeros_like(dmk_ref)
            dmv_ref[...] = jnp.zeros_like(dmv_ref)

        qv, dov, st = q_ref[...], do_ref[...], st_ref[...]
        k_of, v_of = _KvSlabs(mk_ref[...], 1), _KvSlabs(mv_ref[...], 1)
        qms = [_head_only(qv, h) for h in range(C_HEADS)]
        doms = [_head_only(dov, h) for h in range(C_HEADS)]
        for h in range(C_HEADS):
            s_scr[h] = lax.dot_general(qms[h], k_of(h), _NT, preferred_element_type=F32)
            dp_scr[h] = lax.dot_general(doms[h], v_of(h), _NT, preferred_element_type=F32)
        for h in range(C_HEADS):
            p = jnp.exp(s_scr[h] - st[:, h:h + 1])
            p_scr[h] = p.astype(BF16)
            ds_scr[h] = (p * (dp_scr[h] - st[:, DELTA_LANE + h:DELTA_LANE + h + 1])).astype(BF16)
        lane = lax.broadcasted_iota(jnp.int32, (tq, LANES), 1)
        for pr in range(C_HEADS // 2):
            sl = slice(LANES * pr, LANES * (pr + 1))
            even = jnp.dot(ds_scr[2 * pr], k_of(2 * pr), preferred_element_type=F32)
            odd = jnp.dot(ds_scr[2 * pr + 1], k_of(2 * pr + 1), preferred_element_type=F32)
            dq_ref[:, sl] = jnp.where(lane < HEAD_DIM, even, odd).astype(BF16)
            dmk_ref[:, sl] += (lax.dot_general(ds_scr[2 * pr], qms[2 * pr], _TN, preferred_element_type=F32)
                               + lax.dot_general(ds_scr[2 * pr + 1], qms[2 * pr + 1], _TN, preferred_element_type=F32))
            dmv_ref[:, sl] += (lax.dot_general(p_scr[2 * pr], doms[2 * pr], _TN, preferred_element_type=F32)
                               + lax.dot_general(p_scr[2 * pr + 1], doms[2 * pr + 1], _TN, preferred_element_type=F32))

    return pl.pallas_call(
        body, name="cross_bwd", grid=(seq // tq,),
        in_specs=[_row(tq, C_W), _full((N_MEM, C_W)), _full((N_MEM, C_W)), _row(tq, C_W), _row(tq, LANES)],
        out_specs=[_row(tq, C_W), _full((N_MEM, C_W)), _full((N_MEM, C_W))],
        out_shape=[jax.ShapeDtypeStruct((seq, C_W), BF16), jax.ShapeDtypeStruct((N_MEM, C_W), F32),
                   jax.ShapeDtypeStruct((N_MEM, C_W), F32)],
        scratch_shapes=[pltpu.VMEM((C_HEADS, tq, N_MEM), F32)] * 2 + [pltpu.VMEM((C_HEADS, tq, N_MEM), BF16)] * 2,
        compiler_params=_params(dimension_semantics=("arbitrary",)),
    )(q, mk, mv, d_out, stat)


def _per_head(tile, width):
    rows = tile.shape[0]
    return jnp.concatenate(
        [jnp.broadcast_to(tile[:, h:h + 1], (rows, HEAD_DIM)) for h in range(width // HEAD_DIM)], axis=1)


def _with_delta(lse_tile, prod):
    rows = lse_tile.shape[0]
    lane = lax.broadcasted_iota(jnp.int32, (rows, LANES), 1)
    tile = lse_tile
    for h in range(prod.shape[1] // HEAD_DIM):
        d = jnp.sum(prod[:, HEAD_DIM * h:HEAD_DIM * (h + 1)], axis=-1, keepdims=True)
        tile = jnp.where(lane == DELTA_LANE + h, d, tile)
    return tile


def _mid(oa, lse_a, ob, lse_b, oc, lse_c, gate, x, target, w_out_full, post_g, tm=256):
    seq = x.shape[0]
    n_b = B_W // LANES

    def body(oa_ref, la_ref, b1_ref, l1_ref, b4_ref, l4_ref, b16_ref, l16_ref, oc_ref, lc_ref,
             gate_ref, x_ref, t_ref, w_ref, pg_ref,
             dh_ref, dg_ref, doa_ref, sa_ref, dob1_ref, sb1_ref, dob4_ref, sb4_ref, dob16_ref, sb16_ref,
             doc_ref, sc_ref, dw_ref, st_ref, scr_b4, scr_b16, scr_l4, scr_l16, scr_do, scr_sb):
        @pl.when(pl.program_id(0) == 0)
        def _():
            dw_ref[...] = jnp.zeros_like(dw_ref)
            st_ref[...] = jnp.zeros_like(st_ref)

        b1, l1 = b1_ref[...].astype(F32), l1_ref[...]
        b4, l4 = _load_permuted(b4_ref, scr_b4, 4), _load_permuted(l4_ref, scr_l4, 4)
        b16, l16 = _load_permuted(b16_ref, scr_b16, 16), _load_permuted(l16_ref, scr_l16, 16)
        lm = jnp.maximum(jnp.maximum(l1, l4), l16)
        e1, e4, e16 = jnp.exp(l1 - lm), jnp.exp(l4 - lm), jnp.exp(l16 - lm)
        den = e1 + e4 + e16
        lse_b_tile = lm + jnp.log(den)
        ob_v = _per_head(e1 / den, B_W) * b1 + _per_head(e4 / den, B_W) * b4 + _per_head(e16 / den, B_W) * b16
        o_all = jnp.concatenate([oa_ref[...].astype(F32), ob_v, oc_ref[...].astype(F32)], axis=1)
        g = gate_ref[...].astype(F32)
        sig = 1.0 / (1.0 + jnp.exp(-g))
        silu = g * sig
        y = (o_all * silu).astype(BF16)
        w = w_ref[...]
        z = jnp.dot(y, w, preferred_element_type=F32)
        rz = lax.rsqrt(jnp.mean(z * z, axis=-1, keepdims=True) + RMS_EPS)
        hn = z * rz
        pg = pg_ref[...]
        err = (x_ref[...] + hn * pg) - t_ref[...]
        loss = 0.5 * jnp.sum(jnp.mean(err * err, axis=-1, keepdims=True), axis=0, keepdims=True)
        dh = err * (1.0 / D_MODEL)
        dh_ref[...] = dh.astype(BF16)
        st_ref[0:1, :] += jnp.sum(dh * hn, axis=0, keepdims=True)
        st_ref[1:2, :] += jnp.broadcast_to(loss, (1, D_MODEL))
        dhn = dh * pg
        dz = (rz * (dhn - hn * jnp.mean(dhn * hn, axis=-1, keepdims=True))).astype(BF16)
        dy = lax.dot_general(dz, w, (((1,), (1,)), ((), ())), preferred_element_type=F32)
        dw_ref[...] += lax.dot_general(y, dz, (((0,), (0,)), ((), ())), preferred_element_type=F32)
        dg_ref[...] = (dy * o_all * (sig * (1.0 + g * (1.0 - sig)))).astype(BF16)
        d_o = (dy * silu).astype(BF16)
        prod = d_o.astype(F32) * o_all
        doa_ref[...] = d_o[:, 0:A_W]
        sa_ref[...] = _with_delta(la_ref[...], prod[:, 0:A_W])
        doc_ref[...] = d_o[:, A_W + B_W:D_MODEL]
        sc_ref[...] = _with_delta(lc_ref[...], prod[:, A_W + B_W:D_MODEL])
        d_ob = d_o[:, A_W:A_W + B_W]
        stat_b = _with_delta(lse_b_tile, prod[:, A_W:A_W + B_W])
        dob1_ref[...] = d_ob
        sb1_ref[...] = stat_b
        _put(scr_do, d_ob.astype(F32))
        _put(scr_sb, stat_b)
        _store_permuted(scr_do, dob4_ref, 4, BF16)
        _store_permuted(scr_sb, sb4_ref, 4, F32)
        _store_permuted(scr_do, dob16_ref, 16, BF16)
        _store_permuted(scr_sb, sb16_ref, 16, F32)

    p4 = lambda w: _perm_spec(tm, 4, w)
    p16 = lambda w: _perm_spec(tm, 16, w)
    in_specs = [_row(tm, A_W), _row(tm, LANES), _row(tm, B_W), _row(tm, LANES), p4(B_W), p4(LANES), p16(B_W), p16(LANES),
                _row(tm, C_W), _row(tm, LANES), _row(tm, D_MODEL), _row(tm, D_MODEL), _row(tm, D_MODEL),
                _full((D_MODEL, D_MODEL)), _full((1, D_MODEL))]
    sds = jax.ShapeDtypeStruct
    v4 = lambda w, dt: sds((seq // (BLOCK * 4), 4, BLOCK, w), dt)
    v16 = lambda w, dt: sds((seq // (BLOCK * 16), 16, BLOCK, w), dt)
    out_specs = [_row(tm, D_MODEL), _row(tm, D_MODEL), _row(tm, A_W), _row(tm, LANES), _row(tm, B_W), _row(tm, LANES),
                 p4(B_W), p4(LANES), p16(B_W), p16(LANES), _row(tm, C_W), _row(tm, LANES),
                 _full((D_MODEL, D_MODEL)), _full((8, D_MODEL))]
    out_shape = [sds((seq, D_MODEL), BF16), sds((seq, D_MODEL), BF16), sds((seq, A_W), BF16), sds((seq, LANES), F32),
                 sds((seq, B_W), BF16), sds((seq, LANES), F32), v4(B_W, BF16), v4(LANES, F32), v16(B_W, BF16),
                 v16(LANES, F32), sds((seq, C_W), BF16), sds((seq, LANES), F32),
                 sds((D_MODEL, D_MODEL), F32), sds((8, D_MODEL), F32)]
    res = pl.pallas_call(
        body, name="mid", grid=(seq // tm,), in_specs=in_specs, out_specs=out_specs, out_shape=out_shape,
        scratch_shapes=[pltpu.VMEM((n_b, tm, LANES), F32), pltpu.VMEM((n_b, tm, LANES), F32),
                        pltpu.VMEM((1, tm, LANES), F32), pltpu.VMEM((1, tm, LANES), F32),
                        pltpu.VMEM((n_b, tm, LANES), F32), pltpu.VMEM((1, tm, LANES), F32)],
        compiler_params=_params(dimension_semantics=("arbitrary",)),
    )(oa, lse_a, ob[1], lse_b[1], _perm_view(ob[4], 4), _perm_view(lse_b[4], 4), _perm_view(ob[16], 16),
      _perm_view(lse_b[16], 16), oc, lse_c, gate, x, target, w_out_full, post_g)
    dh, d_gate, do_a, st_a, do_b1, st_b1, do_b4, st_b4, do_b16, st_b16, do_c, st_c, d_wout, stats = res
    flat = lambda t: t.reshape(seq, t.shape[-1])
    d_b = {1: (do_b1, st_b1), 4: (flat(do_b4), flat(st_b4)), 16: (flat(do_b16), flat(st_b16))}
    return dh, d_gate, (do_a, st_a), d_b, (do_c, st_c), d_wout, stats


def _inproj_bwd_x(x, dh, pre_g, w_in_full, tabs, dqa, dka, dva, dqkv_b, dqc, dgate, tm=256):
    seq = x.shape[0]
    n_b = B_W // LANES

    def body(x_ref, dh_ref, g_ref, w_ref, c_ref, up_ref, dn_ref, dqa_ref, dka_ref, dva_ref,
             dq1, dk1, dv1, dq4, dk4, dv4, dq16, dk16, dv16, dqc_ref, dg_ref,
             gx_ref, dp_ref, st_ref, scr4, scr16):
        @pl.when(pl.program_id(0) == 0)
        def _():
            st_ref[...] = jnp.zeros_like(st_ref)

        c, up, dn = c_ref[...], -up_ref[...], -dn_ref[...]
        unrot = lambda t: _rotate(t, c, up, dn)
        total = lambda r1, r4, r16: (r1[...].astype(F32) + _load_permuted(r4, scr4, 4)
                                     + _load_permuted(r16, scr16, 16))
        dp_ref[:, 0:384] = (unrot(dqa_ref[...].astype(F32)) * SCALE).astype(BF16)
        dp_ref[:, 384:512] = unrot(dka_ref[...].astype(F32)).astype(BF16)
        dp_ref[:, 512:640] = dva_ref[...]
        dp_ref[:, 640:1024] = dg_ref[:, 0:384]
        dp_ref[:, 1024:1408] = (unrot(total(dq1, dq4, dq16)) * SCALE).astype(BF16)
        dp_ref[:, 1408:1792] = unrot(total(dk1, dk4, dk16)).astype(BF16)
        dp_ref[:, 1792:2176] = total(dv1, dv4, dv16).astype(BF16)
        dp_ref[:, 2176:2560] = dg_ref[:, 384:768]
        dp_ref[:, 2560:2816] = (dqc_ref[...].astype(F32) * SCALE).astype(BF16)
        dp_ref[:, 2816:3072] = dg_ref[:, 768:1024]
        du = jnp.zeros((tm, D_MODEL), F32)
        for k in range(N_DEV):
            du = du + lax.dot_general(dp_ref[:, SHARD_IN * k:SHARD_IN * (k + 1)], w_ref[k],
                                      (((1,), (1,)), ((), ())), preferred_element_type=F32)
        xv = x_ref[...]
        r = lax.rsqrt(jnp.mean(xv * xv, axis=-1, keepdims=True) + RMS_EPS)
        xh = xv * r
        st_ref[0:1, :] += jnp.sum(du * xh, axis=0, keepdims=True)
        dxh = du * g_ref[...]
        gx_ref[...] = dh_ref[...].astype(F32) + r * (dxh - xh * jnp.mean(dxh * xh, axis=-1, keepdims=True))

    in_specs = ([_row(tm, D_MODEL), _row(tm, D_MODEL), _full((1, D_MODEL)), _full((N_DEV, D_MODEL, SHARD_IN)),
                 _row(tm, LANES), _row(tm, LANES), _row(tm, LANES), _row(tm, A_W), _row(tm, A_KV_W), _row(tm, A_KV_W)]
                + [_row(tm, B_W)] * 3 + [_perm_spec(tm, 4, B_W)] * 3 + [_perm_spec(tm, 16, B_W)] * 3
                + [_row(tm, C_W), _row(tm, D_MODEL)])
    return pl.pallas_call(
        body, name="inproj_bwd_x", grid=(seq // tm,), in_specs=in_specs,
        out_specs=[_row(tm, D_MODEL), _row(tm, D_IN), _full((8, D_MODEL))],
        out_shape=[jax.ShapeDtypeStruct((seq, D_MODEL), F32), jax.ShapeDtypeStruct((seq, D_IN), BF16),
                   jax.ShapeDtypeStruct((8, D_MODEL), F32)],
        scratch_shapes=[pltpu.VMEM((n_b, tm, LANES), F32), pltpu.VMEM((n_b, tm, LANES), F32)],
        compiler_params=_params(dimension_semantics=("arbitrary",)),
    )(x, dh, pre_g, w_in_full, *tabs, dqa, dka, dva, *dqkv_b[1], *[_perm_view(t, 4) for t in dqkv_b[4]],
      *[_perm_view(t, 16) for t in dqkv_b[16]], dqc, dgate)


def _inproj_bwd_w(u, dproj, tm=1024):
    seq = u.shape[0]

    def body(u_ref, dp_ref, dw_ref):
        @pl.when(pl.program_id(0) == 0)
        def _():
            dw_ref[...] = jnp.zeros_like(dw_ref)

        res = lax.dot_general(u_ref[...], dp_ref[...], _TN, preferred_element_type=F32)
        for k in range(N_DEV):
            dw_ref[k] += res[:, SHARD_IN * k:SHARD_IN * (k + 1)]

    return pl.pallas_call(
        body, name="inproj_bwd_w", grid=(seq // tm,),
        in_specs=[_row(tm, D_MODEL), _row(tm, D_IN)],
        out_specs=_full((N_DEV, D_MODEL, SHARD_IN)),
        out_shape=jax.ShapeDtypeStruct((N_DEV, D_MODEL, SHARD_IN), F32),
        compiler_params=_params(dimension_semantics=("arbitrary",)),
    )(u, dproj)


def _local_step(x, mem, pre_g, w_in_full, sink, mem_g, w_mem_full, w_out_full, post_g, target):
    seq = x.shape[0]
    tabs = _rope_tables(seq)
    u, qa, ka, va, qkv_b, qc, gate = _inproj(x, pre_g, w_in_full, tabs)
    mn, mk, mv = _memkv_fwd(mem, mem_g, w_mem_full)

    a_cfg = dict(dil=1, heads=A_HEADS, group=A_GROUP, max_dist=BLOCK - 1, nq=ATTN_BLOCKS_PER_STEP)
    b_cfgs = {dil: dict(dil=dil, heads=B_HEADS, group=1, max_dist=win // dil, nq=ATTN_BLOCKS_PER_STEP)
              for win, dil in B_CONFIGS}
    oa, lse_a = _banded_fwd(qa, ka, va, sink, name="attn_a_fwd", **a_cfg)
    ob, lse_b = {}, {}
    for dil, cfg in b_cfgs.items():
        ob[dil], lse_b[dil] = _banded_fwd(*qkv_b[dil], None, name=f"attn_b{dil}_fwd", **cfg)
    oc, lse_c = _cross_fwd(qc, mk, mv)

    dh, d_gate, d_a, d_b, d_c, d_wout, st_mid = _mid(oa, lse_a, ob, lse_b, oc, lse_c, gate, x, target, w_out_full, post_g)

    dqa, dka, dva, dsink = _banded_bwd(qa, ka, va, *d_a, sink, name="attn_a_bwd", **a_cfg)
    dqkv_b = {dil: _banded_bwd(*qkv_b[dil], *d_b[dil], None, name=f"attn_b{dil}_bwd", **cfg)
              for dil, cfg in b_cfgs.items()}
    dqc, dmk, dmv = _cross_bwd(qc, mk, mv, *d_c)
    d_wmem, st_mem = _memkv_bwd(mem, mem_g, mn, w_mem_full, dmk, dmv)

    grad_x, dproj, st_pre = _inproj_bwd_x(x, dh, pre_g, w_in_full, tabs, dqa, dka, dva, dqkv_b, dqc, d_gate)
    d_win = _inproj_bwd_w(u, dproj)

    dsink_row = jnp.pad(dsink[0:1, :], ((0, 0), (0, D_MODEL - LANES)))
    stats = jnp.concatenate([st_pre[0:1], st_mem[0:1], st_mid[0:1], dsink_row, st_mid[1:2],
                             jnp.zeros((3, D_MODEL), F32)], axis=0)
    return grad_x, d_win, d_wmem, d_wout, stats


def _mesh_pos():
    return lax.axis_index("x"), lax.axis_index("y"), lax.axis_index("c")


def _all_gather_weights(w_in, w_mem, w_out):
    def body(win_ref, wm_ref, wo_ref, win_out, wm_out, wo_out, win_b, wm_b, wo_b, send_sems, recv_sems, local_sems):
        x, y, c = _mesh_pos()
        win_b[...] = win_ref[...].astype(BF16)
        wm_b[...] = wm_ref[...].astype(BF16)
        wo_b[...] = wo_ref[...].astype(BF16)
        srcs = (win_b, wm_b, wo_b)
        outs = (win_out, wm_out, wo_out)
        me, sibling = (x, y, c), (x, y, 1 - c)
        chips = [(1 - x, y), (x, 1 - y), (1 - x, 1 - y)]

        def slot(a, p):
            return outs[a].at[4 * p[0] + 2 * p[1] + p[2]]

        def copy(a, k, block, to, src=None):
            return pltpu.make_async_remote_copy(
                src_ref=slot(a, block) if src is None else src, dst_ref=slot(a, block),
                send_sem=send_sems.at[a, k], recv_sem=recv_sems.at[a, k], device_id=to, device_id_type=MESH_ID)

        mine = [pltpu.make_async_copy(srcs[a], slot(a, me), local_sems.at[a]) for a in range(3)]
        for cp in mine:
            cp.start()
        first = []
        for a in range(3):
            first.append(copy(a, 0, me, sibling, src=srcs[a]))
            first += [copy(a, 1 + j, me, (*chip, c), src=srcs[a]) for j, chip in enumerate(chips)]
        for cp in first:
            cp.start()
        passed = []
        for j, chip in enumerate(chips):
            for a in range(3):
                copy(a, 1 + j, (*chip, c), me).wait_recv()
                fwd = copy(a, 4 + j, (*chip, c), sibling)
                fwd.start()
                passed.append(fwd)
        for a in range(3):
            copy(a, 0, sibling, me).wait_recv()
            for j, chip in enumerate(chips):
                copy(a, 4 + j, (*chip, 1 - c), me).wait_recv()
        for cp in first + passed:
            cp.wait_send()
        for cp in mine:
            cp.wait()

    shapes = [(D_MODEL, SHARD_IN), (SHARD_ROWS, 2 * C_W), (SHARD_ROWS, D_MODEL)]
    vmem = pl.BlockSpec(memory_space=pltpu.VMEM)
    hbm = pl.BlockSpec(memory_space=pl.ANY)
    return pl.pallas_call(
        body, name="all_gather_weights",
        in_specs=[vmem] * 3, out_specs=[hbm] * 3,
        out_shape=[jax.ShapeDtypeStruct((N_DEV,) + s, BF16) for s in shapes],
        scratch_shapes=[pltpu.VMEM(s, BF16) for s in shapes]
        + [pltpu.SemaphoreType.DMA((3, 7)), pltpu.SemaphoreType.DMA((3, 7)), pltpu.SemaphoreType.DMA((3,))],
        compiler_params=_params(),
    )(w_in, w_mem, w_out)


def _exchange_grads(d_win, d_wmem, d_wout, stats):
    shapes = [d_win.shape[1:], d_wmem.shape[1:], d_wout.shape[1:]]
    n_arr = len(shapes)

    def body(win, wm, wo, st, g_win, g_wm, g_wo, r_st, *scratch):
        mine, got, snd, rcv = (scratch[n_arr * t:n_arr * (t + 1)] for t in range(4))
        load_sems, d2d_send, d2d_recv, ici_send, ici_recv, st_send, st_recv, st_local = scratch[4 * n_arr:]
        x, y, c = _mesh_pos()
        me = 4 * x + 2 * y + c
        ins, outs = (win, wm, wo), (g_win, g_wm, g_wo)

        def chip(kk):
            return (1 - x if kk & 2 else x, 1 - y if kk & 1 else y)

        st_own = pltpu.make_async_copy(st, r_st.at[me], st_local)
        st_own.start()
        st_copies = []
        for s in range(1, N_DEV):
            peer = (1 - x if s & 4 else x, 1 - y if s & 2 else y, 1 - c if s & 1 else c)
            pid = 4 * peer[0] + 2 * peer[1] + peer[2]
            cp = pltpu.make_async_remote_copy(src_ref=st, dst_ref=r_st.at[me], send_sem=st_send.at[s],
                                              recv_sem=st_recv.at[s], device_id=peer, device_id_type=MESH_ID)
            cp.start()
            arrive = pltpu.make_async_remote_copy(src_ref=st, dst_ref=r_st.at[pid], send_sem=st_send.at[s],
                                                  recv_sem=st_recv.at[s], device_id=peer, device_id_type=MESH_ID)
            st_copies.append((cp, arrive))

        loads, swaps = {}, {}
        for kk in range(4):
            ox, oy = chip(kk)
            for a in range(n_arr):
                ld = pltpu.make_async_copy(ins[a].at[4 * ox + 2 * oy + c], mine[a].at[kk], load_sems.at[a, kk])
                ld.start()
                sw = pltpu.make_async_remote_copy(
                    src_ref=ins[a].at[4 * ox + 2 * oy + (1 - c)], dst_ref=got[a].at[kk],
                    send_sem=d2d_send.at[a, kk], recv_sem=d2d_recv.at[a, kk],
                    device_id=(x, y, 1 - c), device_id_type=MESH_ID)
                sw.start()
                loads[a, kk], swaps[a, kk] = ld, sw

        hops = {}
        for kk in (3, 1, 2):
            ox, oy = chip(kk)
            for a in range(n_arr):
                loads[a, kk].wait()
                swaps[a, kk].wait_recv()
                snd[a][kk - 1] = (mine[a][kk] + got[a][kk]).astype(BF16)
                hop = pltpu.make_async_remote_copy(
                    src_ref=snd[a].at[kk - 1], dst_ref=rcv[a].at[kk - 1],
                    send_sem=ici_send.at[a, kk], recv_sem=ici_recv.at[a, kk],
                    device_id=(ox, oy, c), device_id_type=MESH_ID)
                hop.start()
                hops[a, kk] = hop

        for a in range(n_arr):
            loads[a, 0].wait()
            swaps[a, 0].wait_recv()
            acc = mine[a][0] + got[a][0]
            for kk in (1, 2, 3):
                hops[a, kk].wait_recv()
                acc = acc + rcv[a][kk - 1].astype(F32)
            outs[a][...] = acc

        for sw in swaps.values():
            sw.wait_send()
        for hop in hops.values():
            hop.wait_send()
        for cp, arrive in st_copies:
            arrive.wait_recv()
            cp.wait_send()
        st_own.wait()

    hbm = pl.BlockSpec(memory_space=pl.ANY)
    vmem = pl.BlockSpec(memory_space=pltpu.VMEM)
    out_shape = [jax.ShapeDtypeStruct(s, F32) for s in shapes] + [jax.ShapeDtypeStruct((N_DEV,) + stats.shape, F32)]
    scratch = ([pltpu.VMEM((4,) + s, F32) for s in shapes] + [pltpu.VMEM((4,) + s, F32) for s in shapes]
               + [pltpu.VMEM((3,) + s, BF16) for s in shapes] + [pltpu.VMEM((3,) + s, BF16) for s in shapes]
               + [pltpu.SemaphoreType.DMA((n_arr, 4))] * 5
               + [pltpu.SemaphoreType.DMA((N_DEV,)), pltpu.SemaphoreType.DMA((N_DEV,)), pltpu.SemaphoreType.DMA(())])
    return pl.pallas_call(
        body, name="exchange_grads", in_specs=[hbm] * 4, out_specs=[vmem] * n_arr + [hbm], out_shape=out_shape,
        scratch_shapes=scratch, compiler_params=_params(),
    )(d_win, d_wmem, d_wout, stats)


def _reduce_adamw(recv, w, m, v, *, tr, name):
    n_part, rows, cols = recv.shape

    def body(r_ref, w_ref, m_ref, v_ref, g_ref, d_ref, nm_ref, nv_ref):
        g = r_ref[0]
        for s in range(1, n_part):
            g = g + r_ref[s]
        g_ref[...] = g
        m2 = ADAM_B1 * m_ref[...] + (1.0 - ADAM_B1) * g
        v2 = ADAM_B2 * v_ref[...] + (1.0 - ADAM_B2) * (g * g)
        nm_ref[...] = m2
        nv_ref[...] = v2
        m_hat = m2 / (1.0 - ADAM_B1 ** ADAM_STEP)
        v_hat = v2 / (1.0 - ADAM_B2 ** ADAM_STEP)
        d_ref[...] = -ADAM_LR * (m_hat / (jnp.sqrt(v_hat) + ADAM_EPS) + ADAM_WD * w_ref[...])

    blk = pl.BlockSpec((tr, cols), lambda i: (i, 0))
    return pl.pallas_call(
        body, name=name, grid=(rows // tr,),
        in_specs=[pl.BlockSpec((n_part, tr, cols), lambda i: (0, i, 0)), blk, blk, blk],
        out_specs=[blk] * 4, out_shape=[jax.ShapeDtypeStruct((rows, cols), F32)] * 4,
        compiler_params=_params(dimension_semantics=("arbitrary",)),
    )(recv, w, m, v)


def _pack_rows(pre, memn, post, sink):
    sink_row = jnp.pad(sink, ((0, 0), (0, D_MODEL - A_HEADS)))
    return jnp.concatenate([pre, memn, post, sink_row, jnp.zeros((4, D_MODEL), F32)], axis=0)


def kernel(x, mem, pre_norm, w_in, sink_a, mem_norm, w_mem_kv, w_out, post_norm, loss_target, m_pre_norm, m_w_in, m_sink_a, m_mem_norm, m_w_mem_kv, m_w_out, m_post_norm, v_pre_norm, v_w_in, v_sink_a, v_mem_norm, v_w_mem_kv, v_w_out, v_post_norm):
    w_in_full, w_mem_full, w_out_full = _all_gather_weights(w_in[0], w_mem_kv[0], w_out[0])
    sink = jnp.pad(sink_a[0], (0, 8 - A_HEADS))
    grad_x, d_win, d_wmem, d_wout, stats = _local_step(
        x[0], mem[0], pre_norm, w_in_full, sink, mem_norm, w_mem_full.reshape(D_MODEL, 2 * C_W),
        w_out_full.reshape(D_MODEL, D_MODEL), post_norm, loss_target[0])
    g_win, g_wmem, g_wout, r_stats = _exchange_grads(
        d_win, d_wmem.reshape(N_DEV, SHARD_ROWS, 2 * C_W), d_wout.reshape(N_DEV, SHARD_ROWS, D_MODEL), stats)

    big = {}
    for nm, g, w, m, v in (("w_in", g_win, w_in, m_w_in, v_w_in),
                           ("w_mem_kv", g_wmem, w_mem_kv, m_w_mem_kv, v_w_mem_kv),
                           ("w_out", g_wout, w_out, m_w_out, v_w_out)):
        res = _reduce_adamw(g[None], w[0], m[0], v[0], tr=SHARD_ROWS, name="adamw_" + nm)
        big[nm] = [t[None] for t in res]
    small = _reduce_adamw(
        r_stats, _pack_rows(pre_norm, mem_norm, post_norm, sink_a),
        _pack_rows(m_pre_norm, m_mem_norm, m_post_norm, m_sink_a),
        _pack_rows(v_pre_norm, v_mem_norm, v_post_norm, v_sink_a), tr=8, name="adamw_small")

    def unpack(t):
        return {"pre_norm": t[0:1], "mem_norm": t[1:2], "post_norm": t[2:3], "sink_a": t[3:4, 0:A_HEADS]}

    order = ("pre_norm", "w_in", "sink_a", "mem_norm", "w_mem_kv", "w_out", "post_norm")
    outs = [small[0][4, 0], grad_x[None]]
    for j in range(4):
        sm = unpack(small[j])
        outs += [big[n][j] if n in big else sm[n] for n in order]
    return tuple(outs)
```

```python
import jax
import jax.numpy as jnp
from jax import lax
from jax.experimental import pallas as pl
from jax.experimental.pallas import tpu as pltpu

F32 = jnp.float32
BF16 = jnp.bfloat16

D_MODEL = 1024
HEAD_DIM = 64
ROT_DIM = 16
ROPE_THETA = 500000.0
BLOCK = 128
LANES = 128
N_MEM = 256
RMS_EPS = 1e-6
SCALE = HEAD_DIM ** -0.5
A_HEADS, A_GROUP = 6, 3
B_HEADS = 6
C_HEADS = 4
A_W, A_KV_W, B_W, C_W = 384, 128, 384, 256
D_IN = 3072
N_DEV = 8
SHARD_IN = D_IN // N_DEV
SHARD_ROWS = D_MODEL // N_DEV
B_CONFIGS = ((128, 1), (512, 4), (2048, 16))
DILS = (4, 16)
NEG = -1e30
ATTN_BLOCKS_PER_STEP = 4
DELTA_LANE = 64
VMEM_LIMIT = 56 * 1024 * 1024

ADAM_LR, ADAM_B1, ADAM_B2, ADAM_EPS, ADAM_WD, ADAM_STEP = 0.001, 0.9, 0.999, 1e-08, 0.01, 10
MESH_ID = pl.DeviceIdType.MESH


def _params(**kw):
    return pltpu.CompilerParams(vmem_limit_bytes=VMEM_LIMIT, **kw)


def _full(shape):
    n = len(shape)
    return pl.BlockSpec(shape, lambda *_: (0,) * n)


def _row(tm, w):
    return pl.BlockSpec((tm, w), lambda i: (i, 0))


def _perm_view(a, dil):
    return a.reshape(a.shape[0] // (BLOCK * dil), dil, BLOCK, a.shape[1])


def _perm_spec(tm, dil, w):
    per = BLOCK * dil // tm
    return pl.BlockSpec((1, dil, tm // dil, w), lambda i: (i // per, 0, i % per, 0))


def _put(scr, val):
    for c in range(val.shape[1] // LANES):
        scr[c] = val[:, LANES * c:LANES * (c + 1)]


def _get(scr):
    n = scr.shape[0]
    return scr[0] if n == 1 else jnp.concatenate([scr[c] for c in range(n)], axis=1)


def _get_class(scr, r, dil):
    n, rows = scr.shape[0], scr.shape[1]
    parts = [scr.at[c][pl.ds(r, rows // dil, stride=dil), :] for c in range(n)]
    return parts[0] if n == 1 else jnp.concatenate(parts, axis=1)


def _store_permuted(scr, out_ref, dil, dtype):
    for r in range(dil):
        out_ref[0, r] = _get_class(scr, r, dil).astype(dtype)


def _load_permuted(in_ref, scr, dil):
    n, rows = scr.shape[0], scr.shape[1]
    for r in range(dil):
        val = in_ref[0, r].astype(F32)
        for c in range(n):
            scr.at[c][pl.ds(r, rows // dil, stride=dil), :] = val[:, LANES * c:LANES * (c + 1)]
    return _get(scr)


def _rope_tables(seq):
    j = jnp.arange(LANES) % HEAD_DIM
    freq = ROPE_THETA ** (-(2 * (j % (ROT_DIM // 2))).astype(F32) / ROT_DIM)
    ang = jnp.arange(seq, dtype=F32)[:, None] * freq[None, :]
    cos, sin = jnp.cos(ang), jnp.sin(ang)
    half = ROT_DIM // 2
    c = jnp.where(j < ROT_DIM, cos, 1.0)
    up = jnp.where((j >= half) & (j < ROT_DIM), sin, 0.0)
    dn = jnp.where(j < half, -sin, 0.0)
    return c, up, dn


def _rotate128(t, c, up, dn):
    return t * c + pltpu.roll(t, 8, 1) * up + pltpu.roll(t, LANES - 8, 1) * dn


def _rotate(t, c, up, dn):
    outs = [_rotate128(t[:, LANES * j:LANES * (j + 1)], c, up, dn) for j in range(t.shape[1] // LANES)]
    return outs[0] if len(outs) == 1 else jnp.concatenate(outs, axis=1)


def _w_in_scratch():
    return [pltpu.VMEM((D_MODEL, D_IN), BF16), pltpu.SemaphoreType.DMA((N_DEV,))]


def _stage_w_in(w_hbm, w_scr, sems):
    @pl.when(pl.program_id(0) == 0)
    def _():
        copies = [pltpu.make_async_copy(w_hbm.at[k], w_scr.at[:, pl.ds(SHARD_IN * k, SHARD_IN)], sems.at[k])
                  for k in range(N_DEV)]
        for cp in copies:
            cp.start()
        for cp in copies:
            cp.wait()


def _inproj(x, pre_g, w_in_full, tabs, tm=512):
    seq = x.shape[0]
    n_chunk = D_IN // LANES

    def body(x_ref, g_ref, w_hbm, c_ref, up_ref, dn_ref, u_ref, qa_ref, ka_ref, va_ref,
             qb1_ref, kb1_ref, vb1_ref, qb4_ref, kb4_ref, vb4_ref, qb16_ref, kb16_ref, vb16_ref,
             qc_ref, gate_ref, proj, w_scr, w_sems):
        _stage_w_in(w_hbm, w_scr, w_sems)
        xv = x_ref[...]
        r = lax.rsqrt(jnp.mean(xv * xv, axis=-1, keepdims=True) + RMS_EPS)
        u = ((xv * r) * g_ref[...]).astype(BF16)
        u_ref[...] = u
        for n0 in range(0, D_IN, D_MODEL):
            acc = jnp.dot(u, w_scr[:, n0:n0 + D_MODEL], preferred_element_type=F32)
            for c3 in range(D_MODEL // LANES):
                proj[n0 // LANES + c3] = acc[:, LANES * c3:LANES * (c3 + 1)]
        c, up, dn = c_ref[...], up_ref[...], dn_ref[...]

        def cols(lo, hi, rot=False, scale=None):
            parts = []
            for ch in range(lo // LANES, hi // LANES):
                t = proj[ch]
                if rot:
                    t = _rotate128(t, c, up, dn)
                if scale is not None:
                    t = t * scale
                parts.append(t)
            return parts[0] if len(parts) == 1 else jnp.concatenate(parts, axis=1)

        qa_ref[...] = cols(0, 384, True, SCALE).astype(BF16)
        ka_ref[...] = cols(384, 512, True).astype(BF16)
        va_ref[...] = cols(512, 640).astype(BF16)
        gate_ref[:, 0:384] = cols(640, 1024).astype(BF16)
        gate_ref[:, 384:768] = cols(2176, 2560).astype(BF16)
        gate_ref[:, 768:1024] = cols(2816, 3072).astype(BF16)
        qc_ref[...] = cols(2560, 2816, False, SCALE).astype(BF16)
        for ch in range(1024 // LANES, 1408 // LANES):
            proj[ch] = _rotate128(proj[ch], c, up, dn) * SCALE
        for ch in range(1408 // LANES, 1792 // LANES):
            proj[ch] = _rotate128(proj[ch], c, up, dn)
        for lo, nat, p4, p16 in ((1024, qb1_ref, qb4_ref, qb16_ref), (1408, kb1_ref, kb4_ref, kb16_ref),
                                 (1792, vb1_ref, vb4_ref, vb16_ref)):
            chunks = range(lo // LANES, lo // LANES + B_W // LANES)
            nat[...] = jnp.concatenate([proj[ch] for ch in chunks], axis=1).astype(BF16)
            for dil, ref in ((4, p4), (16, p16)):
                for rr in range(dil):
                    ref[0, rr] = jnp.concatenate(
                        [proj.at[ch][pl.ds(rr, tm // dil, stride=dil), :] for ch in chunks], axis=1).astype(BF16)

    nat_w = (D_MODEL, A_W, A_KV_W, A_KV_W, B_W, B_W, B_W)
    out_specs = [_row(tm, w) for w in nat_w]
    out_shape = [jax.ShapeDtypeStruct((seq, w), BF16) for w in nat_w]
    for dil in DILS:
        out_specs += [_perm_spec(tm, dil, B_W)] * 3
        out_shape += [jax.ShapeDtypeStruct((seq // (BLOCK * dil), dil, BLOCK, B_W), BF16)] * 3
    out_specs += [_row(tm, C_W), _row(tm, D_MODEL)]
    out_shape += [jax.ShapeDtypeStruct((seq, C_W), BF16), jax.ShapeDtypeStruct((seq, D_MODEL), BF16)]
    res = pl.pallas_call(
        body, name="inproj", grid=(seq // tm,),
        in_specs=[_row(tm, D_MODEL), _full((1, D_MODEL)), pl.BlockSpec(memory_space=pl.ANY),
                  _row(tm, LANES), _row(tm, LANES), _row(tm, LANES)],
        out_specs=out_specs, out_shape=out_shape,
        scratch_shapes=[pltpu.VMEM((n_chunk, tm, LANES), F32)] + _w_in_scratch(),
        compiler_params=_params(dimension_semantics=("arbitrary",)),
    )(x, pre_g, w_in_full, *tabs)
    u, qa, ka, va = res[0:4]
    qkv_b = {1: res[4:7], 4: [t.reshape(seq, B_W) for t in res[7:10]], 16: [t.reshape(seq, B_W) for t in res[10:13]]}
    return u, qa, ka, va, qkv_b, res[13], res[14]


def _memkv_fwd(mem, mem_g, w_mem_full):
    def body(mem_ref, g_ref, w_ref, mn_ref, mk_ref, mv_ref):
        mv_ = mem_ref[...]
        r = lax.rsqrt(jnp.mean(mv_ * mv_, axis=-1, keepdims=True) + RMS_EPS)
        mn = ((mv_ * r) * g_ref[...]).astype(BF16)
        mn_ref[...] = mn
        mkv = jnp.dot(mn, w_ref[...], preferred_element_type=F32)
        mk_ref[...] = mkv[:, 0:C_W].astype(BF16)
        mv_ref[...] = mkv[:, C_W:2 * C_W].astype(BF16)

    return pl.pallas_call(
        body, name="memkv_fwd",
        out_shape=[jax.ShapeDtypeStruct((N_MEM, D_MODEL), BF16),
                   jax.ShapeDtypeStruct((N_MEM, C_W), BF16), jax.ShapeDtypeStruct((N_MEM, C_W), BF16)],
        compiler_params=_params(),
    )(mem, mem_g, w_mem_full)


def _memkv_bwd(mem, mem_g, mn, w_mem_full, dmk, dmv):
    def body(mem_ref, g_ref, mn_ref, w_ref, dmk_ref, dmv_ref, dw_ref, st_ref):
        dmkv = jnp.concatenate([dmk_ref[...], dmv_ref[...]], axis=1).astype(BF16)
        dw_ref[...] = lax.dot_general(mn_ref[...], dmkv, (((0,), (0,)), ((), ())), preferred_element_type=F32)
        dmn = lax.dot_general(dmkv, w_ref[...], (((1,), (1,)), ((), ())), preferred_element_type=F32)
        mv_ = mem_ref[...]
        r = lax.rsqrt(jnp.mean(mv_ * mv_, axis=-1, keepdims=True) + RMS_EPS)
        st_ref[...] = jnp.zeros_like(st_ref)
        st_ref[0:1, :] = jnp.sum(dmn * (mv_ * r), axis=0, keepdims=True)

    return pl.pallas_call(
        body, name="memkv_bwd",
        out_shape=[jax.ShapeDtypeStruct((D_MODEL, 2 * C_W), F32), jax.ShapeDtypeStruct((8, D_MODEL), F32)],
        compiler_params=_params(),
    )(mem, mem_g, mn, w_mem_full, dmk, dmv)


def _band_mask(has_prev, max_dist):
    qi = lax.broadcasted_iota(jnp.int32, (BLOCK, 2 * BLOCK), 0)
    kj = lax.broadcasted_iota(jnp.int32, (BLOCK, 2 * BLOCK), 1)
    dist = qi + BLOCK - kj
    return (dist >= 0) & (dist <= max_dist) & ((kj >= BLOCK) | has_prev)


_NT = (((1,), (1,)), ((), ()))
_TN = (((0,), (0,)), ((), ()))


def _head_only(val, h):
    slab = val[:, LANES * (h // 2):LANES * (h // 2 + 1)]
    lane = lax.broadcasted_iota(jnp.int32, slab.shape, 1)
    keep = (lane < HEAD_DIM) if h % 2 == 0 else (lane >= HEAD_DIM)
    return jnp.where(keep, slab, jnp.zeros((), slab.dtype))


class _KvSlabs:
    def __init__(self, cat, group):
        self.cat, self.group, self.swapped = cat, group, {}

    def is_swapped(self, h):
        return (h // self.group) % 2 != h % 2

    def __call__(self, h):
        j = (h // self.group) // 2
        slab = self.cat[:, LANES * j:LANES * (j + 1)]
        if not self.is_swapped(h):
            return slab
        if j not in self.swapped:
            self.swapped[j] = jnp.concatenate([slab[:, HEAD_DIM:], slab[:, :HEAD_DIM]], axis=1)
        return self.swapped[j]


class _BandSteps:
    def __init__(self, seq, dil, nq):
        self.nq, self.rows, self.consecutive = nq, nq * BLOCK, dil == 1
        nb = seq // dil // BLOCK
        if self.consecutive:
            assert nb % nq == 0
            self.outer, self.inner, self.stride = 1, nb // nq, 1
        else:
            assert dil % nq == 0
            self.outer, self.inner, self.stride = dil // nq, nb, dil // nq

    def own(self, w, clamp=False):
        cur = (lambda i: jnp.minimum(i, self.inner - 1)) if clamp else (lambda i: i)
        return pl.BlockSpec((self.rows, w), lambda r, i: (cur(i) * self.stride + r, 0))

    def prev(self, w, clamp=False):
        cur = (lambda i: jnp.minimum(i, self.inner - 1)) if clamp else (lambda i: i)
        if self.consecutive:
            return pl.BlockSpec((BLOCK, w), lambda r, i: (jnp.maximum(cur(i) * self.nq - 1, 0), 0))
        return pl.BlockSpec((self.rows, w), lambda r, i: (jnp.maximum(cur(i) - 1, 0) * self.stride + r, 0))

    def late(self, w):
        return pl.BlockSpec((self.rows, w), lambda r, i: (jnp.maximum(i - 1, 0) * self.stride + r, 0))

    def rows_of(self, j):
        return slice(BLOCK * j, BLOCK * (j + 1))

    def keys(self, p_ref, c_ref, j):
        if not self.consecutive:
            before = p_ref[self.rows_of(j), :]
        elif j == 0:
            before = p_ref[...]
        else:
            before = c_ref[self.rows_of(j - 1), :]
        return jnp.concatenate([before, c_ref[self.rows_of(j), :]], axis=0)

    def has_prev(self, i, j):
        return True if (self.consecutive and j > 0) else (i > 0)


def _banded_fwd(q, k, v, sink, *, dil, heads, group, max_dist, nq, name):
    seq = q.shape[0]
    kvh = heads // group
    qw, kw = heads * HEAD_DIM, kvh * HEAD_DIM
    steps = _BandSteps(seq, dil, nq)

    def body(*refs):
        if sink is not None:
            sink_ref, refs = refs[0], refs[1:]
        q_ref, kp_ref, kc_ref, vp_ref, vc_ref, o_ref, lse_ref, s_scr, p_scr = refs
        i = pl.program_id(1)
        lane = lax.broadcasted_iota(jnp.int32, (BLOCK, LANES), 1)
        k_of = [_KvSlabs(steps.keys(kp_ref, kc_ref, j), group) for j in range(nq)]
        v_of = [_KvSlabs(steps.keys(vp_ref, vc_ref, j), group) for j in range(nq)]
        for j in range(nq):
            qv = q_ref[steps.rows_of(j), :]
            for h in range(heads):
                s_scr[j * heads + h] = lax.dot_general(_head_only(qv, h), k_of[j](h), _NT, preferred_element_type=F32)
        ls = {}
        for j in range(nq):
            valid = _band_mask(steps.has_prev(i, j), max_dist)
            lse_tile = jnp.zeros((BLOCK, LANES), F32)
            for h in range(heads):
                s = jnp.where(valid, s_scr[j * heads + h], NEG)
                m = jnp.max(s, axis=-1, keepdims=True)
                if sink is not None:
                    sk = sink_ref[h]
                    m = jnp.maximum(m, sk)
                p = jnp.exp(s - m)
                l = jnp.sum(p, axis=-1, keepdims=True)
                if sink is not None:
                    l = l + jnp.exp(sk - m)
                p_scr[j * heads + h] = p.astype(BF16)
                ls[j, h] = l
                lse_tile = jnp.where(lane == h, m + jnp.log(l), lse_tile)
            lse_ref[steps.rows_of(j), :] = lse_tile
        for j in range(nq):
            for pr in range(heads // 2):
                he, ho = 2 * pr, 2 * pr + 1
                even = jnp.dot(p_scr[j * heads + he], v_of[j](he), preferred_element_type=F32) / ls[j, he]
                odd = jnp.dot(p_scr[j * heads + ho], v_of[j](ho), preferred_element_type=F32) / ls[j, ho]
                o_ref[steps.rows_of(j), LANES * pr:LANES * (pr + 1)] = jnp.where(lane < HEAD_DIM, even, odd).astype(BF16)

    in_specs = [steps.own(qw), steps.prev(kw), steps.own(kw), steps.prev(kw), steps.own(kw)]
    args = [q, k, k, v, v]
    if sink is not None:
        in_specs = [pl.BlockSpec(memory_space=pltpu.SMEM)] + in_specs
        args = [sink] + args
    return pl.pallas_call(
        body, name=name, grid=(steps.outer, steps.inner), in_specs=in_specs,
        out_specs=[steps.own(qw), steps.own(LANES)],
        out_shape=[jax.ShapeDtypeStruct((seq, qw), BF16), jax.ShapeDtypeStruct((seq, LANES), F32)],
        scratch_shapes=[pltpu.VMEM((nq * heads, BLOCK, 2 * BLOCK), F32), pltpu.VMEM((nq * heads, BLOCK, 2 * BLOCK), BF16)],
        compiler_params=_params(dimension_semantics=("arbitrary", "arbitrary")),
    )(*args)


def _banded_bwd(q, k, v, d_out, stat, sink, *, dil, heads, group, max_dist, nq, name):
    seq = q.shape[0]
    kvh = heads // group
    qw, kw = heads * HEAD_DIM, kvh * HEAD_DIM
    steps = _BandSteps(seq, dil, nq)
    n_in = 7

    def body(*refs):
        if sink is not None:
            sink_ref, refs = refs[0], refs[1:]
            dsink_ref, refs = refs[n_in], refs[:n_in] + refs[n_in + 1:]
        (q_ref, kp_ref, kc_ref, vp_ref, vc_ref, do_ref, st_ref, dq_ref, dk_ref, dv_ref,
         kcar, vcar, s_scr, dp_scr, p_scr, ds_scr) = refs
        r, i = pl.program_id(0), pl.program_id(1)

        @pl.when(i == 0)
        def _():
            kcar[...] = jnp.zeros_like(kcar)
            vcar[...] = jnp.zeros_like(vcar)

        if sink is not None:
            @pl.when((i == 0) & (r == 0))
            def _():
                dsink_ref[...] = jnp.zeros_like(dsink_ref)

        @pl.when(i < steps.inner)
        def _():
            lane = lax.broadcasted_iota(jnp.int32, (1, LANES), 1)
            lane_q = lax.broadcasted_iota(jnp.int32, (BLOCK, LANES), 1)
            k_of = [_KvSlabs(steps.keys(kp_ref, kc_ref, j), group) for j in range(nq)]
            v_of = [_KvSlabs(steps.keys(vp_ref, vc_ref, j), group) for j in range(nq)]
            qms, doms = {}, {}
            for j in range(nq):
                qv, dov = q_ref[steps.rows_of(j), :], do_ref[steps.rows_of(j), :]
                for h in range(heads):
                    qms[j, h], doms[j, h] = _head_only(qv, h), _head_only(dov, h)
                    s_scr[j * heads + h] = lax.dot_general(qms[j, h], k_of[j](h), _NT, preferred_element_type=F32)
                    dp_scr[j * heads + h] = lax.dot_general(doms[j, h], v_of[j](h), _NT, preferred_element_type=F32)
            dsink_row = jnp.zeros((1, LANES), F32)
            for j in range(nq):
                st = st_ref[steps.rows_of(j), :]
                valid = _band_mask(steps.has_prev(i, j), max_dist)
                for h in range(heads):
                    lse_h = st[:, h:h + 1]
                    delta = st[:, DELTA_LANE + h:DELTA_LANE + h + 1]
                    p = jnp.where(valid, jnp.exp(s_scr[j * heads + h] - lse_h), 0.0)
                    p_scr[j * heads + h] = p.astype(BF16)
                    ds_scr[j * heads + h] = (p * (dp_scr[j * heads + h] - delta)).astype(BF16)
                    if sink is not None:
                        ds_sink = jnp.sum(-jnp.exp(sink_ref[h] - lse_h) * delta, axis=0, keepdims=True)
                        dsink_row = dsink_row + jnp.where(lane == h, ds_sink, 0.0)
            for j in range(nq):
                for pr in range(heads // 2):
                    he, ho = 2 * pr, 2 * pr + 1
                    even = jnp.dot(ds_scr[j * heads + he], k_of[j](he), preferred_element_type=F32)
                    odd = jnp.dot(ds_scr[j * heads + ho], k_of[j](ho), preferred_element_type=F32)
                    dq_ref[steps.rows_of(j), LANES * pr:LANES * (pr + 1)] = (
                        jnp.where(lane_q < HEAD_DIM, even, odd).astype(BF16))
            if steps.consecutive:
                dk_ref[...] = kcar[...].astype(BF16)
                dv_ref[...] = vcar[...].astype(BF16)
            for j in range(nq):
                for slab in range(kw // LANES):
                    acc = {}
                    for h in range(heads):
                        if (h // group) // 2 != slab:
                            continue
                        key = k_of[j].is_swapped(h)
                        dk_h = lax.dot_general(ds_scr[j * heads + h], qms[j, h], _TN, preferred_element_type=F32)
                        dv_h = lax.dot_general(p_scr[j * heads + h], doms[j, h], _TN, preferred_element_type=F32)
                        acc[key] = (dk_h, dv_h) if key not in acc else (acc[key][0] + dk_h, acc[key][1] + dv_h)
                    dk_j, dv_j = acc.get(False, (None, None))
                    if True in acc:
                        unswap = lambda t: jnp.concatenate([t[:, HEAD_DIM:], t[:, :HEAD_DIM]], axis=1)
                        dk_s, dv_s = unswap(acc[True][0]), unswap(acc[True][1])
                        dk_j = dk_s if dk_j is None else dk_j + dk_s
                        dv_j = dv_s if dv_j is None else dv_j + dv_s
                    sl = slice(LANES * slab, LANES * (slab + 1))
                    own_rows = steps.rows_of(j)
                    if not steps.consecutive:
                        dk_ref[own_rows, sl] = (kcar[own_rows, sl] + dk_j[0:BLOCK]).astype(BF16)
                        dv_ref[own_rows, sl] = (vcar[own_rows, sl] + dv_j[0:BLOCK]).astype(BF16)
                    elif j == 0:
                        last = steps.rows_of(nq - 1)
                        dk_ref[last, sl] = (kcar[last, sl] + dk_j[0:BLOCK]).astype(BF16)
                        dv_ref[last, sl] = (vcar[last, sl] + dv_j[0:BLOCK]).astype(BF16)
                    else:
                        before = steps.rows_of(j - 1)
                        kcar[before, sl] += dk_j[0:BLOCK]
                        vcar[before, sl] += dv_j[0:BLOCK]
                    kcar[own_rows, sl] = dk_j[BLOCK:2 * BLOCK]
                    vcar[own_rows, sl] = dv_j[BLOCK:2 * BLOCK]
            if sink is not None:
                dsink_ref[0:1, :] += dsink_row

        @pl.when(i == steps.inner)
        def _():
            dk_ref[...] = kcar[...].astype(BF16)
            dv_ref[...] = vcar[...].astype(BF16)

    own, prev = (lambda w: steps.own(w, clamp=True)), (lambda w: steps.prev(w, clamp=True))
    in_specs = [own(qw), prev(kw), own(kw), prev(kw), own(kw), own(qw), own(LANES)]
    args = [q, k, k, v, v, d_out, stat]
    out_specs = [own(qw), steps.late(kw), steps.late(kw)]
    out_shape = [jax.ShapeDtypeStruct((seq, qw), BF16), jax.ShapeDtypeStruct((seq, kw), BF16),
                 jax.ShapeDtypeStruct((seq, kw), BF16)]
    if sink is not None:
        in_specs = [pl.BlockSpec(memory_space=pltpu.SMEM)] + in_specs
        args = [sink] + args
        out_specs = [_full((8, LANES))] + out_specs
        out_shape = [jax.ShapeDtypeStruct((8, LANES), F32)] + out_shape
    n_hb = nq * heads
    res = pl.pallas_call(
        body, name=name, grid=(steps.outer, steps.inner + 1), in_specs=in_specs, out_specs=out_specs,
        out_shape=out_shape,
        scratch_shapes=[pltpu.VMEM((steps.rows, kw), F32), pltpu.VMEM((steps.rows, kw), F32)]
        + [pltpu.VMEM((n_hb, BLOCK, 2 * BLOCK), F32)] * 2 + [pltpu.VMEM((n_hb, BLOCK, 2 * BLOCK), BF16)] * 2,
        compiler_params=_params(dimension_semantics=("arbitrary", "arbitrary")),
    )(*args)
    if sink is not None:
        return res[1], res[2], res[3], res[0]
    return res


def _cross_fwd(q, mk, mv, tq=512):
    seq = q.shape[0]

    def body(q_ref, mk_ref, mv_ref, o_ref, lse_ref, s_scr, p_scr):
        qv = q_ref[...]
        k_of, v_of = _KvSlabs(mk_ref[...], 1), _KvSlabs(mv_ref[...], 1)
        lane = lax.broadcasted_iota(jnp.int32, (tq, LANES), 1)
        lse_tile = jnp.zeros((tq, LANES), F32)
        for h in range(C_HEADS):
            s_scr[h] = lax.dot_general(_head_only(qv, h), k_of(h), _NT, preferred_element_type=F32)
        ls = []
        for h in range(C_HEADS):
            s = s_scr[h]
            m = jnp.max(s, axis=-1, keepdims=True)
            p = jnp.exp(s - m)
            l = jnp.sum(p, axis=-1, keepdims=True)
            p_scr[h] = p.astype(BF16)
            ls.append(l)
            lse_tile = jnp.where(lane == h, m + jnp.log(l), lse_tile)
        for pr in range(C_HEADS // 2):
            even = jnp.dot(p_scr[2 * pr], v_of(2 * pr), preferred_element_type=F32) / ls[2 * pr]
            odd = jnp.dot(p_scr[2 * pr + 1], v_of(2 * pr + 1), preferred_element_type=F32) / ls[2 * pr + 1]
            o_ref[:, LANES * pr:LANES * (pr + 1)] = jnp.where(lane < HEAD_DIM, even, odd).astype(BF16)
        lse_ref[...] = lse_tile

    return pl.pallas_call(
        body, name="cross_fwd", grid=(seq // tq,),
        in_specs=[_row(tq, C_W), _full((N_MEM, C_W)), _full((N_MEM, C_W))],
        out_specs=[_row(tq, C_W), _row(tq, LANES)],
        out_shape=[jax.ShapeDtypeStruct((seq, C_W), BF16), jax.ShapeDtypeStruct((seq, LANES), F32)],
        scratch_shapes=[pltpu.VMEM((C_HEADS, tq, N_MEM), F32), pltpu.VMEM((C_HEADS, tq, N_MEM), BF16)],
        compiler_params=_params(dimension_semantics=("arbitrary",)),
    )(q, mk, mv)


def _cross_bwd(q, mk, mv, d_out, stat, tq=512):
    seq = q.shape[0]

    def body(q_ref, mk_ref, mv_ref, do_ref, st_ref, dq_ref, dmk_ref, dmv_ref, s_scr, dp_scr, p_scr, ds_scr):
        @pl.when(pl.program_id(0) == 0)
        def _():
            dmk_ref[...] = jnp.zeros_like(dmk_ref)
            dmv_ref[...] = jnp.zeros_like(dmv_ref)

        qv, dov, st = q_ref[...], do_ref[...], st_ref[...]
        k_of, v_of = _KvSlabs(mk_ref[...], 1), _KvSlabs(mv_ref[...], 1)
        qms = [_head_only(qv, h) for h in range(C_HEADS)]
        doms = [_head_only(dov, h) for h in range(C_HEADS)]
        for h in range(C_HEADS):
            s_scr[h] = lax.dot_general(qms[h], k_of(h), _NT, preferred_element_type=F32)
            dp_scr[h] = lax.dot_general(doms[h], v_of(h), _NT, preferred_element_type=F32)
        for h in range(C_HEADS):
            p = jnp.exp(s_scr[h] - st[:, h:h + 1])
            p_scr[h] = p.astype(BF16)
            ds_scr[h] = (p * (dp_scr[h] - st[:, DELTA_LANE + h:DELTA_LANE + h + 1])).astype(BF16)
        lane = lax.broadcasted_iota(jnp.int32, (tq, LANES), 1)
        for pr in range(C_HEADS // 2):
            sl = slice(LANES * pr, LANES * (pr + 1))
            even = jnp.dot(ds_scr[2 * pr], k_of(2 * pr), preferred_element_type=F32)
            odd = jnp.dot(ds_scr[2 * pr + 1], k_of(2 * pr + 1), preferred_element_type=F32)
            dq_ref[:, sl] = jnp.where(lane < HEAD_DIM, even, odd).astype(BF16)
            dmk_ref[:, sl] += (lax.dot_general(ds_scr[2 * pr], qms[2 * pr], _TN, preferred_element_type=F32)
                               + lax.dot_general(ds_scr[2 * pr + 1], qms[2 * pr + 1], _TN, preferred_element_type=F32))
            dmv_ref[:, sl] += (lax.dot_general(p_scr[2 * pr], doms[2 * pr], _TN, preferred_element_type=F32)
                               + lax.dot_general(p_scr[2 * pr + 1], doms[2 * pr + 1], _TN, preferred_element_type=F32))

    return pl.pallas_call(
        body, name="cross_bwd", grid=(seq // tq,),
        in_specs=[_row(tq, C_W), _full((N_MEM, C_W)), _full((N_MEM, C_W)), _row(tq, C_W), _row(tq, LANES)],
        out_specs=[_row(tq, C_W), _full((N_MEM, C_W)), _full((N_MEM, C_W))],
        out_shape=[jax.ShapeDtypeStruct((seq, C_W), BF16), jax.ShapeDtypeStruct((N_MEM, C_W), F32),
                   jax.ShapeDtypeStruct((N_MEM, C_W), F32)],
        scratch_shapes=[pltpu.VMEM((C_HEADS, tq, N_MEM), F32)] * 2 + [pltpu.VMEM((C_HEADS, tq, N_MEM), BF16)] * 2,
        compiler_params=_params(dimension_semantics=("arbitrary",)),
    )(q, mk, mv, d_out, stat)


def _per_head(tile, width):
    rows = tile.shape[0]
    return jnp.concatenate(
        [jnp.broadcast_to(tile[:, h:h + 1], (rows, HEAD_DIM)) for h in range(width // HEAD_DIM)], axis=1)


def _with_delta(lse_tile, prod):
    rows = lse_tile.shape[0]
    lane = lax.broadcasted_iota(jnp.int32, (rows, LANES), 1)
    tile = lse_tile
    for h in range(prod.shape[1] // HEAD_DIM):
        d = jnp.sum(prod[:, HEAD_DIM * h:HEAD_DIM * (h + 1)], axis=-1, keepdims=True)
        tile = jnp.where(lane == DELTA_LANE + h, d, tile)
    return tile


def _mid(oa, lse_a, ob, lse_b, oc, lse_c, gate, x, target, w_out_full, post_g, tm=256):
    seq = x.shape[0]
    n_b = B_W // LANES

    def body(oa_ref, la_ref, b1_ref, l1_ref, b4_ref, l4_ref, b16_ref, l16_ref, oc_ref, lc_ref,
             gate_ref, x_ref, t_ref, w_ref, pg_ref,
             dh_ref, dg_ref, doa_ref, sa_ref, dob1_ref, sb1_ref, dob4_ref, sb4_ref, dob16_ref, sb16_ref,
             doc_ref, sc_ref, dw_ref, st_ref, scr_b4, scr_b16, scr_l4, scr_l16, scr_do, scr_sb):
        @pl.when(pl.program_id(0) == 0)
        def _():
            dw_ref[...] = jnp.zeros_like(dw_ref)
            st_ref[...] = jnp.zeros_like(st_ref)

        b1, l1 = b1_ref[...].astype(F32), l1_ref[...]
        b4, l4 = _load_permuted(b4_ref, scr_b4, 4), _load_permuted(l4_ref, scr_l4, 4)
        b16, l16 = _load_permuted(b16_ref, scr_b16, 16), _load_permuted(l16_ref, scr_l16, 16)
        lm = jnp.maximum(jnp.maximum(l1, l4), l16)
        e1, e4, e16 = jnp.exp(l1 - lm), jnp.exp(l4 - lm), jnp.exp(l16 - lm)
        den = e1 + e4 + e16
        lse_b_tile = lm + jnp.log(den)
        ob_v = _per_head(e1 / den, B_W) * b1 + _per_head(e4 / den, B_W) * b4 + _per_head(e16 / den, B_W) * b16
        o_all = jnp.concatenate([oa_ref[...].astype(F32), ob_v, oc_ref[...].astype(F32)], axis=1)
        g = gate_ref[...].astype(F32)
        sig = 1.0 / (1.0 + jnp.exp(-g))
        silu = g * sig
        y = (o_all * silu).astype(BF16)
        w = w_ref[...]
        z = jnp.dot(y, w, preferred_element_type=F32)
        rz = lax.rsqrt(jnp.mean(z * z, axis=-1, keepdims=True) + RMS_EPS)
        hn = z * rz
        pg = pg_ref[...]
        err = (x_ref[...] + hn * pg) - t_ref[...]
        loss = 0.5 * jnp.sum(jnp.mean(err * err, axis=-1, keepdims=True), axis=0, keepdims=True)
        dh = err * (1.0 / D_MODEL)
        dh_ref[...] = dh.astype(BF16)
        st_ref[0:1, :] += jnp.sum(dh * hn, axis=0, keepdims=True)
        st_ref[1:2, :] += jnp.broadcast_to(loss, (1, D_MODEL))
        dhn = dh * pg
        dz = (rz * (dhn - hn * jnp.mean(dhn * hn, axis=-1, keepdims=True))).astype(BF16)
        dy = lax.dot_general(dz, w, (((1,), (1,)), ((), ())), preferred_element_type=F32)
        dw_ref[...] += lax.dot_general(y, dz, (((0,), (0,)), ((), ())), preferred_element_type=F32)
        dg_ref[...] = (dy * o_all * (sig * (1.0 + g * (1.0 - sig)))).astype(BF16)
        d_o = (dy * silu).astype(BF16)
        prod = d_o.astype(F32) * o_all
        doa_ref[...] = d_o[:, 0:A_W]
        sa_ref[...] = _with_delta(la_ref[...], prod[:, 0:A_W])
        doc_ref[...] = d_o[:, A_W + B_W:D_MODEL]
        sc_ref[...] = _with_delta(lc_ref[...], prod[:, A_W + B_W:D_MODEL])
        d_ob = d_o[:, A_W:A_W + B_W]
        stat_b = _with_delta(lse_b_tile, prod[:, A_W:A_W + B_W])
        dob1_ref[...] = d_ob
        sb1_ref[...] = stat_b
        _put(scr_do, d_ob.astype(F32))
        _put(scr_sb, stat_b)
        _store_permuted(scr_do, dob4_ref, 4, BF16)
        _store_permuted(scr_sb, sb4_ref, 4, F32)
        _store_permuted(scr_do, dob16_ref, 16, BF16)
        _store_permuted(scr_sb, sb16_ref, 16, F32)

    p4 = lambda w: _perm_spec(tm, 4, w)
    p16 = lambda w: _perm_spec(tm, 16, w)
    in_specs = [_row(tm, A_W), _row(tm, LANES), _row(tm, B_W), _row(tm, LANES), p4(B_W), p4(LANES), p16(B_W), p16(LANES),
                _row(tm, C_W), _row(tm, LANES), _row(tm, D_MODEL), _row(tm, D_MODEL), _row(tm, D_MODEL),
                _full((D_MODEL, D_MODEL)), _full((1, D_MODEL))]
    sds = jax.ShapeDtypeStruct
    v4 = lambda w, dt: sds((seq // (BLOCK * 4), 4, BLOCK, w), dt)
    v16 = lambda w, dt: sds((seq // (BLOCK * 16), 16, BLOCK, w), dt)
    out_specs = [_row(tm, D_MODEL), _row(tm, D_MODEL), _row(tm, A_W), _row(tm, LANES), _row(tm, B_W), _row(tm, LANES),
                 p4(B_W), p4(LANES), p16(B_W), p16(LANES), _row(tm, C_W), _row(tm, LANES),
                 _full((D_MODEL, D_MODEL)), _full((8, D_MODEL))]
    out_shape = [sds((seq, D_MODEL), BF16), sds((seq, D_MODEL), BF16), sds((seq, A_W), BF16), sds((seq, LANES), F32),
                 sds((seq, B_W), BF16), sds((seq, LANES), F32), v4(B_W, BF16), v4(LANES, F32), v16(B_W, BF16),
                 v16(LANES, F32), sds((seq, C_W), BF16), sds((seq, LANES), F32),
                 sds((D_MODEL, D_MODEL), F32), sds((8, D_MODEL), F32)]
    res = pl.pallas_call(
        body, name="mid", grid=(seq // tm,), in_specs=in_specs, out_specs=out_specs, out_shape=out_shape,
        scratch_shapes=[pltpu.VMEM((n_b, tm, LANES), F32), pltpu.VMEM((n_b, tm, LANES), F32),
                        pltpu.VMEM((1, tm, LANES), F32), pltpu.VMEM((1, tm, LANES), F32),
                        pltpu.VMEM((n_b, tm, LANES), F32), pltpu.VMEM((1, tm, LANES), F32)],
        compiler_params=_params(dimension_semantics=("arbitrary",)),
    )(oa, lse_a, ob[1], lse_b[1], _perm_view(ob[4], 4), _perm_view(lse_b[4], 4), _perm_view(ob[16], 16),
      _perm_view(lse_b[16], 16), oc, lse_c, gate, x, target, w_out_full, post_g)
    dh, d_gate, do_a, st_a, do_b1, st_b1, do_b4, st_b4, do_b16, st_b16, do_c, st_c, d_wout, stats = res
    flat = lambda t: t.reshape(seq, t.shape[-1])
    d_b = {1: (do_b1, st_b1), 4: (flat(do_b4), flat(st_b4)), 16: (flat(do_b16), flat(st_b16))}
    return dh, d_gate, (do_a, st_a), d_b, (do_c, st_c), d_wout, stats


def _inproj_bwd_x(x, dh, pre_g, w_in_full, tabs, dqa, dka, dva, dqkv_b, dqc, dgate, tm=256):
    seq = x.shape[0]
    n_b = B_W // LANES

    def body(x_ref, dh_ref, g_ref, w_hbm, c_ref, up_ref, dn_ref, dqa_ref, dka_ref, dva_ref,
             dq1, dk1, dv1, dq4, dk4, dv4, dq16, dk16, dv16, dqc_ref, dg_ref,
             gx_ref, dp_ref, st_ref, scr4, scr16, w_scr, w_sems):
        _stage_w_in(w_hbm, w_scr, w_sems)

        @pl.when(pl.program_id(0) == 0)
        def _():
            st_ref[...] = jnp.zeros_like(st_ref)

        c, up, dn = c_ref[...], -up_ref[...], -dn_ref[...]
        unrot = lambda t: _rotate(t, c, up, dn)
        total = lambda r1, r4, r16: (r1[...].astype(F32) + _load_permuted(r4, scr4, 4)
                                     + _load_permuted(r16, scr16, 16))
        dp_ref[:, 0:384] = (unrot(dqa_ref[...].astype(F32)) * SCALE).astype(BF16)
        dp_ref[:, 384:512] = unrot(dka_ref[...].astype(F32)).astype(BF16)
        dp_ref[:, 512:640] = dva_ref[...]
        dp_ref[:, 640:1024] = dg_ref[:, 0:384]
        dp_ref[:, 1024:1408] = (unrot(total(dq1, dq4, dq16)) * SCALE).astype(BF16)
        dp_ref[:, 1408:1792] = unrot(total(dk1, dk4, dk16)).astype(BF16)
        dp_ref[:, 1792:2176] = total(dv1, dv4, dv16).astype(BF16)
        dp_ref[:, 2176:2560] = dg_ref[:, 384:768]
        dp_ref[:, 2560:2816] = (dqc_ref[...].astype(F32) * SCALE).astype(BF16)
        dp_ref[:, 2816:3072] = dg_ref[:, 768:1024]
        du = lax.dot_general(dp_ref[...], w_scr[...], _NT, preferred_element_type=F32)
        xv = x_ref[...]
        r = lax.rsqrt(jnp.mean(xv * xv, axis=-1, keepdims=True) + RMS_EPS)
        xh = xv * r
        st_ref[0:1, :] += jnp.sum(du * xh, axis=0, keepdims=True)
        dxh = du * g_ref[...]
        gx_ref[...] = dh_ref[...].astype(F32) + r * (dxh - xh * jnp.mean(dxh * xh, axis=-1, keepdims=True))

    in_specs = ([_row(tm, D_MODEL), _row(tm, D_MODEL), _full((1, D_MODEL)), pl.BlockSpec(memory_space=pl.ANY),
                 _row(tm, LANES), _row(tm, LANES), _row(tm, LANES), _row(tm, A_W), _row(tm, A_KV_W), _row(tm, A_KV_W)]
                + [_row(tm, B_W)] * 3 + [_perm_spec(tm, 4, B_W)] * 3 + [_perm_spec(tm, 16, B_W)] * 3
                + [_row(tm, C_W), _row(tm, D_MODEL)])
    return pl.pallas_call(
        body, name="inproj_bwd_x", grid=(seq // tm,), in_specs=in_specs,
        out_specs=[_row(tm, D_MODEL), _row(tm, D_IN), _full((8, D_MODEL))],
        out_shape=[jax.ShapeDtypeStruct((seq, D_MODEL), F32), jax.ShapeDtypeStruct((seq, D_IN), BF16),
                   jax.ShapeDtypeStruct((8, D_MODEL), F32)],
        scratch_shapes=[pltpu.VMEM((n_b, tm, LANES), F32), pltpu.VMEM((n_b, tm, LANES), F32)] + _w_in_scratch(),
        compiler_params=_params(dimension_semantics=("arbitrary",)),
    )(x, dh, pre_g, w_in_full, *tabs, dqa, dka, dva, *dqkv_b[1], *[_perm_view(t, 4) for t in dqkv_b[4]],
      *[_perm_view(t, 16) for t in dqkv_b[16]], dqc, dgate)


def _inproj_bwd_w(u, dproj, tm=1024):
    seq = u.shape[0]

    def body(u_ref, dp_ref, dw_ref):
        @pl.when(pl.program_id(0) == 0)
        def _():
            dw_ref[...] = jnp.zeros_like(dw_ref)

        res = lax.dot_general(u_ref[...], dp_ref[...], _TN, preferred_element_type=F32)
        for k in range(N_DEV):
            dw_ref[k] += res[:, SHARD_IN * k:SHARD_IN * (k + 1)]

    return pl.pallas_call(
        body, name="inproj_bwd_w", grid=(seq // tm,),
        in_specs=[_row(tm, D_MODEL), _row(tm, D_IN)],
        out_specs=_full((N_DEV, D_MODEL, SHARD_IN)),
        out_shape=jax.ShapeDtypeStruct((N_DEV, D_MODEL, SHARD_IN), F32),
        compiler_params=_params(dimension_semantics=("arbitrary",)),
    )(u, dproj)


def _local_step(x, mem, pre_g, w_in_full, sink, mem_g, w_mem_full, w_out_full, post_g, target):
    seq = x.shape[0]
    tabs = _rope_tables(seq)
    u, qa, ka, va, qkv_b, qc, gate = _inproj(x, pre_g, w_in_full, tabs)
    mn, mk, mv = _memkv_fwd(mem, mem_g, w_mem_full)

    a_cfg = dict(dil=1, heads=A_HEADS, group=A_GROUP, max_dist=BLOCK - 1, nq=ATTN_BLOCKS_PER_STEP)
    b_cfgs = {dil: dict(dil=dil, heads=B_HEADS, group=1, max_dist=win // dil, nq=ATTN_BLOCKS_PER_STEP)
              for win, dil in B_CONFIGS}
    oa, lse_a = _banded_fwd(qa, ka, va, sink, name="attn_a_fwd", **a_cfg)
    ob, lse_b = {}, {}
    for dil, cfg in b_cfgs.items():
        ob[dil], lse_b[dil] = _banded_fwd(*qkv_b[dil], None, name=f"attn_b{dil}_fwd", **cfg)
    oc, lse_c = _cross_fwd(qc, mk, mv)

    dh, d_gate, d_a, d_b, d_c, d_wout, st_mid = _mid(oa, lse_a, ob, lse_b, oc, lse_c, gate, x, target, w_out_full, post_g)

    dqa, dka, dva, dsink = _banded_bwd(qa, ka, va, *d_a, sink, name="attn_a_bwd", **a_cfg)
    dqkv_b = {dil: _banded_bwd(*qkv_b[dil], *d_b[dil], None, name=f"attn_b{dil}_bwd", **cfg)
              for dil, cfg in b_cfgs.items()}
    dqc, dmk, dmv = _cross_bwd(qc, mk, mv, *d_c)
    d_wmem, st_mem = _memkv_bwd(mem, mem_g, mn, w_mem_full, dmk, dmv)

    grad_x, dproj, st_pre = _inproj_bwd_x(x, dh, pre_g, w_in_full, tabs, dqa, dka, dva, dqkv_b, dqc, d_gate)
    d_win = _inproj_bwd_w(u, dproj)

    dsink_row = jnp.pad(dsink[0:1, :], ((0, 0), (0, D_MODEL - LANES)))
    stats = jnp.concatenate([st_pre[0:1], st_mem[0:1], st_mid[0:1], dsink_row, st_mid[1:2],
                             jnp.zeros((3, D_MODEL), F32)], axis=0)
    return grad_x, d_win, d_wmem, d_wout, stats


def _mesh_pos():
    return lax.axis_index("x"), lax.axis_index("y"), lax.axis_index("c")


def _all_gather_weights(w_in, w_mem, w_out):
    def body(win_ref, wm_ref, wo_ref, win_out, wm_out, wo_out, win_b, wm_b, wo_b, send_sems, recv_sems, local_sems):
        x, y, c = _mesh_pos()
        win_b[...] = win_ref[...].astype(BF16)
        wm_b[...] = wm_ref[...].astype(BF16)
        wo_b[...] = wo_ref[...].astype(BF16)
        srcs = (win_b, wm_b, wo_b)
        outs = (win_out, wm_out, wo_out)
        me, sibling = (x, y, c), (x, y, 1 - c)
        chips = [(1 - x, y), (x, 1 - y), (1 - x, 1 - y)]

        def slot(a, p):
            return outs[a].at[4 * p[0] + 2 * p[1] + p[2]]

        def copy(a, k, block, to, src=None):
            return pltpu.make_async_remote_copy(
                src_ref=slot(a, block) if src is None else src, dst_ref=slot(a, block),
                send_sem=send_sems.at[a, k], recv_sem=recv_sems.at[a, k], device_id=to, device_id_type=MESH_ID)

        mine = [pltpu.make_async_copy(srcs[a], slot(a, me), local_sems.at[a]) for a in range(3)]
        for cp in mine:
            cp.start()
        first = []
        for a in range(3):
            first.append(copy(a, 0, me, sibling, src=srcs[a]))
            first += [copy(a, 1 + j, me, (*chip, c), src=srcs[a]) for j, chip in enumerate(chips)]
        for cp in first:
            cp.start()
        passed = []
        for j, chip in enumerate(chips):
            for a in range(3):
                copy(a, 1 + j, (*chip, c), me).wait_recv()
                fwd = copy(a, 4 + j, (*chip, c), sibling)
                fwd.start()
                passed.append(fwd)
        for a in range(3):
            copy(a, 0, sibling, me).wait_recv()
            for j, chip in enumerate(chips):
                copy(a, 4 + j, (*chip, 1 - c), me).wait_recv()
        for cp in first + passed:
            cp.wait_send()
        for cp in mine:
            cp.wait()

    shapes = [(D_MODEL, SHARD_IN), (SHARD_ROWS, 2 * C_W), (SHARD_ROWS, D_MODEL)]
    vmem = pl.BlockSpec(memory_space=pltpu.VMEM)
    hbm = pl.BlockSpec(memory_space=pl.ANY)
    return pl.pallas_call(
        body, name="all_gather_weights",
        in_specs=[vmem] * 3, out_specs=[hbm] * 3,
        out_shape=[jax.ShapeDtypeStruct((N_DEV,) + s, BF16) for s in shapes],
        scratch_shapes=[pltpu.VMEM(s, BF16) for s in shapes]
        + [pltpu.SemaphoreType.DMA((3, 7)), pltpu.SemaphoreType.DMA((3, 7)), pltpu.SemaphoreType.DMA((3,))],
        compiler_params=_params(),
    )(w_in, w_mem, w_out)


def _exchange_grads(d_win, d_wmem, d_wout, stats):
    shapes = [d_win.shape[1:], d_wmem.shape[1:], d_wout.shape[1:]]
    n_arr = len(shapes)

    def body(win, wm, wo, st, g_win, g_wm, g_wo, r_st, *scratch):
        mine, got, snd, rcv = (scratch[n_arr * t:n_arr * (t + 1)] for t in range(4))
        load_sems, d2d_send, d2d_recv, ici_send, ici_recv, st_send, st_recv, st_local = scratch[4 * n_arr:]
        x, y, c = _mesh_pos()
        me = 4 * x + 2 * y + c
        ins, outs = (win, wm, wo), (g_win, g_wm, g_wo)

        def chip(kk):
            return (1 - x if kk & 2 else x, 1 - y if kk & 1 else y)

        st_own = pltpu.make_async_copy(st, r_st.at[me], st_local)
        st_own.start()
        st_copies = []
        for s in range(1, N_DEV):
            peer = (1 - x if s & 4 else x, 1 - y if s & 2 else y, 1 - c if s & 1 else c)
            pid = 4 * peer[0] + 2 * peer[1] + peer[2]
            cp = pltpu.make_async_remote_copy(src_ref=st, dst_ref=r_st.at[me], send_sem=st_send.at[s],
                                              recv_sem=st_recv.at[s], device_id=peer, device_id_type=MESH_ID)
            cp.start()
            arrive = pltpu.make_async_remote_copy(src_ref=st, dst_ref=r_st.at[pid], send_sem=st_send.at[s],
                                                  recv_sem=st_recv.at[s], device_id=peer, device_id_type=MESH_ID)
            st_copies.append((cp, arrive))

        loads, swaps = {}, {}
        for kk in range(4):
            ox, oy = chip(kk)
            for a in range(n_arr):
                ld = pltpu.make_async_copy(ins[a].at[4 * ox + 2 * oy + c], mine[a].at[kk], load_sems.at[a, kk])
                ld.start()
                sw = pltpu.make_async_remote_copy(
                    src_ref=ins[a].at[4 * ox + 2 * oy + (1 - c)], dst_ref=got[a].at[kk],
                    send_sem=d2d_send.at[a, kk], recv_sem=d2d_recv.at[a, kk],
                    device_id=(x, y, 1 - c), device_id_type=MESH_ID)
                sw.start()
                loads[a, kk], swaps[a, kk] = ld, sw

        hops = {}
        for kk in (3, 1, 2):
            ox, oy = chip(kk)
            for a in range(n_arr):
                loads[a, kk].wait()
                swaps[a, kk].wait_recv()
                snd[a][kk - 1] = (mine[a][kk] + got[a][kk]).astype(BF16)
                hop = pltpu.make_async_remote_copy(
                    src_ref=snd[a].at[kk - 1], dst_ref=rcv[a].at[kk - 1],
                    send_sem=ici_send.at[a, kk], recv_sem=ici_recv.at[a, kk],
                    device_id=(ox, oy, c), device_id_type=MESH_ID)
                hop.start()
                hops[a, kk] = hop

        for a in range(n_arr):
            loads[a, 0].wait()
            swaps[a, 0].wait_recv()
            acc = mine[a][0] + got[a][0]
            for kk in (1, 2, 3):
                hops[a, kk].wait_recv()
                acc = acc + rcv[a][kk - 1].astype(F32)
            outs[a][...] = acc

        for sw in swaps.values():
            sw.wait_send()
        for hop in hops.values():
            hop.wait_send()
        for cp, arrive in st_copies:
            arrive.wait_recv()
            cp.wait_send()
        st_own.wait()

    hbm = pl.BlockSpec(memory_space=pl.ANY)
    vmem = pl.BlockSpec(memory_space=pltpu.VMEM)
    out_shape = [jax.ShapeDtypeStruct(s, F32) for s in shapes] + [jax.ShapeDtypeStruct((N_DEV,) + stats.shape, F32)]
    scratch = ([pltpu.VMEM((4,) + s, F32) for s in shapes] + [pltpu.VMEM((4,) + s, F32) for s in shapes]
               + [pltpu.VMEM((3,) + s, BF16) for s in shapes] + [pltpu.VMEM((3,) + s, BF16) for s in shapes]
               + [pltpu.SemaphoreType.DMA((n_arr, 4))] * 5
               + [pltpu.SemaphoreType.DMA((N_DEV,)), pltpu.SemaphoreType.DMA((N_DEV,)), pltpu.SemaphoreType.DMA(())])
    return pl.pallas_call(
        body, name="exchange_grads", in_specs=[hbm] * 4, out_specs=[vmem] * n_arr + [hbm], out_shape=out_shape,
        scratch_shapes=scratch, compiler_params=_params(),
    )(d_win, d_wmem, d_wout, stats)


def _reduce_adamw(recv, w, m, v, *, tr, name):
    n_part, rows, cols = recv.shape

    def body(r_ref, w_ref, m_ref, v_ref, g_ref, d_ref, nm_ref, nv_ref):
        g = r_ref[0]
        for s in range(1, n_part):
            g = g + r_ref[s]
        g_ref[...] = g
        m2 = ADAM_B1 * m_ref[...] + (1.0 - ADAM_B1) * g
        v2 = ADAM_B2 * v_ref[...] + (1.0 - ADAM_B2) * (g * g)
        nm_ref[...] = m2
        nv_ref[...] = v2
        m_hat = m2 / (1.0 - ADAM_B1 ** ADAM_STEP)
        v_hat = v2 / (1.0 - ADAM_B2 ** ADAM_STEP)
        d_ref[...] = -ADAM_LR * (m_hat / (jnp.sqrt(v_hat) + ADAM_EPS) + ADAM_WD * w_ref[...])

    blk = pl.BlockSpec((tr, cols), lambda i: (i, 0))
    return pl.pallas_call(
        body, name=name, grid=(rows // tr,),
        in_specs=[pl.BlockSpec((n_part, tr, cols), lambda i: (0, i, 0)), blk, blk, blk],
        out_specs=[blk] * 4, out_shape=[jax.ShapeDtypeStruct((rows, cols), F32)] * 4,
        compiler_params=_params(dimension_semantics=("arbitrary",)),
    )(recv, w, m, v)


def _pack_rows(pre, memn, post, sink):
    sink_row = jnp.pad(sink, ((0, 0), (0, D_MODEL - A_HEADS)))
    return jnp.concatenate([pre, memn, post, sink_row, jnp.zeros((4, D_MODEL), F32)], axis=0)


def kernel(x, mem, pre_norm, w_in, sink_a, mem_norm, w_mem_kv, w_out, post_norm, loss_target, m_pre_norm, m_w_in, m_sink_a, m_mem_norm, m_w_mem_kv, m_w_out, m_post_norm, v_pre_norm, v_w_in, v_sink_a, v_mem_norm, v_w_mem_kv, v_w_out, v_post_norm):
    w_in_full, w_mem_full, w_out_full = _all_gather_weights(w_in[0], w_mem_kv[0], w_out[0])
    sink = jnp.pad(sink_a[0], (0, 8 - A_HEADS))
    grad_x, d_win, d_wmem, d_wout, stats = _local_step(
        x[0], mem[0], pre_norm, w_in_full, sink, mem_norm, w_mem_full.reshape(D_MODEL, 2 * C_W),
        w_out_full.reshape(D_MODEL, D_MODEL), post_norm, loss_target[0])
    g_win, g_wmem, g_wout, r_stats = _exchange_grads(
        d_win, d_wmem.reshape(N_DEV, SHARD_ROWS, 2 * C_W), d_wout.reshape(N_DEV, SHARD_ROWS, D_MODEL), stats)

    big = {}
    for nm, g, w, m, v in (("w_in", g_win, w_in, m_w_in, v_w_in),
                           ("w_mem_kv", g_wmem, w_mem_kv, m_w_mem_kv, v_w_mem_kv),
                           ("w_out", g_wout, w_out, m_w_out, v_w_out)):
        res = _reduce_adamw(g[None], w[0], m[0], v[0], tr=SHARD_ROWS, name="adamw_" + nm)
        big[nm] = [t[None] for t in res]
    small = _reduce_adamw(
        r_stats, _pack_rows(pre_norm, mem_norm, post_norm, sink_a),
        _pack_rows(m_pre_norm, m_mem_norm, m_post_norm, m_sink_a),
        _pack_rows(v_pre_norm, v_mem_norm, v_post_norm, v_sink_a), tr=8, name="adamw_small")

    def unpack(t):
        return {"pre_norm": t[0:1], "mem_norm": t[1:2], "post_norm": t[2:3], "sink_a": t[3:4, 0:A_HEADS]}

    order = ("pre_norm", "w_in", "sink_a", "mem_norm", "w_mem_kv", "w_out", "post_norm")
    outs = [small[0][4, 0], grad_x[None]]
    for j in range(4):
        sm = unpack(small[j])
        outs += [big[n][j] if n in big else sm[n] for n in order]
    return tuple(outs)
```

```python
import jax
import jax.numpy as jnp
from jax import lax
from jax.experimental import pallas as pl
from jax.experimental.pallas import tpu as pltpu

F32 = jnp.float32
BF16 = jnp.bfloat16

D_MODEL = 1024
HEAD_DIM = 64
ROT_DIM = 16
ROPE_THETA = 500000.0
BLOCK = 128
LANES = 128
N_MEM = 256
RMS_EPS = 1e-6
SCALE = HEAD_DIM ** -0.5
A_HEADS, A_GROUP = 6, 3
B_HEADS = 6
C_HEADS = 4
A_W, A_KV_W, B_W, C_W = 384, 128, 384, 256
D_IN = 3072
N_DEV = 8
SHARD_IN = D_IN // N_DEV
SHARD_ROWS = D_MODEL // N_DEV
B_CONFIGS = ((128, 1), (512, 4), (2048, 16))
DILS = (4, 16)
NEG = -1e30
ATTN_BLOCKS_PER_STEP = 4
DELTA_LANE = 64
VMEM_LIMIT = 56 * 1024 * 1024

ADAM_LR, ADAM_B1, ADAM_B2, ADAM_EPS, ADAM_WD, ADAM_STEP = 0.001, 0.9, 0.999, 1e-08, 0.01, 10
MESH_ID = pl.DeviceIdType.MESH


def _params(**kw):
    return pltpu.CompilerParams(vmem_limit_bytes=VMEM_LIMIT, **kw)


def _full(shape):
    n = len(shape)
    return pl.BlockSpec(shape, lambda *_: (0,) * n)


def _row(tm, w):
    return pl.BlockSpec((tm, w), lambda i: (i, 0))


def _mesh_pos():
    return lax.axis_index("x"), lax.axis_index("y"), lax.axis_index("c")


def _dev_index(pos):
    return 4 * pos[0] + 2 * pos[1] + pos[2]


def _xor_peer(pos, s):
    x, y, c = pos
    return (1 - x if s & 4 else x, 1 - y if s & 2 else y, 1 - c if s & 1 else c)


def _perm_view(a, dil):
    return a.reshape(a.shape[0] // (BLOCK * dil), dil, BLOCK, a.shape[1])


def _perm_spec(tm, dil, w):
    per = BLOCK * dil // tm
    return pl.BlockSpec((1, dil, tm // dil, w), lambda i: (i // per, 0, i % per, 0))


def _put(scr, val):
    for c in range(val.shape[1] // LANES):
        scr[c] = val[:, LANES * c:LANES * (c + 1)]


def _get(scr):
    n = scr.shape[0]
    return scr[0] if n == 1 else jnp.concatenate([scr[c] for c in range(n)], axis=1)


def _get_class(scr, r, dil):
    n, rows = scr.shape[0], scr.shape[1]
    parts = [scr.at[c][pl.ds(r, rows // dil, stride=dil), :] for c in range(n)]
    return parts[0] if n == 1 else jnp.concatenate(parts, axis=1)


def _store_permuted(scr, out_ref, dil, dtype):
    for r in range(dil):
        out_ref[0, r] = _get_class(scr, r, dil).astype(dtype)


def _fill_permuted(in_ref, scr, dil):
    n, rows = scr.shape[0], scr.shape[1]
    for r in range(dil):
        val = in_ref[0, r].astype(F32)
        for c in range(n):
            scr.at[c][pl.ds(r, rows // dil, stride=dil), :] = val[:, LANES * c:LANES * (c + 1)]


def _load_permuted(in_ref, scr, dil):
    _fill_permuted(in_ref, scr, dil)
    return _get(scr)


def _rope_tables(seq):
    j = jnp.arange(LANES) % HEAD_DIM
    freq = ROPE_THETA ** (-(2 * (j % (ROT_DIM // 2))).astype(F32) / ROT_DIM)
    ang = jnp.arange(seq, dtype=F32)[:, None] * freq[None, :]
    cos, sin = jnp.cos(ang), jnp.sin(ang)
    half = ROT_DIM // 2
    c = jnp.where(j < ROT_DIM, cos, 1.0)
    up = jnp.where((j >= half) & (j < ROT_DIM), sin, 0.0)
    dn = jnp.where(j < half, -sin, 0.0)
    return c, up, dn


def _rotate128(t, c, up, dn):
    return t * c + pltpu.roll(t, 8, 1) * up + pltpu.roll(t, LANES - 8, 1) * dn


def _rotate(t, c, up, dn):
    outs = [_rotate128(t[:, LANES * j:LANES * (j + 1)], c, up, dn) for j in range(t.shape[1] // LANES)]
    return outs[0] if len(outs) == 1 else jnp.concatenate(outs, axis=1)


def _w_in_scratch():
    return [pltpu.VMEM((D_MODEL, D_IN), BF16), pltpu.SemaphoreType.DMA((N_DEV,))]


def _stage_w_in(w_hbm, w_scr, sems):
    @pl.when(pl.program_id(0) == 0)
    def _():
        copies = [pltpu.make_async_copy(w_hbm.at[k], w_scr.at[:, pl.ds(SHARD_IN * k, SHARD_IN)], sems.at[k])
                  for k in range(N_DEV)]
        for cp in copies:
            cp.start()
        for cp in copies:
            cp.wait()


def _inproj(x, pre_g, w_in_full, tabs, w_mem, w_out, tm=512):
    seq = x.shape[0]
    n_chunk = D_IN // LANES
    n_steps = seq // tm

    def body(x_ref, g_ref, w_hbm, c_ref, up_ref, dn_ref, wm_ref, wo_ref, u_ref, qa_ref, ka_ref, va_ref,
             qb1_ref, kb1_ref, vb1_ref, qb4_ref, kb4_ref, vb4_ref, qb16_ref, kb16_ref, vb16_ref,
             qc_ref, gate_ref, wm_all, wo_all, proj, w_scr, w_sems, wm_b, wo_b, send_sems, recv_sems, local_sems):
        step = pl.program_id(0)
        shards, gathered = (wm_b, wo_b), (wm_all, wo_all)

        def gather_copies(arriving):
            pos = _mesh_pos()
            me = _dev_index(pos)
            local = [] if arriving else [
                pltpu.make_async_copy(shards[a], gathered[a].at[me], local_sems.at[a]) for a in range(2)]
            remote = []
            for s in range(1, N_DEV):
                peer = _xor_peer(pos, s)
                for a in range(2):
                    remote.append(pltpu.make_async_remote_copy(
                        src_ref=shards[a], dst_ref=gathered[a].at[_dev_index(peer) if arriving else me],
                        send_sem=send_sems.at[a, s], recv_sem=recv_sems.at[a, s], device_id=peer,
                        device_id_type=MESH_ID))
            return local, remote

        @pl.when(step == 0)
        def _():
            wm_b[...] = wm_ref[...].astype(BF16)
            wo_b[...] = wo_ref[...].astype(BF16)
            local, sends = gather_copies(arriving=False)
            for cp in local + sends:
                cp.start()

        _stage_w_in(w_hbm, w_scr, w_sems)
        xv = x_ref[...]
        r = lax.rsqrt(jnp.mean(xv * xv, axis=-1, keepdims=True) + RMS_EPS)
        u = ((xv * r) * g_ref[...]).astype(BF16)
        u_ref[...] = u
        for n0 in range(0, D_IN, D_MODEL):
            acc = jnp.dot(u, w_scr[:, n0:n0 + D_MODEL], preferred_element_type=F32)
            for c3 in range(D_MODEL // LANES):
                proj[n0 // LANES + c3] = acc[:, LANES * c3:LANES * (c3 + 1)]
        c, up, dn = c_ref[...], up_ref[...], dn_ref[...]

        def cols(lo, hi, rot=False, scale=None):
            parts = []
            for ch in range(lo // LANES, hi // LANES):
                t = proj[ch]
                if rot:
                    t = _rotate128(t, c, up, dn)
                if scale is not None:
                    t = t * scale
                parts.append(t)
            return parts[0] if len(parts) == 1 else jnp.concatenate(parts, axis=1)

        qa_ref[...] = cols(0, 384, True, SCALE).astype(BF16)
        ka_ref[...] = cols(384, 512, True).astype(BF16)
        va_ref[...] = cols(512, 640).astype(BF16)
        gate_ref[:, 0:384] = cols(640, 1024).astype(BF16)
        gate_ref[:, 384:768] = cols(2176, 2560).astype(BF16)
        gate_ref[:, 768:1024] = cols(2816, 3072).astype(BF16)
        qc_ref[...] = cols(2560, 2816, False, SCALE).astype(BF16)
        for ch in range(1024 // LANES, 1408 // LANES):
            proj[ch] = _rotate128(proj[ch], c, up, dn) * SCALE
        for ch in range(1408 // LANES, 1792 // LANES):
            proj[ch] = _rotate128(proj[ch], c, up, dn)
        for lo, nat, p4, p16 in ((1024, qb1_ref, qb4_ref, qb16_ref), (1408, kb1_ref, kb4_ref, kb16_ref),
                                 (1792, vb1_ref, vb4_ref, vb16_ref)):
            chunks = range(lo // LANES, lo // LANES + B_W // LANES)
            nat[...] = jnp.concatenate([proj[ch] for ch in chunks], axis=1).astype(BF16)
            for dil, ref in ((4, p4), (16, p16)):
                for rr in range(dil):
                    ref[0, rr] = jnp.concatenate(
                        [proj.at[ch][pl.ds(rr, tm // dil, stride=dil), :] for ch in chunks], axis=1).astype(BF16)

        @pl.when(step == n_steps - 1)
        def _():
            for cp in gather_copies(arriving=True)[1]:
                cp.wait_recv()
            local, sends = gather_copies(arriving=False)
            for cp in sends:
                cp.wait_send()
            for cp in local:
                cp.wait()

    nat_w = (D_MODEL, A_W, A_KV_W, A_KV_W, B_W, B_W, B_W)
    out_specs = [_row(tm, w) for w in nat_w]
    out_shape = [jax.ShapeDtypeStruct((seq, w), BF16) for w in nat_w]
    for dil in DILS:
        out_specs += [_perm_spec(tm, dil, B_W)] * 3
        out_shape += [jax.ShapeDtypeStruct((seq // (BLOCK * dil), dil, BLOCK, B_W), BF16)] * 3
    hbm = pl.BlockSpec(memory_space=pl.ANY)
    out_specs += [_row(tm, C_W), _row(tm, D_MODEL), hbm, hbm]
    out_shape += [jax.ShapeDtypeStruct((seq, C_W), BF16), jax.ShapeDtypeStruct((seq, D_MODEL), BF16),
                  jax.ShapeDtypeStruct((N_DEV,) + w_mem.shape, BF16), jax.ShapeDtypeStruct((N_DEV,) + w_out.shape, BF16)]
    res = pl.pallas_call(
        body, name="inproj", grid=(n_steps,),
        in_specs=[_row(tm, D_MODEL), _full((1, D_MODEL)), hbm, _row(tm, LANES), _row(tm, LANES), _row(tm, LANES),
                  _full(w_mem.shape), _full(w_out.shape)],
        out_specs=out_specs, out_shape=out_shape,
        scratch_shapes=[pltpu.VMEM((n_chunk, tm, LANES), F32)] + _w_in_scratch()
        + [pltpu.VMEM(w_mem.shape, BF16), pltpu.VMEM(w_out.shape, BF16), pltpu.SemaphoreType.DMA((2, N_DEV)),
           pltpu.SemaphoreType.DMA((2, N_DEV)), pltpu.SemaphoreType.DMA((2,))],
        compiler_params=_params(dimension_semantics=("arbitrary",)),
    )(x, pre_g, w_in_full, *tabs, w_mem, w_out)
    u, qa, ka, va = res[0:4]
    qkv_b = {1: res[4:7], 4: [t.reshape(seq, B_W) for t in res[7:10]], 16: [t.reshape(seq, B_W) for t in res[10:13]]}
    return u, qa, ka, va, qkv_b, res[13], res[14], res[15], res[16]


def _memkv_fwd(mem, mem_g, w_mem_full):
    def body(mem_ref, g_ref, w_ref, mn_ref, mk_ref, mv_ref):
        mv_ = mem_ref[...]
        r = lax.rsqrt(jnp.mean(mv_ * mv_, axis=-1, keepdims=True) + RMS_EPS)
        mn = ((mv_ * r) * g_ref[...]).astype(BF16)
        mn_ref[...] = mn
        mkv = jnp.dot(mn, w_ref[...], preferred_element_type=F32)
        mk_ref[...] = mkv[:, 0:C_W].astype(BF16)
        mv_ref[...] = mkv[:, C_W:2 * C_W].astype(BF16)

    return pl.pallas_call(
        body, name="memkv_fwd",
        out_shape=[jax.ShapeDtypeStruct((N_MEM, D_MODEL), BF16),
                   jax.ShapeDtypeStruct((N_MEM, C_W), BF16), jax.ShapeDtypeStruct((N_MEM, C_W), BF16)],
        compiler_params=_params(),
    )(mem, mem_g, w_mem_full)


def _memkv_bwd(mem, mem_g, mn, w_mem_full, dmk, dmv):
    def body(mem_ref, g_ref, mn_ref, w_ref, dmk_ref, dmv_ref, dw_ref, st_ref):
        dmkv = jnp.concatenate([dmk_ref[...], dmv_ref[...]], axis=1).astype(BF16)
        dw_ref[...] = lax.dot_general(mn_ref[...], dmkv, (((0,), (0,)), ((), ())), preferred_element_type=F32)
        dmn = lax.dot_general(dmkv, w_ref[...], (((1,), (1,)), ((), ())), preferred_element_type=F32)
        mv_ = mem_ref[...]
        r = lax.rsqrt(jnp.mean(mv_ * mv_, axis=-1, keepdims=True) + RMS_EPS)
        st_ref[...] = jnp.zeros_like(st_ref)
        st_ref[0:1, :] = jnp.sum(dmn * (mv_ * r), axis=0, keepdims=True)

    return pl.pallas_call(
        body, name="memkv_bwd",
        out_shape=[jax.ShapeDtypeStruct((D_MODEL, 2 * C_W), F32), jax.ShapeDtypeStruct((8, D_MODEL), F32)],
        compiler_params=_params(),
    )(mem, mem_g, mn, w_mem_full, dmk, dmv)


def _band_mask(has_prev, max_dist):
    qi = lax.broadcasted_iota(jnp.int32, (BLOCK, 2 * BLOCK), 0)
    kj = lax.broadcasted_iota(jnp.int32, (BLOCK, 2 * BLOCK), 1)
    dist = qi + BLOCK - kj
    return (dist >= 0) & (dist <= max_dist) & ((kj >= BLOCK) | has_prev)


_NT = (((1,), (1,)), ((), ()))
_TN = (((0,), (0,)), ((), ()))


def _head_only(val, h):
    slab = val[:, LANES * (h // 2):LANES * (h // 2 + 1)]
    lane = lax.broadcasted_iota(jnp.int32, slab.shape, 1)
    keep = (lane < HEAD_DIM) if h % 2 == 0 else (lane >= HEAD_DIM)
    return jnp.where(keep, slab, jnp.zeros((), slab.dtype))


class _KvSlabs:
    def __init__(self, cat, group):
        self.cat, self.group, self.swapped = cat, group, {}

    def is_swapped(self, h):
        return (h // self.group) % 2 != h % 2

    def __call__(self, h):
        j = (h // self.group) // 2
        slab = self.cat[:, LANES * j:LANES * (j + 1)]
        if not self.is_swapped(h):
            return slab
        if j not in self.swapped:
            self.swapped[j] = jnp.concatenate([slab[:, HEAD_DIM:], slab[:, :HEAD_DIM]], axis=1)
        return self.swapped[j]


class _BandSteps:
    def __init__(self, seq, dil, nq):
        self.nq, self.rows, self.consecutive = nq, nq * BLOCK, dil == 1
        nb = seq // dil // BLOCK
        if self.consecutive:
            assert nb % nq == 0
            self.outer, self.inner, self.stride = 1, nb // nq, 1
        else:
            assert dil % nq == 0
            self.outer, self.inner, self.stride = dil // nq, nb, dil // nq

    def own(self, w, clamp=False):
        cur = (lambda i: jnp.minimum(i, self.inner - 1)) if clamp else (lambda i: i)
        return pl.BlockSpec((self.rows, w), lambda r, i: (cur(i) * self.stride + r, 0))

    def prev(self, w, clamp=False):
        cur = (lambda i: jnp.minimum(i, self.inner - 1)) if clamp else (lambda i: i)
        if self.consecutive:
            return pl.BlockSpec((BLOCK, w), lambda r, i: (jnp.maximum(cur(i) * self.nq - 1, 0), 0))
        return pl.BlockSpec((self.rows, w), lambda r, i: (jnp.maximum(cur(i) - 1, 0) * self.stride + r, 0))

    def late(self, w):
        return pl.BlockSpec((self.rows, w), lambda r, i: (jnp.maximum(i - 1, 0) * self.stride + r, 0))

    def rows_of(self, j):
        return slice(BLOCK * j, BLOCK * (j + 1))

    def keys(self, p_ref, c_ref, j):
        if not self.consecutive:
            before = p_ref[self.rows_of(j), :]
        elif j == 0:
            before = p_ref[...]
        else:
            before = c_ref[self.rows_of(j - 1), :]
        return jnp.concatenate([before, c_ref[self.rows_of(j), :]], axis=0)

    def has_prev(self, i, j):
        return True if (self.consecutive and j > 0) else (i > 0)


def _banded_fwd(q, k, v, sink, *, dil, heads, group, max_dist, nq, name):
    seq = q.shape[0]
    kvh = heads // group
    qw, kw = heads * HEAD_DIM, kvh * HEAD_DIM
    steps = _BandSteps(seq, dil, nq)

    def body(*refs):
        if sink is not None:
            sink_ref, refs = refs[0], refs[1:]
        q_ref, kp_ref, kc_ref, vp_ref, vc_ref, o_ref, lse_ref, s_scr, p_scr = refs
        i = pl.program_id(1)
        lane = lax.broadcasted_iota(jnp.int32, (BLOCK, LANES), 1)
        k_of = [_KvSlabs(steps.keys(kp_ref, kc_ref, j), group) for j in range(nq)]
        v_of = [_KvSlabs(steps.keys(vp_ref, vc_ref, j), group) for j in range(nq)]
        for j in range(nq):
            qv = q_ref[steps.rows_of(j), :]
            for h in range(heads):
                s_scr[j * heads + h] = lax.dot_general(_head_only(qv, h), k_of[j](h), _NT, preferred_element_type=F32)
        ls = {}
        for j in range(nq):
            valid = _band_mask(steps.has_prev(i, j), max_dist)
            lse_tile = jnp.zeros((BLOCK, LANES), F32)
            for h in range(heads):
                s = jnp.where(valid, s_scr[j * heads + h], NEG)
                m = jnp.max(s, axis=-1, keepdims=True)
                if sink is not None:
                    sk = sink_ref[h]
                    m = jnp.maximum(m, sk)
                p = jnp.exp(s - m)
                l = jnp.sum(p, axis=-1, keepdims=True)
                if sink is not None:
                    l = l + jnp.exp(sk - m)
                p_scr[j * heads + h] = p.astype(BF16)
                ls[j, h] = l
                lse_tile = jnp.where(lane == h, m + jnp.log(l), lse_tile)
            lse_ref[steps.rows_of(j), :] = lse_tile
        for j in range(nq):
            for pr in range(heads // 2):
                he, ho = 2 * pr, 2 * pr + 1
                even = jnp.dot(p_scr[j * heads + he], v_of[j](he), preferred_element_type=F32) / ls[j, he]
                odd = jnp.dot(p_scr[j * heads + ho], v_of[j](ho), preferred_element_type=F32) / ls[j, ho]
                o_ref[steps.rows_of(j), LANES * pr:LANES * (pr + 1)] = jnp.where(lane < HEAD_DIM, even, odd).astype(BF16)

    in_specs = [steps.own(qw), steps.prev(kw), steps.own(kw), steps.prev(kw), steps.own(kw)]
    args = [q, k, k, v, v]
    if sink is not None:
        in_specs = [pl.BlockSpec(memory_space=pltpu.SMEM)] + in_specs
        args = [sink] + args
    return pl.pallas_call(
        body, name=name, grid=(steps.outer, steps.inner), in_specs=in_specs,
        out_specs=[steps.own(qw), steps.own(LANES)],
        out_shape=[jax.ShapeDtypeStruct((seq, qw), BF16), jax.ShapeDtypeStruct((seq, LANES), F32)],
        scratch_shapes=[pltpu.VMEM((nq * heads, BLOCK, 2 * BLOCK), F32), pltpu.VMEM((nq * heads, BLOCK, 2 * BLOCK), BF16)],
        compiler_params=_params(dimension_semantics=("arbitrary", "arbitrary")),
    )(*args)


def _banded_bwd(q, k, v, d_out, stat, sink, *, dil, heads, group, max_dist, nq, name):
    seq = q.shape[0]
    kvh = heads // group
    qw, kw = heads * HEAD_DIM, kvh * HEAD_DIM
    steps = _BandSteps(seq, dil, nq)
    n_in = 7

    def body(*refs):
        if sink is not None:
            sink_ref, refs = refs[0], refs[1:]
            dsink_ref, refs = refs[n_in], refs[:n_in] + refs[n_in + 1:]
        (q_ref, kp_ref, kc_ref, vp_ref, vc_ref, do_ref, st_ref, dq_ref, dk_ref, dv_ref,
         kcar, vcar, s_scr, dp_scr, p_scr, ds_scr) = refs
        r, i = pl.program_id(0), pl.program_id(1)

        @pl.when(i == 0)
        def _():
            kcar[...] = jnp.zeros_like(kcar)
            vcar[...] = jnp.zeros_like(vcar)

        if sink is not None:
            @pl.when((i == 0) & (r == 0))
            def _():
                dsink_ref[...] = jnp.zeros_like(dsink_ref)

        @pl.when(i < steps.inner)
        def _():
            lane = lax.broadcasted_iota(jnp.int32, (1, LANES), 1)
            lane_q = lax.broadcasted_iota(jnp.int32, (BLOCK, LANES), 1)
            k_of = [_KvSlabs(steps.keys(kp_ref, kc_ref, j), group) for j in range(nq)]
            v_of = [_KvSlabs(steps.keys(vp_ref, vc_ref, j), group) for j in range(nq)]
            qms, doms = {}, {}
            for j in range(nq):
                qv, dov = q_ref[steps.rows_of(j), :], do_ref[steps.rows_of(j), :]
                for h in range(heads):
                    qms[j, h], doms[j, h] = _head_only(qv, h), _head_only(dov, h)
                    s_scr[j * heads + h] = lax.dot_general(qms[j, h], k_of[j](h), _NT, preferred_element_type=F32)
                    dp_scr[j * heads + h] = lax.dot_general(doms[j, h], v_of[j](h), _NT, preferred_element_type=F32)
            dsink_row = jnp.zeros((1, LANES), F32)
            for j in range(nq):
                st = st_ref[steps.rows_of(j), :]
                valid = _band_mask(steps.has_prev(i, j), max_dist)
                for h in range(heads):
                    lse_h = st[:, h:h + 1]
                    delta = st[:, DELTA_LANE + h:DELTA_LANE + h + 1]
                    p = jnp.where(valid, jnp.exp(s_scr[j * heads + h] - lse_h), 0.0)
                    p_scr[j * heads + h] = p.astype(BF16)
                    ds_scr[j * heads + h] = (p * (dp_scr[j * heads + h] - delta)).astype(BF16)
                    if sink is not None:
                        ds_sink = jnp.sum(-jnp.exp(sink_ref[h] - lse_h) * delta, axis=0, keepdims=True)
                        dsink_row = dsink_row + jnp.where(lane == h, ds_sink, 0.0)
            for j in range(nq):
                for pr in range(heads // 2):
                    he, ho = 2 * pr, 2 * pr + 1
                    even = jnp.dot(ds_scr[j * heads + he], k_of[j](he), preferred_element_type=F32)
                    odd = jnp.dot(ds_scr[j * heads + ho], k_of[j](ho), preferred_element_type=F32)
                    dq_ref[steps.rows_of(j), LANES * pr:LANES * (pr + 1)] = (
                        jnp.where(lane_q < HEAD_DIM, even, odd).astype(BF16))
            if steps.consecutive:
                dk_ref[...] = kcar[...].astype(BF16)
                dv_ref[...] = vcar[...].astype(BF16)
            for j in range(nq):
                for slab in range(kw // LANES):
                    acc = {}
                    for h in range(heads):
                        if (h // group) // 2 != slab:
                            continue
                        key = k_of[j].is_swapped(h)
                        dk_h = lax.dot_general(ds_scr[j * heads + h], qms[j, h], _TN, preferred_element_type=F32)
                        dv_h = lax.dot_general(p_scr[j * heads + h], doms[j, h], _TN, preferred_element_type=F32)
                        acc[key] = (dk_h, dv_h) if key not in acc else (acc[key][0] + dk_h, acc[key][1] + dv_h)
                    dk_j, dv_j = acc.get(False, (None, None))
                    if True in acc:
                        unswap = lambda t: jnp.concatenate([t[:, HEAD_DIM:], t[:, :HEAD_DIM]], axis=1)
                        dk_s, dv_s = unswap(acc[True][0]), unswap(acc[True][1])
                        dk_j = dk_s if dk_j is None else dk_j + dk_s
                        dv_j = dv_s if dv_j is None else dv_j + dv_s
                    sl = slice(LANES * slab, LANES * (slab + 1))
                    own_rows = steps.rows_of(j)
                    if not steps.consecutive:
                        dk_ref[own_rows, sl] = (kcar[own_rows, sl] + dk_j[0:BLOCK]).astype(BF16)
                        dv_ref[own_rows, sl] = (vcar[own_rows, sl] + dv_j[0:BLOCK]).astype(BF16)
                    elif j == 0:
                        last = steps.rows_of(nq - 1)
                        dk_ref[last, sl] = (kcar[last, sl] + dk_j[0:BLOCK]).astype(BF16)
                        dv_ref[last, sl] = (vcar[last, sl] + dv_j[0:BLOCK]).astype(BF16)
                    else:
                        before = steps.rows_of(j - 1)
                        kcar[before, sl] += dk_j[0:BLOCK]
                        vcar[before, sl] += dv_j[0:BLOCK]
                    kcar[own_rows, sl] = dk_j[BLOCK:2 * BLOCK]
                    vcar[own_rows, sl] = dv_j[BLOCK:2 * BLOCK]
            if sink is not None:
                dsink_ref[0:1, :] += dsink_row

        @pl.when(i == steps.inner)
        def _():
            dk_ref[...] = kcar[...].astype(BF16)
            dv_ref[...] = vcar[...].astype(BF16)

    own, prev = (lambda w: steps.own(w, clamp=True)), (lambda w: steps.prev(w, clamp=True))
    in_specs = [own(qw), prev(kw), own(kw), prev(kw), own(kw), own(qw), own(LANES)]
    args = [q, k, k, v, v, d_out, stat]
    out_specs = [own(qw), steps.late(kw), steps.late(kw)]
    out_shape = [jax.ShapeDtypeStruct((seq, qw), BF16), jax.ShapeDtypeStruct((seq, kw), BF16),
                 jax.ShapeDtypeStruct((seq, kw), BF16)]
    if sink is not None:
        in_specs = [pl.BlockSpec(memory_space=pltpu.SMEM)] + in_specs
        args = [sink] + args
        out_specs = [_full((8, LANES))] + out_specs
        out_shape = [jax.ShapeDtypeStruct((8, LANES), F32)] + out_shape
    n_hb = nq * heads
    res = pl.pallas_call(
        body, name=name, grid=(steps.outer, steps.inner + 1), in_specs=in_specs, out_specs=out_specs,
        out_shape=out_shape,
        scratch_shapes=[pltpu.VMEM((steps.rows, kw), F32), pltpu.VMEM((steps.rows, kw), F32)]
        + [pltpu.VMEM((n_hb, BLOCK, 2 * BLOCK), F32)] * 2 + [pltpu.VMEM((n_hb, BLOCK, 2 * BLOCK), BF16)] * 2,
        compiler_params=_params(dimension_semantics=("arbitrary", "arbitrary")),
    )(*args)
    if sink is not None:
        return res[1], res[2], res[3], res[0]
    return res


def _cross_fwd(q, mk, mv, tq=512):
    seq = q.shape[0]

    def body(q_ref, mk_ref, mv_ref, o_ref, lse_ref, s_scr, p_scr):
        qv = q_ref[...]
        k_of, v_of = _KvSlabs(mk_ref[...], 1), _KvSlabs(mv_ref[...], 1)
        lane = lax.broadcasted_iota(jnp.int32, (tq, LANES), 1)
        lse_tile = jnp.zeros((tq, LANES), F32)
        for h in range(C_HEADS):
            s_scr[h] = lax.dot_general(_head_only(qv, h), k_of(h), _NT, preferred_element_type=F32)
        ls = []
        for h in range(C_HEADS):
            s = s_scr[h]
            m = jnp.max(s, axis=-1, keepdims=True)
            p = jnp.exp(s - m)
            l = jnp.sum(p, axis=-1, keepdims=True)
            p_scr[h] = p.astype(BF16)
            ls.append(l)
            lse_tile = jnp.where(lane == h, m + jnp.log(l), lse_tile)
        for pr in range(C_HEADS // 2):
            even = jnp.dot(p_scr[2 * pr], v_of(2 * pr), preferred_element_type=F32) / ls[2 * pr]
            odd = jnp.dot(p_scr[2 * pr + 1], v_of(2 * pr + 1), preferred_element_type=F32) / ls[2 * pr + 1]
            o_ref[:, LANES * pr:LANES * (pr + 1)] = jnp.where(lane < HEAD_DIM, even, odd).astype(BF16)
        lse_ref[...] = lse_tile

    return pl.pallas_call(
        body, name="cross_fwd", grid=(seq // tq,),
        in_specs=[_row(tq, C_W), _full((N_MEM, C_W)), _full((N_MEM, C_W))],
        out_specs=[_row(tq, C_W), _row(tq, LANES)],
        out_shape=[jax.ShapeDtypeStruct((seq, C_W), BF16), jax.ShapeDtypeStruct((seq, LANES), F32)],
        scratch_shapes=[pltpu.VMEM((C_HEADS, tq, N_MEM), F32), pltpu.VMEM((C_HEADS, tq, N_MEM), BF16)],
        compiler_params=_params(dimension_semantics=("arbitrary",)),
    )(q, mk, mv)


def _cross_bwd(q, mk, mv, d_out, stat, tq=512):
    seq = q.shape[0]

    def body(q_ref, mk_ref, mv_ref, do_ref, st_ref, dq_ref, dmk_ref, dmv_ref, s_scr, dp_scr, p_scr, ds_scr):
        @pl.when(pl.program_id(0) == 0)
        def _():
            dmk_ref[...] = jnp.zeros_like(dmk_ref)
            dmv_ref[...] = jnp.zeros_like(dmv_ref)

        qv, dov, st = q_ref[...], do_ref[...], st_ref[...]
        k_of, v_of = _KvSlabs(mk_ref[...], 1), _KvSlabs(mv_ref[...], 1)
        qms = [_head_only(qv, h) for h in range(C_HEADS)]
        doms = [_head_only(dov, h) for h in range(C_HEADS)]
        for h in range(C_HEADS):
            s_scr[h] = lax.dot_general(qms[h], k_of(h), _NT, preferred_element_type=F32)
            dp_scr[h] = lax.dot_general(doms[h], v_of(h), _NT, preferred_element_type=F32)
        for h in range(C_HEADS):
            p = jnp.exp(s_scr[h] - st[:, h:h + 1])
            p_scr[h] = p.astype(BF16)
            ds_scr[h] = (p * (dp_scr[h] - st[:, DELTA_LANE + h:DELTA_LANE + h + 1])).astype(BF16)
        lane = lax.broadcasted_iota(jnp.int32, (tq, LANES), 1)
        for pr in range(C_HEADS // 2):
            sl = slice(LANES * pr, LANES * (pr + 1))
            even = jnp.dot(ds_scr[2 * pr], k_of(2 * pr), preferred_element_type=F32)
            odd = jnp.dot(ds_scr[2 * pr + 1], k_of(2 * pr + 1), preferred_element_type=F32)
            dq_ref[:, sl] = jnp.where(lane < HEAD_DIM, even, odd).astype(BF16)
            dmk_ref[:, sl] += (lax.dot_general(ds_scr[2 * pr], qms[2 * pr], _TN, preferred_element_type=F32)
                               + lax.dot_general(ds_scr[2 * pr + 1], qms[2 * pr + 1], _TN, preferred_element_type=F32))
            dmv_ref[:, sl] += (lax.dot_general(p_scr[2 * pr], doms[2 * pr], _TN, preferred_element_type=F32)
                               + lax.dot_general(p_scr[2 * pr + 1], doms[2 * pr + 1], _TN, preferred_element_type=F32))

    return pl.pallas_call(
        body, name="cross_bwd", grid=(seq // tq,),
        in_specs=[_row(tq, C_W), _full((N_MEM, C_W)), _full((N_MEM, C_W)), _row(tq, C_W), _row(tq, LANES)],
        out_specs=[_row(tq, C_W), _full((N_MEM, C_W)), _full((N_MEM, C_W))],
        out_shape=[jax.ShapeDtypeStruct((seq, C_W), BF16), jax.ShapeDtypeStruct((N_MEM, C_W), F32),
                   jax.ShapeDtypeStruct((N_MEM, C_W), F32)],
        scratch_shapes=[pltpu.VMEM((C_HEADS, tq, N_MEM), F32)] * 2 + [pltpu.VMEM((C_HEADS, tq, N_MEM), BF16)] * 2,
        compiler_params=_params(dimension_semantics=("arbitrary",)),
    )(q, mk, mv, d_out, stat)


def _per_head(tile, width):
    rows = tile.shape[0]
    return jnp.concatenate(
        [jnp.broadcast_to(tile[:, h:h + 1], (rows, HEAD_DIM)) for h in range(width // HEAD_DIM)], axis=1)


def _with_delta(lse_tile, prod):
    rows = lse_tile.shape[0]
    lane = lax.broadcasted_iota(jnp.int32, (rows, LANES), 1)
    tile = lse_tile
    for h in range(prod.shape[1] // HEAD_DIM):
        d = jnp.sum(prod[:, HEAD_DIM * h:HEAD_DIM * (h + 1)], axis=-1, keepdims=True)
        tile = jnp.where(lane == DELTA_LANE + h, d, tile)
    return tile


def _mid(oa, lse_a, ob, lse_b, oc, lse_c, gate, x, target, w_out_full, post_g, tm=256):
    seq = x.shape[0]
    n_b = B_W // LANES

    def body(oa_ref, la_ref, b1_ref, l1_ref, b4_ref, l4_ref, b16_ref, l16_ref, oc_ref, lc_ref,
             gate_ref, x_ref, t_ref, w_ref, pg_ref,
             dh_ref, dg_ref, doa_ref, sa_ref, dob1_ref, sb1_ref, dob4_ref, sb4_ref, dob16_ref, sb16_ref,
             doc_ref, sc_ref, dw_ref, st_ref, scr_b4, scr_b16, scr_l4, scr_l16, scr_do, scr_sb):
        @pl.when(pl.program_id(0) == 0)
        def _():
            dw_ref[...] = jnp.zeros_like(dw_ref)
            st_ref[...] = jnp.zeros_like(st_ref)

        b1, l1 = b1_ref[...].astype(F32), l1_ref[...]
        b4, l4 = _load_permuted(b4_ref, scr_b4, 4), _load_permuted(l4_ref, scr_l4, 4)
        b16, l16 = _load_permuted(b16_ref, scr_b16, 16), _load_permuted(l16_ref, scr_l16, 16)
        lm = jnp.maximum(jnp.maximum(l1, l4), l16)
        e1, e4, e16 = jnp.exp(l1 - lm), jnp.exp(l4 - lm), jnp.exp(l16 - lm)
        den = e1 + e4 + e16
        lse_b_tile = lm + jnp.log(den)
        ob_v = _per_head(e1 / den, B_W) * b1 + _per_head(e4 / den, B_W) * b4 + _per_head(e16 / den, B_W) * b16
        o_all = jnp.concatenate([oa_ref[...].astype(F32), ob_v, oc_ref[...].astype(F32)], axis=1)
        g = gate_ref[...].astype(F32)
        sig = 1.0 / (1.0 + jnp.exp(-g))
        silu = g * sig
        y = (o_all * silu).astype(BF16)
        w = w_ref[...]
        z = jnp.dot(y, w, preferred_element_type=F32)
        rz = lax.rsqrt(jnp.mean(z * z, axis=-1, keepdims=True) + RMS_EPS)
        hn = z * rz
        pg = pg_ref[...]
        err = (x_ref[...] + hn * pg) - t_ref[...]
        loss = 0.5 * jnp.sum(jnp.mean(err * err, axis=-1, keepdims=True), axis=0, keepdims=True)
        dh = err * (1.0 / D_MODEL)
        dh_ref[...] = dh.astype(BF16)
        st_ref[0:1, :] += jnp.sum(dh * hn, axis=0, keepdims=True)
        st_ref[1:2, :] += jnp.broadcast_to(loss, (1, D_MODEL))
        dhn = dh * pg
        dz = (rz * (dhn - hn * jnp.mean(dhn * hn, axis=-1, keepdims=True))).astype(BF16)
        dy = lax.dot_general(dz, w, _NT, preferred_element_type=F32)
        dw_ref[...] += lax.dot_general(y, dz, _TN, preferred_element_type=F32)
        dg_ref[...] = (dy * o_all * (sig * (1.0 + g * (1.0 - sig)))).astype(BF16)
        d_o = (dy * silu).astype(BF16)
        prod = d_o.astype(F32) * o_all
        doa_ref[...] = d_o[:, 0:A_W]
        sa_ref[...] = _with_delta(la_ref[...], prod[:, 0:A_W])
        doc_ref[...] = d_o[:, A_W + B_W:D_MODEL]
        sc_ref[...] = _with_delta(lc_ref[...], prod[:, A_W + B_W:D_MODEL])
        d_ob = d_o[:, A_W:A_W + B_W]
        stat_b = _with_delta(lse_b_tile, prod[:, A_W:A_W + B_W])
        dob1_ref[...] = d_ob
        sb1_ref[...] = stat_b
        _put(scr_do, d_ob.astype(F32))
        _put(scr_sb, stat_b)
        _store_permuted(scr_do, dob4_ref, 4, BF16)
        _store_permuted(scr_sb, sb4_ref, 4, F32)
        _store_permuted(scr_do, dob16_ref, 16, BF16)
        _store_permuted(scr_sb, sb16_ref, 16, F32)

    p4 = lambda w: _perm_spec(tm, 4, w)
    p16 = lambda w: _perm_spec(tm, 16, w)
    in_specs = [_row(tm, A_W), _row(tm, LANES), _row(tm, B_W), _row(tm, LANES), p4(B_W), p4(LANES), p16(B_W), p16(LANES),
                _row(tm, C_W), _row(tm, LANES), _row(tm, D_MODEL), _row(tm, D_MODEL), _row(tm, D_MODEL),
                _full((D_MODEL, D_MODEL)), _full((1, D_MODEL))]
    sds = jax.ShapeDtypeStruct
    v4 = lambda w, dt: sds((seq // (BLOCK * 4), 4, BLOCK, w), dt)
    v16 = lambda w, dt: sds((seq // (BLOCK * 16), 16, BLOCK, w), dt)
    out_specs = [_row(tm, D_MODEL), _row(tm, D_MODEL), _row(tm, A_W), _row(tm, LANES), _row(tm, B_W), _row(tm, LANES),
                 p4(B_W), p4(LANES), p16(B_W), p16(LANES), _row(tm, C_W), _row(tm, LANES),
                 _full((D_MODEL, D_MODEL)), _full((8, D_MODEL))]
    out_shape = [sds((seq, D_MODEL), BF16), sds((seq, D_MODEL), BF16), sds((seq, A_W), BF16), sds((seq, LANES), F32),
                 sds((seq, B_W), BF16), sds((seq, LANES), F32), v4(B_W, BF16), v4(LANES, F32), v16(B_W, BF16),
                 v16(LANES, F32), sds((seq, C_W), BF16), sds((seq, LANES), F32),
                 sds((D_MODEL, D_MODEL), F32), sds((8, D_MODEL), F32)]
    res = pl.pallas_call(
        body, name="mid", grid=(seq // tm,), in_specs=in_specs, out_specs=out_specs, out_shape=out_shape,
        scratch_shapes=[pltpu.VMEM((n_b, tm, LANES), F32), pltpu.VMEM((n_b, tm, LANES), F32),
                        pltpu.VMEM((1, tm, LANES), F32), pltpu.VMEM((1, tm, LANES), F32),
                        pltpu.VMEM((n_b, tm, LANES), F32), pltpu.VMEM((1, tm, LANES), F32)],
        compiler_params=_params(dimension_semantics=("arbitrary",)),
    )(oa, lse_a, ob[1], lse_b[1], _perm_view(ob[4], 4), _perm_view(lse_b[4], 4), _perm_view(ob[16], 16),
      _perm_view(lse_b[16], 16), oc, lse_c, gate, x, target, w_out_full, post_g)
    dh, d_gate, do_a, st_a, do_b1, st_b1, do_b4, st_b4, do_b16, st_b16, do_c, st_c, d_wout, stats = res
    flat = lambda t: t.reshape(seq, t.shape[-1])
    d_b = {1: (do_b1, st_b1), 4: (flat(do_b4), flat(st_b4)), 16: (flat(do_b16), flat(st_b16))}
    return dh, d_gate, (do_a, st_a), d_b, (do_c, st_c), d_wout, stats


def _dproj_assemble(tabs, dqa, dka, dva, dqkv_b, dqc, dgate, tm=512):
    seq = dqa.shape[0]
    n_b = B_W // LANES

    def body(c_ref, up_ref, dn_ref, dqa_ref, dka_ref, dva_ref,
             dq1, dk1, dv1, dq4, dk4, dv4, dq16, dk16, dv16, dqc_ref, dg_ref, dp_ref, scr4, scr16):
        c, up, dn = c_ref[...], -up_ref[...], -dn_ref[...]
        unrot = lambda t: _rotate(t, c, up, dn)
        total = lambda r1, r4, r16: (r1[...].astype(F32) + _load_permuted(r4, scr4, 4)
                                     + _load_permuted(r16, scr16, 16))
        dp_ref[:, 0:384] = (unrot(dqa_ref[...].astype(F32)) * SCALE).astype(BF16)
        dp_ref[:, 384:512] = unrot(dka_ref[...].astype(F32)).astype(BF16)
        dp_ref[:, 512:640] = dva_ref[...]
        dp_ref[:, 640:1024] = dg_ref[:, 0:384]
        dp_ref[:, 1024:1408] = (unrot(total(dq1, dq4, dq16)) * SCALE).astype(BF16)
        dp_ref[:, 1408:1792] = unrot(total(dk1, dk4, dk16)).astype(BF16)
        dp_ref[:, 1792:2176] = total(dv1, dv4, dv16).astype(BF16)
        dp_ref[:, 2176:2560] = dg_ref[:, 384:768]
        dp_ref[:, 2560:2816] = (dqc_ref[...].astype(F32) * SCALE).astype(BF16)
        dp_ref[:, 2816:3072] = dg_ref[:, 768:1024]

    in_specs = ([_row(tm, LANES), _row(tm, LANES), _row(tm, LANES), _row(tm, A_W), _row(tm, A_KV_W), _row(tm, A_KV_W)]
                + [_row(tm, B_W)] * 3 + [_perm_spec(tm, 4, B_W)] * 3 + [_perm_spec(tm, 16, B_W)] * 3
                + [_row(tm, C_W), _row(tm, D_MODEL)])
    return pl.pallas_call(
        body, name="dproj_assemble", grid=(seq // tm,), in_specs=in_specs,
        out_specs=_row(tm, D_IN), out_shape=jax.ShapeDtypeStruct((seq, D_IN), BF16),
        scratch_shapes=[pltpu.VMEM((n_b, tm, LANES), F32), pltpu.VMEM((n_b, tm, LANES), F32)],
        compiler_params=_params(dimension_semantics=("arbitrary",)),
    )(*tabs, dqa, dka, dva, *dqkv_b[1], *[_perm_view(t, 4) for t in dqkv_b[4]],
      *[_perm_view(t, 16) for t in dqkv_b[16]], dqc, dgate)


def _gradx_exchange(x, dh, pre_g, w_in_full, dproj, d_win, d_wmem, d_wout, tm=256, sums_at=6):
    seq = x.shape[0]
    n_steps = seq // tm
    sums_at = min(sums_at, n_steps - 1)
    shapes = [d_win.shape[1:], d_wmem.shape[1:], d_wout.shape[1:]]
    n_arr = len(shapes)

    def body(x_ref, dh_ref, g_ref, w_hbm, dp_ref, win, wm, wo, gx_ref, st_ref, g_win, g_wm, g_wo,
             w_scr, w_sems, *scratch):
        mine, got, snd, rcv = (scratch[n_arr * t:n_arr * (t + 1)] for t in range(4))
        load_sems, d2d_send, d2d_recv, ici_send, ici_recv = scratch[4 * n_arr:]
        step = pl.program_id(0)
        x_, y_, c_ = _mesh_pos()
        ins, outs = (win, wm, wo), (g_win, g_wm, g_wo)

        def owner(kk, core):
            return 4 * (1 - x_ if kk & 2 else x_) + 2 * (1 - y_ if kk & 1 else y_) + core

        def load(a, kk):
            return pltpu.make_async_copy(ins[a].at[owner(kk, c_)], mine[a].at[kk], load_sems.at[a, kk])

        def swap(a, kk):
            return pltpu.make_async_remote_copy(
                src_ref=ins[a].at[owner(kk, 1 - c_)], dst_ref=got[a].at[kk], send_sem=d2d_send.at[a, kk],
                recv_sem=d2d_recv.at[a, kk], device_id=(x_, y_, 1 - c_), device_id_type=MESH_ID)

        def hop(a, kk):
            return pltpu.make_async_remote_copy(
                src_ref=snd[a].at[kk - 1], dst_ref=rcv[a].at[kk - 1], send_sem=ici_send.at[a, kk],
                recv_sem=ici_recv.at[a, kk],
                device_id=(1 - x_ if kk & 2 else x_, 1 - y_ if kk & 1 else y_, c_), device_id_type=MESH_ID)

        pairs = [(a, kk) for kk in (3, 1, 2) for a in range(n_arr)]

        @pl.when(step == 0)
        def _():
            st_ref[...] = jnp.zeros_like(st_ref)
            for kk in range(4):
                for a in range(n_arr):
                    load(a, kk).start()
                    swap(a, kk).start()

        _stage_w_in(w_hbm, w_scr, w_sems)

        @pl.when(step == sums_at)
        def _():
            for a, kk in pairs:
                load(a, kk).wait()
                swap(a, kk).wait_recv()
                snd[a][kk - 1] = (mine[a][kk] + got[a][kk]).astype(BF16)
                hop(a, kk).start()

        du = lax.dot_general(dp_ref[...], w_scr[...], _NT, preferred_element_type=F32)
        xv = x_ref[...]
        r = lax.rsqrt(jnp.mean(xv * xv, axis=-1, keepdims=True) + RMS_EPS)
        xh = xv * r
        st_ref[0:1, :] += jnp.sum(du * xh, axis=0, keepdims=True)
        dxh = du * g_ref[...]
        gx_ref[...] = dh_ref[...].astype(F32) + r * (dxh - xh * jnp.mean(dxh * xh, axis=-1, keepdims=True))

        @pl.when(step == n_steps - 1)
        def _():
            for a in range(n_arr):
                load(a, 0).wait()
                swap(a, 0).wait_recv()
                acc = mine[a][0] + got[a][0]
                for kk in (1, 2, 3):
                    hop(a, kk).wait_recv()
                    acc = acc + rcv[a][kk - 1].astype(F32)
                outs[a][...] = acc
            for kk in range(4):
                for a in range(n_arr):
                    swap(a, kk).wait_send()
            for a, kk in pairs:
                hop(a, kk).wait_send()

    hbm = pl.BlockSpec(memory_space=pl.ANY)
    scratch = ([pltpu.VMEM((4,) + s, F32) for s in shapes] + [pltpu.VMEM((4,) + s, F32) for s in shapes]
               + [pltpu.VMEM((3,) + s, BF16) for s in shapes] + [pltpu.VMEM((3,) + s, BF16) for s in shapes]
               + [pltpu.SemaphoreType.DMA((n_arr, 4))] * 5)
    return pl.pallas_call(
        body, name="gradx_exchange", grid=(n_steps,),
        in_specs=[_row(tm, D_MODEL), _row(tm, D_MODEL), _full((1, D_MODEL)), hbm, _row(tm, D_IN), hbm, hbm, hbm],
        out_specs=[_row(tm, D_MODEL), _full((8, D_MODEL))] + [_full(s) for s in shapes],
        out_shape=[jax.ShapeDtypeStruct((seq, D_MODEL), F32), jax.ShapeDtypeStruct((8, D_MODEL), F32)]
        + [jax.ShapeDtypeStruct(s, F32) for s in shapes],
        scratch_shapes=_w_in_scratch() + scratch,
        compiler_params=_params(dimension_semantics=("arbitrary",)),
    )(x, dh, pre_g, w_in_full, dproj, d_win, d_wmem, d_wout)


def _inproj_bwd_w(u, dproj, tm=1024):
    seq = u.shape[0]

    def body(u_ref, dp_ref, dw_ref):
        @pl.when(pl.program_id(0) == 0)
        def _():
            dw_ref[...] = jnp.zeros_like(dw_ref)

        res = lax.dot_general(u_ref[...], dp_ref[...], _TN, preferred_element_type=F32)
        for k in range(N_DEV):
            dw_ref[k] += res[:, SHARD_IN * k:SHARD_IN * (k + 1)]

    return pl.pallas_call(
        body, name="inproj_bwd_w", grid=(seq // tm,),
        in_specs=[_row(tm, D_MODEL), _row(tm, D_IN)],
        out_specs=_full((N_DEV, D_MODEL, SHARD_IN)),
        out_shape=jax.ShapeDtypeStruct((N_DEV, D_MODEL, SHARD_IN), F32),
        compiler_params=_params(dimension_semantics=("arbitrary",)),
    )(u, dproj)


def _local_step(x, mem, pre_g, w_in_full, sink, mem_g, w_mem, w_out, post_g, target):
    seq = x.shape[0]
    tabs = _rope_tables(seq)
    u, qa, ka, va, qkv_b, qc, gate, w_mem_all, w_out_all = _inproj(x, pre_g, w_in_full, tabs, w_mem, w_out)
    w_mem_full = w_mem_all.reshape(D_MODEL, 2 * C_W)
    w_out_full = w_out_all.reshape(D_MODEL, D_MODEL)
    mn, mk, mv = _memkv_fwd(mem, mem_g, w_mem_full)

    a_cfg = dict(dil=1, heads=A_HEADS, group=A_GROUP, max_dist=BLOCK - 1, nq=ATTN_BLOCKS_PER_STEP)
    b_cfgs = {dil: dict(dil=dil, heads=B_HEADS, group=1, max_dist=win // dil, nq=ATTN_BLOCKS_PER_STEP)
              for win, dil in B_CONFIGS}
    oa, lse_a = _banded_fwd(qa, ka, va, sink, name="attn_a_fwd", **a_cfg)
    ob, lse_b = {}, {}
    for dil, cfg in b_cfgs.items():
        ob[dil], lse_b[dil] = _banded_fwd(*qkv_b[dil], None, name=f"attn_b{dil}_fwd", **cfg)
    oc, lse_c = _cross_fwd(qc, mk, mv)

    dh, d_gate, d_a, d_b, d_c, d_wout, st_mid = _mid(oa, lse_a, ob, lse_b, oc, lse_c, gate, x, target, w_out_full, post_g)

    dqa, dka, dva, dsink = _banded_bwd(qa, ka, va, *d_a, sink, name="attn_a_bwd", **a_cfg)
    dqkv_b = {dil: _banded_bwd(*qkv_b[dil], *d_b[dil], None, name=f"attn_b{dil}_bwd", **cfg)
              for dil, cfg in b_cfgs.items()}
    dqc, dmk, dmv = _cross_bwd(qc, mk, mv, *d_c)
    d_wmem, st_mem = _memkv_bwd(mem, mem_g, mn, w_mem_full, dmk, dmv)

    dproj = _dproj_assemble(tabs, dqa, dka, dva, dqkv_b, dqc, d_gate)
    d_win = _inproj_bwd_w(u, dproj)
    grad_x, st_pre, g_win, g_wmem, g_wout = _gradx_exchange(
        x, dh, pre_g, w_in_full, dproj, d_win, d_wmem.reshape(N_DEV, SHARD_ROWS, 2 * C_W),
        d_wout.reshape(N_DEV, SHARD_ROWS, D_MODEL))

    dsink_row = jnp.pad(dsink[0:1, :], ((0, 0), (0, D_MODEL - LANES)))
    stats = jnp.concatenate([st_pre[0:1], st_mem[0:1], st_mid[0:1], dsink_row, st_mid[1:2],
                             jnp.zeros((3, D_MODEL), F32)], axis=0)
    return grad_x, g_win, g_wmem, g_wout, stats


def _all_gather_w_in(w_in, parts=2):
    rows = D_MODEL // parts

    def body(win_ref, win_out, win_b, send_sems, recv_sems, local_sems):
        x, y, c = _mesh_pos()
        win_b[...] = win_ref[...].astype(BF16)
        me, sibling = (x, y, c), (x, y, 1 - c)
        chips = [(1 - x, y), (x, 1 - y), (1 - x, 1 - y)]

        def src(a):
            return win_b.at[pl.ds(rows * a, rows)]

        def slot(a, p):
            return win_out.at[_dev_index(p), pl.ds(rows * a, rows)]

        def copy(a, k, block, to, own=False):
            return pltpu.make_async_remote_copy(
                src_ref=src(a) if own else slot(a, block), dst_ref=slot(a, block),
                send_sem=send_sems.at[a, k], recv_sem=recv_sems.at[a, k], device_id=to, device_id_type=MESH_ID)

        mine = [pltpu.make_async_copy(src(a), slot(a, me), local_sems.at[a]) for a in range(parts)]
        for cp in mine:
            cp.start()
        first = []
        for a in range(parts):
            first += [copy(a, 1 + j, me, (*chip, c), own=True) for j, chip in enumerate(chips)]
            first.append(copy(a, 0, me, sibling, own=True))
        for cp in first:
            cp.start()
        passed = []
        for a in range(parts):
            for j, chip in enumerate(chips):
                copy(a, 1 + j, (*chip, c), me).wait_recv()
                fwd = copy(a, 4 + j, (*chip, c), sibling)
                fwd.start()
                passed.append(fwd)
        for a in range(parts):
            copy(a, 0, sibling, me).wait_recv()
            for j, chip in enumerate(chips):
                copy(a, 4 + j, (*chip, 1 - c), me).wait_recv()
        for cp in first + passed:
            cp.wait_send()
        for cp in mine:
            cp.wait()

    return pl.pallas_call(
        body, name="all_gather_w_in",
        in_specs=[pl.BlockSpec(memory_space=pltpu.VMEM)], out_specs=pl.BlockSpec(memory_space=pl.ANY),
        out_shape=jax.ShapeDtypeStruct((N_DEV,) + w_in.shape, BF16),
        scratch_shapes=[pltpu.VMEM(w_in.shape, BF16), pltpu.SemaphoreType.DMA((parts, 7)),
                        pltpu.SemaphoreType.DMA((parts, 7)), pltpu.SemaphoreType.DMA((parts,))],
        compiler_params=_params(),
    )(w_in)


def _exchange_stats(stats):
    def body(st, r_st, send_sems, recv_sems, local_sem):
        pos = _mesh_pos()
        me = _dev_index(pos)
        own = pltpu.make_async_copy(st, r_st.at[me], local_sem)
        own.start()
        copies = []
        for s in range(1, N_DEV):
            peer = _xor_peer(pos, s)
            mk = lambda slot: pltpu.make_async_remote_copy(
                src_ref=st, dst_ref=r_st.at[slot], send_sem=send_sems.at[s], recv_sem=recv_sems.at[s],
                device_id=peer, device_id_type=MESH_ID)
            send, arrival = mk(me), mk(_dev_index(peer))
            send.start()
            copies.append((send, arrival))
        for send, arrival in copies:
            arrival.wait_recv()
            send.wait_send()
        own.wait()

    hbm = pl.BlockSpec(memory_space=pl.ANY)
    return pl.pallas_call(
        body, name="exchange_stats", in_specs=[hbm], out_specs=hbm,
        out_shape=jax.ShapeDtypeStruct((N_DEV,) + stats.shape, F32),
        scratch_shapes=[pltpu.SemaphoreType.DMA((N_DEV,)), pltpu.SemaphoreType.DMA((N_DEV,)), pltpu.SemaphoreType.DMA(())],
        compiler_params=_params(),
    )(stats)


def _reduce_adamw(recv, w, m, v, *, tr, name):
    n_part, rows, cols = recv.shape

    def body(r_ref, w_ref, m_ref, v_ref, g_ref, d_ref, nm_ref, nv_ref):
        g = r_ref[0]
        for s in range(1, n_part):
            g = g + r_ref[s]
        g_ref[...] = g
        m2 = ADAM_B1 * m_ref[...] + (1.0 - ADAM_B1) * g
        v2 = ADAM_B2 * v_ref[...] + (1.0 - ADAM_B2) * (g * g)
        nm_ref[...] = m2
        nv_ref[...] = v2
        m_hat = m2 / (1.0 - ADAM_B1 ** ADAM_STEP)
        v_hat = v2 / (1.0 - ADAM_B2 ** ADAM_STEP)
        d_ref[...] = -ADAM_LR * (m_hat / (jnp.sqrt(v_hat) + ADAM_EPS) + ADAM_WD * w_ref[...])

    blk = pl.BlockSpec((tr, cols), lambda i: (i, 0))
    return pl.pallas_call(
        body, name=name, grid=(rows // tr,),
        in_specs=[pl.BlockSpec((n_part, tr, cols), lambda i: (0, i, 0)), blk, blk, blk],
        out_specs=[blk] * 4, out_shape=[jax.ShapeDtypeStruct((rows, cols), F32)] * 4,
        compiler_params=_params(dimension_semantics=("arbitrary",)),
    )(recv, w, m, v)


def _pack_rows(pre, memn, post, sink):
    sink_row = jnp.pad(sink, ((0, 0), (0, D_MODEL - A_HEADS)))
    return jnp.concatenate([pre, memn, post, sink_row, jnp.zeros((4, D_MODEL), F32)], axis=0)


def kernel(x, mem, pre_norm, w_in, sink_a, mem_norm, w_mem_kv, w_out, post_norm, loss_target, m_pre_norm, m_w_in, m_sink_a, m_mem_norm, m_w_mem_kv, m_w_out, m_post_norm, v_pre_norm, v_w_in, v_sink_a, v_mem_norm, v_w_mem_kv, v_w_out, v_post_norm):
    w_in_full = _all_gather_w_in(w_in[0])
    sink = jnp.pad(sink_a[0], (0, 8 - A_HEADS))
    grad_x, g_win, g_wmem, g_wout, stats = _local_step(
        x[0], mem[0], pre_norm, w_in_full, sink, mem_norm, w_mem_kv[0], w_out[0], post_norm, loss_target[0])
    r_stats = _exchange_stats(stats)

    big = {}
    for nm, g, w, m, v in (("w_in", g_win, w_in, m_w_in, v_w_in),
                           ("w_mem_kv", g_wmem, w_mem_kv, m_w_mem_kv, v_w_mem_kv),
                           ("w_out", g_wout, w_out, m_w_out, v_w_out)):
        res = _reduce_adamw(g[None], w[0], m[0], v[0], tr=SHARD_ROWS, name="adamw_" + nm)
        big[nm] = [t[None] for t in res]
    small = _reduce_adamw(
        r_stats, _pack_rows(pre_norm, mem_norm, post_norm, sink_a),
        _pack_rows(m_pre_norm, m_mem_norm, m_post_norm, m_sink_a),
        _pack_rows(v_pre_norm, v_mem_norm, v_post_norm, v_sink_a), tr=8, name="adamw_small")

    def unpack(t):
        return {"pre_norm": t[0:1], "mem_norm": t[1:2], "post_norm": t[2:3], "sink_a": t[3:4, 0:A_HEADS]}

    order = ("pre_norm", "w_in", "sink_a", "mem_norm", "w_mem_kv", "w_out", "post_norm")
    outs = [small[0][4, 0], grad_x[None]]
    for j in range(4):
        sm = unpack(small[j])
        outs += [big[n][j] if n in big else sm[n] for n in order]
    return tuple(outs)
```

```python
import jax
import jax.numpy as jnp
from jax import lax
from jax.experimental import pallas as pl
from jax.experimental.pallas import tpu as pltpu

F32 = jnp.float32
BF16 = jnp.bfloat16

D_MODEL = 1024
HEAD_DIM = 64
ROT_DIM = 16
ROPE_THETA = 500000.0
BLOCK = 128
LANES = 128
N_MEM = 256
RMS_EPS = 1e-6
SCALE = HEAD_DIM ** -0.5
A_HEADS, A_GROUP = 6, 3
B_HEADS = 6
C_HEADS = 4
A_W, A_KV_W, B_W, C_W = 384, 128, 384, 256
D_IN = 3072
N_DEV = 8
SHARD_IN = D_IN // N_DEV
SHARD_ROWS = D_MODEL // N_DEV
B_CONFIGS = ((128, 1), (512, 4), (2048, 16))
DILS = (4, 16)
NEG = -1e30
ATTN_BLOCKS_PER_STEP = 4
DELTA_LANE = 64
VMEM_LIMIT = 56 * 1024 * 1024

ADAM_LR, ADAM_B1, ADAM_B2, ADAM_EPS, ADAM_WD, ADAM_STEP = 0.001, 0.9, 0.999, 1e-08, 0.01, 10
MESH_ID = pl.DeviceIdType.MESH


def _params(**kw):
    return pltpu.CompilerParams(vmem_limit_bytes=VMEM_LIMIT, **kw)


def _full(shape):
    n = len(shape)
    return pl.BlockSpec(shape, lambda *_: (0,) * n)


def _row(tm, w):
    return pl.BlockSpec((tm, w), lambda i: (i, 0))


def _mesh_pos():
    return lax.axis_index("x"), lax.axis_index("y"), lax.axis_index("c")


def _dev_index(pos):
    return 4 * pos[0] + 2 * pos[1] + pos[2]


def _xor_peer(pos, s):
    x, y, c = pos
    return (1 - x if s & 4 else x, 1 - y if s & 2 else y, 1 - c if s & 1 else c)


def _perm_view(a, dil):
    return a.reshape(a.shape[0] // (BLOCK * dil), dil, BLOCK, a.shape[1])


def _perm_spec(tm, dil, w):
    per = BLOCK * dil // tm
    return pl.BlockSpec((1, dil, tm // dil, w), lambda i: (i // per, 0, i % per, 0))


def _put(scr, val):
    for c in range(val.shape[1] // LANES):
        scr[c] = val[:, LANES * c:LANES * (c + 1)]


def _get(scr):
    n = scr.shape[0]
    return scr[0] if n == 1 else jnp.concatenate([scr[c] for c in range(n)], axis=1)


def _get_class(scr, r, dil):
    n, rows = scr.shape[0], scr.shape[1]
    parts = [scr.at[c][pl.ds(r, rows // dil, stride=dil), :] for c in range(n)]
    return parts[0] if n == 1 else jnp.concatenate(parts, axis=1)


def _store_permuted(scr, out_ref, dil, dtype):
    for r in range(dil):
        out_ref[0, r] = _get_class(scr, r, dil).astype(dtype)


def _fill_permuted(in_ref, scr, dil):
    n, rows = scr.shape[0], scr.shape[1]
    for r in range(dil):
        val = in_ref[0, r].astype(F32)
        for c in range(n):
            scr.at[c][pl.ds(r, rows // dil, stride=dil), :] = val[:, LANES * c:LANES * (c + 1)]


def _load_permuted(in_ref, scr, dil):
    _fill_permuted(in_ref, scr, dil)
    return _get(scr)


def _rope_tables(seq):
    j = jnp.arange(LANES) % HEAD_DIM
    freq = ROPE_THETA ** (-(2 * (j % (ROT_DIM // 2))).astype(F32) / ROT_DIM)
    ang = jnp.arange(seq, dtype=F32)[:, None] * freq[None, :]
    cos, sin = jnp.cos(ang), jnp.sin(ang)
    half = ROT_DIM // 2
    c = jnp.where(j < ROT_DIM, cos, 1.0)
    up = jnp.where((j >= half) & (j < ROT_DIM), sin, 0.0)
    dn = jnp.where(j < half, -sin, 0.0)
    return c, up, dn


def _rotate128(t, c, up, dn):
    return t * c + pltpu.roll(t, 8, 1) * up + pltpu.roll(t, LANES - 8, 1) * dn


def _rotate(t, c, up, dn):
    outs = [_rotate128(t[:, LANES * j:LANES * (j + 1)], c, up, dn) for j in range(t.shape[1] // LANES)]
    return outs[0] if len(outs) == 1 else jnp.concatenate(outs, axis=1)


def _w_in_scratch():
    return [pltpu.VMEM((D_MODEL, D_IN), BF16), pltpu.SemaphoreType.DMA((N_DEV,))]


def _stage_w_in(w_hbm, w_scr, sems):
    @pl.when(pl.program_id(0) == 0)
    def _():
        copies = [pltpu.make_async_copy(w_hbm.at[k], w_scr.at[:, pl.ds(SHARD_IN * k, SHARD_IN)], sems.at[k])
                  for k in range(N_DEV)]
        for cp in copies:
            cp.start()
        for cp in copies:
            cp.wait()


def _inproj(x, pre_g, w_in_full, tabs, w_mem, w_out, tm=512):
    seq = x.shape[0]
    n_chunk = D_IN // LANES
    n_steps = seq // tm

    def body(x_ref, g_ref, w_hbm, c_ref, up_ref, dn_ref, wm_ref, wo_ref, u_ref, qa_ref, ka_ref, va_ref,
             qb1_ref, kb1_ref, vb1_ref, qb4_ref, kb4_ref, vb4_ref, qb16_ref, kb16_ref, vb16_ref,
             qc_ref, gate_ref, wm_all, wo_all, proj, w_scr, w_sems, wm_b, wo_b, send_sems, recv_sems, local_sems):
        step = pl.program_id(0)
        shards, gathered = (wm_b, wo_b), (wm_all, wo_all)

        def gather_copies(arriving):
            pos = _mesh_pos()
            me = _dev_index(pos)
            local = [] if arriving else [
                pltpu.make_async_copy(shards[a], gathered[a].at[me], local_sems.at[a]) for a in range(2)]
            remote = []
            for s in range(1, N_DEV):
                peer = _xor_peer(pos, s)
                for a in range(2):
                    remote.append(pltpu.make_async_remote_copy(
                        src_ref=shards[a], dst_ref=gathered[a].at[_dev_index(peer) if arriving else me],
                        send_sem=send_sems.at[a, s], recv_sem=recv_sems.at[a, s], device_id=peer,
                        device_id_type=MESH_ID))
            return local, remote

        @pl.when(step == 0)
        def _():
            wm_b[...] = wm_ref[...].astype(BF16)
            wo_b[...] = wo_ref[...].astype(BF16)
            local, sends = gather_copies(arriving=False)
            for cp in local + sends:
                cp.start()

        _stage_w_in(w_hbm, w_scr, w_sems)
        xv = x_ref[...]
        r = lax.rsqrt(jnp.mean(xv * xv, axis=-1, keepdims=True) + RMS_EPS)
        u = ((xv * r) * g_ref[...]).astype(BF16)
        u_ref[...] = u
        for n0 in range(0, D_IN, D_MODEL):
            acc = jnp.dot(u, w_scr[:, n0:n0 + D_MODEL], preferred_element_type=F32)
            for c3 in range(D_MODEL // LANES):
                proj[n0 // LANES + c3] = acc[:, LANES * c3:LANES * (c3 + 1)]
        c, up, dn = c_ref[...], up_ref[...], dn_ref[...]

        def cols(lo, hi, rot=False, scale=None):
            parts = []
            for ch in range(lo // LANES, hi // LANES):
                t = proj[ch]
                if rot:
                    t = _rotate128(t, c, up, dn)
                if scale is not None:
                    t = t * scale
                parts.append(t)
            return parts[0] if len(parts) == 1 else jnp.concatenate(parts, axis=1)

        qa_ref[...] = cols(0, 384, True, SCALE).astype(BF16)
        ka_ref[...] = cols(384, 512, True).astype(BF16)
        va_ref[...] = cols(512, 640).astype(BF16)
        gate_ref[:, 0:384] = cols(640, 1024).astype(BF16)
        gate_ref[:, 384:768] = cols(2176, 2560).astype(BF16)
        gate_ref[:, 768:1024] = cols(2816, 3072).astype(BF16)
        qc_ref[...] = cols(2560, 2816, False, SCALE).astype(BF16)
        for ch in range(1024 // LANES, 1408 // LANES):
            proj[ch] = _rotate128(proj[ch], c, up, dn) * SCALE
        for ch in range(1408 // LANES, 1792 // LANES):
            proj[ch] = _rotate128(proj[ch], c, up, dn)
        for lo, nat, p4, p16 in ((1024, qb1_ref, qb4_ref, qb16_ref), (1408, kb1_ref, kb4_ref, kb16_ref),
                                 (1792, vb1_ref, vb4_ref, vb16_ref)):
            chunks = range(lo // LANES, lo // LANES + B_W // LANES)
            nat[...] = jnp.concatenate([proj[ch] for ch in chunks], axis=1).astype(BF16)
            for dil, ref in ((4, p4), (16, p16)):
                for rr in range(dil):
                    ref[0, rr] = jnp.concatenate(
                        [proj.at[ch][pl.ds(rr, tm // dil, stride=dil), :] for ch in chunks], axis=1).astype(BF16)

        @pl.when(step == n_steps - 1)
        def _():
            for cp in gather_copies(arriving=True)[1]:
                cp.wait_recv()
            local, sends = gather_copies(arriving=False)
            for cp in sends:
                cp.wait_send()
            for cp in local:
                cp.wait()

    nat_w = (D_MODEL, A_W, A_KV_W, A_KV_W, B_W, B_W, B_W)
    out_specs = [_row(tm, w) for w in nat_w]
    out_shape = [jax.ShapeDtypeStruct((seq, w), BF16) for w in nat_w]
    for dil in DILS:
        out_specs += [_perm_spec(tm, dil, B_W)] * 3
        out_shape += [jax.ShapeDtypeStruct((seq // (BLOCK * dil), dil, BLOCK, B_W), BF16)] * 3
    hbm = pl.BlockSpec(memory_space=pl.ANY)
    out_specs += [_row(tm, C_W), _row(tm, D_MODEL), hbm, hbm]
    out_shape += [jax.ShapeDtypeStruct((seq, C_W), BF16), jax.ShapeDtypeStruct((seq, D_MODEL), BF16),
                  jax.ShapeDtypeStruct((N_DEV,) + w_mem.shape, BF16), jax.ShapeDtypeStruct((N_DEV,) + w_out.shape, BF16)]
    res = pl.pallas_call(
        body, name="inproj", grid=(n_steps,),
        in_specs=[_row(tm, D_MODEL), _full((1, D_MODEL)), hbm, _row(tm, LANES), _row(tm, LANES), _row(tm, LANES),
                  _full(w_mem.shape), _full(w_out.shape)],
        out_specs=out_specs, out_shape=out_shape,
        scratch_shapes=[pltpu.VMEM((n_chunk, tm, LANES), F32)] + _w_in_scratch()
        + [pltpu.VMEM(w_mem.shape, BF16), pltpu.VMEM(w_out.shape, BF16), pltpu.SemaphoreType.DMA((2, N_DEV)),
           pltpu.SemaphoreType.DMA((2, N_DEV)), pltpu.SemaphoreType.DMA((2,))],
        compiler_params=_params(dimension_semantics=("arbitrary",)),
    )(x, pre_g, w_in_full, *tabs, w_mem, w_out)
    u, qa, ka, va = res[0:4]
    qkv_b = {1: res[4:7], 4: [t.reshape(seq, B_W) for t in res[7:10]], 16: [t.reshape(seq, B_W) for t in res[10:13]]}
    return u, qa, ka, va, qkv_b, res[13], res[14], res[15], res[16]


def _memkv_fwd(mem, mem_g, w_mem_full):
    def body(mem_ref, g_ref, w_ref, mn_ref, mk_ref, mv_ref):
        mv_ = mem_ref[...]
        r = lax.rsqrt(jnp.mean(mv_ * mv_, axis=-1, keepdims=True) + RMS_EPS)
        mn = ((mv_ * r) * g_ref[...]).astype(BF16)
        mn_ref[...] = mn
        mkv = jnp.dot(mn, w_ref[...], preferred_element_type=F32)
        mk_ref[...] = mkv[:, 0:C_W].astype(BF16)
        mv_ref[...] = mkv[:, C_W:2 * C_W].astype(BF16)

    return pl.pallas_call(
        body, name="memkv_fwd",
        out_shape=[jax.ShapeDtypeStruct((N_MEM, D_MODEL), BF16),
                   jax.ShapeDtypeStruct((N_MEM, C_W), BF16), jax.ShapeDtypeStruct((N_MEM, C_W), BF16)],
        compiler_params=_params(),
    )(mem, mem_g, w_mem_full)


def _memkv_bwd(mem, mem_g, mn, w_mem_full, dmk, dmv):
    def body(mem_ref, g_ref, mn_ref, w_ref, dmk_ref, dmv_ref, dw_ref, st_ref):
        dmkv = jnp.concatenate([dmk_ref[...], dmv_ref[...]], axis=1).astype(BF16)
        dw_ref[...] = lax.dot_general(mn_ref[...], dmkv, (((0,), (0,)), ((), ())), preferred_element_type=F32)
        dmn = lax.dot_general(dmkv, w_ref[...], (((1,), (1,)), ((), ())), preferred_element_type=F32)
        mv_ = mem_ref[...]
        r = lax.rsqrt(jnp.mean(mv_ * mv_, axis=-1, keepdims=True) + RMS_EPS)
        st_ref[...] = jnp.zeros_like(st_ref)
        st_ref[0:1, :] = jnp.sum(dmn * (mv_ * r), axis=0, keepdims=True)

    return pl.pallas_call(
        body, name="memkv_bwd",
        out_shape=[jax.ShapeDtypeStruct((D_MODEL, 2 * C_W), F32), jax.ShapeDtypeStruct((8, D_MODEL), F32)],
        compiler_params=_params(),
    )(mem, mem_g, mn, w_mem_full, dmk, dmv)


def _band_mask(has_prev, max_dist):
    qi = lax.broadcasted_iota(jnp.int32, (BLOCK, 2 * BLOCK), 0)
    kj = lax.broadcasted_iota(jnp.int32, (BLOCK, 2 * BLOCK), 1)
    dist = qi + BLOCK - kj
    return (dist >= 0) & (dist <= max_dist) & ((kj >= BLOCK) | has_prev)


_NT = (((1,), (1,)), ((), ()))
_TN = (((0,), (0,)), ((), ()))


def _head_only(val, h):
    slab = val[:, LANES * (h // 2):LANES * (h // 2 + 1)]
    lane = lax.broadcasted_iota(jnp.int32, slab.shape, 1)
    keep = (lane < HEAD_DIM) if h % 2 == 0 else (lane >= HEAD_DIM)
    return jnp.where(keep, slab, jnp.zeros((), slab.dtype))


class _KvSlabs:
    def __init__(self, cat, group):
        self.cat, self.group, self.swapped = cat, group, {}

    def is_swapped(self, h):
        return (h // self.group) % 2 != h % 2

    def __call__(self, h):
        j = (h // self.group) // 2
        slab = self.cat[:, LANES * j:LANES * (j + 1)]
        if not self.is_swapped(h):
            return slab
        if j not in self.swapped:
            self.swapped[j] = jnp.concatenate([slab[:, HEAD_DIM:], slab[:, :HEAD_DIM]], axis=1)
        return self.swapped[j]


class _BandSteps:
    def __init__(self, seq, dil, nq):
        self.nq, self.rows, self.consecutive = nq, nq * BLOCK, dil == 1
        nb = seq // dil // BLOCK
        if self.consecutive:
            assert nb % nq == 0
            self.outer, self.inner, self.stride = 1, nb // nq, 1
        else:
            assert dil % nq == 0
            self.outer, self.inner, self.stride = dil // nq, nb, dil // nq

    def own(self, w, clamp=False):
        cur = (lambda i: jnp.minimum(i, self.inner - 1)) if clamp else (lambda i: i)
        return pl.BlockSpec((self.rows, w), lambda r, i: (cur(i) * self.stride + r, 0))

    def prev(self, w, clamp=False):
        cur = (lambda i: jnp.minimum(i, self.inner - 1)) if clamp else (lambda i: i)
        if self.consecutive:
            return pl.BlockSpec((BLOCK, w), lambda r, i: (jnp.maximum(cur(i) * self.nq - 1, 0), 0))
        return pl.BlockSpec((self.rows, w), lambda r, i: (jnp.maximum(cur(i) - 1, 0) * self.stride + r, 0))

    def late(self, w):
        return pl.BlockSpec((self.rows, w), lambda r, i: (jnp.maximum(i - 1, 0) * self.stride + r, 0))

    def rows_of(self, j):
        return slice(BLOCK * j, BLOCK * (j + 1))

    def keys(self, p_ref, c_ref, j):
        if not self.consecutive:
            before = p_ref[self.rows_of(j), :]
        elif j == 0:
            before = p_ref[...]
        else:
            before = c_ref[self.rows_of(j - 1), :]
        return jnp.concatenate([before, c_ref[self.rows_of(j), :]], axis=0)

    def has_prev(self, i, j):
        return True if (self.consecutive and j > 0) else (i > 0)


def _banded_fwd(q, k, v, sink, *, dil, heads, group, max_dist, nq, name):
    seq = q.shape[0]
    kvh = heads // group
    qw, kw = heads * HEAD_DIM, kvh * HEAD_DIM
    steps = _BandSteps(seq, dil, nq)

    def body(*refs):
        if sink is not None:
            sink_ref, refs = refs[0], refs[1:]
        q_ref, kp_ref, kc_ref, vp_ref, vc_ref, o_ref, lse_ref, s_scr, p_scr = refs
        i = pl.program_id(1)
        lane = lax.broadcasted_iota(jnp.int32, (BLOCK, LANES), 1)
        k_of = [_KvSlabs(steps.keys(kp_ref, kc_ref, j), group) for j in range(nq)]
        v_of = [_KvSlabs(steps.keys(vp_ref, vc_ref, j), group) for j in range(nq)]
        for j in range(nq):
            qv = q_ref[steps.rows_of(j), :]
            for h in range(heads):
                s_scr[j * heads + h] = lax.dot_general(_head_only(qv, h), k_of[j](h), _NT, preferred_element_type=F32)
        ls = {}
        for j in range(nq):
            valid = _band_mask(steps.has_prev(i, j), max_dist)
            lse_tile = jnp.zeros((BLOCK, LANES), F32)
            for h in range(heads):
                s = jnp.where(valid, s_scr[j * heads + h], NEG)
                m = jnp.max(s, axis=-1, keepdims=True)
                if sink is not None:
                    sk = sink_ref[h]
                    m = jnp.maximum(m, sk)
                p = jnp.exp(s - m)
                l = jnp.sum(p, axis=-1, keepdims=True)
                if sink is not None:
                    l = l + jnp.exp(sk - m)
                p_scr[j * heads + h] = p.astype(BF16)
                ls[j, h] = l
                lse_tile = jnp.where(lane == h, m + jnp.log(l), lse_tile)
            lse_ref[steps.rows_of(j), :] = lse_tile
        for j in range(nq):
            for pr in range(heads // 2):
                he, ho = 2 * pr, 2 * pr + 1
                even = jnp.dot(p_scr[j * heads + he], v_of[j](he), preferred_element_type=F32) / ls[j, he]
                odd = jnp.dot(p_scr[j * heads + ho], v_of[j](ho), preferred_element_type=F32) / ls[j, ho]
                o_ref[steps.rows_of(j), LANES * pr:LANES * (pr + 1)] = jnp.where(lane < HEAD_DIM, even, odd).astype(BF16)

    in_specs = [steps.own(qw), steps.prev(kw), steps.own(kw), steps.prev(kw), steps.own(kw)]
    args = [q, k, k, v, v]
    if sink is not None:
        in_specs = [pl.BlockSpec(memory_space=pltpu.SMEM)] + in_specs
        args = [sink] + args
    return pl.pallas_call(
        body, name=name, grid=(steps.outer, steps.inner), in_specs=in_specs,
        out_specs=[steps.own(qw), steps.own(LANES)],
        out_shape=[jax.ShapeDtypeStruct((seq, qw), BF16), jax.ShapeDtypeStruct((seq, LANES), F32)],
        scratch_shapes=[pltpu.VMEM((nq * heads, BLOCK, 2 * BLOCK), F32), pltpu.VMEM((nq * heads, BLOCK, 2 * BLOCK), BF16)],
        compiler_params=_params(dimension_semantics=("arbitrary", "arbitrary")),
    )(*args)


def _banded_bwd(q, k, v, d_out, stat, sink, *, dil, heads, group, max_dist, nq, name):
    seq = q.shape[0]
    kvh = heads // group
    qw, kw = heads * HEAD_DIM, kvh * HEAD_DIM
    steps = _BandSteps(seq, dil, nq)
    n_in = 7

    def body(*refs):
        if sink is not None:
            sink_ref, refs = refs[0], refs[1:]
            dsink_ref, refs = refs[n_in], refs[:n_in] + refs[n_in + 1:]
        (q_ref, kp_ref, kc_ref, vp_ref, vc_ref, do_ref, st_ref, dq_ref, dk_ref, dv_ref,
         kcar, vcar, s_scr, dp_scr, p_scr, ds_scr) = refs
        r, i = pl.program_id(0), pl.program_id(1)

        @pl.when(i == 0)
        def _():
            kcar[...] = jnp.zeros_like(kcar)
            vcar[...] = jnp.zeros_like(vcar)

        if sink is not None:
            @pl.when((i == 0) & (r == 0))
            def _():
                dsink_ref[...] = jnp.zeros_like(dsink_ref)

        @pl.when(i < steps.inner)
        def _():
            lane = lax.broadcasted_iota(jnp.int32, (1, LANES), 1)
            lane_q = lax.broadcasted_iota(jnp.int32, (BLOCK, LANES), 1)
            k_of = [_KvSlabs(steps.keys(kp_ref, kc_ref, j), group) for j in range(nq)]
            v_of = [_KvSlabs(steps.keys(vp_ref, vc_ref, j), group) for j in range(nq)]
            qms, doms = {}, {}
            for j in range(nq):
                qv, dov = q_ref[steps.rows_of(j), :], do_ref[steps.rows_of(j), :]
                for h in range(heads):
                    qms[j, h], doms[j, h] = _head_only(qv, h), _head_only(dov, h)
                    s_scr[j * heads + h] = lax.dot_general(qms[j, h], k_of[j](h), _NT, preferred_element_type=F32)
                    dp_scr[j * heads + h] = lax.dot_general(doms[j, h], v_of[j](h), _NT, preferred_element_type=F32)
            dsink_row = jnp.zeros((1, LANES), F32)
            for j in range(nq):
                st = st_ref[steps.rows_of(j), :]
                valid = _band_mask(steps.has_prev(i, j), max_dist)
                for h in range(heads):
                    lse_h = st[:, h:h + 1]
                    delta = st[:, DELTA_LANE + h:DELTA_LANE + h + 1]
                    p = jnp.where(valid, jnp.exp(s_scr[j * heads + h] - lse_h), 0.0)
                    p_scr[j * heads + h] = p.astype(BF16)
                    ds_scr[j * heads + h] = (p * (dp_scr[j * heads + h] - delta)).astype(BF16)
                    if sink is not None:
                        ds_sink = jnp.sum(-jnp.exp(sink_ref[h] - lse_h) * delta, axis=0, keepdims=True)
                        dsink_row = dsink_row + jnp.where(lane == h, ds_sink, 0.0)
            for j in range(nq):
                for pr in range(heads // 2):
                    he, ho = 2 * pr, 2 * pr + 1
                    even = jnp.dot(ds_scr[j * heads + he], k_of[j](he), preferred_element_type=F32)
                    odd = jnp.dot(ds_scr[j * heads + ho], k_of[j](ho), preferred_element_type=F32)
                    dq_ref[steps.rows_of(j), LANES * pr:LANES * (pr + 1)] = (
                        jnp.where(lane_q < HEAD_DIM, even, odd).astype(BF16))
            if steps.consecutive:
                dk_ref[...] = kcar[...].astype(BF16)
                dv_ref[...] = vcar[...].astype(BF16)
            for j in range(nq):
                for slab in range(kw // LANES):
                    acc = {}
                    for h in range(heads):
                        if (h // group) // 2 != slab:
                            continue
                        key = k_of[j].is_swapped(h)
                        dk_h = lax.dot_general(ds_scr[j * heads + h], qms[j, h], _TN, preferred_element_type=F32)
                        dv_h = lax.dot_general(p_scr[j * heads + h], doms[j, h], _TN, preferred_element_type=F32)
                        acc[key] = (dk_h, dv_h) if key not in acc else (acc[key][0] + dk_h, acc[key][1] + dv_h)
                    dk_j, dv_j = acc.get(False, (None, None))
                    if True in acc:
                        unswap = lambda t: jnp.concatenate([t[:, HEAD_DIM:], t[:, :HEAD_DIM]], axis=1)
                        dk_s, dv_s = unswap(acc[True][0]), unswap(acc[True][1])
                        dk_j = dk_s if dk_j is None else dk_j + dk_s
                        dv_j = dv_s if dv_j is None else dv_j + dv_s
                    sl = slice(LANES * slab, LANES * (slab + 1))
                    own_rows = steps.rows_of(j)
                    if not steps.consecutive:
                        dk_ref[own_rows, sl] = (kcar[own_rows, sl] + dk_j[0:BLOCK]).astype(BF16)
                        dv_ref[own_rows, sl] = (vcar[own_rows, sl] + dv_j[0:BLOCK]).astype(BF16)
                    elif j == 0:
                        last = steps.rows_of(nq - 1)
                        dk_ref[last, sl] = (kcar[last, sl] + dk_j[0:BLOCK]).astype(BF16)
                        dv_ref[last, sl] = (vcar[last, sl] + dv_j[0:BLOCK]).astype(BF16)
                    else:
                        before = steps.rows_of(j - 1)
                        kcar[before, sl] += dk_j[0:BLOCK]
                        vcar[before, sl] += dv_j[0:BLOCK]
                    kcar[own_rows, sl] = dk_j[BLOCK:2 * BLOCK]
                    vcar[own_rows, sl] = dv_j[BLOCK:2 * BLOCK]
            if sink is not None:
                dsink_ref[0:1, :] += dsink_row

        @pl.when(i == steps.inner)
        def _():
            dk_ref[...] = kcar[...].astype(BF16)
            dv_ref[...] = vcar[...].astype(BF16)

    own, prev = (lambda w: steps.own(w, clamp=True)), (lambda w: steps.prev(w, clamp=True))
    in_specs = [own(qw), prev(kw), own(kw), prev(kw), own(kw), own(qw), own(LANES)]
    args = [q, k, k, v, v, d_out, stat]
    out_specs = [own(qw), steps.late(kw), steps.late(kw)]
    out_shape = [jax.ShapeDtypeStruct((seq, qw), BF16), jax.ShapeDtypeStruct((seq, kw), BF16),
                 jax.ShapeDtypeStruct((seq, kw), BF16)]
    if sink is not None:
        in_specs = [pl.BlockSpec(memory_space=pltpu.SMEM)] + in_specs
        args = [sink] + args
        out_specs = [_full((8, LANES))] + out_specs
        out_shape = [jax.ShapeDtypeStruct((8, LANES), F32)] + out_shape
    n_hb = nq * heads
    res = pl.pallas_call(
        body, name=name, grid=(steps.outer, steps.inner + 1), in_specs=in_specs, out_specs=out_specs,
        out_shape=out_shape,
        scratch_shapes=[pltpu.VMEM((steps.rows, kw), F32), pltpu.VMEM((steps.rows, kw), F32)]
        + [pltpu.VMEM((n_hb, BLOCK, 2 * BLOCK), F32)] * 2 + [pltpu.VMEM((n_hb, BLOCK, 2 * BLOCK), BF16)] * 2,
        compiler_params=_params(dimension_semantics=("arbitrary", "arbitrary")),
    )(*args)
    if sink is not None:
        return res[1], res[2], res[3], res[0]
    return res


def _cross_fwd(q, mk, mv, tq=512):
    seq = q.shape[0]

    def body(q_ref, mk_ref, mv_ref, o_ref, lse_ref, s_scr, p_scr):
        qv = q_ref[...]
        k_of, v_of = _KvSlabs(mk_ref[...], 1), _KvSlabs(mv_ref[...], 1)
        lane = lax.broadcasted_iota(jnp.int32, (tq, LANES), 1)
        lse_tile = jnp.zeros((tq, LANES), F32)
        for h in range(C_HEADS):
            s_scr[h] = lax.dot_general(_head_only(qv, h), k_of(h), _NT, preferred_element_type=F32)
        ls = []
        for h in range(C_HEADS):
            s = s_scr[h]
            m = jnp.max(s, axis=-1, keepdims=True)
            p = jnp.exp(s - m)
            l = jnp.sum(p, axis=-1, keepdims=True)
            p_scr[h] = p.astype(BF16)
            ls.append(l)
            lse_tile = jnp.where(lane == h, m + jnp.log(l), lse_tile)
        for pr in range(C_HEADS // 2):
            even = jnp.dot(p_scr[2 * pr], v_of(2 * pr), preferred_element_type=F32) / ls[2 * pr]
            odd = jnp.dot(p_scr[2 * pr + 1], v_of(2 * pr + 1), preferred_element_type=F32) / ls[2 * pr + 1]
            o_ref[:, LANES * pr:LANES * (pr + 1)] = jnp.where(lane < HEAD_DIM, even, odd).astype(BF16)
        lse_ref[...] = lse_tile

    return pl.pallas_call(
        body, name="cross_fwd", grid=(seq // tq,),
        in_specs=[_row(tq, C_W), _full((N_MEM, C_W)), _full((N_MEM, C_W))],
        out_specs=[_row(tq, C_W), _row(tq, LANES)],
        out_shape=[jax.ShapeDtypeStruct((seq, C_W), BF16), jax.ShapeDtypeStruct((seq, LANES), F32)],
        scratch_shapes=[pltpu.VMEM((C_HEADS, tq, N_MEM), F32), pltpu.VMEM((C_HEADS, tq, N_MEM), BF16)],
        compiler_params=_params(dimension_semantics=("arbitrary",)),
    )(q, mk, mv)


def _cross_bwd(q, mk, mv, d_out, stat, tq=512):
    seq = q.shape[0]

    def body(q_ref, mk_ref, mv_ref, do_ref, st_ref, dq_ref, dmk_ref, dmv_ref, s_scr, dp_scr, p_scr, ds_scr):
        @pl.when(pl.program_id(0) == 0)
        def _():
            dmk_ref[...] = jnp.zeros_like(dmk_ref)
            dmv_ref[...] = jnp.zeros_like(dmv_ref)

        qv, dov, st = q_ref[...], do_ref[...], st_ref[...]
        k_of, v_of = _KvSlabs(mk_ref[...], 1), _KvSlabs(mv_ref[...], 1)
        qms = [_head_only(qv, h) for h in range(C_HEADS)]
        doms = [_head_only(dov, h) for h in range(C_HEADS)]
        for h in range(C_HEADS):
            s_scr[h] = lax.dot_general(qms[h], k_of(h), _NT, preferred_element_type=F32)
            dp_scr[h] = lax.dot_general(doms[h], v_of(h), _NT, preferred_element_type=F32)
        for h in range(C_HEADS):
            p = jnp.exp(s_scr[h] - st[:, h:h + 1])
            p_scr[h] = p.astype(BF16)
            ds_scr[h] = (p * (dp_scr[h] - st[:, DELTA_LANE + h:DELTA_LANE + h + 1])).astype(BF16)
        lane = lax.broadcasted_iota(jnp.int32, (tq, LANES), 1)
        for pr in range(C_HEADS // 2):
            sl = slice(LANES * pr, LANES * (pr + 1))
            even = jnp.dot(ds_scr[2 * pr], k_of(2 * pr), preferred_element_type=F32)
            odd = jnp.dot(ds_scr[2 * pr + 1], k_of(2 * pr + 1), preferred_element_type=F32)
            dq_ref[:, sl] = jnp.where(lane < HEAD_DIM, even, odd).astype(BF16)
            dmk_ref[:, sl] += (lax.dot_general(ds_scr[2 * pr], qms[2 * pr], _TN, preferred_element_type=F32)
                               + lax.dot_general(ds_scr[2 * pr + 1], qms[2 * pr + 1], _TN, preferred_element_type=F32))
            dmv_ref[:, sl] += (lax.dot_general(p_scr[2 * pr], doms[2 * pr], _TN, preferred_element_type=F32)
                               + lax.dot_general(p_scr[2 * pr + 1], doms[2 * pr + 1], _TN, preferred_element_type=F32))

    return pl.pallas_call(
        body, name="cross_bwd", grid=(seq // tq,),
        in_specs=[_row(tq, C_W), _full((N_MEM, C_W)), _full((N_MEM, C_W)), _row(tq, C_W), _row(tq, LANES)],
        out_specs=[_row(tq, C_W), _full((N_MEM, C_W)), _full((N_MEM, C_W))],
        out_shape=[jax.ShapeDtypeStruct((seq, C_W), BF16), jax.ShapeDtypeStruct((N_MEM, C_W), F32),
                   jax.ShapeDtypeStruct((N_MEM, C_W), F32)],
        scratch_shapes=[pltpu.VMEM((C_HEADS, tq, N_MEM), F32)] * 2 + [pltpu.VMEM((C_HEADS, tq, N_MEM), BF16)] * 2,
        compiler_params=_params(dimension_semantics=("arbitrary",)),
    )(q, mk, mv, d_out, stat)


def _per_head(tile, width):
    rows = tile.shape[0]
    return jnp.concatenate(
        [jnp.broadcast_to(tile[:, h:h + 1], (rows, HEAD_DIM)) for h in range(width // HEAD_DIM)], axis=1)


def _with_delta(lse_tile, prod):
    rows = lse_tile.shape[0]
    lane = lax.broadcasted_iota(jnp.int32, (rows, LANES), 1)
    tile = lse_tile
    for h in range(prod.shape[1] // HEAD_DIM):
        d = jnp.sum(prod[:, HEAD_DIM * h:HEAD_DIM * (h + 1)], axis=-1, keepdims=True)
        tile = jnp.where(lane == DELTA_LANE + h, d, tile)
    return tile


def _mid(oa, lse_a, ob, lse_b, oc, lse_c, gate, x, target, w_out_full, post_g, tm=256):
    seq = x.shape[0]
    n_b = B_W // LANES

    def body(oa_ref, la_ref, b1_ref, l1_ref, b4_ref, l4_ref, b16_ref, l16_ref, oc_ref, lc_ref,
             gate_ref, x_ref, t_ref, w_ref, pg_ref,
             dh_ref, dg_ref, doa_ref, sa_ref, dob1_ref, sb1_ref, dob4_ref, sb4_ref, dob16_ref, sb16_ref,
             doc_ref, sc_ref, dw_ref, st_ref, scr_b4, scr_b16, scr_l4, scr_l16, scr_do, scr_sb):
        @pl.when(pl.program_id(0) == 0)
        def _():
            dw_ref[...] = jnp.zeros_like(dw_ref)
            st_ref[...] = jnp.zeros_like(st_ref)

        b1, l1 = b1_ref[...].astype(F32), l1_ref[...]
        b4, l4 = _load_permuted(b4_ref, scr_b4, 4), _load_permuted(l4_ref, scr_l4, 4)
        b16, l16 = _load_permuted(b16_ref, scr_b16, 16), _load_permuted(l16_ref, scr_l16, 16)
        lm = jnp.maximum(jnp.maximum(l1, l4), l16)
        e1, e4, e16 = jnp.exp(l1 - lm), jnp.exp(l4 - lm), jnp.exp(l16 - lm)
        den = e1 + e4 + e16
        lse_b_tile = lm + jnp.log(den)
        ob_v = _per_head(e1 / den, B_W) * b1 + _per_head(e4 / den, B_W) * b4 + _per_head(e16 / den, B_W) * b16
        o_all = jnp.concatenate([oa_ref[...].astype(F32), ob_v, oc_ref[...].astype(F32)], axis=1)
        g = gate_ref[...].astype(F32)
        sig = 1.0 / (1.0 + jnp.exp(-g))
        silu = g * sig
        y = (o_all * silu).astype(BF16)
        w = w_ref[...]
        z = jnp.dot(y, w, preferred_element_type=F32)
        rz = lax.rsqrt(jnp.mean(z * z, axis=-1, keepdims=True) + RMS_EPS)
        hn = z * rz
        pg = pg_ref[...]
        err = (x_ref[...] + hn * pg) - t_ref[...]
        loss = 0.5 * jnp.sum(jnp.mean(err * err, axis=-1, keepdims=True), axis=0, keepdims=True)
        dh = err * (1.0 / D_MODEL)
        dh_ref[...] = dh.astype(BF16)
        st_ref[0:1, :] += jnp.sum(dh * hn, axis=0, keepdims=True)
        st_ref[1:2, :] += jnp.broadcast_to(loss, (1, D_MODEL))
        dhn = dh * pg
        dz = (rz * (dhn - hn * jnp.mean(dhn * hn, axis=-1, keepdims=True))).astype(BF16)
        dy = lax.dot_general(dz, w, _NT, preferred_element_type=F32)
        dw_ref[...] += lax.dot_general(y, dz, _TN, preferred_element_type=F32)
        dg_ref[...] = (dy * o_all * (sig * (1.0 + g * (1.0 - sig)))).astype(BF16)
        d_o = (dy * silu).astype(BF16)
        prod = d_o.astype(F32) * o_all
        doa_ref[...] = d_o[:, 0:A_W]
        sa_ref[...] = _with_delta(la_ref[...], prod[:, 0:A_W])
        doc_ref[...] = d_o[:, A_W + B_W:D_MODEL]
        sc_ref[...] = _with_delta(lc_ref[...], prod[:, A_W + B_W:D_MODEL])
        d_ob = d_o[:, A_W:A_W + B_W]
        stat_b = _with_delta(lse_b_tile, prod[:, A_W:A_W + B_W])
        dob1_ref[...] = d_ob
        sb1_ref[...] = stat_b
        _put(scr_do, d_ob.astype(F32))
        _put(scr_sb, stat_b)
        _store_permuted(scr_do, dob4_ref, 4, BF16)
        _store_permuted(scr_sb, sb4_ref, 4, F32)
        _store_permuted(scr_do, dob16_ref, 16, BF16)
        _store_permuted(scr_sb, sb16_ref, 16, F32)

    p4 = lambda w: _perm_spec(tm, 4, w)
    p16 = lambda w: _perm_spec(tm, 16, w)
    in_specs = [_row(tm, A_W), _row(tm, LANES), _row(tm, B_W), _row(tm, LANES), p4(B_W), p4(LANES), p16(B_W), p16(LANES),
                _row(tm, C_W), _row(tm, LANES), _row(tm, D_MODEL), _row(tm, D_MODEL), _row(tm, D_MODEL),
                _full((D_MODEL, D_MODEL)), _full((1, D_MODEL))]
    sds = jax.ShapeDtypeStruct
    v4 = lambda w, dt: sds((seq // (BLOCK * 4), 4, BLOCK, w), dt)
    v16 = lambda w, dt: sds((seq // (BLOCK * 16), 16, BLOCK, w), dt)
    out_specs = [_row(tm, D_MODEL), _row(tm, D_MODEL), _row(tm, A_W), _row(tm, LANES), _row(tm, B_W), _row(tm, LANES),
                 p4(B_W), p4(LANES), p16(B_W), p16(LANES), _row(tm, C_W), _row(tm, LANES),
                 _full((D_MODEL, D_MODEL)), _full((8, D_MODEL))]
    out_shape = [sds((seq, D_MODEL), BF16), sds((seq, D_MODEL), BF16), sds((seq, A_W), BF16), sds((seq, LANES), F32),
                 sds((seq, B_W), BF16), sds((seq, LANES), F32), v4(B_W, BF16), v4(LANES, F32), v16(B_W, BF16),
                 v16(LANES, F32), sds((seq, C_W), BF16), sds((seq, LANES), F32),
                 sds((D_MODEL, D_MODEL), F32), sds((8, D_MODEL), F32)]
    res = pl.pallas_call(
        body, name="mid", grid=(seq // tm,), in_specs=in_specs, out_specs=out_specs, out_shape=out_shape,
        scratch_shapes=[pltpu.VMEM((n_b, tm, LANES), F32), pltpu.VMEM((n_b, tm, LANES), F32),
                        pltpu.VMEM((1, tm, LANES), F32), pltpu.VMEM((1, tm, LANES), F32),
                        pltpu.VMEM((n_b, tm, LANES), F32), pltpu.VMEM((1, tm, LANES), F32)],
        compiler_params=_params(dimension_semantics=("arbitrary",)),
    )(oa, lse_a, ob[1], lse_b[1], _perm_view(ob[4], 4), _perm_view(lse_b[4], 4), _perm_view(ob[16], 16),
      _perm_view(lse_b[16], 16), oc, lse_c, gate, x, target, w_out_full, post_g)
    dh, d_gate, do_a, st_a, do_b1, st_b1, do_b4, st_b4, do_b16, st_b16, do_c, st_c, d_wout, stats = res
    flat = lambda t: t.reshape(seq, t.shape[-1])
    d_b = {1: (do_b1, st_b1), 4: (flat(do_b4), flat(st_b4)), 16: (flat(do_b16), flat(st_b16))}
    return dh, d_gate, (do_a, st_a), d_b, (do_c, st_c), d_wout, stats


def _inproj_bwd_x(x, dh, pre_g, w_in_full, tabs, dqa, dka, dva, dqkv_b, dqc, dgate, tm=256):
    seq = x.shape[0]
    n_b = B_W // LANES

    def body(x_ref, dh_ref, g_ref, w_hbm, c_ref, up_ref, dn_ref, dqa_ref, dka_ref, dva_ref,
             dq1, dk1, dv1, dq4, dk4, dv4, dq16, dk16, dv16, dqc_ref, dg_ref,
             gx_ref, dp_ref, st_ref, scr4, scr16, w_scr, w_sems):
        _stage_w_in(w_hbm, w_scr, w_sems)

        @pl.when(pl.program_id(0) == 0)
        def _():
            st_ref[...] = jnp.zeros_like(st_ref)

        c, up, dn = c_ref[...], -up_ref[...], -dn_ref[...]
        unrot = lambda t: _rotate(t, c, up, dn)
        total = lambda r1, r4, r16: (r1[...].astype(F32) + _load_permuted(r4, scr4, 4)
                                     + _load_permuted(r16, scr16, 16))
        dp_ref[:, 0:384] = (unrot(dqa_ref[...].astype(F32)) * SCALE).astype(BF16)
        dp_ref[:, 384:512] = unrot(dka_ref[...].astype(F32)).astype(BF16)
        dp_ref[:, 512:640] = dva_ref[...]
        dp_ref[:, 640:1024] = dg_ref[:, 0:384]
        dp_ref[:, 1024:1408] = (unrot(total(dq1, dq4, dq16)) * SCALE).astype(BF16)
        dp_ref[:, 1408:1792] = unrot(total(dk1, dk4, dk16)).astype(BF16)
        dp_ref[:, 1792:2176] = total(dv1, dv4, dv16).astype(BF16)
        dp_ref[:, 2176:2560] = dg_ref[:, 384:768]
        dp_ref[:, 2560:2816] = (dqc_ref[...].astype(F32) * SCALE).astype(BF16)
        dp_ref[:, 2816:3072] = dg_ref[:, 768:1024]
        du = lax.dot_general(dp_ref[...], w_scr[...], _NT, preferred_element_type=F32)
        xv = x_ref[...]
        r = lax.rsqrt(jnp.mean(xv * xv, axis=-1, keepdims=True) + RMS_EPS)
        xh = xv * r
        st_ref[0:1, :] += jnp.sum(du * xh, axis=0, keepdims=True)
        dxh = du * g_ref[...]
        gx_ref[...] = dh_ref[...].astype(F32) + r * (dxh - xh * jnp.mean(dxh * xh, axis=-1, keepdims=True))

    in_specs = ([_row(tm, D_MODEL), _row(tm, D_MODEL), _full((1, D_MODEL)), pl.BlockSpec(memory_space=pl.ANY),
                 _row(tm, LANES), _row(tm, LANES), _row(tm, LANES), _row(tm, A_W), _row(tm, A_KV_W), _row(tm, A_KV_W)]
                + [_row(tm, B_W)] * 3 + [_perm_spec(tm, 4, B_W)] * 3 + [_perm_spec(tm, 16, B_W)] * 3
                + [_row(tm, C_W), _row(tm, D_MODEL)])
    return pl.pallas_call(
        body, name="inproj_bwd_x", grid=(seq // tm,), in_specs=in_specs,
        out_specs=[_row(tm, D_MODEL), _row(tm, D_IN), _full((8, D_MODEL))],
        out_shape=[jax.ShapeDtypeStruct((seq, D_MODEL), F32), jax.ShapeDtypeStruct((seq, D_IN), BF16),
                   jax.ShapeDtypeStruct((8, D_MODEL), F32)],
        scratch_shapes=[pltpu.VMEM((n_b, tm, LANES), F32), pltpu.VMEM((n_b, tm, LANES), F32)] + _w_in_scratch(),
        compiler_params=_params(dimension_semantics=("arbitrary",)),
    )(x, dh, pre_g, w_in_full, *tabs, dqa, dka, dva, *dqkv_b[1], *[_perm_view(t, 4) for t in dqkv_b[4]],
      *[_perm_view(t, 16) for t in dqkv_b[16]], dqc, dgate)


class _ReduceScatter:
    def __init__(self, ins, outs, scratch):
        self.n = n = len(ins)
        self.ins, self.outs = ins, outs
        self.mine, self.got, self.snd, self.rcv = (scratch[n * t:n * (t + 1)] for t in range(4))
        self.load_sems, self.d2d_send, self.d2d_recv, self.ici_send, self.ici_recv = scratch[4 * n:]
        self.pos = _mesh_pos()
        self.pairs = [(a, kk) for kk in (3, 1, 2) for a in range(n)]

    @staticmethod
    def scratch_shapes(shapes):
        return ([pltpu.VMEM((4,) + s, F32) for s in shapes] + [pltpu.VMEM((4,) + s, F32) for s in shapes]
                + [pltpu.VMEM((3,) + s, BF16) for s in shapes] + [pltpu.VMEM((3,) + s, BF16) for s in shapes]
                + [pltpu.SemaphoreType.DMA((len(shapes), 4))] * 5)

    def _chip(self, kk):
        x, y, _ = self.pos
        return (1 - x if kk & 2 else x, 1 - y if kk & 1 else y)

    def _load(self, a, kk):
        block = _dev_index((*self._chip(kk), self.pos[2]))
        return pltpu.make_async_copy(self.ins[a].at[block], self.mine[a].at[kk], self.load_sems.at[a, kk])

    def _swap(self, a, kk):
        x, y, c = self.pos
        return pltpu.make_async_remote_copy(
            src_ref=self.ins[a].at[_dev_index((*self._chip(kk), 1 - c))], dst_ref=self.got[a].at[kk],
            send_sem=self.d2d_send.at[a, kk], recv_sem=self.d2d_recv.at[a, kk],
            device_id=(x, y, 1 - c), device_id_type=MESH_ID)

    def _hop(self, a, kk):
        return pltpu.make_async_remote_copy(
            src_ref=self.snd[a].at[kk - 1], dst_ref=self.rcv[a].at[kk - 1], send_sem=self.ici_send.at[a, kk],
            recv_sem=self.ici_recv.at[a, kk], device_id=(*self._chip(kk), self.pos[2]), device_id_type=MESH_ID)

    def start(self):
        for kk in range(4):
            for a in range(self.n):
                self._load(a, kk).start()
                self._swap(a, kk).start()

    def send_chip_sums(self):
        for a, kk in self.pairs:
            self._load(a, kk).wait()
            self._swap(a, kk).wait_recv()
            self.snd[a][kk - 1] = (self.mine[a][kk] + self.got[a][kk]).astype(BF16)
            self._hop(a, kk).start()

    def finish(self):
        for a in range(self.n):
            self._load(a, 0).wait()
            self._swap(a, 0).wait_recv()
            acc = self.mine[a][0] + self.got[a][0]
            for kk in (1, 2, 3):
                self._hop(a, kk).wait_recv()
                acc = acc + self.rcv[a][kk - 1].astype(F32)
            self.outs[a][...] = acc
        for kk in range(4):
            for a in range(self.n):
                self._swap(a, kk).wait_send()
        for a, kk in self.pairs:
            self._hop(a, kk).wait_send()


def _inproj_bwd_w(u, dproj, d_wmem, d_wout, tm=1024):
    seq = u.shape[0]

    n_steps = seq // tm
    shapes = [d_wmem.shape[1:], d_wout.shape[1:]]

    def body(u_ref, dp_ref, wm, wo, dw_ref, g_wm, g_wo, *scratch):
        step = pl.program_id(0)
        exchange = _ReduceScatter((wm, wo), (g_wm, g_wo), scratch)

        @pl.when(step == 0)
        def _():
            dw_ref[...] = jnp.zeros_like(dw_ref)
            exchange.start()

        @pl.when(step == min(1, n_steps - 1))
        def _():
            exchange.send_chip_sums()

        res = lax.dot_general(u_ref[...], dp_ref[...], _TN, preferred_element_type=F32)
        for k in range(N_DEV):
            dw_ref[k] += res[:, SHARD_IN * k:SHARD_IN * (k + 1)]

        @pl.when(step == n_steps - 1)
        def _():
            exchange.finish()

    hbm = pl.BlockSpec(memory_space=pl.ANY)
    return pl.pallas_call(
        body, name="inproj_bwd_w", grid=(n_steps,),
        in_specs=[_row(tm, D_MODEL), _row(tm, D_IN), hbm, hbm],
        out_specs=[_full((N_DEV, D_MODEL, SHARD_IN))] + [_full(s) for s in shapes],
        out_shape=[jax.ShapeDtypeStruct((N_DEV, D_MODEL, SHARD_IN), F32)] + [jax.ShapeDtypeStruct(s, F32) for s in shapes],
        scratch_shapes=_ReduceScatter.scratch_shapes(shapes),
        compiler_params=_params(dimension_semantics=("arbitrary",)),
    )(u, dproj, d_wmem, d_wout)


def _local_step(x, mem, pre_g, w_in_full, sink, mem_g, w_mem, w_out, post_g, target):
    seq = x.shape[0]
    tabs = _rope_tables(seq)
    u, qa, ka, va, qkv_b, qc, gate, w_mem_all, w_out_all = _inproj(x, pre_g, w_in_full, tabs, w_mem, w_out)
    w_mem_full = w_mem_all.reshape(D_MODEL, 2 * C_W)
    w_out_full = w_out_all.reshape(D_MODEL, D_MODEL)
    mn, mk, mv = _memkv_fwd(mem, mem_g, w_mem_full)

    a_cfg = dict(dil=1, heads=A_HEADS, group=A_GROUP, max_dist=BLOCK - 1, nq=ATTN_BLOCKS_PER_STEP)
    b_cfgs = {dil: dict(dil=dil, heads=B_HEADS, group=1, max_dist=win // dil, nq=ATTN_BLOCKS_PER_STEP)
              for win, dil in B_CONFIGS}
    oa, lse_a = _banded_fwd(qa, ka, va, sink, name="attn_a_fwd", **a_cfg)
    ob, lse_b = {}, {}
    for dil, cfg in b_cfgs.items():
        ob[dil], lse_b[dil] = _banded_fwd(*qkv_b[dil], None, name=f"attn_b{dil}_fwd", **cfg)
    oc, lse_c = _cross_fwd(qc, mk, mv)

    dh, d_gate, d_a, d_b, d_c, d_wout, st_mid = _mid(oa, lse_a, ob, lse_b, oc, lse_c, gate, x, target, w_out_full, post_g)

    dqa, dka, dva, dsink = _banded_bwd(qa, ka, va, *d_a, sink, name="attn_a_bwd", **a_cfg)
    dqkv_b = {dil: _banded_bwd(*qkv_b[dil], *d_b[dil], None, name=f"attn_b{dil}_bwd", **cfg)
              for dil, cfg in b_cfgs.items()}
    dqc, dmk, dmv = _cross_bwd(qc, mk, mv, *d_c)
    d_wmem, st_mem = _memkv_bwd(mem, mem_g, mn, w_mem_full, dmk, dmv)

    grad_x, dproj, st_pre = _inproj_bwd_x(x, dh, pre_g, w_in_full, tabs, dqa, dka, dva, dqkv_b, dqc, d_gate)
    d_win, g_wmem, g_wout = _inproj_bwd_w(u, dproj, d_wmem.reshape(N_DEV, SHARD_ROWS, 2 * C_W),
                                          d_wout.reshape(N_DEV, SHARD_ROWS, D_MODEL))

    dsink_row = jnp.pad(dsink[0:1, :], ((0, 0), (0, D_MODEL - LANES)))
    stats = jnp.concatenate([st_pre[0:1], st_mem[0:1], st_mid[0:1], dsink_row, st_mid[1:2],
                             jnp.zeros((3, D_MODEL), F32)], axis=0)
    return grad_x, d_win, g_wmem, g_wout, stats


def _all_gather_w_in(w_in, parts=2):
    rows = D_MODEL // parts

    def body(win_ref, win_out, win_b, send_sems, recv_sems, local_sems):
        x, y, c = _mesh_pos()
        win_b[...] = win_ref[...].astype(BF16)
        me, sibling = (x, y, c), (x, y, 1 - c)
        chips = [(1 - x, y), (x, 1 - y), (1 - x, 1 - y)]

        def src(a):
            return win_b.at[pl.ds(rows * a, rows)]

        def slot(a, p):
            return win_out.at[_dev_index(p), pl.ds(rows * a, rows)]

        def copy(a, k, block, to, own=False):
            return pltpu.make_async_remote_copy(
                src_ref=src(a) if own else slot(a, block), dst_ref=slot(a, block),
                send_sem=send_sems.at[a, k], recv_sem=recv_sems.at[a, k], device_id=to, device_id_type=MESH_ID)

        mine = [pltpu.make_async_copy(src(a), slot(a, me), local_sems.at[a]) for a in range(parts)]
        for cp in mine:
            cp.start()
        first = []
        for a in range(parts):
            first += [copy(a, 1 + j, me, (*chip, c), own=True) for j, chip in enumerate(chips)]
            first.append(copy(a, 0, me, sibling, own=True))
        for cp in first:
            cp.start()
        passed = []
        for a in range(parts):
            for j, chip in enumerate(chips):
                copy(a, 1 + j, (*chip, c), me).wait_recv()
                fwd = copy(a, 4 + j, (*chip, c), sibling)
                fwd.start()
                passed.append(fwd)
        for a in range(parts):
            copy(a, 0, sibling, me).wait_recv()
            for j, chip in enumerate(chips):
                copy(a, 4 + j, (*chip, 1 - c), me).wait_recv()
        for cp in first + passed:
            cp.wait_send()
        for cp in mine:
            cp.wait()

    return pl.pallas_call(
        body, name="all_gather_w_in",
        in_specs=[pl.BlockSpec(memory_space=pltpu.VMEM)], out_specs=pl.BlockSpec(memory_space=pl.ANY),
        out_shape=jax.ShapeDtypeStruct((N_DEV,) + w_in.shape, BF16),
        scratch_shapes=[pltpu.VMEM(w_in.shape, BF16), pltpu.SemaphoreType.DMA((parts, 7)),
                        pltpu.SemaphoreType.DMA((parts, 7)), pltpu.SemaphoreType.DMA((parts,))],
        compiler_params=_params(),
    )(w_in)


def _exchange_grads(d_win, stats):
    def body(win, st, g_win, r_st, send_sems, recv_sems, local_sem, *scratch):
        exchange = _ReduceScatter((win,), (g_win,), scratch)
        exchange.start()
        pos = _mesh_pos()
        me = _dev_index(pos)
        own = pltpu.make_async_copy(st, r_st.at[me], local_sem)
        own.start()
        copies = []
        for s in range(1, N_DEV):
            peer = _xor_peer(pos, s)
            mk = lambda slot: pltpu.make_async_remote_copy(
                src_ref=st, dst_ref=r_st.at[slot], send_sem=send_sems.at[s], recv_sem=recv_sems.at[s],
                device_id=peer, device_id_type=MESH_ID)
            send, arrival = mk(me), mk(_dev_index(peer))
            send.start()
            copies.append((send, arrival))
        exchange.send_chip_sums()
        exchange.finish()
        for send, arrival in copies:
            arrival.wait_recv()
            send.wait_send()
        own.wait()

    hbm = pl.BlockSpec(memory_space=pl.ANY)
    shard = d_win.shape[1:]
    return pl.pallas_call(
        body, name="exchange_grads", in_specs=[hbm, hbm],
        out_specs=[pl.BlockSpec(memory_space=pltpu.VMEM), hbm],
        out_shape=[jax.ShapeDtypeStruct(shard, F32), jax.ShapeDtypeStruct((N_DEV,) + stats.shape, F32)],
        scratch_shapes=[pltpu.SemaphoreType.DMA((N_DEV,)), pltpu.SemaphoreType.DMA((N_DEV,)), pltpu.SemaphoreType.DMA(())]
        + _ReduceScatter.scratch_shapes([shard]),
        compiler_params=_params(),
    )(d_win, stats)


def _reduce_adamw(recv, w, m, v, *, tr, name):
    n_part, rows, cols = recv.shape

    def body(r_ref, w_ref, m_ref, v_ref, g_ref, d_ref, nm_ref, nv_ref):
        g = r_ref[0]
        for s in range(1, n_part):
            g = g + r_ref[s]
        g_ref[...] = g
        m2 = ADAM_B1 * m_ref[...] + (1.0 - ADAM_B1) * g
        v2 = ADAM_B2 * v_ref[...] + (1.0 - ADAM_B2) * (g * g)
        nm_ref[...] = m2
        nv_ref[...] = v2
        m_hat = m2 / (1.0 - ADAM_B1 ** ADAM_STEP)
        v_hat = v2 / (1.0 - ADAM_B2 ** ADAM_STEP)
        d_ref[...] = -ADAM_LR * (m_hat / (jnp.sqrt(v_hat) + ADAM_EPS) + ADAM_WD * w_ref[...])

    blk = pl.BlockSpec((tr, cols), lambda i: (i, 0))
    return pl.pallas_call(
        body, name=name, grid=(rows // tr,),
        in_specs=[pl.BlockSpec((n_part, tr, cols), lambda i: (0, i, 0)), blk, blk, blk],
        out_specs=[blk] * 4, out_shape=[jax.ShapeDtypeStruct((rows, cols), F32)] * 4,
        compiler_params=_params(dimension_semantics=("arbitrary",)),
    )(recv, w, m, v)


def _pack_rows(pre, memn, post, sink):
    sink_row = jnp.pad(sink, ((0, 0), (0, D_MODEL - A_HEADS)))
    return jnp.concatenate([pre, memn, post, sink_row, jnp.zeros((4, D_MODEL), F32)], axis=0)


def kernel(x, mem, pre_norm, w_in, sink_a, mem_norm, w_mem_kv, w_out, post_norm, loss_target, m_pre_norm, m_w_in, m_sink_a, m_mem_norm, m_w_mem_kv, m_w_out, m_post_norm, v_pre_norm, v_w_in, v_sink_a, v_mem_norm, v_w_mem_kv, v_w_out, v_post_norm):
    w_in_full = _all_gather_w_in(w_in[0])
    sink = jnp.pad(sink_a[0], (0, 8 - A_HEADS))
    grad_x, d_win, g_wmem, g_wout, stats = _local_step(
        x[0], mem[0], pre_norm, w_in_full, sink, mem_norm, w_mem_kv[0], w_out[0], post_norm, loss_target[0])
    g_win, r_stats = _exchange_grads(d_win, stats)

    big = {}
    for nm, g, w, m, v in (("w_in", g_win, w_in, m_w_in, v_w_in),
                           ("w_mem_kv", g_wmem, w_mem_kv, m_w_mem_kv, v_w_mem_kv),
                           ("w_out", g_wout, w_out, m_w_out, v_w_out)):
        res = _reduce_adamw(g[None], w[0], m[0], v[0], tr=SHARD_ROWS, name="adamw_" + nm)
        big[nm] = [t[None] for t in res]
    small = _reduce_adamw(
        r_stats, _pack_rows(pre_norm, mem_norm, post_norm, sink_a),
        _pack_rows(m_pre_norm, m_mem_norm, m_post_norm, m_sink_a),
        _pack_rows(v_pre_norm, v_mem_norm, v_post_norm, v_sink_a), tr=8, name="adamw_small")

    def unpack(t):
        return {"pre_norm": t[0:1], "mem_norm": t[1:2], "post_norm": t[2:3], "sink_a": t[3:4, 0:A_HEADS]}

    order = ("pre_norm", "w_in", "sink_a", "mem_norm", "w_mem_kv", "w_out", "post_norm")
    outs = [small[0][4, 0], grad_x[None]]
    for j in range(4):
        sm = unpack(small[j])
        outs += [big[n][j] if n in big else sm[n] for n in order]
    return tuple(outs)
```

```python
import jax
import jax.numpy as jnp
from jax import lax
from jax.experimental import pallas as pl
from jax.experimental.pallas import tpu as pltpu

F32 = jnp.float32
BF16 = jnp.bfloat16

D_MODEL = 1024
HEAD_DIM = 64
ROT_DIM = 16
ROPE_THETA = 500000.0
BLOCK = 128
LANES = 128
N_MEM = 256
RMS_EPS = 1e-6
SCALE = HEAD_DIM ** -0.5
A_HEADS, A_GROUP = 6, 3
B_HEADS = 6
C_HEADS = 4
A_W, A_KV_W, B_W, C_W = 384, 128, 384, 256
D_IN = 3072
N_DEV = 8
SHARD_IN = D_IN // N_DEV
SHARD_ROWS = D_MODEL // N_DEV
B_CONFIGS = ((128, 1), (512, 4), (2048, 16))
DILS = (4, 16)
NEG = -1e30
ATTN_BLOCKS_PER_STEP = 4
DELTA_LANE = 64
VMEM_LIMIT = 56 * 1024 * 1024

ADAM_LR, ADAM_B1, ADAM_B2, ADAM_EPS, ADAM_WD, ADAM_STEP = 0.001, 0.9, 0.999, 1e-08, 0.01, 10
MESH_ID = pl.DeviceIdType.MESH


def _params(**kw):
    return pltpu.CompilerParams(vmem_limit_bytes=VMEM_LIMIT, **kw)


def _full(shape):
    n = len(shape)
    return pl.BlockSpec(shape, lambda *_: (0,) * n)


def _row(tm, w):
    return pl.BlockSpec((tm, w), lambda i: (i, 0))


def _mesh_pos():
    return lax.axis_index("x"), lax.axis_index("y"), lax.axis_index("c")


def _dev_index(pos):
    return 4 * pos[0] + 2 * pos[1] + pos[2]


def _xor_peer(pos, s):
    x, y, c = pos
    return (1 - x if s & 4 else x, 1 - y if s & 2 else y, 1 - c if s & 1 else c)


def _perm_view(a, dil):
    return a.reshape(a.shape[0] // (BLOCK * dil), dil, BLOCK, a.shape[1])


def _perm_spec(tm, dil, w):
    per = BLOCK * dil // tm
    return pl.BlockSpec((1, dil, tm // dil, w), lambda i: (i // per, 0, i % per, 0))


def _put(scr, val):
    for c in range(val.shape[1] // LANES):
        scr[c] = val[:, LANES * c:LANES * (c + 1)]


def _get(scr):
    n = scr.shape[0]
    return scr[0] if n == 1 else jnp.concatenate([scr[c] for c in range(n)], axis=1)


def _get_class(scr, r, dil):
    n, rows = scr.shape[0], scr.shape[1]
    parts = [scr.at[c][pl.ds(r, rows // dil, stride=dil), :] for c in range(n)]
    return parts[0] if n == 1 else jnp.concatenate(parts, axis=1)


def _store_permuted(scr, out_ref, dil, dtype):
    for r in range(dil):
        out_ref[0, r] = _get_class(scr, r, dil).astype(dtype)


def _fill_permuted(in_ref, scr, dil):
    n, rows = scr.shape[0], scr.shape[1]
    for r in range(dil):
        val = in_ref[0, r].astype(F32)
        for c in range(n):
            scr.at[c][pl.ds(r, rows // dil, stride=dil), :] = val[:, LANES * c:LANES * (c + 1)]


def _load_permuted(in_ref, scr, dil):
    _fill_permuted(in_ref, scr, dil)
    return _get(scr)


def _rotate128(t, c, up, dn):
    return t * c + pltpu.roll(t, 8, 1) * up + pltpu.roll(t, LANES - 8, 1) * dn


def _rotate(t, c, up, dn):
    outs = [_rotate128(t[:, LANES * j:LANES * (j + 1)], c, up, dn) for j in range(t.shape[1] // LANES)]
    return outs[0] if len(outs) == 1 else jnp.concatenate(outs, axis=1)


def _w_in_scratch():
    return [pltpu.VMEM((D_MODEL, D_IN), BF16), pltpu.SemaphoreType.DMA((N_DEV,))]


def _stage_w_in(w_hbm, w_scr, sems):
    @pl.when(pl.program_id(0) == 0)
    def _():
        copies = [pltpu.make_async_copy(w_hbm.at[k], w_scr.at[:, pl.ds(SHARD_IN * k, SHARD_IN)], sems.at[k])
                  for k in range(N_DEV)]
        for cp in copies:
            cp.start()
        for cp in copies:
            cp.wait()


def _inproj(u, w_in_full, tabs, w_mem, w_out, tm=512):
    seq = u.shape[0]
    n_chunk = D_IN // LANES
    n_steps = seq // tm

    def body(u_ref, w_hbm, c_ref, up_ref, dn_ref, wm_ref, wo_ref, qa_ref, ka_ref, va_ref,
             qb1_ref, kb1_ref, vb1_ref, qb4_ref, kb4_ref, vb4_ref, qb16_ref, kb16_ref, vb16_ref,
             qc_ref, gate_ref, wm_all, wo_all, proj, w_scr, w_sems, wm_b, wo_b, send_sems, recv_sems, local_sems):
        step = pl.program_id(0)
        shards, gathered = (wm_b, wo_b), (wm_all, wo_all)

        def gather_copies(arriving):
            pos = _mesh_pos()
            me = _dev_index(pos)
            local = [] if arriving else [
                pltpu.make_async_copy(shards[a], gathered[a].at[me], local_sems.at[a]) for a in range(2)]
            remote = []
            for s in range(1, N_DEV):
                peer = _xor_peer(pos, s)
                for a in range(2):
                    remote.append(pltpu.make_async_remote_copy(
                        src_ref=shards[a], dst_ref=gathered[a].at[_dev_index(peer) if arriving else me],
                        send_sem=send_sems.at[a, s], recv_sem=recv_sems.at[a, s], device_id=peer,
                        device_id_type=MESH_ID))
            return local, remote

        @pl.when(step == 0)
        def _():
            wm_b[...] = wm_ref[...].astype(BF16)
            wo_b[...] = wo_ref[...].astype(BF16)
            local, sends = gather_copies(arriving=False)
            for cp in local + sends:
                cp.start()

        _stage_w_in(w_hbm, w_scr, w_sems)
        u = u_ref[...]
        for n0 in range(0, D_IN, D_MODEL):
            acc = jnp.dot(u, w_scr[:, n0:n0 + D_MODEL], preferred_element_type=F32)
            for c3 in range(D_MODEL // LANES):
                proj[n0 // LANES + c3] = acc[:, LANES * c3:LANES * (c3 + 1)]
        c, up, dn = c_ref[...], up_ref[...], dn_ref[...]

        def cols(lo, hi, rot=False, scale=None):
            parts = []
            for ch in range(lo // LANES, hi // LANES):
                t = proj[ch]
                if rot:
                    t = _rotate128(t, c, up, dn)
                if scale is not None:
                    t = t * scale
                parts.append(t)
            return parts[0] if len(parts) == 1 else jnp.concatenate(parts, axis=1)

        qa_ref[...] = cols(0, 384, True, SCALE).astype(BF16)
        ka_ref[...] = cols(384, 512, True).astype(BF16)
        va_ref[...] = cols(512, 640).astype(BF16)
        gate_ref[:, 0:384] = cols(640, 1024).astype(BF16)
        gate_ref[:, 384:768] = cols(2176, 2560).astype(BF16)
        gate_ref[:, 768:1024] = cols(2816, 3072).astype(BF16)
        qc_ref[...] = cols(2560, 2816, False, SCALE).astype(BF16)
        for ch in range(1024 // LANES, 1408 // LANES):
            proj[ch] = _rotate128(proj[ch], c, up, dn) * SCALE
        for ch in range(1408 // LANES, 1792 // LANES):
            proj[ch] = _rotate128(proj[ch], c, up, dn)
        for lo, nat, p4, p16 in ((1024, qb1_ref, qb4_ref, qb16_ref), (1408, kb1_ref, kb4_ref, kb16_ref),
                                 (1792, vb1_ref, vb4_ref, vb16_ref)):
            chunks = range(lo // LANES, lo // LANES + B_W // LANES)
            nat[...] = jnp.concatenate([proj[ch] for ch in chunks], axis=1).astype(BF16)
            for dil, ref in ((4, p4), (16, p16)):
                for rr in range(dil):
                    ref[0, rr] = jnp.concatenate(
                        [proj.at[ch][pl.ds(rr, tm // dil, stride=dil), :] for ch in chunks], axis=1).astype(BF16)

        @pl.when(step == n_steps - 1)
        def _():
            for cp in gather_copies(arriving=True)[1]:
                cp.wait_recv()
            local, sends = gather_copies(arriving=False)
            for cp in sends:
                cp.wait_send()
            for cp in local:
                cp.wait()

    nat_w = (A_W, A_KV_W, A_KV_W, B_W, B_W, B_W)
    out_specs = [_row(tm, w) for w in nat_w]
    out_shape = [jax.ShapeDtypeStruct((seq, w), BF16) for w in nat_w]
    for dil in DILS:
        out_specs += [_perm_spec(tm, dil, B_W)] * 3
        out_shape += [jax.ShapeDtypeStruct((seq // (BLOCK * dil), dil, BLOCK, B_W), BF16)] * 3
    hbm = pl.BlockSpec(memory_space=pl.ANY)
    out_specs += [_row(tm, C_W), _row(tm, D_MODEL), hbm, hbm]
    out_shape += [jax.ShapeDtypeStruct((seq, C_W), BF16), jax.ShapeDtypeStruct((seq, D_MODEL), BF16),
                  jax.ShapeDtypeStruct((N_DEV,) + w_mem.shape, BF16), jax.ShapeDtypeStruct((N_DEV,) + w_out.shape, BF16)]
    res = pl.pallas_call(
        body, name="inproj", grid=(n_steps,),
        in_specs=[_row(tm, D_MODEL), hbm, _row(tm, LANES), _row(tm, LANES), _row(tm, LANES),
                  _full(w_mem.shape), _full(w_out.shape)],
        out_specs=out_specs, out_shape=out_shape,
        scratch_shapes=[pltpu.VMEM((n_chunk, tm, LANES), F32)] + _w_in_scratch()
        + [pltpu.VMEM(w_mem.shape, BF16), pltpu.VMEM(w_out.shape, BF16), pltpu.SemaphoreType.DMA((2, N_DEV)),
           pltpu.SemaphoreType.DMA((2, N_DEV)), pltpu.SemaphoreType.DMA((2,))],
        compiler_params=_params(dimension_semantics=("arbitrary",)),
    )(u, w_in_full, *tabs, w_mem, w_out)
    qa, ka, va = res[0:3]
    qkv_b = {1: res[3:6], 4: [t.reshape(seq, B_W) for t in res[6:9]], 16: [t.reshape(seq, B_W) for t in res[9:12]]}
    return qa, ka, va, qkv_b, res[12], res[13], res[14], res[15]


def _memkv_fwd(mem, mem_g, w_mem_full):
    def body(mem_ref, g_ref, w_ref, mn_ref, mk_ref, mv_ref):
        mv_ = mem_ref[...]
        r = lax.rsqrt(jnp.mean(mv_ * mv_, axis=-1, keepdims=True) + RMS_EPS)
        mn = ((mv_ * r) * g_ref[...]).astype(BF16)
        mn_ref[...] = mn
        mkv = jnp.dot(mn, w_ref[...], preferred_element_type=F32)
        mk_ref[...] = mkv[:, 0:C_W].astype(BF16)
        mv_ref[...] = mkv[:, C_W:2 * C_W].astype(BF16)

    return pl.pallas_call(
        body, name="memkv_fwd",
        out_shape=[jax.ShapeDtypeStruct((N_MEM, D_MODEL), BF16),
                   jax.ShapeDtypeStruct((N_MEM, C_W), BF16), jax.ShapeDtypeStruct((N_MEM, C_W), BF16)],
        compiler_params=_params(),
    )(mem, mem_g, w_mem_full)


def _memkv_bwd(mem, mem_g, mn, w_mem_full, dmk, dmv):
    def body(mem_ref, g_ref, mn_ref, w_ref, dmk_ref, dmv_ref, dw_ref, st_ref):
        dmkv = jnp.concatenate([dmk_ref[...], dmv_ref[...]], axis=1).astype(BF16)
        dw_ref[...] = lax.dot_general(mn_ref[...], dmkv, (((0,), (0,)), ((), ())), preferred_element_type=F32)
        dmn = lax.dot_general(dmkv, w_ref[...], (((1,), (1,)), ((), ())), preferred_element_type=F32)
        mv_ = mem_ref[...]
        r = lax.rsqrt(jnp.mean(mv_ * mv_, axis=-1, keepdims=True) + RMS_EPS)
        st_ref[...] = jnp.zeros_like(st_ref)
        st_ref[0:1, :] = jnp.sum(dmn * (mv_ * r), axis=0, keepdims=True)

    return pl.pallas_call(
        body, name="memkv_bwd",
        out_shape=[jax.ShapeDtypeStruct((D_MODEL, 2 * C_W), F32), jax.ShapeDtypeStruct((8, D_MODEL), F32)],
        compiler_params=_params(),
    )(mem, mem_g, mn, w_mem_full, dmk, dmv)


def _band_mask(has_prev, max_dist):
    qi = lax.broadcasted_iota(jnp.int32, (BLOCK, 2 * BLOCK), 0)
    kj = lax.broadcasted_iota(jnp.int32, (BLOCK, 2 * BLOCK), 1)
    dist = qi + BLOCK - kj
    return (dist >= 0) & (dist <= max_dist) & ((kj >= BLOCK) | has_prev)


_NT = (((1,), (1,)), ((), ()))
_TN = (((0,), (0,)), ((), ()))


def _head_only(val, h):
    slab = val[:, LANES * (h // 2):LANES * (h // 2 + 1)]
    lane = lax.broadcasted_iota(jnp.int32, slab.shape, 1)
    keep = (lane < HEAD_DIM) if h % 2 == 0 else (lane >= HEAD_DIM)
    return jnp.where(keep, slab, jnp.zeros((), slab.dtype))


class _KvSlabs:
    def __init__(self, cat, group):
        self.cat, self.group, self.swapped = cat, group, {}

    def is_swapped(self, h):
        return (h // self.group) % 2 != h % 2

    def __call__(self, h):
        j = (h // self.group) // 2
        slab = self.cat[:, LANES * j:LANES * (j + 1)]
        if not self.is_swapped(h):
            return slab
        if j not in self.swapped:
            self.swapped[j] = jnp.concatenate([slab[:, HEAD_DIM:], slab[:, :HEAD_DIM]], axis=1)
        return self.swapped[j]


class _BandSteps:
    def __init__(self, seq, dil, nq):
        self.nq, self.rows, self.consecutive = nq, nq * BLOCK, dil == 1
        nb = seq // dil // BLOCK
        if self.consecutive:
            assert nb % nq == 0
            self.outer, self.inner, self.stride = 1, nb // nq, 1
        else:
            assert dil % nq == 0
            self.outer, self.inner, self.stride = dil // nq, nb, dil // nq

    def own(self, w, clamp=False):
        cur = (lambda i: jnp.minimum(i, self.inner - 1)) if clamp else (lambda i: i)
        return pl.BlockSpec((self.rows, w), lambda r, i: (cur(i) * self.stride + r, 0))

    def prev(self, w, clamp=False):
        cur = (lambda i: jnp.minimum(i, self.inner - 1)) if clamp else (lambda i: i)
        if self.consecutive:
            return pl.BlockSpec((BLOCK, w), lambda r, i: (jnp.maximum(cur(i) * self.nq - 1, 0), 0))
        return pl.BlockSpec((self.rows, w), lambda r, i: (jnp.maximum(cur(i) - 1, 0) * self.stride + r, 0))

    def late(self, w):
        return pl.BlockSpec((self.rows, w), lambda r, i: (jnp.maximum(i - 1, 0) * self.stride + r, 0))

    def rows_of(self, j):
        return slice(BLOCK * j, BLOCK * (j + 1))

    def keys(self, p_ref, c_ref, j):
        if not self.consecutive:
            before = p_ref[self.rows_of(j), :]
        elif j == 0:
            before = p_ref[...]
        else:
            before = c_ref[self.rows_of(j - 1), :]
        return jnp.concatenate([before, c_ref[self.rows_of(j), :]], axis=0)

    def has_prev(self, i, j):
        return True if (self.consecutive and j > 0) else (i > 0)


def _banded_fwd(q, k, v, sink, *, dil, heads, group, max_dist, nq, name):
    seq = q.shape[0]
    kvh = heads // group
    qw, kw = heads * HEAD_DIM, kvh * HEAD_DIM
    steps = _BandSteps(seq, dil, nq)

    def body(*refs):
        if sink is not None:
            sink_ref, refs = refs[0], refs[1:]
        q_ref, kp_ref, kc_ref, vp_ref, vc_ref, o_ref, lse_ref, s_scr, p_scr = refs
        i = pl.program_id(1)
        lane = lax.broadcasted_iota(jnp.int32, (BLOCK, LANES), 1)
        k_of = [_KvSlabs(steps.keys(kp_ref, kc_ref, j), group) for j in range(nq)]
        v_of = [_KvSlabs(steps.keys(vp_ref, vc_ref, j), group) for j in range(nq)]
        for j in range(nq):
            qv = q_ref[steps.rows_of(j), :]
            for h in range(heads):
                s_scr[j * heads + h] = lax.dot_general(_head_only(qv, h), k_of[j](h), _NT, preferred_element_type=F32)
        ls = {}
        for j in range(nq):
            valid = _band_mask(steps.has_prev(i, j), max_dist)
            lse_tile = jnp.zeros((BLOCK, LANES), F32)
            for h in range(heads):
                s = jnp.where(valid, s_scr[j * heads + h], NEG)
                m = jnp.max(s, axis=-1, keepdims=True)
                if sink is not None:
                    sk = sink_ref[h]
                    m = jnp.maximum(m, sk)
                p = jnp.exp(s - m)
                l = jnp.sum(p, axis=-1, keepdims=True)
                if sink is not None:
                    l = l + jnp.exp(sk - m)
                p_scr[j * heads + h] = p.astype(BF16)
                ls[j, h] = l
                lse_tile = jnp.where(lane == h, m + jnp.log(l), lse_tile)
            lse_ref[steps.rows_of(j), :] = lse_tile
        for j in range(nq):
            for pr in range(heads // 2):
                he, ho = 2 * pr, 2 * pr + 1
                even = jnp.dot(p_scr[j * heads + he], v_of[j](he), preferred_element_type=F32) / ls[j, he]
                odd = jnp.dot(p_scr[j * heads + ho], v_of[j](ho), preferred_element_type=F32) / ls[j, ho]
                o_ref[steps.rows_of(j), LANES * pr:LANES * (pr + 1)] = jnp.where(lane < HEAD_DIM, even, odd).astype(BF16)

    in_specs = [steps.own(qw), steps.prev(kw), steps.own(kw), steps.prev(kw), steps.own(kw)]
    args = [q, k, k, v, v]
    if sink is not None:
        in_specs = [pl.BlockSpec(memory_space=pltpu.SMEM)] + in_specs
        args = [sink] + args
    return pl.pallas_call(
        body, name=name, grid=(steps.outer, steps.inner), in_specs=in_specs,
        out_specs=[steps.own(qw), steps.own(LANES)],
        out_shape=[jax.ShapeDtypeStruct((seq, qw), BF16), jax.ShapeDtypeStruct((seq, LANES), F32)],
        scratch_shapes=[pltpu.VMEM((nq * heads, BLOCK, 2 * BLOCK), F32), pltpu.VMEM((nq * heads, BLOCK, 2 * BLOCK), BF16)],
        compiler_params=_params(dimension_semantics=("arbitrary", "arbitrary")),
    )(*args)


def _banded_bwd(q, k, v, d_out, stat, sink, *, dil, heads, group, max_dist, nq, name):
    seq = q.shape[0]
    kvh = heads // group
    qw, kw = heads * HEAD_DIM, kvh * HEAD_DIM
    steps = _BandSteps(seq, dil, nq)
    n_in = 7

    def body(*refs):
        if sink is not None:
            sink_ref, refs = refs[0], refs[1:]
            dsink_ref, refs = refs[n_in], refs[:n_in] + refs[n_in + 1:]
        (q_ref, kp_ref, kc_ref, vp_ref, vc_ref, do_ref, st_ref, dq_ref, dk_ref, dv_ref,
         kcar, vcar, s_scr, dp_scr, p_scr, ds_scr) = refs
        r, i = pl.program_id(0), pl.program_id(1)

        @pl.when(i == 0)
        def _():
            kcar[...] = jnp.zeros_like(kcar)
            vcar[...] = jnp.zeros_like(vcar)

        if sink is not None:
            @pl.when((i == 0) & (r == 0))
            def _():
                dsink_ref[...] = jnp.zeros_like(dsink_ref)

        @pl.when(i < steps.inner)
        def _():
            lane = lax.broadcasted_iota(jnp.int32, (1, LANES), 1)
            lane_q = lax.broadcasted_iota(jnp.int32, (BLOCK, LANES), 1)
            k_of = [_KvSlabs(steps.keys(kp_ref, kc_ref, j), group) for j in range(nq)]
            v_of = [_KvSlabs(steps.keys(vp_ref, vc_ref, j), group) for j in range(nq)]
            qms, doms = {}, {}
            for j in range(nq):
                qv, dov = q_ref[steps.rows_of(j), :], do_ref[steps.rows_of(j), :]
                for h in range(heads):
                    qms[j, h], doms[j, h] = _head_only(qv, h), _head_only(dov, h)
                    s_scr[j * heads + h] = lax.dot_general(qms[j, h], k_of[j](h), _NT, preferred_element_type=F32)
                    dp_scr[j * heads + h] = lax.dot_general(doms[j, h], v_of[j](h), _NT, preferred_element_type=F32)
            dsink_row = jnp.zeros((1, LANES), F32)
            for j in range(nq):
                st = st_ref[steps.rows_of(j), :]
                valid = _band_mask(steps.has_prev(i, j), max_dist)
                for h in range(heads):
                    lse_h = st[:, h:h + 1]
                    delta = st[:, DELTA_LANE + h:DELTA_LANE + h + 1]
                    p = jnp.where(valid, jnp.exp(s_scr[j * heads + h] - lse_h), 0.0)
                    p_scr[j * heads + h] = p.astype(BF16)
                    ds_scr[j * heads + h] = (p * (dp_scr[j * heads + h] - delta)).astype(BF16)
                    if sink is not None:
                        ds_sink = jnp.sum(-jnp.exp(sink_ref[h] - lse_h) * delta, axis=0, keepdims=True)
                        dsink_row = dsink_row + jnp.where(lane == h, ds_sink, 0.0)
            for j in range(nq):
                for pr in range(heads // 2):
                    he, ho = 2 * pr, 2 * pr + 1
                    even = jnp.dot(ds_scr[j * heads + he], k_of[j](he), preferred_element_type=F32)
                    odd = jnp.dot(ds_scr[j * heads + ho], k_of[j](ho), preferred_element_type=F32)
                    dq_ref[steps.rows_of(j), LANES * pr:LANES * (pr + 1)] = (
                        jnp.where(lane_q < HEAD_DIM, even, odd).astype(BF16))
            if steps.consecutive:
                dk_ref[...] = kcar[...].astype(BF16)
                dv_ref[...] = vcar[...].astype(BF16)
            for j in range(nq):
                for slab in range(kw // LANES):
                    acc = {}
                    for h in range(heads):
                        if (h // group) // 2 != slab:
                            continue
                        key = k_of[j].is_swapped(h)
                        dk_h = lax.dot_general(ds_scr[j * heads + h], qms[j, h], _TN, preferred_element_type=F32)
                        dv_h = lax.dot_general(p_scr[j * heads + h], doms[j, h], _TN, preferred_element_type=F32)
                        acc[key] = (dk_h, dv_h) if key not in acc else (acc[key][0] + dk_h, acc[key][1] + dv_h)
                    dk_j, dv_j = acc.get(False, (None, None))
                    if True in acc:
                        unswap = lambda t: jnp.concatenate([t[:, HEAD_DIM:], t[:, :HEAD_DIM]], axis=1)
                        dk_s, dv_s = unswap(acc[True][0]), unswap(acc[True][1])
                        dk_j = dk_s if dk_j is None else dk_j + dk_s
                        dv_j = dv_s if dv_j is None else dv_j + dv_s
                    sl = slice(LANES * slab, LANES * (slab + 1))
                    own_rows = steps.rows_of(j)
                    if not steps.consecutive:
                        dk_ref[own_rows, sl] = (kcar[own_rows, sl] + dk_j[0:BLOCK]).astype(BF16)
                        dv_ref[own_rows, sl] = (vcar[own_rows, sl] + dv_j[0:BLOCK]).astype(BF16)
                    elif j == 0:
                        last = steps.rows_of(nq - 1)
                        dk_ref[last, sl] = (kcar[last, sl] + dk_j[0:BLOCK]).astype(BF16)
                        dv_ref[last, sl] = (vcar[last, sl] + dv_j[0:BLOCK]).astype(BF16)
                    else:
                        before = steps.rows_of(j - 1)
                        kcar[before, sl] += dk_j[0:BLOCK]
                        vcar[before, sl] += dv_j[0:BLOCK]
                    kcar[own_rows, sl] = dk_j[BLOCK:2 * BLOCK]
                    vcar[own_rows, sl] = dv_j[BLOCK:2 * BLOCK]
            if sink is not None:
                dsink_ref[0:1, :] += dsink_row

        @pl.when(i == steps.inner)
        def _():
            dk_ref[...] = kcar[...].astype(BF16)
            dv_ref[...] = vcar[...].astype(BF16)

    own, prev = (lambda w: steps.own(w, clamp=True)), (lambda w: steps.prev(w, clamp=True))
    in_specs = [own(qw), prev(kw), own(kw), prev(kw), own(kw), own(qw), own(LANES)]
    args = [q, k, k, v, v, d_out, stat]
    out_specs = [own(qw), steps.late(kw), steps.late(kw)]
    out_shape = [jax.ShapeDtypeStruct((seq, qw), BF16), jax.ShapeDtypeStruct((seq, kw), BF16),
                 jax.ShapeDtypeStruct((seq, kw), BF16)]
    if sink is not None:
        in_specs = [pl.BlockSpec(memory_space=pltpu.SMEM)] + in_specs
        args = [sink] + args
        out_specs = [_full((8, LANES))] + out_specs
        out_shape = [jax.ShapeDtypeStruct((8, LANES), F32)] + out_shape
    n_hb = nq * heads
    res = pl.pallas_call(
        body, name=name, grid=(steps.outer, steps.inner + 1), in_specs=in_specs, out_specs=out_specs,
        out_shape=out_shape,
        scratch_shapes=[pltpu.VMEM((steps.rows, kw), F32), pltpu.VMEM((steps.rows, kw), F32)]
        + [pltpu.VMEM((n_hb, BLOCK, 2 * BLOCK), F32)] * 2 + [pltpu.VMEM((n_hb, BLOCK, 2 * BLOCK), BF16)] * 2,
        compiler_params=_params(dimension_semantics=("arbitrary", "arbitrary")),
    )(*args)
    if sink is not None:
        return res[1], res[2], res[3], res[0]
    return res


def _cross_fwd(q, mk, mv, tq=512):
    seq = q.shape[0]

    def body(q_ref, mk_ref, mv_ref, o_ref, lse_ref, s_scr, p_scr):
        qv = q_ref[...]
        k_of, v_of = _KvSlabs(mk_ref[...], 1), _KvSlabs(mv_ref[...], 1)
        lane = lax.broadcasted_iota(jnp.int32, (tq, LANES), 1)
        lse_tile = jnp.zeros((tq, LANES), F32)
        for h in range(C_HEADS):
            s_scr[h] = lax.dot_general(_head_only(qv, h), k_of(h), _NT, preferred_element_type=F32)
        ls = []
        for h in range(C_HEADS):
            s = s_scr[h]
            m = jnp.max(s, axis=-1, keepdims=True)
            p = jnp.exp(s - m)
            l = jnp.sum(p, axis=-1, keepdims=True)
            p_scr[h] = p.astype(BF16)
            ls.append(l)
            lse_tile = jnp.where(lane == h, m + jnp.log(l), lse_tile)
        for pr in range(C_HEADS // 2):
            even = jnp.dot(p_scr[2 * pr], v_of(2 * pr), preferred_element_type=F32) / ls[2 * pr]
            odd = jnp.dot(p_scr[2 * pr + 1], v_of(2 * pr + 1), preferred_element_type=F32) / ls[2 * pr + 1]
            o_ref[:, LANES * pr:LANES * (pr + 1)] = jnp.where(lane < HEAD_DIM, even, odd).astype(BF16)
        lse_ref[...] = lse_tile

    return pl.pallas_call(
        body, name="cross_fwd", grid=(seq // tq,),
        in_specs=[_row(tq, C_W), _full((N_MEM, C_W)), _full((N_MEM, C_W))],
        out_specs=[_row(tq, C_W), _row(tq, LANES)],
        out_shape=[jax.ShapeDtypeStruct((seq, C_W), BF16), jax.ShapeDtypeStruct((seq, LANES), F32)],
        scratch_shapes=[pltpu.VMEM((C_HEADS, tq, N_MEM), F32), pltpu.VMEM((C_HEADS, tq, N_MEM), BF16)],
        compiler_params=_params(dimension_semantics=("arbitrary",)),
    )(q, mk, mv)


def _cross_bwd(q, mk, mv, d_out, stat, tq=512):
    seq = q.shape[0]

    def body(q_ref, mk_ref, mv_ref, do_ref, st_ref, dq_ref, dmk_ref, dmv_ref, s_scr, dp_scr, p_scr, ds_scr):
        @pl.when(pl.program_id(0) == 0)
        def _():
            dmk_ref[...] = jnp.zeros_like(dmk_ref)
            dmv_ref[...] = jnp.zeros_like(dmv_ref)

        qv, dov, st = q_ref[...], do_ref[...], st_ref[...]
        k_of, v_of = _KvSlabs(mk_ref[...], 1), _KvSlabs(mv_ref[...], 1)
        qms = [_head_only(qv, h) for h in range(C_HEADS)]
        doms = [_head_only(dov, h) for h in range(C_HEADS)]
        for h in range(C_HEADS):
            s_scr[h] = lax.dot_general(qms[h], k_of(h), _NT, preferred_element_type=F32)
            dp_scr[h] = lax.dot_general(doms[h], v_of(h), _NT, preferred_element_type=F32)
        for h in range(C_HEADS):
            p = jnp.exp(s_scr[h] - st[:, h:h + 1])
            p_scr[h] = p.astype(BF16)
            ds_scr[h] = (p * (dp_scr[h] - st[:, DELTA_LANE + h:DELTA_LANE + h + 1])).astype(BF16)
        lane = lax.broadcasted_iota(jnp.int32, (tq, LANES), 1)
        for pr in range(C_HEADS // 2):
            sl = slice(LANES * pr, LANES * (pr + 1))
            even = jnp.dot(ds_scr[2 * pr], k_of(2 * pr), preferred_element_type=F32)
            odd = jnp.dot(ds_scr[2 * pr + 1], k_of(2 * pr + 1), preferred_element_type=F32)
            dq_ref[:, sl] = jnp.where(lane < HEAD_DIM, even, odd).astype(BF16)
            dmk_ref[:, sl] += (lax.dot_general(ds_scr[2 * pr], qms[2 * pr], _TN, preferred_element_type=F32)
                               + lax.dot_general(ds_scr[2 * pr + 1], qms[2 * pr + 1], _TN, preferred_element_type=F32))
            dmv_ref[:, sl] += (lax.dot_general(p_scr[2 * pr], doms[2 * pr], _TN, preferred_element_type=F32)
                               + lax.dot_general(p_scr[2 * pr + 1], doms[2 * pr + 1], _TN, preferred_element_type=F32))

    return pl.pallas_call(
        body, name="cross_bwd", grid=(seq // tq,),
        in_specs=[_row(tq, C_W), _full((N_MEM, C_W)), _full((N_MEM, C_W)), _row(tq, C_W), _row(tq, LANES)],
        out_specs=[_row(tq, C_W), _full((N_MEM, C_W)), _full((N_MEM, C_W))],
        out_shape=[jax.ShapeDtypeStruct((seq, C_W), BF16), jax.ShapeDtypeStruct((N_MEM, C_W), F32),
                   jax.ShapeDtypeStruct((N_MEM, C_W), F32)],
        scratch_shapes=[pltpu.VMEM((C_HEADS, tq, N_MEM), F32)] * 2 + [pltpu.VMEM((C_HEADS, tq, N_MEM), BF16)] * 2,
        compiler_params=_params(dimension_semantics=("arbitrary",)),
    )(q, mk, mv, d_out, stat)


def _per_head(tile, width):
    rows = tile.shape[0]
    return jnp.concatenate(
        [jnp.broadcast_to(tile[:, h:h + 1], (rows, HEAD_DIM)) for h in range(width // HEAD_DIM)], axis=1)


def _with_delta(lse_tile, prod):
    rows = lse_tile.shape[0]
    lane = lax.broadcasted_iota(jnp.int32, (rows, LANES), 1)
    tile = lse_tile
    for h in range(prod.shape[1] // HEAD_DIM):
        d = jnp.sum(prod[:, HEAD_DIM * h:HEAD_DIM * (h + 1)], axis=-1, keepdims=True)
        tile = jnp.where(lane == DELTA_LANE + h, d, tile)
    return tile


def _mid(oa, lse_a, ob, lse_b, oc, lse_c, gate, x, target, w_out_full, post_g, tm=256):
    seq = x.shape[0]
    n_b = B_W // LANES

    def body(oa_ref, la_ref, b1_ref, l1_ref, b4_ref, l4_ref, b16_ref, l16_ref, oc_ref, lc_ref,
             gate_ref, x_ref, t_ref, w_ref, pg_ref,
             dh_ref, dg_ref, doa_ref, sa_ref, dob1_ref, sb1_ref, dob4_ref, sb4_ref, dob16_ref, sb16_ref,
             doc_ref, sc_ref, dw_ref, st_ref, scr_b4, scr_b16, scr_l4, scr_l16, scr_do, scr_sb):
        @pl.when(pl.program_id(0) == 0)
        def _():
            dw_ref[...] = jnp.zeros_like(dw_ref)
            st_ref[...] = jnp.zeros_like(st_ref)

        b1, l1 = b1_ref[...].astype(F32), l1_ref[...]
        b4, l4 = _load_permuted(b4_ref, scr_b4, 4), _load_permuted(l4_ref, scr_l4, 4)
        b16, l16 = _load_permuted(b16_ref, scr_b16, 16), _load_permuted(l16_ref, scr_l16, 16)
        lm = jnp.maximum(jnp.maximum(l1, l4), l16)
        e1, e4, e16 = jnp.exp(l1 - lm), jnp.exp(l4 - lm), jnp.exp(l16 - lm)
        den = e1 + e4 + e16
        lse_b_tile = lm + jnp.log(den)
        ob_v = _per_head(e1 / den, B_W) * b1 + _per_head(e4 / den, B_W) * b4 + _per_head(e16 / den, B_W) * b16
        o_all = jnp.concatenate([oa_ref[...].astype(F32), ob_v, oc_ref[...].astype(F32)], axis=1)
        g = gate_ref[...].astype(F32)
        sig = 1.0 / (1.0 + jnp.exp(-g))
        silu = g * sig
        y = (o_all * silu).astype(BF16)
        w = w_ref[...]
        z = jnp.dot(y, w, preferred_element_type=F32)
        rz = lax.rsqrt(jnp.mean(z * z, axis=-1, keepdims=True) + RMS_EPS)
        hn = z * rz
        pg = pg_ref[...]
        err = (x_ref[...] + hn * pg) - t_ref[...]
        loss = 0.5 * jnp.sum(jnp.mean(err * err, axis=-1, keepdims=True), axis=0, keepdims=True)
        dh = err * (1.0 / D_MODEL)
        dh_ref[...] = dh.astype(BF16)
        st_ref[0:1, :] += jnp.sum(dh * hn, axis=0, keepdims=True)
        st_ref[1:2, :] += jnp.broadcast_to(loss, (1, D_MODEL))
        dhn = dh * pg
        dz = (rz * (dhn - hn * jnp.mean(dhn * hn, axis=-1, keepdims=True))).astype(BF16)
        dy = lax.dot_general(dz, w, _NT, preferred_element_type=F32)
        dw_ref[...] += lax.dot_general(y, dz, _TN, preferred_element_type=F32)
        dg_ref[...] = (dy * o_all * (sig * (1.0 + g * (1.0 - sig)))).astype(BF16)
        d_o = (dy * silu).astype(BF16)
        prod = d_o.astype(F32) * o_all
        doa_ref[...] = d_o[:, 0:A_W]
        sa_ref[...] = _with_delta(la_ref[...], prod[:, 0:A_W])
        doc_ref[...] = d_o[:, A_W + B_W:D_MODEL]
        sc_ref[...] = _with_delta(lc_ref[...], prod[:, A_W + B_W:D_MODEL])
        d_ob = d_o[:, A_W:A_W + B_W]
        stat_b = _with_delta(lse_b_tile, prod[:, A_W:A_W + B_W])
        dob1_ref[...] = d_ob
        sb1_ref[...] = stat_b
        _put(scr_do, d_ob.astype(F32))
        _put(scr_sb, stat_b)
        _store_permuted(scr_do, dob4_ref, 4, BF16)
        _store_permuted(scr_sb, sb4_ref, 4, F32)
        _store_permuted(scr_do, dob16_ref, 16, BF16)
        _store_permuted(scr_sb, sb16_ref, 16, F32)

    p4 = lambda w: _perm_spec(tm, 4, w)
    p16 = lambda w: _perm_spec(tm, 16, w)
    in_specs = [_row(tm, A_W), _row(tm, LANES), _row(tm, B_W), _row(tm, LANES), p4(B_W), p4(LANES), p16(B_W), p16(LANES),
                _row(tm, C_W), _row(tm, LANES), _row(tm, D_MODEL), _row(tm, D_MODEL), _row(tm, D_MODEL),
                _full((D_MODEL, D_MODEL)), _full((1, D_MODEL))]
    sds = jax.ShapeDtypeStruct
    v4 = lambda w, dt: sds((seq // (BLOCK * 4), 4, BLOCK, w), dt)
    v16 = lambda w, dt: sds((seq // (BLOCK * 16), 16, BLOCK, w), dt)
    out_specs = [_row(tm, D_MODEL), _row(tm, D_MODEL), _row(tm, A_W), _row(tm, LANES), _row(tm, B_W), _row(tm, LANES),
                 p4(B_W), p4(LANES), p16(B_W), p16(LANES), _row(tm, C_W), _row(tm, LANES),
                 _full((D_MODEL, D_MODEL)), _full((8, D_MODEL))]
    out_shape = [sds((seq, D_MODEL), BF16), sds((seq, D_MODEL), BF16), sds((seq, A_W), BF16), sds((seq, LANES), F32),
                 sds((seq, B_W), BF16), sds((seq, LANES), F32), v4(B_W, BF16), v4(LANES, F32), v16(B_W, BF16),
                 v16(LANES, F32), sds((seq, C_W), BF16), sds((seq, LANES), F32),
                 sds((D_MODEL, D_MODEL), F32), sds((8, D_MODEL), F32)]
    res = pl.pallas_call(
        body, name="mid", grid=(seq // tm,), in_specs=in_specs, out_specs=out_specs, out_shape=out_shape,
        scratch_shapes=[pltpu.VMEM((n_b, tm, LANES), F32), pltpu.VMEM((n_b, tm, LANES), F32),
                        pltpu.VMEM((1, tm, LANES), F32), pltpu.VMEM((1, tm, LANES), F32),
                        pltpu.VMEM((n_b, tm, LANES), F32), pltpu.VMEM((1, tm, LANES), F32)],
        compiler_params=_params(dimension_semantics=("arbitrary",)),
    )(oa, lse_a, ob[1], lse_b[1], _perm_view(ob[4], 4), _perm_view(lse_b[4], 4), _perm_view(ob[16], 16),
      _perm_view(lse_b[16], 16), oc, lse_c, gate, x, target, w_out_full, post_g)
    dh, d_gate, do_a, st_a, do_b1, st_b1, do_b4, st_b4, do_b16, st_b16, do_c, st_c, d_wout, stats = res
    flat = lambda t: t.reshape(seq, t.shape[-1])
    d_b = {1: (do_b1, st_b1), 4: (flat(do_b4), flat(st_b4)), 16: (flat(do_b16), flat(st_b16))}
    return dh, d_gate, (do_a, st_a), d_b, (do_c, st_c), d_wout, stats


def _inproj_bwd_x(x, dh, pre_g, w_in_full, tabs, dqa, dka, dva, dqkv_b, dqc, dgate, tm=256):
    seq = x.shape[0]
    n_b = B_W // LANES

    def body(x_ref, dh_ref, g_ref, w_hbm, c_ref, up_ref, dn_ref, dqa_ref, dka_ref, dva_ref,
             dq1, dk1, dv1, dq4, dk4, dv4, dq16, dk16, dv16, dqc_ref, dg_ref,
             gx_ref, dp_ref, st_ref, scr4, scr16, w_scr, w_sems):
        _stage_w_in(w_hbm, w_scr, w_sems)

        @pl.when(pl.program_id(0) == 0)
        def _():
            st_ref[...] = jnp.zeros_like(st_ref)

        c, up, dn = c_ref[...], -up_ref[...], -dn_ref[...]
        unrot = lambda t: _rotate(t, c, up, dn)
        total = lambda r1, r4, r16: (r1[...].astype(F32) + _load_permuted(r4, scr4, 4)
                                     + _load_permuted(r16, scr16, 16))
        dp_ref[:, 0:384] = (unrot(dqa_ref[...].astype(F32)) * SCALE).astype(BF16)
        dp_ref[:, 384:512] = unrot(dka_ref[...].astype(F32)).astype(BF16)
        dp_ref[:, 512:640] = dva_ref[...]
        dp_ref[:, 640:1024] = dg_ref[:, 0:384]
        dp_ref[:, 1024:1408] = (unrot(total(dq1, dq4, dq16)) * SCALE).astype(BF16)
        dp_ref[:, 1408:1792] = unrot(total(dk1, dk4, dk16)).astype(BF16)
        dp_ref[:, 1792:2176] = total(dv1, dv4, dv16).astype(BF16)
        dp_ref[:, 2176:2560] = dg_ref[:, 384:768]
        dp_ref[:, 2560:2816] = (dqc_ref[...].astype(F32) * SCALE).astype(BF16)
        dp_ref[:, 2816:3072] = dg_ref[:, 768:1024]
        du = lax.dot_general(dp_ref[...], w_scr[...], _NT, preferred_element_type=F32)
        xv = x_ref[...]
        r = lax.rsqrt(jnp.mean(xv * xv, axis=-1, keepdims=True) + RMS_EPS)
        xh = xv * r
        st_ref[0:1, :] += jnp.sum(du * xh, axis=0, keepdims=True)
        dxh = du * g_ref[...]
        gx_ref[...] = dh_ref[...].astype(F32) + r * (dxh - xh * jnp.mean(dxh * xh, axis=-1, keepdims=True))

    in_specs = ([_row(tm, D_MODEL), _row(tm, D_MODEL), _full((1, D_MODEL)), pl.BlockSpec(memory_space=pl.ANY),
                 _row(tm, LANES), _row(tm, LANES), _row(tm, LANES), _row(tm, A_W), _row(tm, A_KV_W), _row(tm, A_KV_W)]
                + [_row(tm, B_W)] * 3 + [_perm_spec(tm, 4, B_W)] * 3 + [_perm_spec(tm, 16, B_W)] * 3
                + [_row(tm, C_W), _row(tm, D_MODEL)])
    return pl.pallas_call(
        body, name="inproj_bwd_x", grid=(seq // tm,), in_specs=in_specs,
        out_specs=[_row(tm, D_MODEL), _row(tm, D_IN), _full((8, D_MODEL))],
        out_shape=[jax.ShapeDtypeStruct((seq, D_MODEL), F32), jax.ShapeDtypeStruct((seq, D_IN), BF16),
                   jax.ShapeDtypeStruct((8, D_MODEL), F32)],
        scratch_shapes=[pltpu.VMEM((n_b, tm, LANES), F32), pltpu.VMEM((n_b, tm, LANES), F32)] + _w_in_scratch(),
        compiler_params=_params(dimension_semantics=("arbitrary",)),
    )(x, dh, pre_g, w_in_full, *tabs, dqa, dka, dva, *dqkv_b[1], *[_perm_view(t, 4) for t in dqkv_b[4]],
      *[_perm_view(t, 16) for t in dqkv_b[16]], dqc, dgate)


class _ReduceScatter:
    def __init__(self, ins, outs, scratch):
        self.n = n = len(ins)
        self.ins, self.outs = ins, outs
        self.mine, self.got, self.snd, self.rcv = (scratch[n * t:n * (t + 1)] for t in range(4))
        self.load_sems, self.d2d_send, self.d2d_recv, self.ici_send, self.ici_recv = scratch[4 * n:]
        self.pos = _mesh_pos()
        self.pairs = [(a, kk) for kk in (3, 1, 2) for a in range(n)]

    @staticmethod
    def scratch_shapes(shapes):
        return ([pltpu.VMEM((4,) + s, F32) for s in shapes] + [pltpu.VMEM((4,) + s, F32) for s in shapes]
                + [pltpu.VMEM((3,) + s, BF16) for s in shapes] + [pltpu.VMEM((3,) + s, BF16) for s in shapes]
                + [pltpu.SemaphoreType.DMA((len(shapes), 4))] * 5)

    def _chip(self, kk):
        x, y, _ = self.pos
        return (1 - x if kk & 2 else x, 1 - y if kk & 1 else y)

    def _load(self, a, kk):
        block = _dev_index((*self._chip(kk), self.pos[2]))
        return pltpu.make_async_copy(self.ins[a].at[block], self.mine[a].at[kk], self.load_sems.at[a, kk])

    def _swap(self, a, kk):
        x, y, c = self.pos
        return pltpu.make_async_remote_copy(
            src_ref=self.ins[a].at[_dev_index((*self._chip(kk), 1 - c))], dst_ref=self.got[a].at[kk],
            send_sem=self.d2d_send.at[a, kk], recv_sem=self.d2d_recv.at[a, kk],
            device_id=(x, y, 1 - c), device_id_type=MESH_ID)

    def _hop(self, a, kk):
        return pltpu.make_async_remote_copy(
            src_ref=self.snd[a].at[kk - 1], dst_ref=self.rcv[a].at[kk - 1], send_sem=self.ici_send.at[a, kk],
            recv_sem=self.ici_recv.at[a, kk], device_id=(*self._chip(kk), self.pos[2]), device_id_type=MESH_ID)

    def start(self):
        for kk in range(4):
            for a in range(self.n):
                self._load(a, kk).start()
                self._swap(a, kk).start()

    def send_chip_sums(self):
        for a, kk in self.pairs:
            self._load(a, kk).wait()
            self._swap(a, kk).wait_recv()
            self.snd[a][kk - 1] = (self.mine[a][kk] + self.got[a][kk]).astype(BF16)
            self._hop(a, kk).start()

    def finish(self):
        for a in range(self.n):
            self._load(a, 0).wait()
            self._swap(a, 0).wait_recv()
            acc = self.mine[a][0] + self.got[a][0]
            for kk in (1, 2, 3):
                self._hop(a, kk).wait_recv()
                acc = acc + self.rcv[a][kk - 1].astype(F32)
            self.outs[a][...] = acc
        for kk in range(4):
            for a in range(self.n):
                self._swap(a, kk).wait_send()
        for a, kk in self.pairs:
            self._hop(a, kk).wait_send()


def _inproj_bwd_w(u, dproj, d_wmem, d_wout, tm=1024):
    seq = u.shape[0]

    n_steps = seq // tm
    shapes = [d_wmem.shape[1:], d_wout.shape[1:]]

    def body(u_ref, dp_ref, wm, wo, dw_ref, g_wm, g_wo, *scratch):
        step = pl.program_id(0)
        exchange = _ReduceScatter((wm, wo), (g_wm, g_wo), scratch)

        @pl.when(step == 0)
        def _():
            dw_ref[...] = jnp.zeros_like(dw_ref)
            exchange.start()

        @pl.when(step == min(1, n_steps - 1))
        def _():
            exchange.send_chip_sums()

        res = lax.dot_general(u_ref[...], dp_ref[...], _TN, preferred_element_type=F32)
        for k in range(N_DEV):
            dw_ref[k] += res[:, SHARD_IN * k:SHARD_IN * (k + 1)]

        @pl.when(step == n_steps - 1)
        def _():
            exchange.finish()

    hbm = pl.BlockSpec(memory_space=pl.ANY)
    return pl.pallas_call(
        body, name="inproj_bwd_w", grid=(n_steps,),
        in_specs=[_row(tm, D_MODEL), _row(tm, D_IN), hbm, hbm],
        out_specs=[_full((N_DEV, D_MODEL, SHARD_IN))] + [_full(s) for s in shapes],
        out_shape=[jax.ShapeDtypeStruct((N_DEV, D_MODEL, SHARD_IN), F32)] + [jax.ShapeDtypeStruct(s, F32) for s in shapes],
        scratch_shapes=_ReduceScatter.scratch_shapes(shapes),
        compiler_params=_params(dimension_semantics=("arbitrary",)),
    )(u, dproj, d_wmem, d_wout)


def _local_step(x, mem, pre_g, w_in, sink, mem_g, w_mem, w_out, post_g, target):
    u, *tabs, w_in_full = _prep(x, pre_g, w_in)
    qa, ka, va, qkv_b, qc, gate, w_mem_all, w_out_all = _inproj(u, w_in_full, tabs, w_mem, w_out)
    w_mem_full = w_mem_all.reshape(D_MODEL, 2 * C_W)
    w_out_full = w_out_all.reshape(D_MODEL, D_MODEL)
    mn, mk, mv = _memkv_fwd(mem, mem_g, w_mem_full)

    a_cfg = dict(dil=1, heads=A_HEADS, group=A_GROUP, max_dist=BLOCK - 1, nq=ATTN_BLOCKS_PER_STEP)
    b_cfgs = {dil: dict(dil=dil, heads=B_HEADS, group=1, max_dist=win // dil, nq=ATTN_BLOCKS_PER_STEP)
              for win, dil in B_CONFIGS}
    oa, lse_a = _banded_fwd(qa, ka, va, sink, name="attn_a_fwd", **a_cfg)
    ob, lse_b = {}, {}
    for dil, cfg in b_cfgs.items():
        ob[dil], lse_b[dil] = _banded_fwd(*qkv_b[dil], None, name=f"attn_b{dil}_fwd", **cfg)
    oc, lse_c = _cross_fwd(qc, mk, mv)

    dh, d_gate, d_a, d_b, d_c, d_wout, st_mid = _mid(oa, lse_a, ob, lse_b, oc, lse_c, gate, x, target, w_out_full, post_g)

    dqa, dka, dva, dsink = _banded_bwd(qa, ka, va, *d_a, sink, name="attn_a_bwd", **a_cfg)
    dqkv_b = {dil: _banded_bwd(*qkv_b[dil], *d_b[dil], None, name=f"attn_b{dil}_bwd", **cfg)
              for dil, cfg in b_cfgs.items()}
    dqc, dmk, dmv = _cross_bwd(qc, mk, mv, *d_c)
    d_wmem, st_mem = _memkv_bwd(mem, mem_g, mn, w_mem_full, dmk, dmv)

    grad_x, dproj, st_pre = _inproj_bwd_x(x, dh, pre_g, w_in_full, tabs, dqa, dka, dva, dqkv_b, dqc, d_gate)
    d_win, g_wmem, g_wout = _inproj_bwd_w(u, dproj, d_wmem.reshape(N_DEV, SHARD_ROWS, 2 * C_W),
                                          d_wout.reshape(N_DEV, SHARD_ROWS, D_MODEL))

    dsink_row = jnp.pad(dsink[0:1, :], ((0, 0), (0, D_MODEL - LANES)))
    stats = jnp.concatenate([st_pre[0:1], st_mem[0:1], st_mid[0:1], dsink_row, st_mid[1:2],
                             jnp.zeros((3, D_MODEL), F32)], axis=0)
    return grad_x, d_win, g_wmem, g_wout, stats


def _prep(x, pre_g, w_in, tm=1024, parts=2):
    seq = x.shape[0]
    n_steps = seq // tm
    rows = D_MODEL // parts
    pass_on_at = [min(1 + 2 * a, n_steps - 1) for a in range(parts)]
    j = jnp.arange(LANES) % HEAD_DIM
    freq = (ROPE_THETA ** (-(2 * (j % (ROT_DIM // 2))).astype(F32) / ROT_DIM))[None, :]

    def body(x_ref, g_ref, f_ref, win_ref, u_ref, c_ref, up_ref, dn_ref, win_out, win_b,
             send_sems, recv_sems, local_sems):
        step = pl.program_id(0)
        px, py, pc = _mesh_pos()
        me, sibling = (px, py, pc), (px, py, 1 - pc)
        chips = [(1 - px, py), (px, 1 - py), (1 - px, 1 - py)]

        def src(a):
            return win_b.at[pl.ds(rows * a, rows)]

        def slot(a, p):
            return win_out.at[_dev_index(p), pl.ds(rows * a, rows)]

        def copy(a, k, block, to, own=False):
            return pltpu.make_async_remote_copy(
                src_ref=src(a) if own else slot(a, block), dst_ref=slot(a, block),
                send_sem=send_sems.at[a, k], recv_sem=recv_sems.at[a, k], device_id=to, device_id_type=MESH_ID)

        def first_sends(a):
            return [copy(a, 1 + k, me, (*chip, pc), own=True) for k, chip in enumerate(chips)] + [
                copy(a, 0, me, sibling, own=True)]

        def local(a):
            return pltpu.make_async_copy(src(a), slot(a, me), local_sems.at[a])

        @pl.when(step == 0)
        def _():
            win_b[...] = win_ref[...].astype(BF16)
            for a in range(parts):
                local(a).start()
                for cp in first_sends(a):
                    cp.start()

        for a in range(parts):
            @pl.when(step == pass_on_at[a])
            def _(a=a):
                for k, chip in enumerate(chips):
                    copy(a, 1 + k, (*chip, pc), me).wait_recv()
                    copy(a, 4 + k, (*chip, pc), sibling).start()

        xv = x_ref[...]
        r = lax.rsqrt(jnp.mean(xv * xv, axis=-1, keepdims=True) + RMS_EPS)
        u_ref[...] = ((xv * r) * g_ref[...]).astype(BF16)
        pos = (lax.broadcasted_iota(jnp.int32, (tm, LANES), 0) + step * tm).astype(F32)
        head_lane = lax.broadcasted_iota(jnp.int32, (tm, LANES), 1) % HEAD_DIM
        ang = pos * f_ref[...]
        cos, sin = jnp.cos(ang), jnp.sin(ang)
        half = ROT_DIM // 2
        c_ref[...] = jnp.where(head_lane < ROT_DIM, cos, 1.0)
        up_ref[...] = jnp.where((head_lane >= half) & (head_lane < ROT_DIM), sin, 0.0)
        dn_ref[...] = jnp.where(head_lane < half, -sin, 0.0)

        @pl.when(step == n_steps - 1)
        def _():
            for a in range(parts):
                copy(a, 0, sibling, me).wait_recv()
                for k, chip in enumerate(chips):
                    copy(a, 4 + k, (*chip, 1 - pc), me).wait_recv()
            for a in range(parts):
                for cp in first_sends(a):
                    cp.wait_send()
                for k, chip in enumerate(chips):
                    copy(a, 4 + k, (*chip, pc), sibling).wait_send()
                local(a).wait()

    return pl.pallas_call(
        body, name="prep", grid=(n_steps,),
        in_specs=[_row(tm, D_MODEL), _full((1, D_MODEL)), _full((1, LANES)), _full(w_in.shape)],
        out_specs=[_row(tm, D_MODEL), _row(tm, LANES), _row(tm, LANES), _row(tm, LANES),
                   pl.BlockSpec(memory_space=pl.ANY)],
        out_shape=[jax.ShapeDtypeStruct((seq, D_MODEL), BF16)] + [jax.ShapeDtypeStruct((seq, LANES), F32)] * 3
        + [jax.ShapeDtypeStruct((N_DEV,) + w_in.shape, BF16)],
        scratch_shapes=[pltpu.VMEM(w_in.shape, BF16), pltpu.SemaphoreType.DMA((parts, 7)),
                        pltpu.SemaphoreType.DMA((parts, 7)), pltpu.SemaphoreType.DMA((parts,))],
        compiler_params=_params(dimension_semantics=("arbitrary",)),
    )(x, pre_g, freq, w_in)


def _exchange_grads(d_win, stats):
    def body(win, st, g_win, r_st, send_sems, recv_sems, local_sem, *scratch):
        exchange = _ReduceScatter((win,), (g_win,), scratch)
        exchange.start()
        pos = _mesh_pos()
        me = _dev_index(pos)
        own = pltpu.make_async_copy(st, r_st.at[me], local_sem)
        own.start()
        copies = []
        for s in range(1, N_DEV):
            peer = _xor_peer(pos, s)
            mk = lambda slot: pltpu.make_async_remote_copy(
                src_ref=st, dst_ref=r_st.at[slot], send_sem=send_sems.at[s], recv_sem=recv_sems.at[s],
                device_id=peer, device_id_type=MESH_ID)
            send, arrival = mk(me), mk(_dev_index(peer))
            send.start()
            copies.append((send, arrival))
        exchange.send_chip_sums()
        exchange.finish()
        for send, arrival in copies:
            arrival.wait_recv()
            send.wait_send()
        own.wait()

    hbm = pl.BlockSpec(memory_space=pl.ANY)
    shard = d_win.shape[1:]
    return pl.pallas_call(
        body, name="exchange_grads", in_specs=[hbm, hbm],
        out_specs=[pl.BlockSpec(memory_space=pltpu.VMEM), hbm],
        out_shape=[jax.ShapeDtypeStruct(shard, F32), jax.ShapeDtypeStruct((N_DEV,) + stats.shape, F32)],
        scratch_shapes=[pltpu.SemaphoreType.DMA((N_DEV,)), pltpu.SemaphoreType.DMA((N_DEV,)), pltpu.SemaphoreType.DMA(())]
        + _ReduceScatter.scratch_shapes([shard]),
        compiler_params=_params(),
    )(d_win, stats)


def _reduce_adamw(recv, w, m, v, *, tr, name):
    n_part, rows, cols = recv.shape

    def body(r_ref, w_ref, m_ref, v_ref, g_ref, d_ref, nm_ref, nv_ref):
        g = r_ref[0]
        for s in range(1, n_part):
            g = g + r_ref[s]
        g_ref[...] = g
        m2 = ADAM_B1 * m_ref[...] + (1.0 - ADAM_B1) * g
        v2 = ADAM_B2 * v_ref[...] + (1.0 - ADAM_B2) * (g * g)
        nm_ref[...] = m2
        nv_ref[...] = v2
        m_hat = m2 / (1.0 - ADAM_B1 ** ADAM_STEP)
        v_hat = v2 / (1.0 - ADAM_B2 ** ADAM_STEP)
        d_ref[...] = -ADAM_LR * (m_hat / (jnp.sqrt(v_hat) + ADAM_EPS) + ADAM_WD * w_ref[...])

    blk = pl.BlockSpec((tr, cols), lambda i: (i, 0))
    return pl.pallas_call(
        body, name=name, grid=(rows // tr,),
        in_specs=[pl.BlockSpec((n_part, tr, cols), lambda i: (0, i, 0)), blk, blk, blk],
        out_specs=[blk] * 4, out_shape=[jax.ShapeDtypeStruct((rows, cols), F32)] * 4,
        compiler_params=_params(dimension_semantics=("arbitrary",)),
    )(recv, w, m, v)


def _pack_rows(pre, memn, post, sink):
    sink_row = jnp.pad(sink, ((0, 0), (0, D_MODEL - A_HEADS)))
    return jnp.concatenate([pre, memn, post, sink_row, jnp.zeros((4, D_MODEL), F32)], axis=0)


def kernel(x, mem, pre_norm, w_in, sink_a, mem_norm, w_mem_kv, w_out, post_norm, loss_target, m_pre_norm, m_w_in, m_sink_a, m_mem_norm, m_w_mem_kv, m_w_out, m_post_norm, v_pre_norm, v_w_in, v_sink_a, v_mem_norm, v_w_mem_kv, v_w_out, v_post_norm):
    sink = jnp.pad(sink_a[0], (0, 8 - A_HEADS))
    grad_x, d_win, g_wmem, g_wout, stats = _local_step(
        x[0], mem[0], pre_norm, w_in[0], sink, mem_norm, w_mem_kv[0], w_out[0], post_norm, loss_target[0])
    g_win, r_stats = _exchange_grads(d_win, stats)

    big = {}
    for nm, g, w, m, v in (("w_in", g_win, w_in, m_w_in, v_w_in),
                           ("w_mem_kv", g_wmem, w_mem_kv, m_w_mem_kv, v_w_mem_kv),
                           ("w_out", g_wout, w_out, m_w_out, v_w_out)):
        res = _reduce_adamw(g[None], w[0], m[0], v[0], tr=SHARD_ROWS, name="adamw_" + nm)
        big[nm] = [t[None] for t in res]
    small = _reduce_adamw(
        r_stats, _pack_rows(pre_norm, mem_norm, post_norm, sink_a),
        _pack_rows(m_pre_norm, m_mem_norm, m_post_norm, m_sink_a),
        _pack_rows(v_pre_norm, v_mem_norm, v_post_norm, v_sink_a), tr=8, name="adamw_small")

    def unpack(t):
        return {"pre_norm": t[0:1], "mem_norm": t[1:2], "post_norm": t[2:3], "sink_a": t[3:4, 0:A_HEADS]}

    order = ("pre_norm", "w_in", "sink_a", "mem_norm", "w_mem_kv", "w_out", "post_norm")
    outs = [small[0][4, 0], grad_x[None]]
    for j in range(4):
        sm = unpack(small[j])
        outs += [big[n][j] if n in big else sm[n] for n in order]
    return tuple(outs)
```

```python
import jax
import jax.numpy as jnp
from jax import lax
from jax.experimental import pallas as pl
from jax.experimental.pallas import tpu as pltpu

F32 = jnp.float32
BF16 = jnp.bfloat16

D_MODEL = 1024
HEAD_DIM = 64
ROT_DIM = 16
ROPE_THETA = 500000.0
BLOCK = 128
LANES = 128
N_MEM = 256
RMS_EPS = 1e-6
SCALE = HEAD_DIM ** -0.5
A_HEADS, A_GROUP = 6, 3
B_HEADS = 6
C_HEADS = 4
A_W, A_KV_W, B_W, C_W = 384, 128, 384, 256
D_IN = 3072
N_DEV = 8
SHARD_IN = D_IN // N_DEV
SHARD_ROWS = D_MODEL // N_DEV
B_CONFIGS = ((128, 1), (512, 4), (2048, 16))
DILS = (4, 16)
NEG = -1e30
ATTN_BLOCKS_PER_STEP = 4
DELTA_LANE = 64
VMEM_LIMIT = 56 * 1024 * 1024

ADAM_LR, ADAM_B1, ADAM_B2, ADAM_EPS, ADAM_WD, ADAM_STEP = 0.001, 0.9, 0.999, 1e-08, 0.01, 10
MESH_ID = pl.DeviceIdType.MESH


def _params(**kw):
    return pltpu.CompilerParams(vmem_limit_bytes=VMEM_LIMIT, **kw)


def _full(shape):
    n = len(shape)
    return pl.BlockSpec(shape, lambda *_: (0,) * n)


def _row(tm, w):
    return pl.BlockSpec((tm, w), lambda i: (i, 0))


def _mesh_pos():
    return lax.axis_index("x"), lax.axis_index("y"), lax.axis_index("c")


def _dev_index(pos):
    return 4 * pos[0] + 2 * pos[1] + pos[2]


def _xor_peer(pos, s):
    x, y, c = pos
    return (1 - x if s & 4 else x, 1 - y if s & 2 else y, 1 - c if s & 1 else c)


def _perm_view(a, dil):
    return a.reshape(a.shape[0] // (BLOCK * dil), dil, BLOCK, a.shape[1])


def _perm_spec(tm, dil, w):
    per = BLOCK * dil // tm
    return pl.BlockSpec((1, dil, tm // dil, w), lambda i: (i // per, 0, i % per, 0))


def _put(scr, val):
    for c in range(val.shape[1] // LANES):
        scr[c] = val[:, LANES * c:LANES * (c + 1)]


def _get(scr):
    n = scr.shape[0]
    return scr[0] if n == 1 else jnp.concatenate([scr[c] for c in range(n)], axis=1)


def _get_class(scr, r, dil):
    n, rows = scr.shape[0], scr.shape[1]
    parts = [scr.at[c][pl.ds(r, rows // dil, stride=dil), :] for c in range(n)]
    return parts[0] if n == 1 else jnp.concatenate(parts, axis=1)


def _store_permuted(scr, out_ref, dil, dtype):
    for r in range(dil):
        out_ref[0, r] = _get_class(scr, r, dil).astype(dtype)


def _fill_permuted(in_ref, scr, dil):
    n, rows = scr.shape[0], scr.shape[1]
    for r in range(dil):
        val = in_ref[0, r].astype(F32)
        for c in range(n):
            scr.at[c][pl.ds(r, rows // dil, stride=dil), :] = val[:, LANES * c:LANES * (c + 1)]


def _load_permuted(in_ref, scr, dil):
    _fill_permuted(in_ref, scr, dil)
    return _get(scr)


def _rotate128(t, c, up, dn):
    return t * c + pltpu.roll(t, 8, 1) * up + pltpu.roll(t, LANES - 8, 1) * dn


def _rotate(t, c, up, dn):
    outs = [_rotate128(t[:, LANES * j:LANES * (j + 1)], c, up, dn) for j in range(t.shape[1] // LANES)]
    return outs[0] if len(outs) == 1 else jnp.concatenate(outs, axis=1)


def _w_in_scratch():
    return [pltpu.VMEM((D_MODEL, D_IN), BF16), pltpu.SemaphoreType.DMA((N_DEV,))]


def _stage_w_in(w_hbm, w_scr, sems):
    @pl.when(pl.program_id(0) == 0)
    def _():
        copies = [pltpu.make_async_copy(w_hbm.at[k], w_scr.at[:, pl.ds(SHARD_IN * k, SHARD_IN)], sems.at[k])
                  for k in range(N_DEV)]
        for cp in copies:
            cp.start()
        for cp in copies:
            cp.wait()


def _inproj(u, w_in_full, tabs, w_mem, w_out, tm=512):
    seq = u.shape[0]
    n_chunk = D_IN // LANES
    n_steps = seq // tm

    def body(u_ref, w_hbm, c_ref, up_ref, dn_ref, wm_ref, wo_ref, qa_ref, ka_ref, va_ref,
             qb1_ref, kb1_ref, vb1_ref, qb4_ref, kb4_ref, vb4_ref, qb16_ref, kb16_ref, vb16_ref,
             qc_ref, gate_ref, wm_all, wo_all, proj, w_scr, w_sems, wm_b, wo_b, send_sems, recv_sems, local_sems):
        step = pl.program_id(0)
        shards, gathered = (wm_b, wo_b), (wm_all, wo_all)

        def gather_copies(arriving):
            pos = _mesh_pos()
            me = _dev_index(pos)
            local = [] if arriving else [
                pltpu.make_async_copy(shards[a], gathered[a].at[me], local_sems.at[a]) for a in range(2)]
            remote = []
            for s in range(1, N_DEV):
                peer = _xor_peer(pos, s)
                for a in range(2):
                    remote.append(pltpu.make_async_remote_copy(
                        src_ref=shards[a], dst_ref=gathered[a].at[_dev_index(peer) if arriving else me],
                        send_sem=send_sems.at[a, s], recv_sem=recv_sems.at[a, s], device_id=peer,
                        device_id_type=MESH_ID))
            return local, remote

        @pl.when(step == 0)
        def _():
            wm_b[...] = wm_ref[...].astype(BF16)
            wo_b[...] = wo_ref[...].astype(BF16)
            local, sends = gather_copies(arriving=False)
            for cp in local + sends:
                cp.start()

        _stage_w_in(w_hbm, w_scr, w_sems)
        u = u_ref[...]
        for n0 in range(0, D_IN, D_MODEL):
            acc = jnp.dot(u, w_scr[:, n0:n0 + D_MODEL], preferred_element_type=F32)
            for c3 in range(D_MODEL // LANES):
                proj[n0 // LANES + c3] = acc[:, LANES * c3:LANES * (c3 + 1)]
        c, up, dn = c_ref[...], up_ref[...], dn_ref[...]

        def cols(lo, hi, rot=False, scale=None):
            parts = []
            for ch in range(lo // LANES, hi // LANES):
                t = proj[ch]
                if rot:
                    t = _rotate128(t, c, up, dn)
                if scale is not None:
                    t = t * scale
                parts.append(t)
            return parts[0] if len(parts) == 1 else jnp.concatenate(parts, axis=1)

        qa_ref[...] = cols(0, 384, True, SCALE).astype(BF16)
        ka_ref[...] = cols(384, 512, True).astype(BF16)
        va_ref[...] = cols(512, 640).astype(BF16)
        gate_ref[:, 0:384] = cols(640, 1024).astype(BF16)
        gate_ref[:, 384:768] = cols(2176, 2560).astype(BF16)
        gate_ref[:, 768:1024] = cols(2816, 3072).astype(BF16)
        qc_ref[...] = cols(2560, 2816, False, SCALE).astype(BF16)
        for ch in range(1024 // LANES, 1408 // LANES):
            proj[ch] = _rotate128(proj[ch], c, up, dn) * SCALE
        for ch in range(1408 // LANES, 1792 // LANES):
            proj[ch] = _rotate128(proj[ch], c, up, dn)
        for lo, nat, p4, p16 in ((1024, qb1_ref, qb4_ref, qb16_ref), (1408, kb1_ref, kb4_ref, kb16_ref),
                                 (1792, vb1_ref, vb4_ref, vb16_ref)):
            chunks = range(lo // LANES, lo // LANES + B_W // LANES)
            nat[...] = jnp.concatenate([proj[ch] for ch in chunks], axis=1).astype(BF16)
            for dil, ref in ((4, p4), (16, p16)):
                for rr in range(dil):
                    ref[0, rr] = jnp.concatenate(
                        [proj.at[ch][pl.ds(rr, tm // dil, stride=dil), :] for ch in chunks], axis=1).astype(BF16)

        @pl.when(step == n_steps - 1)
        def _():
            for cp in gather_copies(arriving=True)[1]:
                cp.wait_recv()
            local, sends = gather_copies(arriving=False)
            for cp in sends:
                cp.wait_send()
            for cp in local:
                cp.wait()

    nat_w = (A_W, A_KV_W, A_KV_W, B_W, B_W, B_W)
    out_specs = [_row(tm, w) for w in nat_w]
    out_shape = [jax.ShapeDtypeStruct((seq, w), BF16) for w in nat_w]
    for dil in DILS:
        out_specs += [_perm_spec(tm, dil, B_W)] * 3
        out_shape += [jax.ShapeDtypeStruct((seq // (BLOCK * dil), dil, BLOCK, B_W), BF16)] * 3
    hbm = pl.BlockSpec(memory_space=pl.ANY)
    out_specs += [_row(tm, C_W), _row(tm, D_MODEL), hbm, hbm]
    out_shape += [jax.ShapeDtypeStruct((seq, C_W), BF16), jax.ShapeDtypeStruct((seq, D_MODEL), BF16),
                  jax.ShapeDtypeStruct((N_DEV,) + w_mem.shape, BF16), jax.ShapeDtypeStruct((N_DEV,) + w_out.shape, BF16)]
    res = pl.pallas_call(
        body, name="inproj", grid=(n_steps,),
        in_specs=[_row(tm, D_MODEL), hbm, _row(tm, LANES), _row(tm, LANES), _row(tm, LANES),
                  _full(w_mem.shape), _full(w_out.shape)],
        out_specs=out_specs, out_shape=out_shape,
        scratch_shapes=[pltpu.VMEM((n_chunk, tm, LANES), F32)] + _w_in_scratch()
        + [pltpu.VMEM(w_mem.shape, BF16), pltpu.VMEM(w_out.shape, BF16), pltpu.SemaphoreType.DMA((2, N_DEV)),
           pltpu.SemaphoreType.DMA((2, N_DEV)), pltpu.SemaphoreType.DMA((2,))],
        compiler_params=_params(dimension_semantics=("arbitrary",)),
    )(u, w_in_full, *tabs, w_mem, w_out)
    qa, ka, va = res[0:3]
    qkv_b = {1: res[3:6], 4: [t.reshape(seq, B_W) for t in res[6:9]], 16: [t.reshape(seq, B_W) for t in res[9:12]]}
    return qa, ka, va, qkv_b, res[12], res[13], res[14], res[15]


def _memkv_fwd(mem, mem_g, w_mem_full):
    def body(mem_ref, g_ref, w_ref, mn_ref, mk_ref, mv_ref):
        mv_ = mem_ref[...]
        r = lax.rsqrt(jnp.mean(mv_ * mv_, axis=-1, keepdims=True) + RMS_EPS)
        mn = ((mv_ * r) * g_ref[...]).astype(BF16)
        mn_ref[...] = mn
        mkv = jnp.dot(mn, w_ref[...], preferred_element_type=F32)
        mk_ref[...] = mkv[:, 0:C_W].astype(BF16)
        mv_ref[...] = mkv[:, C_W:2 * C_W].astype(BF16)

    return pl.pallas_call(
        body, name="memkv_fwd",
        out_shape=[jax.ShapeDtypeStruct((N_MEM, D_MODEL), BF16),
                   jax.ShapeDtypeStruct((N_MEM, C_W), BF16), jax.ShapeDtypeStruct((N_MEM, C_W), BF16)],
        compiler_params=_params(),
    )(mem, mem_g, w_mem_full)


def _memkv_bwd(mem, mem_g, mn, w_mem_full, dmk, dmv):
    def body(mem_ref, g_ref, mn_ref, w_ref, dmk_ref, dmv_ref, dw_ref, st_ref):
        dmkv = jnp.concatenate([dmk_ref[...], dmv_ref[...]], axis=1).astype(BF16)
        dw_ref[...] = lax.dot_general(mn_ref[...], dmkv, (((0,), (0,)), ((), ())), preferred_element_type=F32)
        dmn = lax.dot_general(dmkv, w_ref[...], (((1,), (1,)), ((), ())), preferred_element_type=F32)
        mv_ = mem_ref[...]
        r = lax.rsqrt(jnp.mean(mv_ * mv_, axis=-1, keepdims=True) + RMS_EPS)
        st_ref[...] = jnp.zeros_like(st_ref)
        st_ref[0:1, :] = jnp.sum(dmn * (mv_ * r), axis=0, keepdims=True)

    return pl.pallas_call(
        body, name="memkv_bwd",
        out_shape=[jax.ShapeDtypeStruct((D_MODEL, 2 * C_W), F32), jax.ShapeDtypeStruct((8, D_MODEL), F32)],
        compiler_params=_params(),
    )(mem, mem_g, mn, w_mem_full, dmk, dmv)


def _band_mask(has_prev, max_dist):
    qi = lax.broadcasted_iota(jnp.int32, (BLOCK, 2 * BLOCK), 0)
    kj = lax.broadcasted_iota(jnp.int32, (BLOCK, 2 * BLOCK), 1)
    dist = qi + BLOCK - kj
    return (dist >= 0) & (dist <= max_dist) & ((kj >= BLOCK) | has_prev)


_NT = (((1,), (1,)), ((), ()))
_TN = (((0,), (0,)), ((), ()))


def _head_only(val, h):
    slab = val[:, LANES * (h // 2):LANES * (h // 2 + 1)]
    lane = lax.broadcasted_iota(jnp.int32, slab.shape, 1)
    keep = (lane < HEAD_DIM) if h % 2 == 0 else (lane >= HEAD_DIM)
    return jnp.where(keep, slab, jnp.zeros((), slab.dtype))


class _KvSlabs:
    def __init__(self, cat, group):
        self.cat, self.group, self.swapped = cat, group, {}

    def is_swapped(self, h):
        return (h // self.group) % 2 != h % 2

    def __call__(self, h):
        j = (h // self.group) // 2
        slab = self.cat[:, LANES * j:LANES * (j + 1)]
        if not self.is_swapped(h):
            return slab
        if j not in self.swapped:
            self.swapped[j] = jnp.concatenate([slab[:, HEAD_DIM:], slab[:, :HEAD_DIM]], axis=1)
        return self.swapped[j]


class _BandSteps:
    def __init__(self, seq, dil, nq):
        self.nq, self.rows, self.consecutive = nq, nq * BLOCK, dil == 1
        nb = seq // dil // BLOCK
        if self.consecutive:
            assert nb % nq == 0
            self.outer, self.inner, self.stride = 1, nb // nq, 1
        else:
            assert dil % nq == 0
            self.outer, self.inner, self.stride = dil // nq, nb, dil // nq

    def own(self, w, clamp=False):
        cur = (lambda i: jnp.minimum(i, self.inner - 1)) if clamp else (lambda i: i)
        return pl.BlockSpec((self.rows, w), lambda r, i: (cur(i) * self.stride + r, 0))

    def prev(self, w, clamp=False):
        cur = (lambda i: jnp.minimum(i, self.inner - 1)) if clamp else (lambda i: i)
        if self.consecutive:
            return pl.BlockSpec((BLOCK, w), lambda r, i: (jnp.maximum(cur(i) * self.nq - 1, 0), 0))
        return pl.BlockSpec((self.rows, w), lambda r, i: (jnp.maximum(cur(i) - 1, 0) * self.stride + r, 0))

    def late(self, w):
        return pl.BlockSpec((self.rows, w), lambda r, i: (jnp.maximum(i - 1, 0) * self.stride + r, 0))

    def rows_of(self, j):
        return slice(BLOCK * j, BLOCK * (j + 1))

    def keys(self, p_ref, c_ref, j):
        if not self.consecutive:
            before = p_ref[self.rows_of(j), :]
        elif j == 0:
            before = p_ref[...]
        else:
            before = c_ref[self.rows_of(j - 1), :]
        return jnp.concatenate([before, c_ref[self.rows_of(j), :]], axis=0)

    def has_prev(self, i, j):
        return True if (self.consecutive and j > 0) else (i > 0)


def _banded_fwd(q, k, v, sink, *, dil, heads, group, max_dist, nq, name):
    seq = q.shape[0]
    kvh = heads // group
    qw, kw = heads * HEAD_DIM, kvh * HEAD_DIM
    steps = _BandSteps(seq, dil, nq)

    def body(*refs):
        if sink is not None:
            sink_ref, refs = refs[0], refs[1:]
        q_ref, kp_ref, kc_ref, vp_ref, vc_ref, o_ref, lse_ref, s_scr, p_scr = refs
        i = pl.program_id(1)
        lane = lax.broadcasted_iota(jnp.int32, (BLOCK, LANES), 1)
        k_of = [_KvSlabs(steps.keys(kp_ref, kc_ref, j), group) for j in range(nq)]
        v_of = [_KvSlabs(steps.keys(vp_ref, vc_ref, j), group) for j in range(nq)]
        for j in range(nq):
            qv = q_ref[steps.rows_of(j), :]
            for h in range(heads):
                s_scr[j * heads + h] = lax.dot_general(_head_only(qv, h), k_of[j](h), _NT, preferred_element_type=F32)
        ls = {}
        for j in range(nq):
            valid = _band_mask(steps.has_prev(i, j), max_dist)
            lse_tile = jnp.zeros((BLOCK, LANES), F32)
            for h in range(heads):
                s = jnp.where(valid, s_scr[j * heads + h], NEG)
                m = jnp.max(s, axis=-1, keepdims=True)
                if sink is not None:
                    sk = sink_ref[h]
                    m = jnp.maximum(m, sk)
                p = jnp.exp(s - m)
                l = jnp.sum(p, axis=-1, keepdims=True)
                if sink is not None:
                    l = l + jnp.exp(sk - m)
                p_scr[j * heads + h] = p.astype(BF16)
                ls[j, h] = l
                lse_tile = jnp.where(lane == h, m + jnp.log(l), lse_tile)
            lse_ref[steps.rows_of(j), :] = lse_tile
        for j in range(nq):
            for pr in range(heads // 2):
                he, ho = 2 * pr, 2 * pr + 1
                even = jnp.dot(p_scr[j * heads + he], v_of[j](he), preferred_element_type=F32) / ls[j, he]
                odd = jnp.dot(p_scr[j * heads + ho], v_of[j](ho), preferred_element_type=F32) / ls[j, ho]
                o_ref[steps.rows_of(j), LANES * pr:LANES * (pr + 1)] = jnp.where(lane < HEAD_DIM, even, odd).astype(BF16)

    in_specs = [steps.own(qw), steps.prev(kw), steps.own(kw), steps.prev(kw), steps.own(kw)]
    args = [q, k, k, v, v]
    if sink is not None:
        in_specs = [pl.BlockSpec(memory_space=pltpu.SMEM)] + in_specs
        args = [sink] + args
    return pl.pallas_call(
        body, name=name, grid=(steps.outer, steps.inner), in_specs=in_specs,
        out_specs=[steps.own(qw), steps.own(LANES)],
        out_shape=[jax.ShapeDtypeStruct((seq, qw), BF16), jax.ShapeDtypeStruct((seq, LANES), F32)],
        scratch_shapes=[pltpu.VMEM((nq * heads, BLOCK, 2 * BLOCK), F32), pltpu.VMEM((nq * heads, BLOCK, 2 * BLOCK), BF16)],
        compiler_params=_params(dimension_semantics=("arbitrary", "arbitrary")),
    )(*args)


def _banded_bwd(q, k, v, d_out, stat, sink, *, dil, heads, group, max_dist, nq, name):
    seq = q.shape[0]
    kvh = heads // group
    qw, kw = heads * HEAD_DIM, kvh * HEAD_DIM
    steps = _BandSteps(seq, dil, nq)
    n_in = 7

    def body(*refs):
        if sink is not None:
            sink_ref, refs = refs[0], refs[1:]
            dsink_ref, refs = refs[n_in], refs[:n_in] + refs[n_in + 1:]
        (q_ref, kp_ref, kc_ref, vp_ref, vc_ref, do_ref, st_ref, dq_ref, dk_ref, dv_ref,
         kcar, vcar, s_scr, dp_scr, p_scr, ds_scr) = refs
        r, i = pl.program_id(0), pl.program_id(1)

        @pl.when(i == 0)
        def _():
            kcar[...] = jnp.zeros_like(kcar)
            vcar[...] = jnp.zeros_like(vcar)

        if sink is not None:
            @pl.when((i == 0) & (r == 0))
            def _():
                dsink_ref[...] = jnp.zeros_like(dsink_ref)

        @pl.when(i < steps.inner)
        def _():
            lane = lax.broadcasted_iota(jnp.int32, (1, LANES), 1)
            lane_q = lax.broadcasted_iota(jnp.int32, (BLOCK, LANES), 1)
            k_of = [_KvSlabs(steps.keys(kp_ref, kc_ref, j), group) for j in range(nq)]
            v_of = [_KvSlabs(steps.keys(vp_ref, vc_ref, j), group) for j in range(nq)]
            qms, doms = {}, {}
            for j in range(nq):
                qv, dov = q_ref[steps.rows_of(j), :], do_ref[steps.rows_of(j), :]
                for h in range(heads):
                    qms[j, h], doms[j, h] = _head_only(qv, h), _head_only(dov, h)
                    s_scr[j * heads + h] = lax.dot_general(qms[j, h], k_of[j](h), _NT, preferred_element_type=F32)
                    dp_scr[j * heads + h] = lax.dot_general(doms[j, h], v_of[j](h), _NT, preferred_element_type=F32)
            dsink_row = jnp.zeros((1, LANES), F32)
            for j in range(nq):
                st = st_ref[steps.rows_of(j), :]
                valid = _band_mask(steps.has_prev(i, j), max_dist)
                for h in range(heads):
                    lse_h = st[:, h:h + 1]
                    delta = st[:, DELTA_LANE + h:DELTA_LANE + h + 1]
                    p = jnp.where(valid, jnp.exp(s_scr[j * heads + h] - lse_h), 0.0)
                    p_scr[j * heads + h] = p.astype(BF16)
                    ds_scr[j * heads + h] = (p * (dp_scr[j * heads + h] - delta)).astype(BF16)
                    if sink is not None:
                        ds_sink = jnp.sum(-jnp.exp(sink_ref[h] - lse_h) * delta, axis=0, keepdims=True)
                        dsink_row = dsink_row + jnp.where(lane == h, ds_sink, 0.0)
            for j in range(nq):
                for pr in range(heads // 2):
                    he, ho = 2 * pr, 2 * pr + 1
                    even = jnp.dot(ds_scr[j * heads + he], k_of[j](he), preferred_element_type=F32)
                    odd = jnp.dot(ds_scr[j * heads + ho], k_of[j](ho), preferred_element_type=F32)
                    dq_ref[steps.rows_of(j), LANES * pr:LANES * (pr + 1)] = (
                        jnp.where(lane_q < HEAD_DIM, even, odd).astype(BF16))
            if steps.consecutive:
                dk_ref[...] = kcar[...].astype(BF16)
                dv_ref[...] = vcar[...].astype(BF16)
            for j in range(nq):
                for slab in range(kw // LANES):
                    acc = {}
                    for h in range(heads):
                        if (h // group) // 2 != slab:
                            continue
                        key = k_of[j].is_swapped(h)
                        dk_h = lax.dot_general(ds_scr[j * heads + h], qms[j, h], _TN, preferred_element_type=F32)
                        dv_h = lax.dot_general(p_scr[j * heads + h], doms[j, h], _TN, preferred_element_type=F32)
                        acc[key] = (dk_h, dv_h) if key not in acc else (acc[key][0] + dk_h, acc[key][1] + dv_h)
                    dk_j, dv_j = acc.get(False, (None, None))
                    if True in acc:
                        unswap = lambda t: jnp.concatenate([t[:, HEAD_DIM:], t[:, :HEAD_DIM]], axis=1)
                        dk_s, dv_s = unswap(acc[True][0]), unswap(acc[True][1])
                        dk_j = dk_s if dk_j is None else dk_j + dk_s
                        dv_j = dv_s if dv_j is None else dv_j + dv_s
                    sl = slice(LANES * slab, LANES * (slab + 1))
                    own_rows = steps.rows_of(j)
                    if not steps.consecutive:
                        dk_ref[own_rows, sl] = (kcar[own_rows, sl] + dk_j[0:BLOCK]).astype(BF16)
                        dv_ref[own_rows, sl] = (vcar[own_rows, sl] + dv_j[0:BLOCK]).astype(BF16)
                    elif j == 0:
                        last = steps.rows_of(nq - 1)
                        dk_ref[last, sl] = (kcar[last, sl] + dk_j[0:BLOCK]).astype(BF16)
                        dv_ref[last, sl] = (vcar[last, sl] + dv_j[0:BLOCK]).astype(BF16)
                    else:
                        before = steps.rows_of(j - 1)
                        kcar[before, sl] += dk_j[0:BLOCK]
                        vcar[before, sl] += dv_j[0:BLOCK]
                    kcar[own_rows, sl] = dk_j[BLOCK:2 * BLOCK]
                    vcar[own_rows, sl] = dv_j[BLOCK:2 * BLOCK]
            if sink is not None:
                dsink_ref[0:1, :] += dsink_row

        @pl.when(i == steps.inner)
        def _():
            dk_ref[...] = kcar[...].astype(BF16)
            dv_ref[...] = vcar[...].astype(BF16)

    own, prev = (lambda w: steps.own(w, clamp=True)), (lambda w: steps.prev(w, clamp=True))
    in_specs = [own(qw), prev(kw), own(kw), prev(kw), own(kw), own(qw), own(LANES)]
    args = [q, k, k, v, v, d_out, stat]
    out_specs = [own(qw), steps.late(kw), steps.late(kw)]
    out_shape = [jax.ShapeDtypeStruct((seq, qw), BF16), jax.ShapeDtypeStruct((seq, kw), BF16),
                 jax.ShapeDtypeStruct((seq, kw), BF16)]
    if sink is not None:
        in_specs = [pl.BlockSpec(memory_space=pltpu.SMEM)] + in_specs
        args = [sink] + args
        out_specs = [_full((8, LANES))] + out_specs
        out_shape = [jax.ShapeDtypeStruct((8, LANES), F32)] + out_shape
    n_hb = nq * heads
    res = pl.pallas_call(
        body, name=name, grid=(steps.outer, steps.inner + 1), in_specs=in_specs, out_specs=out_specs,
        out_shape=out_shape,
        scratch_shapes=[pltpu.VMEM((steps.rows, kw), F32), pltpu.VMEM((steps.rows, kw), F32)]
        + [pltpu.VMEM((n_hb, BLOCK, 2 * BLOCK), F32)] * 2 + [pltpu.VMEM((n_hb, BLOCK, 2 * BLOCK), BF16)] * 2,
        compiler_params=_params(dimension_semantics=("arbitrary", "arbitrary")),
    )(*args)
    if sink is not None:
        return res[1], res[2], res[3], res[0]
    return res


def _cross_fwd(q, mk, mv, tq=512):
    seq = q.shape[0]

    def body(q_ref, mk_ref, mv_ref, o_ref, lse_ref, s_scr, p_scr):
        qv = q_ref[...]
        k_of, v_of = _KvSlabs(mk_ref[...], 1), _KvSlabs(mv_ref[...], 1)
        lane = lax.broadcasted_iota(jnp.int32, (tq, LANES), 1)
        lse_tile = jnp.zeros((tq, LANES), F32)
        for h in range(C_HEADS):
            s_scr[h] = lax.dot_general(_head_only(qv, h), k_of(h), _NT, preferred_element_type=F32)
        ls = []
        for h in range(C_HEADS):
            s = s_scr[h]
            m = jnp.max(s, axis=-1, keepdims=True)
            p = jnp.exp(s - m)
            l = jnp.sum(p, axis=-1, keepdims=True)
            p_scr[h] = p.astype(BF16)
            ls.append(l)
            lse_tile = jnp.where(lane == h, m + jnp.log(l), lse_tile)
        for pr in range(C_HEADS // 2):
            even = jnp.dot(p_scr[2 * pr], v_of(2 * pr), preferred_element_type=F32) / ls[2 * pr]
            odd = jnp.dot(p_scr[2 * pr + 1], v_of(2 * pr + 1), preferred_element_type=F32) / ls[2 * pr + 1]
            o_ref[:, LANES * pr:LANES * (pr + 1)] = jnp.where(lane < HEAD_DIM, even, odd).astype(BF16)
        lse_ref[...] = lse_tile

    return pl.pallas_call(
        body, name="cross_fwd", grid=(seq // tq,),
        in_specs=[_row(tq, C_W), _full((N_MEM, C_W)), _full((N_MEM, C_W))],
        out_specs=[_row(tq, C_W), _row(tq, LANES)],
        out_shape=[jax.ShapeDtypeStruct((seq, C_W), BF16), jax.ShapeDtypeStruct((seq, LANES), F32)],
        scratch_shapes=[pltpu.VMEM((C_HEADS, tq, N_MEM), F32), pltpu.VMEM((C_HEADS, tq, N_MEM), BF16)],
        compiler_params=_params(dimension_semantics=("arbitrary",)),
    )(q, mk, mv)


def _cross_bwd(q, mk, mv, d_out, stat, tq=512):
    seq = q.shape[0]

    def body(q_ref, mk_ref, mv_ref, do_ref, st_ref, dq_ref, dmk_ref, dmv_ref, s_scr, dp_scr, p_scr, ds_scr):
        @pl.when(pl.program_id(0) == 0)
        def _():
            dmk_ref[...] = jnp.zeros_like(dmk_ref)
            dmv_ref[...] = jnp.zeros_like(dmv_ref)

        qv, dov, st = q_ref[...], do_ref[...], st_ref[...]
        k_of, v_of = _KvSlabs(mk_ref[...], 1), _KvSlabs(mv_ref[...], 1)
        qms = [_head_only(qv, h) for h in range(C_HEADS)]
        doms = [_head_only(dov, h) for h in range(C_HEADS)]
        for h in range(C_HEADS):
            s_scr[h] = lax.dot_general(qms[h], k_of(h), _NT, preferred_element_type=F32)
            dp_scr[h] = lax.dot_general(doms[h], v_of(h), _NT, preferred_element_type=F32)
        for h in range(C_HEADS):
            p = jnp.exp(s_scr[h] - st[:, h:h + 1])
            p_scr[h] = p.astype(BF16)
            ds_scr[h] = (p * (dp_scr[h] - st[:, DELTA_LANE + h:DELTA_LANE + h + 1])).astype(BF16)
        lane = lax.broadcasted_iota(jnp.int32, (tq, LANES), 1)
        for pr in range(C_HEADS // 2):
            sl = slice(LANES * pr, LANES * (pr + 1))
            even = jnp.dot(ds_scr[2 * pr], k_of(2 * pr), preferred_element_type=F32)
            odd = jnp.dot(ds_scr[2 * pr + 1], k_of(2 * pr + 1), preferred_element_type=F32)
            dq_ref[:, sl] = jnp.where(lane < HEAD_DIM, even, odd).astype(BF16)
            dmk_ref[:, sl] += (lax.dot_general(ds_scr[2 * pr], qms[2 * pr], _TN, preferred_element_type=F32)
                               + lax.dot_general(ds_scr[2 * pr + 1], qms[2 * pr + 1], _TN, preferred_element_type=F32))
            dmv_ref[:, sl] += (lax.dot_general(p_scr[2 * pr], doms[2 * pr], _TN, preferred_element_type=F32)
                               + lax.dot_general(p_scr[2 * pr + 1], doms[2 * pr + 1], _TN, preferred_element_type=F32))

    return pl.pallas_call(
        body, name="cross_bwd", grid=(seq // tq,),
        in_specs=[_row(tq, C_W), _full((N_MEM, C_W)), _full((N_MEM, C_W)), _row(tq, C_W), _row(tq, LANES)],
        out_specs=[_row(tq, C_W), _full((N_MEM, C_W)), _full((N_MEM, C_W))],
        out_shape=[jax.ShapeDtypeStruct((seq, C_W), BF16), jax.ShapeDtypeStruct((N_MEM, C_W), F32),
                   jax.ShapeDtypeStruct((N_MEM, C_W), F32)],
        scratch_shapes=[pltpu.VMEM((C_HEADS, tq, N_MEM), F32)] * 2 + [pltpu.VMEM((C_HEADS, tq, N_MEM), BF16)] * 2,
        compiler_params=_params(dimension_semantics=("arbitrary",)),
    )(q, mk, mv, d_out, stat)


def _per_head(tile, width):
    rows = tile.shape[0]
    return jnp.concatenate(
        [jnp.broadcast_to(tile[:, h:h + 1], (rows, HEAD_DIM)) for h in range(width // HEAD_DIM)], axis=1)


def _with_delta(lse_tile, prod):
    rows = lse_tile.shape[0]
    lane = lax.broadcasted_iota(jnp.int32, (rows, LANES), 1)
    tile = lse_tile
    for h in range(prod.shape[1] // HEAD_DIM):
        d = jnp.sum(prod[:, HEAD_DIM * h:HEAD_DIM * (h + 1)], axis=-1, keepdims=True)
        tile = jnp.where(lane == DELTA_LANE + h, d, tile)
    return tile


def _mid(oa, lse_a, ob, lse_b, oc, lse_c, gate, x, target, w_out_full, post_g, tm=256):
    seq = x.shape[0]
    n_b = B_W // LANES

    def body(oa_ref, la_ref, b1_ref, l1_ref, b4_ref, l4_ref, b16_ref, l16_ref, oc_ref, lc_ref,
             gate_ref, x_ref, t_ref, w_ref, pg_ref,
             dh_ref, dg_ref, doa_ref, sa_ref, dob1_ref, sb1_ref, dob4_ref, sb4_ref, dob16_ref, sb16_ref,
             doc_ref, sc_ref, dw_ref, st_ref, scr_b4, scr_b16, scr_l4, scr_l16, scr_do, scr_sb):
        @pl.when(pl.program_id(0) == 0)
        def _():
            dw_ref[...] = jnp.zeros_like(dw_ref)
            st_ref[...] = jnp.zeros_like(st_ref)

        b1, l1 = b1_ref[...].astype(F32), l1_ref[...]
        b4, l4 = _load_permuted(b4_ref, scr_b4, 4), _load_permuted(l4_ref, scr_l4, 4)
        b16, l16 = _load_permuted(b16_ref, scr_b16, 16), _load_permuted(l16_ref, scr_l16, 16)
        lm = jnp.maximum(jnp.maximum(l1, l4), l16)
        e1, e4, e16 = jnp.exp(l1 - lm), jnp.exp(l4 - lm), jnp.exp(l16 - lm)
        den = e1 + e4 + e16
        lse_b_tile = lm + jnp.log(den)
        ob_v = _per_head(e1 / den, B_W) * b1 + _per_head(e4 / den, B_W) * b4 + _per_head(e16 / den, B_W) * b16
        o_all = jnp.concatenate([oa_ref[...].astype(F32), ob_v, oc_ref[...].astype(F32)], axis=1)
        g = gate_ref[...].astype(F32)
        sig = 1.0 / (1.0 + jnp.exp(-g))
        silu = g * sig
        y = (o_all * silu).astype(BF16)
        w = w_ref[...]
        z = jnp.dot(y, w, preferred_element_type=F32)
        rz = lax.rsqrt(jnp.mean(z * z, axis=-1, keepdims=True) + RMS_EPS)
        hn = z * rz
        pg = pg_ref[...]
        err = (x_ref[...] + hn * pg) - t_ref[...]
        loss = 0.5 * jnp.sum(jnp.mean(err * err, axis=-1, keepdims=True), axis=0, keepdims=True)
        dh = err * (1.0 / D_MODEL)
        dh_ref[...] = dh.astype(BF16)
        st_ref[0:1, :] += jnp.sum(dh * hn, axis=0, keepdims=True)
        st_ref[1:2, :] += jnp.broadcast_to(loss, (1, D_MODEL))
        dhn = dh * pg
        dz = (rz * (dhn - hn * jnp.mean(dhn * hn, axis=-1, keepdims=True))).astype(BF16)
        dy = lax.dot_general(dz, w, _NT, preferred_element_type=F32)
        dw_ref[...] += lax.dot_general(y, dz, _TN, preferred_element_type=F32)
        dg_ref[...] = (dy * o_all * (sig * (1.0 + g * (1.0 - sig)))).astype(BF16)
        d_o = (dy * silu).astype(BF16)
        prod = d_o.astype(F32) * o_all
        doa_ref[...] = d_o[:, 0:A_W]
        sa_ref[...] = _with_delta(la_ref[...], prod[:, 0:A_W])
        doc_ref[...] = d_o[:, A_W + B_W:D_MODEL]
        sc_ref[...] = _with_delta(lc_ref[...], prod[:, A_W + B_W:D_MODEL])
        d_ob = d_o[:, A_W:A_W + B_W]
        stat_b = _with_delta(lse_b_tile, prod[:, A_W:A_W + B_W])
        dob1_ref[...] = d_ob
        sb1_ref[...] = stat_b
        _put(scr_do, d_ob.astype(F32))
        _put(scr_sb, stat_b)
        _store_permuted(scr_do, dob4_ref, 4, BF16)
        _store_permuted(scr_sb, sb4_ref, 4, F32)
        _store_permuted(scr_do, dob16_ref, 16, BF16)
        _store_permuted(scr_sb, sb16_ref, 16, F32)

    p4 = lambda w: _perm_spec(tm, 4, w)
    p16 = lambda w: _perm_spec(tm, 16, w)
    in_specs = [_row(tm, A_W), _row(tm, LANES), _row(tm, B_W), _row(tm, LANES), p4(B_W), p4(LANES), p16(B_W), p16(LANES),
                _row(tm, C_W), _row(tm, LANES), _row(tm, D_MODEL), _row(tm, D_MODEL), _row(tm, D_MODEL),
                _full((D_MODEL, D_MODEL)), _full((1, D_MODEL))]
    sds = jax.ShapeDtypeStruct
    v4 = lambda w, dt: sds((seq // (BLOCK * 4), 4, BLOCK, w), dt)
    v16 = lambda w, dt: sds((seq // (BLOCK * 16), 16, BLOCK, w), dt)
    out_specs = [_row(tm, D_MODEL), _row(tm, D_MODEL), _row(tm, A_W), _row(tm, LANES), _row(tm, B_W), _row(tm, LANES),
                 p4(B_W), p4(LANES), p16(B_W), p16(LANES), _row(tm, C_W), _row(tm, LANES),
                 _full((D_MODEL, D_MODEL)), _full((8, D_MODEL))]
    out_shape = [sds((seq, D_MODEL), BF16), sds((seq, D_MODEL), BF16), sds((seq, A_W), BF16), sds((seq, LANES), F32),
                 sds((seq, B_W), BF16), sds((seq, LANES), F32), v4(B_W, BF16), v4(LANES, F32), v16(B_W, BF16),
                 v16(LANES, F32), sds((seq, C_W), BF16), sds((seq, LANES), F32),
                 sds((D_MODEL, D_MODEL), F32), sds((8, D_MODEL), F32)]
    res = pl.pallas_call(
        body, name="mid", grid=(seq // tm,), in_specs=in_specs, out_specs=out_specs, out_shape=out_shape,
        scratch_shapes=[pltpu.VMEM((n_b, tm, LANES), F32), pltpu.VMEM((n_b, tm, LANES), F32),
                        pltpu.VMEM((1, tm, LANES), F32), pltpu.VMEM((1, tm, LANES), F32),
                        pltpu.VMEM((n_b, tm, LANES), F32), pltpu.VMEM((1, tm, LANES), F32)],
        compiler_params=_params(dimension_semantics=("arbitrary",)),
    )(oa, lse_a, ob[1], lse_b[1], _perm_view(ob[4], 4), _perm_view(lse_b[4], 4), _perm_view(ob[16], 16),
      _perm_view(lse_b[16], 16), oc, lse_c, gate, x, target, w_out_full, post_g)
    dh, d_gate, do_a, st_a, do_b1, st_b1, do_b4, st_b4, do_b16, st_b16, do_c, st_c, d_wout, stats = res
    flat = lambda t: t.reshape(seq, t.shape[-1])
    d_b = {1: (do_b1, st_b1), 4: (flat(do_b4), flat(st_b4)), 16: (flat(do_b16), flat(st_b16))}
    return dh, d_gate, (do_a, st_a), d_b, (do_c, st_c), d_wout, stats


def _inproj_bwd_x(x, dh, pre_g, w_in_full, tabs, dqa, dka, dva, dqkv_b, dqc, dgate, tm=256):
    seq = x.shape[0]
    n_b = B_W // LANES

    def body(x_ref, dh_ref, g_ref, w_hbm, c_ref, up_ref, dn_ref, dqa_ref, dka_ref, dva_ref,
             dq1, dk1, dv1, dq4, dk4, dv4, dq16, dk16, dv16, dqc_ref, dg_ref,
             gx_ref, dp_ref, st_ref, scr4, scr16, w_scr, w_sems):
        _stage_w_in(w_hbm, w_scr, w_sems)

        @pl.when(pl.program_id(0) == 0)
        def _():
            st_ref[...] = jnp.zeros_like(st_ref)

        c, up, dn = c_ref[...], -up_ref[...], -dn_ref[...]
        unrot = lambda t: _rotate(t, c, up, dn)
        total = lambda r1, r4, r16: (r1[...].astype(F32) + _load_permuted(r4, scr4, 4)
                                     + _load_permuted(r16, scr16, 16))
        dp_ref[:, 0:384] = (unrot(dqa_ref[...].astype(F32)) * SCALE).astype(BF16)
        dp_ref[:, 384:512] = unrot(dka_ref[...].astype(F32)).astype(BF16)
        dp_ref[:, 512:640] = dva_ref[...]
        dp_ref[:, 640:1024] = dg_ref[:, 0:384]
        dp_ref[:, 1024:1408] = (unrot(total(dq1, dq4, dq16)) * SCALE).astype(BF16)
        dp_ref[:, 1408:1792] = unrot(total(dk1, dk4, dk16)).astype(BF16)
        dp_ref[:, 1792:2176] = total(dv1, dv4, dv16).astype(BF16)
        dp_ref[:, 2176:2560] = dg_ref[:, 384:768]
        dp_ref[:, 2560:2816] = (dqc_ref[...].astype(F32) * SCALE).astype(BF16)
        dp_ref[:, 2816:3072] = dg_ref[:, 768:1024]
        du = lax.dot_general(dp_ref[...], w_scr[...], _NT, preferred_element_type=F32)
        xv = x_ref[...]
        r = lax.rsqrt(jnp.mean(xv * xv, axis=-1, keepdims=True) + RMS_EPS)
        xh = xv * r
        st_ref[0:1, :] += jnp.sum(du * xh, axis=0, keepdims=True)
        dxh = du * g_ref[...]
        gx_ref[...] = dh_ref[...].astype(F32) + r * (dxh - xh * jnp.mean(dxh * xh, axis=-1, keepdims=True))

    in_specs = ([_row(tm, D_MODEL), _row(tm, D_MODEL), _full((1, D_MODEL)), pl.BlockSpec(memory_space=pl.ANY),
                 _row(tm, LANES), _row(tm, LANES), _row(tm, LANES), _row(tm, A_W), _row(tm, A_KV_W), _row(tm, A_KV_W)]
                + [_row(tm, B_W)] * 3 + [_perm_spec(tm, 4, B_W)] * 3 + [_perm_spec(tm, 16, B_W)] * 3
                + [_row(tm, C_W), _row(tm, D_MODEL)])
    return pl.pallas_call(
        body, name="inproj_bwd_x", grid=(seq // tm,), in_specs=in_specs,
        out_specs=[_row(tm, D_MODEL), _row(tm, D_IN), _full((8, D_MODEL))],
        out_shape=[jax.ShapeDtypeStruct((seq, D_MODEL), F32), jax.ShapeDtypeStruct((seq, D_IN), BF16),
                   jax.ShapeDtypeStruct((8, D_MODEL), F32)],
        scratch_shapes=[pltpu.VMEM((n_b, tm, LANES), F32), pltpu.VMEM((n_b, tm, LANES), F32)] + _w_in_scratch(),
        compiler_params=_params(dimension_semantics=("arbitrary",)),
    )(x, dh, pre_g, w_in_full, *tabs, dqa, dka, dva, *dqkv_b[1], *[_perm_view(t, 4) for t in dqkv_b[4]],
      *[_perm_view(t, 16) for t in dqkv_b[16]], dqc, dgate)


class _ReduceScatter:
    def __init__(self, ins, outs, scratch):
        self.n = n = len(ins)
        self.ins, self.outs = ins, outs
        self.mine, self.got, self.snd, self.rcv = (scratch[n * t:n * (t + 1)] for t in range(4))
        self.load_sems, self.d2d_send, self.d2d_recv, self.ici_send, self.ici_recv = scratch[4 * n:]
        self.pos = _mesh_pos()
        self.pairs = [(a, kk) for kk in (3, 1, 2) for a in range(n)]

    @staticmethod
    def scratch_shapes(shapes):
        return ([pltpu.VMEM((4,) + s, F32) for s in shapes] + [pltpu.VMEM((4,) + s, F32) for s in shapes]
                + [pltpu.VMEM((3,) + s, BF16) for s in shapes] + [pltpu.VMEM((3,) + s, BF16) for s in shapes]
                + [pltpu.SemaphoreType.DMA((len(shapes), 4))] * 5)

    def _chip(self, kk):
        x, y, _ = self.pos
        return (1 - x if kk & 2 else x, 1 - y if kk & 1 else y)

    def _load(self, a, kk):
        block = _dev_index((*self._chip(kk), self.pos[2]))
        return pltpu.make_async_copy(self.ins[a].at[block], self.mine[a].at[kk], self.load_sems.at[a, kk])

    def _swap(self, a, kk):
        x, y, c = self.pos
        return pltpu.make_async_remote_copy(
            src_ref=self.ins[a].at[_dev_index((*self._chip(kk), 1 - c))], dst_ref=self.got[a].at[kk],
            send_sem=self.d2d_send.at[a, kk], recv_sem=self.d2d_recv.at[a, kk],
            device_id=(x, y, 1 - c), device_id_type=MESH_ID)

    def _hop(self, a, kk):
        return pltpu.make_async_remote_copy(
            src_ref=self.snd[a].at[kk - 1], dst_ref=self.rcv[a].at[kk - 1], send_sem=self.ici_send.at[a, kk],
            recv_sem=self.ici_recv.at[a, kk], device_id=(*self._chip(kk), self.pos[2]), device_id_type=MESH_ID)

    def start(self):
        for kk in range(4):
            for a in range(self.n):
                self._load(a, kk).start()
                self._swap(a, kk).start()

    def send_chip_sums(self):
        for a, kk in self.pairs:
            self._load(a, kk).wait()
            self._swap(a, kk).wait_recv()
            self.snd[a][kk - 1] = (self.mine[a][kk] + self.got[a][kk]).astype(BF16)
            self._hop(a, kk).start()

    def finish(self):
        for a in range(self.n):
            self._load(a, 0).wait()
            self._swap(a, 0).wait_recv()
            acc = self.mine[a][0] + self.got[a][0]
            for kk in (1, 2, 3):
                self._hop(a, kk).wait_recv()
                acc = acc + self.rcv[a][kk - 1].astype(F32)
            self.outs[a][...] = acc
        for kk in range(4):
            for a in range(self.n):
                self._swap(a, kk).wait_send()
        for a, kk in self.pairs:
            self._hop(a, kk).wait_send()


def _inproj_bwd_w(u, dproj, d_wmem, d_wout, tm=1024):
    seq = u.shape[0]

    n_steps = seq // tm
    shapes = [d_wmem.shape[1:], d_wout.shape[1:]]

    def body(u_ref, dp_ref, wm, wo, dw_ref, g_wm, g_wo, *scratch):
        step = pl.program_id(0)
        exchange = _ReduceScatter((wm, wo), (g_wm, g_wo), scratch)

        @pl.when(step == 0)
        def _():
            dw_ref[...] = jnp.zeros_like(dw_ref)
            exchange.start()

        @pl.when(step == min(1, n_steps - 1))
        def _():
            exchange.send_chip_sums()

        res = lax.dot_general(u_ref[...], dp_ref[...], _TN, preferred_element_type=F32)
        for k in range(N_DEV):
            dw_ref[k] += res[:, SHARD_IN * k:SHARD_IN * (k + 1)]

        @pl.when(step == n_steps - 1)
        def _():
            exchange.finish()

    hbm = pl.BlockSpec(memory_space=pl.ANY)
    return pl.pallas_call(
        body, name="inproj_bwd_w", grid=(n_steps,),
        in_specs=[_row(tm, D_MODEL), _row(tm, D_IN), hbm, hbm],
        out_specs=[_full((N_DEV, D_MODEL, SHARD_IN))] + [_full(s) for s in shapes],
        out_shape=[jax.ShapeDtypeStruct((N_DEV, D_MODEL, SHARD_IN), F32)] + [jax.ShapeDtypeStruct(s, F32) for s in shapes],
        scratch_shapes=_ReduceScatter.scratch_shapes(shapes),
        compiler_params=_params(dimension_semantics=("arbitrary",)),
    )(u, dproj, d_wmem, d_wout)


def _local_step(x, mem, pre_g, w_in, sink, mem_g, w_mem, w_out, post_g, target):
    u, *tabs, w_in_full = _prep(x, pre_g, w_in)
    qa, ka, va, qkv_b, qc, gate, w_mem_all, w_out_all = _inproj(u, w_in_full, tabs, w_mem, w_out)
    w_mem_full = w_mem_all.reshape(D_MODEL, 2 * C_W)
    w_out_full = w_out_all.reshape(D_MODEL, D_MODEL)
    mn, mk, mv = _memkv_fwd(mem, mem_g, w_mem_full)

    a_cfg = dict(dil=1, heads=A_HEADS, group=A_GROUP, max_dist=BLOCK - 1, nq=ATTN_BLOCKS_PER_STEP)
    b_cfgs = {dil: dict(dil=dil, heads=B_HEADS, group=1, max_dist=win // dil, nq=ATTN_BLOCKS_PER_STEP)
              for win, dil in B_CONFIGS}
    oa, lse_a = _banded_fwd(qa, ka, va, sink, name="attn_a_fwd", **a_cfg)
    ob, lse_b = {}, {}
    for dil, cfg in b_cfgs.items():
        ob[dil], lse_b[dil] = _banded_fwd(*qkv_b[dil], None, name=f"attn_b{dil}_fwd", **cfg)
    oc, lse_c = _cross_fwd(qc, mk, mv)

    dh, d_gate, d_a, d_b, d_c, d_wout, st_mid = _mid(oa, lse_a, ob, lse_b, oc, lse_c, gate, x, target, w_out_full, post_g)

    dqa, dka, dva, dsink = _banded_bwd(qa, ka, va, *d_a, sink, name="attn_a_bwd", **a_cfg)
    dqkv_b = {dil: _banded_bwd(*qkv_b[dil], *d_b[dil], None, name=f"attn_b{dil}_bwd", **cfg)
              for dil, cfg in b_cfgs.items()}
    dqc, dmk, dmv = _cross_bwd(qc, mk, mv, *d_c)
    d_wmem, st_mem = _memkv_bwd(mem, mem_g, mn, w_mem_full, dmk, dmv)

    grad_x, dproj, st_pre = _inproj_bwd_x(x, dh, pre_g, w_in_full, tabs, dqa, dka, dva, dqkv_b, dqc, d_gate)
    d_win, g_wmem, g_wout = _inproj_bwd_w(u, dproj, d_wmem.reshape(N_DEV, SHARD_ROWS, 2 * C_W),
                                          d_wout.reshape(N_DEV, SHARD_ROWS, D_MODEL))

    dsink_row = jnp.pad(dsink[0:1, :], ((0, 0), (0, D_MODEL - LANES)))
    stats = jnp.concatenate([st_pre[0:1], st_mem[0:1], st_mid[0:1], dsink_row, st_mid[1:2],
                             jnp.zeros((3, D_MODEL), F32)], axis=0)
    return grad_x, d_win, g_wmem, g_wout, stats


def _prep(x, pre_g, w_in, tm=1024, parts=2):
    seq = x.shape[0]
    n_steps = seq // tm
    rows = D_MODEL // parts
    pass_on_at = [max(n_steps - parts + a, 0) for a in range(parts)]
    j = jnp.arange(LANES) % HEAD_DIM
    freq = (ROPE_THETA ** (-(2 * (j % (ROT_DIM // 2))).astype(F32) / ROT_DIM))[None, :]

    def body(x_ref, g_ref, f_ref, win_ref, u_ref, c_ref, up_ref, dn_ref, win_out, win_b,
             send_sems, recv_sems, local_sems):
        step = pl.program_id(0)
        px, py, pc = _mesh_pos()
        me, sibling = (px, py, pc), (px, py, 1 - pc)
        chips = [(1 - px, py), (px, 1 - py), (1 - px, 1 - py)]

        def src(a):
            return win_b.at[pl.ds(rows * a, rows)]

        def slot(a, p):
            return win_out.at[_dev_index(p), pl.ds(rows * a, rows)]

        def copy(a, k, block, to, own=False):
            return pltpu.make_async_remote_copy(
                src_ref=src(a) if own else slot(a, block), dst_ref=slot(a, block),
                send_sem=send_sems.at[a, k], recv_sem=recv_sems.at[a, k], device_id=to, device_id_type=MESH_ID)

        def first_sends(a):
            return [copy(a, 1 + k, me, (*chip, pc), own=True) for k, chip in enumerate(chips)] + [
                copy(a, 0, me, sibling, own=True)]

        def local(a):
            return pltpu.make_async_copy(src(a), slot(a, me), local_sems.at[a])

        @pl.when(step == 0)
        def _():
            win_b[...] = win_ref[...].astype(BF16)
            for a in range(parts):
                local(a).start()
                for cp in first_sends(a):
                    cp.start()

        for a in range(parts):
            @pl.when(step == pass_on_at[a])
            def _(a=a):
                for k, chip in enumerate(chips):
                    copy(a, 1 + k, (*chip, pc), me).wait_recv()
                    copy(a, 4 + k, (*chip, pc), sibling).start()

        xv = x_ref[...]
        r = lax.rsqrt(jnp.mean(xv * xv, axis=-1, keepdims=True) + RMS_EPS)
        u_ref[...] = ((xv * r) * g_ref[...]).astype(BF16)
        pos = (lax.broadcasted_iota(jnp.int32, (tm, LANES), 0) + step * tm).astype(F32)
        head_lane = lax.broadcasted_iota(jnp.int32, (tm, LANES), 1) % HEAD_DIM
        ang = pos * f_ref[...]
        cos, sin = jnp.cos(ang), jnp.sin(ang)
        half = ROT_DIM // 2
        c_ref[...] = jnp.where(head_lane < ROT_DIM, cos, 1.0)
        up_ref[...] = jnp.where((head_lane >= half) & (head_lane < ROT_DIM), sin, 0.0)
        dn_ref[...] = jnp.where(head_lane < half, -sin, 0.0)

        @pl.when(step == n_steps - 1)
        def _():
            for a in range(parts):
                copy(a, 0, sibling, me).wait_recv()
                for k, chip in enumerate(chips):
                    copy(a, 4 + k, (*chip, 1 - pc), me).wait_recv()
            for a in range(parts):
                for cp in first_sends(a):
                    cp.wait_send()
                for k, chip in enumerate(chips):
                    copy(a, 4 + k, (*chip, pc), sibling).wait_send()
                local(a).wait()

    return pl.pallas_call(
        body, name="prep", grid=(n_steps,),
        in_specs=[_row(tm, D_MODEL), _full((1, D_MODEL)), _full((1, LANES)), _full(w_in.shape)],
        out_specs=[_row(tm, D_MODEL), _row(tm, LANES), _row(tm, LANES), _row(tm, LANES),
                   pl.BlockSpec(memory_space=pl.ANY)],
        out_shape=[jax.ShapeDtypeStruct((seq, D_MODEL), BF16)] + [jax.ShapeDtypeStruct((seq, LANES), F32)] * 3
        + [jax.ShapeDtypeStruct((N_DEV,) + w_in.shape, BF16)],
        scratch_shapes=[pltpu.VMEM(w_in.shape, BF16), pltpu.SemaphoreType.DMA((parts, 7)),
                        pltpu.SemaphoreType.DMA((parts, 7)), pltpu.SemaphoreType.DMA((parts,))],
        compiler_params=_params(dimension_semantics=("arbitrary",)),
    )(x, pre_g, freq, w_in)


def _exchange_grads(d_win, stats):
    def body(win, st, g_win, r_st, send_sems, recv_sems, local_sem, *scratch):
        exchange = _ReduceScatter((win,), (g_win,), scratch)
        exchange.start()
        pos = _mesh_pos()
        me = _dev_index(pos)
        own = pltpu.make_async_copy(st, r_st.at[me], local_sem)
        own.start()
        copies = []
        for s in range(1, N_DEV):
            peer = _xor_peer(pos, s)
            mk = lambda slot: pltpu.make_async_remote_copy(
                src_ref=st, dst_ref=r_st.at[slot], send_sem=send_sems.at[s], recv_sem=recv_sems.at[s],
                device_id=peer, device_id_type=MESH_ID)
            send, arrival = mk(me), mk(_dev_index(peer))
            send.start()
            copies.append((send, arrival))
        exchange.send_chip_sums()
        exchange.finish()
        for send, arrival in copies:
            arrival.wait_recv()
            send.wait_send()
        own.wait()

    hbm = pl.BlockSpec(memory_space=pl.ANY)
    shard = d_win.shape[1:]
    return pl.pallas_call(
        body, name="exchange_grads", in_specs=[hbm, hbm],
        out_specs=[pl.BlockSpec(memory_space=pltpu.VMEM), hbm],
        out_shape=[jax.ShapeDtypeStruct(shard, F32), jax.ShapeDtypeStruct((N_DEV,) + stats.shape, F32)],
        scratch_shapes=[pltpu.SemaphoreType.DMA((N_DEV,)), pltpu.SemaphoreType.DMA((N_DEV,)), pltpu.SemaphoreType.DMA(())]
        + _ReduceScatter.scratch_shapes([shard]),
        compiler_params=_params(),
    )(d_win, stats)


def _reduce_adamw(recv, w, m, v, *, tr, name):
    n_part, rows, cols = recv.shape

    def body(r_ref, w_ref, m_ref, v_ref, g_ref, d_ref, nm_ref, nv_ref):
        g = r_ref[0]
        for s in range(1, n_part):
            g = g + r_ref[s]
        g_ref[...] = g
        m2 = ADAM_B1 * m_ref[...] + (1.0 - ADAM_B1) * g
        v2 = ADAM_B2 * v_ref[...] + (1.0 - ADAM_B2) * (g * g)
        nm_ref[...] = m2
        nv_ref[...] = v2
        m_hat = m2 / (1.0 - ADAM_B1 ** ADAM_STEP)
        v_hat = v2 / (1.0 - ADAM_B2 ** ADAM_STEP)
        d_ref[...] = -ADAM_LR * (m_hat / (jnp.sqrt(v_hat) + ADAM_EPS) + ADAM_WD * w_ref[...])

    blk = pl.BlockSpec((tr, cols), lambda i: (i, 0))
    return pl.pallas_call(
        body, name=name, grid=(rows // tr,),
        in_specs=[pl.BlockSpec((n_part, tr, cols), lambda i: (0, i, 0)), blk, blk, blk],
        out_specs=[blk] * 4, out_shape=[jax.ShapeDtypeStruct((rows, cols), F32)] * 4,
        compiler_params=_params(dimension_semantics=("arbitrary",)),
    )(recv, w, m, v)


def _pack_rows(pre, memn, post, sink):
    sink_row = jnp.pad(sink, ((0, 0), (0, D_MODEL - A_HEADS)))
    return jnp.concatenate([pre, memn, post, sink_row, jnp.zeros((4, D_MODEL), F32)], axis=0)


def kernel(x, mem, pre_norm, w_in, sink_a, mem_norm, w_mem_kv, w_out, post_norm, loss_target, m_pre_norm, m_w_in, m_sink_a, m_mem_norm, m_w_mem_kv, m_w_out, m_post_norm, v_pre_norm, v_w_in, v_sink_a, v_mem_norm, v_w_mem_kv, v_w_out, v_post_norm):
    sink = jnp.pad(sink_a[0], (0, 8 - A_HEADS))
    grad_x, d_win, g_wmem, g_wout, stats = _local_step(
        x[0], mem[0], pre_norm, w_in[0], sink, mem_norm, w_mem_kv[0], w_out[0], post_norm, loss_target[0])
    g_win, r_stats = _exchange_grads(d_win, stats)

    big = {}
    for nm, g, w, m, v in (("w_in", g_win, w_in, m_w_in, v_w_in),
                           ("w_mem_kv", g_wmem, w_mem_kv, m_w_mem_kv, v_w_mem_kv),
                           ("w_out", g_wout, w_out, m_w_out, v_w_out)):
        res = _reduce_adamw(g[None], w[0], m[0], v[0], tr=SHARD_ROWS, name="adamw_" + nm)
        big[nm] = [t[None] for t in res]
    small = _reduce_adamw(
        r_stats, _pack_rows(pre_norm, mem_norm, post_norm, sink_a),
        _pack_rows(m_pre_norm, m_mem_norm, m_post_norm, m_sink_a),
        _pack_rows(v_pre_norm, v_mem_norm, v_post_norm, v_sink_a), tr=8, name="adamw_small")

    def unpack(t):
        return {"pre_norm": t[0:1], "mem_norm": t[1:2], "post_norm": t[2:3], "sink_a": t[3:4, 0:A_HEADS]}

    order = ("pre_norm", "w_in", "sink_a", "mem_norm", "w_mem_kv", "w_out", "post_norm")
    outs = [small[0][4, 0], grad_x[None]]
    for j in range(4):
        sm = unpack(small[j])
        outs += [big[n][j] if n in big else sm[n] for n in order]
    return tuple(outs)
```

```python
import jax
import jax.numpy as jnp
from jax import lax
from jax.experimental import pallas as pl
from jax.experimental.pallas import tpu as pltpu

F32 = jnp.float32
BF16 = jnp.bfloat16

D_MODEL = 1024
HEAD_DIM = 64
ROT_DIM = 16
ROPE_THETA = 500000.0
BLOCK = 128
LANES = 128
N_MEM = 256
RMS_EPS = 1e-6
SCALE = HEAD_DIM ** -0.5
A_HEADS, A_GROUP = 6, 3
B_HEADS = 6
C_HEADS = 4
A_W, A_KV_W, B_W, C_W = 384, 128, 384, 256
D_IN = 3072
N_DEV = 8
SHARD_IN = D_IN // N_DEV
SHARD_ROWS = D_MODEL // N_DEV
B_CONFIGS = ((128, 1), (512, 4), (2048, 16))
DILS = (4, 16)
NEG = -1e30
ATTN_BLOCKS_PER_STEP = 4
DELTA_LANE = 64
VMEM_LIMIT = 56 * 1024 * 1024

ADAM_LR, ADAM_B1, ADAM_B2, ADAM_EPS, ADAM_WD, ADAM_STEP = 0.001, 0.9, 0.999, 1e-08, 0.01, 10
MESH_ID = pl.DeviceIdType.MESH


def _params(**kw):
    return pltpu.CompilerParams(vmem_limit_bytes=VMEM_LIMIT, **kw)


def _full(shape):
    n = len(shape)
    return pl.BlockSpec(shape, lambda *_: (0,) * n)


def _row(tm, w):
    return pl.BlockSpec((tm, w), lambda i: (i, 0))


def _mesh_pos():
    return lax.axis_index("x"), lax.axis_index("y"), lax.axis_index("c")


def _dev_index(pos):
    return 4 * pos[0] + 2 * pos[1] + pos[2]


def _xor_peer(pos, s):
    x, y, c = pos
    return (1 - x if s & 4 else x, 1 - y if s & 2 else y, 1 - c if s & 1 else c)


def _perm_view(a, dil):
    return a.reshape(a.shape[0] // (BLOCK * dil), dil, BLOCK, a.shape[1])


def _perm_spec(tm, dil, w):
    chunk = BLOCK * dil
    if tm >= chunk:
        return pl.BlockSpec((tm // chunk, dil, BLOCK, w), lambda i: (i, 0, 0, 0))
    per = chunk // tm
    return pl.BlockSpec((1, dil, tm // dil, w), lambda i: (i // per, 0, i % per, 0))


def _put(scr, val):
    for c in range(val.shape[1] // LANES):
        scr[c] = val[:, LANES * c:LANES * (c + 1)]


def _get(scr):
    n = scr.shape[0]
    return scr[0] if n == 1 else jnp.concatenate([scr[c] for c in range(n)], axis=1)


def _get_class(scr, r, dil):
    n, rows = scr.shape[0], scr.shape[1]
    parts = [scr.at[c][pl.ds(r, rows // dil, stride=dil), :] for c in range(n)]
    return parts[0] if n == 1 else jnp.concatenate(parts, axis=1)


def _store_permuted(scr, out_ref, dil, dtype):
    for r in range(dil):
        out_ref[0, r] = _get_class(scr, r, dil).astype(dtype)


def _fill_permuted(in_ref, scr, dil):
    n, rows = scr.shape[0], scr.shape[1]
    for r in range(dil):
        val = in_ref[0, r].astype(F32)
        for c in range(n):
            scr.at[c][pl.ds(r, rows // dil, stride=dil), :] = val[:, LANES * c:LANES * (c + 1)]


def _load_permuted(in_ref, scr, dil):
    _fill_permuted(in_ref, scr, dil)
    return _get(scr)


def _rotate128(t, c, up, dn):
    return t * c + pltpu.roll(t, 8, 1) * up + pltpu.roll(t, LANES - 8, 1) * dn


def _rotate(t, c, up, dn):
    outs = [_rotate128(t[:, LANES * j:LANES * (j + 1)], c, up, dn) for j in range(t.shape[1] // LANES)]
    return outs[0] if len(outs) == 1 else jnp.concatenate(outs, axis=1)


def _w_in_scratch():
    return [pltpu.VMEM((D_MODEL, D_IN), BF16), pltpu.SemaphoreType.DMA((N_DEV,))]


def _stage_w_in(w_hbm, w_scr, sems):
    @pl.when(pl.program_id(0) == 0)
    def _():
        copies = [pltpu.make_async_copy(w_hbm.at[k], w_scr.at[:, pl.ds(SHARD_IN * k, SHARD_IN)], sems.at[k])
                  for k in range(N_DEV)]
        for cp in copies:
            cp.start()
        for cp in copies:
            cp.wait()


def _inproj(u, w_in_full, tabs, w_mem, w_out, tm=1024):
    seq = u.shape[0]
    n_chunk = D_IN // LANES
    n_steps = seq // tm

    def body(u_ref, w_hbm, c_ref, up_ref, dn_ref, wm_ref, wo_ref, qa_ref, ka_ref, va_ref,
             qb1_ref, kb1_ref, vb1_ref, qb4_ref, kb4_ref, vb4_ref, qb16_ref, kb16_ref, vb16_ref,
             qc_ref, gate_ref, wm_all, wo_all, proj, w_scr, w_sems, wm_b, wo_b, send_sems, recv_sems, local_sems):
        step = pl.program_id(0)
        shards, gathered = (wm_b, wo_b), (wm_all, wo_all)

        def gather_copies(arriving):
            pos = _mesh_pos()
            me = _dev_index(pos)
            local = [] if arriving else [
                pltpu.make_async_copy(shards[a], gathered[a].at[me], local_sems.at[a]) for a in range(2)]
            remote = []
            for s in range(1, N_DEV):
                peer = _xor_peer(pos, s)
                for a in range(2):
                    remote.append(pltpu.make_async_remote_copy(
                        src_ref=shards[a], dst_ref=gathered[a].at[_dev_index(peer) if arriving else me],
                        send_sem=send_sems.at[a, s], recv_sem=recv_sems.at[a, s], device_id=peer,
                        device_id_type=MESH_ID))
            return local, remote

        @pl.when(step == 0)
        def _():
            wm_b[...] = wm_ref[...].astype(BF16)
            wo_b[...] = wo_ref[...].astype(BF16)
            local, sends = gather_copies(arriving=False)
            for cp in local + sends:
                cp.start()

        _stage_w_in(w_hbm, w_scr, w_sems)
        u = u_ref[...]
        for n0 in range(0, D_IN, D_MODEL):
            acc = jnp.dot(u, w_scr[:, n0:n0 + D_MODEL], preferred_element_type=F32)
            for c3 in range(D_MODEL // LANES):
                proj[n0 // LANES + c3] = acc[:, LANES * c3:LANES * (c3 + 1)]
        c, up, dn = c_ref[...], up_ref[...], dn_ref[...]

        def cols(lo, hi, rot=False, scale=None):
            parts = []
            for ch in range(lo // LANES, hi // LANES):
                t = proj[ch]
                if rot:
                    t = _rotate128(t, c, up, dn)
                if scale is not None:
                    t = t * scale
                parts.append(t)
            return parts[0] if len(parts) == 1 else jnp.concatenate(parts, axis=1)

        qa_ref[...] = cols(0, 384, True, SCALE).astype(BF16)
        ka_ref[...] = cols(384, 512, True).astype(BF16)
        va_ref[...] = cols(512, 640).astype(BF16)
        gate_ref[:, 0:384] = cols(640, 1024).astype(BF16)
        gate_ref[:, 384:768] = cols(2176, 2560).astype(BF16)
        gate_ref[:, 768:1024] = cols(2816, 3072).astype(BF16)
        qc_ref[...] = cols(2560, 2816, False, SCALE).astype(BF16)
        for ch in range(1024 // LANES, 1408 // LANES):
            proj[ch] = _rotate128(proj[ch], c, up, dn) * SCALE
        for ch in range(1408 // LANES, 1792 // LANES):
            proj[ch] = _rotate128(proj[ch], c, up, dn)
        for lo, nat, p4, p16 in ((1024, qb1_ref, qb4_ref, qb16_ref), (1408, kb1_ref, kb4_ref, kb16_ref),
                                 (1792, vb1_ref, vb4_ref, vb16_ref)):
            chunks = range(lo // LANES, lo // LANES + B_W // LANES)
            nat[...] = jnp.concatenate([proj[ch] for ch in chunks], axis=1).astype(BF16)
            for dil, ref in ((4, p4), (16, p16)):
                span = min(tm, BLOCK * dil)
                for cc in range(tm // span):
                    for rr in range(dil):
                        ref[cc, rr] = jnp.concatenate(
                            [proj.at[ch][pl.ds(cc * span + rr, span // dil, stride=dil), :] for ch in chunks],
                            axis=1).astype(BF16)

        @pl.when(step == n_steps - 1)
        def _():
            for cp in gather_copies(arriving=True)[1]:
                cp.wait_recv()
            local, sends = gather_copies(arriving=False)
            for cp in sends:
                cp.wait_send()
            for cp in local:
                cp.wait()

    nat_w = (A_W, A_KV_W, A_KV_W, B_W, B_W, B_W)
    out_specs = [_row(tm, w) for w in nat_w]
    out_shape = [jax.ShapeDtypeStruct((seq, w), BF16) for w in nat_w]
    for dil in DILS:
        out_specs += [_perm_spec(tm, dil, B_W)] * 3
        out_shape += [jax.ShapeDtypeStruct((seq // (BLOCK * dil), dil, BLOCK, B_W), BF16)] * 3
    hbm = pl.BlockSpec(memory_space=pl.ANY)
    out_specs += [_row(tm, C_W), _row(tm, D_MODEL), hbm, hbm]
    out_shape += [jax.ShapeDtypeStruct((seq, C_W), BF16), jax.ShapeDtypeStruct((seq, D_MODEL), BF16),
                  jax.ShapeDtypeStruct((N_DEV,) + w_mem.shape, BF16), jax.ShapeDtypeStruct((N_DEV,) + w_out.shape, BF16)]
    res = pl.pallas_call(
        body, name="inproj", grid=(n_steps,),
        in_specs=[_row(tm, D_MODEL), hbm, _row(tm, LANES), _row(tm, LANES), _row(tm, LANES),
                  _full(w_mem.shape), _full(w_out.shape)],
        out_specs=out_specs, out_shape=out_shape,
        scratch_shapes=[pltpu.VMEM((n_chunk, tm, LANES), F32)] + _w_in_scratch()
        + [pltpu.VMEM(w_mem.shape, BF16), pltpu.VMEM(w_out.shape, BF16), pltpu.SemaphoreType.DMA((2, N_DEV)),
           pltpu.SemaphoreType.DMA((2, N_DEV)), pltpu.SemaphoreType.DMA((2,))],
        compiler_params=_params(dimension_semantics=("arbitrary",)),
    )(u, w_in_full, *tabs, w_mem, w_out)
    qa, ka, va = res[0:3]
    qkv_b = {1: res[3:6], 4: [t.reshape(seq, B_W) for t in res[6:9]], 16: [t.reshape(seq, B_W) for t in res[9:12]]}
    return qa, ka, va, qkv_b, res[12], res[13], res[14], res[15]


def _memkv_fwd(mem, mem_g, w_mem_full):
    def body(mem_ref, g_ref, w_ref, mn_ref, mk_ref, mv_ref):
        mv_ = mem_ref[...]
        r = lax.rsqrt(jnp.mean(mv_ * mv_, axis=-1, keepdims=True) + RMS_EPS)
        mn = ((mv_ * r) * g_ref[...]).astype(BF16)
        mn_ref[...] = mn
        mkv = jnp.dot(mn, w_ref[...], preferred_element_type=F32)
        mk_ref[...] = mkv[:, 0:C_W].astype(BF16)
        mv_ref[...] = mkv[:, C_W:2 * C_W].astype(BF16)

    return pl.pallas_call(
        body, name="memkv_fwd",
        out_shape=[jax.ShapeDtypeStruct((N_MEM, D_MODEL), BF16),
                   jax.ShapeDtypeStruct((N_MEM, C_W), BF16), jax.ShapeDtypeStruct((N_MEM, C_W), BF16)],
        compiler_params=_params(),
    )(mem, mem_g, w_mem_full)


def _memkv_bwd(mem, mem_g, mn, w_mem_full, dmk, dmv):
    def body(mem_ref, g_ref, mn_ref, w_ref, dmk_ref, dmv_ref, dw_ref, st_ref):
        dmkv = jnp.concatenate([dmk_ref[...], dmv_ref[...]], axis=1).astype(BF16)
        dw_ref[...] = lax.dot_general(mn_ref[...], dmkv, (((0,), (0,)), ((), ())), preferred_element_type=F32)
        dmn = lax.dot_general(dmkv, w_ref[...], (((1,), (1,)), ((), ())), preferred_element_type=F32)
        mv_ = mem_ref[...]
        r = lax.rsqrt(jnp.mean(mv_ * mv_, axis=-1, keepdims=True) + RMS_EPS)
        st_ref[...] = jnp.zeros_like(st_ref)
        st_ref[0:1, :] = jnp.sum(dmn * (mv_ * r), axis=0, keepdims=True)

    return pl.pallas_call(
        body, name="memkv_bwd",
        out_shape=[jax.ShapeDtypeStruct((D_MODEL, 2 * C_W), F32), jax.ShapeDtypeStruct((8, D_MODEL), F32)],
        compiler_params=_params(),
    )(mem, mem_g, mn, w_mem_full, dmk, dmv)


def _band_mask(has_prev, max_dist):
    qi = lax.broadcasted_iota(jnp.int32, (BLOCK, 2 * BLOCK), 0)
    kj = lax.broadcasted_iota(jnp.int32, (BLOCK, 2 * BLOCK), 1)
    dist = qi + BLOCK - kj
    return (dist >= 0) & (dist <= max_dist) & ((kj >= BLOCK) | has_prev)


_NT = (((1,), (1,)), ((), ()))
_TN = (((0,), (0,)), ((), ()))


def _head_only(val, h):
    slab = val[:, LANES * (h // 2):LANES * (h // 2 + 1)]
    lane = lax.broadcasted_iota(jnp.int32, slab.shape, 1)
    keep = (lane < HEAD_DIM) if h % 2 == 0 else (lane >= HEAD_DIM)
    return jnp.where(keep, slab, jnp.zeros((), slab.dtype))


class _KvSlabs:
    def __init__(self, cat, group):
        self.cat, self.group, self.swapped = cat, group, {}

    def is_swapped(self, h):
        return (h // self.group) % 2 != h % 2

    def __call__(self, h):
        j = (h // self.group) // 2
        slab = self.cat[:, LANES * j:LANES * (j + 1)]
        if not self.is_swapped(h):
            return slab
        if j not in self.swapped:
            self.swapped[j] = jnp.concatenate([slab[:, HEAD_DIM:], slab[:, :HEAD_DIM]], axis=1)
        return self.swapped[j]


class _BandSteps:
    def __init__(self, seq, dil, nq):
        self.nq, self.rows, self.consecutive = nq, nq * BLOCK, dil == 1
        nb = seq // dil // BLOCK
        if self.consecutive:
            assert nb % nq == 0
            self.outer, self.inner, self.stride = 1, nb // nq, 1
        else:
            assert dil % nq == 0
            self.outer, self.inner, self.stride = dil // nq, nb, dil // nq

    def own(self, w, clamp=False):
        cur = (lambda i: jnp.minimum(i, self.inner - 1)) if clamp else (lambda i: i)
        return pl.BlockSpec((self.rows, w), lambda r, i: (cur(i) * self.stride + r, 0))

    def prev(self, w, clamp=False):
        cur = (lambda i: jnp.minimum(i, self.inner - 1)) if clamp else (lambda i: i)
        if self.consecutive:
            return pl.BlockSpec((BLOCK, w), lambda r, i: (jnp.maximum(cur(i) * self.nq - 1, 0), 0))
        return pl.BlockSpec((self.rows, w), lambda r, i: (jnp.maximum(cur(i) - 1, 0) * self.stride + r, 0))

    def late(self, w):
        return pl.BlockSpec((self.rows, w), lambda r, i: (jnp.maximum(i - 1, 0) * self.stride + r, 0))

    def rows_of(self, j):
        return slice(BLOCK * j, BLOCK * (j + 1))

    def keys(self, p_ref, c_ref, j):
        if not self.consecutive:
            before = p_ref[self.rows_of(j), :]
        elif j == 0:
            before = p_ref[...]
        else:
            before = c_ref[self.rows_of(j - 1), :]
        return jnp.concatenate([before, c_ref[self.rows_of(j), :]], axis=0)

    def has_prev(self, i, j):
        return True if (self.consecutive and j > 0) else (i > 0)


def _banded_fwd(q, k, v, sink, *, dil, heads, group, max_dist, nq, name):
    seq = q.shape[0]
    kvh = heads // group
    qw, kw = heads * HEAD_DIM, kvh * HEAD_DIM
    steps = _BandSteps(seq, dil, nq)

    def body(*refs):
        if sink is not None:
            sink_ref, refs = refs[0], refs[1:]
        q_ref, kp_ref, kc_ref, vp_ref, vc_ref, o_ref, lse_ref, s_scr, p_scr = refs
        i = pl.program_id(1)
        lane = lax.broadcasted_iota(jnp.int32, (BLOCK, LANES), 1)
        k_of = [_KvSlabs(steps.keys(kp_ref, kc_ref, j), group) for j in range(nq)]
        v_of = [_KvSlabs(steps.keys(vp_ref, vc_ref, j), group) for j in range(nq)]
        for j in range(nq):
            qv = q_ref[steps.rows_of(j), :]
            for h in range(heads):
                s_scr[j * heads + h] = lax.dot_general(_head_only(qv, h), k_of[j](h), _NT, preferred_element_type=F32)
        ls = {}
        for j in range(nq):
            valid = _band_mask(steps.has_prev(i, j), max_dist)
            lse_tile = jnp.zeros((BLOCK, LANES), F32)
            for h in range(heads):
                s = jnp.where(valid, s_scr[j * heads + h], NEG)
                m = jnp.max(s, axis=-1, keepdims=True)
                if sink is not None:
                    sk = sink_ref[h]
                    m = jnp.maximum(m, sk)
                p = jnp.exp(s - m)
                l = jnp.sum(p, axis=-1, keepdims=True)
                if sink is not None:
                    l = l + jnp.exp(sk - m)
                p_scr[j * heads + h] = p.astype(BF16)
                ls[j, h] = l
                lse_tile = jnp.where(lane == h, m + jnp.log(l), lse_tile)
            lse_ref[steps.rows_of(j), :] = lse_tile
        for j in range(nq):
            for pr in range(heads // 2):
                he, ho = 2 * pr, 2 * pr + 1
                even = jnp.dot(p_scr[j * heads + he], v_of[j](he), preferred_element_type=F32) / ls[j, he]
                odd = jnp.dot(p_scr[j * heads + ho], v_of[j](ho), preferred_element_type=F32) / ls[j, ho]
                o_ref[steps.rows_of(j), LANES * pr:LANES * (pr + 1)] = jnp.where(lane < HEAD_DIM, even, odd).astype(BF16)

    in_specs = [steps.own(qw), steps.prev(kw), steps.own(kw), steps.prev(kw), steps.own(kw)]
    args = [q, k, k, v, v]
    if sink is not None:
        in_specs = [pl.BlockSpec(memory_space=pltpu.SMEM)] + in_specs
        args = [sink] + args
    return pl.pallas_call(
        body, name=name, grid=(steps.outer, steps.inner), in_specs=in_specs,
        out_specs=[steps.own(qw), steps.own(LANES)],
        out_shape=[jax.ShapeDtypeStruct((seq, qw), BF16), jax.ShapeDtypeStruct((seq, LANES), F32)],
        scratch_shapes=[pltpu.VMEM((nq * heads, BLOCK, 2 * BLOCK), F32), pltpu.VMEM((nq * heads, BLOCK, 2 * BLOCK), BF16)],
        compiler_params=_params(dimension_semantics=("arbitrary", "arbitrary")),
    )(*args)


def _banded_bwd(q, k, v, d_out, stat, sink, *, dil, heads, group, max_dist, nq, name):
    seq = q.shape[0]
    kvh = heads // group
    qw, kw = heads * HEAD_DIM, kvh * HEAD_DIM
    steps = _BandSteps(seq, dil, nq)
    n_in = 7

    def body(*refs):
        if sink is not None:
            sink_ref, refs = refs[0], refs[1:]
            dsink_ref, refs = refs[n_in], refs[:n_in] + refs[n_in + 1:]
        (q_ref, kp_ref, kc_ref, vp_ref, vc_ref, do_ref, st_ref, dq_ref, dk_ref, dv_ref,
         kcar, vcar, s_scr, dp_scr, p_scr, ds_scr) = refs
        r, i = pl.program_id(0), pl.program_id(1)

        @pl.when(i == 0)
        def _():
            kcar[...] = jnp.zeros_like(kcar)
            vcar[...] = jnp.zeros_like(vcar)

        if sink is not None:
            @pl.when((i == 0) & (r == 0))
            def _():
                dsink_ref[...] = jnp.zeros_like(dsink_ref)

        @pl.when(i < steps.inner)
        def _():
            lane = lax.broadcasted_iota(jnp.int32, (1, LANES), 1)
            lane_q = lax.broadcasted_iota(jnp.int32, (BLOCK, LANES), 1)
            k_of = [_KvSlabs(steps.keys(kp_ref, kc_ref, j), group) for j in range(nq)]
            v_of = [_KvSlabs(steps.keys(vp_ref, vc_ref, j), group) for j in range(nq)]
            qms, doms = {}, {}
            for j in range(nq):
                qv, dov = q_ref[steps.rows_of(j), :], do_ref[steps.rows_of(j), :]
                for h in range(heads):
                    qms[j, h], doms[j, h] = _head_only(qv, h), _head_only(dov, h)
                    s_scr[j * heads + h] = lax.dot_general(qms[j, h], k_of[j](h), _NT, preferred_element_type=F32)
                    dp_scr[j * heads + h] = lax.dot_general(doms[j, h], v_of[j](h), _NT, preferred_element_type=F32)
            dsink_row = jnp.zeros((1, LANES), F32)
            for j in range(nq):
                st = st_ref[steps.rows_of(j), :]
                valid = _band_mask(steps.has_prev(i, j), max_dist)
                for h in range(heads):
                    lse_h = st[:, h:h + 1]
                    delta = st[:, DELTA_LANE + h:DELTA_LANE + h + 1]
                    p = jnp.where(valid, jnp.exp(s_scr[j * heads + h] - lse_h), 0.0)
                    p_scr[j * heads + h] = p.astype(BF16)
                    ds_scr[j * heads + h] = (p * (dp_scr[j * heads + h] - delta)).astype(BF16)
                    if sink is not None:
                        ds_sink = jnp.sum(-jnp.exp(sink_ref[h] - lse_h) * delta, axis=0, keepdims=True)
                        dsink_row = dsink_row + jnp.where(lane == h, ds_sink, 0.0)
            for j in range(nq):
                for pr in range(heads // 2):
                    he, ho = 2 * pr, 2 * pr + 1
                    even = jnp.dot(ds_scr[j * heads + he], k_of[j](he), preferred_element_type=F32)
                    odd = jnp.dot(ds_scr[j * heads + ho], k_of[j](ho), preferred_element_type=F32)
                    dq_ref[steps.rows_of(j), LANES * pr:LANES * (pr + 1)] = (
                        jnp.where(lane_q < HEAD_DIM, even, odd).astype(BF16))
            if steps.consecutive:
                dk_ref[...] = kcar[...].astype(BF16)
                dv_ref[...] = vcar[...].astype(BF16)
            for j in range(nq):
                for slab in range(kw // LANES):
                    acc = {}
                    for h in range(heads):
                        if (h // group) // 2 != slab:
                            continue
                        key = k_of[j].is_swapped(h)
                        dk_h = lax.dot_general(ds_scr[j * heads + h], qms[j, h], _TN, preferred_element_type=F32)
                        dv_h = lax.dot_general(p_scr[j * heads + h], doms[j, h], _TN, preferred_element_type=F32)
                        acc[key] = (dk_h, dv_h) if key not in acc else (acc[key][0] + dk_h, acc[key][1] + dv_h)
                    dk_j, dv_j = acc.get(False, (None, None))
                    if True in acc:
                        unswap = lambda t: jnp.concatenate([t[:, HEAD_DIM:], t[:, :HEAD_DIM]], axis=1)
                        dk_s, dv_s = unswap(acc[True][0]), unswap(acc[True][1])
                        dk_j = dk_s if dk_j is None else dk_j + dk_s
                        dv_j = dv_s if dv_j is None else dv_j + dv_s
                    sl = slice(LANES * slab, LANES * (slab + 1))
                    own_rows = steps.rows_of(j)
                    if not steps.consecutive:
                        dk_ref[own_rows, sl] = (kcar[own_rows, sl] + dk_j[0:BLOCK]).astype(BF16)
                        dv_ref[own_rows, sl] = (vcar[own_rows, sl] + dv_j[0:BLOCK]).astype(BF16)
                    elif j == 0:
                        last = steps.rows_of(nq - 1)
                        dk_ref[last, sl] = (kcar[last, sl] + dk_j[0:BLOCK]).astype(BF16)
                        dv_ref[last, sl] = (vcar[last, sl] + dv_j[0:BLOCK]).astype(BF16)
                    else:
                        before = steps.rows_of(j - 1)
                        kcar[before, sl] += dk_j[0:BLOCK]
                        vcar[before, sl] += dv_j[0:BLOCK]
                    kcar[own_rows, sl] = dk_j[BLOCK:2 * BLOCK]
                    vcar[own_rows, sl] = dv_j[BLOCK:2 * BLOCK]
            if sink is not None:
                dsink_ref[0:1, :] += dsink_row

        @pl.when(i == steps.inner)
        def _():
            dk_ref[...] = kcar[...].astype(BF16)
            dv_ref[...] = vcar[...].astype(BF16)

    own, prev = (lambda w: steps.own(w, clamp=True)), (lambda w: steps.prev(w, clamp=True))
    in_specs = [own(qw), prev(kw), own(kw), prev(kw), own(kw), own(qw), own(LANES)]
    args = [q, k, k, v, v, d_out, stat]
    out_specs = [own(qw), steps.late(kw), steps.late(kw)]
    out_shape = [jax.ShapeDtypeStruct((seq, qw), BF16), jax.ShapeDtypeStruct((seq, kw), BF16),
                 jax.ShapeDtypeStruct((seq, kw), BF16)]
    if sink is not None:
        in_specs = [pl.BlockSpec(memory_space=pltpu.SMEM)] + in_specs
        args = [sink] + args
        out_specs = [_full((8, LANES))] + out_specs
        out_shape = [jax.ShapeDtypeStruct((8, LANES), F32)] + out_shape
    n_hb = nq * heads
    res = pl.pallas_call(
        body, name=name, grid=(steps.outer, steps.inner + 1), in_specs=in_specs, out_specs=out_specs,
        out_shape=out_shape,
        scratch_shapes=[pltpu.VMEM((steps.rows, kw), F32), pltpu.VMEM((steps.rows, kw), F32)]
        + [pltpu.VMEM((n_hb, BLOCK, 2 * BLOCK), F32)] * 2 + [pltpu.VMEM((n_hb, BLOCK, 2 * BLOCK), BF16)] * 2,
        compiler_params=_params(dimension_semantics=("arbitrary", "arbitrary")),
    )(*args)
    if sink is not None:
        return res[1], res[2], res[3], res[0]
    return res


def _cross_fwd(q, mk, mv, tq=512):
    seq = q.shape[0]

    def body(q_ref, mk_ref, mv_ref, o_ref, lse_ref, s_scr, p_scr):
        qv = q_ref[...]
        k_of, v_of = _KvSlabs(mk_ref[...], 1), _KvSlabs(mv_ref[...], 1)
        lane = lax.broadcasted_iota(jnp.int32, (tq, LANES), 1)
        lse_tile = jnp.zeros((tq, LANES), F32)
        for h in range(C_HEADS):
            s_scr[h] = lax.dot_general(_head_only(qv, h), k_of(h), _NT, preferred_element_type=F32)
        ls = []
        for h in range(C_HEADS):
            s = s_scr[h]
            m = jnp.max(s, axis=-1, keepdims=True)
            p = jnp.exp(s - m)
            l = jnp.sum(p, axis=-1, keepdims=True)
            p_scr[h] = p.astype(BF16)
            ls.append(l)
            lse_tile = jnp.where(lane == h, m + jnp.log(l), lse_tile)
        for pr in range(C_HEADS // 2):
            even = jnp.dot(p_scr[2 * pr], v_of(2 * pr), preferred_element_type=F32) / ls[2 * pr]
            odd = jnp.dot(p_scr[2 * pr + 1], v_of(2 * pr + 1), preferred_element_type=F32) / ls[2 * pr + 1]
            o_ref[:, LANES * pr:LANES * (pr + 1)] = jnp.where(lane < HEAD_DIM, even, odd).astype(BF16)
        lse_ref[...] = lse_tile

    return pl.pallas_call(
        body, name="cross_fwd", grid=(seq // tq,),
        in_specs=[_row(tq, C_W), _full((N_MEM, C_W)), _full((N_MEM, C_W))],
        out_specs=[_row(tq, C_W), _row(tq, LANES)],
        out_shape=[jax.ShapeDtypeStruct((seq, C_W), BF16), jax.ShapeDtypeStruct((seq, LANES), F32)],
        scratch_shapes=[pltpu.VMEM((C_HEADS, tq, N_MEM), F32), pltpu.VMEM((C_HEADS, tq, N_MEM), BF16)],
        compiler_params=_params(dimension_semantics=("arbitrary",)),
    )(q, mk, mv)


def _cross_bwd(q, mk, mv, d_out, stat, tq=512):
    seq = q.shape[0]

    def body(q_ref, mk_ref, mv_ref, do_ref, st_ref, dq_ref, dmk_ref, dmv_ref, s_scr, dp_scr, p_scr, ds_scr):
        @pl.when(pl.program_id(0) == 0)
        def _():
            dmk_ref[...] = jnp.zeros_like(dmk_ref)
            dmv_ref[...] = jnp.zeros_like(dmv_ref)

        qv, dov, st = q_ref[...], do_ref[...], st_ref[...]
        k_of, v_of = _KvSlabs(mk_ref[...], 1), _KvSlabs(mv_ref[...], 1)
        qms = [_head_only(qv, h) for h in range(C_HEADS)]
        doms = [_head_only(dov, h) for h in range(C_HEADS)]
        for h in range(C_HEADS):
            s_scr[h] = lax.dot_general(qms[h], k_of(h), _NT, preferred_element_type=F32)
            dp_scr[h] = lax.dot_general(doms[h], v_of(h), _NT, preferred_element_type=F32)
        for h in range(C_HEADS):
            p = jnp.exp(s_scr[h] - st[:, h:h + 1])
            p_scr[h] = p.astype(BF16)
            ds_scr[h] = (p * (dp_scr[h] - st[:, DELTA_LANE + h:DELTA_LANE + h + 1])).astype(BF16)
        lane = lax.broadcasted_iota(jnp.int32, (tq, LANES), 1)
        for pr in range(C_HEADS // 2):
            sl = slice(LANES * pr, LANES * (pr + 1))
            even = jnp.dot(ds_scr[2 * pr], k_of(2 * pr), preferred_element_type=F32)
            odd = jnp.dot(ds_scr[2 * pr + 1], k_of(2 * pr + 1), preferred_element_type=F32)
            dq_ref[:, sl] = jnp.where(lane < HEAD_DIM, even, odd).astype(BF16)
            dmk_ref[:, sl] += (lax.dot_general(ds_scr[2 * pr], qms[2 * pr], _TN, preferred_element_type=F32)
                               + lax.dot_general(ds_scr[2 * pr + 1], qms[2 * pr + 1], _TN, preferred_element_type=F32))
            dmv_ref[:, sl] += (lax.dot_general(p_scr[2 * pr], doms[2 * pr], _TN, preferred_element_type=F32)
                               + lax.dot_general(p_scr[2 * pr + 1], doms[2 * pr + 1], _TN, preferred_element_type=F32))

    return pl.pallas_call(
        body, name="cross_bwd", grid=(seq // tq,),
        in_specs=[_row(tq, C_W), _full((N_MEM, C_W)), _full((N_MEM, C_W)), _row(tq, C_W), _row(tq, LANES)],
        out_specs=[_row(tq, C_W), _full((N_MEM, C_W)), _full((N_MEM, C_W))],
        out_shape=[jax.ShapeDtypeStruct((seq, C_W), BF16), jax.ShapeDtypeStruct((N_MEM, C_W), F32),
                   jax.ShapeDtypeStruct((N_MEM, C_W), F32)],
        scratch_shapes=[pltpu.VMEM((C_HEADS, tq, N_MEM), F32)] * 2 + [pltpu.VMEM((C_HEADS, tq, N_MEM), BF16)] * 2,
        compiler_params=_params(dimension_semantics=("arbitrary",)),
    )(q, mk, mv, d_out, stat)


def _per_head(tile, width):
    rows = tile.shape[0]
    return jnp.concatenate(
        [jnp.broadcast_to(tile[:, h:h + 1], (rows, HEAD_DIM)) for h in range(width // HEAD_DIM)], axis=1)


def _with_delta(lse_tile, prod):
    rows = lse_tile.shape[0]
    lane = lax.broadcasted_iota(jnp.int32, (rows, LANES), 1)
    tile = lse_tile
    for h in range(prod.shape[1] // HEAD_DIM):
        d = jnp.sum(prod[:, HEAD_DIM * h:HEAD_DIM * (h + 1)], axis=-1, keepdims=True)
        tile = jnp.where(lane == DELTA_LANE + h, d, tile)
    return tile


def _mid(oa, lse_a, ob, lse_b, oc, lse_c, gate, x, target, w_out_full, post_g, tm=512):
    seq = x.shape[0]
    n_b = B_W // LANES

    def body(oa_ref, la_ref, b1_ref, l1_ref, b4_ref, l4_ref, b16_ref, l16_ref, oc_ref, lc_ref,
             gate_ref, x_ref, t_ref, w_ref, pg_ref,
             dh_ref, dg_ref, doa_ref, sa_ref, dob1_ref, sb1_ref, dob4_ref, sb4_ref, dob16_ref, sb16_ref,
             doc_ref, sc_ref, dw_ref, st_ref, scr_b4, scr_b16, scr_l4, scr_l16, scr_do, scr_sb):
        @pl.when(pl.program_id(0) == 0)
        def _():
            dw_ref[...] = jnp.zeros_like(dw_ref)
            st_ref[...] = jnp.zeros_like(st_ref)

        b1, l1 = b1_ref[...].astype(F32), l1_ref[...]
        b4, l4 = _load_permuted(b4_ref, scr_b4, 4), _load_permuted(l4_ref, scr_l4, 4)
        b16, l16 = _load_permuted(b16_ref, scr_b16, 16), _load_permuted(l16_ref, scr_l16, 16)
        lm = jnp.maximum(jnp.maximum(l1, l4), l16)
        e1, e4, e16 = jnp.exp(l1 - lm), jnp.exp(l4 - lm), jnp.exp(l16 - lm)
        den = e1 + e4 + e16
        lse_b_tile = lm + jnp.log(den)
        ob_v = _per_head(e1 / den, B_W) * b1 + _per_head(e4 / den, B_W) * b4 + _per_head(e16 / den, B_W) * b16
        o_all = jnp.concatenate([oa_ref[...].astype(F32), ob_v, oc_ref[...].astype(F32)], axis=1)
        g = gate_ref[...].astype(F32)
        sig = 1.0 / (1.0 + jnp.exp(-g))
        silu = g * sig
        y = (o_all * silu).astype(BF16)
        w = w_ref[...]
        z = jnp.dot(y, w, preferred_element_type=F32)
        rz = lax.rsqrt(jnp.mean(z * z, axis=-1, keepdims=True) + RMS_EPS)
        hn = z * rz
        pg = pg_ref[...]
        err = (x_ref[...] + hn * pg) - t_ref[...]
        loss = 0.5 * jnp.sum(jnp.mean(err * err, axis=-1, keepdims=True), axis=0, keepdims=True)
        dh = err * (1.0 / D_MODEL)
        dh_ref[...] = dh.astype(BF16)
        st_ref[0:1, :] += jnp.sum(dh * hn, axis=0, keepdims=True)
        st_ref[1:2, :] += jnp.broadcast_to(loss, (1, D_MODEL))
        dhn = dh * pg
        dz = (rz * (dhn - hn * jnp.mean(dhn * hn, axis=-1, keepdims=True))).astype(BF16)
        dy = lax.dot_general(dz, w, _NT, preferred_element_type=F32)
        dw_ref[...] += lax.dot_general(y, dz, _TN, preferred_element_type=F32)
        dg_ref[...] = (dy * o_all * (sig * (1.0 + g * (1.0 - sig)))).astype(BF16)
        d_o = (dy * silu).astype(BF16)
        prod = d_o.astype(F32) * o_all
        doa_ref[...] = d_o[:, 0:A_W]
        sa_ref[...] = _with_delta(la_ref[...], prod[:, 0:A_W])
        doc_ref[...] = d_o[:, A_W + B_W:D_MODEL]
        sc_ref[...] = _with_delta(lc_ref[...], prod[:, A_W + B_W:D_MODEL])
        d_ob = d_o[:, A_W:A_W + B_W]
        stat_b = _with_delta(lse_b_tile, prod[:, A_W:A_W + B_W])
        dob1_ref[...] = d_ob
        sb1_ref[...] = stat_b
        _put(scr_do, d_ob.astype(F32))
        _put(scr_sb, stat_b)
        _store_permuted(scr_do, dob4_ref, 4, BF16)
        _store_permuted(scr_sb, sb4_ref, 4, F32)
        _store_permuted(scr_do, dob16_ref, 16, BF16)
        _store_permuted(scr_sb, sb16_ref, 16, F32)

    p4 = lambda w: _perm_spec(tm, 4, w)
    p16 = lambda w: _perm_spec(tm, 16, w)
    in_specs = [_row(tm, A_W), _row(tm, LANES), _row(tm, B_W), _row(tm, LANES), p4(B_W), p4(LANES), p16(B_W), p16(LANES),
                _row(tm, C_W), _row(tm, LANES), _row(tm, D_MODEL), _row(tm, D_MODEL), _row(tm, D_MODEL),
                _full((D_MODEL, D_MODEL)), _full((1, D_MODEL))]
    sds = jax.ShapeDtypeStruct
    v4 = lambda w, dt: sds((seq // (BLOCK * 4), 4, BLOCK, w), dt)
    v16 = lambda w, dt: sds((seq // (BLOCK * 16), 16, BLOCK, w), dt)
    out_specs = [_row(tm, D_MODEL), _row(tm, D_MODEL), _row(tm, A_W), _row(tm, LANES), _row(tm, B_W), _row(tm, LANES),
                 p4(B_W), p4(LANES), p16(B_W), p16(LANES), _row(tm, C_W), _row(tm, LANES),
                 _full((D_MODEL, D_MODEL)), _full((8, D_MODEL))]
    out_shape = [sds((seq, D_MODEL), BF16), sds((seq, D_MODEL), BF16), sds((seq, A_W), BF16), sds((seq, LANES), F32),
                 sds((seq, B_W), BF16), sds((seq, LANES), F32), v4(B_W, BF16), v4(LANES, F32), v16(B_W, BF16),
                 v16(LANES, F32), sds((seq, C_W), BF16), sds((seq, LANES), F32),
                 sds((D_MODEL, D_MODEL), F32), sds((8, D_MODEL), F32)]
    res = pl.pallas_call(
        body, name="mid", grid=(seq // tm,), in_specs=in_specs, out_specs=out_specs, out_shape=out_shape,
        scratch_shapes=[pltpu.VMEM((n_b, tm, LANES), F32), pltpu.VMEM((n_b, tm, LANES), F32),
                        pltpu.VMEM((1, tm, LANES), F32), pltpu.VMEM((1, tm, LANES), F32),
                        pltpu.VMEM((n_b, tm, LANES), F32), pltpu.VMEM((1, tm, LANES), F32)],
        compiler_params=_params(dimension_semantics=("arbitrary",)),
    )(oa, lse_a, ob[1], lse_b[1], _perm_view(ob[4], 4), _perm_view(lse_b[4], 4), _perm_view(ob[16], 16),
      _perm_view(lse_b[16], 16), oc, lse_c, gate, x, target, w_out_full, post_g)
    dh, d_gate, do_a, st_a, do_b1, st_b1, do_b4, st_b4, do_b16, st_b16, do_c, st_c, d_wout, stats = res
    flat = lambda t: t.reshape(seq, t.shape[-1])
    d_b = {1: (do_b1, st_b1), 4: (flat(do_b4), flat(st_b4)), 16: (flat(do_b16), flat(st_b16))}
    return dh, d_gate, (do_a, st_a), d_b, (do_c, st_c), d_wout, stats


def _inproj_bwd_x(x, dh, pre_g, w_in_full, tabs, dqa, dka, dva, dqkv_b, dqc, dgate, tm=512):
    seq = x.shape[0]
    n_b = B_W // LANES

    def body(x_ref, dh_ref, g_ref, w_hbm, c_ref, up_ref, dn_ref, dqa_ref, dka_ref, dva_ref,
             dq1, dk1, dv1, dq4, dk4, dv4, dq16, dk16, dv16, dqc_ref, dg_ref,
             gx_ref, dp_ref, st_ref, scr4, scr16, w_scr, w_sems):
        _stage_w_in(w_hbm, w_scr, w_sems)

        @pl.when(pl.program_id(0) == 0)
        def _():
            st_ref[...] = jnp.zeros_like(st_ref)

        c, up, dn = c_ref[...], -up_ref[...], -dn_ref[...]
        unrot = lambda t: _rotate(t, c, up, dn)
        total = lambda r1, r4, r16: (r1[...].astype(F32) + _load_permuted(r4, scr4, 4)
                                     + _load_permuted(r16, scr16, 16))
        dp_ref[:, 0:384] = (unrot(dqa_ref[...].astype(F32)) * SCALE).astype(BF16)
        dp_ref[:, 384:512] = unrot(dka_ref[...].astype(F32)).astype(BF16)
        dp_ref[:, 512:640] = dva_ref[...]
        dp_ref[:, 640:1024] = dg_ref[:, 0:384]
        dp_ref[:, 1024:1408] = (unrot(total(dq1, dq4, dq16)) * SCALE).astype(BF16)
        dp_ref[:, 1408:1792] = unrot(total(dk1, dk4, dk16)).astype(BF16)
        dp_ref[:, 1792:2176] = total(dv1, dv4, dv16).astype(BF16)
        dp_ref[:, 2176:2560] = dg_ref[:, 384:768]
        dp_ref[:, 2560:2816] = (dqc_ref[...].astype(F32) * SCALE).astype(BF16)
        dp_ref[:, 2816:3072] = dg_ref[:, 768:1024]
        du = lax.dot_general(dp_ref[...], w_scr[...], _NT, preferred_element_type=F32)
        xv = x_ref[...]
        r = lax.rsqrt(jnp.mean(xv * xv, axis=-1, keepdims=True) + RMS_EPS)
        xh = xv * r
        st_ref[0:1, :] += jnp.sum(du * xh, axis=0, keepdims=True)
        dxh = du * g_ref[...]
        gx_ref[...] = dh_ref[...].astype(F32) + r * (dxh - xh * jnp.mean(dxh * xh, axis=-1, keepdims=True))

    in_specs = ([_row(tm, D_MODEL), _row(tm, D_MODEL), _full((1, D_MODEL)), pl.BlockSpec(memory_space=pl.ANY),
                 _row(tm, LANES), _row(tm, LANES), _row(tm, LANES), _row(tm, A_W), _row(tm, A_KV_W), _row(tm, A_KV_W)]
                + [_row(tm, B_W)] * 3 + [_perm_spec(tm, 4, B_W)] * 3 + [_perm_spec(tm, 16, B_W)] * 3
                + [_row(tm, C_W), _row(tm, D_MODEL)])
    return pl.pallas_call(
        body, name="inproj_bwd_x", grid=(seq // tm,), in_specs=in_specs,
        out_specs=[_row(tm, D_MODEL), _row(tm, D_IN), _full((8, D_MODEL))],
        out_shape=[jax.ShapeDtypeStruct((seq, D_MODEL), F32), jax.ShapeDtypeStruct((seq, D_IN), BF16),
                   jax.ShapeDtypeStruct((8, D_MODEL), F32)],
        scratch_shapes=[pltpu.VMEM((n_b, tm, LANES), F32), pltpu.VMEM((n_b, tm, LANES), F32)] + _w_in_scratch(),
        compiler_params=_params(dimension_semantics=("arbitrary",)),
    )(x, dh, pre_g, w_in_full, *tabs, dqa, dka, dva, *dqkv_b[1], *[_perm_view(t, 4) for t in dqkv_b[4]],
      *[_perm_view(t, 16) for t in dqkv_b[16]], dqc, dgate)


class _ReduceScatter:
    def __init__(self, ins, outs, scratch):
        self.n = n = len(ins)
        self.ins, self.outs = ins, outs
        self.mine, self.got, self.snd, self.rcv = (scratch[n * t:n * (t + 1)] for t in range(4))
        self.load_sems, self.d2d_send, self.d2d_recv, self.ici_send, self.ici_recv = scratch[4 * n:]
        self.pos = _mesh_pos()
        self.pairs = [(a, kk) for kk in (3, 1, 2) for a in range(n)]

    @staticmethod
    def scratch_shapes(shapes):
        return ([pltpu.VMEM((4,) + s, F32) for s in shapes] + [pltpu.VMEM((4,) + s, F32) for s in shapes]
                + [pltpu.VMEM((3,) + s, BF16) for s in shapes] + [pltpu.VMEM((3,) + s, BF16) for s in shapes]
                + [pltpu.SemaphoreType.DMA((len(shapes), 4))] * 5)

    def _chip(self, kk):
        x, y, _ = self.pos
        return (1 - x if kk & 2 else x, 1 - y if kk & 1 else y)

    def _load(self, a, kk):
        block = _dev_index((*self._chip(kk), self.pos[2]))
        return pltpu.make_async_copy(self.ins[a].at[block], self.mine[a].at[kk], self.load_sems.at[a, kk])

    def _swap(self, a, kk):
        x, y, c = self.pos
        return pltpu.make_async_remote_copy(
            src_ref=self.ins[a].at[_dev_index((*self._chip(kk), 1 - c))], dst_ref=self.got[a].at[kk],
            send_sem=self.d2d_send.at[a, kk], recv_sem=self.d2d_recv.at[a, kk],
            device_id=(x, y, 1 - c), device_id_type=MESH_ID)

    def _hop(self, a, kk):
        return pltpu.make_async_remote_copy(
            src_ref=self.snd[a].at[kk - 1], dst_ref=self.rcv[a].at[kk - 1], send_sem=self.ici_send.at[a, kk],
            recv_sem=self.ici_recv.at[a, kk], device_id=(*self._chip(kk), self.pos[2]), device_id_type=MESH_ID)

    def start(self):
        for kk in range(4):
            for a in range(self.n):
                self._load(a, kk).start()
                self._swap(a, kk).start()

    def send_chip_sums(self):
        for a, kk in self.pairs:
            self._load(a, kk).wait()
            self._swap(a, kk).wait_recv()
            self.snd[a][kk - 1] = (self.mine[a][kk] + self.got[a][kk]).astype(BF16)
            self._hop(a, kk).start()

    def finish(self):
        for a in range(self.n):
            self._load(a, 0).wait()
            self._swap(a, 0).wait_recv()
            acc = self.mine[a][0] + self.got[a][0]
            for kk in (1, 2, 3):
                self._hop(a, kk).wait_recv()
                acc = acc + self.rcv[a][kk - 1].astype(F32)
            self.outs[a][...] = acc
        for kk in range(4):
            for a in range(self.n):
                self._swap(a, kk).wait_send()
        for a, kk in self.pairs:
            self._hop(a, kk).wait_send()


def _inproj_bwd_w(u, dproj, d_wmem, d_wout, tm=1024):
    seq = u.shape[0]

    n_steps = seq // tm
    shapes = [d_wmem.shape[1:], d_wout.shape[1:]]

    def body(u_ref, dp_ref, wm, wo, dw_ref, g_wm, g_wo, *scratch):
        step = pl.program_id(0)
        exchange = _ReduceScatter((wm, wo), (g_wm, g_wo), scratch)

        @pl.when(step == 0)
        def _():
            dw_ref[...] = jnp.zeros_like(dw_ref)
            exchange.start()

        @pl.when(step == min(1, n_steps - 1))
        def _():
            exchange.send_chip_sums()

        res = lax.dot_general(u_ref[...], dp_ref[...], _TN, preferred_element_type=F32)
        for k in range(N_DEV):
            dw_ref[k] += res[:, SHARD_IN * k:SHARD_IN * (k + 1)]

        @pl.when(step == n_steps - 1)
        def _():
            exchange.finish()

    hbm = pl.BlockSpec(memory_space=pl.ANY)
    return pl.pallas_call(
        body, name="inproj_bwd_w", grid=(n_steps,),
        in_specs=[_row(tm, D_MODEL), _row(tm, D_IN), hbm, hbm],
        out_specs=[_full((N_DEV, D_MODEL, SHARD_IN))] + [_full(s) for s in shapes],
        out_shape=[jax.ShapeDtypeStruct((N_DEV, D_MODEL, SHARD_IN), F32)] + [jax.ShapeDtypeStruct(s, F32) for s in shapes],
        scratch_shapes=_ReduceScatter.scratch_shapes(shapes),
        compiler_params=_params(dimension_semantics=("arbitrary",)),
    )(u, dproj, d_wmem, d_wout)


def _local_step(x, mem, pre_g, w_in, sink, mem_g, w_mem, w_out, post_g, target):
    u, *tabs, w_in_full = _prep(x, pre_g, w_in)
    qa, ka, va, qkv_b, qc, gate, w_mem_all, w_out_all = _inproj(u, w_in_full, tabs, w_mem, w_out)
    w_mem_full = w_mem_all.reshape(D_MODEL, 2 * C_W)
    w_out_full = w_out_all.reshape(D_MODEL, D_MODEL)
    mn, mk, mv = _memkv_fwd(mem, mem_g, w_mem_full)

    a_cfg = dict(dil=1, heads=A_HEADS, group=A_GROUP, max_dist=BLOCK - 1, nq=ATTN_BLOCKS_PER_STEP)
    b_cfgs = {dil: dict(dil=dil, heads=B_HEADS, group=1, max_dist=win // dil, nq=ATTN_BLOCKS_PER_STEP)
              for win, dil in B_CONFIGS}
    oa, lse_a = _banded_fwd(qa, ka, va, sink, name="attn_a_fwd", **a_cfg)
    ob, lse_b = {}, {}
    for dil, cfg in b_cfgs.items():
        ob[dil], lse_b[dil] = _banded_fwd(*qkv_b[dil], None, name=f"attn_b{dil}_fwd", **cfg)
    oc, lse_c = _cross_fwd(qc, mk, mv)

    dh, d_gate, d_a, d_b, d_c, d_wout, st_mid = _mid(oa, lse_a, ob, lse_b, oc, lse_c, gate, x, target, w_out_full, post_g)

    dqa, dka, dva, dsink = _banded_bwd(qa, ka, va, *d_a, sink, name="attn_a_bwd", **a_cfg)
    dqkv_b = {dil: _banded_bwd(*qkv_b[dil], *d_b[dil], None, name=f"attn_b{dil}_bwd", **cfg)
              for dil, cfg in b_cfgs.items()}
    dqc, dmk, dmv = _cross_bwd(qc, mk, mv, *d_c)
    d_wmem, st_mem = _memkv_bwd(mem, mem_g, mn, w_mem_full, dmk, dmv)

    grad_x, dproj, st_pre = _inproj_bwd_x(x, dh, pre_g, w_in_full, tabs, dqa, dka, dva, dqkv_b, dqc, d_gate)
    d_win, g_wmem, g_wout = _inproj_bwd_w(u, dproj, d_wmem.reshape(N_DEV, SHARD_ROWS, 2 * C_W),
                                          d_wout.reshape(N_DEV, SHARD_ROWS, D_MODEL))

    dsink_row = jnp.pad(dsink[0:1, :], ((0, 0), (0, D_MODEL - LANES)))
    stats = jnp.concatenate([st_pre[0:1], st_mem[0:1], st_mid[0:1], dsink_row, st_mid[1:2],
                             jnp.zeros((3, D_MODEL), F32)], axis=0)
    return grad_x, d_win, g_wmem, g_wout, stats


def _prep(x, pre_g, w_in, tm=1024, parts=2):
    seq = x.shape[0]
    n_steps = seq // tm
    rows = D_MODEL // parts
    pass_on_at = [max(n_steps - parts + a, 0) for a in range(parts)]
    j = jnp.arange(LANES) % HEAD_DIM
    freq = (ROPE_THETA ** (-(2 * (j % (ROT_DIM // 2))).astype(F32) / ROT_DIM))[None, :]

    def body(x_ref, g_ref, f_ref, win_ref, u_ref, c_ref, up_ref, dn_ref, win_out, win_b,
             send_sems, recv_sems, local_sems):
        step = pl.program_id(0)
        px, py, pc = _mesh_pos()
        me, sibling = (px, py, pc), (px, py, 1 - pc)
        chips = [(1 - px, py), (px, 1 - py), (1 - px, 1 - py)]

        def src(a):
            return win_b.at[pl.ds(rows * a, rows)]

        def slot(a, p):
            return win_out.at[_dev_index(p), pl.ds(rows * a, rows)]

        def copy(a, k, block, to, own=False):
            return pltpu.make_async_remote_copy(
                src_ref=src(a) if own else slot(a, block), dst_ref=slot(a, block),
                send_sem=send_sems.at[a, k], recv_sem=recv_sems.at[a, k], device_id=to, device_id_type=MESH_ID)

        def first_sends(a):
            return [copy(a, 1 + k, me, (*chip, pc), own=True) for k, chip in enumerate(chips)] + [
                copy(a, 0, me, sibling, own=True)]

        def local(a):
            return pltpu.make_async_copy(src(a), slot(a, me), local_sems.at[a])

        @pl.when(step == 0)
        def _():
            win_b[...] = win_ref[...].astype(BF16)
            for a in range(parts):
                local(a).start()
                for cp in first_sends(a):
                    cp.start()

        for a in range(parts):
            @pl.when(step == pass_on_at[a])
            def _(a=a):
                for k, chip in enumerate(chips):
                    copy(a, 1 + k, (*chip, pc), me).wait_recv()
                    copy(a, 4 + k, (*chip, pc), sibling).start()

        xv = x_ref[...]
        r = lax.rsqrt(jnp.mean(xv * xv, axis=-1, keepdims=True) + RMS_EPS)
        u_ref[...] = ((xv * r) * g_ref[...]).astype(BF16)
        pos = (lax.broadcasted_iota(jnp.int32, (tm, LANES), 0) + step * tm).astype(F32)
        head_lane = lax.broadcasted_iota(jnp.int32, (tm, LANES), 1) % HEAD_DIM
        ang = pos * f_ref[...]
        cos, sin = jnp.cos(ang), jnp.sin(ang)
        half = ROT_DIM // 2
        c_ref[...] = jnp.where(head_lane < ROT_DIM, cos, 1.0)
        up_ref[...] = jnp.where((head_lane >= half) & (head_lane < ROT_DIM), sin, 0.0)
        dn_ref[...] = jnp.where(head_lane < half, -sin, 0.0)

        @pl.when(step == n_steps - 1)
        def _():
            for a in range(parts):
                copy(a, 0, sibling, me).wait_recv()
                for k, chip in enumerate(chips):
                    copy(a, 4 + k, (*chip, 1 - pc), me).wait_recv()
            for a in range(parts):
                for cp in first_sends(a):
                    cp.wait_send()
                for k, chip in enumerate(chips):
                    copy(a, 4 + k, (*chip, pc), sibling).wait_send()
                local(a).wait()

    return pl.pallas_call(
        body, name="prep", grid=(n_steps,),
        in_specs=[_row(tm, D_MODEL), _full((1, D_MODEL)), _full((1, LANES)), _full(w_in.shape)],
        out_specs=[_row(tm, D_MODEL), _row(tm, LANES), _row(tm, LANES), _row(tm, LANES),
                   pl.BlockSpec(memory_space=pl.ANY)],
        out_shape=[jax.ShapeDtypeStruct((seq, D_MODEL), BF16)] + [jax.ShapeDtypeStruct((seq, LANES), F32)] * 3
        + [jax.ShapeDtypeStruct((N_DEV,) + w_in.shape, BF16)],
        scratch_shapes=[pltpu.VMEM(w_in.shape, BF16), pltpu.SemaphoreType.DMA((parts, 7)),
                        pltpu.SemaphoreType.DMA((parts, 7)), pltpu.SemaphoreType.DMA((parts,))],
        compiler_params=_params(dimension_semantics=("arbitrary",)),
    )(x, pre_g, freq, w_in)


def _exchange_grads(d_win, stats):
    def body(win, st, g_win, r_st, send_sems, recv_sems, local_sem, *scratch):
        exchange = _ReduceScatter((win,), (g_win,), scratch)
        exchange.start()
        pos = _mesh_pos()
        me = _dev_index(pos)
        own = pltpu.make_async_copy(st, r_st.at[me], local_sem)
        own.start()
        copies = []
        for s in range(1, N_DEV):
            peer = _xor_peer(pos, s)
            mk = lambda slot: pltpu.make_async_remote_copy(
                src_ref=st, dst_ref=r_st.at[slot], send_sem=send_sems.at[s], recv_sem=recv_sems.at[s],
                device_id=peer, device_id_type=MESH_ID)
            send, arrival = mk(me), mk(_dev_index(peer))
            send.start()
            copies.append((send, arrival))
        exchange.send_chip_sums()
        exchange.finish()
        for send, arrival in copies:
            arrival.wait_recv()
            send.wait_send()
        own.wait()

    hbm = pl.BlockSpec(memory_space=pl.ANY)
    shard = d_win.shape[1:]
    return pl.pallas_call(
        body, name="exchange_grads", in_specs=[hbm, hbm],
        out_specs=[pl.BlockSpec(memory_space=pltpu.VMEM), hbm],
        out_shape=[jax.ShapeDtypeStruct(shard, F32), jax.ShapeDtypeStruct((N_DEV,) + stats.shape, F32)],
        scratch_shapes=[pltpu.SemaphoreType.DMA((N_DEV,)), pltpu.SemaphoreType.DMA((N_DEV,)), pltpu.SemaphoreType.DMA(())]
        + _ReduceScatter.scratch_shapes([shard]),
        compiler_params=_params(),
    )(d_win, stats)


def _reduce_adamw(recv, w, m, v, *, tr, name):
    n_part, rows, cols = recv.shape

    def body(r_ref, w_ref, m_ref, v_ref, g_ref, d_ref, nm_ref, nv_ref):
        g = r_ref[0]
        for s in range(1, n_part):
            g = g + r_ref[s]
        g_ref[...] = g
        m2 = ADAM_B1 * m_ref[...] + (1.0 - ADAM_B1) * g
        v2 = ADAM_B2 * v_ref[...] + (1.0 - ADAM_B2) * (g * g)
        nm_ref[...] = m2
        nv_ref[...] = v2
        m_hat = m2 / (1.0 - ADAM_B1 ** ADAM_STEP)
        v_hat = v2 / (1.0 - ADAM_B2 ** ADAM_STEP)
        d_ref[...] = -ADAM_LR * (m_hat / (jnp.sqrt(v_hat) + ADAM_EPS) + ADAM_WD * w_ref[...])

    blk = pl.BlockSpec((tr, cols), lambda i: (i, 0))
    return pl.pallas_call(
        body, name=name, grid=(rows // tr,),
        in_specs=[pl.BlockSpec((n_part, tr, cols), lambda i: (0, i, 0)), blk, blk, blk],
        out_specs=[blk] * 4, out_shape=[jax.ShapeDtypeStruct((rows, cols), F32)] * 4,
        compiler_params=_params(dimension_semantics=("arbitrary",)),
    )(recv, w, m, v)


def _pack_rows(pre, memn, post, sink):
    sink_row = jnp.pad(sink, ((0, 0), (0, D_MODEL - A_HEADS)))
    return jnp.concatenate([pre, memn, post, sink_row, jnp.zeros((4, D_MODEL), F32)], axis=0)


def kernel(x, mem, pre_norm, w_in, sink_a, mem_norm, w_mem_kv, w_out, post_norm, loss_target, m_pre_norm, m_w_in, m_sink_a, m_mem_norm, m_w_mem_kv, m_w_out, m_post_norm, v_pre_norm, v_w_in, v_sink_a, v_mem_norm, v_w_mem_kv, v_w_out, v_post_norm):
    sink = jnp.pad(sink_a[0], (0, 8 - A_HEADS))
    grad_x, d_win, g_wmem, g_wout, stats = _local_step(
        x[0], mem[0], pre_norm, w_in[0], sink, mem_norm, w_mem_kv[0], w_out[0], post_norm, loss_target[0])
    g_win, r_stats = _exchange_grads(d_win, stats)

    big = {}
    for nm, g, w, m, v in (("w_in", g_win, w_in, m_w_in, v_w_in),
                           ("w_mem_kv", g_wmem, w_mem_kv, m_w_mem_kv, v_w_mem_kv),
                           ("w_out", g_wout, w_out, m_w_out, v_w_out)):
        res = _reduce_adamw(g[None], w[0], m[0], v[0], tr=SHARD_ROWS, name="adamw_" + nm)
        big[nm] = [t[None] for t in res]
    small = _reduce_adamw(
        r_stats, _pack_rows(pre_norm, mem_norm, post_norm, sink_a),
        _pack_rows(m_pre_norm, m_mem_norm, m_post_norm, m_sink_a),
        _pack_rows(v_pre_norm, v_mem_norm, v_post_norm, v_sink_a), tr=8, name="adamw_small")

    def unpack(t):
        return {"pre_norm": t[0:1], "mem_norm": t[1:2], "post_norm": t[2:3], "sink_a": t[3:4, 0:A_HEADS]}

    order = ("pre_norm", "w_in", "sink_a", "mem_norm", "w_mem_kv", "w_out", "post_norm")
    outs = [small[0][4, 0], grad_x[None]]
    for j in range(4):
        sm = unpack(small[j])
        outs += [big[n][j] if n in big else sm[n] for n in order]
    return tuple(outs)
```

```python
import jax
import jax.numpy as jnp
from jax import lax
from jax.experimental import pallas as pl
from jax.experimental.pallas import tpu as pltpu

F32 = jnp.float32
BF16 = jnp.bfloat16

D_MODEL = 1024
HEAD_DIM = 64
ROT_DIM = 16
ROPE_THETA = 500000.0
BLOCK = 128
LANES = 128
N_MEM = 256
RMS_EPS = 1e-6
SCALE = HEAD_DIM ** -0.5
A_HEADS, A_GROUP = 6, 3
B_HEADS = 6
C_HEADS = 4
A_W, A_KV_W, B_W, C_W = 384, 128, 384, 256
D_IN = 3072
N_DEV = 8
SHARD_IN = D_IN // N_DEV
SHARD_ROWS = D_MODEL // N_DEV
B_CONFIGS = ((128, 1), (512, 4), (2048, 16))
DILS = (4, 16)
NEG = -1e30
ATTN_BLOCKS_PER_STEP = 4
DELTA_LANE = 64
VMEM_LIMIT = 56 * 1024 * 1024

ADAM_LR, ADAM_B1, ADAM_B2, ADAM_EPS, ADAM_WD, ADAM_STEP = 0.001, 0.9, 0.999, 1e-08, 0.01, 10
MESH_ID = pl.DeviceIdType.MESH


def _params(**kw):
    return pltpu.CompilerParams(vmem_limit_bytes=VMEM_LIMIT, **kw)


def _full(shape):
    n = len(shape)
    return pl.BlockSpec(shape, lambda *_: (0,) * n)


def _row(tm, w):
    return pl.BlockSpec((tm, w), lambda i: (i, 0))


def _mesh_pos():
    return lax.axis_index("x"), lax.axis_index("y"), lax.axis_index("c")


def _dev_index(pos):
    return 4 * pos[0] + 2 * pos[1] + pos[2]


def _xor_peer(pos, s):
    x, y, c = pos
    return (1 - x if s & 4 else x, 1 - y if s & 2 else y, 1 - c if s & 1 else c)


def _perm_view(a, dil):
    return a.reshape(a.shape[0] // (BLOCK * dil), dil, BLOCK, a.shape[1])


def _perm_spec(tm, dil, w):
    chunk = BLOCK * dil
    if tm >= chunk:
        return pl.BlockSpec((tm // chunk, dil, BLOCK, w), lambda i: (i, 0, 0, 0))
    per = chunk // tm
    return pl.BlockSpec((1, dil, tm // dil, w), lambda i: (i // per, 0, i % per, 0))


def _put(scr, val):
    for c in range(val.shape[1] // LANES):
        scr[c] = val[:, LANES * c:LANES * (c + 1)]


def _get(scr):
    n = scr.shape[0]
    return scr[0] if n == 1 else jnp.concatenate([scr[c] for c in range(n)], axis=1)


def _get_class(scr, r, dil):
    n, rows = scr.shape[0], scr.shape[1]
    parts = [scr.at[c][pl.ds(r, rows // dil, stride=dil), :] for c in range(n)]
    return parts[0] if n == 1 else jnp.concatenate(parts, axis=1)


def _store_permuted(scr, out_ref, dil, dtype):
    for r in range(dil):
        out_ref[0, r] = _get_class(scr, r, dil).astype(dtype)


def _fill_permuted(in_ref, scr, dil):
    n, rows = scr.shape[0], scr.shape[1]
    for r in range(dil):
        val = in_ref[0, r].astype(F32)
        for c in range(n):
            scr.at[c][pl.ds(r, rows // dil, stride=dil), :] = val[:, LANES * c:LANES * (c + 1)]


def _load_permuted(in_ref, scr, dil):
    _fill_permuted(in_ref, scr, dil)
    return _get(scr)


def _rotate128(t, c, up, dn):
    return t * c + pltpu.roll(t, 8, 1) * up + pltpu.roll(t, LANES - 8, 1) * dn


def _rotate(t, c, up, dn):
    outs = [_rotate128(t[:, LANES * j:LANES * (j + 1)], c, up, dn) for j in range(t.shape[1] // LANES)]
    return outs[0] if len(outs) == 1 else jnp.concatenate(outs, axis=1)


def _w_in_scratch():
    return [pltpu.VMEM((D_MODEL, D_IN), BF16), pltpu.SemaphoreType.DMA((N_DEV,))]


def _stage_w_in(w_hbm, w_scr, sems):
    @pl.when(pl.program_id(0) == 0)
    def _():
        copies = [pltpu.make_async_copy(w_hbm.at[k], w_scr.at[:, pl.ds(SHARD_IN * k, SHARD_IN)], sems.at[k])
                  for k in range(N_DEV)]
        for cp in copies:
            cp.start()
        for cp in copies:
            cp.wait()


def _inproj(u, w_in_full, tabs, w_mem, w_out, tm=1024):
    seq = u.shape[0]
    n_chunk = D_IN // LANES
    n_steps = seq // tm

    def body(u_ref, w_hbm, c_ref, up_ref, dn_ref, wm_ref, wo_ref, qa_ref, ka_ref, va_ref,
             qb1_ref, kb1_ref, vb1_ref, qb4_ref, kb4_ref, vb4_ref, qb16_ref, kb16_ref, vb16_ref,
             qc_ref, gate_ref, wm_all, wo_all, proj, w_scr, w_sems, wm_b, wo_b, send_sems, recv_sems, local_sems):
        step = pl.program_id(0)
        shards, gathered = (wm_b, wo_b), (wm_all, wo_all)

        def gather_copies(arriving):
            pos = _mesh_pos()
            me = _dev_index(pos)
            local = [] if arriving else [
                pltpu.make_async_copy(shards[a], gathered[a].at[me], local_sems.at[a]) for a in range(2)]
            remote = []
            for s in range(1, N_DEV):
                peer = _xor_peer(pos, s)
                for a in range(2):
                    remote.append(pltpu.make_async_remote_copy(
                        src_ref=shards[a], dst_ref=gathered[a].at[_dev_index(peer) if arriving else me],
                        send_sem=send_sems.at[a, s], recv_sem=recv_sems.at[a, s], device_id=peer,
                        device_id_type=MESH_ID))
            return local, remote

        @pl.when(step == 0)
        def _():
            wm_b[...] = wm_ref[...].astype(BF16)
            wo_b[...] = wo_ref[...].astype(BF16)
            local, sends = gather_copies(arriving=False)
            for cp in local + sends:
                cp.start()

        _stage_w_in(w_hbm, w_scr, w_sems)
        u = u_ref[...]
        for n0 in range(0, D_IN, D_MODEL):
            acc = jnp.dot(u, w_scr[:, n0:n0 + D_MODEL], preferred_element_type=F32)
            for c3 in range(D_MODEL // LANES):
                proj[n0 // LANES + c3] = acc[:, LANES * c3:LANES * (c3 + 1)]
        c, up, dn = c_ref[...], up_ref[...], dn_ref[...]

        def cols(lo, hi, rot=False, scale=None):
            parts = []
            for ch in range(lo // LANES, hi // LANES):
                t = proj[ch]
                if rot:
                    t = _rotate128(t, c, up, dn)
                if scale is not None:
                    t = t * scale
                parts.append(t)
            return parts[0] if len(parts) == 1 else jnp.concatenate(parts, axis=1)

        qa_ref[...] = cols(0, 384, True, SCALE).astype(BF16)
        ka_ref[...] = cols(384, 512, True).astype(BF16)
        va_ref[...] = cols(512, 640).astype(BF16)
        gate_ref[:, 0:384] = cols(640, 1024).astype(BF16)
        gate_ref[:, 384:768] = cols(2176, 2560).astype(BF16)
        gate_ref[:, 768:1024] = cols(2816, 3072).astype(BF16)
        qc_ref[...] = cols(2560, 2816, False, SCALE).astype(BF16)
        for ch in range(1024 // LANES, 1408 // LANES):
            proj[ch] = _rotate128(proj[ch], c, up, dn) * SCALE
        for ch in range(1408 // LANES, 1792 // LANES):
            proj[ch] = _rotate128(proj[ch], c, up, dn)
        for lo, nat, p4, p16 in ((1024, qb1_ref, qb4_ref, qb16_ref), (1408, kb1_ref, kb4_ref, kb16_ref),
                                 (1792, vb1_ref, vb4_ref, vb16_ref)):
            chunks = range(lo // LANES, lo // LANES + B_W // LANES)
            nat[...] = jnp.concatenate([proj[ch] for ch in chunks], axis=1).astype(BF16)
            for dil, ref in ((4, p4), (16, p16)):
                span = min(tm, BLOCK * dil)
                for cc in range(tm // span):
                    for rr in range(dil):
                        ref[cc, rr] = jnp.concatenate(
                            [proj.at[ch][pl.ds(cc * span + rr, span // dil, stride=dil), :] for ch in chunks],
                            axis=1).astype(BF16)

        @pl.when(step == n_steps - 1)
        def _():
            for cp in gather_copies(arriving=True)[1]:
                cp.wait_recv()
            local, sends = gather_copies(arriving=False)
            for cp in sends:
                cp.wait_send()
            for cp in local:
                cp.wait()

    nat_w = (A_W, A_KV_W, A_KV_W, B_W, B_W, B_W)
    out_specs = [_row(tm, w) for w in nat_w]
    out_shape = [jax.ShapeDtypeStruct((seq, w), BF16) for w in nat_w]
    for dil in DILS:
        out_specs += [_perm_spec(tm, dil, B_W)] * 3
        out_shape += [jax.ShapeDtypeStruct((seq // (BLOCK * dil), dil, BLOCK, B_W), BF16)] * 3
    hbm = pl.BlockSpec(memory_space=pl.ANY)
    out_specs += [_row(tm, C_W), _row(tm, D_MODEL), hbm, hbm]
    out_shape += [jax.ShapeDtypeStruct((seq, C_W), BF16), jax.ShapeDtypeStruct((seq, D_MODEL), BF16),
                  jax.ShapeDtypeStruct((N_DEV,) + w_mem.shape, BF16), jax.ShapeDtypeStruct((N_DEV,) + w_out.shape, BF16)]
    res = pl.pallas_call(
        body, name="inproj", grid=(n_steps,),
        in_specs=[_row(tm, D_MODEL), hbm, _row(tm, LANES), _row(tm, LANES), _row(tm, LANES),
                  _full(w_mem.shape), _full(w_out.shape)],
        out_specs=out_specs, out_shape=out_shape,
        scratch_shapes=[pltpu.VMEM((n_chunk, tm, LANES), F32)] + _w_in_scratch()
        + [pltpu.VMEM(w_mem.shape, BF16), pltpu.VMEM(w_out.shape, BF16), pltpu.SemaphoreType.DMA((2, N_DEV)),
           pltpu.SemaphoreType.DMA((2, N_DEV)), pltpu.SemaphoreType.DMA((2,))],
        compiler_params=_params(dimension_semantics=("arbitrary",)),
    )(u, w_in_full, *tabs, w_mem, w_out)
    qa, ka, va = res[0:3]
    qkv_b = {1: res[3:6], 4: [t.reshape(seq, B_W) for t in res[6:9]], 16: [t.reshape(seq, B_W) for t in res[9:12]]}
    return qa, ka, va, qkv_b, res[12], res[13], res[14], res[15]


def _memkv_fwd(mem, mem_g, w_mem_full):
    def body(mem_ref, g_ref, w_ref, mn_ref, mk_ref, mv_ref):
        mv_ = mem_ref[...]
        r = lax.rsqrt(jnp.mean(mv_ * mv_, axis=-1, keepdims=True) + RMS_EPS)
        mn = ((mv_ * r) * g_ref[...]).astype(BF16)
        mn_ref[...] = mn
        mkv = jnp.dot(mn, w_ref[...], preferred_element_type=F32)
        mk_ref[...] = mkv[:, 0:C_W].astype(BF16)
        mv_ref[...] = mkv[:, C_W:2 * C_W].astype(BF16)

    return pl.pallas_call(
        body, name="memkv_fwd",
        out_shape=[jax.ShapeDtypeStruct((N_MEM, D_MODEL), BF16),
                   jax.ShapeDtypeStruct((N_MEM, C_W), BF16), jax.ShapeDtypeStruct((N_MEM, C_W), BF16)],
        compiler_params=_params(),
    )(mem, mem_g, w_mem_full)


def _memkv_bwd(mem, mem_g, mn, w_mem_full, dmk, dmv):
    def body(mem_ref, g_ref, mn_ref, w_ref, dmk_ref, dmv_ref, dw_ref, st_ref):
        dmkv = jnp.concatenate([dmk_ref[...], dmv_ref[...]], axis=1).astype(BF16)
        dw_ref[...] = lax.dot_general(mn_ref[...], dmkv, (((0,), (0,)), ((), ())), preferred_element_type=F32)
        dmn = lax.dot_general(dmkv, w_ref[...], (((1,), (1,)), ((), ())), preferred_element_type=F32)
        mv_ = mem_ref[...]
        r = lax.rsqrt(jnp.mean(mv_ * mv_, axis=-1, keepdims=True) + RMS_EPS)
        st_ref[...] = jnp.zeros_like(st_ref)
        st_ref[0:1, :] = jnp.sum(dmn * (mv_ * r), axis=0, keepdims=True)

    return pl.pallas_call(
        body, name="memkv_bwd",
        out_shape=[jax.ShapeDtypeStruct((D_MODEL, 2 * C_W), F32), jax.ShapeDtypeStruct((8, D_MODEL), F32)],
        compiler_params=_params(),
    )(mem, mem_g, mn, w_mem_full, dmk, dmv)


def _band_mask(has_prev, max_dist):
    qi = lax.broadcasted_iota(jnp.int32, (BLOCK, 2 * BLOCK), 0)
    kj = lax.broadcasted_iota(jnp.int32, (BLOCK, 2 * BLOCK), 1)
    dist = qi + BLOCK - kj
    return (dist >= 0) & (dist <= max_dist) & ((kj >= BLOCK) | has_prev)


_NT = (((1,), (1,)), ((), ()))
_TN = (((0,), (0,)), ((), ()))


def _head_only(val, h):
    slab = val[:, LANES * (h // 2):LANES * (h // 2 + 1)]
    lane = lax.broadcasted_iota(jnp.int32, slab.shape, 1)
    keep = (lane < HEAD_DIM) if h % 2 == 0 else (lane >= HEAD_DIM)
    return jnp.where(keep, slab, jnp.zeros((), slab.dtype))


class _KvSlabs:
    def __init__(self, cat, group):
        self.cat, self.group, self.swapped = cat, group, {}

    def is_swapped(self, h):
        return (h // self.group) % 2 != h % 2

    def __call__(self, h):
        j = (h // self.group) // 2
        slab = self.cat[:, LANES * j:LANES * (j + 1)]
        if not self.is_swapped(h):
            return slab
        if j not in self.swapped:
            self.swapped[j] = jnp.concatenate([slab[:, HEAD_DIM:], slab[:, :HEAD_DIM]], axis=1)
        return self.swapped[j]


class _BandSteps:
    def __init__(self, seq, dil, nq):
        self.nq, self.rows, self.consecutive = nq, nq * BLOCK, dil == 1
        nb = seq // dil // BLOCK
        if self.consecutive:
            assert nb % nq == 0
            self.outer, self.inner, self.stride = 1, nb // nq, 1
        else:
            assert dil % nq == 0
            self.outer, self.inner, self.stride = dil // nq, nb, dil // nq

    def own(self, w, clamp=False):
        cur = (lambda i: jnp.minimum(i, self.inner - 1)) if clamp else (lambda i: i)
        return pl.BlockSpec((self.rows, w), lambda r, i: (cur(i) * self.stride + r, 0))

    def prev(self, w, clamp=False):
        cur = (lambda i: jnp.minimum(i, self.inner - 1)) if clamp else (lambda i: i)
        if self.consecutive:
            return pl.BlockSpec((BLOCK, w), lambda r, i: (jnp.maximum(cur(i) * self.nq - 1, 0), 0))
        return pl.BlockSpec((self.rows, w), lambda r, i: (jnp.maximum(cur(i) - 1, 0) * self.stride + r, 0))

    def late(self, w):
        return pl.BlockSpec((self.rows, w), lambda r, i: (jnp.maximum(i - 1, 0) * self.stride + r, 0))

    def rows_of(self, j):
        return slice(BLOCK * j, BLOCK * (j + 1))

    def keys(self, p_ref, c_ref, j):
        if not self.consecutive:
            before = p_ref[self.rows_of(j), :]
        elif j == 0:
            before = p_ref[...]
        else:
            before = c_ref[self.rows_of(j - 1), :]
        return jnp.concatenate([before, c_ref[self.rows_of(j), :]], axis=0)

    def has_prev(self, i, j):
        return True if (self.consecutive and j > 0) else (i > 0)


def _banded_fwd(q, k, v, sink, *, dil, heads, group, max_dist, nq, name):
    seq = q.shape[0]
    kvh = heads // group
    qw, kw = heads * HEAD_DIM, kvh * HEAD_DIM
    steps = _BandSteps(seq, dil, nq)

    def body(*refs):
        if sink is not None:
            sink_ref, refs = refs[0], refs[1:]
        q_ref, kp_ref, kc_ref, vp_ref, vc_ref, o_ref, lse_ref, s_scr, p_scr = refs
        i = pl.program_id(1)
        lane = lax.broadcasted_iota(jnp.int32, (BLOCK, LANES), 1)
        k_of = [_KvSlabs(steps.keys(kp_ref, kc_ref, j), group) for j in range(nq)]
        v_of = [_KvSlabs(steps.keys(vp_ref, vc_ref, j), group) for j in range(nq)]
        for j in range(nq):
            qv = q_ref[steps.rows_of(j), :]
            for h in range(heads):
                s_scr[j * heads + h] = lax.dot_general(_head_only(qv, h), k_of[j](h), _NT, preferred_element_type=F32)
        ls = {}
        for j in range(nq):
            valid = _band_mask(steps.has_prev(i, j), max_dist)
            lse_tile = jnp.zeros((BLOCK, LANES), F32)
            for h in range(heads):
                s = jnp.where(valid, s_scr[j * heads + h], NEG)
                m = jnp.max(s, axis=-1, keepdims=True)
                if sink is not None:
                    sk = sink_ref[h]
                    m = jnp.maximum(m, sk)
                p = jnp.exp(s - m)
                l = jnp.sum(p, axis=-1, keepdims=True)
                if sink is not None:
                    l = l + jnp.exp(sk - m)
                p_scr[j * heads + h] = p.astype(BF16)
                ls[j, h] = l
                lse_tile = jnp.where(lane == h, m + jnp.log(l), lse_tile)
            lse_ref[steps.rows_of(j), :] = lse_tile
        for j in range(nq):
            for pr in range(heads // 2):
                he, ho = 2 * pr, 2 * pr + 1
                even = jnp.dot(p_scr[j * heads + he], v_of[j](he), preferred_element_type=F32) / ls[j, he]
                odd = jnp.dot(p_scr[j * heads + ho], v_of[j](ho), preferred_element_type=F32) / ls[j, ho]
                o_ref[steps.rows_of(j), LANES * pr:LANES * (pr + 1)] = jnp.where(lane < HEAD_DIM, even, odd).astype(BF16)

    in_specs = [steps.own(qw), steps.prev(kw), steps.own(kw), steps.prev(kw), steps.own(kw)]
    args = [q, k, k, v, v]
    if sink is not None:
        in_specs = [pl.BlockSpec(memory_space=pltpu.SMEM)] + in_specs
        args = [sink] + args
    return pl.pallas_call(
        body, name=name, grid=(steps.outer, steps.inner), in_specs=in_specs,
        out_specs=[steps.own(qw), steps.own(LANES)],
        out_shape=[jax.ShapeDtypeStruct((seq, qw), BF16), jax.ShapeDtypeStruct((seq, LANES), F32)],
        scratch_shapes=[pltpu.VMEM((nq * heads, BLOCK, 2 * BLOCK), F32), pltpu.VMEM((nq * heads, BLOCK, 2 * BLOCK), BF16)],
        compiler_params=_params(dimension_semantics=("arbitrary", "arbitrary")),
    )(*args)


def _banded_bwd(q, k, v, d_out, stat, sink, *, dil, heads, group, max_dist, nq, name):
    seq = q.shape[0]
    kvh = heads // group
    qw, kw = heads * HEAD_DIM, kvh * HEAD_DIM
    steps = _BandSteps(seq, dil, nq)
    n_in = 7

    def body(*refs):
        if sink is not None:
            sink_ref, refs = refs[0], refs[1:]
            dsink_ref, refs = refs[n_in], refs[:n_in] + refs[n_in + 1:]
        (q_ref, kp_ref, kc_ref, vp_ref, vc_ref, do_ref, st_ref, dq_ref, dk_ref, dv_ref,
         kcar, vcar, s_scr, dp_scr, p_scr, ds_scr) = refs
        r, i = pl.program_id(0), pl.program_id(1)

        @pl.when(i == 0)
        def _():
            kcar[...] = jnp.zeros_like(kcar)
            vcar[...] = jnp.zeros_like(vcar)

        if sink is not None:
            @pl.when((i == 0) & (r == 0))
            def _():
                dsink_ref[...] = jnp.zeros_like(dsink_ref)

        @pl.when(i < steps.inner)
        def _():
            lane = lax.broadcasted_iota(jnp.int32, (1, LANES), 1)
            lane_q = lax.broadcasted_iota(jnp.int32, (BLOCK, LANES), 1)
            k_of = [_KvSlabs(steps.keys(kp_ref, kc_ref, j), group) for j in range(nq)]
            v_of = [_KvSlabs(steps.keys(vp_ref, vc_ref, j), group) for j in range(nq)]
            qms, doms = {}, {}
            for j in range(nq):
                qv, dov = q_ref[steps.rows_of(j), :], do_ref[steps.rows_of(j), :]
                for h in range(heads):
                    qms[j, h], doms[j, h] = _head_only(qv, h), _head_only(dov, h)
                    s_scr[j * heads + h] = lax.dot_general(qms[j, h], k_of[j](h), _NT, preferred_element_type=F32)
                    dp_scr[j * heads + h] = lax.dot_general(doms[j, h], v_of[j](h), _NT, preferred_element_type=F32)
            dsink_row = jnp.zeros((1, LANES), F32)
            for j in range(nq):
                st = st_ref[steps.rows_of(j), :]
                valid = _band_mask(steps.has_prev(i, j), max_dist)
                for h in range(heads):
                    lse_h = st[:, h:h + 1]
                    delta = st[:, DELTA_LANE + h:DELTA_LANE + h + 1]
                    p = jnp.where(valid, jnp.exp(s_scr[j * heads + h] - lse_h), 0.0)
                    p_scr[j * heads + h] = p.astype(BF16)
                    ds_scr[j * heads + h] = (p * (dp_scr[j * heads + h] - delta)).astype(BF16)
                    if sink is not None:
                        ds_sink = jnp.sum(-jnp.exp(sink_ref[h] - lse_h) * delta, axis=0, keepdims=True)
                        dsink_row = dsink_row + jnp.where(lane == h, ds_sink, 0.0)
            for j in range(nq):
                for pr in range(heads // 2):
                    he, ho = 2 * pr, 2 * pr + 1
                    even = jnp.dot(ds_scr[j * heads + he], k_of[j](he), preferred_element_type=F32)
                    odd = jnp.dot(ds_scr[j * heads + ho], k_of[j](ho), preferred_element_type=F32)
                    dq_ref[steps.rows_of(j), LANES * pr:LANES * (pr + 1)] = (
                        jnp.where(lane_q < HEAD_DIM, even, odd).astype(BF16))
            if steps.consecutive:
                dk_ref[...] = kcar[...].astype(BF16)
                dv_ref[...] = vcar[...].astype(BF16)
            for j in range(nq):
                for slab in range(kw // LANES):
                    acc = {}
                    for h in range(heads):
                        if (h // group) // 2 != slab:
                            continue
                        key = k_of[j].is_swapped(h)
                        dk_h = lax.dot_general(ds_scr[j * heads + h], qms[j, h], _TN, preferred_element_type=F32)
                        dv_h = lax.dot_general(p_scr[j * heads + h], doms[j, h], _TN, preferred_element_type=F32)
                        acc[key] = (dk_h, dv_h) if key not in acc else (acc[key][0] + dk_h, acc[key][1] + dv_h)
                    dk_j, dv_j = acc.get(False, (None, None))
                    if True in acc:
                        unswap = lambda t: jnp.concatenate([t[:, HEAD_DIM:], t[:, :HEAD_DIM]], axis=1)
                        dk_s, dv_s = unswap(acc[True][0]), unswap(acc[True][1])
                        dk_j = dk_s if dk_j is None else dk_j + dk_s
                        dv_j = dv_s if dv_j is None else dv_j + dv_s
                    sl = slice(LANES * slab, LANES * (slab + 1))
                    own_rows = steps.rows_of(j)
                    if not steps.consecutive:
                        dk_ref[own_rows, sl] = (kcar[own_rows, sl] + dk_j[0:BLOCK]).astype(BF16)
                        dv_ref[own_rows, sl] = (vcar[own_rows, sl] + dv_j[0:BLOCK]).astype(BF16)
                    elif j == 0:
                        last = steps.rows_of(nq - 1)
                        dk_ref[last, sl] = (kcar[last, sl] + dk_j[0:BLOCK]).astype(BF16)
                        dv_ref[last, sl] = (vcar[last, sl] + dv_j[0:BLOCK]).astype(BF16)
                    else:
                        before = steps.rows_of(j - 1)
                        kcar[before, sl] += dk_j[0:BLOCK]
                        vcar[before, sl] += dv_j[0:BLOCK]
                    kcar[own_rows, sl] = dk_j[BLOCK:2 * BLOCK]
                    vcar[own_rows, sl] = dv_j[BLOCK:2 * BLOCK]
            if sink is not None:
                dsink_ref[0:1, :] += dsink_row

        @pl.when(i == steps.inner)
        def _():
            dk_ref[...] = kcar[...].astype(BF16)
            dv_ref[...] = vcar[...].astype(BF16)

    own, prev = (lambda w: steps.own(w, clamp=True)), (lambda w: steps.prev(w, clamp=True))
    in_specs = [own(qw), prev(kw), own(kw), prev(kw), own(kw), own(qw), own(LANES)]
    args = [q, k, k, v, v, d_out, stat]
    out_specs = [own(qw), steps.late(kw), steps.late(kw)]
    out_shape = [jax.ShapeDtypeStruct((seq, qw), BF16), jax.ShapeDtypeStruct((seq, kw), BF16),
                 jax.ShapeDtypeStruct((seq, kw), BF16)]
    if sink is not None:
        in_specs = [pl.BlockSpec(memory_space=pltpu.SMEM)] + in_specs
        args = [sink] + args
        out_specs = [_full((8, LANES))] + out_specs
        out_shape = [jax.ShapeDtypeStruct((8, LANES), F32)] + out_shape
    n_hb = nq * heads
    res = pl.pallas_call(
        body, name=name, grid=(steps.outer, steps.inner + 1), in_specs=in_specs, out_specs=out_specs,
        out_shape=out_shape,
        scratch_shapes=[pltpu.VMEM((steps.rows, kw), F32), pltpu.VMEM((steps.rows, kw), F32)]
        + [pltpu.VMEM((n_hb, BLOCK, 2 * BLOCK), F32)] * 2 + [pltpu.VMEM((n_hb, BLOCK, 2 * BLOCK), BF16)] * 2,
        compiler_params=_params(dimension_semantics=("arbitrary", "arbitrary")),
    )(*args)
    if sink is not None:
        return res[1], res[2], res[3], res[0]
    return res


def _cross_fwd(q, mk, mv, tq=512):
    seq = q.shape[0]

    def body(q_ref, mk_ref, mv_ref, o_ref, lse_ref, s_scr, p_scr):
        qv = q_ref[...]
        k_of, v_of = _KvSlabs(mk_ref[...], 1), _KvSlabs(mv_ref[...], 1)
        lane = lax.broadcasted_iota(jnp.int32, (tq, LANES), 1)
        lse_tile = jnp.zeros((tq, LANES), F32)
        for h in range(C_HEADS):
            s_scr[h] = lax.dot_general(_head_only(qv, h), k_of(h), _NT, preferred_element_type=F32)
        ls = []
        for h in range(C_HEADS):
            s = s_scr[h]
            m = jnp.max(s, axis=-1, keepdims=True)
            p = jnp.exp(s - m)
            l = jnp.sum(p, axis=-1, keepdims=True)
            p_scr[h] = p.astype(BF16)
            ls.append(l)
            lse_tile = jnp.where(lane == h, m + jnp.log(l), lse_tile)
        for pr in range(C_HEADS // 2):
            even = jnp.dot(p_scr[2 * pr], v_of(2 * pr), preferred_element_type=F32) / ls[2 * pr]
            odd = jnp.dot(p_scr[2 * pr + 1], v_of(2 * pr + 1), preferred_element_type=F32) / ls[2 * pr + 1]
            o_ref[:, LANES * pr:LANES * (pr + 1)] = jnp.where(lane < HEAD_DIM, even, odd).astype(BF16)
        lse_ref[...] = lse_tile

    return pl.pallas_call(
        body, name="cross_fwd", grid=(seq // tq,),
        in_specs=[_row(tq, C_W), _full((N_MEM, C_W)), _full((N_MEM, C_W))],
        out_specs=[_row(tq, C_W), _row(tq, LANES)],
        out_shape=[jax.ShapeDtypeStruct((seq, C_W), BF16), jax.ShapeDtypeStruct((seq, LANES), F32)],
        scratch_shapes=[pltpu.VMEM((C_HEADS, tq, N_MEM), F32), pltpu.VMEM((C_HEADS, tq, N_MEM), BF16)],
        compiler_params=_params(dimension_semantics=("arbitrary",)),
    )(q, mk, mv)


def _cross_bwd(q, mk, mv, d_out, stat, tq=512):
    seq = q.shape[0]

    def body(q_ref, mk_ref, mv_ref, do_ref, st_ref, dq_ref, dmk_ref, dmv_ref, s_scr, dp_scr, p_scr, ds_scr):
        @pl.when(pl.program_id(0) == 0)
        def _():
            dmk_ref[...] = jnp.zeros_like(dmk_ref)
            dmv_ref[...] = jnp.zeros_like(dmv_ref)

        qv, dov, st = q_ref[...], do_ref[...], st_ref[...]
        k_of, v_of = _KvSlabs(mk_ref[...], 1), _KvSlabs(mv_ref[...], 1)
        qms = [_head_only(qv, h) for h in range(C_HEADS)]
        doms = [_head_only(dov, h) for h in range(C_HEADS)]
        for h in range(C_HEADS):
            s_scr[h] = lax.dot_general(qms[h], k_of(h), _NT, preferred_element_type=F32)
            dp_scr[h] = lax.dot_general(doms[h], v_of(h), _NT, preferred_element_type=F32)
        for h in range(C_HEADS):
            p = jnp.exp(s_scr[h] - st[:, h:h + 1])
            p_scr[h] = p.astype(BF16)
            ds_scr[h] = (p * (dp_scr[h] - st[:, DELTA_LANE + h:DELTA_LANE + h + 1])).astype(BF16)
        lane = lax.broadcasted_iota(jnp.int32, (tq, LANES), 1)
        for pr in range(C_HEADS // 2):
            sl = slice(LANES * pr, LANES * (pr + 1))
            even = jnp.dot(ds_scr[2 * pr], k_of(2 * pr), preferred_element_type=F32)
            odd = jnp.dot(ds_scr[2 * pr + 1], k_of(2 * pr + 1), preferred_element_type=F32)
            dq_ref[:, sl] = jnp.where(lane < HEAD_DIM, even, odd).astype(BF16)
            dmk_ref[:, sl] += (lax.dot_general(ds_scr[2 * pr], qms[2 * pr], _TN, preferred_element_type=F32)
                               + lax.dot_general(ds_scr[2 * pr + 1], qms[2 * pr + 1], _TN, preferred_element_type=F32))
            dmv_ref[:, sl] += (lax.dot_general(p_scr[2 * pr], doms[2 * pr], _TN, preferred_element_type=F32)
                               + lax.dot_general(p_scr[2 * pr + 1], doms[2 * pr + 1], _TN, preferred_element_type=F32))

    return pl.pallas_call(
        body, name="cross_bwd", grid=(seq // tq,),
        in_specs=[_row(tq, C_W), _full((N_MEM, C_W)), _full((N_MEM, C_W)), _row(tq, C_W), _row(tq, LANES)],
        out_specs=[_row(tq, C_W), _full((N_MEM, C_W)), _full((N_MEM, C_W))],
        out_shape=[jax.ShapeDtypeStruct((seq, C_W), BF16), jax.ShapeDtypeStruct((N_MEM, C_W), F32),
                   jax.ShapeDtypeStruct((N_MEM, C_W), F32)],
        scratch_shapes=[pltpu.VMEM((C_HEADS, tq, N_MEM), F32)] * 2 + [pltpu.VMEM((C_HEADS, tq, N_MEM), BF16)] * 2,
        compiler_params=_params(dimension_semantics=("arbitrary",)),
    )(q, mk, mv, d_out, stat)


def _per_head(tile, width):
    rows = tile.shape[0]
    return jnp.concatenate(
        [jnp.broadcast_to(tile[:, h:h + 1], (rows, HEAD_DIM)) for h in range(width // HEAD_DIM)], axis=1)


def _with_delta(lse_tile, prod):
    rows = lse_tile.shape[0]
    lane = lax.broadcasted_iota(jnp.int32, (rows, LANES), 1)
    tile = lse_tile
    for h in range(prod.shape[1] // HEAD_DIM):
        d = jnp.sum(prod[:, HEAD_DIM * h:HEAD_DIM * (h + 1)], axis=-1, keepdims=True)
        tile = jnp.where(lane == DELTA_LANE + h, d, tile)
    return tile


def _mid(oa, lse_a, ob, lse_b, oc, lse_c, gate, x, target, w_out_full, post_g, tm=512):
    seq = x.shape[0]
    n_b = B_W // LANES

    def body(oa_ref, la_ref, b1_ref, l1_ref, b4_ref, l4_ref, b16_ref, l16_ref, oc_ref, lc_ref,
             gate_ref, x_ref, t_ref, w_ref, pg_ref,
             dh_ref, dg_ref, doa_ref, sa_ref, dob1_ref, sb1_ref, dob4_ref, sb4_ref, dob16_ref, sb16_ref,
             doc_ref, sc_ref, dw_ref, st_ref, scr_b4, scr_b16, scr_l4, scr_l16, scr_do, scr_sb):
        @pl.when(pl.program_id(0) == 0)
        def _():
            dw_ref[...] = jnp.zeros_like(dw_ref)
            st_ref[...] = jnp.zeros_like(st_ref)

        b1, l1 = b1_ref[...].astype(F32), l1_ref[...]
        b4, l4 = _load_permuted(b4_ref, scr_b4, 4), _load_permuted(l4_ref, scr_l4, 4)
        b16, l16 = _load_permuted(b16_ref, scr_b16, 16), _load_permuted(l16_ref, scr_l16, 16)
        lm = jnp.maximum(jnp.maximum(l1, l4), l16)
        e1, e4, e16 = jnp.exp(l1 - lm), jnp.exp(l4 - lm), jnp.exp(l16 - lm)
        den = e1 + e4 + e16
        lse_b_tile = lm + jnp.log(den)
        ob_v = _per_head(e1 / den, B_W) * b1 + _per_head(e4 / den, B_W) * b4 + _per_head(e16 / den, B_W) * b16
        o_all = jnp.concatenate([oa_ref[...].astype(F32), ob_v, oc_ref[...].astype(F32)], axis=1)
        g = gate_ref[...].astype(F32)
        sig = 1.0 / (1.0 + jnp.exp(-g))
        silu = g * sig
        y = (o_all * silu).astype(BF16)
        w = w_ref[...]
        z = jnp.dot(y, w, preferred_element_type=F32)
        rz = lax.rsqrt(jnp.mean(z * z, axis=-1, keepdims=True) + RMS_EPS)
        hn = z * rz
        pg = pg_ref[...]
        err = (x_ref[...] + hn * pg) - t_ref[...]
        loss = 0.5 * jnp.sum(jnp.mean(err * err, axis=-1, keepdims=True), axis=0, keepdims=True)
        dh = err * (1.0 / D_MODEL)
        dh_ref[...] = dh.astype(BF16)
        st_ref[0:1, :] += jnp.sum(dh * hn, axis=0, keepdims=True)
        st_ref[1:2, :] += jnp.broadcast_to(loss, (1, D_MODEL))
        dhn = dh * pg
        dz = (rz * (dhn - hn * jnp.mean(dhn * hn, axis=-1, keepdims=True))).astype(BF16)
        dy = lax.dot_general(dz, w, _NT, preferred_element_type=F32)
        dw_ref[...] += lax.dot_general(y, dz, _TN, preferred_element_type=F32)
        dg_ref[...] = (dy * o_all * (sig * (1.0 + g * (1.0 - sig)))).astype(BF16)
        d_o = (dy * silu).astype(BF16)
        prod = d_o.astype(F32) * o_all
        doa_ref[...] = d_o[:, 0:A_W]
        sa_ref[...] = _with_delta(la_ref[...], prod[:, 0:A_W])
        doc_ref[...] = d_o[:, A_W + B_W:D_MODEL]
        sc_ref[...] = _with_delta(lc_ref[...], prod[:, A_W + B_W:D_MODEL])
        d_ob = d_o[:, A_W:A_W + B_W]
        stat_b = _with_delta(lse_b_tile, prod[:, A_W:A_W + B_W])
        dob1_ref[...] = d_ob
        sb1_ref[...] = stat_b
        _put(scr_do, d_ob.astype(F32))
        _put(scr_sb, stat_b)
        _store_permuted(scr_do, dob4_ref, 4, BF16)
        _store_permuted(scr_sb, sb4_ref, 4, F32)
        _store_permuted(scr_do, dob16_ref, 16, BF16)
        _store_permuted(scr_sb, sb16_ref, 16, F32)

    p4 = lambda w: _perm_spec(tm, 4, w)
    p16 = lambda w: _perm_spec(tm, 16, w)
    in_specs = [_row(tm, A_W), _row(tm, LANES), _row(tm, B_W), _row(tm, LANES), p4(B_W), p4(LANES), p16(B_W), p16(LANES),
                _row(tm, C_W), _row(tm, LANES), _row(tm, D_MODEL), _row(tm, D_MODEL), _row(tm, D_MODEL),
                _full((D_MODEL, D_MODEL)), _full((1, D_MODEL))]
    sds = jax.ShapeDtypeStruct
    v4 = lambda w, dt: sds((seq // (BLOCK * 4), 4, BLOCK, w), dt)
    v16 = lambda w, dt: sds((seq // (BLOCK * 16), 16, BLOCK, w), dt)
    out_specs = [_row(tm, D_MODEL), _row(tm, D_MODEL), _row(tm, A_W), _row(tm, LANES), _row(tm, B_W), _row(tm, LANES),
                 p4(B_W), p4(LANES), p16(B_W), p16(LANES), _row(tm, C_W), _row(tm, LANES),
                 _full((D_MODEL, D_MODEL)), _full((8, D_MODEL))]
    out_shape = [sds((seq, D_MODEL), BF16), sds((seq, D_MODEL), BF16), sds((seq, A_W), BF16), sds((seq, LANES), F32),
                 sds((seq, B_W), BF16), sds((seq, LANES), F32), v4(B_W, BF16), v4(LANES, F32), v16(B_W, BF16),
                 v16(LANES, F32), sds((seq, C_W), BF16), sds((seq, LANES), F32),
                 sds((D_MODEL, D_MODEL), F32), sds((8, D_MODEL), F32)]
    res = pl.pallas_call(
        body, name="mid", grid=(seq // tm,), in_specs=in_specs, out_specs=out_specs, out_shape=out_shape,
        scratch_shapes=[pltpu.VMEM((n_b, tm, LANES), F32), pltpu.VMEM((n_b, tm, LANES), F32),
                        pltpu.VMEM((1, tm, LANES), F32), pltpu.VMEM((1, tm, LANES), F32),
                        pltpu.VMEM((n_b, tm, LANES), F32), pltpu.VMEM((1, tm, LANES), F32)],
        compiler_params=_params(dimension_semantics=("arbitrary",)),
    )(oa, lse_a, ob[1], lse_b[1], _perm_view(ob[4], 4), _perm_view(lse_b[4], 4), _perm_view(ob[16], 16),
      _perm_view(lse_b[16], 16), oc, lse_c, gate, x, target, w_out_full, post_g)
    dh, d_gate, do_a, st_a, do_b1, st_b1, do_b4, st_b4, do_b16, st_b16, do_c, st_c, d_wout, stats = res
    flat = lambda t: t.reshape(seq, t.shape[-1])
    d_b = {1: (do_b1, st_b1), 4: (flat(do_b4), flat(st_b4)), 16: (flat(do_b16), flat(st_b16))}
    return dh, d_gate, (do_a, st_a), d_b, (do_c, st_c), d_wout, stats


def _inproj_bwd_x(x, dh, pre_g, w_in_full, tabs, dqa, dka, dva, dqkv_b, dqc, dgate, tm=512):
    seq = x.shape[0]
    n_b = B_W // LANES

    def body(x_ref, dh_ref, g_ref, w_hbm, c_ref, up_ref, dn_ref, dqa_ref, dka_ref, dva_ref,
             dq1, dk1, dv1, dq4, dk4, dv4, dq16, dk16, dv16, dqc_ref, dg_ref,
             gx_ref, dp_ref, st_ref, scr4, scr16, w_scr, w_sems):
        _stage_w_in(w_hbm, w_scr, w_sems)

        @pl.when(pl.program_id(0) == 0)
        def _():
            st_ref[...] = jnp.zeros_like(st_ref)

        c, up, dn = c_ref[...], -up_ref[...], -dn_ref[...]
        unrot = lambda t: _rotate(t, c, up, dn)
        total = lambda r1, r4, r16: (r1[...].astype(F32) + _load_permuted(r4, scr4, 4)
                                     + _load_permuted(r16, scr16, 16))
        dp_ref[:, 0:384] = (unrot(dqa_ref[...].astype(F32)) * SCALE).astype(BF16)
        dp_ref[:, 384:512] = unrot(dka_ref[...].astype(F32)).astype(BF16)
        dp_ref[:, 512:640] = dva_ref[...]
        dp_ref[:, 640:1024] = dg_ref[:, 0:384]
        dp_ref[:, 1024:1408] = (unrot(total(dq1, dq4, dq16)) * SCALE).astype(BF16)
        dp_ref[:, 1408:1792] = unrot(total(dk1, dk4, dk16)).astype(BF16)
        dp_ref[:, 1792:2176] = total(dv1, dv4, dv16).astype(BF16)
        dp_ref[:, 2176:2560] = dg_ref[:, 384:768]
        dp_ref[:, 2560:2816] = (dqc_ref[...].astype(F32) * SCALE).astype(BF16)
        dp_ref[:, 2816:3072] = dg_ref[:, 768:1024]
        du = lax.dot_general(dp_ref[...], w_scr[...], _NT, preferred_element_type=F32)
        xv = x_ref[...]
        r = lax.rsqrt(jnp.mean(xv * xv, axis=-1, keepdims=True) + RMS_EPS)
        xh = xv * r
        st_ref[0:1, :] += jnp.sum(du * xh, axis=0, keepdims=True)
        dxh = du * g_ref[...]
        gx_ref[...] = dh_ref[...].astype(F32) + r * (dxh - xh * jnp.mean(dxh * xh, axis=-1, keepdims=True))

    in_specs = ([_row(tm, D_MODEL), _row(tm, D_MODEL), _full((1, D_MODEL)), pl.BlockSpec(memory_space=pl.ANY),
                 _row(tm, LANES), _row(tm, LANES), _row(tm, LANES), _row(tm, A_W), _row(tm, A_KV_W), _row(tm, A_KV_W)]
                + [_row(tm, B_W)] * 3 + [_perm_spec(tm, 4, B_W)] * 3 + [_perm_spec(tm, 16, B_W)] * 3
                + [_row(tm, C_W), _row(tm, D_MODEL)])
    return pl.pallas_call(
        body, name="inproj_bwd_x", grid=(seq // tm,), in_specs=in_specs,
        out_specs=[_row(tm, D_MODEL), _row(tm, D_IN), _full((8, D_MODEL))],
        out_shape=[jax.ShapeDtypeStruct((seq, D_MODEL), F32), jax.ShapeDtypeStruct((seq, D_IN), BF16),
                   jax.ShapeDtypeStruct((8, D_MODEL), F32)],
        scratch_shapes=[pltpu.VMEM((n_b, tm, LANES), F32), pltpu.VMEM((n_b, tm, LANES), F32)] + _w_in_scratch(),
        compiler_params=_params(dimension_semantics=("arbitrary",)),
    )(x, dh, pre_g, w_in_full, *tabs, dqa, dka, dva, *dqkv_b[1], *[_perm_view(t, 4) for t in dqkv_b[4]],
      *[_perm_view(t, 16) for t in dqkv_b[16]], dqc, dgate)


class _ReduceScatter:
    def __init__(self, ins, outs, scratch):
        self.n = n = len(ins)
        self.ins, self.outs = ins, outs
        self.mine, self.got, self.snd, self.rcv = (scratch[n * t:n * (t + 1)] for t in range(4))
        self.load_sems, self.d2d_send, self.d2d_recv, self.ici_send, self.ici_recv = scratch[4 * n:]
        self.pos = _mesh_pos()
        self.pairs = [(a, kk) for kk in (3, 1, 2) for a in range(n)]

    @staticmethod
    def scratch_shapes(shapes):
        return ([pltpu.VMEM((4,) + s, F32) for s in shapes] + [pltpu.VMEM((4,) + s, F32) for s in shapes]
                + [pltpu.VMEM((3,) + s, BF16) for s in shapes] + [pltpu.VMEM((3,) + s, BF16) for s in shapes]
                + [pltpu.SemaphoreType.DMA((len(shapes), 4))] * 5)

    def _chip(self, kk):
        x, y, _ = self.pos
        return (1 - x if kk & 2 else x, 1 - y if kk & 1 else y)

    def _load(self, a, kk):
        block = _dev_index((*self._chip(kk), self.pos[2]))
        return pltpu.make_async_copy(self.ins[a].at[block], self.mine[a].at[kk], self.load_sems.at[a, kk])

    def _swap(self, a, kk):
        x, y, c = self.pos
        return pltpu.make_async_remote_copy(
            src_ref=self.ins[a].at[_dev_index((*self._chip(kk), 1 - c))], dst_ref=self.got[a].at[kk],
            send_sem=self.d2d_send.at[a, kk], recv_sem=self.d2d_recv.at[a, kk],
            device_id=(x, y, 1 - c), device_id_type=MESH_ID)

    def _hop(self, a, kk):
        return pltpu.make_async_remote_copy(
            src_ref=self.snd[a].at[kk - 1], dst_ref=self.rcv[a].at[kk - 1], send_sem=self.ici_send.at[a, kk],
            recv_sem=self.ici_recv.at[a, kk], device_id=(*self._chip(kk), self.pos[2]), device_id_type=MESH_ID)

    def start(self):
        for kk in range(4):
            for a in range(self.n):
                self._load(a, kk).start()
                self._swap(a, kk).start()

    def send_chip_sums(self):
        for a, kk in self.pairs:
            self._load(a, kk).wait()
            self._swap(a, kk).wait_recv()
            self.snd[a][kk - 1] = (self.mine[a][kk] + self.got[a][kk]).astype(BF16)
            self._hop(a, kk).start()

    def finish(self):
        for a in range(self.n):
            self._load(a, 0).wait()
            self._swap(a, 0).wait_recv()
            acc = self.mine[a][0] + self.got[a][0]
            for kk in (1, 2, 3):
                self._hop(a, kk).wait_recv()
                acc = acc + self.rcv[a][kk - 1].astype(F32)
            self.outs[a][...] = acc
        for kk in range(4):
            for a in range(self.n):
                self._swap(a, kk).wait_send()
        for a, kk in self.pairs:
            self._hop(a, kk).wait_send()


def _inproj_bwd_w(u, dproj, d_wmem, d_wout, tm=1024):
    seq = u.shape[0]

    n_steps = seq // tm
    shapes = [d_wmem.shape[1:], d_wout.shape[1:]]

    def body(u_ref, dp_ref, wm, wo, dw_ref, g_wm, g_wo, *scratch):
        step = pl.program_id(0)
        exchange = _ReduceScatter((wm, wo), (g_wm, g_wo), scratch)

        @pl.when(step == 0)
        def _():
            dw_ref[...] = jnp.zeros_like(dw_ref)
            exchange.start()

        @pl.when(step == min(1, n_steps - 1))
        def _():
            exchange.send_chip_sums()

        res = lax.dot_general(u_ref[...], dp_ref[...], _TN, preferred_element_type=F32)
        for k in range(N_DEV):
            dw_ref[k] += res[:, SHARD_IN * k:SHARD_IN * (k + 1)]

        @pl.when(step == n_steps - 1)
        def _():
            exchange.finish()

    hbm = pl.BlockSpec(memory_space=pl.ANY)
    return pl.pallas_call(
        body, name="inproj_bwd_w", grid=(n_steps,),
        in_specs=[_row(tm, D_MODEL), _row(tm, D_IN), hbm, hbm],
        out_specs=[_full((N_DEV, D_MODEL, SHARD_IN))] + [_full(s) for s in shapes],
        out_shape=[jax.ShapeDtypeStruct((N_DEV, D_MODEL, SHARD_IN), F32)] + [jax.ShapeDtypeStruct(s, F32) for s in shapes],
        scratch_shapes=_ReduceScatter.scratch_shapes(shapes),
        compiler_params=_params(dimension_semantics=("arbitrary",)),
    )(u, dproj, d_wmem, d_wout)


def _local_step(x, mem, pre_g, w_in, sink, mem_g, w_mem, w_out, post_g, target):
    u, *tabs, w_in_full = _prep(x, pre_g, w_in)
    qa, ka, va, qkv_b, qc, gate, w_mem_all, w_out_all = _inproj(u, w_in_full, tabs, w_mem, w_out)
    w_mem_full = w_mem_all.reshape(D_MODEL, 2 * C_W)
    w_out_full = w_out_all.reshape(D_MODEL, D_MODEL)
    mn, mk, mv = _memkv_fwd(mem, mem_g, w_mem_full)

    a_cfg = dict(dil=1, heads=A_HEADS, group=A_GROUP, max_dist=BLOCK - 1, nq=ATTN_BLOCKS_PER_STEP)
    b_cfgs = {dil: dict(dil=dil, heads=B_HEADS, group=1, max_dist=win // dil, nq=ATTN_BLOCKS_PER_STEP)
              for win, dil in B_CONFIGS}
    oa, lse_a = _banded_fwd(qa, ka, va, sink, name="attn_a_fwd", **a_cfg)
    ob, lse_b = {}, {}
    for dil, cfg in b_cfgs.items():
        ob[dil], lse_b[dil] = _banded_fwd(*qkv_b[dil], None, name=f"attn_b{dil}_fwd", **cfg)
    oc, lse_c = _cross_fwd(qc, mk, mv)

    dh, d_gate, d_a, d_b, d_c, d_wout, st_mid = _mid(oa, lse_a, ob, lse_b, oc, lse_c, gate, x, target, w_out_full, post_g)

    dqa, dka, dva, dsink = _banded_bwd(qa, ka, va, *d_a, sink, name="attn_a_bwd", **a_cfg)
    dqkv_b = {dil: _banded_bwd(*qkv_b[dil], *d_b[dil], None, name=f"attn_b{dil}_bwd", **cfg)
              for dil, cfg in b_cfgs.items()}
    dqc, dmk, dmv = _cross_bwd(qc, mk, mv, *d_c)
    d_wmem, st_mem = _memkv_bwd(mem, mem_g, mn, w_mem_full, dmk, dmv)

    grad_x, dproj, st_pre = _inproj_bwd_x(x, dh, pre_g, w_in_full, tabs, dqa, dka, dva, dqkv_b, dqc, d_gate)
    d_win, g_wmem, g_wout = _inproj_bwd_w(u, dproj, d_wmem.reshape(N_DEV, SHARD_ROWS, 2 * C_W),
                                          d_wout.reshape(N_DEV, SHARD_ROWS, D_MODEL))

    dsink_row = jnp.pad(dsink[0:1, :], ((0, 0), (0, D_MODEL - LANES)))
    stats = jnp.concatenate([st_pre[0:1], st_mem[0:1], st_mid[0:1], dsink_row, st_mid[1:2],
                             jnp.zeros((3, D_MODEL), F32)], axis=0)
    return grad_x, d_win, g_wmem, g_wout, stats


def _prep(x, pre_g, w_in, tm=1024, parts=2):
    seq = x.shape[0]
    n_steps = seq // tm
    rows = D_MODEL // parts
    pass_on_at = [max(n_steps - parts + a, 0) for a in range(parts)]
    j = jnp.arange(LANES) % HEAD_DIM
    freq = (ROPE_THETA ** (-(2 * (j % (ROT_DIM // 2))).astype(F32) / ROT_DIM))[None, :]

    def body(x_ref, g_ref, f_ref, win_ref, u_ref, c_ref, up_ref, dn_ref, win_out, win_b,
             send_sems, recv_sems, local_sems):
        step = pl.program_id(0)
        px, py, pc = _mesh_pos()
        me, sibling = (px, py, pc), (px, py, 1 - pc)
        chips = [(1 - px, py), (px, 1 - py), (1 - px, 1 - py)]

        def src(a):
            return win_b.at[pl.ds(rows * a, rows)]

        def slot(a, p):
            return win_out.at[_dev_index(p), pl.ds(rows * a, rows)]

        def copy(a, k, block, to, own=False):
            return pltpu.make_async_remote_copy(
                src_ref=src(a) if own else slot(a, block), dst_ref=slot(a, block),
                send_sem=send_sems.at[a, k], recv_sem=recv_sems.at[a, k], device_id=to, device_id_type=MESH_ID)

        def first_sends(a):
            return [copy(a, 1 + k, me, (*chip, pc), own=True) for k, chip in enumerate(chips)] + [
                copy(a, 0, me, sibling, own=True)]

        def local(a):
            return pltpu.make_async_copy(src(a), slot(a, me), local_sems.at[a])

        @pl.when(step == 0)
        def _():
            win_b[...] = win_ref[...].astype(BF16)
            for a in range(parts):
                local(a).start()
                for cp in first_sends(a):
                    cp.start()

        for a in range(parts):
            @pl.when(step == pass_on_at[a])
            def _(a=a):
                for k, chip in enumerate(chips):
                    copy(a, 1 + k, (*chip, pc), me).wait_recv()
                    copy(a, 4 + k, (*chip, pc), sibling).start()

        xv = x_ref[...]
        r = lax.rsqrt(jnp.mean(xv * xv, axis=-1, keepdims=True) + RMS_EPS)
        u_ref[...] = ((xv * r) * g_ref[...]).astype(BF16)
        pos = (lax.broadcasted_iota(jnp.int32, (tm, LANES), 0) + step * tm).astype(F32)
        head_lane = lax.broadcasted_iota(jnp.int32, (tm, LANES), 1) % HEAD_DIM
        ang = pos * f_ref[...]
        cos, sin = jnp.cos(ang), jnp.sin(ang)
        half = ROT_DIM // 2
        c_ref[...] = jnp.where(head_lane < ROT_DIM, cos, 1.0)
        up_ref[...] = jnp.where((head_lane >= half) & (head_lane < ROT_DIM), sin, 0.0)
        dn_ref[...] = jnp.where(head_lane < half, -sin, 0.0)

        @pl.when(step == n_steps - 1)
        def _():
            for a in range(parts):
                copy(a, 0, sibling, me).wait_recv()
                for k, chip in enumerate(chips):
                    copy(a, 4 + k, (*chip, 1 - pc), me).wait_recv()
            for a in range(parts):
                for cp in first_sends(a):
                    cp.wait_send()
                for k, chip in enumerate(chips):
                    copy(a, 4 + k, (*chip, pc), sibling).wait_send()
                local(a).wait()

    return pl.pallas_call(
        body, name="prep", grid=(n_steps,),
        in_specs=[_row(tm, D_MODEL), _full((1, D_MODEL)), _full((1, LANES)), _full(w_in.shape)],
        out_specs=[_row(tm, D_MODEL), _row(tm, LANES), _row(tm, LANES), _row(tm, LANES),
                   pl.BlockSpec(memory_space=pl.ANY)],
        out_shape=[jax.ShapeDtypeStruct((seq, D_MODEL), BF16)] + [jax.ShapeDtypeStruct((seq, LANES), F32)] * 3
        + [jax.ShapeDtypeStruct((N_DEV,) + w_in.shape, BF16)],
        scratch_shapes=[pltpu.VMEM(w_in.shape, BF16), pltpu.SemaphoreType.DMA((parts, 7)),
                        pltpu.SemaphoreType.DMA((parts, 7)), pltpu.SemaphoreType.DMA((parts,))],
        compiler_params=_params(dimension_semantics=("arbitrary",)),
    )(x, pre_g, freq, w_in)


def _exchange_grads(d_win, stats):
    def body(win, st, g_win, r_st, send_sems, recv_sems, local_sem, *scratch):
        exchange = _ReduceScatter((win,), (g_win,), scratch)
        exchange.start()
        pos = _mesh_pos()
        me = _dev_index(pos)
        own = pltpu.make_async_copy(st, r_st.at[me], local_sem)
        own.start()
        copies = []
        for s in range(1, N_DEV):
            peer = _xor_peer(pos, s)
            mk = lambda slot: pltpu.make_async_remote_copy(
                src_ref=st, dst_ref=r_st.at[slot], send_sem=send_sems.at[s], recv_sem=recv_sems.at[s],
                device_id=peer, device_id_type=MESH_ID)
            send, arrival = mk(me), mk(_dev_index(peer))
            send.start()
            copies.append((send, arrival))
        exchange.send_chip_sums()
        exchange.finish()
        for send, arrival in copies:
            arrival.wait_recv()
            send.wait_send()
        own.wait()

    hbm = pl.BlockSpec(memory_space=pl.ANY)
    shard = d_win.shape[1:]
    return pl.pallas_call(
        body, name="exchange_grads", in_specs=[hbm, hbm],
        out_specs=[pl.BlockSpec(memory_space=pltpu.VMEM), hbm],
        out_shape=[jax.ShapeDtypeStruct(shard, F32), jax.ShapeDtypeStruct((N_DEV,) + stats.shape, F32)],
        scratch_shapes=[pltpu.SemaphoreType.DMA((N_DEV,)), pltpu.SemaphoreType.DMA((N_DEV,)), pltpu.SemaphoreType.DMA(())]
        + _ReduceScatter.scratch_shapes([shard]),
        compiler_params=_params(),
    )(d_win, stats)


WEIGHT_ORDER = ("pre_norm", "w_in", "sink_a", "mem_norm", "w_mem_kv", "w_out", "post_norm")


def _adamw_all(grads, r_stats, weights, moments_m, moments_v):
    n = len(WEIGHT_ORDER)
    stat_row = {"pre_norm": 0, "mem_norm": 1, "post_norm": 2, "sink_a": 3}

    def body(*refs):
        gw_in, gw_mem, gw_out, st_ref = refs[0:4]
        w_refs, m_refs, v_refs = (dict(zip(WEIGHT_ORDER, refs[4 + n * t:4 + n * (t + 1)])) for t in range(3))
        loss_ref = refs[4 + 3 * n]
        outs = refs[5 + 3 * n:]
        g_small = st_ref[0]
        for s in range(1, N_DEV):
            g_small = g_small + st_ref[s]
        loss_ref[...] = g_small[4:5, 0:1]
        big = {"w_in": gw_in, "w_mem_kv": gw_mem, "w_out": gw_out}
        for i, name in enumerate(WEIGHT_ORDER):
            if name in big:
                g = big[name][...]
                at = lambda ref: ref[0]
            else:
                width = w_refs[name].shape[-1]
                g = g_small[stat_row[name]:stat_row[name] + 1, 0:width]
                at = lambda ref: ref[...]
            m2 = ADAM_B1 * at(m_refs[name]) + (1.0 - ADAM_B1) * g
            v2 = ADAM_B2 * at(v_refs[name]) + (1.0 - ADAM_B2) * (g * g)
            m_hat = m2 / (1.0 - ADAM_B1 ** ADAM_STEP)
            v_hat = v2 / (1.0 - ADAM_B2 ** ADAM_STEP)
            delta = -ADAM_LR * (m_hat / (jnp.sqrt(v_hat) + ADAM_EPS) + ADAM_WD * at(w_refs[name]))
            for kind, val in enumerate((g, delta, m2, v2)):
                out = outs[kind * n + i]
                if name in big:
                    out[0] = val
                else:
                    out[...] = val

    shapes = [weights[name].shape for name in WEIGHT_ORDER]
    res = pl.pallas_call(
        body, name="adamw_all",
        out_shape=[jax.ShapeDtypeStruct((1, 1), F32)] + [jax.ShapeDtypeStruct(sh, F32) for sh in shapes] * 4,
        compiler_params=_params(),
    )(grads["w_in"], grads["w_mem_kv"], grads["w_out"], r_stats,
      *[weights[k] for k in WEIGHT_ORDER], *[moments_m[k] for k in WEIGHT_ORDER], *[moments_v[k] for k in WEIGHT_ORDER])
    return res[0].reshape(()), res[1:]


def kernel(x, mem, pre_norm, w_in, sink_a, mem_norm, w_mem_kv, w_out, post_norm, loss_target, m_pre_norm, m_w_in, m_sink_a, m_mem_norm, m_w_mem_kv, m_w_out, m_post_norm, v_pre_norm, v_w_in, v_sink_a, v_mem_norm, v_w_mem_kv, v_w_out, v_post_norm):
    sink = jnp.pad(sink_a[0], (0, 8 - A_HEADS))
    grad_x, d_win, g_wmem, g_wout, stats = _local_step(
        x[0], mem[0], pre_norm, w_in[0], sink, mem_norm, w_mem_kv[0], w_out[0], post_norm, loss_target[0])
    g_win, r_stats = _exchange_grads(d_win, stats)
    weights = dict(pre_norm=pre_norm, w_in=w_in, sink_a=sink_a, mem_norm=mem_norm, w_mem_kv=w_mem_kv, w_out=w_out,
                   post_norm=post_norm)
    moments_m = dict(pre_norm=m_pre_norm, w_in=m_w_in, sink_a=m_sink_a, mem_norm=m_mem_norm, w_mem_kv=m_w_mem_kv,
                     w_out=m_w_out, post_norm=m_post_norm)
    moments_v = dict(pre_norm=v_pre_norm, w_in=v_w_in, sink_a=v_sink_a, mem_norm=v_mem_norm, w_mem_kv=v_w_mem_kv,
                     w_out=v_w_out, post_norm=v_post_norm)
    loss, rest = _adamw_all(dict(w_in=g_win, w_mem_kv=g_wmem, w_out=g_wout), r_stats, weights, moments_m, moments_v)
    return (loss, grad_x[None], *rest)
```

```python
import jax
import jax.numpy as jnp
from jax import lax
from jax.experimental import pallas as pl
from jax.experimental.pallas import tpu as pltpu

F32 = jnp.float32
BF16 = jnp.bfloat16

D_MODEL = 1024
HEAD_DIM = 64
ROT_DIM = 16
ROPE_THETA = 500000.0
BLOCK = 128
LANES = 128
N_MEM = 256
RMS_EPS = 1e-6
SCALE = HEAD_DIM ** -0.5
A_HEADS, A_GROUP = 6, 3
B_HEADS = 6
C_HEADS = 4
A_W, A_KV_W, B_W, C_W = 384, 128, 384, 256
D_IN = 3072
N_DEV = 8
SHARD_IN = D_IN // N_DEV
SHARD_ROWS = D_MODEL // N_DEV
B_CONFIGS = ((128, 1), (512, 4), (2048, 16))
DILS = (4, 16)
NEG = -1e30
ATTN_BLOCKS_PER_STEP = 4
DELTA_LANE = 64
VMEM_LIMIT = 56 * 1024 * 1024

ADAM_LR, ADAM_B1, ADAM_B2, ADAM_EPS, ADAM_WD, ADAM_STEP = 0.001, 0.9, 0.999, 1e-08, 0.01, 10
MESH_ID = pl.DeviceIdType.MESH


def _params(**kw):
    return pltpu.CompilerParams(vmem_limit_bytes=VMEM_LIMIT, **kw)


def _full(shape):
    n = len(shape)
    return pl.BlockSpec(shape, lambda *_: (0,) * n)


def _row(tm, w):
    return pl.BlockSpec((tm, w), lambda i: (i, 0))


def _mesh_pos():
    return lax.axis_index("x"), lax.axis_index("y"), lax.axis_index("c")


def _dev_index(pos):
    return 4 * pos[0] + 2 * pos[1] + pos[2]


def _xor_peer(pos, s):
    x, y, c = pos
    return (1 - x if s & 4 else x, 1 - y if s & 2 else y, 1 - c if s & 1 else c)


def _perm_view(a, dil):
    return a.reshape(a.shape[0] // (BLOCK * dil), dil, BLOCK, a.shape[1])


def _perm_spec(tm, dil, w):
    chunk = BLOCK * dil
    if tm >= chunk:
        return pl.BlockSpec((tm // chunk, dil, BLOCK, w), lambda i: (i, 0, 0, 0))
    per = chunk // tm
    return pl.BlockSpec((1, dil, tm // dil, w), lambda i: (i // per, 0, i % per, 0))


def _put(scr, val):
    for c in range(val.shape[1] // LANES):
        scr[c] = val[:, LANES * c:LANES * (c + 1)]


def _get(scr):
    n = scr.shape[0]
    return scr[0] if n == 1 else jnp.concatenate([scr[c] for c in range(n)], axis=1)


def _get_class(scr, r, dil):
    n, rows = scr.shape[0], scr.shape[1]
    parts = [scr.at[c][pl.ds(r, rows // dil, stride=dil), :] for c in range(n)]
    return parts[0] if n == 1 else jnp.concatenate(parts, axis=1)


def _store_permuted(scr, out_ref, dil, dtype):
    for r in range(dil):
        out_ref[0, r] = _get_class(scr, r, dil).astype(dtype)


def _fill_permuted(in_ref, scr, dil):
    n, rows = scr.shape[0], scr.shape[1]
    for r in range(dil):
        val = in_ref[0, r].astype(F32)
        for c in range(n):
            scr.at[c][pl.ds(r, rows // dil, stride=dil), :] = val[:, LANES * c:LANES * (c + 1)]


def _load_permuted(in_ref, scr, dil):
    _fill_permuted(in_ref, scr, dil)
    return _get(scr)


def _rotate128(t, c, up, dn):
    return t * c + pltpu.roll(t, 8, 1) * up + pltpu.roll(t, LANES - 8, 1) * dn


def _rotate(t, c, up, dn):
    outs = [_rotate128(t[:, LANES * j:LANES * (j + 1)], c, up, dn) for j in range(t.shape[1] // LANES)]
    return outs[0] if len(outs) == 1 else jnp.concatenate(outs, axis=1)


def _w_in_scratch():
    return [pltpu.VMEM((D_MODEL, D_IN), BF16), pltpu.SemaphoreType.DMA((N_DEV,))]


def _stage_w_in(w_hbm, w_scr, sems):
    @pl.when(pl.program_id(0) == 0)
    def _():
        copies = [pltpu.make_async_copy(w_hbm.at[k], w_scr.at[:, pl.ds(SHARD_IN * k, SHARD_IN)], sems.at[k])
                  for k in range(N_DEV)]
        for cp in copies:
            cp.start()
        for cp in copies:
            cp.wait()


def _inproj(u, w_in_full, tabs, w_mem, w_out, tm=1024):
    seq = u.shape[0]
    n_chunk = D_IN // LANES
    n_steps = seq // tm

    def body(u_ref, w_hbm, c_ref, up_ref, dn_ref, wm_ref, wo_ref, qa_ref, ka_ref, va_ref,
             qb1_ref, kb1_ref, vb1_ref, qb4_ref, kb4_ref, vb4_ref, qb16_ref, kb16_ref, vb16_ref,
             qc_ref, gate_ref, wm_all, wo_all, proj, w_scr, w_sems, wm_b, wo_b, send_sems, recv_sems, local_sems):
        step = pl.program_id(0)
        shards, gathered = (wm_b, wo_b), (wm_all, wo_all)

        def gather_copies(arriving):
            pos = _mesh_pos()
            me = _dev_index(pos)
            local = [] if arriving else [
                pltpu.make_async_copy(shards[a], gathered[a].at[me], local_sems.at[a]) for a in range(2)]
            remote = []
            for s in range(1, N_DEV):
                peer = _xor_peer(pos, s)
                for a in range(2):
                    remote.append(pltpu.make_async_remote_copy(
                        src_ref=shards[a], dst_ref=gathered[a].at[_dev_index(peer) if arriving else me],
                        send_sem=send_sems.at[a, s], recv_sem=recv_sems.at[a, s], device_id=peer,
                        device_id_type=MESH_ID))
            return local, remote

        @pl.when(step == 0)
        def _():
            wm_b[...] = wm_ref[...].astype(BF16)
            wo_b[...] = wo_ref[...].astype(BF16)
            local, sends = gather_copies(arriving=False)
            for cp in local + sends:
                cp.start()

        _stage_w_in(w_hbm, w_scr, w_sems)
        u = u_ref[...]
        for n0 in range(0, D_IN, D_MODEL):
            acc = jnp.dot(u, w_scr[:, n0:n0 + D_MODEL], preferred_element_type=F32)
            for c3 in range(D_MODEL // LANES):
                proj[n0 // LANES + c3] = acc[:, LANES * c3:LANES * (c3 + 1)]
        c, up, dn = c_ref[...], up_ref[...], dn_ref[...]

        def cols(lo, hi, rot=False, scale=None):
            parts = []
            for ch in range(lo // LANES, hi // LANES):
                t = proj[ch]
                if rot:
                    t = _rotate128(t, c, up, dn)
                if scale is not None:
                    t = t * scale
                parts.append(t)
            return parts[0] if len(parts) == 1 else jnp.concatenate(parts, axis=1)

        qa_ref[...] = cols(0, 384, True, SCALE).astype(BF16)
        ka_ref[...] = cols(384, 512, True).astype(BF16)
        va_ref[...] = cols(512, 640).astype(BF16)
        gate_ref[:, 0:384] = cols(640, 1024).astype(BF16)
        gate_ref[:, 384:768] = cols(2176, 2560).astype(BF16)
        gate_ref[:, 768:1024] = cols(2816, 3072).astype(BF16)
        qc_ref[...] = cols(2560, 2816, False, SCALE).astype(BF16)
        for ch in range(1024 // LANES, 1408 // LANES):
            proj[ch] = _rotate128(proj[ch], c, up, dn) * SCALE
        for ch in range(1408 // LANES, 1792 // LANES):
            proj[ch] = _rotate128(proj[ch], c, up, dn)
        for lo, nat, p4, p16 in ((1024, qb1_ref, qb4_ref, qb16_ref), (1408, kb1_ref, kb4_ref, kb16_ref),
                                 (1792, vb1_ref, vb4_ref, vb16_ref)):
            chunks = range(lo // LANES, lo // LANES + B_W // LANES)
            nat[...] = jnp.concatenate([proj[ch] for ch in chunks], axis=1).astype(BF16)
            for dil, ref in ((4, p4), (16, p16)):
                span = min(tm, BLOCK * dil)
                for cc in range(tm // span):
                    for rr in range(dil):
                        ref[cc, rr] = jnp.concatenate(
                            [proj.at[ch][pl.ds(cc * span + rr, span // dil, stride=dil), :] for ch in chunks],
                            axis=1).astype(BF16)

        @pl.when(step == n_steps - 1)
        def _():
            for cp in gather_copies(arriving=True)[1]:
                cp.wait_recv()
            local, sends = gather_copies(arriving=False)
            for cp in sends:
                cp.wait_send()
            for cp in local:
                cp.wait()

    nat_w = (A_W, A_KV_W, A_KV_W, B_W, B_W, B_W)
    out_specs = [_row(tm, w) for w in nat_w]
    out_shape = [jax.ShapeDtypeStruct((seq, w), BF16) for w in nat_w]
    for dil in DILS:
        out_specs += [_perm_spec(tm, dil, B_W)] * 3
        out_shape += [jax.ShapeDtypeStruct((seq // (BLOCK * dil), dil, BLOCK, B_W), BF16)] * 3
    hbm = pl.BlockSpec(memory_space=pl.ANY)
    out_specs += [_row(tm, C_W), _row(tm, D_MODEL), hbm, hbm]
    out_shape += [jax.ShapeDtypeStruct((seq, C_W), BF16), jax.ShapeDtypeStruct((seq, D_MODEL), BF16),
                  jax.ShapeDtypeStruct((N_DEV,) + w_mem.shape, BF16), jax.ShapeDtypeStruct((N_DEV,) + w_out.shape, BF16)]
    res = pl.pallas_call(
        body, name="inproj", grid=(n_steps,),
        in_specs=[_row(tm, D_MODEL), hbm, _row(tm, LANES), _row(tm, LANES), _row(tm, LANES),
                  _full(w_mem.shape), _full(w_out.shape)],
        out_specs=out_specs, out_shape=out_shape,
        scratch_shapes=[pltpu.VMEM((n_chunk, tm, LANES), F32)] + _w_in_scratch()
        + [pltpu.VMEM(w_mem.shape, BF16), pltpu.VMEM(w_out.shape, BF16), pltpu.SemaphoreType.DMA((2, N_DEV)),
           pltpu.SemaphoreType.DMA((2, N_DEV)), pltpu.SemaphoreType.DMA((2,))],
        compiler_params=_params(dimension_semantics=("arbitrary",)),
    )(u, w_in_full, *tabs, w_mem, w_out)
    qa, ka, va = res[0:3]
    qkv_b = {1: res[3:6], 4: [t.reshape(seq, B_W) for t in res[6:9]], 16: [t.reshape(seq, B_W) for t in res[9:12]]}
    return qa, ka, va, qkv_b, res[12], res[13], res[14], res[15]


def _memkv_fwd(mem, mem_g, w_mem_full):
    def body(mem_ref, g_ref, w_ref, mn_ref, mk_ref, mv_ref):
        mv_ = mem_ref[...]
        r = lax.rsqrt(jnp.mean(mv_ * mv_, axis=-1, keepdims=True) + RMS_EPS)
        mn = ((mv_ * r) * g_ref[...]).astype(BF16)
        mn_ref[...] = mn
        mkv = jnp.dot(mn, w_ref[...], preferred_element_type=F32)
        mk_ref[...] = mkv[:, 0:C_W].astype(BF16)
        mv_ref[...] = mkv[:, C_W:2 * C_W].astype(BF16)

    return pl.pallas_call(
        body, name="memkv_fwd",
        out_shape=[jax.ShapeDtypeStruct((N_MEM, D_MODEL), BF16),
                   jax.ShapeDtypeStruct((N_MEM, C_W), BF16), jax.ShapeDtypeStruct((N_MEM, C_W), BF16)],
        compiler_params=_params(),
    )(mem, mem_g, w_mem_full)


def _memkv_bwd(mem, mem_g, mn, w_mem_full, dmk, dmv):
    def body(mem_ref, g_ref, mn_ref, w_ref, dmk_ref, dmv_ref, dw_ref, st_ref):
        dmkv = jnp.concatenate([dmk_ref[...], dmv_ref[...]], axis=1).astype(BF16)
        dw_ref[...] = lax.dot_general(mn_ref[...], dmkv, (((0,), (0,)), ((), ())), preferred_element_type=F32)
        dmn = lax.dot_general(dmkv, w_ref[...], (((1,), (1,)), ((), ())), preferred_element_type=F32)
        mv_ = mem_ref[...]
        r = lax.rsqrt(jnp.mean(mv_ * mv_, axis=-1, keepdims=True) + RMS_EPS)
        st_ref[...] = jnp.zeros_like(st_ref)
        st_ref[0:1, :] = jnp.sum(dmn * (mv_ * r), axis=0, keepdims=True)

    return pl.pallas_call(
        body, name="memkv_bwd",
        out_shape=[jax.ShapeDtypeStruct((D_MODEL, 2 * C_W), F32), jax.ShapeDtypeStruct((8, D_MODEL), F32)],
        compiler_params=_params(),
    )(mem, mem_g, mn, w_mem_full, dmk, dmv)


def _band_mask(has_prev, max_dist):
    qi = lax.broadcasted_iota(jnp.int32, (BLOCK, 2 * BLOCK), 0)
    kj = lax.broadcasted_iota(jnp.int32, (BLOCK, 2 * BLOCK), 1)
    dist = qi + BLOCK - kj
    return (dist >= 0) & (dist <= max_dist) & ((kj >= BLOCK) | has_prev)


_NT = (((1,), (1,)), ((), ()))
_TN = (((0,), (0,)), ((), ()))


def _head_only(val, h):
    slab = val[:, LANES * (h // 2):LANES * (h // 2 + 1)]
    lane = lax.broadcasted_iota(jnp.int32, slab.shape, 1)
    keep = (lane < HEAD_DIM) if h % 2 == 0 else (lane >= HEAD_DIM)
    return jnp.where(keep, slab, jnp.zeros((), slab.dtype))


class _KvSlabs:
    def __init__(self, cat, group):
        self.cat, self.group, self.swapped = cat, group, {}

    def is_swapped(self, h):
        return (h // self.group) % 2 != h % 2

    def __call__(self, h):
        j = (h // self.group) // 2
        slab = self.cat[:, LANES * j:LANES * (j + 1)]
        if not self.is_swapped(h):
            return slab
        if j not in self.swapped:
            self.swapped[j] = jnp.concatenate([slab[:, HEAD_DIM:], slab[:, :HEAD_DIM]], axis=1)
        return self.swapped[j]


class _BandSteps:
    def __init__(self, seq, dil, nq):
        self.nq, self.rows, self.consecutive = nq, nq * BLOCK, dil == 1
        nb = seq // dil // BLOCK
        if self.consecutive:
            assert nb % nq == 0
            self.outer, self.inner, self.stride = 1, nb // nq, 1
        else:
            assert dil % nq == 0
            self.outer, self.inner, self.stride = dil // nq, nb, dil // nq

    def own(self, w, clamp=False):
        cur = (lambda i: jnp.minimum(i, self.inner - 1)) if clamp else (lambda i: i)
        return pl.BlockSpec((self.rows, w), lambda r, i: (cur(i) * self.stride + r, 0))

    def prev(self, w, clamp=False):
        cur = (lambda i: jnp.minimum(i, self.inner - 1)) if clamp else (lambda i: i)
        if self.consecutive:
            return pl.BlockSpec((BLOCK, w), lambda r, i: (jnp.maximum(cur(i) * self.nq - 1, 0), 0))
        return pl.BlockSpec((self.rows, w), lambda r, i: (jnp.maximum(cur(i) - 1, 0) * self.stride + r, 0))

    def late(self, w):
        return pl.BlockSpec((self.rows, w), lambda r, i: (jnp.maximum(i - 1, 0) * self.stride + r, 0))

    def rows_of(self, j):
        return slice(BLOCK * j, BLOCK * (j + 1))

    def keys(self, p_ref, c_ref, j):
        if not self.consecutive:
            before = p_ref[self.rows_of(j), :]
        elif j == 0:
            before = p_ref[...]
        else:
            before = c_ref[self.rows_of(j - 1), :]
        return jnp.concatenate([before, c_ref[self.rows_of(j), :]], axis=0)

    def has_prev(self, i, j):
        return True if (self.consecutive and j > 0) else (i > 0)


def _banded_fwd(q, k, v, sink, *, dil, heads, group, max_dist, nq, name):
    seq = q.shape[0]
    kvh = heads // group
    qw, kw = heads * HEAD_DIM, kvh * HEAD_DIM
    steps = _BandSteps(seq, dil, nq)

    def body(*refs):
        if sink is not None:
            sink_ref, refs = refs[0], refs[1:]
        q_ref, kp_ref, kc_ref, vp_ref, vc_ref, o_ref, lse_ref, s_scr, p_scr = refs
        i = pl.program_id(1)
        lane = lax.broadcasted_iota(jnp.int32, (BLOCK, LANES), 1)
        k_of = [_KvSlabs(steps.keys(kp_ref, kc_ref, j), group) for j in range(nq)]
        v_of = [_KvSlabs(steps.keys(vp_ref, vc_ref, j), group) for j in range(nq)]
        for j in range(nq):
            qv = q_ref[steps.rows_of(j), :]
            for h in range(heads):
                s_scr[j * heads + h] = lax.dot_general(_head_only(qv, h), k_of[j](h), _NT, preferred_element_type=F32)
        ls = {}
        for j in range(nq):
            valid = _band_mask(steps.has_prev(i, j), max_dist)
            lse_tile = jnp.zeros((BLOCK, LANES), F32)
            for h in range(heads):
                s = jnp.where(valid, s_scr[j * heads + h], NEG)
                m = jnp.max(s, axis=-1, keepdims=True)
                if sink is not None:
                    sk = sink_ref[h]
                    m = jnp.maximum(m, sk)
                p = jnp.exp(s - m)
                l = jnp.sum(p, axis=-1, keepdims=True)
                if sink is not None:
                    l = l + jnp.exp(sk - m)
                p_scr[j * heads + h] = p.astype(BF16)
                ls[j, h] = l
                lse_tile = jnp.where(lane == h, m + jnp.log(l), lse_tile)
            lse_ref[steps.rows_of(j), :] = lse_tile
        for j in range(nq):
            for pr in range(heads // 2):
                he, ho = 2 * pr, 2 * pr + 1
                even = jnp.dot(p_scr[j * heads + he], v_of[j](he), preferred_element_type=F32) / ls[j, he]
                odd = jnp.dot(p_scr[j * heads + ho], v_of[j](ho), preferred_element_type=F32) / ls[j, ho]
                o_ref[steps.rows_of(j), LANES * pr:LANES * (pr + 1)] = jnp.where(lane < HEAD_DIM, even, odd).astype(BF16)

    in_specs = [steps.own(qw), steps.prev(kw), steps.own(kw), steps.prev(kw), steps.own(kw)]
    args = [q, k, k, v, v]
    if sink is not None:
        in_specs = [pl.BlockSpec(memory_space=pltpu.SMEM)] + in_specs
        args = [sink] + args
    return pl.pallas_call(
        body, name=name, grid=(steps.outer, steps.inner), in_specs=in_specs,
        out_specs=[steps.own(qw), steps.own(LANES)],
        out_shape=[jax.ShapeDtypeStruct((seq, qw), BF16), jax.ShapeDtypeStruct((seq, LANES), F32)],
        scratch_shapes=[pltpu.VMEM((nq * heads, BLOCK, 2 * BLOCK), F32), pltpu.VMEM((nq * heads, BLOCK, 2 * BLOCK), BF16)],
        compiler_params=_params(dimension_semantics=("arbitrary", "arbitrary")),
    )(*args)


def _banded_bwd(q, k, v, d_out, stat, sink, *, dil, heads, group, max_dist, nq, name, reduce_scatter=()):
    seq = q.shape[0]
    kvh = heads // group
    qw, kw = heads * HEAD_DIM, kvh * HEAD_DIM
    steps = _BandSteps(seq, dil, nq)
    n_rs = len(reduce_scatter)
    n_in = 7 + n_rs
    n_flat = steps.outer * (steps.inner + 1)

    def body(*refs):
        refs = list(refs)
        sink_ref = refs.pop(0) if sink is not None else None
        (q_ref, kp_ref, kc_ref, vp_ref, vc_ref, do_ref, st_ref), partials = refs[:7], refs[7:n_in]
        refs = refs[n_in:]
        dsink_ref = refs.pop(0) if sink is not None else None
        (dq_ref, dk_ref, dv_ref), sums = refs[:3], refs[3:3 + n_rs]
        kcar, vcar, s_scr, dp_scr, p_scr, ds_scr = refs[3 + n_rs:9 + n_rs]
        r, i = pl.program_id(0), pl.program_id(1)
        if n_rs:
            exchange = _ReduceScatter(tuple(partials), tuple(sums), refs[9 + n_rs:])
            flat = r * (steps.inner + 1) + i

            @pl.when(flat == 0)
            def _():
                exchange.start()

            @pl.when(flat == min(2, n_flat - 1))
            def _():
                exchange.send_chip_sums()

        @pl.when(i == 0)
        def _():
            kcar[...] = jnp.zeros_like(kcar)
            vcar[...] = jnp.zeros_like(vcar)

        if sink is not None:
            @pl.when((i == 0) & (r == 0))
            def _():
                dsink_ref[...] = jnp.zeros_like(dsink_ref)

        @pl.when(i < steps.inner)
        def _():
            lane = lax.broadcasted_iota(jnp.int32, (1, LANES), 1)
            lane_q = lax.broadcasted_iota(jnp.int32, (BLOCK, LANES), 1)
            k_of = [_KvSlabs(steps.keys(kp_ref, kc_ref, j), group) for j in range(nq)]
            v_of = [_KvSlabs(steps.keys(vp_ref, vc_ref, j), group) for j in range(nq)]
            qms, doms = {}, {}
            for j in range(nq):
                qv, dov = q_ref[steps.rows_of(j), :], do_ref[steps.rows_of(j), :]
                for h in range(heads):
                    qms[j, h], doms[j, h] = _head_only(qv, h), _head_only(dov, h)
                    s_scr[j * heads + h] = lax.dot_general(qms[j, h], k_of[j](h), _NT, preferred_element_type=F32)
                    dp_scr[j * heads + h] = lax.dot_general(doms[j, h], v_of[j](h), _NT, preferred_element_type=F32)
            dsink_row = jnp.zeros((1, LANES), F32)
            for j in range(nq):
                st = st_ref[steps.rows_of(j), :]
                valid = _band_mask(steps.has_prev(i, j), max_dist)
                for h in range(heads):
                    lse_h = st[:, h:h + 1]
                    delta = st[:, DELTA_LANE + h:DELTA_LANE + h + 1]
                    p = jnp.where(valid, jnp.exp(s_scr[j * heads + h] - lse_h), 0.0)
                    p_scr[j * heads + h] = p.astype(BF16)
                    ds_scr[j * heads + h] = (p * (dp_scr[j * heads + h] - delta)).astype(BF16)
                    if sink is not None:
                        ds_sink = jnp.sum(-jnp.exp(sink_ref[h] - lse_h) * delta, axis=0, keepdims=True)
                        dsink_row = dsink_row + jnp.where(lane == h, ds_sink, 0.0)
            for j in range(nq):
                for pr in range(heads // 2):
                    he, ho = 2 * pr, 2 * pr + 1
                    even = jnp.dot(ds_scr[j * heads + he], k_of[j](he), preferred_element_type=F32)
                    odd = jnp.dot(ds_scr[j * heads + ho], k_of[j](ho), preferred_element_type=F32)
                    dq_ref[steps.rows_of(j), LANES * pr:LANES * (pr + 1)] = (
                        jnp.where(lane_q < HEAD_DIM, even, odd).astype(BF16))
            if steps.consecutive:
                dk_ref[...] = kcar[...].astype(BF16)
                dv_ref[...] = vcar[...].astype(BF16)
            for j in range(nq):
                for slab in range(kw // LANES):
                    acc = {}
                    for h in range(heads):
                        if (h // group) // 2 != slab:
                            continue
                        key = k_of[j].is_swapped(h)
                        dk_h = lax.dot_general(ds_scr[j * heads + h], qms[j, h], _TN, preferred_element_type=F32)
                        dv_h = lax.dot_general(p_scr[j * heads + h], doms[j, h], _TN, preferred_element_type=F32)
                        acc[key] = (dk_h, dv_h) if key not in acc else (acc[key][0] + dk_h, acc[key][1] + dv_h)
                    dk_j, dv_j = acc.get(False, (None, None))
                    if True in acc:
                        unswap = lambda t: jnp.concatenate([t[:, HEAD_DIM:], t[:, :HEAD_DIM]], axis=1)
                        dk_s, dv_s = unswap(acc[True][0]), unswap(acc[True][1])
                        dk_j = dk_s if dk_j is None else dk_j + dk_s
                        dv_j = dv_s if dv_j is None else dv_j + dv_s
                    sl = slice(LANES * slab, LANES * (slab + 1))
                    own_rows = steps.rows_of(j)
                    if not steps.consecutive:
                        dk_ref[own_rows, sl] = (kcar[own_rows, sl] + dk_j[0:BLOCK]).astype(BF16)
                        dv_ref[own_rows, sl] = (vcar[own_rows, sl] + dv_j[0:BLOCK]).astype(BF16)
                    elif j == 0:
                        last = steps.rows_of(nq - 1)
                        dk_ref[last, sl] = (kcar[last, sl] + dk_j[0:BLOCK]).astype(BF16)
                        dv_ref[last, sl] = (vcar[last, sl] + dv_j[0:BLOCK]).astype(BF16)
                    else:
                        before = steps.rows_of(j - 1)
                        kcar[before, sl] += dk_j[0:BLOCK]
                        vcar[before, sl] += dv_j[0:BLOCK]
                    kcar[own_rows, sl] = dk_j[BLOCK:2 * BLOCK]
                    vcar[own_rows, sl] = dv_j[BLOCK:2 * BLOCK]
            if sink is not None:
                dsink_ref[0:1, :] += dsink_row

        @pl.when(i == steps.inner)
        def _():
            dk_ref[...] = kcar[...].astype(BF16)
            dv_ref[...] = vcar[...].astype(BF16)

        if n_rs:
            @pl.when(flat == n_flat - 1)
            def _():
                exchange.finish()

    own, prev = (lambda w: steps.own(w, clamp=True)), (lambda w: steps.prev(w, clamp=True))
    rs_shapes = [t.shape[1:] for t in reduce_scatter]
    in_specs = ([own(qw), prev(kw), own(kw), prev(kw), own(kw), own(qw), own(LANES)]
                + [pl.BlockSpec(memory_space=pl.ANY)] * n_rs)
    args = [q, k, k, v, v, d_out, stat, *reduce_scatter]
    out_specs = [own(qw), steps.late(kw), steps.late(kw)] + [_full(s) for s in rs_shapes]
    out_shape = [jax.ShapeDtypeStruct((seq, qw), BF16), jax.ShapeDtypeStruct((seq, kw), BF16),
                 jax.ShapeDtypeStruct((seq, kw), BF16)] + [jax.ShapeDtypeStruct(s, F32) for s in rs_shapes]
    if sink is not None:
        in_specs = [pl.BlockSpec(memory_space=pltpu.SMEM)] + in_specs
        args = [sink] + args
        out_specs = [_full((8, LANES))] + out_specs
        out_shape = [jax.ShapeDtypeStruct((8, LANES), F32)] + out_shape
    n_hb = nq * heads
    res = pl.pallas_call(
        body, name=name, grid=(steps.outer, steps.inner + 1), in_specs=in_specs, out_specs=out_specs,
        out_shape=out_shape,
        scratch_shapes=[pltpu.VMEM((steps.rows, kw), F32), pltpu.VMEM((steps.rows, kw), F32)]
        + [pltpu.VMEM((n_hb, BLOCK, 2 * BLOCK), F32)] * 2 + [pltpu.VMEM((n_hb, BLOCK, 2 * BLOCK), BF16)] * 2
        + (_ReduceScatter.scratch_shapes(rs_shapes) if n_rs else []),
        compiler_params=_params(dimension_semantics=("arbitrary", "arbitrary")),
    )(*args)
    if sink is not None:
        return (*res[1:4], res[0], *res[4:])
    return res


def _cross_fwd(q, mk, mv, tq=512):
    seq = q.shape[0]

    def body(q_ref, mk_ref, mv_ref, o_ref, lse_ref, s_scr, p_scr):
        qv = q_ref[...]
        k_of, v_of = _KvSlabs(mk_ref[...], 1), _KvSlabs(mv_ref[...], 1)
        lane = lax.broadcasted_iota(jnp.int32, (tq, LANES), 1)
        lse_tile = jnp.zeros((tq, LANES), F32)
        for h in range(C_HEADS):
            s_scr[h] = lax.dot_general(_head_only(qv, h), k_of(h), _NT, preferred_element_type=F32)
        ls = []
        for h in range(C_HEADS):
            s = s_scr[h]
            m = jnp.max(s, axis=-1, keepdims=True)
            p = jnp.exp(s - m)
            l = jnp.sum(p, axis=-1, keepdims=True)
            p_scr[h] = p.astype(BF16)
            ls.append(l)
            lse_tile = jnp.where(lane == h, m + jnp.log(l), lse_tile)
        for pr in range(C_HEADS // 2):
            even = jnp.dot(p_scr[2 * pr], v_of(2 * pr), preferred_element_type=F32) / ls[2 * pr]
            odd = jnp.dot(p_scr[2 * pr + 1], v_of(2 * pr + 1), preferred_element_type=F32) / ls[2 * pr + 1]
            o_ref[:, LANES * pr:LANES * (pr + 1)] = jnp.where(lane < HEAD_DIM, even, odd).astype(BF16)
        lse_ref[...] = lse_tile

    return pl.pallas_call(
        body, name="cross_fwd", grid=(seq // tq,),
        in_specs=[_row(tq, C_W), _full((N_MEM, C_W)), _full((N_MEM, C_W))],
        out_specs=[_row(tq, C_W), _row(tq, LANES)],
        out_shape=[jax.ShapeDtypeStruct((seq, C_W), BF16), jax.ShapeDtypeStruct((seq, LANES), F32)],
        scratch_shapes=[pltpu.VMEM((C_HEADS, tq, N_MEM), F32), pltpu.VMEM((C_HEADS, tq, N_MEM), BF16)],
        compiler_params=_params(dimension_semantics=("arbitrary",)),
    )(q, mk, mv)


def _cross_bwd(q, mk, mv, d_out, stat, tq=512):
    seq = q.shape[0]

    def body(q_ref, mk_ref, mv_ref, do_ref, st_ref, dq_ref, dmk_ref, dmv_ref, s_scr, dp_scr, p_scr, ds_scr):
        @pl.when(pl.program_id(0) == 0)
        def _():
            dmk_ref[...] = jnp.zeros_like(dmk_ref)
            dmv_ref[...] = jnp.zeros_like(dmv_ref)

        qv, dov, st = q_ref[...], do_ref[...], st_ref[...]
        k_of, v_of = _KvSlabs(mk_ref[...], 1), _KvSlabs(mv_ref[...], 1)
        qms = [_head_only(qv, h) for h in range(C_HEADS)]
        doms = [_head_only(dov, h) for h in range(C_HEADS)]
        for h in range(C_HEADS):
            s_scr[h] = lax.dot_general(qms[h], k_of(h), _NT, preferred_element_type=F32)
            dp_scr[h] = lax.dot_general(doms[h], v_of(h), _NT, preferred_element_type=F32)
        for h in range(C_HEADS):
            p = jnp.exp(s_scr[h] - st[:, h:h + 1])
            p_scr[h] = p.astype(BF16)
            ds_scr[h] = (p * (dp_scr[h] - st[:, DELTA_LANE + h:DELTA_LANE + h + 1])).astype(BF16)
        lane = lax.broadcasted_iota(jnp.int32, (tq, LANES), 1)
        for pr in range(C_HEADS // 2):
            sl = slice(LANES * pr, LANES * (pr + 1))
            even = jnp.dot(ds_scr[2 * pr], k_of(2 * pr), preferred_element_type=F32)
            odd = jnp.dot(ds_scr[2 * pr + 1], k_of(2 * pr + 1), preferred_element_type=F32)
            dq_ref[:, sl] = jnp.where(lane < HEAD_DIM, even, odd).astype(BF16)
            dmk_ref[:, sl] += (lax.dot_general(ds_scr[2 * pr], qms[2 * pr], _TN, preferred_element_type=F32)
                               + lax.dot_general(ds_scr[2 * pr + 1], qms[2 * pr + 1], _TN, preferred_element_type=F32))
            dmv_ref[:, sl] += (lax.dot_general(p_scr[2 * pr], doms[2 * pr], _TN, preferred_element_type=F32)
                               + lax.dot_general(p_scr[2 * pr + 1], doms[2 * pr + 1], _TN, preferred_element_type=F32))

    return pl.pallas_call(
        body, name="cross_bwd", grid=(seq // tq,),
        in_specs=[_row(tq, C_W), _full((N_MEM, C_W)), _full((N_MEM, C_W)), _row(tq, C_W), _row(tq, LANES)],
        out_specs=[_row(tq, C_W), _full((N_MEM, C_W)), _full((N_MEM, C_W))],
        out_shape=[jax.ShapeDtypeStruct((seq, C_W), BF16), jax.ShapeDtypeStruct((N_MEM, C_W), F32),
                   jax.ShapeDtypeStruct((N_MEM, C_W), F32)],
        scratch_shapes=[pltpu.VMEM((C_HEADS, tq, N_MEM), F32)] * 2 + [pltpu.VMEM((C_HEADS, tq, N_MEM), BF16)] * 2,
        compiler_params=_params(dimension_semantics=("arbitrary",)),
    )(q, mk, mv, d_out, stat)


def _per_head(tile, width):
    rows = tile.shape[0]
    return jnp.concatenate(
        [jnp.broadcast_to(tile[:, h:h + 1], (rows, HEAD_DIM)) for h in range(width // HEAD_DIM)], axis=1)


def _with_delta(lse_tile, prod):
    rows = lse_tile.shape[0]
    lane = lax.broadcasted_iota(jnp.int32, (rows, LANES), 1)
    tile = lse_tile
    for h in range(prod.shape[1] // HEAD_DIM):
        d = jnp.sum(prod[:, HEAD_DIM * h:HEAD_DIM * (h + 1)], axis=-1, keepdims=True)
        tile = jnp.where(lane == DELTA_LANE + h, d, tile)
    return tile


def _mid(oa, lse_a, ob, lse_b, oc, lse_c, gate, x, target, w_out_full, post_g, tm=512):
    seq = x.shape[0]
    n_b = B_W // LANES

    def body(oa_ref, la_ref, b1_ref, l1_ref, b4_ref, l4_ref, b16_ref, l16_ref, oc_ref, lc_ref,
             gate_ref, x_ref, t_ref, w_ref, pg_ref,
             dh_ref, dg_ref, doa_ref, sa_ref, dob1_ref, sb1_ref, dob4_ref, sb4_ref, dob16_ref, sb16_ref,
             doc_ref, sc_ref, dw_ref, st_ref, scr_b4, scr_b16, scr_l4, scr_l16, scr_do, scr_sb):
        @pl.when(pl.program_id(0) == 0)
        def _():
            dw_ref[...] = jnp.zeros_like(dw_ref)
            st_ref[...] = jnp.zeros_like(st_ref)

        b1, l1 = b1_ref[...].astype(F32), l1_ref[...]
        b4, l4 = _load_permuted(b4_ref, scr_b4, 4), _load_permuted(l4_ref, scr_l4, 4)
        b16, l16 = _load_permuted(b16_ref, scr_b16, 16), _load_permuted(l16_ref, scr_l16, 16)
        lm = jnp.maximum(jnp.maximum(l1, l4), l16)
        e1, e4, e16 = jnp.exp(l1 - lm), jnp.exp(l4 - lm), jnp.exp(l16 - lm)
        den = e1 + e4 + e16
        lse_b_tile = lm + jnp.log(den)
        ob_v = _per_head(e1 / den, B_W) * b1 + _per_head(e4 / den, B_W) * b4 + _per_head(e16 / den, B_W) * b16
        o_all = jnp.concatenate([oa_ref[...].astype(F32), ob_v, oc_ref[...].astype(F32)], axis=1)
        g = gate_ref[...].astype(F32)
        sig = 1.0 / (1.0 + jnp.exp(-g))
        silu = g * sig
        y = (o_all * silu).astype(BF16)
        w = w_ref[...]
        z = jnp.dot(y, w, preferred_element_type=F32)
        rz = lax.rsqrt(jnp.mean(z * z, axis=-1, keepdims=True) + RMS_EPS)
        hn = z * rz
        pg = pg_ref[...]
        err = (x_ref[...] + hn * pg) - t_ref[...]
        loss = 0.5 * jnp.sum(jnp.mean(err * err, axis=-1, keepdims=True), axis=0, keepdims=True)
        dh = err * (1.0 / D_MODEL)
        dh_ref[...] = dh.astype(BF16)
        st_ref[0:1, :] += jnp.sum(dh * hn, axis=0, keepdims=True)
        st_ref[1:2, :] += jnp.broadcast_to(loss, (1, D_MODEL))
        dhn = dh * pg
        dz = (rz * (dhn - hn * jnp.mean(dhn * hn, axis=-1, keepdims=True))).astype(BF16)
        dy = lax.dot_general(dz, w, _NT, preferred_element_type=F32)
        dw_ref[...] += lax.dot_general(y, dz, _TN, preferred_element_type=F32)
        dg_ref[...] = (dy * o_all * (sig * (1.0 + g * (1.0 - sig)))).astype(BF16)
        d_o = (dy * silu).astype(BF16)
        prod = d_o.astype(F32) * o_all
        doa_ref[...] = d_o[:, 0:A_W]
        sa_ref[...] = _with_delta(la_ref[...], prod[:, 0:A_W])
        doc_ref[...] = d_o[:, A_W + B_W:D_MODEL]
        sc_ref[...] = _with_delta(lc_ref[...], prod[:, A_W + B_W:D_MODEL])
        d_ob = d_o[:, A_W:A_W + B_W]
        stat_b = _with_delta(lse_b_tile, prod[:, A_W:A_W + B_W])
        dob1_ref[...] = d_ob
        sb1_ref[...] = stat_b
        _put(scr_do, d_ob.astype(F32))
        _put(scr_sb, stat_b)
        _store_permuted(scr_do, dob4_ref, 4, BF16)
        _store_permuted(scr_sb, sb4_ref, 4, F32)
        _store_permuted(scr_do, dob16_ref, 16, BF16)
        _store_permuted(scr_sb, sb16_ref, 16, F32)

    p4 = lambda w: _perm_spec(tm, 4, w)
    p16 = lambda w: _perm_spec(tm, 16, w)
    in_specs = [_row(tm, A_W), _row(tm, LANES), _row(tm, B_W), _row(tm, LANES), p4(B_W), p4(LANES), p16(B_W), p16(LANES),
                _row(tm, C_W), _row(tm, LANES), _row(tm, D_MODEL), _row(tm, D_MODEL), _row(tm, D_MODEL),
                _full((D_MODEL, D_MODEL)), _full((1, D_MODEL))]
    sds = jax.ShapeDtypeStruct
    v4 = lambda w, dt: sds((seq // (BLOCK * 4), 4, BLOCK, w), dt)
    v16 = lambda w, dt: sds((seq // (BLOCK * 16), 16, BLOCK, w), dt)
    out_specs = [_row(tm, D_MODEL), _row(tm, D_MODEL), _row(tm, A_W), _row(tm, LANES), _row(tm, B_W), _row(tm, LANES),
                 p4(B_W), p4(LANES), p16(B_W), p16(LANES), _row(tm, C_W), _row(tm, LANES),
                 _full((D_MODEL, D_MODEL)), _full((8, D_MODEL))]
    out_shape = [sds((seq, D_MODEL), BF16), sds((seq, D_MODEL), BF16), sds((seq, A_W), BF16), sds((seq, LANES), F32),
                 sds((seq, B_W), BF16), sds((seq, LANES), F32), v4(B_W, BF16), v4(LANES, F32), v16(B_W, BF16),
                 v16(LANES, F32), sds((seq, C_W), BF16), sds((seq, LANES), F32),
                 sds((D_MODEL, D_MODEL), F32), sds((8, D_MODEL), F32)]
    res = pl.pallas_call(
        body, name="mid", grid=(seq // tm,), in_specs=in_specs, out_specs=out_specs, out_shape=out_shape,
        scratch_shapes=[pltpu.VMEM((n_b, tm, LANES), F32), pltpu.VMEM((n_b, tm, LANES), F32),
                        pltpu.VMEM((1, tm, LANES), F32), pltpu.VMEM((1, tm, LANES), F32),
                        pltpu.VMEM((n_b, tm, LANES), F32), pltpu.VMEM((1, tm, LANES), F32)],
        compiler_params=_params(dimension_semantics=("arbitrary",)),
    )(oa, lse_a, ob[1], lse_b[1], _perm_view(ob[4], 4), _perm_view(lse_b[4], 4), _perm_view(ob[16], 16),
      _perm_view(lse_b[16], 16), oc, lse_c, gate, x, target, w_out_full, post_g)
    dh, d_gate, do_a, st_a, do_b1, st_b1, do_b4, st_b4, do_b16, st_b16, do_c, st_c, d_wout, stats = res
    flat = lambda t: t.reshape(seq, t.shape[-1])
    d_b = {1: (do_b1, st_b1), 4: (flat(do_b4), flat(st_b4)), 16: (flat(do_b16), flat(st_b16))}
    return dh, d_gate, (do_a, st_a), d_b, (do_c, st_c), d_wout, stats


def _inproj_bwd(x, u, dh, pre_g, w_in_full, tabs, dqa, dka, dva, dqkv_b, dqc, dgate, tm=512):
    seq = x.shape[0]
    n_b = B_W // LANES

    def body(x_ref, u_ref, dh_ref, g_ref, w_hbm, c_ref, up_ref, dn_ref, dqa_ref, dka_ref, dva_ref,
             dq1, dk1, dv1, dq4, dk4, dv4, dq16, dk16, dv16, dqc_ref, dg_ref,
             gx_ref, dw_ref, st_ref, scr4, scr16, w_scr, w_sems, dp_ref):
        _stage_w_in(w_hbm, w_scr, w_sems)

        @pl.when(pl.program_id(0) == 0)
        def _():
            st_ref[...] = jnp.zeros_like(st_ref)
            dw_ref[...] = jnp.zeros_like(dw_ref)

        c, up, dn = c_ref[...], -up_ref[...], -dn_ref[...]
        unrot = lambda t: _rotate(t, c, up, dn)
        total = lambda r1, r4, r16: (r1[...].astype(F32) + _load_permuted(r4, scr4, 4)
                                     + _load_permuted(r16, scr16, 16))
        dp_ref[:, 0:384] = (unrot(dqa_ref[...].astype(F32)) * SCALE).astype(BF16)
        dp_ref[:, 384:512] = unrot(dka_ref[...].astype(F32)).astype(BF16)
        dp_ref[:, 512:640] = dva_ref[...]
        dp_ref[:, 640:1024] = dg_ref[:, 0:384]
        dp_ref[:, 1024:1408] = (unrot(total(dq1, dq4, dq16)) * SCALE).astype(BF16)
        dp_ref[:, 1408:1792] = unrot(total(dk1, dk4, dk16)).astype(BF16)
        dp_ref[:, 1792:2176] = total(dv1, dv4, dv16).astype(BF16)
        dp_ref[:, 2176:2560] = dg_ref[:, 384:768]
        dp_ref[:, 2560:2816] = (dqc_ref[...].astype(F32) * SCALE).astype(BF16)
        dp_ref[:, 2816:3072] = dg_ref[:, 768:1024]
        du = lax.dot_general(dp_ref[...], w_scr[...], _NT, preferred_element_type=F32)
        res = lax.dot_general(u_ref[...], dp_ref[...], _TN, preferred_element_type=F32)
        for k in range(N_DEV):
            dw_ref[k] += res[:, SHARD_IN * k:SHARD_IN * (k + 1)]
        xv = x_ref[...]
        r = lax.rsqrt(jnp.mean(xv * xv, axis=-1, keepdims=True) + RMS_EPS)
        xh = xv * r
        st_ref[0:1, :] += jnp.sum(du * xh, axis=0, keepdims=True)
        dxh = du * g_ref[...]
        gx_ref[...] = dh_ref[...].astype(F32) + r * (dxh - xh * jnp.mean(dxh * xh, axis=-1, keepdims=True))

    in_specs = ([_row(tm, D_MODEL), _row(tm, D_MODEL), _row(tm, D_MODEL), _full((1, D_MODEL)),
                 pl.BlockSpec(memory_space=pl.ANY),
                 _row(tm, LANES), _row(tm, LANES), _row(tm, LANES), _row(tm, A_W), _row(tm, A_KV_W), _row(tm, A_KV_W)]
                + [_row(tm, B_W)] * 3 + [_perm_spec(tm, 4, B_W)] * 3 + [_perm_spec(tm, 16, B_W)] * 3
                + [_row(tm, C_W), _row(tm, D_MODEL)])
    dw_spec = pl.BlockSpec((N_DEV, D_MODEL, SHARD_IN), lambda i: (0, 0, 0), pipeline_mode=pl.Buffered(1))
    return pl.pallas_call(
        body, name="inproj_bwd", grid=(seq // tm,), in_specs=in_specs,
        out_specs=[_row(tm, D_MODEL), dw_spec, _full((8, D_MODEL))],
        out_shape=[jax.ShapeDtypeStruct((seq, D_MODEL), F32), jax.ShapeDtypeStruct((N_DEV, D_MODEL, SHARD_IN), F32),
                   jax.ShapeDtypeStruct((8, D_MODEL), F32)],
        scratch_shapes=[pltpu.VMEM((n_b, tm, LANES), F32), pltpu.VMEM((n_b, tm, LANES), F32)] + _w_in_scratch()
        + [pltpu.VMEM((tm, D_IN), BF16)],
        compiler_params=_params(dimension_semantics=("arbitrary",)),
    )(x, u, dh, pre_g, w_in_full, *tabs, dqa, dka, dva, *dqkv_b[1], *[_perm_view(t, 4) for t in dqkv_b[4]],
      *[_perm_view(t, 16) for t in dqkv_b[16]], dqc, dgate)


class _ReduceScatter:
    def __init__(self, ins, outs, scratch):
        self.n = n = len(ins)
        self.ins, self.outs = ins, outs
        self.mine, self.got, self.snd, self.rcv = (scratch[n * t:n * (t + 1)] for t in range(4))
        self.load_sems, self.d2d_send, self.d2d_recv, self.ici_send, self.ici_recv = scratch[4 * n:]
        self.pos = _mesh_pos()
        self.pairs = [(a, kk) for kk in (3, 1, 2) for a in range(n)]

    @staticmethod
    def scratch_shapes(shapes):
        return ([pltpu.VMEM((4,) + s, F32) for s in shapes] + [pltpu.VMEM((4,) + s, F32) for s in shapes]
                + [pltpu.VMEM((3,) + s, BF16) for s in shapes] + [pltpu.VMEM((3,) + s, BF16) for s in shapes]
                + [pltpu.SemaphoreType.DMA((len(shapes), 4))] * 5)

    def _chip(self, kk):
        x, y, _ = self.pos
        return (1 - x if kk & 2 else x, 1 - y if kk & 1 else y)

    def _load(self, a, kk):
        block = _dev_index((*self._chip(kk), self.pos[2]))
        return pltpu.make_async_copy(self.ins[a].at[block], self.mine[a].at[kk], self.load_sems.at[a, kk])

    def _swap(self, a, kk):
        x, y, c = self.pos
        return pltpu.make_async_remote_copy(
            src_ref=self.ins[a].at[_dev_index((*self._chip(kk), 1 - c))], dst_ref=self.got[a].at[kk],
            send_sem=self.d2d_send.at[a, kk], recv_sem=self.d2d_recv.at[a, kk],
            device_id=(x, y, 1 - c), device_id_type=MESH_ID)

    def _hop(self, a, kk):
        return pltpu.make_async_remote_copy(
            src_ref=self.snd[a].at[kk - 1], dst_ref=self.rcv[a].at[kk - 1], send_sem=self.ici_send.at[a, kk],
            recv_sem=self.ici_recv.at[a, kk], device_id=(*self._chip(kk), self.pos[2]), device_id_type=MESH_ID)

    def start(self):
        for kk in range(4):
            for a in range(self.n):
                self._load(a, kk).start()
                self._swap(a, kk).start()

    def send_chip_sums(self):
        for a, kk in self.pairs:
            self._load(a, kk).wait()
            self._swap(a, kk).wait_recv()
            self.snd[a][kk - 1] = (self.mine[a][kk] + self.got[a][kk]).astype(BF16)
            self._hop(a, kk).start()

    def finish(self):
        for a in range(self.n):
            self._load(a, 0).wait()
            self._swap(a, 0).wait_recv()
            acc = self.mine[a][0] + self.got[a][0]
            for kk in (1, 2, 3):
                self._hop(a, kk).wait_recv()
                acc = acc + self.rcv[a][kk - 1].astype(F32)
            self.outs[a][...] = acc
        for kk in range(4):
            for a in range(self.n):
                self._swap(a, kk).wait_send()
        for a, kk in self.pairs:
            self._hop(a, kk).wait_send()


def _local_step(x, mem, pre_g, w_in, sink, mem_g, w_mem, w_out, post_g, target):
    u, *tabs, w_in_full = _prep(x, pre_g, w_in)
    qa, ka, va, qkv_b, qc, gate, w_mem_all, w_out_all = _inproj(u, w_in_full, tabs, w_mem, w_out)
    w_mem_full = w_mem_all.reshape(D_MODEL, 2 * C_W)
    w_out_full = w_out_all.reshape(D_MODEL, D_MODEL)
    mn, mk, mv = _memkv_fwd(mem, mem_g, w_mem_full)

    a_cfg = dict(dil=1, heads=A_HEADS, group=A_GROUP, max_dist=BLOCK - 1, nq=ATTN_BLOCKS_PER_STEP)
    b_cfgs = {dil: dict(dil=dil, heads=B_HEADS, group=1, max_dist=win // dil, nq=ATTN_BLOCKS_PER_STEP)
              for win, dil in B_CONFIGS}
    oa, lse_a = _banded_fwd(qa, ka, va, sink, name="attn_a_fwd", **a_cfg)
    ob, lse_b = {}, {}
    for dil, cfg in b_cfgs.items():
        ob[dil], lse_b[dil] = _banded_fwd(*qkv_b[dil], None, name=f"attn_b{dil}_fwd", **cfg)
    oc, lse_c = _cross_fwd(qc, mk, mv)

    dh, d_gate, d_a, d_b, d_c, d_wout, st_mid = _mid(oa, lse_a, ob, lse_b, oc, lse_c, gate, x, target, w_out_full, post_g)

    dqc, dmk, dmv = _cross_bwd(qc, mk, mv, *d_c)
    d_wmem, st_mem = _memkv_bwd(mem, mem_g, mn, w_mem_full, dmk, dmv)
    dqkv_b = {dil: _banded_bwd(*qkv_b[dil], *d_b[dil], None, name=f"attn_b{dil}_bwd", **cfg)
              for dil, cfg in b_cfgs.items()}
    dqa, dka, dva, dsink, g_wmem, g_wout = _banded_bwd(
        qa, ka, va, *d_a, sink, name="attn_a_bwd", **a_cfg,
        reduce_scatter=(d_wmem.reshape(N_DEV, SHARD_ROWS, 2 * C_W), d_wout.reshape(N_DEV, SHARD_ROWS, D_MODEL)))

    grad_x, d_win, st_pre = _inproj_bwd(x, u, dh, pre_g, w_in_full, tabs, dqa, dka, dva, dqkv_b, dqc, d_gate)

    dsink_row = jnp.pad(dsink[0:1, :], ((0, 0), (0, D_MODEL - LANES)))
    stats = jnp.concatenate([st_pre[0:1], st_mem[0:1], st_mid[0:1], dsink_row, st_mid[1:2],
                             jnp.zeros((3, D_MODEL), F32)], axis=0)
    return grad_x, d_win, g_wmem, g_wout, stats


def _prep(x, pre_g, w_in, tm=1024, parts=2):
    seq = x.shape[0]
    n_steps = seq // tm
    rows = D_MODEL // parts
    pass_on_at = [max(n_steps - parts + a, 0) for a in range(parts)]
    j = jnp.arange(LANES) % HEAD_DIM
    freq = (ROPE_THETA ** (-(2 * (j % (ROT_DIM // 2))).astype(F32) / ROT_DIM))[None, :]

    def body(x_ref, g_ref, f_ref, win_ref, u_ref, c_ref, up_ref, dn_ref, win_out, win_b,
             send_sems, recv_sems, local_sems):
        step = pl.program_id(0)
        px, py, pc = _mesh_pos()
        me, sibling = (px, py, pc), (px, py, 1 - pc)
        chips = [(1 - px, py), (px, 1 - py), (1 - px, 1 - py)]

        def src(a):
            return win_b.at[pl.ds(rows * a, rows)]

        def slot(a, p):
            return win_out.at[_dev_index(p), pl.ds(rows * a, rows)]

        def copy(a, k, block, to, own=False):
            return pltpu.make_async_remote_copy(
                src_ref=src(a) if own else slot(a, block), dst_ref=slot(a, block),
                send_sem=send_sems.at[a, k], recv_sem=recv_sems.at[a, k], device_id=to, device_id_type=MESH_ID)

        def first_sends(a):
            return [copy(a, 1 + k, me, (*chip, pc), own=True) for k, chip in enumerate(chips)] + [
                copy(a, 0, me, sibling, own=True)]

        def local(a):
            return pltpu.make_async_copy(src(a), slot(a, me), local_sems.at[a])

        @pl.when(step == 0)
        def _():
            win_b[...] = win_ref[...].astype(BF16)
            for a in range(parts):
                local(a).start()
                for cp in first_sends(a):
                    cp.start()

        for a in range(parts):
            @pl.when(step == pass_on_at[a])
            def _(a=a):
                for k, chip in enumerate(chips):
                    copy(a, 1 + k, (*chip, pc), me).wait_recv()
                    copy(a, 4 + k, (*chip, pc), sibling).start()

        xv = x_ref[...]
        r = lax.rsqrt(jnp.mean(xv * xv, axis=-1, keepdims=True) + RMS_EPS)
        u_ref[...] = ((xv * r) * g_ref[...]).astype(BF16)
        pos = (lax.broadcasted_iota(jnp.int32, (tm, LANES), 0) + step * tm).astype(F32)
        head_lane = lax.broadcasted_iota(jnp.int32, (tm, LANES), 1) % HEAD_DIM
        ang = pos * f_ref[...]
        cos, sin = jnp.cos(ang), jnp.sin(ang)
        half = ROT_DIM // 2
        c_ref[...] = jnp.where(head_lane < ROT_DIM, cos, 1.0)
        up_ref[...] = jnp.where((head_lane >= half) & (head_lane < ROT_DIM), sin, 0.0)
        dn_ref[...] = jnp.where(head_lane < half, -sin, 0.0)

        @pl.when(step == n_steps - 1)
        def _():
            for a in range(parts):
                copy(a, 0, sibling, me).wait_recv()
                for k, chip in enumerate(chips):
                    copy(a, 4 + k, (*chip, 1 - pc), me).wait_recv()
            for a in range(parts):
                for cp in first_sends(a):
                    cp.wait_send()
                for k, chip in enumerate(chips):
                    copy(a, 4 + k, (*chip, pc), sibling).wait_send()
                local(a).wait()

    return pl.pallas_call(
        body, name="prep", grid=(n_steps,),
        in_specs=[_row(tm, D_MODEL), _full((1, D_MODEL)), _full((1, LANES)), _full(w_in.shape)],
        out_specs=[_row(tm, D_MODEL), _row(tm, LANES), _row(tm, LANES), _row(tm, LANES),
                   pl.BlockSpec(memory_space=pl.ANY)],
        out_shape=[jax.ShapeDtypeStruct((seq, D_MODEL), BF16)] + [jax.ShapeDtypeStruct((seq, LANES), F32)] * 3
        + [jax.ShapeDtypeStruct((N_DEV,) + w_in.shape, BF16)],
        scratch_shapes=[pltpu.VMEM(w_in.shape, BF16), pltpu.SemaphoreType.DMA((parts, 7)),
                        pltpu.SemaphoreType.DMA((parts, 7)), pltpu.SemaphoreType.DMA((parts,))],
        compiler_params=_params(dimension_semantics=("arbitrary",)),
    )(x, pre_g, freq, w_in)


def _exchange_grads(d_win, stats):
    def body(win, st, g_win, r_st, send_sems, recv_sems, local_sem, *scratch):
        exchange = _ReduceScatter((win,), (g_win,), scratch)
        exchange.start()
        pos = _mesh_pos()
        me = _dev_index(pos)
        own = pltpu.make_async_copy(st, r_st.at[me], local_sem)
        own.start()
        copies = []
        for s in range(1, N_DEV):
            peer = _xor_peer(pos, s)
            mk = lambda slot: pltpu.make_async_remote_copy(
                src_ref=st, dst_ref=r_st.at[slot], send_sem=send_sems.at[s], recv_sem=recv_sems.at[s],
                device_id=peer, device_id_type=MESH_ID)
            send, arrival = mk(me), mk(_dev_index(peer))
            send.start()
            copies.append((send, arrival))
        exchange.send_chip_sums()
        exchange.finish()
        for send, arrival in copies:
            arrival.wait_recv()
            send.wait_send()
        own.wait()

    hbm = pl.BlockSpec(memory_space=pl.ANY)
    shard = d_win.shape[1:]
    return pl.pallas_call(
        body, name="exchange_grads", in_specs=[hbm, hbm],
        out_specs=[pl.BlockSpec(memory_space=pltpu.VMEM), hbm],
        out_shape=[jax.ShapeDtypeStruct(shard, F32), jax.ShapeDtypeStruct((N_DEV,) + stats.shape, F32)],
        scratch_shapes=[pltpu.SemaphoreType.DMA((N_DEV,)), pltpu.SemaphoreType.DMA((N_DEV,)), pltpu.SemaphoreType.DMA(())]
        + _ReduceScatter.scratch_shapes([shard]),
        compiler_params=_params(),
    )(d_win, stats)


WEIGHT_ORDER = ("pre_norm", "w_in", "sink_a", "mem_norm", "w_mem_kv", "w_out", "post_norm")


def _adamw_all(grads, r_stats, weights, moments_m, moments_v):
    n = len(WEIGHT_ORDER)
    stat_row = {"pre_norm": 0, "mem_norm": 1, "post_norm": 2, "sink_a": 3}

    def body(*refs):
        gw_in, gw_mem, gw_out, st_ref = refs[0:4]
        w_refs, m_refs, v_refs = (dict(zip(WEIGHT_ORDER, refs[4 + n * t:4 + n * (t + 1)])) for t in range(3))
        loss_ref = refs[4 + 3 * n]
        outs = refs[5 + 3 * n:]
        g_small = st_ref[0]
        for s in range(1, N_DEV):
            g_small = g_small + st_ref[s]
        loss_ref[...] = g_small[4:5, 0:1]
        big = {"w_in": gw_in, "w_mem_kv": gw_mem, "w_out": gw_out}
        for i, name in enumerate(WEIGHT_ORDER):
            if name in big:
                g = big[name][...]
                at = lambda ref: ref[0]
            else:
                width = w_refs[name].shape[-1]
                g = g_small[stat_row[name]:stat_row[name] + 1, 0:width]
                at = lambda ref: ref[...]
            m2 = ADAM_B1 * at(m_refs[name]) + (1.0 - ADAM_B1) * g
            v2 = ADAM_B2 * at(v_refs[name]) + (1.0 - ADAM_B2) * (g * g)
            m_hat = m2 / (1.0 - ADAM_B1 ** ADAM_STEP)
            v_hat = v2 / (1.0 - ADAM_B2 ** ADAM_STEP)
            delta = -ADAM_LR * (m_hat / (jnp.sqrt(v_hat) + ADAM_EPS) + ADAM_WD * at(w_refs[name]))
            for kind, val in enumerate((g, delta, m2, v2)):
                out = outs[kind * n + i]
                if name in big:
                    out[0] = val
                else:
                    out[...] = val

    shapes = [weights[name].shape for name in WEIGHT_ORDER]
    res = pl.pallas_call(
        body, name="adamw_all",
        out_shape=[jax.ShapeDtypeStruct((1, 1), F32)] + [jax.ShapeDtypeStruct(sh, F32) for sh in shapes] * 4,
        compiler_params=_params(),
    )(grads["w_in"], grads["w_mem_kv"], grads["w_out"], r_stats,
      *[weights[k] for k in WEIGHT_ORDER], *[moments_m[k] for k in WEIGHT_ORDER], *[moments_v[k] for k in WEIGHT_ORDER])
    return res[0].reshape(()), res[1:]


def kernel(x, mem, pre_norm, w_in, sink_a, mem_norm, w_mem_kv, w_out, post_norm, loss_target, m_pre_norm, m_w_in, m_sink_a, m_mem_norm, m_w_mem_kv, m_w_out, m_post_norm, v_pre_norm, v_w_in, v_sink_a, v_mem_norm, v_w_mem_kv, v_w_out, v_post_norm):
    sink = jnp.pad(sink_a[0], (0, 8 - A_HEADS))
    grad_x, d_win, g_wmem, g_wout, stats = _local_step(
        x[0], mem[0], pre_norm, w_in[0], sink, mem_norm, w_mem_kv[0], w_out[0], post_norm, loss_target[0])
    g_win, r_stats = _exchange_grads(d_win, stats)
    weights = dict(pre_norm=pre_norm, w_in=w_in, sink_a=sink_a, mem_norm=mem_norm, w_mem_kv=w_mem_kv, w_out=w_out,
                   post_norm=post_norm)
    moments_m = dict(pre_norm=m_pre_norm, w_in=m_w_in, sink_a=m_sink_a, mem_norm=m_mem_norm, w_mem_kv=m_w_mem_kv,
                     w_out=m_w_out, post_norm=m_post_norm)
    moments_v = dict(pre_norm=v_pre_norm, w_in=v_w_in, sink_a=v_sink_a, mem_norm=v_mem_norm, w_mem_kv=v_w_mem_kv,
                     w_out=v_w_out, post_norm=v_post_norm)
    loss, rest = _adamw_all(dict(w_in=g_win, w_mem_kv=g_wmem, w_out=g_wout), r_stats, weights, moments_m, moments_v)
    return (loss, grad_x[None], *rest)
```

```python
import jax
import jax.numpy as jnp
from jax import lax
from jax.experimental import pallas as pl
from jax.experimental.pallas import tpu as pltpu

F32 = jnp.float32
BF16 = jnp.bfloat16

D_MODEL = 1024
HEAD_DIM = 64
ROT_DIM = 16
ROPE_THETA = 500000.0
BLOCK = 128
LANES = 128
N_MEM = 256
RMS_EPS = 1e-6
SCALE = HEAD_DIM ** -0.5
A_HEADS, A_GROUP = 6, 3
B_HEADS = 6
C_HEADS = 4
A_W, A_KV_W, B_W, C_W = 384, 128, 384, 256
D_IN = 3072
N_DEV = 8
SHARD_IN = D_IN // N_DEV
SHARD_ROWS = D_MODEL // N_DEV
B_CONFIGS = ((128, 1), (512, 4), (2048, 16))
DILS = (4, 16)
NEG = -1e30
ATTN_BLOCKS_PER_STEP = 4
DELTA_LANE = 64
VMEM_LIMIT = 56 * 1024 * 1024

ADAM_LR, ADAM_B1, ADAM_B2, ADAM_EPS, ADAM_WD, ADAM_STEP = 0.001, 0.9, 0.999, 1e-08, 0.01, 10
MESH_ID = pl.DeviceIdType.MESH


def _params(**kw):
    return pltpu.CompilerParams(vmem_limit_bytes=VMEM_LIMIT, **kw)


def _full(shape):
    n = len(shape)
    return pl.BlockSpec(shape, lambda *_: (0,) * n)


def _row(tm, w):
    return pl.BlockSpec((tm, w), lambda i: (i, 0))


def _mesh_pos():
    return lax.axis_index("x"), lax.axis_index("y"), lax.axis_index("c")


def _dev_index(pos):
    return 4 * pos[0] + 2 * pos[1] + pos[2]


def _xor_peer(pos, s):
    x, y, c = pos
    return (1 - x if s & 4 else x, 1 - y if s & 2 else y, 1 - c if s & 1 else c)


def _perm_view(a, dil):
    return a.reshape(a.shape[0] // (BLOCK * dil), dil, BLOCK, a.shape[1])


def _perm_spec(tm, dil, w):
    chunk = BLOCK * dil
    if tm >= chunk:
        return pl.BlockSpec((tm // chunk, dil, BLOCK, w), lambda i: (i, 0, 0, 0))
    per = chunk // tm
    return pl.BlockSpec((1, dil, tm // dil, w), lambda i: (i // per, 0, i % per, 0))


def _put(scr, val):
    for c in range(val.shape[1] // LANES):
        scr[c] = val[:, LANES * c:LANES * (c + 1)]


def _get(scr):
    n = scr.shape[0]
    return scr[0] if n == 1 else jnp.concatenate([scr[c] for c in range(n)], axis=1)


def _get_class(scr, r, dil):
    n, rows = scr.shape[0], scr.shape[1]
    parts = [scr.at[c][pl.ds(r, rows // dil, stride=dil), :] for c in range(n)]
    return parts[0] if n == 1 else jnp.concatenate(parts, axis=1)


def _store_permuted(scr, out_ref, dil, dtype):
    for r in range(dil):
        out_ref[0, r] = _get_class(scr, r, dil).astype(dtype)


def _fill_permuted(in_ref, scr, dil):
    n, rows = scr.shape[0], scr.shape[1]
    for r in range(dil):
        val = in_ref[0, r].astype(F32)
        for c in range(n):
            scr.at[c][pl.ds(r, rows // dil, stride=dil), :] = val[:, LANES * c:LANES * (c + 1)]


def _load_permuted(in_ref, scr, dil):
    _fill_permuted(in_ref, scr, dil)
    return _get(scr)


def _rotate128(t, c, up, dn):
    return t * c + pltpu.roll(t, 8, 1) * up + pltpu.roll(t, LANES - 8, 1) * dn


def _rotate(t, c, up, dn):
    outs = [_rotate128(t[:, LANES * j:LANES * (j + 1)], c, up, dn) for j in range(t.shape[1] // LANES)]
    return outs[0] if len(outs) == 1 else jnp.concatenate(outs, axis=1)


def _w_in_scratch():
    return [pltpu.VMEM((D_MODEL, D_IN), BF16), pltpu.SemaphoreType.DMA((N_DEV,))]


def _stage_w_in(w_hbm, w_scr, sems):
    @pl.when(pl.program_id(0) == 0)
    def _():
        copies = [pltpu.make_async_copy(w_hbm.at[k], w_scr.at[:, pl.ds(SHARD_IN * k, SHARD_IN)], sems.at[k])
                  for k in range(N_DEV)]
        for cp in copies:
            cp.start()
        for cp in copies:
            cp.wait()


def _inproj(u, w_in_full, tabs, w_mem, w_out, tm=1024):
    seq = u.shape[0]
    n_chunk = D_IN // LANES
    n_steps = seq // tm

    def body(u_ref, w_hbm, c_ref, up_ref, dn_ref, wm_ref, wo_ref, qa_ref, ka_ref, va_ref,
             qb1_ref, kb1_ref, vb1_ref, qb4_ref, kb4_ref, vb4_ref, qb16_ref, kb16_ref, vb16_ref,
             qc_ref, gate_ref, wm_all, wo_all, proj, w_scr, w_sems, wm_b, wo_b, send_sems, recv_sems, local_sems):
        step = pl.program_id(0)
        shards, gathered = (wm_b, wo_b), (wm_all, wo_all)

        def gather_copies(arriving):
            pos = _mesh_pos()
            me = _dev_index(pos)
            local = [] if arriving else [
                pltpu.make_async_copy(shards[a], gathered[a].at[me], local_sems.at[a]) for a in range(2)]
            remote = []
            for s in range(1, N_DEV):
                peer = _xor_peer(pos, s)
                for a in range(2):
                    remote.append(pltpu.make_async_remote_copy(
                        src_ref=shards[a], dst_ref=gathered[a].at[_dev_index(peer) if arriving else me],
                        send_sem=send_sems.at[a, s], recv_sem=recv_sems.at[a, s], device_id=peer,
                        device_id_type=MESH_ID))
            return local, remote

        @pl.when(step == 0)
        def _():
            wm_b[...] = wm_ref[...].astype(BF16)
            wo_b[...] = wo_ref[...].astype(BF16)
            local, sends = gather_copies(arriving=False)
            for cp in local + sends:
                cp.start()

        _stage_w_in(w_hbm, w_scr, w_sems)
        u = u_ref[...]
        for n0 in range(0, D_IN, D_MODEL):
            acc = jnp.dot(u, w_scr[:, n0:n0 + D_MODEL], preferred_element_type=F32)
            for c3 in range(D_MODEL // LANES):
                proj[n0 // LANES + c3] = acc[:, LANES * c3:LANES * (c3 + 1)]
        c, up, dn = c_ref[...], up_ref[...], dn_ref[...]

        def cols(lo, hi, rot=False, scale=None):
            parts = []
            for ch in range(lo // LANES, hi // LANES):
                t = proj[ch]
                if rot:
                    t = _rotate128(t, c, up, dn)
                if scale is not None:
                    t = t * scale
                parts.append(t)
            return parts[0] if len(parts) == 1 else jnp.concatenate(parts, axis=1)

        qa_ref[...] = cols(0, 384, True, SCALE).astype(BF16)
        ka_ref[...] = cols(384, 512, True).astype(BF16)
        va_ref[...] = cols(512, 640).astype(BF16)
        gate_ref[:, 0:384] = cols(640, 1024).astype(BF16)
        gate_ref[:, 384:768] = cols(2176, 2560).astype(BF16)
        gate_ref[:, 768:1024] = cols(2816, 3072).astype(BF16)
        qc_ref[...] = cols(2560, 2816, False, SCALE).astype(BF16)
        for ch in range(1024 // LANES, 1408 // LANES):
            proj[ch] = _rotate128(proj[ch], c, up, dn) * SCALE
        for ch in range(1408 // LANES, 1792 // LANES):
            proj[ch] = _rotate128(proj[ch], c, up, dn)
        for lo, nat, p4, p16 in ((1024, qb1_ref, qb4_ref, qb16_ref), (1408, kb1_ref, kb4_ref, kb16_ref),
                                 (1792, vb1_ref, vb4_ref, vb16_ref)):
            chunks = range(lo // LANES, lo // LANES + B_W // LANES)
            nat[...] = jnp.concatenate([proj[ch] for ch in chunks], axis=1).astype(BF16)
            for dil, ref in ((4, p4), (16, p16)):
                span = min(tm, BLOCK * dil)
                for cc in range(tm // span):
                    for rr in range(dil):
                        ref[cc, rr] = jnp.concatenate(
                            [proj.at[ch][pl.ds(cc * span + rr, span // dil, stride=dil), :] for ch in chunks],
                            axis=1).astype(BF16)

        @pl.when(step == n_steps - 1)
        def _():
            for cp in gather_copies(arriving=True)[1]:
                cp.wait_recv()
            local, sends = gather_copies(arriving=False)
            for cp in sends:
                cp.wait_send()
            for cp in local:
                cp.wait()

    nat_w = (A_W, A_KV_W, A_KV_W, B_W, B_W, B_W)
    out_specs = [_row(tm, w) for w in nat_w]
    out_shape = [jax.ShapeDtypeStruct((seq, w), BF16) for w in nat_w]
    for dil in DILS:
        out_specs += [_perm_spec(tm, dil, B_W)] * 3
        out_shape += [jax.ShapeDtypeStruct((seq // (BLOCK * dil), dil, BLOCK, B_W), BF16)] * 3
    hbm = pl.BlockSpec(memory_space=pl.ANY)
    out_specs += [_row(tm, C_W), _row(tm, D_MODEL), hbm, hbm]
    out_shape += [jax.ShapeDtypeStruct((seq, C_W), BF16), jax.ShapeDtypeStruct((seq, D_MODEL), BF16),
                  jax.ShapeDtypeStruct((N_DEV,) + w_mem.shape, BF16), jax.ShapeDtypeStruct((N_DEV,) + w_out.shape, BF16)]
    res = pl.pallas_call(
        body, name="inproj", grid=(n_steps,),
        in_specs=[_row(tm, D_MODEL), hbm, _row(tm, LANES), _row(tm, LANES), _row(tm, LANES),
                  _full(w_mem.shape), _full(w_out.shape)],
        out_specs=out_specs, out_shape=out_shape,
        scratch_shapes=[pltpu.VMEM((n_chunk, tm, LANES), F32)] + _w_in_scratch()
        + [pltpu.VMEM(w_mem.shape, BF16), pltpu.VMEM(w_out.shape, BF16), pltpu.SemaphoreType.DMA((2, N_DEV)),
           pltpu.SemaphoreType.DMA((2, N_DEV)), pltpu.SemaphoreType.DMA((2,))],
        compiler_params=_params(dimension_semantics=("arbitrary",)),
    )(u, w_in_full, *tabs, w_mem, w_out)
    qa, ka, va = res[0:3]
    qkv_b = {1: res[3:6], 4: [t.reshape(seq, B_W) for t in res[6:9]], 16: [t.reshape(seq, B_W) for t in res[9:12]]}
    return qa, ka, va, qkv_b, res[12], res[13], res[14], res[15]


def _memkv_fwd(mem, mem_g, w_mem_full):
    def body(mem_ref, g_ref, w_ref, mn_ref, mk_ref, mv_ref):
        mv_ = mem_ref[...]
        r = lax.rsqrt(jnp.mean(mv_ * mv_, axis=-1, keepdims=True) + RMS_EPS)
        mn = ((mv_ * r) * g_ref[...]).astype(BF16)
        mn_ref[...] = mn
        mkv = jnp.dot(mn, w_ref[...], preferred_element_type=F32)
        mk_ref[...] = mkv[:, 0:C_W].astype(BF16)
        mv_ref[...] = mkv[:, C_W:2 * C_W].astype(BF16)

    return pl.pallas_call(
        body, name="memkv_fwd",
        out_shape=[jax.ShapeDtypeStruct((N_MEM, D_MODEL), BF16),
                   jax.ShapeDtypeStruct((N_MEM, C_W), BF16), jax.ShapeDtypeStruct((N_MEM, C_W), BF16)],
        compiler_params=_params(),
    )(mem, mem_g, w_mem_full)


def _memkv_bwd(mem, mem_g, mn, w_mem_full, dmk, dmv):
    def body(mem_ref, g_ref, mn_ref, w_ref, dmk_ref, dmv_ref, dw_ref, st_ref):
        dmkv = jnp.concatenate([dmk_ref[...], dmv_ref[...]], axis=1).astype(BF16)
        dw_ref[...] = lax.dot_general(mn_ref[...], dmkv, (((0,), (0,)), ((), ())), preferred_element_type=F32)
        dmn = lax.dot_general(dmkv, w_ref[...], (((1,), (1,)), ((), ())), preferred_element_type=F32)
        mv_ = mem_ref[...]
        r = lax.rsqrt(jnp.mean(mv_ * mv_, axis=-1, keepdims=True) + RMS_EPS)
        st_ref[...] = jnp.zeros_like(st_ref)
        st_ref[0:1, :] = jnp.sum(dmn * (mv_ * r), axis=0, keepdims=True)

    return pl.pallas_call(
        body, name="memkv_bwd",
        out_shape=[jax.ShapeDtypeStruct((D_MODEL, 2 * C_W), F32), jax.ShapeDtypeStruct((8, D_MODEL), F32)],
        compiler_params=_params(),
    )(mem, mem_g, mn, w_mem_full, dmk, dmv)


def _band_mask(has_prev, max_dist):
    qi = lax.broadcasted_iota(jnp.int32, (BLOCK, 2 * BLOCK), 0)
    kj = lax.broadcasted_iota(jnp.int32, (BLOCK, 2 * BLOCK), 1)
    dist = qi + BLOCK - kj
    return (dist >= 0) & (dist <= max_dist) & ((kj >= BLOCK) | has_prev)


_NT = (((1,), (1,)), ((), ()))
_TN = (((0,), (0,)), ((), ()))


def _head_only(val, h):
    slab = val[:, LANES * (h // 2):LANES * (h // 2 + 1)]
    lane = lax.broadcasted_iota(jnp.int32, slab.shape, 1)
    keep = (lane < HEAD_DIM) if h % 2 == 0 else (lane >= HEAD_DIM)
    return jnp.where(keep, slab, jnp.zeros((), slab.dtype))


class _KvSlabs:
    def __init__(self, cat, group):
        self.cat, self.group, self.swapped = cat, group, {}

    def is_swapped(self, h):
        return (h // self.group) % 2 != h % 2

    def __call__(self, h):
        j = (h // self.group) // 2
        slab = self.cat[:, LANES * j:LANES * (j + 1)]
        if not self.is_swapped(h):
            return slab
        if j not in self.swapped:
            self.swapped[j] = jnp.concatenate([slab[:, HEAD_DIM:], slab[:, :HEAD_DIM]], axis=1)
        return self.swapped[j]


class _BandSteps:
    def __init__(self, seq, dil, nq):
        self.nq, self.rows, self.consecutive = nq, nq * BLOCK, dil == 1
        nb = seq // dil // BLOCK
        if self.consecutive:
            assert nb % nq == 0
            self.outer, self.inner, self.stride = 1, nb // nq, 1
        else:
            assert dil % nq == 0
            self.outer, self.inner, self.stride = dil // nq, nb, dil // nq

    def own(self, w, clamp=False):
        cur = (lambda i: jnp.minimum(i, self.inner - 1)) if clamp else (lambda i: i)
        return pl.BlockSpec((self.rows, w), lambda r, i: (cur(i) * self.stride + r, 0))

    def prev(self, w, clamp=False):
        cur = (lambda i: jnp.minimum(i, self.inner - 1)) if clamp else (lambda i: i)
        if self.consecutive:
            return pl.BlockSpec((BLOCK, w), lambda r, i: (jnp.maximum(cur(i) * self.nq - 1, 0), 0))
        return pl.BlockSpec((self.rows, w), lambda r, i: (jnp.maximum(cur(i) - 1, 0) * self.stride + r, 0))

    def late(self, w):
        return pl.BlockSpec((self.rows, w), lambda r, i: (jnp.maximum(i - 1, 0) * self.stride + r, 0))

    def rows_of(self, j):
        return slice(BLOCK * j, BLOCK * (j + 1))

    def keys(self, p_ref, c_ref, j):
        if not self.consecutive:
            before = p_ref[self.rows_of(j), :]
        elif j == 0:
            before = p_ref[...]
        else:
            before = c_ref[self.rows_of(j - 1), :]
        return jnp.concatenate([before, c_ref[self.rows_of(j), :]], axis=0)

    def has_prev(self, i, j):
        return True if (self.consecutive and j > 0) else (i > 0)


def _banded_fwd(q, k, v, sink, *, dil, heads, group, max_dist, nq, name):
    seq = q.shape[0]
    kvh = heads // group
    qw, kw = heads * HEAD_DIM, kvh * HEAD_DIM
    steps = _BandSteps(seq, dil, nq)

    def body(*refs):
        if sink is not None:
            sink_ref, refs = refs[0], refs[1:]
        q_ref, kp_ref, kc_ref, vp_ref, vc_ref, o_ref, lse_ref, s_scr, p_scr = refs
        i = pl.program_id(1)
        lane = lax.broadcasted_iota(jnp.int32, (BLOCK, LANES), 1)
        k_of = [_KvSlabs(steps.keys(kp_ref, kc_ref, j), group) for j in range(nq)]
        v_of = [_KvSlabs(steps.keys(vp_ref, vc_ref, j), group) for j in range(nq)]
        for j in range(nq):
            qv = q_ref[steps.rows_of(j), :]
            for h in range(heads):
                s_scr[j * heads + h] = lax.dot_general(_head_only(qv, h), k_of[j](h), _NT, preferred_element_type=F32)
        ls = {}
        for j in range(nq):
            valid = _band_mask(steps.has_prev(i, j), max_dist)
            lse_tile = jnp.zeros((BLOCK, LANES), F32)
            for h in range(heads):
                s = jnp.where(valid, s_scr[j * heads + h], NEG)
                m = jnp.max(s, axis=-1, keepdims=True)
                if sink is not None:
                    sk = sink_ref[h]
                    m = jnp.maximum(m, sk)
                p = jnp.exp(s - m)
                l = jnp.sum(p, axis=-1, keepdims=True)
                if sink is not None:
                    l = l + jnp.exp(sk - m)
                p_scr[j * heads + h] = p.astype(BF16)
                ls[j, h] = l
                lse_tile = jnp.where(lane == h, m + jnp.log(l), lse_tile)
            lse_ref[steps.rows_of(j), :] = lse_tile
        for j in range(nq):
            for pr in range(heads // 2):
                he, ho = 2 * pr, 2 * pr + 1
                even = jnp.dot(p_scr[j * heads + he], v_of[j](he), preferred_element_type=F32) / ls[j, he]
                odd = jnp.dot(p_scr[j * heads + ho], v_of[j](ho), preferred_element_type=F32) / ls[j, ho]
                o_ref[steps.rows_of(j), LANES * pr:LANES * (pr + 1)] = jnp.where(lane < HEAD_DIM, even, odd).astype(BF16)

    in_specs = [steps.own(qw), steps.prev(kw), steps.own(kw), steps.prev(kw), steps.own(kw)]
    args = [q, k, k, v, v]
    if sink is not None:
        in_specs = [pl.BlockSpec(memory_space=pltpu.SMEM)] + in_specs
        args = [sink] + args
    return pl.pallas_call(
        body, name=name, grid=(steps.outer, steps.inner), in_specs=in_specs,
        out_specs=[steps.own(qw), steps.own(LANES)],
        out_shape=[jax.ShapeDtypeStruct((seq, qw), BF16), jax.ShapeDtypeStruct((seq, LANES), F32)],
        scratch_shapes=[pltpu.VMEM((nq * heads, BLOCK, 2 * BLOCK), F32), pltpu.VMEM((nq * heads, BLOCK, 2 * BLOCK), BF16)],
        compiler_params=_params(dimension_semantics=("arbitrary", "arbitrary")),
    )(*args)


def _banded_bwd(q, k, v, d_out, stat, sink, *, dil, heads, group, max_dist, nq, name, reduce_scatter=()):
    seq = q.shape[0]
    kvh = heads // group
    qw, kw = heads * HEAD_DIM, kvh * HEAD_DIM
    steps = _BandSteps(seq, dil, nq)
    n_rs = len(reduce_scatter)
    n_in = 7 + n_rs
    n_flat = steps.outer * (steps.inner + 1)

    def body(*refs):
        refs = list(refs)
        sink_ref = refs.pop(0) if sink is not None else None
        (q_ref, kp_ref, kc_ref, vp_ref, vc_ref, do_ref, st_ref), partials = refs[:7], refs[7:n_in]
        refs = refs[n_in:]
        dsink_ref = refs.pop(0) if sink is not None else None
        (dq_ref, dk_ref, dv_ref), sums = refs[:3], refs[3:3 + n_rs]
        kcar, vcar, s_scr, dp_scr, p_scr, ds_scr = refs[3 + n_rs:9 + n_rs]
        r, i = pl.program_id(0), pl.program_id(1)
        if n_rs:
            exchange = _ReduceScatter(tuple(partials), tuple(sums), refs[9 + n_rs:])
            flat = r * (steps.inner + 1) + i

            @pl.when(flat == 0)
            def _():
                exchange.start()

            @pl.when(flat == min(2, n_flat - 1))
            def _():
                exchange.send_chip_sums()

        @pl.when(i == 0)
        def _():
            kcar[...] = jnp.zeros_like(kcar)
            vcar[...] = jnp.zeros_like(vcar)

        if sink is not None:
            @pl.when((i == 0) & (r == 0))
            def _():
                dsink_ref[...] = jnp.zeros_like(dsink_ref)

        @pl.when(i < steps.inner)
        def _():
            lane = lax.broadcasted_iota(jnp.int32, (1, LANES), 1)
            lane_q = lax.broadcasted_iota(jnp.int32, (BLOCK, LANES), 1)
            k_of = [_KvSlabs(steps.keys(kp_ref, kc_ref, j), group) for j in range(nq)]
            v_of = [_KvSlabs(steps.keys(vp_ref, vc_ref, j), group) for j in range(nq)]
            qms, doms = {}, {}
            for j in range(nq):
                qv, dov = q_ref[steps.rows_of(j), :], do_ref[steps.rows_of(j), :]
                for h in range(heads):
                    qms[j, h], doms[j, h] = _head_only(qv, h), _head_only(dov, h)
                    s_scr[j * heads + h] = lax.dot_general(qms[j, h], k_of[j](h), _NT, preferred_element_type=F32)
                    dp_scr[j * heads + h] = lax.dot_general(doms[j, h], v_of[j](h), _NT, preferred_element_type=F32)
            dsink_row = jnp.zeros((1, LANES), F32)
            for j in range(nq):
                st = st_ref[steps.rows_of(j), :]
                valid = _band_mask(steps.has_prev(i, j), max_dist)
                for h in range(heads):
                    lse_h = st[:, h:h + 1]
                    delta = st[:, DELTA_LANE + h:DELTA_LANE + h + 1]
                    p = jnp.where(valid, jnp.exp(s_scr[j * heads + h] - lse_h), 0.0)
                    p_scr[j * heads + h] = p.astype(BF16)
                    ds_scr[j * heads + h] = (p * (dp_scr[j * heads + h] - delta)).astype(BF16)
                    if sink is not None:
                        ds_sink = jnp.sum(-jnp.exp(sink_ref[h] - lse_h) * delta, axis=0, keepdims=True)
                        dsink_row = dsink_row + jnp.where(lane == h, ds_sink, 0.0)
            for j in range(nq):
                for pr in range(heads // 2):
                    he, ho = 2 * pr, 2 * pr + 1
                    even = jnp.dot(ds_scr[j * heads + he], k_of[j](he), preferred_element_type=F32)
                    odd = jnp.dot(ds_scr[j * heads + ho], k_of[j](ho), preferred_element_type=F32)
                    dq_ref[steps.rows_of(j), LANES * pr:LANES * (pr + 1)] = (
                        jnp.where(lane_q < HEAD_DIM, even, odd).astype(BF16))
            if steps.consecutive:
                dk_ref[...] = kcar[...].astype(BF16)
                dv_ref[...] = vcar[...].astype(BF16)
            for j in range(nq):
                for slab in range(kw // LANES):
                    acc = {}
                    for h in range(heads):
                        if (h // group) // 2 != slab:
                            continue
                        key = k_of[j].is_swapped(h)
                        dk_h = lax.dot_general(ds_scr[j * heads + h], qms[j, h], _TN, preferred_element_type=F32)
                        dv_h = lax.dot_general(p_scr[j * heads + h], doms[j, h], _TN, preferred_element_type=F32)
                        acc[key] = (dk_h, dv_h) if key not in acc else (acc[key][0] + dk_h, acc[key][1] + dv_h)
                    dk_j, dv_j = acc.get(False, (None, None))
                    if True in acc:
                        unswap = lambda t: jnp.concatenate([t[:, HEAD_DIM:], t[:, :HEAD_DIM]], axis=1)
                        dk_s, dv_s = unswap(acc[True][0]), unswap(acc[True][1])
                        dk_j = dk_s if dk_j is None else dk_j + dk_s
                        dv_j = dv_s if dv_j is None else dv_j + dv_s
                    sl = slice(LANES * slab, LANES * (slab + 1))
                    own_rows = steps.rows_of(j)
                    if not steps.consecutive:
                        dk_ref[own_rows, sl] = (kcar[own_rows, sl] + dk_j[0:BLOCK]).astype(BF16)
                        dv_ref[own_rows, sl] = (vcar[own_rows, sl] + dv_j[0:BLOCK]).astype(BF16)
                    elif j == 0:
                        last = steps.rows_of(nq - 1)
                        dk_ref[last, sl] = (kcar[last, sl] + dk_j[0:BLOCK]).astype(BF16)
                        dv_ref[last, sl] = (vcar[last, sl] + dv_j[0:BLOCK]).astype(BF16)
                    else:
                        before = steps.rows_of(j - 1)
                        kcar[before, sl] += dk_j[0:BLOCK]
                        vcar[before, sl] += dv_j[0:BLOCK]
                    kcar[own_rows, sl] = dk_j[BLOCK:2 * BLOCK]
                    vcar[own_rows, sl] = dv_j[BLOCK:2 * BLOCK]
            if sink is not None:
                dsink_ref[0:1, :] += dsink_row

        @pl.when(i == steps.inner)
        def _():
            dk_ref[...] = kcar[...].astype(BF16)
            dv_ref[...] = vcar[...].astype(BF16)

        if n_rs:
            @pl.when(flat == n_flat - 1)
            def _():
                exchange.finish()

    own, prev = (lambda w: steps.own(w, clamp=True)), (lambda w: steps.prev(w, clamp=True))
    rs_shapes = [t.shape[1:] for t in reduce_scatter]
    in_specs = ([own(qw), prev(kw), own(kw), prev(kw), own(kw), own(qw), own(LANES)]
                + [pl.BlockSpec(memory_space=pl.ANY)] * n_rs)
    args = [q, k, k, v, v, d_out, stat, *reduce_scatter]
    out_specs = [own(qw), steps.late(kw), steps.late(kw)] + [_full(s) for s in rs_shapes]
    out_shape = [jax.ShapeDtypeStruct((seq, qw), BF16), jax.ShapeDtypeStruct((seq, kw), BF16),
                 jax.ShapeDtypeStruct((seq, kw), BF16)] + [jax.ShapeDtypeStruct(s, F32) for s in rs_shapes]
    if sink is not None:
        in_specs = [pl.BlockSpec(memory_space=pltpu.SMEM)] + in_specs
        args = [sink] + args
        out_specs = [_full((8, LANES))] + out_specs
        out_shape = [jax.ShapeDtypeStruct((8, LANES), F32)] + out_shape
    n_hb = nq * heads
    res = pl.pallas_call(
        body, name=name, grid=(steps.outer, steps.inner + 1), in_specs=in_specs, out_specs=out_specs,
        out_shape=out_shape,
        scratch_shapes=[pltpu.VMEM((steps.rows, kw), F32), pltpu.VMEM((steps.rows, kw), F32)]
        + [pltpu.VMEM((n_hb, BLOCK, 2 * BLOCK), F32)] * 2 + [pltpu.VMEM((n_hb, BLOCK, 2 * BLOCK), BF16)] * 2
        + (_ReduceScatter.scratch_shapes(rs_shapes) if n_rs else []),
        compiler_params=_params(dimension_semantics=("arbitrary", "arbitrary")),
    )(*args)
    if sink is not None:
        return (*res[1:4], res[0], *res[4:])
    return res


def _cross_fwd(q, mk, mv, tq=512):
    seq = q.shape[0]

    def body(q_ref, mk_ref, mv_ref, o_ref, lse_ref, s_scr, p_scr):
        qv = q_ref[...]
        k_of, v_of = _KvSlabs(mk_ref[...], 1), _KvSlabs(mv_ref[...], 1)
        lane = lax.broadcasted_iota(jnp.int32, (tq, LANES), 1)
        lse_tile = jnp.zeros((tq, LANES), F32)
        for h in range(C_HEADS):
            s_scr[h] = lax.dot_general(_head_only(qv, h), k_of(h), _NT, preferred_element_type=F32)
        ls = []
        for h in range(C_HEADS):
            s = s_scr[h]
            m = jnp.max(s, axis=-1, keepdims=True)
            p = jnp.exp(s - m)
            l = jnp.sum(p, axis=-1, keepdims=True)
            p_scr[h] = p.astype(BF16)
            ls.append(l)
            lse_tile = jnp.where(lane == h, m + jnp.log(l), lse_tile)
        for pr in range(C_HEADS // 2):
            even = jnp.dot(p_scr[2 * pr], v_of(2 * pr), preferred_element_type=F32) / ls[2 * pr]
            odd = jnp.dot(p_scr[2 * pr + 1], v_of(2 * pr + 1), preferred_element_type=F32) / ls[2 * pr + 1]
            o_ref[:, LANES * pr:LANES * (pr + 1)] = jnp.where(lane < HEAD_DIM, even, odd).astype(BF16)
        lse_ref[...] = lse_tile

    return pl.pallas_call(
        body, name="cross_fwd", grid=(seq // tq,),
        in_specs=[_row(tq, C_W), _full((N_MEM, C_W)), _full((N_MEM, C_W))],
        out_specs=[_row(tq, C_W), _row(tq, LANES)],
        out_shape=[jax.ShapeDtypeStruct((seq, C_W), BF16), jax.ShapeDtypeStruct((seq, LANES), F32)],
        scratch_shapes=[pltpu.VMEM((C_HEADS, tq, N_MEM), F32), pltpu.VMEM((C_HEADS, tq, N_MEM), BF16)],
        compiler_params=_params(dimension_semantics=("arbitrary",)),
    )(q, mk, mv)


def _cross_bwd(q, mk, mv, d_out, stat, tq=512):
    seq = q.shape[0]

    def body(q_ref, mk_ref, mv_ref, do_ref, st_ref, dq_ref, dmk_ref, dmv_ref, s_scr, dp_scr, p_scr, ds_scr):
        @pl.when(pl.program_id(0) == 0)
        def _():
            dmk_ref[...] = jnp.zeros_like(dmk_ref)
            dmv_ref[...] = jnp.zeros_like(dmv_ref)

        qv, dov, st = q_ref[...], do_ref[...], st_ref[...]
        k_of, v_of = _KvSlabs(mk_ref[...], 1), _KvSlabs(mv_ref[...], 1)
        qms = [_head_only(qv, h) for h in range(C_HEADS)]
        doms = [_head_only(dov, h) for h in range(C_HEADS)]
        for h in range(C_HEADS):
            s_scr[h] = lax.dot_general(qms[h], k_of(h), _NT, preferred_element_type=F32)
            dp_scr[h] = lax.dot_general(doms[h], v_of(h), _NT, preferred_element_type=F32)
        for h in range(C_HEADS):
            p = jnp.exp(s_scr[h] - st[:, h:h + 1])
            p_scr[h] = p.astype(BF16)
            ds_scr[h] = (p * (dp_scr[h] - st[:, DELTA_LANE + h:DELTA_LANE + h + 1])).astype(BF16)
        lane = lax.broadcasted_iota(jnp.int32, (tq, LANES), 1)
        for pr in range(C_HEADS // 2):
            sl = slice(LANES * pr, LANES * (pr + 1))
            even = jnp.dot(ds_scr[2 * pr], k_of(2 * pr), preferred_element_type=F32)
            odd = jnp.dot(ds_scr[2 * pr + 1], k_of(2 * pr + 1), preferred_element_type=F32)
            dq_ref[:, sl] = jnp.where(lane < HEAD_DIM, even, odd).astype(BF16)
            dmk_ref[:, sl] += (lax.dot_general(ds_scr[2 * pr], qms[2 * pr], _TN, preferred_element_type=F32)
                               + lax.dot_general(ds_scr[2 * pr + 1], qms[2 * pr + 1], _TN, preferred_element_type=F32))
            dmv_ref[:, sl] += (lax.dot_general(p_scr[2 * pr], doms[2 * pr], _TN, preferred_element_type=F32)
                               + lax.dot_general(p_scr[2 * pr + 1], doms[2 * pr + 1], _TN, preferred_element_type=F32))

    return pl.pallas_call(
        body, name="cross_bwd", grid=(seq // tq,),
        in_specs=[_row(tq, C_W), _full((N_MEM, C_W)), _full((N_MEM, C_W)), _row(tq, C_W), _row(tq, LANES)],
        out_specs=[_row(tq, C_W), _full((N_MEM, C_W)), _full((N_MEM, C_W))],
        out_shape=[jax.ShapeDtypeStruct((seq, C_W), BF16), jax.ShapeDtypeStruct((N_MEM, C_W), F32),
                   jax.ShapeDtypeStruct((N_MEM, C_W), F32)],
        scratch_shapes=[pltpu.VMEM((C_HEADS, tq, N_MEM), F32)] * 2 + [pltpu.VMEM((C_HEADS, tq, N_MEM), BF16)] * 2,
        compiler_params=_params(dimension_semantics=("arbitrary",)),
    )(q, mk, mv, d_out, stat)


def _per_head(tile, width):
    rows = tile.shape[0]
    return jnp.concatenate(
        [jnp.broadcast_to(tile[:, h:h + 1], (rows, HEAD_DIM)) for h in range(width // HEAD_DIM)], axis=1)


def _with_delta(lse_tile, prod):
    rows = lse_tile.shape[0]
    lane = lax.broadcasted_iota(jnp.int32, (rows, LANES), 1)
    tile = lse_tile
    for h in range(prod.shape[1] // HEAD_DIM):
        d = jnp.sum(prod[:, HEAD_DIM * h:HEAD_DIM * (h + 1)], axis=-1, keepdims=True)
        tile = jnp.where(lane == DELTA_LANE + h, d, tile)
    return tile


def _mid(oa, lse_a, ob, lse_b, oc, lse_c, gate, x, target, w_out_full, post_g, tm=512):
    seq = x.shape[0]
    n_b = B_W // LANES

    def body(oa_ref, la_ref, b1_ref, l1_ref, b4_ref, l4_ref, b16_ref, l16_ref, oc_ref, lc_ref,
             gate_ref, x_ref, t_ref, w_ref, pg_ref,
             dh_ref, dg_ref, doa_ref, sa_ref, dob1_ref, sb1_ref, dob4_ref, sb4_ref, dob16_ref, sb16_ref,
             doc_ref, sc_ref, dw_ref, st_ref, scr_b4, scr_b16, scr_l4, scr_l16, scr_do, scr_sb):
        @pl.when(pl.program_id(0) == 0)
        def _():
            dw_ref[...] = jnp.zeros_like(dw_ref)
            st_ref[...] = jnp.zeros_like(st_ref)

        b1, l1 = b1_ref[...].astype(F32), l1_ref[...]
        b4, l4 = _load_permuted(b4_ref, scr_b4, 4), _load_permuted(l4_ref, scr_l4, 4)
        b16, l16 = _load_permuted(b16_ref, scr_b16, 16), _load_permuted(l16_ref, scr_l16, 16)
        lm = jnp.maximum(jnp.maximum(l1, l4), l16)
        e1, e4, e16 = jnp.exp(l1 - lm), jnp.exp(l4 - lm), jnp.exp(l16 - lm)
        den = e1 + e4 + e16
        lse_b_tile = lm + jnp.log(den)
        ob_v = _per_head(e1 / den, B_W) * b1 + _per_head(e4 / den, B_W) * b4 + _per_head(e16 / den, B_W) * b16
        o_all = jnp.concatenate([oa_ref[...].astype(F32), ob_v, oc_ref[...].astype(F32)], axis=1)
        g = gate_ref[...].astype(F32)
        sig = 1.0 / (1.0 + jnp.exp(-g))
        silu = g * sig
        y = (o_all * silu).astype(BF16)
        w = w_ref[...]
        z = jnp.dot(y, w, preferred_element_type=F32)
        rz = lax.rsqrt(jnp.mean(z * z, axis=-1, keepdims=True) + RMS_EPS)
        hn = z * rz
        pg = pg_ref[...]
        err = (x_ref[...] + hn * pg) - t_ref[...]
        loss = 0.5 * jnp.sum(jnp.mean(err * err, axis=-1, keepdims=True), axis=0, keepdims=True)
        dh = err * (1.0 / D_MODEL)
        dh_ref[...] = dh.astype(BF16)
        st_ref[0:1, :] += jnp.sum(dh * hn, axis=0, keepdims=True)
        st_ref[1:2, :] += jnp.broadcast_to(loss, (1, D_MODEL))
        dhn = dh * pg
        dz = (rz * (dhn - hn * jnp.mean(dhn * hn, axis=-1, keepdims=True))).astype(BF16)
        dy = lax.dot_general(dz, w, _NT, preferred_element_type=F32)
        dw_ref[...] += lax.dot_general(y, dz, _TN, preferred_element_type=F32)
        dg_ref[...] = (dy * o_all * (sig * (1.0 + g * (1.0 - sig)))).astype(BF16)
        d_o = (dy * silu).astype(BF16)
        prod = d_o.astype(F32) * o_all
        doa_ref[...] = d_o[:, 0:A_W]
        sa_ref[...] = _with_delta(la_ref[...], prod[:, 0:A_W])
        doc_ref[...] = d_o[:, A_W + B_W:D_MODEL]
        sc_ref[...] = _with_delta(lc_ref[...], prod[:, A_W + B_W:D_MODEL])
        d_ob = d_o[:, A_W:A_W + B_W]
        stat_b = _with_delta(lse_b_tile, prod[:, A_W:A_W + B_W])
        dob1_ref[...] = d_ob
        sb1_ref[...] = stat_b
        _put(scr_do, d_ob.astype(F32))
        _put(scr_sb, stat_b)
        _store_permuted(scr_do, dob4_ref, 4, BF16)
        _store_permuted(scr_sb, sb4_ref, 4, F32)
        _store_permuted(scr_do, dob16_ref, 16, BF16)
        _store_permuted(scr_sb, sb16_ref, 16, F32)

    p4 = lambda w: _perm_spec(tm, 4, w)
    p16 = lambda w: _perm_spec(tm, 16, w)
    in_specs = [_row(tm, A_W), _row(tm, LANES), _row(tm, B_W), _row(tm, LANES), p4(B_W), p4(LANES), p16(B_W), p16(LANES),
                _row(tm, C_W), _row(tm, LANES), _row(tm, D_MODEL), _row(tm, D_MODEL), _row(tm, D_MODEL),
                _full((D_MODEL, D_MODEL)), _full((1, D_MODEL))]
    sds = jax.ShapeDtypeStruct
    v4 = lambda w, dt: sds((seq // (BLOCK * 4), 4, BLOCK, w), dt)
    v16 = lambda w, dt: sds((seq // (BLOCK * 16), 16, BLOCK, w), dt)
    out_specs = [_row(tm, D_MODEL), _row(tm, D_MODEL), _row(tm, A_W), _row(tm, LANES), _row(tm, B_W), _row(tm, LANES),
                 p4(B_W), p4(LANES), p16(B_W), p16(LANES), _row(tm, C_W), _row(tm, LANES),
                 _full((D_MODEL, D_MODEL)), _full((8, D_MODEL))]
    out_shape = [sds((seq, D_MODEL), BF16), sds((seq, D_MODEL), BF16), sds((seq, A_W), BF16), sds((seq, LANES), F32),
                 sds((seq, B_W), BF16), sds((seq, LANES), F32), v4(B_W, BF16), v4(LANES, F32), v16(B_W, BF16),
                 v16(LANES, F32), sds((seq, C_W), BF16), sds((seq, LANES), F32),
                 sds((D_MODEL, D_MODEL), F32), sds((8, D_MODEL), F32)]
    res = pl.pallas_call(
        body, name="mid", grid=(seq // tm,), in_specs=in_specs, out_specs=out_specs, out_shape=out_shape,
        scratch_shapes=[pltpu.VMEM((n_b, tm, LANES), F32), pltpu.VMEM((n_b, tm, LANES), F32),
                        pltpu.VMEM((1, tm, LANES), F32), pltpu.VMEM((1, tm, LANES), F32),
                        pltpu.VMEM((n_b, tm, LANES), F32), pltpu.VMEM((1, tm, LANES), F32)],
        compiler_params=_params(dimension_semantics=("arbitrary",)),
    )(oa, lse_a, ob[1], lse_b[1], _perm_view(ob[4], 4), _perm_view(lse_b[4], 4), _perm_view(ob[16], 16),
      _perm_view(lse_b[16], 16), oc, lse_c, gate, x, target, w_out_full, post_g)
    dh, d_gate, do_a, st_a, do_b1, st_b1, do_b4, st_b4, do_b16, st_b16, do_c, st_c, d_wout, stats = res
    flat = lambda t: t.reshape(seq, t.shape[-1])
    d_b = {1: (do_b1, st_b1), 4: (flat(do_b4), flat(st_b4)), 16: (flat(do_b16), flat(st_b16))}
    return dh, d_gate, (do_a, st_a), d_b, (do_c, st_c), d_wout, stats


def _inproj_bwd(x, u, dh, pre_g, w_in_full, tabs, dqa, dka, dva, dqkv_b, dqc, dgate, tm=512):
    seq = x.shape[0]
    n_b = B_W // LANES

    def body(x_ref, u_ref, dh_ref, g_ref, w_hbm, c_ref, up_ref, dn_ref, dqa_ref, dka_ref, dva_ref,
             dq1, dk1, dv1, dq4, dk4, dv4, dq16, dk16, dv16, dqc_ref, dg_ref,
             gx_ref, dw_ref, st_ref, scr4, scr16, w_scr, w_sems, dp_ref):
        _stage_w_in(w_hbm, w_scr, w_sems)

        @pl.when(pl.program_id(0) == 0)
        def _():
            st_ref[...] = jnp.zeros_like(st_ref)
            dw_ref[...] = jnp.zeros_like(dw_ref)

        c, up, dn = c_ref[...], -up_ref[...], -dn_ref[...]
        unrot = lambda t: _rotate(t, c, up, dn)
        total = lambda r1, r4, r16: (r1[...].astype(F32) + _load_permuted(r4, scr4, 4)
                                     + _load_permuted(r16, scr16, 16))
        dp_ref[:, 0:384] = (unrot(dqa_ref[...].astype(F32)) * SCALE).astype(BF16)
        dp_ref[:, 384:512] = unrot(dka_ref[...].astype(F32)).astype(BF16)
        dp_ref[:, 512:640] = dva_ref[...]
        dp_ref[:, 640:1024] = dg_ref[:, 0:384]
        dp_ref[:, 1024:1408] = (unrot(total(dq1, dq4, dq16)) * SCALE).astype(BF16)
        dp_ref[:, 1408:1792] = unrot(total(dk1, dk4, dk16)).astype(BF16)
        dp_ref[:, 1792:2176] = total(dv1, dv4, dv16).astype(BF16)
        dp_ref[:, 2176:2560] = dg_ref[:, 384:768]
        dp_ref[:, 2560:2816] = (dqc_ref[...].astype(F32) * SCALE).astype(BF16)
        dp_ref[:, 2816:3072] = dg_ref[:, 768:1024]
        du = lax.dot_general(dp_ref[...], w_scr[...], _NT, preferred_element_type=F32)
        res = lax.dot_general(u_ref[...], dp_ref[...], _TN, preferred_element_type=F32)
        for k in range(N_DEV):
            dw_ref[k] += res[:, SHARD_IN * k:SHARD_IN * (k + 1)]
        xv = x_ref[...]
        r = lax.rsqrt(jnp.mean(xv * xv, axis=-1, keepdims=True) + RMS_EPS)
        xh = xv * r
        st_ref[0:1, :] += jnp.sum(du * xh, axis=0, keepdims=True)
        dxh = du * g_ref[...]
        gx_ref[...] = dh_ref[...].astype(F32) + r * (dxh - xh * jnp.mean(dxh * xh, axis=-1, keepdims=True))

    in_specs = ([_row(tm, D_MODEL), _row(tm, D_MODEL), _row(tm, D_MODEL), _full((1, D_MODEL)),
                 pl.BlockSpec(memory_space=pl.ANY),
                 _row(tm, LANES), _row(tm, LANES), _row(tm, LANES), _row(tm, A_W), _row(tm, A_KV_W), _row(tm, A_KV_W)]
                + [_row(tm, B_W)] * 3 + [_perm_spec(tm, 4, B_W)] * 3 + [_perm_spec(tm, 16, B_W)] * 3
                + [_row(tm, C_W), _row(tm, D_MODEL)])
    dw_spec = pl.BlockSpec((N_DEV, D_MODEL, SHARD_IN), lambda i: (0, 0, 0), pipeline_mode=pl.Buffered(1))
    return pl.pallas_call(
        body, name="inproj_bwd", grid=(seq // tm,), in_specs=in_specs,
        out_specs=[_row(tm, D_MODEL), dw_spec, _full((8, D_MODEL))],
        out_shape=[jax.ShapeDtypeStruct((seq, D_MODEL), F32), jax.ShapeDtypeStruct((N_DEV, D_MODEL, SHARD_IN), F32),
                   jax.ShapeDtypeStruct((8, D_MODEL), F32)],
        scratch_shapes=[pltpu.VMEM((n_b, tm, LANES), F32), pltpu.VMEM((n_b, tm, LANES), F32)] + _w_in_scratch()
        + [pltpu.VMEM((tm, D_IN), BF16)],
        compiler_params=_params(dimension_semantics=("arbitrary",)),
    )(x, u, dh, pre_g, w_in_full, *tabs, dqa, dka, dva, *dqkv_b[1], *[_perm_view(t, 4) for t in dqkv_b[4]],
      *[_perm_view(t, 16) for t in dqkv_b[16]], dqc, dgate)


class _ReduceScatter:
    def __init__(self, ins, outs, scratch):
        self.n = n = len(ins)
        self.ins, self.outs = ins, outs
        self.mine, self.got, self.snd, self.rcv = (scratch[n * t:n * (t + 1)] for t in range(4))
        self.load_sems, self.d2d_send, self.d2d_recv, self.ici_send, self.ici_recv = scratch[4 * n:]
        self.pos = _mesh_pos()
        self.pairs = [(a, kk) for kk in (3, 1, 2) for a in range(n)]

    @staticmethod
    def scratch_shapes(shapes):
        return ([pltpu.VMEM((4,) + s, F32) for s in shapes] + [pltpu.VMEM((4,) + s, F32) for s in shapes]
                + [pltpu.VMEM((3,) + s, BF16) for s in shapes] + [pltpu.VMEM((3,) + s, BF16) for s in shapes]
                + [pltpu.SemaphoreType.DMA((len(shapes), 4))] * 5)

    def _chip(self, kk):
        x, y, _ = self.pos
        return (1 - x if kk & 2 else x, 1 - y if kk & 1 else y)

    def _load(self, a, kk):
        block = _dev_index((*self._chip(kk), self.pos[2]))
        return pltpu.make_async_copy(self.ins[a].at[block], self.mine[a].at[kk], self.load_sems.at[a, kk])

    def _swap(self, a, kk):
        x, y, c = self.pos
        return pltpu.make_async_remote_copy(
            src_ref=self.ins[a].at[_dev_index((*self._chip(kk), 1 - c))], dst_ref=self.got[a].at[kk],
            send_sem=self.d2d_send.at[a, kk], recv_sem=self.d2d_recv.at[a, kk],
            device_id=(x, y, 1 - c), device_id_type=MESH_ID)

    def _hop(self, a, kk):
        return pltpu.make_async_remote_copy(
            src_ref=self.snd[a].at[kk - 1], dst_ref=self.rcv[a].at[kk - 1], send_sem=self.ici_send.at[a, kk],
            recv_sem=self.ici_recv.at[a, kk], device_id=(*self._chip(kk), self.pos[2]), device_id_type=MESH_ID)

    def start(self):
        for kk in (3, 1, 2, 0):
            for a in range(self.n):
                self._load(a, kk).start()
                self._swap(a, kk).start()

    def send_chip_sums(self):
        for a, kk in self.pairs:
            self._load(a, kk).wait()
            self._swap(a, kk).wait_recv()
            self.snd[a][kk - 1] = (self.mine[a][kk] + self.got[a][kk]).astype(BF16)
            self._hop(a, kk).start()

    def finish(self):
        for a in range(self.n):
            self._load(a, 0).wait()
            self._swap(a, 0).wait_recv()
            acc = self.mine[a][0] + self.got[a][0]
            for kk in (1, 2, 3):
                self._hop(a, kk).wait_recv()
                acc = acc + self.rcv[a][kk - 1].astype(F32)
            self.outs[a][...] = acc
        for kk in range(4):
            for a in range(self.n):
                self._swap(a, kk).wait_send()
        for a, kk in self.pairs:
            self._hop(a, kk).wait_send()


def _local_step(x, mem, pre_g, w_in, sink, mem_g, w_mem, w_out, post_g, target):
    u, *tabs, w_in_full = _prep(x, pre_g, w_in)
    qa, ka, va, qkv_b, qc, gate, w_mem_all, w_out_all = _inproj(u, w_in_full, tabs, w_mem, w_out)
    w_mem_full = w_mem_all.reshape(D_MODEL, 2 * C_W)
    w_out_full = w_out_all.reshape(D_MODEL, D_MODEL)
    mn, mk, mv = _memkv_fwd(mem, mem_g, w_mem_full)

    a_cfg = dict(dil=1, heads=A_HEADS, group=A_GROUP, max_dist=BLOCK - 1, nq=ATTN_BLOCKS_PER_STEP)
    b_cfgs = {dil: dict(dil=dil, heads=B_HEADS, group=1, max_dist=win // dil, nq=ATTN_BLOCKS_PER_STEP)
              for win, dil in B_CONFIGS}
    oa, lse_a = _banded_fwd(qa, ka, va, sink, name="attn_a_fwd", **a_cfg)
    ob, lse_b = {}, {}
    for dil, cfg in b_cfgs.items():
        ob[dil], lse_b[dil] = _banded_fwd(*qkv_b[dil], None, name=f"attn_b{dil}_fwd", **cfg)
    oc, lse_c = _cross_fwd(qc, mk, mv)

    dh, d_gate, d_a, d_b, d_c, d_wout, st_mid = _mid(oa, lse_a, ob, lse_b, oc, lse_c, gate, x, target, w_out_full, post_g)

    dqc, dmk, dmv = _cross_bwd(qc, mk, mv, *d_c)
    d_wmem, st_mem = _memkv_bwd(mem, mem_g, mn, w_mem_full, dmk, dmv)
    dqkv_b = {dil: _banded_bwd(*qkv_b[dil], *d_b[dil], None, name=f"attn_b{dil}_bwd", **cfg)
              for dil, cfg in b_cfgs.items()}
    dqa, dka, dva, dsink, g_wmem, g_wout = _banded_bwd(
        qa, ka, va, *d_a, sink, name="attn_a_bwd", **a_cfg,
        reduce_scatter=(d_wmem.reshape(N_DEV, SHARD_ROWS, 2 * C_W), d_wout.reshape(N_DEV, SHARD_ROWS, D_MODEL)))

    grad_x, d_win, st_pre = _inproj_bwd(x, u, dh, pre_g, w_in_full, tabs, dqa, dka, dva, dqkv_b, dqc, d_gate)

    dsink_row = jnp.pad(dsink[0:1, :], ((0, 0), (0, D_MODEL - LANES)))
    stats = jnp.concatenate([st_pre[0:1], st_mem[0:1], st_mid[0:1], dsink_row, st_mid[1:2],
                             jnp.zeros((3, D_MODEL), F32)], axis=0)
    return grad_x, d_win, g_wmem, g_wout, stats


def _prep(x, pre_g, w_in, tm=1024, parts=2):
    seq = x.shape[0]
    n_steps = seq // tm
    rows = D_MODEL // parts
    pass_on_at = [max(n_steps - parts + a, 0) for a in range(parts)]
    j = jnp.arange(LANES) % HEAD_DIM
    freq = (ROPE_THETA ** (-(2 * (j % (ROT_DIM // 2))).astype(F32) / ROT_DIM))[None, :]

    def body(x_ref, g_ref, f_ref, win_ref, u_ref, c_ref, up_ref, dn_ref, win_out, win_b,
             send_sems, recv_sems, local_sems):
        step = pl.program_id(0)
        px, py, pc = _mesh_pos()
        me, sibling = (px, py, pc), (px, py, 1 - pc)
        chips = [(1 - px, py), (px, 1 - py), (1 - px, 1 - py)]

        def src(a):
            return win_b.at[pl.ds(rows * a, rows)]

        def slot(a, p):
            return win_out.at[_dev_index(p), pl.ds(rows * a, rows)]

        def copy(a, k, block, to, own=False):
            return pltpu.make_async_remote_copy(
                src_ref=src(a) if own else slot(a, block), dst_ref=slot(a, block),
                send_sem=send_sems.at[a, k], recv_sem=recv_sems.at[a, k], device_id=to, device_id_type=MESH_ID)

        def first_sends(a):
            return [copy(a, 1 + k, me, (*chip, pc), own=True) for k, chip in enumerate(chips)] + [
                copy(a, 0, me, sibling, own=True)]

        def local(a):
            return pltpu.make_async_copy(src(a), slot(a, me), local_sems.at[a])

        @pl.when(step == 0)
        def _():
            win_b[...] = win_ref[...].astype(BF16)
            for a in range(parts):
                local(a).start()
                for cp in first_sends(a):
                    cp.start()

        for a in range(parts):
            @pl.when(step == pass_on_at[a])
            def _(a=a):
                for k, chip in enumerate(chips):
                    copy(a, 1 + k, (*chip, pc), me).wait_recv()
                    copy(a, 4 + k, (*chip, pc), sibling).start()

        xv = x_ref[...]
        r = lax.rsqrt(jnp.mean(xv * xv, axis=-1, keepdims=True) + RMS_EPS)
        u_ref[...] = ((xv * r) * g_ref[...]).astype(BF16)
        pos = (lax.broadcasted_iota(jnp.int32, (tm, LANES), 0) + step * tm).astype(F32)
        head_lane = lax.broadcasted_iota(jnp.int32, (tm, LANES), 1) % HEAD_DIM
        ang = pos * f_ref[...]
        cos, sin = jnp.cos(ang), jnp.sin(ang)
        half = ROT_DIM // 2
        c_ref[...] = jnp.where(head_lane < ROT_DIM, cos, 1.0)
        up_ref[...] = jnp.where((head_lane >= half) & (head_lane < ROT_DIM), sin, 0.0)
        dn_ref[...] = jnp.where(head_lane < half, -sin, 0.0)

        @pl.when(step == n_steps - 1)
        def _():
            for a in range(parts):
                copy(a, 0, sibling, me).wait_recv()
                for k, chip in enumerate(chips):
                    copy(a, 4 + k, (*chip, 1 - pc), me).wait_recv()
            for a in range(parts):
                for cp in first_sends(a):
                    cp.wait_send()
                for k, chip in enumerate(chips):
                    copy(a, 4 + k, (*chip, pc), sibling).wait_send()
                local(a).wait()

    return pl.pallas_call(
        body, name="prep", grid=(n_steps,),
        in_specs=[_row(tm, D_MODEL), _full((1, D_MODEL)), _full((1, LANES)), _full(w_in.shape)],
        out_specs=[_row(tm, D_MODEL), _row(tm, LANES), _row(tm, LANES), _row(tm, LANES),
                   pl.BlockSpec(memory_space=pl.ANY)],
        out_shape=[jax.ShapeDtypeStruct((seq, D_MODEL), BF16)] + [jax.ShapeDtypeStruct((seq, LANES), F32)] * 3
        + [jax.ShapeDtypeStruct((N_DEV,) + w_in.shape, BF16)],
        scratch_shapes=[pltpu.VMEM(w_in.shape, BF16), pltpu.SemaphoreType.DMA((parts, 7)),
                        pltpu.SemaphoreType.DMA((parts, 7)), pltpu.SemaphoreType.DMA((parts,))],
        compiler_params=_params(dimension_semantics=("arbitrary",)),
    )(x, pre_g, freq, w_in)


def _exchange_grads(d_win, stats):
    def body(win, st, g_win, r_st, send_sems, recv_sems, local_sem, *scratch):
        exchange = _ReduceScatter((win,), (g_win,), scratch)
        exchange.start()
        pos = _mesh_pos()
        me = _dev_index(pos)
        own = pltpu.make_async_copy(st, r_st.at[me], local_sem)
        own.start()
        copies = []
        for s in range(1, N_DEV):
            peer = _xor_peer(pos, s)
            mk = lambda slot: pltpu.make_async_remote_copy(
                src_ref=st, dst_ref=r_st.at[slot], send_sem=send_sems.at[s], recv_sem=recv_sems.at[s],
                device_id=peer, device_id_type=MESH_ID)
            send, arrival = mk(me), mk(_dev_index(peer))
            send.start()
            copies.append((send, arrival))
        exchange.send_chip_sums()
        exchange.finish()
        for send, arrival in copies:
            arrival.wait_recv()
            send.wait_send()
        own.wait()

    hbm = pl.BlockSpec(memory_space=pl.ANY)
    shard = d_win.shape[1:]
    return pl.pallas_call(
        body, name="exchange_grads", in_specs=[hbm, hbm],
        out_specs=[pl.BlockSpec(memory_space=pltpu.VMEM), hbm],
        out_shape=[jax.ShapeDtypeStruct(shard, F32), jax.ShapeDtypeStruct((N_DEV,) + stats.shape, F32)],
        scratch_shapes=[pltpu.SemaphoreType.DMA((N_DEV,)), pltpu.SemaphoreType.DMA((N_DEV,)), pltpu.SemaphoreType.DMA(())]
        + _ReduceScatter.scratch_shapes([shard]),
        compiler_params=_params(),
    )(d_win, stats)


WEIGHT_ORDER = ("pre_norm", "w_in", "sink_a", "mem_norm", "w_mem_kv", "w_out", "post_norm")


def _adamw_all(grads, r_stats, weights, moments_m, moments_v):
    n = len(WEIGHT_ORDER)
    stat_row = {"pre_norm": 0, "mem_norm": 1, "post_norm": 2, "sink_a": 3}

    def body(*refs):
        gw_in, gw_mem, gw_out, st_ref = refs[0:4]
        w_refs, m_refs, v_refs = (dict(zip(WEIGHT_ORDER, refs[4 + n * t:4 + n * (t + 1)])) for t in range(3))
        loss_ref = refs[4 + 3 * n]
        outs = refs[5 + 3 * n:]
        g_small = st_ref[0]
        for s in range(1, N_DEV):
            g_small = g_small + st_ref[s]
        loss_ref[...] = g_small[4:5, 0:1]
        big = {"w_in": gw_in, "w_mem_kv": gw_mem, "w_out": gw_out}
        for i, name in enumerate(WEIGHT_ORDER):
            if name in big:
                g = big[name][...]
                at = lambda ref: ref[0]
            else:
                width = w_refs[name].shape[-1]
                g = g_small[stat_row[name]:stat_row[name] + 1, 0:width]
                at = lambda ref: ref[...]
            m2 = ADAM_B1 * at(m_refs[name]) + (1.0 - ADAM_B1) * g
            v2 = ADAM_B2 * at(v_refs[name]) + (1.0 - ADAM_B2) * (g * g)
            m_hat = m2 / (1.0 - ADAM_B1 ** ADAM_STEP)
            v_hat = v2 / (1.0 - ADAM_B2 ** ADAM_STEP)
            delta = -ADAM_LR * (m_hat / (jnp.sqrt(v_hat) + ADAM_EPS) + ADAM_WD * at(w_refs[name]))
            for kind, val in enumerate((g, delta, m2, v2)):
                out = outs[kind * n + i]
                if name in big:
                    out[0] = val
                else:
                    out[...] = val

    shapes = [weights[name].shape for name in WEIGHT_ORDER]
    res = pl.pallas_call(
        body, name="adamw_all",
        out_shape=[jax.ShapeDtypeStruct((1, 1), F32)] + [jax.ShapeDtypeStruct(sh, F32) for sh in shapes] * 4,
        compiler_params=_params(),
    )(grads["w_in"], grads["w_mem_kv"], grads["w_out"], r_stats,
      *[weights[k] for k in WEIGHT_ORDER], *[moments_m[k] for k in WEIGHT_ORDER], *[moments_v[k] for k in WEIGHT_ORDER])
    return res[0].reshape(()), res[1:]


def kernel(x, mem, pre_norm, w_in, sink_a, mem_norm, w_mem_kv, w_out, post_norm, loss_target, m_pre_norm, m_w_in, m_sink_a, m_mem_norm, m_w_mem_kv, m_w_out, m_post_norm, v_pre_norm, v_w_in, v_sink_a, v_mem_norm, v_w_mem_kv, v_w_out, v_post_norm):
    sink = jnp.pad(sink_a[0], (0, 8 - A_HEADS))
    grad_x, d_win, g_wmem, g_wout, stats = _local_step(
        x[0], mem[0], pre_norm, w_in[0], sink, mem_norm, w_mem_kv[0], w_out[0], post_norm, loss_target[0])
    g_win, r_stats = _exchange_grads(d_win, stats)
    weights = dict(pre_norm=pre_norm, w_in=w_in, sink_a=sink_a, mem_norm=mem_norm, w_mem_kv=w_mem_kv, w_out=w_out,
                   post_norm=post_norm)
    moments_m = dict(pre_norm=m_pre_norm, w_in=m_w_in, sink_a=m_sink_a, mem_norm=m_mem_norm, w_mem_kv=m_w_mem_kv,
                     w_out=m_w_out, post_norm=m_post_norm)
    moments_v = dict(pre_norm=v_pre_norm, w_in=v_w_in, sink_a=v_sink_a, mem_norm=v_mem_norm, w_mem_kv=v_w_mem_kv,
                     w_out=v_w_out, post_norm=v_post_norm)
    loss, rest = _adamw_all(dict(w_in=g_win, w_mem_kv=g_wmem, w_out=g_wout), r_stats, weights, moments_m, moments_v)
    return (loss, grad_x[None], *rest)
```

```python
import jax
import jax.numpy as jnp
from jax import lax
from jax.experimental import pallas as pl
from jax.experimental.pallas import tpu as pltpu

F32 = jnp.float32
BF16 = jnp.bfloat16

D_MODEL = 1024
HEAD_DIM = 64
ROT_DIM = 16
ROPE_THETA = 500000.0
BLOCK = 128
LANES = 128
N_MEM = 256
RMS_EPS = 1e-6
SCALE = HEAD_DIM ** -0.5
A_HEADS, A_GROUP = 6, 3
B_HEADS = 6
C_HEADS = 4
A_W, A_KV_W, B_W, C_W = 384, 128, 384, 256
D_IN = 3072
N_DEV = 8
SHARD_IN = D_IN // N_DEV
SHARD_ROWS = D_MODEL // N_DEV
B_CONFIGS = ((128, 1), (512, 4), (2048, 16))
DILS = (4, 16)
NEG = -1e30
ATTN_BLOCKS_PER_STEP = 4
DELTA_LANE = 64
VMEM_LIMIT = 56 * 1024 * 1024

ADAM_LR, ADAM_B1, ADAM_B2, ADAM_EPS, ADAM_WD, ADAM_STEP = 0.001, 0.9, 0.999, 1e-08, 0.01, 10
MESH_ID = pl.DeviceIdType.MESH


def _params(**kw):
    return pltpu.CompilerParams(vmem_limit_bytes=VMEM_LIMIT, **kw)


def _full(shape):
    n = len(shape)
    return pl.BlockSpec(shape, lambda *_: (0,) * n)


def _row(tm, w):
    return pl.BlockSpec((tm, w), lambda i: (i, 0))


def _mesh_pos():
    return lax.axis_index("x"), lax.axis_index("y"), lax.axis_index("c")


def _dev_index(pos):
    return 4 * pos[0] + 2 * pos[1] + pos[2]


def _xor_peer(pos, s):
    x, y, c = pos
    return (1 - x if s & 4 else x, 1 - y if s & 2 else y, 1 - c if s & 1 else c)


def _perm_view(a, dil):
    return a.reshape(a.shape[0] // (BLOCK * dil), dil, BLOCK, a.shape[1])


def _perm_spec(tm, dil, w):
    chunk = BLOCK * dil
    if tm >= chunk:
        return pl.BlockSpec((tm // chunk, dil, BLOCK, w), lambda i: (i, 0, 0, 0))
    per = chunk // tm
    return pl.BlockSpec((1, dil, tm // dil, w), lambda i: (i // per, 0, i % per, 0))


def _put(scr, val):
    for c in range(val.shape[1] // LANES):
        scr[c] = val[:, LANES * c:LANES * (c + 1)]


def _get(scr):
    n = scr.shape[0]
    return scr[0] if n == 1 else jnp.concatenate([scr[c] for c in range(n)], axis=1)


def _get_class(scr, r, dil):
    n, rows = scr.shape[0], scr.shape[1]
    parts = [scr.at[c][pl.ds(r, rows // dil, stride=dil), :] for c in range(n)]
    return parts[0] if n == 1 else jnp.concatenate(parts, axis=1)


def _store_permuted(scr, out_ref, dil, dtype):
    for r in range(dil):
        out_ref[0, r] = _get_class(scr, r, dil).astype(dtype)


def _fill_permuted(in_ref, scr, dil):
    n, rows = scr.shape[0], scr.shape[1]
    for r in range(dil):
        val = in_ref[0, r].astype(F32)
        for c in range(n):
            scr.at[c][pl.ds(r, rows // dil, stride=dil), :] = val[:, LANES * c:LANES * (c + 1)]


def _load_permuted(in_ref, scr, dil):
    _fill_permuted(in_ref, scr, dil)
    return _get(scr)


def _rotate128(t, c, up, dn):
    return t * c + pltpu.roll(t, 8, 1) * up + pltpu.roll(t, LANES - 8, 1) * dn


def _rotate(t, c, up, dn):
    outs = [_rotate128(t[:, LANES * j:LANES * (j + 1)], c, up, dn) for j in range(t.shape[1] // LANES)]
    return outs[0] if len(outs) == 1 else jnp.concatenate(outs, axis=1)


def _w_in_scratch():
    return [pltpu.VMEM((D_MODEL, D_IN), BF16), pltpu.SemaphoreType.DMA((N_DEV,))]


def _stage_w_in(w_hbm, w_scr, sems):
    @pl.when(pl.program_id(0) == 0)
    def _():
        copies = [pltpu.make_async_copy(w_hbm.at[k], w_scr.at[:, pl.ds(SHARD_IN * k, SHARD_IN)], sems.at[k])
                  for k in range(N_DEV)]
        for cp in copies:
            cp.start()
        for cp in copies:
            cp.wait()


def _inproj(u, w_in_full, tabs, w_mem, w_out, tm=1024):
    seq = u.shape[0]
    n_chunk = D_IN // LANES
    n_steps = seq // tm

    def body(u_ref, w_hbm, c_ref, up_ref, dn_ref, wm_ref, wo_ref, qa_ref, ka_ref, va_ref,
             qb1_ref, kb1_ref, vb1_ref, qb4_ref, kb4_ref, vb4_ref, qb16_ref, kb16_ref, vb16_ref,
             qc_ref, gate_ref, wm_all, wo_all, proj, w_scr, w_sems, wm_b, wo_b, send_sems, recv_sems, local_sems):
        step = pl.program_id(0)
        shards, gathered = (wm_b, wo_b), (wm_all, wo_all)

        def gather_copies(arriving):
            pos = _mesh_pos()
            me = _dev_index(pos)
            local = [] if arriving else [
                pltpu.make_async_copy(shards[a], gathered[a].at[me], local_sems.at[a]) for a in range(2)]
            remote = []
            for s in range(1, N_DEV):
                peer = _xor_peer(pos, s)
                for a in range(2):
                    remote.append(pltpu.make_async_remote_copy(
                        src_ref=shards[a], dst_ref=gathered[a].at[_dev_index(peer) if arriving else me],
                        send_sem=send_sems.at[a, s], recv_sem=recv_sems.at[a, s], device_id=peer,
                        device_id_type=MESH_ID))
            return local, remote

        @pl.when(step == 0)
        def _():
            wm_b[...] = wm_ref[...].astype(BF16)
            wo_b[...] = wo_ref[...].astype(BF16)
            local, sends = gather_copies(arriving=False)
            for cp in local + sends:
                cp.start()

        _stage_w_in(w_hbm, w_scr, w_sems)
        u = u_ref[...]
        for n0 in range(0, D_IN, D_MODEL):
            acc = jnp.dot(u, w_scr[:, n0:n0 + D_MODEL], preferred_element_type=F32)
            for c3 in range(D_MODEL // LANES):
                proj[n0 // LANES + c3] = acc[:, LANES * c3:LANES * (c3 + 1)]
        c, up, dn = c_ref[...], up_ref[...], dn_ref[...]

        def cols(lo, hi, rot=False, scale=None):
            parts = []
            for ch in range(lo // LANES, hi // LANES):
                t = proj[ch]
                if rot:
                    t = _rotate128(t, c, up, dn)
                if scale is not None:
                    t = t * scale
                parts.append(t)
            return parts[0] if len(parts) == 1 else jnp.concatenate(parts, axis=1)

        qa_ref[...] = cols(0, 384, True, SCALE).astype(BF16)
        ka_ref[...] = cols(384, 512, True).astype(BF16)
        va_ref[...] = cols(512, 640).astype(BF16)
        gate_ref[:, 0:384] = cols(640, 1024).astype(BF16)
        gate_ref[:, 384:768] = cols(2176, 2560).astype(BF16)
        gate_ref[:, 768:1024] = cols(2816, 3072).astype(BF16)
        qc_ref[...] = cols(2560, 2816, False, SCALE).astype(BF16)
        for ch in range(1024 // LANES, 1408 // LANES):
            proj[ch] = _rotate128(proj[ch], c, up, dn) * SCALE
        for ch in range(1408 // LANES, 1792 // LANES):
            proj[ch] = _rotate128(proj[ch], c, up, dn)
        for lo, nat, p4, p16 in ((1024, qb1_ref, qb4_ref, qb16_ref), (1408, kb1_ref, kb4_ref, kb16_ref),
                                 (1792, vb1_ref, vb4_ref, vb16_ref)):
            chunks = range(lo // LANES, lo // LANES + B_W // LANES)
            nat[...] = jnp.concatenate([proj[ch] for ch in chunks], axis=1).astype(BF16)
            for dil, ref in ((4, p4), (16, p16)):
                span = min(tm, BLOCK * dil)
                for cc in range(tm // span):
                    for rr in range(dil):
                        ref[cc, rr] = jnp.concatenate(
                            [proj.at[ch][pl.ds(cc * span + rr, span // dil, stride=dil), :] for ch in chunks],
                            axis=1).astype(BF16)

        @pl.when(step == n_steps - 1)
        def _():
            for cp in gather_copies(arriving=True)[1]:
                cp.wait_recv()
            local, sends = gather_copies(arriving=False)
            for cp in sends:
                cp.wait_send()
            for cp in local:
                cp.wait()

    nat_w = (A_W, A_KV_W, A_KV_W, B_W, B_W, B_W)
    out_specs = [_row(tm, w) for w in nat_w]
    out_shape = [jax.ShapeDtypeStruct((seq, w), BF16) for w in nat_w]
    for dil in DILS:
        out_specs += [_perm_spec(tm, dil, B_W)] * 3
        out_shape += [jax.ShapeDtypeStruct((seq // (BLOCK * dil), dil, BLOCK, B_W), BF16)] * 3
    hbm = pl.BlockSpec(memory_space=pl.ANY)
    out_specs += [_row(tm, C_W), _row(tm, D_MODEL), hbm, hbm]
    out_shape += [jax.ShapeDtypeStruct((seq, C_W), BF16), jax.ShapeDtypeStruct((seq, D_MODEL), BF16),
                  jax.ShapeDtypeStruct((N_DEV,) + w_mem.shape, BF16), jax.ShapeDtypeStruct((N_DEV,) + w_out.shape, BF16)]
    res = pl.pallas_call(
        body, name="inproj", grid=(n_steps,),
        in_specs=[_row(tm, D_MODEL), hbm, _row(tm, LANES), _row(tm, LANES), _row(tm, LANES),
                  _full(w_mem.shape), _full(w_out.shape)],
        out_specs=out_specs, out_shape=out_shape,
        scratch_shapes=[pltpu.VMEM((n_chunk, tm, LANES), F32)] + _w_in_scratch()
        + [pltpu.VMEM(w_mem.shape, BF16), pltpu.VMEM(w_out.shape, BF16), pltpu.SemaphoreType.DMA((2, N_DEV)),
           pltpu.SemaphoreType.DMA((2, N_DEV)), pltpu.SemaphoreType.DMA((2,))],
        compiler_params=_params(dimension_semantics=("arbitrary",)),
    )(u, w_in_full, *tabs, w_mem, w_out)
    qa, ka, va = res[0:3]
    qkv_b = {1: res[3:6], 4: [t.reshape(seq, B_W) for t in res[6:9]], 16: [t.reshape(seq, B_W) for t in res[9:12]]}
    return qa, ka, va, qkv_b, res[12], res[13], res[14], res[15]


def _memkv_fwd(mem, mem_g, w_mem_full):
    def body(mem_ref, g_ref, w_ref, mn_ref, mk_ref, mv_ref):
        mv_ = mem_ref[...]
        r = lax.rsqrt(jnp.mean(mv_ * mv_, axis=-1, keepdims=True) + RMS_EPS)
        mn = ((mv_ * r) * g_ref[...]).astype(BF16)
        mn_ref[...] = mn
        mkv = jnp.dot(mn, w_ref[...], preferred_element_type=F32)
        mk_ref[...] = mkv[:, 0:C_W].astype(BF16)
        mv_ref[...] = mkv[:, C_W:2 * C_W].astype(BF16)

    return pl.pallas_call(
        body, name="memkv_fwd",
        out_shape=[jax.ShapeDtypeStruct((N_MEM, D_MODEL), BF16),
                   jax.ShapeDtypeStruct((N_MEM, C_W), BF16), jax.ShapeDtypeStruct((N_MEM, C_W), BF16)],
        compiler_params=_params(),
    )(mem, mem_g, w_mem_full)


def _memkv_bwd(mem, mem_g, mn, w_mem_full, dmk, dmv):
    def body(mem_ref, g_ref, mn_ref, w_ref, dmk_ref, dmv_ref, dw_ref, st_ref):
        dmkv = jnp.concatenate([dmk_ref[...], dmv_ref[...]], axis=1).astype(BF16)
        dw_ref[...] = lax.dot_general(mn_ref[...], dmkv, (((0,), (0,)), ((), ())), preferred_element_type=F32)
        dmn = lax.dot_general(dmkv, w_ref[...], (((1,), (1,)), ((), ())), preferred_element_type=F32)
        mv_ = mem_ref[...]
        r = lax.rsqrt(jnp.mean(mv_ * mv_, axis=-1, keepdims=True) + RMS_EPS)
        st_ref[...] = jnp.zeros_like(st_ref)
        st_ref[0:1, :] = jnp.sum(dmn * (mv_ * r), axis=0, keepdims=True)

    return pl.pallas_call(
        body, name="memkv_bwd",
        out_shape=[jax.ShapeDtypeStruct((D_MODEL, 2 * C_W), F32), jax.ShapeDtypeStruct((8, D_MODEL), F32)],
        compiler_params=_params(),
    )(mem, mem_g, mn, w_mem_full, dmk, dmv)


def _band_mask(has_prev, max_dist):
    qi = lax.broadcasted_iota(jnp.int32, (BLOCK, 2 * BLOCK), 0)
    kj = lax.broadcasted_iota(jnp.int32, (BLOCK, 2 * BLOCK), 1)
    dist = qi + BLOCK - kj
    return (dist >= 0) & (dist <= max_dist) & ((kj >= BLOCK) | has_prev)


_NT = (((1,), (1,)), ((), ()))
_TN = (((0,), (0,)), ((), ()))


def _head_only(val, h):
    slab = val[:, LANES * (h // 2):LANES * (h // 2 + 1)]
    lane = lax.broadcasted_iota(jnp.int32, slab.shape, 1)
    keep = (lane < HEAD_DIM) if h % 2 == 0 else (lane >= HEAD_DIM)
    return jnp.where(keep, slab, jnp.zeros((), slab.dtype))


class _KvSlabs:
    def __init__(self, cat, group):
        self.cat, self.group, self.swapped = cat, group, {}

    def is_swapped(self, h):
        return (h // self.group) % 2 != h % 2

    def __call__(self, h):
        j = (h // self.group) // 2
        slab = self.cat[:, LANES * j:LANES * (j + 1)]
        if not self.is_swapped(h):
            return slab
        if j not in self.swapped:
            self.swapped[j] = jnp.concatenate([slab[:, HEAD_DIM:], slab[:, :HEAD_DIM]], axis=1)
        return self.swapped[j]


class _BandSteps:
    def __init__(self, seq, dil, nq):
        self.nq, self.rows, self.consecutive = nq, nq * BLOCK, dil == 1
        nb = seq // dil // BLOCK
        if self.consecutive:
            assert nb % nq == 0
            self.outer, self.inner, self.stride = 1, nb // nq, 1
        else:
            assert dil % nq == 0
            self.outer, self.inner, self.stride = dil // nq, nb, dil // nq

    def own(self, w, clamp=False):
        cur = (lambda i: jnp.minimum(i, self.inner - 1)) if clamp else (lambda i: i)
        return pl.BlockSpec((self.rows, w), lambda r, i: (cur(i) * self.stride + r, 0))

    def prev(self, w, clamp=False):
        cur = (lambda i: jnp.minimum(i, self.inner - 1)) if clamp else (lambda i: i)
        if self.consecutive:
            return pl.BlockSpec((BLOCK, w), lambda r, i: (jnp.maximum(cur(i) * self.nq - 1, 0), 0))
        return pl.BlockSpec((self.rows, w), lambda r, i: (jnp.maximum(cur(i) - 1, 0) * self.stride + r, 0))

    def late(self, w):
        return pl.BlockSpec((self.rows, w), lambda r, i: (jnp.maximum(i - 1, 0) * self.stride + r, 0))

    def rows_of(self, j):
        return slice(BLOCK * j, BLOCK * (j + 1))

    def keys(self, p_ref, c_ref, j):
        if not self.consecutive:
            before = p_ref[self.rows_of(j), :]
        elif j == 0:
            before = p_ref[...]
        else:
            before = c_ref[self.rows_of(j - 1), :]
        return jnp.concatenate([before, c_ref[self.rows_of(j), :]], axis=0)

    def has_prev(self, i, j):
        return True if (self.consecutive and j > 0) else (i > 0)


def _banded_fwd(q, k, v, sink, *, dil, heads, group, max_dist, nq, name):
    seq = q.shape[0]
    kvh = heads // group
    qw, kw = heads * HEAD_DIM, kvh * HEAD_DIM
    steps = _BandSteps(seq, dil, nq)

    def body(*refs):
        if sink is not None:
            sink_ref, refs = refs[0], refs[1:]
        q_ref, kp_ref, kc_ref, vp_ref, vc_ref, o_ref, lse_ref, s_scr, p_scr = refs
        i = pl.program_id(1)
        lane = lax.broadcasted_iota(jnp.int32, (BLOCK, LANES), 1)
        k_of = [_KvSlabs(steps.keys(kp_ref, kc_ref, j), group) for j in range(nq)]
        v_of = [_KvSlabs(steps.keys(vp_ref, vc_ref, j), group) for j in range(nq)]
        for j in range(nq):
            qv = q_ref[steps.rows_of(j), :]
            for h in range(heads):
                s_scr[j * heads + h] = lax.dot_general(_head_only(qv, h), k_of[j](h), _NT, preferred_element_type=F32)
        ls = {}
        for j in range(nq):
            valid = _band_mask(steps.has_prev(i, j), max_dist)
            lse_tile = jnp.zeros((BLOCK, LANES), F32)
            for h in range(heads):
                s = jnp.where(valid, s_scr[j * heads + h], NEG)
                m = jnp.max(s, axis=-1, keepdims=True)
                if sink is not None:
                    sk = sink_ref[h]
                    m = jnp.maximum(m, sk)
                p = jnp.exp(s - m)
                l = jnp.sum(p, axis=-1, keepdims=True)
                if sink is not None:
                    l = l + jnp.exp(sk - m)
                p_scr[j * heads + h] = p.astype(BF16)
                ls[j, h] = l
                lse_tile = jnp.where(lane == h, m + jnp.log(l), lse_tile)
            lse_ref[steps.rows_of(j), :] = lse_tile
        for j in range(nq):
            for pr in range(heads // 2):
                he, ho = 2 * pr, 2 * pr + 1
                even = jnp.dot(p_scr[j * heads + he], v_of[j](he), preferred_element_type=F32) / ls[j, he]
                odd = jnp.dot(p_scr[j * heads + ho], v_of[j](ho), preferred_element_type=F32) / ls[j, ho]
                o_ref[steps.rows_of(j), LANES * pr:LANES * (pr + 1)] = jnp.where(lane < HEAD_DIM, even, odd).astype(BF16)

    in_specs = [steps.own(qw), steps.prev(kw), steps.own(kw), steps.prev(kw), steps.own(kw)]
    args = [q, k, k, v, v]
    if sink is not None:
        in_specs = [pl.BlockSpec(memory_space=pltpu.SMEM)] + in_specs
        args = [sink] + args
    return pl.pallas_call(
        body, name=name, grid=(steps.outer, steps.inner), in_specs=in_specs,
        out_specs=[steps.own(qw), steps.own(LANES)],
        out_shape=[jax.ShapeDtypeStruct((seq, qw), BF16), jax.ShapeDtypeStruct((seq, LANES), F32)],
        scratch_shapes=[pltpu.VMEM((nq * heads, BLOCK, 2 * BLOCK), F32), pltpu.VMEM((nq * heads, BLOCK, 2 * BLOCK), BF16)],
        compiler_params=_params(dimension_semantics=("arbitrary", "arbitrary")),
    )(*args)


def _banded_bwd(q, k, v, d_out, stat, sink, *, dil, heads, group, max_dist, nq, name, reduce_scatter=()):
    seq = q.shape[0]
    kvh = heads // group
    qw, kw = heads * HEAD_DIM, kvh * HEAD_DIM
    steps = _BandSteps(seq, dil, nq)
    n_rs = len(reduce_scatter)
    n_in = 7 + n_rs
    n_flat = steps.outer * (steps.inner + 1)

    def body(*refs):
        refs = list(refs)
        sink_ref = refs.pop(0) if sink is not None else None
        (q_ref, kp_ref, kc_ref, vp_ref, vc_ref, do_ref, st_ref), partials = refs[:7], refs[7:n_in]
        refs = refs[n_in:]
        dsink_ref = refs.pop(0) if sink is not None else None
        (dq_ref, dk_ref, dv_ref), sums = refs[:3], refs[3:3 + n_rs]
        kcar, vcar, s_scr, dp_scr, p_scr, ds_scr = refs[3 + n_rs:9 + n_rs]
        r, i = pl.program_id(0), pl.program_id(1)
        if n_rs:
            exchange = _ReduceScatter(tuple(partials), tuple(sums), refs[9 + n_rs:])
            flat = r * (steps.inner + 1) + i

            @pl.when(flat == 0)
            def _():
                exchange.start()

            @pl.when(flat == min(2, n_flat - 1))
            def _():
                exchange.send_chip_sums()

        @pl.when(i == 0)
        def _():
            kcar[...] = jnp.zeros_like(kcar)
            vcar[...] = jnp.zeros_like(vcar)

        if sink is not None:
            @pl.when((i == 0) & (r == 0))
            def _():
                dsink_ref[...] = jnp.zeros_like(dsink_ref)

        @pl.when(i < steps.inner)
        def _():
            lane = lax.broadcasted_iota(jnp.int32, (1, LANES), 1)
            lane_q = lax.broadcasted_iota(jnp.int32, (BLOCK, LANES), 1)
            k_of = [_KvSlabs(steps.keys(kp_ref, kc_ref, j), group) for j in range(nq)]
            v_of = [_KvSlabs(steps.keys(vp_ref, vc_ref, j), group) for j in range(nq)]
            qms, doms = {}, {}
            for j in range(nq):
                qv, dov = q_ref[steps.rows_of(j), :], do_ref[steps.rows_of(j), :]
                for h in range(heads):
                    qms[j, h], doms[j, h] = _head_only(qv, h), _head_only(dov, h)
                    s_scr[j * heads + h] = lax.dot_general(qms[j, h], k_of[j](h), _NT, preferred_element_type=F32)
                    dp_scr[j * heads + h] = lax.dot_general(doms[j, h], v_of[j](h), _NT, preferred_element_type=F32)
            dsink_row = jnp.zeros((1, LANES), F32)
            for j in range(nq):
                st = st_ref[steps.rows_of(j), :]
                valid = _band_mask(steps.has_prev(i, j), max_dist)
                for h in range(heads):
                    lse_h = st[:, h:h + 1]
                    delta = st[:, DELTA_LANE + h:DELTA_LANE + h + 1]
                    p = jnp.where(valid, jnp.exp(s_scr[j * heads + h] - lse_h), 0.0)
                    p_scr[j * heads + h] = p.astype(BF16)
                    ds_scr[j * heads + h] = (p * (dp_scr[j * heads + h] - delta)).astype(BF16)
                    if sink is not None:
                        ds_sink = jnp.sum(-jnp.exp(sink_ref[h] - lse_h) * delta, axis=0, keepdims=True)
                        dsink_row = dsink_row + jnp.where(lane == h, ds_sink, 0.0)
            for j in range(nq):
                for pr in range(heads // 2):
                    he, ho = 2 * pr, 2 * pr + 1
                    even = jnp.dot(ds_scr[j * heads + he], k_of[j](he), preferred_element_type=F32)
                    odd = jnp.dot(ds_scr[j * heads + ho], k_of[j](ho), preferred_element_type=F32)
                    dq_ref[steps.rows_of(j), LANES * pr:LANES * (pr + 1)] = (
                        jnp.where(lane_q < HEAD_DIM, even, odd).astype(BF16))
            if steps.consecutive:
                dk_ref[...] = kcar[...].astype(BF16)
                dv_ref[...] = vcar[...].astype(BF16)
            for j in range(nq):
                for slab in range(kw // LANES):
                    acc = {}
                    for h in range(heads):
                        if (h // group) // 2 != slab:
                            continue
                        key = k_of[j].is_swapped(h)
                        dk_h = lax.dot_general(ds_scr[j * heads + h], qms[j, h], _TN, preferred_element_type=F32)
                        dv_h = lax.dot_general(p_scr[j * heads + h], doms[j, h], _TN, preferred_element_type=F32)
                        acc[key] = (dk_h, dv_h) if key not in acc else (acc[key][0] + dk_h, acc[key][1] + dv_h)
                    dk_j, dv_j = acc.get(False, (None, None))
                    if True in acc:
                        unswap = lambda t: jnp.concatenate([t[:, HEAD_DIM:], t[:, :HEAD_DIM]], axis=1)
                        dk_s, dv_s = unswap(acc[True][0]), unswap(acc[True][1])
                        dk_j = dk_s if dk_j is None else dk_j + dk_s
                        dv_j = dv_s if dv_j is None else dv_j + dv_s
                    sl = slice(LANES * slab, LANES * (slab + 1))
                    own_rows = steps.rows_of(j)
                    if not steps.consecutive:
                        dk_ref[own_rows, sl] = (kcar[own_rows, sl] + dk_j[0:BLOCK]).astype(BF16)
                        dv_ref[own_rows, sl] = (vcar[own_rows, sl] + dv_j[0:BLOCK]).astype(BF16)
                    elif j == 0:
                        last = steps.rows_of(nq - 1)
                        dk_ref[last, sl] = (kcar[last, sl] + dk_j[0:BLOCK]).astype(BF16)
                        dv_ref[last, sl] = (vcar[last, sl] + dv_j[0:BLOCK]).astype(BF16)
                    else:
                        before = steps.rows_of(j - 1)
                        kcar[before, sl] += dk_j[0:BLOCK]
                        vcar[before, sl] += dv_j[0:BLOCK]
                    kcar[own_rows, sl] = dk_j[BLOCK:2 * BLOCK]
                    vcar[own_rows, sl] = dv_j[BLOCK:2 * BLOCK]
            if sink is not None:
                dsink_ref[0:1, :] += dsink_row

        @pl.when(i == steps.inner)
        def _():
            dk_ref[...] = kcar[...].astype(BF16)
            dv_ref[...] = vcar[...].astype(BF16)

        if n_rs:
            @pl.when(flat == n_flat - 1)
            def _():
                exchange.finish()

    own, prev = (lambda w: steps.own(w, clamp=True)), (lambda w: steps.prev(w, clamp=True))
    rs_shapes = [t.shape[1:] for t in reduce_scatter]
    in_specs = ([own(qw), prev(kw), own(kw), prev(kw), own(kw), own(qw), own(LANES)]
                + [pl.BlockSpec(memory_space=pl.ANY)] * n_rs)
    args = [q, k, k, v, v, d_out, stat, *reduce_scatter]
    out_specs = [own(qw), steps.late(kw), steps.late(kw)] + [_full(s) for s in rs_shapes]
    out_shape = [jax.ShapeDtypeStruct((seq, qw), BF16), jax.ShapeDtypeStruct((seq, kw), BF16),
                 jax.ShapeDtypeStruct((seq, kw), BF16)] + [jax.ShapeDtypeStruct(s, F32) for s in rs_shapes]
    if sink is not None:
        in_specs = [pl.BlockSpec(memory_space=pltpu.SMEM)] + in_specs
        args = [sink] + args
        out_specs = [_full((8, LANES))] + out_specs
        out_shape = [jax.ShapeDtypeStruct((8, LANES), F32)] + out_shape
    n_hb = nq * heads
    res = pl.pallas_call(
        body, name=name, grid=(steps.outer, steps.inner + 1), in_specs=in_specs, out_specs=out_specs,
        out_shape=out_shape,
        scratch_shapes=[pltpu.VMEM((steps.rows, kw), F32), pltpu.VMEM((steps.rows, kw), F32)]
        + [pltpu.VMEM((n_hb, BLOCK, 2 * BLOCK), F32)] * 2 + [pltpu.VMEM((n_hb, BLOCK, 2 * BLOCK), BF16)] * 2
        + (_ReduceScatter.scratch_shapes(rs_shapes) if n_rs else []),
        compiler_params=_params(dimension_semantics=("arbitrary", "arbitrary")),
    )(*args)
    if sink is not None:
        return (*res[1:4], res[0], *res[4:])
    return res


def _cross_fwd(q, mk, mv, tq=512):
    seq = q.shape[0]

    def body(q_ref, mk_ref, mv_ref, o_ref, lse_ref, s_scr, p_scr):
        qv = q_ref[...]
        k_of, v_of = _KvSlabs(mk_ref[...], 1), _KvSlabs(mv_ref[...], 1)
        lane = lax.broadcasted_iota(jnp.int32, (tq, LANES), 1)
        lse_tile = jnp.zeros((tq, LANES), F32)
        for h in range(C_HEADS):
            s_scr[h] = lax.dot_general(_head_only(qv, h), k_of(h), _NT, preferred_element_type=F32)
        ls = []
        for h in range(C_HEADS):
            s = s_scr[h]
            m = jnp.max(s, axis=-1, keepdims=True)
            p = jnp.exp(s - m)
            l = jnp.sum(p, axis=-1, keepdims=True)
            p_scr[h] = p.astype(BF16)
            ls.append(l)
            lse_tile = jnp.where(lane == h, m + jnp.log(l), lse_tile)
        for pr in range(C_HEADS // 2):
            even = jnp.dot(p_scr[2 * pr], v_of(2 * pr), preferred_element_type=F32) / ls[2 * pr]
            odd = jnp.dot(p_scr[2 * pr + 1], v_of(2 * pr + 1), preferred_element_type=F32) / ls[2 * pr + 1]
            o_ref[:, LANES * pr:LANES * (pr + 1)] = jnp.where(lane < HEAD_DIM, even, odd).astype(BF16)
        lse_ref[...] = lse_tile

    return pl.pallas_call(
        body, name="cross_fwd", grid=(seq // tq,),
        in_specs=[_row(tq, C_W), _full((N_MEM, C_W)), _full((N_MEM, C_W))],
        out_specs=[_row(tq, C_W), _row(tq, LANES)],
        out_shape=[jax.ShapeDtypeStruct((seq, C_W), BF16), jax.ShapeDtypeStruct((seq, LANES), F32)],
        scratch_shapes=[pltpu.VMEM((C_HEADS, tq, N_MEM), F32), pltpu.VMEM((C_HEADS, tq, N_MEM), BF16)],
        compiler_params=_params(dimension_semantics=("arbitrary",)),
    )(q, mk, mv)


def _cross_bwd(q, mk, mv, d_out, stat, tq=512):
    seq = q.shape[0]

    def body(q_ref, mk_ref, mv_ref, do_ref, st_ref, dq_ref, dmk_ref, dmv_ref, s_scr, dp_scr, p_scr, ds_scr):
        @pl.when(pl.program_id(0) == 0)
        def _():
            dmk_ref[...] = jnp.zeros_like(dmk_ref)
            dmv_ref[...] = jnp.zeros_like(dmv_ref)

        qv, dov, st = q_ref[...], do_ref[...], st_ref[...]
        k_of, v_of = _KvSlabs(mk_ref[...], 1), _KvSlabs(mv_ref[...], 1)
        qms = [_head_only(qv, h) for h in range(C_HEADS)]
        doms = [_head_only(dov, h) for h in range(C_HEADS)]
        for h in range(C_HEADS):
            s_scr[h] = lax.dot_general(qms[h], k_of(h), _NT, preferred_element_type=F32)
            dp_scr[h] = lax.dot_general(doms[h], v_of(h), _NT, preferred_element_type=F32)
        for h in range(C_HEADS):
            p = jnp.exp(s_scr[h] - st[:, h:h + 1])
            p_scr[h] = p.astype(BF16)
            ds_scr[h] = (p * (dp_scr[h] - st[:, DELTA_LANE + h:DELTA_LANE + h + 1])).astype(BF16)
        lane = lax.broadcasted_iota(jnp.int32, (tq, LANES), 1)
        for pr in range(C_HEADS // 2):
            sl = slice(LANES * pr, LANES * (pr + 1))
            even = jnp.dot(ds_scr[2 * pr], k_of(2 * pr), preferred_element_type=F32)
            odd = jnp.dot(ds_scr[2 * pr + 1], k_of(2 * pr + 1), preferred_element_type=F32)
            dq_ref[:, sl] = jnp.where(lane < HEAD_DIM, even, odd).astype(BF16)
            dmk_ref[:, sl] += (lax.dot_general(ds_scr[2 * pr], qms[2 * pr], _TN, preferred_element_type=F32)
                               + lax.dot_general(ds_scr[2 * pr + 1], qms[2 * pr + 1], _TN, preferred_element_type=F32))
            dmv_ref[:, sl] += (lax.dot_general(p_scr[2 * pr], doms[2 * pr], _TN, preferred_element_type=F32)
                               + lax.dot_general(p_scr[2 * pr + 1], doms[2 * pr + 1], _TN, preferred_element_type=F32))

    return pl.pallas_call(
        body, name="cross_bwd", grid=(seq // tq,),
        in_specs=[_row(tq, C_W), _full((N_MEM, C_W)), _full((N_MEM, C_W)), _row(tq, C_W), _row(tq, LANES)],
        out_specs=[_row(tq, C_W), _full((N_MEM, C_W)), _full((N_MEM, C_W))],
        out_shape=[jax.ShapeDtypeStruct((seq, C_W), BF16), jax.ShapeDtypeStruct((N_MEM, C_W), F32),
                   jax.ShapeDtypeStruct((N_MEM, C_W), F32)],
        scratch_shapes=[pltpu.VMEM((C_HEADS, tq, N_MEM), F32)] * 2 + [pltpu.VMEM((C_HEADS, tq, N_MEM), BF16)] * 2,
        compiler_params=_params(dimension_semantics=("arbitrary",)),
    )(q, mk, mv, d_out, stat)


def _per_head(tile, width):
    rows = tile.shape[0]
    return jnp.concatenate(
        [jnp.broadcast_to(tile[:, h:h + 1], (rows, HEAD_DIM)) for h in range(width // HEAD_DIM)], axis=1)


def _with_delta(lse_tile, prod):
    rows = lse_tile.shape[0]
    lane = lax.broadcasted_iota(jnp.int32, (rows, LANES), 1)
    tile = lse_tile
    for h in range(prod.shape[1] // HEAD_DIM):
        d = jnp.sum(prod[:, HEAD_DIM * h:HEAD_DIM * (h + 1)], axis=-1, keepdims=True)
        tile = jnp.where(lane == DELTA_LANE + h, d, tile)
    return tile


def _mid(oa, lse_a, ob, lse_b, oc, lse_c, gate, x, target, w_out_full, post_g, tm=512):
    seq = x.shape[0]
    n_b = B_W // LANES

    def body(oa_ref, la_ref, b1_ref, l1_ref, b4_ref, l4_ref, b16_ref, l16_ref, oc_ref, lc_ref,
             gate_ref, x_ref, t_ref, w_ref, pg_ref,
             dh_ref, dg_ref, doa_ref, sa_ref, dob1_ref, sb1_ref, dob4_ref, sb4_ref, dob16_ref, sb16_ref,
             doc_ref, sc_ref, dw_ref, st_ref, scr_b4, scr_b16, scr_l4, scr_l16, scr_do, scr_sb):
        @pl.when(pl.program_id(0) == 0)
        def _():
            dw_ref[...] = jnp.zeros_like(dw_ref)
            st_ref[...] = jnp.zeros_like(st_ref)

        b1, l1 = b1_ref[...].astype(F32), l1_ref[...]
        b4, l4 = _load_permuted(b4_ref, scr_b4, 4), _load_permuted(l4_ref, scr_l4, 4)
        b16, l16 = _load_permuted(b16_ref, scr_b16, 16), _load_permuted(l16_ref, scr_l16, 16)
        lm = jnp.maximum(jnp.maximum(l1, l4), l16)
        e1, e4, e16 = jnp.exp(l1 - lm), jnp.exp(l4 - lm), jnp.exp(l16 - lm)
        den = e1 + e4 + e16
        lse_b_tile = lm + jnp.log(den)
        ob_v = _per_head(e1 / den, B_W) * b1 + _per_head(e4 / den, B_W) * b4 + _per_head(e16 / den, B_W) * b16
        o_all = jnp.concatenate([oa_ref[...].astype(F32), ob_v, oc_ref[...].astype(F32)], axis=1)
        g = gate_ref[...].astype(F32)
        sig = 1.0 / (1.0 + jnp.exp(-g))
        silu = g * sig
        y = (o_all * silu).astype(BF16)
        w = w_ref[...]
        z = jnp.dot(y, w, preferred_element_type=F32)
        rz = lax.rsqrt(jnp.mean(z * z, axis=-1, keepdims=True) + RMS_EPS)
        hn = z * rz
        pg = pg_ref[...]
        err = (x_ref[...] + hn * pg) - t_ref[...]
        loss = 0.5 * jnp.sum(jnp.mean(err * err, axis=-1, keepdims=True), axis=0, keepdims=True)
        dh = err * (1.0 / D_MODEL)
        dh_ref[...] = dh.astype(BF16)
        st_ref[0:1, :] += jnp.sum(dh * hn, axis=0, keepdims=True)
        st_ref[1:2, :] += jnp.broadcast_to(loss, (1, D_MODEL))
        dhn = dh * pg
        dz = (rz * (dhn - hn * jnp.mean(dhn * hn, axis=-1, keepdims=True))).astype(BF16)
        dy = lax.dot_general(dz, w, _NT, preferred_element_type=F32)
        dw_ref[...] += lax.dot_general(y, dz, _TN, preferred_element_type=F32)
        dg_ref[...] = (dy * o_all * (sig * (1.0 + g * (1.0 - sig)))).astype(BF16)
        d_o = (dy * silu).astype(BF16)
        prod = d_o.astype(F32) * o_all
        doa_ref[...] = d_o[:, 0:A_W]
        sa_ref[...] = _with_delta(la_ref[...], prod[:, 0:A_W])
        doc_ref[...] = d_o[:, A_W + B_W:D_MODEL]
        sc_ref[...] = _with_delta(lc_ref[...], prod[:, A_W + B_W:D_MODEL])
        d_ob = d_o[:, A_W:A_W + B_W]
        stat_b = _with_delta(lse_b_tile, prod[:, A_W:A_W + B_W])
        dob1_ref[...] = d_ob
        sb1_ref[...] = stat_b
        _put(scr_do, d_ob.astype(F32))
        _put(scr_sb, stat_b)
        _store_permuted(scr_do, dob4_ref, 4, BF16)
        _store_permuted(scr_sb, sb4_ref, 4, F32)
        _store_permuted(scr_do, dob16_ref, 16, BF16)
        _store_permuted(scr_sb, sb16_ref, 16, F32)

    p4 = lambda w: _perm_spec(tm, 4, w)
    p16 = lambda w: _perm_spec(tm, 16, w)
    in_specs = [_row(tm, A_W), _row(tm, LANES), _row(tm, B_W), _row(tm, LANES), p4(B_W), p4(LANES), p16(B_W), p16(LANES),
                _row(tm, C_W), _row(tm, LANES), _row(tm, D_MODEL), _row(tm, D_MODEL), _row(tm, D_MODEL),
                _full((D_MODEL, D_MODEL)), _full((1, D_MODEL))]
    sds = jax.ShapeDtypeStruct
    v4 = lambda w, dt: sds((seq // (BLOCK * 4), 4, BLOCK, w), dt)
    v16 = lambda w, dt: sds((seq // (BLOCK * 16), 16, BLOCK, w), dt)
    out_specs = [_row(tm, D_MODEL), _row(tm, D_MODEL), _row(tm, A_W), _row(tm, LANES), _row(tm, B_W), _row(tm, LANES),
                 p4(B_W), p4(LANES), p16(B_W), p16(LANES), _row(tm, C_W), _row(tm, LANES),
                 _full((D_MODEL, D_MODEL)), _full((8, D_MODEL))]
    out_shape = [sds((seq, D_MODEL), BF16), sds((seq, D_MODEL), BF16), sds((seq, A_W), BF16), sds((seq, LANES), F32),
                 sds((seq, B_W), BF16), sds((seq, LANES), F32), v4(B_W, BF16), v4(LANES, F32), v16(B_W, BF16),
                 v16(LANES, F32), sds((seq, C_W), BF16), sds((seq, LANES), F32),
                 sds((D_MODEL, D_MODEL), F32), sds((8, D_MODEL), F32)]
    res = pl.pallas_call(
        body, name="mid", grid=(seq // tm,), in_specs=in_specs, out_specs=out_specs, out_shape=out_shape,
        scratch_shapes=[pltpu.VMEM((n_b, tm, LANES), F32), pltpu.VMEM((n_b, tm, LANES), F32),
                        pltpu.VMEM((1, tm, LANES), F32), pltpu.VMEM((1, tm, LANES), F32),
                        pltpu.VMEM((n_b, tm, LANES), F32), pltpu.VMEM((1, tm, LANES), F32)],
        compiler_params=_params(dimension_semantics=("arbitrary",)),
    )(oa, lse_a, ob[1], lse_b[1], _perm_view(ob[4], 4), _perm_view(lse_b[4], 4), _perm_view(ob[16], 16),
      _perm_view(lse_b[16], 16), oc, lse_c, gate, x, target, w_out_full, post_g)
    dh, d_gate, do_a, st_a, do_b1, st_b1, do_b4, st_b4, do_b16, st_b16, do_c, st_c, d_wout, stats = res
    flat = lambda t: t.reshape(seq, t.shape[-1])
    d_b = {1: (do_b1, st_b1), 4: (flat(do_b4), flat(st_b4)), 16: (flat(do_b16), flat(st_b16))}
    return dh, d_gate, (do_a, st_a), d_b, (do_c, st_c), d_wout, stats


def _inproj_bwd(x, u, dh, pre_g, w_in_full, tabs, dqa, dka, dva, dqkv_b, dqc, dgate, tm=512):
    seq = x.shape[0]
    n_b = B_W // LANES

    def body(x_ref, u_ref, dh_ref, g_ref, w_hbm, c_ref, up_ref, dn_ref, dqa_ref, dka_ref, dva_ref,
             dq1, dk1, dv1, dq4, dk4, dv4, dq16, dk16, dv16, dqc_ref, dg_ref,
             gx_ref, dw_ref, st_ref, scr4, scr16, w_scr, w_sems, dp_ref):
        _stage_w_in(w_hbm, w_scr, w_sems)

        @pl.when(pl.program_id(0) == 0)
        def _():
            st_ref[...] = jnp.zeros_like(st_ref)
            dw_ref[...] = jnp.zeros_like(dw_ref)

        c, up, dn = c_ref[...], -up_ref[...], -dn_ref[...]
        unrot = lambda t: _rotate(t, c, up, dn)
        total = lambda r1, r4, r16: (r1[...].astype(F32) + _load_permuted(r4, scr4, 4)
                                     + _load_permuted(r16, scr16, 16))
        dp_ref[:, 0:384] = (unrot(dqa_ref[...].astype(F32)) * SCALE).astype(BF16)
        dp_ref[:, 384:512] = unrot(dka_ref[...].astype(F32)).astype(BF16)
        dp_ref[:, 512:640] = dva_ref[...]
        dp_ref[:, 640:1024] = dg_ref[:, 0:384]
        dp_ref[:, 1024:1408] = (unrot(total(dq1, dq4, dq16)) * SCALE).astype(BF16)
        dp_ref[:, 1408:1792] = unrot(total(dk1, dk4, dk16)).astype(BF16)
        dp_ref[:, 1792:2176] = total(dv1, dv4, dv16).astype(BF16)
        dp_ref[:, 2176:2560] = dg_ref[:, 384:768]
        dp_ref[:, 2560:2816] = (dqc_ref[...].astype(F32) * SCALE).astype(BF16)
        dp_ref[:, 2816:3072] = dg_ref[:, 768:1024]
        du = lax.dot_general(dp_ref[...], w_scr[...], _NT, preferred_element_type=F32)
        res = lax.dot_general(u_ref[...], dp_ref[...], _TN, preferred_element_type=F32)
        for k in range(N_DEV):
            dw_ref[k] += res[:, SHARD_IN * k:SHARD_IN * (k + 1)]
        xv = x_ref[...]
        r = lax.rsqrt(jnp.mean(xv * xv, axis=-1, keepdims=True) + RMS_EPS)
        xh = xv * r
        st_ref[0:1, :] += jnp.sum(du * xh, axis=0, keepdims=True)
        dxh = du * g_ref[...]
        gx_ref[...] = dh_ref[...].astype(F32) + r * (dxh - xh * jnp.mean(dxh * xh, axis=-1, keepdims=True))

    in_specs = ([_row(tm, D_MODEL), _row(tm, D_MODEL), _row(tm, D_MODEL), _full((1, D_MODEL)),
                 pl.BlockSpec(memory_space=pl.ANY),
                 _row(tm, LANES), _row(tm, LANES), _row(tm, LANES), _row(tm, A_W), _row(tm, A_KV_W), _row(tm, A_KV_W)]
                + [_row(tm, B_W)] * 3 + [_perm_spec(tm, 4, B_W)] * 3 + [_perm_spec(tm, 16, B_W)] * 3
                + [_row(tm, C_W), _row(tm, D_MODEL)])
    dw_spec = pl.BlockSpec((N_DEV, D_MODEL, SHARD_IN), lambda i: (0, 0, 0), pipeline_mode=pl.Buffered(1))
    return pl.pallas_call(
        body, name="inproj_bwd", grid=(seq // tm,), in_specs=in_specs,
        out_specs=[_row(tm, D_MODEL), dw_spec, _full((8, D_MODEL))],
        out_shape=[jax.ShapeDtypeStruct((seq, D_MODEL), F32), jax.ShapeDtypeStruct((N_DEV, D_MODEL, SHARD_IN), F32),
                   jax.ShapeDtypeStruct((8, D_MODEL), F32)],
        scratch_shapes=[pltpu.VMEM((n_b, tm, LANES), F32), pltpu.VMEM((n_b, tm, LANES), F32)] + _w_in_scratch()
        + [pltpu.VMEM((tm, D_IN), BF16)],
        compiler_params=_params(dimension_semantics=("arbitrary",)),
    )(x, u, dh, pre_g, w_in_full, *tabs, dqa, dka, dva, *dqkv_b[1], *[_perm_view(t, 4) for t in dqkv_b[4]],
      *[_perm_view(t, 16) for t in dqkv_b[16]], dqc, dgate)


class _ReduceScatter:
    def __init__(self, ins, outs, scratch):
        self.n = n = len(ins)
        self.ins, self.outs = ins, outs
        self.mine, self.got, self.snd, self.rcv = (scratch[n * t:n * (t + 1)] for t in range(4))
        self.load_sems, self.d2d_send, self.d2d_recv, self.ici_send, self.ici_recv = scratch[4 * n:]
        self.pos = _mesh_pos()
        self.pairs = [(a, kk) for kk in (3, 1, 2) for a in range(n)]

    @staticmethod
    def scratch_shapes(shapes):
        return ([pltpu.VMEM((4,) + s, F32) for s in shapes] + [pltpu.VMEM((4,) + s, F32) for s in shapes]
                + [pltpu.VMEM((3,) + s, BF16) for s in shapes] + [pltpu.VMEM((3,) + s, BF16) for s in shapes]
                + [pltpu.SemaphoreType.DMA((len(shapes), 4))] * 5)

    def _chip(self, kk):
        x, y, _ = self.pos
        return (1 - x if kk & 2 else x, 1 - y if kk & 1 else y)

    def _load(self, a, kk):
        block = _dev_index((*self._chip(kk), self.pos[2]))
        return pltpu.make_async_copy(self.ins[a].at[block], self.mine[a].at[kk], self.load_sems.at[a, kk])

    def _swap(self, a, kk):
        x, y, c = self.pos
        return pltpu.make_async_remote_copy(
            src_ref=self.ins[a].at[_dev_index((*self._chip(kk), 1 - c))], dst_ref=self.got[a].at[kk],
            send_sem=self.d2d_send.at[a, kk], recv_sem=self.d2d_recv.at[a, kk],
            device_id=(x, y, 1 - c), device_id_type=MESH_ID)

    def _hop(self, a, kk):
        return pltpu.make_async_remote_copy(
            src_ref=self.snd[a].at[kk - 1], dst_ref=self.rcv[a].at[kk - 1], send_sem=self.ici_send.at[a, kk],
            recv_sem=self.ici_recv.at[a, kk], device_id=(*self._chip(kk), self.pos[2]), device_id_type=MESH_ID)

    def start(self):
        for kk in (3, 1, 2, 0):
            for a in range(self.n):
                self._load(a, kk).start()
                self._swap(a, kk).start()

    def send_chip_sums(self):
        for a, kk in self.pairs:
            self._load(a, kk).wait()
            self._swap(a, kk).wait_recv()
            self.snd[a][kk - 1] = (self.mine[a][kk] + self.got[a][kk]).astype(BF16)
            self._hop(a, kk).start()

    def finish(self):
        for a in range(self.n):
            self._load(a, 0).wait()
            self._swap(a, 0).wait_recv()
            acc = self.mine[a][0] + self.got[a][0]
            for kk in (1, 2, 3):
                self._hop(a, kk).wait_recv()
                acc = acc + self.rcv[a][kk - 1].astype(F32)
            self.outs[a][...] = acc
        for kk in range(4):
            for a in range(self.n):
                self._swap(a, kk).wait_send()
        for a, kk in self.pairs:
            self._hop(a, kk).wait_send()


def _local_step(x, mem, pre_g, w_in, sink, mem_g, w_mem, w_out, post_g, target):
    u, *tabs, w_in_full = _prep(x, pre_g, w_in)
    qa, ka, va, qkv_b, qc, gate, w_mem_all, w_out_all = _inproj(u, w_in_full, tabs, w_mem, w_out)
    w_mem_full = w_mem_all.reshape(D_MODEL, 2 * C_W)
    w_out_full = w_out_all.reshape(D_MODEL, D_MODEL)
    mn, mk, mv = _memkv_fwd(mem, mem_g, w_mem_full)

    a_cfg = dict(dil=1, heads=A_HEADS, group=A_GROUP, max_dist=BLOCK - 1, nq=ATTN_BLOCKS_PER_STEP)
    b_cfgs = {dil: dict(dil=dil, heads=B_HEADS, group=1, max_dist=win // dil, nq=ATTN_BLOCKS_PER_STEP)
              for win, dil in B_CONFIGS}
    oa, lse_a = _banded_fwd(qa, ka, va, sink, name="attn_a_fwd", **a_cfg)
    ob, lse_b = {}, {}
    for dil, cfg in b_cfgs.items():
        ob[dil], lse_b[dil] = _banded_fwd(*qkv_b[dil], None, name=f"attn_b{dil}_fwd", **cfg)
    oc, lse_c = _cross_fwd(qc, mk, mv)

    dh, d_gate, d_a, d_b, d_c, d_wout, st_mid = _mid(oa, lse_a, ob, lse_b, oc, lse_c, gate, x, target, w_out_full, post_g)

    dqc, dmk, dmv = _cross_bwd(qc, mk, mv, *d_c)
    d_wmem, st_mem = _memkv_bwd(mem, mem_g, mn, w_mem_full, dmk, dmv)
    dqkv_b = {dil: _banded_bwd(*qkv_b[dil], *d_b[dil], None, name=f"attn_b{dil}_bwd", **cfg)
              for dil, cfg in b_cfgs.items()}
    dqa, dka, dva, dsink, g_wmem, g_wout = _banded_bwd(
        qa, ka, va, *d_a, sink, name="attn_a_bwd", **a_cfg,
        reduce_scatter=(d_wmem.reshape(N_DEV, SHARD_ROWS, 2 * C_W), d_wout.reshape(N_DEV, SHARD_ROWS, D_MODEL)))

    grad_x, d_win, st_pre = _inproj_bwd(x, u, dh, pre_g, w_in_full, tabs, dqa, dka, dva, dqkv_b, dqc, d_gate)

    dsink_row = jnp.pad(dsink[0:1, :], ((0, 0), (0, D_MODEL - LANES)))
    stats = jnp.concatenate([st_pre[0:1], st_mem[0:1], st_mid[0:1], dsink_row, st_mid[1:2],
                             jnp.zeros((3, D_MODEL), F32)], axis=0)
    return grad_x, d_win, g_wmem, g_wout, stats


def _prep(x, pre_g, w_in, tm=1024):
    seq = x.shape[0]
    n_steps = seq // tm
    parts = 2
    rows = D_MODEL // parts
    relay_at = min(1, n_steps - 1)
    j = jnp.arange(LANES) % HEAD_DIM
    freq = (ROPE_THETA ** (-(2 * (j % (ROT_DIM // 2))).astype(F32) / ROT_DIM))[None, :]
    SIB, NB_X, NB_Y, RELAY, FWD = 0, 1, 2, 3, 4

    def body(x_ref, g_ref, f_ref, win_ref, u_ref, c_ref, up_ref, dn_ref, win_out, win_b,
             send_sems, recv_sems, local_sems):
        step = pl.program_id(0)
        px, py, pc = _mesh_pos()
        me, sibling = (px, py, pc), (px, py, 1 - pc)
        others = lambda core: ((1 - px, py, core), (px, 1 - py, core), (1 - px, 1 - py, core))
        x_nb, y_nb, diag = others(pc)
        relay_from = [x_nb, y_nb]
        relay_to = [y_nb, x_nb]

        def src(a):
            return win_b.at[pl.ds(rows * a, rows)]

        def slot(a, p):
            return win_out.at[_dev_index(p), pl.ds(rows * a, rows)]

        def copy(a, k, block, to, own=False):
            return pltpu.make_async_remote_copy(
                src_ref=src(a) if own else slot(a, block), dst_ref=slot(a, block),
                send_sem=send_sems.at[a, k], recv_sem=recv_sems.at[a, k], device_id=to, device_id_type=MESH_ID)

        def first_sends():
            return [copy(0, NB_X, me, x_nb, own=True), copy(1, NB_Y, me, y_nb, own=True),
                    copy(1, NB_X, me, x_nb, own=True), copy(0, NB_Y, me, y_nb, own=True),
                    copy(0, SIB, me, sibling, own=True), copy(1, SIB, me, sibling, own=True)]

        def relay(a):
            return copy(a, RELAY, relay_from[a], relay_to[a])

        def to_sibling(a, which):
            return copy(a, FWD + which, others(pc)[which], sibling)

        def local(a):
            return pltpu.make_async_copy(src(a), slot(a, me), local_sems.at[a])

        @pl.when(step == 0)
        def _():
            win_b[...] = win_ref[...].astype(BF16)
            for a in range(parts):
                local(a).start()
            for cp in first_sends():
                cp.start()

        @pl.when(step == relay_at)
        def _():
            for a in range(parts):
                copy(a, NB_X + a, relay_from[a], me).wait_recv()
                relay(a).start()
                to_sibling(a, a).start()

        xv = x_ref[...]
        r = lax.rsqrt(jnp.mean(xv * xv, axis=-1, keepdims=True) + RMS_EPS)
        u_ref[...] = ((xv * r) * g_ref[...]).astype(BF16)
        pos = (lax.broadcasted_iota(jnp.int32, (tm, LANES), 0) + step * tm).astype(F32)
        head_lane = lax.broadcasted_iota(jnp.int32, (tm, LANES), 1) % HEAD_DIM
        ang = pos * f_ref[...]
        cos, sin = jnp.cos(ang), jnp.sin(ang)
        half = ROT_DIM // 2
        c_ref[...] = jnp.where(head_lane < ROT_DIM, cos, 1.0)
        up_ref[...] = jnp.where((head_lane >= half) & (head_lane < ROT_DIM), sin, 0.0)
        dn_ref[...] = jnp.where(head_lane < half, -sin, 0.0)

        @pl.when(step == n_steps - 1)
        def _():
            copy(1, NB_X, x_nb, me).wait_recv()
            to_sibling(1, 0).start()
            copy(0, NB_Y, y_nb, me).wait_recv()
            to_sibling(0, 1).start()
            for a in range(parts):
                copy(a, RELAY, diag, me).wait_recv()
                to_sibling(a, 2).start()
            for a in range(parts):
                copy(a, SIB, sibling, me).wait_recv()
                for which in range(3):
                    copy(a, FWD + which, others(1 - pc)[which], me).wait_recv()
            for cp in first_sends():
                cp.wait_send()
            for a in range(parts):
                relay(a).wait_send()
                for which in range(3):
                    to_sibling(a, which).wait_send()
                local(a).wait()

    return pl.pallas_call(
        body, name="prep", grid=(n_steps,),
        in_specs=[_row(tm, D_MODEL), _full((1, D_MODEL)), _full((1, LANES)), _full(w_in.shape)],
        out_specs=[_row(tm, D_MODEL), _row(tm, LANES), _row(tm, LANES), _row(tm, LANES),
                   pl.BlockSpec(memory_space=pl.ANY)],
        out_shape=[jax.ShapeDtypeStruct((seq, D_MODEL), BF16)] + [jax.ShapeDtypeStruct((seq, LANES), F32)] * 3
        + [jax.ShapeDtypeStruct((N_DEV,) + w_in.shape, BF16)],
        scratch_shapes=[pltpu.VMEM(w_in.shape, BF16), pltpu.SemaphoreType.DMA((parts, FWD + 3)),
                        pltpu.SemaphoreType.DMA((parts, FWD + 3)), pltpu.SemaphoreType.DMA((parts,))],
        compiler_params=_params(dimension_semantics=("arbitrary",)),
    )(x, pre_g, freq, w_in)


def _exchange_grads(d_win, stats):
    def body(win, st, g_win, r_st, send_sems, recv_sems, local_sem, *scratch):
        exchange = _ReduceScatter((win,), (g_win,), scratch)
        exchange.start()
        pos = _mesh_pos()
        me = _dev_index(pos)
        own = pltpu.make_async_copy(st, r_st.at[me], local_sem)
        own.start()
        copies = []
        for s in range(1, N_DEV):
            peer = _xor_peer(pos, s)
            mk = lambda slot: pltpu.make_async_remote_copy(
                src_ref=st, dst_ref=r_st.at[slot], send_sem=send_sems.at[s], recv_sem=recv_sems.at[s],
                device_id=peer, device_id_type=MESH_ID)
            send, arrival = mk(me), mk(_dev_index(peer))
            send.start()
            copies.append((send, arrival))
        exchange.send_chip_sums()
        exchange.finish()
        for send, arrival in copies:
            arrival.wait_recv()
            send.wait_send()
        own.wait()

    hbm = pl.BlockSpec(memory_space=pl.ANY)
    shard = d_win.shape[1:]
    return pl.pallas_call(
        body, name="exchange_grads", in_specs=[hbm, hbm],
        out_specs=[pl.BlockSpec(memory_space=pltpu.VMEM), hbm],
        out_shape=[jax.ShapeDtypeStruct(shard, F32), jax.ShapeDtypeStruct((N_DEV,) + stats.shape, F32)],
        scratch_shapes=[pltpu.SemaphoreType.DMA((N_DEV,)), pltpu.SemaphoreType.DMA((N_DEV,)), pltpu.SemaphoreType.DMA(())]
        + _ReduceScatter.scratch_shapes([shard]),
        compiler_params=_params(),
    )(d_win, stats)


WEIGHT_ORDER = ("pre_norm", "w_in", "sink_a", "mem_norm", "w_mem_kv", "w_out", "post_norm")


def _adamw_all(grads, r_stats, weights, moments_m, moments_v):
    n = len(WEIGHT_ORDER)
    stat_row = {"pre_norm": 0, "mem_norm": 1, "post_norm": 2, "sink_a": 3}

    def body(*refs):
        gw_in, gw_mem, gw_out, st_ref = refs[0:4]
        w_refs, m_refs, v_refs = (dict(zip(WEIGHT_ORDER, refs[4 + n * t:4 + n * (t + 1)])) for t in range(3))
        loss_ref = refs[4 + 3 * n]
        outs = refs[5 + 3 * n:]
        g_small = st_ref[0]
        for s in range(1, N_DEV):
            g_small = g_small + st_ref[s]
        loss_ref[...] = g_small[4:5, 0:1]
        big = {"w_in": gw_in, "w_mem_kv": gw_mem, "w_out": gw_out}
        for i, name in enumerate(WEIGHT_ORDER):
            if name in big:
                g = big[name][...]
                at = lambda ref: ref[0]
            else:
                width = w_refs[name].shape[-1]
                g = g_small[stat_row[name]:stat_row[name] + 1, 0:width]
                at = lambda ref: ref[...]
            m2 = ADAM_B1 * at(m_refs[name]) + (1.0 - ADAM_B1) * g
            v2 = ADAM_B2 * at(v_refs[name]) + (1.0 - ADAM_B2) * (g * g)
            m_hat = m2 / (1.0 - ADAM_B1 ** ADAM_STEP)
            v_hat = v2 / (1.0 - ADAM_B2 ** ADAM_STEP)
            delta = -ADAM_LR * (m_hat / (jnp.sqrt(v_hat) + ADAM_EPS) + ADAM_WD * at(w_refs[name]))
            for kind, val in enumerate((g, delta, m2, v2)):
                out = outs[kind * n + i]
                if name in big:
                    out[0] = val
                else:
                    out[...] = val

    shapes = [weights[name].shape for name in WEIGHT_ORDER]
    res = pl.pallas_call(
        body, name="adamw_all",
        out_shape=[jax.ShapeDtypeStruct((1, 1), F32)] + [jax.ShapeDtypeStruct(sh, F32) for sh in shapes] * 4,
        compiler_params=_params(),
    )(grads["w_in"], grads["w_mem_kv"], grads["w_out"], r_stats,
      *[weights[k] for k in WEIGHT_ORDER], *[moments_m[k] for k in WEIGHT_ORDER], *[moments_v[k] for k in WEIGHT_ORDER])
    return res[0].reshape(()), res[1:]


def kernel(x, mem, pre_norm, w_in, sink_a, mem_norm, w_mem_kv, w_out, post_norm, loss_target, m_pre_norm, m_w_in, m_sink_a, m_mem_norm, m_w_mem_kv, m_w_out, m_post_norm, v_pre_norm, v_w_in, v_sink_a, v_mem_norm, v_w_mem_kv, v_w_out, v_post_norm):
    sink = jnp.pad(sink_a[0], (0, 8 - A_HEADS))
    grad_x, d_win, g_wmem, g_wout, stats = _local_step(
        x[0], mem[0], pre_norm, w_in[0], sink, mem_norm, w_mem_kv[0], w_out[0], post_norm, loss_target[0])
    g_win, r_stats = _exchange_grads(d_win, stats)
    weights = dict(pre_norm=pre_norm, w_in=w_in, sink_a=sink_a, mem_norm=mem_norm, w_mem_kv=w_mem_kv, w_out=w_out,
                   post_norm=post_norm)
    moments_m = dict(pre_norm=m_pre_norm, w_in=m_w_in, sink_a=m_sink_a, mem_norm=m_mem_norm, w_mem_kv=m_w_mem_kv,
                     w_out=m_w_out, post_norm=m_post_norm)
    moments_v = dict(pre_norm=v_pre_norm, w_in=v_w_in, sink_a=v_sink_a, mem_norm=v_mem_norm, w_mem_kv=v_w_mem_kv,
                     w_out=v_w_out, post_norm=v_post_norm)
    loss, rest = _adamw_all(dict(w_in=g_win, w_mem_kv=g_wmem, w_out=g_wout), r_stats, weights, moments_m, moments_v)
    return (loss, grad_x[None], *rest)
```

```python
import jax
import jax.numpy as jnp
from jax import lax
from jax.experimental import pallas as pl
from jax.experimental.pallas import tpu as pltpu

F32 = jnp.float32
BF16 = jnp.bfloat16

D_MODEL = 1024
HEAD_DIM = 64
ROT_DIM = 16
ROPE_THETA = 500000.0
BLOCK = 128
LANES = 128
N_MEM = 256
RMS_EPS = 1e-6
SCALE = HEAD_DIM ** -0.5
A_HEADS, A_GROUP = 6, 3
B_HEADS = 6
C_HEADS = 4
A_W, A_KV_W, B_W, C_W = 384, 128, 384, 256
D_IN = 3072
N_DEV = 8
SHARD_IN = D_IN // N_DEV
SHARD_ROWS = D_MODEL // N_DEV
B_CONFIGS = ((128, 1), (512, 4), (2048, 16))
DILS = (4, 16)
NEG = -1e30
ATTN_BLOCKS_PER_STEP = 4
DELTA_LANE = 64
VMEM_LIMIT = 56 * 1024 * 1024

ADAM_LR, ADAM_B1, ADAM_B2, ADAM_EPS, ADAM_WD, ADAM_STEP = 0.001, 0.9, 0.999, 1e-08, 0.01, 10
MESH_ID = pl.DeviceIdType.MESH


def _params(**kw):
    return pltpu.CompilerParams(vmem_limit_bytes=VMEM_LIMIT, **kw)


def _full(shape):
    n = len(shape)
    return pl.BlockSpec(shape, lambda *_: (0,) * n)


def _row(tm, w):
    return pl.BlockSpec((tm, w), lambda i: (i, 0))


def _mesh_pos():
    return lax.axis_index("x"), lax.axis_index("y"), lax.axis_index("c")


def _dev_index(pos):
    return 4 * pos[0] + 2 * pos[1] + pos[2]


def _xor_peer(pos, s):
    x, y, c = pos
    return (1 - x if s & 4 else x, 1 - y if s & 2 else y, 1 - c if s & 1 else c)


def _perm_view(a, dil):
    return a.reshape(a.shape[0] // (BLOCK * dil), dil, BLOCK, a.shape[1])


def _perm_spec(tm, dil, w):
    chunk = BLOCK * dil
    if tm >= chunk:
        return pl.BlockSpec((tm // chunk, dil, BLOCK, w), lambda i: (i, 0, 0, 0))
    per = chunk // tm
    return pl.BlockSpec((1, dil, tm // dil, w), lambda i: (i // per, 0, i % per, 0))


def _put(scr, val):
    for c in range(val.shape[1] // LANES):
        scr[c] = val[:, LANES * c:LANES * (c + 1)]


def _get(scr):
    n = scr.shape[0]
    return scr[0] if n == 1 else jnp.concatenate([scr[c] for c in range(n)], axis=1)


def _get_class(scr, r, dil):
    n, rows = scr.shape[0], scr.shape[1]
    parts = [scr.at[c][pl.ds(r, rows // dil, stride=dil), :] for c in range(n)]
    return parts[0] if n == 1 else jnp.concatenate(parts, axis=1)


def _store_permuted(scr, out_ref, dil, dtype):
    for r in range(dil):
        out_ref[0, r] = _get_class(scr, r, dil).astype(dtype)


def _fill_permuted(in_ref, scr, dil):
    n, rows = scr.shape[0], scr.shape[1]
    for r in range(dil):
        val = in_ref[0, r].astype(F32)
        for c in range(n):
            scr.at[c][pl.ds(r, rows // dil, stride=dil), :] = val[:, LANES * c:LANES * (c + 1)]


def _load_permuted(in_ref, scr, dil):
    _fill_permuted(in_ref, scr, dil)
    return _get(scr)


def _rotate128(t, c, up, dn):
    return t * c + pltpu.roll(t, 8, 1) * up + pltpu.roll(t, LANES - 8, 1) * dn


def _rotate(t, c, up, dn):
    outs = [_rotate128(t[:, LANES * j:LANES * (j + 1)], c, up, dn) for j in range(t.shape[1] // LANES)]
    return outs[0] if len(outs) == 1 else jnp.concatenate(outs, axis=1)


def _w_in_scratch():
    return [pltpu.VMEM((D_MODEL, D_IN), BF16), pltpu.SemaphoreType.DMA((N_DEV,))]


def _stage_w_in(w_hbm, w_scr, sems):
    @pl.when(pl.program_id(0) == 0)
    def _():
        copies = [pltpu.make_async_copy(w_hbm.at[k], w_scr.at[:, pl.ds(SHARD_IN * k, SHARD_IN)], sems.at[k])
                  for k in range(N_DEV)]
        for cp in copies:
            cp.start()
        for cp in copies:
            cp.wait()


def _inproj(u, w_in_full, tabs, w_mem, w_out, tm=1024):
    seq = u.shape[0]
    n_chunk = D_IN // LANES
    n_steps = seq // tm

    def body(u_ref, w_hbm, c_ref, up_ref, dn_ref, wm_ref, wo_ref, qa_ref, ka_ref, va_ref,
             qb1_ref, kb1_ref, vb1_ref, qb4_ref, kb4_ref, vb4_ref, qb16_ref, kb16_ref, vb16_ref,
             qc_ref, gate_ref, wm_all, wo_all, proj, w_scr, w_sems, wm_b, wo_b, send_sems, recv_sems, local_sems):
        step = pl.program_id(0)
        shards, gathered = (wm_b, wo_b), (wm_all, wo_all)

        def gather_copies(arriving):
            pos = _mesh_pos()
            me = _dev_index(pos)
            local = [] if arriving else [
                pltpu.make_async_copy(shards[a], gathered[a].at[me], local_sems.at[a]) for a in range(2)]
            remote = []
            for s in range(1, N_DEV):
                peer = _xor_peer(pos, s)
                for a in range(2):
                    remote.append(pltpu.make_async_remote_copy(
                        src_ref=shards[a], dst_ref=gathered[a].at[_dev_index(peer) if arriving else me],
                        send_sem=send_sems.at[a, s], recv_sem=recv_sems.at[a, s], device_id=peer,
                        device_id_type=MESH_ID))
            return local, remote

        @pl.when(step == 0)
        def _():
            wm_b[...] = wm_ref[...].astype(BF16)
            wo_b[...] = wo_ref[...].astype(BF16)
            local, sends = gather_copies(arriving=False)
            for cp in local + sends:
                cp.start()

        _stage_w_in(w_hbm, w_scr, w_sems)
        u = u_ref[...]
        for n0 in range(0, D_IN, D_MODEL):
            acc = jnp.dot(u, w_scr[:, n0:n0 + D_MODEL], preferred_element_type=F32)
            for c3 in range(D_MODEL // LANES):
                proj[n0 // LANES + c3] = acc[:, LANES * c3:LANES * (c3 + 1)]
        c, up, dn = c_ref[...], up_ref[...], dn_ref[...]

        def cols(lo, hi, rot=False, scale=None):
            parts = []
            for ch in range(lo // LANES, hi // LANES):
                t = proj[ch]
                if rot:
                    t = _rotate128(t, c, up, dn)
                if scale is not None:
                    t = t * scale
                parts.append(t)
            return parts[0] if len(parts) == 1 else jnp.concatenate(parts, axis=1)

        qa_ref[...] = cols(0, 384, True, SCALE).astype(BF16)
        ka_ref[...] = cols(384, 512, True).astype(BF16)
        va_ref[...] = cols(512, 640).astype(BF16)
        gate_ref[:, 0:384] = cols(640, 1024).astype(BF16)
        gate_ref[:, 384:768] = cols(2176, 2560).astype(BF16)
        gate_ref[:, 768:1024] = cols(2816, 3072).astype(BF16)
        qc_ref[...] = cols(2560, 2816, False, SCALE).astype(BF16)
        for ch in range(1024 // LANES, 1408 // LANES):
            proj[ch] = _rotate128(proj[ch], c, up, dn) * SCALE
        for ch in range(1408 // LANES, 1792 // LANES):
            proj[ch] = _rotate128(proj[ch], c, up, dn)
        for lo, nat, p4, p16 in ((1024, qb1_ref, qb4_ref, qb16_ref), (1408, kb1_ref, kb4_ref, kb16_ref),
                                 (1792, vb1_ref, vb4_ref, vb16_ref)):
            chunks = range(lo // LANES, lo // LANES + B_W // LANES)
            nat[...] = jnp.concatenate([proj[ch] for ch in chunks], axis=1).astype(BF16)
            for dil, ref in ((4, p4), (16, p16)):
                span = min(tm, BLOCK * dil)
                for cc in range(tm // span):
                    for rr in range(dil):
                        ref[cc, rr] = jnp.concatenate(
                            [proj.at[ch][pl.ds(cc * span + rr, span // dil, stride=dil), :] for ch in chunks],
                            axis=1).astype(BF16)

        @pl.when(step == n_steps - 1)
        def _():
            for cp in gather_copies(arriving=True)[1]:
                cp.wait_recv()
            local, sends = gather_copies(arriving=False)
            for cp in sends:
                cp.wait_send()
            for cp in local:
                cp.wait()

    nat_w = (A_W, A_KV_W, A_KV_W, B_W, B_W, B_W)
    out_specs = [_row(tm, w) for w in nat_w]
    out_shape = [jax.ShapeDtypeStruct((seq, w), BF16) for w in nat_w]
    for dil in DILS:
        out_specs += [_perm_spec(tm, dil, B_W)] * 3
        out_shape += [jax.ShapeDtypeStruct((seq // (BLOCK * dil), dil, BLOCK, B_W), BF16)] * 3
    hbm = pl.BlockSpec(memory_space=pl.ANY)
    out_specs += [_row(tm, C_W), _row(tm, D_MODEL), hbm, hbm]
    out_shape += [jax.ShapeDtypeStruct((seq, C_W), BF16), jax.ShapeDtypeStruct((seq, D_MODEL), BF16),
                  jax.ShapeDtypeStruct((N_DEV,) + w_mem.shape, BF16), jax.ShapeDtypeStruct((N_DEV,) + w_out.shape, BF16)]
    res = pl.pallas_call(
        body, name="inproj", grid=(n_steps,),
        in_specs=[_row(tm, D_MODEL), hbm, _row(tm, LANES), _row(tm, LANES), _row(tm, LANES),
                  _full(w_mem.shape), _full(w_out.shape)],
        out_specs=out_specs, out_shape=out_shape,
        scratch_shapes=[pltpu.VMEM((n_chunk, tm, LANES), F32)] + _w_in_scratch()
        + [pltpu.VMEM(w_mem.shape, BF16), pltpu.VMEM(w_out.shape, BF16), pltpu.SemaphoreType.DMA((2, N_DEV)),
           pltpu.SemaphoreType.DMA((2, N_DEV)), pltpu.SemaphoreType.DMA((2,))],
        compiler_params=_params(dimension_semantics=("arbitrary",)),
    )(u, w_in_full, *tabs, w_mem, w_out)
    qa, ka, va = res[0:3]
    qkv_b = {1: res[3:6], 4: [t.reshape(seq, B_W) for t in res[6:9]], 16: [t.reshape(seq, B_W) for t in res[9:12]]}
    return qa, ka, va, qkv_b, res[12], res[13], res[14], res[15]


def _memkv_fwd(mem, mem_g, w_mem_full):
    def body(mem_ref, g_ref, w_ref, mn_ref, mk_ref, mv_ref):
        mv_ = mem_ref[...]
        r = lax.rsqrt(jnp.mean(mv_ * mv_, axis=-1, keepdims=True) + RMS_EPS)
        mn = ((mv_ * r) * g_ref[...]).astype(BF16)
        mn_ref[...] = mn
        mkv = jnp.dot(mn, w_ref[...], preferred_element_type=F32)
        mk_ref[...] = mkv[:, 0:C_W].astype(BF16)
        mv_ref[...] = mkv[:, C_W:2 * C_W].astype(BF16)

    return pl.pallas_call(
        body, name="memkv_fwd",
        out_shape=[jax.ShapeDtypeStruct((N_MEM, D_MODEL), BF16),
                   jax.ShapeDtypeStruct((N_MEM, C_W), BF16), jax.ShapeDtypeStruct((N_MEM, C_W), BF16)],
        compiler_params=_params(),
    )(mem, mem_g, w_mem_full)


def _memkv_bwd(mem, mem_g, mn, w_mem_full, dmk, dmv):
    def body(mem_ref, g_ref, mn_ref, w_ref, dmk_ref, dmv_ref, dw_ref, st_ref):
        dmkv = jnp.concatenate([dmk_ref[...], dmv_ref[...]], axis=1).astype(BF16)
        dw_ref[...] = lax.dot_general(mn_ref[...], dmkv, (((0,), (0,)), ((), ())), preferred_element_type=F32)
        dmn = lax.dot_general(dmkv, w_ref[...], (((1,), (1,)), ((), ())), preferred_element_type=F32)
        mv_ = mem_ref[...]
        r = lax.rsqrt(jnp.mean(mv_ * mv_, axis=-1, keepdims=True) + RMS_EPS)
        st_ref[...] = jnp.zeros_like(st_ref)
        st_ref[0:1, :] = jnp.sum(dmn * (mv_ * r), axis=0, keepdims=True)

    return pl.pallas_call(
        body, name="memkv_bwd",
        out_shape=[jax.ShapeDtypeStruct((D_MODEL, 2 * C_W), F32), jax.ShapeDtypeStruct((8, D_MODEL), F32)],
        compiler_params=_params(),
    )(mem, mem_g, mn, w_mem_full, dmk, dmv)


def _band_mask(has_prev, max_dist):
    qi = lax.broadcasted_iota(jnp.int32, (BLOCK, 2 * BLOCK), 0)
    kj = lax.broadcasted_iota(jnp.int32, (BLOCK, 2 * BLOCK), 1)
    dist = qi + BLOCK - kj
    return (dist >= 0) & (dist <= max_dist) & ((kj >= BLOCK) | has_prev)


_NT = (((1,), (1,)), ((), ()))
_TN = (((0,), (0,)), ((), ()))


def _head_only(val, h):
    slab = val[:, LANES * (h // 2):LANES * (h // 2 + 1)]
    lane = lax.broadcasted_iota(jnp.int32, slab.shape, 1)
    keep = (lane < HEAD_DIM) if h % 2 == 0 else (lane >= HEAD_DIM)
    return jnp.where(keep, slab, jnp.zeros((), slab.dtype))


class _KvSlabs:
    def __init__(self, cat, group):
        self.cat, self.group, self.swapped = cat, group, {}

    def is_swapped(self, h):
        return (h // self.group) % 2 != h % 2

    def __call__(self, h):
        j = (h // self.group) // 2
        slab = self.cat[:, LANES * j:LANES * (j + 1)]
        if not self.is_swapped(h):
            return slab
        if j not in self.swapped:
            self.swapped[j] = jnp.concatenate([slab[:, HEAD_DIM:], slab[:, :HEAD_DIM]], axis=1)
        return self.swapped[j]


class _BandSteps:
    def __init__(self, seq, dil, nq):
        self.nq, self.rows, self.consecutive = nq, nq * BLOCK, dil == 1
        nb = seq // dil // BLOCK
        if self.consecutive:
            assert nb % nq == 0
            self.outer, self.inner, self.stride = 1, nb // nq, 1
        else:
            assert dil % nq == 0
            self.outer, self.inner, self.stride = dil // nq, nb, dil // nq

    def own(self, w, clamp=False):
        cur = (lambda i: jnp.minimum(i, self.inner - 1)) if clamp else (lambda i: i)
        return pl.BlockSpec((self.rows, w), lambda r, i: (cur(i) * self.stride + r, 0))

    def prev(self, w, clamp=False):
        cur = (lambda i: jnp.minimum(i, self.inner - 1)) if clamp else (lambda i: i)
        if self.consecutive:
            return pl.BlockSpec((BLOCK, w), lambda r, i: (jnp.maximum(cur(i) * self.nq - 1, 0), 0))
        return pl.BlockSpec((self.rows, w), lambda r, i: (jnp.maximum(cur(i) - 1, 0) * self.stride + r, 0))

    def late(self, w):
        return pl.BlockSpec((self.rows, w), lambda r, i: (jnp.maximum(i - 1, 0) * self.stride + r, 0))

    def rows_of(self, j):
        return slice(BLOCK * j, BLOCK * (j + 1))

    def keys(self, p_ref, c_ref, j):
        if not self.consecutive:
            before = p_ref[self.rows_of(j), :]
        elif j == 0:
            before = p_ref[...]
        else:
            before = c_ref[self.rows_of(j - 1), :]
        return jnp.concatenate([before, c_ref[self.rows_of(j), :]], axis=0)

    def has_prev(self, i, j):
        return True if (self.consecutive and j > 0) else (i > 0)


def _banded_fwd(q, k, v, sink, *, dil, heads, group, max_dist, nq, name):
    seq = q.shape[0]
    kvh = heads // group
    qw, kw = heads * HEAD_DIM, kvh * HEAD_DIM
    steps = _BandSteps(seq, dil, nq)

    def body(*refs):
        if sink is not None:
            sink_ref, refs = refs[0], refs[1:]
        q_ref, kp_ref, kc_ref, vp_ref, vc_ref, o_ref, lse_ref, s_scr, p_scr = refs
        i = pl.program_id(1)
        lane = lax.broadcasted_iota(jnp.int32, (BLOCK, LANES), 1)
        k_of = [_KvSlabs(steps.keys(kp_ref, kc_ref, j), group) for j in range(nq)]
        v_of = [_KvSlabs(steps.keys(vp_ref, vc_ref, j), group) for j in range(nq)]
        for j in range(nq):
            qv = q_ref[steps.rows_of(j), :]
            for h in range(heads):
                s_scr[j * heads + h] = lax.dot_general(_head_only(qv, h), k_of[j](h), _NT, preferred_element_type=F32)
        ls = {}
        for j in range(nq):
            valid = _band_mask(steps.has_prev(i, j), max_dist)
            lse_tile = jnp.zeros((BLOCK, LANES), F32)
            for h in range(heads):
                s = jnp.where(valid, s_scr[j * heads + h], NEG)
                m = jnp.max(s, axis=-1, keepdims=True)
                if sink is not None:
                    sk = sink_ref[h]
                    m = jnp.maximum(m, sk)
                p = jnp.exp(s - m)
                l = jnp.sum(p, axis=-1, keepdims=True)
                if sink is not None:
                    l = l + jnp.exp(sk - m)
                p_scr[j * heads + h] = p.astype(BF16)
                ls[j, h] = l
                lse_tile = jnp.where(lane == h, m + jnp.log(l), lse_tile)
            lse_ref[steps.rows_of(j), :] = lse_tile
        for j in range(nq):
            for pr in range(heads // 2):
                he, ho = 2 * pr, 2 * pr + 1
                even = jnp.dot(p_scr[j * heads + he], v_of[j](he), preferred_element_type=F32) / ls[j, he]
                odd = jnp.dot(p_scr[j * heads + ho], v_of[j](ho), preferred_element_type=F32) / ls[j, ho]
                o_ref[steps.rows_of(j), LANES * pr:LANES * (pr + 1)] = jnp.where(lane < HEAD_DIM, even, odd).astype(BF16)

    in_specs = [steps.own(qw), steps.prev(kw), steps.own(kw), steps.prev(kw), steps.own(kw)]
    args = [q, k, k, v, v]
    if sink is not None:
        in_specs = [pl.BlockSpec(memory_space=pltpu.SMEM)] + in_specs
        args = [sink] + args
    return pl.pallas_call(
        body, name=name, grid=(steps.outer, steps.inner), in_specs=in_specs,
        out_specs=[steps.own(qw), steps.own(LANES)],
        out_shape=[jax.ShapeDtypeStruct((seq, qw), BF16), jax.ShapeDtypeStruct((seq, LANES), F32)],
        scratch_shapes=[pltpu.VMEM((nq * heads, BLOCK, 2 * BLOCK), F32), pltpu.VMEM((nq * heads, BLOCK, 2 * BLOCK), BF16)],
        compiler_params=_params(dimension_semantics=("arbitrary", "arbitrary")),
    )(*args)


def _banded_bwd(q, k, v, d_out, stat, sink, *, dil, heads, group, max_dist, nq, name, reduce_scatter=()):
    seq = q.shape[0]
    kvh = heads // group
    qw, kw = heads * HEAD_DIM, kvh * HEAD_DIM
    steps = _BandSteps(seq, dil, nq)
    n_rs = len(reduce_scatter)
    n_in = 7 + n_rs
    n_flat = steps.outer * (steps.inner + 1)

    def body(*refs):
        refs = list(refs)
        sink_ref = refs.pop(0) if sink is not None else None
        (q_ref, kp_ref, kc_ref, vp_ref, vc_ref, do_ref, st_ref), partials = refs[:7], refs[7:n_in]
        refs = refs[n_in:]
        dsink_ref = refs.pop(0) if sink is not None else None
        (dq_ref, dk_ref, dv_ref), sums = refs[:3], refs[3:3 + n_rs]
        kcar, vcar, s_scr, dp_scr, p_scr, ds_scr = refs[3 + n_rs:9 + n_rs]
        r, i = pl.program_id(0), pl.program_id(1)
        if n_rs:
            exchange = _ReduceScatter(tuple(partials), tuple(sums), refs[9 + n_rs:])
            flat = r * (steps.inner + 1) + i

            @pl.when(flat == 0)
            def _():
                exchange.start()

            @pl.when(flat == min(2, n_flat - 1))
            def _():
                exchange.send_chip_sums()

        @pl.when(i == 0)
        def _():
            kcar[...] = jnp.zeros_like(kcar)
            vcar[...] = jnp.zeros_like(vcar)

        if sink is not None:
            @pl.when((i == 0) & (r == 0))
            def _():
                dsink_ref[...] = jnp.zeros_like(dsink_ref)

        @pl.when(i < steps.inner)
        def _():
            lane = lax.broadcasted_iota(jnp.int32, (1, LANES), 1)
            lane_q = lax.broadcasted_iota(jnp.int32, (BLOCK, LANES), 1)
            k_of = [_KvSlabs(steps.keys(kp_ref, kc_ref, j), group) for j in range(nq)]
            v_of = [_KvSlabs(steps.keys(vp_ref, vc_ref, j), group) for j in range(nq)]
            qms, doms = {}, {}
            for j in range(nq):
                qv, dov = q_ref[steps.rows_of(j), :], do_ref[steps.rows_of(j), :]
                for h in range(heads):
                    qms[j, h], doms[j, h] = _head_only(qv, h), _head_only(dov, h)
                    s_scr[j * heads + h] = lax.dot_general(qms[j, h], k_of[j](h), _NT, preferred_element_type=F32)
                    dp_scr[j * heads + h] = lax.dot_general(doms[j, h], v_of[j](h), _NT, preferred_element_type=F32)
            dsink_row = jnp.zeros((1, LANES), F32)
            for j in range(nq):
                st = st_ref[steps.rows_of(j), :]
                valid = _band_mask(steps.has_prev(i, j), max_dist)
                for h in range(heads):
                    lse_h = st[:, h:h + 1]
                    delta = st[:, DELTA_LANE + h:DELTA_LANE + h + 1]
                    p = jnp.where(valid, jnp.exp(s_scr[j * heads + h] - lse_h), 0.0)
                    p_scr[j * heads + h] = p.astype(BF16)
                    ds_scr[j * heads + h] = (p * (dp_scr[j * heads + h] - delta)).astype(BF16)
                    if sink is not None:
                        ds_sink = jnp.sum(-jnp.exp(sink_ref[h] - lse_h) * delta, axis=0, keepdims=True)
                        dsink_row = dsink_row + jnp.where(lane == h, ds_sink, 0.0)
            for j in range(nq):
                for pr in range(heads // 2):
                    he, ho = 2 * pr, 2 * pr + 1
                    even = jnp.dot(ds_scr[j * heads + he], k_of[j](he), preferred_element_type=F32)
                    odd = jnp.dot(ds_scr[j * heads + ho], k_of[j](ho), preferred_element_type=F32)
                    dq_ref[steps.rows_of(j), LANES * pr:LANES * (pr + 1)] = (
                        jnp.where(lane_q < HEAD_DIM, even, odd).astype(BF16))
            if steps.consecutive:
                dk_ref[...] = kcar[...].astype(BF16)
                dv_ref[...] = vcar[...].astype(BF16)
            for j in range(nq):
                for slab in range(kw // LANES):
                    acc = {}
                    for h in range(heads):
                        if (h // group) // 2 != slab:
                            continue
                        key = k_of[j].is_swapped(h)
                        dk_h = lax.dot_general(ds_scr[j * heads + h], qms[j, h], _TN, preferred_element_type=F32)
                        dv_h = lax.dot_general(p_scr[j * heads + h], doms[j, h], _TN, preferred_element_type=F32)
                        acc[key] = (dk_h, dv_h) if key not in acc else (acc[key][0] + dk_h, acc[key][1] + dv_h)
                    dk_j, dv_j = acc.get(False, (None, None))
                    if True in acc:
                        unswap = lambda t: jnp.concatenate([t[:, HEAD_DIM:], t[:, :HEAD_DIM]], axis=1)
                        dk_s, dv_s = unswap(acc[True][0]), unswap(acc[True][1])
                        dk_j = dk_s if dk_j is None else dk_j + dk_s
                        dv_j = dv_s if dv_j is None else dv_j + dv_s
                    sl = slice(LANES * slab, LANES * (slab + 1))
                    own_rows = steps.rows_of(j)
                    if not steps.consecutive:
                        dk_ref[own_rows, sl] = (kcar[own_rows, sl] + dk_j[0:BLOCK]).astype(BF16)
                        dv_ref[own_rows, sl] = (vcar[own_rows, sl] + dv_j[0:BLOCK]).astype(BF16)
                    elif j == 0:
                        last = steps.rows_of(nq - 1)
                        dk_ref[last, sl] = (kcar[last, sl] + dk_j[0:BLOCK]).astype(BF16)
                        dv_ref[last, sl] = (vcar[last, sl] + dv_j[0:BLOCK]).astype(BF16)
                    else:
                        before = steps.rows_of(j - 1)
                        kcar[before, sl] += dk_j[0:BLOCK]
                        vcar[before, sl] += dv_j[0:BLOCK]
                    kcar[own_rows, sl] = dk_j[BLOCK:2 * BLOCK]
                    vcar[own_rows, sl] = dv_j[BLOCK:2 * BLOCK]
            if sink is not None:
                dsink_ref[0:1, :] += dsink_row

        @pl.when(i == steps.inner)
        def _():
            dk_ref[...] = kcar[...].astype(BF16)
            dv_ref[...] = vcar[...].astype(BF16)

        if n_rs:
            @pl.when(flat == n_flat - 1)
            def _():
                exchange.finish()

    own, prev = (lambda w: steps.own(w, clamp=True)), (lambda w: steps.prev(w, clamp=True))
    rs_shapes = [t.shape[1:] for t in reduce_scatter]
    in_specs = ([own(qw), prev(kw), own(kw), prev(kw), own(kw), own(qw), own(LANES)]
                + [pl.BlockSpec(memory_space=pl.ANY)] * n_rs)
    args = [q, k, k, v, v, d_out, stat, *reduce_scatter]
    out_specs = [own(qw), steps.late(kw), steps.late(kw)] + [_full(s) for s in rs_shapes]
    out_shape = [jax.ShapeDtypeStruct((seq, qw), BF16), jax.ShapeDtypeStruct((seq, kw), BF16),
                 jax.ShapeDtypeStruct((seq, kw), BF16)] + [jax.ShapeDtypeStruct(s, F32) for s in rs_shapes]
    if sink is not None:
        in_specs = [pl.BlockSpec(memory_space=pltpu.SMEM)] + in_specs
        args = [sink] + args
        out_specs = [_full((8, LANES))] + out_specs
        out_shape = [jax.ShapeDtypeStruct((8, LANES), F32)] + out_shape
    n_hb = nq * heads
    res = pl.pallas_call(
        body, name=name, grid=(steps.outer, steps.inner + 1), in_specs=in_specs, out_specs=out_specs,
        out_shape=out_shape,
        scratch_shapes=[pltpu.VMEM((steps.rows, kw), F32), pltpu.VMEM((steps.rows, kw), F32)]
        + [pltpu.VMEM((n_hb, BLOCK, 2 * BLOCK), F32)] * 2 + [pltpu.VMEM((n_hb, BLOCK, 2 * BLOCK), BF16)] * 2
        + (_ReduceScatter.scratch_shapes(rs_shapes) if n_rs else []),
        compiler_params=_params(dimension_semantics=("arbitrary", "arbitrary")),
    )(*args)
    if sink is not None:
        return (*res[1:4], res[0], *res[4:])
    return res


def _cross_fwd(q, mk, mv, tq=512):
    seq = q.shape[0]

    def body(q_ref, mk_ref, mv_ref, o_ref, lse_ref, s_scr, p_scr):
        qv = q_ref[...]
        k_of, v_of = _KvSlabs(mk_ref[...], 1), _KvSlabs(mv_ref[...], 1)
        lane = lax.broadcasted_iota(jnp.int32, (tq, LANES), 1)
        lse_tile = jnp.zeros((tq, LANES), F32)
        for h in range(C_HEADS):
            s_scr[h] = lax.dot_general(_head_only(qv, h), k_of(h), _NT, preferred_element_type=F32)
        ls = []
        for h in range(C_HEADS):
            s = s_scr[h]
            m = jnp.max(s, axis=-1, keepdims=True)
            p = jnp.exp(s - m)
            l = jnp.sum(p, axis=-1, keepdims=True)
            p_scr[h] = p.astype(BF16)
            ls.append(l)
            lse_tile = jnp.where(lane == h, m + jnp.log(l), lse_tile)
        for pr in range(C_HEADS // 2):
            even = jnp.dot(p_scr[2 * pr], v_of(2 * pr), preferred_element_type=F32) / ls[2 * pr]
            odd = jnp.dot(p_scr[2 * pr + 1], v_of(2 * pr + 1), preferred_element_type=F32) / ls[2 * pr + 1]
            o_ref[:, LANES * pr:LANES * (pr + 1)] = jnp.where(lane < HEAD_DIM, even, odd).astype(BF16)
        lse_ref[...] = lse_tile

    return pl.pallas_call(
        body, name="cross_fwd", grid=(seq // tq,),
        in_specs=[_row(tq, C_W), _full((N_MEM, C_W)), _full((N_MEM, C_W))],
        out_specs=[_row(tq, C_W), _row(tq, LANES)],
        out_shape=[jax.ShapeDtypeStruct((seq, C_W), BF16), jax.ShapeDtypeStruct((seq, LANES), F32)],
        scratch_shapes=[pltpu.VMEM((C_HEADS, tq, N_MEM), F32), pltpu.VMEM((C_HEADS, tq, N_MEM), BF16)],
        compiler_params=_params(dimension_semantics=("arbitrary",)),
    )(q, mk, mv)


def _cross_bwd(q, mk, mv, d_out, stat, tq=512):
    seq = q.shape[0]

    def body(q_ref, mk_ref, mv_ref, do_ref, st_ref, dq_ref, dmk_ref, dmv_ref, s_scr, dp_scr, p_scr, ds_scr):
        @pl.when(pl.program_id(0) == 0)
        def _():
            dmk_ref[...] = jnp.zeros_like(dmk_ref)
            dmv_ref[...] = jnp.zeros_like(dmv_ref)

        qv, dov, st = q_ref[...], do_ref[...], st_ref[...]
        k_of, v_of = _KvSlabs(mk_ref[...], 1), _KvSlabs(mv_ref[...], 1)
        qms = [_head_only(qv, h) for h in range(C_HEADS)]
        doms = [_head_only(dov, h) for h in range(C_HEADS)]
        for h in range(C_HEADS):
            s_scr[h] = lax.dot_general(qms[h], k_of(h), _NT, preferred_element_type=F32)
            dp_scr[h] = lax.dot_general(doms[h], v_of(h), _NT, preferred_element_type=F32)
        for h in range(C_HEADS):
            p = jnp.exp(s_scr[h] - st[:, h:h + 1])
            p_scr[h] = p.astype(BF16)
            ds_scr[h] = (p * (dp_scr[h] - st[:, DELTA_LANE + h:DELTA_LANE + h + 1])).astype(BF16)
        lane = lax.broadcasted_iota(jnp.int32, (tq, LANES), 1)
        for pr in range(C_HEADS // 2):
            sl = slice(LANES * pr, LANES * (pr + 1))
            even = jnp.dot(ds_scr[2 * pr], k_of(2 * pr), preferred_element_type=F32)
            odd = jnp.dot(ds_scr[2 * pr + 1], k_of(2 * pr + 1), preferred_element_type=F32)
            dq_ref[:, sl] = jnp.where(lane < HEAD_DIM, even, odd).astype(BF16)
            dmk_ref[:, sl] += (lax.dot_general(ds_scr[2 * pr], qms[2 * pr], _TN, preferred_element_type=F32)
                               + lax.dot_general(ds_scr[2 * pr + 1], qms[2 * pr + 1], _TN, preferred_element_type=F32))
            dmv_ref[:, sl] += (lax.dot_general(p_scr[2 * pr], doms[2 * pr], _TN, preferred_element_type=F32)
                               + lax.dot_general(p_scr[2 * pr + 1], doms[2 * pr + 1], _TN, preferred_element_type=F32))

    return pl.pallas_call(
        body, name="cross_bwd", grid=(seq // tq,),
        in_specs=[_row(tq, C_W), _full((N_MEM, C_W)), _full((N_MEM, C_W)), _row(tq, C_W), _row(tq, LANES)],
        out_specs=[_row(tq, C_W), _full((N_MEM, C_W)), _full((N_MEM, C_W))],
        out_shape=[jax.ShapeDtypeStruct((seq, C_W), BF16), jax.ShapeDtypeStruct((N_MEM, C_W), F32),
                   jax.ShapeDtypeStruct((N_MEM, C_W), F32)],
        scratch_shapes=[pltpu.VMEM((C_HEADS, tq, N_MEM), F32)] * 2 + [pltpu.VMEM((C_HEADS, tq, N_MEM), BF16)] * 2,
        compiler_params=_params(dimension_semantics=("arbitrary",)),
    )(q, mk, mv, d_out, stat)


def _per_head(tile, width):
    rows = tile.shape[0]
    return jnp.concatenate(
        [jnp.broadcast_to(tile[:, h:h + 1], (rows, HEAD_DIM)) for h in range(width // HEAD_DIM)], axis=1)


def _with_delta(lse_tile, prod):
    rows = lse_tile.shape[0]
    lane = lax.broadcasted_iota(jnp.int32, (rows, LANES), 1)
    tile = lse_tile
    for h in range(prod.shape[1] // HEAD_DIM):
        d = jnp.sum(prod[:, HEAD_DIM * h:HEAD_DIM * (h + 1)], axis=-1, keepdims=True)
        tile = jnp.where(lane == DELTA_LANE + h, d, tile)
    return tile


def _mid(oa, lse_a, ob, lse_b, oc, lse_c, gate, x, target, w_out_full, post_g, tm=512):
    seq = x.shape[0]
    n_b = B_W // LANES

    def body(oa_ref, la_ref, b1_ref, l1_ref, b4_ref, l4_ref, b16_ref, l16_ref, oc_ref, lc_ref,
             gate_ref, x_ref, t_ref, w_ref, pg_ref,
             dh_ref, dg_ref, doa_ref, sa_ref, dob1_ref, sb1_ref, dob4_ref, sb4_ref, dob16_ref, sb16_ref,
             doc_ref, sc_ref, dw_ref, st_ref, scr_b4, scr_b16, scr_l4, scr_l16, scr_do, scr_sb):
        @pl.when(pl.program_id(0) == 0)
        def _():
            dw_ref[...] = jnp.zeros_like(dw_ref)
            st_ref[...] = jnp.zeros_like(st_ref)

        b1, l1 = b1_ref[...].astype(F32), l1_ref[...]
        b4, l4 = _load_permuted(b4_ref, scr_b4, 4), _load_permuted(l4_ref, scr_l4, 4)
        b16, l16 = _load_permuted(b16_ref, scr_b16, 16), _load_permuted(l16_ref, scr_l16, 16)
        lm = jnp.maximum(jnp.maximum(l1, l4), l16)
        e1, e4, e16 = jnp.exp(l1 - lm), jnp.exp(l4 - lm), jnp.exp(l16 - lm)
        den = e1 + e4 + e16
        lse_b_tile = lm + jnp.log(den)
        ob_v = _per_head(e1 / den, B_W) * b1 + _per_head(e4 / den, B_W) * b4 + _per_head(e16 / den, B_W) * b16
        o_all = jnp.concatenate([oa_ref[...].astype(F32), ob_v, oc_ref[...].astype(F32)], axis=1)
        g = gate_ref[...].astype(F32)
        sig = 1.0 / (1.0 + jnp.exp(-g))
        silu = g * sig
        y = (o_all * silu).astype(BF16)
        w = w_ref[...]
        z = jnp.dot(y, w, preferred_element_type=F32)
        rz = lax.rsqrt(jnp.mean(z * z, axis=-1, keepdims=True) + RMS_EPS)
        hn = z * rz
        pg = pg_ref[...]
        err = (x_ref[...] + hn * pg) - t_ref[...]
        loss = 0.5 * jnp.sum(jnp.mean(err * err, axis=-1, keepdims=True), axis=0, keepdims=True)
        dh = err * (1.0 / D_MODEL)
        dh_ref[...] = dh.astype(BF16)
        st_ref[0:1, :] += jnp.sum(dh * hn, axis=0, keepdims=True)
        st_ref[1:2, :] += jnp.broadcast_to(loss, (1, D_MODEL))
        dhn = dh * pg
        dz = (rz * (dhn - hn * jnp.mean(dhn * hn, axis=-1, keepdims=True))).astype(BF16)
        dy = lax.dot_general(dz, w, _NT, preferred_element_type=F32)
        dw_ref[...] += lax.dot_general(y, dz, _TN, preferred_element_type=F32)
        dg_ref[...] = (dy * o_all * (sig * (1.0 + g * (1.0 - sig)))).astype(BF16)
        d_o = (dy * silu).astype(BF16)
        prod = d_o.astype(F32) * o_all
        doa_ref[...] = d_o[:, 0:A_W]
        sa_ref[...] = _with_delta(la_ref[...], prod[:, 0:A_W])
        doc_ref[...] = d_o[:, A_W + B_W:D_MODEL]
        sc_ref[...] = _with_delta(lc_ref[...], prod[:, A_W + B_W:D_MODEL])
        d_ob = d_o[:, A_W:A_W + B_W]
        stat_b = _with_delta(lse_b_tile, prod[:, A_W:A_W + B_W])
        dob1_ref[...] = d_ob
        sb1_ref[...] = stat_b
        _put(scr_do, d_ob.astype(F32))
        _put(scr_sb, stat_b)
        _store_permuted(scr_do, dob4_ref, 4, BF16)
        _store_permuted(scr_sb, sb4_ref, 4, F32)
        _store_permuted(scr_do, dob16_ref, 16, BF16)
        _store_permuted(scr_sb, sb16_ref, 16, F32)

    p4 = lambda w: _perm_spec(tm, 4, w)
    p16 = lambda w: _perm_spec(tm, 16, w)
    in_specs = [_row(tm, A_W), _row(tm, LANES), _row(tm, B_W), _row(tm, LANES), p4(B_W), p4(LANES), p16(B_W), p16(LANES),
                _row(tm, C_W), _row(tm, LANES), _row(tm, D_MODEL), _row(tm, D_MODEL), _row(tm, D_MODEL),
                _full((D_MODEL, D_MODEL)), _full((1, D_MODEL))]
    sds = jax.ShapeDtypeStruct
    v4 = lambda w, dt: sds((seq // (BLOCK * 4), 4, BLOCK, w), dt)
    v16 = lambda w, dt: sds((seq // (BLOCK * 16), 16, BLOCK, w), dt)
    out_specs = [_row(tm, D_MODEL), _row(tm, D_MODEL), _row(tm, A_W), _row(tm, LANES), _row(tm, B_W), _row(tm, LANES),
                 p4(B_W), p4(LANES), p16(B_W), p16(LANES), _row(tm, C_W), _row(tm, LANES),
                 _full((D_MODEL, D_MODEL)), _full((8, D_MODEL))]
    out_shape = [sds((seq, D_MODEL), BF16), sds((seq, D_MODEL), BF16), sds((seq, A_W), BF16), sds((seq, LANES), F32),
                 sds((seq, B_W), BF16), sds((seq, LANES), F32), v4(B_W, BF16), v4(LANES, F32), v16(B_W, BF16),
                 v16(LANES, F32), sds((seq, C_W), BF16), sds((seq, LANES), F32),
                 sds((D_MODEL, D_MODEL), F32), sds((8, D_MODEL), F32)]
    res = pl.pallas_call(
        body, name="mid", grid=(seq // tm,), in_specs=in_specs, out_specs=out_specs, out_shape=out_shape,
        scratch_shapes=[pltpu.VMEM((n_b, tm, LANES), F32), pltpu.VMEM((n_b, tm, LANES), F32),
                        pltpu.VMEM((1, tm, LANES), F32), pltpu.VMEM((1, tm, LANES), F32),
                        pltpu.VMEM((n_b, tm, LANES), F32), pltpu.VMEM((1, tm, LANES), F32)],
        compiler_params=_params(dimension_semantics=("arbitrary",)),
    )(oa, lse_a, ob[1], lse_b[1], _perm_view(ob[4], 4), _perm_view(lse_b[4], 4), _perm_view(ob[16], 16),
      _perm_view(lse_b[16], 16), oc, lse_c, gate, x, target, w_out_full, post_g)
    dh, d_gate, do_a, st_a, do_b1, st_b1, do_b4, st_b4, do_b16, st_b16, do_c, st_c, d_wout, stats = res
    flat = lambda t: t.reshape(seq, t.shape[-1])
    d_b = {1: (do_b1, st_b1), 4: (flat(do_b4), flat(st_b4)), 16: (flat(do_b16), flat(st_b16))}
    return dh, d_gate, (do_a, st_a), d_b, (do_c, st_c), d_wout, stats


def _inproj_bwd(x, u, dh, pre_g, w_in_full, tabs, dqa, dka, dva, dqkv_b, dqc, dgate, tm=512):
    seq = x.shape[0]
    n_b = B_W // LANES

    def body(x_ref, u_ref, dh_ref, g_ref, w_hbm, c_ref, up_ref, dn_ref, dqa_ref, dka_ref, dva_ref,
             dq1, dk1, dv1, dq4, dk4, dv4, dq16, dk16, dv16, dqc_ref, dg_ref,
             gx_ref, dw_ref, st_ref, scr4, scr16, w_scr, w_sems, dp_ref):
        _stage_w_in(w_hbm, w_scr, w_sems)

        @pl.when(pl.program_id(0) == 0)
        def _():
            st_ref[...] = jnp.zeros_like(st_ref)
            dw_ref[...] = jnp.zeros_like(dw_ref)

        c, up, dn = c_ref[...], -up_ref[...], -dn_ref[...]
        unrot = lambda t: _rotate(t, c, up, dn)
        total = lambda r1, r4, r16: (r1[...].astype(F32) + _load_permuted(r4, scr4, 4)
                                     + _load_permuted(r16, scr16, 16))
        dp_ref[:, 0:384] = (unrot(dqa_ref[...].astype(F32)) * SCALE).astype(BF16)
        dp_ref[:, 384:512] = unrot(dka_ref[...].astype(F32)).astype(BF16)
        dp_ref[:, 512:640] = dva_ref[...]
        dp_ref[:, 640:1024] = dg_ref[:, 0:384]
        dp_ref[:, 1024:1408] = (unrot(total(dq1, dq4, dq16)) * SCALE).astype(BF16)
        dp_ref[:, 1408:1792] = unrot(total(dk1, dk4, dk16)).astype(BF16)
        dp_ref[:, 1792:2176] = total(dv1, dv4, dv16).astype(BF16)
        dp_ref[:, 2176:2560] = dg_ref[:, 384:768]
        dp_ref[:, 2560:2816] = (dqc_ref[...].astype(F32) * SCALE).astype(BF16)
        dp_ref[:, 2816:3072] = dg_ref[:, 768:1024]
        du = lax.dot_general(dp_ref[...], w_scr[...], _NT, preferred_element_type=F32)
        res = lax.dot_general(u_ref[...], dp_ref[...], _TN, preferred_element_type=F32)
        for k in range(N_DEV):
            dw_ref[k] += res[:, SHARD_IN * k:SHARD_IN * (k + 1)]
        xv = x_ref[...]
        r = lax.rsqrt(jnp.mean(xv * xv, axis=-1, keepdims=True) + RMS_EPS)
        xh = xv * r
        st_ref[0:1, :] += jnp.sum(du * xh, axis=0, keepdims=True)
        dxh = du * g_ref[...]
        gx_ref[...] = dh_ref[...].astype(F32) + r * (dxh - xh * jnp.mean(dxh * xh, axis=-1, keepdims=True))

    in_specs = ([_row(tm, D_MODEL), _row(tm, D_MODEL), _row(tm, D_MODEL), _full((1, D_MODEL)),
                 pl.BlockSpec(memory_space=pl.ANY),
                 _row(tm, LANES), _row(tm, LANES), _row(tm, LANES), _row(tm, A_W), _row(tm, A_KV_W), _row(tm, A_KV_W)]
                + [_row(tm, B_W)] * 3 + [_perm_spec(tm, 4, B_W)] * 3 + [_perm_spec(tm, 16, B_W)] * 3
                + [_row(tm, C_W), _row(tm, D_MODEL)])
    dw_spec = pl.BlockSpec((N_DEV, D_MODEL, SHARD_IN), lambda i: (0, 0, 0), pipeline_mode=pl.Buffered(1))
    return pl.pallas_call(
        body, name="inproj_bwd", grid=(seq // tm,), in_specs=in_specs,
        out_specs=[_row(tm, D_MODEL), dw_spec, _full((8, D_MODEL))],
        out_shape=[jax.ShapeDtypeStruct((seq, D_MODEL), F32), jax.ShapeDtypeStruct((N_DEV, D_MODEL, SHARD_IN), F32),
                   jax.ShapeDtypeStruct((8, D_MODEL), F32)],
        scratch_shapes=[pltpu.VMEM((n_b, tm, LANES), F32), pltpu.VMEM((n_b, tm, LANES), F32)] + _w_in_scratch()
        + [pltpu.VMEM((tm, D_IN), BF16)],
        compiler_params=_params(dimension_semantics=("arbitrary",)),
    )(x, u, dh, pre_g, w_in_full, *tabs, dqa, dka, dva, *dqkv_b[1], *[_perm_view(t, 4) for t in dqkv_b[4]],
      *[_perm_view(t, 16) for t in dqkv_b[16]], dqc, dgate)


class _ReduceScatter:
    def __init__(self, ins, outs, scratch):
        self.n = n = len(ins)
        self.ins, self.outs = ins, outs
        self.mine, self.got, self.snd, self.rcv = (scratch[n * t:n * (t + 1)] for t in range(4))
        self.load_sems, self.d2d_send, self.d2d_recv, self.ici_send, self.ici_recv = scratch[4 * n:]
        self.pos = _mesh_pos()
        self.pairs = [(a, kk) for kk in (3, 1, 2) for a in range(n)]

    @staticmethod
    def scratch_shapes(shapes):
        return ([pltpu.VMEM((4,) + s, F32) for s in shapes] + [pltpu.VMEM((4,) + s, F32) for s in shapes]
                + [pltpu.VMEM((3,) + s, BF16) for s in shapes] + [pltpu.VMEM((3,) + s, BF16) for s in shapes]
                + [pltpu.SemaphoreType.DMA((len(shapes), 4))] * 5)

    def _chip(self, kk):
        x, y, _ = self.pos
        return (1 - x if kk & 2 else x, 1 - y if kk & 1 else y)

    def _load(self, a, kk):
        block = _dev_index((*self._chip(kk), self.pos[2]))
        return pltpu.make_async_copy(self.ins[a].at[block], self.mine[a].at[kk], self.load_sems.at[a, kk])

    def _swap(self, a, kk):
        x, y, c = self.pos
        return pltpu.make_async_remote_copy(
            src_ref=self.ins[a].at[_dev_index((*self._chip(kk), 1 - c))], dst_ref=self.got[a].at[kk],
            send_sem=self.d2d_send.at[a, kk], recv_sem=self.d2d_recv.at[a, kk],
            device_id=(x, y, 1 - c), device_id_type=MESH_ID)

    def _hop(self, a, kk):
        return pltpu.make_async_remote_copy(
            src_ref=self.snd[a].at[kk - 1], dst_ref=self.rcv[a].at[kk - 1], send_sem=self.ici_send.at[a, kk],
            recv_sem=self.ici_recv.at[a, kk], device_id=(*self._chip(kk), self.pos[2]), device_id_type=MESH_ID)

    def start(self):
        for kk in (3, 1, 2, 0):
            for a in range(self.n):
                self._load(a, kk).start()
                self._swap(a, kk).start()

    def send_chip_sums(self):
        for a, kk in self.pairs:
            self._load(a, kk).wait()
            self._swap(a, kk).wait_recv()
            self.snd[a][kk - 1] = (self.mine[a][kk] + self.got[a][kk]).astype(BF16)
            self._hop(a, kk).start()

    def finish(self):
        for a in range(self.n):
            self._load(a, 0).wait()
            self._swap(a, 0).wait_recv()
            acc = self.mine[a][0] + self.got[a][0]
            for kk in (1, 2, 3):
                self._hop(a, kk).wait_recv()
                acc = acc + self.rcv[a][kk - 1].astype(F32)
            self.outs[a][...] = acc
        for kk in range(4):
            for a in range(self.n):
                self._swap(a, kk).wait_send()
        for a, kk in self.pairs:
            self._hop(a, kk).wait_send()


def _local_step(x, mem, pre_g, w_in, sink, mem_g, w_mem, w_out, post_g, target):
    u, *tabs, w_in_full = _prep(x, pre_g, w_in)
    qa, ka, va, qkv_b, qc, gate, w_mem_all, w_out_all = _inproj(u, w_in_full, tabs, w_mem, w_out)
    w_mem_full = w_mem_all.reshape(D_MODEL, 2 * C_W)
    w_out_full = w_out_all.reshape(D_MODEL, D_MODEL)
    mn, mk, mv = _memkv_fwd(mem, mem_g, w_mem_full)

    a_cfg = dict(dil=1, heads=A_HEADS, group=A_GROUP, max_dist=BLOCK - 1, nq=ATTN_BLOCKS_PER_STEP)
    b_cfgs = {dil: dict(dil=dil, heads=B_HEADS, group=1, max_dist=win // dil, nq=ATTN_BLOCKS_PER_STEP)
              for win, dil in B_CONFIGS}
    oa, lse_a = _banded_fwd(qa, ka, va, sink, name="attn_a_fwd", **a_cfg)
    ob, lse_b = {}, {}
    for dil, cfg in b_cfgs.items():
        ob[dil], lse_b[dil] = _banded_fwd(*qkv_b[dil], None, name=f"attn_b{dil}_fwd", **cfg)
    oc, lse_c = _cross_fwd(qc, mk, mv)

    dh, d_gate, d_a, d_b, d_c, d_wout, st_mid = _mid(oa, lse_a, ob, lse_b, oc, lse_c, gate, x, target, w_out_full, post_g)

    dqc, dmk, dmv = _cross_bwd(qc, mk, mv, *d_c)
    d_wmem, st_mem = _memkv_bwd(mem, mem_g, mn, w_mem_full, dmk, dmv)
    dqkv_b = {dil: _banded_bwd(*qkv_b[dil], *d_b[dil], None, name=f"attn_b{dil}_bwd", **cfg)
              for dil, cfg in b_cfgs.items()}
    dqa, dka, dva, dsink, g_wmem, g_wout = _banded_bwd(
        qa, ka, va, *d_a, sink, name="attn_a_bwd", **a_cfg,
        reduce_scatter=(d_wmem.reshape(N_DEV, SHARD_ROWS, 2 * C_W), d_wout.reshape(N_DEV, SHARD_ROWS, D_MODEL)))

    grad_x, d_win, st_pre = _inproj_bwd(x, u, dh, pre_g, w_in_full, tabs, dqa, dka, dva, dqkv_b, dqc, d_gate)

    dsink_row = jnp.pad(dsink[0:1, :], ((0, 0), (0, D_MODEL - LANES)))
    stats = jnp.concatenate([st_pre[0:1], st_mem[0:1], st_mid[0:1], dsink_row, st_mid[1:2],
                             jnp.zeros((3, D_MODEL), F32)], axis=0)
    return grad_x, d_win, g_wmem, g_wout, stats


def _prep(x, pre_g, w_in, tm=1024):
    seq = x.shape[0]
    n_steps = seq // tm
    parts = 2
    rows = D_MODEL // parts
    relay_at = min(3, n_steps - 1)
    j = jnp.arange(LANES) % HEAD_DIM
    freq = (ROPE_THETA ** (-(2 * (j % (ROT_DIM // 2))).astype(F32) / ROT_DIM))[None, :]
    SIB, NB_X, NB_Y, RELAY, FWD = 0, 1, 2, 3, 4

    def body(x_ref, g_ref, f_ref, win_ref, u_ref, c_ref, up_ref, dn_ref, win_out, win_b,
             send_sems, recv_sems, local_sems):
        step = pl.program_id(0)
        px, py, pc = _mesh_pos()
        me, sibling = (px, py, pc), (px, py, 1 - pc)
        others = lambda core: ((1 - px, py, core), (px, 1 - py, core), (1 - px, 1 - py, core))
        x_nb, y_nb, diag = others(pc)
        relay_from = [x_nb, y_nb]
        relay_to = [y_nb, x_nb]

        def src(a):
            return win_b.at[pl.ds(rows * a, rows)]

        def slot(a, p):
            return win_out.at[_dev_index(p), pl.ds(rows * a, rows)]

        def copy(a, k, block, to, own=False):
            return pltpu.make_async_remote_copy(
                src_ref=src(a) if own else slot(a, block), dst_ref=slot(a, block),
                send_sem=send_sems.at[a, k], recv_sem=recv_sems.at[a, k], device_id=to, device_id_type=MESH_ID)

        def first_sends():
            return [copy(0, NB_X, me, x_nb, own=True), copy(1, NB_Y, me, y_nb, own=True),
                    copy(1, NB_X, me, x_nb, own=True), copy(0, NB_Y, me, y_nb, own=True),
                    copy(0, SIB, me, sibling, own=True), copy(1, SIB, me, sibling, own=True)]

        def relay(a):
            return copy(a, RELAY, relay_from[a], relay_to[a])

        def to_sibling(a, which):
            return copy(a, FWD + which, others(pc)[which], sibling)

        def local(a):
            return pltpu.make_async_copy(src(a), slot(a, me), local_sems.at[a])

        @pl.when(step == 0)
        def _():
            win_b[...] = win_ref[...].astype(BF16)
            for a in range(parts):
                local(a).start()
            for cp in first_sends():
                cp.start()

        @pl.when(step == relay_at)
        def _():
            for a in range(parts):
                copy(a, NB_X + a, relay_from[a], me).wait_recv()
                relay(a).start()
                to_sibling(a, a).start()

        xv = x_ref[...]
        r = lax.rsqrt(jnp.mean(xv * xv, axis=-1, keepdims=True) + RMS_EPS)
        u_ref[...] = ((xv * r) * g_ref[...]).astype(BF16)
        pos = (lax.broadcasted_iota(jnp.int32, (tm, LANES), 0) + step * tm).astype(F32)
        head_lane = lax.broadcasted_iota(jnp.int32, (tm, LANES), 1) % HEAD_DIM
        ang = pos * f_ref[...]
        cos, sin = jnp.cos(ang), jnp.sin(ang)
        half = ROT_DIM // 2
        c_ref[...] = jnp.where(head_lane < ROT_DIM, cos, 1.0)
        up_ref[...] = jnp.where((head_lane >= half) & (head_lane < ROT_DIM), sin, 0.0)
        dn_ref[...] = jnp.where(head_lane < half, -sin, 0.0)

        @pl.when(step == n_steps - 1)
        def _():
            copy(1, NB_X, x_nb, me).wait_recv()
            to_sibling(1, 0).start()
            copy(0, NB_Y, y_nb, me).wait_recv()
            to_sibling(0, 1).start()
            for a in range(parts):
                copy(a, RELAY, diag, me).wait_recv()
                to_sibling(a, 2).start()
            for a in range(parts):
                copy(a, SIB, sibling, me).wait_recv()
                for which in range(3):
                    copy(a, FWD + which, others(1 - pc)[which], me).wait_recv()
            for cp in first_sends():
                cp.wait_send()
            for a in range(parts):
                relay(a).wait_send()
                for which in range(3):
                    to_sibling(a, which).wait_send()
                local(a).wait()

    return pl.pallas_call(
        body, name="prep", grid=(n_steps,),
        in_specs=[_row(tm, D_MODEL), _full((1, D_MODEL)), _full((1, LANES)), _full(w_in.shape)],
        out_specs=[_row(tm, D_MODEL), _row(tm, LANES), _row(tm, LANES), _row(tm, LANES),
                   pl.BlockSpec(memory_space=pl.ANY)],
        out_shape=[jax.ShapeDtypeStruct((seq, D_MODEL), BF16)] + [jax.ShapeDtypeStruct((seq, LANES), F32)] * 3
        + [jax.ShapeDtypeStruct((N_DEV,) + w_in.shape, BF16)],
        scratch_shapes=[pltpu.VMEM(w_in.shape, BF16), pltpu.SemaphoreType.DMA((parts, FWD + 3)),
                        pltpu.SemaphoreType.DMA((parts, FWD + 3)), pltpu.SemaphoreType.DMA((parts,))],
        compiler_params=_params(dimension_semantics=("arbitrary",)),
    )(x, pre_g, freq, w_in)


def _exchange_grads(d_win, stats):
    def body(win, st, g_win, r_st, send_sems, recv_sems, local_sem, *scratch):
        exchange = _ReduceScatter((win,), (g_win,), scratch)
        exchange.start()
        pos = _mesh_pos()
        me = _dev_index(pos)
        own = pltpu.make_async_copy(st, r_st.at[me], local_sem)
        own.start()
        copies = []
        for s in range(1, N_DEV):
            peer = _xor_peer(pos, s)
            mk = lambda slot: pltpu.make_async_remote_copy(
                src_ref=st, dst_ref=r_st.at[slot], send_sem=send_sems.at[s], recv_sem=recv_sems.at[s],
                device_id=peer, device_id_type=MESH_ID)
            send, arrival = mk(me), mk(_dev_index(peer))
            send.start()
            copies.append((send, arrival))
        exchange.send_chip_sums()
        exchange.finish()
        for send, arrival in copies:
            arrival.wait_recv()
            send.wait_send()
        own.wait()

    hbm = pl.BlockSpec(memory_space=pl.ANY)
    shard = d_win.shape[1:]
    return pl.pallas_call(
        body, name="exchange_grads", in_specs=[hbm, hbm],
        out_specs=[pl.BlockSpec(memory_space=pltpu.VMEM), hbm],
        out_shape=[jax.ShapeDtypeStruct(shard, F32), jax.ShapeDtypeStruct((N_DEV,) + stats.shape, F32)],
        scratch_shapes=[pltpu.SemaphoreType.DMA((N_DEV,)), pltpu.SemaphoreType.DMA((N_DEV,)), pltpu.SemaphoreType.DMA(())]
        + _ReduceScatter.scratch_shapes([shard]),
        compiler_params=_params(),
    )(d_win, stats)


WEIGHT_ORDER = ("pre_norm", "w_in", "sink_a", "mem_norm", "w_mem_kv", "w_out", "post_norm")


def _adamw_all(grads, r_stats, weights, moments_m, moments_v):
    n = len(WEIGHT_ORDER)
    stat_row = {"pre_norm": 0, "mem_norm": 1, "post_norm": 2, "sink_a": 3}

    def body(*refs):
        gw_in, gw_mem, gw_out, st_ref = refs[0:4]
        w_refs, m_refs, v_refs = (dict(zip(WEIGHT_ORDER, refs[4 + n * t:4 + n * (t + 1)])) for t in range(3))
        loss_ref = refs[4 + 3 * n]
        outs = refs[5 + 3 * n:]
        g_small = st_ref[0]
        for s in range(1, N_DEV):
            g_small = g_small + st_ref[s]
        loss_ref[...] = g_small[4:5, 0:1]
        big = {"w_in": gw_in, "w_mem_kv": gw_mem, "w_out": gw_out}
        for i, name in enumerate(WEIGHT_ORDER):
            if name in big:
                g = big[name][...]
                at = lambda ref: ref[0]
            else:
                width = w_refs[name].shape[-1]
                g = g_small[stat_row[name]:stat_row[name] + 1, 0:width]
                at = lambda ref: ref[...]
            m2 = ADAM_B1 * at(m_refs[name]) + (1.0 - ADAM_B1) * g
            v2 = ADAM_B2 * at(v_refs[name]) + (1.0 - ADAM_B2) * (g * g)
            m_hat = m2 / (1.0 - ADAM_B1 ** ADAM_STEP)
            v_hat = v2 / (1.0 - ADAM_B2 ** ADAM_STEP)
            delta = -ADAM_LR * (m_hat / (jnp.sqrt(v_hat) + ADAM_EPS) + ADAM_WD * at(w_refs[name]))
            for kind, val in enumerate((g, delta, m2, v2)):
                out = outs[kind * n + i]
                if name in big:
                    out[0] = val
                else:
                    out[...] = val

    shapes = [weights[name].shape for name in WEIGHT_ORDER]
    res = pl.pallas_call(
        body, name="adamw_all",
        out_shape=[jax.ShapeDtypeStruct((1, 1), F32)] + [jax.ShapeDtypeStruct(sh, F32) for sh in shapes] * 4,
        compiler_params=_params(),
    )(grads["w_in"], grads["w_mem_kv"], grads["w_out"], r_stats,
      *[weights[k] for k in WEIGHT_ORDER], *[moments_m[k] for k in WEIGHT_ORDER], *[moments_v[k] for k in WEIGHT_ORDER])
    return res[0].reshape(()), res[1:]


def kernel(x, mem, pre_norm, w_in, sink_a, mem_norm, w_mem_kv, w_out, post_norm, loss_target, m_pre_norm, m_w_in, m_sink_a, m_mem_norm, m_w_mem_kv, m_w_out, m_post_norm, v_pre_norm, v_w_in, v_sink_a, v_mem_norm, v_w_mem_kv, v_w_out, v_post_norm):
    sink = jnp.pad(sink_a[0], (0, 8 - A_HEADS))
    grad_x, d_win, g_wmem, g_wout, stats = _local_step(
        x[0], mem[0], pre_norm, w_in[0], sink, mem_norm, w_mem_kv[0], w_out[0], post_norm, loss_target[0])
    g_win, r_stats = _exchange_grads(d_win, stats)
    weights = dict(pre_norm=pre_norm, w_in=w_in, sink_a=sink_a, mem_norm=mem_norm, w_mem_kv=w_mem_kv, w_out=w_out,
                   post_norm=post_norm)
    moments_m = dict(pre_norm=m_pre_norm, w_in=m_w_in, sink_a=m_sink_a, mem_norm=m_mem_norm, w_mem_kv=m_w_mem_kv,
                     w_out=m_w_out, post_norm=m_post_norm)
    moments_v = dict(pre_norm=v_pre_norm, w_in=v_w_in, sink_a=v_sink_a, mem_norm=v_mem_norm, w_mem_kv=v_w_mem_kv,
                     w_out=v_w_out, post_norm=v_post_norm)
    loss, rest = _adamw_all(dict(w_in=g_win, w_mem_kv=g_wmem, w_out=g_wout), r_stats, weights, moments_m, moments_v)
    return (loss, grad_x[None], *rest)
```

```python
import jax
import jax.numpy as jnp
from jax import lax
from jax.experimental import pallas as pl
from jax.experimental.pallas import tpu as pltpu

F32 = jnp.float32
BF16 = jnp.bfloat16

D_MODEL = 1024
HEAD_DIM = 64
ROT_DIM = 16
ROPE_THETA = 500000.0
BLOCK = 128
LANES = 128
N_MEM = 256
RMS_EPS = 1e-6
SCALE = HEAD_DIM ** -0.5
A_HEADS, A_GROUP = 6, 3
B_HEADS = 6
C_HEADS = 4
A_W, A_KV_W, B_W, C_W = 384, 128, 384, 256
_IN_PIECES = (("qa", A_W), ("ka", A_KV_W), ("va", A_KV_W), ("ga", A_W), ("qb", B_W), ("kb", B_W), ("vb", B_W),
              ("gb", B_W), ("qc", C_W), ("gc", C_W))
COLS, D_IN = {}, 0
for _name, _width in _IN_PIECES:
    COLS[_name] = (D_IN, D_IN + _width)
    D_IN += _width
N_DEV = 8
SHARD_IN = D_IN // N_DEV
SHARD_ROWS = D_MODEL // N_DEV
B_CONFIGS = ((128, 1), (512, 4), (2048, 16))
DILS = (4, 16)
NEG = -1e30
ATTN_BLOCKS_PER_STEP = 4
DELTA_LANE = 64
VMEM_LIMIT = 56 * 1024 * 1024

ADAM_LR, ADAM_B1, ADAM_B2, ADAM_EPS, ADAM_WD, ADAM_STEP = 0.001, 0.9, 0.999, 1e-08, 0.01, 10
MESH_ID = pl.DeviceIdType.MESH


def _params(**kw):
    return pltpu.CompilerParams(vmem_limit_bytes=VMEM_LIMIT, **kw)


def _full(shape):
    n = len(shape)
    return pl.BlockSpec(shape, lambda *_: (0,) * n)


def _row(tm, w):
    return pl.BlockSpec((tm, w), lambda i: (i, 0))


def _mesh_pos():
    return lax.axis_index("x"), lax.axis_index("y"), lax.axis_index("c")


def _dev_index(pos):
    return 4 * pos[0] + 2 * pos[1] + pos[2]


def _xor_peer(pos, s):
    x, y, c = pos
    return (1 - x if s & 4 else x, 1 - y if s & 2 else y, 1 - c if s & 1 else c)


def _perm_view(a, dil):
    return a.reshape(a.shape[0] // (BLOCK * dil), dil, BLOCK, a.shape[1])


def _perm_spec(tm, dil, w):
    chunk = BLOCK * dil
    if tm >= chunk:
        return pl.BlockSpec((tm // chunk, dil, BLOCK, w), lambda i: (i, 0, 0, 0))
    per = chunk // tm
    return pl.BlockSpec((1, dil, tm // dil, w), lambda i: (i // per, 0, i % per, 0))


def _put(scr, val):
    for c in range(val.shape[1] // LANES):
        scr[c] = val[:, LANES * c:LANES * (c + 1)]


def _get(scr):
    n = scr.shape[0]
    return scr[0] if n == 1 else jnp.concatenate([scr[c] for c in range(n)], axis=1)


def _get_class(scr, r, dil):
    n, rows = scr.shape[0], scr.shape[1]
    parts = [scr.at[c][pl.ds(r, rows // dil, stride=dil), :] for c in range(n)]
    return parts[0] if n == 1 else jnp.concatenate(parts, axis=1)


def _store_permuted(scr, out_ref, dil, dtype):
    for r in range(dil):
        out_ref[0, r] = _get_class(scr, r, dil).astype(dtype)


def _load_permuted(in_ref, scr, dil):
    n, rows = scr.shape[0], scr.shape[1]
    for r in range(dil):
        val = in_ref[0, r].astype(F32)
        for c in range(n):
            scr.at[c][pl.ds(r, rows // dil, stride=dil), :] = val[:, LANES * c:LANES * (c + 1)]
    return _get(scr)


def _rotate128(t, c, up, dn):
    half = ROT_DIM // 2
    return t * c + pltpu.roll(t, half, 1) * up + pltpu.roll(t, LANES - half, 1) * dn


def _rotate(t, c, up, dn):
    outs = [_rotate128(t[:, LANES * j:LANES * (j + 1)], c, up, dn) for j in range(t.shape[1] // LANES)]
    return outs[0] if len(outs) == 1 else jnp.concatenate(outs, axis=1)


def _w_in_scratch():
    return [pltpu.VMEM((D_MODEL, D_IN), BF16), pltpu.SemaphoreType.DMA((N_DEV,))]


def _stage_w_in(w_hbm, w_scr, sems):
    @pl.when(pl.program_id(0) == 0)
    def _():
        copies = [pltpu.make_async_copy(w_hbm.at[k], w_scr.at[:, pl.ds(SHARD_IN * k, SHARD_IN)], sems.at[k])
                  for k in range(N_DEV)]
        for cp in copies:
            cp.start()
        for cp in copies:
            cp.wait()


def _inproj(u, w_in_full, tabs, w_mem, w_out, tm=1024):
    seq = u.shape[0]
    n_chunk = D_IN // LANES
    n_steps = seq // tm

    def body(u_ref, w_hbm, c_ref, up_ref, dn_ref, wm_ref, wo_ref, qa_ref, ka_ref, va_ref,
             qb1_ref, kb1_ref, vb1_ref, qb4_ref, kb4_ref, vb4_ref, qb16_ref, kb16_ref, vb16_ref,
             qc_ref, gate_ref, wm_all, wo_all, proj, w_scr, w_sems, wm_b, wo_b, send_sems, recv_sems, local_sems):
        step = pl.program_id(0)
        shards, gathered = (wm_b, wo_b), (wm_all, wo_all)

        def gather_copies(arriving):
            pos = _mesh_pos()
            me = _dev_index(pos)
            local = [] if arriving else [
                pltpu.make_async_copy(shards[a], gathered[a].at[me], local_sems.at[a]) for a in range(2)]
            remote = []
            for s in range(1, N_DEV):
                peer = _xor_peer(pos, s)
                for a in range(2):
                    remote.append(pltpu.make_async_remote_copy(
                        src_ref=shards[a], dst_ref=gathered[a].at[_dev_index(peer) if arriving else me],
                        send_sem=send_sems.at[a, s], recv_sem=recv_sems.at[a, s], device_id=peer,
                        device_id_type=MESH_ID))
            return local, remote

        @pl.when(step == 0)
        def _():
            wm_b[...] = wm_ref[...].astype(BF16)
            wo_b[...] = wo_ref[...].astype(BF16)
            local, sends = gather_copies(arriving=False)
            for cp in local + sends:
                cp.start()

        _stage_w_in(w_hbm, w_scr, w_sems)
        u = u_ref[...]
        for n0 in range(0, D_IN, D_MODEL):
            acc = jnp.dot(u, w_scr[:, n0:n0 + D_MODEL], preferred_element_type=F32)
            for c3 in range(D_MODEL // LANES):
                proj[n0 // LANES + c3] = acc[:, LANES * c3:LANES * (c3 + 1)]
        c, up, dn = c_ref[...], up_ref[...], dn_ref[...]

        def chunks_of(piece):
            lo, hi = COLS[piece]
            return range(lo // LANES, hi // LANES)

        def cols(piece, rot=False, scale=None):
            parts = []
            for ch in chunks_of(piece):
                t = proj[ch]
                if rot:
                    t = _rotate128(t, c, up, dn)
                if scale is not None:
                    t = t * scale
                parts.append(t)
            return parts[0] if len(parts) == 1 else jnp.concatenate(parts, axis=1)

        qa_ref[...] = cols("qa", True, SCALE).astype(BF16)
        ka_ref[...] = cols("ka", True).astype(BF16)
        va_ref[...] = cols("va").astype(BF16)
        gate_ref[:, 0:A_W] = cols("ga").astype(BF16)
        gate_ref[:, A_W:A_W + B_W] = cols("gb").astype(BF16)
        gate_ref[:, A_W + B_W:D_MODEL] = cols("gc").astype(BF16)
        qc_ref[...] = cols("qc", False, SCALE).astype(BF16)
        for ch in chunks_of("qb"):
            proj[ch] = _rotate128(proj[ch], c, up, dn) * SCALE
        for ch in chunks_of("kb"):
            proj[ch] = _rotate128(proj[ch], c, up, dn)
        for piece, nat, p4, p16 in (("qb", qb1_ref, qb4_ref, qb16_ref), ("kb", kb1_ref, kb4_ref, kb16_ref),
                                    ("vb", vb1_ref, vb4_ref, vb16_ref)):
            chunks = chunks_of(piece)
            nat[...] = jnp.concatenate([proj[ch] for ch in chunks], axis=1).astype(BF16)
            for dil, ref in ((4, p4), (16, p16)):
                span = min(tm, BLOCK * dil)
                for cc in range(tm // span):
                    for rr in range(dil):
                        ref[cc, rr] = jnp.concatenate(
                            [proj.at[ch][pl.ds(cc * span + rr, span // dil, stride=dil), :] for ch in chunks],
                            axis=1).astype(BF16)

        @pl.when(step == n_steps - 1)
        def _():
            for cp in gather_copies(arriving=True)[1]:
                cp.wait_recv()
            local, sends = gather_copies(arriving=False)
            for cp in sends:
                cp.wait_send()
            for cp in local:
                cp.wait()

    nat_w = (A_W, A_KV_W, A_KV_W, B_W, B_W, B_W)
    out_specs = [_row(tm, w) for w in nat_w]
    out_shape = [jax.ShapeDtypeStruct((seq, w), BF16) for w in nat_w]
    for dil in DILS:
        out_specs += [_perm_spec(tm, dil, B_W)] * 3
        out_shape += [jax.ShapeDtypeStruct((seq // (BLOCK * dil), dil, BLOCK, B_W), BF16)] * 3
    hbm = pl.BlockSpec(memory_space=pl.ANY)
    out_specs += [_row(tm, C_W), _row(tm, D_MODEL), hbm, hbm]
    out_shape += [jax.ShapeDtypeStruct((seq, C_W), BF16), jax.ShapeDtypeStruct((seq, D_MODEL), BF16),
                  jax.ShapeDtypeStruct((N_DEV,) + w_mem.shape, BF16), jax.ShapeDtypeStruct((N_DEV,) + w_out.shape, BF16)]
    res = pl.pallas_call(
        body, name="inproj", grid=(n_steps,),
        in_specs=[_row(tm, D_MODEL), hbm, _row(tm, LANES), _row(tm, LANES), _row(tm, LANES),
                  _full(w_mem.shape), _full(w_out.shape)],
        out_specs=out_specs, out_shape=out_shape,
        scratch_shapes=[pltpu.VMEM((n_chunk, tm, LANES), F32)] + _w_in_scratch()
        + [pltpu.VMEM(w_mem.shape, BF16), pltpu.VMEM(w_out.shape, BF16), pltpu.SemaphoreType.DMA((2, N_DEV)),
           pltpu.SemaphoreType.DMA((2, N_DEV)), pltpu.SemaphoreType.DMA((2,))],
        compiler_params=_params(dimension_semantics=("arbitrary",)),
    )(u, w_in_full, *tabs, w_mem, w_out)
    qa, ka, va = res[0:3]
    qkv_b = {1: res[3:6], 4: [t.reshape(seq, B_W) for t in res[6:9]], 16: [t.reshape(seq, B_W) for t in res[9:12]]}
    return qa, ka, va, qkv_b, res[12], res[13], res[14], res[15]


def _memkv_fwd(mem, mem_g, w_mem_full):
    def body(mem_ref, g_ref, w_ref, mn_ref, mk_ref, mv_ref):
        mv_ = mem_ref[...]
        r = lax.rsqrt(jnp.mean(mv_ * mv_, axis=-1, keepdims=True) + RMS_EPS)
        mn = ((mv_ * r) * g_ref[...]).astype(BF16)
        mn_ref[...] = mn
        mkv = jnp.dot(mn, w_ref[...], preferred_element_type=F32)
        mk_ref[...] = mkv[:, 0:C_W].astype(BF16)
        mv_ref[...] = mkv[:, C_W:2 * C_W].astype(BF16)

    return pl.pallas_call(
        body, name="memkv_fwd",
        out_shape=[jax.ShapeDtypeStruct((N_MEM, D_MODEL), BF16),
                   jax.ShapeDtypeStruct((N_MEM, C_W), BF16), jax.ShapeDtypeStruct((N_MEM, C_W), BF16)],
        compiler_params=_params(),
    )(mem, mem_g, w_mem_full)


def _memkv_bwd(mem, mem_g, mn, w_mem_full, dmk, dmv):
    def body(mem_ref, g_ref, mn_ref, w_ref, dmk_ref, dmv_ref, dw_ref, st_ref):
        dmkv = jnp.concatenate([dmk_ref[...], dmv_ref[...]], axis=1).astype(BF16)
        dw_ref[...] = lax.dot_general(mn_ref[...], dmkv, (((0,), (0,)), ((), ())), preferred_element_type=F32)
        dmn = lax.dot_general(dmkv, w_ref[...], (((1,), (1,)), ((), ())), preferred_element_type=F32)
        mv_ = mem_ref[...]
        r = lax.rsqrt(jnp.mean(mv_ * mv_, axis=-1, keepdims=True) + RMS_EPS)
        st_ref[...] = jnp.zeros_like(st_ref)
        st_ref[0:1, :] = jnp.sum(dmn * (mv_ * r), axis=0, keepdims=True)

    return pl.pallas_call(
        body, name="memkv_bwd",
        out_shape=[jax.ShapeDtypeStruct((D_MODEL, 2 * C_W), F32), jax.ShapeDtypeStruct((8, D_MODEL), F32)],
        compiler_params=_params(),
    )(mem, mem_g, mn, w_mem_full, dmk, dmv)


def _band_mask(has_prev, max_dist):
    qi = lax.broadcasted_iota(jnp.int32, (BLOCK, 2 * BLOCK), 0)
    kj = lax.broadcasted_iota(jnp.int32, (BLOCK, 2 * BLOCK), 1)
    dist = qi + BLOCK - kj
    return (dist >= 0) & (dist <= max_dist) & ((kj >= BLOCK) | has_prev)


_NT = (((1,), (1,)), ((), ()))
_TN = (((0,), (0,)), ((), ()))


def _head_only(val, h):
    slab = val[:, LANES * (h // 2):LANES * (h // 2 + 1)]
    lane = lax.broadcasted_iota(jnp.int32, slab.shape, 1)
    keep = (lane < HEAD_DIM) if h % 2 == 0 else (lane >= HEAD_DIM)
    return jnp.where(keep, slab, jnp.zeros((), slab.dtype))


class _KvSlabs:
    def __init__(self, cat, group):
        self.cat, self.group, self.swapped = cat, group, {}

    def is_swapped(self, h):
        return (h // self.group) % 2 != h % 2

    def __call__(self, h):
        j = (h // self.group) // 2
        slab = self.cat[:, LANES * j:LANES * (j + 1)]
        if not self.is_swapped(h):
            return slab
        if j not in self.swapped:
            self.swapped[j] = jnp.concatenate([slab[:, HEAD_DIM:], slab[:, :HEAD_DIM]], axis=1)
        return self.swapped[j]


class _BandSteps:
    def __init__(self, seq, dil, nq):
        self.nq, self.rows, self.consecutive = nq, nq * BLOCK, dil == 1
        nb = seq // dil // BLOCK
        if self.consecutive:
            assert nb % nq == 0
            self.outer, self.inner, self.stride = 1, nb // nq, 1
        else:
            assert dil % nq == 0
            self.outer, self.inner, self.stride = dil // nq, nb, dil // nq

    def own(self, w, clamp=False):
        cur = (lambda i: jnp.minimum(i, self.inner - 1)) if clamp else (lambda i: i)
        return pl.BlockSpec((self.rows, w), lambda r, i: (cur(i) * self.stride + r, 0))

    def prev(self, w, clamp=False):
        cur = (lambda i: jnp.minimum(i, self.inner - 1)) if clamp else (lambda i: i)
        if self.consecutive:
            return pl.BlockSpec((BLOCK, w), lambda r, i: (jnp.maximum(cur(i) * self.nq - 1, 0), 0))
        return pl.BlockSpec((self.rows, w), lambda r, i: (jnp.maximum(cur(i) - 1, 0) * self.stride + r, 0))

    def late(self, w):
        return pl.BlockSpec((self.rows, w), lambda r, i: (jnp.maximum(i - 1, 0) * self.stride + r, 0))

    def rows_of(self, j):
        return slice(BLOCK * j, BLOCK * (j + 1))

    def keys(self, p_ref, c_ref, j):
        if not self.consecutive:
            before = p_ref[self.rows_of(j), :]
        elif j == 0:
            before = p_ref[...]
        else:
            before = c_ref[self.rows_of(j - 1), :]
        return jnp.concatenate([before, c_ref[self.rows_of(j), :]], axis=0)

    def has_prev(self, i, j):
        return True if (self.consecutive and j > 0) else (i > 0)


def _banded_fwd(q, k, v, sink, *, dil, heads, group, max_dist, nq, name):
    seq = q.shape[0]
    kvh = heads // group
    qw, kw = heads * HEAD_DIM, kvh * HEAD_DIM
    steps = _BandSteps(seq, dil, nq)

    def body(*refs):
        if sink is not None:
            sink_ref, refs = refs[0], refs[1:]
        q_ref, kp_ref, kc_ref, vp_ref, vc_ref, o_ref, lse_ref, s_scr, p_scr = refs
        i = pl.program_id(1)
        lane = lax.broadcasted_iota(jnp.int32, (BLOCK, LANES), 1)
        k_of = [_KvSlabs(steps.keys(kp_ref, kc_ref, j), group) for j in range(nq)]
        v_of = [_KvSlabs(steps.keys(vp_ref, vc_ref, j), group) for j in range(nq)]
        for j in range(nq):
            qv = q_ref[steps.rows_of(j), :]
            for h in range(heads):
                s_scr[j * heads + h] = lax.dot_general(_head_only(qv, h), k_of[j](h), _NT, preferred_element_type=F32)
        ls = {}
        for j in range(nq):
            valid = _band_mask(steps.has_prev(i, j), max_dist)
            lse_tile = jnp.zeros((BLOCK, LANES), F32)
            for h in range(heads):
                s = jnp.where(valid, s_scr[j * heads + h], NEG)
                m = jnp.max(s, axis=-1, keepdims=True)
                if sink is not None:
                    sk = sink_ref[h]
                    m = jnp.maximum(m, sk)
                p = jnp.exp(s - m)
                l = jnp.sum(p, axis=-1, keepdims=True)
                if sink is not None:
                    l = l + jnp.exp(sk - m)
                p_scr[j * heads + h] = p.astype(BF16)
                ls[j, h] = l
                lse_tile = jnp.where(lane == h, m + jnp.log(l), lse_tile)
            lse_ref[steps.rows_of(j), :] = lse_tile
        for j in range(nq):
            for pr in range(heads // 2):
                he, ho = 2 * pr, 2 * pr + 1
                even = jnp.dot(p_scr[j * heads + he], v_of[j](he), preferred_element_type=F32) / ls[j, he]
                odd = jnp.dot(p_scr[j * heads + ho], v_of[j](ho), preferred_element_type=F32) / ls[j, ho]
                o_ref[steps.rows_of(j), LANES * pr:LANES * (pr + 1)] = jnp.where(lane < HEAD_DIM, even, odd).astype(BF16)

    in_specs = [steps.own(qw), steps.prev(kw), steps.own(kw), steps.prev(kw), steps.own(kw)]
    args = [q, k, k, v, v]
    if sink is not None:
        in_specs = [pl.BlockSpec(memory_space=pltpu.SMEM)] + in_specs
        args = [sink] + args
    return pl.pallas_call(
        body, name=name, grid=(steps.outer, steps.inner), in_specs=in_specs,
        out_specs=[steps.own(qw), steps.own(LANES)],
        out_shape=[jax.ShapeDtypeStruct((seq, qw), BF16), jax.ShapeDtypeStruct((seq, LANES), F32)],
        scratch_shapes=[pltpu.VMEM((nq * heads, BLOCK, 2 * BLOCK), F32), pltpu.VMEM((nq * heads, BLOCK, 2 * BLOCK), BF16)],
        compiler_params=_params(dimension_semantics=("arbitrary", "arbitrary")),
    )(*args)


def _banded_bwd(q, k, v, d_out, stat, sink, *, dil, heads, group, max_dist, nq, name, reduce_scatter=()):
    seq = q.shape[0]
    kvh = heads // group
    qw, kw = heads * HEAD_DIM, kvh * HEAD_DIM
    steps = _BandSteps(seq, dil, nq)
    n_rs = len(reduce_scatter)
    n_in = 7 + n_rs
    n_flat = steps.outer * (steps.inner + 1)

    def body(*refs):
        refs = list(refs)
        sink_ref = refs.pop(0) if sink is not None else None
        (q_ref, kp_ref, kc_ref, vp_ref, vc_ref, do_ref, st_ref), partials = refs[:7], refs[7:n_in]
        refs = refs[n_in:]
        dsink_ref = refs.pop(0) if sink is not None else None
        (dq_ref, dk_ref, dv_ref), sums = refs[:3], refs[3:3 + n_rs]
        kcar, vcar, s_scr, dp_scr, p_scr, ds_scr = refs[3 + n_rs:9 + n_rs]
        r, i = pl.program_id(0), pl.program_id(1)
        if n_rs:
            exchange = _ReduceScatter(tuple(partials), tuple(sums), refs[9 + n_rs:])
            flat = r * (steps.inner + 1) + i

            @pl.when(flat == 0)
            def _():
                exchange.start()

            @pl.when(flat == min(2, n_flat - 1))
            def _():
                exchange.send_chip_sums()

        @pl.when(i == 0)
        def _():
            kcar[...] = jnp.zeros_like(kcar)
            vcar[...] = jnp.zeros_like(vcar)

        if sink is not None:
            @pl.when((i == 0) & (r == 0))
            def _():
                dsink_ref[...] = jnp.zeros_like(dsink_ref)

        @pl.when(i < steps.inner)
        def _():
            lane = lax.broadcasted_iota(jnp.int32, (1, LANES), 1)
            lane_q = lax.broadcasted_iota(jnp.int32, (BLOCK, LANES), 1)
            k_of = [_KvSlabs(steps.keys(kp_ref, kc_ref, j), group) for j in range(nq)]
            v_of = [_KvSlabs(steps.keys(vp_ref, vc_ref, j), group) for j in range(nq)]
            qms, doms = {}, {}
            for j in range(nq):
                qv, dov = q_ref[steps.rows_of(j), :], do_ref[steps.rows_of(j), :]
                for h in range(heads):
                    qms[j, h], doms[j, h] = _head_only(qv, h), _head_only(dov, h)
                    s_scr[j * heads + h] = lax.dot_general(qms[j, h], k_of[j](h), _NT, preferred_element_type=F32)
                    dp_scr[j * heads + h] = lax.dot_general(doms[j, h], v_of[j](h), _NT, preferred_element_type=F32)
            dsink_row = jnp.zeros((1, LANES), F32)
            for j in range(nq):
                st = st_ref[steps.rows_of(j), :]
                valid = _band_mask(steps.has_prev(i, j), max_dist)
                for h in range(heads):
                    lse_h = st[:, h:h + 1]
                    delta = st[:, DELTA_LANE + h:DELTA_LANE + h + 1]
                    p = jnp.where(valid, jnp.exp(s_scr[j * heads + h] - lse_h), 0.0)
                    p_scr[j * heads + h] = p.astype(BF16)
                    ds_scr[j * heads + h] = (p * (dp_scr[j * heads + h] - delta)).astype(BF16)
                    if sink is not None:
                        ds_sink = jnp.sum(-jnp.exp(sink_ref[h] - lse_h) * delta, axis=0, keepdims=True)
                        dsink_row = dsink_row + jnp.where(lane == h, ds_sink, 0.0)
            for j in range(nq):
                for pr in range(heads // 2):
                    he, ho = 2 * pr, 2 * pr + 1
                    even = jnp.dot(ds_scr[j * heads + he], k_of[j](he), preferred_element_type=F32)
                    odd = jnp.dot(ds_scr[j * heads + ho], k_of[j](ho), preferred_element_type=F32)
                    dq_ref[steps.rows_of(j), LANES * pr:LANES * (pr + 1)] = (
                        jnp.where(lane_q < HEAD_DIM, even, odd).astype(BF16))
            if steps.consecutive:
                dk_ref[...] = kcar[...].astype(BF16)
                dv_ref[...] = vcar[...].astype(BF16)
            for j in range(nq):
                for slab in range(kw // LANES):
                    acc = {}
                    for h in range(heads):
                        if (h // group) // 2 != slab:
                            continue
                        key = k_of[j].is_swapped(h)
                        dk_h = lax.dot_general(ds_scr[j * heads + h], qms[j, h], _TN, preferred_element_type=F32)
                        dv_h = lax.dot_general(p_scr[j * heads + h], doms[j, h], _TN, preferred_element_type=F32)
                        acc[key] = (dk_h, dv_h) if key not in acc else (acc[key][0] + dk_h, acc[key][1] + dv_h)
                    dk_j, dv_j = acc.get(False, (None, None))
                    if True in acc:
                        unswap = lambda t: jnp.concatenate([t[:, HEAD_DIM:], t[:, :HEAD_DIM]], axis=1)
                        dk_s, dv_s = unswap(acc[True][0]), unswap(acc[True][1])
                        dk_j = dk_s if dk_j is None else dk_j + dk_s
                        dv_j = dv_s if dv_j is None else dv_j + dv_s
                    sl = slice(LANES * slab, LANES * (slab + 1))
                    own_rows = steps.rows_of(j)
                    if not steps.consecutive:
                        dk_ref[own_rows, sl] = (kcar[own_rows, sl] + dk_j[0:BLOCK]).astype(BF16)
                        dv_ref[own_rows, sl] = (vcar[own_rows, sl] + dv_j[0:BLOCK]).astype(BF16)
                    elif j == 0:
                        last = steps.rows_of(nq - 1)
                        dk_ref[last, sl] = (kcar[last, sl] + dk_j[0:BLOCK]).astype(BF16)
                        dv_ref[last, sl] = (vcar[last, sl] + dv_j[0:BLOCK]).astype(BF16)
                    else:
                        before = steps.rows_of(j - 1)
                        kcar[before, sl] += dk_j[0:BLOCK]
                        vcar[before, sl] += dv_j[0:BLOCK]
                    kcar[own_rows, sl] = dk_j[BLOCK:2 * BLOCK]
                    vcar[own_rows, sl] = dv_j[BLOCK:2 * BLOCK]
            if sink is not None:
                dsink_ref[0:1, :] += dsink_row

        @pl.when(i == steps.inner)
        def _():
            dk_ref[...] = kcar[...].astype(BF16)
            dv_ref[...] = vcar[...].astype(BF16)

        if n_rs:
            @pl.when(flat == n_flat - 1)
            def _():
                exchange.finish()

    own, prev = (lambda w: steps.own(w, clamp=True)), (lambda w: steps.prev(w, clamp=True))
    rs_shapes = [t.shape[1:] for t in reduce_scatter]
    in_specs = ([own(qw), prev(kw), own(kw), prev(kw), own(kw), own(qw), own(LANES)]
                + [pl.BlockSpec(memory_space=pl.ANY)] * n_rs)
    args = [q, k, k, v, v, d_out, stat, *reduce_scatter]
    out_specs = [own(qw), steps.late(kw), steps.late(kw)] + [_full(s) for s in rs_shapes]
    out_shape = [jax.ShapeDtypeStruct((seq, qw), BF16), jax.ShapeDtypeStruct((seq, kw), BF16),
                 jax.ShapeDtypeStruct((seq, kw), BF16)] + [jax.ShapeDtypeStruct(s, F32) for s in rs_shapes]
    if sink is not None:
        in_specs = [pl.BlockSpec(memory_space=pltpu.SMEM)] + in_specs
        args = [sink] + args
        out_specs = [_full((8, LANES))] + out_specs
        out_shape = [jax.ShapeDtypeStruct((8, LANES), F32)] + out_shape
    n_hb = nq * heads
    res = pl.pallas_call(
        body, name=name, grid=(steps.outer, steps.inner + 1), in_specs=in_specs, out_specs=out_specs,
        out_shape=out_shape,
        scratch_shapes=[pltpu.VMEM((steps.rows, kw), F32), pltpu.VMEM((steps.rows, kw), F32)]
        + [pltpu.VMEM((n_hb, BLOCK, 2 * BLOCK), F32)] * 2 + [pltpu.VMEM((n_hb, BLOCK, 2 * BLOCK), BF16)] * 2
        + (_ReduceScatter.scratch_shapes(rs_shapes) if n_rs else []),
        compiler_params=_params(dimension_semantics=("arbitrary", "arbitrary")),
    )(*args)
    if sink is not None:
        return (*res[1:4], res[0], *res[4:])
    return res


def _cross_fwd(q, mk, mv, tq=1024):
    seq = q.shape[0]

    def body(q_ref, mk_ref, mv_ref, o_ref, lse_ref, s_scr, p_scr):
        qv = q_ref[...]
        k_of, v_of = _KvSlabs(mk_ref[...], 1), _KvSlabs(mv_ref[...], 1)
        lane = lax.broadcasted_iota(jnp.int32, (tq, LANES), 1)
        lse_tile = jnp.zeros((tq, LANES), F32)
        for h in range(C_HEADS):
            s_scr[h] = lax.dot_general(_head_only(qv, h), k_of(h), _NT, preferred_element_type=F32)
        ls = []
        for h in range(C_HEADS):
            s = s_scr[h]
            m = jnp.max(s, axis=-1, keepdims=True)
            p = jnp.exp(s - m)
            l = jnp.sum(p, axis=-1, keepdims=True)
            p_scr[h] = p.astype(BF16)
            ls.append(l)
            lse_tile = jnp.where(lane == h, m + jnp.log(l), lse_tile)
        for pr in range(C_HEADS // 2):
            even = jnp.dot(p_scr[2 * pr], v_of(2 * pr), preferred_element_type=F32) / ls[2 * pr]
            odd = jnp.dot(p_scr[2 * pr + 1], v_of(2 * pr + 1), preferred_element_type=F32) / ls[2 * pr + 1]
            o_ref[:, LANES * pr:LANES * (pr + 1)] = jnp.where(lane < HEAD_DIM, even, odd).astype(BF16)
        lse_ref[...] = lse_tile

    return pl.pallas_call(
        body, name="cross_fwd", grid=(seq // tq,),
        in_specs=[_row(tq, C_W), _full((N_MEM, C_W)), _full((N_MEM, C_W))],
        out_specs=[_row(tq, C_W), _row(tq, LANES)],
        out_shape=[jax.ShapeDtypeStruct((seq, C_W), BF16), jax.ShapeDtypeStruct((seq, LANES), F32)],
        scratch_shapes=[pltpu.VMEM((C_HEADS, tq, N_MEM), F32), pltpu.VMEM((C_HEADS, tq, N_MEM), BF16)],
        compiler_params=_params(dimension_semantics=("arbitrary",)),
    )(q, mk, mv)


def _cross_bwd(q, mk, mv, d_out, stat, tq=1024):
    seq = q.shape[0]

    def body(q_ref, mk_ref, mv_ref, do_ref, st_ref, dq_ref, dmk_ref, dmv_ref, s_scr, dp_scr, p_scr, ds_scr):
        @pl.when(pl.program_id(0) == 0)
        def _():
            dmk_ref[...] = jnp.zeros_like(dmk_ref)
            dmv_ref[...] = jnp.zeros_like(dmv_ref)

        qv, dov, st = q_ref[...], do_ref[...], st_ref[...]
        k_of, v_of = _KvSlabs(mk_ref[...], 1), _KvSlabs(mv_ref[...], 1)
        qms = [_head_only(qv, h) for h in range(C_HEADS)]
        doms = [_head_only(dov, h) for h in range(C_HEADS)]
        for h in range(C_HEADS):
            s_scr[h] = lax.dot_general(qms[h], k_of(h), _NT, preferred_element_type=F32)
            dp_scr[h] = lax.dot_general(doms[h], v_of(h), _NT, preferred_element_type=F32)
        for h in range(C_HEADS):
            p = jnp.exp(s_scr[h] - st[:, h:h + 1])
            p_scr[h] = p.astype(BF16)
            ds_scr[h] = (p * (dp_scr[h] - st[:, DELTA_LANE + h:DELTA_LANE + h + 1])).astype(BF16)
        lane = lax.broadcasted_iota(jnp.int32, (tq, LANES), 1)
        for pr in range(C_HEADS // 2):
            sl = slice(LANES * pr, LANES * (pr + 1))
            even = jnp.dot(ds_scr[2 * pr], k_of(2 * pr), preferred_element_type=F32)
            odd = jnp.dot(ds_scr[2 * pr + 1], k_of(2 * pr + 1), preferred_element_type=F32)
            dq_ref[:, sl] = jnp.where(lane < HEAD_DIM, even, odd).astype(BF16)
            dmk_ref[:, sl] += (lax.dot_general(ds_scr[2 * pr], qms[2 * pr], _TN, preferred_element_type=F32)
                               + lax.dot_general(ds_scr[2 * pr + 1], qms[2 * pr + 1], _TN, preferred_element_type=F32))
            dmv_ref[:, sl] += (lax.dot_general(p_scr[2 * pr], doms[2 * pr], _TN, preferred_element_type=F32)
                               + lax.dot_general(p_scr[2 * pr + 1], doms[2 * pr + 1], _TN, preferred_element_type=F32))

    return pl.pallas_call(
        body, name="cross_bwd", grid=(seq // tq,),
        in_specs=[_row(tq, C_W), _full((N_MEM, C_W)), _full((N_MEM, C_W)), _row(tq, C_W), _row(tq, LANES)],
        out_specs=[_row(tq, C_W), _full((N_MEM, C_W)), _full((N_MEM, C_W))],
        out_shape=[jax.ShapeDtypeStruct((seq, C_W), BF16), jax.ShapeDtypeStruct((N_MEM, C_W), F32),
                   jax.ShapeDtypeStruct((N_MEM, C_W), F32)],
        scratch_shapes=[pltpu.VMEM((C_HEADS, tq, N_MEM), F32)] * 2 + [pltpu.VMEM((C_HEADS, tq, N_MEM), BF16)] * 2,
        compiler_params=_params(dimension_semantics=("arbitrary",)),
    )(q, mk, mv, d_out, stat)


def _per_head(tile, width):
    rows = tile.shape[0]
    return jnp.concatenate(
        [jnp.broadcast_to(tile[:, h:h + 1], (rows, HEAD_DIM)) for h in range(width // HEAD_DIM)], axis=1)


def _with_delta(lse_tile, prod):
    rows = lse_tile.shape[0]
    lane = lax.broadcasted_iota(jnp.int32, (rows, LANES), 1)
    tile = lse_tile
    for h in range(prod.shape[1] // HEAD_DIM):
        d = jnp.sum(prod[:, HEAD_DIM * h:HEAD_DIM * (h + 1)], axis=-1, keepdims=True)
        tile = jnp.where(lane == DELTA_LANE + h, d, tile)
    return tile


def _mid(oa, lse_a, ob, lse_b, oc, lse_c, gate, x, target, w_out_full, post_g, tm=512):
    seq = x.shape[0]
    n_b = B_W // LANES

    def body(oa_ref, la_ref, b1_ref, l1_ref, b4_ref, l4_ref, b16_ref, l16_ref, oc_ref, lc_ref,
             gate_ref, x_ref, t_ref, w_ref, pg_ref,
             dh_ref, dg_ref, doa_ref, sa_ref, dob1_ref, sb1_ref, dob4_ref, sb4_ref, dob16_ref, sb16_ref,
             doc_ref, sc_ref, dw_ref, st_ref, scr_b4, scr_b16, scr_l4, scr_l16, scr_do, scr_sb):
        @pl.when(pl.program_id(0) == 0)
        def _():
            dw_ref[...] = jnp.zeros_like(dw_ref)
            st_ref[...] = jnp.zeros_like(st_ref)

        b1, l1 = b1_ref[...].astype(F32), l1_ref[...]
        b4, l4 = _load_permuted(b4_ref, scr_b4, 4), _load_permuted(l4_ref, scr_l4, 4)
        b16, l16 = _load_permuted(b16_ref, scr_b16, 16), _load_permuted(l16_ref, scr_l16, 16)
        lm = jnp.maximum(jnp.maximum(l1, l4), l16)
        e1, e4, e16 = jnp.exp(l1 - lm), jnp.exp(l4 - lm), jnp.exp(l16 - lm)
        den = e1 + e4 + e16
        lse_b_tile = lm + jnp.log(den)
        ob_v = _per_head(e1 / den, B_W) * b1 + _per_head(e4 / den, B_W) * b4 + _per_head(e16 / den, B_W) * b16
        o_all = jnp.concatenate([oa_ref[...].astype(F32), ob_v, oc_ref[...].astype(F32)], axis=1)
        g = gate_ref[...].astype(F32)
        sig = 1.0 / (1.0 + jnp.exp(-g))
        silu = g * sig
        y = (o_all * silu).astype(BF16)
        w = w_ref[...]
        z = jnp.dot(y, w, preferred_element_type=F32)
        rz = lax.rsqrt(jnp.mean(z * z, axis=-1, keepdims=True) + RMS_EPS)
        hn = z * rz
        pg = pg_ref[...]
        err = (x_ref[...] + hn * pg) - t_ref[...]
        loss = 0.5 * jnp.sum(jnp.mean(err * err, axis=-1, keepdims=True), axis=0, keepdims=True)
        dh = err * (1.0 / D_MODEL)
        dh_ref[...] = dh.astype(BF16)
        st_ref[0:1, :] += jnp.sum(dh * hn, axis=0, keepdims=True)
        st_ref[1:2, :] += jnp.broadcast_to(loss, (1, D_MODEL))
        dhn = dh * pg
        dz = (rz * (dhn - hn * jnp.mean(dhn * hn, axis=-1, keepdims=True))).astype(BF16)
        dy = lax.dot_general(dz, w, _NT, preferred_element_type=F32)
        dw_ref[...] += lax.dot_general(y, dz, _TN, preferred_element_type=F32)
        dg_ref[...] = (dy * o_all * (sig * (1.0 + g * (1.0 - sig)))).astype(BF16)
        d_o = (dy * silu).astype(BF16)
        prod = d_o.astype(F32) * o_all
        doa_ref[...] = d_o[:, 0:A_W]
        sa_ref[...] = _with_delta(la_ref[...], prod[:, 0:A_W])
        doc_ref[...] = d_o[:, A_W + B_W:D_MODEL]
        sc_ref[...] = _with_delta(lc_ref[...], prod[:, A_W + B_W:D_MODEL])
        d_ob = d_o[:, A_W:A_W + B_W]
        stat_b = _with_delta(lse_b_tile, prod[:, A_W:A_W + B_W])
        dob1_ref[...] = d_ob
        sb1_ref[...] = stat_b
        _put(scr_do, d_ob.astype(F32))
        _put(scr_sb, stat_b)
        _store_permuted(scr_do, dob4_ref, 4, BF16)
        _store_permuted(scr_sb, sb4_ref, 4, F32)
        _store_permuted(scr_do, dob16_ref, 16, BF16)
        _store_permuted(scr_sb, sb16_ref, 16, F32)

    p4 = lambda w: _perm_spec(tm, 4, w)
    p16 = lambda w: _perm_spec(tm, 16, w)
    in_specs = [_row(tm, A_W), _row(tm, LANES), _row(tm, B_W), _row(tm, LANES), p4(B_W), p4(LANES), p16(B_W), p16(LANES),
                _row(tm, C_W), _row(tm, LANES), _row(tm, D_MODEL), _row(tm, D_MODEL), _row(tm, D_MODEL),
                _full((D_MODEL, D_MODEL)), _full((1, D_MODEL))]
    sds = jax.ShapeDtypeStruct
    v4 = lambda w, dt: sds((seq // (BLOCK * 4), 4, BLOCK, w), dt)
    v16 = lambda w, dt: sds((seq // (BLOCK * 16), 16, BLOCK, w), dt)
    out_specs = [_row(tm, D_MODEL), _row(tm, D_MODEL), _row(tm, A_W), _row(tm, LANES), _row(tm, B_W), _row(tm, LANES),
                 p4(B_W), p4(LANES), p16(B_W), p16(LANES), _row(tm, C_W), _row(tm, LANES),
                 _full((D_MODEL, D_MODEL)), _full((8, D_MODEL))]
    out_shape = [sds((seq, D_MODEL), BF16), sds((seq, D_MODEL), BF16), sds((seq, A_W), BF16), sds((seq, LANES), F32),
                 sds((seq, B_W), BF16), sds((seq, LANES), F32), v4(B_W, BF16), v4(LANES, F32), v16(B_W, BF16),
                 v16(LANES, F32), sds((seq, C_W), BF16), sds((seq, LANES), F32),
                 sds((D_MODEL, D_MODEL), F32), sds((8, D_MODEL), F32)]
    res = pl.pallas_call(
        body, name="mid", grid=(seq // tm,), in_specs=in_specs, out_specs=out_specs, out_shape=out_shape,
        scratch_shapes=[pltpu.VMEM((n_b, tm, LANES), F32), pltpu.VMEM((n_b, tm, LANES), F32),
                        pltpu.VMEM((1, tm, LANES), F32), pltpu.VMEM((1, tm, LANES), F32),
                        pltpu.VMEM((n_b, tm, LANES), F32), pltpu.VMEM((1, tm, LANES), F32)],
        compiler_params=_params(dimension_semantics=("arbitrary",)),
    )(oa, lse_a, ob[1], lse_b[1], _perm_view(ob[4], 4), _perm_view(lse_b[4], 4), _perm_view(ob[16], 16),
      _perm_view(lse_b[16], 16), oc, lse_c, gate, x, target, w_out_full, post_g)
    dh, d_gate, do_a, st_a, do_b1, st_b1, do_b4, st_b4, do_b16, st_b16, do_c, st_c, d_wout, stats = res
    flat = lambda t: t.reshape(seq, t.shape[-1])
    d_b = {1: (do_b1, st_b1), 4: (flat(do_b4), flat(st_b4)), 16: (flat(do_b16), flat(st_b16))}
    return dh, d_gate, (do_a, st_a), d_b, (do_c, st_c), d_wout, stats


def _inproj_bwd(x, u, dh, pre_g, w_in_full, tabs, dqa, dka, dva, dqkv_b, dqc, dgate, tm=512):
    seq = x.shape[0]
    n_b = B_W // LANES

    def body(x_ref, u_ref, dh_ref, g_ref, w_hbm, c_ref, up_ref, dn_ref, dqa_ref, dka_ref, dva_ref,
             dq1, dk1, dv1, dq4, dk4, dv4, dq16, dk16, dv16, dqc_ref, dg_ref,
             gx_ref, dw_ref, st_ref, scr4, scr16, w_scr, w_sems, dp_ref):
        _stage_w_in(w_hbm, w_scr, w_sems)

        @pl.when(pl.program_id(0) == 0)
        def _():
            st_ref[...] = jnp.zeros_like(st_ref)
            dw_ref[...] = jnp.zeros_like(dw_ref)

        c, up, dn = c_ref[...], -up_ref[...], -dn_ref[...]
        unrot = lambda t: _rotate(t, c, up, dn)
        total = lambda r1, r4, r16: (r1[...].astype(F32) + _load_permuted(r4, scr4, 4)
                                     + _load_permuted(r16, scr16, 16))
        at = lambda piece: slice(*COLS[piece])
        dp_ref[:, at("qa")] = (unrot(dqa_ref[...].astype(F32)) * SCALE).astype(BF16)
        dp_ref[:, at("ka")] = unrot(dka_ref[...].astype(F32)).astype(BF16)
        dp_ref[:, at("va")] = dva_ref[...]
        dp_ref[:, at("ga")] = dg_ref[:, 0:A_W]
        dp_ref[:, at("qb")] = (unrot(total(dq1, dq4, dq16)) * SCALE).astype(BF16)
        dp_ref[:, at("kb")] = unrot(total(dk1, dk4, dk16)).astype(BF16)
        dp_ref[:, at("vb")] = total(dv1, dv4, dv16).astype(BF16)
        dp_ref[:, at("gb")] = dg_ref[:, A_W:A_W + B_W]
        dp_ref[:, at("qc")] = (dqc_ref[...].astype(F32) * SCALE).astype(BF16)
        dp_ref[:, at("gc")] = dg_ref[:, A_W + B_W:D_MODEL]
        du = lax.dot_general(dp_ref[...], w_scr[...], _NT, preferred_element_type=F32)
        res = lax.dot_general(u_ref[...], dp_ref[...], _TN, preferred_element_type=F32)
        for k in range(N_DEV):
            dw_ref[k] += res[:, SHARD_IN * k:SHARD_IN * (k + 1)]
        xv = x_ref[...]
        r = lax.rsqrt(jnp.mean(xv * xv, axis=-1, keepdims=True) + RMS_EPS)
        xh = xv * r
        st_ref[0:1, :] += jnp.sum(du * xh, axis=0, keepdims=True)
        dxh = du * g_ref[...]
        gx_ref[...] = dh_ref[...].astype(F32) + r * (dxh - xh * jnp.mean(dxh * xh, axis=-1, keepdims=True))

    in_specs = ([_row(tm, D_MODEL), _row(tm, D_MODEL), _row(tm, D_MODEL), _full((1, D_MODEL)),
                 pl.BlockSpec(memory_space=pl.ANY),
                 _row(tm, LANES), _row(tm, LANES), _row(tm, LANES), _row(tm, A_W), _row(tm, A_KV_W), _row(tm, A_KV_W)]
                + [_row(tm, B_W)] * 3 + [_perm_spec(tm, 4, B_W)] * 3 + [_perm_spec(tm, 16, B_W)] * 3
                + [_row(tm, C_W), _row(tm, D_MODEL)])
    dw_spec = pl.BlockSpec((N_DEV, D_MODEL, SHARD_IN), lambda i: (0, 0, 0), pipeline_mode=pl.Buffered(1))
    return pl.pallas_call(
        body, name="inproj_bwd", grid=(seq // tm,), in_specs=in_specs,
        out_specs=[_row(tm, D_MODEL), dw_spec, _full((8, D_MODEL))],
        out_shape=[jax.ShapeDtypeStruct((seq, D_MODEL), F32), jax.ShapeDtypeStruct((N_DEV, D_MODEL, SHARD_IN), F32),
                   jax.ShapeDtypeStruct((8, D_MODEL), F32)],
        scratch_shapes=[pltpu.VMEM((n_b, tm, LANES), F32), pltpu.VMEM((n_b, tm, LANES), F32)] + _w_in_scratch()
        + [pltpu.VMEM((tm, D_IN), BF16)],
        compiler_params=_params(dimension_semantics=("arbitrary",)),
    )(x, u, dh, pre_g, w_in_full, *tabs, dqa, dka, dva, *dqkv_b[1], *[_perm_view(t, 4) for t in dqkv_b[4]],
      *[_perm_view(t, 16) for t in dqkv_b[16]], dqc, dgate)


class _ReduceScatter:
    def __init__(self, ins, outs, scratch):
        self.n = n = len(ins)
        self.ins, self.outs = ins, outs
        self.mine, self.got, self.snd, self.rcv = (scratch[n * t:n * (t + 1)] for t in range(4))
        self.load_sems, self.d2d_send, self.d2d_recv, self.ici_send, self.ici_recv = scratch[4 * n:]
        self.pos = _mesh_pos()
        self.pairs = [(a, kk) for kk in (3, 1, 2) for a in range(n)]

    @staticmethod
    def scratch_shapes(shapes):
        return ([pltpu.VMEM((4,) + s, F32) for s in shapes] + [pltpu.VMEM((4,) + s, F32) for s in shapes]
                + [pltpu.VMEM((3,) + s, BF16) for s in shapes] + [pltpu.VMEM((3,) + s, BF16) for s in shapes]
                + [pltpu.SemaphoreType.DMA((len(shapes), 4))] * 5)

    def _chip(self, kk):
        x, y, _ = self.pos
        return (1 - x if kk & 2 else x, 1 - y if kk & 1 else y)

    def _load(self, a, kk):
        block = _dev_index((*self._chip(kk), self.pos[2]))
        return pltpu.make_async_copy(self.ins[a].at[block], self.mine[a].at[kk], self.load_sems.at[a, kk])

    def _swap(self, a, kk):
        x, y, c = self.pos
        return pltpu.make_async_remote_copy(
            src_ref=self.ins[a].at[_dev_index((*self._chip(kk), 1 - c))], dst_ref=self.got[a].at[kk],
            send_sem=self.d2d_send.at[a, kk], recv_sem=self.d2d_recv.at[a, kk],
            device_id=(x, y, 1 - c), device_id_type=MESH_ID)

    def _hop(self, a, kk):
        return pltpu.make_async_remote_copy(
            src_ref=self.snd[a].at[kk - 1], dst_ref=self.rcv[a].at[kk - 1], send_sem=self.ici_send.at[a, kk],
            recv_sem=self.ici_recv.at[a, kk], device_id=(*self._chip(kk), self.pos[2]), device_id_type=MESH_ID)

    def start(self):
        for kk in (3, 1, 2, 0):
            for a in range(self.n):
                self._load(a, kk).start()
                self._swap(a, kk).start()

    def send_chip_sums(self):
        for a, kk in self.pairs:
            self._load(a, kk).wait()
            self._swap(a, kk).wait_recv()
            self.snd[a][kk - 1] = (self.mine[a][kk] + self.got[a][kk]).astype(BF16)
            self._hop(a, kk).start()

    def finish(self):
        for a in range(self.n):
            self._load(a, 0).wait()
            self._swap(a, 0).wait_recv()
            acc = self.mine[a][0] + self.got[a][0]
            for kk in (1, 2, 3):
                self._hop(a, kk).wait_recv()
                acc = acc + self.rcv[a][kk - 1].astype(F32)
            self.outs[a][...] = acc
        for kk in range(4):
            for a in range(self.n):
                self._swap(a, kk).wait_send()
        for a, kk in self.pairs:
            self._hop(a, kk).wait_send()


def _local_step(x, mem, pre_g, w_in, sink, mem_g, w_mem, w_out, post_g, target):
    u, *tabs, w_in_full = _prep(x, pre_g, w_in)
    qa, ka, va, qkv_b, qc, gate, w_mem_all, w_out_all = _inproj(u, w_in_full, tabs, w_mem, w_out)
    w_mem_full = w_mem_all.reshape(D_MODEL, 2 * C_W)
    w_out_full = w_out_all.reshape(D_MODEL, D_MODEL)
    mn, mk, mv = _memkv_fwd(mem, mem_g, w_mem_full)

    a_cfg = dict(dil=1, heads=A_HEADS, group=A_GROUP, max_dist=BLOCK - 1, nq=ATTN_BLOCKS_PER_STEP)
    b_cfgs = {dil: dict(dil=dil, heads=B_HEADS, group=1, max_dist=win // dil, nq=ATTN_BLOCKS_PER_STEP)
              for win, dil in B_CONFIGS}
    oa, lse_a = _banded_fwd(qa, ka, va, sink, name="attn_a_fwd", **a_cfg)
    ob, lse_b = {}, {}
    for dil, cfg in b_cfgs.items():
        ob[dil], lse_b[dil] = _banded_fwd(*qkv_b[dil], None, name=f"attn_b{dil}_fwd", **cfg)
    oc, lse_c = _cross_fwd(qc, mk, mv)

    dh, d_gate, d_a, d_b, d_c, d_wout, st_mid = _mid(oa, lse_a, ob, lse_b, oc, lse_c, gate, x, target, w_out_full, post_g)

    dqc, dmk, dmv = _cross_bwd(qc, mk, mv, *d_c)
    d_wmem, st_mem = _memkv_bwd(mem, mem_g, mn, w_mem_full, dmk, dmv)
    dqkv_b = {dil: _banded_bwd(*qkv_b[dil], *d_b[dil], None, name=f"attn_b{dil}_bwd", **cfg)
              for dil, cfg in b_cfgs.items()}
    dqa, dka, dva, dsink, g_wmem, g_wout = _banded_bwd(
        qa, ka, va, *d_a, sink, name="attn_a_bwd", **a_cfg,
        reduce_scatter=(d_wmem.reshape(N_DEV, SHARD_ROWS, 2 * C_W), d_wout.reshape(N_DEV, SHARD_ROWS, D_MODEL)))

    grad_x, d_win, st_pre = _inproj_bwd(x, u, dh, pre_g, w_in_full, tabs, dqa, dka, dva, dqkv_b, dqc, d_gate)

    dsink_row = jnp.pad(dsink[0:1, :], ((0, 0), (0, D_MODEL - LANES)))
    stats = jnp.concatenate([st_pre[0:1], st_mem[0:1], st_mid[0:1], dsink_row, st_mid[1:2],
                             jnp.zeros((3, D_MODEL), F32)], axis=0)
    return grad_x, d_win, g_wmem, g_wout, stats


def _prep(x, pre_g, w_in, tm=1024):
    seq = x.shape[0]
    n_steps = seq // tm
    parts = 2
    rows = D_MODEL // parts
    relay_at = min(3, n_steps - 1)
    j = jnp.arange(LANES) % HEAD_DIM
    freq = (ROPE_THETA ** (-(2 * (j % (ROT_DIM // 2))).astype(F32) / ROT_DIM))[None, :]
    SIB, NB_X, NB_Y, RELAY, FWD = 0, 1, 2, 3, 4

    def body(x_ref, g_ref, f_ref, win_ref, u_ref, c_ref, up_ref, dn_ref, win_out, win_b,
             send_sems, recv_sems, local_sems):
        step = pl.program_id(0)
        px, py, pc = _mesh_pos()
        me, sibling = (px, py, pc), (px, py, 1 - pc)
        others = lambda core: ((1 - px, py, core), (px, 1 - py, core), (1 - px, 1 - py, core))
        x_nb, y_nb, diag = others(pc)
        relay_from = [x_nb, y_nb]
        relay_to = [y_nb, x_nb]

        def src(a):
            return win_b.at[pl.ds(rows * a, rows)]

        def slot(a, p):
            return win_out.at[_dev_index(p), pl.ds(rows * a, rows)]

        def copy(a, k, block, to, own=False):
            return pltpu.make_async_remote_copy(
                src_ref=src(a) if own else slot(a, block), dst_ref=slot(a, block),
                send_sem=send_sems.at[a, k], recv_sem=recv_sems.at[a, k], device_id=to, device_id_type=MESH_ID)

        def first_sends():
            return [copy(0, NB_X, me, x_nb, own=True), copy(1, NB_Y, me, y_nb, own=True),
                    copy(1, NB_X, me, x_nb, own=True), copy(0, NB_Y, me, y_nb, own=True),
                    copy(0, SIB, me, sibling, own=True), copy(1, SIB, me, sibling, own=True)]

        def relay(a):
            return copy(a, RELAY, relay_from[a], relay_to[a])

        def to_sibling(a, which):
            return copy(a, FWD + which, others(pc)[which], sibling)

        def local(a):
            return pltpu.make_async_copy(src(a), slot(a, me), local_sems.at[a])

        @pl.when(step == 0)
        def _():
            win_b[...] = win_ref[...].astype(BF16)
            for a in range(parts):
                local(a).start()
            for cp in first_sends():
                cp.start()

        @pl.when(step == relay_at)
        def _():
            for a in range(parts):
                copy(a, NB_X + a, relay_from[a], me).wait_recv()
                relay(a).start()
                to_sibling(a, a).start()

        xv = x_ref[...]
        r = lax.rsqrt(jnp.mean(xv * xv, axis=-1, keepdims=True) + RMS_EPS)
        u_ref[...] = ((xv * r) * g_ref[...]).astype(BF16)
        pos = (lax.broadcasted_iota(jnp.int32, (tm, LANES), 0) + step * tm).astype(F32)
        head_lane = lax.broadcasted_iota(jnp.int32, (tm, LANES), 1) % HEAD_DIM
        ang = pos * f_ref[...]
        cos, sin = jnp.cos(ang), jnp.sin(ang)
        half = ROT_DIM // 2
        c_ref[...] = jnp.where(head_lane < ROT_DIM, cos, 1.0)
        up_ref[...] = jnp.where((head_lane >= half) & (head_lane < ROT_DIM), sin, 0.0)
        dn_ref[...] = jnp.where(head_lane < half, -sin, 0.0)

        @pl.when(step == n_steps - 1)
        def _():
            copy(1, NB_X, x_nb, me).wait_recv()
            to_sibling(1, 0).start()
            copy(0, NB_Y, y_nb, me).wait_recv()
            to_sibling(0, 1).start()
            for a in range(parts):
                copy(a, RELAY, diag, me).wait_recv()
                to_sibling(a, 2).start()
            for a in range(parts):
                copy(a, SIB, sibling, me).wait_recv()
                for which in range(3):
                    copy(a, FWD + which, others(1 - pc)[which], me).wait_recv()
            for cp in first_sends():
                cp.wait_send()
            for a in range(parts):
                relay(a).wait_send()
                for which in range(3):
                    to_sibling(a, which).wait_send()
                local(a).wait()

    return pl.pallas_call(
        body, name="prep", grid=(n_steps,),
        in_specs=[_row(tm, D_MODEL), _full((1, D_MODEL)), _full((1, LANES)), _full(w_in.shape)],
        out_specs=[_row(tm, D_MODEL), _row(tm, LANES), _row(tm, LANES), _row(tm, LANES),
                   pl.BlockSpec(memory_space=pl.ANY)],
        out_shape=[jax.ShapeDtypeStruct((seq, D_MODEL), BF16)] + [jax.ShapeDtypeStruct((seq, LANES), F32)] * 3
        + [jax.ShapeDtypeStruct((N_DEV,) + w_in.shape, BF16)],
        scratch_shapes=[pltpu.VMEM(w_in.shape, BF16), pltpu.SemaphoreType.DMA((parts, FWD + 3)),
                        pltpu.SemaphoreType.DMA((parts, FWD + 3)), pltpu.SemaphoreType.DMA((parts,))],
        compiler_params=_params(dimension_semantics=("arbitrary",)),
    )(x, pre_g, freq, w_in)


def _exchange_grads(d_win, stats):
    def body(win, st, g_win, r_st, send_sems, recv_sems, local_sem, *scratch):
        exchange = _ReduceScatter((win,), (g_win,), scratch)
        exchange.start()
        pos = _mesh_pos()
        me = _dev_index(pos)
        own = pltpu.make_async_copy(st, r_st.at[me], local_sem)
        own.start()
        copies = []
        for s in range(1, N_DEV):
            peer = _xor_peer(pos, s)
            mk = lambda slot: pltpu.make_async_remote_copy(
                src_ref=st, dst_ref=r_st.at[slot], send_sem=send_sems.at[s], recv_sem=recv_sems.at[s],
                device_id=peer, device_id_type=MESH_ID)
            send, arrival = mk(me), mk(_dev_index(peer))
            send.start()
            copies.append((send, arrival))
        exchange.send_chip_sums()
        exchange.finish()
        for send, arrival in copies:
            arrival.wait_recv()
            send.wait_send()
        own.wait()

    hbm = pl.BlockSpec(memory_space=pl.ANY)
    shard = d_win.shape[1:]
    return pl.pallas_call(
        body, name="exchange_grads", in_specs=[hbm, hbm],
        out_specs=[pl.BlockSpec(memory_space=pltpu.VMEM), hbm],
        out_shape=[jax.ShapeDtypeStruct(shard, F32), jax.ShapeDtypeStruct((N_DEV,) + stats.shape, F32)],
        scratch_shapes=[pltpu.SemaphoreType.DMA((N_DEV,)), pltpu.SemaphoreType.DMA((N_DEV,)), pltpu.SemaphoreType.DMA(())]
        + _ReduceScatter.scratch_shapes([shard]),
        compiler_params=_params(),
    )(d_win, stats)


WEIGHT_ORDER = ("pre_norm", "w_in", "sink_a", "mem_norm", "w_mem_kv", "w_out", "post_norm")


def _adamw_all(grads, r_stats, weights, moments_m, moments_v):
    n = len(WEIGHT_ORDER)
    stat_row = {"pre_norm": 0, "mem_norm": 1, "post_norm": 2, "sink_a": 3}

    def body(*refs):
        gw_in, gw_mem, gw_out, st_ref = refs[0:4]
        w_refs, m_refs, v_refs = (dict(zip(WEIGHT_ORDER, refs[4 + n * t:4 + n * (t + 1)])) for t in range(3))
        loss_ref = refs[4 + 3 * n]
        outs = refs[5 + 3 * n:]
        g_small = st_ref[0]
        for s in range(1, N_DEV):
            g_small = g_small + st_ref[s]
        loss_ref[...] = g_small[4:5, 0:1]
        big = {"w_in": gw_in, "w_mem_kv": gw_mem, "w_out": gw_out}
        for i, name in enumerate(WEIGHT_ORDER):
            if name in big:
                g = big[name][...]
                at = lambda ref: ref[0]
            else:
                width = w_refs[name].shape[-1]
                g = g_small[stat_row[name]:stat_row[name] + 1, 0:width]
                at = lambda ref: ref[...]
            m2 = ADAM_B1 * at(m_refs[name]) + (1.0 - ADAM_B1) * g
            v2 = ADAM_B2 * at(v_refs[name]) + (1.0 - ADAM_B2) * (g * g)
            m_hat = m2 / (1.0 - ADAM_B1 ** ADAM_STEP)
            v_hat = v2 / (1.0 - ADAM_B2 ** ADAM_STEP)
            delta = -ADAM_LR * (m_hat / (jnp.sqrt(v_hat) + ADAM_EPS) + ADAM_WD * at(w_refs[name]))
            for kind, val in enumerate((g, delta, m2, v2)):
                out = outs[kind * n + i]
                if name in big:
                    out[0] = val
                else:
                    out[...] = val

    shapes = [weights[name].shape for name in WEIGHT_ORDER]
    res = pl.pallas_call(
        body, name="adamw_all",
        out_shape=[jax.ShapeDtypeStruct((1, 1), F32)] + [jax.ShapeDtypeStruct(sh, F32) for sh in shapes] * 4,
        compiler_params=_params(),
    )(grads["w_in"], grads["w_mem_kv"], grads["w_out"], r_stats,
      *[weights[k] for k in WEIGHT_ORDER], *[moments_m[k] for k in WEIGHT_ORDER], *[moments_v[k] for k in WEIGHT_ORDER])
    return res[0].reshape(()), res[1:]


def kernel(x, mem, pre_norm, w_in, sink_a, mem_norm, w_mem_kv, w_out, post_norm, loss_target, m_pre_norm, m_w_in, m_sink_a, m_mem_norm, m_w_mem_kv, m_w_out, m_post_norm, v_pre_norm, v_w_in, v_sink_a, v_mem_norm, v_w_mem_kv, v_w_out, v_post_norm):
    sink = jnp.pad(sink_a[0], (0, 8 - A_HEADS))
    grad_x, d_win, g_wmem, g_wout, stats = _local_step(
        x[0], mem[0], pre_norm, w_in[0], sink, mem_norm, w_mem_kv[0], w_out[0], post_norm, loss_target[0])
    g_win, r_stats = _exchange_grads(d_win, stats)
    weights = dict(pre_norm=pre_norm, w_in=w_in, sink_a=sink_a, mem_norm=mem_norm, w_mem_kv=w_mem_kv, w_out=w_out,
                   post_norm=post_norm)
    moments_m = dict(pre_norm=m_pre_norm, w_in=m_w_in, sink_a=m_sink_a, mem_norm=m_mem_norm, w_mem_kv=m_w_mem_kv,
                     w_out=m_w_out, post_norm=m_post_norm)
    moments_v = dict(pre_norm=v_pre_norm, w_in=v_w_in, sink_a=v_sink_a, mem_norm=v_mem_norm, w_mem_kv=v_w_mem_kv,
                     w_out=v_w_out, post_norm=v_post_norm)
    loss, rest = _adamw_all(dict(w_in=g_win, w_mem_kv=g_wmem, w_out=g_wout), r_stats, weights, moments_m, moments_v)
    return (loss, grad_x[None], *rest)
```

```python
import jax
import jax.numpy as jnp
from jax import lax
from jax.experimental import pallas as pl
from jax.experimental.pallas import tpu as pltpu

F32 = jnp.float32
BF16 = jnp.bfloat16

D_MODEL = 1024
HEAD_DIM = 64
ROT_DIM = 16
ROPE_THETA = 500000.0
BLOCK = 128
LANES = 128
N_MEM = 256
RMS_EPS = 1e-6
SCALE = HEAD_DIM ** -0.5
A_HEADS, A_GROUP = 6, 3
B_HEADS = 6
C_HEADS = 4
A_W, A_KV_W, B_W, C_W = 384, 128, 384, 256
_IN_PIECES = (("qa", A_W), ("ka", A_KV_W), ("va", A_KV_W), ("ga", A_W), ("qb", B_W), ("kb", B_W), ("vb", B_W),
              ("gb", B_W), ("qc", C_W), ("gc", C_W))
COLS, D_IN = {}, 0
for _name, _width in _IN_PIECES:
    COLS[_name] = (D_IN, D_IN + _width)
    D_IN += _width
N_DEV = 8
SHARD_IN = D_IN // N_DEV
SHARD_ROWS = D_MODEL // N_DEV
B_CONFIGS = ((128, 1), (512, 4), (2048, 16))
DILS = (4, 16)
NEG = -1e30
ATTN_BLOCKS_PER_STEP = 4
DELTA_LANE = 64
VMEM_LIMIT = 56 * 1024 * 1024

ADAM_LR, ADAM_B1, ADAM_B2, ADAM_EPS, ADAM_WD, ADAM_STEP = 0.001, 0.9, 0.999, 1e-08, 0.01, 10
MESH_ID = pl.DeviceIdType.MESH


def _params(**kw):
    return pltpu.CompilerParams(vmem_limit_bytes=VMEM_LIMIT, **kw)


def _full(shape):
    n = len(shape)
    return pl.BlockSpec(shape, lambda *_: (0,) * n)


def _row(tm, w):
    return pl.BlockSpec((tm, w), lambda i: (i, 0))


def _mesh_pos():
    return lax.axis_index("x"), lax.axis_index("y"), lax.axis_index("c")


def _dev_index(pos):
    return 4 * pos[0] + 2 * pos[1] + pos[2]


def _xor_peer(pos, s):
    x, y, c = pos
    return (1 - x if s & 4 else x, 1 - y if s & 2 else y, 1 - c if s & 1 else c)


def _perm_view(a, dil):
    return a.reshape(a.shape[0] // (BLOCK * dil), dil, BLOCK, a.shape[1])


def _perm_spec(tm, dil, w):
    chunk = BLOCK * dil
    if tm >= chunk:
        return pl.BlockSpec((tm // chunk, dil, BLOCK, w), lambda i: (i, 0, 0, 0))
    per = chunk // tm
    return pl.BlockSpec((1, dil, tm // dil, w), lambda i: (i // per, 0, i % per, 0))


def _put(scr, val):
    for c in range(val.shape[1] // LANES):
        scr[c] = val[:, LANES * c:LANES * (c + 1)]


def _get(scr):
    n = scr.shape[0]
    return scr[0] if n == 1 else jnp.concatenate([scr[c] for c in range(n)], axis=1)


def _get_class(scr, r, dil):
    n, rows = scr.shape[0], scr.shape[1]
    parts = [scr.at[c][pl.ds(r, rows // dil, stride=dil), :] for c in range(n)]
    return parts[0] if n == 1 else jnp.concatenate(parts, axis=1)


def _store_permuted(scr, out_ref, dil, dtype):
    for r in range(dil):
        out_ref[0, r] = _get_class(scr, r, dil).astype(dtype)


def _load_permuted(in_ref, scr, dil):
    n, rows = scr.shape[0], scr.shape[1]
    for r in range(dil):
        val = in_ref[0, r].astype(F32)
        for c in range(n):
            scr.at[c][pl.ds(r, rows // dil, stride=dil), :] = val[:, LANES * c:LANES * (c + 1)]
    return _get(scr)


def _rotate128(t, c, up, dn):
    half = ROT_DIM // 2
    return t * c + pltpu.roll(t, half, 1) * up + pltpu.roll(t, LANES - half, 1) * dn


def _rotate(t, c, up, dn):
    outs = [_rotate128(t[:, LANES * j:LANES * (j + 1)], c, up, dn) for j in range(t.shape[1] // LANES)]
    return outs[0] if len(outs) == 1 else jnp.concatenate(outs, axis=1)


def _w_in_scratch():
    return [pltpu.VMEM((D_MODEL, D_IN), BF16), pltpu.SemaphoreType.DMA((N_DEV,))]


def _stage_w_in(w_hbm, w_scr, sems):
    @pl.when(pl.program_id(0) == 0)
    def _():
        copies = [pltpu.make_async_copy(w_hbm.at[k], w_scr.at[:, pl.ds(SHARD_IN * k, SHARD_IN)], sems.at[k])
                  for k in range(N_DEV)]
        for cp in copies:
            cp.start()
        for cp in copies:
            cp.wait()


def _inproj(u, w_in_full, tabs, w_mem, w_out, tm=1024):
    seq = u.shape[0]
    n_chunk = D_IN // LANES
    n_steps = seq // tm

    def body(u_ref, w_hbm, c_ref, up_ref, dn_ref, wm_ref, wo_ref, qa_ref, ka_ref, va_ref,
             qb1_ref, kb1_ref, vb1_ref, qb4_ref, kb4_ref, vb4_ref, qb16_ref, kb16_ref, vb16_ref,
             qc_ref, gate_ref, wm_all, wo_all, proj, w_scr, w_sems, wm_b, wo_b, send_sems, recv_sems, local_sems):
        step = pl.program_id(0)
        shards, gathered = (wm_b, wo_b), (wm_all, wo_all)

        def gather_copies(arriving):
            pos = _mesh_pos()
            me = _dev_index(pos)
            local = [] if arriving else [
                pltpu.make_async_copy(shards[a], gathered[a].at[me], local_sems.at[a]) for a in range(2)]
            remote = []
            for s in range(1, N_DEV):
                peer = _xor_peer(pos, s)
                for a in range(2):
                    remote.append(pltpu.make_async_remote_copy(
                        src_ref=shards[a], dst_ref=gathered[a].at[_dev_index(peer) if arriving else me],
                        send_sem=send_sems.at[a, s], recv_sem=recv_sems.at[a, s], device_id=peer,
                        device_id_type=MESH_ID))
            return local, remote

        @pl.when(step == 0)
        def _():
            wm_b[...] = wm_ref[...].astype(BF16)
            wo_b[...] = wo_ref[...].astype(BF16)
            local, sends = gather_copies(arriving=False)
            for cp in local + sends:
                cp.start()

        _stage_w_in(w_hbm, w_scr, w_sems)
        u = u_ref[...]
        for n0 in range(0, D_IN, D_MODEL):
            acc = jnp.dot(u, w_scr[:, n0:n0 + D_MODEL], preferred_element_type=F32)
            for c3 in range(D_MODEL // LANES):
                proj[n0 // LANES + c3] = acc[:, LANES * c3:LANES * (c3 + 1)]
        c, up, dn = c_ref[...], up_ref[...], dn_ref[...]

        def chunks_of(piece):
            lo, hi = COLS[piece]
            return range(lo // LANES, hi // LANES)

        def cols(piece, rot=False, scale=None):
            parts = []
            for ch in chunks_of(piece):
                t = proj[ch]
                if rot:
                    t = _rotate128(t, c, up, dn)
                if scale is not None:
                    t = t * scale
                parts.append(t)
            return parts[0] if len(parts) == 1 else jnp.concatenate(parts, axis=1)

        qa_ref[...] = cols("qa", True, SCALE).astype(BF16)
        ka_ref[...] = cols("ka", True).astype(BF16)
        va_ref[...] = cols("va").astype(BF16)
        gate_ref[:, 0:A_W] = cols("ga").astype(BF16)
        gate_ref[:, A_W:A_W + B_W] = cols("gb").astype(BF16)
        gate_ref[:, A_W + B_W:D_MODEL] = cols("gc").astype(BF16)
        qc_ref[...] = cols("qc", False, SCALE).astype(BF16)
        for ch in chunks_of("qb"):
            proj[ch] = _rotate128(proj[ch], c, up, dn) * SCALE
        for ch in chunks_of("kb"):
            proj[ch] = _rotate128(proj[ch], c, up, dn)
        for piece, nat, p4, p16 in (("qb", qb1_ref, qb4_ref, qb16_ref), ("kb", kb1_ref, kb4_ref, kb16_ref),
                                    ("vb", vb1_ref, vb4_ref, vb16_ref)):
            chunks = chunks_of(piece)
            nat[...] = jnp.concatenate([proj[ch] for ch in chunks], axis=1).astype(BF16)
            for dil, ref in ((4, p4), (16, p16)):
                span = min(tm, BLOCK * dil)
                for cc in range(tm // span):
                    for rr in range(dil):
                        ref[cc, rr] = jnp.concatenate(
                            [proj.at[ch][pl.ds(cc * span + rr, span // dil, stride=dil), :] for ch in chunks],
                            axis=1).astype(BF16)

        @pl.when(step == n_steps - 1)
        def _():
            for cp in gather_copies(arriving=True)[1]:
                cp.wait_recv()
            local, sends = gather_copies(arriving=False)
            for cp in sends:
                cp.wait_send()
            for cp in local:
                cp.wait()

    nat_w = (A_W, A_KV_W, A_KV_W, B_W, B_W, B_W)
    out_specs = [_row(tm, w) for w in nat_w]
    out_shape = [jax.ShapeDtypeStruct((seq, w), BF16) for w in nat_w]
    for dil in DILS:
        out_specs += [_perm_spec(tm, dil, B_W)] * 3
        out_shape += [jax.ShapeDtypeStruct((seq // (BLOCK * dil), dil, BLOCK, B_W), BF16)] * 3
    hbm = pl.BlockSpec(memory_space=pl.ANY)
    out_specs += [_row(tm, C_W), _row(tm, D_MODEL), hbm, hbm]
    out_shape += [jax.ShapeDtypeStruct((seq, C_W), BF16), jax.ShapeDtypeStruct((seq, D_MODEL), BF16),
                  jax.ShapeDtypeStruct((N_DEV,) + w_mem.shape, BF16), jax.ShapeDtypeStruct((N_DEV,) + w_out.shape, BF16)]
    res = pl.pallas_call(
        body, name="inproj", grid=(n_steps,),
        in_specs=[_row(tm, D_MODEL), hbm, _row(tm, LANES), _row(tm, LANES), _row(tm, LANES),
                  _full(w_mem.shape), _full(w_out.shape)],
        out_specs=out_specs, out_shape=out_shape,
        scratch_shapes=[pltpu.VMEM((n_chunk, tm, LANES), F32)] + _w_in_scratch()
        + [pltpu.VMEM(w_mem.shape, BF16), pltpu.VMEM(w_out.shape, BF16), pltpu.SemaphoreType.DMA((2, N_DEV)),
           pltpu.SemaphoreType.DMA((2, N_DEV)), pltpu.SemaphoreType.DMA((2,))],
        compiler_params=_params(dimension_semantics=("arbitrary",)),
    )(u, w_in_full, *tabs, w_mem, w_out)
    qa, ka, va = res[0:3]
    qkv_b = {1: res[3:6], 4: [t.reshape(seq, B_W) for t in res[6:9]], 16: [t.reshape(seq, B_W) for t in res[9:12]]}
    return qa, ka, va, qkv_b, res[12], res[13], res[14], res[15]


def _memkv_fwd(mem, mem_g, w_mem_full):
    def body(mem_ref, g_ref, w_ref, mn_ref, mk_ref, mv_ref):
        mv_ = mem_ref[...]
        r = lax.rsqrt(jnp.mean(mv_ * mv_, axis=-1, keepdims=True) + RMS_EPS)
        mn = ((mv_ * r) * g_ref[...]).astype(BF16)
        mn_ref[...] = mn
        mkv = jnp.dot(mn, w_ref[...], preferred_element_type=F32)
        mk_ref[...] = mkv[:, 0:C_W].astype(BF16)
        mv_ref[...] = mkv[:, C_W:2 * C_W].astype(BF16)

    return pl.pallas_call(
        body, name="memkv_fwd",
        out_shape=[jax.ShapeDtypeStruct((N_MEM, D_MODEL), BF16),
                   jax.ShapeDtypeStruct((N_MEM, C_W), BF16), jax.ShapeDtypeStruct((N_MEM, C_W), BF16)],
        compiler_params=_params(),
    )(mem, mem_g, w_mem_full)


def _memkv_bwd(mem, mem_g, mn, w_mem_full, dmk, dmv):
    def body(mem_ref, g_ref, mn_ref, w_ref, dmk_ref, dmv_ref, dw_ref, st_ref):
        dmkv = jnp.concatenate([dmk_ref[...], dmv_ref[...]], axis=1).astype(BF16)
        dw_ref[...] = lax.dot_general(mn_ref[...], dmkv, (((0,), (0,)), ((), ())), preferred_element_type=F32)
        dmn = lax.dot_general(dmkv, w_ref[...], (((1,), (1,)), ((), ())), preferred_element_type=F32)
        mv_ = mem_ref[...]
        r = lax.rsqrt(jnp.mean(mv_ * mv_, axis=-1, keepdims=True) + RMS_EPS)
        st_ref[...] = jnp.zeros_like(st_ref)
        st_ref[0:1, :] = jnp.sum(dmn * (mv_ * r), axis=0, keepdims=True)

    return pl.pallas_call(
        body, name="memkv_bwd",
        out_shape=[jax.ShapeDtypeStruct((D_MODEL, 2 * C_W), F32), jax.ShapeDtypeStruct((8, D_MODEL), F32)],
        compiler_params=_params(),
    )(mem, mem_g, mn, w_mem_full, dmk, dmv)


def _band_mask(has_prev, max_dist):
    qi = lax.broadcasted_iota(jnp.int32, (BLOCK, 2 * BLOCK), 0)
    kj = lax.broadcasted_iota(jnp.int32, (BLOCK, 2 * BLOCK), 1)
    dist = qi + BLOCK - kj
    return (dist >= 0) & (dist <= max_dist) & ((kj >= BLOCK) | has_prev)


_NT = (((1,), (1,)), ((), ()))
_TN = (((0,), (0,)), ((), ()))


def _head_only(val, h):
    slab = val[:, LANES * (h // 2):LANES * (h // 2 + 1)]
    lane = lax.broadcasted_iota(jnp.int32, slab.shape, 1)
    keep = (lane < HEAD_DIM) if h % 2 == 0 else (lane >= HEAD_DIM)
    return jnp.where(keep, slab, jnp.zeros((), slab.dtype))


class _KvSlabs:
    def __init__(self, cat, group):
        self.cat, self.group, self.swapped = cat, group, {}

    def is_swapped(self, h):
        return (h // self.group) % 2 != h % 2

    def __call__(self, h):
        j = (h // self.group) // 2
        slab = self.cat[:, LANES * j:LANES * (j + 1)]
        if not self.is_swapped(h):
            return slab
        if j not in self.swapped:
            self.swapped[j] = jnp.concatenate([slab[:, HEAD_DIM:], slab[:, :HEAD_DIM]], axis=1)
        return self.swapped[j]


class _BandSteps:
    def __init__(self, seq, dil, nq):
        self.nq, self.rows, self.consecutive = nq, nq * BLOCK, dil == 1
        nb = seq // dil // BLOCK
        if self.consecutive:
            assert nb % nq == 0
            self.outer, self.inner, self.stride = 1, nb // nq, 1
        else:
            assert dil % nq == 0
            self.outer, self.inner, self.stride = dil // nq, nb, dil // nq

    def own(self, w, clamp=False):
        cur = (lambda i: jnp.minimum(i, self.inner - 1)) if clamp else (lambda i: i)
        return pl.BlockSpec((self.rows, w), lambda r, i: (cur(i) * self.stride + r, 0))

    def prev(self, w, clamp=False):
        cur = (lambda i: jnp.minimum(i, self.inner - 1)) if clamp else (lambda i: i)
        if self.consecutive:
            return pl.BlockSpec((BLOCK, w), lambda r, i: (jnp.maximum(cur(i) * self.nq - 1, 0), 0))
        return pl.BlockSpec((self.rows, w), lambda r, i: (jnp.maximum(cur(i) - 1, 0) * self.stride + r, 0))

    def late(self, w):
        return pl.BlockSpec((self.rows, w), lambda r, i: (jnp.maximum(i - 1, 0) * self.stride + r, 0))

    def rows_of(self, j):
        return slice(BLOCK * j, BLOCK * (j + 1))

    def keys(self, p_ref, c_ref, j):
        if not self.consecutive:
            before = p_ref[self.rows_of(j), :]
        elif j == 0:
            before = p_ref[...]
        else:
            before = c_ref[self.rows_of(j - 1), :]
        return jnp.concatenate([before, c_ref[self.rows_of(j), :]], axis=0)

    def has_prev(self, i, j):
        return True if (self.consecutive and j > 0) else (i > 0)


def _banded_fwd(q, k, v, sink, *, dil, heads, group, max_dist, nq, name):
    seq = q.shape[0]
    kvh = heads // group
    qw, kw = heads * HEAD_DIM, kvh * HEAD_DIM
    steps = _BandSteps(seq, dil, nq)

    def body(*refs):
        if sink is not None:
            sink_ref, refs = refs[0], refs[1:]
        q_ref, kp_ref, kc_ref, vp_ref, vc_ref, o_ref, lse_ref, s_scr, p_scr = refs
        i = pl.program_id(1)
        lane = lax.broadcasted_iota(jnp.int32, (BLOCK, LANES), 1)
        k_of = [_KvSlabs(steps.keys(kp_ref, kc_ref, j), group) for j in range(nq)]
        v_of = [_KvSlabs(steps.keys(vp_ref, vc_ref, j), group) for j in range(nq)]
        for j in range(nq):
            qv = q_ref[steps.rows_of(j), :]
            for h in range(heads):
                s_scr[j * heads + h] = lax.dot_general(_head_only(qv, h), k_of[j](h), _NT, preferred_element_type=F32)
        ls = {}
        for j in range(nq):
            valid = _band_mask(steps.has_prev(i, j), max_dist)
            lse_tile = jnp.zeros((BLOCK, LANES), F32)
            for h in range(heads):
                s = jnp.where(valid, s_scr[j * heads + h], NEG)
                m = jnp.max(s, axis=-1, keepdims=True)
                if sink is not None:
                    sk = sink_ref[h]
                    m = jnp.maximum(m, sk)
                p = jnp.exp(s - m)
                l = jnp.sum(p, axis=-1, keepdims=True)
                if sink is not None:
                    l = l + jnp.exp(sk - m)
                p_scr[j * heads + h] = p.astype(BF16)
                ls[j, h] = l
                lse_tile = jnp.where(lane == h, m + jnp.log(l), lse_tile)
            lse_ref[steps.rows_of(j), :] = lse_tile
        for j in range(nq):
            for pr in range(heads // 2):
                he, ho = 2 * pr, 2 * pr + 1
                even = jnp.dot(p_scr[j * heads + he], v_of[j](he), preferred_element_type=F32) / ls[j, he]
                odd = jnp.dot(p_scr[j * heads + ho], v_of[j](ho), preferred_element_type=F32) / ls[j, ho]
                o_ref[steps.rows_of(j), LANES * pr:LANES * (pr + 1)] = jnp.where(lane < HEAD_DIM, even, odd).astype(BF16)

    in_specs = [steps.own(qw), steps.prev(kw), steps.own(kw), steps.prev(kw), steps.own(kw)]
    args = [q, k, k, v, v]
    if sink is not None:
        in_specs = [pl.BlockSpec(memory_space=pltpu.SMEM)] + in_specs
        args = [sink] + args
    return pl.pallas_call(
        body, name=name, grid=(steps.outer, steps.inner), in_specs=in_specs,
        out_specs=[steps.own(qw), steps.own(LANES)],
        out_shape=[jax.ShapeDtypeStruct((seq, qw), BF16), jax.ShapeDtypeStruct((seq, LANES), F32)],
        scratch_shapes=[pltpu.VMEM((nq * heads, BLOCK, 2 * BLOCK), F32), pltpu.VMEM((nq * heads, BLOCK, 2 * BLOCK), BF16)],
        compiler_params=_params(dimension_semantics=("arbitrary", "arbitrary")),
    )(*args)


def _banded_bwd(q, k, v, d_out, stat, sink, *, dil, heads, group, max_dist, nq, name, reduce_scatter=()):
    seq = q.shape[0]
    kvh = heads // group
    qw, kw = heads * HEAD_DIM, kvh * HEAD_DIM
    steps = _BandSteps(seq, dil, nq)
    n_rs = len(reduce_scatter)
    n_in = 7 + n_rs
    n_flat = steps.outer * (steps.inner + 1)

    def body(*refs):
        refs = list(refs)
        sink_ref = refs.pop(0) if sink is not None else None
        (q_ref, kp_ref, kc_ref, vp_ref, vc_ref, do_ref, st_ref), partials = refs[:7], refs[7:n_in]
        refs = refs[n_in:]
        dsink_ref = refs.pop(0) if sink is not None else None
        (dq_ref, dk_ref, dv_ref), sums = refs[:3], refs[3:3 + n_rs]
        kcar, vcar, s_scr, dp_scr, p_scr, ds_scr = refs[3 + n_rs:9 + n_rs]
        r, i = pl.program_id(0), pl.program_id(1)
        if n_rs:
            exchange = _ReduceScatter(tuple(partials), tuple(sums), refs[9 + n_rs:])
            flat = r * (steps.inner + 1) + i

            @pl.when(flat == 0)
            def _():
                exchange.start()

            @pl.when(flat == min(2, n_flat - 1))
            def _():
                exchange.send_chip_sums()

        @pl.when(i == 0)
        def _():
            kcar[...] = jnp.zeros_like(kcar)
            vcar[...] = jnp.zeros_like(vcar)

        if sink is not None:
            @pl.when((i == 0) & (r == 0))
            def _():
                dsink_ref[...] = jnp.zeros_like(dsink_ref)

        @pl.when(i < steps.inner)
        def _():
            lane = lax.broadcasted_iota(jnp.int32, (1, LANES), 1)
            lane_q = lax.broadcasted_iota(jnp.int32, (BLOCK, LANES), 1)
            k_of = [_KvSlabs(steps.keys(kp_ref, kc_ref, j), group) for j in range(nq)]
            v_of = [_KvSlabs(steps.keys(vp_ref, vc_ref, j), group) for j in range(nq)]
            qms, doms = {}, {}
            for j in range(nq):
                qv, dov = q_ref[steps.rows_of(j), :], do_ref[steps.rows_of(j), :]
                for h in range(heads):
                    qms[j, h], doms[j, h] = _head_only(qv, h), _head_only(dov, h)
                    s_scr[j * heads + h] = lax.dot_general(qms[j, h], k_of[j](h), _NT, preferred_element_type=F32)
                    dp_scr[j * heads + h] = lax.dot_general(doms[j, h], v_of[j](h), _NT, preferred_element_type=F32)
            dsink_row = jnp.zeros((1, LANES), F32)
            for j in range(nq):
                st = st_ref[steps.rows_of(j), :]
                valid = _band_mask(steps.has_prev(i, j), max_dist)
                for h in range(heads):
                    lse_h = st[:, h:h + 1]
                    delta = st[:, DELTA_LANE + h:DELTA_LANE + h + 1]
                    p = jnp.where(valid, jnp.exp(s_scr[j * heads + h] - lse_h), 0.0)
                    p_scr[j * heads + h] = p.astype(BF16)
                    ds_scr[j * heads + h] = (p * (dp_scr[j * heads + h] - delta)).astype(BF16)
                    if sink is not None:
                        ds_sink = jnp.sum(-jnp.exp(sink_ref[h] - lse_h) * delta, axis=0, keepdims=True)
                        dsink_row = dsink_row + jnp.where(lane == h, ds_sink, 0.0)
            for j in range(nq):
                for pr in range(heads // 2):
                    he, ho = 2 * pr, 2 * pr + 1
                    even = jnp.dot(ds_scr[j * heads + he], k_of[j](he), preferred_element_type=F32)
                    odd = jnp.dot(ds_scr[j * heads + ho], k_of[j](ho), preferred_element_type=F32)
                    dq_ref[steps.rows_of(j), LANES * pr:LANES * (pr + 1)] = (
                        jnp.where(lane_q < HEAD_DIM, even, odd).astype(BF16))
            if steps.consecutive:
                dk_ref[...] = kcar[...].astype(BF16)
                dv_ref[...] = vcar[...].astype(BF16)
            for j in range(nq):
                for slab in range(kw // LANES):
                    acc = {}
                    for h in range(heads):
                        if (h // group) // 2 != slab:
                            continue
                        key = k_of[j].is_swapped(h)
                        dk_h = lax.dot_general(ds_scr[j * heads + h], qms[j, h], _TN, preferred_element_type=F32)
                        dv_h = lax.dot_general(p_scr[j * heads + h], doms[j, h], _TN, preferred_element_type=F32)
                        acc[key] = (dk_h, dv_h) if key not in acc else (acc[key][0] + dk_h, acc[key][1] + dv_h)
                    dk_j, dv_j = acc.get(False, (None, None))
                    if True in acc:
                        unswap = lambda t: jnp.concatenate([t[:, HEAD_DIM:], t[:, :HEAD_DIM]], axis=1)
                        dk_s, dv_s = unswap(acc[True][0]), unswap(acc[True][1])
                        dk_j = dk_s if dk_j is None else dk_j + dk_s
                        dv_j = dv_s if dv_j is None else dv_j + dv_s
                    sl = slice(LANES * slab, LANES * (slab + 1))
                    own_rows = steps.rows_of(j)
                    if not steps.consecutive:
                        dk_ref[own_rows, sl] = (kcar[own_rows, sl] + dk_j[0:BLOCK]).astype(BF16)
                        dv_ref[own_rows, sl] = (vcar[own_rows, sl] + dv_j[0:BLOCK]).astype(BF16)
                    elif j == 0:
                        last = steps.rows_of(nq - 1)
                        dk_ref[last, sl] = (kcar[last, sl] + dk_j[0:BLOCK]).astype(BF16)
                        dv_ref[last, sl] = (vcar[last, sl] + dv_j[0:BLOCK]).astype(BF16)
                    else:
                        before = steps.rows_of(j - 1)
                        kcar[before, sl] += dk_j[0:BLOCK]
                        vcar[before, sl] += dv_j[0:BLOCK]
                    kcar[own_rows, sl] = dk_j[BLOCK:2 * BLOCK]
                    vcar[own_rows, sl] = dv_j[BLOCK:2 * BLOCK]
            if sink is not None:
                dsink_ref[0:1, :] += dsink_row

        @pl.when(i == steps.inner)
        def _():
            dk_ref[...] = kcar[...].astype(BF16)
            dv_ref[...] = vcar[...].astype(BF16)

        if n_rs:
            @pl.when(flat == n_flat - 1)
            def _():
                exchange.finish()

    own, prev = (lambda w: steps.own(w, clamp=True)), (lambda w: steps.prev(w, clamp=True))
    rs_shapes = [t.shape[1:] for t in reduce_scatter]
    in_specs = ([own(qw), prev(kw), own(kw), prev(kw), own(kw), own(qw), own(LANES)]
                + [pl.BlockSpec(memory_space=pl.ANY)] * n_rs)
    args = [q, k, k, v, v, d_out, stat, *reduce_scatter]
    out_specs = [own(qw), steps.late(kw), steps.late(kw)] + [_full(s) for s in rs_shapes]
    out_shape = [jax.ShapeDtypeStruct((seq, qw), BF16), jax.ShapeDtypeStruct((seq, kw), BF16),
                 jax.ShapeDtypeStruct((seq, kw), BF16)] + [jax.ShapeDtypeStruct(s, F32) for s in rs_shapes]
    if sink is not None:
        in_specs = [pl.BlockSpec(memory_space=pltpu.SMEM)] + in_specs
        args = [sink] + args
        out_specs = [_full((8, LANES))] + out_specs
        out_shape = [jax.ShapeDtypeStruct((8, LANES), F32)] + out_shape
    n_hb = nq * heads
    res = pl.pallas_call(
        body, name=name, grid=(steps.outer, steps.inner + 1), in_specs=in_specs, out_specs=out_specs,
        out_shape=out_shape,
        scratch_shapes=[pltpu.VMEM((steps.rows, kw), F32), pltpu.VMEM((steps.rows, kw), F32)]
        + [pltpu.VMEM((n_hb, BLOCK, 2 * BLOCK), F32)] * 2 + [pltpu.VMEM((n_hb, BLOCK, 2 * BLOCK), BF16)] * 2
        + (_ReduceScatter.scratch_shapes(rs_shapes) if n_rs else []),
        compiler_params=_params(dimension_semantics=("arbitrary", "arbitrary")),
    )(*args)
    if sink is not None:
        return (*res[1:4], res[0], *res[4:])
    return res


def _cross_fwd(q, mk, mv, tq=1024):
    seq = q.shape[0]

    def body(q_ref, mk_ref, mv_ref, o_ref, lse_ref, s_scr, p_scr):
        qv = q_ref[...]
        k_of, v_of = _KvSlabs(mk_ref[...], 1), _KvSlabs(mv_ref[...], 1)
        lane = lax.broadcasted_iota(jnp.int32, (tq, LANES), 1)
        lse_tile = jnp.zeros((tq, LANES), F32)
        for h in range(C_HEADS):
            s_scr[h] = lax.dot_general(_head_only(qv, h), k_of(h), _NT, preferred_element_type=F32)
        ls = []
        for h in range(C_HEADS):
            s = s_scr[h]
            m = jnp.max(s, axis=-1, keepdims=True)
            p = jnp.exp(s - m)
            l = jnp.sum(p, axis=-1, keepdims=True)
            p_scr[h] = p.astype(BF16)
            ls.append(l)
            lse_tile = jnp.where(lane == h, m + jnp.log(l), lse_tile)
        for pr in range(C_HEADS // 2):
            even = jnp.dot(p_scr[2 * pr], v_of(2 * pr), preferred_element_type=F32) / ls[2 * pr]
            odd = jnp.dot(p_scr[2 * pr + 1], v_of(2 * pr + 1), preferred_element_type=F32) / ls[2 * pr + 1]
            o_ref[:, LANES * pr:LANES * (pr + 1)] = jnp.where(lane < HEAD_DIM, even, odd).astype(BF16)
        lse_ref[...] = lse_tile

    return pl.pallas_call(
        body, name="cross_fwd", grid=(seq // tq,),
        in_specs=[_row(tq, C_W), _full((N_MEM, C_W)), _full((N_MEM, C_W))],
        out_specs=[_row(tq, C_W), _row(tq, LANES)],
        out_shape=[jax.ShapeDtypeStruct((seq, C_W), BF16), jax.ShapeDtypeStruct((seq, LANES), F32)],
        scratch_shapes=[pltpu.VMEM((C_HEADS, tq, N_MEM), F32), pltpu.VMEM((C_HEADS, tq, N_MEM), BF16)],
        compiler_params=_params(dimension_semantics=("arbitrary",)),
    )(q, mk, mv)


def _cross_bwd(q, mk, mv, d_out, stat, tq=1024):
    seq = q.shape[0]

    def body(q_ref, mk_ref, mv_ref, do_ref, st_ref, dq_ref, dmk_ref, dmv_ref, s_scr, dp_scr, p_scr, ds_scr):
        @pl.when(pl.program_id(0) == 0)
        def _():
            dmk_ref[...] = jnp.zeros_like(dmk_ref)
            dmv_ref[...] = jnp.zeros_like(dmv_ref)

        qv, dov, st = q_ref[...], do_ref[...], st_ref[...]
        k_of, v_of = _KvSlabs(mk_ref[...], 1), _KvSlabs(mv_ref[...], 1)
        qms = [_head_only(qv, h) for h in range(C_HEADS)]
        doms = [_head_only(dov, h) for h in range(C_HEADS)]
        for h in range(C_HEADS):
            s_scr[h] = lax.dot_general(qms[h], k_of(h), _NT, preferred_element_type=F32)
            dp_scr[h] = lax.dot_general(doms[h], v_of(h), _NT, preferred_element_type=F32)
        for h in range(C_HEADS):
            p = jnp.exp(s_scr[h] - st[:, h:h + 1])
            p_scr[h] = p.astype(BF16)
            ds_scr[h] = (p * (dp_scr[h] - st[:, DELTA_LANE + h:DELTA_LANE + h + 1])).astype(BF16)
        lane = lax.broadcasted_iota(jnp.int32, (tq, LANES), 1)
        for pr in range(C_HEADS // 2):
            sl = slice(LANES * pr, LANES * (pr + 1))
            even = jnp.dot(ds_scr[2 * pr], k_of(2 * pr), preferred_element_type=F32)
            odd = jnp.dot(ds_scr[2 * pr + 1], k_of(2 * pr + 1), preferred_element_type=F32)
            dq_ref[:, sl] = jnp.where(lane < HEAD_DIM, even, odd).astype(BF16)
            dmk_ref[:, sl] += (lax.dot_general(ds_scr[2 * pr], qms[2 * pr], _TN, preferred_element_type=F32)
                               + lax.dot_general(ds_scr[2 * pr + 1], qms[2 * pr + 1], _TN, preferred_element_type=F32))
            dmv_ref[:, sl] += (lax.dot_general(p_scr[2 * pr], doms[2 * pr], _TN, preferred_element_type=F32)
                               + lax.dot_general(p_scr[2 * pr + 1], doms[2 * pr + 1], _TN, preferred_element_type=F32))

    return pl.pallas_call(
        body, name="cross_bwd", grid=(seq // tq,),
        in_specs=[_row(tq, C_W), _full((N_MEM, C_W)), _full((N_MEM, C_W)), _row(tq, C_W), _row(tq, LANES)],
        out_specs=[_row(tq, C_W), _full((N_MEM, C_W)), _full((N_MEM, C_W))],
        out_shape=[jax.ShapeDtypeStruct((seq, C_W), BF16), jax.ShapeDtypeStruct((N_MEM, C_W), F32),
                   jax.ShapeDtypeStruct((N_MEM, C_W), F32)],
        scratch_shapes=[pltpu.VMEM((C_HEADS, tq, N_MEM), F32)] * 2 + [pltpu.VMEM((C_HEADS, tq, N_MEM), BF16)] * 2,
        compiler_params=_params(dimension_semantics=("arbitrary",)),
    )(q, mk, mv, d_out, stat)


def _per_head(tile, width):
    rows = tile.shape[0]
    lane = lax.broadcasted_iota(jnp.int32, (rows, LANES), 1)
    slabs = []
    for p in range(width // LANES):
        even = jnp.broadcast_to(tile[:, 2 * p:2 * p + 1], (rows, LANES))
        odd = jnp.broadcast_to(tile[:, 2 * p + 1:2 * p + 2], (rows, LANES))
        slabs.append(jnp.where(lane < HEAD_DIM, even, odd))
    return slabs[0] if len(slabs) == 1 else jnp.concatenate(slabs, axis=1)


def _with_delta(lse_tile, prod):
    rows = lse_tile.shape[0]
    lane = lax.broadcasted_iota(jnp.int32, (rows, LANES), 1)
    tile = lse_tile
    for p in range(prod.shape[1] // LANES):
        slab = prod[:, LANES * p:LANES * (p + 1)]
        even = jnp.sum(jnp.where(lane < HEAD_DIM, slab, 0.0), axis=-1, keepdims=True)
        odd = jnp.sum(jnp.where(lane >= HEAD_DIM, slab, 0.0), axis=-1, keepdims=True)
        tile = jnp.where(lane == DELTA_LANE + 2 * p, even, tile)
        tile = jnp.where(lane == DELTA_LANE + 2 * p + 1, odd, tile)
    return tile


def _mid(oa, lse_a, ob, lse_b, oc, lse_c, gate, x, target, w_out_full, post_g, tm=512):
    seq = x.shape[0]
    n_b = B_W // LANES

    def body(oa_ref, la_ref, b1_ref, l1_ref, b4_ref, l4_ref, b16_ref, l16_ref, oc_ref, lc_ref,
             gate_ref, x_ref, t_ref, w_ref, pg_ref,
             dh_ref, dg_ref, doa_ref, sa_ref, dob1_ref, sb1_ref, dob4_ref, sb4_ref, dob16_ref, sb16_ref,
             doc_ref, sc_ref, dw_ref, st_ref, scr_b4, scr_b16, scr_l4, scr_l16, scr_do, scr_sb):
        @pl.when(pl.program_id(0) == 0)
        def _():
            dw_ref[...] = jnp.zeros_like(dw_ref)
            st_ref[...] = jnp.zeros_like(st_ref)

        b1, l1 = b1_ref[...].astype(F32), l1_ref[...]
        b4, l4 = _load_permuted(b4_ref, scr_b4, 4), _load_permuted(l4_ref, scr_l4, 4)
        b16, l16 = _load_permuted(b16_ref, scr_b16, 16), _load_permuted(l16_ref, scr_l16, 16)
        lm = jnp.maximum(jnp.maximum(l1, l4), l16)
        e1, e4, e16 = jnp.exp(l1 - lm), jnp.exp(l4 - lm), jnp.exp(l16 - lm)
        den = e1 + e4 + e16
        lse_b_tile = lm + jnp.log(den)
        ob_v = _per_head(e1 / den, B_W) * b1 + _per_head(e4 / den, B_W) * b4 + _per_head(e16 / den, B_W) * b16
        o_all = jnp.concatenate([oa_ref[...].astype(F32), ob_v, oc_ref[...].astype(F32)], axis=1)
        g = gate_ref[...].astype(F32)
        sig = 1.0 / (1.0 + jnp.exp(-g))
        silu = g * sig
        y = (o_all * silu).astype(BF16)
        w = w_ref[...]
        z = jnp.dot(y, w, preferred_element_type=F32)
        rz = lax.rsqrt(jnp.mean(z * z, axis=-1, keepdims=True) + RMS_EPS)
        hn = z * rz
        pg = pg_ref[...]
        err = (x_ref[...] + hn * pg) - t_ref[...]
        loss = 0.5 * jnp.sum(jnp.mean(err * err, axis=-1, keepdims=True), axis=0, keepdims=True)
        dh = err * (1.0 / D_MODEL)
        dh_ref[...] = dh.astype(BF16)
        st_ref[0:1, :] += jnp.sum(dh * hn, axis=0, keepdims=True)
        st_ref[1:2, :] += jnp.broadcast_to(loss, (1, D_MODEL))
        dhn = dh * pg
        dz = (rz * (dhn - hn * jnp.mean(dhn * hn, axis=-1, keepdims=True))).astype(BF16)
        dy = lax.dot_general(dz, w, _NT, preferred_element_type=F32)
        dw_ref[...] += lax.dot_general(y, dz, _TN, preferred_element_type=F32)
        dg_ref[...] = (dy * o_all * (sig * (1.0 + g * (1.0 - sig)))).astype(BF16)
        d_o = (dy * silu).astype(BF16)
        prod = d_o.astype(F32) * o_all
        doa_ref[...] = d_o[:, 0:A_W]
        sa_ref[...] = _with_delta(la_ref[...], prod[:, 0:A_W])
        doc_ref[...] = d_o[:, A_W + B_W:D_MODEL]
        sc_ref[...] = _with_delta(lc_ref[...], prod[:, A_W + B_W:D_MODEL])
        d_ob = d_o[:, A_W:A_W + B_W]
        stat_b = _with_delta(lse_b_tile, prod[:, A_W:A_W + B_W])
        dob1_ref[...] = d_ob
        sb1_ref[...] = stat_b
        _put(scr_do, d_ob.astype(F32))
        _put(scr_sb, stat_b)
        _store_permuted(scr_do, dob4_ref, 4, BF16)
        _store_permuted(scr_sb, sb4_ref, 4, F32)
        _store_permuted(scr_do, dob16_ref, 16, BF16)
        _store_permuted(scr_sb, sb16_ref, 16, F32)

    p4 = lambda w: _perm_spec(tm, 4, w)
    p16 = lambda w: _perm_spec(tm, 16, w)
    in_specs = [_row(tm, A_W), _row(tm, LANES), _row(tm, B_W), _row(tm, LANES), p4(B_W), p4(LANES), p16(B_W), p16(LANES),
                _row(tm, C_W), _row(tm, LANES), _row(tm, D_MODEL), _row(tm, D_MODEL), _row(tm, D_MODEL),
                _full((D_MODEL, D_MODEL)), _full((1, D_MODEL))]
    sds = jax.ShapeDtypeStruct
    v4 = lambda w, dt: sds((seq // (BLOCK * 4), 4, BLOCK, w), dt)
    v16 = lambda w, dt: sds((seq // (BLOCK * 16), 16, BLOCK, w), dt)
    out_specs = [_row(tm, D_MODEL), _row(tm, D_MODEL), _row(tm, A_W), _row(tm, LANES), _row(tm, B_W), _row(tm, LANES),
                 p4(B_W), p4(LANES), p16(B_W), p16(LANES), _row(tm, C_W), _row(tm, LANES),
                 _full((D_MODEL, D_MODEL)), _full((8, D_MODEL))]
    out_shape = [sds((seq, D_MODEL), BF16), sds((seq, D_MODEL), BF16), sds((seq, A_W), BF16), sds((seq, LANES), F32),
                 sds((seq, B_W), BF16), sds((seq, LANES), F32), v4(B_W, BF16), v4(LANES, F32), v16(B_W, BF16),
                 v16(LANES, F32), sds((seq, C_W), BF16), sds((seq, LANES), F32),
                 sds((D_MODEL, D_MODEL), F32), sds((8, D_MODEL), F32)]
    res = pl.pallas_call(
        body, name="mid", grid=(seq // tm,), in_specs=in_specs, out_specs=out_specs, out_shape=out_shape,
        scratch_shapes=[pltpu.VMEM((n_b, tm, LANES), F32), pltpu.VMEM((n_b, tm, LANES), F32),
                        pltpu.VMEM((1, tm, LANES), F32), pltpu.VMEM((1, tm, LANES), F32),
                        pltpu.VMEM((n_b, tm, LANES), F32), pltpu.VMEM((1, tm, LANES), F32)],
        compiler_params=_params(dimension_semantics=("arbitrary",)),
    )(oa, lse_a, ob[1], lse_b[1], _perm_view(ob[4], 4), _perm_view(lse_b[4], 4), _perm_view(ob[16], 16),
      _perm_view(lse_b[16], 16), oc, lse_c, gate, x, target, w_out_full, post_g)
    dh, d_gate, do_a, st_a, do_b1, st_b1, do_b4, st_b4, do_b16, st_b16, do_c, st_c, d_wout, stats = res
    flat = lambda t: t.reshape(seq, t.shape[-1])
    d_b = {1: (do_b1, st_b1), 4: (flat(do_b4), flat(st_b4)), 16: (flat(do_b16), flat(st_b16))}
    return dh, d_gate, (do_a, st_a), d_b, (do_c, st_c), d_wout, stats


def _inproj_bwd(x, u, dh, pre_g, w_in_full, tabs, dqa, dka, dva, dqkv_b, dqc, dgate, tm=512):
    seq = x.shape[0]
    n_b = B_W // LANES

    def body(x_ref, u_ref, dh_ref, g_ref, w_hbm, c_ref, up_ref, dn_ref, dqa_ref, dka_ref, dva_ref,
             dq1, dk1, dv1, dq4, dk4, dv4, dq16, dk16, dv16, dqc_ref, dg_ref,
             gx_ref, dw_ref, st_ref, scr4, scr16, w_scr, w_sems, dp_ref):
        _stage_w_in(w_hbm, w_scr, w_sems)

        @pl.when(pl.program_id(0) == 0)
        def _():
            st_ref[...] = jnp.zeros_like(st_ref)
            dw_ref[...] = jnp.zeros_like(dw_ref)

        c, up, dn = c_ref[...], -up_ref[...], -dn_ref[...]
        unrot = lambda t: _rotate(t, c, up, dn)
        total = lambda r1, r4, r16: (r1[...].astype(F32) + _load_permuted(r4, scr4, 4)
                                     + _load_permuted(r16, scr16, 16))
        at = lambda piece: slice(*COLS[piece])
        dp_ref[:, at("qa")] = (unrot(dqa_ref[...].astype(F32)) * SCALE).astype(BF16)
        dp_ref[:, at("ka")] = unrot(dka_ref[...].astype(F32)).astype(BF16)
        dp_ref[:, at("va")] = dva_ref[...]
        dp_ref[:, at("ga")] = dg_ref[:, 0:A_W]
        dp_ref[:, at("qb")] = (unrot(total(dq1, dq4, dq16)) * SCALE).astype(BF16)
        dp_ref[:, at("kb")] = unrot(total(dk1, dk4, dk16)).astype(BF16)
        dp_ref[:, at("vb")] = total(dv1, dv4, dv16).astype(BF16)
        dp_ref[:, at("gb")] = dg_ref[:, A_W:A_W + B_W]
        dp_ref[:, at("qc")] = (dqc_ref[...].astype(F32) * SCALE).astype(BF16)
        dp_ref[:, at("gc")] = dg_ref[:, A_W + B_W:D_MODEL]
        du = lax.dot_general(dp_ref[...], w_scr[...], _NT, preferred_element_type=F32)
        res = lax.dot_general(u_ref[...], dp_ref[...], _TN, preferred_element_type=F32)
        for k in range(N_DEV):
            dw_ref[k] += res[:, SHARD_IN * k:SHARD_IN * (k + 1)]
        xv = x_ref[...]
        r = lax.rsqrt(jnp.mean(xv * xv, axis=-1, keepdims=True) + RMS_EPS)
        xh = xv * r
        st_ref[0:1, :] += jnp.sum(du * xh, axis=0, keepdims=True)
        dxh = du * g_ref[...]
        gx_ref[...] = dh_ref[...].astype(F32) + r * (dxh - xh * jnp.mean(dxh * xh, axis=-1, keepdims=True))

    in_specs = ([_row(tm, D_MODEL), _row(tm, D_MODEL), _row(tm, D_MODEL), _full((1, D_MODEL)),
                 pl.BlockSpec(memory_space=pl.ANY),
                 _row(tm, LANES), _row(tm, LANES), _row(tm, LANES), _row(tm, A_W), _row(tm, A_KV_W), _row(tm, A_KV_W)]
                + [_row(tm, B_W)] * 3 + [_perm_spec(tm, 4, B_W)] * 3 + [_perm_spec(tm, 16, B_W)] * 3
                + [_row(tm, C_W), _row(tm, D_MODEL)])
    dw_spec = pl.BlockSpec((N_DEV, D_MODEL, SHARD_IN), lambda i: (0, 0, 0), pipeline_mode=pl.Buffered(1))
    return pl.pallas_call(
        body, name="inproj_bwd", grid=(seq // tm,), in_specs=in_specs,
        out_specs=[_row(tm, D_MODEL), dw_spec, _full((8, D_MODEL))],
        out_shape=[jax.ShapeDtypeStruct((seq, D_MODEL), F32), jax.ShapeDtypeStruct((N_DEV, D_MODEL, SHARD_IN), F32),
                   jax.ShapeDtypeStruct((8, D_MODEL), F32)],
        scratch_shapes=[pltpu.VMEM((n_b, tm, LANES), F32), pltpu.VMEM((n_b, tm, LANES), F32)] + _w_in_scratch()
        + [pltpu.VMEM((tm, D_IN), BF16)],
        compiler_params=_params(dimension_semantics=("arbitrary",)),
    )(x, u, dh, pre_g, w_in_full, *tabs, dqa, dka, dva, *dqkv_b[1], *[_perm_view(t, 4) for t in dqkv_b[4]],
      *[_perm_view(t, 16) for t in dqkv_b[16]], dqc, dgate)


class _ReduceScatter:
    def __init__(self, ins, outs, scratch):
        self.n = n = len(ins)
        self.ins, self.outs = ins, outs
        self.mine, self.got, self.snd, self.rcv = (scratch[n * t:n * (t + 1)] for t in range(4))
        self.load_sems, self.d2d_send, self.d2d_recv, self.ici_send, self.ici_recv = scratch[4 * n:]
        self.pos = _mesh_pos()
        self.pairs = [(a, kk) for kk in (3, 1, 2) for a in range(n)]

    @staticmethod
    def scratch_shapes(shapes):
        return ([pltpu.VMEM((4,) + s, F32) for s in shapes] + [pltpu.VMEM((4,) + s, F32) for s in shapes]
                + [pltpu.VMEM((3,) + s, BF16) for s in shapes] + [pltpu.VMEM((3,) + s, BF16) for s in shapes]
                + [pltpu.SemaphoreType.DMA((len(shapes), 4))] * 5)

    def _chip(self, kk):
        x, y, _ = self.pos
        return (1 - x if kk & 2 else x, 1 - y if kk & 1 else y)

    def _load(self, a, kk):
        block = _dev_index((*self._chip(kk), self.pos[2]))
        return pltpu.make_async_copy(self.ins[a].at[block], self.mine[a].at[kk], self.load_sems.at[a, kk])

    def _swap(self, a, kk):
        x, y, c = self.pos
        return pltpu.make_async_remote_copy(
            src_ref=self.ins[a].at[_dev_index((*self._chip(kk), 1 - c))], dst_ref=self.got[a].at[kk],
            send_sem=self.d2d_send.at[a, kk], recv_sem=self.d2d_recv.at[a, kk],
            device_id=(x, y, 1 - c), device_id_type=MESH_ID)

    def _hop(self, a, kk):
        return pltpu.make_async_remote_copy(
            src_ref=self.snd[a].at[kk - 1], dst_ref=self.rcv[a].at[kk - 1], send_sem=self.ici_send.at[a, kk],
            recv_sem=self.ici_recv.at[a, kk], device_id=(*self._chip(kk), self.pos[2]), device_id_type=MESH_ID)

    def start(self):
        for kk in (3, 1, 2, 0):
            for a in range(self.n):
                self._load(a, kk).start()
                self._swap(a, kk).start()

    def send_chip_sums(self):
        for a, kk in self.pairs:
            self._load(a, kk).wait()
            self._swap(a, kk).wait_recv()
            self.snd[a][kk - 1] = (self.mine[a][kk] + self.got[a][kk]).astype(BF16)
            self._hop(a, kk).start()

    def finish(self):
        for a in range(self.n):
            self._load(a, 0).wait()
            self._swap(a, 0).wait_recv()
            acc = self.mine[a][0] + self.got[a][0]
            for kk in (1, 2, 3):
                self._hop(a, kk).wait_recv()
                acc = acc + self.rcv[a][kk - 1].astype(F32)
            self.outs[a][...] = acc
        for kk in range(4):
            for a in range(self.n):
                self._swap(a, kk).wait_send()
        for a, kk in self.pairs:
            self._hop(a, kk).wait_send()


def _local_step(x, mem, pre_g, w_in, sink, mem_g, w_mem, w_out, post_g, target):
    u, *tabs, w_in_full = _prep(x, pre_g, w_in)
    qa, ka, va, qkv_b, qc, gate, w_mem_all, w_out_all = _inproj(u, w_in_full, tabs, w_mem, w_out)
    w_mem_full = w_mem_all.reshape(D_MODEL, 2 * C_W)
    w_out_full = w_out_all.reshape(D_MODEL, D_MODEL)
    mn, mk, mv = _memkv_fwd(mem, mem_g, w_mem_full)

    a_cfg = dict(dil=1, heads=A_HEADS, group=A_GROUP, max_dist=BLOCK - 1, nq=ATTN_BLOCKS_PER_STEP)
    b_cfgs = {dil: dict(dil=dil, heads=B_HEADS, group=1, max_dist=win // dil, nq=ATTN_BLOCKS_PER_STEP)
              for win, dil in B_CONFIGS}
    oa, lse_a = _banded_fwd(qa, ka, va, sink, name="attn_a_fwd", **a_cfg)
    ob, lse_b = {}, {}
    for dil, cfg in b_cfgs.items():
        ob[dil], lse_b[dil] = _banded_fwd(*qkv_b[dil], None, name=f"attn_b{dil}_fwd", **cfg)
    oc, lse_c = _cross_fwd(qc, mk, mv)

    dh, d_gate, d_a, d_b, d_c, d_wout, st_mid = _mid(oa, lse_a, ob, lse_b, oc, lse_c, gate, x, target, w_out_full, post_g)

    dqc, dmk, dmv = _cross_bwd(qc, mk, mv, *d_c)
    d_wmem, st_mem = _memkv_bwd(mem, mem_g, mn, w_mem_full, dmk, dmv)
    dqkv_b = {dil: _banded_bwd(*qkv_b[dil], *d_b[dil], None, name=f"attn_b{dil}_bwd", **cfg)
              for dil, cfg in b_cfgs.items()}
    dqa, dka, dva, dsink, g_wmem, g_wout = _banded_bwd(
        qa, ka, va, *d_a, sink, name="attn_a_bwd", **a_cfg,
        reduce_scatter=(d_wmem.reshape(N_DEV, SHARD_ROWS, 2 * C_W), d_wout.reshape(N_DEV, SHARD_ROWS, D_MODEL)))

    grad_x, d_win, st_pre = _inproj_bwd(x, u, dh, pre_g, w_in_full, tabs, dqa, dka, dva, dqkv_b, dqc, d_gate)

    dsink_row = jnp.pad(dsink[0:1, :], ((0, 0), (0, D_MODEL - LANES)))
    stats = jnp.concatenate([st_pre[0:1], st_mem[0:1], st_mid[0:1], dsink_row, st_mid[1:2],
                             jnp.zeros((3, D_MODEL), F32)], axis=0)
    return grad_x, d_win, g_wmem, g_wout, stats


def _prep(x, pre_g, w_in, tm=1024):
    seq = x.shape[0]
    n_steps = seq // tm
    parts = 2
    rows = D_MODEL // parts
    relay_at = min(3, n_steps - 1)
    j = jnp.arange(LANES) % HEAD_DIM
    freq = (ROPE_THETA ** (-(2 * (j % (ROT_DIM // 2))).astype(F32) / ROT_DIM))[None, :]
    SIB, NB_X, NB_Y, RELAY, FWD = 0, 1, 2, 3, 4

    def body(x_ref, g_ref, f_ref, win_ref, u_ref, c_ref, up_ref, dn_ref, win_out, win_b,
             send_sems, recv_sems, local_sems):
        step = pl.program_id(0)
        px, py, pc = _mesh_pos()
        me, sibling = (px, py, pc), (px, py, 1 - pc)
        others = lambda core: ((1 - px, py, core), (px, 1 - py, core), (1 - px, 1 - py, core))
        x_nb, y_nb, diag = others(pc)
        relay_from = [x_nb, y_nb]
        relay_to = [y_nb, x_nb]

        def src(a):
            return win_b.at[pl.ds(rows * a, rows)]

        def slot(a, p):
            return win_out.at[_dev_index(p), pl.ds(rows * a, rows)]

        def copy(a, k, block, to, own=False):
            return pltpu.make_async_remote_copy(
                src_ref=src(a) if own else slot(a, block), dst_ref=slot(a, block),
                send_sem=send_sems.at[a, k], recv_sem=recv_sems.at[a, k], device_id=to, device_id_type=MESH_ID)

        def first_sends():
            return [copy(0, NB_X, me, x_nb, own=True), copy(1, NB_Y, me, y_nb, own=True),
                    copy(1, NB_X, me, x_nb, own=True), copy(0, NB_Y, me, y_nb, own=True),
                    copy(0, SIB, me, sibling, own=True), copy(1, SIB, me, sibling, own=True)]

        def relay(a):
            return copy(a, RELAY, relay_from[a], relay_to[a])

        def to_sibling(a, which):
            return copy(a, FWD + which, others(pc)[which], sibling)

        def local(a):
            return pltpu.make_async_copy(src(a), slot(a, me), local_sems.at[a])

        @pl.when(step == 0)
        def _():
            win_b[...] = win_ref[...].astype(BF16)
            for a in range(parts):
                local(a).start()
            for cp in first_sends():
                cp.start()

        @pl.when(step == relay_at)
        def _():
            for a in range(parts):
                copy(a, NB_X + a, relay_from[a], me).wait_recv()
                relay(a).start()
                to_sibling(a, a).start()

        xv = x_ref[...]
        r = lax.rsqrt(jnp.mean(xv * xv, axis=-1, keepdims=True) + RMS_EPS)
        u_ref[...] = ((xv * r) * g_ref[...]).astype(BF16)
        pos = (lax.broadcasted_iota(jnp.int32, (tm, LANES), 0) + step * tm).astype(F32)
        head_lane = lax.broadcasted_iota(jnp.int32, (tm, LANES), 1) % HEAD_DIM
        ang = pos * f_ref[...]
        cos, sin = jnp.cos(ang), jnp.sin(ang)
        half = ROT_DIM // 2
        c_ref[...] = jnp.where(head_lane < ROT_DIM, cos, 1.0)
        up_ref[...] = jnp.where((head_lane >= half) & (head_lane < ROT_DIM), sin, 0.0)
        dn_ref[...] = jnp.where(head_lane < half, -sin, 0.0)

        @pl.when(step == n_steps - 1)
        def _():
            copy(1, NB_X, x_nb, me).wait_recv()
            to_sibling(1, 0).start()
            copy(0, NB_Y, y_nb, me).wait_recv()
            to_sibling(0, 1).start()
            for a in range(parts):
                copy(a, RELAY, diag, me).wait_recv()
                to_sibling(a, 2).start()
            for a in range(parts):
                copy(a, SIB, sibling, me).wait_recv()
                for which in range(3):
                    copy(a, FWD + which, others(1 - pc)[which], me).wait_recv()
            for cp in first_sends():
                cp.wait_send()
            for a in range(parts):
                relay(a).wait_send()
                for which in range(3):
                    to_sibling(a, which).wait_send()
                local(a).wait()

    return pl.pallas_call(
        body, name="prep", grid=(n_steps,),
        in_specs=[_row(tm, D_MODEL), _full((1, D_MODEL)), _full((1, LANES)), _full(w_in.shape)],
        out_specs=[_row(tm, D_MODEL), _row(tm, LANES), _row(tm, LANES), _row(tm, LANES),
                   pl.BlockSpec(memory_space=pl.ANY)],
        out_shape=[jax.ShapeDtypeStruct((seq, D_MODEL), BF16)] + [jax.ShapeDtypeStruct((seq, LANES), F32)] * 3
        + [jax.ShapeDtypeStruct((N_DEV,) + w_in.shape, BF16)],
        scratch_shapes=[pltpu.VMEM(w_in.shape, BF16), pltpu.SemaphoreType.DMA((parts, FWD + 3)),
                        pltpu.SemaphoreType.DMA((parts, FWD + 3)), pltpu.SemaphoreType.DMA((parts,))],
        compiler_params=_params(dimension_semantics=("arbitrary",)),
    )(x, pre_g, freq, w_in)


def _exchange_grads(d_win, stats):
    def body(win, st, g_win, r_st, send_sems, recv_sems, local_sem, *scratch):
        exchange = _ReduceScatter((win,), (g_win,), scratch)
        exchange.start()
        pos = _mesh_pos()
        me = _dev_index(pos)
        own = pltpu.make_async_copy(st, r_st.at[me], local_sem)
        own.start()
        copies = []
        for s in range(1, N_DEV):
            peer = _xor_peer(pos, s)
            mk = lambda slot: pltpu.make_async_remote_copy(
                src_ref=st, dst_ref=r_st.at[slot], send_sem=send_sems.at[s], recv_sem=recv_sems.at[s],
                device_id=peer, device_id_type=MESH_ID)
            send, arrival = mk(me), mk(_dev_index(peer))
            send.start()
            copies.append((send, arrival))
        exchange.send_chip_sums()
        exchange.finish()
        for send, arrival in copies:
            arrival.wait_recv()
            send.wait_send()
        own.wait()

    hbm = pl.BlockSpec(memory_space=pl.ANY)
    shard = d_win.shape[1:]
    return pl.pallas_call(
        body, name="exchange_grads", in_specs=[hbm, hbm],
        out_specs=[pl.BlockSpec(memory_space=pltpu.VMEM), hbm],
        out_shape=[jax.ShapeDtypeStruct(shard, F32), jax.ShapeDtypeStruct((N_DEV,) + stats.shape, F32)],
        scratch_shapes=[pltpu.SemaphoreType.DMA((N_DEV,)), pltpu.SemaphoreType.DMA((N_DEV,)), pltpu.SemaphoreType.DMA(())]
        + _ReduceScatter.scratch_shapes([shard]),
        compiler_params=_params(),
    )(d_win, stats)


WEIGHT_ORDER = ("pre_norm", "w_in", "sink_a", "mem_norm", "w_mem_kv", "w_out", "post_norm")


def _adamw_all(grads, r_stats, weights, moments_m, moments_v):
    n = len(WEIGHT_ORDER)
    stat_row = {"pre_norm": 0, "mem_norm": 1, "post_norm": 2, "sink_a": 3}

    def body(*refs):
        gw_in, gw_mem, gw_out, st_ref = refs[0:4]
        w_refs, m_refs, v_refs = (dict(zip(WEIGHT_ORDER, refs[4 + n * t:4 + n * (t + 1)])) for t in range(3))
        loss_ref = refs[4 + 3 * n]
        outs = refs[5 + 3 * n:]
        g_small = st_ref[0]
        for s in range(1, N_DEV):
            g_small = g_small + st_ref[s]
        loss_ref[...] = g_small[4:5, 0:1]
        big = {"w_in": gw_in, "w_mem_kv": gw_mem, "w_out": gw_out}
        for i, name in enumerate(WEIGHT_ORDER):
            if name in big:
                g = big[name][...]
                at = lambda ref: ref[0]
            else:
                width = w_refs[name].shape[-1]
                g = g_small[stat_row[name]:stat_row[name] + 1, 0:width]
                at = lambda ref: ref[...]
            m2 = ADAM_B1 * at(m_refs[name]) + (1.0 - ADAM_B1) * g
            v2 = ADAM_B2 * at(v_refs[name]) + (1.0 - ADAM_B2) * (g * g)
            m_hat = m2 / (1.0 - ADAM_B1 ** ADAM_STEP)
            v_hat = v2 / (1.0 - ADAM_B2 ** ADAM_STEP)
            delta = -ADAM_LR * (m_hat / (jnp.sqrt(v_hat) + ADAM_EPS) + ADAM_WD * at(w_refs[name]))
            for kind, val in enumerate((g, delta, m2, v2)):
                out = outs[kind * n + i]
                if name in big:
                    out[0] = val
                else:
                    out[...] = val

    shapes = [weights[name].shape for name in WEIGHT_ORDER]
    res = pl.pallas_call(
        body, name="adamw_all",
        out_shape=[jax.ShapeDtypeStruct((1, 1), F32)] + [jax.ShapeDtypeStruct(sh, F32) for sh in shapes] * 4,
        compiler_params=_params(),
    )(grads["w_in"], grads["w_mem_kv"], grads["w_out"], r_stats,
      *[weights[k] for k in WEIGHT_ORDER], *[moments_m[k] for k in WEIGHT_ORDER], *[moments_v[k] for k in WEIGHT_ORDER])
    return res[0].reshape(()), res[1:]


def kernel(x, mem, pre_norm, w_in, sink_a, mem_norm, w_mem_kv, w_out, post_norm, loss_target, m_pre_norm, m_w_in, m_sink_a, m_mem_norm, m_w_mem_kv, m_w_out, m_post_norm, v_pre_norm, v_w_in, v_sink_a, v_mem_norm, v_w_mem_kv, v_w_out, v_post_norm):
    sink = jnp.pad(sink_a[0], (0, 8 - A_HEADS))
    grad_x, d_win, g_wmem, g_wout, stats = _local_step(
        x[0], mem[0], pre_norm, w_in[0], sink, mem_norm, w_mem_kv[0], w_out[0], post_norm, loss_target[0])
    g_win, r_stats = _exchange_grads(d_win, stats)
    weights = dict(pre_norm=pre_norm, w_in=w_in, sink_a=sink_a, mem_norm=mem_norm, w_mem_kv=w_mem_kv, w_out=w_out,
                   post_norm=post_norm)
    moments_m = dict(pre_norm=m_pre_norm, w_in=m_w_in, sink_a=m_sink_a, mem_norm=m_mem_norm, w_mem_kv=m_w_mem_kv,
                     w_out=m_w_out, post_norm=m_post_norm)
    moments_v = dict(pre_norm=v_pre_norm, w_in=v_w_in, sink_a=v_sink_a, mem_norm=v_mem_norm, w_mem_kv=v_w_mem_kv,
                     w_out=v_w_out, post_norm=v_post_norm)
    loss, rest = _adamw_all(dict(w_in=g_win, w_mem_kv=g_wmem, w_out=g_wout), r_stats, weights, moments_m, moments_v)
    return (loss, grad_x[None], *rest)
```

```python
import jax
import jax.numpy as jnp
from jax import lax
from jax.experimental import pallas as pl
from jax.experimental.pallas import tpu as pltpu

F32 = jnp.float32
BF16 = jnp.bfloat16

D_MODEL = 1024
HEAD_DIM = 64
ROT_DIM = 16
ROPE_THETA = 500000.0
BLOCK = 128
LANES = 128
N_MEM = 256
RMS_EPS = 1e-6
SCALE = HEAD_DIM ** -0.5
A_HEADS, A_GROUP = 6, 3
B_HEADS = 6
C_HEADS = 4
A_W, A_KV_W, B_W, C_W = 384, 128, 384, 256
_IN_PIECES = (("qa", A_W), ("ka", A_KV_W), ("va", A_KV_W), ("ga", A_W), ("qb", B_W), ("kb", B_W), ("vb", B_W),
              ("gb", B_W), ("qc", C_W), ("gc", C_W))
COLS, D_IN = {}, 0
for _name, _width in _IN_PIECES:
    COLS[_name] = (D_IN, D_IN + _width)
    D_IN += _width
N_DEV = 8
SHARD_IN = D_IN // N_DEV
SHARD_ROWS = D_MODEL // N_DEV
B_CONFIGS = ((128, 1), (512, 4), (2048, 16))
DILS = (4, 16)
NEG = -1e30
ATTN_BLOCKS_PER_STEP = 4
DELTA_LANE = 64
VMEM_LIMIT = 56 * 1024 * 1024

ADAM_LR, ADAM_B1, ADAM_B2, ADAM_EPS, ADAM_WD, ADAM_STEP = 0.001, 0.9, 0.999, 1e-08, 0.01, 10
MESH_ID = pl.DeviceIdType.MESH


def _params(**kw):
    return pltpu.CompilerParams(vmem_limit_bytes=VMEM_LIMIT, **kw)


def _full(shape):
    n = len(shape)
    return pl.BlockSpec(shape, lambda *_: (0,) * n)


def _row(tm, w):
    return pl.BlockSpec((tm, w), lambda i: (i, 0))


def _mesh_pos():
    return lax.axis_index("x"), lax.axis_index("y"), lax.axis_index("c")


def _dev_index(pos):
    return 4 * pos[0] + 2 * pos[1] + pos[2]


def _xor_peer(pos, s):
    x, y, c = pos
    return (1 - x if s & 4 else x, 1 - y if s & 2 else y, 1 - c if s & 1 else c)


def _perm_view(a, dil):
    return a.reshape(a.shape[0] // (BLOCK * dil), dil, BLOCK, a.shape[1])


def _perm_spec(tm, dil, w):
    chunk = BLOCK * dil
    if tm >= chunk:
        return pl.BlockSpec((tm // chunk, dil, BLOCK, w), lambda i: (i, 0, 0, 0))
    per = chunk // tm
    return pl.BlockSpec((1, dil, tm // dil, w), lambda i: (i // per, 0, i % per, 0))


def _put(scr, val):
    for c in range(val.shape[1] // LANES):
        scr[c] = val[:, LANES * c:LANES * (c + 1)]


def _get(scr):
    n = scr.shape[0]
    return scr[0] if n == 1 else jnp.concatenate([scr[c] for c in range(n)], axis=1)


def _get_class(scr, r, dil):
    n, rows = scr.shape[0], scr.shape[1]
    parts = [scr.at[c][pl.ds(r, rows // dil, stride=dil), :] for c in range(n)]
    return parts[0] if n == 1 else jnp.concatenate(parts, axis=1)


def _store_permuted(scr, out_ref, dil, dtype):
    for r in range(dil):
        out_ref[0, r] = _get_class(scr, r, dil).astype(dtype)


def _load_permuted(in_ref, scr, dil):
    n, rows = scr.shape[0], scr.shape[1]
    for r in range(dil):
        val = in_ref[0, r].astype(F32)
        for c in range(n):
            scr.at[c][pl.ds(r, rows // dil, stride=dil), :] = val[:, LANES * c:LANES * (c + 1)]
    return _get(scr)


def _rotate128(t, c, up, dn):
    half = ROT_DIM // 2
    return t * c + pltpu.roll(t, half, 1) * up + pltpu.roll(t, LANES - half, 1) * dn


def _rotate(t, c, up, dn):
    outs = [_rotate128(t[:, LANES * j:LANES * (j + 1)], c, up, dn) for j in range(t.shape[1] // LANES)]
    return outs[0] if len(outs) == 1 else jnp.concatenate(outs, axis=1)


def _per_query_head(kv):
    lane = lax.broadcasted_iota(jnp.int32, kv.shape, 1)
    other = pltpu.roll(kv, HEAD_DIM, 1)
    return jnp.concatenate([jnp.where(lane < HEAD_DIM, kv, other), kv, jnp.where(lane < HEAD_DIM, other, kv)], axis=1)


def _per_kv_head(d):
    s0, s1, s2 = (d[:, LANES * p:LANES * (p + 1)] for p in range(3))
    lane = lax.broadcasted_iota(jnp.int32, s0.shape, 1)
    return jnp.where(lane < HEAD_DIM, s0 + pltpu.roll(s0, HEAD_DIM, 1) + s1, s1 + s2 + pltpu.roll(s2, HEAD_DIM, 1))


def _w_in_scratch():
    return [pltpu.VMEM((D_MODEL, D_IN), BF16), pltpu.SemaphoreType.DMA((N_DEV,))]


def _stage_w_in(w_hbm, w_scr, sems):
    @pl.when(pl.program_id(0) == 0)
    def _():
        copies = [pltpu.make_async_copy(w_hbm.at[k], w_scr.at[:, pl.ds(SHARD_IN * k, SHARD_IN)], sems.at[k])
                  for k in range(N_DEV)]
        for cp in copies:
            cp.start()
        for cp in copies:
            cp.wait()


def _inproj(u, w_in_full, tabs, w_mem, w_out, tm=1024):
    seq = u.shape[0]
    n_chunk = D_IN // LANES
    n_steps = seq // tm

    def body(u_ref, w_hbm, c_ref, up_ref, dn_ref, wm_ref, wo_ref, qa_ref, ka_ref, va_ref,
             qb1_ref, kb1_ref, vb1_ref, qb4_ref, kb4_ref, vb4_ref, qb16_ref, kb16_ref, vb16_ref,
             qc_ref, gate_ref, wm_all, wo_all, proj, w_scr, w_sems, wm_b, wo_b, send_sems, recv_sems, local_sems):
        step = pl.program_id(0)
        shards, gathered = (wm_b, wo_b), (wm_all, wo_all)

        def gather_copies(arriving):
            pos = _mesh_pos()
            me = _dev_index(pos)
            local = [] if arriving else [
                pltpu.make_async_copy(shards[a], gathered[a].at[me], local_sems.at[a]) for a in range(2)]
            remote = []
            for s in range(1, N_DEV):
                peer = _xor_peer(pos, s)
                for a in range(2):
                    remote.append(pltpu.make_async_remote_copy(
                        src_ref=shards[a], dst_ref=gathered[a].at[_dev_index(peer) if arriving else me],
                        send_sem=send_sems.at[a, s], recv_sem=recv_sems.at[a, s], device_id=peer,
                        device_id_type=MESH_ID))
            return local, remote

        @pl.when(step == 0)
        def _():
            wm_b[...] = wm_ref[...].astype(BF16)
            wo_b[...] = wo_ref[...].astype(BF16)
            local, sends = gather_copies(arriving=False)
            for cp in local + sends:
                cp.start()

        _stage_w_in(w_hbm, w_scr, w_sems)
        u = u_ref[...]
        for n0 in range(0, D_IN, D_MODEL):
            acc = jnp.dot(u, w_scr[:, n0:n0 + D_MODEL], preferred_element_type=F32)
            for c3 in range(D_MODEL // LANES):
                proj[n0 // LANES + c3] = acc[:, LANES * c3:LANES * (c3 + 1)]
        c, up, dn = c_ref[...], up_ref[...], dn_ref[...]

        def chunks_of(piece):
            lo, hi = COLS[piece]
            return range(lo // LANES, hi // LANES)

        def cols(piece, rot=False, scale=None):
            parts = []
            for ch in chunks_of(piece):
                t = proj[ch]
                if rot:
                    t = _rotate128(t, c, up, dn)
                if scale is not None:
                    t = t * scale
                parts.append(t)
            return parts[0] if len(parts) == 1 else jnp.concatenate(parts, axis=1)

        qa_ref[...] = cols("qa", True, SCALE).astype(BF16)
        ka_ref[...] = _per_query_head(cols("ka", True)).astype(BF16)
        va_ref[...] = _per_query_head(cols("va")).astype(BF16)
        gate_ref[:, 0:A_W] = cols("ga").astype(BF16)
        gate_ref[:, A_W:A_W + B_W] = cols("gb").astype(BF16)
        gate_ref[:, A_W + B_W:D_MODEL] = cols("gc").astype(BF16)
        qc_ref[...] = cols("qc", False, SCALE).astype(BF16)
        for ch in chunks_of("qb"):
            proj[ch] = _rotate128(proj[ch], c, up, dn) * SCALE
        for ch in chunks_of("kb"):
            proj[ch] = _rotate128(proj[ch], c, up, dn)
        for piece, nat, p4, p16 in (("qb", qb1_ref, qb4_ref, qb16_ref), ("kb", kb1_ref, kb4_ref, kb16_ref),
                                    ("vb", vb1_ref, vb4_ref, vb16_ref)):
            chunks = chunks_of(piece)
            nat[...] = jnp.concatenate([proj[ch] for ch in chunks], axis=1).astype(BF16)
            for dil, ref in ((4, p4), (16, p16)):
                span = min(tm, BLOCK * dil)
                for cc in range(tm // span):
                    for rr in range(dil):
                        ref[cc, rr] = jnp.concatenate(
                            [proj.at[ch][pl.ds(cc * span + rr, span // dil, stride=dil), :] for ch in chunks],
                            axis=1).astype(BF16)

        @pl.when(step == n_steps - 1)
        def _():
            for cp in gather_copies(arriving=True)[1]:
                cp.wait_recv()
            local, sends = gather_copies(arriving=False)
            for cp in sends:
                cp.wait_send()
            for cp in local:
                cp.wait()

    nat_w = (A_W, A_W, A_W, B_W, B_W, B_W)
    out_specs = [_row(tm, w) for w in nat_w]
    out_shape = [jax.ShapeDtypeStruct((seq, w), BF16) for w in nat_w]
    for dil in DILS:
        out_specs += [_perm_spec(tm, dil, B_W)] * 3
        out_shape += [jax.ShapeDtypeStruct((seq // (BLOCK * dil), dil, BLOCK, B_W), BF16)] * 3
    hbm = pl.BlockSpec(memory_space=pl.ANY)
    out_specs += [_row(tm, C_W), _row(tm, D_MODEL), hbm, hbm]
    out_shape += [jax.ShapeDtypeStruct((seq, C_W), BF16), jax.ShapeDtypeStruct((seq, D_MODEL), BF16),
                  jax.ShapeDtypeStruct((N_DEV,) + w_mem.shape, BF16), jax.ShapeDtypeStruct((N_DEV,) + w_out.shape, BF16)]
    res = pl.pallas_call(
        body, name="inproj", grid=(n_steps,),
        in_specs=[_row(tm, D_MODEL), hbm, _row(tm, LANES), _row(tm, LANES), _row(tm, LANES),
                  _full(w_mem.shape), _full(w_out.shape)],
        out_specs=out_specs, out_shape=out_shape,
        scratch_shapes=[pltpu.VMEM((n_chunk, tm, LANES), F32)] + _w_in_scratch()
        + [pltpu.VMEM(w_mem.shape, BF16), pltpu.VMEM(w_out.shape, BF16), pltpu.SemaphoreType.DMA((2, N_DEV)),
           pltpu.SemaphoreType.DMA((2, N_DEV)), pltpu.SemaphoreType.DMA((2,))],
        compiler_params=_params(dimension_semantics=("arbitrary",)),
    )(u, w_in_full, *tabs, w_mem, w_out)
    qa, ka, va = res[0:3]
    qkv_b = {1: res[3:6], 4: [t.reshape(seq, B_W) for t in res[6:9]], 16: [t.reshape(seq, B_W) for t in res[9:12]]}
    return qa, ka, va, qkv_b, res[12], res[13], res[14], res[15]


def _memkv_fwd(mem, mem_g, w_mem_full):
    def body(mem_ref, g_ref, w_ref, mn_ref, mk_ref, mv_ref):
        mv_ = mem_ref[...]
        r = lax.rsqrt(jnp.mean(mv_ * mv_, axis=-1, keepdims=True) + RMS_EPS)
        mn = ((mv_ * r) * g_ref[...]).astype(BF16)
        mn_ref[...] = mn
        mkv = jnp.dot(mn, w_ref[...], preferred_element_type=F32)
        mk_ref[...] = mkv[:, 0:C_W].astype(BF16)
        mv_ref[...] = mkv[:, C_W:2 * C_W].astype(BF16)

    return pl.pallas_call(
        body, name="memkv_fwd",
        out_shape=[jax.ShapeDtypeStruct((N_MEM, D_MODEL), BF16),
                   jax.ShapeDtypeStruct((N_MEM, C_W), BF16), jax.ShapeDtypeStruct((N_MEM, C_W), BF16)],
        compiler_params=_params(),
    )(mem, mem_g, w_mem_full)


def _memkv_bwd(mem, mem_g, mn, w_mem_full, dmk, dmv):
    def body(mem_ref, g_ref, mn_ref, w_ref, dmk_ref, dmv_ref, dw_ref, st_ref):
        dmkv = jnp.concatenate([dmk_ref[...], dmv_ref[...]], axis=1).astype(BF16)
        dw_ref[...] = lax.dot_general(mn_ref[...], dmkv, (((0,), (0,)), ((), ())), preferred_element_type=F32)
        dmn = lax.dot_general(dmkv, w_ref[...], (((1,), (1,)), ((), ())), preferred_element_type=F32)
        mv_ = mem_ref[...]
        r = lax.rsqrt(jnp.mean(mv_ * mv_, axis=-1, keepdims=True) + RMS_EPS)
        st_ref[...] = jnp.zeros_like(st_ref)
        st_ref[0:1, :] = jnp.sum(dmn * (mv_ * r), axis=0, keepdims=True)

    return pl.pallas_call(
        body, name="memkv_bwd",
        out_shape=[jax.ShapeDtypeStruct((D_MODEL, 2 * C_W), F32), jax.ShapeDtypeStruct((8, D_MODEL), F32)],
        compiler_params=_params(),
    )(mem, mem_g, mn, w_mem_full, dmk, dmv)


def _band_mask(has_prev, max_dist):
    qi = lax.broadcasted_iota(jnp.int32, (BLOCK, 2 * BLOCK), 0)
    kj = lax.broadcasted_iota(jnp.int32, (BLOCK, 2 * BLOCK), 1)
    dist = qi + BLOCK - kj
    return (dist >= 0) & (dist <= max_dist) & ((kj >= BLOCK) | has_prev)


_NT = (((1,), (1,)), ((), ()))
_TN = (((0,), (0,)), ((), ()))


def _head_only(val, h):
    slab = val[:, LANES * (h // 2):LANES * (h // 2 + 1)]
    lane = lax.broadcasted_iota(jnp.int32, slab.shape, 1)
    keep = (lane < HEAD_DIM) if h % 2 == 0 else (lane >= HEAD_DIM)
    return jnp.where(keep, slab, jnp.zeros((), slab.dtype))


class _KvSlabs:
    def __init__(self, cat, group):
        self.cat, self.group, self.swapped = cat, group, {}

    def is_swapped(self, h):
        return (h // self.group) % 2 != h % 2

    def __call__(self, h):
        j = (h // self.group) // 2
        slab = self.cat[:, LANES * j:LANES * (j + 1)]
        if not self.is_swapped(h):
            return slab
        if j not in self.swapped:
            self.swapped[j] = jnp.concatenate([slab[:, HEAD_DIM:], slab[:, :HEAD_DIM]], axis=1)
        return self.swapped[j]


class _BandSteps:
    def __init__(self, seq, dil, nq):
        self.nq, self.rows, self.consecutive = nq, nq * BLOCK, dil == 1
        nb = seq // dil // BLOCK
        if self.consecutive:
            assert nb % nq == 0
            self.outer, self.inner, self.stride = 1, nb // nq, 1
        else:
            assert dil % nq == 0
            self.outer, self.inner, self.stride = dil // nq, nb, dil // nq

    def own(self, w, clamp=False):
        cur = (lambda i: jnp.minimum(i, self.inner - 1)) if clamp else (lambda i: i)
        return pl.BlockSpec((self.rows, w), lambda r, i: (cur(i) * self.stride + r, 0))

    def prev(self, w, clamp=False):
        cur = (lambda i: jnp.minimum(i, self.inner - 1)) if clamp else (lambda i: i)
        if self.consecutive:
            return pl.BlockSpec((BLOCK, w), lambda r, i: (jnp.maximum(cur(i) * self.nq - 1, 0), 0))
        return pl.BlockSpec((self.rows, w), lambda r, i: (jnp.maximum(cur(i) - 1, 0) * self.stride + r, 0))

    def late(self, w):
        return pl.BlockSpec((self.rows, w), lambda r, i: (jnp.maximum(i - 1, 0) * self.stride + r, 0))

    def rows_of(self, j):
        return slice(BLOCK * j, BLOCK * (j + 1))

    def keys(self, p_ref, c_ref, j):
        if not self.consecutive:
            before = p_ref[self.rows_of(j), :]
        elif j == 0:
            before = p_ref[...]
        else:
            before = c_ref[self.rows_of(j - 1), :]
        return jnp.concatenate([before, c_ref[self.rows_of(j), :]], axis=0)

    def has_prev(self, i, j):
        return True if (self.consecutive and j > 0) else (i > 0)


def _banded_fwd(q, k, v, sink, *, dil, heads, group, max_dist, nq, name):
    seq = q.shape[0]
    kvh = heads // group
    qw, kw = heads * HEAD_DIM, kvh * HEAD_DIM
    steps = _BandSteps(seq, dil, nq)

    def body(*refs):
        if sink is not None:
            sink_ref, refs = refs[0], refs[1:]
        q_ref, kp_ref, kc_ref, vp_ref, vc_ref, o_ref, lse_ref, s_scr, p_scr = refs
        i = pl.program_id(1)
        lane = lax.broadcasted_iota(jnp.int32, (BLOCK, LANES), 1)
        k_of = [_KvSlabs(steps.keys(kp_ref, kc_ref, j), group) for j in range(nq)]
        v_of = [_KvSlabs(steps.keys(vp_ref, vc_ref, j), group) for j in range(nq)]
        for j in range(nq):
            qv = q_ref[steps.rows_of(j), :]
            for h in range(heads):
                s_scr[j * heads + h] = lax.dot_general(_head_only(qv, h), k_of[j](h), _NT, preferred_element_type=F32)
        ls = {}
        for j in range(nq):
            valid = _band_mask(steps.has_prev(i, j), max_dist)
            lse_tile = jnp.zeros((BLOCK, LANES), F32)
            for h in range(heads):
                s = jnp.where(valid, s_scr[j * heads + h], NEG)
                m = jnp.max(s, axis=-1, keepdims=True)
                if sink is not None:
                    sk = sink_ref[h]
                    m = jnp.maximum(m, sk)
                p = jnp.exp(s - m)
                l = jnp.sum(p, axis=-1, keepdims=True)
                if sink is not None:
                    l = l + jnp.exp(sk - m)
                p_scr[j * heads + h] = p.astype(BF16)
                ls[j, h] = l
                lse_tile = jnp.where(lane == h, m + jnp.log(l), lse_tile)
            lse_ref[steps.rows_of(j), :] = lse_tile
        for j in range(nq):
            for pr in range(heads // 2):
                he, ho = 2 * pr, 2 * pr + 1
                even = jnp.dot(p_scr[j * heads + he], v_of[j](he), preferred_element_type=F32) / ls[j, he]
                odd = jnp.dot(p_scr[j * heads + ho], v_of[j](ho), preferred_element_type=F32) / ls[j, ho]
                o_ref[steps.rows_of(j), LANES * pr:LANES * (pr + 1)] = jnp.where(lane < HEAD_DIM, even, odd).astype(BF16)

    in_specs = [steps.own(qw), steps.prev(kw), steps.own(kw), steps.prev(kw), steps.own(kw)]
    args = [q, k, k, v, v]
    if sink is not None:
        in_specs = [pl.BlockSpec(memory_space=pltpu.SMEM)] + in_specs
        args = [sink] + args
    return pl.pallas_call(
        body, name=name, grid=(steps.outer, steps.inner), in_specs=in_specs,
        out_specs=[steps.own(qw), steps.own(LANES)],
        out_shape=[jax.ShapeDtypeStruct((seq, qw), BF16), jax.ShapeDtypeStruct((seq, LANES), F32)],
        scratch_shapes=[pltpu.VMEM((nq * heads, BLOCK, 2 * BLOCK), F32), pltpu.VMEM((nq * heads, BLOCK, 2 * BLOCK), BF16)],
        compiler_params=_params(dimension_semantics=("arbitrary", "arbitrary")),
    )(*args)


def _banded_bwd(q, k, v, d_out, stat, sink, *, dil, heads, group, max_dist, nq, name, reduce_scatter=()):
    seq = q.shape[0]
    kvh = heads // group
    qw, kw = heads * HEAD_DIM, kvh * HEAD_DIM
    steps = _BandSteps(seq, dil, nq)
    n_rs = len(reduce_scatter)
    n_in = 7 + n_rs
    n_flat = steps.outer * (steps.inner + 1)

    def body(*refs):
        refs = list(refs)
        sink_ref = refs.pop(0) if sink is not None else None
        (q_ref, kp_ref, kc_ref, vp_ref, vc_ref, do_ref, st_ref), partials = refs[:7], refs[7:n_in]
        refs = refs[n_in:]
        dsink_ref = refs.pop(0) if sink is not None else None
        (dq_ref, dk_ref, dv_ref), sums = refs[:3], refs[3:3 + n_rs]
        kcar, vcar, s_scr, dp_scr, p_scr, ds_scr = refs[3 + n_rs:9 + n_rs]
        r, i = pl.program_id(0), pl.program_id(1)
        if n_rs:
            exchange = _ReduceScatter(tuple(partials), tuple(sums), refs[9 + n_rs:])
            flat = r * (steps.inner + 1) + i

            @pl.when(flat == 0)
            def _():
                exchange.start()

            @pl.when(flat == min(2, n_flat - 1))
            def _():
                exchange.send_chip_sums()

        @pl.when(i == 0)
        def _():
            kcar[...] = jnp.zeros_like(kcar)
            vcar[...] = jnp.zeros_like(vcar)

        if sink is not None:
            @pl.when((i == 0) & (r == 0))
            def _():
                dsink_ref[...] = jnp.zeros_like(dsink_ref)

        @pl.when(i < steps.inner)
        def _():
            lane = lax.broadcasted_iota(jnp.int32, (1, LANES), 1)
            lane_q = lax.broadcasted_iota(jnp.int32, (BLOCK, LANES), 1)
            k_of = [_KvSlabs(steps.keys(kp_ref, kc_ref, j), group) for j in range(nq)]
            v_of = [_KvSlabs(steps.keys(vp_ref, vc_ref, j), group) for j in range(nq)]
            qms, doms = {}, {}
            for j in range(nq):
                qv, dov = q_ref[steps.rows_of(j), :], do_ref[steps.rows_of(j), :]
                for h in range(heads):
                    qms[j, h], doms[j, h] = _head_only(qv, h), _head_only(dov, h)
                    s_scr[j * heads + h] = lax.dot_general(qms[j, h], k_of[j](h), _NT, preferred_element_type=F32)
                    dp_scr[j * heads + h] = lax.dot_general(doms[j, h], v_of[j](h), _NT, preferred_element_type=F32)
            dsink_row = jnp.zeros((1, LANES), F32)
            for j in range(nq):
                st = st_ref[steps.rows_of(j), :]
                valid = _band_mask(steps.has_prev(i, j), max_dist)
                for h in range(heads):
                    lse_h = st[:, h:h + 1]
                    delta = st[:, DELTA_LANE + h:DELTA_LANE + h + 1]
                    p = jnp.where(valid, jnp.exp(s_scr[j * heads + h] - lse_h), 0.0)
                    p_scr[j * heads + h] = p.astype(BF16)
                    ds_scr[j * heads + h] = (p * (dp_scr[j * heads + h] - delta)).astype(BF16)
                    if sink is not None:
                        ds_sink = jnp.sum(-jnp.exp(sink_ref[h] - lse_h) * delta, axis=0, keepdims=True)
                        dsink_row = dsink_row + jnp.where(lane == h, ds_sink, 0.0)
            for j in range(nq):
                for pr in range(heads // 2):
                    he, ho = 2 * pr, 2 * pr + 1
                    even = jnp.dot(ds_scr[j * heads + he], k_of[j](he), preferred_element_type=F32)
                    odd = jnp.dot(ds_scr[j * heads + ho], k_of[j](ho), preferred_element_type=F32)
                    dq_ref[steps.rows_of(j), LANES * pr:LANES * (pr + 1)] = (
                        jnp.where(lane_q < HEAD_DIM, even, odd).astype(BF16))
            if steps.consecutive:
                dk_ref[...] = kcar[...].astype(BF16)
                dv_ref[...] = vcar[...].astype(BF16)
            for j in range(nq):
                for slab in range(kw // LANES):
                    acc = {}
                    for h in range(heads):
                        if (h // group) // 2 != slab:
                            continue
                        key = k_of[j].is_swapped(h)
                        dk_h = lax.dot_general(ds_scr[j * heads + h], qms[j, h], _TN, preferred_element_type=F32)
                        dv_h = lax.dot_general(p_scr[j * heads + h], doms[j, h], _TN, preferred_element_type=F32)
                        acc[key] = (dk_h, dv_h) if key not in acc else (acc[key][0] + dk_h, acc[key][1] + dv_h)
                    dk_j, dv_j = acc.get(False, (None, None))
                    if True in acc:
                        unswap = lambda t: jnp.concatenate([t[:, HEAD_DIM:], t[:, :HEAD_DIM]], axis=1)
                        dk_s, dv_s = unswap(acc[True][0]), unswap(acc[True][1])
                        dk_j = dk_s if dk_j is None else dk_j + dk_s
                        dv_j = dv_s if dv_j is None else dv_j + dv_s
                    sl = slice(LANES * slab, LANES * (slab + 1))
                    own_rows = steps.rows_of(j)
                    if not steps.consecutive:
                        dk_ref[own_rows, sl] = (kcar[own_rows, sl] + dk_j[0:BLOCK]).astype(BF16)
                        dv_ref[own_rows, sl] = (vcar[own_rows, sl] + dv_j[0:BLOCK]).astype(BF16)
                    elif j == 0:
                        last = steps.rows_of(nq - 1)
                        dk_ref[last, sl] = (kcar[last, sl] + dk_j[0:BLOCK]).astype(BF16)
                        dv_ref[last, sl] = (vcar[last, sl] + dv_j[0:BLOCK]).astype(BF16)
                    else:
                        before = steps.rows_of(j - 1)
                        kcar[before, sl] += dk_j[0:BLOCK]
                        vcar[before, sl] += dv_j[0:BLOCK]
                    kcar[own_rows, sl] = dk_j[BLOCK:2 * BLOCK]
                    vcar[own_rows, sl] = dv_j[BLOCK:2 * BLOCK]
            if sink is not None:
                dsink_ref[0:1, :] += dsink_row

        @pl.when(i == steps.inner)
        def _():
            dk_ref[...] = kcar[...].astype(BF16)
            dv_ref[...] = vcar[...].astype(BF16)

        if n_rs:
            @pl.when(flat == n_flat - 1)
            def _():
                exchange.finish()

    own, prev = (lambda w: steps.own(w, clamp=True)), (lambda w: steps.prev(w, clamp=True))
    rs_shapes = [t.shape[1:] for t in reduce_scatter]
    in_specs = ([own(qw), prev(kw), own(kw), prev(kw), own(kw), own(qw), own(LANES)]
                + [pl.BlockSpec(memory_space=pl.ANY)] * n_rs)
    args = [q, k, k, v, v, d_out, stat, *reduce_scatter]
    out_specs = [own(qw), steps.late(kw), steps.late(kw)] + [_full(s) for s in rs_shapes]
    out_shape = [jax.ShapeDtypeStruct((seq, qw), BF16), jax.ShapeDtypeStruct((seq, kw), BF16),
                 jax.ShapeDtypeStruct((seq, kw), BF16)] + [jax.ShapeDtypeStruct(s, F32) for s in rs_shapes]
    if sink is not None:
        in_specs = [pl.BlockSpec(memory_space=pltpu.SMEM)] + in_specs
        args = [sink] + args
        out_specs = [_full((8, LANES))] + out_specs
        out_shape = [jax.ShapeDtypeStruct((8, LANES), F32)] + out_shape
    n_hb = nq * heads
    res = pl.pallas_call(
        body, name=name, grid=(steps.outer, steps.inner + 1), in_specs=in_specs, out_specs=out_specs,
        out_shape=out_shape,
        scratch_shapes=[pltpu.VMEM((steps.rows, kw), F32), pltpu.VMEM((steps.rows, kw), F32)]
        + [pltpu.VMEM((n_hb, BLOCK, 2 * BLOCK), F32)] * 2 + [pltpu.VMEM((n_hb, BLOCK, 2 * BLOCK), BF16)] * 2
        + (_ReduceScatter.scratch_shapes(rs_shapes) if n_rs else []),
        compiler_params=_params(dimension_semantics=("arbitrary", "arbitrary")),
    )(*args)
    if sink is not None:
        return (*res[1:4], res[0], *res[4:])
    return res


def _cross_fwd(q, mk, mv, tq=1024):
    seq = q.shape[0]

    def body(q_ref, mk_ref, mv_ref, o_ref, lse_ref, s_scr, p_scr):
        qv = q_ref[...]
        k_of, v_of = _KvSlabs(mk_ref[...], 1), _KvSlabs(mv_ref[...], 1)
        lane = lax.broadcasted_iota(jnp.int32, (tq, LANES), 1)
        lse_tile = jnp.zeros((tq, LANES), F32)
        for h in range(C_HEADS):
            s_scr[h] = lax.dot_general(_head_only(qv, h), k_of(h), _NT, preferred_element_type=F32)
        ls = []
        for h in range(C_HEADS):
            s = s_scr[h]
            m = jnp.max(s, axis=-1, keepdims=True)
            p = jnp.exp(s - m)
            l = jnp.sum(p, axis=-1, keepdims=True)
            p_scr[h] = p.astype(BF16)
            ls.append(l)
            lse_tile = jnp.where(lane == h, m + jnp.log(l), lse_tile)
        for pr in range(C_HEADS // 2):
            even = jnp.dot(p_scr[2 * pr], v_of(2 * pr), preferred_element_type=F32) / ls[2 * pr]
            odd = jnp.dot(p_scr[2 * pr + 1], v_of(2 * pr + 1), preferred_element_type=F32) / ls[2 * pr + 1]
            o_ref[:, LANES * pr:LANES * (pr + 1)] = jnp.where(lane < HEAD_DIM, even, odd).astype(BF16)
        lse_ref[...] = lse_tile

    return pl.pallas_call(
        body, name="cross_fwd", grid=(seq // tq,),
        in_specs=[_row(tq, C_W), _full((N_MEM, C_W)), _full((N_MEM, C_W))],
        out_specs=[_row(tq, C_W), _row(tq, LANES)],
        out_shape=[jax.ShapeDtypeStruct((seq, C_W), BF16), jax.ShapeDtypeStruct((seq, LANES), F32)],
        scratch_shapes=[pltpu.VMEM((C_HEADS, tq, N_MEM), F32), pltpu.VMEM((C_HEADS, tq, N_MEM), BF16)],
        compiler_params=_params(dimension_semantics=("arbitrary",)),
    )(q, mk, mv)


def _cross_bwd(q, mk, mv, d_out, stat, tq=1024):
    seq = q.shape[0]

    def body(q_ref, mk_ref, mv_ref, do_ref, st_ref, dq_ref, dmk_ref, dmv_ref, s_scr, dp_scr, p_scr, ds_scr):
        @pl.when(pl.program_id(0) == 0)
        def _():
            dmk_ref[...] = jnp.zeros_like(dmk_ref)
            dmv_ref[...] = jnp.zeros_like(dmv_ref)

        qv, dov, st = q_ref[...], do_ref[...], st_ref[...]
        k_of, v_of = _KvSlabs(mk_ref[...], 1), _KvSlabs(mv_ref[...], 1)
        qms = [_head_only(qv, h) for h in range(C_HEADS)]
        doms = [_head_only(dov, h) for h in range(C_HEADS)]
        for h in range(C_HEADS):
            s_scr[h] = lax.dot_general(qms[h], k_of(h), _NT, preferred_element_type=F32)
            dp_scr[h] = lax.dot_general(doms[h], v_of(h), _NT, preferred_element_type=F32)
        for h in range(C_HEADS):
            p = jnp.exp(s_scr[h] - st[:, h:h + 1])
            p_scr[h] = p.astype(BF16)
            ds_scr[h] = (p * (dp_scr[h] - st[:, DELTA_LANE + h:DELTA_LANE + h + 1])).astype(BF16)
        lane = lax.broadcasted_iota(jnp.int32, (tq, LANES), 1)
        for pr in range(C_HEADS // 2):
            sl = slice(LANES * pr, LANES * (pr + 1))
            even = jnp.dot(ds_scr[2 * pr], k_of(2 * pr), preferred_element_type=F32)
            odd = jnp.dot(ds_scr[2 * pr + 1], k_of(2 * pr + 1), preferred_element_type=F32)
            dq_ref[:, sl] = jnp.where(lane < HEAD_DIM, even, odd).astype(BF16)
            dmk_ref[:, sl] += (lax.dot_general(ds_scr[2 * pr], qms[2 * pr], _TN, preferred_element_type=F32)
                               + lax.dot_general(ds_scr[2 * pr + 1], qms[2 * pr + 1], _TN, preferred_element_type=F32))
            dmv_ref[:, sl] += (lax.dot_general(p_scr[2 * pr], doms[2 * pr], _TN, preferred_element_type=F32)
                               + lax.dot_general(p_scr[2 * pr + 1], doms[2 * pr + 1], _TN, preferred_element_type=F32))

    return pl.pallas_call(
        body, name="cross_bwd", grid=(seq // tq,),
        in_specs=[_row(tq, C_W), _full((N_MEM, C_W)), _full((N_MEM, C_W)), _row(tq, C_W), _row(tq, LANES)],
        out_specs=[_row(tq, C_W), _full((N_MEM, C_W)), _full((N_MEM, C_W))],
        out_shape=[jax.ShapeDtypeStruct((seq, C_W), BF16), jax.ShapeDtypeStruct((N_MEM, C_W), F32),
                   jax.ShapeDtypeStruct((N_MEM, C_W), F32)],
        scratch_shapes=[pltpu.VMEM((C_HEADS, tq, N_MEM), F32)] * 2 + [pltpu.VMEM((C_HEADS, tq, N_MEM), BF16)] * 2,
        compiler_params=_params(dimension_semantics=("arbitrary",)),
    )(q, mk, mv, d_out, stat)


def _per_head(tile, width):
    rows = tile.shape[0]
    lane = lax.broadcasted_iota(jnp.int32, (rows, LANES), 1)
    slabs = []
    for p in range(width // LANES):
        even = jnp.broadcast_to(tile[:, 2 * p:2 * p + 1], (rows, LANES))
        odd = jnp.broadcast_to(tile[:, 2 * p + 1:2 * p + 2], (rows, LANES))
        slabs.append(jnp.where(lane < HEAD_DIM, even, odd))
    return slabs[0] if len(slabs) == 1 else jnp.concatenate(slabs, axis=1)


def _with_delta(lse_tile, prod):
    rows = lse_tile.shape[0]
    lane = lax.broadcasted_iota(jnp.int32, (rows, LANES), 1)
    tile = lse_tile
    for p in range(prod.shape[1] // LANES):
        slab = prod[:, LANES * p:LANES * (p + 1)]
        even = jnp.sum(jnp.where(lane < HEAD_DIM, slab, 0.0), axis=-1, keepdims=True)
        odd = jnp.sum(jnp.where(lane >= HEAD_DIM, slab, 0.0), axis=-1, keepdims=True)
        tile = jnp.where(lane == DELTA_LANE + 2 * p, even, tile)
        tile = jnp.where(lane == DELTA_LANE + 2 * p + 1, odd, tile)
    return tile


def _mid(oa, lse_a, ob, lse_b, oc, lse_c, gate, x, target, w_out_full, post_g, tm=512):
    seq = x.shape[0]
    n_b = B_W // LANES

    def body(oa_ref, la_ref, b1_ref, l1_ref, b4_ref, l4_ref, b16_ref, l16_ref, oc_ref, lc_ref,
             gate_ref, x_ref, t_ref, w_ref, pg_ref,
             dh_ref, dg_ref, doa_ref, sa_ref, dob1_ref, sb1_ref, dob4_ref, sb4_ref, dob16_ref, sb16_ref,
             doc_ref, sc_ref, dw_ref, st_ref, scr_b4, scr_b16, scr_l4, scr_l16, scr_do, scr_sb):
        @pl.when(pl.program_id(0) == 0)
        def _():
            dw_ref[...] = jnp.zeros_like(dw_ref)
            st_ref[...] = jnp.zeros_like(st_ref)

        b1, l1 = b1_ref[...].astype(F32), l1_ref[...]
        b4, l4 = _load_permuted(b4_ref, scr_b4, 4), _load_permuted(l4_ref, scr_l4, 4)
        b16, l16 = _load_permuted(b16_ref, scr_b16, 16), _load_permuted(l16_ref, scr_l16, 16)
        lm = jnp.maximum(jnp.maximum(l1, l4), l16)
        e1, e4, e16 = jnp.exp(l1 - lm), jnp.exp(l4 - lm), jnp.exp(l16 - lm)
        den = e1 + e4 + e16
        lse_b_tile = lm + jnp.log(den)
        ob_v = _per_head(e1 / den, B_W) * b1 + _per_head(e4 / den, B_W) * b4 + _per_head(e16 / den, B_W) * b16
        o_all = jnp.concatenate([oa_ref[...].astype(F32), ob_v, oc_ref[...].astype(F32)], axis=1)
        g = gate_ref[...].astype(F32)
        sig = 1.0 / (1.0 + jnp.exp(-g))
        silu = g * sig
        y = (o_all * silu).astype(BF16)
        w = w_ref[...]
        z = jnp.dot(y, w, preferred_element_type=F32)
        rz = lax.rsqrt(jnp.mean(z * z, axis=-1, keepdims=True) + RMS_EPS)
        hn = z * rz
        pg = pg_ref[...]
        err = (x_ref[...] + hn * pg) - t_ref[...]
        loss = 0.5 * jnp.sum(jnp.mean(err * err, axis=-1, keepdims=True), axis=0, keepdims=True)
        dh = err * (1.0 / D_MODEL)
        dh_ref[...] = dh.astype(BF16)
        st_ref[0:1, :] += jnp.sum(dh * hn, axis=0, keepdims=True)
        st_ref[1:2, :] += jnp.broadcast_to(loss, (1, D_MODEL))
        dhn = dh * pg
        dz = (rz * (dhn - hn * jnp.mean(dhn * hn, axis=-1, keepdims=True))).astype(BF16)
        dy = lax.dot_general(dz, w, _NT, preferred_element_type=F32)
        dw_ref[...] += lax.dot_general(y, dz, _TN, preferred_element_type=F32)
        dg_ref[...] = (dy * o_all * (sig * (1.0 + g * (1.0 - sig)))).astype(BF16)
        d_o = (dy * silu).astype(BF16)
        prod = d_o.astype(F32) * o_all
        doa_ref[...] = d_o[:, 0:A_W]
        sa_ref[...] = _with_delta(la_ref[...], prod[:, 0:A_W])
        doc_ref[...] = d_o[:, A_W + B_W:D_MODEL]
        sc_ref[...] = _with_delta(lc_ref[...], prod[:, A_W + B_W:D_MODEL])
        d_ob = d_o[:, A_W:A_W + B_W]
        stat_b = _with_delta(lse_b_tile, prod[:, A_W:A_W + B_W])
        dob1_ref[...] = d_ob
        sb1_ref[...] = stat_b
        _put(scr_do, d_ob.astype(F32))
        _put(scr_sb, stat_b)
        _store_permuted(scr_do, dob4_ref, 4, BF16)
        _store_permuted(scr_sb, sb4_ref, 4, F32)
        _store_permuted(scr_do, dob16_ref, 16, BF16)
        _store_permuted(scr_sb, sb16_ref, 16, F32)

    p4 = lambda w: _perm_spec(tm, 4, w)
    p16 = lambda w: _perm_spec(tm, 16, w)
    in_specs = [_row(tm, A_W), _row(tm, LANES), _row(tm, B_W), _row(tm, LANES), p4(B_W), p4(LANES), p16(B_W), p16(LANES),
                _row(tm, C_W), _row(tm, LANES), _row(tm, D_MODEL), _row(tm, D_MODEL), _row(tm, D_MODEL),
                _full((D_MODEL, D_MODEL)), _full((1, D_MODEL))]
    sds = jax.ShapeDtypeStruct
    v4 = lambda w, dt: sds((seq // (BLOCK * 4), 4, BLOCK, w), dt)
    v16 = lambda w, dt: sds((seq // (BLOCK * 16), 16, BLOCK, w), dt)
    out_specs = [_row(tm, D_MODEL), _row(tm, D_MODEL), _row(tm, A_W), _row(tm, LANES), _row(tm, B_W), _row(tm, LANES),
                 p4(B_W), p4(LANES), p16(B_W), p16(LANES), _row(tm, C_W), _row(tm, LANES),
                 _full((D_MODEL, D_MODEL)), _full((8, D_MODEL))]
    out_shape = [sds((seq, D_MODEL), BF16), sds((seq, D_MODEL), BF16), sds((seq, A_W), BF16), sds((seq, LANES), F32),
                 sds((seq, B_W), BF16), sds((seq, LANES), F32), v4(B_W, BF16), v4(LANES, F32), v16(B_W, BF16),
                 v16(LANES, F32), sds((seq, C_W), BF16), sds((seq, LANES), F32),
                 sds((D_MODEL, D_MODEL), F32), sds((8, D_MODEL), F32)]
    res = pl.pallas_call(
        body, name="mid", grid=(seq // tm,), in_specs=in_specs, out_specs=out_specs, out_shape=out_shape,
        scratch_shapes=[pltpu.VMEM((n_b, tm, LANES), F32), pltpu.VMEM((n_b, tm, LANES), F32),
                        pltpu.VMEM((1, tm, LANES), F32), pltpu.VMEM((1, tm, LANES), F32),
                        pltpu.VMEM((n_b, tm, LANES), F32), pltpu.VMEM((1, tm, LANES), F32)],
        compiler_params=_params(dimension_semantics=("arbitrary",)),
    )(oa, lse_a, ob[1], lse_b[1], _perm_view(ob[4], 4), _perm_view(lse_b[4], 4), _perm_view(ob[16], 16),
      _perm_view(lse_b[16], 16), oc, lse_c, gate, x, target, w_out_full, post_g)
    dh, d_gate, do_a, st_a, do_b1, st_b1, do_b4, st_b4, do_b16, st_b16, do_c, st_c, d_wout, stats = res
    flat = lambda t: t.reshape(seq, t.shape[-1])
    d_b = {1: (do_b1, st_b1), 4: (flat(do_b4), flat(st_b4)), 16: (flat(do_b16), flat(st_b16))}
    return dh, d_gate, (do_a, st_a), d_b, (do_c, st_c), d_wout, stats


def _inproj_bwd(x, u, dh, pre_g, w_in_full, tabs, dqa, dka, dva, dqkv_b, dqc, dgate, tm=512):
    seq = x.shape[0]
    n_b = B_W // LANES

    def body(x_ref, u_ref, dh_ref, g_ref, w_hbm, c_ref, up_ref, dn_ref, dqa_ref, dka_ref, dva_ref,
             dq1, dk1, dv1, dq4, dk4, dv4, dq16, dk16, dv16, dqc_ref, dg_ref,
             gx_ref, dw_ref, st_ref, scr4, scr16, w_scr, w_sems, dp_ref):
        _stage_w_in(w_hbm, w_scr, w_sems)

        @pl.when(pl.program_id(0) == 0)
        def _():
            st_ref[...] = jnp.zeros_like(st_ref)
            dw_ref[...] = jnp.zeros_like(dw_ref)

        c, up, dn = c_ref[...], -up_ref[...], -dn_ref[...]
        unrot = lambda t: _rotate(t, c, up, dn)
        total = lambda r1, r4, r16: (r1[...].astype(F32) + _load_permuted(r4, scr4, 4)
                                     + _load_permuted(r16, scr16, 16))
        at = lambda piece: slice(*COLS[piece])
        dp_ref[:, at("qa")] = (unrot(dqa_ref[...].astype(F32)) * SCALE).astype(BF16)
        dp_ref[:, at("ka")] = unrot(_per_kv_head(dka_ref[...].astype(F32))).astype(BF16)
        dp_ref[:, at("va")] = _per_kv_head(dva_ref[...].astype(F32)).astype(BF16)
        dp_ref[:, at("ga")] = dg_ref[:, 0:A_W]
        dp_ref[:, at("qb")] = (unrot(total(dq1, dq4, dq16)) * SCALE).astype(BF16)
        dp_ref[:, at("kb")] = unrot(total(dk1, dk4, dk16)).astype(BF16)
        dp_ref[:, at("vb")] = total(dv1, dv4, dv16).astype(BF16)
        dp_ref[:, at("gb")] = dg_ref[:, A_W:A_W + B_W]
        dp_ref[:, at("qc")] = (dqc_ref[...].astype(F32) * SCALE).astype(BF16)
        dp_ref[:, at("gc")] = dg_ref[:, A_W + B_W:D_MODEL]
        du = lax.dot_general(dp_ref[...], w_scr[...], _NT, preferred_element_type=F32)
        res = lax.dot_general(u_ref[...], dp_ref[...], _TN, preferred_element_type=F32)
        for k in range(N_DEV):
            dw_ref[k] += res[:, SHARD_IN * k:SHARD_IN * (k + 1)]
        xv = x_ref[...]
        r = lax.rsqrt(jnp.mean(xv * xv, axis=-1, keepdims=True) + RMS_EPS)
        xh = xv * r
        st_ref[0:1, :] += jnp.sum(du * xh, axis=0, keepdims=True)
        dxh = du * g_ref[...]
        gx_ref[...] = dh_ref[...].astype(F32) + r * (dxh - xh * jnp.mean(dxh * xh, axis=-1, keepdims=True))

    in_specs = ([_row(tm, D_MODEL), _row(tm, D_MODEL), _row(tm, D_MODEL), _full((1, D_MODEL)),
                 pl.BlockSpec(memory_space=pl.ANY),
                 _row(tm, LANES), _row(tm, LANES), _row(tm, LANES), _row(tm, A_W), _row(tm, A_W), _row(tm, A_W)]
                + [_row(tm, B_W)] * 3 + [_perm_spec(tm, 4, B_W)] * 3 + [_perm_spec(tm, 16, B_W)] * 3
                + [_row(tm, C_W), _row(tm, D_MODEL)])
    dw_spec = pl.BlockSpec((N_DEV, D_MODEL, SHARD_IN), lambda i: (0, 0, 0), pipeline_mode=pl.Buffered(1))
    return pl.pallas_call(
        body, name="inproj_bwd", grid=(seq // tm,), in_specs=in_specs,
        out_specs=[_row(tm, D_MODEL), dw_spec, _full((8, D_MODEL))],
        out_shape=[jax.ShapeDtypeStruct((seq, D_MODEL), F32), jax.ShapeDtypeStruct((N_DEV, D_MODEL, SHARD_IN), F32),
                   jax.ShapeDtypeStruct((8, D_MODEL), F32)],
        scratch_shapes=[pltpu.VMEM((n_b, tm, LANES), F32), pltpu.VMEM((n_b, tm, LANES), F32)] + _w_in_scratch()
        + [pltpu.VMEM((tm, D_IN), BF16)],
        compiler_params=_params(dimension_semantics=("arbitrary",)),
    )(x, u, dh, pre_g, w_in_full, *tabs, dqa, dka, dva, *dqkv_b[1], *[_perm_view(t, 4) for t in dqkv_b[4]],
      *[_perm_view(t, 16) for t in dqkv_b[16]], dqc, dgate)


class _ReduceScatter:
    def __init__(self, ins, outs, scratch):
        self.n = n = len(ins)
        self.ins, self.outs = ins, outs
        self.mine, self.got, self.snd, self.rcv = (scratch[n * t:n * (t + 1)] for t in range(4))
        self.load_sems, self.d2d_send, self.d2d_recv, self.ici_send, self.ici_recv = scratch[4 * n:]
        self.pos = _mesh_pos()
        self.pairs = [(a, kk) for kk in (3, 1, 2) for a in range(n)]

    @staticmethod
    def scratch_shapes(shapes):
        return ([pltpu.VMEM((4,) + s, F32) for s in shapes] + [pltpu.VMEM((4,) + s, F32) for s in shapes]
                + [pltpu.VMEM((3,) + s, BF16) for s in shapes] + [pltpu.VMEM((3,) + s, BF16) for s in shapes]
                + [pltpu.SemaphoreType.DMA((len(shapes), 4))] * 5)

    def _chip(self, kk):
        x, y, _ = self.pos
        return (1 - x if kk & 2 else x, 1 - y if kk & 1 else y)

    def _load(self, a, kk):
        block = _dev_index((*self._chip(kk), self.pos[2]))
        return pltpu.make_async_copy(self.ins[a].at[block], self.mine[a].at[kk], self.load_sems.at[a, kk])

    def _swap(self, a, kk):
        x, y, c = self.pos
        return pltpu.make_async_remote_copy(
            src_ref=self.ins[a].at[_dev_index((*self._chip(kk), 1 - c))], dst_ref=self.got[a].at[kk],
            send_sem=self.d2d_send.at[a, kk], recv_sem=self.d2d_recv.at[a, kk],
            device_id=(x, y, 1 - c), device_id_type=MESH_ID)

    def _hop(self, a, kk):
        return pltpu.make_async_remote_copy(
            src_ref=self.snd[a].at[kk - 1], dst_ref=self.rcv[a].at[kk - 1], send_sem=self.ici_send.at[a, kk],
            recv_sem=self.ici_recv.at[a, kk], device_id=(*self._chip(kk), self.pos[2]), device_id_type=MESH_ID)

    def start(self):
        for kk in (3, 1, 2, 0):
            for a in range(self.n):
                self._load(a, kk).start()
                self._swap(a, kk).start()

    def send_chip_sums(self):
        for a, kk in self.pairs:
            self._load(a, kk).wait()
            self._swap(a, kk).wait_recv()
            self.snd[a][kk - 1] = (self.mine[a][kk] + self.got[a][kk]).astype(BF16)
            self._hop(a, kk).start()

    def finish(self):
        for a in range(self.n):
            self._load(a, 0).wait()
            self._swap(a, 0).wait_recv()
            acc = self.mine[a][0] + self.got[a][0]
            for kk in (1, 2, 3):
                self._hop(a, kk).wait_recv()
                acc = acc + self.rcv[a][kk - 1].astype(F32)
            self.outs[a][...] = acc
        for kk in range(4):
            for a in range(self.n):
                self._swap(a, kk).wait_send()
        for a, kk in self.pairs:
            self._hop(a, kk).wait_send()


def _local_step(x, mem, pre_g, w_in, sink, mem_g, w_mem, w_out, post_g, target):
    u, *tabs, w_in_full = _prep(x, pre_g, w_in)
    qa, ka, va, qkv_b, qc, gate, w_mem_all, w_out_all = _inproj(u, w_in_full, tabs, w_mem, w_out)
    w_mem_full = w_mem_all.reshape(D_MODEL, 2 * C_W)
    w_out_full = w_out_all.reshape(D_MODEL, D_MODEL)
    mn, mk, mv = _memkv_fwd(mem, mem_g, w_mem_full)

    a_cfg = dict(dil=1, heads=A_HEADS, group=1, max_dist=BLOCK - 1, nq=ATTN_BLOCKS_PER_STEP)
    b_cfgs = {dil: dict(dil=dil, heads=B_HEADS, group=1, max_dist=win // dil, nq=ATTN_BLOCKS_PER_STEP)
              for win, dil in B_CONFIGS}
    oa, lse_a = _banded_fwd(qa, ka, va, sink, name="attn_a_fwd", **a_cfg)
    ob, lse_b = {}, {}
    for dil, cfg in b_cfgs.items():
        ob[dil], lse_b[dil] = _banded_fwd(*qkv_b[dil], None, name=f"attn_b{dil}_fwd", **cfg)
    oc, lse_c = _cross_fwd(qc, mk, mv)

    dh, d_gate, d_a, d_b, d_c, d_wout, st_mid = _mid(oa, lse_a, ob, lse_b, oc, lse_c, gate, x, target, w_out_full, post_g)

    dqc, dmk, dmv = _cross_bwd(qc, mk, mv, *d_c)
    d_wmem, st_mem = _memkv_bwd(mem, mem_g, mn, w_mem_full, dmk, dmv)
    dqkv_b = {dil: _banded_bwd(*qkv_b[dil], *d_b[dil], None, name=f"attn_b{dil}_bwd", **cfg)
              for dil, cfg in b_cfgs.items()}
    dqa, dka, dva, dsink, g_wmem, g_wout = _banded_bwd(
        qa, ka, va, *d_a, sink, name="attn_a_bwd", **a_cfg,
        reduce_scatter=(d_wmem.reshape(N_DEV, SHARD_ROWS, 2 * C_W), d_wout.reshape(N_DEV, SHARD_ROWS, D_MODEL)))

    grad_x, d_win, st_pre = _inproj_bwd(x, u, dh, pre_g, w_in_full, tabs, dqa, dka, dva, dqkv_b, dqc, d_gate)

    dsink_row = jnp.pad(dsink[0:1, :], ((0, 0), (0, D_MODEL - LANES)))
    stats = jnp.concatenate([st_pre[0:1], st_mem[0:1], st_mid[0:1], dsink_row, st_mid[1:2],
                             jnp.zeros((3, D_MODEL), F32)], axis=0)
    return grad_x, d_win, g_wmem, g_wout, stats


def _prep(x, pre_g, w_in, tm=1024):
    seq = x.shape[0]
    n_steps = seq // tm
    parts = 2
    rows = D_MODEL // parts
    relay_at = min(3, n_steps - 1)
    j = jnp.arange(LANES) % HEAD_DIM
    freq = (ROPE_THETA ** (-(2 * (j % (ROT_DIM // 2))).astype(F32) / ROT_DIM))[None, :]
    SIB, NB_X, NB_Y, RELAY, FWD = 0, 1, 2, 3, 4

    def body(x_ref, g_ref, f_ref, win_ref, u_ref, c_ref, up_ref, dn_ref, win_out, win_b,
             send_sems, recv_sems, local_sems):
        step = pl.program_id(0)
        px, py, pc = _mesh_pos()
        me, sibling = (px, py, pc), (px, py, 1 - pc)
        others = lambda core: ((1 - px, py, core), (px, 1 - py, core), (1 - px, 1 - py, core))
        x_nb, y_nb, diag = others(pc)
        relay_from = [x_nb, y_nb]
        relay_to = [y_nb, x_nb]

        def src(a):
            return win_b.at[pl.ds(rows * a, rows)]

        def slot(a, p):
            return win_out.at[_dev_index(p), pl.ds(rows * a, rows)]

        def copy(a, k, block, to, own=False):
            return pltpu.make_async_remote_copy(
                src_ref=src(a) if own else slot(a, block), dst_ref=slot(a, block),
                send_sem=send_sems.at[a, k], recv_sem=recv_sems.at[a, k], device_id=to, device_id_type=MESH_ID)

        def first_sends():
            return [copy(0, NB_X, me, x_nb, own=True), copy(1, NB_Y, me, y_nb, own=True),
                    copy(1, NB_X, me, x_nb, own=True), copy(0, NB_Y, me, y_nb, own=True),
                    copy(0, SIB, me, sibling, own=True), copy(1, SIB, me, sibling, own=True)]

        def relay(a):
            return copy(a, RELAY, relay_from[a], relay_to[a])

        def to_sibling(a, which):
            return copy(a, FWD + which, others(pc)[which], sibling)

        def local(a):
            return pltpu.make_async_copy(src(a), slot(a, me), local_sems.at[a])

        @pl.when(step == 0)
        def _():
            win_b[...] = win_ref[...].astype(BF16)
            for a in range(parts):
                local(a).start()
            for cp in first_sends():
                cp.start()

        @pl.when(step == relay_at)
        def _():
            for a in range(parts):
                copy(a, NB_X + a, relay_from[a], me).wait_recv()
                relay(a).start()
                to_sibling(a, a).start()

        xv = x_ref[...]
        r = lax.rsqrt(jnp.mean(xv * xv, axis=-1, keepdims=True) + RMS_EPS)
        u_ref[...] = ((xv * r) * g_ref[...]).astype(BF16)
        pos = (lax.broadcasted_iota(jnp.int32, (tm, LANES), 0) + step * tm).astype(F32)
        head_lane = lax.broadcasted_iota(jnp.int32, (tm, LANES), 1) % HEAD_DIM
        ang = pos * f_ref[...]
        cos, sin = jnp.cos(ang), jnp.sin(ang)
        half = ROT_DIM // 2
        c_ref[...] = jnp.where(head_lane < ROT_DIM, cos, 1.0)
        up_ref[...] = jnp.where((head_lane >= half) & (head_lane < ROT_DIM), sin, 0.0)
        dn_ref[...] = jnp.where(head_lane < half, -sin, 0.0)

        @pl.when(step == n_steps - 1)
        def _():
            copy(1, NB_X, x_nb, me).wait_recv()
            to_sibling(1, 0).start()
            copy(0, NB_Y, y_nb, me).wait_recv()
            to_sibling(0, 1).start()
            for a in range(parts):
                copy(a, RELAY, diag, me).wait_recv()
                to_sibling(a, 2).start()
            for a in range(parts):
                copy(a, SIB, sibling, me).wait_recv()
                for which in range(3):
                    copy(a, FWD + which, others(1 - pc)[which], me).wait_recv()
            for cp in first_sends():
                cp.wait_send()
            for a in range(parts):
                relay(a).wait_send()
                for which in range(3):
                    to_sibling(a, which).wait_send()
                local(a).wait()

    return pl.pallas_call(
        body, name="prep", grid=(n_steps,),
        in_specs=[_row(tm, D_MODEL), _full((1, D_MODEL)), _full((1, LANES)), _full(w_in.shape)],
        out_specs=[_row(tm, D_MODEL), _row(tm, LANES), _row(tm, LANES), _row(tm, LANES),
                   pl.BlockSpec(memory_space=pl.ANY)],
        out_shape=[jax.ShapeDtypeStruct((seq, D_MODEL), BF16)] + [jax.ShapeDtypeStruct((seq, LANES), F32)] * 3
        + [jax.ShapeDtypeStruct((N_DEV,) + w_in.shape, BF16)],
        scratch_shapes=[pltpu.VMEM(w_in.shape, BF16), pltpu.SemaphoreType.DMA((parts, FWD + 3)),
                        pltpu.SemaphoreType.DMA((parts, FWD + 3)), pltpu.SemaphoreType.DMA((parts,))],
        compiler_params=_params(dimension_semantics=("arbitrary",)),
    )(x, pre_g, freq, w_in)


def _exchange_grads(d_win, stats):
    def body(win, st, g_win, r_st, send_sems, recv_sems, local_sem, *scratch):
        exchange = _ReduceScatter((win,), (g_win,), scratch)
        exchange.start()
        pos = _mesh_pos()
        me = _dev_index(pos)
        own = pltpu.make_async_copy(st, r_st.at[me], local_sem)
        own.start()
        copies = []
        for s in range(1, N_DEV):
            peer = _xor_peer(pos, s)
            mk = lambda slot: pltpu.make_async_remote_copy(
                src_ref=st, dst_ref=r_st.at[slot], send_sem=send_sems.at[s], recv_sem=recv_sems.at[s],
                device_id=peer, device_id_type=MESH_ID)
            send, arrival = mk(me), mk(_dev_index(peer))
            send.start()
            copies.append((send, arrival))
        exchange.send_chip_sums()
        exchange.finish()
        for send, arrival in copies:
            arrival.wait_recv()
            send.wait_send()
        own.wait()

    hbm = pl.BlockSpec(memory_space=pl.ANY)
    shard = d_win.shape[1:]
    return pl.pallas_call(
        body, name="exchange_grads", in_specs=[hbm, hbm],
        out_specs=[pl.BlockSpec(memory_space=pltpu.VMEM), hbm],
        out_shape=[jax.ShapeDtypeStruct(shard, F32), jax.ShapeDtypeStruct((N_DEV,) + stats.shape, F32)],
        scratch_shapes=[pltpu.SemaphoreType.DMA((N_DEV,)), pltpu.SemaphoreType.DMA((N_DEV,)), pltpu.SemaphoreType.DMA(())]
        + _ReduceScatter.scratch_shapes([shard]),
        compiler_params=_params(),
    )(d_win, stats)


WEIGHT_ORDER = ("pre_norm", "w_in", "sink_a", "mem_norm", "w_mem_kv", "w_out", "post_norm")


def _adamw_all(grads, r_stats, weights, moments_m, moments_v):
    n = len(WEIGHT_ORDER)
    stat_row = {"pre_norm": 0, "mem_norm": 1, "post_norm": 2, "sink_a": 3}

    def body(*refs):
        gw_in, gw_mem, gw_out, st_ref = refs[0:4]
        w_refs, m_refs, v_refs = (dict(zip(WEIGHT_ORDER, refs[4 + n * t:4 + n * (t + 1)])) for t in range(3))
        loss_ref = refs[4 + 3 * n]
        outs = refs[5 + 3 * n:]
        g_small = st_ref[0]
        for s in range(1, N_DEV):
            g_small = g_small + st_ref[s]
        loss_ref[...] = g_small[4:5, 0:1]
        big = {"w_in": gw_in, "w_mem_kv": gw_mem, "w_out": gw_out}
        for i, name in enumerate(WEIGHT_ORDER):
            if name in big:
                g = big[name][...]
                at = lambda ref: ref[0]
            else:
                width = w_refs[name].shape[-1]
                g = g_small[stat_row[name]:stat_row[name] + 1, 0:width]
                at = lambda ref: ref[...]
            m2 = ADAM_B1 * at(m_refs[name]) + (1.0 - ADAM_B1) * g
            v2 = ADAM_B2 * at(v_refs[name]) + (1.0 - ADAM_B2) * (g * g)
            m_hat = m2 / (1.0 - ADAM_B1 ** ADAM_STEP)
            v_hat = v2 / (1.0 - ADAM_B2 ** ADAM_STEP)
            delta = -ADAM_LR * (m_hat / (jnp.sqrt(v_hat) + ADAM_EPS) + ADAM_WD * at(w_refs[name]))
            for kind, val in enumerate((g, delta, m2, v2)):
                out = outs[kind * n + i]
                if name in big:
                    out[0] = val
                else:
                    out[...] = val

    shapes = [weights[name].shape for name in WEIGHT_ORDER]
    res = pl.pallas_call(
        body, name="adamw_all",
        out_shape=[jax.ShapeDtypeStruct((1, 1), F32)] + [jax.ShapeDtypeStruct(sh, F32) for sh in shapes] * 4,
        compiler_params=_params(),
    )(grads["w_in"], grads["w_mem_kv"], grads["w_out"], r_stats,
      *[weights[k] for k in WEIGHT_ORDER], *[moments_m[k] for k in WEIGHT_ORDER], *[moments_v[k] for k in WEIGHT_ORDER])
    return res[0].reshape(()), res[1:]


def kernel(x, mem, pre_norm, w_in, sink_a, mem_norm, w_mem_kv, w_out, post_norm, loss_target, m_pre_norm, m_w_in, m_sink_a, m_mem_norm, m_w_mem_kv, m_w_out, m_post_norm, v_pre_norm, v_w_in, v_sink_a, v_mem_norm, v_w_mem_kv, v_w_out, v_post_norm):
    sink = jnp.pad(sink_a[0], (0, 8 - A_HEADS))
    grad_x, d_win, g_wmem, g_wout, stats = _local_step(
        x[0], mem[0], pre_norm, w_in[0], sink, mem_norm, w_mem_kv[0], w_out[0], post_norm, loss_target[0])
    g_win, r_stats = _exchange_grads(d_win, stats)
    weights = dict(pre_norm=pre_norm, w_in=w_in, sink_a=sink_a, mem_norm=mem_norm, w_mem_kv=w_mem_kv, w_out=w_out,
                   post_norm=post_norm)
    moments_m = dict(pre_norm=m_pre_norm, w_in=m_w_in, sink_a=m_sink_a, mem_norm=m_mem_norm, w_mem_kv=m_w_mem_kv,
                     w_out=m_w_out, post_norm=m_post_norm)
    moments_v = dict(pre_norm=v_pre_norm, w_in=v_w_in, sink_a=v_sink_a, mem_norm=v_mem_norm, w_mem_kv=v_w_mem_kv,
                     w_out=v_w_out, post_norm=v_post_norm)
    loss, rest = _adamw_all(dict(w_in=g_win, w_mem_kv=g_wmem, w_out=g_wout), r_stats, weights, moments_m, moments_v)
    return (loss, grad_x[None], *rest)
```

```python
import jax
import jax.numpy as jnp
from jax import lax
from jax.experimental import pallas as pl
from jax.experimental.pallas import tpu as pltpu

F32 = jnp.float32
BF16 = jnp.bfloat16

D_MODEL = 1024
HEAD_DIM = 64
ROT_DIM = 16
ROPE_THETA = 500000.0
BLOCK = 128
LANES = 128
N_MEM = 256
RMS_EPS = 1e-6
SCALE = HEAD_DIM ** -0.5
A_HEADS = 6
B_HEADS = 6
C_HEADS = 4
A_W, A_KV_W, B_W, C_W = 384, 128, 384, 256
_IN_PIECES = (("qa", A_W), ("ka", A_KV_W), ("va", A_KV_W), ("ga", A_W), ("qb", B_W), ("kb", B_W), ("vb", B_W),
              ("gb", B_W), ("qc", C_W), ("gc", C_W))
COLS, D_IN = {}, 0
for _name, _width in _IN_PIECES:
    COLS[_name] = (D_IN, D_IN + _width)
    D_IN += _width
N_DEV = 8
SHARD_IN = D_IN // N_DEV
SHARD_ROWS = D_MODEL // N_DEV
B_CONFIGS = ((128, 1), (512, 4), (2048, 16))
DILS = (4, 16)
NEG = -1e30
ATTN_BLOCKS_PER_STEP = 4
DELTA_LANE = 64
VMEM_LIMIT = 56 * 1024 * 1024

ADAM_LR, ADAM_B1, ADAM_B2, ADAM_EPS, ADAM_WD, ADAM_STEP = 0.001, 0.9, 0.999, 1e-08, 0.01, 10
MESH_ID = pl.DeviceIdType.MESH


def _params(**kw):
    return pltpu.CompilerParams(vmem_limit_bytes=VMEM_LIMIT, **kw)


def _full(shape):
    n = len(shape)
    return pl.BlockSpec(shape, lambda *_: (0,) * n)


def _row(tm, w):
    return pl.BlockSpec((tm, w), lambda i: (i, 0))


def _mesh_pos():
    return lax.axis_index("x"), lax.axis_index("y"), lax.axis_index("c")


def _dev_index(pos):
    return 4 * pos[0] + 2 * pos[1] + pos[2]


def _xor_peer(pos, s):
    x, y, c = pos
    return (1 - x if s & 4 else x, 1 - y if s & 2 else y, 1 - c if s & 1 else c)


def _perm_view(a, dil):
    return a.reshape(a.shape[0] // (BLOCK * dil), dil, BLOCK, a.shape[1])


def _perm_spec(tm, dil, w):
    chunk = BLOCK * dil
    if tm >= chunk:
        return pl.BlockSpec((tm // chunk, dil, BLOCK, w), lambda i: (i, 0, 0, 0))
    per = chunk // tm
    return pl.BlockSpec((1, dil, tm // dil, w), lambda i: (i // per, 0, i % per, 0))


def _put(scr, val):
    for c in range(val.shape[1] // LANES):
        scr[c] = val[:, LANES * c:LANES * (c + 1)]


def _get(scr):
    n = scr.shape[0]
    return scr[0] if n == 1 else jnp.concatenate([scr[c] for c in range(n)], axis=1)


def _get_class(scr, r, dil):
    n, rows = scr.shape[0], scr.shape[1]
    parts = [scr.at[c][pl.ds(r, rows // dil, stride=dil), :] for c in range(n)]
    return parts[0] if n == 1 else jnp.concatenate(parts, axis=1)


def _store_permuted(scr, out_ref, dil, dtype):
    for r in range(dil):
        out_ref[0, r] = _get_class(scr, r, dil).astype(dtype)


def _load_permuted(in_ref, scr, dil):
    n, rows = scr.shape[0], scr.shape[1]
    for r in range(dil):
        val = in_ref[0, r].astype(F32)
        for c in range(n):
            scr.at[c][pl.ds(r, rows // dil, stride=dil), :] = val[:, LANES * c:LANES * (c + 1)]
    return _get(scr)


def _rotate128(t, c, up, dn):
    half = ROT_DIM // 2
    return t * c + pltpu.roll(t, half, 1) * up + pltpu.roll(t, LANES - half, 1) * dn


def _rotate(t, c, up, dn):
    outs = [_rotate128(t[:, LANES * j:LANES * (j + 1)], c, up, dn) for j in range(t.shape[1] // LANES)]
    return outs[0] if len(outs) == 1 else jnp.concatenate(outs, axis=1)


def _per_query_head(kv):
    lane = lax.broadcasted_iota(jnp.int32, kv.shape, 1)
    other = pltpu.roll(kv, HEAD_DIM, 1)
    return jnp.concatenate([jnp.where(lane < HEAD_DIM, kv, other), kv, jnp.where(lane < HEAD_DIM, other, kv)], axis=1)


def _per_kv_head(d):
    s0, s1, s2 = (d[:, LANES * p:LANES * (p + 1)] for p in range(3))
    lane = lax.broadcasted_iota(jnp.int32, s0.shape, 1)
    return jnp.where(lane < HEAD_DIM, s0 + pltpu.roll(s0, HEAD_DIM, 1) + s1, s1 + s2 + pltpu.roll(s2, HEAD_DIM, 1))


def _w_in_scratch():
    return [pltpu.VMEM((D_MODEL, D_IN), BF16), pltpu.SemaphoreType.DMA((N_DEV,))]


def _stage_w_in(w_hbm, w_scr, sems):
    @pl.when(pl.program_id(0) == 0)
    def _():
        copies = [pltpu.make_async_copy(w_hbm.at[k], w_scr.at[:, pl.ds(SHARD_IN * k, SHARD_IN)], sems.at[k])
                  for k in range(N_DEV)]
        for cp in copies:
            cp.start()
        for cp in copies:
            cp.wait()


def _inproj(u, w_in_full, tabs, w_mem, w_out, tm=1024):
    seq = u.shape[0]
    n_chunk = D_IN // LANES
    n_steps = seq // tm

    def body(u_ref, w_hbm, c_ref, up_ref, dn_ref, wm_ref, wo_ref, qa_ref, ka_ref, va_ref,
             qb1_ref, kb1_ref, vb1_ref, qb4_ref, kb4_ref, vb4_ref, qb16_ref, kb16_ref, vb16_ref,
             qc_ref, gate_ref, wm_all, wo_all, proj, w_scr, w_sems, wm_b, wo_b, send_sems, recv_sems, local_sems):
        step = pl.program_id(0)
        shards, gathered = (wm_b, wo_b), (wm_all, wo_all)

        def gather_copies(arriving):
            pos = _mesh_pos()
            me = _dev_index(pos)
            local = [] if arriving else [
                pltpu.make_async_copy(shards[a], gathered[a].at[me], local_sems.at[a]) for a in range(2)]
            remote = []
            for s in range(1, N_DEV):
                peer = _xor_peer(pos, s)
                for a in range(2):
                    remote.append(pltpu.make_async_remote_copy(
                        src_ref=shards[a], dst_ref=gathered[a].at[_dev_index(peer) if arriving else me],
                        send_sem=send_sems.at[a, s], recv_sem=recv_sems.at[a, s], device_id=peer,
                        device_id_type=MESH_ID))
            return local, remote

        @pl.when(step == 0)
        def _():
            wm_b[...] = wm_ref[...].astype(BF16)
            wo_b[...] = wo_ref[...].astype(BF16)
            local, sends = gather_copies(arriving=False)
            for cp in local + sends:
                cp.start()

        _stage_w_in(w_hbm, w_scr, w_sems)
        u = u_ref[...]
        for n0 in range(0, D_IN, D_MODEL):
            acc = jnp.dot(u, w_scr[:, n0:n0 + D_MODEL], preferred_element_type=F32)
            for c3 in range(D_MODEL // LANES):
                proj[n0 // LANES + c3] = acc[:, LANES * c3:LANES * (c3 + 1)]
        c, up, dn = c_ref[...], up_ref[...], dn_ref[...]

        def chunks_of(piece):
            lo, hi = COLS[piece]
            return range(lo // LANES, hi // LANES)

        def cols(piece, rot=False, scale=None):
            parts = []
            for ch in chunks_of(piece):
                t = proj[ch]
                if rot:
                    t = _rotate128(t, c, up, dn)
                if scale is not None:
                    t = t * scale
                parts.append(t)
            return parts[0] if len(parts) == 1 else jnp.concatenate(parts, axis=1)

        qa_ref[...] = cols("qa", True, SCALE).astype(BF16)
        ka_ref[...] = _per_query_head(cols("ka", True)).astype(BF16)
        va_ref[...] = _per_query_head(cols("va")).astype(BF16)
        gate_ref[:, 0:A_W] = cols("ga").astype(BF16)
        gate_ref[:, A_W:A_W + B_W] = cols("gb").astype(BF16)
        gate_ref[:, A_W + B_W:D_MODEL] = cols("gc").astype(BF16)
        qc_ref[...] = cols("qc", False, SCALE).astype(BF16)
        for ch in chunks_of("qb"):
            proj[ch] = _rotate128(proj[ch], c, up, dn) * SCALE
        for ch in chunks_of("kb"):
            proj[ch] = _rotate128(proj[ch], c, up, dn)
        for piece, nat, p4, p16 in (("qb", qb1_ref, qb4_ref, qb16_ref), ("kb", kb1_ref, kb4_ref, kb16_ref),
                                    ("vb", vb1_ref, vb4_ref, vb16_ref)):
            chunks = chunks_of(piece)
            nat[...] = jnp.concatenate([proj[ch] for ch in chunks], axis=1).astype(BF16)
            for dil, ref in ((4, p4), (16, p16)):
                span = min(tm, BLOCK * dil)
                for cc in range(tm // span):
                    for rr in range(dil):
                        ref[cc, rr] = jnp.concatenate(
                            [proj.at[ch][pl.ds(cc * span + rr, span // dil, stride=dil), :] for ch in chunks],
                            axis=1).astype(BF16)

        @pl.when(step == n_steps - 1)
        def _():
            for cp in gather_copies(arriving=True)[1]:
                cp.wait_recv()
            local, sends = gather_copies(arriving=False)
            for cp in sends:
                cp.wait_send()
            for cp in local:
                cp.wait()

    nat_w = (A_W, A_W, A_W, B_W, B_W, B_W)
    out_specs = [_row(tm, w) for w in nat_w]
    out_shape = [jax.ShapeDtypeStruct((seq, w), BF16) for w in nat_w]
    for dil in DILS:
        out_specs += [_perm_spec(tm, dil, B_W)] * 3
        out_shape += [jax.ShapeDtypeStruct((seq // (BLOCK * dil), dil, BLOCK, B_W), BF16)] * 3
    hbm = pl.BlockSpec(memory_space=pl.ANY)
    out_specs += [_row(tm, C_W), _row(tm, D_MODEL), hbm, hbm]
    out_shape += [jax.ShapeDtypeStruct((seq, C_W), BF16), jax.ShapeDtypeStruct((seq, D_MODEL), BF16),
                  jax.ShapeDtypeStruct((N_DEV,) + w_mem.shape, BF16), jax.ShapeDtypeStruct((N_DEV,) + w_out.shape, BF16)]
    res = pl.pallas_call(
        body, name="inproj", grid=(n_steps,),
        in_specs=[_row(tm, D_MODEL), hbm, _row(tm, LANES), _row(tm, LANES), _row(tm, LANES),
                  _full(w_mem.shape), _full(w_out.shape)],
        out_specs=out_specs, out_shape=out_shape,
        scratch_shapes=[pltpu.VMEM((n_chunk, tm, LANES), F32)] + _w_in_scratch()
        + [pltpu.VMEM(w_mem.shape, BF16), pltpu.VMEM(w_out.shape, BF16), pltpu.SemaphoreType.DMA((2, N_DEV)),
           pltpu.SemaphoreType.DMA((2, N_DEV)), pltpu.SemaphoreType.DMA((2,))],
        compiler_params=_params(dimension_semantics=("arbitrary",)),
    )(u, w_in_full, *tabs, w_mem, w_out)
    qa, ka, va = res[0:3]
    qkv_b = {1: res[3:6], 4: [t.reshape(seq, B_W) for t in res[6:9]], 16: [t.reshape(seq, B_W) for t in res[9:12]]}
    return qa, ka, va, qkv_b, res[12], res[13], res[14], res[15]


def _memkv_fwd(mem, mem_g, w_mem_full):
    def body(mem_ref, g_ref, w_ref, mn_ref, mk_ref, mv_ref):
        mv_ = mem_ref[...]
        r = lax.rsqrt(jnp.mean(mv_ * mv_, axis=-1, keepdims=True) + RMS_EPS)
        mn = ((mv_ * r) * g_ref[...]).astype(BF16)
        mn_ref[...] = mn
        mkv = jnp.dot(mn, w_ref[...], preferred_element_type=F32)
        mk_ref[...] = mkv[:, 0:C_W].astype(BF16)
        mv_ref[...] = mkv[:, C_W:2 * C_W].astype(BF16)

    return pl.pallas_call(
        body, name="memkv_fwd",
        out_shape=[jax.ShapeDtypeStruct((N_MEM, D_MODEL), BF16),
                   jax.ShapeDtypeStruct((N_MEM, C_W), BF16), jax.ShapeDtypeStruct((N_MEM, C_W), BF16)],
        compiler_params=_params(),
    )(mem, mem_g, w_mem_full)


def _memkv_bwd(mem, mem_g, mn, w_mem_full, dmk, dmv):
    def body(mem_ref, g_ref, mn_ref, w_ref, dmk_ref, dmv_ref, dw_ref, st_ref):
        dmkv = jnp.concatenate([dmk_ref[...], dmv_ref[...]], axis=1).astype(BF16)
        dw_ref[...] = lax.dot_general(mn_ref[...], dmkv, (((0,), (0,)), ((), ())), preferred_element_type=F32)
        dmn = lax.dot_general(dmkv, w_ref[...], (((1,), (1,)), ((), ())), preferred_element_type=F32)
        mv_ = mem_ref[...]
        r = lax.rsqrt(jnp.mean(mv_ * mv_, axis=-1, keepdims=True) + RMS_EPS)
        st_ref[...] = jnp.zeros_like(st_ref)
        st_ref[0:1, :] = jnp.sum(dmn * (mv_ * r), axis=0, keepdims=True)

    return pl.pallas_call(
        body, name="memkv_bwd",
        out_shape=[jax.ShapeDtypeStruct((D_MODEL, 2 * C_W), F32), jax.ShapeDtypeStruct((8, D_MODEL), F32)],
        compiler_params=_params(),
    )(mem, mem_g, mn, w_mem_full, dmk, dmv)


def _band_mask(has_prev, max_dist):
    qi = lax.broadcasted_iota(jnp.int32, (BLOCK, 2 * BLOCK), 0)
    kj = lax.broadcasted_iota(jnp.int32, (BLOCK, 2 * BLOCK), 1)
    dist = qi + BLOCK - kj
    return (dist >= 0) & (dist <= max_dist) & ((kj >= BLOCK) | has_prev)


_NT = (((1,), (1,)), ((), ()))
_TN = (((0,), (0,)), ((), ()))


def _head_only(val, h):
    slab = _slabs_of(val)(h)
    lane = lax.broadcasted_iota(jnp.int32, slab.shape, 1)
    keep = (lane < HEAD_DIM) if h % 2 == 0 else (lane >= HEAD_DIM)
    return jnp.where(keep, slab, jnp.zeros((), slab.dtype))


def _slabs_of(val):
    return lambda h: val[:, LANES * (h // 2):LANES * (h // 2 + 1)]


class _BandSteps:
    def __init__(self, seq, dil, nq):
        self.nq, self.rows, self.consecutive = nq, nq * BLOCK, dil == 1
        nb = seq // dil // BLOCK
        if self.consecutive:
            assert nb % nq == 0
            self.outer, self.inner, self.stride = 1, nb // nq, 1
        else:
            assert dil % nq == 0
            self.outer, self.inner, self.stride = dil // nq, nb, dil // nq

    def own(self, w, clamp=False):
        cur = (lambda i: jnp.minimum(i, self.inner - 1)) if clamp else (lambda i: i)
        return pl.BlockSpec((self.rows, w), lambda r, i: (cur(i) * self.stride + r, 0))

    def prev(self, w, clamp=False):
        cur = (lambda i: jnp.minimum(i, self.inner - 1)) if clamp else (lambda i: i)
        if self.consecutive:
            return pl.BlockSpec((BLOCK, w), lambda r, i: (jnp.maximum(cur(i) * self.nq - 1, 0), 0))
        return pl.BlockSpec((self.rows, w), lambda r, i: (jnp.maximum(cur(i) - 1, 0) * self.stride + r, 0))

    def late(self, w):
        return pl.BlockSpec((self.rows, w), lambda r, i: (jnp.maximum(i - 1, 0) * self.stride + r, 0))

    def rows_of(self, j):
        return slice(BLOCK * j, BLOCK * (j + 1))

    def keys(self, p_ref, c_ref, j):
        if not self.consecutive:
            before = p_ref[self.rows_of(j), :]
        elif j == 0:
            before = p_ref[...]
        else:
            before = c_ref[self.rows_of(j - 1), :]
        return jnp.concatenate([before, c_ref[self.rows_of(j), :]], axis=0)

    def has_prev(self, i, j):
        return True if (self.consecutive and j > 0) else (i > 0)


def _banded_fwd(q, k, v, sink, *, dil, heads, max_dist, nq, name):
    seq = q.shape[0]
    qw = kw = heads * HEAD_DIM
    steps = _BandSteps(seq, dil, nq)

    def body(*refs):
        if sink is not None:
            sink_ref, refs = refs[0], refs[1:]
        q_ref, kp_ref, kc_ref, vp_ref, vc_ref, o_ref, lse_ref, s_scr, p_scr = refs
        i = pl.program_id(1)
        lane = lax.broadcasted_iota(jnp.int32, (BLOCK, LANES), 1)
        k_of = [_slabs_of(steps.keys(kp_ref, kc_ref, j)) for j in range(nq)]
        v_of = [_slabs_of(steps.keys(vp_ref, vc_ref, j)) for j in range(nq)]
        for j in range(nq):
            qv = q_ref[steps.rows_of(j), :]
            for h in range(heads):
                s_scr[j * heads + h] = lax.dot_general(_head_only(qv, h), k_of[j](h), _NT, preferred_element_type=F32)
        ls = {}
        for j in range(nq):
            valid = _band_mask(steps.has_prev(i, j), max_dist)
            lse_tile = jnp.zeros((BLOCK, LANES), F32)
            for h in range(heads):
                s = jnp.where(valid, s_scr[j * heads + h], NEG)
                m = jnp.max(s, axis=-1, keepdims=True)
                if sink is not None:
                    sk = sink_ref[h]
                    m = jnp.maximum(m, sk)
                p = jnp.exp(s - m)
                l = jnp.sum(p, axis=-1, keepdims=True)
                if sink is not None:
                    l = l + jnp.exp(sk - m)
                p_scr[j * heads + h] = p.astype(BF16)
                ls[j, h] = l
                lse_tile = jnp.where(lane == h, m + jnp.log(l), lse_tile)
            lse_ref[steps.rows_of(j), :] = lse_tile
        for j in range(nq):
            for pr in range(heads // 2):
                he, ho = 2 * pr, 2 * pr + 1
                even = jnp.dot(p_scr[j * heads + he], v_of[j](he), preferred_element_type=F32) / ls[j, he]
                odd = jnp.dot(p_scr[j * heads + ho], v_of[j](ho), preferred_element_type=F32) / ls[j, ho]
                o_ref[steps.rows_of(j), LANES * pr:LANES * (pr + 1)] = jnp.where(lane < HEAD_DIM, even, odd).astype(BF16)

    in_specs = [steps.own(qw), steps.prev(kw), steps.own(kw), steps.prev(kw), steps.own(kw)]
    args = [q, k, k, v, v]
    if sink is not None:
        in_specs = [pl.BlockSpec(memory_space=pltpu.SMEM)] + in_specs
        args = [sink] + args
    return pl.pallas_call(
        body, name=name, grid=(steps.outer, steps.inner), in_specs=in_specs,
        out_specs=[steps.own(qw), steps.own(LANES)],
        out_shape=[jax.ShapeDtypeStruct((seq, qw), BF16), jax.ShapeDtypeStruct((seq, LANES), F32)],
        scratch_shapes=[pltpu.VMEM((nq * heads, BLOCK, 2 * BLOCK), F32), pltpu.VMEM((nq * heads, BLOCK, 2 * BLOCK), BF16)],
        compiler_params=_params(dimension_semantics=("arbitrary", "arbitrary")),
    )(*args)


def _banded_bwd(q, k, v, d_out, stat, sink, *, dil, heads, max_dist, nq, name, reduce_scatter=()):
    seq = q.shape[0]
    qw = kw = heads * HEAD_DIM
    steps = _BandSteps(seq, dil, nq)
    n_rs = len(reduce_scatter)
    n_in = 7 + n_rs
    n_flat = steps.outer * (steps.inner + 1)

    def body(*refs):
        refs = list(refs)
        sink_ref = refs.pop(0) if sink is not None else None
        (q_ref, kp_ref, kc_ref, vp_ref, vc_ref, do_ref, st_ref), partials = refs[:7], refs[7:n_in]
        refs = refs[n_in:]
        dsink_ref = refs.pop(0) if sink is not None else None
        (dq_ref, dk_ref, dv_ref), sums = refs[:3], refs[3:3 + n_rs]
        kcar, vcar, s_scr, dp_scr, p_scr, ds_scr = refs[3 + n_rs:9 + n_rs]
        r, i = pl.program_id(0), pl.program_id(1)
        if n_rs:
            exchange = _ReduceScatter(tuple(partials), tuple(sums), refs[9 + n_rs:])
            flat = r * (steps.inner + 1) + i

            @pl.when(flat == 0)
            def _():
                exchange.start()

            @pl.when(flat == min(2, n_flat - 1))
            def _():
                exchange.send_chip_sums()

        @pl.when(i == 0)
        def _():
            kcar[...] = jnp.zeros_like(kcar)
            vcar[...] = jnp.zeros_like(vcar)

        if sink is not None:
            @pl.when((i == 0) & (r == 0))
            def _():
                dsink_ref[...] = jnp.zeros_like(dsink_ref)

        @pl.when(i < steps.inner)
        def _():
            lane = lax.broadcasted_iota(jnp.int32, (1, LANES), 1)
            lane_q = lax.broadcasted_iota(jnp.int32, (BLOCK, LANES), 1)
            k_of = [_slabs_of(steps.keys(kp_ref, kc_ref, j)) for j in range(nq)]
            v_of = [_slabs_of(steps.keys(vp_ref, vc_ref, j)) for j in range(nq)]
            qms, doms = {}, {}
            for j in range(nq):
                qv, dov = q_ref[steps.rows_of(j), :], do_ref[steps.rows_of(j), :]
                for h in range(heads):
                    qms[j, h], doms[j, h] = _head_only(qv, h), _head_only(dov, h)
                    s_scr[j * heads + h] = lax.dot_general(qms[j, h], k_of[j](h), _NT, preferred_element_type=F32)
                    dp_scr[j * heads + h] = lax.dot_general(doms[j, h], v_of[j](h), _NT, preferred_element_type=F32)
            dsink_row = jnp.zeros((1, LANES), F32)
            for j in range(nq):
                st = st_ref[steps.rows_of(j), :]
                valid = _band_mask(steps.has_prev(i, j), max_dist)
                for h in range(heads):
                    lse_h = st[:, h:h + 1]
                    delta = st[:, DELTA_LANE + h:DELTA_LANE + h + 1]
                    p = jnp.where(valid, jnp.exp(s_scr[j * heads + h] - lse_h), 0.0)
                    p_scr[j * heads + h] = p.astype(BF16)
                    ds_scr[j * heads + h] = (p * (dp_scr[j * heads + h] - delta)).astype(BF16)
                    if sink is not None:
                        ds_sink = jnp.sum(-jnp.exp(sink_ref[h] - lse_h) * delta, axis=0, keepdims=True)
                        dsink_row = dsink_row + jnp.where(lane == h, ds_sink, 0.0)
            for j in range(nq):
                for pr in range(heads // 2):
                    he, ho = 2 * pr, 2 * pr + 1
                    even = jnp.dot(ds_scr[j * heads + he], k_of[j](he), preferred_element_type=F32)
                    odd = jnp.dot(ds_scr[j * heads + ho], k_of[j](ho), preferred_element_type=F32)
                    dq_ref[steps.rows_of(j), LANES * pr:LANES * (pr + 1)] = (
                        jnp.where(lane_q < HEAD_DIM, even, odd).astype(BF16))
            if steps.consecutive:
                dk_ref[...] = kcar[...].astype(BF16)
                dv_ref[...] = vcar[...].astype(BF16)
            for j in range(nq):
                for slab in range(kw // LANES):
                    he, ho = j * heads + 2 * slab, j * heads + 2 * slab + 1
                    dk_j = (lax.dot_general(ds_scr[he], qms[j, 2 * slab], _TN, preferred_element_type=F32)
                            + lax.dot_general(ds_scr[ho], qms[j, 2 * slab + 1], _TN, preferred_element_type=F32))
                    dv_j = (lax.dot_general(p_scr[he], doms[j, 2 * slab], _TN, preferred_element_type=F32)
                            + lax.dot_general(p_scr[ho], doms[j, 2 * slab + 1], _TN, preferred_element_type=F32))
                    sl = slice(LANES * slab, LANES * (slab + 1))
                    own_rows = steps.rows_of(j)
                    if not steps.consecutive:
                        dk_ref[own_rows, sl] = (kcar[own_rows, sl] + dk_j[0:BLOCK]).astype(BF16)
                        dv_ref[own_rows, sl] = (vcar[own_rows, sl] + dv_j[0:BLOCK]).astype(BF16)
                    elif j == 0:
                        last = steps.rows_of(nq - 1)
                        dk_ref[last, sl] = (kcar[last, sl] + dk_j[0:BLOCK]).astype(BF16)
                        dv_ref[last, sl] = (vcar[last, sl] + dv_j[0:BLOCK]).astype(BF16)
                    else:
                        before = steps.rows_of(j - 1)
                        kcar[before, sl] += dk_j[0:BLOCK]
                        vcar[before, sl] += dv_j[0:BLOCK]
                    kcar[own_rows, sl] = dk_j[BLOCK:2 * BLOCK]
                    vcar[own_rows, sl] = dv_j[BLOCK:2 * BLOCK]
            if sink is not None:
                dsink_ref[0:1, :] += dsink_row

        @pl.when(i == steps.inner)
        def _():
            dk_ref[...] = kcar[...].astype(BF16)
            dv_ref[...] = vcar[...].astype(BF16)

        if n_rs:
            @pl.when(flat == n_flat - 1)
            def _():
                exchange.finish()

    own, prev = (lambda w: steps.own(w, clamp=True)), (lambda w: steps.prev(w, clamp=True))
    rs_shapes = [t.shape[1:] for t in reduce_scatter]
    in_specs = ([own(qw), prev(kw), own(kw), prev(kw), own(kw), own(qw), own(LANES)]
                + [pl.BlockSpec(memory_space=pl.ANY)] * n_rs)
    args = [q, k, k, v, v, d_out, stat, *reduce_scatter]
    out_specs = [own(qw), steps.late(kw), steps.late(kw)] + [_full(s) for s in rs_shapes]
    out_shape = [jax.ShapeDtypeStruct((seq, qw), BF16), jax.ShapeDtypeStruct((seq, kw), BF16),
                 jax.ShapeDtypeStruct((seq, kw), BF16)] + [jax.ShapeDtypeStruct(s, F32) for s in rs_shapes]
    if sink is not None:
        in_specs = [pl.BlockSpec(memory_space=pltpu.SMEM)] + in_specs
        args = [sink] + args
        out_specs = [_full((8, LANES))] + out_specs
        out_shape = [jax.ShapeDtypeStruct((8, LANES), F32)] + out_shape
    n_hb = nq * heads
    res = pl.pallas_call(
        body, name=name, grid=(steps.outer, steps.inner + 1), in_specs=in_specs, out_specs=out_specs,
        out_shape=out_shape,
        scratch_shapes=[pltpu.VMEM((steps.rows, kw), F32), pltpu.VMEM((steps.rows, kw), F32)]
        + [pltpu.VMEM((n_hb, BLOCK, 2 * BLOCK), F32)] * 2 + [pltpu.VMEM((n_hb, BLOCK, 2 * BLOCK), BF16)] * 2
        + (_ReduceScatter.scratch_shapes(rs_shapes) if n_rs else []),
        compiler_params=_params(dimension_semantics=("arbitrary", "arbitrary")),
    )(*args)
    if sink is not None:
        return (*res[1:4], res[0], *res[4:])
    return res


def _cross_fwd(q, mk, mv, tq=1024):
    seq = q.shape[0]

    def body(q_ref, mk_ref, mv_ref, o_ref, lse_ref, s_scr, p_scr):
        qv = q_ref[...]
        k_of, v_of = _slabs_of(mk_ref[...]), _slabs_of(mv_ref[...])
        lane = lax.broadcasted_iota(jnp.int32, (tq, LANES), 1)
        lse_tile = jnp.zeros((tq, LANES), F32)
        for h in range(C_HEADS):
            s_scr[h] = lax.dot_general(_head_only(qv, h), k_of(h), _NT, preferred_element_type=F32)
        ls = []
        for h in range(C_HEADS):
            s = s_scr[h]
            m = jnp.max(s, axis=-1, keepdims=True)
            p = jnp.exp(s - m)
            l = jnp.sum(p, axis=-1, keepdims=True)
            p_scr[h] = p.astype(BF16)
            ls.append(l)
            lse_tile = jnp.where(lane == h, m + jnp.log(l), lse_tile)
        for pr in range(C_HEADS // 2):
            even = jnp.dot(p_scr[2 * pr], v_of(2 * pr), preferred_element_type=F32) / ls[2 * pr]
            odd = jnp.dot(p_scr[2 * pr + 1], v_of(2 * pr + 1), preferred_element_type=F32) / ls[2 * pr + 1]
            o_ref[:, LANES * pr:LANES * (pr + 1)] = jnp.where(lane < HEAD_DIM, even, odd).astype(BF16)
        lse_ref[...] = lse_tile

    return pl.pallas_call(
        body, name="cross_fwd", grid=(seq // tq,),
        in_specs=[_row(tq, C_W), _full((N_MEM, C_W)), _full((N_MEM, C_W))],
        out_specs=[_row(tq, C_W), _row(tq, LANES)],
        out_shape=[jax.ShapeDtypeStruct((seq, C_W), BF16), jax.ShapeDtypeStruct((seq, LANES), F32)],
        scratch_shapes=[pltpu.VMEM((C_HEADS, tq, N_MEM), F32), pltpu.VMEM((C_HEADS, tq, N_MEM), BF16)],
        compiler_params=_params(dimension_semantics=("arbitrary",)),
    )(q, mk, mv)


def _cross_bwd(q, mk, mv, d_out, stat, tq=1024):
    seq = q.shape[0]

    def body(q_ref, mk_ref, mv_ref, do_ref, st_ref, dq_ref, dmk_ref, dmv_ref, s_scr, dp_scr, p_scr, ds_scr):
        @pl.when(pl.program_id(0) == 0)
        def _():
            dmk_ref[...] = jnp.zeros_like(dmk_ref)
            dmv_ref[...] = jnp.zeros_like(dmv_ref)

        qv, dov, st = q_ref[...], do_ref[...], st_ref[...]
        k_of, v_of = _slabs_of(mk_ref[...]), _slabs_of(mv_ref[...])
        qms = [_head_only(qv, h) for h in range(C_HEADS)]
        doms = [_head_only(dov, h) for h in range(C_HEADS)]
        for h in range(C_HEADS):
            s_scr[h] = lax.dot_general(qms[h], k_of(h), _NT, preferred_element_type=F32)
            dp_scr[h] = lax.dot_general(doms[h], v_of(h), _NT, preferred_element_type=F32)
        for h in range(C_HEADS):
            p = jnp.exp(s_scr[h] - st[:, h:h + 1])
            p_scr[h] = p.astype(BF16)
            ds_scr[h] = (p * (dp_scr[h] - st[:, DELTA_LANE + h:DELTA_LANE + h + 1])).astype(BF16)
        lane = lax.broadcasted_iota(jnp.int32, (tq, LANES), 1)
        for pr in range(C_HEADS // 2):
            sl = slice(LANES * pr, LANES * (pr + 1))
            even = jnp.dot(ds_scr[2 * pr], k_of(2 * pr), preferred_element_type=F32)
            odd = jnp.dot(ds_scr[2 * pr + 1], k_of(2 * pr + 1), preferred_element_type=F32)
            dq_ref[:, sl] = jnp.where(lane < HEAD_DIM, even, odd).astype(BF16)
            dmk_ref[:, sl] += (lax.dot_general(ds_scr[2 * pr], qms[2 * pr], _TN, preferred_element_type=F32)
                               + lax.dot_general(ds_scr[2 * pr + 1], qms[2 * pr + 1], _TN, preferred_element_type=F32))
            dmv_ref[:, sl] += (lax.dot_general(p_scr[2 * pr], doms[2 * pr], _TN, preferred_element_type=F32)
                               + lax.dot_general(p_scr[2 * pr + 1], doms[2 * pr + 1], _TN, preferred_element_type=F32))

    return pl.pallas_call(
        body, name="cross_bwd", grid=(seq // tq,),
        in_specs=[_row(tq, C_W), _full((N_MEM, C_W)), _full((N_MEM, C_W)), _row(tq, C_W), _row(tq, LANES)],
        out_specs=[_row(tq, C_W), _full((N_MEM, C_W)), _full((N_MEM, C_W))],
        out_shape=[jax.ShapeDtypeStruct((seq, C_W), BF16), jax.ShapeDtypeStruct((N_MEM, C_W), F32),
                   jax.ShapeDtypeStruct((N_MEM, C_W), F32)],
        scratch_shapes=[pltpu.VMEM((C_HEADS, tq, N_MEM), F32)] * 2 + [pltpu.VMEM((C_HEADS, tq, N_MEM), BF16)] * 2,
        compiler_params=_params(dimension_semantics=("arbitrary",)),
    )(q, mk, mv, d_out, stat)


def _per_head(tile, width):
    rows = tile.shape[0]
    lane = lax.broadcasted_iota(jnp.int32, (rows, LANES), 1)
    slabs = []
    for p in range(width // LANES):
        even = jnp.broadcast_to(tile[:, 2 * p:2 * p + 1], (rows, LANES))
        odd = jnp.broadcast_to(tile[:, 2 * p + 1:2 * p + 2], (rows, LANES))
        slabs.append(jnp.where(lane < HEAD_DIM, even, odd))
    return slabs[0] if len(slabs) == 1 else jnp.concatenate(slabs, axis=1)


def _with_delta(lse_tile, prod):
    rows = lse_tile.shape[0]
    lane = lax.broadcasted_iota(jnp.int32, (rows, LANES), 1)
    tile = lse_tile
    for p in range(prod.shape[1] // LANES):
        slab = prod[:, LANES * p:LANES * (p + 1)]
        even = jnp.sum(jnp.where(lane < HEAD_DIM, slab, 0.0), axis=-1, keepdims=True)
        odd = jnp.sum(jnp.where(lane >= HEAD_DIM, slab, 0.0), axis=-1, keepdims=True)
        tile = jnp.where(lane == DELTA_LANE + 2 * p, even, tile)
        tile = jnp.where(lane == DELTA_LANE + 2 * p + 1, odd, tile)
    return tile


def _mid(oa, lse_a, ob, lse_b, oc, lse_c, gate, x, target, w_out_full, post_g, tm=512):
    seq = x.shape[0]
    n_b = B_W // LANES

    def body(oa_ref, la_ref, b1_ref, l1_ref, b4_ref, l4_ref, b16_ref, l16_ref, oc_ref, lc_ref,
             gate_ref, x_ref, t_ref, w_ref, pg_ref,
             dh_ref, dg_ref, doa_ref, sa_ref, dob1_ref, sb1_ref, dob4_ref, sb4_ref, dob16_ref, sb16_ref,
             doc_ref, sc_ref, dw_ref, st_ref, scr_b4, scr_b16, scr_l4, scr_l16, scr_do, scr_sb):
        @pl.when(pl.program_id(0) == 0)
        def _():
            dw_ref[...] = jnp.zeros_like(dw_ref)
            st_ref[...] = jnp.zeros_like(st_ref)

        b1, l1 = b1_ref[...].astype(F32), l1_ref[...]
        b4, l4 = _load_permuted(b4_ref, scr_b4, 4), _load_permuted(l4_ref, scr_l4, 4)
        b16, l16 = _load_permuted(b16_ref, scr_b16, 16), _load_permuted(l16_ref, scr_l16, 16)
        lm = jnp.maximum(jnp.maximum(l1, l4), l16)
        e1, e4, e16 = jnp.exp(l1 - lm), jnp.exp(l4 - lm), jnp.exp(l16 - lm)
        den = e1 + e4 + e16
        lse_b_tile = lm + jnp.log(den)
        ob_v = _per_head(e1 / den, B_W) * b1 + _per_head(e4 / den, B_W) * b4 + _per_head(e16 / den, B_W) * b16
        o_all = jnp.concatenate([oa_ref[...].astype(F32), ob_v, oc_ref[...].astype(F32)], axis=1)
        g = gate_ref[...].astype(F32)
        sig = 1.0 / (1.0 + jnp.exp(-g))
        silu = g * sig
        y = (o_all * silu).astype(BF16)
        w = w_ref[...]
        z = jnp.dot(y, w, preferred_element_type=F32)
        rz = lax.rsqrt(jnp.mean(z * z, axis=-1, keepdims=True) + RMS_EPS)
        hn = z * rz
        pg = pg_ref[...]
        err = (x_ref[...] + hn * pg) - t_ref[...]
        loss = 0.5 * jnp.sum(jnp.mean(err * err, axis=-1, keepdims=True), axis=0, keepdims=True)
        dh = err * (1.0 / D_MODEL)
        dh_ref[...] = dh.astype(BF16)
        st_ref[0:1, :] += jnp.sum(dh * hn, axis=0, keepdims=True)
        st_ref[1:2, :] += jnp.broadcast_to(loss, (1, D_MODEL))
        dhn = dh * pg
        dz = (rz * (dhn - hn * jnp.mean(dhn * hn, axis=-1, keepdims=True))).astype(BF16)
        dy = lax.dot_general(dz, w, _NT, preferred_element_type=F32)
        dw_ref[...] += lax.dot_general(y, dz, _TN, preferred_element_type=F32)
        dg_ref[...] = (dy * o_all * (sig * (1.0 + g * (1.0 - sig)))).astype(BF16)
        d_o = (dy * silu).astype(BF16)
        prod = d_o.astype(F32) * o_all
        doa_ref[...] = d_o[:, 0:A_W]
        sa_ref[...] = _with_delta(la_ref[...], prod[:, 0:A_W])
        doc_ref[...] = d_o[:, A_W + B_W:D_MODEL]
        sc_ref[...] = _with_delta(lc_ref[...], prod[:, A_W + B_W:D_MODEL])
        d_ob = d_o[:, A_W:A_W + B_W]
        stat_b = _with_delta(lse_b_tile, prod[:, A_W:A_W + B_W])
        dob1_ref[...] = d_ob
        sb1_ref[...] = stat_b
        _put(scr_do, d_ob.astype(F32))
        _put(scr_sb, stat_b)
        _store_permuted(scr_do, dob4_ref, 4, BF16)
        _store_permuted(scr_sb, sb4_ref, 4, F32)
        _store_permuted(scr_do, dob16_ref, 16, BF16)
        _store_permuted(scr_sb, sb16_ref, 16, F32)

    p4 = lambda w: _perm_spec(tm, 4, w)
    p16 = lambda w: _perm_spec(tm, 16, w)
    in_specs = [_row(tm, A_W), _row(tm, LANES), _row(tm, B_W), _row(tm, LANES), p4(B_W), p4(LANES), p16(B_W), p16(LANES),
                _row(tm, C_W), _row(tm, LANES), _row(tm, D_MODEL), _row(tm, D_MODEL), _row(tm, D_MODEL),
                _full((D_MODEL, D_MODEL)), _full((1, D_MODEL))]
    sds = jax.ShapeDtypeStruct
    v4 = lambda w, dt: sds((seq // (BLOCK * 4), 4, BLOCK, w), dt)
    v16 = lambda w, dt: sds((seq // (BLOCK * 16), 16, BLOCK, w), dt)
    out_specs = [_row(tm, D_MODEL), _row(tm, D_MODEL), _row(tm, A_W), _row(tm, LANES), _row(tm, B_W), _row(tm, LANES),
                 p4(B_W), p4(LANES), p16(B_W), p16(LANES), _row(tm, C_W), _row(tm, LANES),
                 _full((D_MODEL, D_MODEL)), _full((8, D_MODEL))]
    out_shape = [sds((seq, D_MODEL), BF16), sds((seq, D_MODEL), BF16), sds((seq, A_W), BF16), sds((seq, LANES), F32),
                 sds((seq, B_W), BF16), sds((seq, LANES), F32), v4(B_W, BF16), v4(LANES, F32), v16(B_W, BF16),
                 v16(LANES, F32), sds((seq, C_W), BF16), sds((seq, LANES), F32),
                 sds((D_MODEL, D_MODEL), F32), sds((8, D_MODEL), F32)]
    res = pl.pallas_call(
        body, name="mid", grid=(seq // tm,), in_specs=in_specs, out_specs=out_specs, out_shape=out_shape,
        scratch_shapes=[pltpu.VMEM((n_b, tm, LANES), F32), pltpu.VMEM((n_b, tm, LANES), F32),
                        pltpu.VMEM((1, tm, LANES), F32), pltpu.VMEM((1, tm, LANES), F32),
                        pltpu.VMEM((n_b, tm, LANES), F32), pltpu.VMEM((1, tm, LANES), F32)],
        compiler_params=_params(dimension_semantics=("arbitrary",)),
    )(oa, lse_a, ob[1], lse_b[1], _perm_view(ob[4], 4), _perm_view(lse_b[4], 4), _perm_view(ob[16], 16),
      _perm_view(lse_b[16], 16), oc, lse_c, gate, x, target, w_out_full, post_g)
    dh, d_gate, do_a, st_a, do_b1, st_b1, do_b4, st_b4, do_b16, st_b16, do_c, st_c, d_wout, stats = res
    flat = lambda t: t.reshape(seq, t.shape[-1])
    d_b = {1: (do_b1, st_b1), 4: (flat(do_b4), flat(st_b4)), 16: (flat(do_b16), flat(st_b16))}
    return dh, d_gate, (do_a, st_a), d_b, (do_c, st_c), d_wout, stats


def _inproj_bwd(x, u, dh, pre_g, w_in_full, tabs, dqa, dka, dva, dqkv_b, dqc, dgate, tm=512):
    seq = x.shape[0]
    n_b = B_W // LANES

    def body(x_ref, u_ref, dh_ref, g_ref, w_hbm, c_ref, up_ref, dn_ref, dqa_ref, dka_ref, dva_ref,
             dq1, dk1, dv1, dq4, dk4, dv4, dq16, dk16, dv16, dqc_ref, dg_ref,
             gx_ref, dw_ref, st_ref, scr4, scr16, w_scr, w_sems, dp_ref):
        _stage_w_in(w_hbm, w_scr, w_sems)

        @pl.when(pl.program_id(0) == 0)
        def _():
            st_ref[...] = jnp.zeros_like(st_ref)
            dw_ref[...] = jnp.zeros_like(dw_ref)

        c, up, dn = c_ref[...], -up_ref[...], -dn_ref[...]
        unrot = lambda t: _rotate(t, c, up, dn)
        total = lambda r1, r4, r16: (r1[...].astype(F32) + _load_permuted(r4, scr4, 4)
                                     + _load_permuted(r16, scr16, 16))
        at = lambda piece: slice(*COLS[piece])
        dp_ref[:, at("qa")] = (unrot(dqa_ref[...].astype(F32)) * SCALE).astype(BF16)
        dp_ref[:, at("ka")] = unrot(_per_kv_head(dka_ref[...].astype(F32))).astype(BF16)
        dp_ref[:, at("va")] = _per_kv_head(dva_ref[...].astype(F32)).astype(BF16)
        dp_ref[:, at("ga")] = dg_ref[:, 0:A_W]
        dp_ref[:, at("qb")] = (unrot(total(dq1, dq4, dq16)) * SCALE).astype(BF16)
        dp_ref[:, at("kb")] = unrot(total(dk1, dk4, dk16)).astype(BF16)
        dp_ref[:, at("vb")] = total(dv1, dv4, dv16).astype(BF16)
        dp_ref[:, at("gb")] = dg_ref[:, A_W:A_W + B_W]
        dp_ref[:, at("qc")] = (dqc_ref[...].astype(F32) * SCALE).astype(BF16)
        dp_ref[:, at("gc")] = dg_ref[:, A_W + B_W:D_MODEL]
        du = lax.dot_general(dp_ref[...], w_scr[...], _NT, preferred_element_type=F32)
        res = lax.dot_general(u_ref[...], dp_ref[...], _TN, preferred_element_type=F32)
        for k in range(N_DEV):
            dw_ref[k] += res[:, SHARD_IN * k:SHARD_IN * (k + 1)]
        xv = x_ref[...]
        r = lax.rsqrt(jnp.mean(xv * xv, axis=-1, keepdims=True) + RMS_EPS)
        xh = xv * r
        st_ref[0:1, :] += jnp.sum(du * xh, axis=0, keepdims=True)
        dxh = du * g_ref[...]
        gx_ref[...] = dh_ref[...].astype(F32) + r * (dxh - xh * jnp.mean(dxh * xh, axis=-1, keepdims=True))

    in_specs = ([_row(tm, D_MODEL), _row(tm, D_MODEL), _row(tm, D_MODEL), _full((1, D_MODEL)),
                 pl.BlockSpec(memory_space=pl.ANY),
                 _row(tm, LANES), _row(tm, LANES), _row(tm, LANES), _row(tm, A_W), _row(tm, A_W), _row(tm, A_W)]
                + [_row(tm, B_W)] * 3 + [_perm_spec(tm, 4, B_W)] * 3 + [_perm_spec(tm, 16, B_W)] * 3
                + [_row(tm, C_W), _row(tm, D_MODEL)])
    dw_spec = pl.BlockSpec((N_DEV, D_MODEL, SHARD_IN), lambda i: (0, 0, 0), pipeline_mode=pl.Buffered(1))
    return pl.pallas_call(
        body, name="inproj_bwd", grid=(seq // tm,), in_specs=in_specs,
        out_specs=[_row(tm, D_MODEL), dw_spec, _full((8, D_MODEL))],
        out_shape=[jax.ShapeDtypeStruct((seq, D_MODEL), F32), jax.ShapeDtypeStruct((N_DEV, D_MODEL, SHARD_IN), F32),
                   jax.ShapeDtypeStruct((8, D_MODEL), F32)],
        scratch_shapes=[pltpu.VMEM((n_b, tm, LANES), F32), pltpu.VMEM((n_b, tm, LANES), F32)] + _w_in_scratch()
        + [pltpu.VMEM((tm, D_IN), BF16)],
        compiler_params=_params(dimension_semantics=("arbitrary",)),
    )(x, u, dh, pre_g, w_in_full, *tabs, dqa, dka, dva, *dqkv_b[1], *[_perm_view(t, 4) for t in dqkv_b[4]],
      *[_perm_view(t, 16) for t in dqkv_b[16]], dqc, dgate)


class _ReduceScatter:
    def __init__(self, ins, outs, scratch):
        self.n = n = len(ins)
        self.ins, self.outs = ins, outs
        self.mine, self.got, self.snd, self.rcv = (scratch[n * t:n * (t + 1)] for t in range(4))
        self.load_sems, self.d2d_send, self.d2d_recv, self.ici_send, self.ici_recv = scratch[4 * n:]
        self.pos = _mesh_pos()
        self.pairs = [(a, kk) for kk in (3, 1, 2) for a in range(n)]

    @staticmethod
    def scratch_shapes(shapes):
        return ([pltpu.VMEM((4,) + s, F32) for s in shapes] + [pltpu.VMEM((4,) + s, F32) for s in shapes]
                + [pltpu.VMEM((3,) + s, BF16) for s in shapes] + [pltpu.VMEM((3,) + s, BF16) for s in shapes]
                + [pltpu.SemaphoreType.DMA((len(shapes), 4))] * 5)

    def _chip(self, kk):
        x, y, _ = self.pos
        return (1 - x if kk & 2 else x, 1 - y if kk & 1 else y)

    def _load(self, a, kk):
        block = _dev_index((*self._chip(kk), self.pos[2]))
        return pltpu.make_async_copy(self.ins[a].at[block], self.mine[a].at[kk], self.load_sems.at[a, kk])

    def _swap(self, a, kk):
        x, y, c = self.pos
        return pltpu.make_async_remote_copy(
            src_ref=self.ins[a].at[_dev_index((*self._chip(kk), 1 - c))], dst_ref=self.got[a].at[kk],
            send_sem=self.d2d_send.at[a, kk], recv_sem=self.d2d_recv.at[a, kk],
            device_id=(x, y, 1 - c), device_id_type=MESH_ID)

    def _hop(self, a, kk):
        return pltpu.make_async_remote_copy(
            src_ref=self.snd[a].at[kk - 1], dst_ref=self.rcv[a].at[kk - 1], send_sem=self.ici_send.at[a, kk],
            recv_sem=self.ici_recv.at[a, kk], device_id=(*self._chip(kk), self.pos[2]), device_id_type=MESH_ID)

    def start(self):
        for kk in (3, 1, 2, 0):
            for a in range(self.n):
                self._load(a, kk).start()
                self._swap(a, kk).start()

    def send_chip_sums(self):
        for a, kk in self.pairs:
            self._load(a, kk).wait()
            self._swap(a, kk).wait_recv()
            self.snd[a][kk - 1] = (self.mine[a][kk] + self.got[a][kk]).astype(BF16)
            self._hop(a, kk).start()

    def finish(self):
        for a in range(self.n):
            self._load(a, 0).wait()
            self._swap(a, 0).wait_recv()
            acc = self.mine[a][0] + self.got[a][0]
            for kk in (1, 2, 3):
                self._hop(a, kk).wait_recv()
                acc = acc + self.rcv[a][kk - 1].astype(F32)
            self.outs[a][...] = acc
        for kk in range(4):
            for a in range(self.n):
                self._swap(a, kk).wait_send()
        for a, kk in self.pairs:
            self._hop(a, kk).wait_send()


def _local_step(x, mem, pre_g, w_in, sink, mem_g, w_mem, w_out, post_g, target):
    u, *tabs, w_in_full = _prep(x, pre_g, w_in)
    qa, ka, va, qkv_b, qc, gate, w_mem_all, w_out_all = _inproj(u, w_in_full, tabs, w_mem, w_out)
    w_mem_full = w_mem_all.reshape(D_MODEL, 2 * C_W)
    w_out_full = w_out_all.reshape(D_MODEL, D_MODEL)
    mn, mk, mv = _memkv_fwd(mem, mem_g, w_mem_full)

    a_cfg = dict(dil=1, heads=A_HEADS, max_dist=BLOCK - 1, nq=ATTN_BLOCKS_PER_STEP)
    b_cfgs = {dil: dict(dil=dil, heads=B_HEADS, max_dist=win // dil, nq=ATTN_BLOCKS_PER_STEP)
              for win, dil in B_CONFIGS}
    oa, lse_a = _banded_fwd(qa, ka, va, sink, name="attn_a_fwd", **a_cfg)
    ob, lse_b = {}, {}
    for dil, cfg in b_cfgs.items():
        ob[dil], lse_b[dil] = _banded_fwd(*qkv_b[dil], None, name=f"attn_b{dil}_fwd", **cfg)
    oc, lse_c = _cross_fwd(qc, mk, mv)

    dh, d_gate, d_a, d_b, d_c, d_wout, st_mid = _mid(oa, lse_a, ob, lse_b, oc, lse_c, gate, x, target, w_out_full, post_g)

    dqc, dmk, dmv = _cross_bwd(qc, mk, mv, *d_c)
    d_wmem, st_mem = _memkv_bwd(mem, mem_g, mn, w_mem_full, dmk, dmv)
    dqkv_b = {dil: _banded_bwd(*qkv_b[dil], *d_b[dil], None, name=f"attn_b{dil}_bwd", **cfg)
              for dil, cfg in b_cfgs.items()}
    dqa, dka, dva, dsink, g_wmem, g_wout = _banded_bwd(
        qa, ka, va, *d_a, sink, name="attn_a_bwd", **a_cfg,
        reduce_scatter=(d_wmem.reshape(N_DEV, SHARD_ROWS, 2 * C_W), d_wout.reshape(N_DEV, SHARD_ROWS, D_MODEL)))

    grad_x, d_win, st_pre = _inproj_bwd(x, u, dh, pre_g, w_in_full, tabs, dqa, dka, dva, dqkv_b, dqc, d_gate)

    dsink_row = jnp.pad(dsink[0:1, :], ((0, 0), (0, D_MODEL - LANES)))
    stats = jnp.concatenate([st_pre[0:1], st_mem[0:1], st_mid[0:1], dsink_row, st_mid[1:2],
                             jnp.zeros((3, D_MODEL), F32)], axis=0)
    return grad_x, d_win, g_wmem, g_wout, stats


def _prep(x, pre_g, w_in, tm=1024):
    seq = x.shape[0]
    n_steps = seq // tm
    parts = 2
    rows = D_MODEL // parts
    relay_at = min(3, n_steps - 1)
    j = jnp.arange(LANES) % HEAD_DIM
    freq = (ROPE_THETA ** (-(2 * (j % (ROT_DIM // 2))).astype(F32) / ROT_DIM))[None, :]
    SIB, NB_X, NB_Y, RELAY, FWD = 0, 1, 2, 3, 4

    def body(x_ref, g_ref, f_ref, win_ref, u_ref, c_ref, up_ref, dn_ref, win_out, win_b,
             send_sems, recv_sems, local_sems):
        step = pl.program_id(0)
        px, py, pc = _mesh_pos()
        me, sibling = (px, py, pc), (px, py, 1 - pc)
        others = lambda core: ((1 - px, py, core), (px, 1 - py, core), (1 - px, 1 - py, core))
        x_nb, y_nb, diag = others(pc)
        relay_from = [x_nb, y_nb]
        relay_to = [y_nb, x_nb]

        def src(a):
            return win_b.at[pl.ds(rows * a, rows)]

        def slot(a, p):
            return win_out.at[_dev_index(p), pl.ds(rows * a, rows)]

        def copy(a, k, block, to, own=False):
            return pltpu.make_async_remote_copy(
                src_ref=src(a) if own else slot(a, block), dst_ref=slot(a, block),
                send_sem=send_sems.at[a, k], recv_sem=recv_sems.at[a, k], device_id=to, device_id_type=MESH_ID)

        def first_sends():
            return [copy(0, NB_X, me, x_nb, own=True), copy(1, NB_Y, me, y_nb, own=True),
                    copy(1, NB_X, me, x_nb, own=True), copy(0, NB_Y, me, y_nb, own=True),
                    copy(0, SIB, me, sibling, own=True), copy(1, SIB, me, sibling, own=True)]

        def relay(a):
            return copy(a, RELAY, relay_from[a], relay_to[a])

        def to_sibling(a, which):
            return copy(a, FWD + which, others(pc)[which], sibling)

        def local(a):
            return pltpu.make_async_copy(src(a), slot(a, me), local_sems.at[a])

        @pl.when(step == 0)
        def _():
            win_b[...] = win_ref[...].astype(BF16)
            for a in range(parts):
                local(a).start()
            for cp in first_sends():
                cp.start()

        @pl.when(step == relay_at)
        def _():
            for a in range(parts):
                copy(a, NB_X + a, relay_from[a], me).wait_recv()
                relay(a).start()
                to_sibling(a, a).start()

        xv = x_ref[...]
        r = lax.rsqrt(jnp.mean(xv * xv, axis=-1, keepdims=True) + RMS_EPS)
        u_ref[...] = ((xv * r) * g_ref[...]).astype(BF16)
        pos = (lax.broadcasted_iota(jnp.int32, (tm, LANES), 0) + step * tm).astype(F32)
        head_lane = lax.broadcasted_iota(jnp.int32, (tm, LANES), 1) % HEAD_DIM
        ang = pos * f_ref[...]
        cos, sin = jnp.cos(ang), jnp.sin(ang)
        half = ROT_DIM // 2
        c_ref[...] = jnp.where(head_lane < ROT_DIM, cos, 1.0)
        up_ref[...] = jnp.where((head_lane >= half) & (head_lane < ROT_DIM), sin, 0.0)
        dn_ref[...] = jnp.where(head_lane < half, -sin, 0.0)

        @pl.when(step == n_steps - 1)
        def _():
            copy(1, NB_X, x_nb, me).wait_recv()
            to_sibling(1, 0).start()
            copy(0, NB_Y, y_nb, me).wait_recv()
            to_sibling(0, 1).start()
            for a in range(parts):
                copy(a, RELAY, diag, me).wait_recv()
                to_sibling(a, 2).start()
            for a in range(parts):
                copy(a, SIB, sibling, me).wait_recv()
                for which in range(3):
                    copy(a, FWD + which, others(1 - pc)[which], me).wait_recv()
            for cp in first_sends():
                cp.wait_send()
            for a in range(parts):
                relay(a).wait_send()
                for which in range(3):
                    to_sibling(a, which).wait_send()
                local(a).wait()

    return pl.pallas_call(
        body, name="prep", grid=(n_steps,),
        in_specs=[_row(tm, D_MODEL), _full((1, D_MODEL)), _full((1, LANES)), _full(w_in.shape)],
        out_specs=[_row(tm, D_MODEL), _row(tm, LANES), _row(tm, LANES), _row(tm, LANES),
                   pl.BlockSpec(memory_space=pl.ANY)],
        out_shape=[jax.ShapeDtypeStruct((seq, D_MODEL), BF16)] + [jax.ShapeDtypeStruct((seq, LANES), F32)] * 3
        + [jax.ShapeDtypeStruct((N_DEV,) + w_in.shape, BF16)],
        scratch_shapes=[pltpu.VMEM(w_in.shape, BF16), pltpu.SemaphoreType.DMA((parts, FWD + 3)),
                        pltpu.SemaphoreType.DMA((parts, FWD + 3)), pltpu.SemaphoreType.DMA((parts,))],
        compiler_params=_params(dimension_semantics=("arbitrary",)),
    )(x, pre_g, freq, w_in)


def _exchange_grads(d_win, stats):
    def body(win, st, g_win, r_st, send_sems, recv_sems, local_sem, *scratch):
        exchange = _ReduceScatter((win,), (g_win,), scratch)
        exchange.start()
        pos = _mesh_pos()
        me = _dev_index(pos)
        own = pltpu.make_async_copy(st, r_st.at[me], local_sem)
        own.start()
        copies = []
        for s in range(1, N_DEV):
            peer = _xor_peer(pos, s)
            mk = lambda slot: pltpu.make_async_remote_copy(
                src_ref=st, dst_ref=r_st.at[slot], send_sem=send_sems.at[s], recv_sem=recv_sems.at[s],
                device_id=peer, device_id_type=MESH_ID)
            send, arrival = mk(me), mk(_dev_index(peer))
            send.start()
            copies.append((send, arrival))
        exchange.send_chip_sums()
        exchange.finish()
        for send, arrival in copies:
            arrival.wait_recv()
            send.wait_send()
        own.wait()

    hbm = pl.BlockSpec(memory_space=pl.ANY)
    shard = d_win.shape[1:]
    return pl.pallas_call(
        body, name="exchange_grads", in_specs=[hbm, hbm],
        out_specs=[pl.BlockSpec(memory_space=pltpu.VMEM), hbm],
        out_shape=[jax.ShapeDtypeStruct(shard, F32), jax.ShapeDtypeStruct((N_DEV,) + stats.shape, F32)],
        scratch_shapes=[pltpu.SemaphoreType.DMA((N_DEV,)), pltpu.SemaphoreType.DMA((N_DEV,)), pltpu.SemaphoreType.DMA(())]
        + _ReduceScatter.scratch_shapes([shard]),
        compiler_params=_params(),
    )(d_win, stats)


WEIGHT_ORDER = ("pre_norm", "w_in", "sink_a", "mem_norm", "w_mem_kv", "w_out", "post_norm")


def _adamw_all(grads, r_stats, weights, moments_m, moments_v):
    n = len(WEIGHT_ORDER)
    stat_row = {"pre_norm": 0, "mem_norm": 1, "post_norm": 2, "sink_a": 3}

    def body(*refs):
        gw_in, gw_mem, gw_out, st_ref = refs[0:4]
        w_refs, m_refs, v_refs = (dict(zip(WEIGHT_ORDER, refs[4 + n * t:4 + n * (t + 1)])) for t in range(3))
        loss_ref = refs[4 + 3 * n]
        outs = refs[5 + 3 * n:]
        g_small = st_ref[0]
        for s in range(1, N_DEV):
            g_small = g_small + st_ref[s]
        loss_ref[...] = g_small[4:5, 0:1]
        big = {"w_in": gw_in, "w_mem_kv": gw_mem, "w_out": gw_out}
        for i, name in enumerate(WEIGHT_ORDER):
            if name in big:
                g = big[name][...]
                at = lambda ref: ref[0]
            else:
                width = w_refs[name].shape[-1]
                g = g_small[stat_row[name]:stat_row[name] + 1, 0:width]
                at = lambda ref: ref[...]
            m2 = ADAM_B1 * at(m_refs[name]) + (1.0 - ADAM_B1) * g
            v2 = ADAM_B2 * at(v_refs[name]) + (1.0 - ADAM_B2) * (g * g)
            m_hat = m2 / (1.0 - ADAM_B1 ** ADAM_STEP)
            v_hat = v2 / (1.0 - ADAM_B2 ** ADAM_STEP)
            delta = -ADAM_LR * (m_hat / (jnp.sqrt(v_hat) + ADAM_EPS) + ADAM_WD * at(w_refs[name]))
            for kind, val in enumerate((g, delta, m2, v2)):
                out = outs[kind * n + i]
                if name in big:
                    out[0] = val
                else:
                    out[...] = val

    shapes = [weights[name].shape for name in WEIGHT_ORDER]
    res = pl.pallas_call(
        body, name="adamw_all",
        out_shape=[jax.ShapeDtypeStruct((1, 1), F32)] + [jax.ShapeDtypeStruct(sh, F32) for sh in shapes] * 4,
        compiler_params=_params(),
    )(grads["w_in"], grads["w_mem_kv"], grads["w_out"], r_stats,
      *[weights[k] for k in WEIGHT_ORDER], *[moments_m[k] for k in WEIGHT_ORDER], *[moments_v[k] for k in WEIGHT_ORDER])
    return res[0].reshape(()), res[1:]


def kernel(x, mem, pre_norm, w_in, sink_a, mem_norm, w_mem_kv, w_out, post_norm, loss_target, m_pre_norm, m_w_in, m_sink_a, m_mem_norm, m_w_mem_kv, m_w_out, m_post_norm, v_pre_norm, v_w_in, v_sink_a, v_mem_norm, v_w_mem_kv, v_w_out, v_post_norm):
    sink = jnp.pad(sink_a[0], (0, 8 - A_HEADS))
    grad_x, d_win, g_wmem, g_wout, stats = _local_step(
        x[0], mem[0], pre_norm, w_in[0], sink, mem_norm, w_mem_kv[0], w_out[0], post_norm, loss_target[0])
    g_win, r_stats = _exchange_grads(d_win, stats)
    weights = dict(pre_norm=pre_norm, w_in=w_in, sink_a=sink_a, mem_norm=mem_norm, w_mem_kv=w_mem_kv, w_out=w_out,
                   post_norm=post_norm)
    moments_m = dict(pre_norm=m_pre_norm, w_in=m_w_in, sink_a=m_sink_a, mem_norm=m_mem_norm, w_mem_kv=m_w_mem_kv,
                     w_out=m_w_out, post_norm=m_post_norm)
    moments_v = dict(pre_norm=v_pre_norm, w_in=v_w_in, sink_a=v_sink_a, mem_norm=v_mem_norm, w_mem_kv=v_w_mem_kv,
                     w_out=v_w_out, post_norm=v_post_norm)
    loss, rest = _adamw_all(dict(w_in=g_win, w_mem_kv=g_wmem, w_out=g_wout), r_stats, weights, moments_m, moments_v)
    return (loss, grad_x[None], *rest)
```

```python
import jax
import jax.numpy as jnp
from jax import lax
from jax.experimental import pallas as pl
from jax.experimental.pallas import tpu as pltpu

F32 = jnp.float32
BF16 = jnp.bfloat16

D_MODEL = 1024
HEAD_DIM = 64
ROT_DIM = 16
ROPE_THETA = 500000.0
BLOCK = 128
LANES = 128
N_MEM = 256
RMS_EPS = 1e-6
SCALE = HEAD_DIM ** -0.5
A_HEADS = 6
B_HEADS = 6
C_HEADS = 4
A_W, A_KV_W, B_W, C_W = 384, 128, 384, 256
_IN_PIECES = (("qa", A_W), ("ka", A_KV_W), ("va", A_KV_W), ("ga", A_W), ("qb", B_W), ("kb", B_W), ("vb", B_W),
              ("gb", B_W), ("qc", C_W), ("gc", C_W))
COLS, D_IN = {}, 0
for _name, _width in _IN_PIECES:
    COLS[_name] = (D_IN, D_IN + _width)
    D_IN += _width
N_DEV = 8
SHARD_IN = D_IN // N_DEV
SHARD_ROWS = D_MODEL // N_DEV
B_CONFIGS = ((128, 1), (512, 4), (2048, 16))
DILS = (4, 16)
NEG = -1e30
ATTN_BLOCKS_PER_STEP = 4
DELTA_LANE = 64
VMEM_LIMIT = 56 * 1024 * 1024

ADAM_LR, ADAM_B1, ADAM_B2, ADAM_EPS, ADAM_WD, ADAM_STEP = 0.001, 0.9, 0.999, 1e-08, 0.01, 10
MESH_ID = pl.DeviceIdType.MESH


def _params(**kw):
    return pltpu.CompilerParams(vmem_limit_bytes=VMEM_LIMIT, **kw)


def _full(shape):
    n = len(shape)
    return pl.BlockSpec(shape, lambda *_: (0,) * n)


def _row(tm, w):
    return pl.BlockSpec((tm, w), lambda i: (i, 0))


def _mesh_pos():
    return lax.axis_index("x"), lax.axis_index("y"), lax.axis_index("c")


def _dev_index(pos):
    return 4 * pos[0] + 2 * pos[1] + pos[2]


def _xor_peer(pos, s):
    x, y, c = pos
    return (1 - x if s & 4 else x, 1 - y if s & 2 else y, 1 - c if s & 1 else c)


def _perm_view(a, dil):
    return a.reshape(a.shape[0] // (BLOCK * dil), dil, BLOCK, a.shape[1])


def _perm_spec(tm, dil, w):
    chunk = BLOCK * dil
    if tm >= chunk:
        return pl.BlockSpec((tm // chunk, dil, BLOCK, w), lambda i: (i, 0, 0, 0))
    per = chunk // tm
    return pl.BlockSpec((1, dil, tm // dil, w), lambda i: (i // per, 0, i % per, 0))


def _put(scr, val):
    for c in range(val.shape[1] // LANES):
        scr[c] = val[:, LANES * c:LANES * (c + 1)]


def _get(scr):
    n = scr.shape[0]
    return scr[0] if n == 1 else jnp.concatenate([scr[c] for c in range(n)], axis=1)


def _get_class(scr, r, dil):
    n, rows = scr.shape[0], scr.shape[1]
    parts = [scr.at[c][pl.ds(r, rows // dil, stride=dil), :] for c in range(n)]
    return parts[0] if n == 1 else jnp.concatenate(parts, axis=1)


def _store_permuted(scr, out_ref, dil, dtype):
    for r in range(dil):
        out_ref[0, r] = _get_class(scr, r, dil).astype(dtype)


def _load_permuted(in_ref, scr, dil):
    n, rows = scr.shape[0], scr.shape[1]
    for r in range(dil):
        val = in_ref[0, r].astype(F32)
        for c in range(n):
            scr.at[c][pl.ds(r, rows // dil, stride=dil), :] = val[:, LANES * c:LANES * (c + 1)]
    return _get(scr)


def _rotate128(t, c, up, dn):
    half = ROT_DIM // 2
    return t * c + pltpu.roll(t, half, 1) * up + pltpu.roll(t, LANES - half, 1) * dn


def _rotate(t, c, up, dn):
    outs = [_rotate128(t[:, LANES * j:LANES * (j + 1)], c, up, dn) for j in range(t.shape[1] // LANES)]
    return outs[0] if len(outs) == 1 else jnp.concatenate(outs, axis=1)


def _per_query_head(kv):
    lane = lax.broadcasted_iota(jnp.int32, kv.shape, 1)
    other = pltpu.roll(kv, HEAD_DIM, 1)
    return jnp.concatenate([jnp.where(lane < HEAD_DIM, kv, other), kv, jnp.where(lane < HEAD_DIM, other, kv)], axis=1)


def _per_kv_head(d):
    s0, s1, s2 = (d[:, LANES * p:LANES * (p + 1)] for p in range(3))
    lane = lax.broadcasted_iota(jnp.int32, s0.shape, 1)
    return jnp.where(lane < HEAD_DIM, s0 + pltpu.roll(s0, HEAD_DIM, 1) + s1, s1 + s2 + pltpu.roll(s2, HEAD_DIM, 1))


def _w_in_scratch():
    return [pltpu.VMEM((D_MODEL, D_IN), BF16), pltpu.SemaphoreType.DMA((N_DEV,))]


def _stage_w_in(w_hbm, w_scr, sems):
    @pl.when(pl.program_id(0) == 0)
    def _():
        copies = [pltpu.make_async_copy(w_hbm.at[k], w_scr.at[:, pl.ds(SHARD_IN * k, SHARD_IN)], sems.at[k])
                  for k in range(N_DEV)]
        for cp in copies:
            cp.start()
        for cp in copies:
            cp.wait()


def _inproj(u, w_in_full, tabs, w_mem, w_out, tm=1024):
    seq = u.shape[0]
    n_chunk = D_IN // LANES
    n_steps = seq // tm

    def body(u_ref, w_hbm, c_ref, up_ref, dn_ref, wm_ref, wo_ref, qa_ref, ka_ref, va_ref,
             qb1_ref, kb1_ref, vb1_ref, qb4_ref, kb4_ref, vb4_ref, qb16_ref, kb16_ref, vb16_ref,
             qc_ref, gate_ref, wm_all, wo_all, proj, w_scr, w_sems, wm_b, wo_b, send_sems, recv_sems, local_sems):
        step = pl.program_id(0)
        shards, gathered = (wm_b, wo_b), (wm_all, wo_all)

        def gather_copies(arriving):
            pos = _mesh_pos()
            me = _dev_index(pos)
            local = [] if arriving else [
                pltpu.make_async_copy(shards[a], gathered[a].at[me], local_sems.at[a]) for a in range(2)]
            remote = []
            for s in range(1, N_DEV):
                peer = _xor_peer(pos, s)
                for a in range(2):
                    remote.append(pltpu.make_async_remote_copy(
                        src_ref=shards[a], dst_ref=gathered[a].at[_dev_index(peer) if arriving else me],
                        send_sem=send_sems.at[a, s], recv_sem=recv_sems.at[a, s], device_id=peer,
                        device_id_type=MESH_ID))
            return local, remote

        @pl.when(step == 0)
        def _():
            wm_b[...] = wm_ref[...].astype(BF16)
            wo_b[...] = wo_ref[...].astype(BF16)
            local, sends = gather_copies(arriving=False)
            for cp in local + sends:
                cp.start()

        _stage_w_in(w_hbm, w_scr, w_sems)
        u = u_ref[...]
        for n0 in range(0, D_IN, D_MODEL):
            acc = jnp.dot(u, w_scr[:, n0:n0 + D_MODEL], preferred_element_type=F32)
            for c3 in range(D_MODEL // LANES):
                proj[n0 // LANES + c3] = acc[:, LANES * c3:LANES * (c3 + 1)]
        c, up, dn = c_ref[...], up_ref[...], dn_ref[...]

        def chunks_of(piece):
            lo, hi = COLS[piece]
            return range(lo // LANES, hi // LANES)

        def cols(piece, rot=False, scale=None):
            parts = []
            for ch in chunks_of(piece):
                t = proj[ch]
                if rot:
                    t = _rotate128(t, c, up, dn)
                if scale is not None:
                    t = t * scale
                parts.append(t)
            return parts[0] if len(parts) == 1 else jnp.concatenate(parts, axis=1)

        qa_ref[...] = cols("qa", True, SCALE).astype(BF16)
        ka_ref[...] = _per_query_head(cols("ka", True)).astype(BF16)
        va_ref[...] = _per_query_head(cols("va")).astype(BF16)
        gate_ref[:, 0:A_W] = cols("ga").astype(BF16)
        gate_ref[:, A_W:A_W + B_W] = cols("gb").astype(BF16)
        gate_ref[:, A_W + B_W:D_MODEL] = cols("gc").astype(BF16)
        qc_ref[...] = cols("qc", False, SCALE).astype(BF16)
        for ch in chunks_of("qb"):
            proj[ch] = _rotate128(proj[ch], c, up, dn) * SCALE
        for ch in chunks_of("kb"):
            proj[ch] = _rotate128(proj[ch], c, up, dn)
        for piece, nat, p4, p16 in (("qb", qb1_ref, qb4_ref, qb16_ref), ("kb", kb1_ref, kb4_ref, kb16_ref),
                                    ("vb", vb1_ref, vb4_ref, vb16_ref)):
            chunks = chunks_of(piece)
            nat[...] = jnp.concatenate([proj[ch] for ch in chunks], axis=1).astype(BF16)
            for dil, ref in ((4, p4), (16, p16)):
                span = min(tm, BLOCK * dil)
                for cc in range(tm // span):
                    for rr in range(dil):
                        ref[cc, rr] = jnp.concatenate(
                            [proj.at[ch][pl.ds(cc * span + rr, span // dil, stride=dil), :] for ch in chunks],
                            axis=1).astype(BF16)

        @pl.when(step == n_steps - 1)
        def _():
            for cp in gather_copies(arriving=True)[1]:
                cp.wait_recv()
            local, sends = gather_copies(arriving=False)
            for cp in sends:
                cp.wait_send()
            for cp in local:
                cp.wait()

    nat_w = (A_W, A_W, A_W, B_W, B_W, B_W)
    out_specs = [_row(tm, w) for w in nat_w]
    out_shape = [jax.ShapeDtypeStruct((seq, w), BF16) for w in nat_w]
    for dil in DILS:
        out_specs += [_perm_spec(tm, dil, B_W)] * 3
        out_shape += [jax.ShapeDtypeStruct((seq // (BLOCK * dil), dil, BLOCK, B_W), BF16)] * 3
    hbm = pl.BlockSpec(memory_space=pl.ANY)
    out_specs += [_row(tm, C_W), _row(tm, D_MODEL), hbm, hbm]
    out_shape += [jax.ShapeDtypeStruct((seq, C_W), BF16), jax.ShapeDtypeStruct((seq, D_MODEL), BF16),
                  jax.ShapeDtypeStruct((N_DEV,) + w_mem.shape, BF16), jax.ShapeDtypeStruct((N_DEV,) + w_out.shape, BF16)]
    res = pl.pallas_call(
        body, name="inproj", grid=(n_steps,),
        in_specs=[_row(tm, D_MODEL), hbm, _row(tm, LANES), _row(tm, LANES), _row(tm, LANES),
                  _full(w_mem.shape), _full(w_out.shape)],
        out_specs=out_specs, out_shape=out_shape,
        scratch_shapes=[pltpu.VMEM((n_chunk, tm, LANES), F32)] + _w_in_scratch()
        + [pltpu.VMEM(w_mem.shape, BF16), pltpu.VMEM(w_out.shape, BF16), pltpu.SemaphoreType.DMA((2, N_DEV)),
           pltpu.SemaphoreType.DMA((2, N_DEV)), pltpu.SemaphoreType.DMA((2,))],
        compiler_params=_params(dimension_semantics=("arbitrary",)),
    )(u, w_in_full, *tabs, w_mem, w_out)
    qa, ka, va = res[0:3]
    qkv_b = {1: res[3:6], 4: [t.reshape(seq, B_W) for t in res[6:9]], 16: [t.reshape(seq, B_W) for t in res[9:12]]}
    return qa, ka, va, qkv_b, res[12], res[13], res[14], res[15]


def _memkv_fwd(mem, mem_g, w_mem_full):
    def body(mem_ref, g_ref, w_ref, mn_ref, mk_ref, mv_ref):
        mv_ = mem_ref[...]
        r = lax.rsqrt(jnp.mean(mv_ * mv_, axis=-1, keepdims=True) + RMS_EPS)
        mn = ((mv_ * r) * g_ref[...]).astype(BF16)
        mn_ref[...] = mn
        mkv = jnp.dot(mn, w_ref[...], preferred_element_type=F32)
        mk_ref[...] = mkv[:, 0:C_W].astype(BF16)
        mv_ref[...] = mkv[:, C_W:2 * C_W].astype(BF16)

    return pl.pallas_call(
        body, name="memkv_fwd",
        out_shape=[jax.ShapeDtypeStruct((N_MEM, D_MODEL), BF16),
                   jax.ShapeDtypeStruct((N_MEM, C_W), BF16), jax.ShapeDtypeStruct((N_MEM, C_W), BF16)],
        compiler_params=_params(),
    )(mem, mem_g, w_mem_full)


def _memkv_bwd(mem, mem_g, mn, w_mem_full, dmk, dmv):
    def body(mem_ref, g_ref, mn_ref, w_ref, dmk_ref, dmv_ref, dw_ref, st_ref):
        dmkv = jnp.concatenate([dmk_ref[...], dmv_ref[...]], axis=1).astype(BF16)
        dw_ref[...] = lax.dot_general(mn_ref[...], dmkv, (((0,), (0,)), ((), ())), preferred_element_type=F32)
        dmn = lax.dot_general(dmkv, w_ref[...], (((1,), (1,)), ((), ())), preferred_element_type=F32)
        mv_ = mem_ref[...]
        r = lax.rsqrt(jnp.mean(mv_ * mv_, axis=-1, keepdims=True) + RMS_EPS)
        st_ref[...] = jnp.zeros_like(st_ref)
        st_ref[0:1, :] = jnp.sum(dmn * (mv_ * r), axis=0, keepdims=True)

    return pl.pallas_call(
        body, name="memkv_bwd",
        out_shape=[jax.ShapeDtypeStruct((D_MODEL, 2 * C_W), F32), jax.ShapeDtypeStruct((8, D_MODEL), F32)],
        compiler_params=_params(),
    )(mem, mem_g, mn, w_mem_full, dmk, dmv)


def _band_mask(has_prev, max_dist):
    qi = lax.broadcasted_iota(jnp.int32, (BLOCK, 2 * BLOCK), 0)
    kj = lax.broadcasted_iota(jnp.int32, (BLOCK, 2 * BLOCK), 1)
    dist = qi + BLOCK - kj
    return (dist >= 0) & (dist <= max_dist) & ((kj >= BLOCK) | has_prev)


_NT = (((1,), (1,)), ((), ()))
_TN = (((0,), (0,)), ((), ()))


def _head_only(val, h):
    slab = _slabs_of(val)(h)
    lane = lax.broadcasted_iota(jnp.int32, slab.shape, 1)
    keep = (lane < HEAD_DIM) if h % 2 == 0 else (lane >= HEAD_DIM)
    return jnp.where(keep, slab, jnp.zeros((), slab.dtype))


def _slabs_of(val):
    return lambda h: val[:, LANES * (h // 2):LANES * (h // 2 + 1)]


class _BandSteps:
    def __init__(self, seq, dil, nq):
        self.nq, self.rows, self.consecutive = nq, nq * BLOCK, dil == 1
        nb = seq // dil // BLOCK
        if self.consecutive:
            assert nb % nq == 0
            self.outer, self.inner, self.stride = 1, nb // nq, 1
        else:
            assert dil % nq == 0
            self.outer, self.inner, self.stride = dil // nq, nb, dil // nq

    def own(self, w, clamp=False):
        cur = (lambda i: jnp.minimum(i, self.inner - 1)) if clamp else (lambda i: i)
        return pl.BlockSpec((self.rows, w), lambda r, i: (cur(i) * self.stride + r, 0))

    def prev(self, w, clamp=False):
        cur = (lambda i: jnp.minimum(i, self.inner - 1)) if clamp else (lambda i: i)
        if self.consecutive:
            return pl.BlockSpec((BLOCK, w), lambda r, i: (jnp.maximum(cur(i) * self.nq - 1, 0), 0))
        return pl.BlockSpec((self.rows, w), lambda r, i: (jnp.maximum(cur(i) - 1, 0) * self.stride + r, 0))

    def late(self, w):
        return pl.BlockSpec((self.rows, w), lambda r, i: (jnp.maximum(i - 1, 0) * self.stride + r, 0))

    def rows_of(self, j):
        return slice(BLOCK * j, BLOCK * (j + 1))

    def keys(self, p_ref, c_ref, j):
        if not self.consecutive:
            before = p_ref[self.rows_of(j), :]
        elif j == 0:
            before = p_ref[...]
        else:
            before = c_ref[self.rows_of(j - 1), :]
        return jnp.concatenate([before, c_ref[self.rows_of(j), :]], axis=0)

    def has_prev(self, i, j):
        return True if (self.consecutive and j > 0) else (i > 0)


def _banded_fwd(q, k, v, sink, *, dil, heads, max_dist, nq, name):
    seq = q.shape[0]
    qw = kw = heads * HEAD_DIM
    steps = _BandSteps(seq, dil, nq)

    def body(*refs):
        if sink is not None:
            sink_ref, refs = refs[0], refs[1:]
        q_ref, kp_ref, kc_ref, vp_ref, vc_ref, o_ref, lse_ref, s_scr, p_scr = refs
        i = pl.program_id(1)
        lane = lax.broadcasted_iota(jnp.int32, (BLOCK, LANES), 1)
        k_of = [_slabs_of(steps.keys(kp_ref, kc_ref, j)) for j in range(nq)]
        v_of = [_slabs_of(steps.keys(vp_ref, vc_ref, j)) for j in range(nq)]
        for j in range(nq):
            qv = q_ref[steps.rows_of(j), :]
            for h in range(heads):
                s_scr[j * heads + h] = lax.dot_general(_head_only(qv, h), k_of[j](h), _NT, preferred_element_type=F32)
        ls = {}
        for j in range(nq):
            valid = _band_mask(steps.has_prev(i, j), max_dist)
            lse_tile = jnp.zeros((BLOCK, LANES), F32)
            for h in range(heads):
                s = jnp.where(valid, s_scr[j * heads + h], NEG)
                m = jnp.max(s, axis=-1, keepdims=True)
                if sink is not None:
                    sk = sink_ref[h]
                    m = jnp.maximum(m, sk)
                p = jnp.exp(s - m)
                l = jnp.sum(p, axis=-1, keepdims=True)
                if sink is not None:
                    l = l + jnp.exp(sk - m)
                p_scr[j * heads + h] = p.astype(BF16)
                ls[j, h] = l
                lse_tile = jnp.where(lane == h, m + jnp.log(l), lse_tile)
            lse_ref[steps.rows_of(j), :] = lse_tile
        for j in range(nq):
            for pr in range(heads // 2):
                he, ho = 2 * pr, 2 * pr + 1
                even = jnp.dot(p_scr[j * heads + he], v_of[j](he), preferred_element_type=F32) / ls[j, he]
                odd = jnp.dot(p_scr[j * heads + ho], v_of[j](ho), preferred_element_type=F32) / ls[j, ho]
                o_ref[steps.rows_of(j), LANES * pr:LANES * (pr + 1)] = jnp.where(lane < HEAD_DIM, even, odd).astype(BF16)

    in_specs = [steps.own(qw), steps.prev(kw), steps.own(kw), steps.prev(kw), steps.own(kw)]
    args = [q, k, k, v, v]
    if sink is not None:
        in_specs = [pl.BlockSpec(memory_space=pltpu.SMEM)] + in_specs
        args = [sink] + args
    return pl.pallas_call(
        body, name=name, grid=(steps.outer, steps.inner), in_specs=in_specs,
        out_specs=[steps.own(qw), steps.own(LANES)],
        out_shape=[jax.ShapeDtypeStruct((seq, qw), BF16), jax.ShapeDtypeStruct((seq, LANES), F32)],
        scratch_shapes=[pltpu.VMEM((nq * heads, BLOCK, 2 * BLOCK), F32), pltpu.VMEM((nq * heads, BLOCK, 2 * BLOCK), BF16)],
        compiler_params=_params(dimension_semantics=("arbitrary", "arbitrary")),
    )(*args)


def _banded_bwd(q, k, v, d_out, stat, sink, *, dil, heads, max_dist, nq, name, reduce_scatter=()):
    seq = q.shape[0]
    qw = kw = heads * HEAD_DIM
    steps = _BandSteps(seq, dil, nq)
    n_rs = len(reduce_scatter)
    n_in = 7 + n_rs
    n_flat = steps.outer * (steps.inner + 1)

    def body(*refs):
        refs = list(refs)
        sink_ref = refs.pop(0) if sink is not None else None
        (q_ref, kp_ref, kc_ref, vp_ref, vc_ref, do_ref, st_ref), partials = refs[:7], refs[7:n_in]
        refs = refs[n_in:]
        dsink_ref = refs.pop(0) if sink is not None else None
        (dq_ref, dk_ref, dv_ref), sums = refs[:3], refs[3:3 + n_rs]
        kcar, vcar, s_scr, dp_scr, p_scr, ds_scr = refs[3 + n_rs:9 + n_rs]
        r, i = pl.program_id(0), pl.program_id(1)
        if n_rs:
            exchange = _ReduceScatter(tuple(partials), tuple(sums), refs[9 + n_rs:])
            flat = r * (steps.inner + 1) + i

            @pl.when(flat == 0)
            def _():
                exchange.start()

            @pl.when(flat == min(2, n_flat - 1))
            def _():
                exchange.send_chip_sums()

        @pl.when(i == 0)
        def _():
            kcar[...] = jnp.zeros_like(kcar)
            vcar[...] = jnp.zeros_like(vcar)

        if sink is not None:
            @pl.when((i == 0) & (r == 0))
            def _():
                dsink_ref[...] = jnp.zeros_like(dsink_ref)

        @pl.when(i < steps.inner)
        def _():
            lane = lax.broadcasted_iota(jnp.int32, (1, LANES), 1)
            lane_q = lax.broadcasted_iota(jnp.int32, (BLOCK, LANES), 1)
            k_of = [_slabs_of(steps.keys(kp_ref, kc_ref, j)) for j in range(nq)]
            v_of = [_slabs_of(steps.keys(vp_ref, vc_ref, j)) for j in range(nq)]
            qms, doms = {}, {}
            for j in range(nq):
                qv, dov = q_ref[steps.rows_of(j), :], do_ref[steps.rows_of(j), :]
                for h in range(heads):
                    qms[j, h], doms[j, h] = _head_only(qv, h), _head_only(dov, h)
                    s_scr[j * heads + h] = lax.dot_general(qms[j, h], k_of[j](h), _NT, preferred_element_type=F32)
                    dp_scr[j * heads + h] = lax.dot_general(doms[j, h], v_of[j](h), _NT, preferred_element_type=F32)
            dsink_row = jnp.zeros((1, LANES), F32)
            if sink is not None:
                sink_row = jnp.zeros((1, LANES), F32)
                for h in range(heads):
                    sink_row = jnp.where(lane == h, sink_ref[h], sink_row)
            for j in range(nq):
                st = st_ref[steps.rows_of(j), :]
                valid = _band_mask(steps.has_prev(i, j), max_dist)
                for h in range(heads):
                    lse_h = st[:, h:h + 1]
                    delta = st[:, DELTA_LANE + h:DELTA_LANE + h + 1]
                    p = jnp.where(valid, jnp.exp(s_scr[j * heads + h] - lse_h), 0.0)
                    p_scr[j * heads + h] = p.astype(BF16)
                    ds_scr[j * heads + h] = (p * (dp_scr[j * heads + h] - delta)).astype(BF16)
                if sink is not None:
                    term = -jnp.exp(sink_row - st) * pltpu.roll(st, LANES - DELTA_LANE, 1)
                    dsink_row = dsink_row + jnp.sum(jnp.where(lane_q < heads, term, 0.0), axis=0, keepdims=True)
            for j in range(nq):
                for pr in range(heads // 2):
                    he, ho = 2 * pr, 2 * pr + 1
                    even = jnp.dot(ds_scr[j * heads + he], k_of[j](he), preferred_element_type=F32)
                    odd = jnp.dot(ds_scr[j * heads + ho], k_of[j](ho), preferred_element_type=F32)
                    dq_ref[steps.rows_of(j), LANES * pr:LANES * (pr + 1)] = (
                        jnp.where(lane_q < HEAD_DIM, even, odd).astype(BF16))
            if steps.consecutive:
                dk_ref[...] = kcar[...].astype(BF16)
                dv_ref[...] = vcar[...].astype(BF16)
            for j in range(nq):
                for slab in range(kw // LANES):
                    he, ho = j * heads + 2 * slab, j * heads + 2 * slab + 1
                    dk_j = (lax.dot_general(ds_scr[he], qms[j, 2 * slab], _TN, preferred_element_type=F32)
                            + lax.dot_general(ds_scr[ho], qms[j, 2 * slab + 1], _TN, preferred_element_type=F32))
                    dv_j = (lax.dot_general(p_scr[he], doms[j, 2 * slab], _TN, preferred_element_type=F32)
                            + lax.dot_general(p_scr[ho], doms[j, 2 * slab + 1], _TN, preferred_element_type=F32))
                    sl = slice(LANES * slab, LANES * (slab + 1))
                    own_rows = steps.rows_of(j)
                    if not steps.consecutive:
                        dk_ref[own_rows, sl] = (kcar[own_rows, sl] + dk_j[0:BLOCK]).astype(BF16)
                        dv_ref[own_rows, sl] = (vcar[own_rows, sl] + dv_j[0:BLOCK]).astype(BF16)
                    elif j == 0:
                        last = steps.rows_of(nq - 1)
                        dk_ref[last, sl] = (kcar[last, sl] + dk_j[0:BLOCK]).astype(BF16)
                        dv_ref[last, sl] = (vcar[last, sl] + dv_j[0:BLOCK]).astype(BF16)
                    else:
                        before = steps.rows_of(j - 1)
                        kcar[before, sl] += dk_j[0:BLOCK]
                        vcar[before, sl] += dv_j[0:BLOCK]
                    kcar[own_rows, sl] = dk_j[BLOCK:2 * BLOCK]
                    vcar[own_rows, sl] = dv_j[BLOCK:2 * BLOCK]
            if sink is not None:
                dsink_ref[0:1, :] += dsink_row

        @pl.when(i == steps.inner)
        def _():
            dk_ref[...] = kcar[...].astype(BF16)
            dv_ref[...] = vcar[...].astype(BF16)

        if n_rs:
            @pl.when(flat == n_flat - 1)
            def _():
                exchange.finish()

    own, prev = (lambda w: steps.own(w, clamp=True)), (lambda w: steps.prev(w, clamp=True))
    rs_shapes = [t.shape[1:] for t in reduce_scatter]
    in_specs = ([own(qw), prev(kw), own(kw), prev(kw), own(kw), own(qw), own(LANES)]
                + [pl.BlockSpec(memory_space=pl.ANY)] * n_rs)
    args = [q, k, k, v, v, d_out, stat, *reduce_scatter]
    out_specs = [own(qw), steps.late(kw), steps.late(kw)] + [_full(s) for s in rs_shapes]
    out_shape = [jax.ShapeDtypeStruct((seq, qw), BF16), jax.ShapeDtypeStruct((seq, kw), BF16),
                 jax.ShapeDtypeStruct((seq, kw), BF16)] + [jax.ShapeDtypeStruct(s, F32) for s in rs_shapes]
    if sink is not None:
        in_specs = [pl.BlockSpec(memory_space=pltpu.SMEM)] + in_specs
        args = [sink] + args
        out_specs = [_full((8, LANES))] + out_specs
        out_shape = [jax.ShapeDtypeStruct((8, LANES), F32)] + out_shape
    n_hb = nq * heads
    res = pl.pallas_call(
        body, name=name, grid=(steps.outer, steps.inner + 1), in_specs=in_specs, out_specs=out_specs,
        out_shape=out_shape,
        scratch_shapes=[pltpu.VMEM((steps.rows, kw), F32), pltpu.VMEM((steps.rows, kw), F32)]
        + [pltpu.VMEM((n_hb, BLOCK, 2 * BLOCK), F32)] * 2 + [pltpu.VMEM((n_hb, BLOCK, 2 * BLOCK), BF16)] * 2
        + (_ReduceScatter.scratch_shapes(rs_shapes) if n_rs else []),
        compiler_params=_params(dimension_semantics=("arbitrary", "arbitrary")),
    )(*args)
    if sink is not None:
        return (*res[1:4], res[0], *res[4:])
    return res


def _cross_fwd(q, mk, mv, tq=1024):
    seq = q.shape[0]

    def body(q_ref, mk_ref, mv_ref, o_ref, lse_ref, s_scr, p_scr):
        qv = q_ref[...]
        k_of, v_of = _slabs_of(mk_ref[...]), _slabs_of(mv_ref[...])
        lane = lax.broadcasted_iota(jnp.int32, (tq, LANES), 1)
        lse_tile = jnp.zeros((tq, LANES), F32)
        for h in range(C_HEADS):
            s_scr[h] = lax.dot_general(_head_only(qv, h), k_of(h), _NT, preferred_element_type=F32)
        ls = []
        for h in range(C_HEADS):
            s = s_scr[h]
            m = jnp.max(s, axis=-1, keepdims=True)
            p = jnp.exp(s - m)
            l = jnp.sum(p, axis=-1, keepdims=True)
            p_scr[h] = p.astype(BF16)
            ls.append(l)
            lse_tile = jnp.where(lane == h, m + jnp.log(l), lse_tile)
        for pr in range(C_HEADS // 2):
            even = jnp.dot(p_scr[2 * pr], v_of(2 * pr), preferred_element_type=F32) / ls[2 * pr]
            odd = jnp.dot(p_scr[2 * pr + 1], v_of(2 * pr + 1), preferred_element_type=F32) / ls[2 * pr + 1]
            o_ref[:, LANES * pr:LANES * (pr + 1)] = jnp.where(lane < HEAD_DIM, even, odd).astype(BF16)
        lse_ref[...] = lse_tile

    return pl.pallas_call(
        body, name="cross_fwd", grid=(seq // tq,),
        in_specs=[_row(tq, C_W), _full((N_MEM, C_W)), _full((N_MEM, C_W))],
        out_specs=[_row(tq, C_W), _row(tq, LANES)],
        out_shape=[jax.ShapeDtypeStruct((seq, C_W), BF16), jax.ShapeDtypeStruct((seq, LANES), F32)],
        scratch_shapes=[pltpu.VMEM((C_HEADS, tq, N_MEM), F32), pltpu.VMEM((C_HEADS, tq, N_MEM), BF16)],
        compiler_params=_params(dimension_semantics=("arbitrary",)),
    )(q, mk, mv)


def _cross_bwd(q, mk, mv, d_out, stat, tq=1024):
    seq = q.shape[0]

    def body(q_ref, mk_ref, mv_ref, do_ref, st_ref, dq_ref, dmk_ref, dmv_ref, s_scr, dp_scr, p_scr, ds_scr):
        @pl.when(pl.program_id(0) == 0)
        def _():
            dmk_ref[...] = jnp.zeros_like(dmk_ref)
            dmv_ref[...] = jnp.zeros_like(dmv_ref)

        qv, dov, st = q_ref[...], do_ref[...], st_ref[...]
        k_of, v_of = _slabs_of(mk_ref[...]), _slabs_of(mv_ref[...])
        qms = [_head_only(qv, h) for h in range(C_HEADS)]
        doms = [_head_only(dov, h) for h in range(C_HEADS)]
        for h in range(C_HEADS):
            s_scr[h] = lax.dot_general(qms[h], k_of(h), _NT, preferred_element_type=F32)
            dp_scr[h] = lax.dot_general(doms[h], v_of(h), _NT, preferred_element_type=F32)
        for h in range(C_HEADS):
            p = jnp.exp(s_scr[h] - st[:, h:h + 1])
            p_scr[h] = p.astype(BF16)
            ds_scr[h] = (p * (dp_scr[h] - st[:, DELTA_LANE + h:DELTA_LANE + h + 1])).astype(BF16)
        lane = lax.broadcasted_iota(jnp.int32, (tq, LANES), 1)
        for pr in range(C_HEADS // 2):
            sl = slice(LANES * pr, LANES * (pr + 1))
            even = jnp.dot(ds_scr[2 * pr], k_of(2 * pr), preferred_element_type=F32)
            odd = jnp.dot(ds_scr[2 * pr + 1], k_of(2 * pr + 1), preferred_element_type=F32)
            dq_ref[:, sl] = jnp.where(lane < HEAD_DIM, even, odd).astype(BF16)
            dmk_ref[:, sl] += (lax.dot_general(ds_scr[2 * pr], qms[2 * pr], _TN, preferred_element_type=F32)
                               + lax.dot_general(ds_scr[2 * pr + 1], qms[2 * pr + 1], _TN, preferred_element_type=F32))
            dmv_ref[:, sl] += (lax.dot_general(p_scr[2 * pr], doms[2 * pr], _TN, preferred_element_type=F32)
                               + lax.dot_general(p_scr[2 * pr + 1], doms[2 * pr + 1], _TN, preferred_element_type=F32))

    return pl.pallas_call(
        body, name="cross_bwd", grid=(seq // tq,),
        in_specs=[_row(tq, C_W), _full((N_MEM, C_W)), _full((N_MEM, C_W)), _row(tq, C_W), _row(tq, LANES)],
        out_specs=[_row(tq, C_W), _full((N_MEM, C_W)), _full((N_MEM, C_W))],
        out_shape=[jax.ShapeDtypeStruct((seq, C_W), BF16), jax.ShapeDtypeStruct((N_MEM, C_W), F32),
                   jax.ShapeDtypeStruct((N_MEM, C_W), F32)],
        scratch_shapes=[pltpu.VMEM((C_HEADS, tq, N_MEM), F32)] * 2 + [pltpu.VMEM((C_HEADS, tq, N_MEM), BF16)] * 2,
        compiler_params=_params(dimension_semantics=("arbitrary",)),
    )(q, mk, mv, d_out, stat)


def _per_head(tile, width):
    rows = tile.shape[0]
    lane = lax.broadcasted_iota(jnp.int32, (rows, LANES), 1)
    slabs = []
    for p in range(width // LANES):
        even = jnp.broadcast_to(tile[:, 2 * p:2 * p + 1], (rows, LANES))
        odd = jnp.broadcast_to(tile[:, 2 * p + 1:2 * p + 2], (rows, LANES))
        slabs.append(jnp.where(lane < HEAD_DIM, even, odd))
    return slabs[0] if len(slabs) == 1 else jnp.concatenate(slabs, axis=1)


def _with_delta(lse_tile, prod):
    rows = lse_tile.shape[0]
    lane = lax.broadcasted_iota(jnp.int32, (rows, LANES), 1)
    tile = lse_tile
    for p in range(prod.shape[1] // LANES):
        slab = prod[:, LANES * p:LANES * (p + 1)]
        even = jnp.sum(jnp.where(lane < HEAD_DIM, slab, 0.0), axis=-1, keepdims=True)
        odd = jnp.sum(jnp.where(lane >= HEAD_DIM, slab, 0.0), axis=-1, keepdims=True)
        tile = jnp.where(lane == DELTA_LANE + 2 * p, even, tile)
        tile = jnp.where(lane == DELTA_LANE + 2 * p + 1, odd, tile)
    return tile


def _mid(oa, lse_a, ob, lse_b, oc, lse_c, gate, x, target, w_out_full, post_g, tm=512):
    seq = x.shape[0]
    n_b = B_W // LANES

    def body(oa_ref, la_ref, b1_ref, l1_ref, b4_ref, l4_ref, b16_ref, l16_ref, oc_ref, lc_ref,
             gate_ref, x_ref, t_ref, w_ref, pg_ref,
             dh_ref, dg_ref, doa_ref, sa_ref, dob1_ref, sb1_ref, dob4_ref, sb4_ref, dob16_ref, sb16_ref,
             doc_ref, sc_ref, dw_ref, st_ref, scr_b4, scr_b16, scr_l4, scr_l16, scr_do, scr_sb):
        @pl.when(pl.program_id(0) == 0)
        def _():
            dw_ref[...] = jnp.zeros_like(dw_ref)
            st_ref[...] = jnp.zeros_like(st_ref)

        b1, l1 = b1_ref[...].astype(F32), l1_ref[...]
        b4, l4 = _load_permuted(b4_ref, scr_b4, 4), _load_permuted(l4_ref, scr_l4, 4)
        b16, l16 = _load_permuted(b16_ref, scr_b16, 16), _load_permuted(l16_ref, scr_l16, 16)
        lm = jnp.maximum(jnp.maximum(l1, l4), l16)
        e1, e4, e16 = jnp.exp(l1 - lm), jnp.exp(l4 - lm), jnp.exp(l16 - lm)
        den = e1 + e4 + e16
        lse_b_tile = lm + jnp.log(den)
        ob_v = _per_head(e1 / den, B_W) * b1 + _per_head(e4 / den, B_W) * b4 + _per_head(e16 / den, B_W) * b16
        o_all = jnp.concatenate([oa_ref[...].astype(F32), ob_v, oc_ref[...].astype(F32)], axis=1)
        g = gate_ref[...].astype(F32)
        sig = 1.0 / (1.0 + jnp.exp(-g))
        silu = g * sig
        y = (o_all * silu).astype(BF16)
        w = w_ref[...]
        z = jnp.dot(y, w, preferred_element_type=F32)
        rz = lax.rsqrt(jnp.mean(z * z, axis=-1, keepdims=True) + RMS_EPS)
        hn = z * rz
        pg = pg_ref[...]
        err = (x_ref[...] + hn * pg) - t_ref[...]
        loss = 0.5 * jnp.sum(jnp.mean(err * err, axis=-1, keepdims=True), axis=0, keepdims=True)
        dh = err * (1.0 / D_MODEL)
        dh_ref[...] = dh.astype(BF16)
        st_ref[0:1, :] += jnp.sum(dh * hn, axis=0, keepdims=True)
        st_ref[1:2, :] += jnp.broadcast_to(loss, (1, D_MODEL))
        dhn = dh * pg
        dz = (rz * (dhn - hn * jnp.mean(dhn * hn, axis=-1, keepdims=True))).astype(BF16)
        dy = lax.dot_general(dz, w, _NT, preferred_element_type=F32)
        dw_ref[...] += lax.dot_general(y, dz, _TN, preferred_element_type=F32)
        dg_ref[...] = (dy * o_all * (sig * (1.0 + g * (1.0 - sig)))).astype(BF16)
        d_o = (dy * silu).astype(BF16)
        prod = d_o.astype(F32) * o_all
        doa_ref[...] = d_o[:, 0:A_W]
        sa_ref[...] = _with_delta(la_ref[...], prod[:, 0:A_W])
        doc_ref[...] = d_o[:, A_W + B_W:D_MODEL]
        sc_ref[...] = _with_delta(lc_ref[...], prod[:, A_W + B_W:D_MODEL])
        d_ob = d_o[:, A_W:A_W + B_W]
        stat_b = _with_delta(lse_b_tile, prod[:, A_W:A_W + B_W])
        dob1_ref[...] = d_ob
        sb1_ref[...] = stat_b
        _put(scr_do, d_ob.astype(F32))
        _put(scr_sb, stat_b)
        _store_permuted(scr_do, dob4_ref, 4, BF16)
        _store_permuted(scr_sb, sb4_ref, 4, F32)
        _store_permuted(scr_do, dob16_ref, 16, BF16)
        _store_permuted(scr_sb, sb16_ref, 16, F32)

    p4 = lambda w: _perm_spec(tm, 4, w)
    p16 = lambda w: _perm_spec(tm, 16, w)
    in_specs = [_row(tm, A_W), _row(tm, LANES), _row(tm, B_W), _row(tm, LANES), p4(B_W), p4(LANES), p16(B_W), p16(LANES),
                _row(tm, C_W), _row(tm, LANES), _row(tm, D_MODEL), _row(tm, D_MODEL), _row(tm, D_MODEL),
                _full((D_MODEL, D_MODEL)), _full((1, D_MODEL))]
    sds = jax.ShapeDtypeStruct
    v4 = lambda w, dt: sds((seq // (BLOCK * 4), 4, BLOCK, w), dt)
    v16 = lambda w, dt: sds((seq // (BLOCK * 16), 16, BLOCK, w), dt)
    out_specs = [_row(tm, D_MODEL), _row(tm, D_MODEL), _row(tm, A_W), _row(tm, LANES), _row(tm, B_W), _row(tm, LANES),
                 p4(B_W), p4(LANES), p16(B_W), p16(LANES), _row(tm, C_W), _row(tm, LANES),
                 _full((D_MODEL, D_MODEL)), _full((8, D_MODEL))]
    out_shape = [sds((seq, D_MODEL), BF16), sds((seq, D_MODEL), BF16), sds((seq, A_W), BF16), sds((seq, LANES), F32),
                 sds((seq, B_W), BF16), sds((seq, LANES), F32), v4(B_W, BF16), v4(LANES, F32), v16(B_W, BF16),
                 v16(LANES, F32), sds((seq, C_W), BF16), sds((seq, LANES), F32),
                 sds((D_MODEL, D_MODEL), F32), sds((8, D_MODEL), F32)]
    res = pl.pallas_call(
        body, name="mid", grid=(seq // tm,), in_specs=in_specs, out_specs=out_specs, out_shape=out_shape,
        scratch_shapes=[pltpu.VMEM((n_b, tm, LANES), F32), pltpu.VMEM((n_b, tm, LANES), F32),
                        pltpu.VMEM((1, tm, LANES), F32), pltpu.VMEM((1, tm, LANES), F32),
                        pltpu.VMEM((n_b, tm, LANES), F32), pltpu.VMEM((1, tm, LANES), F32)],
        compiler_params=_params(dimension_semantics=("arbitrary",)),
    )(oa, lse_a, ob[1], lse_b[1], _perm_view(ob[4], 4), _perm_view(lse_b[4], 4), _perm_view(ob[16], 16),
      _perm_view(lse_b[16], 16), oc, lse_c, gate, x, target, w_out_full, post_g)
    dh, d_gate, do_a, st_a, do_b1, st_b1, do_b4, st_b4, do_b16, st_b16, do_c, st_c, d_wout, stats = res
    flat = lambda t: t.reshape(seq, t.shape[-1])
    d_b = {1: (do_b1, st_b1), 4: (flat(do_b4), flat(st_b4)), 16: (flat(do_b16), flat(st_b16))}
    return dh, d_gate, (do_a, st_a), d_b, (do_c, st_c), d_wout, stats


def _inproj_bwd(x, u, dh, pre_g, w_in_full, tabs, dqa, dka, dva, dqkv_b, dqc, dgate, tm=512):
    seq = x.shape[0]
    n_b = B_W // LANES

    def body(x_ref, u_ref, dh_ref, g_ref, w_hbm, c_ref, up_ref, dn_ref, dqa_ref, dka_ref, dva_ref,
             dq1, dk1, dv1, dq4, dk4, dv4, dq16, dk16, dv16, dqc_ref, dg_ref,
             gx_ref, dw_ref, st_ref, scr4, scr16, w_scr, w_sems, dp_ref):
        _stage_w_in(w_hbm, w_scr, w_sems)

        @pl.when(pl.program_id(0) == 0)
        def _():
            st_ref[...] = jnp.zeros_like(st_ref)
            dw_ref[...] = jnp.zeros_like(dw_ref)

        c, up, dn = c_ref[...], -up_ref[...], -dn_ref[...]
        unrot = lambda t: _rotate(t, c, up, dn)
        total = lambda r1, r4, r16: (r1[...].astype(F32) + _load_permuted(r4, scr4, 4)
                                     + _load_permuted(r16, scr16, 16))
        at = lambda piece: slice(*COLS[piece])
        dp_ref[:, at("qa")] = (unrot(dqa_ref[...].astype(F32)) * SCALE).astype(BF16)
        dp_ref[:, at("ka")] = unrot(_per_kv_head(dka_ref[...].astype(F32))).astype(BF16)
        dp_ref[:, at("va")] = _per_kv_head(dva_ref[...].astype(F32)).astype(BF16)
        dp_ref[:, at("ga")] = dg_ref[:, 0:A_W]
        dp_ref[:, at("qb")] = (unrot(total(dq1, dq4, dq16)) * SCALE).astype(BF16)
        dp_ref[:, at("kb")] = unrot(total(dk1, dk4, dk16)).astype(BF16)
        dp_ref[:, at("vb")] = total(dv1, dv4, dv16).astype(BF16)
        dp_ref[:, at("gb")] = dg_ref[:, A_W:A_W + B_W]
        dp_ref[:, at("qc")] = (dqc_ref[...].astype(F32) * SCALE).astype(BF16)
        dp_ref[:, at("gc")] = dg_ref[:, A_W + B_W:D_MODEL]
        du = lax.dot_general(dp_ref[...], w_scr[...], _NT, preferred_element_type=F32)
        res = lax.dot_general(u_ref[...], dp_ref[...], _TN, preferred_element_type=F32)
        for k in range(N_DEV):
            dw_ref[k] += res[:, SHARD_IN * k:SHARD_IN * (k + 1)]
        xv = x_ref[...]
        r = lax.rsqrt(jnp.mean(xv * xv, axis=-1, keepdims=True) + RMS_EPS)
        xh = xv * r
        st_ref[0:1, :] += jnp.sum(du * xh, axis=0, keepdims=True)
        dxh = du * g_ref[...]
        gx_ref[...] = dh_ref[...].astype(F32) + r * (dxh - xh * jnp.mean(dxh * xh, axis=-1, keepdims=True))

    in_specs = ([_row(tm, D_MODEL), _row(tm, D_MODEL), _row(tm, D_MODEL), _full((1, D_MODEL)),
                 pl.BlockSpec(memory_space=pl.ANY),
                 _row(tm, LANES), _row(tm, LANES), _row(tm, LANES), _row(tm, A_W), _row(tm, A_W), _row(tm, A_W)]
                + [_row(tm, B_W)] * 3 + [_perm_spec(tm, 4, B_W)] * 3 + [_perm_spec(tm, 16, B_W)] * 3
                + [_row(tm, C_W), _row(tm, D_MODEL)])
    dw_spec = pl.BlockSpec((N_DEV, D_MODEL, SHARD_IN), lambda i: (0, 0, 0), pipeline_mode=pl.Buffered(1))
    return pl.pallas_call(
        body, name="inproj_bwd", grid=(seq // tm,), in_specs=in_specs,
        out_specs=[_row(tm, D_MODEL), dw_spec, _full((8, D_MODEL))],
        out_shape=[jax.ShapeDtypeStruct((seq, D_MODEL), F32), jax.ShapeDtypeStruct((N_DEV, D_MODEL, SHARD_IN), F32),
                   jax.ShapeDtypeStruct((8, D_MODEL), F32)],
        scratch_shapes=[pltpu.VMEM((n_b, tm, LANES), F32), pltpu.VMEM((n_b, tm, LANES), F32)] + _w_in_scratch()
        + [pltpu.VMEM((tm, D_IN), BF16)],
        compiler_params=_params(dimension_semantics=("arbitrary",)),
    )(x, u, dh, pre_g, w_in_full, *tabs, dqa, dka, dva, *dqkv_b[1], *[_perm_view(t, 4) for t in dqkv_b[4]],
      *[_perm_view(t, 16) for t in dqkv_b[16]], dqc, dgate)


class _ReduceScatter:
    def __init__(self, ins, outs, scratch):
        self.n = n = len(ins)
        self.ins, self.outs = ins, outs
        self.mine, self.got, self.snd, self.rcv = (scratch[n * t:n * (t + 1)] for t in range(4))
        self.load_sems, self.d2d_send, self.d2d_recv, self.ici_send, self.ici_recv = scratch[4 * n:]
        self.pos = _mesh_pos()
        self.pairs = [(a, kk) for kk in (3, 1, 2) for a in range(n)]

    @staticmethod
    def scratch_shapes(shapes):
        return ([pltpu.VMEM((4,) + s, F32) for s in shapes] + [pltpu.VMEM((4,) + s, F32) for s in shapes]
                + [pltpu.VMEM((3,) + s, BF16) for s in shapes] + [pltpu.VMEM((3,) + s, BF16) for s in shapes]
                + [pltpu.SemaphoreType.DMA((len(shapes), 4))] * 5)

    def _chip(self, kk):
        x, y, _ = self.pos
        return (1 - x if kk & 2 else x, 1 - y if kk & 1 else y)

    def _load(self, a, kk):
        block = _dev_index((*self._chip(kk), self.pos[2]))
        return pltpu.make_async_copy(self.ins[a].at[block], self.mine[a].at[kk], self.load_sems.at[a, kk])

    def _swap(self, a, kk):
        x, y, c = self.pos
        return pltpu.make_async_remote_copy(
            src_ref=self.ins[a].at[_dev_index((*self._chip(kk), 1 - c))], dst_ref=self.got[a].at[kk],
            send_sem=self.d2d_send.at[a, kk], recv_sem=self.d2d_recv.at[a, kk],
            device_id=(x, y, 1 - c), device_id_type=MESH_ID)

    def _hop(self, a, kk):
        return pltpu.make_async_remote_copy(
            src_ref=self.snd[a].at[kk - 1], dst_ref=self.rcv[a].at[kk - 1], send_sem=self.ici_send.at[a, kk],
            recv_sem=self.ici_recv.at[a, kk], device_id=(*self._chip(kk), self.pos[2]), device_id_type=MESH_ID)

    def start(self):
        for kk in (3, 1, 2, 0):
            for a in range(self.n):
                self._load(a, kk).start()
                self._swap(a, kk).start()

    def send_chip_sums(self):
        for a, kk in self.pairs:
            self._load(a, kk).wait()
            self._swap(a, kk).wait_recv()
            self.snd[a][kk - 1] = (self.mine[a][kk] + self.got[a][kk]).astype(BF16)
            self._hop(a, kk).start()

    def finish(self):
        for a in range(self.n):
            self._load(a, 0).wait()
            self._swap(a, 0).wait_recv()
            acc = self.mine[a][0] + self.got[a][0]
            for kk in (1, 2, 3):
                self._hop(a, kk).wait_recv()
                acc = acc + self.rcv[a][kk - 1].astype(F32)
            self.outs[a][...] = acc
        for kk in range(4):
            for a in range(self.n):
                self._swap(a, kk).wait_send()
        for a, kk in self.pairs:
            self._hop(a, kk).wait_send()


def _local_step(x, mem, pre_g, w_in, sink, mem_g, w_mem, w_out, post_g, target):
    u, *tabs, w_in_full = _prep(x, pre_g, w_in)
    qa, ka, va, qkv_b, qc, gate, w_mem_all, w_out_all = _inproj(u, w_in_full, tabs, w_mem, w_out)
    w_mem_full = w_mem_all.reshape(D_MODEL, 2 * C_W)
    w_out_full = w_out_all.reshape(D_MODEL, D_MODEL)
    mn, mk, mv = _memkv_fwd(mem, mem_g, w_mem_full)

    a_cfg = dict(dil=1, heads=A_HEADS, max_dist=BLOCK - 1, nq=ATTN_BLOCKS_PER_STEP)
    b_cfgs = {dil: dict(dil=dil, heads=B_HEADS, max_dist=win // dil, nq=ATTN_BLOCKS_PER_STEP)
              for win, dil in B_CONFIGS}
    oa, lse_a = _banded_fwd(qa, ka, va, sink, name="attn_a_fwd", **a_cfg)
    ob, lse_b = {}, {}
    for dil, cfg in b_cfgs.items():
        ob[dil], lse_b[dil] = _banded_fwd(*qkv_b[dil], None, name=f"attn_b{dil}_fwd", **cfg)
    oc, lse_c = _cross_fwd(qc, mk, mv)

    dh, d_gate, d_a, d_b, d_c, d_wout, st_mid = _mid(oa, lse_a, ob, lse_b, oc, lse_c, gate, x, target, w_out_full, post_g)

    dqc, dmk, dmv = _cross_bwd(qc, mk, mv, *d_c)
    d_wmem, st_mem = _memkv_bwd(mem, mem_g, mn, w_mem_full, dmk, dmv)
    dqkv_b = {dil: _banded_bwd(*qkv_b[dil], *d_b[dil], None, name=f"attn_b{dil}_bwd", **cfg)
              for dil, cfg in b_cfgs.items()}
    dqa, dka, dva, dsink, g_wmem, g_wout = _banded_bwd(
        qa, ka, va, *d_a, sink, name="attn_a_bwd", **a_cfg,
        reduce_scatter=(d_wmem.reshape(N_DEV, SHARD_ROWS, 2 * C_W), d_wout.reshape(N_DEV, SHARD_ROWS, D_MODEL)))

    grad_x, d_win, st_pre = _inproj_bwd(x, u, dh, pre_g, w_in_full, tabs, dqa, dka, dva, dqkv_b, dqc, d_gate)

    dsink_row = jnp.pad(dsink[0:1, :], ((0, 0), (0, D_MODEL - LANES)))
    stats = jnp.concatenate([st_pre[0:1], st_mem[0:1], st_mid[0:1], dsink_row, st_mid[1:2],
                             jnp.zeros((3, D_MODEL), F32)], axis=0)
    return grad_x, d_win, g_wmem, g_wout, stats


def _prep(x, pre_g, w_in, tm=1024):
    seq = x.shape[0]
    n_steps = seq // tm
    parts = 2
    rows = D_MODEL // parts
    relay_at = min(3, n_steps - 1)
    j = jnp.arange(LANES) % HEAD_DIM
    freq = (ROPE_THETA ** (-(2 * (j % (ROT_DIM // 2))).astype(F32) / ROT_DIM))[None, :]
    SIB, NB_X, NB_Y, RELAY, FWD = 0, 1, 2, 3, 4

    def body(x_ref, g_ref, f_ref, win_ref, u_ref, c_ref, up_ref, dn_ref, win_out, win_b,
             send_sems, recv_sems, local_sems):
        step = pl.program_id(0)
        px, py, pc = _mesh_pos()
        me, sibling = (px, py, pc), (px, py, 1 - pc)
        others = lambda core: ((1 - px, py, core), (px, 1 - py, core), (1 - px, 1 - py, core))
        x_nb, y_nb, diag = others(pc)
        relay_from = [x_nb, y_nb]
        relay_to = [y_nb, x_nb]

        def src(a):
            return win_b.at[pl.ds(rows * a, rows)]

        def slot(a, p):
            return win_out.at[_dev_index(p), pl.ds(rows * a, rows)]

        def copy(a, k, block, to, own=False):
            return pltpu.make_async_remote_copy(
                src_ref=src(a) if own else slot(a, block), dst_ref=slot(a, block),
                send_sem=send_sems.at[a, k], recv_sem=recv_sems.at[a, k], device_id=to, device_id_type=MESH_ID)

        def first_sends():
            return [copy(0, NB_X, me, x_nb, own=True), copy(1, NB_Y, me, y_nb, own=True),
                    copy(1, NB_X, me, x_nb, own=True), copy(0, NB_Y, me, y_nb, own=True),
                    copy(0, SIB, me, sibling, own=True), copy(1, SIB, me, sibling, own=True)]

        def relay(a):
            return copy(a, RELAY, relay_from[a], relay_to[a])

        def to_sibling(a, which):
            return copy(a, FWD + which, others(pc)[which], sibling)

        def local(a):
            return pltpu.make_async_copy(src(a), slot(a, me), local_sems.at[a])

        @pl.when(step == 0)
        def _():
            win_b[...] = win_ref[...].astype(BF16)
            for a in range(parts):
                local(a).start()
            for cp in first_sends():
                cp.start()

        @pl.when(step == relay_at)
        def _():
            for a in range(parts):
                copy(a, NB_X + a, relay_from[a], me).wait_recv()
                relay(a).start()
                to_sibling(a, a).start()

        xv = x_ref[...]
        r = lax.rsqrt(jnp.mean(xv * xv, axis=-1, keepdims=True) + RMS_EPS)
        u_ref[...] = ((xv * r) * g_ref[...]).astype(BF16)
        pos = (lax.broadcasted_iota(jnp.int32, (tm, LANES), 0) + step * tm).astype(F32)
        head_lane = lax.broadcasted_iota(jnp.int32, (tm, LANES), 1) % HEAD_DIM
        ang = pos * f_ref[...]
        cos, sin = jnp.cos(ang), jnp.sin(ang)
        half = ROT_DIM // 2
        c_ref[...] = jnp.where(head_lane < ROT_DIM, cos, 1.0)
        up_ref[...] = jnp.where((head_lane >= half) & (head_lane < ROT_DIM), sin, 0.0)
        dn_ref[...] = jnp.where(head_lane < half, -sin, 0.0)

        @pl.when(step == n_steps - 1)
        def _():
            copy(1, NB_X, x_nb, me).wait_recv()
            to_sibling(1, 0).start()
            copy(0, NB_Y, y_nb, me).wait_recv()
            to_sibling(0, 1).start()
            for a in range(parts):
                copy(a, RELAY, diag, me).wait_recv()
                to_sibling(a, 2).start()
            for a in range(parts):
                copy(a, SIB, sibling, me).wait_recv()
                for which in range(3):
                    copy(a, FWD + which, others(1 - pc)[which], me).wait_recv()
            for cp in first_sends():
                cp.wait_send()
            for a in range(parts):
                relay(a).wait_send()
                for which in range(3):
                    to_sibling(a, which).wait_send()
                local(a).wait()

    return pl.pallas_call(
        body, name="prep", grid=(n_steps,),
        in_specs=[_row(tm, D_MODEL), _full((1, D_MODEL)), _full((1, LANES)), _full(w_in.shape)],
        out_specs=[_row(tm, D_MODEL), _row(tm, LANES), _row(tm, LANES), _row(tm, LANES),
                   pl.BlockSpec(memory_space=pl.ANY)],
        out_shape=[jax.ShapeDtypeStruct((seq, D_MODEL), BF16)] + [jax.ShapeDtypeStruct((seq, LANES), F32)] * 3
        + [jax.ShapeDtypeStruct((N_DEV,) + w_in.shape, BF16)],
        scratch_shapes=[pltpu.VMEM(w_in.shape, BF16), pltpu.SemaphoreType.DMA((parts, FWD + 3)),
                        pltpu.SemaphoreType.DMA((parts, FWD + 3)), pltpu.SemaphoreType.DMA((parts,))],
        compiler_params=_params(dimension_semantics=("arbitrary",)),
    )(x, pre_g, freq, w_in)


def _exchange_grads(d_win, stats):
    def body(win, st, g_win, r_st, send_sems, recv_sems, local_sem, *scratch):
        exchange = _ReduceScatter((win,), (g_win,), scratch)
        exchange.start()
        pos = _mesh_pos()
        me = _dev_index(pos)
        own = pltpu.make_async_copy(st, r_st.at[me], local_sem)
        own.start()
        copies = []
        for s in range(1, N_DEV):
            peer = _xor_peer(pos, s)
            mk = lambda slot: pltpu.make_async_remote_copy(
                src_ref=st, dst_ref=r_st.at[slot], send_sem=send_sems.at[s], recv_sem=recv_sems.at[s],
                device_id=peer, device_id_type=MESH_ID)
            send, arrival = mk(me), mk(_dev_index(peer))
            send.start()
            copies.append((send, arrival))
        exchange.send_chip_sums()
        exchange.finish()
        for send, arrival in copies:
            arrival.wait_recv()
            send.wait_send()
        own.wait()

    hbm = pl.BlockSpec(memory_space=pl.ANY)
    shard = d_win.shape[1:]
    return pl.pallas_call(
        body, name="exchange_grads", in_specs=[hbm, hbm],
        out_specs=[pl.BlockSpec(memory_space=pltpu.VMEM), hbm],
        out_shape=[jax.ShapeDtypeStruct(shard, F32), jax.ShapeDtypeStruct((N_DEV,) + stats.shape, F32)],
        scratch_shapes=[pltpu.SemaphoreType.DMA((N_DEV,)), pltpu.SemaphoreType.DMA((N_DEV,)), pltpu.SemaphoreType.DMA(())]
        + _ReduceScatter.scratch_shapes([shard]),
        compiler_params=_params(),
    )(d_win, stats)


WEIGHT_ORDER = ("pre_norm", "w_in", "sink_a", "mem_norm", "w_mem_kv", "w_out", "post_norm")


def _adamw_all(grads, r_stats, weights, moments_m, moments_v):
    n = len(WEIGHT_ORDER)
    stat_row = {"pre_norm": 0, "mem_norm": 1, "post_norm": 2, "sink_a": 3}

    def body(*refs):
        gw_in, gw_mem, gw_out, st_ref = refs[0:4]
        w_refs, m_refs, v_refs = (dict(zip(WEIGHT_ORDER, refs[4 + n * t:4 + n * (t + 1)])) for t in range(3))
        loss_ref = refs[4 + 3 * n]
        outs = refs[5 + 3 * n:]
        g_small = st_ref[0]
        for s in range(1, N_DEV):
            g_small = g_small + st_ref[s]
        loss_ref[...] = g_small[4:5, 0:1]
        big = {"w_in": gw_in, "w_mem_kv": gw_mem, "w_out": gw_out}
        for i, name in enumerate(WEIGHT_ORDER):
            if name in big:
                g = big[name][...]
                at = lambda ref: ref[0]
            else:
                width = w_refs[name].shape[-1]
                g = g_small[stat_row[name]:stat_row[name] + 1, 0:width]
                at = lambda ref: ref[...]
            m2 = ADAM_B1 * at(m_refs[name]) + (1.0 - ADAM_B1) * g
            v2 = ADAM_B2 * at(v_refs[name]) + (1.0 - ADAM_B2) * (g * g)
            m_hat = m2 / (1.0 - ADAM_B1 ** ADAM_STEP)
            v_hat = v2 / (1.0 - ADAM_B2 ** ADAM_STEP)
            delta = -ADAM_LR * (m_hat / (jnp.sqrt(v_hat) + ADAM_EPS) + ADAM_WD * at(w_refs[name]))
            for kind, val in enumerate((g, delta, m2, v2)):
                out = outs[kind * n + i]
                if name in big:
                    out[0] = val
                else:
                    out[...] = val

    shapes = [weights[name].shape for name in WEIGHT_ORDER]
    res = pl.pallas_call(
        body, name="adamw_all",
        out_shape=[jax.ShapeDtypeStruct((1, 1), F32)] + [jax.ShapeDtypeStruct(sh, F32) for sh in shapes] * 4,
        compiler_params=_params(),
    )(grads["w_in"], grads["w_mem_kv"], grads["w_out"], r_stats,
      *[weights[k] for k in WEIGHT_ORDER], *[moments_m[k] for k in WEIGHT_ORDER], *[moments_v[k] for k in WEIGHT_ORDER])
    return res[0].reshape(()), res[1:]


def kernel(x, mem, pre_norm, w_in, sink_a, mem_norm, w_mem_kv, w_out, post_norm, loss_target, m_pre_norm, m_w_in, m_sink_a, m_mem_norm, m_w_mem_kv, m_w_out, m_post_norm, v_pre_norm, v_w_in, v_sink_a, v_mem_norm, v_w_mem_kv, v_w_out, v_post_norm):
    sink = jnp.pad(sink_a[0], (0, 8 - A_HEADS))
    grad_x, d_win, g_wmem, g_wout, stats = _local_step(
        x[0], mem[0], pre_norm, w_in[0], sink, mem_norm, w_mem_kv[0], w_out[0], post_norm, loss_target[0])
    g_win, r_stats = _exchange_grads(d_win, stats)
    weights = dict(pre_norm=pre_norm, w_in=w_in, sink_a=sink_a, mem_norm=mem_norm, w_mem_kv=w_mem_kv, w_out=w_out,
                   post_norm=post_norm)
    moments_m = dict(pre_norm=m_pre_norm, w_in=m_w_in, sink_a=m_sink_a, mem_norm=m_mem_norm, w_mem_kv=m_w_mem_kv,
                     w_out=m_w_out, post_norm=m_post_norm)
    moments_v = dict(pre_norm=v_pre_norm, w_in=v_w_in, sink_a=v_sink_a, mem_norm=v_mem_norm, w_mem_kv=v_w_mem_kv,
                     w_out=v_w_out, post_norm=v_post_norm)
    loss, rest = _adamw_all(dict(w_in=g_win, w_mem_kv=g_wmem, w_out=g_wout), r_stats, weights, moments_m, moments_v)
    return (loss, grad_x[None], *rest)
```

```python
import jax
import jax.numpy as jnp
from jax import lax
from jax.experimental import pallas as pl
from jax.experimental.pallas import tpu as pltpu

F32 = jnp.float32
BF16 = jnp.bfloat16

D_MODEL = 1024
HEAD_DIM = 64
ROT_DIM = 16
ROPE_THETA = 500000.0
BLOCK = 128
LANES = 128
N_MEM = 256
RMS_EPS = 1e-6
SCALE = HEAD_DIM ** -0.5
A_HEADS = 6
B_HEADS = 6
C_HEADS = 4
A_W, A_KV_W, B_W, C_W = 384, 128, 384, 256
_IN_PIECES = (("qa", A_W), ("ka", A_KV_W), ("va", A_KV_W), ("ga", A_W), ("qb", B_W), ("kb", B_W), ("vb", B_W),
              ("gb", B_W), ("qc", C_W), ("gc", C_W))
COLS, D_IN = {}, 0
for _name, _width in _IN_PIECES:
    COLS[_name] = (D_IN, D_IN + _width)
    D_IN += _width
N_DEV = 8
SHARD_IN = D_IN // N_DEV
SHARD_ROWS = D_MODEL // N_DEV
B_CONFIGS = ((128, 1), (512, 4), (2048, 16))
DILS = (4, 16)
NEG = -1e30
ATTN_BLOCKS_PER_STEP = 4
ATTN_BWD_CLASSES_PER_STEP = 8
DELTA_LANE = 64
VMEM_LIMIT = 56 * 1024 * 1024

ADAM_LR, ADAM_B1, ADAM_B2, ADAM_EPS, ADAM_WD, ADAM_STEP = 0.001, 0.9, 0.999, 1e-08, 0.01, 10
MESH_ID = pl.DeviceIdType.MESH


def _params(**kw):
    return pltpu.CompilerParams(vmem_limit_bytes=VMEM_LIMIT, **kw)


def _full(shape):
    n = len(shape)
    return pl.BlockSpec(shape, lambda *_: (0,) * n)


def _row(tm, w):
    return pl.BlockSpec((tm, w), lambda i: (i, 0))


def _mesh_pos():
    return lax.axis_index("x"), lax.axis_index("y"), lax.axis_index("c")


def _dev_index(pos):
    return 4 * pos[0] + 2 * pos[1] + pos[2]


def _xor_peer(pos, s):
    x, y, c = pos
    return (1 - x if s & 4 else x, 1 - y if s & 2 else y, 1 - c if s & 1 else c)


def _perm_view(a, dil):
    return a.reshape(a.shape[0] // (BLOCK * dil), dil, BLOCK, a.shape[1])


def _perm_spec(tm, dil, w):
    chunk = BLOCK * dil
    if tm >= chunk:
        return pl.BlockSpec((tm // chunk, dil, BLOCK, w), lambda i: (i, 0, 0, 0))
    per = chunk // tm
    return pl.BlockSpec((1, dil, tm // dil, w), lambda i: (i // per, 0, i % per, 0))


def _put(scr, val):
    for c in range(val.shape[1] // LANES):
        scr[c] = val[:, LANES * c:LANES * (c + 1)]


def _get(scr):
    n = scr.shape[0]
    return scr[0] if n == 1 else jnp.concatenate([scr[c] for c in range(n)], axis=1)


def _get_class(scr, r, dil):
    n, rows = scr.shape[0], scr.shape[1]
    parts = [scr.at[c][pl.ds(r, rows // dil, stride=dil), :] for c in range(n)]
    return parts[0] if n == 1 else jnp.concatenate(parts, axis=1)


def _store_permuted(scr, out_ref, dil, dtype):
    for r in range(dil):
        out_ref[0, r] = _get_class(scr, r, dil).astype(dtype)


def _load_permuted(in_ref, scr, dil):
    n, rows = scr.shape[0], scr.shape[1]
    for r in range(dil):
        val = in_ref[0, r].astype(F32)
        for c in range(n):
            scr.at[c][pl.ds(r, rows // dil, stride=dil), :] = val[:, LANES * c:LANES * (c + 1)]
    return _get(scr)


def _rotate128(t, c, up, dn):
    half = ROT_DIM // 2
    return t * c + pltpu.roll(t, half, 1) * up + pltpu.roll(t, LANES - half, 1) * dn


def _rotate(t, c, up, dn):
    outs = [_rotate128(t[:, LANES * j:LANES * (j + 1)], c, up, dn) for j in range(t.shape[1] // LANES)]
    return outs[0] if len(outs) == 1 else jnp.concatenate(outs, axis=1)


def _per_query_head(kv):
    lane = lax.broadcasted_iota(jnp.int32, kv.shape, 1)
    other = pltpu.roll(kv, HEAD_DIM, 1)
    return jnp.concatenate([jnp.where(lane < HEAD_DIM, kv, other), kv, jnp.where(lane < HEAD_DIM, other, kv)], axis=1)


def _per_kv_head(d):
    s0, s1, s2 = (d[:, LANES * p:LANES * (p + 1)] for p in range(3))
    lane = lax.broadcasted_iota(jnp.int32, s0.shape, 1)
    return jnp.where(lane < HEAD_DIM, s0 + pltpu.roll(s0, HEAD_DIM, 1) + s1, s1 + s2 + pltpu.roll(s2, HEAD_DIM, 1))


def _w_in_scratch():
    return [pltpu.VMEM((D_MODEL, D_IN), BF16), pltpu.SemaphoreType.DMA((N_DEV,))]


def _stage_w_in(w_hbm, w_scr, sems):
    @pl.when(pl.program_id(0) == 0)
    def _():
        copies = [pltpu.make_async_copy(w_hbm.at[k], w_scr.at[:, pl.ds(SHARD_IN * k, SHARD_IN)], sems.at[k])
                  for k in range(N_DEV)]
        for cp in copies:
            cp.start()
        for cp in copies:
            cp.wait()


def _inproj(u, w_in_full, tabs, w_mem, w_out, tm=1024):
    seq = u.shape[0]
    n_chunk = D_IN // LANES
    n_steps = seq // tm

    def body(u_ref, w_hbm, c_ref, up_ref, dn_ref, wm_ref, wo_ref, qa_ref, ka_ref, va_ref,
             qb1_ref, kb1_ref, vb1_ref, qb4_ref, kb4_ref, vb4_ref, qb16_ref, kb16_ref, vb16_ref,
             qc_ref, gate_ref, wm_all, wo_all, proj, w_scr, w_sems, wm_b, wo_b, send_sems, recv_sems, local_sems):
        step = pl.program_id(0)
        shards, gathered = (wm_b, wo_b), (wm_all, wo_all)

        def gather_copies(arriving):
            pos = _mesh_pos()
            me = _dev_index(pos)
            local = [] if arriving else [
                pltpu.make_async_copy(shards[a], gathered[a].at[me], local_sems.at[a]) for a in range(2)]
            remote = []
            for s in range(1, N_DEV):
                peer = _xor_peer(pos, s)
                for a in range(2):
                    remote.append(pltpu.make_async_remote_copy(
                        src_ref=shards[a], dst_ref=gathered[a].at[_dev_index(peer) if arriving else me],
                        send_sem=send_sems.at[a, s], recv_sem=recv_sems.at[a, s], device_id=peer,
                        device_id_type=MESH_ID))
            return local, remote

        @pl.when(step == 0)
        def _():
            wm_b[...] = wm_ref[...].astype(BF16)
            wo_b[...] = wo_ref[...].astype(BF16)
            local, sends = gather_copies(arriving=False)
            for cp in local + sends:
                cp.start()

        _stage_w_in(w_hbm, w_scr, w_sems)
        u = u_ref[...]
        for n0 in range(0, D_IN, D_MODEL):
            acc = jnp.dot(u, w_scr[:, n0:n0 + D_MODEL], preferred_element_type=F32)
            for c3 in range(D_MODEL // LANES):
                proj[n0 // LANES + c3] = acc[:, LANES * c3:LANES * (c3 + 1)]
        c, up, dn = c_ref[...], up_ref[...], dn_ref[...]

        def chunks_of(piece):
            lo, hi = COLS[piece]
            return range(lo // LANES, hi // LANES)

        def cols(piece, rot=False, scale=None):
            parts = []
            for ch in chunks_of(piece):
                t = proj[ch]
                if rot:
                    t = _rotate128(t, c, up, dn)
                if scale is not None:
                    t = t * scale
                parts.append(t)
            return parts[0] if len(parts) == 1 else jnp.concatenate(parts, axis=1)

        qa_ref[...] = cols("qa", True, SCALE).astype(BF16)
        ka_ref[...] = _per_query_head(cols("ka", True)).astype(BF16)
        va_ref[...] = _per_query_head(cols("va")).astype(BF16)
        gate_ref[:, 0:A_W] = cols("ga").astype(BF16)
        gate_ref[:, A_W:A_W + B_W] = cols("gb").astype(BF16)
        gate_ref[:, A_W + B_W:D_MODEL] = cols("gc").astype(BF16)
        qc_ref[...] = cols("qc", False, SCALE).astype(BF16)
        for ch in chunks_of("qb"):
            proj[ch] = _rotate128(proj[ch], c, up, dn) * SCALE
        for ch in chunks_of("kb"):
            proj[ch] = _rotate128(proj[ch], c, up, dn)
        for piece, nat, p4, p16 in (("qb", qb1_ref, qb4_ref, qb16_ref), ("kb", kb1_ref, kb4_ref, kb16_ref),
                                    ("vb", vb1_ref, vb4_ref, vb16_ref)):
            chunks = chunks_of(piece)
            nat[...] = jnp.concatenate([proj[ch] for ch in chunks], axis=1).astype(BF16)
            for dil, ref in ((4, p4), (16, p16)):
                span = min(tm, BLOCK * dil)
                for cc in range(tm // span):
                    for rr in range(dil):
                        ref[cc, rr] = jnp.concatenate(
                            [proj.at[ch][pl.ds(cc * span + rr, span // dil, stride=dil), :] for ch in chunks],
                            axis=1).astype(BF16)

        @pl.when(step == n_steps - 1)
        def _():
            for cp in gather_copies(arriving=True)[1]:
                cp.wait_recv()
            local, sends = gather_copies(arriving=False)
            for cp in sends:
                cp.wait_send()
            for cp in local:
                cp.wait()

    nat_w = (A_W, A_W, A_W, B_W, B_W, B_W)
    out_specs = [_row(tm, w) for w in nat_w]
    out_shape = [jax.ShapeDtypeStruct((seq, w), BF16) for w in nat_w]
    for dil in DILS:
        out_specs += [_perm_spec(tm, dil, B_W)] * 3
        out_shape += [jax.ShapeDtypeStruct((seq // (BLOCK * dil), dil, BLOCK, B_W), BF16)] * 3
    hbm = pl.BlockSpec(memory_space=pl.ANY)
    out_specs += [_row(tm, C_W), _row(tm, D_MODEL), hbm, hbm]
    out_shape += [jax.ShapeDtypeStruct((seq, C_W), BF16), jax.ShapeDtypeStruct((seq, D_MODEL), BF16),
                  jax.ShapeDtypeStruct((N_DEV,) + w_mem.shape, BF16), jax.ShapeDtypeStruct((N_DEV,) + w_out.shape, BF16)]
    res = pl.pallas_call(
        body, name="inproj", grid=(n_steps,),
        in_specs=[_row(tm, D_MODEL), hbm, _row(tm, LANES), _row(tm, LANES), _row(tm, LANES),
                  _full(w_mem.shape), _full(w_out.shape)],
        out_specs=out_specs, out_shape=out_shape,
        scratch_shapes=[pltpu.VMEM((n_chunk, tm, LANES), F32)] + _w_in_scratch()
        + [pltpu.VMEM(w_mem.shape, BF16), pltpu.VMEM(w_out.shape, BF16), pltpu.SemaphoreType.DMA((2, N_DEV)),
           pltpu.SemaphoreType.DMA((2, N_DEV)), pltpu.SemaphoreType.DMA((2,))],
        compiler_params=_params(dimension_semantics=("arbitrary",)),
    )(u, w_in_full, *tabs, w_mem, w_out)
    qa, ka, va = res[0:3]
    qkv_b = {1: res[3:6], 4: [t.reshape(seq, B_W) for t in res[6:9]], 16: [t.reshape(seq, B_W) for t in res[9:12]]}
    return qa, ka, va, qkv_b, res[12], res[13], res[14], res[15]


def _memkv_fwd(mem, mem_g, w_mem_full):
    def body(mem_ref, g_ref, w_ref, mn_ref, mk_ref, mv_ref):
        mv_ = mem_ref[...]
        r = lax.rsqrt(jnp.mean(mv_ * mv_, axis=-1, keepdims=True) + RMS_EPS)
        mn = ((mv_ * r) * g_ref[...]).astype(BF16)
        mn_ref[...] = mn
        mkv = jnp.dot(mn, w_ref[...], preferred_element_type=F32)
        mk_ref[...] = mkv[:, 0:C_W].astype(BF16)
        mv_ref[...] = mkv[:, C_W:2 * C_W].astype(BF16)

    return pl.pallas_call(
        body, name="memkv_fwd",
        out_shape=[jax.ShapeDtypeStruct((N_MEM, D_MODEL), BF16),
                   jax.ShapeDtypeStruct((N_MEM, C_W), BF16), jax.ShapeDtypeStruct((N_MEM, C_W), BF16)],
        compiler_params=_params(),
    )(mem, mem_g, w_mem_full)


def _memkv_bwd(mem, mem_g, mn, w_mem_full, dmk, dmv):
    def body(mem_ref, g_ref, mn_ref, w_ref, dmk_ref, dmv_ref, dw_ref, st_ref):
        dmkv = jnp.concatenate([dmk_ref[...], dmv_ref[...]], axis=1).astype(BF16)
        dw_ref[...] = lax.dot_general(mn_ref[...], dmkv, (((0,), (0,)), ((), ())), preferred_element_type=F32)
        dmn = lax.dot_general(dmkv, w_ref[...], (((1,), (1,)), ((), ())), preferred_element_type=F32)
        mv_ = mem_ref[...]
        r = lax.rsqrt(jnp.mean(mv_ * mv_, axis=-1, keepdims=True) + RMS_EPS)
        st_ref[...] = jnp.zeros_like(st_ref)
        st_ref[0:1, :] = jnp.sum(dmn * (mv_ * r), axis=0, keepdims=True)

    return pl.pallas_call(
        body, name="memkv_bwd",
        out_shape=[jax.ShapeDtypeStruct((D_MODEL, 2 * C_W), F32), jax.ShapeDtypeStruct((8, D_MODEL), F32)],
        compiler_params=_params(),
    )(mem, mem_g, mn, w_mem_full, dmk, dmv)


def _band_mask(has_prev, max_dist):
    qi = lax.broadcasted_iota(jnp.int32, (BLOCK, 2 * BLOCK), 0)
    kj = lax.broadcasted_iota(jnp.int32, (BLOCK, 2 * BLOCK), 1)
    dist = qi + BLOCK - kj
    return (dist >= 0) & (dist <= max_dist) & ((kj >= BLOCK) | has_prev)


_NT = (((1,), (1,)), ((), ()))
_TN = (((0,), (0,)), ((), ()))


def _head_only(val, h):
    slab = _slabs_of(val)(h)
    lane = lax.broadcasted_iota(jnp.int32, slab.shape, 1)
    keep = (lane < HEAD_DIM) if h % 2 == 0 else (lane >= HEAD_DIM)
    return jnp.where(keep, slab, jnp.zeros((), slab.dtype))


def _slabs_of(val):
    return lambda h: val[:, LANES * (h // 2):LANES * (h // 2 + 1)]


class _BandSteps:
    def __init__(self, seq, dil, nq):
        self.nq, self.rows, self.consecutive = nq, nq * BLOCK, dil == 1
        nb = seq // dil // BLOCK
        if self.consecutive:
            assert nb % nq == 0
            self.outer, self.inner, self.stride = 1, nb // nq, 1
        else:
            assert dil % nq == 0
            self.outer, self.inner, self.stride = dil // nq, nb, dil // nq

    def own(self, w, clamp=False):
        cur = (lambda i: jnp.minimum(i, self.inner - 1)) if clamp else (lambda i: i)
        return pl.BlockSpec((self.rows, w), lambda r, i: (cur(i) * self.stride + r, 0))

    def prev(self, w, clamp=False):
        cur = (lambda i: jnp.minimum(i, self.inner - 1)) if clamp else (lambda i: i)
        if self.consecutive:
            return pl.BlockSpec((BLOCK, w), lambda r, i: (jnp.maximum(cur(i) * self.nq - 1, 0), 0))
        return pl.BlockSpec((self.rows, w), lambda r, i: (jnp.maximum(cur(i) - 1, 0) * self.stride + r, 0))

    def late(self, w):
        return pl.BlockSpec((self.rows, w), lambda r, i: (jnp.maximum(i - 1, 0) * self.stride + r, 0))

    def rows_of(self, j):
        return slice(BLOCK * j, BLOCK * (j + 1))

    def keys(self, p_ref, c_ref, j):
        if not self.consecutive:
            before = p_ref[self.rows_of(j), :]
        elif j == 0:
            before = p_ref[...]
        else:
            before = c_ref[self.rows_of(j - 1), :]
        return jnp.concatenate([before, c_ref[self.rows_of(j), :]], axis=0)

    def has_prev(self, i, j):
        return True if (self.consecutive and j > 0) else (i > 0)


def _banded_fwd(q, k, v, sink, *, dil, heads, max_dist, nq, name):
    seq = q.shape[0]
    qw = kw = heads * HEAD_DIM
    steps = _BandSteps(seq, dil, nq)

    def body(*refs):
        if sink is not None:
            sink_ref, refs = refs[0], refs[1:]
        q_ref, kp_ref, kc_ref, vp_ref, vc_ref, o_ref, lse_ref, s_scr, p_scr = refs
        i = pl.program_id(1)
        lane = lax.broadcasted_iota(jnp.int32, (BLOCK, LANES), 1)
        k_of = [_slabs_of(steps.keys(kp_ref, kc_ref, j)) for j in range(nq)]
        v_of = [_slabs_of(steps.keys(vp_ref, vc_ref, j)) for j in range(nq)]
        for j in range(nq):
            qv = q_ref[steps.rows_of(j), :]
            for h in range(heads):
                s_scr[j * heads + h] = lax.dot_general(_head_only(qv, h), k_of[j](h), _NT, preferred_element_type=F32)
        ls = {}
        for j in range(nq):
            valid = _band_mask(steps.has_prev(i, j), max_dist)
            lse_tile = jnp.zeros((BLOCK, LANES), F32)
            for h in range(heads):
                s = jnp.where(valid, s_scr[j * heads + h], NEG)
                m = jnp.max(s, axis=-1, keepdims=True)
                if sink is not None:
                    sk = sink_ref[h]
                    m = jnp.maximum(m, sk)
                p = jnp.exp(s - m)
                l = jnp.sum(p, axis=-1, keepdims=True)
                if sink is not None:
                    l = l + jnp.exp(sk - m)
                p_scr[j * heads + h] = p.astype(BF16)
                ls[j, h] = l
                lse_tile = jnp.where(lane == h, m + jnp.log(l), lse_tile)
            lse_ref[steps.rows_of(j), :] = lse_tile
        for j in range(nq):
            for pr in range(heads // 2):
                he, ho = 2 * pr, 2 * pr + 1
                even = jnp.dot(p_scr[j * heads + he], v_of[j](he), preferred_element_type=F32) / ls[j, he]
                odd = jnp.dot(p_scr[j * heads + ho], v_of[j](ho), preferred_element_type=F32) / ls[j, ho]
                o_ref[steps.rows_of(j), LANES * pr:LANES * (pr + 1)] = jnp.where(lane < HEAD_DIM, even, odd).astype(BF16)

    in_specs = [steps.own(qw), steps.prev(kw), steps.own(kw), steps.prev(kw), steps.own(kw)]
    args = [q, k, k, v, v]
    if sink is not None:
        in_specs = [pl.BlockSpec(memory_space=pltpu.SMEM)] + in_specs
        args = [sink] + args
    return pl.pallas_call(
        body, name=name, grid=(steps.outer, steps.inner), in_specs=in_specs,
        out_specs=[steps.own(qw), steps.own(LANES)],
        out_shape=[jax.ShapeDtypeStruct((seq, qw), BF16), jax.ShapeDtypeStruct((seq, LANES), F32)],
        scratch_shapes=[pltpu.VMEM((nq * heads, BLOCK, 2 * BLOCK), F32), pltpu.VMEM((nq * heads, BLOCK, 2 * BLOCK), BF16)],
        compiler_params=_params(dimension_semantics=("arbitrary", "arbitrary")),
    )(*args)


def _banded_bwd(q, k, v, d_out, stat, sink, *, dil, heads, max_dist, nq, name, reduce_scatter=()):
    seq = q.shape[0]
    qw = kw = heads * HEAD_DIM
    steps = _BandSteps(seq, dil, nq)
    n_rs = len(reduce_scatter)
    n_in = 7 + n_rs
    n_flat = steps.outer * (steps.inner + 1)

    def body(*refs):
        refs = list(refs)
        sink_ref = refs.pop(0) if sink is not None else None
        (q_ref, kp_ref, kc_ref, vp_ref, vc_ref, do_ref, st_ref), partials = refs[:7], refs[7:n_in]
        refs = refs[n_in:]
        dsink_ref = refs.pop(0) if sink is not None else None
        (dq_ref, dk_ref, dv_ref), sums = refs[:3], refs[3:3 + n_rs]
        kcar, vcar, s_scr, dp_scr, p_scr, ds_scr = refs[3 + n_rs:9 + n_rs]
        r, i = pl.program_id(0), pl.program_id(1)
        if n_rs:
            exchange = _ReduceScatter(tuple(partials), tuple(sums), refs[9 + n_rs:])
            flat = r * (steps.inner + 1) + i

            @pl.when(flat == 0)
            def _():
                exchange.start()

            @pl.when(flat == min(2, n_flat - 1))
            def _():
                exchange.send_chip_sums()

        @pl.when(i == 0)
        def _():
            kcar[...] = jnp.zeros_like(kcar)
            vcar[...] = jnp.zeros_like(vcar)

        if sink is not None:
            @pl.when((i == 0) & (r == 0))
            def _():
                dsink_ref[...] = jnp.zeros_like(dsink_ref)

        @pl.when(i < steps.inner)
        def _():
            lane = lax.broadcasted_iota(jnp.int32, (1, LANES), 1)
            lane_q = lax.broadcasted_iota(jnp.int32, (BLOCK, LANES), 1)
            k_of = [_slabs_of(steps.keys(kp_ref, kc_ref, j)) for j in range(nq)]
            v_of = [_slabs_of(steps.keys(vp_ref, vc_ref, j)) for j in range(nq)]
            qms, doms = {}, {}
            for j in range(nq):
                qv, dov = q_ref[steps.rows_of(j), :], do_ref[steps.rows_of(j), :]
                for h in range(heads):
                    qms[j, h], doms[j, h] = _head_only(qv, h), _head_only(dov, h)
                    s_scr[j * heads + h] = lax.dot_general(qms[j, h], k_of[j](h), _NT, preferred_element_type=F32)
                    dp_scr[j * heads + h] = lax.dot_general(doms[j, h], v_of[j](h), _NT, preferred_element_type=F32)
            dsink_row = jnp.zeros((1, LANES), F32)
            if sink is not None:
                sink_row = jnp.zeros((1, LANES), F32)
                for h in range(heads):
                    sink_row = jnp.where(lane == h, sink_ref[h], sink_row)
            for j in range(nq):
                st = st_ref[steps.rows_of(j), :]
                valid = _band_mask(steps.has_prev(i, j), max_dist)
                for h in range(heads):
                    lse_h = st[:, h:h + 1]
                    delta = st[:, DELTA_LANE + h:DELTA_LANE + h + 1]
                    p = jnp.where(valid, jnp.exp(s_scr[j * heads + h] - lse_h), 0.0)
                    p_scr[j * heads + h] = p.astype(BF16)
                    ds_scr[j * heads + h] = (p * (dp_scr[j * heads + h] - delta)).astype(BF16)
                if sink is not None:
                    term = -jnp.exp(sink_row - st) * pltpu.roll(st, LANES - DELTA_LANE, 1)
                    dsink_row = dsink_row + jnp.sum(jnp.where(lane_q < heads, term, 0.0), axis=0, keepdims=True)
            for j in range(nq):
                for pr in range(heads // 2):
                    he, ho = 2 * pr, 2 * pr + 1
                    even = jnp.dot(ds_scr[j * heads + he], k_of[j](he), preferred_element_type=F32)
                    odd = jnp.dot(ds_scr[j * heads + ho], k_of[j](ho), preferred_element_type=F32)
                    dq_ref[steps.rows_of(j), LANES * pr:LANES * (pr + 1)] = (
                        jnp.where(lane_q < HEAD_DIM, even, odd).astype(BF16))
            if steps.consecutive:
                dk_ref[...] = kcar[...].astype(BF16)
                dv_ref[...] = vcar[...].astype(BF16)
            for j in range(nq):
                for slab in range(kw // LANES):
                    he, ho = j * heads + 2 * slab, j * heads + 2 * slab + 1
                    dk_j = (lax.dot_general(ds_scr[he], qms[j, 2 * slab], _TN, preferred_element_type=F32)
                            + lax.dot_general(ds_scr[ho], qms[j, 2 * slab + 1], _TN, preferred_element_type=F32))
                    dv_j = (lax.dot_general(p_scr[he], doms[j, 2 * slab], _TN, preferred_element_type=F32)
                            + lax.dot_general(p_scr[ho], doms[j, 2 * slab + 1], _TN, preferred_element_type=F32))
                    sl = slice(LANES * slab, LANES * (slab + 1))
                    own_rows = steps.rows_of(j)
                    if not steps.consecutive:
                        dk_ref[own_rows, sl] = (kcar[own_rows, sl] + dk_j[0:BLOCK]).astype(BF16)
                        dv_ref[own_rows, sl] = (vcar[own_rows, sl] + dv_j[0:BLOCK]).astype(BF16)
                    elif j == 0:
                        last = steps.rows_of(nq - 1)
                        dk_ref[last, sl] = (kcar[last, sl] + dk_j[0:BLOCK]).astype(BF16)
                        dv_ref[last, sl] = (vcar[last, sl] + dv_j[0:BLOCK]).astype(BF16)
                    else:
                        before = steps.rows_of(j - 1)
                        kcar[before, sl] += dk_j[0:BLOCK]
                        vcar[before, sl] += dv_j[0:BLOCK]
                    kcar[own_rows, sl] = dk_j[BLOCK:2 * BLOCK]
                    vcar[own_rows, sl] = dv_j[BLOCK:2 * BLOCK]
            if sink is not None:
                dsink_ref[0:1, :] += dsink_row

        @pl.when(i == steps.inner)
        def _():
            dk_ref[...] = kcar[...].astype(BF16)
            dv_ref[...] = vcar[...].astype(BF16)

        if n_rs:
            @pl.when(flat == n_flat - 1)
            def _():
                exchange.finish()

    own, prev = (lambda w: steps.own(w, clamp=True)), (lambda w: steps.prev(w, clamp=True))
    rs_shapes = [t.shape[1:] for t in reduce_scatter]
    in_specs = ([own(qw), prev(kw), own(kw), prev(kw), own(kw), own(qw), own(LANES)]
                + [pl.BlockSpec(memory_space=pl.ANY)] * n_rs)
    args = [q, k, k, v, v, d_out, stat, *reduce_scatter]
    out_specs = [own(qw), steps.late(kw), steps.late(kw)] + [_full(s) for s in rs_shapes]
    out_shape = [jax.ShapeDtypeStruct((seq, qw), BF16), jax.ShapeDtypeStruct((seq, kw), BF16),
                 jax.ShapeDtypeStruct((seq, kw), BF16)] + [jax.ShapeDtypeStruct(s, F32) for s in rs_shapes]
    if sink is not None:
        in_specs = [pl.BlockSpec(memory_space=pltpu.SMEM)] + in_specs
        args = [sink] + args
        out_specs = [_full((8, LANES))] + out_specs
        out_shape = [jax.ShapeDtypeStruct((8, LANES), F32)] + out_shape
    n_hb = nq * heads
    res = pl.pallas_call(
        body, name=name, grid=(steps.outer, steps.inner + 1), in_specs=in_specs, out_specs=out_specs,
        out_shape=out_shape,
        scratch_shapes=[pltpu.VMEM((steps.rows, kw), F32), pltpu.VMEM((steps.rows, kw), F32)]
        + [pltpu.VMEM((n_hb, BLOCK, 2 * BLOCK), F32)] * 2 + [pltpu.VMEM((n_hb, BLOCK, 2 * BLOCK), BF16)] * 2
        + (_ReduceScatter.scratch_shapes(rs_shapes) if n_rs else []),
        compiler_params=_params(dimension_semantics=("arbitrary", "arbitrary")),
    )(*args)
    if sink is not None:
        return (*res[1:4], res[0], *res[4:])
    return res


def _cross_fwd(q, mk, mv, tq=1024):
    seq = q.shape[0]

    def body(q_ref, mk_ref, mv_ref, o_ref, lse_ref, s_scr, p_scr):
        qv = q_ref[...]
        k_of, v_of = _slabs_of(mk_ref[...]), _slabs_of(mv_ref[...])
        lane = lax.broadcasted_iota(jnp.int32, (tq, LANES), 1)
        lse_tile = jnp.zeros((tq, LANES), F32)
        for h in range(C_HEADS):
            s_scr[h] = lax.dot_general(_head_only(qv, h), k_of(h), _NT, preferred_element_type=F32)
        ls = []
        for h in range(C_HEADS):
            s = s_scr[h]
            m = jnp.max(s, axis=-1, keepdims=True)
            p = jnp.exp(s - m)
            l = jnp.sum(p, axis=-1, keepdims=True)
            p_scr[h] = p.astype(BF16)
            ls.append(l)
            lse_tile = jnp.where(lane == h, m + jnp.log(l), lse_tile)
        for pr in range(C_HEADS // 2):
            even = jnp.dot(p_scr[2 * pr], v_of(2 * pr), preferred_element_type=F32) / ls[2 * pr]
            odd = jnp.dot(p_scr[2 * pr + 1], v_of(2 * pr + 1), preferred_element_type=F32) / ls[2 * pr + 1]
            o_ref[:, LANES * pr:LANES * (pr + 1)] = jnp.where(lane < HEAD_DIM, even, odd).astype(BF16)
        lse_ref[...] = lse_tile

    return pl.pallas_call(
        body, name="cross_fwd", grid=(seq // tq,),
        in_specs=[_row(tq, C_W), _full((N_MEM, C_W)), _full((N_MEM, C_W))],
        out_specs=[_row(tq, C_W), _row(tq, LANES)],
        out_shape=[jax.ShapeDtypeStruct((seq, C_W), BF16), jax.ShapeDtypeStruct((seq, LANES), F32)],
        scratch_shapes=[pltpu.VMEM((C_HEADS, tq, N_MEM), F32), pltpu.VMEM((C_HEADS, tq, N_MEM), BF16)],
        compiler_params=_params(dimension_semantics=("arbitrary",)),
    )(q, mk, mv)


def _cross_bwd(q, mk, mv, d_out, stat, tq=1024):
    seq = q.shape[0]

    def body(q_ref, mk_ref, mv_ref, do_ref, st_ref, dq_ref, dmk_ref, dmv_ref, s_scr, dp_scr, p_scr, ds_scr):
        @pl.when(pl.program_id(0) == 0)
        def _():
            dmk_ref[...] = jnp.zeros_like(dmk_ref)
            dmv_ref[...] = jnp.zeros_like(dmv_ref)

        qv, dov, st = q_ref[...], do_ref[...], st_ref[...]
        k_of, v_of = _slabs_of(mk_ref[...]), _slabs_of(mv_ref[...])
        qms = [_head_only(qv, h) for h in range(C_HEADS)]
        doms = [_head_only(dov, h) for h in range(C_HEADS)]
        for h in range(C_HEADS):
            s_scr[h] = lax.dot_general(qms[h], k_of(h), _NT, preferred_element_type=F32)
            dp_scr[h] = lax.dot_general(doms[h], v_of(h), _NT, preferred_element_type=F32)
        for h in range(C_HEADS):
            p = jnp.exp(s_scr[h] - st[:, h:h + 1])
            p_scr[h] = p.astype(BF16)
            ds_scr[h] = (p * (dp_scr[h] - st[:, DELTA_LANE + h:DELTA_LANE + h + 1])).astype(BF16)
        lane = lax.broadcasted_iota(jnp.int32, (tq, LANES), 1)
        for pr in range(C_HEADS // 2):
            sl = slice(LANES * pr, LANES * (pr + 1))
            even = jnp.dot(ds_scr[2 * pr], k_of(2 * pr), preferred_element_type=F32)
            odd = jnp.dot(ds_scr[2 * pr + 1], k_of(2 * pr + 1), preferred_element_type=F32)
            dq_ref[:, sl] = jnp.where(lane < HEAD_DIM, even, odd).astype(BF16)
            dmk_ref[:, sl] += (lax.dot_general(ds_scr[2 * pr], qms[2 * pr], _TN, preferred_element_type=F32)
                               + lax.dot_general(ds_scr[2 * pr + 1], qms[2 * pr + 1], _TN, preferred_element_type=F32))
            dmv_ref[:, sl] += (lax.dot_general(p_scr[2 * pr], doms[2 * pr], _TN, preferred_element_type=F32)
                               + lax.dot_general(p_scr[2 * pr + 1], doms[2 * pr + 1], _TN, preferred_element_type=F32))

    return pl.pallas_call(
        body, name="cross_bwd", grid=(seq // tq,),
        in_specs=[_row(tq, C_W), _full((N_MEM, C_W)), _full((N_MEM, C_W)), _row(tq, C_W), _row(tq, LANES)],
        out_specs=[_row(tq, C_W), _full((N_MEM, C_W)), _full((N_MEM, C_W))],
        out_shape=[jax.ShapeDtypeStruct((seq, C_W), BF16), jax.ShapeDtypeStruct((N_MEM, C_W), F32),
                   jax.ShapeDtypeStruct((N_MEM, C_W), F32)],
        scratch_shapes=[pltpu.VMEM((C_HEADS, tq, N_MEM), F32)] * 2 + [pltpu.VMEM((C_HEADS, tq, N_MEM), BF16)] * 2,
        compiler_params=_params(dimension_semantics=("arbitrary",)),
    )(q, mk, mv, d_out, stat)


def _per_head(tile, width):
    rows = tile.shape[0]
    lane = lax.broadcasted_iota(jnp.int32, (rows, LANES), 1)
    slabs = []
    for p in range(width // LANES):
        even = jnp.broadcast_to(tile[:, 2 * p:2 * p + 1], (rows, LANES))
        odd = jnp.broadcast_to(tile[:, 2 * p + 1:2 * p + 2], (rows, LANES))
        slabs.append(jnp.where(lane < HEAD_DIM, even, odd))
    return slabs[0] if len(slabs) == 1 else jnp.concatenate(slabs, axis=1)


def _with_delta(lse_tile, prod):
    rows = lse_tile.shape[0]
    lane = lax.broadcasted_iota(jnp.int32, (rows, LANES), 1)
    tile = lse_tile
    for p in range(prod.shape[1] // LANES):
        slab = prod[:, LANES * p:LANES * (p + 1)]
        even = jnp.sum(jnp.where(lane < HEAD_DIM, slab, 0.0), axis=-1, keepdims=True)
        odd = jnp.sum(jnp.where(lane >= HEAD_DIM, slab, 0.0), axis=-1, keepdims=True)
        tile = jnp.where(lane == DELTA_LANE + 2 * p, even, tile)
        tile = jnp.where(lane == DELTA_LANE + 2 * p + 1, odd, tile)
    return tile


def _mid(oa, lse_a, ob, lse_b, oc, lse_c, gate, x, target, w_out_full, post_g, tm=512):
    seq = x.shape[0]
    n_b = B_W // LANES

    def body(oa_ref, la_ref, b1_ref, l1_ref, b4_ref, l4_ref, b16_ref, l16_ref, oc_ref, lc_ref,
             gate_ref, x_ref, t_ref, w_ref, pg_ref,
             dh_ref, dg_ref, doa_ref, sa_ref, dob1_ref, sb1_ref, dob4_ref, sb4_ref, dob16_ref, sb16_ref,
             doc_ref, sc_ref, dw_ref, st_ref, scr_b4, scr_b16, scr_l4, scr_l16, scr_do, scr_sb):
        @pl.when(pl.program_id(0) == 0)
        def _():
            dw_ref[...] = jnp.zeros_like(dw_ref)
            st_ref[...] = jnp.zeros_like(st_ref)

        b1, l1 = b1_ref[...].astype(F32), l1_ref[...]
        b4, l4 = _load_permuted(b4_ref, scr_b4, 4), _load_permuted(l4_ref, scr_l4, 4)
        b16, l16 = _load_permuted(b16_ref, scr_b16, 16), _load_permuted(l16_ref, scr_l16, 16)
        lm = jnp.maximum(jnp.maximum(l1, l4), l16)
        e1, e4, e16 = jnp.exp(l1 - lm), jnp.exp(l4 - lm), jnp.exp(l16 - lm)
        den = e1 + e4 + e16
        lse_b_tile = lm + jnp.log(den)
        ob_v = _per_head(e1 / den, B_W) * b1 + _per_head(e4 / den, B_W) * b4 + _per_head(e16 / den, B_W) * b16
        o_all = jnp.concatenate([oa_ref[...].astype(F32), ob_v, oc_ref[...].astype(F32)], axis=1)
        g = gate_ref[...].astype(F32)
        sig = 1.0 / (1.0 + jnp.exp(-g))
        silu = g * sig
        y = (o_all * silu).astype(BF16)
        w = w_ref[...]
        z = jnp.dot(y, w, preferred_element_type=F32)
        rz = lax.rsqrt(jnp.mean(z * z, axis=-1, keepdims=True) + RMS_EPS)
        hn = z * rz
        pg = pg_ref[...]
        err = (x_ref[...] + hn * pg) - t_ref[...]
        loss = 0.5 * jnp.sum(jnp.mean(err * err, axis=-1, keepdims=True), axis=0, keepdims=True)
        dh = err * (1.0 / D_MODEL)
        dh_ref[...] = dh.astype(BF16)
        st_ref[0:1, :] += jnp.sum(dh * hn, axis=0, keepdims=True)
        st_ref[1:2, :] += jnp.broadcast_to(loss, (1, D_MODEL))
        dhn = dh * pg
        dz = (rz * (dhn - hn * jnp.mean(dhn * hn, axis=-1, keepdims=True))).astype(BF16)
        dy = lax.dot_general(dz, w, _NT, preferred_element_type=F32)
        dw_ref[...] += lax.dot_general(y, dz, _TN, preferred_element_type=F32)
        dg_ref[...] = (dy * o_all * (sig * (1.0 + g * (1.0 - sig)))).astype(BF16)
        d_o = (dy * silu).astype(BF16)
        prod = d_o.astype(F32) * o_all
        doa_ref[...] = d_o[:, 0:A_W]
        sa_ref[...] = _with_delta(la_ref[...], prod[:, 0:A_W])
        doc_ref[...] = d_o[:, A_W + B_W:D_MODEL]
        sc_ref[...] = _with_delta(lc_ref[...], prod[:, A_W + B_W:D_MODEL])
        d_ob = d_o[:, A_W:A_W + B_W]
        stat_b = _with_delta(lse_b_tile, prod[:, A_W:A_W + B_W])
        dob1_ref[...] = d_ob
        sb1_ref[...] = stat_b
        _put(scr_do, d_ob.astype(F32))
        _put(scr_sb, stat_b)
        _store_permuted(scr_do, dob4_ref, 4, BF16)
        _store_permuted(scr_sb, sb4_ref, 4, F32)
        _store_permuted(scr_do, dob16_ref, 16, BF16)
        _store_permuted(scr_sb, sb16_ref, 16, F32)

    p4 = lambda w: _perm_spec(tm, 4, w)
    p16 = lambda w: _perm_spec(tm, 16, w)
    in_specs = [_row(tm, A_W), _row(tm, LANES), _row(tm, B_W), _row(tm, LANES), p4(B_W), p4(LANES), p16(B_W), p16(LANES),
                _row(tm, C_W), _row(tm, LANES), _row(tm, D_MODEL), _row(tm, D_MODEL), _row(tm, D_MODEL),
                _full((D_MODEL, D_MODEL)), _full((1, D_MODEL))]
    sds = jax.ShapeDtypeStruct
    v4 = lambda w, dt: sds((seq // (BLOCK * 4), 4, BLOCK, w), dt)
    v16 = lambda w, dt: sds((seq // (BLOCK * 16), 16, BLOCK, w), dt)
    out_specs = [_row(tm, D_MODEL), _row(tm, D_MODEL), _row(tm, A_W), _row(tm, LANES), _row(tm, B_W), _row(tm, LANES),
                 p4(B_W), p4(LANES), p16(B_W), p16(LANES), _row(tm, C_W), _row(tm, LANES),
                 _full((D_MODEL, D_MODEL)), _full((8, D_MODEL))]
    out_shape = [sds((seq, D_MODEL), BF16), sds((seq, D_MODEL), BF16), sds((seq, A_W), BF16), sds((seq, LANES), F32),
                 sds((seq, B_W), BF16), sds((seq, LANES), F32), v4(B_W, BF16), v4(LANES, F32), v16(B_W, BF16),
                 v16(LANES, F32), sds((seq, C_W), BF16), sds((seq, LANES), F32),
                 sds((D_MODEL, D_MODEL), F32), sds((8, D_MODEL), F32)]
    res = pl.pallas_call(
        body, name="mid", grid=(seq // tm,), in_specs=in_specs, out_specs=out_specs, out_shape=out_shape,
        scratch_shapes=[pltpu.VMEM((n_b, tm, LANES), F32), pltpu.VMEM((n_b, tm, LANES), F32),
                        pltpu.VMEM((1, tm, LANES), F32), pltpu.VMEM((1, tm, LANES), F32),
                        pltpu.VMEM((n_b, tm, LANES), F32), pltpu.VMEM((1, tm, LANES), F32)],
        compiler_params=_params(dimension_semantics=("arbitrary",)),
    )(oa, lse_a, ob[1], lse_b[1], _perm_view(ob[4], 4), _perm_view(lse_b[4], 4), _perm_view(ob[16], 16),
      _perm_view(lse_b[16], 16), oc, lse_c, gate, x, target, w_out_full, post_g)
    dh, d_gate, do_a, st_a, do_b1, st_b1, do_b4, st_b4, do_b16, st_b16, do_c, st_c, d_wout, stats = res
    flat = lambda t: t.reshape(seq, t.shape[-1])
    d_b = {1: (do_b1, st_b1), 4: (flat(do_b4), flat(st_b4)), 16: (flat(do_b16), flat(st_b16))}
    return dh, d_gate, (do_a, st_a), d_b, (do_c, st_c), d_wout, stats


def _inproj_bwd(x, u, dh, pre_g, w_in_full, tabs, dqa, dka, dva, dqkv_b, dqc, dgate, tm=512):
    seq = x.shape[0]
    n_b = B_W // LANES

    def body(x_ref, u_ref, dh_ref, g_ref, w_hbm, c_ref, up_ref, dn_ref, dqa_ref, dka_ref, dva_ref,
             dq1, dk1, dv1, dq4, dk4, dv4, dq16, dk16, dv16, dqc_ref, dg_ref,
             gx_ref, dw_ref, st_ref, scr4, scr16, w_scr, w_sems, dp_ref):
        _stage_w_in(w_hbm, w_scr, w_sems)

        @pl.when(pl.program_id(0) == 0)
        def _():
            st_ref[...] = jnp.zeros_like(st_ref)
            dw_ref[...] = jnp.zeros_like(dw_ref)

        c, up, dn = c_ref[...], -up_ref[...], -dn_ref[...]
        unrot = lambda t: _rotate(t, c, up, dn)
        total = lambda r1, r4, r16: (r1[...].astype(F32) + _load_permuted(r4, scr4, 4)
                                     + _load_permuted(r16, scr16, 16))
        at = lambda piece: slice(*COLS[piece])
        dp_ref[:, at("qa")] = (unrot(dqa_ref[...].astype(F32)) * SCALE).astype(BF16)
        dp_ref[:, at("ka")] = unrot(_per_kv_head(dka_ref[...].astype(F32))).astype(BF16)
        dp_ref[:, at("va")] = _per_kv_head(dva_ref[...].astype(F32)).astype(BF16)
        dp_ref[:, at("ga")] = dg_ref[:, 0:A_W]
        dp_ref[:, at("qb")] = (unrot(total(dq1, dq4, dq16)) * SCALE).astype(BF16)
        dp_ref[:, at("kb")] = unrot(total(dk1, dk4, dk16)).astype(BF16)
        dp_ref[:, at("vb")] = total(dv1, dv4, dv16).astype(BF16)
        dp_ref[:, at("gb")] = dg_ref[:, A_W:A_W + B_W]
        dp_ref[:, at("qc")] = (dqc_ref[...].astype(F32) * SCALE).astype(BF16)
        dp_ref[:, at("gc")] = dg_ref[:, A_W + B_W:D_MODEL]
        du = lax.dot_general(dp_ref[...], w_scr[...], _NT, preferred_element_type=F32)
        res = lax.dot_general(u_ref[...], dp_ref[...], _TN, preferred_element_type=F32)
        for k in range(N_DEV):
            dw_ref[k] += res[:, SHARD_IN * k:SHARD_IN * (k + 1)]
        xv = x_ref[...]
        r = lax.rsqrt(jnp.mean(xv * xv, axis=-1, keepdims=True) + RMS_EPS)
        xh = xv * r
        st_ref[0:1, :] += jnp.sum(du * xh, axis=0, keepdims=True)
        dxh = du * g_ref[...]
        gx_ref[...] = dh_ref[...].astype(F32) + r * (dxh - xh * jnp.mean(dxh * xh, axis=-1, keepdims=True))

    in_specs = ([_row(tm, D_MODEL), _row(tm, D_MODEL), _row(tm, D_MODEL), _full((1, D_MODEL)),
                 pl.BlockSpec(memory_space=pl.ANY),
                 _row(tm, LANES), _row(tm, LANES), _row(tm, LANES), _row(tm, A_W), _row(tm, A_W), _row(tm, A_W)]
                + [_row(tm, B_W)] * 3 + [_perm_spec(tm, 4, B_W)] * 3 + [_perm_spec(tm, 16, B_W)] * 3
                + [_row(tm, C_W), _row(tm, D_MODEL)])
    dw_spec = pl.BlockSpec((N_DEV, D_MODEL, SHARD_IN), lambda i: (0, 0, 0), pipeline_mode=pl.Buffered(1))
    return pl.pallas_call(
        body, name="inproj_bwd", grid=(seq // tm,), in_specs=in_specs,
        out_specs=[_row(tm, D_MODEL), dw_spec, _full((8, D_MODEL))],
        out_shape=[jax.ShapeDtypeStruct((seq, D_MODEL), F32), jax.ShapeDtypeStruct((N_DEV, D_MODEL, SHARD_IN), F32),
                   jax.ShapeDtypeStruct((8, D_MODEL), F32)],
        scratch_shapes=[pltpu.VMEM((n_b, tm, LANES), F32), pltpu.VMEM((n_b, tm, LANES), F32)] + _w_in_scratch()
        + [pltpu.VMEM((tm, D_IN), BF16)],
        compiler_params=_params(dimension_semantics=("arbitrary",)),
    )(x, u, dh, pre_g, w_in_full, *tabs, dqa, dka, dva, *dqkv_b[1], *[_perm_view(t, 4) for t in dqkv_b[4]],
      *[_perm_view(t, 16) for t in dqkv_b[16]], dqc, dgate)


class _ReduceScatter:
    def __init__(self, ins, outs, scratch):
        self.n = n = len(ins)
        self.ins, self.outs = ins, outs
        self.mine, self.got, self.snd, self.rcv = (scratch[n * t:n * (t + 1)] for t in range(4))
        self.load_sems, self.d2d_send, self.d2d_recv, self.ici_send, self.ici_recv = scratch[4 * n:]
        self.pos = _mesh_pos()
        self.pairs = [(a, kk) for kk in (3, 1, 2) for a in range(n)]

    @staticmethod
    def scratch_shapes(shapes):
        return ([pltpu.VMEM((4,) + s, F32) for s in shapes] + [pltpu.VMEM((4,) + s, F32) for s in shapes]
                + [pltpu.VMEM((3,) + s, BF16) for s in shapes] + [pltpu.VMEM((3,) + s, BF16) for s in shapes]
                + [pltpu.SemaphoreType.DMA((len(shapes), 4))] * 5)

    def _chip(self, kk):
        x, y, _ = self.pos
        return (1 - x if kk & 2 else x, 1 - y if kk & 1 else y)

    def _load(self, a, kk):
        block = _dev_index((*self._chip(kk), self.pos[2]))
        return pltpu.make_async_copy(self.ins[a].at[block], self.mine[a].at[kk], self.load_sems.at[a, kk])

    def _swap(self, a, kk):
        x, y, c = self.pos
        return pltpu.make_async_remote_copy(
            src_ref=self.ins[a].at[_dev_index((*self._chip(kk), 1 - c))], dst_ref=self.got[a].at[kk],
            send_sem=self.d2d_send.at[a, kk], recv_sem=self.d2d_recv.at[a, kk],
            device_id=(x, y, 1 - c), device_id_type=MESH_ID)

    def _hop(self, a, kk):
        return pltpu.make_async_remote_copy(
            src_ref=self.snd[a].at[kk - 1], dst_ref=self.rcv[a].at[kk - 1], send_sem=self.ici_send.at[a, kk],
            recv_sem=self.ici_recv.at[a, kk], device_id=(*self._chip(kk), self.pos[2]), device_id_type=MESH_ID)

    def start(self):
        for kk in (3, 1, 2, 0):
            for a in range(self.n):
                self._load(a, kk).start()
                self._swap(a, kk).start()

    def send_chip_sums(self):
        for a, kk in self.pairs:
            self._load(a, kk).wait()
            self._swap(a, kk).wait_recv()
            self.snd[a][kk - 1] = (self.mine[a][kk] + self.got[a][kk]).astype(BF16)
            self._hop(a, kk).start()

    def finish(self):
        for a in range(self.n):
            self._load(a, 0).wait()
            self._swap(a, 0).wait_recv()
            acc = self.mine[a][0] + self.got[a][0]
            for kk in (1, 2, 3):
                self._hop(a, kk).wait_recv()
                acc = acc + self.rcv[a][kk - 1].astype(F32)
            self.outs[a][...] = acc
        for kk in range(4):
            for a in range(self.n):
                self._swap(a, kk).wait_send()
        for a, kk in self.pairs:
            self._hop(a, kk).wait_send()


def _local_step(x, mem, pre_g, w_in, sink, mem_g, w_mem, w_out, post_g, target):
    u, *tabs, w_in_full = _prep(x, pre_g, w_in)
    qa, ka, va, qkv_b, qc, gate, w_mem_all, w_out_all = _inproj(u, w_in_full, tabs, w_mem, w_out)
    w_mem_full = w_mem_all.reshape(D_MODEL, 2 * C_W)
    w_out_full = w_out_all.reshape(D_MODEL, D_MODEL)
    mn, mk, mv = _memkv_fwd(mem, mem_g, w_mem_full)

    a_cfg = dict(dil=1, heads=A_HEADS, max_dist=BLOCK - 1, nq=ATTN_BLOCKS_PER_STEP)
    b_cfgs = {dil: dict(dil=dil, heads=B_HEADS, max_dist=win // dil, nq=ATTN_BLOCKS_PER_STEP)
              for win, dil in B_CONFIGS}
    oa, lse_a = _banded_fwd(qa, ka, va, sink, name="attn_a_fwd", **a_cfg)
    ob, lse_b = {}, {}
    for dil, cfg in b_cfgs.items():
        ob[dil], lse_b[dil] = _banded_fwd(*qkv_b[dil], None, name=f"attn_b{dil}_fwd", **cfg)
    oc, lse_c = _cross_fwd(qc, mk, mv)

    dh, d_gate, d_a, d_b, d_c, d_wout, st_mid = _mid(oa, lse_a, ob, lse_b, oc, lse_c, gate, x, target, w_out_full, post_g)

    dqc, dmk, dmv = _cross_bwd(qc, mk, mv, *d_c)
    d_wmem, st_mem = _memkv_bwd(mem, mem_g, mn, w_mem_full, dmk, dmv)
    dqkv_b = {dil: _banded_bwd(*qkv_b[dil], *d_b[dil], None, name=f"attn_b{dil}_bwd",
                               **{**cfg, "nq": cfg["nq"] if dil == 1 else min(dil, ATTN_BWD_CLASSES_PER_STEP)})
              for dil, cfg in b_cfgs.items()}
    dqa, dka, dva, dsink, g_wmem, g_wout = _banded_bwd(
        qa, ka, va, *d_a, sink, name="attn_a_bwd", **a_cfg,
        reduce_scatter=(d_wmem.reshape(N_DEV, SHARD_ROWS, 2 * C_W), d_wout.reshape(N_DEV, SHARD_ROWS, D_MODEL)))

    grad_x, d_win, st_pre = _inproj_bwd(x, u, dh, pre_g, w_in_full, tabs, dqa, dka, dva, dqkv_b, dqc, d_gate)

    dsink_row = jnp.pad(dsink[0:1, :], ((0, 0), (0, D_MODEL - LANES)))
    stats = jnp.concatenate([st_pre[0:1], st_mem[0:1], st_mid[0:1], dsink_row, st_mid[1:2],
                             jnp.zeros((3, D_MODEL), F32)], axis=0)
    return grad_x, d_win, g_wmem, g_wout, stats


def _prep(x, pre_g, w_in, tm=1024):
    seq = x.shape[0]
    n_steps = seq // tm
    parts = 2
    rows = D_MODEL // parts
    relay_at = min(4, n_steps - 1)
    j = jnp.arange(LANES) % HEAD_DIM
    freq = (ROPE_THETA ** (-(2 * (j % (ROT_DIM // 2))).astype(F32) / ROT_DIM))[None, :]
    SIB, NB_X, NB_Y, RELAY, FWD = 0, 1, 2, 3, 4

    def body(x_ref, g_ref, f_ref, win_ref, u_ref, c_ref, up_ref, dn_ref, win_out, win_b,
             send_sems, recv_sems, local_sems):
        step = pl.program_id(0)
        px, py, pc = _mesh_pos()
        me, sibling = (px, py, pc), (px, py, 1 - pc)
        others = lambda core: ((1 - px, py, core), (px, 1 - py, core), (1 - px, 1 - py, core))
        x_nb, y_nb, diag = others(pc)
        relay_from = [x_nb, y_nb]
        relay_to = [y_nb, x_nb]

        def src(a):
            return win_b.at[pl.ds(rows * a, rows)]

        def slot(a, p):
            return win_out.at[_dev_index(p), pl.ds(rows * a, rows)]

        def copy(a, k, block, to, own=False):
            return pltpu.make_async_remote_copy(
                src_ref=src(a) if own else slot(a, block), dst_ref=slot(a, block),
                send_sem=send_sems.at[a, k], recv_sem=recv_sems.at[a, k], device_id=to, device_id_type=MESH_ID)

        def first_sends():
            return [copy(0, NB_X, me, x_nb, own=True), copy(1, NB_Y, me, y_nb, own=True),
                    copy(1, NB_X, me, x_nb, own=True), copy(0, NB_Y, me, y_nb, own=True),
                    copy(0, SIB, me, sibling, own=True), copy(1, SIB, me, sibling, own=True)]

        def relay(a):
            return copy(a, RELAY, relay_from[a], relay_to[a])

        def to_sibling(a, which):
            return copy(a, FWD + which, others(pc)[which], sibling)

        def local(a):
            return pltpu.make_async_copy(src(a), slot(a, me), local_sems.at[a])

        @pl.when(step == 0)
        def _():
            win_b[...] = win_ref[...].astype(BF16)
            for a in range(parts):
                local(a).start()
            for cp in first_sends():
                cp.start()

        @pl.when(step == relay_at)
        def _():
            for a in range(parts):
                copy(a, NB_X + a, relay_from[a], me).wait_recv()
                relay(a).start()
                to_sibling(a, a).start()

        xv = x_ref[...]
        r = lax.rsqrt(jnp.mean(xv * xv, axis=-1, keepdims=True) + RMS_EPS)
        u_ref[...] = ((xv * r) * g_ref[...]).astype(BF16)
        pos = (lax.broadcasted_iota(jnp.int32, (tm, LANES), 0) + step * tm).astype(F32)
        head_lane = lax.broadcasted_iota(jnp.int32, (tm, LANES), 1) % HEAD_DIM
        ang = pos * f_ref[...]
        cos, sin = jnp.cos(ang), jnp.sin(ang)
        half = ROT_DIM // 2
        c_ref[...] = jnp.where(head_lane < ROT_DIM, cos, 1.0)
        up_ref[...] = jnp.where((head_lane >= half) & (head_lane < ROT_DIM), sin, 0.0)
        dn_ref[...] = jnp.where(head_lane < half, -sin, 0.0)

        @pl.when(step == n_steps - 1)
        def _():
            copy(1, NB_X, x_nb, me).wait_recv()
            to_sibling(1, 0).start()
            copy(0, NB_Y, y_nb, me).wait_recv()
            to_sibling(0, 1).start()
            for a in range(parts):
                copy(a, RELAY, diag, me).wait_recv()
                to_sibling(a, 2).start()
            for a in range(parts):
                copy(a, SIB, sibling, me).wait_recv()
                for which in range(3):
                    copy(a, FWD + which, others(1 - pc)[which], me).wait_recv()
            for cp in first_sends():
                cp.wait_send()
            for a in range(parts):
                relay(a).wait_send()
                for which in range(3):
                    to_sibling(a, which).wait_send()
                local(a).wait()

    return pl.pallas_call(
        body, name="prep", grid=(n_steps,),
        in_specs=[_row(tm, D_MODEL), _full((1, D_MODEL)), _full((1, LANES)), _full(w_in.shape)],
        out_specs=[_row(tm, D_MODEL), _row(tm, LANES), _row(tm, LANES), _row(tm, LANES),
                   pl.BlockSpec(memory_space=pl.ANY)],
        out_shape=[jax.ShapeDtypeStruct((seq, D_MODEL), BF16)] + [jax.ShapeDtypeStruct((seq, LANES), F32)] * 3
        + [jax.ShapeDtypeStruct((N_DEV,) + w_in.shape, BF16)],
        scratch_shapes=[pltpu.VMEM(w_in.shape, BF16), pltpu.SemaphoreType.DMA((parts, FWD + 3)),
                        pltpu.SemaphoreType.DMA((parts, FWD + 3)), pltpu.SemaphoreType.DMA((parts,))],
        compiler_params=_params(dimension_semantics=("arbitrary",)),
    )(x, pre_g, freq, w_in)


def _exchange_grads(d_win, stats):
    def body(win, st, g_win, r_st, send_sems, recv_sems, local_sem, *scratch):
        exchange = _ReduceScatter((win,), (g_win,), scratch)
        exchange.start()
        pos = _mesh_pos()
        me = _dev_index(pos)
        own = pltpu.make_async_copy(st, r_st.at[me], local_sem)
        own.start()
        copies = []
        for s in range(1, N_DEV):
            peer = _xor_peer(pos, s)
            mk = lambda slot: pltpu.make_async_remote_copy(
                src_ref=st, dst_ref=r_st.at[slot], send_sem=send_sems.at[s], recv_sem=recv_sems.at[s],
                device_id=peer, device_id_type=MESH_ID)
            send, arrival = mk(me), mk(_dev_index(peer))
            send.start()
            copies.append((send, arrival))
        exchange.send_chip_sums()
        exchange.finish()
        for send, arrival in copies:
            arrival.wait_recv()
            send.wait_send()
        own.wait()

    hbm = pl.BlockSpec(memory_space=pl.ANY)
    shard = d_win.shape[1:]
    return pl.pallas_call(
        body, name="exchange_grads", in_specs=[hbm, hbm],
        out_specs=[pl.BlockSpec(memory_space=pltpu.VMEM), hbm],
        out_shape=[jax.ShapeDtypeStruct(shard, F32), jax.ShapeDtypeStruct((N_DEV,) + stats.shape, F32)],
        scratch_shapes=[pltpu.SemaphoreType.DMA((N_DEV,)), pltpu.SemaphoreType.DMA((N_DEV,)), pltpu.SemaphoreType.DMA(())]
        + _ReduceScatter.scratch_shapes([shard]),
        compiler_params=_params(),
    )(d_win, stats)


WEIGHT_ORDER = ("pre_norm", "w_in", "sink_a", "mem_norm", "w_mem_kv", "w_out", "post_norm")


def _adamw_all(grads, r_stats, weights, moments_m, moments_v):
    n = len(WEIGHT_ORDER)
    stat_row = {"pre_norm": 0, "mem_norm": 1, "post_norm": 2, "sink_a": 3}

    def body(*refs):
        gw_in, gw_mem, gw_out, st_ref = refs[0:4]
        w_refs, m_refs, v_refs = (dict(zip(WEIGHT_ORDER, refs[4 + n * t:4 + n * (t + 1)])) for t in range(3))
        loss_ref = refs[4 + 3 * n]
        outs = refs[5 + 3 * n:]
        g_small = st_ref[0]
        for s in range(1, N_DEV):
            g_small = g_small + st_ref[s]
        loss_ref[...] = g_small[4:5, 0:1]
        big = {"w_in": gw_in, "w_mem_kv": gw_mem, "w_out": gw_out}
        for i, name in enumerate(WEIGHT_ORDER):
            if name in big:
                g = big[name][...]
                at = lambda ref: ref[0]
            else:
                width = w_refs[name].shape[-1]
                g = g_small[stat_row[name]:stat_row[name] + 1, 0:width]
                at = lambda ref: ref[...]
            m2 = ADAM_B1 * at(m_refs[name]) + (1.0 - ADAM_B1) * g
            v2 = ADAM_B2 * at(v_refs[name]) + (1.0 - ADAM_B2) * (g * g)
            m_hat = m2 / (1.0 - ADAM_B1 ** ADAM_STEP)
            v_hat = v2 / (1.0 - ADAM_B2 ** ADAM_STEP)
            delta = -ADAM_LR * (m_hat / (jnp.sqrt(v_hat) + ADAM_EPS) + ADAM_WD * at(w_refs[name]))
            for kind, val in enumerate((g, delta, m2, v2)):
                out = outs[kind * n + i]
                if name in big:
                    out[0] = val
                else:
                    out[...] = val

    shapes = [weights[name].shape for name in WEIGHT_ORDER]
    res = pl.pallas_call(
        body, name="adamw_all",
        out_shape=[jax.ShapeDtypeStruct((1, 1), F32)] + [jax.ShapeDtypeStruct(sh, F32) for sh in shapes] * 4,
        compiler_params=_params(),
    )(grads["w_in"], grads["w_mem_kv"], grads["w_out"], r_stats,
      *[weights[k] for k in WEIGHT_ORDER], *[moments_m[k] for k in WEIGHT_ORDER], *[moments_v[k] for k in WEIGHT_ORDER])
    return res[0].reshape(()), res[1:]


def kernel(x, mem, pre_norm, w_in, sink_a, mem_norm, w_mem_kv, w_out, post_norm, loss_target, m_pre_norm, m_w_in, m_sink_a, m_mem_norm, m_w_mem_kv, m_w_out, m_post_norm, v_pre_norm, v_w_in, v_sink_a, v_mem_norm, v_w_mem_kv, v_w_out, v_post_norm):
    sink = jnp.pad(sink_a[0], (0, 8 - A_HEADS))
    grad_x, d_win, g_wmem, g_wout, stats = _local_step(
        x[0], mem[0], pre_norm, w_in[0], sink, mem_norm, w_mem_kv[0], w_out[0], post_norm, loss_target[0])
    g_win, r_stats = _exchange_grads(d_win, stats)
    weights = dict(pre_norm=pre_norm, w_in=w_in, sink_a=sink_a, mem_norm=mem_norm, w_mem_kv=w_mem_kv, w_out=w_out,
                   post_norm=post_norm)
    moments_m = dict(pre_norm=m_pre_norm, w_in=m_w_in, sink_a=m_sink_a, mem_norm=m_mem_norm, w_mem_kv=m_w_mem_kv,
                     w_out=m_w_out, post_norm=m_post_norm)
    moments_v = dict(pre_norm=v_pre_norm, w_in=v_w_in, sink_a=v_sink_a, mem_norm=v_mem_norm, w_mem_kv=v_w_mem_kv,
                     w_out=v_w_out, post_norm=v_post_norm)
    loss, rest = _adamw_all(dict(w_in=g_win, w_mem_kv=g_wmem, w_out=g_wout), r_stats, weights, moments_m, moments_v)
    return (loss, grad_x[None], *rest)
```

```python
import jax
import jax.numpy as jnp
from jax import lax
from jax.experimental import pallas as pl
from jax.experimental.pallas import tpu as pltpu

F32 = jnp.float32
BF16 = jnp.bfloat16

D_MODEL = 1024
HEAD_DIM = 64
ROT_DIM = 16
ROPE_THETA = 500000.0
BLOCK = 128
LANES = 128
N_MEM = 256
RMS_EPS = 1e-6
SCALE = HEAD_DIM ** -0.5
A_HEADS = 6
B_HEADS = 6
C_HEADS = 4
A_W, A_KV_W, B_W, C_W = 384, 128, 384, 256
_IN_PIECES = (("qa", A_W), ("ka", A_KV_W), ("va", A_KV_W), ("ga", A_W), ("qb", B_W), ("kb", B_W), ("vb", B_W),
              ("gb", B_W), ("qc", C_W), ("gc", C_W))
COLS, D_IN = {}, 0
for _name, _width in _IN_PIECES:
    COLS[_name] = (D_IN, D_IN + _width)
    D_IN += _width
N_DEV = 8
SHARD_IN = D_IN // N_DEV
SHARD_ROWS = D_MODEL // N_DEV
B_CONFIGS = ((128, 1), (512, 4), (2048, 16))
DILS = (4, 16)
NEG = -1e30
ATTN_BLOCKS_PER_STEP = 4
ATTN_BWD_BLOCKS_PER_STEP = 8
DELTA_LANE = 64
VMEM_LIMIT = 56 * 1024 * 1024

ADAM_LR, ADAM_B1, ADAM_B2, ADAM_EPS, ADAM_WD, ADAM_STEP = 0.001, 0.9, 0.999, 1e-08, 0.01, 10
MESH_ID = pl.DeviceIdType.MESH


def _params(**kw):
    return pltpu.CompilerParams(vmem_limit_bytes=VMEM_LIMIT, **kw)


def _full(shape):
    n = len(shape)
    return pl.BlockSpec(shape, lambda *_: (0,) * n)


def _row(tm, w):
    return pl.BlockSpec((tm, w), lambda i: (i, 0))


def _mesh_pos():
    return lax.axis_index("x"), lax.axis_index("y"), lax.axis_index("c")


def _dev_index(pos):
    return 4 * pos[0] + 2 * pos[1] + pos[2]


def _xor_peer(pos, s):
    x, y, c = pos
    return (1 - x if s & 4 else x, 1 - y if s & 2 else y, 1 - c if s & 1 else c)


def _perm_view(a, dil):
    return a.reshape(a.shape[0] // (BLOCK * dil), dil, BLOCK, a.shape[1])


def _perm_spec(tm, dil, w):
    chunk = BLOCK * dil
    if tm >= chunk:
        return pl.BlockSpec((tm // chunk, dil, BLOCK, w), lambda i: (i, 0, 0, 0))
    per = chunk // tm
    return pl.BlockSpec((1, dil, tm // dil, w), lambda i: (i // per, 0, i % per, 0))


def _put(scr, val):
    for c in range(val.shape[1] // LANES):
        scr[c] = val[:, LANES * c:LANES * (c + 1)]


def _get(scr):
    n = scr.shape[0]
    return scr[0] if n == 1 else jnp.concatenate([scr[c] for c in range(n)], axis=1)


def _get_class(scr, r, dil):
    n, rows = scr.shape[0], scr.shape[1]
    parts = [scr.at[c][pl.ds(r, rows // dil, stride=dil), :] for c in range(n)]
    return parts[0] if n == 1 else jnp.concatenate(parts, axis=1)


def _store_permuted(scr, out_ref, dil, dtype):
    for r in range(dil):
        out_ref[0, r] = _get_class(scr, r, dil).astype(dtype)


def _load_permuted(in_ref, scr, dil):
    n, rows = scr.shape[0], scr.shape[1]
    for r in range(dil):
        val = in_ref[0, r].astype(F32)
        for c in range(n):
            scr.at[c][pl.ds(r, rows // dil, stride=dil), :] = val[:, LANES * c:LANES * (c + 1)]
    return _get(scr)


def _rotate128(t, c, up, dn):
    half = ROT_DIM // 2
    return t * c + pltpu.roll(t, half, 1) * up + pltpu.roll(t, LANES - half, 1) * dn


def _rotate(t, c, up, dn):
    outs = [_rotate128(t[:, LANES * j:LANES * (j + 1)], c, up, dn) for j in range(t.shape[1] // LANES)]
    return outs[0] if len(outs) == 1 else jnp.concatenate(outs, axis=1)


def _per_query_head(kv):
    lane = lax.broadcasted_iota(jnp.int32, kv.shape, 1)
    other = pltpu.roll(kv, HEAD_DIM, 1)
    return jnp.concatenate([jnp.where(lane < HEAD_DIM, kv, other), kv, jnp.where(lane < HEAD_DIM, other, kv)], axis=1)


def _per_kv_head(d):
    s0, s1, s2 = (d[:, LANES * p:LANES * (p + 1)] for p in range(3))
    lane = lax.broadcasted_iota(jnp.int32, s0.shape, 1)
    return jnp.where(lane < HEAD_DIM, s0 + pltpu.roll(s0, HEAD_DIM, 1) + s1, s1 + s2 + pltpu.roll(s2, HEAD_DIM, 1))


def _w_in_scratch():
    return [pltpu.VMEM((D_MODEL, D_IN), BF16), pltpu.SemaphoreType.DMA((N_DEV,))]


def _stage_w_in(w_hbm, w_scr, sems):
    @pl.when(pl.program_id(0) == 0)
    def _():
        copies = [pltpu.make_async_copy(w_hbm.at[k], w_scr.at[:, pl.ds(SHARD_IN * k, SHARD_IN)], sems.at[k])
                  for k in range(N_DEV)]
        for cp in copies:
            cp.start()
        for cp in copies:
            cp.wait()


def _inproj(u, w_in_full, tabs, w_mem, w_out, tm=1024):
    seq = u.shape[0]
    n_chunk = D_IN // LANES
    n_steps = seq // tm

    def body(u_ref, w_hbm, c_ref, up_ref, dn_ref, wm_ref, wo_ref, qa_ref, ka_ref, va_ref,
             qb1_ref, kb1_ref, vb1_ref, qb4_ref, kb4_ref, vb4_ref, qb16_ref, kb16_ref, vb16_ref,
             qc_ref, gate_ref, wm_all, wo_all, proj, w_scr, w_sems, wm_b, wo_b, send_sems, recv_sems, local_sems):
        step = pl.program_id(0)
        shards, gathered = (wm_b, wo_b), (wm_all, wo_all)

        def gather_copies(arriving):
            pos = _mesh_pos()
            me = _dev_index(pos)
            local = [] if arriving else [
                pltpu.make_async_copy(shards[a], gathered[a].at[me], local_sems.at[a]) for a in range(2)]
            remote = []
            for s in range(1, N_DEV):
                peer = _xor_peer(pos, s)
                for a in range(2):
                    remote.append(pltpu.make_async_remote_copy(
                        src_ref=shards[a], dst_ref=gathered[a].at[_dev_index(peer) if arriving else me],
                        send_sem=send_sems.at[a, s], recv_sem=recv_sems.at[a, s], device_id=peer,
                        device_id_type=MESH_ID))
            return local, remote

        @pl.when(step == 0)
        def _():
            wm_b[...] = wm_ref[...].astype(BF16)
            wo_b[...] = wo_ref[...].astype(BF16)
            local, sends = gather_copies(arriving=False)
            for cp in local + sends:
                cp.start()

        _stage_w_in(w_hbm, w_scr, w_sems)
        u = u_ref[...]
        for n0 in range(0, D_IN, D_MODEL):
            acc = jnp.dot(u, w_scr[:, n0:n0 + D_MODEL], preferred_element_type=F32)
            for c3 in range(D_MODEL // LANES):
                proj[n0 // LANES + c3] = acc[:, LANES * c3:LANES * (c3 + 1)]
        c, up, dn = c_ref[...], up_ref[...], dn_ref[...]

        def chunks_of(piece):
            lo, hi = COLS[piece]
            return range(lo // LANES, hi // LANES)

        def cols(piece, rot=False, scale=None):
            parts = []
            for ch in chunks_of(piece):
                t = proj[ch]
                if rot:
                    t = _rotate128(t, c, up, dn)
                if scale is not None:
                    t = t * scale
                parts.append(t)
            return parts[0] if len(parts) == 1 else jnp.concatenate(parts, axis=1)

        qa_ref[...] = cols("qa", True, SCALE).astype(BF16)
        ka_ref[...] = _per_query_head(cols("ka", True)).astype(BF16)
        va_ref[...] = _per_query_head(cols("va")).astype(BF16)
        gate_ref[:, 0:A_W] = cols("ga").astype(BF16)
        gate_ref[:, A_W:A_W + B_W] = cols("gb").astype(BF16)
        gate_ref[:, A_W + B_W:D_MODEL] = cols("gc").astype(BF16)
        qc_ref[...] = cols("qc", False, SCALE).astype(BF16)
        for ch in chunks_of("qb"):
            proj[ch] = _rotate128(proj[ch], c, up, dn) * SCALE
        for ch in chunks_of("kb"):
            proj[ch] = _rotate128(proj[ch], c, up, dn)
        for piece, nat, p4, p16 in (("qb", qb1_ref, qb4_ref, qb16_ref), ("kb", kb1_ref, kb4_ref, kb16_ref),
                                    ("vb", vb1_ref, vb4_ref, vb16_ref)):
            chunks = chunks_of(piece)
            nat[...] = jnp.concatenate([proj[ch] for ch in chunks], axis=1).astype(BF16)
            for dil, ref in ((4, p4), (16, p16)):
                span = min(tm, BLOCK * dil)
                for cc in range(tm // span):
                    for rr in range(dil):
                        ref[cc, rr] = jnp.concatenate(
                            [proj.at[ch][pl.ds(cc * span + rr, span // dil, stride=dil), :] for ch in chunks],
                            axis=1).astype(BF16)

        @pl.when(step == n_steps - 1)
        def _():
            for cp in gather_copies(arriving=True)[1]:
                cp.wait_recv()
            local, sends = gather_copies(arriving=False)
            for cp in sends:
                cp.wait_send()
            for cp in local:
                cp.wait()

    nat_w = (A_W, A_W, A_W, B_W, B_W, B_W)
    out_specs = [_row(tm, w) for w in nat_w]
    out_shape = [jax.ShapeDtypeStruct((seq, w), BF16) for w in nat_w]
    for dil in DILS:
        out_specs += [_perm_spec(tm, dil, B_W)] * 3
        out_shape += [jax.ShapeDtypeStruct((seq // (BLOCK * dil), dil, BLOCK, B_W), BF16)] * 3
    hbm = pl.BlockSpec(memory_space=pl.ANY)
    out_specs += [_row(tm, C_W), _row(tm, D_MODEL), hbm, hbm]
    out_shape += [jax.ShapeDtypeStruct((seq, C_W), BF16), jax.ShapeDtypeStruct((seq, D_MODEL), BF16),
                  jax.ShapeDtypeStruct((N_DEV,) + w_mem.shape, BF16), jax.ShapeDtypeStruct((N_DEV,) + w_out.shape, BF16)]
    res = pl.pallas_call(
        body, name="inproj", grid=(n_steps,),
        in_specs=[_row(tm, D_MODEL), hbm, _row(tm, LANES), _row(tm, LANES), _row(tm, LANES),
                  _full(w_mem.shape), _full(w_out.shape)],
        out_specs=out_specs, out_shape=out_shape,
        scratch_shapes=[pltpu.VMEM((n_chunk, tm, LANES), F32)] + _w_in_scratch()
        + [pltpu.VMEM(w_mem.shape, BF16), pltpu.VMEM(w_out.shape, BF16), pltpu.SemaphoreType.DMA((2, N_DEV)),
           pltpu.SemaphoreType.DMA((2, N_DEV)), pltpu.SemaphoreType.DMA((2,))],
        compiler_params=_params(dimension_semantics=("arbitrary",)),
    )(u, w_in_full, *tabs, w_mem, w_out)
    qa, ka, va = res[0:3]
    qkv_b = {1: res[3:6], 4: [t.reshape(seq, B_W) for t in res[6:9]], 16: [t.reshape(seq, B_W) for t in res[9:12]]}
    return qa, ka, va, qkv_b, res[12], res[13], res[14], res[15]


def _memkv_fwd(mem, mem_g, w_mem_full):
    def body(mem_ref, g_ref, w_ref, mn_ref, mk_ref, mv_ref):
        mv_ = mem_ref[...]
        r = lax.rsqrt(jnp.mean(mv_ * mv_, axis=-1, keepdims=True) + RMS_EPS)
        mn = ((mv_ * r) * g_ref[...]).astype(BF16)
        mn_ref[...] = mn
        mkv = jnp.dot(mn, w_ref[...], preferred_element_type=F32)
        mk_ref[...] = mkv[:, 0:C_W].astype(BF16)
        mv_ref[...] = mkv[:, C_W:2 * C_W].astype(BF16)

    return pl.pallas_call(
        body, name="memkv_fwd",
        out_shape=[jax.ShapeDtypeStruct((N_MEM, D_MODEL), BF16),
                   jax.ShapeDtypeStruct((N_MEM, C_W), BF16), jax.ShapeDtypeStruct((N_MEM, C_W), BF16)],
        compiler_params=_params(),
    )(mem, mem_g, w_mem_full)


def _memkv_bwd(mem, mem_g, mn, w_mem_full, dmk, dmv):
    def body(mem_ref, g_ref, mn_ref, w_ref, dmk_ref, dmv_ref, dw_ref, st_ref):
        dmkv = jnp.concatenate([dmk_ref[...], dmv_ref[...]], axis=1).astype(BF16)
        dw_ref[...] = lax.dot_general(mn_ref[...], dmkv, (((0,), (0,)), ((), ())), preferred_element_type=F32)
        dmn = lax.dot_general(dmkv, w_ref[...], (((1,), (1,)), ((), ())), preferred_element_type=F32)
        mv_ = mem_ref[...]
        r = lax.rsqrt(jnp.mean(mv_ * mv_, axis=-1, keepdims=True) + RMS_EPS)
        st_ref[...] = jnp.zeros_like(st_ref)
        st_ref[0:1, :] = jnp.sum(dmn * (mv_ * r), axis=0, keepdims=True)

    return pl.pallas_call(
        body, name="memkv_bwd",
        out_shape=[jax.ShapeDtypeStruct((D_MODEL, 2 * C_W), F32), jax.ShapeDtypeStruct((8, D_MODEL), F32)],
        compiler_params=_params(),
    )(mem, mem_g, mn, w_mem_full, dmk, dmv)


def _band_mask(has_prev, max_dist):
    qi = lax.broadcasted_iota(jnp.int32, (BLOCK, 2 * BLOCK), 0)
    kj = lax.broadcasted_iota(jnp.int32, (BLOCK, 2 * BLOCK), 1)
    dist = qi + BLOCK - kj
    return (dist >= 0) & (dist <= max_dist) & ((kj >= BLOCK) | has_prev)


_NT = (((1,), (1,)), ((), ()))
_TN = (((0,), (0,)), ((), ()))


def _head_only(val, h):
    slab = _slabs_of(val)(h)
    lane = lax.broadcasted_iota(jnp.int32, slab.shape, 1)
    keep = (lane < HEAD_DIM) if h % 2 == 0 else (lane >= HEAD_DIM)
    return jnp.where(keep, slab, jnp.zeros((), slab.dtype))


def _slabs_of(val):
    return lambda h: val[:, LANES * (h // 2):LANES * (h // 2 + 1)]


class _BandSteps:
    def __init__(self, seq, dil, nq):
        self.nq, self.rows, self.consecutive = nq, nq * BLOCK, dil == 1
        nb = seq // dil // BLOCK
        if self.consecutive:
            assert nb % nq == 0
            self.outer, self.inner, self.stride = 1, nb // nq, 1
        else:
            assert dil % nq == 0
            self.outer, self.inner, self.stride = dil // nq, nb, dil // nq

    def own(self, w, clamp=False):
        cur = (lambda i: jnp.minimum(i, self.inner - 1)) if clamp else (lambda i: i)
        return pl.BlockSpec((self.rows, w), lambda r, i: (cur(i) * self.stride + r, 0))

    def prev(self, w, clamp=False):
        cur = (lambda i: jnp.minimum(i, self.inner - 1)) if clamp else (lambda i: i)
        if self.consecutive:
            return pl.BlockSpec((BLOCK, w), lambda r, i: (jnp.maximum(cur(i) * self.nq - 1, 0), 0))
        return pl.BlockSpec((self.rows, w), lambda r, i: (jnp.maximum(cur(i) - 1, 0) * self.stride + r, 0))

    def late(self, w):
        return pl.BlockSpec((self.rows, w), lambda r, i: (jnp.maximum(i - 1, 0) * self.stride + r, 0))

    def rows_of(self, j):
        return slice(BLOCK * j, BLOCK * (j + 1))

    def keys(self, p_ref, c_ref, j):
        if not self.consecutive:
            before = p_ref[self.rows_of(j), :]
        elif j == 0:
            before = p_ref[...]
        else:
            before = c_ref[self.rows_of(j - 1), :]
        return jnp.concatenate([before, c_ref[self.rows_of(j), :]], axis=0)

    def has_prev(self, i, j):
        return True if (self.consecutive and j > 0) else (i > 0)


def _banded_fwd(q, k, v, sink, *, dil, heads, max_dist, nq, name):
    seq = q.shape[0]
    qw = kw = heads * HEAD_DIM
    steps = _BandSteps(seq, dil, nq)

    def body(*refs):
        if sink is not None:
            sink_ref, refs = refs[0], refs[1:]
        q_ref, kp_ref, kc_ref, vp_ref, vc_ref, o_ref, lse_ref, s_scr, p_scr = refs
        i = pl.program_id(1)
        lane = lax.broadcasted_iota(jnp.int32, (BLOCK, LANES), 1)
        k_of = [_slabs_of(steps.keys(kp_ref, kc_ref, j)) for j in range(nq)]
        v_of = [_slabs_of(steps.keys(vp_ref, vc_ref, j)) for j in range(nq)]
        for j in range(nq):
            qv = q_ref[steps.rows_of(j), :]
            for h in range(heads):
                s_scr[j * heads + h] = lax.dot_general(_head_only(qv, h), k_of[j](h), _NT, preferred_element_type=F32)
        ls = {}
        for j in range(nq):
            valid = _band_mask(steps.has_prev(i, j), max_dist)
            lse_tile = jnp.zeros((BLOCK, LANES), F32)
            for h in range(heads):
                s = jnp.where(valid, s_scr[j * heads + h], NEG)
                m = jnp.max(s, axis=-1, keepdims=True)
                if sink is not None:
                    sk = sink_ref[h]
                    m = jnp.maximum(m, sk)
                p = jnp.exp(s - m)
                l = jnp.sum(p, axis=-1, keepdims=True)
                if sink is not None:
                    l = l + jnp.exp(sk - m)
                p_scr[j * heads + h] = p.astype(BF16)
                ls[j, h] = l
                lse_tile = jnp.where(lane == h, m + jnp.log(l), lse_tile)
            lse_ref[steps.rows_of(j), :] = lse_tile
        for j in range(nq):
            for pr in range(heads // 2):
                he, ho = 2 * pr, 2 * pr + 1
                even = jnp.dot(p_scr[j * heads + he], v_of[j](he), preferred_element_type=F32) / ls[j, he]
                odd = jnp.dot(p_scr[j * heads + ho], v_of[j](ho), preferred_element_type=F32) / ls[j, ho]
                o_ref[steps.rows_of(j), LANES * pr:LANES * (pr + 1)] = jnp.where(lane < HEAD_DIM, even, odd).astype(BF16)

    in_specs = [steps.own(qw), steps.prev(kw), steps.own(kw), steps.prev(kw), steps.own(kw)]
    args = [q, k, k, v, v]
    if sink is not None:
        in_specs = [pl.BlockSpec(memory_space=pltpu.SMEM)] + in_specs
        args = [sink] + args
    return pl.pallas_call(
        body, name=name, grid=(steps.outer, steps.inner), in_specs=in_specs,
        out_specs=[steps.own(qw), steps.own(LANES)],
        out_shape=[jax.ShapeDtypeStruct((seq, qw), BF16), jax.ShapeDtypeStruct((seq, LANES), F32)],
        scratch_shapes=[pltpu.VMEM((nq * heads, BLOCK, 2 * BLOCK), F32), pltpu.VMEM((nq * heads, BLOCK, 2 * BLOCK), BF16)],
        compiler_params=_params(dimension_semantics=("arbitrary", "arbitrary")),
    )(*args)


def _banded_bwd(q, k, v, d_out, stat, sink, *, dil, heads, max_dist, nq, name, reduce_scatter=()):
    seq = q.shape[0]
    qw = kw = heads * HEAD_DIM
    steps = _BandSteps(seq, dil, nq)
    n_rs = len(reduce_scatter)
    n_in = 7 + n_rs
    n_flat = steps.outer * (steps.inner + 1)

    def body(*refs):
        refs = list(refs)
        sink_ref = refs.pop(0) if sink is not None else None
        (q_ref, kp_ref, kc_ref, vp_ref, vc_ref, do_ref, st_ref), partials = refs[:7], refs[7:n_in]
        refs = refs[n_in:]
        dsink_ref = refs.pop(0) if sink is not None else None
        (dq_ref, dk_ref, dv_ref), sums = refs[:3], refs[3:3 + n_rs]
        kcar, vcar, s_scr, dp_scr, p_scr, ds_scr = refs[3 + n_rs:9 + n_rs]
        r, i = pl.program_id(0), pl.program_id(1)
        if n_rs:
            exchange = _ReduceScatter(tuple(partials), tuple(sums), refs[9 + n_rs:])
            flat = r * (steps.inner + 1) + i

            @pl.when(flat == 0)
            def _():
                exchange.start()

            @pl.when(flat == min(2, n_flat - 1))
            def _():
                exchange.send_chip_sums()

        @pl.when(i == 0)
        def _():
            kcar[...] = jnp.zeros_like(kcar)
            vcar[...] = jnp.zeros_like(vcar)

        if sink is not None:
            @pl.when((i == 0) & (r == 0))
            def _():
                dsink_ref[...] = jnp.zeros_like(dsink_ref)

        @pl.when(i < steps.inner)
        def _():
            lane = lax.broadcasted_iota(jnp.int32, (1, LANES), 1)
            lane_q = lax.broadcasted_iota(jnp.int32, (BLOCK, LANES), 1)
            k_of = [_slabs_of(steps.keys(kp_ref, kc_ref, j)) for j in range(nq)]
            v_of = [_slabs_of(steps.keys(vp_ref, vc_ref, j)) for j in range(nq)]
            qms, doms = {}, {}
            for j in range(nq):
                qv, dov = q_ref[steps.rows_of(j), :], do_ref[steps.rows_of(j), :]
                for h in range(heads):
                    qms[j, h], doms[j, h] = _head_only(qv, h), _head_only(dov, h)
                    s_scr[j * heads + h] = lax.dot_general(qms[j, h], k_of[j](h), _NT, preferred_element_type=F32)
                    dp_scr[j * heads + h] = lax.dot_general(doms[j, h], v_of[j](h), _NT, preferred_element_type=F32)
            dsink_row = jnp.zeros((1, LANES), F32)
            if sink is not None:
                sink_row = jnp.zeros((1, LANES), F32)
                for h in range(heads):
                    sink_row = jnp.where(lane == h, sink_ref[h], sink_row)
            for j in range(nq):
                st = st_ref[steps.rows_of(j), :]
                valid = _band_mask(steps.has_prev(i, j), max_dist)
                for h in range(heads):
                    lse_h = st[:, h:h + 1]
                    delta = st[:, DELTA_LANE + h:DELTA_LANE + h + 1]
                    p = jnp.where(valid, jnp.exp(s_scr[j * heads + h] - lse_h), 0.0)
                    p_scr[j * heads + h] = p.astype(BF16)
                    ds_scr[j * heads + h] = (p * (dp_scr[j * heads + h] - delta)).astype(BF16)
                if sink is not None:
                    term = -jnp.exp(sink_row - st) * pltpu.roll(st, LANES - DELTA_LANE, 1)
                    dsink_row = dsink_row + jnp.sum(jnp.where(lane_q < heads, term, 0.0), axis=0, keepdims=True)
            for j in range(nq):
                for pr in range(heads // 2):
                    he, ho = 2 * pr, 2 * pr + 1
                    even = jnp.dot(ds_scr[j * heads + he], k_of[j](he), preferred_element_type=F32)
                    odd = jnp.dot(ds_scr[j * heads + ho], k_of[j](ho), preferred_element_type=F32)
                    dq_ref[steps.rows_of(j), LANES * pr:LANES * (pr + 1)] = (
                        jnp.where(lane_q < HEAD_DIM, even, odd).astype(BF16))
            if steps.consecutive:
                dk_ref[...] = kcar[...].astype(BF16)
                dv_ref[...] = vcar[...].astype(BF16)
            for j in range(nq):
                for slab in range(kw // LANES):
                    he, ho = j * heads + 2 * slab, j * heads + 2 * slab + 1
                    dk_j = (lax.dot_general(ds_scr[he], qms[j, 2 * slab], _TN, preferred_element_type=F32)
                            + lax.dot_general(ds_scr[ho], qms[j, 2 * slab + 1], _TN, preferred_element_type=F32))
                    dv_j = (lax.dot_general(p_scr[he], doms[j, 2 * slab], _TN, preferred_element_type=F32)
                            + lax.dot_general(p_scr[ho], doms[j, 2 * slab + 1], _TN, preferred_element_type=F32))
                    sl = slice(LANES * slab, LANES * (slab + 1))
                    own_rows = steps.rows_of(j)
                    if not steps.consecutive:
                        dk_ref[own_rows, sl] = (kcar[own_rows, sl] + dk_j[0:BLOCK]).astype(BF16)
                        dv_ref[own_rows, sl] = (vcar[own_rows, sl] + dv_j[0:BLOCK]).astype(BF16)
                    elif j == 0:
                        last = steps.rows_of(nq - 1)
                        dk_ref[last, sl] = (kcar[last, sl] + dk_j[0:BLOCK]).astype(BF16)
                        dv_ref[last, sl] = (vcar[last, sl] + dv_j[0:BLOCK]).astype(BF16)
                    else:
                        before = steps.rows_of(j - 1)
                        kcar[before, sl] += dk_j[0:BLOCK]
                        vcar[before, sl] += dv_j[0:BLOCK]
                    kcar[own_rows, sl] = dk_j[BLOCK:2 * BLOCK]
                    vcar[own_rows, sl] = dv_j[BLOCK:2 * BLOCK]
            if sink is not None:
                dsink_ref[0:1, :] += dsink_row

        @pl.when(i == steps.inner)
        def _():
            dk_ref[...] = kcar[...].astype(BF16)
            dv_ref[...] = vcar[...].astype(BF16)

        if n_rs:
            @pl.when(flat == n_flat - 1)
            def _():
                exchange.finish()

    own, prev = (lambda w: steps.own(w, clamp=True)), (lambda w: steps.prev(w, clamp=True))
    rs_shapes = [t.shape[1:] for t in reduce_scatter]
    in_specs = ([own(qw), prev(kw), own(kw), prev(kw), own(kw), own(qw), own(LANES)]
                + [pl.BlockSpec(memory_space=pl.ANY)] * n_rs)
    args = [q, k, k, v, v, d_out, stat, *reduce_scatter]
    out_specs = [own(qw), steps.late(kw), steps.late(kw)] + [_full(s) for s in rs_shapes]
    out_shape = [jax.ShapeDtypeStruct((seq, qw), BF16), jax.ShapeDtypeStruct((seq, kw), BF16),
                 jax.ShapeDtypeStruct((seq, kw), BF16)] + [jax.ShapeDtypeStruct(s, F32) for s in rs_shapes]
    if sink is not None:
        in_specs = [pl.BlockSpec(memory_space=pltpu.SMEM)] + in_specs
        args = [sink] + args
        out_specs = [_full((8, LANES))] + out_specs
        out_shape = [jax.ShapeDtypeStruct((8, LANES), F32)] + out_shape
    n_hb = nq * heads
    res = pl.pallas_call(
        body, name=name, grid=(steps.outer, steps.inner + 1), in_specs=in_specs, out_specs=out_specs,
        out_shape=out_shape,
        scratch_shapes=[pltpu.VMEM((steps.rows, kw), F32), pltpu.VMEM((steps.rows, kw), F32)]
        + [pltpu.VMEM((n_hb, BLOCK, 2 * BLOCK), F32)] * 2 + [pltpu.VMEM((n_hb, BLOCK, 2 * BLOCK), BF16)] * 2
        + (_ReduceScatter.scratch_shapes(rs_shapes) if n_rs else []),
        compiler_params=_params(dimension_semantics=("arbitrary", "arbitrary")),
    )(*args)
    if sink is not None:
        return (*res[1:4], res[0], *res[4:])
    return res


def _cross_fwd(q, mk, mv, tq=1024):
    seq = q.shape[0]

    def body(q_ref, mk_ref, mv_ref, o_ref, lse_ref, s_scr, p_scr):
        qv = q_ref[...]
        k_of, v_of = _slabs_of(mk_ref[...]), _slabs_of(mv_ref[...])
        lane = lax.broadcasted_iota(jnp.int32, (tq, LANES), 1)
        lse_tile = jnp.zeros((tq, LANES), F32)
        for h in range(C_HEADS):
            s_scr[h] = lax.dot_general(_head_only(qv, h), k_of(h), _NT, preferred_element_type=F32)
        ls = []
        for h in range(C_HEADS):
            s = s_scr[h]
            m = jnp.max(s, axis=-1, keepdims=True)
            p = jnp.exp(s - m)
            l = jnp.sum(p, axis=-1, keepdims=True)
            p_scr[h] = p.astype(BF16)
            ls.append(l)
            lse_tile = jnp.where(lane == h, m + jnp.log(l), lse_tile)
        for pr in range(C_HEADS // 2):
            even = jnp.dot(p_scr[2 * pr], v_of(2 * pr), preferred_element_type=F32) / ls[2 * pr]
            odd = jnp.dot(p_scr[2 * pr + 1], v_of(2 * pr + 1), preferred_element_type=F32) / ls[2 * pr + 1]
            o_ref[:, LANES * pr:LANES * (pr + 1)] = jnp.where(lane < HEAD_DIM, even, odd).astype(BF16)
        lse_ref[...] = lse_tile

    return pl.pallas_call(
        body, name="cross_fwd", grid=(seq // tq,),
        in_specs=[_row(tq, C_W), _full((N_MEM, C_W)), _full((N_MEM, C_W))],
        out_specs=[_row(tq, C_W), _row(tq, LANES)],
        out_shape=[jax.ShapeDtypeStruct((seq, C_W), BF16), jax.ShapeDtypeStruct((seq, LANES), F32)],
        scratch_shapes=[pltpu.VMEM((C_HEADS, tq, N_MEM), F32), pltpu.VMEM((C_HEADS, tq, N_MEM), BF16)],
        compiler_params=_params(dimension_semantics=("arbitrary",)),
    )(q, mk, mv)


def _cross_bwd(q, mk, mv, d_out, stat, tq=1024):
    seq = q.shape[0]

    def body(q_ref, mk_ref, mv_ref, do_ref, st_ref, dq_ref, dmk_ref, dmv_ref, s_scr, dp_scr, p_scr, ds_scr):
        @pl.when(pl.program_id(0) == 0)
        def _():
            dmk_ref[...] = jnp.zeros_like(dmk_ref)
            dmv_ref[...] = jnp.zeros_like(dmv_ref)

        qv, dov, st = q_ref[...], do_ref[...], st_ref[...]
        k_of, v_of = _slabs_of(mk_ref[...]), _slabs_of(mv_ref[...])
        qms = [_head_only(qv, h) for h in range(C_HEADS)]
        doms = [_head_only(dov, h) for h in range(C_HEADS)]
        for h in range(C_HEADS):
            s_scr[h] = lax.dot_general(qms[h], k_of(h), _NT, preferred_element_type=F32)
            dp_scr[h] = lax.dot_general(doms[h], v_of(h), _NT, preferred_element_type=F32)
        for h in range(C_HEADS):
            p = jnp.exp(s_scr[h] - st[:, h:h + 1])
            p_scr[h] = p.astype(BF16)
            ds_scr[h] = (p * (dp_scr[h] - st[:, DELTA_LANE + h:DELTA_LANE + h + 1])).astype(BF16)
        lane = lax.broadcasted_iota(jnp.int32, (tq, LANES), 1)
        for pr in range(C_HEADS // 2):
            sl = slice(LANES * pr, LANES * (pr + 1))
            even = jnp.dot(ds_scr[2 * pr], k_of(2 * pr), preferred_element_type=F32)
            odd = jnp.dot(ds_scr[2 * pr + 1], k_of(2 * pr + 1), preferred_element_type=F32)
            dq_ref[:, sl] = jnp.where(lane < HEAD_DIM, even, odd).astype(BF16)
            dmk_ref[:, sl] += (lax.dot_general(ds_scr[2 * pr], qms[2 * pr], _TN, preferred_element_type=F32)
                               + lax.dot_general(ds_scr[2 * pr + 1], qms[2 * pr + 1], _TN, preferred_element_type=F32))
            dmv_ref[:, sl] += (lax.dot_general(p_scr[2 * pr], doms[2 * pr], _TN, preferred_element_type=F32)
                               + lax.dot_general(p_scr[2 * pr + 1], doms[2 * pr + 1], _TN, preferred_element_type=F32))

    return pl.pallas_call(
        body, name="cross_bwd", grid=(seq // tq,),
        in_specs=[_row(tq, C_W), _full((N_MEM, C_W)), _full((N_MEM, C_W)), _row(tq, C_W), _row(tq, LANES)],
        out_specs=[_row(tq, C_W), _full((N_MEM, C_W)), _full((N_MEM, C_W))],
        out_shape=[jax.ShapeDtypeStruct((seq, C_W), BF16), jax.ShapeDtypeStruct((N_MEM, C_W), F32),
                   jax.ShapeDtypeStruct((N_MEM, C_W), F32)],
        scratch_shapes=[pltpu.VMEM((C_HEADS, tq, N_MEM), F32)] * 2 + [pltpu.VMEM((C_HEADS, tq, N_MEM), BF16)] * 2,
        compiler_params=_params(dimension_semantics=("arbitrary",)),
    )(q, mk, mv, d_out, stat)


def _per_head(tile, width):
    rows = tile.shape[0]
    lane = lax.broadcasted_iota(jnp.int32, (rows, LANES), 1)
    slabs = []
    for p in range(width // LANES):
        even = jnp.broadcast_to(tile[:, 2 * p:2 * p + 1], (rows, LANES))
        odd = jnp.broadcast_to(tile[:, 2 * p + 1:2 * p + 2], (rows, LANES))
        slabs.append(jnp.where(lane < HEAD_DIM, even, odd))
    return slabs[0] if len(slabs) == 1 else jnp.concatenate(slabs, axis=1)


def _with_delta(lse_tile, prod):
    rows = lse_tile.shape[0]
    lane = lax.broadcasted_iota(jnp.int32, (rows, LANES), 1)
    tile = lse_tile
    for p in range(prod.shape[1] // LANES):
        slab = prod[:, LANES * p:LANES * (p + 1)]
        even = jnp.sum(jnp.where(lane < HEAD_DIM, slab, 0.0), axis=-1, keepdims=True)
        odd = jnp.sum(jnp.where(lane >= HEAD_DIM, slab, 0.0), axis=-1, keepdims=True)
        tile = jnp.where(lane == DELTA_LANE + 2 * p, even, tile)
        tile = jnp.where(lane == DELTA_LANE + 2 * p + 1, odd, tile)
    return tile


def _mid(oa, lse_a, ob, lse_b, oc, lse_c, gate, x, target, w_out_full, post_g, tm=512):
    seq = x.shape[0]
    n_b = B_W // LANES

    def body(oa_ref, la_ref, b1_ref, l1_ref, b4_ref, l4_ref, b16_ref, l16_ref, oc_ref, lc_ref,
             gate_ref, x_ref, t_ref, w_ref, pg_ref,
             dh_ref, dg_ref, doa_ref, sa_ref, dob1_ref, sb1_ref, dob4_ref, sb4_ref, dob16_ref, sb16_ref,
             doc_ref, sc_ref, dw_ref, st_ref, scr_b4, scr_b16, scr_l4, scr_l16, scr_do, scr_sb):
        @pl.when(pl.program_id(0) == 0)
        def _():
            dw_ref[...] = jnp.zeros_like(dw_ref)
            st_ref[...] = jnp.zeros_like(st_ref)

        b1, l1 = b1_ref[...].astype(F32), l1_ref[...]
        b4, l4 = _load_permuted(b4_ref, scr_b4, 4), _load_permuted(l4_ref, scr_l4, 4)
        b16, l16 = _load_permuted(b16_ref, scr_b16, 16), _load_permuted(l16_ref, scr_l16, 16)
        lm = jnp.maximum(jnp.maximum(l1, l4), l16)
        e1, e4, e16 = jnp.exp(l1 - lm), jnp.exp(l4 - lm), jnp.exp(l16 - lm)
        den = e1 + e4 + e16
        lse_b_tile = lm + jnp.log(den)
        ob_v = _per_head(e1 / den, B_W) * b1 + _per_head(e4 / den, B_W) * b4 + _per_head(e16 / den, B_W) * b16
        o_all = jnp.concatenate([oa_ref[...].astype(F32), ob_v, oc_ref[...].astype(F32)], axis=1)
        g = gate_ref[...].astype(F32)
        sig = 1.0 / (1.0 + jnp.exp(-g))
        silu = g * sig
        y = (o_all * silu).astype(BF16)
        w = w_ref[...]
        z = jnp.dot(y, w, preferred_element_type=F32)
        rz = lax.rsqrt(jnp.mean(z * z, axis=-1, keepdims=True) + RMS_EPS)
        hn = z * rz
        pg = pg_ref[...]
        err = (x_ref[...] + hn * pg) - t_ref[...]
        loss = 0.5 * jnp.sum(jnp.mean(err * err, axis=-1, keepdims=True), axis=0, keepdims=True)
        dh = err * (1.0 / D_MODEL)
        dh_ref[...] = dh.astype(BF16)
        st_ref[0:1, :] += jnp.sum(dh * hn, axis=0, keepdims=True)
        st_ref[1:2, :] += jnp.broadcast_to(loss, (1, D_MODEL))
        dhn = dh * pg
        dz = (rz * (dhn - hn * jnp.mean(dhn * hn, axis=-1, keepdims=True))).astype(BF16)
        dy = lax.dot_general(dz, w, _NT, preferred_element_type=F32)
        dw_ref[...] += lax.dot_general(y, dz, _TN, preferred_element_type=F32)
        dg_ref[...] = (dy * o_all * (sig * (1.0 + g * (1.0 - sig)))).astype(BF16)
        d_o = (dy * silu).astype(BF16)
        prod = d_o.astype(F32) * o_all
        doa_ref[...] = d_o[:, 0:A_W]
        sa_ref[...] = _with_delta(la_ref[...], prod[:, 0:A_W])
        doc_ref[...] = d_o[:, A_W + B_W:D_MODEL]
        sc_ref[...] = _with_delta(lc_ref[...], prod[:, A_W + B_W:D_MODEL])
        d_ob = d_o[:, A_W:A_W + B_W]
        stat_b = _with_delta(lse_b_tile, prod[:, A_W:A_W + B_W])
        dob1_ref[...] = d_ob
        sb1_ref[...] = stat_b
        _put(scr_do, d_ob.astype(F32))
        _put(scr_sb, stat_b)
        _store_permuted(scr_do, dob4_ref, 4, BF16)
        _store_permuted(scr_sb, sb4_ref, 4, F32)
        _store_permuted(scr_do, dob16_ref, 16, BF16)
        _store_permuted(scr_sb, sb16_ref, 16, F32)

    p4 = lambda w: _perm_spec(tm, 4, w)
    p16 = lambda w: _perm_spec(tm, 16, w)
    in_specs = [_row(tm, A_W), _row(tm, LANES), _row(tm, B_W), _row(tm, LANES), p4(B_W), p4(LANES), p16(B_W), p16(LANES),
                _row(tm, C_W), _row(tm, LANES), _row(tm, D_MODEL), _row(tm, D_MODEL), _row(tm, D_MODEL),
                _full((D_MODEL, D_MODEL)), _full((1, D_MODEL))]
    sds = jax.ShapeDtypeStruct
    v4 = lambda w, dt: sds((seq // (BLOCK * 4), 4, BLOCK, w), dt)
    v16 = lambda w, dt: sds((seq // (BLOCK * 16), 16, BLOCK, w), dt)
    out_specs = [_row(tm, D_MODEL), _row(tm, D_MODEL), _row(tm, A_W), _row(tm, LANES), _row(tm, B_W), _row(tm, LANES),
                 p4(B_W), p4(LANES), p16(B_W), p16(LANES), _row(tm, C_W), _row(tm, LANES),
                 _full((D_MODEL, D_MODEL)), _full((8, D_MODEL))]
    out_shape = [sds((seq, D_MODEL), BF16), sds((seq, D_MODEL), BF16), sds((seq, A_W), BF16), sds((seq, LANES), F32),
                 sds((seq, B_W), BF16), sds((seq, LANES), F32), v4(B_W, BF16), v4(LANES, F32), v16(B_W, BF16),
                 v16(LANES, F32), sds((seq, C_W), BF16), sds((seq, LANES), F32),
                 sds((D_MODEL, D_MODEL), F32), sds((8, D_MODEL), F32)]
    res = pl.pallas_call(
        body, name="mid", grid=(seq // tm,), in_specs=in_specs, out_specs=out_specs, out_shape=out_shape,
        scratch_shapes=[pltpu.VMEM((n_b, tm, LANES), F32), pltpu.VMEM((n_b, tm, LANES), F32),
                        pltpu.VMEM((1, tm, LANES), F32), pltpu.VMEM((1, tm, LANES), F32),
                        pltpu.VMEM((n_b, tm, LANES), F32), pltpu.VMEM((1, tm, LANES), F32)],
        compiler_params=_params(dimension_semantics=("arbitrary",)),
    )(oa, lse_a, ob[1], lse_b[1], _perm_view(ob[4], 4), _perm_view(lse_b[4], 4), _perm_view(ob[16], 16),
      _perm_view(lse_b[16], 16), oc, lse_c, gate, x, target, w_out_full, post_g)
    dh, d_gate, do_a, st_a, do_b1, st_b1, do_b4, st_b4, do_b16, st_b16, do_c, st_c, d_wout, stats = res
    flat = lambda t: t.reshape(seq, t.shape[-1])
    d_b = {1: (do_b1, st_b1), 4: (flat(do_b4), flat(st_b4)), 16: (flat(do_b16), flat(st_b16))}
    return dh, d_gate, (do_a, st_a), d_b, (do_c, st_c), d_wout, stats


def _inproj_bwd(x, u, dh, pre_g, w_in_full, tabs, dqa, dka, dva, dqkv_b, dqc, dgate, tm=512):
    seq = x.shape[0]
    n_b = B_W // LANES

    def body(x_ref, u_ref, dh_ref, g_ref, w_hbm, c_ref, up_ref, dn_ref, dqa_ref, dka_ref, dva_ref,
             dq1, dk1, dv1, dq4, dk4, dv4, dq16, dk16, dv16, dqc_ref, dg_ref,
             gx_ref, dw_ref, st_ref, scr4, scr16, w_scr, w_sems, dp_ref):
        _stage_w_in(w_hbm, w_scr, w_sems)

        @pl.when(pl.program_id(0) == 0)
        def _():
            st_ref[...] = jnp.zeros_like(st_ref)
            dw_ref[...] = jnp.zeros_like(dw_ref)

        c, up, dn = c_ref[...], -up_ref[...], -dn_ref[...]
        unrot = lambda t: _rotate(t, c, up, dn)
        total = lambda r1, r4, r16: (r1[...].astype(F32) + _load_permuted(r4, scr4, 4)
                                     + _load_permuted(r16, scr16, 16))
        at = lambda piece: slice(*COLS[piece])
        dp_ref[:, at("qa")] = (unrot(dqa_ref[...].astype(F32)) * SCALE).astype(BF16)
        dp_ref[:, at("ka")] = unrot(_per_kv_head(dka_ref[...].astype(F32))).astype(BF16)
        dp_ref[:, at("va")] = _per_kv_head(dva_ref[...].astype(F32)).astype(BF16)
        dp_ref[:, at("ga")] = dg_ref[:, 0:A_W]
        dp_ref[:, at("qb")] = (unrot(total(dq1, dq4, dq16)) * SCALE).astype(BF16)
        dp_ref[:, at("kb")] = unrot(total(dk1, dk4, dk16)).astype(BF16)
        dp_ref[:, at("vb")] = total(dv1, dv4, dv16).astype(BF16)
        dp_ref[:, at("gb")] = dg_ref[:, A_W:A_W + B_W]
        dp_ref[:, at("qc")] = (dqc_ref[...].astype(F32) * SCALE).astype(BF16)
        dp_ref[:, at("gc")] = dg_ref[:, A_W + B_W:D_MODEL]
        du = lax.dot_general(dp_ref[...], w_scr[...], _NT, preferred_element_type=F32)
        res = lax.dot_general(u_ref[...], dp_ref[...], _TN, preferred_element_type=F32)
        for k in range(N_DEV):
            dw_ref[k] += res[:, SHARD_IN * k:SHARD_IN * (k + 1)]
        xv = x_ref[...]
        r = lax.rsqrt(jnp.mean(xv * xv, axis=-1, keepdims=True) + RMS_EPS)
        xh = xv * r
        st_ref[0:1, :] += jnp.sum(du * xh, axis=0, keepdims=True)
        dxh = du * g_ref[...]
        gx_ref[...] = dh_ref[...].astype(F32) + r * (dxh - xh * jnp.mean(dxh * xh, axis=-1, keepdims=True))

    in_specs = ([_row(tm, D_MODEL), _row(tm, D_MODEL), _row(tm, D_MODEL), _full((1, D_MODEL)),
                 pl.BlockSpec(memory_space=pl.ANY),
                 _row(tm, LANES), _row(tm, LANES), _row(tm, LANES), _row(tm, A_W), _row(tm, A_W), _row(tm, A_W)]
                + [_row(tm, B_W)] * 3 + [_perm_spec(tm, 4, B_W)] * 3 + [_perm_spec(tm, 16, B_W)] * 3
                + [_row(tm, C_W), _row(tm, D_MODEL)])
    dw_spec = pl.BlockSpec((N_DEV, D_MODEL, SHARD_IN), lambda i: (0, 0, 0), pipeline_mode=pl.Buffered(1))
    return pl.pallas_call(
        body, name="inproj_bwd", grid=(seq // tm,), in_specs=in_specs,
        out_specs=[_row(tm, D_MODEL), dw_spec, _full((8, D_MODEL))],
        out_shape=[jax.ShapeDtypeStruct((seq, D_MODEL), F32), jax.ShapeDtypeStruct((N_DEV, D_MODEL, SHARD_IN), F32),
                   jax.ShapeDtypeStruct((8, D_MODEL), F32)],
        scratch_shapes=[pltpu.VMEM((n_b, tm, LANES), F32), pltpu.VMEM((n_b, tm, LANES), F32)] + _w_in_scratch()
        + [pltpu.VMEM((tm, D_IN), BF16)],
        compiler_params=_params(dimension_semantics=("arbitrary",)),
    )(x, u, dh, pre_g, w_in_full, *tabs, dqa, dka, dva, *dqkv_b[1], *[_perm_view(t, 4) for t in dqkv_b[4]],
      *[_perm_view(t, 16) for t in dqkv_b[16]], dqc, dgate)


class _ReduceScatter:
    def __init__(self, ins, outs, scratch):
        self.n = n = len(ins)
        self.ins, self.outs = ins, outs
        self.mine, self.got, self.snd, self.rcv = (scratch[n * t:n * (t + 1)] for t in range(4))
        self.load_sems, self.d2d_send, self.d2d_recv, self.ici_send, self.ici_recv = scratch[4 * n:]
        self.pos = _mesh_pos()
        self.pairs = [(a, kk) for kk in (3, 1, 2) for a in range(n)]

    @staticmethod
    def scratch_shapes(shapes):
        return ([pltpu.VMEM((4,) + s, F32) for s in shapes] + [pltpu.VMEM((4,) + s, F32) for s in shapes]
                + [pltpu.VMEM((3,) + s, BF16) for s in shapes] + [pltpu.VMEM((3,) + s, BF16) for s in shapes]
                + [pltpu.SemaphoreType.DMA((len(shapes), 4))] * 5)

    def _chip(self, kk):
        x, y, _ = self.pos
        return (1 - x if kk & 2 else x, 1 - y if kk & 1 else y)

    def _load(self, a, kk):
        block = _dev_index((*self._chip(kk), self.pos[2]))
        return pltpu.make_async_copy(self.ins[a].at[block], self.mine[a].at[kk], self.load_sems.at[a, kk])

    def _swap(self, a, kk):
        x, y, c = self.pos
        return pltpu.make_async_remote_copy(
            src_ref=self.ins[a].at[_dev_index((*self._chip(kk), 1 - c))], dst_ref=self.got[a].at[kk],
            send_sem=self.d2d_send.at[a, kk], recv_sem=self.d2d_recv.at[a, kk],
            device_id=(x, y, 1 - c), device_id_type=MESH_ID)

    def _hop(self, a, kk):
        return pltpu.make_async_remote_copy(
            src_ref=self.snd[a].at[kk - 1], dst_ref=self.rcv[a].at[kk - 1], send_sem=self.ici_send.at[a, kk],
            recv_sem=self.ici_recv.at[a, kk], device_id=(*self._chip(kk), self.pos[2]), device_id_type=MESH_ID)

    def start(self):
        for kk in (3, 1, 2, 0):
            for a in range(self.n):
                self._load(a, kk).start()
                self._swap(a, kk).start()

    def send_chip_sums(self):
        for a, kk in self.pairs:
            self._load(a, kk).wait()
            self._swap(a, kk).wait_recv()
            self.snd[a][kk - 1] = (self.mine[a][kk] + self.got[a][kk]).astype(BF16)
            self._hop(a, kk).start()

    def finish(self):
        for a in range(self.n):
            self._load(a, 0).wait()
            self._swap(a, 0).wait_recv()
            acc = self.mine[a][0] + self.got[a][0]
            for kk in (1, 2, 3):
                self._hop(a, kk).wait_recv()
                acc = acc + self.rcv[a][kk - 1].astype(F32)
            self.outs[a][...] = acc
        for kk in range(4):
            for a in range(self.n):
                self._swap(a, kk).wait_send()
        for a, kk in self.pairs:
            self._hop(a, kk).wait_send()


def _local_step(x, mem, pre_g, w_in, sink, mem_g, w_mem, w_out, post_g, target):
    u, *tabs, w_in_full = _prep(x, pre_g, w_in)
    qa, ka, va, qkv_b, qc, gate, w_mem_all, w_out_all = _inproj(u, w_in_full, tabs, w_mem, w_out)
    w_mem_full = w_mem_all.reshape(D_MODEL, 2 * C_W)
    w_out_full = w_out_all.reshape(D_MODEL, D_MODEL)
    mn, mk, mv = _memkv_fwd(mem, mem_g, w_mem_full)

    a_cfg = dict(dil=1, heads=A_HEADS, max_dist=BLOCK - 1, nq=ATTN_BLOCKS_PER_STEP)
    b_cfgs = {dil: dict(dil=dil, heads=B_HEADS, max_dist=win // dil, nq=ATTN_BLOCKS_PER_STEP)
              for win, dil in B_CONFIGS}
    oa, lse_a = _banded_fwd(qa, ka, va, sink, name="attn_a_fwd", **a_cfg)
    ob, lse_b = {}, {}
    for dil, cfg in b_cfgs.items():
        ob[dil], lse_b[dil] = _banded_fwd(*qkv_b[dil], None, name=f"attn_b{dil}_fwd", **cfg)
    oc, lse_c = _cross_fwd(qc, mk, mv)

    dh, d_gate, d_a, d_b, d_c, d_wout, st_mid = _mid(oa, lse_a, ob, lse_b, oc, lse_c, gate, x, target, w_out_full, post_g)

    dqc, dmk, dmv = _cross_bwd(qc, mk, mv, *d_c)
    d_wmem, st_mem = _memkv_bwd(mem, mem_g, mn, w_mem_full, dmk, dmv)
    bwd_nq = lambda dil: ATTN_BWD_BLOCKS_PER_STEP if dil == 1 else min(dil, ATTN_BWD_BLOCKS_PER_STEP)
    dqkv_b = {dil: _banded_bwd(*qkv_b[dil], *d_b[dil], None, name=f"attn_b{dil}_bwd", **{**cfg, "nq": bwd_nq(dil)})
              for dil, cfg in b_cfgs.items()}
    dqa, dka, dva, dsink, g_wmem, g_wout = _banded_bwd(
        qa, ka, va, *d_a, sink, name="attn_a_bwd", **{**a_cfg, "nq": bwd_nq(1)},
        reduce_scatter=(d_wmem.reshape(N_DEV, SHARD_ROWS, 2 * C_W), d_wout.reshape(N_DEV, SHARD_ROWS, D_MODEL)))

    grad_x, d_win, st_pre = _inproj_bwd(x, u, dh, pre_g, w_in_full, tabs, dqa, dka, dva, dqkv_b, dqc, d_gate)

    dsink_row = jnp.pad(dsink[0:1, :], ((0, 0), (0, D_MODEL - LANES)))
    stats = jnp.concatenate([st_pre[0:1], st_mem[0:1], st_mid[0:1], dsink_row, st_mid[1:2],
                             jnp.zeros((3, D_MODEL), F32)], axis=0)
    return grad_x, d_win, g_wmem, g_wout, stats


def _prep(x, pre_g, w_in, tm=1024):
    seq = x.shape[0]
    n_steps = seq // tm
    parts = 2
    rows = D_MODEL // parts
    relay_at = min(4, n_steps - 1)
    j = jnp.arange(LANES) % HEAD_DIM
    freq = (ROPE_THETA ** (-(2 * (j % (ROT_DIM // 2))).astype(F32) / ROT_DIM))[None, :]
    SIB, NB_X, NB_Y, RELAY, FWD = 0, 1, 2, 3, 4

    def body(x_ref, g_ref, f_ref, win_ref, u_ref, c_ref, up_ref, dn_ref, win_out, win_b,
             send_sems, recv_sems, local_sems):
        step = pl.program_id(0)
        px, py, pc = _mesh_pos()
        me, sibling = (px, py, pc), (px, py, 1 - pc)
        others = lambda core: ((1 - px, py, core), (px, 1 - py, core), (1 - px, 1 - py, core))
        x_nb, y_nb, diag = others(pc)
        relay_from = [x_nb, y_nb]
        relay_to = [y_nb, x_nb]

        def src(a):
            return win_b.at[pl.ds(rows * a, rows)]

        def slot(a, p):
            return win_out.at[_dev_index(p), pl.ds(rows * a, rows)]

        def copy(a, k, block, to, own=False):
            return pltpu.make_async_remote_copy(
                src_ref=src(a) if own else slot(a, block), dst_ref=slot(a, block),
                send_sem=send_sems.at[a, k], recv_sem=recv_sems.at[a, k], device_id=to, device_id_type=MESH_ID)

        def first_sends():
            return [copy(0, NB_X, me, x_nb, own=True), copy(1, NB_Y, me, y_nb, own=True),
                    copy(1, NB_X, me, x_nb, own=True), copy(0, NB_Y, me, y_nb, own=True),
                    copy(0, SIB, me, sibling, own=True), copy(1, SIB, me, sibling, own=True)]

        def relay(a):
            return copy(a, RELAY, relay_from[a], relay_to[a])

        def to_sibling(a, which):
            return copy(a, FWD + which, others(pc)[which], sibling)

        def local(a):
            return pltpu.make_async_copy(src(a), slot(a, me), local_sems.at[a])

        @pl.when(step == 0)
        def _():
            win_b[...] = win_ref[...].astype(BF16)
            for a in range(parts):
                local(a).start()
            for cp in first_sends():
                cp.start()

        @pl.when(step == relay_at)
        def _():
            for a in range(parts):
                copy(a, NB_X + a, relay_from[a], me).wait_recv()
                relay(a).start()
                to_sibling(a, a).start()

        xv = x_ref[...]
        r = lax.rsqrt(jnp.mean(xv * xv, axis=-1, keepdims=True) + RMS_EPS)
        u_ref[...] = ((xv * r) * g_ref[...]).astype(BF16)
        pos = (lax.broadcasted_iota(jnp.int32, (tm, LANES), 0) + step * tm).astype(F32)
        head_lane = lax.broadcasted_iota(jnp.int32, (tm, LANES), 1) % HEAD_DIM
        ang = pos * f_ref[...]
        cos, sin = jnp.cos(ang), jnp.sin(ang)
        half = ROT_DIM // 2
        c_ref[...] = jnp.where(head_lane < ROT_DIM, cos, 1.0)
        up_ref[...] = jnp.where((head_lane >= half) & (head_lane < ROT_DIM), sin, 0.0)
        dn_ref[...] = jnp.where(head_lane < half, -sin, 0.0)

        @pl.when(step == n_steps - 1)
        def _():
            copy(1, NB_X, x_nb, me).wait_recv()
            to_sibling(1, 0).start()
            copy(0, NB_Y, y_nb, me).wait_recv()
            to_sibling(0, 1).start()
            for a in range(parts):
                copy(a, RELAY, diag, me).wait_recv()
                to_sibling(a, 2).start()
            for a in range(parts):
                copy(a, SIB, sibling, me).wait_recv()
                for which in range(3):
                    copy(a, FWD + which, others(1 - pc)[which], me).wait_recv()
            for cp in first_sends():
                cp.wait_send()
            for a in range(parts):
                relay(a).wait_send()
                for which in range(3):
                    to_sibling(a, which).wait_send()
                local(a).wait()

    return pl.pallas_call(
        body, name="prep", grid=(n_steps,),
        in_specs=[_row(tm, D_MODEL), _full((1, D_MODEL)), _full((1, LANES)), _full(w_in.shape)],
        out_specs=[_row(tm, D_MODEL), _row(tm, LANES), _row(tm, LANES), _row(tm, LANES),
                   pl.BlockSpec(memory_space=pl.ANY)],
        out_shape=[jax.ShapeDtypeStruct((seq, D_MODEL), BF16)] + [jax.ShapeDtypeStruct((seq, LANES), F32)] * 3
        + [jax.ShapeDtypeStruct((N_DEV,) + w_in.shape, BF16)],
        scratch_shapes=[pltpu.VMEM(w_in.shape, BF16), pltpu.SemaphoreType.DMA((parts, FWD + 3)),
                        pltpu.SemaphoreType.DMA((parts, FWD + 3)), pltpu.SemaphoreType.DMA((parts,))],
        compiler_params=_params(dimension_semantics=("arbitrary",)),
    )(x, pre_g, freq, w_in)


def _exchange_grads(d_win, stats):
    def body(win, st, g_win, r_st, send_sems, recv_sems, local_sem, *scratch):
        exchange = _ReduceScatter((win,), (g_win,), scratch)
        exchange.start()
        pos = _mesh_pos()
        me = _dev_index(pos)
        own = pltpu.make_async_copy(st, r_st.at[me], local_sem)
        own.start()
        copies = []
        for s in range(1, N_DEV):
            peer = _xor_peer(pos, s)
            mk = lambda slot: pltpu.make_async_remote_copy(
                src_ref=st, dst_ref=r_st.at[slot], send_sem=send_sems.at[s], recv_sem=recv_sems.at[s],
                device_id=peer, device_id_type=MESH_ID)
            send, arrival = mk(me), mk(_dev_index(peer))
            send.start()
            copies.append((send, arrival))
        exchange.send_chip_sums()
        exchange.finish()
        for send, arrival in copies:
            arrival.wait_recv()
            send.wait_send()
        own.wait()

    hbm = pl.BlockSpec(memory_space=pl.ANY)
    shard = d_win.shape[1:]
    return pl.pallas_call(
        body, name="exchange_grads", in_specs=[hbm, hbm],
        out_specs=[pl.BlockSpec(memory_space=pltpu.VMEM), hbm],
        out_shape=[jax.ShapeDtypeStruct(shard, F32), jax.ShapeDtypeStruct((N_DEV,) + stats.shape, F32)],
        scratch_shapes=[pltpu.SemaphoreType.DMA((N_DEV,)), pltpu.SemaphoreType.DMA((N_DEV,)), pltpu.SemaphoreType.DMA(())]
        + _ReduceScatter.scratch_shapes([shard]),
        compiler_params=_params(),
    )(d_win, stats)


WEIGHT_ORDER = ("pre_norm", "w_in", "sink_a", "mem_norm", "w_mem_kv", "w_out", "post_norm")


def _adamw_all(grads, r_stats, weights, moments_m, moments_v):
    n = len(WEIGHT_ORDER)
    stat_row = {"pre_norm": 0, "mem_norm": 1, "post_norm": 2, "sink_a": 3}

    def body(*refs):
        gw_in, gw_mem, gw_out, st_ref = refs[0:4]
        w_refs, m_refs, v_refs = (dict(zip(WEIGHT_ORDER, refs[4 + n * t:4 + n * (t + 1)])) for t in range(3))
        loss_ref = refs[4 + 3 * n]
        outs = refs[5 + 3 * n:]
        g_small = st_ref[0]
        for s in range(1, N_DEV):
            g_small = g_small + st_ref[s]
        loss_ref[...] = g_small[4:5, 0:1]
        big = {"w_in": gw_in, "w_mem_kv": gw_mem, "w_out": gw_out}
        for i, name in enumerate(WEIGHT_ORDER):
            if name in big:
                g = big[name][...]
                at = lambda ref: ref[0]
            else:
                width = w_refs[name].shape[-1]
                g = g_small[stat_row[name]:stat_row[name] + 1, 0:width]
                at = lambda ref: ref[...]
            m2 = ADAM_B1 * at(m_refs[name]) + (1.0 - ADAM_B1) * g
            v2 = ADAM_B2 * at(v_refs[name]) + (1.0 - ADAM_B2) * (g * g)
            m_hat = m2 / (1.0 - ADAM_B1 ** ADAM_STEP)
            v_hat = v2 / (1.0 - ADAM_B2 ** ADAM_STEP)
            delta = -ADAM_LR * (m_hat / (jnp.sqrt(v_hat) + ADAM_EPS) + ADAM_WD * at(w_refs[name]))
            for kind, val in enumerate((g, delta, m2, v2)):
                out = outs[kind * n + i]
                if name in big:
                    out[0] = val
                else:
                    out[...] = val

    shapes = [weights[name].shape for name in WEIGHT_ORDER]
    res = pl.pallas_call(
        body, name="adamw_all",
        out_shape=[jax.ShapeDtypeStruct((1, 1), F32)] + [jax.ShapeDtypeStruct(sh, F32) for sh in shapes] * 4,
        compiler_params=_params(),
    )(grads["w_in"], grads["w_mem_kv"], grads["w_out"], r_stats,
      *[weights[k] for k in WEIGHT_ORDER], *[moments_m[k] for k in WEIGHT_ORDER], *[moments_v[k] for k in WEIGHT_ORDER])
    return res[0].reshape(()), res[1:]


def kernel(x, mem, pre_norm, w_in, sink_a, mem_norm, w_mem_kv, w_out, post_norm, loss_target, m_pre_norm, m_w_in, m_sink_a, m_mem_norm, m_w_mem_kv, m_w_out, m_post_norm, v_pre_norm, v_w_in, v_sink_a, v_mem_norm, v_w_mem_kv, v_w_out, v_post_norm):
    sink = jnp.pad(sink_a[0], (0, 8 - A_HEADS))
    grad_x, d_win, g_wmem, g_wout, stats = _local_step(
        x[0], mem[0], pre_norm, w_in[0], sink, mem_norm, w_mem_kv[0], w_out[0], post_norm, loss_target[0])
    g_win, r_stats = _exchange_grads(d_win, stats)
    weights = dict(pre_norm=pre_norm, w_in=w_in, sink_a=sink_a, mem_norm=mem_norm, w_mem_kv=w_mem_kv, w_out=w_out,
                   post_norm=post_norm)
    moments_m = dict(pre_norm=m_pre_norm, w_in=m_w_in, sink_a=m_sink_a, mem_norm=m_mem_norm, w_mem_kv=m_w_mem_kv,
                     w_out=m_w_out, post_norm=m_post_norm)
    moments_v = dict(pre_norm=v_pre_norm, w_in=v_w_in, sink_a=v_sink_a, mem_norm=v_mem_norm, w_mem_kv=v_w_mem_kv,
                     w_out=v_w_out, post_norm=v_post_norm)
    loss, rest = _adamw_all(dict(w_in=g_win, w_mem_kv=g_wmem, w_out=g_wout), r_stats, weights, moments_m, moments_v)
    return (loss, grad_x[None], *rest)
```

```python
import jax
import jax.numpy as jnp
from jax import lax
from jax.experimental import pallas as pl
from jax.experimental.pallas import tpu as pltpu

F32 = jnp.float32
BF16 = jnp.bfloat16

D_MODEL = 1024
HEAD_DIM = 64
ROT_DIM = 16
ROPE_THETA = 500000.0
BLOCK = 128
LANES = 128
N_MEM = 256
RMS_EPS = 1e-6
SCALE = HEAD_DIM ** -0.5
A_HEADS = 6
B_HEADS = 6
C_HEADS = 4
A_W, A_KV_W, B_W, C_W = 384, 128, 384, 256
_IN_PIECES = (("qa", A_W), ("ka", A_KV_W), ("va", A_KV_W), ("ga", A_W), ("qb", B_W), ("kb", B_W), ("vb", B_W),
              ("gb", B_W), ("qc", C_W), ("gc", C_W))
COLS, D_IN = {}, 0
for _name, _width in _IN_PIECES:
    COLS[_name] = (D_IN, D_IN + _width)
    D_IN += _width
N_DEV = 8
SHARD_IN = D_IN // N_DEV
SHARD_ROWS = D_MODEL // N_DEV
B_CONFIGS = ((128, 1), (512, 4), (2048, 16))
DILS = (4, 16)
NEG = -1e30
ATTN_BLOCKS_PER_STEP = 4
ATTN_BWD_BLOCKS_PER_STEP = 8
DELTA_LANE = 64
VMEM_LIMIT = 56 * 1024 * 1024

ADAM_LR, ADAM_B1, ADAM_B2, ADAM_EPS, ADAM_WD, ADAM_STEP = 0.001, 0.9, 0.999, 1e-08, 0.01, 10
MESH_ID = pl.DeviceIdType.MESH


def _params(**kw):
    return pltpu.CompilerParams(vmem_limit_bytes=VMEM_LIMIT, **kw)


def _full(shape):
    n = len(shape)
    return pl.BlockSpec(shape, lambda *_: (0,) * n)


def _row(tm, w):
    return pl.BlockSpec((tm, w), lambda i: (i, 0))


def _mesh_pos():
    return lax.axis_index("x"), lax.axis_index("y"), lax.axis_index("c")


def _dev_index(pos):
    return 4 * pos[0] + 2 * pos[1] + pos[2]


def _xor_peer(pos, s):
    x, y, c = pos
    return (1 - x if s & 4 else x, 1 - y if s & 2 else y, 1 - c if s & 1 else c)


def _perm_view(a, dil):
    return a.reshape(a.shape[0] // (BLOCK * dil), dil, BLOCK, a.shape[1])


def _perm_spec(tm, dil, w):
    chunk = BLOCK * dil
    if tm >= chunk:
        return pl.BlockSpec((tm // chunk, dil, BLOCK, w), lambda i: (i, 0, 0, 0))
    per = chunk // tm
    return pl.BlockSpec((1, dil, tm // dil, w), lambda i: (i // per, 0, i % per, 0))


def _put(scr, val):
    for c in range(val.shape[1] // LANES):
        scr[c] = val[:, LANES * c:LANES * (c + 1)]


def _get(scr):
    n = scr.shape[0]
    return scr[0] if n == 1 else jnp.concatenate([scr[c] for c in range(n)], axis=1)


def _get_class(scr, r, dil):
    n, rows = scr.shape[0], scr.shape[1]
    parts = [scr.at[c][pl.ds(r, rows // dil, stride=dil), :] for c in range(n)]
    return parts[0] if n == 1 else jnp.concatenate(parts, axis=1)


def _store_permuted(scr, out_ref, dil, dtype):
    for r in range(dil):
        out_ref[0, r] = _get_class(scr, r, dil).astype(dtype)


def _load_permuted(in_ref, scr, dil):
    n, rows = scr.shape[0], scr.shape[1]
    for r in range(dil):
        val = in_ref[0, r].astype(F32)
        for c in range(n):
            scr.at[c][pl.ds(r, rows // dil, stride=dil), :] = val[:, LANES * c:LANES * (c + 1)]
    return _get(scr)


def _rotate128(t, c, up, dn):
    half = ROT_DIM // 2
    return t * c + pltpu.roll(t, half, 1) * up + pltpu.roll(t, LANES - half, 1) * dn


def _rotate(t, c, up, dn):
    outs = [_rotate128(t[:, LANES * j:LANES * (j + 1)], c, up, dn) for j in range(t.shape[1] // LANES)]
    return outs[0] if len(outs) == 1 else jnp.concatenate(outs, axis=1)


def _per_query_head(kv):
    lane = lax.broadcasted_iota(jnp.int32, kv.shape, 1)
    other = pltpu.roll(kv, HEAD_DIM, 1)
    return jnp.concatenate([jnp.where(lane < HEAD_DIM, kv, other), kv, jnp.where(lane < HEAD_DIM, other, kv)], axis=1)


def _per_kv_head(d):
    s0, s1, s2 = (d[:, LANES * p:LANES * (p + 1)] for p in range(3))
    lane = lax.broadcasted_iota(jnp.int32, s0.shape, 1)
    return jnp.where(lane < HEAD_DIM, s0 + pltpu.roll(s0, HEAD_DIM, 1) + s1, s1 + s2 + pltpu.roll(s2, HEAD_DIM, 1))


def _w_in_scratch():
    return [pltpu.VMEM((D_MODEL, D_IN), BF16), pltpu.SemaphoreType.DMA((N_DEV,))]


def _stage_w_in(w_hbm, w_scr, sems):
    @pl.when(pl.program_id(0) == 0)
    def _():
        copies = [pltpu.make_async_copy(w_hbm.at[k], w_scr.at[:, pl.ds(SHARD_IN * k, SHARD_IN)], sems.at[k])
                  for k in range(N_DEV)]
        for cp in copies:
            cp.start()
        for cp in copies:
            cp.wait()


def _inproj(u, w_in_full, tabs, w_mem, w_out, tm=1024):
    seq = u.shape[0]
    n_chunk = D_IN // LANES
    n_steps = seq // tm

    def body(u_ref, w_hbm, c_ref, up_ref, dn_ref, wm_ref, wo_ref, qa_ref, ka_ref, va_ref,
             qb1_ref, kb1_ref, vb1_ref, qb4_ref, kb4_ref, vb4_ref, qb16_ref, kb16_ref, vb16_ref,
             qc_ref, gate_ref, wm_all, wo_all, proj, w_scr, w_sems, wm_b, wo_b, send_sems, recv_sems, local_sems):
        step = pl.program_id(0)
        shards, gathered = (wm_b, wo_b), (wm_all, wo_all)

        def gather_copies(arriving):
            pos = _mesh_pos()
            me = _dev_index(pos)
            local = [] if arriving else [
                pltpu.make_async_copy(shards[a], gathered[a].at[me], local_sems.at[a]) for a in range(2)]
            remote = []
            for s in range(1, N_DEV):
                peer = _xor_peer(pos, s)
                for a in range(2):
                    remote.append(pltpu.make_async_remote_copy(
                        src_ref=shards[a], dst_ref=gathered[a].at[_dev_index(peer) if arriving else me],
                        send_sem=send_sems.at[a, s], recv_sem=recv_sems.at[a, s], device_id=peer,
                        device_id_type=MESH_ID))
            return local, remote

        @pl.when(step == 0)
        def _():
            wm_b[...] = wm_ref[...].astype(BF16)
            wo_b[...] = wo_ref[...].astype(BF16)
            local, sends = gather_copies(arriving=False)
            for cp in local + sends:
                cp.start()

        _stage_w_in(w_hbm, w_scr, w_sems)
        u = u_ref[...]
        for n0 in range(0, D_IN, D_MODEL):
            acc = jnp.dot(u, w_scr[:, n0:n0 + D_MODEL], preferred_element_type=F32)
            for c3 in range(D_MODEL // LANES):
                proj[n0 // LANES + c3] = acc[:, LANES * c3:LANES * (c3 + 1)]
        c, up, dn = c_ref[...], up_ref[...], dn_ref[...]

        def chunks_of(piece):
            lo, hi = COLS[piece]
            return range(lo // LANES, hi // LANES)

        def cols(piece, rot=False, scale=None):
            parts = []
            for ch in chunks_of(piece):
                t = proj[ch]
                if rot:
                    t = _rotate128(t, c, up, dn)
                if scale is not None:
                    t = t * scale
                parts.append(t)
            return parts[0] if len(parts) == 1 else jnp.concatenate(parts, axis=1)

        qa_ref[...] = cols("qa", True, SCALE).astype(BF16)
        ka_ref[...] = _per_query_head(cols("ka", True)).astype(BF16)
        va_ref[...] = _per_query_head(cols("va")).astype(BF16)
        gate_ref[:, 0:A_W] = cols("ga").astype(BF16)
        gate_ref[:, A_W:A_W + B_W] = cols("gb").astype(BF16)
        gate_ref[:, A_W + B_W:D_MODEL] = cols("gc").astype(BF16)
        qc_ref[...] = cols("qc", False, SCALE).astype(BF16)
        for ch in chunks_of("qb"):
            proj[ch] = _rotate128(proj[ch], c, up, dn) * SCALE
        for ch in chunks_of("kb"):
            proj[ch] = _rotate128(proj[ch], c, up, dn)
        for piece, nat, p4, p16 in (("qb", qb1_ref, qb4_ref, qb16_ref), ("kb", kb1_ref, kb4_ref, kb16_ref),
                                    ("vb", vb1_ref, vb4_ref, vb16_ref)):
            chunks = chunks_of(piece)
            nat[...] = jnp.concatenate([proj[ch] for ch in chunks], axis=1).astype(BF16)
            for dil, ref in ((4, p4), (16, p16)):
                span = min(tm, BLOCK * dil)
                for cc in range(tm // span):
                    for rr in range(dil):
                        ref[cc, rr] = jnp.concatenate(
                            [proj.at[ch][pl.ds(cc * span + rr, span // dil, stride=dil), :] for ch in chunks],
                            axis=1).astype(BF16)

        @pl.when(step == n_steps - 1)
        def _():
            for cp in gather_copies(arriving=True)[1]:
                cp.wait_recv()
            local, sends = gather_copies(arriving=False)
            for cp in sends:
                cp.wait_send()
            for cp in local:
                cp.wait()

    nat_w = (A_W, A_W, A_W, B_W, B_W, B_W)
    out_specs = [_row(tm, w) for w in nat_w]
    out_shape = [jax.ShapeDtypeStruct((seq, w), BF16) for w in nat_w]
    for dil in DILS:
        out_specs += [_perm_spec(tm, dil, B_W)] * 3
        out_shape += [jax.ShapeDtypeStruct((seq // (BLOCK * dil), dil, BLOCK, B_W), BF16)] * 3
    hbm = pl.BlockSpec(memory_space=pl.ANY)
    out_specs += [_row(tm, C_W), _row(tm, D_MODEL), hbm, hbm]
    out_shape += [jax.ShapeDtypeStruct((seq, C_W), BF16), jax.ShapeDtypeStruct((seq, D_MODEL), BF16),
                  jax.ShapeDtypeStruct((N_DEV,) + w_mem.shape, BF16), jax.ShapeDtypeStruct((N_DEV,) + w_out.shape, BF16)]
    res = pl.pallas_call(
        body, name="inproj", grid=(n_steps,),
        in_specs=[_row(tm, D_MODEL), hbm, _row(tm, LANES), _row(tm, LANES), _row(tm, LANES),
                  _full(w_mem.shape), _full(w_out.shape)],
        out_specs=out_specs, out_shape=out_shape,
        scratch_shapes=[pltpu.VMEM((n_chunk, tm, LANES), F32)] + _w_in_scratch()
        + [pltpu.VMEM(w_mem.shape, BF16), pltpu.VMEM(w_out.shape, BF16), pltpu.SemaphoreType.DMA((2, N_DEV)),
           pltpu.SemaphoreType.DMA((2, N_DEV)), pltpu.SemaphoreType.DMA((2,))],
        compiler_params=_params(dimension_semantics=("arbitrary",)),
    )(u, w_in_full, *tabs, w_mem, w_out)
    qa, ka, va = res[0:3]
    qkv_b = {1: res[3:6], 4: [t.reshape(seq, B_W) for t in res[6:9]], 16: [t.reshape(seq, B_W) for t in res[9:12]]}
    return qa, ka, va, qkv_b, res[12], res[13], res[14], res[15]


def _memkv_fwd(mem, mem_g, w_mem_full):
    def body(mem_ref, g_ref, w_ref, mn_ref, mk_ref, mv_ref):
        mv_ = mem_ref[...]
        r = lax.rsqrt(jnp.mean(mv_ * mv_, axis=-1, keepdims=True) + RMS_EPS)
        mn = ((mv_ * r) * g_ref[...]).astype(BF16)
        mn_ref[...] = mn
        mkv = jnp.dot(mn, w_ref[...], preferred_element_type=F32)
        mk_ref[...] = mkv[:, 0:C_W].astype(BF16)
        mv_ref[...] = mkv[:, C_W:2 * C_W].astype(BF16)

    return pl.pallas_call(
        body, name="memkv_fwd",
        out_shape=[jax.ShapeDtypeStruct((N_MEM, D_MODEL), BF16),
                   jax.ShapeDtypeStruct((N_MEM, C_W), BF16), jax.ShapeDtypeStruct((N_MEM, C_W), BF16)],
        compiler_params=_params(),
    )(mem, mem_g, w_mem_full)


def _memkv_bwd(mem, mem_g, mn, w_mem_full, dmk, dmv):
    def body(mem_ref, g_ref, mn_ref, w_ref, dmk_ref, dmv_ref, dw_ref, st_ref):
        dmkv = jnp.concatenate([dmk_ref[...], dmv_ref[...]], axis=1).astype(BF16)
        dw_ref[...] = lax.dot_general(mn_ref[...], dmkv, (((0,), (0,)), ((), ())), preferred_element_type=F32)
        dmn = lax.dot_general(dmkv, w_ref[...], (((1,), (1,)), ((), ())), preferred_element_type=F32)
        mv_ = mem_ref[...]
        r = lax.rsqrt(jnp.mean(mv_ * mv_, axis=-1, keepdims=True) + RMS_EPS)
        st_ref[...] = jnp.zeros_like(st_ref)
        st_ref[0:1, :] = jnp.sum(dmn * (mv_ * r), axis=0, keepdims=True)

    return pl.pallas_call(
        body, name="memkv_bwd",
        out_shape=[jax.ShapeDtypeStruct((D_MODEL, 2 * C_W), F32), jax.ShapeDtypeStruct((8, D_MODEL), F32)],
        compiler_params=_params(),
    )(mem, mem_g, mn, w_mem_full, dmk, dmv)


def _band_mask(has_prev, max_dist):
    qi = lax.broadcasted_iota(jnp.int32, (BLOCK, 2 * BLOCK), 0)
    kj = lax.broadcasted_iota(jnp.int32, (BLOCK, 2 * BLOCK), 1)
    dist = qi + BLOCK - kj
    return (dist >= 0) & (dist <= max_dist) & ((kj >= BLOCK) | has_prev)


_NT = (((1,), (1,)), ((), ()))
_TN = (((0,), (0,)), ((), ()))


def _head_only(val, h):
    slab = _slabs_of(val)(h)
    lane = lax.broadcasted_iota(jnp.int32, slab.shape, 1)
    keep = (lane < HEAD_DIM) if h % 2 == 0 else (lane >= HEAD_DIM)
    return jnp.where(keep, slab, jnp.zeros((), slab.dtype))


def _slabs_of(val):
    return lambda h: val[:, LANES * (h // 2):LANES * (h // 2 + 1)]


class _BandSteps:
    def __init__(self, seq, dil, nq):
        self.nq, self.rows, self.consecutive = nq, nq * BLOCK, dil == 1
        nb = seq // dil // BLOCK
        if self.consecutive:
            assert nb % nq == 0
            self.outer, self.inner, self.stride = 1, nb // nq, 1
        else:
            assert dil % nq == 0
            self.outer, self.inner, self.stride = dil // nq, nb, dil // nq

    def own(self, w, clamp=False):
        cur = (lambda i: jnp.minimum(i, self.inner - 1)) if clamp else (lambda i: i)
        return pl.BlockSpec((self.rows, w), lambda r, i: (cur(i) * self.stride + r, 0))

    def prev(self, w, clamp=False):
        cur = (lambda i: jnp.minimum(i, self.inner - 1)) if clamp else (lambda i: i)
        if self.consecutive:
            return pl.BlockSpec((BLOCK, w), lambda r, i: (jnp.maximum(cur(i) * self.nq - 1, 0), 0))
        return pl.BlockSpec((self.rows, w), lambda r, i: (jnp.maximum(cur(i) - 1, 0) * self.stride + r, 0))

    def late(self, w):
        return pl.BlockSpec((self.rows, w), lambda r, i: (jnp.maximum(i - 1, 0) * self.stride + r, 0))

    def rows_of(self, j):
        return slice(BLOCK * j, BLOCK * (j + 1))

    def keys(self, p_ref, c_ref, j):
        if not self.consecutive:
            before = p_ref[self.rows_of(j), :]
        elif j == 0:
            before = p_ref[...]
        else:
            before = c_ref[self.rows_of(j - 1), :]
        return jnp.concatenate([before, c_ref[self.rows_of(j), :]], axis=0)

    def has_prev(self, i, j):
        return True if (self.consecutive and j > 0) else (i > 0)


def _banded_fwd(q, k, v, sink, *, dil, heads, max_dist, nq, name):
    seq = q.shape[0]
    qw = kw = heads * HEAD_DIM
    steps = _BandSteps(seq, dil, nq)

    def body(*refs):
        if sink is not None:
            sink_ref, refs = refs[0], refs[1:]
        q_ref, kp_ref, kc_ref, vp_ref, vc_ref, o_ref, lse_ref, s_scr, p_scr = refs
        i = pl.program_id(1)
        lane = lax.broadcasted_iota(jnp.int32, (BLOCK, LANES), 1)
        k_of = [_slabs_of(steps.keys(kp_ref, kc_ref, j)) for j in range(nq)]
        v_of = [_slabs_of(steps.keys(vp_ref, vc_ref, j)) for j in range(nq)]
        for j in range(nq):
            qv = q_ref[steps.rows_of(j), :]
            for h in range(heads):
                s_scr[j * heads + h] = lax.dot_general(_head_only(qv, h), k_of[j](h), _NT, preferred_element_type=F32)
        ls = {}
        for j in range(nq):
            valid = _band_mask(steps.has_prev(i, j), max_dist)
            lse_tile = jnp.zeros((BLOCK, LANES), F32)
            for h in range(heads):
                s = jnp.where(valid, s_scr[j * heads + h], NEG)
                m = jnp.max(s, axis=-1, keepdims=True)
                if sink is not None:
                    sk = sink_ref[h]
                    m = jnp.maximum(m, sk)
                p = jnp.exp(s - m)
                l = jnp.sum(p, axis=-1, keepdims=True)
                if sink is not None:
                    l = l + jnp.exp(sk - m)
                p_scr[j * heads + h] = p.astype(BF16)
                ls[j, h] = l
                lse_tile = jnp.where(lane == h, m + jnp.log(l), lse_tile)
            lse_ref[steps.rows_of(j), :] = lse_tile
        for j in range(nq):
            for pr in range(heads // 2):
                he, ho = 2 * pr, 2 * pr + 1
                even = jnp.dot(p_scr[j * heads + he], v_of[j](he), preferred_element_type=F32) / ls[j, he]
                odd = jnp.dot(p_scr[j * heads + ho], v_of[j](ho), preferred_element_type=F32) / ls[j, ho]
                o_ref[steps.rows_of(j), LANES * pr:LANES * (pr + 1)] = jnp.where(lane < HEAD_DIM, even, odd).astype(BF16)

    in_specs = [steps.own(qw), steps.prev(kw), steps.own(kw), steps.prev(kw), steps.own(kw)]
    args = [q, k, k, v, v]
    if sink is not None:
        in_specs = [pl.BlockSpec(memory_space=pltpu.SMEM)] + in_specs
        args = [sink] + args
    return pl.pallas_call(
        body, name=name, grid=(steps.outer, steps.inner), in_specs=in_specs,
        out_specs=[steps.own(qw), steps.own(LANES)],
        out_shape=[jax.ShapeDtypeStruct((seq, qw), BF16), jax.ShapeDtypeStruct((seq, LANES), F32)],
        scratch_shapes=[pltpu.VMEM((nq * heads, BLOCK, 2 * BLOCK), F32), pltpu.VMEM((nq * heads, BLOCK, 2 * BLOCK), BF16)],
        compiler_params=_params(dimension_semantics=("arbitrary", "arbitrary")),
    )(*args)


def _banded_bwd(q, k, v, d_out, stat, sink, *, dil, heads, max_dist, nq, name, reduce_scatter=()):
    seq = q.shape[0]
    qw = kw = heads * HEAD_DIM
    steps = _BandSteps(seq, dil, nq)
    n_rs = len(reduce_scatter)
    n_in = 7 + n_rs
    n_flat = steps.outer * (steps.inner + 1)

    def body(*refs):
        refs = list(refs)
        sink_ref = refs.pop(0) if sink is not None else None
        (q_ref, kp_ref, kc_ref, vp_ref, vc_ref, do_ref, st_ref), partials = refs[:7], refs[7:n_in]
        refs = refs[n_in:]
        dsink_ref = refs.pop(0) if sink is not None else None
        (dq_ref, dk_ref, dv_ref), sums = refs[:3], refs[3:3 + n_rs]
        kcar, vcar, s_scr, dp_scr, p_scr, ds_scr = refs[3 + n_rs:9 + n_rs]
        r, i = pl.program_id(0), pl.program_id(1)
        if n_rs:
            exchange = _ReduceScatter(tuple(partials), tuple(sums), refs[9 + n_rs:])
            flat = r * (steps.inner + 1) + i

            @pl.when(flat == 0)
            def _():
                exchange.start()

            @pl.when(flat == min(2, n_flat - 1))
            def _():
                exchange.send_chip_sums()

        @pl.when(i == 0)
        def _():
            kcar[...] = jnp.zeros_like(kcar)
            vcar[...] = jnp.zeros_like(vcar)

        if sink is not None:
            @pl.when((i == 0) & (r == 0))
            def _():
                dsink_ref[...] = jnp.zeros_like(dsink_ref)

        @pl.when(i < steps.inner)
        def _():
            lane = lax.broadcasted_iota(jnp.int32, (1, LANES), 1)
            lane_q = lax.broadcasted_iota(jnp.int32, (BLOCK, LANES), 1)
            k_of = [_slabs_of(steps.keys(kp_ref, kc_ref, j)) for j in range(nq)]
            v_of = [_slabs_of(steps.keys(vp_ref, vc_ref, j)) for j in range(nq)]
            qms, doms = {}, {}
            for j in range(nq):
                qv, dov = q_ref[steps.rows_of(j), :], do_ref[steps.rows_of(j), :]
                for h in range(heads):
                    qms[j, h], doms[j, h] = _head_only(qv, h), _head_only(dov, h)
                    s_scr[j * heads + h] = lax.dot_general(qms[j, h], k_of[j](h), _NT, preferred_element_type=F32)
                    dp_scr[j * heads + h] = lax.dot_general(doms[j, h], v_of[j](h), _NT, preferred_element_type=F32)
            dsink_row = jnp.zeros((1, LANES), F32)
            if sink is not None:
                sink_row = jnp.zeros((1, LANES), F32)
                for h in range(heads):
                    sink_row = jnp.where(lane == h, sink_ref[h], sink_row)
            for j in range(nq):
                st = st_ref[steps.rows_of(j), :]
                valid = _band_mask(steps.has_prev(i, j), max_dist)
                for h in range(heads):
                    lse_h = st[:, h:h + 1]
                    delta = st[:, DELTA_LANE + h:DELTA_LANE + h + 1]
                    p = jnp.where(valid, jnp.exp(s_scr[j * heads + h] - lse_h), 0.0)
                    p_scr[j * heads + h] = p.astype(BF16)
                    ds_scr[j * heads + h] = (p * (dp_scr[j * heads + h] - delta)).astype(BF16)
                if sink is not None:
                    term = -jnp.exp(sink_row - st) * pltpu.roll(st, LANES - DELTA_LANE, 1)
                    dsink_row = dsink_row + jnp.sum(jnp.where(lane_q < heads, term, 0.0), axis=0, keepdims=True)
            for j in range(nq):
                for pr in range(heads // 2):
                    he, ho = 2 * pr, 2 * pr + 1
                    even = jnp.dot(ds_scr[j * heads + he], k_of[j](he), preferred_element_type=F32)
                    odd = jnp.dot(ds_scr[j * heads + ho], k_of[j](ho), preferred_element_type=F32)
                    dq_ref[steps.rows_of(j), LANES * pr:LANES * (pr + 1)] = (
                        jnp.where(lane_q < HEAD_DIM, even, odd).astype(BF16))
            if steps.consecutive:
                dk_ref[...] = kcar[...].astype(BF16)
                dv_ref[...] = vcar[...].astype(BF16)
            for j in range(nq):
                for slab in range(kw // LANES):
                    he, ho = j * heads + 2 * slab, j * heads + 2 * slab + 1
                    dk_j = (lax.dot_general(ds_scr[he], qms[j, 2 * slab], _TN, preferred_element_type=F32)
                            + lax.dot_general(ds_scr[ho], qms[j, 2 * slab + 1], _TN, preferred_element_type=F32))
                    dv_j = (lax.dot_general(p_scr[he], doms[j, 2 * slab], _TN, preferred_element_type=F32)
                            + lax.dot_general(p_scr[ho], doms[j, 2 * slab + 1], _TN, preferred_element_type=F32))
                    sl = slice(LANES * slab, LANES * (slab + 1))
                    own_rows = steps.rows_of(j)
                    if not steps.consecutive:
                        dk_ref[own_rows, sl] = (kcar[own_rows, sl] + dk_j[0:BLOCK]).astype(BF16)
                        dv_ref[own_rows, sl] = (vcar[own_rows, sl] + dv_j[0:BLOCK]).astype(BF16)
                    elif j == 0:
                        last = steps.rows_of(nq - 1)
                        dk_ref[last, sl] = (kcar[last, sl] + dk_j[0:BLOCK]).astype(BF16)
                        dv_ref[last, sl] = (vcar[last, sl] + dv_j[0:BLOCK]).astype(BF16)
                    else:
                        before = steps.rows_of(j - 1)
                        kcar[before, sl] += dk_j[0:BLOCK]
                        vcar[before, sl] += dv_j[0:BLOCK]
                    kcar[own_rows, sl] = dk_j[BLOCK:2 * BLOCK]
                    vcar[own_rows, sl] = dv_j[BLOCK:2 * BLOCK]
            if sink is not None:
                dsink_ref[0:1, :] += dsink_row

        @pl.when(i == steps.inner)
        def _():
            dk_ref[...] = kcar[...].astype(BF16)
            dv_ref[...] = vcar[...].astype(BF16)

        if n_rs:
            @pl.when(flat == n_flat - 1)
            def _():
                exchange.finish()

    own, prev = (lambda w: steps.own(w, clamp=True)), (lambda w: steps.prev(w, clamp=True))
    rs_shapes = [t.shape[1:] for t in reduce_scatter]
    in_specs = ([own(qw), prev(kw), own(kw), prev(kw), own(kw), own(qw), own(LANES)]
                + [pl.BlockSpec(memory_space=pl.ANY)] * n_rs)
    args = [q, k, k, v, v, d_out, stat, *reduce_scatter]
    out_specs = [own(qw), steps.late(kw), steps.late(kw)] + [_full(s) for s in rs_shapes]
    out_shape = [jax.ShapeDtypeStruct((seq, qw), BF16), jax.ShapeDtypeStruct((seq, kw), BF16),
                 jax.ShapeDtypeStruct((seq, kw), BF16)] + [jax.ShapeDtypeStruct(s, F32) for s in rs_shapes]
    if sink is not None:
        in_specs = [pl.BlockSpec(memory_space=pltpu.SMEM)] + in_specs
        args = [sink] + args
        out_specs = [_full((8, LANES))] + out_specs
        out_shape = [jax.ShapeDtypeStruct((8, LANES), F32)] + out_shape
    n_hb = nq * heads
    res = pl.pallas_call(
        body, name=name, grid=(steps.outer, steps.inner + 1), in_specs=in_specs, out_specs=out_specs,
        out_shape=out_shape,
        scratch_shapes=[pltpu.VMEM((steps.rows, kw), F32), pltpu.VMEM((steps.rows, kw), F32)]
        + [pltpu.VMEM((n_hb, BLOCK, 2 * BLOCK), F32)] * 2 + [pltpu.VMEM((n_hb, BLOCK, 2 * BLOCK), BF16)] * 2
        + (_ReduceScatter.scratch_shapes(rs_shapes) if n_rs else []),
        compiler_params=_params(dimension_semantics=("arbitrary", "arbitrary")),
    )(*args)
    if sink is not None:
        return (*res[1:4], res[0], *res[4:])
    return res


def _cross_fwd(q, mk, mv, tq=1024):
    seq = q.shape[0]

    def body(q_ref, mk_ref, mv_ref, o_ref, lse_ref, s_scr, p_scr):
        qv = q_ref[...]
        k_of, v_of = _slabs_of(mk_ref[...]), _slabs_of(mv_ref[...])
        lane = lax.broadcasted_iota(jnp.int32, (tq, LANES), 1)
        lse_tile = jnp.zeros((tq, LANES), F32)
        for h in range(C_HEADS):
            s_scr[h] = lax.dot_general(_head_only(qv, h), k_of(h), _NT, preferred_element_type=F32)
        ls = []
        for h in range(C_HEADS):
            s = s_scr[h]
            m = jnp.max(s, axis=-1, keepdims=True)
            p = jnp.exp(s - m)
            l = jnp.sum(p, axis=-1, keepdims=True)
            p_scr[h] = p.astype(BF16)
            ls.append(l)
            lse_tile = jnp.where(lane == h, m + jnp.log(l), lse_tile)
        for pr in range(C_HEADS // 2):
            even = jnp.dot(p_scr[2 * pr], v_of(2 * pr), preferred_element_type=F32) / ls[2 * pr]
            odd = jnp.dot(p_scr[2 * pr + 1], v_of(2 * pr + 1), preferred_element_type=F32) / ls[2 * pr + 1]
            o_ref[:, LANES * pr:LANES * (pr + 1)] = jnp.where(lane < HEAD_DIM, even, odd).astype(BF16)
        lse_ref[...] = lse_tile

    return pl.pallas_call(
        body, name="cross_fwd", grid=(seq // tq,),
        in_specs=[_row(tq, C_W), _full((N_MEM, C_W)), _full((N_MEM, C_W))],
        out_specs=[_row(tq, C_W), _row(tq, LANES)],
        out_shape=[jax.ShapeDtypeStruct((seq, C_W), BF16), jax.ShapeDtypeStruct((seq, LANES), F32)],
        scratch_shapes=[pltpu.VMEM((C_HEADS, tq, N_MEM), F32), pltpu.VMEM((C_HEADS, tq, N_MEM), BF16)],
        compiler_params=_params(dimension_semantics=("arbitrary",)),
    )(q, mk, mv)


def _cross_bwd(q, mk, mv, d_out, stat, tq=1024):
    seq = q.shape[0]

    def body(q_ref, mk_ref, mv_ref, do_ref, st_ref, dq_ref, dmk_ref, dmv_ref, s_scr, dp_scr, p_scr, ds_scr):
        @pl.when(pl.program_id(0) == 0)
        def _():
            dmk_ref[...] = jnp.zeros_like(dmk_ref)
            dmv_ref[...] = jnp.zeros_like(dmv_ref)

        qv, dov, st = q_ref[...], do_ref[...], st_ref[...]
        k_of, v_of = _slabs_of(mk_ref[...]), _slabs_of(mv_ref[...])
        qms = [_head_only(qv, h) for h in range(C_HEADS)]
        doms = [_head_only(dov, h) for h in range(C_HEADS)]
        for h in range(C_HEADS):
            s_scr[h] = lax.dot_general(qms[h], k_of(h), _NT, preferred_element_type=F32)
            dp_scr[h] = lax.dot_general(doms[h], v_of(h), _NT, preferred_element_type=F32)
        for h in range(C_HEADS):
            p = jnp.exp(s_scr[h] - st[:, h:h + 1])
            p_scr[h] = p.astype(BF16)
            ds_scr[h] = (p * (dp_scr[h] - st[:, DELTA_LANE + h:DELTA_LANE + h + 1])).astype(BF16)
        lane = lax.broadcasted_iota(jnp.int32, (tq, LANES), 1)
        for pr in range(C_HEADS // 2):
            sl = slice(LANES * pr, LANES * (pr + 1))
            even = jnp.dot(ds_scr[2 * pr], k_of(2 * pr), preferred_element_type=F32)
            odd = jnp.dot(ds_scr[2 * pr + 1], k_of(2 * pr + 1), preferred_element_type=F32)
            dq_ref[:, sl] = jnp.where(lane < HEAD_DIM, even, odd).astype(BF16)
            dmk_ref[:, sl] += (lax.dot_general(ds_scr[2 * pr], qms[2 * pr], _TN, preferred_element_type=F32)
                               + lax.dot_general(ds_scr[2 * pr + 1], qms[2 * pr + 1], _TN, preferred_element_type=F32))
            dmv_ref[:, sl] += (lax.dot_general(p_scr[2 * pr], doms[2 * pr], _TN, preferred_element_type=F32)
                               + lax.dot_general(p_scr[2 * pr + 1], doms[2 * pr + 1], _TN, preferred_element_type=F32))

    return pl.pallas_call(
        body, name="cross_bwd", grid=(seq // tq,),
        in_specs=[_row(tq, C_W), _full((N_MEM, C_W)), _full((N_MEM, C_W)), _row(tq, C_W), _row(tq, LANES)],
        out_specs=[_row(tq, C_W), _full((N_MEM, C_W)), _full((N_MEM, C_W))],
        out_shape=[jax.ShapeDtypeStruct((seq, C_W), BF16), jax.ShapeDtypeStruct((N_MEM, C_W), F32),
                   jax.ShapeDtypeStruct((N_MEM, C_W), F32)],
        scratch_shapes=[pltpu.VMEM((C_HEADS, tq, N_MEM), F32)] * 2 + [pltpu.VMEM((C_HEADS, tq, N_MEM), BF16)] * 2,
        compiler_params=_params(dimension_semantics=("arbitrary",)),
    )(q, mk, mv, d_out, stat)


def _per_head(tile, width):
    rows = tile.shape[0]
    lane = lax.broadcasted_iota(jnp.int32, (rows, LANES), 1)
    slabs = []
    for p in range(width // LANES):
        even = jnp.broadcast_to(tile[:, 2 * p:2 * p + 1], (rows, LANES))
        odd = jnp.broadcast_to(tile[:, 2 * p + 1:2 * p + 2], (rows, LANES))
        slabs.append(jnp.where(lane < HEAD_DIM, even, odd))
    return slabs[0] if len(slabs) == 1 else jnp.concatenate(slabs, axis=1)


def _with_delta(lse_tile, prod):
    rows = lse_tile.shape[0]
    lane = lax.broadcasted_iota(jnp.int32, (rows, LANES), 1)
    tile = lse_tile
    for p in range(prod.shape[1] // LANES):
        slab = prod[:, LANES * p:LANES * (p + 1)]
        even = jnp.sum(jnp.where(lane < HEAD_DIM, slab, 0.0), axis=-1, keepdims=True)
        odd = jnp.sum(jnp.where(lane >= HEAD_DIM, slab, 0.0), axis=-1, keepdims=True)
        tile = jnp.where(lane == DELTA_LANE + 2 * p, even, tile)
        tile = jnp.where(lane == DELTA_LANE + 2 * p + 1, odd, tile)
    return tile


def _mid(oa, lse_a, ob, lse_b, oc, lse_c, gate, x, target, w_out_full, post_g, tm=512):
    seq = x.shape[0]
    n_b = B_W // LANES

    def body(oa_ref, la_ref, b1_ref, l1_ref, b4_ref, l4_ref, b16_ref, l16_ref, oc_ref, lc_ref,
             gate_ref, x_ref, t_ref, w_ref, pg_ref,
             dh_ref, dg_ref, doa_ref, sa_ref, dob1_ref, sb1_ref, dob4_ref, sb4_ref, dob16_ref, sb16_ref,
             doc_ref, sc_ref, dw_ref, st_ref, scr_b4, scr_b16, scr_l4, scr_l16, scr_do, scr_sb):
        @pl.when(pl.program_id(0) == 0)
        def _():
            dw_ref[...] = jnp.zeros_like(dw_ref)
            st_ref[...] = jnp.zeros_like(st_ref)

        b1, l1 = b1_ref[...].astype(F32), l1_ref[...]
        b4, l4 = _load_permuted(b4_ref, scr_b4, 4), _load_permuted(l4_ref, scr_l4, 4)
        b16, l16 = _load_permuted(b16_ref, scr_b16, 16), _load_permuted(l16_ref, scr_l16, 16)
        lm = jnp.maximum(jnp.maximum(l1, l4), l16)
        e1, e4, e16 = jnp.exp(l1 - lm), jnp.exp(l4 - lm), jnp.exp(l16 - lm)
        den = e1 + e4 + e16
        lse_b_tile = lm + jnp.log(den)
        ob_v = _per_head(e1 / den, B_W) * b1 + _per_head(e4 / den, B_W) * b4 + _per_head(e16 / den, B_W) * b16
        o_all = jnp.concatenate([oa_ref[...].astype(F32), ob_v, oc_ref[...].astype(F32)], axis=1)
        g = gate_ref[...].astype(F32)
        sig = 1.0 / (1.0 + jnp.exp(-g))
        silu = g * sig
        y = (o_all * silu).astype(BF16)
        w = w_ref[...]
        z = jnp.dot(y, w, preferred_element_type=F32)
        rz = lax.rsqrt(jnp.mean(z * z, axis=-1, keepdims=True) + RMS_EPS)
        hn = z * rz
        pg = pg_ref[...]
        err = (x_ref[...] + hn * pg) - t_ref[...]
        loss = 0.5 * jnp.sum(jnp.mean(err * err, axis=-1, keepdims=True), axis=0, keepdims=True)
        dh = err * (1.0 / D_MODEL)
        dh_ref[...] = dh.astype(BF16)
        st_ref[0:1, :] += jnp.sum(dh * hn, axis=0, keepdims=True)
        st_ref[1:2, :] += jnp.broadcast_to(loss, (1, D_MODEL))
        dhn = dh * pg
        dz = (rz * (dhn - hn * jnp.mean(dhn * hn, axis=-1, keepdims=True))).astype(BF16)
        dy = lax.dot_general(dz, w, _NT, preferred_element_type=F32)
        dw_ref[...] += lax.dot_general(y, dz, _TN, preferred_element_type=F32)
        dg_ref[...] = (dy * o_all * (sig * (1.0 + g * (1.0 - sig)))).astype(BF16)
        d_o = (dy * silu).astype(BF16)
        prod = d_o.astype(F32) * o_all
        doa_ref[...] = d_o[:, 0:A_W]
        sa_ref[...] = _with_delta(la_ref[...], prod[:, 0:A_W])
        doc_ref[...] = d_o[:, A_W + B_W:D_MODEL]
        sc_ref[...] = _with_delta(lc_ref[...], prod[:, A_W + B_W:D_MODEL])
        d_ob = d_o[:, A_W:A_W + B_W]
        stat_b = _with_delta(lse_b_tile, prod[:, A_W:A_W + B_W])
        dob1_ref[...] = d_ob
        sb1_ref[...] = stat_b
        _put(scr_do, d_ob.astype(F32))
        _put(scr_sb, stat_b)
        _store_permuted(scr_do, dob4_ref, 4, BF16)
        _store_permuted(scr_sb, sb4_ref, 4, F32)
        _store_permuted(scr_do, dob16_ref, 16, BF16)
        _store_permuted(scr_sb, sb16_ref, 16, F32)

    p4 = lambda w: _perm_spec(tm, 4, w)
    p16 = lambda w: _perm_spec(tm, 16, w)
    in_specs = [_row(tm, A_W), _row(tm, LANES), _row(tm, B_W), _row(tm, LANES), p4(B_W), p4(LANES), p16(B_W), p16(LANES),
                _row(tm, C_W), _row(tm, LANES), _row(tm, D_MODEL), _row(tm, D_MODEL), _row(tm, D_MODEL),
                _full((D_MODEL, D_MODEL)), _full((1, D_MODEL))]
    sds = jax.ShapeDtypeStruct
    v4 = lambda w, dt: sds((seq // (BLOCK * 4), 4, BLOCK, w), dt)
    v16 = lambda w, dt: sds((seq // (BLOCK * 16), 16, BLOCK, w), dt)
    out_specs = [_row(tm, D_MODEL), _row(tm, D_MODEL), _row(tm, A_W), _row(tm, LANES), _row(tm, B_W), _row(tm, LANES),
                 p4(B_W), p4(LANES), p16(B_W), p16(LANES), _row(tm, C_W), _row(tm, LANES),
                 _full((D_MODEL, D_MODEL)), _full((8, D_MODEL))]
    out_shape = [sds((seq, D_MODEL), BF16), sds((seq, D_MODEL), BF16), sds((seq, A_W), BF16), sds((seq, LANES), F32),
                 sds((seq, B_W), BF16), sds((seq, LANES), F32), v4(B_W, BF16), v4(LANES, F32), v16(B_W, BF16),
                 v16(LANES, F32), sds((seq, C_W), BF16), sds((seq, LANES), F32),
                 sds((D_MODEL, D_MODEL), F32), sds((8, D_MODEL), F32)]
    res = pl.pallas_call(
        body, name="mid", grid=(seq // tm,), in_specs=in_specs, out_specs=out_specs, out_shape=out_shape,
        scratch_shapes=[pltpu.VMEM((n_b, tm, LANES), F32), pltpu.VMEM((n_b, tm, LANES), F32),
                        pltpu.VMEM((1, tm, LANES), F32), pltpu.VMEM((1, tm, LANES), F32),
                        pltpu.VMEM((n_b, tm, LANES), F32), pltpu.VMEM((1, tm, LANES), F32)],
        compiler_params=_params(dimension_semantics=("arbitrary",)),
    )(oa, lse_a, ob[1], lse_b[1], _perm_view(ob[4], 4), _perm_view(lse_b[4], 4), _perm_view(ob[16], 16),
      _perm_view(lse_b[16], 16), oc, lse_c, gate, x, target, w_out_full, post_g)
    dh, d_gate, do_a, st_a, do_b1, st_b1, do_b4, st_b4, do_b16, st_b16, do_c, st_c, d_wout, stats = res
    flat = lambda t: t.reshape(seq, t.shape[-1])
    d_b = {1: (do_b1, st_b1), 4: (flat(do_b4), flat(st_b4)), 16: (flat(do_b16), flat(st_b16))}
    return dh, d_gate, (do_a, st_a), d_b, (do_c, st_c), d_wout, stats


def _inproj_bwd(x, u, dh, pre_g, w_in_full, tabs, dqa, dka, dva, dqkv_b, dqc, dgate, tm=512):
    seq = x.shape[0]
    n_b = B_W // LANES

    def body(x_ref, u_ref, dh_ref, g_ref, w_hbm, c_ref, up_ref, dn_ref, dqa_ref, dka_ref, dva_ref,
             dq1, dk1, dv1, dq4, dk4, dv4, dq16, dk16, dv16, dqc_ref, dg_ref,
             gx_ref, dw_ref, st_ref, scr4, scr16, w_scr, w_sems, dp_ref):
        _stage_w_in(w_hbm, w_scr, w_sems)

        @pl.when(pl.program_id(0) == 0)
        def _():
            st_ref[...] = jnp.zeros_like(st_ref)
            dw_ref[...] = jnp.zeros_like(dw_ref)

        c, up, dn = c_ref[...], -up_ref[...], -dn_ref[...]
        unrot = lambda t: _rotate(t, c, up, dn)
        total = lambda r1, r4, r16: (r1[...].astype(F32) + _load_permuted(r4, scr4, 4)
                                     + _load_permuted(r16, scr16, 16))
        at = lambda piece: slice(*COLS[piece])
        dp_ref[:, at("qa")] = (unrot(dqa_ref[...].astype(F32)) * SCALE).astype(BF16)
        dp_ref[:, at("ka")] = unrot(_per_kv_head(dka_ref[...].astype(F32))).astype(BF16)
        dp_ref[:, at("va")] = _per_kv_head(dva_ref[...].astype(F32)).astype(BF16)
        dp_ref[:, at("ga")] = dg_ref[:, 0:A_W]
        dp_ref[:, at("qb")] = (unrot(total(dq1, dq4, dq16)) * SCALE).astype(BF16)
        dp_ref[:, at("kb")] = unrot(total(dk1, dk4, dk16)).astype(BF16)
        dp_ref[:, at("vb")] = total(dv1, dv4, dv16).astype(BF16)
        dp_ref[:, at("gb")] = dg_ref[:, A_W:A_W + B_W]
        dp_ref[:, at("qc")] = (dqc_ref[...].astype(F32) * SCALE).astype(BF16)
        dp_ref[:, at("gc")] = dg_ref[:, A_W + B_W:D_MODEL]
        du = lax.dot_general(dp_ref[...], w_scr[...], _NT, preferred_element_type=F32)
        res = lax.dot_general(u_ref[...], dp_ref[...], _TN, preferred_element_type=F32)
        for k in range(N_DEV):
            dw_ref[k] += res[:, SHARD_IN * k:SHARD_IN * (k + 1)]
        xv = x_ref[...]
        r = lax.rsqrt(jnp.mean(xv * xv, axis=-1, keepdims=True) + RMS_EPS)
        xh = xv * r
        st_ref[0:1, :] += jnp.sum(du * xh, axis=0, keepdims=True)
        dxh = du * g_ref[...]
        gx_ref[...] = dh_ref[...].astype(F32) + r * (dxh - xh * jnp.mean(dxh * xh, axis=-1, keepdims=True))

    in_specs = ([_row(tm, D_MODEL), _row(tm, D_MODEL), _row(tm, D_MODEL), _full((1, D_MODEL)),
                 pl.BlockSpec(memory_space=pl.ANY),
                 _row(tm, LANES), _row(tm, LANES), _row(tm, LANES), _row(tm, A_W), _row(tm, A_W), _row(tm, A_W)]
                + [_row(tm, B_W)] * 3 + [_perm_spec(tm, 4, B_W)] * 3 + [_perm_spec(tm, 16, B_W)] * 3
                + [_row(tm, C_W), _row(tm, D_MODEL)])
    dw_spec = pl.BlockSpec((N_DEV, D_MODEL, SHARD_IN), lambda i: (0, 0, 0), pipeline_mode=pl.Buffered(1))
    return pl.pallas_call(
        body, name="inproj_bwd", grid=(seq // tm,), in_specs=in_specs,
        out_specs=[_row(tm, D_MODEL), dw_spec, _full((8, D_MODEL))],
        out_shape=[jax.ShapeDtypeStruct((seq, D_MODEL), F32), jax.ShapeDtypeStruct((N_DEV, D_MODEL, SHARD_IN), F32),
                   jax.ShapeDtypeStruct((8, D_MODEL), F32)],
        scratch_shapes=[pltpu.VMEM((n_b, tm, LANES), F32), pltpu.VMEM((n_b, tm, LANES), F32)] + _w_in_scratch()
        + [pltpu.VMEM((tm, D_IN), BF16)],
        compiler_params=_params(dimension_semantics=("arbitrary",)),
    )(x, u, dh, pre_g, w_in_full, *tabs, dqa, dka, dva, *dqkv_b[1], *[_perm_view(t, 4) for t in dqkv_b[4]],
      *[_perm_view(t, 16) for t in dqkv_b[16]], dqc, dgate)


class _ReduceScatter:
    def __init__(self, ins, outs, scratch):
        self.n = n = len(ins)
        self.ins, self.outs = ins, outs
        self.mine, self.got, self.snd, self.rcv = (scratch[n * t:n * (t + 1)] for t in range(4))
        self.load_sems, self.d2d_send, self.d2d_recv, self.ici_send, self.ici_recv = scratch[4 * n:]
        self.pos = _mesh_pos()
        self.pairs = [(a, kk) for kk in (3, 1, 2) for a in range(n)]

    @staticmethod
    def scratch_shapes(shapes):
        return ([pltpu.VMEM((4,) + s, F32) for s in shapes] + [pltpu.VMEM((4,) + s, F32) for s in shapes]
                + [pltpu.VMEM((3,) + s, BF16) for s in shapes] + [pltpu.VMEM((3,) + s, BF16) for s in shapes]
                + [pltpu.SemaphoreType.DMA((len(shapes), 4))] * 5)

    def _chip(self, kk):
        x, y, _ = self.pos
        return (1 - x if kk & 2 else x, 1 - y if kk & 1 else y)

    def _load(self, a, kk):
        block = _dev_index((*self._chip(kk), self.pos[2]))
        return pltpu.make_async_copy(self.ins[a].at[block], self.mine[a].at[kk], self.load_sems.at[a, kk])

    def _swap(self, a, kk):
        x, y, c = self.pos
        return pltpu.make_async_remote_copy(
            src_ref=self.ins[a].at[_dev_index((*self._chip(kk), 1 - c))], dst_ref=self.got[a].at[kk],
            send_sem=self.d2d_send.at[a, kk], recv_sem=self.d2d_recv.at[a, kk],
            device_id=(x, y, 1 - c), device_id_type=MESH_ID)

    def _hop(self, a, kk):
        return pltpu.make_async_remote_copy(
            src_ref=self.snd[a].at[kk - 1], dst_ref=self.rcv[a].at[kk - 1], send_sem=self.ici_send.at[a, kk],
            recv_sem=self.ici_recv.at[a, kk], device_id=(*self._chip(kk), self.pos[2]), device_id_type=MESH_ID)

    def start(self):
        for kk in (3, 1, 2, 0):
            for a in range(self.n):
                self._load(a, kk).start()
                self._swap(a, kk).start()

    def send_chip_sums(self):
        for a, kk in self.pairs:
            self._load(a, kk).wait()
            self._swap(a, kk).wait_recv()
            self.snd[a][kk - 1] = (self.mine[a][kk] + self.got[a][kk]).astype(BF16)
            self._hop(a, kk).start()

    def finish(self):
        for a in range(self.n):
            self._load(a, 0).wait()
            self._swap(a, 0).wait_recv()
            acc = self.mine[a][0] + self.got[a][0]
            for kk in (1, 2, 3):
                self._hop(a, kk).wait_recv()
                acc = acc + self.rcv[a][kk - 1].astype(F32)
            self.outs[a][...] = acc
        for kk in range(4):
            for a in range(self.n):
                self._swap(a, kk).wait_send()
        for a, kk in self.pairs:
            self._hop(a, kk).wait_send()


def _local_step(x, mem, pre_g, w_in, sink, mem_g, w_mem, w_out, post_g, target):
    u, *tabs, w_in_full = _prep(x, pre_g, w_in)
    qa, ka, va, qkv_b, qc, gate, w_mem_all, w_out_all = _inproj(u, w_in_full, tabs, w_mem, w_out)
    w_mem_full = w_mem_all.reshape(D_MODEL, 2 * C_W)
    w_out_full = w_out_all.reshape(D_MODEL, D_MODEL)
    mn, mk, mv = _memkv_fwd(mem, mem_g, w_mem_full)

    a_cfg = dict(dil=1, heads=A_HEADS, max_dist=BLOCK - 1, nq=ATTN_BLOCKS_PER_STEP)
    b_cfgs = {dil: dict(dil=dil, heads=B_HEADS, max_dist=win // dil, nq=ATTN_BLOCKS_PER_STEP)
              for win, dil in B_CONFIGS}
    oa, lse_a = _banded_fwd(qa, ka, va, sink, name="attn_a_fwd", **a_cfg)
    ob, lse_b = {}, {}
    for dil, cfg in b_cfgs.items():
        ob[dil], lse_b[dil] = _banded_fwd(*qkv_b[dil], None, name=f"attn_b{dil}_fwd", **cfg)
    oc, lse_c = _cross_fwd(qc, mk, mv)

    dh, d_gate, d_a, d_b, d_c, d_wout, st_mid = _mid(oa, lse_a, ob, lse_b, oc, lse_c, gate, x, target, w_out_full, post_g)

    dqc, dmk, dmv = _cross_bwd(qc, mk, mv, *d_c)
    d_wmem, st_mem = _memkv_bwd(mem, mem_g, mn, w_mem_full, dmk, dmv)
    bwd_nq = lambda dil: ATTN_BWD_BLOCKS_PER_STEP if dil == 1 else min(dil, ATTN_BWD_BLOCKS_PER_STEP)
    dqkv_b = {dil: _banded_bwd(*qkv_b[dil], *d_b[dil], None, name=f"attn_b{dil}_bwd", **{**cfg, "nq": bwd_nq(dil)})
              for dil, cfg in b_cfgs.items()}
    dqa, dka, dva, dsink, g_wmem, g_wout = _banded_bwd(
        qa, ka, va, *d_a, sink, name="attn_a_bwd", **{**a_cfg, "nq": bwd_nq(1)},
        reduce_scatter=(d_wmem.reshape(N_DEV, SHARD_ROWS, 2 * C_W), d_wout.reshape(N_DEV, SHARD_ROWS, D_MODEL)))

    grad_x, d_win, st_pre = _inproj_bwd(x, u, dh, pre_g, w_in_full, tabs, dqa, dka, dva, dqkv_b, dqc, d_gate)

    dsink_row = jnp.pad(dsink[0:1, :], ((0, 0), (0, D_MODEL - LANES)))
    stats = jnp.concatenate([st_pre[0:1], st_mem[0:1], st_mid[0:1], dsink_row, st_mid[1:2],
                             jnp.zeros((3, D_MODEL), F32)], axis=0)
    return grad_x, d_win, g_wmem, g_wout, stats


def _prep(x, pre_g, w_in, tm=1024):
    seq = x.shape[0]
    n_steps = seq // tm
    parts = 2
    rows = D_MODEL // parts
    relay_at = min(4, n_steps - 1)
    j = jnp.arange(LANES) % HEAD_DIM
    freq = (ROPE_THETA ** (-(2 * (j % (ROT_DIM // 2))).astype(F32) / ROT_DIM))[None, :]
    SIB, NB_X, NB_Y, RELAY, FWD = 0, 1, 2, 3, 4

    def body(x_ref, g_ref, f_ref, win_ref, u_ref, c_ref, up_ref, dn_ref, win_out, win_b,
             send_sems, recv_sems, local_sems):
        step = pl.program_id(0)
        px, py, pc = _mesh_pos()
        me, sibling = (px, py, pc), (px, py, 1 - pc)
        others = lambda core: ((1 - px, py, core), (px, 1 - py, core), (1 - px, 1 - py, core))
        x_nb, y_nb, diag = others(pc)
        relay_from = [x_nb, y_nb]
        relay_to = [y_nb, x_nb]

        def src(a):
            return win_b.at[pl.ds(rows * a, rows)]

        def slot(a, p):
            return win_out.at[_dev_index(p), pl.ds(rows * a, rows)]

        def copy(a, k, block, to, own=False):
            return pltpu.make_async_remote_copy(
                src_ref=src(a) if own else slot(a, block), dst_ref=slot(a, block),
                send_sem=send_sems.at[a, k], recv_sem=recv_sems.at[a, k], device_id=to, device_id_type=MESH_ID)

        def first_sends():
            return [copy(0, NB_X, me, x_nb, own=True), copy(1, NB_Y, me, y_nb, own=True),
                    copy(1, NB_X, me, x_nb, own=True), copy(0, NB_Y, me, y_nb, own=True),
                    copy(0, SIB, me, sibling, own=True), copy(1, SIB, me, sibling, own=True)]

        def relay(a):
            return copy(a, RELAY, relay_from[a], relay_to[a])

        def to_sibling(a, which):
            return copy(a, FWD + which, others(pc)[which], sibling)

        def local(a):
            return pltpu.make_async_copy(src(a), slot(a, me), local_sems.at[a])

        @pl.when(step == 0)
        def _():
            win_b[...] = win_ref[...].astype(BF16)
            for a in range(parts):
                local(a).start()
            for cp in first_sends():
                cp.start()

        @pl.when(step == relay_at)
        def _():
            for a in range(parts):
                copy(a, NB_X + a, relay_from[a], me).wait_recv()
                relay(a).start()
                to_sibling(a, a).start()

        xv = x_ref[...]
        r = lax.rsqrt(jnp.mean(xv * xv, axis=-1, keepdims=True) + RMS_EPS)
        u_ref[...] = ((xv * r) * g_ref[...]).astype(BF16)
        freq = f_ref[...]
        ang_row = lax.broadcasted_iota(jnp.int32, (BLOCK, LANES), 0).astype(F32) * freq
        cos_row, sin_row = jnp.cos(ang_row), jnp.sin(ang_row)
        head_lane = lax.broadcasted_iota(jnp.int32, (BLOCK, LANES), 1) % HEAD_DIM
        half = ROT_DIM // 2
        for blk in range(tm // BLOCK):
            rows_b = slice(BLOCK * blk, BLOCK * (blk + 1))
            ang_0 = (step * tm + BLOCK * blk).astype(F32) * freq
            cos_0, sin_0 = jnp.cos(ang_0), jnp.sin(ang_0)
            cos = cos_0 * cos_row - sin_0 * sin_row
            sin = sin_0 * cos_row + cos_0 * sin_row
            c_ref[rows_b, :] = jnp.where(head_lane < ROT_DIM, cos, 1.0)
            up_ref[rows_b, :] = jnp.where((head_lane >= half) & (head_lane < ROT_DIM), sin, 0.0)
            dn_ref[rows_b, :] = jnp.where(head_lane < half, -sin, 0.0)

        @pl.when(step == n_steps - 1)
        def _():
            copy(1, NB_X, x_nb, me).wait_recv()
            to_sibling(1, 0).start()
            copy(0, NB_Y, y_nb, me).wait_recv()
            to_sibling(0, 1).start()
            for a in range(parts):
                copy(a, RELAY, diag, me).wait_recv()
                to_sibling(a, 2).start()
            for a in range(parts):
                copy(a, SIB, sibling, me).wait_recv()
                for which in range(3):
                    copy(a, FWD + which, others(1 - pc)[which], me).wait_recv()
            for cp in first_sends():
                cp.wait_send()
            for a in range(parts):
                relay(a).wait_send()
                for which in range(3):
                    to_sibling(a, which).wait_send()
                local(a).wait()

    return pl.pallas_call(
        body, name="prep", grid=(n_steps,),
        in_specs=[_row(tm, D_MODEL), _full((1, D_MODEL)), _full((1, LANES)), _full(w_in.shape)],
        out_specs=[_row(tm, D_MODEL), _row(tm, LANES), _row(tm, LANES), _row(tm, LANES),
                   pl.BlockSpec(memory_space=pl.ANY)],
        out_shape=[jax.ShapeDtypeStruct((seq, D_MODEL), BF16)] + [jax.ShapeDtypeStruct((seq, LANES), F32)] * 3
        + [jax.ShapeDtypeStruct((N_DEV,) + w_in.shape, BF16)],
        scratch_shapes=[pltpu.VMEM(w_in.shape, BF16), pltpu.SemaphoreType.DMA((parts, FWD + 3)),
                        pltpu.SemaphoreType.DMA((parts, FWD + 3)), pltpu.SemaphoreType.DMA((parts,))],
        compiler_params=_params(dimension_semantics=("arbitrary",)),
    )(x, pre_g, freq, w_in)


def _exchange_grads(d_win, stats):
    def body(win, st, g_win, r_st, send_sems, recv_sems, local_sem, *scratch):
        exchange = _ReduceScatter((win,), (g_win,), scratch)
        exchange.start()
        pos = _mesh_pos()
        me = _dev_index(pos)
        own = pltpu.make_async_copy(st, r_st.at[me], local_sem)
        own.start()
        copies = []
        for s in range(1, N_DEV):
            peer = _xor_peer(pos, s)
            mk = lambda slot: pltpu.make_async_remote_copy(
                src_ref=st, dst_ref=r_st.at[slot], send_sem=send_sems.at[s], recv_sem=recv_sems.at[s],
                device_id=peer, device_id_type=MESH_ID)
            send, arrival = mk(me), mk(_dev_index(peer))
            send.start()
            copies.append((send, arrival))
        exchange.send_chip_sums()
        exchange.finish()
        for send, arrival in copies:
            arrival.wait_recv()
            send.wait_send()
        own.wait()

    hbm = pl.BlockSpec(memory_space=pl.ANY)
    shard = d_win.shape[1:]
    return pl.pallas_call(
        body, name="exchange_grads", in_specs=[hbm, hbm],
        out_specs=[pl.BlockSpec(memory_space=pltpu.VMEM), hbm],
        out_shape=[jax.ShapeDtypeStruct(shard, F32), jax.ShapeDtypeStruct((N_DEV,) + stats.shape, F32)],
        scratch_shapes=[pltpu.SemaphoreType.DMA((N_DEV,)), pltpu.SemaphoreType.DMA((N_DEV,)), pltpu.SemaphoreType.DMA(())]
        + _ReduceScatter.scratch_shapes([shard]),
        compiler_params=_params(),
    )(d_win, stats)


WEIGHT_ORDER = ("pre_norm", "w_in", "sink_a", "mem_norm", "w_mem_kv", "w_out", "post_norm")


def _adamw_all(grads, r_stats, weights, moments_m, moments_v):
    n = len(WEIGHT_ORDER)
    stat_row = {"pre_norm": 0, "mem_norm": 1, "post_norm": 2, "sink_a": 3}

    def body(*refs):
        gw_in, gw_mem, gw_out, st_ref = refs[0:4]
        w_refs, m_refs, v_refs = (dict(zip(WEIGHT_ORDER, refs[4 + n * t:4 + n * (t + 1)])) for t in range(3))
        loss_ref = refs[4 + 3 * n]
        outs = refs[5 + 3 * n:]
        g_small = st_ref[0]
        for s in range(1, N_DEV):
            g_small = g_small + st_ref[s]
        loss_ref[...] = g_small[4:5, 0:1]
        big = {"w_in": gw_in, "w_mem_kv": gw_mem, "w_out": gw_out}
        for i, name in enumerate(WEIGHT_ORDER):
            if name in big:
                g = big[name][...]
                at = lambda ref: ref[0]
            else:
                width = w_refs[name].shape[-1]
                g = g_small[stat_row[name]:stat_row[name] + 1, 0:width]
                at = lambda ref: ref[...]
            m2 = ADAM_B1 * at(m_refs[name]) + (1.0 - ADAM_B1) * g
            v2 = ADAM_B2 * at(v_refs[name]) + (1.0 - ADAM_B2) * (g * g)
            m_hat = m2 / (1.0 - ADAM_B1 ** ADAM_STEP)
            v_hat = v2 / (1.0 - ADAM_B2 ** ADAM_STEP)
            delta = -ADAM_LR * (m_hat / (jnp.sqrt(v_hat) + ADAM_EPS) + ADAM_WD * at(w_refs[name]))
            for kind, val in enumerate((g, delta, m2, v2)):
                out = outs[kind * n + i]
                if name in big:
                    out[0] = val
                else:
                    out[...] = val

    shapes = [weights[name].shape for name in WEIGHT_ORDER]
    res = pl.pallas_call(
        body, name="adamw_all",
        out_shape=[jax.ShapeDtypeStruct((1, 1), F32)] + [jax.ShapeDtypeStruct(sh, F32) for sh in shapes] * 4,
        compiler_params=_params(),
    )(grads["w_in"], grads["w_mem_kv"], grads["w_out"], r_stats,
      *[weights[k] for k in WEIGHT_ORDER], *[moments_m[k] for k in WEIGHT_ORDER], *[moments_v[k] for k in WEIGHT_ORDER])
    return res[0].reshape(()), res[1:]


def kernel(x, mem, pre_norm, w_in, sink_a, mem_norm, w_mem_kv, w_out, post_norm, loss_target, m_pre_norm, m_w_in, m_sink_a, m_mem_norm, m_w_mem_kv, m_w_out, m_post_norm, v_pre_norm, v_w_in, v_sink_a, v_mem_norm, v_w_mem_kv, v_w_out, v_post_norm):
    sink = jnp.pad(sink_a[0], (0, 8 - A_HEADS))
    grad_x, d_win, g_wmem, g_wout, stats = _local_step(
        x[0], mem[0], pre_norm, w_in[0], sink, mem_norm, w_mem_kv[0], w_out[0], post_norm, loss_target[0])
    g_win, r_stats = _exchange_grads(d_win, stats)
    weights = dict(pre_norm=pre_norm, w_in=w_in, sink_a=sink_a, mem_norm=mem_norm, w_mem_kv=w_mem_kv, w_out=w_out,
                   post_norm=post_norm)
    moments_m = dict(pre_norm=m_pre_norm, w_in=m_w_in, sink_a=m_sink_a, mem_norm=m_mem_norm, w_mem_kv=m_w_mem_kv,
                     w_out=m_w_out, post_norm=m_post_norm)
    moments_v = dict(pre_norm=v_pre_norm, w_in=v_w_in, sink_a=v_sink_a, mem_norm=v_mem_norm, w_mem_kv=v_w_mem_kv,
                     w_out=v_w_out, post_norm=v_post_norm)
    loss, rest = _adamw_all(dict(w_in=g_win, w_mem_kv=g_wmem, w_out=g_wout), r_stats, weights, moments_m, moments_v)
    return (loss, grad_x[None], *rest)
```

```python
import jax
import jax.numpy as jnp
from jax import lax
from jax.experimental import pallas as pl
from jax.experimental.pallas import tpu as pltpu

F32 = jnp.float32
BF16 = jnp.bfloat16

D_MODEL = 1024
HEAD_DIM = 64
ROT_DIM = 16
ROPE_THETA = 500000.0
BLOCK = 128
LANES = 128
N_MEM = 256
RMS_EPS = 1e-6
SCALE = HEAD_DIM ** -0.5
A_HEADS = 6
B_HEADS = 6
C_HEADS = 4
A_W, A_KV_W, B_W, C_W = 384, 128, 384, 256
_IN_PIECES = (("qa", A_W), ("ka", A_KV_W), ("va", A_KV_W), ("ga", A_W), ("qb", B_W), ("kb", B_W), ("vb", B_W),
              ("gb", B_W), ("qc", C_W), ("gc", C_W))
COLS, D_IN = {}, 0
for _name, _width in _IN_PIECES:
    COLS[_name] = (D_IN, D_IN + _width)
    D_IN += _width
N_DEV = 8
SHARD_IN = D_IN // N_DEV
SHARD_ROWS = D_MODEL // N_DEV
B_CONFIGS = ((128, 1), (512, 4), (2048, 16))
DILS = (4, 16)
NEG = -1e30
ATTN_BLOCKS_PER_STEP = 4
ATTN_BWD_BLOCKS_PER_STEP = 8
DELTA_LANE = 64
VMEM_LIMIT = 56 * 1024 * 1024

ADAM_LR, ADAM_B1, ADAM_B2, ADAM_EPS, ADAM_WD, ADAM_STEP = 0.001, 0.9, 0.999, 1e-08, 0.01, 10
MESH_ID = pl.DeviceIdType.MESH


def _params(**kw):
    return pltpu.CompilerParams(vmem_limit_bytes=VMEM_LIMIT, **kw)


def _full(shape):
    n = len(shape)
    return pl.BlockSpec(shape, lambda *_: (0,) * n)


def _row(tm, w):
    return pl.BlockSpec((tm, w), lambda i: (i, 0))


def _mesh_pos():
    return lax.axis_index("x"), lax.axis_index("y"), lax.axis_index("c")


def _dev_index(pos):
    return 4 * pos[0] + 2 * pos[1] + pos[2]


def _xor_peer(pos, s):
    x, y, c = pos
    return (1 - x if s & 4 else x, 1 - y if s & 2 else y, 1 - c if s & 1 else c)


def _perm_view(a, dil):
    return a.reshape(a.shape[0] // (BLOCK * dil), dil, BLOCK, a.shape[1])


def _perm_spec(tm, dil, w):
    chunk = BLOCK * dil
    if tm >= chunk:
        return pl.BlockSpec((tm // chunk, dil, BLOCK, w), lambda i: (i, 0, 0, 0))
    per = chunk // tm
    return pl.BlockSpec((1, dil, tm // dil, w), lambda i: (i // per, 0, i % per, 0))


def _put(scr, val):
    for c in range(val.shape[1] // LANES):
        scr[c] = val[:, LANES * c:LANES * (c + 1)]


def _get(scr):
    n = scr.shape[0]
    return scr[0] if n == 1 else jnp.concatenate([scr[c] for c in range(n)], axis=1)


def _get_class(scr, r, dil):
    n, rows = scr.shape[0], scr.shape[1]
    parts = [scr.at[c][pl.ds(r, rows // dil, stride=dil), :] for c in range(n)]
    return parts[0] if n == 1 else jnp.concatenate(parts, axis=1)


def _store_permuted(scr, out_ref, dil, dtype):
    for r in range(dil):
        out_ref[0, r] = _get_class(scr, r, dil).astype(dtype)


def _load_permuted(in_ref, scr, dil):
    n, rows = scr.shape[0], scr.shape[1]
    for r in range(dil):
        val = in_ref[0, r].astype(F32)
        for c in range(n):
            scr.at[c][pl.ds(r, rows // dil, stride=dil), :] = val[:, LANES * c:LANES * (c + 1)]
    return _get(scr)


def _rotate128(t, c, up, dn):
    half = ROT_DIM // 2
    return t * c + pltpu.roll(t, half, 1) * up + pltpu.roll(t, LANES - half, 1) * dn


def _rotate(t, c, up, dn):
    outs = [_rotate128(t[:, LANES * j:LANES * (j + 1)], c, up, dn) for j in range(t.shape[1] // LANES)]
    return outs[0] if len(outs) == 1 else jnp.concatenate(outs, axis=1)


def _per_query_head(kv):
    lane = lax.broadcasted_iota(jnp.int32, kv.shape, 1)
    other = pltpu.roll(kv, HEAD_DIM, 1)
    return jnp.concatenate([jnp.where(lane < HEAD_DIM, kv, other), kv, jnp.where(lane < HEAD_DIM, other, kv)], axis=1)


def _per_kv_head(d):
    s0, s1, s2 = (d[:, LANES * p:LANES * (p + 1)] for p in range(3))
    lane = lax.broadcasted_iota(jnp.int32, s0.shape, 1)
    return jnp.where(lane < HEAD_DIM, s0 + pltpu.roll(s0, HEAD_DIM, 1) + s1, s1 + s2 + pltpu.roll(s2, HEAD_DIM, 1))


def _w_in_scratch():
    return [pltpu.VMEM((D_MODEL, D_IN), BF16), pltpu.SemaphoreType.DMA((N_DEV,))]


def _stage_w_in(w_hbm, w_scr, sems):
    @pl.when(pl.program_id(0) == 0)
    def _():
        copies = [pltpu.make_async_copy(w_hbm.at[k], w_scr.at[:, pl.ds(SHARD_IN * k, SHARD_IN)], sems.at[k])
                  for k in range(N_DEV)]
        for cp in copies:
            cp.start()
        for cp in copies:
            cp.wait()


def _inproj(u, w_in_full, tabs, w_mem, w_out, tm=1024):
    seq = u.shape[0]
    n_chunk = D_IN // LANES
    n_steps = seq // tm

    def body(u_ref, w_hbm, c_ref, up_ref, dn_ref, wm_ref, wo_ref, qa_ref, ka_ref, va_ref,
             qb1_ref, kb1_ref, vb1_ref, qb4_ref, kb4_ref, vb4_ref, qb16_ref, kb16_ref, vb16_ref,
             qc_ref, gate_ref, wm_all, wo_all, proj, w_scr, w_sems, wm_b, wo_b, send_sems, recv_sems, local_sems):
        step = pl.program_id(0)
        shards, gathered = (wm_b, wo_b), (wm_all, wo_all)

        def gather_copies(arriving):
            pos = _mesh_pos()
            me = _dev_index(pos)
            local = [] if arriving else [
                pltpu.make_async_copy(shards[a], gathered[a].at[me], local_sems.at[a]) for a in range(2)]
            remote = []
            for s in range(1, N_DEV):
                peer = _xor_peer(pos, s)
                for a in range(2):
                    remote.append(pltpu.make_async_remote_copy(
                        src_ref=shards[a], dst_ref=gathered[a].at[_dev_index(peer) if arriving else me],
                        send_sem=send_sems.at[a, s], recv_sem=recv_sems.at[a, s], device_id=peer,
                        device_id_type=MESH_ID))
            return local, remote

        @pl.when(step == 0)
        def _():
            wm_b[...] = wm_ref[...].astype(BF16)
            wo_b[...] = wo_ref[...].astype(BF16)
            local, sends = gather_copies(arriving=False)
            for cp in local + sends:
                cp.start()

        _stage_w_in(w_hbm, w_scr, w_sems)
        u = u_ref[...]
        for n0 in range(0, D_IN, D_MODEL):
            acc = jnp.dot(u, w_scr[:, n0:n0 + D_MODEL], preferred_element_type=F32)
            for c3 in range(D_MODEL // LANES):
                proj[n0 // LANES + c3] = acc[:, LANES * c3:LANES * (c3 + 1)]
        c, up, dn = c_ref[...], up_ref[...], dn_ref[...]

        def chunks_of(piece):
            lo, hi = COLS[piece]
            return range(lo // LANES, hi // LANES)

        def cols(piece, rot=False, scale=None):
            parts = []
            for ch in chunks_of(piece):
                t = proj[ch]
                if rot:
                    t = _rotate128(t, c, up, dn)
                if scale is not None:
                    t = t * scale
                parts.append(t)
            return parts[0] if len(parts) == 1 else jnp.concatenate(parts, axis=1)

        qa_ref[...] = cols("qa", True, SCALE).astype(BF16)
        ka_ref[...] = _per_query_head(cols("ka", True)).astype(BF16)
        va_ref[...] = _per_query_head(cols("va")).astype(BF16)
        gate_ref[:, 0:A_W] = cols("ga").astype(BF16)
        gate_ref[:, A_W:A_W + B_W] = cols("gb").astype(BF16)
        gate_ref[:, A_W + B_W:D_MODEL] = cols("gc").astype(BF16)
        qc_ref[...] = cols("qc", False, SCALE).astype(BF16)
        for ch in chunks_of("qb"):
            proj[ch] = _rotate128(proj[ch], c, up, dn) * SCALE
        for ch in chunks_of("kb"):
            proj[ch] = _rotate128(proj[ch], c, up, dn)
        for piece, nat, p4, p16 in (("qb", qb1_ref, qb4_ref, qb16_ref), ("kb", kb1_ref, kb4_ref, kb16_ref),
                                    ("vb", vb1_ref, vb4_ref, vb16_ref)):
            chunks = chunks_of(piece)
            nat[...] = jnp.concatenate([proj[ch] for ch in chunks], axis=1).astype(BF16)
            for dil, ref in ((4, p4), (16, p16)):
                span = min(tm, BLOCK * dil)
                for cc in range(tm // span):
                    for rr in range(dil):
                        ref[cc, rr] = jnp.concatenate(
                            [proj.at[ch][pl.ds(cc * span + rr, span // dil, stride=dil), :] for ch in chunks],
                            axis=1).astype(BF16)

        @pl.when(step == n_steps - 1)
        def _():
            for cp in gather_copies(arriving=True)[1]:
                cp.wait_recv()
            local, sends = gather_copies(arriving=False)
            for cp in sends:
                cp.wait_send()
            for cp in local:
                cp.wait()

    nat_w = (A_W, A_W, A_W, B_W, B_W, B_W)
    out_specs = [_row(tm, w) for w in nat_w]
    out_shape = [jax.ShapeDtypeStruct((seq, w), BF16) for w in nat_w]
    for dil in DILS:
        out_specs += [_perm_spec(tm, dil, B_W)] * 3
        out_shape += [jax.ShapeDtypeStruct((seq // (BLOCK * dil), dil, BLOCK, B_W), BF16)] * 3
    hbm = pl.BlockSpec(memory_space=pl.ANY)
    out_specs += [_row(tm, C_W), _row(tm, D_MODEL), hbm, hbm]
    out_shape += [jax.ShapeDtypeStruct((seq, C_W), BF16), jax.ShapeDtypeStruct((seq, D_MODEL), BF16),
                  jax.ShapeDtypeStruct((N_DEV,) + w_mem.shape, BF16), jax.ShapeDtypeStruct((N_DEV,) + w_out.shape, BF16)]
    res = pl.pallas_call(
        body, name="inproj", grid=(n_steps,),
        in_specs=[_row(tm, D_MODEL), hbm, _row(tm, LANES), _row(tm, LANES), _row(tm, LANES),
                  _full(w_mem.shape), _full(w_out.shape)],
        out_specs=out_specs, out_shape=out_shape,
        scratch_shapes=[pltpu.VMEM((n_chunk, tm, LANES), F32)] + _w_in_scratch()
        + [pltpu.VMEM(w_mem.shape, BF16), pltpu.VMEM(w_out.shape, BF16), pltpu.SemaphoreType.DMA((2, N_DEV)),
           pltpu.SemaphoreType.DMA((2, N_DEV)), pltpu.SemaphoreType.DMA((2,))],
        compiler_params=_params(dimension_semantics=("arbitrary",)),
    )(u, w_in_full, *tabs, w_mem, w_out)
    qa, ka, va = res[0:3]
    qkv_b = {1: res[3:6], 4: [t.reshape(seq, B_W) for t in res[6:9]], 16: [t.reshape(seq, B_W) for t in res[9:12]]}
    return qa, ka, va, qkv_b, res[12], res[13], res[14], res[15]


def _memkv_fwd(mem, mem_g, w_mem_full):
    def body(mem_ref, g_ref, w_ref, mn_ref, mk_ref, mv_ref):
        mv_ = mem_ref[...]
        r = lax.rsqrt(jnp.mean(mv_ * mv_, axis=-1, keepdims=True) + RMS_EPS)
        mn = ((mv_ * r) * g_ref[...]).astype(BF16)
        mn_ref[...] = mn
        mkv = jnp.dot(mn, w_ref[...], preferred_element_type=F32)
        mk_ref[...] = mkv[:, 0:C_W].astype(BF16)
        mv_ref[...] = mkv[:, C_W:2 * C_W].astype(BF16)

    return pl.pallas_call(
        body, name="memkv_fwd",
        out_shape=[jax.ShapeDtypeStruct((N_MEM, D_MODEL), BF16),
                   jax.ShapeDtypeStruct((N_MEM, C_W), BF16), jax.ShapeDtypeStruct((N_MEM, C_W), BF16)],
        compiler_params=_params(),
    )(mem, mem_g, w_mem_full)


def _memkv_bwd(mem, mem_g, mn, w_mem_full, dmk, dmv):
    def body(mem_ref, g_ref, mn_ref, w_ref, dmk_ref, dmv_ref, dw_ref, st_ref):
        dmkv = jnp.concatenate([dmk_ref[...], dmv_ref[...]], axis=1).astype(BF16)
        dw_ref[...] = lax.dot_general(mn_ref[...], dmkv, (((0,), (0,)), ((), ())), preferred_element_type=F32)
        dmn = lax.dot_general(dmkv, w_ref[...], (((1,), (1,)), ((), ())), preferred_element_type=F32)
        mv_ = mem_ref[...]
        r = lax.rsqrt(jnp.mean(mv_ * mv_, axis=-1, keepdims=True) + RMS_EPS)
        st_ref[...] = jnp.zeros_like(st_ref)
        st_ref[0:1, :] = jnp.sum(dmn * (mv_ * r), axis=0, keepdims=True)

    return pl.pallas_call(
        body, name="memkv_bwd",
        out_shape=[jax.ShapeDtypeStruct((D_MODEL, 2 * C_W), F32), jax.ShapeDtypeStruct((8, D_MODEL), F32)],
        compiler_params=_params(),
    )(mem, mem_g, mn, w_mem_full, dmk, dmv)


def _band_mask(has_prev, max_dist):
    qi = lax.broadcasted_iota(jnp.int32, (BLOCK, 2 * BLOCK), 0)
    kj = lax.broadcasted_iota(jnp.int32, (BLOCK, 2 * BLOCK), 1)
    dist = qi + BLOCK - kj
    return (dist >= 0) & (dist <= max_dist) & ((kj >= BLOCK) | has_prev)


_NT = (((1,), (1,)), ((), ()))
_TN = (((0,), (0,)), ((), ()))


def _head_only(val, h):
    slab = _slabs_of(val)(h)
    lane = lax.broadcasted_iota(jnp.int32, slab.shape, 1)
    keep = (lane < HEAD_DIM) if h % 2 == 0 else (lane >= HEAD_DIM)
    return jnp.where(keep, slab, jnp.zeros((), slab.dtype))


def _slabs_of(val):
    return lambda h: val[:, LANES * (h // 2):LANES * (h // 2 + 1)]


class _BandSteps:
    def __init__(self, seq, dil, nq):
        self.nq, self.rows, self.consecutive = nq, nq * BLOCK, dil == 1
        nb = seq // dil // BLOCK
        if self.consecutive:
            assert nb % nq == 0
            self.outer, self.inner, self.stride = 1, nb // nq, 1
        else:
            assert dil % nq == 0
            self.outer, self.inner, self.stride = dil // nq, nb, dil // nq

    def own(self, w, clamp=False):
        cur = (lambda i: jnp.minimum(i, self.inner - 1)) if clamp else (lambda i: i)
        return pl.BlockSpec((self.rows, w), lambda r, i: (cur(i) * self.stride + r, 0))

    def prev(self, w, clamp=False):
        cur = (lambda i: jnp.minimum(i, self.inner - 1)) if clamp else (lambda i: i)
        if self.consecutive:
            return pl.BlockSpec((BLOCK, w), lambda r, i: (jnp.maximum(cur(i) * self.nq - 1, 0), 0))
        return pl.BlockSpec((self.rows, w), lambda r, i: (jnp.maximum(cur(i) - 1, 0) * self.stride + r, 0))

    def late(self, w):
        return pl.BlockSpec((self.rows, w), lambda r, i: (jnp.maximum(i - 1, 0) * self.stride + r, 0))

    def rows_of(self, j):
        return slice(BLOCK * j, BLOCK * (j + 1))

    def keys(self, p_ref, c_ref, j):
        if not self.consecutive:
            before = p_ref[self.rows_of(j), :]
        elif j == 0:
            before = p_ref[...]
        else:
            before = c_ref[self.rows_of(j - 1), :]
        return jnp.concatenate([before, c_ref[self.rows_of(j), :]], axis=0)

    def has_prev(self, i, j):
        return True if (self.consecutive and j > 0) else (i > 0)


def _banded_fwd(q, k, v, sink, *, dil, heads, max_dist, nq, name):
    seq = q.shape[0]
    qw = kw = heads * HEAD_DIM
    steps = _BandSteps(seq, dil, nq)

    def body(*refs):
        if sink is not None:
            sink_ref, refs = refs[0], refs[1:]
        q_ref, kp_ref, kc_ref, vp_ref, vc_ref, o_ref, lse_ref, s_scr, p_scr = refs
        i = pl.program_id(1)
        lane = lax.broadcasted_iota(jnp.int32, (BLOCK, LANES), 1)
        k_of = [_slabs_of(steps.keys(kp_ref, kc_ref, j)) for j in range(nq)]
        v_of = [_slabs_of(steps.keys(vp_ref, vc_ref, j)) for j in range(nq)]
        for j in range(nq):
            qv = q_ref[steps.rows_of(j), :]
            for h in range(heads):
                s_scr[j * heads + h] = lax.dot_general(_head_only(qv, h), k_of[j](h), _NT, preferred_element_type=F32)
        ls = {}
        for j in range(nq):
            valid = _band_mask(steps.has_prev(i, j), max_dist)
            lse_tile = jnp.zeros((BLOCK, LANES), F32)
            for h in range(heads):
                s = jnp.where(valid, s_scr[j * heads + h], NEG)
                m = jnp.max(s, axis=-1, keepdims=True)
                if sink is not None:
                    sk = sink_ref[h]
                    m = jnp.maximum(m, sk)
                p = jnp.exp(s - m)
                l = jnp.sum(p, axis=-1, keepdims=True)
                if sink is not None:
                    l = l + jnp.exp(sk - m)
                p_scr[j * heads + h] = p.astype(BF16)
                ls[j, h] = l
                lse_tile = jnp.where(lane == h, m + jnp.log(l), lse_tile)
            lse_ref[steps.rows_of(j), :] = lse_tile
        for j in range(nq):
            for pr in range(heads // 2):
                he, ho = 2 * pr, 2 * pr + 1
                even = jnp.dot(p_scr[j * heads + he], v_of[j](he), preferred_element_type=F32) / ls[j, he]
                odd = jnp.dot(p_scr[j * heads + ho], v_of[j](ho), preferred_element_type=F32) / ls[j, ho]
                o_ref[steps.rows_of(j), LANES * pr:LANES * (pr + 1)] = jnp.where(lane < HEAD_DIM, even, odd).astype(BF16)

    in_specs = [steps.own(qw), steps.prev(kw), steps.own(kw), steps.prev(kw), steps.own(kw)]
    args = [q, k, k, v, v]
    if sink is not None:
        in_specs = [pl.BlockSpec(memory_space=pltpu.SMEM)] + in_specs
        args = [sink] + args
    return pl.pallas_call(
        body, name=name, grid=(steps.outer, steps.inner), in_specs=in_specs,
        out_specs=[steps.own(qw), steps.own(LANES)],
        out_shape=[jax.ShapeDtypeStruct((seq, qw), BF16), jax.ShapeDtypeStruct((seq, LANES), F32)],
        scratch_shapes=[pltpu.VMEM((nq * heads, BLOCK, 2 * BLOCK), F32), pltpu.VMEM((nq * heads, BLOCK, 2 * BLOCK), BF16)],
        compiler_params=_params(dimension_semantics=("arbitrary", "arbitrary")),
    )(*args)


def _banded_bwd(q, k, v, d_out, stat, sink, *, dil, heads, max_dist, nq, name, reduce_scatter=()):
    seq = q.shape[0]
    qw = kw = heads * HEAD_DIM
    steps = _BandSteps(seq, dil, nq)
    n_rs = len(reduce_scatter)
    n_in = 7 + n_rs
    n_flat = steps.outer * (steps.inner + 1)

    def body(*refs):
        refs = list(refs)
        sink_ref = refs.pop(0) if sink is not None else None
        (q_ref, kp_ref, kc_ref, vp_ref, vc_ref, do_ref, st_ref), partials = refs[:7], refs[7:n_in]
        refs = refs[n_in:]
        dsink_ref = refs.pop(0) if sink is not None else None
        (dq_ref, dk_ref, dv_ref), sums = refs[:3], refs[3:3 + n_rs]
        kcar, vcar, s_scr, dp_scr, p_scr, ds_scr = refs[3 + n_rs:9 + n_rs]
        r, i = pl.program_id(0), pl.program_id(1)
        if n_rs:
            exchange = _ReduceScatter(tuple(partials), tuple(sums), refs[9 + n_rs:])
            flat = r * (steps.inner + 1) + i

            @pl.when(flat == 0)
            def _():
                exchange.start()

            @pl.when(flat == min(2, n_flat - 1))
            def _():
                exchange.send_chip_sums()

        @pl.when(i == 0)
        def _():
            kcar[...] = jnp.zeros_like(kcar)
            vcar[...] = jnp.zeros_like(vcar)

        if sink is not None:
            @pl.when((i == 0) & (r == 0))
            def _():
                dsink_ref[...] = jnp.zeros_like(dsink_ref)

        @pl.when(i < steps.inner)
        def _():
            lane = lax.broadcasted_iota(jnp.int32, (1, LANES), 1)
            lane_q = lax.broadcasted_iota(jnp.int32, (BLOCK, LANES), 1)
            k_of = [_slabs_of(steps.keys(kp_ref, kc_ref, j)) for j in range(nq)]
            v_of = [_slabs_of(steps.keys(vp_ref, vc_ref, j)) for j in range(nq)]
            qms, doms = {}, {}
            for j in range(nq):
                qv, dov = q_ref[steps.rows_of(j), :], do_ref[steps.rows_of(j), :]
                for h in range(heads):
                    qms[j, h], doms[j, h] = _head_only(qv, h), _head_only(dov, h)
                    s_scr[j * heads + h] = lax.dot_general(qms[j, h], k_of[j](h), _NT, preferred_element_type=F32)
                    dp_scr[j * heads + h] = lax.dot_general(doms[j, h], v_of[j](h), _NT, preferred_element_type=F32)
            dsink_row = jnp.zeros((1, LANES), F32)
            if sink is not None:
                sink_row = jnp.zeros((1, LANES), F32)
                for h in range(heads):
                    sink_row = jnp.where(lane == h, sink_ref[h], sink_row)
            for j in range(nq):
                st = st_ref[steps.rows_of(j), :]
                valid = _band_mask(steps.has_prev(i, j), max_dist)
                for h in range(heads):
                    lse_h = st[:, h:h + 1]
                    delta = st[:, DELTA_LANE + h:DELTA_LANE + h + 1]
                    p = jnp.where(valid, jnp.exp(s_scr[j * heads + h] - lse_h), 0.0)
                    p_scr[j * heads + h] = p.astype(BF16)
                    ds_scr[j * heads + h] = (p * (dp_scr[j * heads + h] - delta)).astype(BF16)
                if sink is not None:
                    term = -jnp.exp(sink_row - st) * pltpu.roll(st, LANES - DELTA_LANE, 1)
                    dsink_row = dsink_row + jnp.sum(jnp.where(lane_q < heads, term, 0.0), axis=0, keepdims=True)
            for j in range(nq):
                for pr in range(heads // 2):
                    he, ho = 2 * pr, 2 * pr + 1
                    even = jnp.dot(ds_scr[j * heads + he], k_of[j](he), preferred_element_type=F32)
                    odd = jnp.dot(ds_scr[j * heads + ho], k_of[j](ho), preferred_element_type=F32)
                    dq_ref[steps.rows_of(j), LANES * pr:LANES * (pr + 1)] = (
                        jnp.where(lane_q < HEAD_DIM, even, odd).astype(BF16))
            if steps.consecutive:
                dk_ref[...] = kcar[...].astype(BF16)
                dv_ref[...] = vcar[...].astype(BF16)
            for j in range(nq):
                for slab in range(kw // LANES):
                    he, ho = j * heads + 2 * slab, j * heads + 2 * slab + 1
                    dk_j = (lax.dot_general(ds_scr[he], qms[j, 2 * slab], _TN, preferred_element_type=F32)
                            + lax.dot_general(ds_scr[ho], qms[j, 2 * slab + 1], _TN, preferred_element_type=F32))
                    dv_j = (lax.dot_general(p_scr[he], doms[j, 2 * slab], _TN, preferred_element_type=F32)
                            + lax.dot_general(p_scr[ho], doms[j, 2 * slab + 1], _TN, preferred_element_type=F32))
                    sl = slice(LANES * slab, LANES * (slab + 1))
                    own_rows = steps.rows_of(j)
                    if not steps.consecutive:
                        dk_ref[own_rows, sl] = (kcar[own_rows, sl] + dk_j[0:BLOCK]).astype(BF16)
                        dv_ref[own_rows, sl] = (vcar[own_rows, sl] + dv_j[0:BLOCK]).astype(BF16)
                    elif j == 0:
                        last = steps.rows_of(nq - 1)
                        dk_ref[last, sl] = (kcar[last, sl] + dk_j[0:BLOCK]).astype(BF16)
                        dv_ref[last, sl] = (vcar[last, sl] + dv_j[0:BLOCK]).astype(BF16)
                    else:
                        before = steps.rows_of(j - 1)
                        kcar[before, sl] += dk_j[0:BLOCK]
                        vcar[before, sl] += dv_j[0:BLOCK]
                    kcar[own_rows, sl] = dk_j[BLOCK:2 * BLOCK]
                    vcar[own_rows, sl] = dv_j[BLOCK:2 * BLOCK]
            if sink is not None:
                dsink_ref[0:1, :] += dsink_row

        @pl.when(i == steps.inner)
        def _():
            dk_ref[...] = kcar[...].astype(BF16)
            dv_ref[...] = vcar[...].astype(BF16)

        if n_rs:
            @pl.when(flat == n_flat - 1)
            def _():
                exchange.finish()

    own, prev = (lambda w: steps.own(w, clamp=True)), (lambda w: steps.prev(w, clamp=True))
    rs_shapes = [t.shape[1:] for t in reduce_scatter]
    in_specs = ([own(qw), prev(kw), own(kw), prev(kw), own(kw), own(qw), own(LANES)]
                + [pl.BlockSpec(memory_space=pl.ANY)] * n_rs)
    args = [q, k, k, v, v, d_out, stat, *reduce_scatter]
    out_specs = [own(qw), steps.late(kw), steps.late(kw)] + [_full(s) for s in rs_shapes]
    out_shape = [jax.ShapeDtypeStruct((seq, qw), BF16), jax.ShapeDtypeStruct((seq, kw), BF16),
                 jax.ShapeDtypeStruct((seq, kw), BF16)] + [jax.ShapeDtypeStruct(s, F32) for s in rs_shapes]
    if sink is not None:
        in_specs = [pl.BlockSpec(memory_space=pltpu.SMEM)] + in_specs
        args = [sink] + args
        out_specs = [_full((8, LANES))] + out_specs
        out_shape = [jax.ShapeDtypeStruct((8, LANES), F32)] + out_shape
    n_hb = nq * heads
    res = pl.pallas_call(
        body, name=name, grid=(steps.outer, steps.inner + 1), in_specs=in_specs, out_specs=out_specs,
        out_shape=out_shape,
        scratch_shapes=[pltpu.VMEM((steps.rows, kw), F32), pltpu.VMEM((steps.rows, kw), F32)]
        + [pltpu.VMEM((n_hb, BLOCK, 2 * BLOCK), F32)] * 2 + [pltpu.VMEM((n_hb, BLOCK, 2 * BLOCK), BF16)] * 2
        + (_ReduceScatter.scratch_shapes(rs_shapes) if n_rs else []),
        compiler_params=_params(dimension_semantics=("arbitrary", "arbitrary")),
    )(*args)
    if sink is not None:
        return (*res[1:4], res[0], *res[4:])
    return res


def _cross_fwd(q, mk, mv, tq=1024):
    seq = q.shape[0]

    def body(q_ref, mk_ref, mv_ref, o_ref, lse_ref, s_scr, p_scr):
        qv = q_ref[...]
        k_of, v_of = _slabs_of(mk_ref[...]), _slabs_of(mv_ref[...])
        lane = lax.broadcasted_iota(jnp.int32, (tq, LANES), 1)
        lse_tile = jnp.zeros((tq, LANES), F32)
        for h in range(C_HEADS):
            s_scr[h] = lax.dot_general(_head_only(qv, h), k_of(h), _NT, preferred_element_type=F32)
        ls = []
        for h in range(C_HEADS):
            s = s_scr[h]
            m = jnp.max(s, axis=-1, keepdims=True)
            p = jnp.exp(s - m)
            l = jnp.sum(p, axis=-1, keepdims=True)
            p_scr[h] = p.astype(BF16)
            ls.append(l)
            lse_tile = jnp.where(lane == h, m + jnp.log(l), lse_tile)
        for pr in range(C_HEADS // 2):
            even = jnp.dot(p_scr[2 * pr], v_of(2 * pr), preferred_element_type=F32) / ls[2 * pr]
            odd = jnp.dot(p_scr[2 * pr + 1], v_of(2 * pr + 1), preferred_element_type=F32) / ls[2 * pr + 1]
            o_ref[:, LANES * pr:LANES * (pr + 1)] = jnp.where(lane < HEAD_DIM, even, odd).astype(BF16)
        lse_ref[...] = lse_tile

    return pl.pallas_call(
        body, name="cross_fwd", grid=(seq // tq,),
        in_specs=[_row(tq, C_W), _full((N_MEM, C_W)), _full((N_MEM, C_W))],
        out_specs=[_row(tq, C_W), _row(tq, LANES)],
        out_shape=[jax.ShapeDtypeStruct((seq, C_W), BF16), jax.ShapeDtypeStruct((seq, LANES), F32)],
        scratch_shapes=[pltpu.VMEM((C_HEADS, tq, N_MEM), F32), pltpu.VMEM((C_HEADS, tq, N_MEM), BF16)],
        compiler_params=_params(dimension_semantics=("arbitrary",)),
    )(q, mk, mv)


def _cross_bwd(q, mk, mv, d_out, stat, tq=1024):
    seq = q.shape[0]

    def body(q_ref, mk_ref, mv_ref, do_ref, st_ref, dq_ref, dmk_ref, dmv_ref, s_scr, dp_scr, p_scr, ds_scr):
        @pl.when(pl.program_id(0) == 0)
        def _():
            dmk_ref[...] = jnp.zeros_like(dmk_ref)
            dmv_ref[...] = jnp.zeros_like(dmv_ref)

        qv, dov, st = q_ref[...], do_ref[...], st_ref[...]
        k_of, v_of = _slabs_of(mk_ref[...]), _slabs_of(mv_ref[...])
        qms = [_head_only(qv, h) for h in range(C_HEADS)]
        doms = [_head_only(dov, h) for h in range(C_HEADS)]
        for h in range(C_HEADS):
            s_scr[h] = lax.dot_general(qms[h], k_of(h), _NT, preferred_element_type=F32)
            dp_scr[h] = lax.dot_general(doms[h], v_of(h), _NT, preferred_element_type=F32)
        for h in range(C_HEADS):
            p = jnp.exp(s_scr[h] - st[:, h:h + 1])
            p_scr[h] = p.astype(BF16)
            ds_scr[h] = (p * (dp_scr[h] - st[:, DELTA_LANE + h:DELTA_LANE + h + 1])).astype(BF16)
        lane = lax.broadcasted_iota(jnp.int32, (tq, LANES), 1)
        for pr in range(C_HEADS // 2):
            sl = slice(LANES * pr, LANES * (pr + 1))
            even = jnp.dot(ds_scr[2 * pr], k_of(2 * pr), preferred_element_type=F32)
            odd = jnp.dot(ds_scr[2 * pr + 1], k_of(2 * pr + 1), preferred_element_type=F32)
            dq_ref[:, sl] = jnp.where(lane < HEAD_DIM, even, odd).astype(BF16)
            dmk_ref[:, sl] += (lax.dot_general(ds_scr[2 * pr], qms[2 * pr], _TN, preferred_element_type=F32)
                               + lax.dot_general(ds_scr[2 * pr + 1], qms[2 * pr + 1], _TN, preferred_element_type=F32))
            dmv_ref[:, sl] += (lax.dot_general(p_scr[2 * pr], doms[2 * pr], _TN, preferred_element_type=F32)
                               + lax.dot_general(p_scr[2 * pr + 1], doms[2 * pr + 1], _TN, preferred_element_type=F32))

    return pl.pallas_call(
        body, name="cross_bwd", grid=(seq // tq,),
        in_specs=[_row(tq, C_W), _full((N_MEM, C_W)), _full((N_MEM, C_W)), _row(tq, C_W), _row(tq, LANES)],
        out_specs=[_row(tq, C_W), _full((N_MEM, C_W)), _full((N_MEM, C_W))],
        out_shape=[jax.ShapeDtypeStruct((seq, C_W), BF16), jax.ShapeDtypeStruct((N_MEM, C_W), F32),
                   jax.ShapeDtypeStruct((N_MEM, C_W), F32)],
        scratch_shapes=[pltpu.VMEM((C_HEADS, tq, N_MEM), F32)] * 2 + [pltpu.VMEM((C_HEADS, tq, N_MEM), BF16)] * 2,
        compiler_params=_params(dimension_semantics=("arbitrary",)),
    )(q, mk, mv, d_out, stat)


def _per_head(tile, width):
    rows = tile.shape[0]
    lane = lax.broadcasted_iota(jnp.int32, (rows, LANES), 1)
    slabs = []
    for p in range(width // LANES):
        even = jnp.broadcast_to(tile[:, 2 * p:2 * p + 1], (rows, LANES))
        odd = jnp.broadcast_to(tile[:, 2 * p + 1:2 * p + 2], (rows, LANES))
        slabs.append(jnp.where(lane < HEAD_DIM, even, odd))
    return slabs[0] if len(slabs) == 1 else jnp.concatenate(slabs, axis=1)


def _with_delta(lse_tile, prod):
    rows = lse_tile.shape[0]
    lane = lax.broadcasted_iota(jnp.int32, (rows, LANES), 1)
    tile = lse_tile
    for p in range(prod.shape[1] // LANES):
        slab = prod[:, LANES * p:LANES * (p + 1)]
        even = jnp.sum(jnp.where(lane < HEAD_DIM, slab, 0.0), axis=-1, keepdims=True)
        odd = jnp.sum(jnp.where(lane >= HEAD_DIM, slab, 0.0), axis=-1, keepdims=True)
        tile = jnp.where(lane == DELTA_LANE + 2 * p, even, tile)
        tile = jnp.where(lane == DELTA_LANE + 2 * p + 1, odd, tile)
    return tile


def _mid(oa, lse_a, ob, lse_b, oc, lse_c, gate, x, target, w_out_full, post_g, tm=512):
    seq = x.shape[0]
    n_b = B_W // LANES

    def body(oa_ref, la_ref, b1_ref, l1_ref, b4_ref, l4_ref, b16_ref, l16_ref, oc_ref, lc_ref,
             gate_ref, x_ref, t_ref, w_ref, pg_ref,
             dh_ref, dg_ref, doa_ref, sa_ref, dob1_ref, sb1_ref, dob4_ref, sb4_ref, dob16_ref, sb16_ref,
             doc_ref, sc_ref, dw_ref, st_ref, scr_b4, scr_b16, scr_l4, scr_l16, scr_do, scr_sb):
        @pl.when(pl.program_id(0) == 0)
        def _():
            dw_ref[...] = jnp.zeros_like(dw_ref)
            st_ref[...] = jnp.zeros_like(st_ref)

        b1, l1 = b1_ref[...].astype(F32), l1_ref[...]
        b4, l4 = _load_permuted(b4_ref, scr_b4, 4), _load_permuted(l4_ref, scr_l4, 4)
        b16, l16 = _load_permuted(b16_ref, scr_b16, 16), _load_permuted(l16_ref, scr_l16, 16)
        lm = jnp.maximum(jnp.maximum(l1, l4), l16)
        e1, e4, e16 = jnp.exp(l1 - lm), jnp.exp(l4 - lm), jnp.exp(l16 - lm)
        den = e1 + e4 + e16
        lse_b_tile = lm + jnp.log(den)
        ob_v = _per_head(e1 / den, B_W) * b1 + _per_head(e4 / den, B_W) * b4 + _per_head(e16 / den, B_W) * b16
        o_all = jnp.concatenate([oa_ref[...].astype(F32), ob_v, oc_ref[...].astype(F32)], axis=1)
        g = gate_ref[...].astype(F32)
        sig = 1.0 / (1.0 + jnp.exp(-g))
        silu = g * sig
        y = (o_all * silu).astype(BF16)
        w = w_ref[...]
        z = jnp.dot(y, w, preferred_element_type=F32)
        rz = lax.rsqrt(jnp.mean(z * z, axis=-1, keepdims=True) + RMS_EPS)
        hn = z * rz
        pg = pg_ref[...]
        err = (x_ref[...] + hn * pg) - t_ref[...]
        loss = 0.5 * jnp.sum(jnp.mean(err * err, axis=-1, keepdims=True), axis=0, keepdims=True)
        dh = err * (1.0 / D_MODEL)
        dh_ref[...] = dh.astype(BF16)
        st_ref[0:1, :] += jnp.sum(dh * hn, axis=0, keepdims=True)
        st_ref[1:2, :] += jnp.broadcast_to(loss, (1, D_MODEL))
        dhn = dh * pg
        dz = (rz * (dhn - hn * jnp.mean(dhn * hn, axis=-1, keepdims=True))).astype(BF16)
        dy = lax.dot_general(dz, w, _NT, preferred_element_type=F32)
        dw_ref[...] += lax.dot_general(y, dz, _TN, preferred_element_type=F32)
        dg_ref[...] = (dy * o_all * (sig * (1.0 + g * (1.0 - sig)))).astype(BF16)
        d_o = (dy * silu).astype(BF16)
        prod = d_o.astype(F32) * o_all
        doa_ref[...] = d_o[:, 0:A_W]
        sa_ref[...] = _with_delta(la_ref[...], prod[:, 0:A_W])
        doc_ref[...] = d_o[:, A_W + B_W:D_MODEL]
        sc_ref[...] = _with_delta(lc_ref[...], prod[:, A_W + B_W:D_MODEL])
        d_ob = d_o[:, A_W:A_W + B_W]
        stat_b = _with_delta(lse_b_tile, prod[:, A_W:A_W + B_W])
        dob1_ref[...] = d_ob
        sb1_ref[...] = stat_b
        _put(scr_do, d_ob.astype(F32))
        _put(scr_sb, stat_b)
        _store_permuted(scr_do, dob4_ref, 4, BF16)
        _store_permuted(scr_sb, sb4_ref, 4, F32)
        _store_permuted(scr_do, dob16_ref, 16, BF16)
        _store_permuted(scr_sb, sb16_ref, 16, F32)

    p4 = lambda w: _perm_spec(tm, 4, w)
    p16 = lambda w: _perm_spec(tm, 16, w)
    in_specs = [_row(tm, A_W), _row(tm, LANES), _row(tm, B_W), _row(tm, LANES), p4(B_W), p4(LANES), p16(B_W), p16(LANES),
                _row(tm, C_W), _row(tm, LANES), _row(tm, D_MODEL), _row(tm, D_MODEL), _row(tm, D_MODEL),
                _full((D_MODEL, D_MODEL)), _full((1, D_MODEL))]
    sds = jax.ShapeDtypeStruct
    v4 = lambda w, dt: sds((seq // (BLOCK * 4), 4, BLOCK, w), dt)
    v16 = lambda w, dt: sds((seq // (BLOCK * 16), 16, BLOCK, w), dt)
    out_specs = [_row(tm, D_MODEL), _row(tm, D_MODEL), _row(tm, A_W), _row(tm, LANES), _row(tm, B_W), _row(tm, LANES),
                 p4(B_W), p4(LANES), p16(B_W), p16(LANES), _row(tm, C_W), _row(tm, LANES),
                 _full((D_MODEL, D_MODEL)), _full((8, D_MODEL))]
    out_shape = [sds((seq, D_MODEL), BF16), sds((seq, D_MODEL), BF16), sds((seq, A_W), BF16), sds((seq, LANES), F32),
                 sds((seq, B_W), BF16), sds((seq, LANES), F32), v4(B_W, BF16), v4(LANES, F32), v16(B_W, BF16),
                 v16(LANES, F32), sds((seq, C_W), BF16), sds((seq, LANES), F32),
                 sds((D_MODEL, D_MODEL), F32), sds((8, D_MODEL), F32)]
    res = pl.pallas_call(
        body, name="mid", grid=(seq // tm,), in_specs=in_specs, out_specs=out_specs, out_shape=out_shape,
        scratch_shapes=[pltpu.VMEM((n_b, tm, LANES), F32), pltpu.VMEM((n_b, tm, LANES), F32),
                        pltpu.VMEM((1, tm, LANES), F32), pltpu.VMEM((1, tm, LANES), F32),
                        pltpu.VMEM((n_b, tm, LANES), F32), pltpu.VMEM((1, tm, LANES), F32)],
        compiler_params=_params(dimension_semantics=("arbitrary",)),
    )(oa, lse_a, ob[1], lse_b[1], _perm_view(ob[4], 4), _perm_view(lse_b[4], 4), _perm_view(ob[16], 16),
      _perm_view(lse_b[16], 16), oc, lse_c, gate, x, target, w_out_full, post_g)
    dh, d_gate, do_a, st_a, do_b1, st_b1, do_b4, st_b4, do_b16, st_b16, do_c, st_c, d_wout, stats = res
    flat = lambda t: t.reshape(seq, t.shape[-1])
    d_b = {1: (do_b1, st_b1), 4: (flat(do_b4), flat(st_b4)), 16: (flat(do_b16), flat(st_b16))}
    return dh, d_gate, (do_a, st_a), d_b, (do_c, st_c), d_wout, stats


def _inproj_bwd(x, u, dh, pre_g, w_in_full, tabs, dqa, dka, dva, dqkv_b, dqc, dgate, tm=512):
    seq = x.shape[0]
    n_b = B_W // LANES

    def body(x_ref, u_ref, dh_ref, g_ref, w_hbm, c_ref, up_ref, dn_ref, dqa_ref, dka_ref, dva_ref,
             dq1, dk1, dv1, dq4, dk4, dv4, dq16, dk16, dv16, dqc_ref, dg_ref,
             gx_ref, dw_ref, st_ref, scr4, scr16, w_scr, w_sems, dp_ref):
        _stage_w_in(w_hbm, w_scr, w_sems)

        @pl.when(pl.program_id(0) == 0)
        def _():
            st_ref[...] = jnp.zeros_like(st_ref)
            dw_ref[...] = jnp.zeros_like(dw_ref)

        c, up, dn = c_ref[...], -up_ref[...], -dn_ref[...]
        unrot = lambda t: _rotate(t, c, up, dn)
        total = lambda r1, r4, r16: (r1[...].astype(F32) + _load_permuted(r4, scr4, 4)
                                     + _load_permuted(r16, scr16, 16))
        at = lambda piece: slice(*COLS[piece])
        dp_ref[:, at("qa")] = (unrot(dqa_ref[...].astype(F32)) * SCALE).astype(BF16)
        dp_ref[:, at("ka")] = unrot(_per_kv_head(dka_ref[...].astype(F32))).astype(BF16)
        dp_ref[:, at("va")] = _per_kv_head(dva_ref[...].astype(F32)).astype(BF16)
        dp_ref[:, at("ga")] = dg_ref[:, 0:A_W]
        dp_ref[:, at("qb")] = (unrot(total(dq1, dq4, dq16)) * SCALE).astype(BF16)
        dp_ref[:, at("kb")] = unrot(total(dk1, dk4, dk16)).astype(BF16)
        dp_ref[:, at("vb")] = total(dv1, dv4, dv16).astype(BF16)
        dp_ref[:, at("gb")] = dg_ref[:, A_W:A_W + B_W]
        dp_ref[:, at("qc")] = (dqc_ref[...].astype(F32) * SCALE).astype(BF16)
        dp_ref[:, at("gc")] = dg_ref[:, A_W + B_W:D_MODEL]
        du = lax.dot_general(dp_ref[...], w_scr[...], _NT, preferred_element_type=F32)
        res = lax.dot_general(u_ref[...], dp_ref[...], _TN, preferred_element_type=F32)
        for k in range(N_DEV):
            dw_ref[k] += res[:, SHARD_IN * k:SHARD_IN * (k + 1)]
        xv = x_ref[...]
        r = lax.rsqrt(jnp.mean(xv * xv, axis=-1, keepdims=True) + RMS_EPS)
        xh = xv * r
        st_ref[0:1, :] += jnp.sum(du * xh, axis=0, keepdims=True)
        dxh = du * g_ref[...]
        gx_ref[...] = dh_ref[...].astype(F32) + r * (dxh - xh * jnp.mean(dxh * xh, axis=-1, keepdims=True))

    in_specs = ([_row(tm, D_MODEL), _row(tm, D_MODEL), _row(tm, D_MODEL), _full((1, D_MODEL)),
                 pl.BlockSpec(memory_space=pl.ANY),
                 _row(tm, LANES), _row(tm, LANES), _row(tm, LANES), _row(tm, A_W), _row(tm, A_W), _row(tm, A_W)]
                + [_row(tm, B_W)] * 3 + [_perm_spec(tm, 4, B_W)] * 3 + [_perm_spec(tm, 16, B_W)] * 3
                + [_row(tm, C_W), _row(tm, D_MODEL)])
    dw_spec = pl.BlockSpec((N_DEV, D_MODEL, SHARD_IN), lambda i: (0, 0, 0), pipeline_mode=pl.Buffered(1))
    return pl.pallas_call(
        body, name="inproj_bwd", grid=(seq // tm,), in_specs=in_specs,
        out_specs=[_row(tm, D_MODEL), dw_spec, _full((8, D_MODEL))],
        out_shape=[jax.ShapeDtypeStruct((seq, D_MODEL), F32), jax.ShapeDtypeStruct((N_DEV, D_MODEL, SHARD_IN), F32),
                   jax.ShapeDtypeStruct((8, D_MODEL), F32)],
        scratch_shapes=[pltpu.VMEM((n_b, tm, LANES), F32), pltpu.VMEM((n_b, tm, LANES), F32)] + _w_in_scratch()
        + [pltpu.VMEM((tm, D_IN), BF16)],
        compiler_params=_params(dimension_semantics=("arbitrary",)),
    )(x, u, dh, pre_g, w_in_full, *tabs, dqa, dka, dva, *dqkv_b[1], *[_perm_view(t, 4) for t in dqkv_b[4]],
      *[_perm_view(t, 16) for t in dqkv_b[16]], dqc, dgate)


class _ReduceScatter:
    def __init__(self, ins, outs, scratch):
        self.n = n = len(ins)
        self.ins, self.outs = ins, outs
        self.mine, self.got, self.snd, self.rcv = (scratch[n * t:n * (t + 1)] for t in range(4))
        self.load_sems, self.d2d_send, self.d2d_recv, self.ici_send, self.ici_recv = scratch[4 * n:]
        self.pos = _mesh_pos()
        self.pairs = [(a, kk) for kk in (3, 1, 2) for a in range(n)]

    @staticmethod
    def scratch_shapes(shapes):
        return ([pltpu.VMEM((4,) + s, F32) for s in shapes] + [pltpu.VMEM((4,) + s, F32) for s in shapes]
                + [pltpu.VMEM((3,) + s, BF16) for s in shapes] + [pltpu.VMEM((3,) + s, BF16) for s in shapes]
                + [pltpu.SemaphoreType.DMA((len(shapes), 4))] * 5)

    def _chip(self, kk):
        x, y, _ = self.pos
        return (1 - x if kk & 2 else x, 1 - y if kk & 1 else y)

    def _load(self, a, kk):
        block = _dev_index((*self._chip(kk), self.pos[2]))
        return pltpu.make_async_copy(self.ins[a].at[block], self.mine[a].at[kk], self.load_sems.at[a, kk])

    def _swap(self, a, kk):
        x, y, c = self.pos
        return pltpu.make_async_remote_copy(
            src_ref=self.ins[a].at[_dev_index((*self._chip(kk), 1 - c))], dst_ref=self.got[a].at[kk],
            send_sem=self.d2d_send.at[a, kk], recv_sem=self.d2d_recv.at[a, kk],
            device_id=(x, y, 1 - c), device_id_type=MESH_ID)

    def _hop(self, a, kk):
        return pltpu.make_async_remote_copy(
            src_ref=self.snd[a].at[kk - 1], dst_ref=self.rcv[a].at[kk - 1], send_sem=self.ici_send.at[a, kk],
            recv_sem=self.ici_recv.at[a, kk], device_id=(*self._chip(kk), self.pos[2]), device_id_type=MESH_ID)

    def start(self):
        for kk in (3, 1, 2, 0):
            for a in range(self.n):
                self._load(a, kk).start()
                self._swap(a, kk).start()

    def send_chip_sums(self):
        for a, kk in self.pairs:
            self._load(a, kk).wait()
            self._swap(a, kk).wait_recv()
            self.snd[a][kk - 1] = (self.mine[a][kk] + self.got[a][kk]).astype(BF16)
            self._hop(a, kk).start()

    def finish(self):
        for a in range(self.n):
            self._load(a, 0).wait()
            self._swap(a, 0).wait_recv()
            acc = self.mine[a][0] + self.got[a][0]
            for kk in (1, 2, 3):
                self._hop(a, kk).wait_recv()
                acc = acc + self.rcv[a][kk - 1].astype(F32)
            self.outs[a][...] = acc
        for kk in range(4):
            for a in range(self.n):
                self._swap(a, kk).wait_send()
        for a, kk in self.pairs:
            self._hop(a, kk).wait_send()


def _local_step(x, mem, pre_g, w_in, sink, mem_g, w_mem, w_out, post_g, target):
    u, *tabs, w_in_full = _prep(x, pre_g, w_in)
    qa, ka, va, qkv_b, qc, gate, w_mem_all, w_out_all = _inproj(u, w_in_full, tabs, w_mem, w_out)
    w_mem_full = w_mem_all.reshape(D_MODEL, 2 * C_W)
    w_out_full = w_out_all.reshape(D_MODEL, D_MODEL)
    mn, mk, mv = _memkv_fwd(mem, mem_g, w_mem_full)

    a_cfg = dict(dil=1, heads=A_HEADS, max_dist=BLOCK - 1, nq=ATTN_BLOCKS_PER_STEP)
    b_cfgs = {dil: dict(dil=dil, heads=B_HEADS, max_dist=win // dil, nq=ATTN_BLOCKS_PER_STEP)
              for win, dil in B_CONFIGS}
    oa, lse_a = _banded_fwd(qa, ka, va, sink, name="attn_a_fwd", **a_cfg)
    ob, lse_b = {}, {}
    for dil, cfg in b_cfgs.items():
        ob[dil], lse_b[dil] = _banded_fwd(*qkv_b[dil], None, name=f"attn_b{dil}_fwd", **cfg)
    oc, lse_c = _cross_fwd(qc, mk, mv)

    dh, d_gate, d_a, d_b, d_c, d_wout, st_mid = _mid(oa, lse_a, ob, lse_b, oc, lse_c, gate, x, target, w_out_full, post_g)

    dqc, dmk, dmv = _cross_bwd(qc, mk, mv, *d_c)
    d_wmem, st_mem = _memkv_bwd(mem, mem_g, mn, w_mem_full, dmk, dmv)
    bwd_nq = lambda dil: ATTN_BWD_BLOCKS_PER_STEP if dil == 1 else min(dil, ATTN_BWD_BLOCKS_PER_STEP)
    dqkv_b = {dil: _banded_bwd(*qkv_b[dil], *d_b[dil], None, name=f"attn_b{dil}_bwd", **{**cfg, "nq": bwd_nq(dil)})
              for dil, cfg in b_cfgs.items()}
    dqa, dka, dva, dsink, g_wmem, g_wout = _banded_bwd(
        qa, ka, va, *d_a, sink, name="attn_a_bwd", **{**a_cfg, "nq": bwd_nq(1)},
        reduce_scatter=(d_wmem.reshape(N_DEV, SHARD_ROWS, 2 * C_W), d_wout.reshape(N_DEV, SHARD_ROWS, D_MODEL)))

    grad_x, d_win, st_pre = _inproj_bwd(x, u, dh, pre_g, w_in_full, tabs, dqa, dka, dva, dqkv_b, dqc, d_gate)

    dsink_row = jnp.pad(dsink[0:1, :], ((0, 0), (0, D_MODEL - LANES)))
    stats = jnp.concatenate([st_pre[0:1], st_mem[0:1], st_mid[0:1], dsink_row, st_mid[1:2],
                             jnp.zeros((3, D_MODEL), F32)], axis=0)
    return grad_x, d_win, g_wmem, g_wout, stats


def _prep(x, pre_g, w_in, tm=1024):
    seq = x.shape[0]
    n_steps = seq // tm
    parts = 2
    rows = D_MODEL // parts
    relay_at = min(2, n_steps - 1)
    j = jnp.arange(LANES) % HEAD_DIM
    freq = (ROPE_THETA ** (-(2 * (j % (ROT_DIM // 2))).astype(F32) / ROT_DIM))[None, :]
    SIB, NB_X, NB_Y, RELAY, FWD = 0, 1, 2, 3, 4

    def body(x_ref, g_ref, f_ref, win_ref, u_ref, c_ref, up_ref, dn_ref, win_out, win_b,
             send_sems, recv_sems, local_sems):
        step = pl.program_id(0)
        px, py, pc = _mesh_pos()
        me, sibling = (px, py, pc), (px, py, 1 - pc)
        others = lambda core: ((1 - px, py, core), (px, 1 - py, core), (1 - px, 1 - py, core))
        x_nb, y_nb, diag = others(pc)
        relay_from = [x_nb, y_nb]
        relay_to = [y_nb, x_nb]

        def src(a):
            return win_b.at[pl.ds(rows * a, rows)]

        def slot(a, p):
            return win_out.at[_dev_index(p), pl.ds(rows * a, rows)]

        def copy(a, k, block, to, own=False):
            return pltpu.make_async_remote_copy(
                src_ref=src(a) if own else slot(a, block), dst_ref=slot(a, block),
                send_sem=send_sems.at[a, k], recv_sem=recv_sems.at[a, k], device_id=to, device_id_type=MESH_ID)

        def first_sends():
            return [copy(0, NB_X, me, x_nb, own=True), copy(1, NB_Y, me, y_nb, own=True),
                    copy(1, NB_X, me, x_nb, own=True), copy(0, NB_Y, me, y_nb, own=True),
                    copy(0, SIB, me, sibling, own=True), copy(1, SIB, me, sibling, own=True)]

        def relay(a):
            return copy(a, RELAY, relay_from[a], relay_to[a])

        def to_sibling(a, which):
            return copy(a, FWD + which, others(pc)[which], sibling)

        def local(a):
            return pltpu.make_async_copy(src(a), slot(a, me), local_sems.at[a])

        @pl.when(step == 0)
        def _():
            win_b[...] = win_ref[...].astype(BF16)
            for a in range(parts):
                local(a).start()
            for cp in first_sends():
                cp.start()

        @pl.when(step == relay_at)
        def _():
            for a in range(parts):
                copy(a, NB_X + a, relay_from[a], me).wait_recv()
                relay(a).start()
                to_sibling(a, a).start()

        xv = x_ref[...]
        r = lax.rsqrt(jnp.mean(xv * xv, axis=-1, keepdims=True) + RMS_EPS)
        u_ref[...] = ((xv * r) * g_ref[...]).astype(BF16)
        freq = f_ref[...]
        ang_row = lax.broadcasted_iota(jnp.int32, (BLOCK, LANES), 0).astype(F32) * freq
        cos_row, sin_row = jnp.cos(ang_row), jnp.sin(ang_row)
        head_lane = lax.broadcasted_iota(jnp.int32, (BLOCK, LANES), 1) % HEAD_DIM
        half = ROT_DIM // 2
        for blk in range(tm // BLOCK):
            rows_b = slice(BLOCK * blk, BLOCK * (blk + 1))
            ang_0 = (step * tm + BLOCK * blk).astype(F32) * freq
            cos_0, sin_0 = jnp.cos(ang_0), jnp.sin(ang_0)
            cos = cos_0 * cos_row - sin_0 * sin_row
            sin = sin_0 * cos_row + cos_0 * sin_row
            c_ref[rows_b, :] = jnp.where(head_lane < ROT_DIM, cos, 1.0)
            up_ref[rows_b, :] = jnp.where((head_lane >= half) & (head_lane < ROT_DIM), sin, 0.0)
            dn_ref[rows_b, :] = jnp.where(head_lane < half, -sin, 0.0)

        @pl.when(step == n_steps - 1)
        def _():
            copy(1, NB_X, x_nb, me).wait_recv()
            to_sibling(1, 0).start()
            copy(0, NB_Y, y_nb, me).wait_recv()
            to_sibling(0, 1).start()
            for a in range(parts):
                copy(a, RELAY, diag, me).wait_recv()
                to_sibling(a, 2).start()
            for a in range(parts):
                copy(a, SIB, sibling, me).wait_recv()
                for which in range(3):
                    copy(a, FWD + which, others(1 - pc)[which], me).wait_recv()
            for cp in first_sends():
                cp.wait_send()
            for a in range(parts):
                relay(a).wait_send()
                for which in range(3):
                    to_sibling(a, which).wait_send()
                local(a).wait()

    return pl.pallas_call(
        body, name="prep", grid=(n_steps,),
        in_specs=[_row(tm, D_MODEL), _full((1, D_MODEL)), _full((1, LANES)), _full(w_in.shape)],
        out_specs=[_row(tm, D_MODEL), _row(tm, LANES), _row(tm, LANES), _row(tm, LANES),
                   pl.BlockSpec(memory_space=pl.ANY)],
        out_shape=[jax.ShapeDtypeStruct((seq, D_MODEL), BF16)] + [jax.ShapeDtypeStruct((seq, LANES), F32)] * 3
        + [jax.ShapeDtypeStruct((N_DEV,) + w_in.shape, BF16)],
        scratch_shapes=[pltpu.VMEM(w_in.shape, BF16), pltpu.SemaphoreType.DMA((parts, FWD + 3)),
                        pltpu.SemaphoreType.DMA((parts, FWD + 3)), pltpu.SemaphoreType.DMA((parts,))],
        compiler_params=_params(dimension_semantics=("arbitrary",)),
    )(x, pre_g, freq, w_in)


def _exchange_grads(d_win, stats):
    def body(win, st, g_win, r_st, send_sems, recv_sems, local_sem, *scratch):
        exchange = _ReduceScatter((win,), (g_win,), scratch)
        exchange.start()
        pos = _mesh_pos()
        me = _dev_index(pos)
        own = pltpu.make_async_copy(st, r_st.at[me], local_sem)
        own.start()
        copies = []
        for s in range(1, N_DEV):
            peer = _xor_peer(pos, s)
            mk = lambda slot: pltpu.make_async_remote_copy(
                src_ref=st, dst_ref=r_st.at[slot], send_sem=send_sems.at[s], recv_sem=recv_sems.at[s],
                device_id=peer, device_id_type=MESH_ID)
            send, arrival = mk(me), mk(_dev_index(peer))
            send.start()
            copies.append((send, arrival))
        exchange.send_chip_sums()
        exchange.finish()
        for send, arrival in copies:
            arrival.wait_recv()
            send.wait_send()
        own.wait()

    hbm = pl.BlockSpec(memory_space=pl.ANY)
    shard = d_win.shape[1:]
    return pl.pallas_call(
        body, name="exchange_grads", in_specs=[hbm, hbm],
        out_specs=[pl.BlockSpec(memory_space=pltpu.VMEM), hbm],
        out_shape=[jax.ShapeDtypeStruct(shard, F32), jax.ShapeDtypeStruct((N_DEV,) + stats.shape, F32)],
        scratch_shapes=[pltpu.SemaphoreType.DMA((N_DEV,)), pltpu.SemaphoreType.DMA((N_DEV,)), pltpu.SemaphoreType.DMA(())]
        + _ReduceScatter.scratch_shapes([shard]),
        compiler_params=_params(),
    )(d_win, stats)


WEIGHT_ORDER = ("pre_norm", "w_in", "sink_a", "mem_norm", "w_mem_kv", "w_out", "post_norm")


def _adamw_all(grads, r_stats, weights, moments_m, moments_v):
    n = len(WEIGHT_ORDER)
    stat_row = {"pre_norm": 0, "mem_norm": 1, "post_norm": 2, "sink_a": 3}

    def body(*refs):
        gw_in, gw_mem, gw_out, st_ref = refs[0:4]
        w_refs, m_refs, v_refs = (dict(zip(WEIGHT_ORDER, refs[4 + n * t:4 + n * (t + 1)])) for t in range(3))
        loss_ref = refs[4 + 3 * n]
        outs = refs[5 + 3 * n:]
        g_small = st_ref[0]
        for s in range(1, N_DEV):
            g_small = g_small + st_ref[s]
        loss_ref[...] = g_small[4:5, 0:1]
        big = {"w_in": gw_in, "w_mem_kv": gw_mem, "w_out": gw_out}
        for i, name in enumerate(WEIGHT_ORDER):
            if name in big:
                g = big[name][...]
                at = lambda ref: ref[0]
            else:
                width = w_refs[name].shape[-1]
                g = g_small[stat_row[name]:stat_row[name] + 1, 0:width]
                at = lambda ref: ref[...]
            m2 = ADAM_B1 * at(m_refs[name]) + (1.0 - ADAM_B1) * g
            v2 = ADAM_B2 * at(v_refs[name]) + (1.0 - ADAM_B2) * (g * g)
            m_hat = m2 / (1.0 - ADAM_B1 ** ADAM_STEP)
            v_hat = v2 / (1.0 - ADAM_B2 ** ADAM_STEP)
            delta = -ADAM_LR * (m_hat / (jnp.sqrt(v_hat) + ADAM_EPS) + ADAM_WD * at(w_refs[name]))
            for kind, val in enumerate((g, delta, m2, v2)):
                out = outs[kind * n + i]
                if name in big:
                    out[0] = val
                else:
                    out[...] = val

    shapes = [weights[name].shape for name in WEIGHT_ORDER]
    res = pl.pallas_call(
        body, name="adamw_all",
        out_shape=[jax.ShapeDtypeStruct((1, 1), F32)] + [jax.ShapeDtypeStruct(sh, F32) for sh in shapes] * 4,
        compiler_params=_params(),
    )(grads["w_in"], grads["w_mem_kv"], grads["w_out"], r_stats,
      *[weights[k] for k in WEIGHT_ORDER], *[moments_m[k] for k in WEIGHT_ORDER], *[moments_v[k] for k in WEIGHT_ORDER])
    return res[0].reshape(()), res[1:]


def kernel(x, mem, pre_norm, w_in, sink_a, mem_norm, w_mem_kv, w_out, post_norm, loss_target, m_pre_norm, m_w_in, m_sink_a, m_mem_norm, m_w_mem_kv, m_w_out, m_post_norm, v_pre_norm, v_w_in, v_sink_a, v_mem_norm, v_w_mem_kv, v_w_out, v_post_norm):
    sink = jnp.pad(sink_a[0], (0, 8 - A_HEADS))
    grad_x, d_win, g_wmem, g_wout, stats = _local_step(
        x[0], mem[0], pre_norm, w_in[0], sink, mem_norm, w_mem_kv[0], w_out[0], post_norm, loss_target[0])
    g_win, r_stats = _exchange_grads(d_win, stats)
    weights = dict(pre_norm=pre_norm, w_in=w_in, sink_a=sink_a, mem_norm=mem_norm, w_mem_kv=w_mem_kv, w_out=w_out,
                   post_norm=post_norm)
    moments_m = dict(pre_norm=m_pre_norm, w_in=m_w_in, sink_a=m_sink_a, mem_norm=m_mem_norm, w_mem_kv=m_w_mem_kv,
                     w_out=m_w_out, post_norm=m_post_norm)
    moments_v = dict(pre_norm=v_pre_norm, w_in=v_w_in, sink_a=v_sink_a, mem_norm=v_mem_norm, w_mem_kv=v_w_mem_kv,
                     w_out=v_w_out, post_norm=v_post_norm)
    loss, rest = _adamw_all(dict(w_in=g_win, w_mem_kv=g_wmem, w_out=g_wout), r_stats, weights, moments_m, moments_v)
    return (loss, grad_x[None], *rest)
```

```python
import jax
import jax.numpy as jnp
from jax import lax
from jax.experimental import pallas as pl
from jax.experimental.pallas import tpu as pltpu

F32 = jnp.float32
BF16 = jnp.bfloat16

D_MODEL = 1024
HEAD_DIM = 64
ROT_DIM = 16
ROPE_THETA = 500000.0
BLOCK = 128
LANES = 128
N_MEM = 256
RMS_EPS = 1e-6
SCALE = HEAD_DIM ** -0.5
A_HEADS = 6
B_HEADS = 6
C_HEADS = 4
A_W, A_KV_W, B_W, C_W = 384, 128, 384, 256
_IN_PIECES = (("qa", A_W), ("ka", A_KV_W), ("va", A_KV_W), ("ga", A_W), ("qb", B_W), ("kb", B_W), ("vb", B_W),
              ("gb", B_W), ("qc", C_W), ("gc", C_W))
COLS, D_IN = {}, 0
for _name, _width in _IN_PIECES:
    COLS[_name] = (D_IN, D_IN + _width)
    D_IN += _width
N_DEV = 8
SHARD_IN = D_IN // N_DEV
SHARD_ROWS = D_MODEL // N_DEV
B_CONFIGS = ((128, 1), (512, 4), (2048, 16))
DILS = (4, 16)
NEG = -1e30
ATTN_BLOCKS_PER_STEP = 4
ATTN_BWD_BLOCKS_PER_STEP = 8
DELTA_LANE = 64
VMEM_LIMIT = 56 * 1024 * 1024

ADAM_LR, ADAM_B1, ADAM_B2, ADAM_EPS, ADAM_WD, ADAM_STEP = 0.001, 0.9, 0.999, 1e-08, 0.01, 10
MESH_ID = pl.DeviceIdType.MESH


def _params(**kw):
    return pltpu.CompilerParams(vmem_limit_bytes=VMEM_LIMIT, **kw)


def _full(shape):
    n = len(shape)
    return pl.BlockSpec(shape, lambda *_: (0,) * n)


def _row(tm, w):
    return pl.BlockSpec((tm, w), lambda i: (i, 0))


def _mesh_pos():
    return lax.axis_index("x"), lax.axis_index("y"), lax.axis_index("c")


def _dev_index(pos):
    return 4 * pos[0] + 2 * pos[1] + pos[2]


def _xor_peer(pos, s):
    x, y, c = pos
    return (1 - x if s & 4 else x, 1 - y if s & 2 else y, 1 - c if s & 1 else c)


def _perm_view(a, dil):
    return a.reshape(a.shape[0] // (BLOCK * dil), dil, BLOCK, a.shape[1])


def _perm_spec(tm, dil, w):
    chunk = BLOCK * dil
    if tm >= chunk:
        return pl.BlockSpec((tm // chunk, dil, BLOCK, w), lambda i: (i, 0, 0, 0))
    per = chunk // tm
    return pl.BlockSpec((1, dil, tm // dil, w), lambda i: (i // per, 0, i % per, 0))


def _put(scr, val):
    for c in range(val.shape[1] // LANES):
        scr[c] = val[:, LANES * c:LANES * (c + 1)]


def _get(scr):
    n = scr.shape[0]
    return scr[0] if n == 1 else jnp.concatenate([scr[c] for c in range(n)], axis=1)


def _get_class(scr, r, dil):
    n, rows = scr.shape[0], scr.shape[1]
    parts = [scr.at[c][pl.ds(r, rows // dil, stride=dil), :] for c in range(n)]
    return parts[0] if n == 1 else jnp.concatenate(parts, axis=1)


def _store_permuted(scr, out_ref, dil, dtype):
    for r in range(dil):
        out_ref[0, r] = _get_class(scr, r, dil).astype(dtype)


def _load_permuted(in_ref, scr, dil):
    n, rows = scr.shape[0], scr.shape[1]
    for r in range(dil):
        val = in_ref[0, r].astype(F32)
        for c in range(n):
            scr.at[c][pl.ds(r, rows // dil, stride=dil), :] = val[:, LANES * c:LANES * (c + 1)]
    return _get(scr)


def _rotate128(t, c, up, dn):
    half = ROT_DIM // 2
    return t * c + pltpu.roll(t, half, 1) * up + pltpu.roll(t, LANES - half, 1) * dn


def _rotate(t, c, up, dn):
    outs = [_rotate128(t[:, LANES * j:LANES * (j + 1)], c, up, dn) for j in range(t.shape[1] // LANES)]
    return outs[0] if len(outs) == 1 else jnp.concatenate(outs, axis=1)


def _per_query_head(kv):
    lane = lax.broadcasted_iota(jnp.int32, kv.shape, 1)
    other = pltpu.roll(kv, HEAD_DIM, 1)
    return jnp.concatenate([jnp.where(lane < HEAD_DIM, kv, other), kv, jnp.where(lane < HEAD_DIM, other, kv)], axis=1)


def _per_kv_head(d):
    s0, s1, s2 = (d[:, LANES * p:LANES * (p + 1)] for p in range(3))
    lane = lax.broadcasted_iota(jnp.int32, s0.shape, 1)
    return jnp.where(lane < HEAD_DIM, s0 + pltpu.roll(s0, HEAD_DIM, 1) + s1, s1 + s2 + pltpu.roll(s2, HEAD_DIM, 1))


def _w_in_scratch():
    return [pltpu.VMEM((D_MODEL, D_IN), BF16), pltpu.SemaphoreType.DMA((N_DEV,))]


def _stage_w_in(w_hbm, w_scr, sems):
    @pl.when(pl.program_id(0) == 0)
    def _():
        copies = [pltpu.make_async_copy(w_hbm.at[k], w_scr.at[:, pl.ds(SHARD_IN * k, SHARD_IN)], sems.at[k])
                  for k in range(N_DEV)]
        for cp in copies:
            cp.start()
        for cp in copies:
            cp.wait()


def _inproj(u, w_in_full, tabs, w_mem, w_out, tm=1024):
    seq = u.shape[0]
    n_chunk = D_IN // LANES
    n_steps = seq // tm

    def body(u_ref, w_hbm, c_ref, up_ref, dn_ref, wm_ref, wo_ref, qa_ref, ka_ref, va_ref,
             qb1_ref, kb1_ref, vb1_ref, qb4_ref, kb4_ref, vb4_ref, qb16_ref, kb16_ref, vb16_ref,
             qc_ref, gate_ref, wm_all, wo_all, proj, w_scr, w_sems, wm_b, wo_b, send_sems, recv_sems, local_sems):
        step = pl.program_id(0)
        shards, gathered = (wm_b, wo_b), (wm_all, wo_all)

        def gather_copies(arriving):
            pos = _mesh_pos()
            me = _dev_index(pos)
            local = [] if arriving else [
                pltpu.make_async_copy(shards[a], gathered[a].at[me], local_sems.at[a]) for a in range(2)]
            remote = []
            for s in range(1, N_DEV):
                peer = _xor_peer(pos, s)
                for a in range(2):
                    remote.append(pltpu.make_async_remote_copy(
                        src_ref=shards[a], dst_ref=gathered[a].at[_dev_index(peer) if arriving else me],
                        send_sem=send_sems.at[a, s], recv_sem=recv_sems.at[a, s], device_id=peer,
                        device_id_type=MESH_ID))
            return local, remote

        @pl.when(step == 0)
        def _():
            wm_b[...] = wm_ref[...].astype(BF16)
            wo_b[...] = wo_ref[...].astype(BF16)
            local, sends = gather_copies(arriving=False)
            for cp in local + sends:
                cp.start()

        _stage_w_in(w_hbm, w_scr, w_sems)
        u = u_ref[...]
        for n0 in range(0, D_IN, D_MODEL):
            acc = jnp.dot(u, w_scr[:, n0:n0 + D_MODEL], preferred_element_type=F32)
            for c3 in range(D_MODEL // LANES):
                proj[n0 // LANES + c3] = acc[:, LANES * c3:LANES * (c3 + 1)]
        c, up, dn = c_ref[...], up_ref[...], dn_ref[...]

        def chunks_of(piece):
            lo, hi = COLS[piece]
            return range(lo // LANES, hi // LANES)

        def cols(piece, rot=False, scale=None):
            parts = []
            for ch in chunks_of(piece):
                t = proj[ch]
                if rot:
                    t = _rotate128(t, c, up, dn)
                if scale is not None:
                    t = t * scale
                parts.append(t)
            return parts[0] if len(parts) == 1 else jnp.concatenate(parts, axis=1)

        qa_ref[...] = cols("qa", True, SCALE).astype(BF16)
        ka_ref[...] = _per_query_head(cols("ka", True)).astype(BF16)
        va_ref[...] = _per_query_head(cols("va")).astype(BF16)
        gate_ref[:, 0:A_W] = cols("ga").astype(BF16)
        gate_ref[:, A_W:A_W + B_W] = cols("gb").astype(BF16)
        gate_ref[:, A_W + B_W:D_MODEL] = cols("gc").astype(BF16)
        qc_ref[...] = cols("qc", False, SCALE).astype(BF16)
        for ch in chunks_of("qb"):
            proj[ch] = _rotate128(proj[ch], c, up, dn) * SCALE
        for ch in chunks_of("kb"):
            proj[ch] = _rotate128(proj[ch], c, up, dn)
        for piece, nat, p4, p16 in (("qb", qb1_ref, qb4_ref, qb16_ref), ("kb", kb1_ref, kb4_ref, kb16_ref),
                                    ("vb", vb1_ref, vb4_ref, vb16_ref)):
            chunks = chunks_of(piece)
            nat[...] = jnp.concatenate([proj[ch] for ch in chunks], axis=1).astype(BF16)
            for dil, ref in ((4, p4), (16, p16)):
                span = min(tm, BLOCK * dil)
                for cc in range(tm // span):
                    for rr in range(dil):
                        ref[cc, rr] = jnp.concatenate(
                            [proj.at[ch][pl.ds(cc * span + rr, span // dil, stride=dil), :] for ch in chunks],
                            axis=1).astype(BF16)

        @pl.when(step == n_steps - 1)
        def _():
            for cp in gather_copies(arriving=True)[1]:
                cp.wait_recv()
            local, sends = gather_copies(arriving=False)
            for cp in sends:
                cp.wait_send()
            for cp in local:
                cp.wait()

    nat_w = (A_W, A_W, A_W, B_W, B_W, B_W)
    out_specs = [_row(tm, w) for w in nat_w]
    out_shape = [jax.ShapeDtypeStruct((seq, w), BF16) for w in nat_w]
    for dil in DILS:
        out_specs += [_perm_spec(tm, dil, B_W)] * 3
        out_shape += [jax.ShapeDtypeStruct((seq // (BLOCK * dil), dil, BLOCK, B_W), BF16)] * 3
    hbm = pl.BlockSpec(memory_space=pl.ANY)
    out_specs += [_row(tm, C_W), _row(tm, D_MODEL), hbm, hbm]
    out_shape += [jax.ShapeDtypeStruct((seq, C_W), BF16), jax.ShapeDtypeStruct((seq, D_MODEL), BF16),
                  jax.ShapeDtypeStruct((N_DEV,) + w_mem.shape, BF16), jax.ShapeDtypeStruct((N_DEV,) + w_out.shape, BF16)]
    res = pl.pallas_call(
        body, name="inproj", grid=(n_steps,),
        in_specs=[_row(tm, D_MODEL), hbm, _row(tm, LANES), _row(tm, LANES), _row(tm, LANES),
                  _full(w_mem.shape), _full(w_out.shape)],
        out_specs=out_specs, out_shape=out_shape,
        scratch_shapes=[pltpu.VMEM((n_chunk, tm, LANES), F32)] + _w_in_scratch()
        + [pltpu.VMEM(w_mem.shape, BF16), pltpu.VMEM(w_out.shape, BF16), pltpu.SemaphoreType.DMA((2, N_DEV)),
           pltpu.SemaphoreType.DMA((2, N_DEV)), pltpu.SemaphoreType.DMA((2,))],
        compiler_params=_params(dimension_semantics=("arbitrary",)),
    )(u, w_in_full, *tabs, w_mem, w_out)
    qa, ka, va = res[0:3]
    qkv_b = {1: res[3:6], 4: [t.reshape(seq, B_W) for t in res[6:9]], 16: [t.reshape(seq, B_W) for t in res[9:12]]}
    return qa, ka, va, qkv_b, res[12], res[13], res[14], res[15]


def _memkv_fwd(mem, mem_g, w_mem_full):
    def body(mem_ref, g_ref, w_ref, mn_ref, mk_ref, mv_ref):
        mv_ = mem_ref[...]
        r = lax.rsqrt(jnp.mean(mv_ * mv_, axis=-1, keepdims=True) + RMS_EPS)
        mn = ((mv_ * r) * g_ref[...]).astype(BF16)
        mn_ref[...] = mn
        mkv = jnp.dot(mn, w_ref[...], preferred_element_type=F32)
        mk_ref[...] = mkv[:, 0:C_W].astype(BF16)
        mv_ref[...] = mkv[:, C_W:2 * C_W].astype(BF16)

    return pl.pallas_call(
        body, name="memkv_fwd",
        out_shape=[jax.ShapeDtypeStruct((N_MEM, D_MODEL), BF16),
                   jax.ShapeDtypeStruct((N_MEM, C_W), BF16), jax.ShapeDtypeStruct((N_MEM, C_W), BF16)],
        compiler_params=_params(),
    )(mem, mem_g, w_mem_full)


def _memkv_bwd(mem, mem_g, mn, w_mem_full, dmk, dmv):
    def body(mem_ref, g_ref, mn_ref, w_ref, dmk_ref, dmv_ref, dw_ref, st_ref):
        dmkv = jnp.concatenate([dmk_ref[...], dmv_ref[...]], axis=1).astype(BF16)
        dw_ref[...] = lax.dot_general(mn_ref[...], dmkv, (((0,), (0,)), ((), ())), preferred_element_type=F32)
        dmn = lax.dot_general(dmkv, w_ref[...], (((1,), (1,)), ((), ())), preferred_element_type=F32)
        mv_ = mem_ref[...]
        r = lax.rsqrt(jnp.mean(mv_ * mv_, axis=-1, keepdims=True) + RMS_EPS)
        st_ref[...] = jnp.zeros_like(st_ref)
        st_ref[0:1, :] = jnp.sum(dmn * (mv_ * r), axis=0, keepdims=True)

    return pl.pallas_call(
        body, name="memkv_bwd",
        out_shape=[jax.ShapeDtypeStruct((D_MODEL, 2 * C_W), F32), jax.ShapeDtypeStruct((8, D_MODEL), F32)],
        compiler_params=_params(),
    )(mem, mem_g, mn, w_mem_full, dmk, dmv)


def _band_mask(has_prev, max_dist):
    qi = lax.broadcasted_iota(jnp.int32, (BLOCK, 2 * BLOCK), 0)
    kj = lax.broadcasted_iota(jnp.int32, (BLOCK, 2 * BLOCK), 1)
    dist = qi + BLOCK - kj
    return (dist >= 0) & (dist <= max_dist) & ((kj >= BLOCK) | has_prev)


_NT = (((1,), (1,)), ((), ()))
_TN = (((0,), (0,)), ((), ()))


def _head_only(val, h):
    slab = _slabs_of(val)(h)
    lane = lax.broadcasted_iota(jnp.int32, slab.shape, 1)
    keep = (lane < HEAD_DIM) if h % 2 == 0 else (lane >= HEAD_DIM)
    return jnp.where(keep, slab, jnp.zeros((), slab.dtype))


def _slabs_of(val):
    return lambda h: val[:, LANES * (h // 2):LANES * (h // 2 + 1)]


class _BandSteps:
    def __init__(self, seq, dil, nq):
        self.nq, self.rows, self.consecutive = nq, nq * BLOCK, dil == 1
        nb = seq // dil // BLOCK
        if self.consecutive:
            assert nb % nq == 0
            self.outer, self.inner, self.stride = 1, nb // nq, 1
        else:
            assert dil % nq == 0
            self.outer, self.inner, self.stride = dil // nq, nb, dil // nq

    def own(self, w, clamp=False):
        cur = (lambda i: jnp.minimum(i, self.inner - 1)) if clamp else (lambda i: i)
        return pl.BlockSpec((self.rows, w), lambda r, i: (cur(i) * self.stride + r, 0))

    def prev(self, w, clamp=False):
        cur = (lambda i: jnp.minimum(i, self.inner - 1)) if clamp else (lambda i: i)
        if self.consecutive:
            return pl.BlockSpec((BLOCK, w), lambda r, i: (jnp.maximum(cur(i) * self.nq - 1, 0), 0))
        return pl.BlockSpec((self.rows, w), lambda r, i: (jnp.maximum(cur(i) - 1, 0) * self.stride + r, 0))

    def late(self, w):
        return pl.BlockSpec((self.rows, w), lambda r, i: (jnp.maximum(i - 1, 0) * self.stride + r, 0))

    def rows_of(self, j):
        return slice(BLOCK * j, BLOCK * (j + 1))

    def keys(self, p_ref, c_ref, j):
        if not self.consecutive:
            before = p_ref[self.rows_of(j), :]
        elif j == 0:
            before = p_ref[...]
        else:
            before = c_ref[self.rows_of(j - 1), :]
        return jnp.concatenate([before, c_ref[self.rows_of(j), :]], axis=0)

    def has_prev(self, i, j):
        return True if (self.consecutive and j > 0) else (i > 0)


def _banded_fwd(q, k, v, sink, *, dil, heads, max_dist, nq, name):
    seq = q.shape[0]
    qw = kw = heads * HEAD_DIM
    steps = _BandSteps(seq, dil, nq)

    def body(*refs):
        if sink is not None:
            sink_ref, refs = refs[0], refs[1:]
        q_ref, kp_ref, kc_ref, vp_ref, vc_ref, o_ref, lse_ref, s_scr, p_scr = refs
        i = pl.program_id(1)
        lane = lax.broadcasted_iota(jnp.int32, (BLOCK, LANES), 1)
        k_of = [_slabs_of(steps.keys(kp_ref, kc_ref, j)) for j in range(nq)]
        v_of = [_slabs_of(steps.keys(vp_ref, vc_ref, j)) for j in range(nq)]
        for j in range(nq):
            qv = q_ref[steps.rows_of(j), :]
            for h in range(heads):
                s_scr[j * heads + h] = lax.dot_general(_head_only(qv, h), k_of[j](h), _NT, preferred_element_type=F32)
        ls = {}
        for j in range(nq):
            valid = _band_mask(steps.has_prev(i, j), max_dist)
            lse_tile = jnp.zeros((BLOCK, LANES), F32)
            for h in range(heads):
                s = jnp.where(valid, s_scr[j * heads + h], NEG)
                m = jnp.max(s, axis=-1, keepdims=True)
                if sink is not None:
                    sk = sink_ref[h]
                    m = jnp.maximum(m, sk)
                p = jnp.exp(s - m)
                l = jnp.sum(p, axis=-1, keepdims=True)
                if sink is not None:
                    l = l + jnp.exp(sk - m)
                p_scr[j * heads + h] = p.astype(BF16)
                ls[j, h] = l
                lse_tile = jnp.where(lane == h, m + jnp.log(l), lse_tile)
            lse_ref[steps.rows_of(j), :] = lse_tile
        for j in range(nq):
            for pr in range(heads // 2):
                he, ho = 2 * pr, 2 * pr + 1
                even = jnp.dot(p_scr[j * heads + he], v_of[j](he), preferred_element_type=F32) / ls[j, he]
                odd = jnp.dot(p_scr[j * heads + ho], v_of[j](ho), preferred_element_type=F32) / ls[j, ho]
                o_ref[steps.rows_of(j), LANES * pr:LANES * (pr + 1)] = jnp.where(lane < HEAD_DIM, even, odd).astype(BF16)

    in_specs = [steps.own(qw), steps.prev(kw), steps.own(kw), steps.prev(kw), steps.own(kw)]
    args = [q, k, k, v, v]
    if sink is not None:
        in_specs = [pl.BlockSpec(memory_space=pltpu.SMEM)] + in_specs
        args = [sink] + args
    return pl.pallas_call(
        body, name=name, grid=(steps.outer, steps.inner), in_specs=in_specs,
        out_specs=[steps.own(qw), steps.own(LANES)],
        out_shape=[jax.ShapeDtypeStruct((seq, qw), BF16), jax.ShapeDtypeStruct((seq, LANES), F32)],
        scratch_shapes=[pltpu.VMEM((nq * heads, BLOCK, 2 * BLOCK), F32), pltpu.VMEM((nq * heads, BLOCK, 2 * BLOCK), BF16)],
        compiler_params=_params(dimension_semantics=("arbitrary", "arbitrary")),
    )(*args)


def _banded_bwd(q, k, v, d_out, stat, sink, *, dil, heads, max_dist, nq, name, reduce_scatter=()):
    seq = q.shape[0]
    qw = kw = heads * HEAD_DIM
    steps = _BandSteps(seq, dil, nq)
    n_rs = len(reduce_scatter)
    n_in = 7 + n_rs
    n_flat = steps.outer * (steps.inner + 1)

    def body(*refs):
        refs = list(refs)
        sink_ref = refs.pop(0) if sink is not None else None
        (q_ref, kp_ref, kc_ref, vp_ref, vc_ref, do_ref, st_ref), partials = refs[:7], refs[7:n_in]
        refs = refs[n_in:]
        dsink_ref = refs.pop(0) if sink is not None else None
        (dq_ref, dk_ref, dv_ref), sums = refs[:3], refs[3:3 + n_rs]
        kcar, vcar, s_scr, dp_scr, p_scr, ds_scr = refs[3 + n_rs:9 + n_rs]
        r, i = pl.program_id(0), pl.program_id(1)
        if n_rs:
            exchange = _ReduceScatter(tuple(partials), tuple(sums), refs[9 + n_rs:])
            flat = r * (steps.inner + 1) + i

            @pl.when(flat == 0)
            def _():
                exchange.start()

            @pl.when(flat == min(2, n_flat - 1))
            def _():
                exchange.send_chip_sums()

        @pl.when(i == 0)
        def _():
            kcar[...] = jnp.zeros_like(kcar)
            vcar[...] = jnp.zeros_like(vcar)

        if sink is not None:
            @pl.when((i == 0) & (r == 0))
            def _():
                dsink_ref[...] = jnp.zeros_like(dsink_ref)

        @pl.when(i < steps.inner)
        def _():
            lane = lax.broadcasted_iota(jnp.int32, (1, LANES), 1)
            lane_q = lax.broadcasted_iota(jnp.int32, (BLOCK, LANES), 1)
            k_of = [_slabs_of(steps.keys(kp_ref, kc_ref, j)) for j in range(nq)]
            v_of = [_slabs_of(steps.keys(vp_ref, vc_ref, j)) for j in range(nq)]
            qms, doms = {}, {}
            for j in range(nq):
                qv, dov = q_ref[steps.rows_of(j), :], do_ref[steps.rows_of(j), :]
                for h in range(heads):
                    qms[j, h], doms[j, h] = _head_only(qv, h), _head_only(dov, h)
                    s_scr[j * heads + h] = lax.dot_general(qms[j, h], k_of[j](h), _NT, preferred_element_type=F32)
                    dp_scr[j * heads + h] = lax.dot_general(doms[j, h], v_of[j](h), _NT, preferred_element_type=F32)
            dsink_row = jnp.zeros((1, LANES), F32)
            if sink is not None:
                sink_row = jnp.zeros((1, LANES), F32)
                for h in range(heads):
                    sink_row = jnp.where(lane == h, sink_ref[h], sink_row)
            for j in range(nq):
                st = st_ref[steps.rows_of(j), :]
                valid = _band_mask(steps.has_prev(i, j), max_dist)
                for h in range(heads):
                    lse_h = st[:, h:h + 1]
                    delta = st[:, DELTA_LANE + h:DELTA_LANE + h + 1]
                    p = jnp.where(valid, jnp.exp(s_scr[j * heads + h] - lse_h), 0.0)
                    p_scr[j * heads + h] = p.astype(BF16)
                    ds_scr[j * heads + h] = (p * (dp_scr[j * heads + h] - delta)).astype(BF16)
                if sink is not None:
                    term = -jnp.exp(sink_row - st) * pltpu.roll(st, LANES - DELTA_LANE, 1)
                    dsink_row = dsink_row + jnp.sum(jnp.where(lane_q < heads, term, 0.0), axis=0, keepdims=True)
            for j in range(nq):
                for pr in range(heads // 2):
                    he, ho = 2 * pr, 2 * pr + 1
                    even = jnp.dot(ds_scr[j * heads + he], k_of[j](he), preferred_element_type=F32)
                    odd = jnp.dot(ds_scr[j * heads + ho], k_of[j](ho), preferred_element_type=F32)
                    dq_ref[steps.rows_of(j), LANES * pr:LANES * (pr + 1)] = (
                        jnp.where(lane_q < HEAD_DIM, even, odd).astype(BF16))
            if steps.consecutive:
                dk_ref[...] = kcar[...].astype(BF16)
                dv_ref[...] = vcar[...].astype(BF16)
            for j in range(nq):
                for slab in range(kw // LANES):
                    he, ho = j * heads + 2 * slab, j * heads + 2 * slab + 1
                    dk_j = (lax.dot_general(ds_scr[he], qms[j, 2 * slab], _TN, preferred_element_type=F32)
                            + lax.dot_general(ds_scr[ho], qms[j, 2 * slab + 1], _TN, preferred_element_type=F32))
                    dv_j = (lax.dot_general(p_scr[he], doms[j, 2 * slab], _TN, preferred_element_type=F32)
                            + lax.dot_general(p_scr[ho], doms[j, 2 * slab + 1], _TN, preferred_element_type=F32))
                    sl = slice(LANES * slab, LANES * (slab + 1))
                    own_rows = steps.rows_of(j)
                    if not steps.consecutive:
                        dk_ref[own_rows, sl] = (kcar[own_rows, sl] + dk_j[0:BLOCK]).astype(BF16)
                        dv_ref[own_rows, sl] = (vcar[own_rows, sl] + dv_j[0:BLOCK]).astype(BF16)
                    elif j == 0:
                        last = steps.rows_of(nq - 1)
                        dk_ref[last, sl] = (kcar[last, sl] + dk_j[0:BLOCK]).astype(BF16)
                        dv_ref[last, sl] = (vcar[last, sl] + dv_j[0:BLOCK]).astype(BF16)
                    else:
                        before = steps.rows_of(j - 1)
                        kcar[before, sl] += dk_j[0:BLOCK]
                        vcar[before, sl] += dv_j[0:BLOCK]
                    kcar[own_rows, sl] = dk_j[BLOCK:2 * BLOCK]
                    vcar[own_rows, sl] = dv_j[BLOCK:2 * BLOCK]
            if sink is not None:
                dsink_ref[0:1, :] += dsink_row

        @pl.when(i == steps.inner)
        def _():
            dk_ref[...] = kcar[...].astype(BF16)
            dv_ref[...] = vcar[...].astype(BF16)

        if n_rs:
            @pl.when(flat == n_flat - 1)
            def _():
                exchange.finish()

    own, prev = (lambda w: steps.own(w, clamp=True)), (lambda w: steps.prev(w, clamp=True))
    rs_shapes = [t.shape[1:] for t in reduce_scatter]
    in_specs = ([own(qw), prev(kw), own(kw), prev(kw), own(kw), own(qw), own(LANES)]
                + [pl.BlockSpec(memory_space=pl.ANY)] * n_rs)
    args = [q, k, k, v, v, d_out, stat, *reduce_scatter]
    out_specs = [own(qw), steps.late(kw), steps.late(kw)] + [_full(s) for s in rs_shapes]
    out_shape = [jax.ShapeDtypeStruct((seq, qw), BF16), jax.ShapeDtypeStruct((seq, kw), BF16),
                 jax.ShapeDtypeStruct((seq, kw), BF16)] + [jax.ShapeDtypeStruct(s, F32) for s in rs_shapes]
    if sink is not None:
        in_specs = [pl.BlockSpec(memory_space=pltpu.SMEM)] + in_specs
        args = [sink] + args
        out_specs = [_full((8, LANES))] + out_specs
        out_shape = [jax.ShapeDtypeStruct((8, LANES), F32)] + out_shape
    n_hb = nq * heads
    res = pl.pallas_call(
        body, name=name, grid=(steps.outer, steps.inner + 1), in_specs=in_specs, out_specs=out_specs,
        out_shape=out_shape,
        scratch_shapes=[pltpu.VMEM((steps.rows, kw), F32), pltpu.VMEM((steps.rows, kw), F32)]
        + [pltpu.VMEM((n_hb, BLOCK, 2 * BLOCK), F32)] * 2 + [pltpu.VMEM((n_hb, BLOCK, 2 * BLOCK), BF16)] * 2
        + (_ReduceScatter.scratch_shapes(rs_shapes) if n_rs else []),
        compiler_params=_params(dimension_semantics=("arbitrary", "arbitrary")),
    )(*args)
    if sink is not None:
        return (*res[1:4], res[0], *res[4:])
    return res


def _cross_fwd(q, mk, mv, tq=1024):
    seq = q.shape[0]

    def body(q_ref, mk_ref, mv_ref, o_ref, lse_ref, s_scr, p_scr):
        qv = q_ref[...]
        k_of, v_of = _slabs_of(mk_ref[...]), _slabs_of(mv_ref[...])
        lane = lax.broadcasted_iota(jnp.int32, (tq, LANES), 1)
        lse_tile = jnp.zeros((tq, LANES), F32)
        for h in range(C_HEADS):
            s_scr[h] = lax.dot_general(_head_only(qv, h), k_of(h), _NT, preferred_element_type=F32)
        ls = []
        for h in range(C_HEADS):
            s = s_scr[h]
            m = jnp.max(s, axis=-1, keepdims=True)
            p = jnp.exp(s - m)
            l = jnp.sum(p, axis=-1, keepdims=True)
            p_scr[h] = p.astype(BF16)
            ls.append(l)
            lse_tile = jnp.where(lane == h, m + jnp.log(l), lse_tile)
        for pr in range(C_HEADS // 2):
            even = jnp.dot(p_scr[2 * pr], v_of(2 * pr), preferred_element_type=F32) / ls[2 * pr]
            odd = jnp.dot(p_scr[2 * pr + 1], v_of(2 * pr + 1), preferred_element_type=F32) / ls[2 * pr + 1]
            o_ref[:, LANES * pr:LANES * (pr + 1)] = jnp.where(lane < HEAD_DIM, even, odd).astype(BF16)
        lse_ref[...] = lse_tile

    return pl.pallas_call(
        body, name="cross_fwd", grid=(seq // tq,),
        in_specs=[_row(tq, C_W), _full((N_MEM, C_W)), _full((N_MEM, C_W))],
        out_specs=[_row(tq, C_W), _row(tq, LANES)],
        out_shape=[jax.ShapeDtypeStruct((seq, C_W), BF16), jax.ShapeDtypeStruct((seq, LANES), F32)],
        scratch_shapes=[pltpu.VMEM((C_HEADS, tq, N_MEM), F32), pltpu.VMEM((C_HEADS, tq, N_MEM), BF16)],
        compiler_params=_params(dimension_semantics=("arbitrary",)),
    )(q, mk, mv)


def _cross_bwd(q, mk, mv, d_out, stat, tq=1024):
    seq = q.shape[0]

    def body(q_ref, mk_ref, mv_ref, do_ref, st_ref, dq_ref, dmk_ref, dmv_ref, s_scr, dp_scr, p_scr, ds_scr):
        @pl.when(pl.program_id(0) == 0)
        def _():
            dmk_ref[...] = jnp.zeros_like(dmk_ref)
            dmv_ref[...] = jnp.zeros_like(dmv_ref)

        qv, dov, st = q_ref[...], do_ref[...], st_ref[...]
        k_of, v_of = _slabs_of(mk_ref[...]), _slabs_of(mv_ref[...])
        qms = [_head_only(qv, h) for h in range(C_HEADS)]
        doms = [_head_only(dov, h) for h in range(C_HEADS)]
        for h in range(C_HEADS):
            s_scr[h] = lax.dot_general(qms[h], k_of(h), _NT, preferred_element_type=F32)
            dp_scr[h] = lax.dot_general(doms[h], v_of(h), _NT, preferred_element_type=F32)
        for h in range(C_HEADS):
            p = jnp.exp(s_scr[h] - st[:, h:h + 1])
            p_scr[h] = p.astype(BF16)
            ds_scr[h] = (p * (dp_scr[h] - st[:, DELTA_LANE + h:DELTA_LANE + h + 1])).astype(BF16)
        lane = lax.broadcasted_iota(jnp.int32, (tq, LANES), 1)
        for pr in range(C_HEADS // 2):
            sl = slice(LANES * pr, LANES * (pr + 1))
            even = jnp.dot(ds_scr[2 * pr], k_of(2 * pr), preferred_element_type=F32)
            odd = jnp.dot(ds_scr[2 * pr + 1], k_of(2 * pr + 1), preferred_element_type=F32)
            dq_ref[:, sl] = jnp.where(lane < HEAD_DIM, even, odd).astype(BF16)
            dmk_ref[:, sl] += (lax.dot_general(ds_scr[2 * pr], qms[2 * pr], _TN, preferred_element_type=F32)
                               + lax.dot_general(ds_scr[2 * pr + 1], qms[2 * pr + 1], _TN, preferred_element_type=F32))
            dmv_ref[:, sl] += (lax.dot_general(p_scr[2 * pr], doms[2 * pr], _TN, preferred_element_type=F32)
                               + lax.dot_general(p_scr[2 * pr + 1], doms[2 * pr + 1], _TN, preferred_element_type=F32))

    return pl.pallas_call(
        body, name="cross_bwd", grid=(seq // tq,),
        in_specs=[_row(tq, C_W), _full((N_MEM, C_W)), _full((N_MEM, C_W)), _row(tq, C_W), _row(tq, LANES)],
        out_specs=[_row(tq, C_W), _full((N_MEM, C_W)), _full((N_MEM, C_W))],
        out_shape=[jax.ShapeDtypeStruct((seq, C_W), BF16), jax.ShapeDtypeStruct((N_MEM, C_W), F32),
                   jax.ShapeDtypeStruct((N_MEM, C_W), F32)],
        scratch_shapes=[pltpu.VMEM((C_HEADS, tq, N_MEM), F32)] * 2 + [pltpu.VMEM((C_HEADS, tq, N_MEM), BF16)] * 2,
        compiler_params=_params(dimension_semantics=("arbitrary",)),
    )(q, mk, mv, d_out, stat)


def _per_head(tile, width):
    rows = tile.shape[0]
    lane = lax.broadcasted_iota(jnp.int32, (rows, LANES), 1)
    slabs = []
    for p in range(width // LANES):
        even = jnp.broadcast_to(tile[:, 2 * p:2 * p + 1], (rows, LANES))
        odd = jnp.broadcast_to(tile[:, 2 * p + 1:2 * p + 2], (rows, LANES))
        slabs.append(jnp.where(lane < HEAD_DIM, even, odd))
    return slabs[0] if len(slabs) == 1 else jnp.concatenate(slabs, axis=1)


def _with_delta(lse_tile, prod):
    rows = lse_tile.shape[0]
    lane = lax.broadcasted_iota(jnp.int32, (rows, LANES), 1)
    tile = lse_tile
    for p in range(prod.shape[1] // LANES):
        slab = prod[:, LANES * p:LANES * (p + 1)]
        even = jnp.sum(jnp.where(lane < HEAD_DIM, slab, 0.0), axis=-1, keepdims=True)
        odd = jnp.sum(jnp.where(lane >= HEAD_DIM, slab, 0.0), axis=-1, keepdims=True)
        tile = jnp.where(lane == DELTA_LANE + 2 * p, even, tile)
        tile = jnp.where(lane == DELTA_LANE + 2 * p + 1, odd, tile)
    return tile


def _mid(oa, lse_a, ob, lse_b, oc, lse_c, gate, x, target, w_out_full, post_g, tm=512):
    seq = x.shape[0]
    n_b = B_W // LANES

    def body(oa_ref, la_ref, b1_ref, l1_ref, b4_ref, l4_ref, b16_ref, l16_ref, oc_ref, lc_ref,
             gate_ref, x_ref, t_ref, w_ref, pg_ref,
             dh_ref, dg_ref, doa_ref, sa_ref, dob1_ref, sb1_ref, dob4_ref, sb4_ref, dob16_ref, sb16_ref,
             doc_ref, sc_ref, dw_ref, st_ref, scr_b4, scr_b16, scr_l4, scr_l16, scr_do, scr_sb):
        @pl.when(pl.program_id(0) == 0)
        def _():
            dw_ref[...] = jnp.zeros_like(dw_ref)
            st_ref[...] = jnp.zeros_like(st_ref)

        b1, l1 = b1_ref[...].astype(F32), l1_ref[...]
        b4, l4 = _load_permuted(b4_ref, scr_b4, 4), _load_permuted(l4_ref, scr_l4, 4)
        b16, l16 = _load_permuted(b16_ref, scr_b16, 16), _load_permuted(l16_ref, scr_l16, 16)
        lm = jnp.maximum(jnp.maximum(l1, l4), l16)
        e1, e4, e16 = jnp.exp(l1 - lm), jnp.exp(l4 - lm), jnp.exp(l16 - lm)
        den = e1 + e4 + e16
        lse_b_tile = lm + jnp.log(den)
        ob_v = _per_head(e1 / den, B_W) * b1 + _per_head(e4 / den, B_W) * b4 + _per_head(e16 / den, B_W) * b16
        o_all = jnp.concatenate([oa_ref[...].astype(F32), ob_v, oc_ref[...].astype(F32)], axis=1)
        g = gate_ref[...].astype(F32)
        sig = 1.0 / (1.0 + jnp.exp(-g))
        silu = g * sig
        y = (o_all * silu).astype(BF16)
        w = w_ref[...]
        z = jnp.dot(y, w, preferred_element_type=F32)
        rz = lax.rsqrt(jnp.mean(z * z, axis=-1, keepdims=True) + RMS_EPS)
        hn = z * rz
        pg = pg_ref[...]
        err = (x_ref[...] + hn * pg) - t_ref[...]
        loss = 0.5 * jnp.sum(jnp.mean(err * err, axis=-1, keepdims=True), axis=0, keepdims=True)
        dh = err * (1.0 / D_MODEL)
        dh_ref[...] = dh.astype(BF16)
        st_ref[0:1, :] += jnp.sum(dh * hn, axis=0, keepdims=True)
        st_ref[1:2, :] += jnp.broadcast_to(loss, (1, D_MODEL))
        dhn = dh * pg
        dz = (rz * (dhn - hn * jnp.mean(dhn * hn, axis=-1, keepdims=True))).astype(BF16)
        dy = lax.dot_general(dz, w, _NT, preferred_element_type=F32)
        dw_ref[...] += lax.dot_general(y, dz, _TN, preferred_element_type=F32)
        dg_ref[...] = (dy * o_all * (sig * (1.0 + g * (1.0 - sig)))).astype(BF16)
        d_o = (dy * silu).astype(BF16)
        prod = d_o.astype(F32) * o_all
        doa_ref[...] = d_o[:, 0:A_W]
        sa_ref[...] = _with_delta(la_ref[...], prod[:, 0:A_W])
        doc_ref[...] = d_o[:, A_W + B_W:D_MODEL]
        sc_ref[...] = _with_delta(lc_ref[...], prod[:, A_W + B_W:D_MODEL])
        d_ob = d_o[:, A_W:A_W + B_W]
        stat_b = _with_delta(lse_b_tile, prod[:, A_W:A_W + B_W])
        dob1_ref[...] = d_ob
        sb1_ref[...] = stat_b
        _put(scr_do, d_ob.astype(F32))
        _put(scr_sb, stat_b)
        _store_permuted(scr_do, dob4_ref, 4, BF16)
        _store_permuted(scr_sb, sb4_ref, 4, F32)
        _store_permuted(scr_do, dob16_ref, 16, BF16)
        _store_permuted(scr_sb, sb16_ref, 16, F32)

    p4 = lambda w: _perm_spec(tm, 4, w)
    p16 = lambda w: _perm_spec(tm, 16, w)
    in_specs = [_row(tm, A_W), _row(tm, LANES), _row(tm, B_W), _row(tm, LANES), p4(B_W), p4(LANES), p16(B_W), p16(LANES),
                _row(tm, C_W), _row(tm, LANES), _row(tm, D_MODEL), _row(tm, D_MODEL), _row(tm, D_MODEL),
                _full((D_MODEL, D_MODEL)), _full((1, D_MODEL))]
    sds = jax.ShapeDtypeStruct
    v4 = lambda w, dt: sds((seq // (BLOCK * 4), 4, BLOCK, w), dt)
    v16 = lambda w, dt: sds((seq // (BLOCK * 16), 16, BLOCK, w), dt)
    out_specs = [_row(tm, D_MODEL), _row(tm, D_MODEL), _row(tm, A_W), _row(tm, LANES), _row(tm, B_W), _row(tm, LANES),
                 p4(B_W), p4(LANES), p16(B_W), p16(LANES), _row(tm, C_W), _row(tm, LANES),
                 _full((D_MODEL, D_MODEL)), _full((8, D_MODEL))]
    out_shape = [sds((seq, D_MODEL), BF16), sds((seq, D_MODEL), BF16), sds((seq, A_W), BF16), sds((seq, LANES), F32),
                 sds((seq, B_W), BF16), sds((seq, LANES), F32), v4(B_W, BF16), v4(LANES, F32), v16(B_W, BF16),
                 v16(LANES, F32), sds((seq, C_W), BF16), sds((seq, LANES), F32),
                 sds((D_MODEL, D_MODEL), F32), sds((8, D_MODEL), F32)]
    res = pl.pallas_call(
        body, name="mid", grid=(seq // tm,), in_specs=in_specs, out_specs=out_specs, out_shape=out_shape,
        scratch_shapes=[pltpu.VMEM((n_b, tm, LANES), F32), pltpu.VMEM((n_b, tm, LANES), F32),
                        pltpu.VMEM((1, tm, LANES), F32), pltpu.VMEM((1, tm, LANES), F32),
                        pltpu.VMEM((n_b, tm, LANES), F32), pltpu.VMEM((1, tm, LANES), F32)],
        compiler_params=_params(dimension_semantics=("arbitrary",)),
    )(oa, lse_a, ob[1], lse_b[1], _perm_view(ob[4], 4), _perm_view(lse_b[4], 4), _perm_view(ob[16], 16),
      _perm_view(lse_b[16], 16), oc, lse_c, gate, x, target, w_out_full, post_g)
    dh, d_gate, do_a, st_a, do_b1, st_b1, do_b4, st_b4, do_b16, st_b16, do_c, st_c, d_wout, stats = res
    flat = lambda t: t.reshape(seq, t.shape[-1])
    d_b = {1: (do_b1, st_b1), 4: (flat(do_b4), flat(st_b4)), 16: (flat(do_b16), flat(st_b16))}
    return dh, d_gate, (do_a, st_a), d_b, (do_c, st_c), d_wout, stats


def _inproj_bwd(x, u, dh, pre_g, w_in_full, tabs, dqa, dka, dva, dqkv_b, dqc, dgate, tm=512):
    seq = x.shape[0]
    n_b = B_W // LANES

    def body(x_ref, u_ref, dh_ref, g_ref, w_hbm, c_ref, up_ref, dn_ref, dqa_ref, dka_ref, dva_ref,
             dq1, dk1, dv1, dq4, dk4, dv4, dq16, dk16, dv16, dqc_ref, dg_ref,
             gx_ref, dw_ref, st_ref, scr4, scr16, w_scr, w_sems, dp_ref):
        _stage_w_in(w_hbm, w_scr, w_sems)

        @pl.when(pl.program_id(0) == 0)
        def _():
            st_ref[...] = jnp.zeros_like(st_ref)
            dw_ref[...] = jnp.zeros_like(dw_ref)

        c, up, dn = c_ref[...], -up_ref[...], -dn_ref[...]
        unrot = lambda t: _rotate(t, c, up, dn)
        total = lambda r1, r4, r16: (r1[...].astype(F32) + _load_permuted(r4, scr4, 4)
                                     + _load_permuted(r16, scr16, 16))
        at = lambda piece: slice(*COLS[piece])
        dp_ref[:, at("qa")] = (unrot(dqa_ref[...].astype(F32)) * SCALE).astype(BF16)
        dp_ref[:, at("ka")] = unrot(_per_kv_head(dka_ref[...].astype(F32))).astype(BF16)
        dp_ref[:, at("va")] = _per_kv_head(dva_ref[...].astype(F32)).astype(BF16)
        dp_ref[:, at("ga")] = dg_ref[:, 0:A_W]
        dp_ref[:, at("qb")] = (unrot(total(dq1, dq4, dq16)) * SCALE).astype(BF16)
        dp_ref[:, at("kb")] = unrot(total(dk1, dk4, dk16)).astype(BF16)
        dp_ref[:, at("vb")] = total(dv1, dv4, dv16).astype(BF16)
        dp_ref[:, at("gb")] = dg_ref[:, A_W:A_W + B_W]
        dp_ref[:, at("qc")] = (dqc_ref[...].astype(F32) * SCALE).astype(BF16)
        dp_ref[:, at("gc")] = dg_ref[:, A_W + B_W:D_MODEL]
        du = lax.dot_general(dp_ref[...], w_scr[...], _NT, preferred_element_type=F32)
        res = lax.dot_general(u_ref[...], dp_ref[...], _TN, preferred_element_type=F32)
        for k in range(N_DEV):
            dw_ref[k] += res[:, SHARD_IN * k:SHARD_IN * (k + 1)]
        xv = x_ref[...]
        r = lax.rsqrt(jnp.mean(xv * xv, axis=-1, keepdims=True) + RMS_EPS)
        xh = xv * r
        st_ref[0:1, :] += jnp.sum(du * xh, axis=0, keepdims=True)
        dxh = du * g_ref[...]
        gx_ref[...] = dh_ref[...].astype(F32) + r * (dxh - xh * jnp.mean(dxh * xh, axis=-1, keepdims=True))

    in_specs = ([_row(tm, D_MODEL), _row(tm, D_MODEL), _row(tm, D_MODEL), _full((1, D_MODEL)),
                 pl.BlockSpec(memory_space=pl.ANY),
                 _row(tm, LANES), _row(tm, LANES), _row(tm, LANES), _row(tm, A_W), _row(tm, A_W), _row(tm, A_W)]
                + [_row(tm, B_W)] * 3 + [_perm_spec(tm, 4, B_W)] * 3 + [_perm_spec(tm, 16, B_W)] * 3
                + [_row(tm, C_W), _row(tm, D_MODEL)])
    dw_spec = pl.BlockSpec((N_DEV, D_MODEL, SHARD_IN), lambda i: (0, 0, 0), pipeline_mode=pl.Buffered(1))
    return pl.pallas_call(
        body, name="inproj_bwd", grid=(seq // tm,), in_specs=in_specs,
        out_specs=[_row(tm, D_MODEL), dw_spec, _full((8, D_MODEL))],
        out_shape=[jax.ShapeDtypeStruct((seq, D_MODEL), F32), jax.ShapeDtypeStruct((N_DEV, D_MODEL, SHARD_IN), F32),
                   jax.ShapeDtypeStruct((8, D_MODEL), F32)],
        scratch_shapes=[pltpu.VMEM((n_b, tm, LANES), F32), pltpu.VMEM((n_b, tm, LANES), F32)] + _w_in_scratch()
        + [pltpu.VMEM((tm, D_IN), BF16)],
        compiler_params=_params(dimension_semantics=("arbitrary",)),
    )(x, u, dh, pre_g, w_in_full, *tabs, dqa, dka, dva, *dqkv_b[1], *[_perm_view(t, 4) for t in dqkv_b[4]],
      *[_perm_view(t, 16) for t in dqkv_b[16]], dqc, dgate)


class _ReduceScatter:
    def __init__(self, ins, outs, scratch):
        self.n = n = len(ins)
        self.ins, self.outs = ins, outs
        self.mine, self.got, self.snd, self.rcv = (scratch[n * t:n * (t + 1)] for t in range(4))
        self.load_sems, self.d2d_send, self.d2d_recv, self.ici_send, self.ici_recv = scratch[4 * n:]
        self.pos = _mesh_pos()
        self.pairs = [(a, kk) for kk in (3, 1, 2) for a in range(n)]

    @staticmethod
    def scratch_shapes(shapes):
        return ([pltpu.VMEM((4,) + s, F32) for s in shapes] + [pltpu.VMEM((4,) + s, F32) for s in shapes]
                + [pltpu.VMEM((3,) + s, BF16) for s in shapes] + [pltpu.VMEM((3,) + s, BF16) for s in shapes]
                + [pltpu.SemaphoreType.DMA((len(shapes), 4))] * 5)

    def _chip(self, kk):
        x, y, _ = self.pos
        return (1 - x if kk & 2 else x, 1 - y if kk & 1 else y)

    def _load(self, a, kk):
        block = _dev_index((*self._chip(kk), self.pos[2]))
        return pltpu.make_async_copy(self.ins[a].at[block], self.mine[a].at[kk], self.load_sems.at[a, kk])

    def _swap(self, a, kk):
        x, y, c = self.pos
        return pltpu.make_async_remote_copy(
            src_ref=self.ins[a].at[_dev_index((*self._chip(kk), 1 - c))], dst_ref=self.got[a].at[kk],
            send_sem=self.d2d_send.at[a, kk], recv_sem=self.d2d_recv.at[a, kk],
            device_id=(x, y, 1 - c), device_id_type=MESH_ID)

    def _hop(self, a, kk):
        return pltpu.make_async_remote_copy(
            src_ref=self.snd[a].at[kk - 1], dst_ref=self.rcv[a].at[kk - 1], send_sem=self.ici_send.at[a, kk],
            recv_sem=self.ici_recv.at[a, kk], device_id=(*self._chip(kk), self.pos[2]), device_id_type=MESH_ID)

    def start(self):
        for kk in (3, 1, 2, 0):
            for a in range(self.n):
                self._load(a, kk).start()
                self._swap(a, kk).start()

    def send_chip_sums(self):
        for a, kk in self.pairs:
            self._load(a, kk).wait()
            self._swap(a, kk).wait_recv()
            self.snd[a][kk - 1] = (self.mine[a][kk] + self.got[a][kk]).astype(BF16)
            self._hop(a, kk).start()

    def finish(self):
        for a in range(self.n):
            self._load(a, 0).wait()
            self._swap(a, 0).wait_recv()
            acc = self.mine[a][0] + self.got[a][0]
            for kk in (1, 2, 3):
                self._hop(a, kk).wait_recv()
                acc = acc + self.rcv[a][kk - 1].astype(F32)
            self.outs[a][...] = acc
        for kk in range(4):
            for a in range(self.n):
                self._swap(a, kk).wait_send()
        for a, kk in self.pairs:
            self._hop(a, kk).wait_send()


def _local_step(x, mem, pre_g, w_in, sink, mem_g, w_mem, w_out, post_g, target):
    u, *tabs, w_in_full = _prep(x, pre_g, w_in)
    qa, ka, va, qkv_b, qc, gate, w_mem_all, w_out_all = _inproj(u, w_in_full, tabs, w_mem, w_out)
    w_mem_full = w_mem_all.reshape(D_MODEL, 2 * C_W)
    w_out_full = w_out_all.reshape(D_MODEL, D_MODEL)
    mn, mk, mv = _memkv_fwd(mem, mem_g, w_mem_full)

    a_cfg = dict(dil=1, heads=A_HEADS, max_dist=BLOCK - 1, nq=ATTN_BLOCKS_PER_STEP)
    b_cfgs = {dil: dict(dil=dil, heads=B_HEADS, max_dist=win // dil, nq=ATTN_BLOCKS_PER_STEP)
              for win, dil in B_CONFIGS}
    oa, lse_a = _banded_fwd(qa, ka, va, sink, name="attn_a_fwd", **a_cfg)
    ob, lse_b = {}, {}
    for dil, cfg in b_cfgs.items():
        ob[dil], lse_b[dil] = _banded_fwd(*qkv_b[dil], None, name=f"attn_b{dil}_fwd", **cfg)
    oc, lse_c = _cross_fwd(qc, mk, mv)

    dh, d_gate, d_a, d_b, d_c, d_wout, st_mid = _mid(oa, lse_a, ob, lse_b, oc, lse_c, gate, x, target, w_out_full, post_g)

    dqc, dmk, dmv = _cross_bwd(qc, mk, mv, *d_c)
    d_wmem, st_mem = _memkv_bwd(mem, mem_g, mn, w_mem_full, dmk, dmv)
    bwd_nq = lambda dil: ATTN_BWD_BLOCKS_PER_STEP if dil == 1 else min(dil, ATTN_BWD_BLOCKS_PER_STEP)
    dqkv_b = {dil: _banded_bwd(*qkv_b[dil], *d_b[dil], None, name=f"attn_b{dil}_bwd", **{**cfg, "nq": bwd_nq(dil)})
              for dil, cfg in b_cfgs.items()}
    dqa, dka, dva, dsink, g_wmem, g_wout = _banded_bwd(
        qa, ka, va, *d_a, sink, name="attn_a_bwd", **{**a_cfg, "nq": bwd_nq(1)},
        reduce_scatter=(d_wmem.reshape(N_DEV, SHARD_ROWS, 2 * C_W), d_wout.reshape(N_DEV, SHARD_ROWS, D_MODEL)))

    grad_x, d_win, st_pre = _inproj_bwd(x, u, dh, pre_g, w_in_full, tabs, dqa, dka, dva, dqkv_b, dqc, d_gate)

    dsink_row = jnp.pad(dsink[0:1, :], ((0, 0), (0, D_MODEL - LANES)))
    stats = jnp.concatenate([st_pre[0:1], st_mem[0:1], st_mid[0:1], dsink_row, st_mid[1:2],
                             jnp.zeros((3, D_MODEL), F32)], axis=0)
    return grad_x, d_win, g_wmem, g_wout, stats


def _prep(x, pre_g, w_in, tm=1024):
    seq = x.shape[0]
    n_steps = seq // tm
    parts = 2
    rows = D_MODEL // parts
    relay_at = min(4, n_steps - 1)
    j = jnp.arange(LANES) % HEAD_DIM
    freq = (ROPE_THETA ** (-(2 * (j % (ROT_DIM // 2))).astype(F32) / ROT_DIM))[None, :]
    SIB, NB_X, NB_Y, RELAY, FWD = 0, 1, 2, 3, 4

    def body(x_ref, g_ref, f_ref, win_ref, u_ref, c_ref, up_ref, dn_ref, win_out, win_b,
             send_sems, recv_sems, local_sems):
        step = pl.program_id(0)
        px, py, pc = _mesh_pos()
        me, sibling = (px, py, pc), (px, py, 1 - pc)
        others = lambda core: ((1 - px, py, core), (px, 1 - py, core), (1 - px, 1 - py, core))
        x_nb, y_nb, diag = others(pc)
        relay_from = [x_nb, y_nb]
        relay_to = [y_nb, x_nb]

        def src(a):
            return win_b.at[pl.ds(rows * a, rows)]

        def slot(a, p):
            return win_out.at[_dev_index(p), pl.ds(rows * a, rows)]

        def copy(a, k, block, to, own=False):
            return pltpu.make_async_remote_copy(
                src_ref=src(a) if own else slot(a, block), dst_ref=slot(a, block),
                send_sem=send_sems.at[a, k], recv_sem=recv_sems.at[a, k], device_id=to, device_id_type=MESH_ID)

        def first_sends():
            return [copy(0, NB_X, me, x_nb, own=True), copy(1, NB_Y, me, y_nb, own=True),
                    copy(1, NB_X, me, x_nb, own=True), copy(0, NB_Y, me, y_nb, own=True),
                    copy(0, SIB, me, sibling, own=True), copy(1, SIB, me, sibling, own=True)]

        def relay(a):
            return copy(a, RELAY, relay_from[a], relay_to[a])

        def to_sibling(a, which):
            return copy(a, FWD + which, others(pc)[which], sibling)

        def local(a):
            return pltpu.make_async_copy(src(a), slot(a, me), local_sems.at[a])

        @pl.when(step == 0)
        def _():
            win_b[...] = win_ref[...].astype(BF16)
            for a in range(parts):
                local(a).start()
            for cp in first_sends():
                cp.start()

        @pl.when(step == relay_at)
        def _():
            for a in range(parts):
                copy(a, NB_X + a, relay_from[a], me).wait_recv()
                relay(a).start()
                to_sibling(a, a).start()

        xv = x_ref[...]
        r = lax.rsqrt(jnp.mean(xv * xv, axis=-1, keepdims=True) + RMS_EPS)
        u_ref[...] = ((xv * r) * g_ref[...]).astype(BF16)
        freq = f_ref[...]
        ang_row = lax.broadcasted_iota(jnp.int32, (BLOCK, LANES), 0).astype(F32) * freq
        cos_row, sin_row = jnp.cos(ang_row), jnp.sin(ang_row)
        head_lane = lax.broadcasted_iota(jnp.int32, (BLOCK, LANES), 1) % HEAD_DIM
        half = ROT_DIM // 2
        for blk in range(tm // BLOCK):
            rows_b = slice(BLOCK * blk, BLOCK * (blk + 1))
            ang_0 = (step * tm + BLOCK * blk).astype(F32) * freq
            cos_0, sin_0 = jnp.cos(ang_0), jnp.sin(ang_0)
            cos = cos_0 * cos_row - sin_0 * sin_row
            sin = sin_0 * cos_row + cos_0 * sin_row
            c_ref[rows_b, :] = jnp.where(head_lane < ROT_DIM, cos, 1.0)
            up_ref[rows_b, :] = jnp.where((head_lane >= half) & (head_lane < ROT_DIM), sin, 0.0)
            dn_ref[rows_b, :] = jnp.where(head_lane < half, -sin, 0.0)

        @pl.when(step == n_steps - 1)
        def _():
            copy(1, NB_X, x_nb, me).wait_recv()
            to_sibling(1, 0).start()
            copy(0, NB_Y, y_nb, me).wait_recv()
            to_sibling(0, 1).start()
            for a in range(parts):
                copy(a, RELAY, diag, me).wait_recv()
                to_sibling(a, 2).start()
            for a in range(parts):
                copy(a, SIB, sibling, me).wait_recv()
                for which in range(3):
                    copy(a, FWD + which, others(1 - pc)[which], me).wait_recv()
            for cp in first_sends():
                cp.wait_send()
            for a in range(parts):
                relay(a).wait_send()
                for which in range(3):
                    to_sibling(a, which).wait_send()
                local(a).wait()

    return pl.pallas_call(
        body, name="prep", grid=(n_steps,),
        in_specs=[_row(tm, D_MODEL), _full((1, D_MODEL)), _full((1, LANES)), _full(w_in.shape)],
        out_specs=[_row(tm, D_MODEL), _row(tm, LANES), _row(tm, LANES), _row(tm, LANES),
                   pl.BlockSpec(memory_space=pl.ANY)],
        out_shape=[jax.ShapeDtypeStruct((seq, D_MODEL), BF16)] + [jax.ShapeDtypeStruct((seq, LANES), F32)] * 3
        + [jax.ShapeDtypeStruct((N_DEV,) + w_in.shape, BF16)],
        scratch_shapes=[pltpu.VMEM(w_in.shape, BF16), pltpu.SemaphoreType.DMA((parts, FWD + 3)),
                        pltpu.SemaphoreType.DMA((parts, FWD + 3)), pltpu.SemaphoreType.DMA((parts,))],
        compiler_params=_params(dimension_semantics=("arbitrary",)),
    )(x, pre_g, freq, w_in)


def _exchange_grads(d_win, stats):
    def body(win, st, g_win, r_st, send_sems, recv_sems, local_sem, *scratch):
        exchange = _ReduceScatter((win,), (g_win,), scratch)
        exchange.start()
        pos = _mesh_pos()
        me = _dev_index(pos)
        own = pltpu.make_async_copy(st, r_st.at[me], local_sem)
        own.start()
        copies = []
        for s in range(1, N_DEV):
            peer = _xor_peer(pos, s)
            mk = lambda slot: pltpu.make_async_remote_copy(
                src_ref=st, dst_ref=r_st.at[slot], send_sem=send_sems.at[s], recv_sem=recv_sems.at[s],
                device_id=peer, device_id_type=MESH_ID)
            send, arrival = mk(me), mk(_dev_index(peer))
            send.start()
            copies.append((send, arrival))
        exchange.send_chip_sums()
        exchange.finish()
        for send, arrival in copies:
            arrival.wait_recv()
            send.wait_send()
        own.wait()

    hbm = pl.BlockSpec(memory_space=pl.ANY)
    shard = d_win.shape[1:]
    return pl.pallas_call(
        body, name="exchange_grads", in_specs=[hbm, hbm],
        out_specs=[pl.BlockSpec(memory_space=pltpu.VMEM), hbm],
        out_shape=[jax.ShapeDtypeStruct(shard, F32), jax.ShapeDtypeStruct((N_DEV,) + stats.shape, F32)],
        scratch_shapes=[pltpu.SemaphoreType.DMA((N_DEV,)), pltpu.SemaphoreType.DMA((N_DEV,)), pltpu.SemaphoreType.DMA(())]
        + _ReduceScatter.scratch_shapes([shard]),
        compiler_params=_params(),
    )(d_win, stats)


WEIGHT_ORDER = ("pre_norm", "w_in", "sink_a", "mem_norm", "w_mem_kv", "w_out", "post_norm")


def _adamw_all(grads, r_stats, weights, moments_m, moments_v):
    n = len(WEIGHT_ORDER)
    stat_row = {"pre_norm": 0, "mem_norm": 1, "post_norm": 2, "sink_a": 3}

    n_steps = 8

    def body(*refs):
        gw_in, gw_mem, gw_out, st_ref = refs[0:4]
        w_refs, m_refs, v_refs = (dict(zip(WEIGHT_ORDER, refs[4 + n * t:4 + n * (t + 1)])) for t in range(3))
        loss_ref = refs[4 + 3 * n]
        outs = refs[5 + 3 * n:]
        big = {"w_in": gw_in, "w_mem_kv": gw_mem, "w_out": gw_out}

        def update(i, name, g, at, put):
            m2 = ADAM_B1 * at(m_refs[name]) + (1.0 - ADAM_B1) * g
            v2 = ADAM_B2 * at(v_refs[name]) + (1.0 - ADAM_B2) * (g * g)
            m_hat = m2 / (1.0 - ADAM_B1 ** ADAM_STEP)
            v_hat = v2 / (1.0 - ADAM_B2 ** ADAM_STEP)
            delta = -ADAM_LR * (m_hat / (jnp.sqrt(v_hat) + ADAM_EPS) + ADAM_WD * at(w_refs[name]))
            for kind, val in enumerate((g, delta, m2, v2)):
                put(outs[kind * n + i], val)

        def put_big(ref, val):
            ref[0] = val

        def put_small(ref, val):
            ref[...] = val

        for i, name in enumerate(WEIGHT_ORDER):
            if name in big:
                update(i, name, big[name][...], lambda ref: ref[0], put_big)

        @pl.when(pl.program_id(0) == 0)
        def _():
            g_small = st_ref[0]
            for s in range(1, N_DEV):
                g_small = g_small + st_ref[s]
            loss_ref[...] = g_small[4:5, 0:1]
            for i, name in enumerate(WEIGHT_ORDER):
                if name not in big:
                    width = w_refs[name].shape[-1]
                    g = g_small[stat_row[name]:stat_row[name] + 1, 0:width]
                    update(i, name, g, lambda ref: ref[...], put_small)

    shapes = [weights[name].shape for name in WEIGHT_ORDER]
    is_big = [name in grads for name in WEIGHT_ORDER]

    def spec(shape, tiled):
        if not tiled:
            return _full(shape)
        if len(shape) == 2:
            return pl.BlockSpec((shape[0] // n_steps, shape[1]), lambda i: (i, 0))
        return pl.BlockSpec((1, shape[1] // n_steps, shape[2]), lambda i: (0, i, 0))

    per_weight = [spec(sh, tiled) for sh, tiled in zip(shapes, is_big)]
    res = pl.pallas_call(
        body, name="adamw_all", grid=(n_steps,),
        in_specs=[spec(grads[k].shape, True) for k in ("w_in", "w_mem_kv", "w_out")] + [_full(r_stats.shape)]
        + per_weight * 3,
        out_specs=[_full((1, 1))] + per_weight * 4,
        out_shape=[jax.ShapeDtypeStruct((1, 1), F32)] + [jax.ShapeDtypeStruct(sh, F32) for sh in shapes] * 4,
        compiler_params=_params(dimension_semantics=("arbitrary",)),
    )(grads["w_in"], grads["w_mem_kv"], grads["w_out"], r_stats,
      *[weights[k] for k in WEIGHT_ORDER], *[moments_m[k] for k in WEIGHT_ORDER], *[moments_v[k] for k in WEIGHT_ORDER])
    return res[0].reshape(()), res[1:]


def kernel(x, mem, pre_norm, w_in, sink_a, mem_norm, w_mem_kv, w_out, post_norm, loss_target, m_pre_norm, m_w_in, m_sink_a, m_mem_norm, m_w_mem_kv, m_w_out, m_post_norm, v_pre_norm, v_w_in, v_sink_a, v_mem_norm, v_w_mem_kv, v_w_out, v_post_norm):
    sink = jnp.pad(sink_a[0], (0, 8 - A_HEADS))
    grad_x, d_win, g_wmem, g_wout, stats = _local_step(
        x[0], mem[0], pre_norm, w_in[0], sink, mem_norm, w_mem_kv[0], w_out[0], post_norm, loss_target[0])
    g_win, r_stats = _exchange_grads(d_win, stats)
    weights = dict(pre_norm=pre_norm, w_in=w_in, sink_a=sink_a, mem_norm=mem_norm, w_mem_kv=w_mem_kv, w_out=w_out,
                   post_norm=post_norm)
    moments_m = dict(pre_norm=m_pre_norm, w_in=m_w_in, sink_a=m_sink_a, mem_norm=m_mem_norm, w_mem_kv=m_w_mem_kv,
                     w_out=m_w_out, post_norm=m_post_norm)
    moments_v = dict(pre_norm=v_pre_norm, w_in=v_w_in, sink_a=v_sink_a, mem_norm=v_mem_norm, w_mem_kv=v_w_mem_kv,
                     w_out=v_w_out, post_norm=v_post_norm)
    loss, rest = _adamw_all(dict(w_in=g_win, w_mem_kv=g_wmem, w_out=g_wout), r_stats, weights, moments_m, moments_v)
    return (loss, grad_x[None], *rest)
```

```python
import jax
import jax.numpy as jnp
from jax import lax
from jax.experimental import pallas as pl
from jax.experimental.pallas import tpu as pltpu

F32 = jnp.float32
BF16 = jnp.bfloat16

D_MODEL = 1024
HEAD_DIM = 64
ROT_DIM = 16
ROPE_THETA = 500000.0
BLOCK = 128
LANES = 128
N_MEM = 256
RMS_EPS = 1e-6
SCALE = HEAD_DIM ** -0.5
A_HEADS = 6
B_HEADS = 6
C_HEADS = 4
A_W, A_KV_W, B_W, C_W = 384, 128, 384, 256
_IN_PIECES = (("qa", A_W), ("ka", A_KV_W), ("va", A_KV_W), ("ga", A_W), ("qb", B_W), ("kb", B_W), ("vb", B_W),
              ("gb", B_W), ("qc", C_W), ("gc", C_W))
COLS, D_IN = {}, 0
for _name, _width in _IN_PIECES:
    COLS[_name] = (D_IN, D_IN + _width)
    D_IN += _width
N_DEV = 8
SHARD_IN = D_IN // N_DEV
SHARD_ROWS = D_MODEL // N_DEV
B_CONFIGS = ((128, 1), (512, 4), (2048, 16))
DILS = (4, 16)
NEG = -1e30
ATTN_BLOCKS_PER_STEP = 4
ATTN_BWD_BLOCKS_PER_STEP = 8
DELTA_LANE = 64
VMEM_LIMIT = 56 * 1024 * 1024

ADAM_LR, ADAM_B1, ADAM_B2, ADAM_EPS, ADAM_WD, ADAM_STEP = 0.001, 0.9, 0.999, 1e-08, 0.01, 10
MESH_ID = pl.DeviceIdType.MESH


def _params(**kw):
    return pltpu.CompilerParams(vmem_limit_bytes=VMEM_LIMIT, **kw)


def _full(shape):
    n = len(shape)
    return pl.BlockSpec(shape, lambda *_: (0,) * n)


def _row(tm, w):
    return pl.BlockSpec((tm, w), lambda i: (i, 0))


def _mesh_pos():
    return lax.axis_index("x"), lax.axis_index("y"), lax.axis_index("c")


def _dev_index(pos):
    return 4 * pos[0] + 2 * pos[1] + pos[2]


def _xor_peer(pos, s):
    x, y, c = pos
    return (1 - x if s & 4 else x, 1 - y if s & 2 else y, 1 - c if s & 1 else c)


def _perm_view(a, dil):
    return a.reshape(a.shape[0] // (BLOCK * dil), dil, BLOCK, a.shape[1])


def _perm_spec(tm, dil, w):
    chunk = BLOCK * dil
    if tm >= chunk:
        return pl.BlockSpec((tm // chunk, dil, BLOCK, w), lambda i: (i, 0, 0, 0))
    per = chunk // tm
    return pl.BlockSpec((1, dil, tm // dil, w), lambda i: (i // per, 0, i % per, 0))


def _put(scr, val):
    for c in range(val.shape[1] // LANES):
        scr[c] = val[:, LANES * c:LANES * (c + 1)]


def _get(scr):
    n = scr.shape[0]
    return scr[0] if n == 1 else jnp.concatenate([scr[c] for c in range(n)], axis=1)


def _get_class(scr, r, dil):
    n, rows = scr.shape[0], scr.shape[1]
    parts = [scr.at[c][pl.ds(r, rows // dil, stride=dil), :] for c in range(n)]
    return parts[0] if n == 1 else jnp.concatenate(parts, axis=1)


def _store_permuted(scr, out_ref, dil, dtype):
    for r in range(dil):
        out_ref[0, r] = _get_class(scr, r, dil).astype(dtype)


def _load_permuted(in_ref, scr, dil):
    n, rows = scr.shape[0], scr.shape[1]
    for r in range(dil):
        val = in_ref[0, r].astype(F32)
        for c in range(n):
            scr.at[c][pl.ds(r, rows // dil, stride=dil), :] = val[:, LANES * c:LANES * (c + 1)]
    return _get(scr)


def _rotate128(t, c, up, dn):
    half = ROT_DIM // 2
    return t * c + pltpu.roll(t, half, 1) * up + pltpu.roll(t, LANES - half, 1) * dn


def _rotate(t, c, up, dn):
    outs = [_rotate128(t[:, LANES * j:LANES * (j + 1)], c, up, dn) for j in range(t.shape[1] // LANES)]
    return outs[0] if len(outs) == 1 else jnp.concatenate(outs, axis=1)


def _per_query_head(kv):
    lane = lax.broadcasted_iota(jnp.int32, kv.shape, 1)
    other = pltpu.roll(kv, HEAD_DIM, 1)
    return jnp.concatenate([jnp.where(lane < HEAD_DIM, kv, other), kv, jnp.where(lane < HEAD_DIM, other, kv)], axis=1)


def _per_kv_head(d):
    s0, s1, s2 = (d[:, LANES * p:LANES * (p + 1)] for p in range(3))
    lane = lax.broadcasted_iota(jnp.int32, s0.shape, 1)
    return jnp.where(lane < HEAD_DIM, s0 + pltpu.roll(s0, HEAD_DIM, 1) + s1, s1 + s2 + pltpu.roll(s2, HEAD_DIM, 1))


def _w_in_scratch():
    return [pltpu.VMEM((D_MODEL, D_IN), BF16), pltpu.SemaphoreType.DMA((N_DEV,))]


def _stage_w_in(w_hbm, w_scr, sems):
    @pl.when(pl.program_id(0) == 0)
    def _():
        copies = [pltpu.make_async_copy(w_hbm.at[k], w_scr.at[:, pl.ds(SHARD_IN * k, SHARD_IN)], sems.at[k])
                  for k in range(N_DEV)]
        for cp in copies:
            cp.start()
        for cp in copies:
            cp.wait()


def _inproj(u, w_in_full, tabs, w_mem, w_out, tm=1024):
    seq = u.shape[0]
    n_chunk = D_IN // LANES
    n_steps = seq // tm

    def body(u_ref, w_hbm, c_ref, up_ref, dn_ref, wm_ref, wo_ref, qa_ref, ka_ref, va_ref,
             qb1_ref, kb1_ref, vb1_ref, qb4_ref, kb4_ref, vb4_ref, qb16_ref, kb16_ref, vb16_ref,
             qc_ref, gate_ref, wm_all, wo_all, proj, w_scr, w_sems, wm_b, wo_b, send_sems, recv_sems, local_sems):
        step = pl.program_id(0)
        shards, gathered = (wm_b, wo_b), (wm_all, wo_all)

        def gather_copies(arriving):
            pos = _mesh_pos()
            me = _dev_index(pos)
            local = [] if arriving else [
                pltpu.make_async_copy(shards[a], gathered[a].at[me], local_sems.at[a]) for a in range(2)]
            remote = []
            for s in range(1, N_DEV):
                peer = _xor_peer(pos, s)
                for a in range(2):
                    remote.append(pltpu.make_async_remote_copy(
                        src_ref=shards[a], dst_ref=gathered[a].at[_dev_index(peer) if arriving else me],
                        send_sem=send_sems.at[a, s], recv_sem=recv_sems.at[a, s], device_id=peer,
                        device_id_type=MESH_ID))
            return local, remote

        @pl.when(step == 0)
        def _():
            wm_b[...] = wm_ref[...].astype(BF16)
            wo_b[...] = wo_ref[...].astype(BF16)
            local, sends = gather_copies(arriving=False)
            for cp in local + sends:
                cp.start()

        _stage_w_in(w_hbm, w_scr, w_sems)
        u = u_ref[...]
        for n0 in range(0, D_IN, D_MODEL):
            acc = jnp.dot(u, w_scr[:, n0:n0 + D_MODEL], preferred_element_type=F32)
            for c3 in range(D_MODEL // LANES):
                proj[n0 // LANES + c3] = acc[:, LANES * c3:LANES * (c3 + 1)]
        c, up, dn = c_ref[...], up_ref[...], dn_ref[...]

        def chunks_of(piece):
            lo, hi = COLS[piece]
            return range(lo // LANES, hi // LANES)

        def cols(piece, rot=False, scale=None):
            parts = []
            for ch in chunks_of(piece):
                t = proj[ch]
                if rot:
                    t = _rotate128(t, c, up, dn)
                if scale is not None:
                    t = t * scale
                parts.append(t)
            return parts[0] if len(parts) == 1 else jnp.concatenate(parts, axis=1)

        qa_ref[...] = cols("qa", True, SCALE).astype(BF16)
        ka_ref[...] = _per_query_head(cols("ka", True)).astype(BF16)
        va_ref[...] = _per_query_head(cols("va")).astype(BF16)
        gate_ref[:, 0:A_W] = cols("ga").astype(BF16)
        gate_ref[:, A_W:A_W + B_W] = cols("gb").astype(BF16)
        gate_ref[:, A_W + B_W:D_MODEL] = cols("gc").astype(BF16)
        qc_ref[...] = cols("qc", False, SCALE).astype(BF16)
        for ch in chunks_of("qb"):
            proj[ch] = _rotate128(proj[ch], c, up, dn) * SCALE
        for ch in chunks_of("kb"):
            proj[ch] = _rotate128(proj[ch], c, up, dn)
        for piece, nat, p4, p16 in (("qb", qb1_ref, qb4_ref, qb16_ref), ("kb", kb1_ref, kb4_ref, kb16_ref),
                                    ("vb", vb1_ref, vb4_ref, vb16_ref)):
            chunks = chunks_of(piece)
            nat[...] = jnp.concatenate([proj[ch] for ch in chunks], axis=1).astype(BF16)
            for dil, ref in ((4, p4), (16, p16)):
                span = min(tm, BLOCK * dil)
                for cc in range(tm // span):
                    for rr in range(dil):
                        ref[cc, rr] = jnp.concatenate(
                            [proj.at[ch][pl.ds(cc * span + rr, span // dil, stride=dil), :] for ch in chunks],
                            axis=1).astype(BF16)

        @pl.when(step == n_steps - 1)
        def _():
            for cp in gather_copies(arriving=True)[1]:
                cp.wait_recv()
            local, sends = gather_copies(arriving=False)
            for cp in sends:
                cp.wait_send()
            for cp in local:
                cp.wait()

    nat_w = (A_W, A_W, A_W, B_W, B_W, B_W)
    out_specs = [_row(tm, w) for w in nat_w]
    out_shape = [jax.ShapeDtypeStruct((seq, w), BF16) for w in nat_w]
    for dil in DILS:
        out_specs += [_perm_spec(tm, dil, B_W)] * 3
        out_shape += [jax.ShapeDtypeStruct((seq // (BLOCK * dil), dil, BLOCK, B_W), BF16)] * 3
    hbm = pl.BlockSpec(memory_space=pl.ANY)
    out_specs += [_row(tm, C_W), _row(tm, D_MODEL), hbm, hbm]
    out_shape += [jax.ShapeDtypeStruct((seq, C_W), BF16), jax.ShapeDtypeStruct((seq, D_MODEL), BF16),
                  jax.ShapeDtypeStruct((N_DEV,) + w_mem.shape, BF16), jax.ShapeDtypeStruct((N_DEV,) + w_out.shape, BF16)]
    res = pl.pallas_call(
        body, name="inproj", grid=(n_steps,),
        in_specs=[_row(tm, D_MODEL), hbm, _row(tm, LANES), _row(tm, LANES), _row(tm, LANES),
                  _full(w_mem.shape), _full(w_out.shape)],
        out_specs=out_specs, out_shape=out_shape,
        scratch_shapes=[pltpu.VMEM((n_chunk, tm, LANES), F32)] + _w_in_scratch()
        + [pltpu.VMEM(w_mem.shape, BF16), pltpu.VMEM(w_out.shape, BF16), pltpu.SemaphoreType.DMA((2, N_DEV)),
           pltpu.SemaphoreType.DMA((2, N_DEV)), pltpu.SemaphoreType.DMA((2,))],
        compiler_params=_params(dimension_semantics=("arbitrary",)),
    )(u, w_in_full, *tabs, w_mem, w_out)
    qa, ka, va = res[0:3]
    qkv_b = {1: res[3:6], 4: [t.reshape(seq, B_W) for t in res[6:9]], 16: [t.reshape(seq, B_W) for t in res[9:12]]}
    return qa, ka, va, qkv_b, res[12], res[13], res[14], res[15]


def _memkv_bwd(mem, mem_g, mn, w_mem_full, dmk, dmv):
    def body(mem_ref, g_ref, mn_ref, w_ref, dmk_ref, dmv_ref, dw_ref, st_ref):
        dmkv = jnp.concatenate([dmk_ref[...], dmv_ref[...]], axis=1).astype(BF16)
        dw_ref[...] = lax.dot_general(mn_ref[...], dmkv, (((0,), (0,)), ((), ())), preferred_element_type=F32)
        dmn = lax.dot_general(dmkv, w_ref[...], (((1,), (1,)), ((), ())), preferred_element_type=F32)
        mv_ = mem_ref[...]
        r = lax.rsqrt(jnp.mean(mv_ * mv_, axis=-1, keepdims=True) + RMS_EPS)
        st_ref[...] = jnp.zeros_like(st_ref)
        st_ref[0:1, :] = jnp.sum(dmn * (mv_ * r), axis=0, keepdims=True)

    return pl.pallas_call(
        body, name="memkv_bwd",
        out_shape=[jax.ShapeDtypeStruct((D_MODEL, 2 * C_W), F32), jax.ShapeDtypeStruct((8, D_MODEL), F32)],
        compiler_params=_params(),
    )(mem, mem_g, mn, w_mem_full, dmk, dmv)


def _band_mask(has_prev, max_dist):
    qi = lax.broadcasted_iota(jnp.int32, (BLOCK, 2 * BLOCK), 0)
    kj = lax.broadcasted_iota(jnp.int32, (BLOCK, 2 * BLOCK), 1)
    dist = qi + BLOCK - kj
    return (dist >= 0) & (dist <= max_dist) & ((kj >= BLOCK) | has_prev)


_NT = (((1,), (1,)), ((), ()))
_TN = (((0,), (0,)), ((), ()))


def _head_only(val, h):
    slab = _slabs_of(val)(h)
    lane = lax.broadcasted_iota(jnp.int32, slab.shape, 1)
    keep = (lane < HEAD_DIM) if h % 2 == 0 else (lane >= HEAD_DIM)
    return jnp.where(keep, slab, jnp.zeros((), slab.dtype))


def _slabs_of(val):
    return lambda h: val[:, LANES * (h // 2):LANES * (h // 2 + 1)]


class _BandSteps:
    def __init__(self, seq, dil, nq):
        self.nq, self.rows, self.consecutive = nq, nq * BLOCK, dil == 1
        nb = seq // dil // BLOCK
        if self.consecutive:
            assert nb % nq == 0
            self.outer, self.inner, self.stride = 1, nb // nq, 1
        else:
            assert dil % nq == 0
            self.outer, self.inner, self.stride = dil // nq, nb, dil // nq

    def own(self, w, clamp=False):
        cur = (lambda i: jnp.minimum(i, self.inner - 1)) if clamp else (lambda i: i)
        return pl.BlockSpec((self.rows, w), lambda r, i: (cur(i) * self.stride + r, 0))

    def prev(self, w, clamp=False):
        cur = (lambda i: jnp.minimum(i, self.inner - 1)) if clamp else (lambda i: i)
        if self.consecutive:
            return pl.BlockSpec((BLOCK, w), lambda r, i: (jnp.maximum(cur(i) * self.nq - 1, 0), 0))
        return pl.BlockSpec((self.rows, w), lambda r, i: (jnp.maximum(cur(i) - 1, 0) * self.stride + r, 0))

    def late(self, w):
        return pl.BlockSpec((self.rows, w), lambda r, i: (jnp.maximum(i - 1, 0) * self.stride + r, 0))

    def rows_of(self, j):
        return slice(BLOCK * j, BLOCK * (j + 1))

    def keys(self, p_ref, c_ref, j):
        if not self.consecutive:
            before = p_ref[self.rows_of(j), :]
        elif j == 0:
            before = p_ref[...]
        else:
            before = c_ref[self.rows_of(j - 1), :]
        return jnp.concatenate([before, c_ref[self.rows_of(j), :]], axis=0)

    def has_prev(self, i, j):
        return True if (self.consecutive and j > 0) else (i > 0)


def _banded_fwd(q, k, v, sink, *, dil, heads, max_dist, nq, name):
    seq = q.shape[0]
    qw = kw = heads * HEAD_DIM
    steps = _BandSteps(seq, dil, nq)

    def body(*refs):
        if sink is not None:
            sink_ref, refs = refs[0], refs[1:]
        q_ref, kp_ref, kc_ref, vp_ref, vc_ref, o_ref, lse_ref, s_scr, p_scr = refs
        i = pl.program_id(1)
        lane = lax.broadcasted_iota(jnp.int32, (BLOCK, LANES), 1)
        k_of = [_slabs_of(steps.keys(kp_ref, kc_ref, j)) for j in range(nq)]
        v_of = [_slabs_of(steps.keys(vp_ref, vc_ref, j)) for j in range(nq)]
        for j in range(nq):
            qv = q_ref[steps.rows_of(j), :]
            for h in range(heads):
                s_scr[j * heads + h] = lax.dot_general(_head_only(qv, h), k_of[j](h), _NT, preferred_element_type=F32)
        ls = {}
        for j in range(nq):
            valid = _band_mask(steps.has_prev(i, j), max_dist)
            lse_tile = jnp.zeros((BLOCK, LANES), F32)
            for h in range(heads):
                s = jnp.where(valid, s_scr[j * heads + h], NEG)
                m = jnp.max(s, axis=-1, keepdims=True)
                if sink is not None:
                    sk = sink_ref[h]
                    m = jnp.maximum(m, sk)
                p = jnp.exp(s - m)
                l = jnp.sum(p, axis=-1, keepdims=True)
                if sink is not None:
                    l = l + jnp.exp(sk - m)
                p_scr[j * heads + h] = p.astype(BF16)
                ls[j, h] = l
                lse_tile = jnp.where(lane == h, m + jnp.log(l), lse_tile)
            lse_ref[steps.rows_of(j), :] = lse_tile
        for j in range(nq):
            for pr in range(heads // 2):
                he, ho = 2 * pr, 2 * pr + 1
                even = jnp.dot(p_scr[j * heads + he], v_of[j](he), preferred_element_type=F32) / ls[j, he]
                odd = jnp.dot(p_scr[j * heads + ho], v_of[j](ho), preferred_element_type=F32) / ls[j, ho]
                o_ref[steps.rows_of(j), LANES * pr:LANES * (pr + 1)] = jnp.where(lane < HEAD_DIM, even, odd).astype(BF16)

    in_specs = [steps.own(qw), steps.prev(kw), steps.own(kw), steps.prev(kw), steps.own(kw)]
    args = [q, k, k, v, v]
    if sink is not None:
        in_specs = [pl.BlockSpec(memory_space=pltpu.SMEM)] + in_specs
        args = [sink] + args
    return pl.pallas_call(
        body, name=name, grid=(steps.outer, steps.inner), in_specs=in_specs,
        out_specs=[steps.own(qw), steps.own(LANES)],
        out_shape=[jax.ShapeDtypeStruct((seq, qw), BF16), jax.ShapeDtypeStruct((seq, LANES), F32)],
        scratch_shapes=[pltpu.VMEM((nq * heads, BLOCK, 2 * BLOCK), F32), pltpu.VMEM((nq * heads, BLOCK, 2 * BLOCK), BF16)],
        compiler_params=_params(dimension_semantics=("arbitrary", "arbitrary")),
    )(*args)


def _banded_bwd(q, k, v, d_out, stat, sink, *, dil, heads, max_dist, nq, name, reduce_scatter=()):
    seq = q.shape[0]
    qw = kw = heads * HEAD_DIM
    steps = _BandSteps(seq, dil, nq)
    n_rs = len(reduce_scatter)
    n_in = 7 + n_rs
    n_flat = steps.outer * (steps.inner + 1)

    def body(*refs):
        refs = list(refs)
        sink_ref = refs.pop(0) if sink is not None else None
        (q_ref, kp_ref, kc_ref, vp_ref, vc_ref, do_ref, st_ref), partials = refs[:7], refs[7:n_in]
        refs = refs[n_in:]
        dsink_ref = refs.pop(0) if sink is not None else None
        (dq_ref, dk_ref, dv_ref), sums = refs[:3], refs[3:3 + n_rs]
        kcar, vcar, s_scr, dp_scr, p_scr, ds_scr = refs[3 + n_rs:9 + n_rs]
        r, i = pl.program_id(0), pl.program_id(1)
        if n_rs:
            exchange = _ReduceScatter(tuple(partials), tuple(sums), refs[9 + n_rs:])
            flat = r * (steps.inner + 1) + i

            @pl.when(flat == 0)
            def _():
                exchange.start()

            @pl.when(flat == min(2, n_flat - 1))
            def _():
                exchange.send_chip_sums()

        @pl.when(i == 0)
        def _():
            kcar[...] = jnp.zeros_like(kcar)
            vcar[...] = jnp.zeros_like(vcar)

        if sink is not None:
            @pl.when((i == 0) & (r == 0))
            def _():
                dsink_ref[...] = jnp.zeros_like(dsink_ref)

        @pl.when(i < steps.inner)
        def _():
            lane = lax.broadcasted_iota(jnp.int32, (1, LANES), 1)
            lane_q = lax.broadcasted_iota(jnp.int32, (BLOCK, LANES), 1)
            k_of = [_slabs_of(steps.keys(kp_ref, kc_ref, j)) for j in range(nq)]
            v_of = [_slabs_of(steps.keys(vp_ref, vc_ref, j)) for j in range(nq)]
            qms, doms = {}, {}
            for j in range(nq):
                qv, dov = q_ref[steps.rows_of(j), :], do_ref[steps.rows_of(j), :]
                for h in range(heads):
                    qms[j, h], doms[j, h] = _head_only(qv, h), _head_only(dov, h)
                    s_scr[j * heads + h] = lax.dot_general(qms[j, h], k_of[j](h), _NT, preferred_element_type=F32)
                    dp_scr[j * heads + h] = lax.dot_general(doms[j, h], v_of[j](h), _NT, preferred_element_type=F32)
            dsink_row = jnp.zeros((1, LANES), F32)
            if sink is not None:
                sink_row = jnp.zeros((1, LANES), F32)
                for h in range(heads):
                    sink_row = jnp.where(lane == h, sink_ref[h], sink_row)
            for j in range(nq):
                st = st_ref[steps.rows_of(j), :]
                valid = _band_mask(steps.has_prev(i, j), max_dist)
                for h in range(heads):
                    lse_h = st[:, h:h + 1]
                    delta = st[:, DELTA_LANE + h:DELTA_LANE + h + 1]
                    p = jnp.where(valid, jnp.exp(s_scr[j * heads + h] - lse_h), 0.0)
                    p_scr[j * heads + h] = p.astype(BF16)
                    ds_scr[j * heads + h] = (p * (dp_scr[j * heads + h] - delta)).astype(BF16)
                if sink is not None:
                    term = -jnp.exp(sink_row - st) * pltpu.roll(st, LANES - DELTA_LANE, 1)
                    dsink_row = dsink_row + jnp.sum(jnp.where(lane_q < heads, term, 0.0), axis=0, keepdims=True)
            for j in range(nq):
                for pr in range(heads // 2):
                    he, ho = 2 * pr, 2 * pr + 1
                    even = jnp.dot(ds_scr[j * heads + he], k_of[j](he), preferred_element_type=F32)
                    odd = jnp.dot(ds_scr[j * heads + ho], k_of[j](ho), preferred_element_type=F32)
                    dq_ref[steps.rows_of(j), LANES * pr:LANES * (pr + 1)] = (
                        jnp.where(lane_q < HEAD_DIM, even, odd).astype(BF16))
            if steps.consecutive:
                dk_ref[...] = kcar[...].astype(BF16)
                dv_ref[...] = vcar[...].astype(BF16)
            for j in range(nq):
                for slab in range(kw // LANES):
                    he, ho = j * heads + 2 * slab, j * heads + 2 * slab + 1
                    dk_j = (lax.dot_general(ds_scr[he], qms[j, 2 * slab], _TN, preferred_element_type=F32)
                            + lax.dot_general(ds_scr[ho], qms[j, 2 * slab + 1], _TN, preferred_element_type=F32))
                    dv_j = (lax.dot_general(p_scr[he], doms[j, 2 * slab], _TN, preferred_element_type=F32)
                            + lax.dot_general(p_scr[ho], doms[j, 2 * slab + 1], _TN, preferred_element_type=F32))
                    sl = slice(LANES * slab, LANES * (slab + 1))
                    own_rows = steps.rows_of(j)
                    if not steps.consecutive:
                        dk_ref[own_rows, sl] = (kcar[own_rows, sl] + dk_j[0:BLOCK]).astype(BF16)
                        dv_ref[own_rows, sl] = (vcar[own_rows, sl] + dv_j[0:BLOCK]).astype(BF16)
                    elif j == 0:
                        last = steps.rows_of(nq - 1)
                        dk_ref[last, sl] = (kcar[last, sl] + dk_j[0:BLOCK]).astype(BF16)
                        dv_ref[last, sl] = (vcar[last, sl] + dv_j[0:BLOCK]).astype(BF16)
                    else:
                        before = steps.rows_of(j - 1)
                        kcar[before, sl] += dk_j[0:BLOCK]
                        vcar[before, sl] += dv_j[0:BLOCK]
                    kcar[own_rows, sl] = dk_j[BLOCK:2 * BLOCK]
                    vcar[own_rows, sl] = dv_j[BLOCK:2 * BLOCK]
            if sink is not None:
                dsink_ref[0:1, :] += dsink_row

        @pl.when(i == steps.inner)
        def _():
            dk_ref[...] = kcar[...].astype(BF16)
            dv_ref[...] = vcar[...].astype(BF16)

        if n_rs:
            @pl.when(flat == n_flat - 1)
            def _():
                exchange.finish()

    own, prev = (lambda w: steps.own(w, clamp=True)), (lambda w: steps.prev(w, clamp=True))
    rs_shapes = [t.shape[1:] for t in reduce_scatter]
    in_specs = ([own(qw), prev(kw), own(kw), prev(kw), own(kw), own(qw), own(LANES)]
                + [pl.BlockSpec(memory_space=pl.ANY)] * n_rs)
    args = [q, k, k, v, v, d_out, stat, *reduce_scatter]
    out_specs = [own(qw), steps.late(kw), steps.late(kw)] + [_full(s) for s in rs_shapes]
    out_shape = [jax.ShapeDtypeStruct((seq, qw), BF16), jax.ShapeDtypeStruct((seq, kw), BF16),
                 jax.ShapeDtypeStruct((seq, kw), BF16)] + [jax.ShapeDtypeStruct(s, F32) for s in rs_shapes]
    if sink is not None:
        in_specs = [pl.BlockSpec(memory_space=pltpu.SMEM)] + in_specs
        args = [sink] + args
        out_specs = [_full((8, LANES))] + out_specs
        out_shape = [jax.ShapeDtypeStruct((8, LANES), F32)] + out_shape
    n_hb = nq * heads
    res = pl.pallas_call(
        body, name=name, grid=(steps.outer, steps.inner + 1), in_specs=in_specs, out_specs=out_specs,
        out_shape=out_shape,
        scratch_shapes=[pltpu.VMEM((steps.rows, kw), F32), pltpu.VMEM((steps.rows, kw), F32)]
        + [pltpu.VMEM((n_hb, BLOCK, 2 * BLOCK), F32)] * 2 + [pltpu.VMEM((n_hb, BLOCK, 2 * BLOCK), BF16)] * 2
        + (_ReduceScatter.scratch_shapes(rs_shapes) if n_rs else []),
        compiler_params=_params(dimension_semantics=("arbitrary", "arbitrary")),
    )(*args)
    if sink is not None:
        return (*res[1:4], res[0], *res[4:])
    return res


def _cross_fwd(q, mem, mem_g, w_mem_full, tq=1024):
    seq = q.shape[0]

    def body(q_ref, mem_ref, g_ref, w_ref, o_ref, lse_ref, mn_ref, mk_ref, mv_ref, s_scr, p_scr):
        @pl.when(pl.program_id(0) == 0)
        def _():
            mv_ = mem_ref[...]
            r = lax.rsqrt(jnp.mean(mv_ * mv_, axis=-1, keepdims=True) + RMS_EPS)
            mn = ((mv_ * r) * g_ref[...]).astype(BF16)
            mn_ref[...] = mn
            mkv = jnp.dot(mn, w_ref[...], preferred_element_type=F32)
            mk_ref[...] = mkv[:, 0:C_W].astype(BF16)
            mv_ref[...] = mkv[:, C_W:2 * C_W].astype(BF16)

        qv = q_ref[...]
        k_of, v_of = _slabs_of(mk_ref[...]), _slabs_of(mv_ref[...])
        lane = lax.broadcasted_iota(jnp.int32, (tq, LANES), 1)
        lse_tile = jnp.zeros((tq, LANES), F32)
        for h in range(C_HEADS):
            s_scr[h] = lax.dot_general(_head_only(qv, h), k_of(h), _NT, preferred_element_type=F32)
        ls = []
        for h in range(C_HEADS):
            s = s_scr[h]
            m = jnp.max(s, axis=-1, keepdims=True)
            p = jnp.exp(s - m)
            l = jnp.sum(p, axis=-1, keepdims=True)
            p_scr[h] = p.astype(BF16)
            ls.append(l)
            lse_tile = jnp.where(lane == h, m + jnp.log(l), lse_tile)
        for pr in range(C_HEADS // 2):
            even = jnp.dot(p_scr[2 * pr], v_of(2 * pr), preferred_element_type=F32) / ls[2 * pr]
            odd = jnp.dot(p_scr[2 * pr + 1], v_of(2 * pr + 1), preferred_element_type=F32) / ls[2 * pr + 1]
            o_ref[:, LANES * pr:LANES * (pr + 1)] = jnp.where(lane < HEAD_DIM, even, odd).astype(BF16)
        lse_ref[...] = lse_tile

    return pl.pallas_call(
        body, name="cross_fwd", grid=(seq // tq,),
        in_specs=[_row(tq, C_W), _full((N_MEM, D_MODEL)), _full((1, D_MODEL)), _full((D_MODEL, 2 * C_W))],
        out_specs=[_row(tq, C_W), _row(tq, LANES), _full((N_MEM, D_MODEL)), _full((N_MEM, C_W)), _full((N_MEM, C_W))],
        out_shape=[jax.ShapeDtypeStruct((seq, C_W), BF16), jax.ShapeDtypeStruct((seq, LANES), F32),
                   jax.ShapeDtypeStruct((N_MEM, D_MODEL), BF16), jax.ShapeDtypeStruct((N_MEM, C_W), BF16),
                   jax.ShapeDtypeStruct((N_MEM, C_W), BF16)],
        scratch_shapes=[pltpu.VMEM((C_HEADS, tq, N_MEM), F32), pltpu.VMEM((C_HEADS, tq, N_MEM), BF16)],
        compiler_params=_params(dimension_semantics=("arbitrary",)),
    )(q, mem, mem_g, w_mem_full)


def _cross_bwd(q, mk, mv, d_out, stat, tq=1024):
    seq = q.shape[0]

    def body(q_ref, mk_ref, mv_ref, do_ref, st_ref, dq_ref, dmk_ref, dmv_ref, s_scr, dp_scr, p_scr, ds_scr):
        @pl.when(pl.program_id(0) == 0)
        def _():
            dmk_ref[...] = jnp.zeros_like(dmk_ref)
            dmv_ref[...] = jnp.zeros_like(dmv_ref)

        qv, dov, st = q_ref[...], do_ref[...], st_ref[...]
        k_of, v_of = _slabs_of(mk_ref[...]), _slabs_of(mv_ref[...])
        qms = [_head_only(qv, h) for h in range(C_HEADS)]
        doms = [_head_only(dov, h) for h in range(C_HEADS)]
        for h in range(C_HEADS):
            s_scr[h] = lax.dot_general(qms[h], k_of(h), _NT, preferred_element_type=F32)
            dp_scr[h] = lax.dot_general(doms[h], v_of(h), _NT, preferred_element_type=F32)
        for h in range(C_HEADS):
            p = jnp.exp(s_scr[h] - st[:, h:h + 1])
            p_scr[h] = p.astype(BF16)
            ds_scr[h] = (p * (dp_scr[h] - st[:, DELTA_LANE + h:DELTA_LANE + h + 1])).astype(BF16)
        lane = lax.broadcasted_iota(jnp.int32, (tq, LANES), 1)
        for pr in range(C_HEADS // 2):
            sl = slice(LANES * pr, LANES * (pr + 1))
            even = jnp.dot(ds_scr[2 * pr], k_of(2 * pr), preferred_element_type=F32)
            odd = jnp.dot(ds_scr[2 * pr + 1], k_of(2 * pr + 1), preferred_element_type=F32)
            dq_ref[:, sl] = jnp.where(lane < HEAD_DIM, even, odd).astype(BF16)
            dmk_ref[:, sl] += (lax.dot_general(ds_scr[2 * pr], qms[2 * pr], _TN, preferred_element_type=F32)
                               + lax.dot_general(ds_scr[2 * pr + 1], qms[2 * pr + 1], _TN, preferred_element_type=F32))
            dmv_ref[:, sl] += (lax.dot_general(p_scr[2 * pr], doms[2 * pr], _TN, preferred_element_type=F32)
                               + lax.dot_general(p_scr[2 * pr + 1], doms[2 * pr + 1], _TN, preferred_element_type=F32))

    return pl.pallas_call(
        body, name="cross_bwd", grid=(seq // tq,),
        in_specs=[_row(tq, C_W), _full((N_MEM, C_W)), _full((N_MEM, C_W)), _row(tq, C_W), _row(tq, LANES)],
        out_specs=[_row(tq, C_W), _full((N_MEM, C_W)), _full((N_MEM, C_W))],
        out_shape=[jax.ShapeDtypeStruct((seq, C_W), BF16), jax.ShapeDtypeStruct((N_MEM, C_W), F32),
                   jax.ShapeDtypeStruct((N_MEM, C_W), F32)],
        scratch_shapes=[pltpu.VMEM((C_HEADS, tq, N_MEM), F32)] * 2 + [pltpu.VMEM((C_HEADS, tq, N_MEM), BF16)] * 2,
        compiler_params=_params(dimension_semantics=("arbitrary",)),
    )(q, mk, mv, d_out, stat)


def _per_head(tile, width):
    rows = tile.shape[0]
    lane = lax.broadcasted_iota(jnp.int32, (rows, LANES), 1)
    slabs = []
    for p in range(width // LANES):
        even = jnp.broadcast_to(tile[:, 2 * p:2 * p + 1], (rows, LANES))
        odd = jnp.broadcast_to(tile[:, 2 * p + 1:2 * p + 2], (rows, LANES))
        slabs.append(jnp.where(lane < HEAD_DIM, even, odd))
    return slabs[0] if len(slabs) == 1 else jnp.concatenate(slabs, axis=1)


def _with_delta(lse_tile, prod):
    rows = lse_tile.shape[0]
    lane = lax.broadcasted_iota(jnp.int32, (rows, LANES), 1)
    tile = lse_tile
    for p in range(prod.shape[1] // LANES):
        slab = prod[:, LANES * p:LANES * (p + 1)]
        even = jnp.sum(jnp.where(lane < HEAD_DIM, slab, 0.0), axis=-1, keepdims=True)
        odd = jnp.sum(jnp.where(lane >= HEAD_DIM, slab, 0.0), axis=-1, keepdims=True)
        tile = jnp.where(lane == DELTA_LANE + 2 * p, even, tile)
        tile = jnp.where(lane == DELTA_LANE + 2 * p + 1, odd, tile)
    return tile


def _mid(oa, lse_a, ob, lse_b, oc, lse_c, gate, x, target, w_out_full, post_g, tm=512):
    seq = x.shape[0]
    n_b = B_W // LANES

    def body(oa_ref, la_ref, b1_ref, l1_ref, b4_ref, l4_ref, b16_ref, l16_ref, oc_ref, lc_ref,
             gate_ref, x_ref, t_ref, w_ref, pg_ref,
             dh_ref, dg_ref, doa_ref, sa_ref, dob1_ref, sb1_ref, dob4_ref, sb4_ref, dob16_ref, sb16_ref,
             doc_ref, sc_ref, dw_ref, st_ref, scr_b4, scr_b16, scr_l4, scr_l16, scr_do, scr_sb):
        @pl.when(pl.program_id(0) == 0)
        def _():
            dw_ref[...] = jnp.zeros_like(dw_ref)
            st_ref[...] = jnp.zeros_like(st_ref)

        b1, l1 = b1_ref[...].astype(F32), l1_ref[...]
        b4, l4 = _load_permuted(b4_ref, scr_b4, 4), _load_permuted(l4_ref, scr_l4, 4)
        b16, l16 = _load_permuted(b16_ref, scr_b16, 16), _load_permuted(l16_ref, scr_l16, 16)
        lm = jnp.maximum(jnp.maximum(l1, l4), l16)
        e1, e4, e16 = jnp.exp(l1 - lm), jnp.exp(l4 - lm), jnp.exp(l16 - lm)
        den = e1 + e4 + e16
        lse_b_tile = lm + jnp.log(den)
        ob_v = _per_head(e1 / den, B_W) * b1 + _per_head(e4 / den, B_W) * b4 + _per_head(e16 / den, B_W) * b16
        o_all = jnp.concatenate([oa_ref[...].astype(F32), ob_v, oc_ref[...].astype(F32)], axis=1)
        g = gate_ref[...].astype(F32)
        sig = 1.0 / (1.0 + jnp.exp(-g))
        silu = g * sig
        y = (o_all * silu).astype(BF16)
        w = w_ref[...]
        z = jnp.dot(y, w, preferred_element_type=F32)
        rz = lax.rsqrt(jnp.mean(z * z, axis=-1, keepdims=True) + RMS_EPS)
        hn = z * rz
        pg = pg_ref[...]
        err = (x_ref[...] + hn * pg) - t_ref[...]
        loss = 0.5 * jnp.sum(jnp.mean(err * err, axis=-1, keepdims=True), axis=0, keepdims=True)
        dh = err * (1.0 / D_MODEL)
        dh_ref[...] = dh.astype(BF16)
        st_ref[0:1, :] += jnp.sum(dh * hn, axis=0, keepdims=True)
        st_ref[1:2, :] += jnp.broadcast_to(loss, (1, D_MODEL))
        dhn = dh * pg
        dz = (rz * (dhn - hn * jnp.mean(dhn * hn, axis=-1, keepdims=True))).astype(BF16)
        dy = lax.dot_general(dz, w, _NT, preferred_element_type=F32)
        dw_ref[...] += lax.dot_general(y, dz, _TN, preferred_element_type=F32)
        dg_ref[...] = (dy * o_all * (sig * (1.0 + g * (1.0 - sig)))).astype(BF16)
        d_o = (dy * silu).astype(BF16)
        prod = d_o.astype(F32) * o_all
        doa_ref[...] = d_o[:, 0:A_W]
        sa_ref[...] = _with_delta(la_ref[...], prod[:, 0:A_W])
        doc_ref[...] = d_o[:, A_W + B_W:D_MODEL]
        sc_ref[...] = _with_delta(lc_ref[...], prod[:, A_W + B_W:D_MODEL])
        d_ob = d_o[:, A_W:A_W + B_W]
        stat_b = _with_delta(lse_b_tile, prod[:, A_W:A_W + B_W])
        dob1_ref[...] = d_ob
        sb1_ref[...] = stat_b
        _put(scr_do, d_ob.astype(F32))
        _put(scr_sb, stat_b)
        _store_permuted(scr_do, dob4_ref, 4, BF16)
        _store_permuted(scr_sb, sb4_ref, 4, F32)
        _store_permuted(scr_do, dob16_ref, 16, BF16)
        _store_permuted(scr_sb, sb16_ref, 16, F32)

    p4 = lambda w: _perm_spec(tm, 4, w)
    p16 = lambda w: _perm_spec(tm, 16, w)
    in_specs = [_row(tm, A_W), _row(tm, LANES), _row(tm, B_W), _row(tm, LANES), p4(B_W), p4(LANES), p16(B_W), p16(LANES),
                _row(tm, C_W), _row(tm, LANES), _row(tm, D_MODEL), _row(tm, D_MODEL), _row(tm, D_MODEL),
                _full((D_MODEL, D_MODEL)), _full((1, D_MODEL))]
    sds = jax.ShapeDtypeStruct
    v4 = lambda w, dt: sds((seq // (BLOCK * 4), 4, BLOCK, w), dt)
    v16 = lambda w, dt: sds((seq // (BLOCK * 16), 16, BLOCK, w), dt)
    out_specs = [_row(tm, D_MODEL), _row(tm, D_MODEL), _row(tm, A_W), _row(tm, LANES), _row(tm, B_W), _row(tm, LANES),
                 p4(B_W), p4(LANES), p16(B_W), p16(LANES), _row(tm, C_W), _row(tm, LANES),
                 _full((D_MODEL, D_MODEL)), _full((8, D_MODEL))]
    out_shape = [sds((seq, D_MODEL), BF16), sds((seq, D_MODEL), BF16), sds((seq, A_W), BF16), sds((seq, LANES), F32),
                 sds((seq, B_W), BF16), sds((seq, LANES), F32), v4(B_W, BF16), v4(LANES, F32), v16(B_W, BF16),
                 v16(LANES, F32), sds((seq, C_W), BF16), sds((seq, LANES), F32),
                 sds((D_MODEL, D_MODEL), F32), sds((8, D_MODEL), F32)]
    res = pl.pallas_call(
        body, name="mid", grid=(seq // tm,), in_specs=in_specs, out_specs=out_specs, out_shape=out_shape,
        scratch_shapes=[pltpu.VMEM((n_b, tm, LANES), F32), pltpu.VMEM((n_b, tm, LANES), F32),
                        pltpu.VMEM((1, tm, LANES), F32), pltpu.VMEM((1, tm, LANES), F32),
                        pltpu.VMEM((n_b, tm, LANES), F32), pltpu.VMEM((1, tm, LANES), F32)],
        compiler_params=_params(dimension_semantics=("arbitrary",)),
    )(oa, lse_a, ob[1], lse_b[1], _perm_view(ob[4], 4), _perm_view(lse_b[4], 4), _perm_view(ob[16], 16),
      _perm_view(lse_b[16], 16), oc, lse_c, gate, x, target, w_out_full, post_g)
    dh, d_gate, do_a, st_a, do_b1, st_b1, do_b4, st_b4, do_b16, st_b16, do_c, st_c, d_wout, stats = res
    flat = lambda t: t.reshape(seq, t.shape[-1])
    d_b = {1: (do_b1, st_b1), 4: (flat(do_b4), flat(st_b4)), 16: (flat(do_b16), flat(st_b16))}
    return dh, d_gate, (do_a, st_a), d_b, (do_c, st_c), d_wout, stats


def _inproj_bwd(x, u, dh, pre_g, w_in_full, tabs, dqa, dka, dva, dqkv_b, dqc, dgate, tm=512):
    seq = x.shape[0]
    n_b = B_W // LANES

    def body(x_ref, u_ref, dh_ref, g_ref, w_hbm, c_ref, up_ref, dn_ref, dqa_ref, dka_ref, dva_ref,
             dq1, dk1, dv1, dq4, dk4, dv4, dq16, dk16, dv16, dqc_ref, dg_ref,
             gx_ref, dw_ref, st_ref, scr4, scr16, w_scr, w_sems, dp_ref):
        _stage_w_in(w_hbm, w_scr, w_sems)

        @pl.when(pl.program_id(0) == 0)
        def _():
            st_ref[...] = jnp.zeros_like(st_ref)
            dw_ref[...] = jnp.zeros_like(dw_ref)

        c, up, dn = c_ref[...], -up_ref[...], -dn_ref[...]
        unrot = lambda t: _rotate(t, c, up, dn)
        total = lambda r1, r4, r16: (r1[...].astype(F32) + _load_permuted(r4, scr4, 4)
                                     + _load_permuted(r16, scr16, 16))
        at = lambda piece: slice(*COLS[piece])
        dp_ref[:, at("qa")] = (unrot(dqa_ref[...].astype(F32)) * SCALE).astype(BF16)
        dp_ref[:, at("ka")] = unrot(_per_kv_head(dka_ref[...].astype(F32))).astype(BF16)
        dp_ref[:, at("va")] = _per_kv_head(dva_ref[...].astype(F32)).astype(BF16)
        dp_ref[:, at("ga")] = dg_ref[:, 0:A_W]
        dp_ref[:, at("qb")] = (unrot(total(dq1, dq4, dq16)) * SCALE).astype(BF16)
        dp_ref[:, at("kb")] = unrot(total(dk1, dk4, dk16)).astype(BF16)
        dp_ref[:, at("vb")] = total(dv1, dv4, dv16).astype(BF16)
        dp_ref[:, at("gb")] = dg_ref[:, A_W:A_W + B_W]
        dp_ref[:, at("qc")] = (dqc_ref[...].astype(F32) * SCALE).astype(BF16)
        dp_ref[:, at("gc")] = dg_ref[:, A_W + B_W:D_MODEL]
        du = lax.dot_general(dp_ref[...], w_scr[...], _NT, preferred_element_type=F32)
        res = lax.dot_general(u_ref[...], dp_ref[...], _TN, preferred_element_type=F32)
        for k in range(N_DEV):
            dw_ref[k] += res[:, SHARD_IN * k:SHARD_IN * (k + 1)]
        xv = x_ref[...]
        r = lax.rsqrt(jnp.mean(xv * xv, axis=-1, keepdims=True) + RMS_EPS)
        xh = xv * r
        st_ref[0:1, :] += jnp.sum(du * xh, axis=0, keepdims=True)
        dxh = du * g_ref[...]
        gx_ref[...] = dh_ref[...].astype(F32) + r * (dxh - xh * jnp.mean(dxh * xh, axis=-1, keepdims=True))

    in_specs = ([_row(tm, D_MODEL), _row(tm, D_MODEL), _row(tm, D_MODEL), _full((1, D_MODEL)),
                 pl.BlockSpec(memory_space=pl.ANY),
                 _row(tm, LANES), _row(tm, LANES), _row(tm, LANES), _row(tm, A_W), _row(tm, A_W), _row(tm, A_W)]
                + [_row(tm, B_W)] * 3 + [_perm_spec(tm, 4, B_W)] * 3 + [_perm_spec(tm, 16, B_W)] * 3
                + [_row(tm, C_W), _row(tm, D_MODEL)])
    dw_spec = pl.BlockSpec((N_DEV, D_MODEL, SHARD_IN), lambda i: (0, 0, 0), pipeline_mode=pl.Buffered(1))
    return pl.pallas_call(
        body, name="inproj_bwd", grid=(seq // tm,), in_specs=in_specs,
        out_specs=[_row(tm, D_MODEL), dw_spec, _full((8, D_MODEL))],
        out_shape=[jax.ShapeDtypeStruct((seq, D_MODEL), F32), jax.ShapeDtypeStruct((N_DEV, D_MODEL, SHARD_IN), F32),
                   jax.ShapeDtypeStruct((8, D_MODEL), F32)],
        scratch_shapes=[pltpu.VMEM((n_b, tm, LANES), F32), pltpu.VMEM((n_b, tm, LANES), F32)] + _w_in_scratch()
        + [pltpu.VMEM((tm, D_IN), BF16)],
        compiler_params=_params(dimension_semantics=("arbitrary",)),
    )(x, u, dh, pre_g, w_in_full, *tabs, dqa, dka, dva, *dqkv_b[1], *[_perm_view(t, 4) for t in dqkv_b[4]],
      *[_perm_view(t, 16) for t in dqkv_b[16]], dqc, dgate)


class _ReduceScatter:
    def __init__(self, ins, outs, scratch):
        self.n = n = len(ins)
        self.ins, self.outs = ins, outs
        self.mine, self.got, self.snd, self.rcv = (scratch[n * t:n * (t + 1)] for t in range(4))
        self.load_sems, self.d2d_send, self.d2d_recv, self.ici_send, self.ici_recv = scratch[4 * n:]
        self.pos = _mesh_pos()
        self.pairs = [(a, kk) for kk in (3, 1, 2) for a in range(n)]

    @staticmethod
    def scratch_shapes(shapes):
        return ([pltpu.VMEM((4,) + s, F32) for s in shapes] + [pltpu.VMEM((4,) + s, F32) for s in shapes]
                + [pltpu.VMEM((3,) + s, BF16) for s in shapes] + [pltpu.VMEM((3,) + s, BF16) for s in shapes]
                + [pltpu.SemaphoreType.DMA((len(shapes), 4))] * 5)

    def _chip(self, kk):
        x, y, _ = self.pos
        return (1 - x if kk & 2 else x, 1 - y if kk & 1 else y)

    def _load(self, a, kk):
        block = _dev_index((*self._chip(kk), self.pos[2]))
        return pltpu.make_async_copy(self.ins[a].at[block], self.mine[a].at[kk], self.load_sems.at[a, kk])

    def _swap(self, a, kk):
        x, y, c = self.pos
        return pltpu.make_async_remote_copy(
            src_ref=self.ins[a].at[_dev_index((*self._chip(kk), 1 - c))], dst_ref=self.got[a].at[kk],
            send_sem=self.d2d_send.at[a, kk], recv_sem=self.d2d_recv.at[a, kk],
            device_id=(x, y, 1 - c), device_id_type=MESH_ID)

    def _hop(self, a, kk):
        return pltpu.make_async_remote_copy(
            src_ref=self.snd[a].at[kk - 1], dst_ref=self.rcv[a].at[kk - 1], send_sem=self.ici_send.at[a, kk],
            recv_sem=self.ici_recv.at[a, kk], device_id=(*self._chip(kk), self.pos[2]), device_id_type=MESH_ID)

    def start(self):
        for kk in (3, 1, 2, 0):
            for a in range(self.n):
                self._load(a, kk).start()
                self._swap(a, kk).start()

    def send_chip_sums(self):
        for a, kk in self.pairs:
            self._load(a, kk).wait()
            self._swap(a, kk).wait_recv()
            self.snd[a][kk - 1] = (self.mine[a][kk] + self.got[a][kk]).astype(BF16)
            self._hop(a, kk).start()

    def finish(self):
        for a in range(self.n):
            self._load(a, 0).wait()
            self._swap(a, 0).wait_recv()
            acc = self.mine[a][0] + self.got[a][0]
            for kk in (1, 2, 3):
                self._hop(a, kk).wait_recv()
                acc = acc + self.rcv[a][kk - 1].astype(F32)
            self.outs[a][...] = acc
        for kk in range(4):
            for a in range(self.n):
                self._swap(a, kk).wait_send()
        for a, kk in self.pairs:
            self._hop(a, kk).wait_send()


def _local_step(x, mem, pre_g, w_in, sink, mem_g, w_mem, w_out, post_g, target):
    u, *tabs, w_in_full = _prep(x, pre_g, w_in)
    qa, ka, va, qkv_b, qc, gate, w_mem_all, w_out_all = _inproj(u, w_in_full, tabs, w_mem, w_out)
    w_mem_full = w_mem_all.reshape(D_MODEL, 2 * C_W)
    w_out_full = w_out_all.reshape(D_MODEL, D_MODEL)

    a_cfg = dict(dil=1, heads=A_HEADS, max_dist=BLOCK - 1, nq=ATTN_BLOCKS_PER_STEP)
    b_cfgs = {dil: dict(dil=dil, heads=B_HEADS, max_dist=win // dil, nq=ATTN_BLOCKS_PER_STEP)
              for win, dil in B_CONFIGS}
    oa, lse_a = _banded_fwd(qa, ka, va, sink, name="attn_a_fwd", **a_cfg)
    ob, lse_b = {}, {}
    for dil, cfg in b_cfgs.items():
        ob[dil], lse_b[dil] = _banded_fwd(*qkv_b[dil], None, name=f"attn_b{dil}_fwd", **cfg)
    oc, lse_c, mn, mk, mv = _cross_fwd(qc, mem, mem_g, w_mem_full)

    dh, d_gate, d_a, d_b, d_c, d_wout, st_mid = _mid(oa, lse_a, ob, lse_b, oc, lse_c, gate, x, target, w_out_full, post_g)

    dqc, dmk, dmv = _cross_bwd(qc, mk, mv, *d_c)
    d_wmem, st_mem = _memkv_bwd(mem, mem_g, mn, w_mem_full, dmk, dmv)
    bwd_nq = lambda dil: ATTN_BWD_BLOCKS_PER_STEP if dil == 1 else min(dil, ATTN_BWD_BLOCKS_PER_STEP)
    dqkv_b = {dil: _banded_bwd(*qkv_b[dil], *d_b[dil], None, name=f"attn_b{dil}_bwd", **{**cfg, "nq": bwd_nq(dil)})
              for dil, cfg in b_cfgs.items()}
    dqa, dka, dva, dsink, g_wmem, g_wout = _banded_bwd(
        qa, ka, va, *d_a, sink, name="attn_a_bwd", **{**a_cfg, "nq": bwd_nq(1)},
        reduce_scatter=(d_wmem.reshape(N_DEV, SHARD_ROWS, 2 * C_W), d_wout.reshape(N_DEV, SHARD_ROWS, D_MODEL)))

    grad_x, d_win, st_pre = _inproj_bwd(x, u, dh, pre_g, w_in_full, tabs, dqa, dka, dva, dqkv_b, dqc, d_gate)

    dsink_row = jnp.pad(dsink[0:1, :], ((0, 0), (0, D_MODEL - LANES)))
    stats = jnp.concatenate([st_pre[0:1], st_mem[0:1], st_mid[0:1], dsink_row, st_mid[1:2],
                             jnp.zeros((3, D_MODEL), F32)], axis=0)
    return grad_x, d_win, g_wmem, g_wout, stats


def _prep(x, pre_g, w_in, tm=1024):
    seq = x.shape[0]
    n_steps = seq // tm
    parts = 2
    rows = D_MODEL // parts
    relay_at = min(4, n_steps - 1)
    j = jnp.arange(LANES) % HEAD_DIM
    freq = (ROPE_THETA ** (-(2 * (j % (ROT_DIM // 2))).astype(F32) / ROT_DIM))[None, :]
    SIB, NB_X, NB_Y, RELAY, FWD = 0, 1, 2, 3, 4

    def body(x_ref, g_ref, f_ref, win_ref, u_ref, c_ref, up_ref, dn_ref, win_out, win_b,
             send_sems, recv_sems, local_sems):
        step = pl.program_id(0)
        px, py, pc = _mesh_pos()
        me, sibling = (px, py, pc), (px, py, 1 - pc)
        others = lambda core: ((1 - px, py, core), (px, 1 - py, core), (1 - px, 1 - py, core))
        x_nb, y_nb, diag = others(pc)
        relay_from = [x_nb, y_nb]
        relay_to = [y_nb, x_nb]

        def src(a):
            return win_b.at[pl.ds(rows * a, rows)]

        def slot(a, p):
            return win_out.at[_dev_index(p), pl.ds(rows * a, rows)]

        def copy(a, k, block, to, own=False):
            return pltpu.make_async_remote_copy(
                src_ref=src(a) if own else slot(a, block), dst_ref=slot(a, block),
                send_sem=send_sems.at[a, k], recv_sem=recv_sems.at[a, k], device_id=to, device_id_type=MESH_ID)

        def first_sends():
            return [copy(0, NB_X, me, x_nb, own=True), copy(1, NB_Y, me, y_nb, own=True),
                    copy(1, NB_X, me, x_nb, own=True), copy(0, NB_Y, me, y_nb, own=True),
                    copy(0, SIB, me, sibling, own=True), copy(1, SIB, me, sibling, own=True)]

        def relay(a):
            return copy(a, RELAY, relay_from[a], relay_to[a])

        def to_sibling(a, which):
            return copy(a, FWD + which, others(pc)[which], sibling)

        def local(a):
            return pltpu.make_async_copy(src(a), slot(a, me), local_sems.at[a])

        @pl.when(step == 0)
        def _():
            win_b[...] = win_ref[...].astype(BF16)
            for a in range(parts):
                local(a).start()
            for cp in first_sends():
                cp.start()

        @pl.when(step == relay_at)
        def _():
            for a in range(parts):
                copy(a, NB_X + a, relay_from[a], me).wait_recv()
                relay(a).start()
                to_sibling(a, a).start()

        xv = x_ref[...]
        r = lax.rsqrt(jnp.mean(xv * xv, axis=-1, keepdims=True) + RMS_EPS)
        u_ref[...] = ((xv * r) * g_ref[...]).astype(BF16)
        freq = f_ref[...]
        ang_row = lax.broadcasted_iota(jnp.int32, (BLOCK, LANES), 0).astype(F32) * freq
        cos_row, sin_row = jnp.cos(ang_row), jnp.sin(ang_row)
        head_lane = lax.broadcasted_iota(jnp.int32, (BLOCK, LANES), 1) % HEAD_DIM
        half = ROT_DIM // 2
        for blk in range(tm // BLOCK):
            rows_b = slice(BLOCK * blk, BLOCK * (blk + 1))
            ang_0 = (step * tm + BLOCK * blk).astype(F32) * freq
            cos_0, sin_0 = jnp.cos(ang_0), jnp.sin(ang_0)
            cos = cos_0 * cos_row - sin_0 * sin_row
            sin = sin_0 * cos_row + cos_0 * sin_row
            c_ref[rows_b, :] = jnp.where(head_lane < ROT_DIM, cos, 1.0)
            up_ref[rows_b, :] = jnp.where((head_lane >= half) & (head_lane < ROT_DIM), sin, 0.0)
            dn_ref[rows_b, :] = jnp.where(head_lane < half, -sin, 0.0)

        @pl.when(step == n_steps - 1)
        def _():
            copy(1, NB_X, x_nb, me).wait_recv()
            to_sibling(1, 0).start()
            copy(0, NB_Y, y_nb, me).wait_recv()
            to_sibling(0, 1).start()
            for a in range(parts):
                copy(a, RELAY, diag, me).wait_recv()
                to_sibling(a, 2).start()
            for a in range(parts):
                copy(a, SIB, sibling, me).wait_recv()
                for which in range(3):
                    copy(a, FWD + which, others(1 - pc)[which], me).wait_recv()
            for cp in first_sends():
                cp.wait_send()
            for a in range(parts):
                relay(a).wait_send()
                for which in range(3):
                    to_sibling(a, which).wait_send()
                local(a).wait()

    return pl.pallas_call(
        body, name="prep", grid=(n_steps,),
        in_specs=[_row(tm, D_MODEL), _full((1, D_MODEL)), _full((1, LANES)), _full(w_in.shape)],
        out_specs=[_row(tm, D_MODEL), _row(tm, LANES), _row(tm, LANES), _row(tm, LANES),
                   pl.BlockSpec(memory_space=pl.ANY)],
        out_shape=[jax.ShapeDtypeStruct((seq, D_MODEL), BF16)] + [jax.ShapeDtypeStruct((seq, LANES), F32)] * 3
        + [jax.ShapeDtypeStruct((N_DEV,) + w_in.shape, BF16)],
        scratch_shapes=[pltpu.VMEM(w_in.shape, BF16), pltpu.SemaphoreType.DMA((parts, FWD + 3)),
                        pltpu.SemaphoreType.DMA((parts, FWD + 3)), pltpu.SemaphoreType.DMA((parts,))],
        compiler_params=_params(dimension_semantics=("arbitrary",)),
    )(x, pre_g, freq, w_in)


def _exchange_grads(d_win, stats):
    def body(win, st, g_win, r_st, send_sems, recv_sems, local_sem, *scratch):
        exchange = _ReduceScatter((win,), (g_win,), scratch)
        exchange.start()
        pos = _mesh_pos()
        me = _dev_index(pos)
        own = pltpu.make_async_copy(st, r_st.at[me], local_sem)
        own.start()
        copies = []
        for s in range(1, N_DEV):
            peer = _xor_peer(pos, s)
            mk = lambda slot: pltpu.make_async_remote_copy(
                src_ref=st, dst_ref=r_st.at[slot], send_sem=send_sems.at[s], recv_sem=recv_sems.at[s],
                device_id=peer, device_id_type=MESH_ID)
            send, arrival = mk(me), mk(_dev_index(peer))
            send.start()
            copies.append((send, arrival))
        exchange.send_chip_sums()
        exchange.finish()
        for send, arrival in copies:
            arrival.wait_recv()
            send.wait_send()
        own.wait()

    hbm = pl.BlockSpec(memory_space=pl.ANY)
    shard = d_win.shape[1:]
    return pl.pallas_call(
        body, name="exchange_grads", in_specs=[hbm, hbm],
        out_specs=[pl.BlockSpec(memory_space=pltpu.VMEM), hbm],
        out_shape=[jax.ShapeDtypeStruct(shard, F32), jax.ShapeDtypeStruct((N_DEV,) + stats.shape, F32)],
        scratch_shapes=[pltpu.SemaphoreType.DMA((N_DEV,)), pltpu.SemaphoreType.DMA((N_DEV,)), pltpu.SemaphoreType.DMA(())]
        + _ReduceScatter.scratch_shapes([shard]),
        compiler_params=_params(),
    )(d_win, stats)


WEIGHT_ORDER = ("pre_norm", "w_in", "sink_a", "mem_norm", "w_mem_kv", "w_out", "post_norm")


def _adamw_all(grads, r_stats, weights, moments_m, moments_v):
    n = len(WEIGHT_ORDER)
    stat_row = {"pre_norm": 0, "mem_norm": 1, "post_norm": 2, "sink_a": 3}

    def body(*refs):
        gw_in, gw_mem, gw_out, st_ref = refs[0:4]
        w_refs, m_refs, v_refs = (dict(zip(WEIGHT_ORDER, refs[4 + n * t:4 + n * (t + 1)])) for t in range(3))
        loss_ref = refs[4 + 3 * n]
        outs = refs[5 + 3 * n:]
        g_small = st_ref[0]
        for s in range(1, N_DEV):
            g_small = g_small + st_ref[s]
        loss_ref[...] = g_small[4:5, 0:1]
        big = {"w_in": gw_in, "w_mem_kv": gw_mem, "w_out": gw_out}
        for i, name in enumerate(WEIGHT_ORDER):
            if name in big:
                g = big[name][...]
                at = lambda ref: ref[0]
            else:
                width = w_refs[name].shape[-1]
                g = g_small[stat_row[name]:stat_row[name] + 1, 0:width]
                at = lambda ref: ref[...]
            m2 = ADAM_B1 * at(m_refs[name]) + (1.0 - ADAM_B1) * g
            v2 = ADAM_B2 * at(v_refs[name]) + (1.0 - ADAM_B2) * (g * g)
            m_hat = m2 / (1.0 - ADAM_B1 ** ADAM_STEP)
            v_hat = v2 / (1.0 - ADAM_B2 ** ADAM_STEP)
            delta = -ADAM_LR * (m_hat / (jnp.sqrt(v_hat) + ADAM_EPS) + ADAM_WD * at(w_refs[name]))
            for kind, val in enumerate((g, delta, m2, v2)):
                out = outs[kind * n + i]
                if name in big:
                    out[0] = val
                else:
                    out[...] = val

    shapes = [weights[name].shape for name in WEIGHT_ORDER]
    res = pl.pallas_call(
        body, name="adamw_all",
        out_shape=[jax.ShapeDtypeStruct((1, 1), F32)] + [jax.ShapeDtypeStruct(sh, F32) for sh in shapes] * 4,
        compiler_params=_params(),
    )(grads["w_in"], grads["w_mem_kv"], grads["w_out"], r_stats,
      *[weights[k] for k in WEIGHT_ORDER], *[moments_m[k] for k in WEIGHT_ORDER], *[moments_v[k] for k in WEIGHT_ORDER])
    return res[0].reshape(()), res[1:]


def kernel(x, mem, pre_norm, w_in, sink_a, mem_norm, w_mem_kv, w_out, post_norm, loss_target, m_pre_norm, m_w_in, m_sink_a, m_mem_norm, m_w_mem_kv, m_w_out, m_post_norm, v_pre_norm, v_w_in, v_sink_a, v_mem_norm, v_w_mem_kv, v_w_out, v_post_norm):
    sink = jnp.pad(sink_a[0], (0, 8 - A_HEADS))
    grad_x, d_win, g_wmem, g_wout, stats = _local_step(
        x[0], mem[0], pre_norm, w_in[0], sink, mem_norm, w_mem_kv[0], w_out[0], post_norm, loss_target[0])
    g_win, r_stats = _exchange_grads(d_win, stats)
    weights = dict(pre_norm=pre_norm, w_in=w_in, sink_a=sink_a, mem_norm=mem_norm, w_mem_kv=w_mem_kv, w_out=w_out,
                   post_norm=post_norm)
    moments_m = dict(pre_norm=m_pre_norm, w_in=m_w_in, sink_a=m_sink_a, mem_norm=m_mem_norm, w_mem_kv=m_w_mem_kv,
                     w_out=m_w_out, post_norm=m_post_norm)
    moments_v = dict(pre_norm=v_pre_norm, w_in=v_w_in, sink_a=v_sink_a, mem_norm=v_mem_norm, w_mem_kv=v_w_mem_kv,
                     w_out=v_w_out, post_norm=v_post_norm)
    loss, rest = _adamw_all(dict(w_in=g_win, w_mem_kv=g_wmem, w_out=g_wout), r_stats, weights, moments_m, moments_v)
    return (loss, grad_x[None], *rest)
```

```python
import jax
import jax.numpy as jnp
from jax import lax
from jax.experimental import pallas as pl
from jax.experimental.pallas import tpu as pltpu

F32 = jnp.float32
BF16 = jnp.bfloat16

D_MODEL = 1024
HEAD_DIM = 64
ROT_DIM = 16
ROPE_THETA = 500000.0
BLOCK = 128
LANES = 128
N_MEM = 256
RMS_EPS = 1e-6
SCALE = HEAD_DIM ** -0.5
A_HEADS = 6
B_HEADS = 6
C_HEADS = 4
A_W, A_KV_W, B_W, C_W = 384, 128, 384, 256
_IN_PIECES = (("qa", A_W), ("ka", A_KV_W), ("va", A_KV_W), ("ga", A_W), ("qb", B_W), ("kb", B_W), ("vb", B_W),
              ("gb", B_W), ("qc", C_W), ("gc", C_W))
COLS, D_IN = {}, 0
for _name, _width in _IN_PIECES:
    COLS[_name] = (D_IN, D_IN + _width)
    D_IN += _width
N_DEV = 8
SHARD_IN = D_IN // N_DEV
SHARD_ROWS = D_MODEL // N_DEV
B_CONFIGS = ((128, 1), (512, 4), (2048, 16))
DILS = (4, 16)
NEG = -1e30
ATTN_BLOCKS_PER_STEP = 4
ATTN_BWD_BLOCKS_PER_STEP = 8
DELTA_LANE = 64
VMEM_LIMIT = 56 * 1024 * 1024

ADAM_LR, ADAM_B1, ADAM_B2, ADAM_EPS, ADAM_WD, ADAM_STEP = 0.001, 0.9, 0.999, 1e-08, 0.01, 10
MESH_ID = pl.DeviceIdType.MESH


def _params(**kw):
    return pltpu.CompilerParams(vmem_limit_bytes=VMEM_LIMIT, **kw)


def _full(shape):
    n = len(shape)
    return pl.BlockSpec(shape, lambda *_: (0,) * n)


def _row(tm, w):
    return pl.BlockSpec((tm, w), lambda i: (i, 0))


def _mesh_pos():
    return lax.axis_index("x"), lax.axis_index("y"), lax.axis_index("c")


def _dev_index(pos):
    return 4 * pos[0] + 2 * pos[1] + pos[2]


def _xor_peer(pos, s):
    x, y, c = pos
    return (1 - x if s & 4 else x, 1 - y if s & 2 else y, 1 - c if s & 1 else c)


def _perm_view(a, dil):
    return a.reshape(a.shape[0] // (BLOCK * dil), dil, BLOCK, a.shape[1])


def _perm_spec(tm, dil, w):
    chunk = BLOCK * dil
    if tm >= chunk:
        return pl.BlockSpec((tm // chunk, dil, BLOCK, w), lambda i: (i, 0, 0, 0))
    per = chunk // tm
    return pl.BlockSpec((1, dil, tm // dil, w), lambda i: (i // per, 0, i % per, 0))


def _put(scr, val):
    for c in range(val.shape[1] // LANES):
        scr[c] = val[:, LANES * c:LANES * (c + 1)]


def _get(scr):
    n = scr.shape[0]
    return scr[0] if n == 1 else jnp.concatenate([scr[c] for c in range(n)], axis=1)


def _get_class(scr, r, dil):
    n, rows = scr.shape[0], scr.shape[1]
    parts = [scr.at[c][pl.ds(r, rows // dil, stride=dil), :] for c in range(n)]
    return parts[0] if n == 1 else jnp.concatenate(parts, axis=1)


def _store_permuted(scr, out_ref, dil, dtype):
    for r in range(dil):
        out_ref[0, r] = _get_class(scr, r, dil).astype(dtype)


def _load_permuted(in_ref, scr, dil):
    n, rows = scr.shape[0], scr.shape[1]
    for r in range(dil):
        val = in_ref[0, r].astype(F32)
        for c in range(n):
            scr.at[c][pl.ds(r, rows // dil, stride=dil), :] = val[:, LANES * c:LANES * (c + 1)]
    return _get(scr)


def _rotate128(t, c, up, dn):
    half = ROT_DIM // 2
    return t * c + pltpu.roll(t, half, 1) * up + pltpu.roll(t, LANES - half, 1) * dn


def _rotate(t, c, up, dn):
    outs = [_rotate128(t[:, LANES * j:LANES * (j + 1)], c, up, dn) for j in range(t.shape[1] // LANES)]
    return outs[0] if len(outs) == 1 else jnp.concatenate(outs, axis=1)


def _per_query_head(kv):
    lane = lax.broadcasted_iota(jnp.int32, kv.shape, 1)
    other = pltpu.roll(kv, HEAD_DIM, 1)
    return jnp.concatenate([jnp.where(lane < HEAD_DIM, kv, other), kv, jnp.where(lane < HEAD_DIM, other, kv)], axis=1)


def _per_kv_head(d):
    s0, s1, s2 = (d[:, LANES * p:LANES * (p + 1)] for p in range(3))
    lane = lax.broadcasted_iota(jnp.int32, s0.shape, 1)
    return jnp.where(lane < HEAD_DIM, s0 + pltpu.roll(s0, HEAD_DIM, 1) + s1, s1 + s2 + pltpu.roll(s2, HEAD_DIM, 1))


def _w_in_scratch():
    return [pltpu.VMEM((D_MODEL, D_IN), BF16), pltpu.SemaphoreType.DMA((N_DEV,))]


def _stage_w_in(w_hbm, w_scr, sems):
    @pl.when(pl.program_id(0) == 0)
    def _():
        copies = [pltpu.make_async_copy(w_hbm.at[k], w_scr.at[:, pl.ds(SHARD_IN * k, SHARD_IN)], sems.at[k])
                  for k in range(N_DEV)]
        for cp in copies:
            cp.start()
        for cp in copies:
            cp.wait()


def _inproj(u, w_in_full, tabs, w_mem, w_out, tm=1024):
    seq = u.shape[0]
    n_chunk = D_IN // LANES
    n_steps = seq // tm

    def body(u_ref, w_hbm, c_ref, up_ref, dn_ref, wm_ref, wo_ref, qa_ref, ka_ref, va_ref,
             qb1_ref, kb1_ref, vb1_ref, qb4_ref, kb4_ref, vb4_ref, qb16_ref, kb16_ref, vb16_ref,
             qc_ref, gate_ref, wm_all, wo_all, proj, w_scr, w_sems, wm_b, wo_b, send_sems, recv_sems, local_sems):
        step = pl.program_id(0)
        shards, gathered = (wm_b, wo_b), (wm_all, wo_all)

        def gather_copies(arriving):
            pos = _mesh_pos()
            me = _dev_index(pos)
            local = [] if arriving else [
                pltpu.make_async_copy(shards[a], gathered[a].at[me], local_sems.at[a]) for a in range(2)]
            remote = []
            for s in range(1, N_DEV):
                peer = _xor_peer(pos, s)
                for a in range(2):
                    remote.append(pltpu.make_async_remote_copy(
                        src_ref=shards[a], dst_ref=gathered[a].at[_dev_index(peer) if arriving else me],
                        send_sem=send_sems.at[a, s], recv_sem=recv_sems.at[a, s], device_id=peer,
                        device_id_type=MESH_ID))
            return local, remote

        @pl.when(step == 0)
        def _():
            wm_b[...] = wm_ref[...].astype(BF16)
            wo_b[...] = wo_ref[...].astype(BF16)
            local, sends = gather_copies(arriving=False)
            for cp in local + sends:
                cp.start()

        _stage_w_in(w_hbm, w_scr, w_sems)
        u = u_ref[...]
        for n0 in range(0, D_IN, D_MODEL):
            acc = jnp.dot(u, w_scr[:, n0:n0 + D_MODEL], preferred_element_type=F32)
            for c3 in range(D_MODEL // LANES):
                proj[n0 // LANES + c3] = acc[:, LANES * c3:LANES * (c3 + 1)]
        c, up, dn = c_ref[...], up_ref[...], dn_ref[...]

        def chunks_of(piece):
            lo, hi = COLS[piece]
            return range(lo // LANES, hi // LANES)

        def cols(piece, rot=False, scale=None):
            parts = []
            for ch in chunks_of(piece):
                t = proj[ch]
                if rot:
                    t = _rotate128(t, c, up, dn)
                if scale is not None:
                    t = t * scale
                parts.append(t)
            return parts[0] if len(parts) == 1 else jnp.concatenate(parts, axis=1)

        qa_ref[...] = cols("qa", True, SCALE).astype(BF16)
        ka_ref[...] = _per_query_head(cols("ka", True)).astype(BF16)
        va_ref[...] = _per_query_head(cols("va")).astype(BF16)
        gate_ref[:, 0:A_W] = cols("ga").astype(BF16)
        gate_ref[:, A_W:A_W + B_W] = cols("gb").astype(BF16)
        gate_ref[:, A_W + B_W:D_MODEL] = cols("gc").astype(BF16)
        qc_ref[...] = cols("qc", False, SCALE).astype(BF16)
        for ch in chunks_of("qb"):
            proj[ch] = _rotate128(proj[ch], c, up, dn) * SCALE
        for ch in chunks_of("kb"):
            proj[ch] = _rotate128(proj[ch], c, up, dn)
        for piece, nat, p4, p16 in (("qb", qb1_ref, qb4_ref, qb16_ref), ("kb", kb1_ref, kb4_ref, kb16_ref),
                                    ("vb", vb1_ref, vb4_ref, vb16_ref)):
            chunks = chunks_of(piece)
            nat[...] = jnp.concatenate([proj[ch] for ch in chunks], axis=1).astype(BF16)
            for dil, ref in ((4, p4), (16, p16)):
                span = min(tm, BLOCK * dil)
                for cc in range(tm // span):
                    for rr in range(dil):
                        ref[cc, rr] = jnp.concatenate(
                            [proj.at[ch][pl.ds(cc * span + rr, span // dil, stride=dil), :] for ch in chunks],
                            axis=1).astype(BF16)

        @pl.when(step == n_steps - 1)
        def _():
            for cp in gather_copies(arriving=True)[1]:
                cp.wait_recv()
            local, sends = gather_copies(arriving=False)
            for cp in sends:
                cp.wait_send()
            for cp in local:
                cp.wait()

    nat_w = (A_W, A_W, A_W, B_W, B_W, B_W)
    out_specs = [_row(tm, w) for w in nat_w]
    out_shape = [jax.ShapeDtypeStruct((seq, w), BF16) for w in nat_w]
    for dil in DILS:
        out_specs += [_perm_spec(tm, dil, B_W)] * 3
        out_shape += [jax.ShapeDtypeStruct((seq // (BLOCK * dil), dil, BLOCK, B_W), BF16)] * 3
    hbm = pl.BlockSpec(memory_space=pl.ANY)
    out_specs += [_row(tm, C_W), _row(tm, D_MODEL), hbm, hbm]
    out_shape += [jax.ShapeDtypeStruct((seq, C_W), BF16), jax.ShapeDtypeStruct((seq, D_MODEL), BF16),
                  jax.ShapeDtypeStruct((N_DEV,) + w_mem.shape, BF16), jax.ShapeDtypeStruct((N_DEV,) + w_out.shape, BF16)]
    res = pl.pallas_call(
        body, name="inproj", grid=(n_steps,),
        in_specs=[_row(tm, D_MODEL), hbm, _row(tm, LANES), _row(tm, LANES), _row(tm, LANES),
                  _full(w_mem.shape), _full(w_out.shape)],
        out_specs=out_specs, out_shape=out_shape,
        scratch_shapes=[pltpu.VMEM((n_chunk, tm, LANES), F32)] + _w_in_scratch()
        + [pltpu.VMEM(w_mem.shape, BF16), pltpu.VMEM(w_out.shape, BF16), pltpu.SemaphoreType.DMA((2, N_DEV)),
           pltpu.SemaphoreType.DMA((2, N_DEV)), pltpu.SemaphoreType.DMA((2,))],
        compiler_params=_params(dimension_semantics=("arbitrary",)),
    )(u, w_in_full, *tabs, w_mem, w_out)
    qa, ka, va = res[0:3]
    qkv_b = {1: res[3:6], 4: [t.reshape(seq, B_W) for t in res[6:9]], 16: [t.reshape(seq, B_W) for t in res[9:12]]}
    return qa, ka, va, qkv_b, res[12], res[13], res[14], res[15]


def _band_mask(has_prev, max_dist):
    qi = lax.broadcasted_iota(jnp.int32, (BLOCK, 2 * BLOCK), 0)
    kj = lax.broadcasted_iota(jnp.int32, (BLOCK, 2 * BLOCK), 1)
    dist = qi + BLOCK - kj
    return (dist >= 0) & (dist <= max_dist) & ((kj >= BLOCK) | has_prev)


_NT = (((1,), (1,)), ((), ()))
_TN = (((0,), (0,)), ((), ()))


def _head_only(val, h):
    slab = _slabs_of(val)(h)
    lane = lax.broadcasted_iota(jnp.int32, slab.shape, 1)
    keep = (lane < HEAD_DIM) if h % 2 == 0 else (lane >= HEAD_DIM)
    return jnp.where(keep, slab, jnp.zeros((), slab.dtype))


def _slabs_of(val):
    return lambda h: val[:, LANES * (h // 2):LANES * (h // 2 + 1)]


class _BandSteps:
    def __init__(self, seq, dil, nq):
        self.nq, self.rows, self.consecutive = nq, nq * BLOCK, dil == 1
        nb = seq // dil // BLOCK
        if self.consecutive:
            assert nb % nq == 0
            self.outer, self.inner, self.stride = 1, nb // nq, 1
        else:
            assert dil % nq == 0
            self.outer, self.inner, self.stride = dil // nq, nb, dil // nq

    def own(self, w, clamp=False):
        cur = (lambda i: jnp.minimum(i, self.inner - 1)) if clamp else (lambda i: i)
        return pl.BlockSpec((self.rows, w), lambda r, i: (cur(i) * self.stride + r, 0))

    def prev(self, w, clamp=False):
        cur = (lambda i: jnp.minimum(i, self.inner - 1)) if clamp else (lambda i: i)
        if self.consecutive:
            return pl.BlockSpec((BLOCK, w), lambda r, i: (jnp.maximum(cur(i) * self.nq - 1, 0), 0))
        return pl.BlockSpec((self.rows, w), lambda r, i: (jnp.maximum(cur(i) - 1, 0) * self.stride + r, 0))

    def late(self, w):
        return pl.BlockSpec((self.rows, w), lambda r, i: (jnp.maximum(i - 1, 0) * self.stride + r, 0))

    def rows_of(self, j):
        return slice(BLOCK * j, BLOCK * (j + 1))

    def keys(self, p_ref, c_ref, j):
        if not self.consecutive:
            before = p_ref[self.rows_of(j), :]
        elif j == 0:
            before = p_ref[...]
        else:
            before = c_ref[self.rows_of(j - 1), :]
        return jnp.concatenate([before, c_ref[self.rows_of(j), :]], axis=0)

    def has_prev(self, i, j):
        return True if (self.consecutive and j > 0) else (i > 0)


def _banded_fwd(q, k, v, sink, *, dil, heads, max_dist, nq, name):
    seq = q.shape[0]
    qw = kw = heads * HEAD_DIM
    steps = _BandSteps(seq, dil, nq)

    def body(*refs):
        if sink is not None:
            sink_ref, refs = refs[0], refs[1:]
        q_ref, kp_ref, kc_ref, vp_ref, vc_ref, o_ref, lse_ref, s_scr, p_scr = refs
        i = pl.program_id(1)
        lane = lax.broadcasted_iota(jnp.int32, (BLOCK, LANES), 1)
        k_of = [_slabs_of(steps.keys(kp_ref, kc_ref, j)) for j in range(nq)]
        v_of = [_slabs_of(steps.keys(vp_ref, vc_ref, j)) for j in range(nq)]
        for j in range(nq):
            qv = q_ref[steps.rows_of(j), :]
            for h in range(heads):
                s_scr[j * heads + h] = lax.dot_general(_head_only(qv, h), k_of[j](h), _NT, preferred_element_type=F32)
        ls = {}
        for j in range(nq):
            valid = _band_mask(steps.has_prev(i, j), max_dist)
            lse_tile = jnp.zeros((BLOCK, LANES), F32)
            for h in range(heads):
                s = jnp.where(valid, s_scr[j * heads + h], NEG)
                m = jnp.max(s, axis=-1, keepdims=True)
                if sink is not None:
                    sk = sink_ref[h]
                    m = jnp.maximum(m, sk)
                p = jnp.exp(s - m)
                l = jnp.sum(p, axis=-1, keepdims=True)
                if sink is not None:
                    l = l + jnp.exp(sk - m)
                p_scr[j * heads + h] = p.astype(BF16)
                ls[j, h] = l
                lse_tile = jnp.where(lane == h, m + jnp.log(l), lse_tile)
            lse_ref[steps.rows_of(j), :] = lse_tile
        for j in range(nq):
            for pr in range(heads // 2):
                he, ho = 2 * pr, 2 * pr + 1
                even = jnp.dot(p_scr[j * heads + he], v_of[j](he), preferred_element_type=F32) / ls[j, he]
                odd = jnp.dot(p_scr[j * heads + ho], v_of[j](ho), preferred_element_type=F32) / ls[j, ho]
                o_ref[steps.rows_of(j), LANES * pr:LANES * (pr + 1)] = jnp.where(lane < HEAD_DIM, even, odd).astype(BF16)

    in_specs = [steps.own(qw), steps.prev(kw), steps.own(kw), steps.prev(kw), steps.own(kw)]
    args = [q, k, k, v, v]
    if sink is not None:
        in_specs = [pl.BlockSpec(memory_space=pltpu.SMEM)] + in_specs
        args = [sink] + args
    return pl.pallas_call(
        body, name=name, grid=(steps.outer, steps.inner), in_specs=in_specs,
        out_specs=[steps.own(qw), steps.own(LANES)],
        out_shape=[jax.ShapeDtypeStruct((seq, qw), BF16), jax.ShapeDtypeStruct((seq, LANES), F32)],
        scratch_shapes=[pltpu.VMEM((nq * heads, BLOCK, 2 * BLOCK), F32), pltpu.VMEM((nq * heads, BLOCK, 2 * BLOCK), BF16)],
        compiler_params=_params(dimension_semantics=("arbitrary", "arbitrary")),
    )(*args)


def _banded_bwd(q, k, v, d_out, stat, sink, *, dil, heads, max_dist, nq, name, reduce_scatter=()):
    seq = q.shape[0]
    qw = kw = heads * HEAD_DIM
    steps = _BandSteps(seq, dil, nq)
    n_rs = len(reduce_scatter)
    n_in = 7 + n_rs
    n_flat = steps.outer * (steps.inner + 1)

    def body(*refs):
        refs = list(refs)
        sink_ref = refs.pop(0) if sink is not None else None
        (q_ref, kp_ref, kc_ref, vp_ref, vc_ref, do_ref, st_ref), partials = refs[:7], refs[7:n_in]
        refs = refs[n_in:]
        dsink_ref = refs.pop(0) if sink is not None else None
        (dq_ref, dk_ref, dv_ref), sums = refs[:3], refs[3:3 + n_rs]
        kcar, vcar, s_scr, dp_scr, p_scr, ds_scr = refs[3 + n_rs:9 + n_rs]
        r, i = pl.program_id(0), pl.program_id(1)
        if n_rs:
            exchange = _ReduceScatter(tuple(partials), tuple(sums), refs[9 + n_rs:])
            flat = r * (steps.inner + 1) + i

            @pl.when(flat == 0)
            def _():
                exchange.start()

            @pl.when(flat == min(2, n_flat - 1))
            def _():
                exchange.send_chip_sums()

        @pl.when(i == 0)
        def _():
            kcar[...] = jnp.zeros_like(kcar)
            vcar[...] = jnp.zeros_like(vcar)

        if sink is not None:
            @pl.when((i == 0) & (r == 0))
            def _():
                dsink_ref[...] = jnp.zeros_like(dsink_ref)

        @pl.when(i < steps.inner)
        def _():
            lane = lax.broadcasted_iota(jnp.int32, (1, LANES), 1)
            lane_q = lax.broadcasted_iota(jnp.int32, (BLOCK, LANES), 1)
            k_of = [_slabs_of(steps.keys(kp_ref, kc_ref, j)) for j in range(nq)]
            v_of = [_slabs_of(steps.keys(vp_ref, vc_ref, j)) for j in range(nq)]
            qms, doms = {}, {}
            for j in range(nq):
                qv, dov = q_ref[steps.rows_of(j), :], do_ref[steps.rows_of(j), :]
                for h in range(heads):
                    qms[j, h], doms[j, h] = _head_only(qv, h), _head_only(dov, h)
                    s_scr[j * heads + h] = lax.dot_general(qms[j, h], k_of[j](h), _NT, preferred_element_type=F32)
                    dp_scr[j * heads + h] = lax.dot_general(doms[j, h], v_of[j](h), _NT, preferred_element_type=F32)
            dsink_row = jnp.zeros((1, LANES), F32)
            if sink is not None:
                sink_row = jnp.zeros((1, LANES), F32)
                for h in range(heads):
                    sink_row = jnp.where(lane == h, sink_ref[h], sink_row)
            for j in range(nq):
                st = st_ref[steps.rows_of(j), :]
                valid = _band_mask(steps.has_prev(i, j), max_dist)
                for h in range(heads):
                    lse_h = st[:, h:h + 1]
                    delta = st[:, DELTA_LANE + h:DELTA_LANE + h + 1]
                    p = jnp.where(valid, jnp.exp(s_scr[j * heads + h] - lse_h), 0.0)
                    p_scr[j * heads + h] = p.astype(BF16)
                    ds_scr[j * heads + h] = (p * (dp_scr[j * heads + h] - delta)).astype(BF16)
                if sink is not None:
                    term = -jnp.exp(sink_row - st) * pltpu.roll(st, LANES - DELTA_LANE, 1)
                    dsink_row = dsink_row + jnp.sum(jnp.where(lane_q < heads, term, 0.0), axis=0, keepdims=True)
            for j in range(nq):
                for pr in range(heads // 2):
                    he, ho = 2 * pr, 2 * pr + 1
                    even = jnp.dot(ds_scr[j * heads + he], k_of[j](he), preferred_element_type=F32)
                    odd = jnp.dot(ds_scr[j * heads + ho], k_of[j](ho), preferred_element_type=F32)
                    dq_ref[steps.rows_of(j), LANES * pr:LANES * (pr + 1)] = (
                        jnp.where(lane_q < HEAD_DIM, even, odd).astype(BF16))
            if steps.consecutive:
                dk_ref[...] = kcar[...].astype(BF16)
                dv_ref[...] = vcar[...].astype(BF16)
            for j in range(nq):
                for slab in range(kw // LANES):
                    he, ho = j * heads + 2 * slab, j * heads + 2 * slab + 1
                    dk_j = (lax.dot_general(ds_scr[he], qms[j, 2 * slab], _TN, preferred_element_type=F32)
                            + lax.dot_general(ds_scr[ho], qms[j, 2 * slab + 1], _TN, preferred_element_type=F32))
                    dv_j = (lax.dot_general(p_scr[he], doms[j, 2 * slab], _TN, preferred_element_type=F32)
                            + lax.dot_general(p_scr[ho], doms[j, 2 * slab + 1], _TN, preferred_element_type=F32))
                    sl = slice(LANES * slab, LANES * (slab + 1))
                    own_rows = steps.rows_of(j)
                    if not steps.consecutive:
                        dk_ref[own_rows, sl] = (kcar[own_rows, sl] + dk_j[0:BLOCK]).astype(BF16)
                        dv_ref[own_rows, sl] = (vcar[own_rows, sl] + dv_j[0:BLOCK]).astype(BF16)
                    elif j == 0:
                        last = steps.rows_of(nq - 1)
                        dk_ref[last, sl] = (kcar[last, sl] + dk_j[0:BLOCK]).astype(BF16)
                        dv_ref[last, sl] = (vcar[last, sl] + dv_j[0:BLOCK]).astype(BF16)
                    else:
                        before = steps.rows_of(j - 1)
                        kcar[before, sl] += dk_j[0:BLOCK]
                        vcar[before, sl] += dv_j[0:BLOCK]
                    kcar[own_rows, sl] = dk_j[BLOCK:2 * BLOCK]
                    vcar[own_rows, sl] = dv_j[BLOCK:2 * BLOCK]
            if sink is not None:
                dsink_ref[0:1, :] += dsink_row

        @pl.when(i == steps.inner)
        def _():
            dk_ref[...] = kcar[...].astype(BF16)
            dv_ref[...] = vcar[...].astype(BF16)

        if n_rs:
            @pl.when(flat == n_flat - 1)
            def _():
                exchange.finish()

    own, prev = (lambda w: steps.own(w, clamp=True)), (lambda w: steps.prev(w, clamp=True))
    rs_shapes = [t.shape[1:] for t in reduce_scatter]
    in_specs = ([own(qw), prev(kw), own(kw), prev(kw), own(kw), own(qw), own(LANES)]
                + [pl.BlockSpec(memory_space=pl.ANY)] * n_rs)
    args = [q, k, k, v, v, d_out, stat, *reduce_scatter]
    out_specs = [own(qw), steps.late(kw), steps.late(kw)] + [_full(s) for s in rs_shapes]
    out_shape = [jax.ShapeDtypeStruct((seq, qw), BF16), jax.ShapeDtypeStruct((seq, kw), BF16),
                 jax.ShapeDtypeStruct((seq, kw), BF16)] + [jax.ShapeDtypeStruct(s, F32) for s in rs_shapes]
    if sink is not None:
        in_specs = [pl.BlockSpec(memory_space=pltpu.SMEM)] + in_specs
        args = [sink] + args
        out_specs = [_full((8, LANES))] + out_specs
        out_shape = [jax.ShapeDtypeStruct((8, LANES), F32)] + out_shape
    n_hb = nq * heads
    res = pl.pallas_call(
        body, name=name, grid=(steps.outer, steps.inner + 1), in_specs=in_specs, out_specs=out_specs,
        out_shape=out_shape,
        scratch_shapes=[pltpu.VMEM((steps.rows, kw), F32), pltpu.VMEM((steps.rows, kw), F32)]
        + [pltpu.VMEM((n_hb, BLOCK, 2 * BLOCK), F32)] * 2 + [pltpu.VMEM((n_hb, BLOCK, 2 * BLOCK), BF16)] * 2
        + (_ReduceScatter.scratch_shapes(rs_shapes) if n_rs else []),
        compiler_params=_params(dimension_semantics=("arbitrary", "arbitrary")),
    )(*args)
    if sink is not None:
        return (*res[1:4], res[0], *res[4:])
    return res


def _cross_fwd(q, mem, mem_g, w_mem_full, tq=1024):
    seq = q.shape[0]

    def body(q_ref, mem_ref, g_ref, w_ref, o_ref, lse_ref, mn_ref, mk_ref, mv_ref, s_scr, p_scr):
        @pl.when(pl.program_id(0) == 0)
        def _():
            mv_ = mem_ref[...]
            r = lax.rsqrt(jnp.mean(mv_ * mv_, axis=-1, keepdims=True) + RMS_EPS)
            mn = ((mv_ * r) * g_ref[...]).astype(BF16)
            mn_ref[...] = mn
            mkv = jnp.dot(mn, w_ref[...], preferred_element_type=F32)
            mk_ref[...] = mkv[:, 0:C_W].astype(BF16)
            mv_ref[...] = mkv[:, C_W:2 * C_W].astype(BF16)

        qv = q_ref[...]
        k_of, v_of = _slabs_of(mk_ref[...]), _slabs_of(mv_ref[...])
        lane = lax.broadcasted_iota(jnp.int32, (tq, LANES), 1)
        lse_tile = jnp.zeros((tq, LANES), F32)
        for h in range(C_HEADS):
            s_scr[h] = lax.dot_general(_head_only(qv, h), k_of(h), _NT, preferred_element_type=F32)
        ls = []
        for h in range(C_HEADS):
            s = s_scr[h]
            m = jnp.max(s, axis=-1, keepdims=True)
            p = jnp.exp(s - m)
            l = jnp.sum(p, axis=-1, keepdims=True)
            p_scr[h] = p.astype(BF16)
            ls.append(l)
            lse_tile = jnp.where(lane == h, m + jnp.log(l), lse_tile)
        for pr in range(C_HEADS // 2):
            even = jnp.dot(p_scr[2 * pr], v_of(2 * pr), preferred_element_type=F32) / ls[2 * pr]
            odd = jnp.dot(p_scr[2 * pr + 1], v_of(2 * pr + 1), preferred_element_type=F32) / ls[2 * pr + 1]
            o_ref[:, LANES * pr:LANES * (pr + 1)] = jnp.where(lane < HEAD_DIM, even, odd).astype(BF16)
        lse_ref[...] = lse_tile

    return pl.pallas_call(
        body, name="cross_fwd", grid=(seq // tq,),
        in_specs=[_row(tq, C_W), _full((N_MEM, D_MODEL)), _full((1, D_MODEL)), _full((D_MODEL, 2 * C_W))],
        out_specs=[_row(tq, C_W), _row(tq, LANES), _full((N_MEM, D_MODEL)), _full((N_MEM, C_W)), _full((N_MEM, C_W))],
        out_shape=[jax.ShapeDtypeStruct((seq, C_W), BF16), jax.ShapeDtypeStruct((seq, LANES), F32),
                   jax.ShapeDtypeStruct((N_MEM, D_MODEL), BF16), jax.ShapeDtypeStruct((N_MEM, C_W), BF16),
                   jax.ShapeDtypeStruct((N_MEM, C_W), BF16)],
        scratch_shapes=[pltpu.VMEM((C_HEADS, tq, N_MEM), F32), pltpu.VMEM((C_HEADS, tq, N_MEM), BF16)],
        compiler_params=_params(dimension_semantics=("arbitrary",)),
    )(q, mem, mem_g, w_mem_full)


def _cross_bwd(q, mk, mv, d_out, stat, mem, mn, w_mem_full, tq=1024):
    seq = q.shape[0]

    def body(q_ref, mk_ref, mv_ref, do_ref, st_ref, mem_ref, mn_ref, w_ref, dq_ref, dw_ref, mst_ref,
             s_scr, dp_scr, p_scr, ds_scr, dmk_ref, dmv_ref):
        @pl.when(pl.program_id(0) == 0)
        def _():
            dmk_ref[...] = jnp.zeros_like(dmk_ref)
            dmv_ref[...] = jnp.zeros_like(dmv_ref)

        qv, dov, st = q_ref[...], do_ref[...], st_ref[...]
        k_of, v_of = _slabs_of(mk_ref[...]), _slabs_of(mv_ref[...])
        qms = [_head_only(qv, h) for h in range(C_HEADS)]
        doms = [_head_only(dov, h) for h in range(C_HEADS)]
        for h in range(C_HEADS):
            s_scr[h] = lax.dot_general(qms[h], k_of(h), _NT, preferred_element_type=F32)
            dp_scr[h] = lax.dot_general(doms[h], v_of(h), _NT, preferred_element_type=F32)
        for h in range(C_HEADS):
            p = jnp.exp(s_scr[h] - st[:, h:h + 1])
            p_scr[h] = p.astype(BF16)
            ds_scr[h] = (p * (dp_scr[h] - st[:, DELTA_LANE + h:DELTA_LANE + h + 1])).astype(BF16)
        lane = lax.broadcasted_iota(jnp.int32, (tq, LANES), 1)
        for pr in range(C_HEADS // 2):
            sl = slice(LANES * pr, LANES * (pr + 1))
            even = jnp.dot(ds_scr[2 * pr], k_of(2 * pr), preferred_element_type=F32)
            odd = jnp.dot(ds_scr[2 * pr + 1], k_of(2 * pr + 1), preferred_element_type=F32)
            dq_ref[:, sl] = jnp.where(lane < HEAD_DIM, even, odd).astype(BF16)
            dmk_ref[:, sl] += (lax.dot_general(ds_scr[2 * pr], qms[2 * pr], _TN, preferred_element_type=F32)
                               + lax.dot_general(ds_scr[2 * pr + 1], qms[2 * pr + 1], _TN, preferred_element_type=F32))
            dmv_ref[:, sl] += (lax.dot_general(p_scr[2 * pr], doms[2 * pr], _TN, preferred_element_type=F32)
                               + lax.dot_general(p_scr[2 * pr + 1], doms[2 * pr + 1], _TN, preferred_element_type=F32))

        @pl.when(pl.program_id(0) == seq // tq - 1)
        def _():
            dmkv = jnp.concatenate([dmk_ref[...], dmv_ref[...]], axis=1).astype(BF16)
            dw_ref[...] = lax.dot_general(mn_ref[...], dmkv, _TN, preferred_element_type=F32)
            dmn = lax.dot_general(dmkv, w_ref[...], _NT, preferred_element_type=F32)
            mv_ = mem_ref[...]
            r = lax.rsqrt(jnp.mean(mv_ * mv_, axis=-1, keepdims=True) + RMS_EPS)
            mst_ref[...] = jnp.zeros_like(mst_ref)
            mst_ref[0:1, :] = jnp.sum(dmn * (mv_ * r), axis=0, keepdims=True)

    return pl.pallas_call(
        body, name="cross_bwd", grid=(seq // tq,),
        in_specs=[_row(tq, C_W), _full((N_MEM, C_W)), _full((N_MEM, C_W)), _row(tq, C_W), _row(tq, LANES),
                  _full((N_MEM, D_MODEL)), _full((N_MEM, D_MODEL)), _full((D_MODEL, 2 * C_W))],
        out_specs=[_row(tq, C_W), _full((D_MODEL, 2 * C_W)), _full((8, D_MODEL))],
        out_shape=[jax.ShapeDtypeStruct((seq, C_W), BF16), jax.ShapeDtypeStruct((D_MODEL, 2 * C_W), F32),
                   jax.ShapeDtypeStruct((8, D_MODEL), F32)],
        scratch_shapes=[pltpu.VMEM((C_HEADS, tq, N_MEM), F32)] * 2 + [pltpu.VMEM((C_HEADS, tq, N_MEM), BF16)] * 2
        + [pltpu.VMEM((N_MEM, C_W), F32)] * 2,
        compiler_params=_params(dimension_semantics=("arbitrary",)),
    )(q, mk, mv, d_out, stat, mem, mn, w_mem_full)


def _per_head(tile, width):
    rows = tile.shape[0]
    lane = lax.broadcasted_iota(jnp.int32, (rows, LANES), 1)
    slabs = []
    for p in range(width // LANES):
        even = jnp.broadcast_to(tile[:, 2 * p:2 * p + 1], (rows, LANES))
        odd = jnp.broadcast_to(tile[:, 2 * p + 1:2 * p + 2], (rows, LANES))
        slabs.append(jnp.where(lane < HEAD_DIM, even, odd))
    return slabs[0] if len(slabs) == 1 else jnp.concatenate(slabs, axis=1)


def _with_delta(lse_tile, prod):
    rows = lse_tile.shape[0]
    lane = lax.broadcasted_iota(jnp.int32, (rows, LANES), 1)
    tile = lse_tile
    for p in range(prod.shape[1] // LANES):
        slab = prod[:, LANES * p:LANES * (p + 1)]
        even = jnp.sum(jnp.where(lane < HEAD_DIM, slab, 0.0), axis=-1, keepdims=True)
        odd = jnp.sum(jnp.where(lane >= HEAD_DIM, slab, 0.0), axis=-1, keepdims=True)
        tile = jnp.where(lane == DELTA_LANE + 2 * p, even, tile)
        tile = jnp.where(lane == DELTA_LANE + 2 * p + 1, odd, tile)
    return tile


def _mid(oa, lse_a, ob, lse_b, oc, lse_c, gate, x, target, w_out_full, post_g, tm=512):
    seq = x.shape[0]
    n_b = B_W // LANES

    def body(oa_ref, la_ref, b1_ref, l1_ref, b4_ref, l4_ref, b16_ref, l16_ref, oc_ref, lc_ref,
             gate_ref, x_ref, t_ref, w_ref, pg_ref,
             dh_ref, dg_ref, doa_ref, sa_ref, dob1_ref, sb1_ref, dob4_ref, sb4_ref, dob16_ref, sb16_ref,
             doc_ref, sc_ref, dw_ref, st_ref, scr_b4, scr_b16, scr_l4, scr_l16, scr_do, scr_sb):
        @pl.when(pl.program_id(0) == 0)
        def _():
            dw_ref[...] = jnp.zeros_like(dw_ref)
            st_ref[...] = jnp.zeros_like(st_ref)

        b1, l1 = b1_ref[...].astype(F32), l1_ref[...]
        b4, l4 = _load_permuted(b4_ref, scr_b4, 4), _load_permuted(l4_ref, scr_l4, 4)
        b16, l16 = _load_permuted(b16_ref, scr_b16, 16), _load_permuted(l16_ref, scr_l16, 16)
        lm = jnp.maximum(jnp.maximum(l1, l4), l16)
        e1, e4, e16 = jnp.exp(l1 - lm), jnp.exp(l4 - lm), jnp.exp(l16 - lm)
        den = e1 + e4 + e16
        lse_b_tile = lm + jnp.log(den)
        ob_v = _per_head(e1 / den, B_W) * b1 + _per_head(e4 / den, B_W) * b4 + _per_head(e16 / den, B_W) * b16
        o_all = jnp.concatenate([oa_ref[...].astype(F32), ob_v, oc_ref[...].astype(F32)], axis=1)
        g = gate_ref[...].astype(F32)
        sig = 1.0 / (1.0 + jnp.exp(-g))
        silu = g * sig
        y = (o_all * silu).astype(BF16)
        w = w_ref[...]
        z = jnp.dot(y, w, preferred_element_type=F32)
        rz = lax.rsqrt(jnp.mean(z * z, axis=-1, keepdims=True) + RMS_EPS)
        hn = z * rz
        pg = pg_ref[...]
        err = (x_ref[...] + hn * pg) - t_ref[...]
        loss = 0.5 * jnp.sum(jnp.mean(err * err, axis=-1, keepdims=True), axis=0, keepdims=True)
        dh = err * (1.0 / D_MODEL)
        dh_ref[...] = dh.astype(BF16)
        st_ref[0:1, :] += jnp.sum(dh * hn, axis=0, keepdims=True)
        st_ref[1:2, :] += jnp.broadcast_to(loss, (1, D_MODEL))
        dhn = dh * pg
        dz = (rz * (dhn - hn * jnp.mean(dhn * hn, axis=-1, keepdims=True))).astype(BF16)
        dy = lax.dot_general(dz, w, _NT, preferred_element_type=F32)
        dw_ref[...] += lax.dot_general(y, dz, _TN, preferred_element_type=F32)
        dg_ref[...] = (dy * o_all * (sig * (1.0 + g * (1.0 - sig)))).astype(BF16)
        d_o = (dy * silu).astype(BF16)
        prod = d_o.astype(F32) * o_all
        doa_ref[...] = d_o[:, 0:A_W]
        sa_ref[...] = _with_delta(la_ref[...], prod[:, 0:A_W])
        doc_ref[...] = d_o[:, A_W + B_W:D_MODEL]
        sc_ref[...] = _with_delta(lc_ref[...], prod[:, A_W + B_W:D_MODEL])
        d_ob = d_o[:, A_W:A_W + B_W]
        stat_b = _with_delta(lse_b_tile, prod[:, A_W:A_W + B_W])
        dob1_ref[...] = d_ob
        sb1_ref[...] = stat_b
        _put(scr_do, d_ob.astype(F32))
        _put(scr_sb, stat_b)
        _store_permuted(scr_do, dob4_ref, 4, BF16)
        _store_permuted(scr_sb, sb4_ref, 4, F32)
        _store_permuted(scr_do, dob16_ref, 16, BF16)
        _store_permuted(scr_sb, sb16_ref, 16, F32)

    p4 = lambda w: _perm_spec(tm, 4, w)
    p16 = lambda w: _perm_spec(tm, 16, w)
    in_specs = [_row(tm, A_W), _row(tm, LANES), _row(tm, B_W), _row(tm, LANES), p4(B_W), p4(LANES), p16(B_W), p16(LANES),
                _row(tm, C_W), _row(tm, LANES), _row(tm, D_MODEL), _row(tm, D_MODEL), _row(tm, D_MODEL),
                _full((D_MODEL, D_MODEL)), _full((1, D_MODEL))]
    sds = jax.ShapeDtypeStruct
    v4 = lambda w, dt: sds((seq // (BLOCK * 4), 4, BLOCK, w), dt)
    v16 = lambda w, dt: sds((seq // (BLOCK * 16), 16, BLOCK, w), dt)
    out_specs = [_row(tm, D_MODEL), _row(tm, D_MODEL), _row(tm, A_W), _row(tm, LANES), _row(tm, B_W), _row(tm, LANES),
                 p4(B_W), p4(LANES), p16(B_W), p16(LANES), _row(tm, C_W), _row(tm, LANES),
                 _full((D_MODEL, D_MODEL)), _full((8, D_MODEL))]
    out_shape = [sds((seq, D_MODEL), BF16), sds((seq, D_MODEL), BF16), sds((seq, A_W), BF16), sds((seq, LANES), F32),
                 sds((seq, B_W), BF16), sds((seq, LANES), F32), v4(B_W, BF16), v4(LANES, F32), v16(B_W, BF16),
                 v16(LANES, F32), sds((seq, C_W), BF16), sds((seq, LANES), F32),
                 sds((D_MODEL, D_MODEL), F32), sds((8, D_MODEL), F32)]
    res = pl.pallas_call(
        body, name="mid", grid=(seq // tm,), in_specs=in_specs, out_specs=out_specs, out_shape=out_shape,
        scratch_shapes=[pltpu.VMEM((n_b, tm, LANES), F32), pltpu.VMEM((n_b, tm, LANES), F32),
                        pltpu.VMEM((1, tm, LANES), F32), pltpu.VMEM((1, tm, LANES), F32),
                        pltpu.VMEM((n_b, tm, LANES), F32), pltpu.VMEM((1, tm, LANES), F32)],
        compiler_params=_params(dimension_semantics=("arbitrary",)),
    )(oa, lse_a, ob[1], lse_b[1], _perm_view(ob[4], 4), _perm_view(lse_b[4], 4), _perm_view(ob[16], 16),
      _perm_view(lse_b[16], 16), oc, lse_c, gate, x, target, w_out_full, post_g)
    dh, d_gate, do_a, st_a, do_b1, st_b1, do_b4, st_b4, do_b16, st_b16, do_c, st_c, d_wout, stats = res
    flat = lambda t: t.reshape(seq, t.shape[-1])
    d_b = {1: (do_b1, st_b1), 4: (flat(do_b4), flat(st_b4)), 16: (flat(do_b16), flat(st_b16))}
    return dh, d_gate, (do_a, st_a), d_b, (do_c, st_c), d_wout, stats


def _inproj_bwd(x, u, dh, pre_g, w_in_full, tabs, dqa, dka, dva, dqkv_b, dqc, dgate, tm=512):
    seq = x.shape[0]
    n_b = B_W // LANES

    def body(x_ref, u_ref, dh_ref, g_ref, w_hbm, c_ref, up_ref, dn_ref, dqa_ref, dka_ref, dva_ref,
             dq1, dk1, dv1, dq4, dk4, dv4, dq16, dk16, dv16, dqc_ref, dg_ref,
             gx_ref, dw_ref, st_ref, scr4, scr16, w_scr, w_sems, dp_ref):
        _stage_w_in(w_hbm, w_scr, w_sems)

        @pl.when(pl.program_id(0) == 0)
        def _():
            st_ref[...] = jnp.zeros_like(st_ref)
            dw_ref[...] = jnp.zeros_like(dw_ref)

        c, up, dn = c_ref[...], -up_ref[...], -dn_ref[...]
        unrot = lambda t: _rotate(t, c, up, dn)
        total = lambda r1, r4, r16: (r1[...].astype(F32) + _load_permuted(r4, scr4, 4)
                                     + _load_permuted(r16, scr16, 16))
        at = lambda piece: slice(*COLS[piece])
        dp_ref[:, at("qa")] = (unrot(dqa_ref[...].astype(F32)) * SCALE).astype(BF16)
        dp_ref[:, at("ka")] = unrot(_per_kv_head(dka_ref[...].astype(F32))).astype(BF16)
        dp_ref[:, at("va")] = _per_kv_head(dva_ref[...].astype(F32)).astype(BF16)
        dp_ref[:, at("ga")] = dg_ref[:, 0:A_W]
        dp_ref[:, at("qb")] = (unrot(total(dq1, dq4, dq16)) * SCALE).astype(BF16)
        dp_ref[:, at("kb")] = unrot(total(dk1, dk4, dk16)).astype(BF16)
        dp_ref[:, at("vb")] = total(dv1, dv4, dv16).astype(BF16)
        dp_ref[:, at("gb")] = dg_ref[:, A_W:A_W + B_W]
        dp_ref[:, at("qc")] = (dqc_ref[...].astype(F32) * SCALE).astype(BF16)
        dp_ref[:, at("gc")] = dg_ref[:, A_W + B_W:D_MODEL]
        du = lax.dot_general(dp_ref[...], w_scr[...], _NT, preferred_element_type=F32)
        res = lax.dot_general(u_ref[...], dp_ref[...], _TN, preferred_element_type=F32)
        for k in range(N_DEV):
            dw_ref[k] += res[:, SHARD_IN * k:SHARD_IN * (k + 1)]
        xv = x_ref[...]
        r = lax.rsqrt(jnp.mean(xv * xv, axis=-1, keepdims=True) + RMS_EPS)
        xh = xv * r
        st_ref[0:1, :] += jnp.sum(du * xh, axis=0, keepdims=True)
        dxh = du * g_ref[...]
        gx_ref[...] = dh_ref[...].astype(F32) + r * (dxh - xh * jnp.mean(dxh * xh, axis=-1, keepdims=True))

    in_specs = ([_row(tm, D_MODEL), _row(tm, D_MODEL), _row(tm, D_MODEL), _full((1, D_MODEL)),
                 pl.BlockSpec(memory_space=pl.ANY),
                 _row(tm, LANES), _row(tm, LANES), _row(tm, LANES), _row(tm, A_W), _row(tm, A_W), _row(tm, A_W)]
                + [_row(tm, B_W)] * 3 + [_perm_spec(tm, 4, B_W)] * 3 + [_perm_spec(tm, 16, B_W)] * 3
                + [_row(tm, C_W), _row(tm, D_MODEL)])
    dw_spec = pl.BlockSpec((N_DEV, D_MODEL, SHARD_IN), lambda i: (0, 0, 0), pipeline_mode=pl.Buffered(1))
    return pl.pallas_call(
        body, name="inproj_bwd", grid=(seq // tm,), in_specs=in_specs,
        out_specs=[_row(tm, D_MODEL), dw_spec, _full((8, D_MODEL))],
        out_shape=[jax.ShapeDtypeStruct((seq, D_MODEL), F32), jax.ShapeDtypeStruct((N_DEV, D_MODEL, SHARD_IN), F32),
                   jax.ShapeDtypeStruct((8, D_MODEL), F32)],
        scratch_shapes=[pltpu.VMEM((n_b, tm, LANES), F32), pltpu.VMEM((n_b, tm, LANES), F32)] + _w_in_scratch()
        + [pltpu.VMEM((tm, D_IN), BF16)],
        compiler_params=_params(dimension_semantics=("arbitrary",)),
    )(x, u, dh, pre_g, w_in_full, *tabs, dqa, dka, dva, *dqkv_b[1], *[_perm_view(t, 4) for t in dqkv_b[4]],
      *[_perm_view(t, 16) for t in dqkv_b[16]], dqc, dgate)


class _ReduceScatter:
    def __init__(self, ins, outs, scratch):
        self.n = n = len(ins)
        self.ins, self.outs = ins, outs
        self.mine, self.got, self.snd, self.rcv = (scratch[n * t:n * (t + 1)] for t in range(4))
        self.load_sems, self.d2d_send, self.d2d_recv, self.ici_send, self.ici_recv = scratch[4 * n:]
        self.pos = _mesh_pos()
        self.pairs = [(a, kk) for kk in (3, 1, 2) for a in range(n)]

    @staticmethod
    def scratch_shapes(shapes):
        return ([pltpu.VMEM((4,) + s, F32) for s in shapes] + [pltpu.VMEM((4,) + s, F32) for s in shapes]
                + [pltpu.VMEM((3,) + s, BF16) for s in shapes] + [pltpu.VMEM((3,) + s, BF16) for s in shapes]
                + [pltpu.SemaphoreType.DMA((len(shapes), 4))] * 5)

    def _chip(self, kk):
        x, y, _ = self.pos
        return (1 - x if kk & 2 else x, 1 - y if kk & 1 else y)

    def _load(self, a, kk):
        block = _dev_index((*self._chip(kk), self.pos[2]))
        return pltpu.make_async_copy(self.ins[a].at[block], self.mine[a].at[kk], self.load_sems.at[a, kk])

    def _swap(self, a, kk):
        x, y, c = self.pos
        return pltpu.make_async_remote_copy(
            src_ref=self.ins[a].at[_dev_index((*self._chip(kk), 1 - c))], dst_ref=self.got[a].at[kk],
            send_sem=self.d2d_send.at[a, kk], recv_sem=self.d2d_recv.at[a, kk],
            device_id=(x, y, 1 - c), device_id_type=MESH_ID)

    def _hop(self, a, kk):
        return pltpu.make_async_remote_copy(
            src_ref=self.snd[a].at[kk - 1], dst_ref=self.rcv[a].at[kk - 1], send_sem=self.ici_send.at[a, kk],
            recv_sem=self.ici_recv.at[a, kk], device_id=(*self._chip(kk), self.pos[2]), device_id_type=MESH_ID)

    def start(self):
        for kk in (3, 1, 2, 0):
            for a in range(self.n):
                self._load(a, kk).start()
                self._swap(a, kk).start()

    def send_chip_sums(self):
        for a, kk in self.pairs:
            self._load(a, kk).wait()
            self._swap(a, kk).wait_recv()
            self.snd[a][kk - 1] = (self.mine[a][kk] + self.got[a][kk]).astype(BF16)
            self._hop(a, kk).start()

    def finish(self):
        for a in range(self.n):
            self._load(a, 0).wait()
            self._swap(a, 0).wait_recv()
            acc = self.mine[a][0] + self.got[a][0]
            for kk in (1, 2, 3):
                self._hop(a, kk).wait_recv()
                acc = acc + self.rcv[a][kk - 1].astype(F32)
            self.outs[a][...] = acc
        for kk in range(4):
            for a in range(self.n):
                self._swap(a, kk).wait_send()
        for a, kk in self.pairs:
            self._hop(a, kk).wait_send()


def _local_step(x, mem, pre_g, w_in, sink, mem_g, w_mem, w_out, post_g, target):
    u, *tabs, w_in_full = _prep(x, pre_g, w_in)
    qa, ka, va, qkv_b, qc, gate, w_mem_all, w_out_all = _inproj(u, w_in_full, tabs, w_mem, w_out)
    w_mem_full = w_mem_all.reshape(D_MODEL, 2 * C_W)
    w_out_full = w_out_all.reshape(D_MODEL, D_MODEL)

    a_cfg = dict(dil=1, heads=A_HEADS, max_dist=BLOCK - 1, nq=ATTN_BLOCKS_PER_STEP)
    b_cfgs = {dil: dict(dil=dil, heads=B_HEADS, max_dist=win // dil, nq=ATTN_BLOCKS_PER_STEP)
              for win, dil in B_CONFIGS}
    oa, lse_a = _banded_fwd(qa, ka, va, sink, name="attn_a_fwd", **a_cfg)
    ob, lse_b = {}, {}
    for dil, cfg in b_cfgs.items():
        ob[dil], lse_b[dil] = _banded_fwd(*qkv_b[dil], None, name=f"attn_b{dil}_fwd", **cfg)
    oc, lse_c, mn, mk, mv = _cross_fwd(qc, mem, mem_g, w_mem_full)

    dh, d_gate, d_a, d_b, d_c, d_wout, st_mid = _mid(oa, lse_a, ob, lse_b, oc, lse_c, gate, x, target, w_out_full, post_g)

    dqc, d_wmem, st_mem = _cross_bwd(qc, mk, mv, *d_c, mem, mn, w_mem_full)
    bwd_nq = lambda dil: ATTN_BWD_BLOCKS_PER_STEP if dil == 1 else min(dil, ATTN_BWD_BLOCKS_PER_STEP)
    dqkv_b = {dil: _banded_bwd(*qkv_b[dil], *d_b[dil], None, name=f"attn_b{dil}_bwd", **{**cfg, "nq": bwd_nq(dil)})
              for dil, cfg in b_cfgs.items()}
    dqa, dka, dva, dsink, g_wmem, g_wout = _banded_bwd(
        qa, ka, va, *d_a, sink, name="attn_a_bwd", **{**a_cfg, "nq": bwd_nq(1)},
        reduce_scatter=(d_wmem.reshape(N_DEV, SHARD_ROWS, 2 * C_W), d_wout.reshape(N_DEV, SHARD_ROWS, D_MODEL)))

    grad_x, d_win, st_pre = _inproj_bwd(x, u, dh, pre_g, w_in_full, tabs, dqa, dka, dva, dqkv_b, dqc, d_gate)

    dsink_row = jnp.pad(dsink[0:1, :], ((0, 0), (0, D_MODEL - LANES)))
    stats = jnp.concatenate([st_pre[0:1], st_mem[0:1], st_mid[0:1], dsink_row, st_mid[1:2],
                             jnp.zeros((3, D_MODEL), F32)], axis=0)
    return grad_x, d_win, g_wmem, g_wout, stats


def _prep(x, pre_g, w_in, tm=1024):
    seq = x.shape[0]
    n_steps = seq // tm
    parts = 2
    rows = D_MODEL // parts
    relay_at = min(4, n_steps - 1)
    j = jnp.arange(LANES) % HEAD_DIM
    freq = (ROPE_THETA ** (-(2 * (j % (ROT_DIM // 2))).astype(F32) / ROT_DIM))[None, :]
    SIB, NB_X, NB_Y, RELAY, FWD = 0, 1, 2, 3, 4

    def body(x_ref, g_ref, f_ref, win_ref, u_ref, c_ref, up_ref, dn_ref, win_out, win_b,
             send_sems, recv_sems, local_sems):
        step = pl.program_id(0)
        px, py, pc = _mesh_pos()
        me, sibling = (px, py, pc), (px, py, 1 - pc)
        others = lambda core: ((1 - px, py, core), (px, 1 - py, core), (1 - px, 1 - py, core))
        x_nb, y_nb, diag = others(pc)
        relay_from = [x_nb, y_nb]
        relay_to = [y_nb, x_nb]

        def src(a):
            return win_b.at[pl.ds(rows * a, rows)]

        def slot(a, p):
            return win_out.at[_dev_index(p), pl.ds(rows * a, rows)]

        def copy(a, k, block, to, own=False):
            return pltpu.make_async_remote_copy(
                src_ref=src(a) if own else slot(a, block), dst_ref=slot(a, block),
                send_sem=send_sems.at[a, k], recv_sem=recv_sems.at[a, k], device_id=to, device_id_type=MESH_ID)

        def first_sends():
            return [copy(0, NB_X, me, x_nb, own=True), copy(1, NB_Y, me, y_nb, own=True),
                    copy(1, NB_X, me, x_nb, own=True), copy(0, NB_Y, me, y_nb, own=True),
                    copy(0, SIB, me, sibling, own=True), copy(1, SIB, me, sibling, own=True)]

        def relay(a):
            return copy(a, RELAY, relay_from[a], relay_to[a])

        def to_sibling(a, which):
            return copy(a, FWD + which, others(pc)[which], sibling)

        def local(a):
            return pltpu.make_async_copy(src(a), slot(a, me), local_sems.at[a])

        @pl.when(step == 0)
        def _():
            win_b[...] = win_ref[...].astype(BF16)
            for a in range(parts):
                local(a).start()
            for cp in first_sends():
                cp.start()

        @pl.when(step == relay_at)
        def _():
            for a in range(parts):
                copy(a, NB_X + a, relay_from[a], me).wait_recv()
                relay(a).start()
                to_sibling(a, a).start()

        xv = x_ref[...]
        r = lax.rsqrt(jnp.mean(xv * xv, axis=-1, keepdims=True) + RMS_EPS)
        u_ref[...] = ((xv * r) * g_ref[...]).astype(BF16)
        freq = f_ref[...]
        ang_row = lax.broadcasted_iota(jnp.int32, (BLOCK, LANES), 0).astype(F32) * freq
        cos_row, sin_row = jnp.cos(ang_row), jnp.sin(ang_row)
        head_lane = lax.broadcasted_iota(jnp.int32, (BLOCK, LANES), 1) % HEAD_DIM
        half = ROT_DIM // 2
        for blk in range(tm // BLOCK):
            rows_b = slice(BLOCK * blk, BLOCK * (blk + 1))
            ang_0 = (step * tm + BLOCK * blk).astype(F32) * freq
            cos_0, sin_0 = jnp.cos(ang_0), jnp.sin(ang_0)
            cos = cos_0 * cos_row - sin_0 * sin_row
            sin = sin_0 * cos_row + cos_0 * sin_row
            c_ref[rows_b, :] = jnp.where(head_lane < ROT_DIM, cos, 1.0)
            up_ref[rows_b, :] = jnp.where((head_lane >= half) & (head_lane < ROT_DIM), sin, 0.0)
            dn_ref[rows_b, :] = jnp.where(head_lane < half, -sin, 0.0)

        @pl.when(step == n_steps - 1)
        def _():
            copy(1, NB_X, x_nb, me).wait_recv()
            to_sibling(1, 0).start()
            copy(0, NB_Y, y_nb, me).wait_recv()
            to_sibling(0, 1).start()
            for a in range(parts):
                copy(a, RELAY, diag, me).wait_recv()
                to_sibling(a, 2).start()
            for a in range(parts):
                copy(a, SIB, sibling, me).wait_recv()
                for which in range(3):
                    copy(a, FWD + which, others(1 - pc)[which], me).wait_recv()
            for cp in first_sends():
                cp.wait_send()
            for a in range(parts):
                relay(a).wait_send()
                for which in range(3):
                    to_sibling(a, which).wait_send()
                local(a).wait()

    return pl.pallas_call(
        body, name="prep", grid=(n_steps,),
        in_specs=[_row(tm, D_MODEL), _full((1, D_MODEL)), _full((1, LANES)), _full(w_in.shape)],
        out_specs=[_row(tm, D_MODEL), _row(tm, LANES), _row(tm, LANES), _row(tm, LANES),
                   pl.BlockSpec(memory_space=pl.ANY)],
        out_shape=[jax.ShapeDtypeStruct((seq, D_MODEL), BF16)] + [jax.ShapeDtypeStruct((seq, LANES), F32)] * 3
        + [jax.ShapeDtypeStruct((N_DEV,) + w_in.shape, BF16)],
        scratch_shapes=[pltpu.VMEM(w_in.shape, BF16), pltpu.SemaphoreType.DMA((parts, FWD + 3)),
                        pltpu.SemaphoreType.DMA((parts, FWD + 3)), pltpu.SemaphoreType.DMA((parts,))],
        compiler_params=_params(dimension_semantics=("arbitrary",)),
    )(x, pre_g, freq, w_in)


def _exchange_grads(d_win, stats):
    def body(win, st, g_win, r_st, send_sems, recv_sems, local_sem, *scratch):
        exchange = _ReduceScatter((win,), (g_win,), scratch)
        exchange.start()
        pos = _mesh_pos()
        me = _dev_index(pos)
        own = pltpu.make_async_copy(st, r_st.at[me], local_sem)
        own.start()
        copies = []
        for s in range(1, N_DEV):
            peer = _xor_peer(pos, s)
            mk = lambda slot: pltpu.make_async_remote_copy(
                src_ref=st, dst_ref=r_st.at[slot], send_sem=send_sems.at[s], recv_sem=recv_sems.at[s],
                device_id=peer, device_id_type=MESH_ID)
            send, arrival = mk(me), mk(_dev_index(peer))
            send.start()
            copies.append((send, arrival))
        exchange.send_chip_sums()
        exchange.finish()
        for send, arrival in copies:
            arrival.wait_recv()
            send.wait_send()
        own.wait()

    hbm = pl.BlockSpec(memory_space=pl.ANY)
    shard = d_win.shape[1:]
    return pl.pallas_call(
        body, name="exchange_grads", in_specs=[hbm, hbm],
        out_specs=[pl.BlockSpec(memory_space=pltpu.VMEM), hbm],
        out_shape=[jax.ShapeDtypeStruct(shard, F32), jax.ShapeDtypeStruct((N_DEV,) + stats.shape, F32)],
        scratch_shapes=[pltpu.SemaphoreType.DMA((N_DEV,)), pltpu.SemaphoreType.DMA((N_DEV,)), pltpu.SemaphoreType.DMA(())]
        + _ReduceScatter.scratch_shapes([shard]),
        compiler_params=_params(),
    )(d_win, stats)


WEIGHT_ORDER = ("pre_norm", "w_in", "sink_a", "mem_norm", "w_mem_kv", "w_out", "post_norm")


def _adamw_all(grads, r_stats, weights, moments_m, moments_v):
    n = len(WEIGHT_ORDER)
    stat_row = {"pre_norm": 0, "mem_norm": 1, "post_norm": 2, "sink_a": 3}

    def body(*refs):
        gw_in, gw_mem, gw_out, st_ref = refs[0:4]
        w_refs, m_refs, v_refs = (dict(zip(WEIGHT_ORDER, refs[4 + n * t:4 + n * (t + 1)])) for t in range(3))
        loss_ref = refs[4 + 3 * n]
        outs = refs[5 + 3 * n:]
        g_small = st_ref[0]
        for s in range(1, N_DEV):
            g_small = g_small + st_ref[s]
        loss_ref[...] = g_small[4:5, 0:1]
        big = {"w_in": gw_in, "w_mem_kv": gw_mem, "w_out": gw_out}
        for i, name in enumerate(WEIGHT_ORDER):
            if name in big:
                g = big[name][...]
                at = lambda ref: ref[0]
            else:
                width = w_refs[name].shape[-1]
                g = g_small[stat_row[name]:stat_row[name] + 1, 0:width]
                at = lambda ref: ref[...]
            m2 = ADAM_B1 * at(m_refs[name]) + (1.0 - ADAM_B1) * g
            v2 = ADAM_B2 * at(v_refs[name]) + (1.0 - ADAM_B2) * (g * g)
            m_hat = m2 / (1.0 - ADAM_B1 ** ADAM_STEP)
            v_hat = v2 / (1.0 - ADAM_B2 ** ADAM_STEP)
            delta = -ADAM_LR * (m_hat / (jnp.sqrt(v_hat) + ADAM_EPS) + ADAM_WD * at(w_refs[name]))
            for kind, val in enumerate((g, delta, m2, v2)):
                out = outs[kind * n + i]
                if name in big:
                    out[0] = val
                else:
                    out[...] = val

    shapes = [weights[name].shape for name in WEIGHT_ORDER]
    res = pl.pallas_call(
        body, name="adamw_all",
        out_shape=[jax.ShapeDtypeStruct((1, 1), F32)] + [jax.ShapeDtypeStruct(sh, F32) for sh in shapes] * 4,
        compiler_params=_params(),
    )(grads["w_in"], grads["w_mem_kv"], grads["w_out"], r_stats,
      *[weights[k] for k in WEIGHT_ORDER], *[moments_m[k] for k in WEIGHT_ORDER], *[moments_v[k] for k in WEIGHT_ORDER])
    return res[0].reshape(()), res[1:]


def kernel(x, mem, pre_norm, w_in, sink_a, mem_norm, w_mem_kv, w_out, post_norm, loss_target, m_pre_norm, m_w_in, m_sink_a, m_mem_norm, m_w_mem_kv, m_w_out, m_post_norm, v_pre_norm, v_w_in, v_sink_a, v_mem_norm, v_w_mem_kv, v_w_out, v_post_norm):
    sink = jnp.pad(sink_a[0], (0, 8 - A_HEADS))
    grad_x, d_win, g_wmem, g_wout, stats = _local_step(
        x[0], mem[0], pre_norm, w_in[0], sink, mem_norm, w_mem_kv[0], w_out[0], post_norm, loss_target[0])
    g_win, r_stats = _exchange_grads(d_win, stats)
    weights = dict(pre_norm=pre_norm, w_in=w_in, sink_a=sink_a, mem_norm=mem_norm, w_mem_kv=w_mem_kv, w_out=w_out,
                   post_norm=post_norm)
    moments_m = dict(pre_norm=m_pre_norm, w_in=m_w_in, sink_a=m_sink_a, mem_norm=m_mem_norm, w_mem_kv=m_w_mem_kv,
                     w_out=m_w_out, post_norm=m_post_norm)
    moments_v = dict(pre_norm=v_pre_norm, w_in=v_w_in, sink_a=v_sink_a, mem_norm=v_mem_norm, w_mem_kv=v_w_mem_kv,
                     w_out=v_w_out, post_norm=v_post_norm)
    loss, rest = _adamw_all(dict(w_in=g_win, w_mem_kv=g_wmem, w_out=g_wout), r_stats, weights, moments_m, moments_v)
    return (loss, grad_x[None], *rest)
```
